```python
import jax, jax.numpy as jnp
from jax import lax
import numpy as np

D_MODEL = 1024
BATCH = 16
SEQ = 2048
DEPTH = 1

MEM_LEN = 256
HEAD_DIM = 64
CHUNK = 128
A_GROUPS = 4
A_WIDTH = D_MODEL // 2
A_GROUP_W = A_WIDTH // A_GROUPS
SWA_HEADS = 4
SWA_KV_HEADS = 2
SWA_WIDTH = SWA_HEADS * HEAD_DIM
SWA_KV_WIDTH = SWA_KV_HEADS * HEAD_DIM
WINDOW = 128
MEM_HEADS = 4
MEM_WIDTH = MEM_HEADS * HEAD_DIM
MIX_WIDTH = A_WIDTH + SWA_WIDTH + MEM_WIDTH
IN_WIDTH = 2 * A_WIDTH + SWA_WIDTH + 2 * SWA_KV_WIDTH + MEM_WIDTH + MIX_WIDTH
N_BUCKETS = 32
MAX_DISTANCE = 128
EPS = 1e-6
NEG = -1e30

kernel_name = "hymba_gmlp_swa_sink_memxattn_layer"


def rms_norm(x, g):
    xf = x.astype(jnp.float32)
    y = xf * lax.rsqrt(jnp.mean(xf * xf, axis=-1, keepdims=True) + EPS)
    return (y * g.astype(jnp.float32)).astype(x.dtype)


def t5_causal_buckets(dist):
    n = np.maximum(dist, 0)
    max_exact = N_BUCKETS // 2
    large = max_exact + (np.log(np.maximum(n, 1) / max_exact) / np.log(MAX_DISTANCE / max_exact)
                         * (N_BUCKETS - max_exact)).astype(np.int32)
    large = np.minimum(large, N_BUCKETS - 1)
    return np.where(n < max_exact, n, large).astype(np.int32)


def chunked_spatial_gating(u, v, v_g, v_b, w_s, b_s):
    b, s, _ = u.shape
    nc = s // CHUNK
    vg = v.reshape(b, s, A_GROUPS, A_GROUP_W).astype(jnp.float32)
    mu = jnp.mean(vg, axis=-1, keepdims=True)
    var = jnp.mean(jnp.square(vg - mu), axis=-1, keepdims=True)
    vg = (vg - mu) * lax.rsqrt(var + EPS)
    vg = vg * v_g.reshape(A_GROUPS, A_GROUP_W).astype(jnp.float32) + v_b.reshape(A_GROUPS, A_GROUP_W).astype(jnp.float32)
    vc = vg.astype(v.dtype).reshape(b, nc, CHUNK, A_GROUPS, A_GROUP_W)
    causal = jnp.tril(jnp.ones((CHUNK, CHUNK), dtype=w_s.dtype))
    w = w_s * causal[None]
    sv = jnp.einsum('gts,bnsgc->bntgc', w, vc) + b_s.T[None, None, :, :, None]
    return u * sv.reshape(b, s, A_WIDTH)


def sliding_window_attention(q, k, v, sinks, rel_bias):
    b, s, hq, dh = q.shape
    nb = s // CHUNK
    g = hq // SWA_KV_HEADS
    qb = q.reshape(b, nb, CHUNK, SWA_KV_HEADS, g, dh)

    def band(t):
        tb = t.reshape(b, nb, CHUNK, SWA_KV_HEADS, dh)
        prev = jnp.pad(tb, ((0, 0), (1, 0), (0, 0), (0, 0), (0, 0)))[:, :-1]
        return jnp.concatenate([prev, tb], axis=2)

    kb, vb = band(k), band(v)
    logits = jnp.einsum('bnqhgd,bnjhd->bnhgqj', qb, kb).astype(jnp.float32) * (dh ** -0.5)

    qi = np.arange(CHUNK)[:, None]
    kj = np.arange(2 * CHUNK)[None, :]
    dist = qi + CHUNK - kj
    blk = np.arange(nb)[:, None, None]
    valid = (dist >= 0) & (dist < WINDOW) & (blk * CHUNK + kj - CHUNK >= 0)
    buckets = t5_causal_buckets(dist)
    bias = rel_bias.astype(jnp.float32)[buckets]
    bias = jnp.transpose(bias, (2, 0, 1)).reshape(SWA_KV_HEADS, g, CHUNK, 2 * CHUNK)

    logits = jnp.where(valid[None, :, None, None], logits + bias[None, None], NEG)
    sink = sinks.astype(jnp.float32).reshape(1, 1, SWA_KV_HEADS, g, 1, 1)
    m = jnp.maximum(jnp.max(logits, axis=-1, keepdims=True), sink)
    p = jnp.exp(logits - m)
    probs = p / (jnp.sum(p, axis=-1, keepdims=True) + jnp.exp(sink - m))
    out = jnp.einsum('bnhgqj,bnjhd->bnqhgd', probs.astype(v.dtype), vb)
    return out.reshape(b, s, hq * dh)


def memory_cross_attention(q, mem_k, mem_v):
    b, s, h, dh = q.shape
    logits = jnp.einsum('bshd,bmhd->bhsm', q, mem_k).astype(jnp.float32) * (dh ** -0.5)
    probs = jax.nn.softmax(logits, axis=-1)
    out = jnp.einsum('bhsm,bmhd->bshd', probs.astype(mem_v.dtype), mem_v)
    return out.reshape(b, s, h * dh)


def _fwd_setup_inputs(seed: int = 0) -> dict:
    key = jax.random.key(seed)
    ks = jax.random.split(key, 16)
    f32 = jnp.float32
    x = jax.random.normal(ks[0], (BATCH, SEQ, D_MODEL), f32)
    mem = jax.random.normal(ks[1], (BATCH, MEM_LEN, D_MODEL), f32)
    pre_norm_g = 1.0 + 0.05 * jax.random.normal(ks[2], (DEPTH, D_MODEL), f32)
    post_norm_g = 1.0 + 0.05 * jax.random.normal(ks[3], (DEPTH, D_MODEL), f32)
    mem_norm_g = 1.0 + 0.05 * jax.random.normal(ks[4], (DEPTH, D_MODEL), f32)
    w_in = jax.random.normal(ks[5], (DEPTH, D_MODEL, IN_WIDTH), f32) * D_MODEL ** -0.5
    w_mem_kv = jax.random.normal(ks[6], (DEPTH, D_MODEL, 2 * MEM_WIDTH), f32) * D_MODEL ** -0.5
    v_norm_g = 1.0 + 0.05 * jax.random.normal(ks[7], (DEPTH, A_WIDTH), f32)
    v_norm_b = 0.02 * jax.random.normal(ks[8], (DEPTH, A_WIDTH), f32)
    w_spatial = jax.random.normal(ks[9], (DEPTH, A_GROUPS, CHUNK, CHUNK), f32) * CHUNK ** -0.5
    b_spatial = 1.0 + 0.1 * jax.random.normal(ks[10], (DEPTH, A_GROUPS, CHUNK), f32)
    attn_sinks = 0.5 * jax.random.normal(ks[11], (DEPTH, SWA_HEADS), f32)
    rel_bias = 0.5 * jax.random.normal(ks[12], (N_BUCKETS, SWA_HEADS), f32)
    w_out = jax.random.normal(ks[13], (DEPTH, MIX_WIDTH, D_MODEL), f32) * MIX_WIDTH ** -0.5
    return {"x": x, "mem": mem, "pre_norm_g": pre_norm_g, "post_norm_g": post_norm_g,
            "mem_norm_g": mem_norm_g, "w_in": w_in, "w_mem_kv": w_mem_kv,
            "v_norm_g": v_norm_g, "v_norm_b": v_norm_b, "w_spatial": w_spatial,
            "b_spatial": b_spatial, "attn_sinks": attn_sinks, "rel_bias": rel_bias,
            "w_out": w_out}


def _fwd_reference(x, mem, pre_norm_g, post_norm_g, mem_norm_g, w_in, w_mem_kv, v_norm_g, v_norm_b,
              w_spatial, b_spatial, attn_sinks, rel_bias, w_out):
    b, s, _ = x.shape
    m_len = mem.shape[1]
    split_at = np.cumsum([A_WIDTH, A_WIDTH, SWA_WIDTH, SWA_KV_WIDTH, SWA_KV_WIDTH, MEM_WIDTH]).tolist()
    for layer in range(DEPTH):
        h = rms_norm(x, pre_norm_g[layer])
        proj = h @ w_in[layer]
        a_u, a_v, sq, sk, sv, mq, z = jnp.split(proj, split_at, axis=-1)

        y_a = chunked_spatial_gating(jax.nn.gelu(a_u), jax.nn.gelu(a_v), v_norm_g[layer],
                                     v_norm_b[layer], w_spatial[layer], b_spatial[layer])

        y_b = sliding_window_attention(sq.reshape(b, s, SWA_HEADS, HEAD_DIM),
                                       sk.reshape(b, s, SWA_KV_HEADS, HEAD_DIM),
                                       sv.reshape(b, s, SWA_KV_HEADS, HEAD_DIM),
                                       attn_sinks[layer], rel_bias)

        mkv = rms_norm(mem, mem_norm_g[layer]) @ w_mem_kv[layer]
        mk, mv = jnp.split(mkv, 2, axis=-1)
        y_c = memory_cross_attention(mq.reshape(b, s, MEM_HEADS, HEAD_DIM),
                                     mk.reshape(b, m_len, MEM_HEADS, HEAD_DIM),
                                     mv.reshape(b, m_len, MEM_HEADS, HEAD_DIM))

        y = jnp.concatenate([y_a, y_b, y_c], axis=-1) * jax.nn.silu(z)
        x = x + rms_norm(y @ w_out[layer], post_norm_g[layer])
    return x


import jax as _jax
import jax.numpy as _jnp

TWIN_FORMAT = 'train_step'
FWD_PARAMS = ['x', 'mem', 'pre_norm_g', 'post_norm_g', 'mem_norm_g', 'w_in', 'w_mem_kv', 'v_norm_g', 'v_norm_b', 'w_spatial', 'b_spatial', 'attn_sinks', 'rel_bias', 'w_out']
TWIN_WEIGHTS = ['pre_norm_g', 'post_norm_g', 'mem_norm_g', 'w_in', 'w_mem_kv', 'v_norm_g', 'v_norm_b', 'w_spatial', 'b_spatial', 'attn_sinks', 'rel_bias', 'w_out']
TWIN_DIFF_INPUT = 'x'
TWIN_INPUTS = ['x', 'mem', 'pre_norm_g', 'post_norm_g', 'mem_norm_g', 'w_in', 'w_mem_kv', 'v_norm_g', 'v_norm_b', 'w_spatial', 'b_spatial', 'attn_sinks', 'rel_bias', 'w_out', 'loss_target', 'm_pre_norm_g', 'm_post_norm_g', 'm_mem_norm_g', 'm_w_in', 'm_w_mem_kv', 'm_v_norm_g', 'm_v_norm_b', 'm_w_spatial', 'm_b_spatial', 'm_attn_sinks', 'm_rel_bias', 'm_w_out', 'v_pre_norm_g', 'v_post_norm_g', 'v_mem_norm_g', 'v_w_in', 'v_w_mem_kv', 'v_v_norm_g', 'v_v_norm_b', 'v_w_spatial', 'v_b_spatial', 'v_attn_sinks', 'v_rel_bias', 'v_w_out']
TWIN_OUTPUTS = ['loss', 'grad_x', 'grad_pre_norm_g', 'grad_post_norm_g', 'grad_mem_norm_g', 'grad_w_in', 'grad_w_mem_kv', 'grad_v_norm_g', 'grad_v_norm_b', 'grad_w_spatial', 'grad_b_spatial', 'grad_attn_sinks', 'grad_rel_bias', 'grad_w_out', 'delta_pre_norm_g', 'delta_post_norm_g', 'delta_mem_norm_g', 'delta_w_in', 'delta_w_mem_kv', 'delta_v_norm_g', 'delta_v_norm_b', 'delta_w_spatial', 'delta_b_spatial', 'delta_attn_sinks', 'delta_rel_bias', 'delta_w_out', 'new_m_pre_norm_g', 'new_m_post_norm_g', 'new_m_mem_norm_g', 'new_m_w_in', 'new_m_w_mem_kv', 'new_m_v_norm_g', 'new_m_v_norm_b', 'new_m_w_spatial', 'new_m_b_spatial', 'new_m_attn_sinks', 'new_m_rel_bias', 'new_m_w_out', 'new_v_pre_norm_g', 'new_v_post_norm_g', 'new_v_mem_norm_g', 'new_v_w_in', 'new_v_w_mem_kv', 'new_v_v_norm_g', 'new_v_v_norm_b', 'new_v_w_spatial', 'new_v_b_spatial', 'new_v_attn_sinks', 'new_v_rel_bias', 'new_v_w_out']
TWIN_LEAF_KINDS = {'loss': 'loss', 'grad_x': 'grad_x', 'grad_pre_norm_g': 'grad_w', 'grad_post_norm_g': 'grad_w', 'grad_mem_norm_g': 'grad_w', 'grad_w_in': 'grad_w', 'grad_w_mem_kv': 'grad_w', 'grad_v_norm_g': 'grad_w', 'grad_v_norm_b': 'grad_w', 'grad_w_spatial': 'grad_w', 'grad_b_spatial': 'grad_w', 'grad_attn_sinks': 'grad_w', 'grad_rel_bias': 'grad_w', 'grad_w_out': 'grad_w', 'delta_pre_norm_g': 'delta_w', 'delta_post_norm_g': 'delta_w', 'delta_mem_norm_g': 'delta_w', 'delta_w_in': 'delta_w', 'delta_w_mem_kv': 'delta_w', 'delta_v_norm_g': 'delta_w', 'delta_v_norm_b': 'delta_w', 'delta_w_spatial': 'delta_w', 'delta_b_spatial': 'delta_w', 'delta_attn_sinks': 'delta_w', 'delta_rel_bias': 'delta_w', 'delta_w_out': 'delta_w', 'new_m_pre_norm_g': 'new_m', 'new_m_post_norm_g': 'new_m', 'new_m_mem_norm_g': 'new_m', 'new_m_w_in': 'new_m', 'new_m_w_mem_kv': 'new_m', 'new_m_v_norm_g': 'new_m', 'new_m_v_norm_b': 'new_m', 'new_m_w_spatial': 'new_m', 'new_m_b_spatial': 'new_m', 'new_m_attn_sinks': 'new_m', 'new_m_rel_bias': 'new_m', 'new_m_w_out': 'new_m', 'new_v_pre_norm_g': 'new_v', 'new_v_post_norm_g': 'new_v', 'new_v_mem_norm_g': 'new_v', 'new_v_w_in': 'new_v', 'new_v_w_mem_kv': 'new_v', 'new_v_v_norm_g': 'new_v', 'new_v_v_norm_b': 'new_v', 'new_v_w_spatial': 'new_v', 'new_v_b_spatial': 'new_v', 'new_v_attn_sinks': 'new_v', 'new_v_rel_bias': 'new_v', 'new_v_w_out': 'new_v'}


def _forward(args):
    return _fwd_reference(*[args[k] for k in FWD_PARAMS])


def _output_shape():
    out = _jax.eval_shape(lambda: _forward(_fwd_setup_inputs(0)))
    return out.shape, out.dtype

N_MICROBATCH = 1
ADAM_LR = 0.001
ADAM_B1 = 0.9
ADAM_B2 = 0.999
ADAM_EPS = 1e-08
ADAM_WD = 0.01
ADAM_STEP = 10
PER_EXAMPLE_BATCH_AXIS = {'x': 0, 'mem': 0, 'loss_target': 0}
SHARED_INPUTS = []
_WEIGHT_DTYPES = {'pre_norm_g': _jnp.float32, 'post_norm_g': _jnp.float32, 'mem_norm_g': _jnp.float32, 'w_in': _jnp.float32, 'w_mem_kv': _jnp.float32, 'v_norm_g': _jnp.float32, 'v_norm_b': _jnp.float32, 'w_spatial': _jnp.float32, 'b_spatial': _jnp.float32, 'attn_sinks': _jnp.float32, 'rel_bias': _jnp.float32, 'w_out': _jnp.float32}
MOMENT_SCALE = {'pre_norm_g': 4.233450e-01, 'post_norm_g': 3.192690e+01, 'mem_norm_g': 4.565777e-02, 'w_in': 2.704969e-01, 'w_mem_kv': 5.728352e-02, 'v_norm_g': 1.817637e-01, 'v_norm_b': 1.815925e-01, 'w_spatial': 1.964086e-01, 'b_spatial': 2.851223e-01, 'attn_sinks': 8.224469e-02, 'rel_bias': 1.122018e-01, 'w_out': 3.331538e-01}


def _to_microbatches(a, axis):
    t = _jnp.moveaxis(a, axis, 0)
    t = t.reshape((N_MICROBATCH, t.shape[0] // N_MICROBATCH) + t.shape[1:])
    return _jnp.moveaxis(t, 1, axis + 1)


def setup_inputs(seed: int = 0) -> dict:
    inp = _fwd_setup_inputs(seed)
    key = _jax.random.fold_in(_jax.random.key(seed), 7919)
    shape, _ = _output_shape()
    out = dict(inp)
    out["loss_target"] = _jax.random.normal(_jax.random.fold_in(key, 0), shape, _jnp.float32)
    for i, name in enumerate(TWIN_WEIGHTS):
        w = inp[name].astype(_jnp.float32)
        if MOMENT_SCALE is None:
            s = _jnp.sqrt(_jnp.mean(_jnp.square(w)) + 1e-30)
        else:
            s = MOMENT_SCALE[name]
        km, kv = _jax.random.split(_jax.random.fold_in(key, i + 1))
        out[name] = w
        out["m_" + name] = s * _jax.random.normal(km, w.shape, _jnp.float32)
        out["v_" + name] = (s * s) * _jax.random.uniform(kv, w.shape, _jnp.float32, 0.5, 1.5)
    if N_MICROBATCH > 1:
        for name, axis in PER_EXAMPLE_BATCH_AXIS.items():
            out[name] = _to_microbatches(out[name], axis)
    return {'x': out['x'], 'mem': out['mem'], 'pre_norm_g': out['pre_norm_g'], 'post_norm_g': out['post_norm_g'], 'mem_norm_g': out['mem_norm_g'], 'w_in': out['w_in'], 'w_mem_kv': out['w_mem_kv'], 'v_norm_g': out['v_norm_g'], 'v_norm_b': out['v_norm_b'], 'w_spatial': out['w_spatial'], 'b_spatial': out['b_spatial'], 'attn_sinks': out['attn_sinks'], 'rel_bias': out['rel_bias'], 'w_out': out['w_out'], 'loss_target': out['loss_target'], 'm_pre_norm_g': out['m_pre_norm_g'], 'm_post_norm_g': out['m_post_norm_g'], 'm_mem_norm_g': out['m_mem_norm_g'], 'm_w_in': out['m_w_in'], 'm_w_mem_kv': out['m_w_mem_kv'], 'm_v_norm_g': out['m_v_norm_g'], 'm_v_norm_b': out['m_v_norm_b'], 'm_w_spatial': out['m_w_spatial'], 'm_b_spatial': out['m_b_spatial'], 'm_attn_sinks': out['m_attn_sinks'], 'm_rel_bias': out['m_rel_bias'], 'm_w_out': out['m_w_out'], 'v_pre_norm_g': out['v_pre_norm_g'], 'v_post_norm_g': out['v_post_norm_g'], 'v_mem_norm_g': out['v_mem_norm_g'], 'v_w_in': out['v_w_in'], 'v_w_mem_kv': out['v_w_mem_kv'], 'v_v_norm_g': out['v_v_norm_g'], 'v_v_norm_b': out['v_v_norm_b'], 'v_w_spatial': out['v_w_spatial'], 'v_b_spatial': out['v_b_spatial'], 'v_attn_sinks': out['v_attn_sinks'], 'v_rel_bias': out['v_rel_bias'], 'v_w_out': out['v_w_out']}


def _loss(weights, diff, rest, loss_target):
    with _jax.named_scope("forward"):
        args = {**rest, TWIN_DIFF_INPUT: diff, **{k: w.astype(_WEIGHT_DTYPES[k]) for k, w in weights.items()}}
        y = _forward(args)
    with _jax.named_scope("loss_head"):
        err = _jnp.square(y.astype(_jnp.float32) - loss_target)
        return 0.5 * _jnp.sum(_jnp.mean(err, axis=-1)) if err.ndim else 0.5 * err


def _adamw(w, g, m, v):
    m = ADAM_B1 * m + (1.0 - ADAM_B1) * g
    v = ADAM_B2 * v + (1.0 - ADAM_B2) * _jnp.square(g)
    m_hat = m / (1.0 - ADAM_B1 ** ADAM_STEP)
    v_hat = v / (1.0 - ADAM_B2 ** ADAM_STEP)
    delta = -ADAM_LR * (m_hat / (_jnp.sqrt(v_hat) + ADAM_EPS) + ADAM_WD * w)
    return delta, m, v


def reference(x, mem, pre_norm_g, post_norm_g, mem_norm_g, w_in, w_mem_kv, v_norm_g, v_norm_b, w_spatial, b_spatial, attn_sinks, rel_bias, w_out, loss_target, m_pre_norm_g, m_post_norm_g, m_mem_norm_g, m_w_in, m_w_mem_kv, m_v_norm_g, m_v_norm_b, m_w_spatial, m_b_spatial, m_attn_sinks, m_rel_bias, m_w_out, v_pre_norm_g, v_post_norm_g, v_mem_norm_g, v_w_in, v_w_mem_kv, v_v_norm_g, v_v_norm_b, v_w_spatial, v_b_spatial, v_attn_sinks, v_rel_bias, v_w_out):
    given = dict(x=x, mem=mem, pre_norm_g=pre_norm_g, post_norm_g=post_norm_g, mem_norm_g=mem_norm_g, w_in=w_in, w_mem_kv=w_mem_kv, v_norm_g=v_norm_g, v_norm_b=v_norm_b, w_spatial=w_spatial, b_spatial=b_spatial, attn_sinks=attn_sinks, rel_bias=rel_bias, w_out=w_out, loss_target=loss_target, m_pre_norm_g=m_pre_norm_g, m_post_norm_g=m_post_norm_g, m_mem_norm_g=m_mem_norm_g, m_w_in=m_w_in, m_w_mem_kv=m_w_mem_kv, m_v_norm_g=m_v_norm_g, m_v_norm_b=m_v_norm_b, m_w_spatial=m_w_spatial, m_b_spatial=m_b_spatial, m_attn_sinks=m_attn_sinks, m_rel_bias=m_rel_bias, m_w_out=m_w_out, v_pre_norm_g=v_pre_norm_g, v_post_norm_g=v_post_norm_g, v_mem_norm_g=v_mem_norm_g, v_w_in=v_w_in, v_w_mem_kv=v_w_mem_kv, v_v_norm_g=v_v_norm_g, v_v_norm_b=v_v_norm_b, v_w_spatial=v_w_spatial, v_b_spatial=v_b_spatial, v_attn_sinks=v_attn_sinks, v_rel_bias=v_rel_bias, v_w_out=v_w_out)
    weights = {n: given[n] for n in TWIN_WEIGHTS}
    shared = {n: given[n] for n in SHARED_INPUTS}
    per_example = {n: given[n] for n in ['x', 'mem']}
    grad_fn = _jax.value_and_grad(_loss, argnums=(0, 1))

    def one_microbatch(ex, loss_target):
        ex = dict(ex)
        diff = ex.pop(TWIN_DIFF_INPUT)
        return grad_fn(weights, diff, {**shared, **ex}, loss_target)

    if N_MICROBATCH == 1:
        loss, (grad_w, grad_x) = one_microbatch(per_example, given["loss_target"])
    else:
        def body(carry, xs):
            loss_sum, grad_sum = carry
            l_k, (gw_k, gx_k) = one_microbatch(xs[0], xs[1])
            with _jax.named_scope("update"):
                return (loss_sum + l_k, _jax.tree.map(_jnp.add, grad_sum, gw_k)), gx_k

        init = (_jnp.zeros((), _jnp.float32), _jax.tree.map(_jnp.zeros_like, weights))
        (loss, grad_w), grad_x = _jax.lax.scan(body, init, (per_example, given["loss_target"]))
    with _jax.named_scope("update"):
        delta_w, new_m, new_v = {}, {}, {}
        for n in TWIN_WEIGHTS:
            delta_w[n], new_m[n], new_v[n] = _adamw(weights[n], grad_w[n], given["m_" + n], given["v_" + n])
    return (loss, grad_x, *[grad_w[n] for n in TWIN_WEIGHTS], *[delta_w[n] for n in TWIN_WEIGHTS],
            *[new_m[n] for n in TWIN_WEIGHTS], *[new_v[n] for n in TWIN_WEIGHTS])
```

```python
import functools

import numpy as np
import jax
import jax.numpy as jnp
from jax import lax
from jax.experimental import pallas as pl
from jax.experimental.pallas import tpu as pltpu

F32 = jnp.float32
BF16 = jnp.bfloat16
MM = jnp.bfloat16

D_MODEL = 1024
CHUNK = 128
A_GROUPS = 4
A_WIDTH = 512
UV_W = 1024
QKV_W = 768
Z_W = 1024
IN_WIDTH = UV_W + QKV_W + Z_W
MEM_LEN = 256
N_BUCKETS = 32
MAX_DISTANCE = 128
EPS = 1e-6
NEG = -1e30
SCALE = 0.125
N_DEV = 8
SHARD_IN = IN_WIDTH // N_DEV
SHARD_O = D_MODEL // N_DEV

ADAM_LR = 0.001
ADAM_B1 = 0.9
ADAM_B2 = 0.999
ADAM_EPS = 1e-08
ADAM_WD = 0.01
ADAM_STEP = 10

VMEM_LIMIT = 58 * 1024 * 1024

_GELU_C = 0.7978845608028654
_GELU_A = 0.044715

MESH = pl.DeviceIdType.MESH


def _dot(a, b):
    return lax.dot_general(a, b, (((1,), (0,)), ((), ())), preferred_element_type=F32)


def _dot_nt(a, b):
    return lax.dot_general(a, b, (((1,), (1,)), ((), ())), preferred_element_type=F32)


def _dot_tn(a, b):
    return lax.dot_general(a, b, (((0,), (0,)), ((), ())), preferred_element_type=F32)


def _gelu(x):
    x2 = x * x
    t = jnp.tanh(_GELU_C * (x + _GELU_A * x * x2))
    return 0.5 * x * (1.0 + t)


def _gelu_and_grad(x):
    x2 = x * x
    t = jnp.tanh(_GELU_C * (x + _GELU_A * x * x2))
    g = 0.5 * x * (1.0 + t)
    dg = 0.5 * (1.0 + t) + 0.5 * x * (1.0 - t * t) * (_GELU_C * (1.0 + 3.0 * _GELU_A * x2))
    return g, dg


def _t5_buckets():
    qi = np.arange(CHUNK)[:, None]
    kj = np.arange(2 * CHUNK)[None, :]
    n = np.maximum(qi + CHUNK - kj, 0)
    max_exact = N_BUCKETS // 2
    large = max_exact + (np.log(np.maximum(n, 1) / max_exact) / np.log(MAX_DISTANCE / max_exact)
                         * (N_BUCKETS - max_exact)).astype(np.int32)
    large = np.minimum(large, N_BUCKETS - 1)
    return np.where(n < max_exact, n, large).astype(np.int32)


def _params(**kw):
    return pltpu.CompilerParams(vmem_limit_bytes=VMEM_LIMIT, **kw)


def _full(shape):
    nd = len(shape)
    return pl.BlockSpec(shape, lambda *_: (0,) * nd)


def _window_valid():
    qi = lax.broadcasted_iota(jnp.int32, (CHUNK, 2 * CHUNK), 0)
    kj = lax.broadcasted_iota(jnp.int32, (CHUNK, 2 * CHUNK), 1)
    dist = qi + CHUNK - kj
    return (dist >= 0) & (dist < CHUNK)


def _wgather(a, b, c):
    def body(a_ref, b_ref, c_ref, oa, ob, oc, ssem, rsem):
        x, y, cc = lax.axis_index("x"), lax.axis_index("y"), lax.axis_index("c")
        me = 4 * x + 2 * y + cc
        sib = (x, y, 1 - cc)
        chips = [(1 - x, y), (x, 1 - y), (1 - x, 1 - y)]
        outs = (oa, ob, oc)
        oa[me] = a_ref[...].astype(BF16)
        ob[me] = b_ref[...].astype(BF16)
        oc[me] = c_ref[...].astype(BF16)

        def copy(arr, k, blk, to):
            r = outs[arr].at[blk]
            return pltpu.make_async_remote_copy(src_ref=r, dst_ref=r, send_sem=ssem.at[arr, k],
                                                recv_sem=rsem.at[arr, k], device_id=to, device_id_type=MESH)

        def idx(chip, core):
            return 4 * chip[0] + 2 * chip[1] + core

        first = []
        for arr in range(3):
            first.append(copy(arr, 0, me, sib))
            for j, chip in enumerate(chips):
                first.append(copy(arr, 1 + j, me, (chip[0], chip[1], cc)))
        for cp in first:
            cp.start()
        passed = []
        for j, chip in enumerate(chips):
            for arr in range(3):
                copy(arr, 1 + j, idx(chip, cc), (x, y, cc)).wait_recv()
                cp = copy(arr, 4 + j, idx(chip, cc), sib)
                cp.start()
                passed.append(cp)
        for arr in range(3):
            copy(arr, 0, idx((x, y), 1 - cc), (x, y, cc)).wait_recv()
            for j, chip in enumerate(chips):
                copy(arr, 4 + j, idx(chip, 1 - cc), (x, y, cc)).wait_recv()
        for cp in first + passed:
            cp.wait_send()

    vm = pl.BlockSpec(memory_space=pltpu.VMEM)
    return pl.pallas_call(
        body, name="wgather",
        out_shape=(jax.ShapeDtypeStruct((N_DEV,) + a.shape, BF16),
                   jax.ShapeDtypeStruct((N_DEV,) + b.shape, BF16),
                   jax.ShapeDtypeStruct((N_DEV,) + c.shape, BF16)),
        in_specs=[vm, vm, vm], out_specs=(vm, vm, vm),
        scratch_shapes=[pltpu.SemaphoreType.DMA((3, 7)), pltpu.SemaphoreType.DMA((3, 7))],
        compiler_params=_params(),
    )(a, b, c)


def _prep(rel_bias, w_sp, b_sp, buckets):
    def body(rb_ref, w_ref, b_ref, bk_ref, bias_ref, wt_ref, wtt_ref, bcol_ref):
        valid = _window_valid()
        bk = bk_ref[...]
        acc = [jnp.full((CHUNK, 2 * CHUNK), NEG, F32) for _ in range(4)]
        for b in range(N_BUCKETS):
            hit = (bk == b) & valid
            for h in range(4):
                acc[h] = jnp.where(hit, rb_ref[b, h], acc[h])
        for h in range(4):
            bias_ref[h] = acc[h]
        r = lax.broadcasted_iota(jnp.int32, (CHUNK, CHUNK), 0)
        c = lax.broadcasted_iota(jnp.int32, (CHUNK, CHUNK), 1)
        for g in range(A_GROUPS):
            w = jnp.where(r >= c, w_ref[g], 0.0)
            wt_ref[g] = w.astype(MM)
            wtt_ref[g] = w.T.astype(MM)
            bcol_ref[g] = jnp.broadcast_to(b_ref[g:g + 1, :], (CHUNK, CHUNK)).T

    return pl.pallas_call(
        body, name="prep",
        out_shape=(jax.ShapeDtypeStruct((4, CHUNK, 2 * CHUNK), F32),
                   jax.ShapeDtypeStruct((A_GROUPS, CHUNK, CHUNK), MM),
                   jax.ShapeDtypeStruct((A_GROUPS, CHUNK, CHUNK), MM),
                   jax.ShapeDtypeStruct((A_GROUPS, CHUNK, CHUNK), F32)),
        in_specs=[pl.BlockSpec(memory_space=pltpu.SMEM), pl.BlockSpec(memory_space=pltpu.VMEM),
                  pl.BlockSpec(memory_space=pltpu.VMEM), pl.BlockSpec(memory_space=pltpu.VMEM)],
        out_specs=tuple(pl.BlockSpec(memory_space=pltpu.VMEM) for _ in range(4)),
    )(rel_bias, w_sp, b_sp, buckets)


def _inproj_fwd(x2, g1, w_in_t, tm):
    t = x2.shape[0]

    def body(x_ref, g_ref, w_ref, uv_ref, qkv_ref, z_ref):
        xf = x_ref[...]
        r = lax.rsqrt(jnp.mean(xf * xf, axis=-1, keepdims=True) + EPS)
        h = (xf * r * g_ref[...]).astype(MM)
        uv_ref[...] = _dot_nt(h, w_ref[0:UV_W, :])
        qkv_ref[...] = _dot_nt(h, w_ref[UV_W:UV_W + QKV_W, :]).astype(MM)
        z_ref[...] = _dot_nt(h, w_ref[UV_W + QKV_W:IN_WIDTH, :])

    return pl.pallas_call(
        body, name="inproj_fwd", grid=(t // tm,),
        out_shape=(jax.ShapeDtypeStruct((t, UV_W), F32),
                   jax.ShapeDtypeStruct((t, QKV_W), MM),
                   jax.ShapeDtypeStruct((t, Z_W), F32)),
        in_specs=[pl.BlockSpec((tm, D_MODEL), lambda i: (i, 0)),
                  _full((1, D_MODEL)),
                  pl.BlockSpec((IN_WIDTH, D_MODEL), lambda i: (0, 0), pipeline_mode=pl.Buffered(1))],
        out_specs=(pl.BlockSpec((tm, UV_W), lambda i: (i, 0)),
                   pl.BlockSpec((tm, QKV_W), lambda i: (i, 0)),
                   pl.BlockSpec((tm, Z_W), lambda i: (i, 0))),
        compiler_params=_params(dimension_semantics=("arbitrary",)),
    )(x2, g1, w_in_t)


def _memkv_fwd(mem2, gm, w_mkv):
    tmem = mem2.shape[0]

    def body(m_ref, g_ref, w_ref, o_ref):
        xf = m_ref[...]
        r = lax.rsqrt(jnp.mean(xf * xf, axis=-1, keepdims=True) + EPS)
        hm = (xf * r * g_ref[...]).astype(MM)
        o_ref[...] = _dot(hm, w_ref[...]).astype(MM)

    vm = pl.BlockSpec(memory_space=pltpu.VMEM)
    return pl.pallas_call(
        body, name="memkv_fwd",
        out_shape=jax.ShapeDtypeStruct((tmem, 2 * MEM_LEN), MM),
        in_specs=[vm, vm, vm], out_specs=vm,
        compiler_params=_params(),
    )(mem2, gm, w_mkv)


def _memkv_bwd(dmkv, mem2, gm, w_mkv):
    def body(d_ref, m_ref, g_ref, w_ref, dw_ref, dg_ref):
        xf = m_ref[...]
        r = lax.rsqrt(jnp.mean(xf * xf, axis=-1, keepdims=True) + EPS)
        nm = xf * r
        hm = (nm * g_ref[...]).astype(MM)
        d = d_ref[...].astype(MM)
        dw_ref[...] = _dot_tn(hm, d)
        dhm = _dot_nt(d, w_ref[...])
        dg_ref[...] = jnp.sum(dhm * nm, axis=0, keepdims=True)

    vm = pl.BlockSpec(memory_space=pltpu.VMEM)
    return pl.pallas_call(
        body, name="memkv_bwd",
        out_shape=(jax.ShapeDtypeStruct((D_MODEL, 2 * MEM_LEN), F32),
                   jax.ShapeDtypeStruct((1, D_MODEL), F32)),
        in_specs=[vm, vm, vm, vm], out_specs=(vm, vm),
        compiler_params=_params(),
    )(dmkv, mem2, gm, w_mkv)


def _half_masks(rows):
    lane = lax.broadcasted_iota(jnp.int32, (rows, CHUNK), 1)
    return lane < 64


def _dup_heads(band):
    b32 = band.astype(F32)
    rolled = pltpu.roll(b32, 64, 1)
    lo = _half_masks(band.shape[0])
    return (jnp.where(lo, b32, rolled).astype(MM), jnp.where(lo, rolled, b32).astype(MM))


def _swa_probs(qsel, kd, bias_h, sink_h, first_add):
    s = _dot_nt(qsel, kd) * SCALE + bias_h + first_add
    m = jnp.maximum(jnp.max(s, axis=-1, keepdims=True), sink_h)
    p = jnp.exp(s - m)
    es = jnp.exp(sink_h - m)
    inv = 1.0 / (jnp.sum(p, axis=-1, keepdims=True) + es)
    return p * inv, es * inv


def _softmax(s):
    m = jnp.max(s, axis=-1, keepdims=True)
    p = jnp.exp(s - m)
    return p * (1.0 / jnp.sum(p, axis=-1, keepdims=True))


def _band_rows(n):
    cstart = pl.multiple_of(n * CHUNK, CHUNK)
    pstart = pl.multiple_of(jnp.maximum(n - 1, 0) * CHUNK, CHUNK)
    return pstart, cstart


def _first_block_mask(n):
    col = lax.broadcasted_iota(jnp.int32, (CHUNK, 2 * CHUNK), 1)
    return jnp.where((col < CHUNK) & (n == 0), NEG, 0.0)


def _spatial_group(uv_ref, r0, g, vg_ref, vb_ref, wt_ref, bcol_ref, with_grad):
    au = uv_ref[r0:r0 + CHUNK, g * CHUNK:(g + 1) * CHUNK]
    av = uv_ref[r0:r0 + CHUNK, A_WIDTH + g * CHUNK:A_WIDTH + (g + 1) * CHUNK]
    if with_grad:
        u, du = _gelu_and_grad(au)
        v, dv = _gelu_and_grad(av)
    else:
        u, v = _gelu(au), _gelu(av)
        du = dv = None
    mu = jnp.mean(v, axis=-1, keepdims=True)
    xc = v - mu
    rstd = lax.rsqrt(jnp.mean(xc * xc, axis=-1, keepdims=True) + EPS)
    xhat = xc * rstd
    gam = vg_ref[:, g * CHUNK:(g + 1) * CHUNK]
    vc = xhat * gam + vb_ref[:, g * CHUNK:(g + 1) * CHUNK]
    sv = _dot(wt_ref[g], vc.astype(MM)) + bcol_ref[g]
    return u, du, dv, rstd, xhat, gam, vc, sv


def _mix_fwd(uv, z, qkv3, mkv3, x2, tgt2, bias, sinks, vg, vb, wt, bcol, g2, w_o, tm):
    nb, s = qkv3.shape[0], qkv3.shape[1]
    nt = s // tm
    bpt = tm // CHUNK

    def body(uv_ref, z_ref, qkv_ref, mkv_ref, x_ref, t_ref, bias_ref, sink_ref, vg_ref, vb_ref, wt_ref,
             bcol_ref, g2_ref, wo_ref, dyc_ref, dz_ref, dxo_ref, dwo_ref, dg2_ref, loss_ref, ycat):
        b, j = pl.program_id(0), pl.program_id(1)

        @pl.when((b == 0) & (j == 0))
        def _():
            dwo_ref[...] = jnp.zeros_like(dwo_ref)
            dg2_ref[...] = jnp.zeros_like(dg2_ref)
            loss_ref[...] = jnp.zeros_like(loss_ref)

        lo = _half_masks(CHUNK)
        for blk in range(bpt):
            r0 = blk * CHUNK
            n = j * bpt + blk
            for g in range(A_GROUPS):
                u, _, _, _, _, _, _, sv = _spatial_group(uv_ref, r0, g, vg_ref, vb_ref, wt_ref, bcol_ref, False)
                ycat[r0:r0 + CHUNK, g * CHUNK:(g + 1) * CHUNK] = u * sv
            pstart, cstart = _band_rows(n)
            kb = jnp.concatenate([qkv_ref[pl.ds(pstart, CHUNK), 256:384], qkv_ref[pl.ds(cstart, CHUNK), 256:384]], axis=0)
            vbnd = jnp.concatenate([qkv_ref[pl.ds(pstart, CHUNK), 384:512], qkv_ref[pl.ds(cstart, CHUNK), 384:512]], axis=0)
            kd = _dup_heads(kb)
            vd = _dup_heads(vbnd)
            first_add = _first_block_mask(n)
            for kvh in range(2):
                q128 = qkv_ref[pl.ds(cstart, CHUNK), kvh * CHUNK:(kvh + 1) * CHUNK]
                outs = []
                for gi in range(2):
                    h = 2 * kvh + gi
                    qsel = jnp.where(lo if gi == 0 else ~lo, q128.astype(F32), 0.0).astype(MM)
                    probs, _ = _swa_probs(qsel, kd[kvh], bias_ref[h], sink_ref[h], first_add)
                    outs.append(_dot(probs.astype(MM), vd[kvh]))
                ycat[r0:r0 + CHUNK, A_WIDTH + kvh * CHUNK:A_WIDTH + (kvh + 1) * CHUNK] = jnp.where(lo, outs[0], outs[1])
        lot = _half_masks(tm)
        row0 = pl.multiple_of(j * tm, tm)
        for g in range(2):
            q128 = qkv_ref[pl.ds(row0, tm), 512 + g * CHUNK:512 + (g + 1) * CHUNK]
            k128 = mkv_ref[:, g * CHUNK:(g + 1) * CHUNK]
            v128 = mkv_ref[:, MEM_LEN + g * CHUNK:MEM_LEN + (g + 1) * CHUNK]
            outs = []
            for hh in range(2):
                qsel = jnp.where(lot if hh == 0 else ~lot, q128.astype(F32), 0.0).astype(MM)
                probs = _softmax(_dot_nt(qsel, k128) * SCALE)
                outs.append(_dot(probs.astype(MM), v128))
            ycat[:, 768 + g * CHUNK:768 + (g + 1) * CHUNK] = jnp.where(lot, outs[0], outs[1])
        zt = z_ref[...]
        sig = 1.0 / (1.0 + jnp.exp(-zt))
        silu = zt * sig
        yc = ycat[...]
        yb = (yc * silu).astype(MM)
        o = _dot(yb, wo_ref[...])
        r2 = lax.rsqrt(jnp.mean(o * o, axis=-1, keepdims=True) + EPS)
        nrm = o * r2
        g2v = g2_ref[...]
        e = x_ref[...] + nrm * g2v - t_ref[...]
        l1 = jnp.sum(e * e, axis=-1, keepdims=True)
        loss_ref[...] += jnp.broadcast_to(jnp.sum(l1, axis=0, keepdims=True) * (0.5 / D_MODEL), loss_ref.shape)
        dxo = e * (1.0 / D_MODEL)
        dxo_ref[...] = dxo
        dg2_ref[...] += jnp.sum(dxo * nrm, axis=0, keepdims=True)
        dn = dxo * g2v
        do = r2 * (dn - nrm * jnp.mean(dn * nrm, axis=-1, keepdims=True))
        dob = do.astype(MM)
        dy = _dot_nt(dob, wo_ref[...])
        dz_ref[...] = (dy * yc * (sig * (1.0 + zt * (1.0 - sig)))).astype(MM)
        dyc_ref[...] = dy * silu
        dwo_ref[...] += _dot_tn(yb, dob)

    t = nb * s
    tile = lambda w: pl.BlockSpec((tm, w), lambda b, j: (b * nt + j, 0))
    return pl.pallas_call(
        body, name="mix_fwd", grid=(nb, nt),
        out_shape=(jax.ShapeDtypeStruct((t, D_MODEL), F32),
                   jax.ShapeDtypeStruct((t, Z_W), MM),
                   jax.ShapeDtypeStruct((t, D_MODEL), F32),
                   jax.ShapeDtypeStruct((D_MODEL, D_MODEL), F32),
                   jax.ShapeDtypeStruct((1, D_MODEL), F32),
                   jax.ShapeDtypeStruct((8, CHUNK), F32)),
        in_specs=[tile(UV_W), tile(Z_W),
                  pl.BlockSpec((None, s, QKV_W), lambda b, j: (b, 0, 0)),
                  pl.BlockSpec((None, MEM_LEN, 2 * MEM_LEN), lambda b, j: (b, 0, 0)),
                  tile(D_MODEL), tile(D_MODEL),
                  _full((4, CHUNK, 2 * CHUNK)),
                  pl.BlockSpec(memory_space=pltpu.SMEM),
                  _full((1, A_WIDTH)), _full((1, A_WIDTH)),
                  _full((A_GROUPS, CHUNK, CHUNK)), _full((A_GROUPS, CHUNK, CHUNK)),
                  _full((1, D_MODEL)), _full((D_MODEL, D_MODEL))],
        out_specs=(tile(D_MODEL), tile(Z_W), tile(D_MODEL),
                   _full((D_MODEL, D_MODEL)), _full((1, D_MODEL)), _full((8, CHUNK))),
        scratch_shapes=[pltpu.VMEM((tm, D_MODEL), F32)],
        compiler_params=_params(dimension_semantics=("arbitrary", "arbitrary")),
    )(uv, z, qkv3, mkv3, x2, tgt2, bias, sinks, vg, vb, wt, bcol, g2, w_o)


def _mix_bwd(uv, dyc, qkv3, mkv3, bias, sinks, vg, vb, wt, wtt, bcol, buckets, tm):
    nb, s = qkv3.shape[0], qkv3.shape[1]
    nt = s // tm
    bpt = tm // CHUNK

    def body(uv_ref, dyc_ref, qkv_ref, mkv_ref, bias_ref, sink_ref, vg_ref, vb_ref, wt_ref, wtt_ref, bcol_ref,
             bk_ref, duv_ref, dqkv_ref, dmkv_ref, dwsp_ref, dbs_ref, dvg_ref, dvb_ref, dsink_ref, drel_ref,
             dkv_acc, dbias_acc, dsv_acc, dsink_acc):
        b, j = pl.program_id(0), pl.program_id(1)
        jt = nt - 1 - j

        @pl.when((b == 0) & (j == 0))
        def _():
            dwsp_ref[...] = jnp.zeros_like(dwsp_ref)
            dvg_ref[...] = jnp.zeros_like(dvg_ref)
            dvb_ref[...] = jnp.zeros_like(dvb_ref)
            dbias_acc[...] = jnp.zeros_like(dbias_acc)
            dsv_acc[...] = jnp.zeros_like(dsv_acc)
            dsink_acc[...] = jnp.zeros_like(dsink_acc)

        @pl.when(j == 0)
        def _():
            dmkv_ref[...] = jnp.zeros_like(dmkv_ref)
            dkv_acc[...] = jnp.zeros_like(dkv_acc)

        carry = dkv_acc[0:CHUNK, :]
        dkv_acc[...] = jnp.zeros_like(dkv_acc)
        dkv_acc[tm:tm + CHUNK, :] = carry

        lo = _half_masks(CHUNK)
        lob = _half_masks(2 * CHUNK)
        for blk in range(bpt):
            r0 = blk * CHUNK
            n = jt * bpt + blk
            for g in range(A_GROUPS):
                u, gu, gv, rstd, xhat, gam, vc, sv = _spatial_group(uv_ref, r0, g, vg_ref, vb_ref, wt_ref, bcol_ref, True)
                dya = dyc_ref[r0:r0 + CHUNK, g * CHUNK:(g + 1) * CHUNK]
                duv_ref[r0:r0 + CHUNK, g * CHUNK:(g + 1) * CHUNK] = (dya * sv * gu).astype(MM)
                dsv = dya * u
                dsvb = dsv.astype(MM)
                dsv_acc[g] += dsv
                dwsp_ref[g] += _dot_nt(dsvb, vc.astype(MM))
                dvc = _dot(wtt_ref[g], dsvb)
                dvg_ref[:, g * CHUNK:(g + 1) * CHUNK] += jnp.sum(dvc * xhat, axis=0, keepdims=True)
                dvb_ref[:, g * CHUNK:(g + 1) * CHUNK] += jnp.sum(dvc, axis=0, keepdims=True)
                dxh = dvc * gam
                dv = rstd * (dxh - jnp.mean(dxh, axis=-1, keepdims=True)
                             - xhat * jnp.mean(dxh * xhat, axis=-1, keepdims=True))
                duv_ref[r0:r0 + CHUNK, A_WIDTH + g * CHUNK:A_WIDTH + (g + 1) * CHUNK] = (dv * gv).astype(MM)
            pstart, cstart = _band_rows(n)
            kb = jnp.concatenate([qkv_ref[pl.ds(pstart, CHUNK), 256:384], qkv_ref[pl.ds(cstart, CHUNK), 256:384]], axis=0)
            vbnd = jnp.concatenate([qkv_ref[pl.ds(pstart, CHUNK), 384:512], qkv_ref[pl.ds(cstart, CHUNK), 384:512]], axis=0)
            kd = _dup_heads(kb)
            vd = _dup_heads(vbnd)
            first_add = _first_block_mask(n)
            dk_f, dv_f = [], []
            for kvh in range(2):
                q128 = qkv_ref[pl.ds(cstart, CHUNK), kvh * CHUNK:(kvh + 1) * CHUNK]
                do128 = dyc_ref[r0:r0 + CHUNK, A_WIDTH + kvh * CHUNK:A_WIDTH + (kvh + 1) * CHUNK]
                dq128 = jnp.zeros((CHUNK, CHUNK), F32)
                dkd = jnp.zeros((2 * CHUNK, CHUNK), F32)
                dvd = jnp.zeros((2 * CHUNK, CHUNK), F32)
                for gi in range(2):
                    h = 2 * kvh + gi
                    half = lo if gi == 0 else ~lo
                    qsel = jnp.where(half, q128.astype(F32), 0.0).astype(MM)
                    dosel = jnp.where(half, do128, 0.0).astype(MM)
                    probs, ps = _swa_probs(qsel, kd[kvh], bias_ref[h], sink_ref[h], first_add)
                    dp = _dot_nt(dosel, vd[kvh])
                    delta = jnp.sum(probs * dp, axis=-1, keepdims=True)
                    ds = probs * (dp - delta)
                    dbias_acc[h] += ds
                    dsink_acc[h:h + 1, :] += jnp.broadcast_to(
                        -jnp.sum(ps * delta, axis=0, keepdims=True), (1, CHUNK))
                    dss = (ds * SCALE).astype(MM)
                    dq128 = dq128 + jnp.where(half, _dot(dss, kd[kvh]), 0.0)
                    dkd = dkd + _dot_tn(dss, qsel)
                    dvd = dvd + _dot_tn(probs.astype(MM), dosel)
                dqkv_ref[r0:r0 + CHUNK, kvh * CHUNK:(kvh + 1) * CHUNK] = dq128.astype(MM)
                dk_f.append(dkd + pltpu.roll(dkd, 64, 1))
                dv_f.append(dvd + pltpu.roll(dvd, 64, 1))
            dkv_acc[r0:r0 + 2 * CHUNK, 0:CHUNK] += jnp.where(lob, dk_f[0], dk_f[1])
            dkv_acc[r0:r0 + 2 * CHUNK, CHUNK:2 * CHUNK] += jnp.where(lob, dv_f[0], dv_f[1])
        dqkv_ref[:, 256:512] = dkv_acc[CHUNK:CHUNK + tm, :].astype(MM)
        lot = _half_masks(tm)
        row0 = pl.multiple_of(jt * tm, tm)
        for g in range(2):
            q128 = qkv_ref[pl.ds(row0, tm), 512 + g * CHUNK:512 + (g + 1) * CHUNK]
            k128 = mkv_ref[:, g * CHUNK:(g + 1) * CHUNK]
            v128 = mkv_ref[:, MEM_LEN + g * CHUNK:MEM_LEN + (g + 1) * CHUNK]
            do128 = dyc_ref[:, 768 + g * CHUNK:768 + (g + 1) * CHUNK]
            dq128 = jnp.zeros((tm, CHUNK), F32)
            dk128 = jnp.zeros((MEM_LEN, CHUNK), F32)
            dv128 = jnp.zeros((MEM_LEN, CHUNK), F32)
            for hh in range(2):
                half = lot if hh == 0 else ~lot
                qsel = jnp.where(half, q128.astype(F32), 0.0).astype(MM)
                dosel = jnp.where(half, do128, 0.0).astype(MM)
                probs = _softmax(_dot_nt(qsel, k128) * SCALE)
                dp = _dot_nt(dosel, v128)
                ds = probs * (dp - jnp.sum(probs * dp, axis=-1, keepdims=True))
                dss = (ds * SCALE).astype(MM)
                dq128 = dq128 + jnp.where(half, _dot(dss, k128), 0.0)
                dk128 = dk128 + _dot_tn(dss, qsel)
                dv128 = dv128 + _dot_tn(probs.astype(MM), dosel)
            dqkv_ref[:, 512 + g * CHUNK:512 + (g + 1) * CHUNK] = dq128.astype(MM)
            dmkv_ref[:, g * CHUNK:(g + 1) * CHUNK] += dk128
            dmkv_ref[:, MEM_LEN + g * CHUNK:MEM_LEN + (g + 1) * CHUNK] += dv128

        @pl.when((b == nb - 1) & (j == nt - 1))
        def _():
            r = lax.broadcasted_iota(jnp.int32, (CHUNK, CHUNK), 0)
            c = lax.broadcasted_iota(jnp.int32, (CHUNK, CHUNK), 1)
            for g in range(A_GROUPS):
                dwsp_ref[g] = jnp.where(r >= c, dwsp_ref[g], 0.0)
                dbs_ref[g:g + 1, :] = jnp.sum(dsv_acc[g].T, axis=0, keepdims=True)
            rows = lax.broadcasted_iota(jnp.int32, (8, CHUNK), 0)
            cols = lax.broadcasted_iota(jnp.int32, (8, CHUNK), 1)
            sk = jnp.zeros((8, CHUNK), F32)
            for h in range(4):
                sk = sk + jnp.where((rows == 0) & (cols == h), jnp.broadcast_to(dsink_acc[h:h + 1, :], (8, CHUNK)), 0.0)
            dsink_ref[...] = sk
            bk = bk_ref[...]
            valid = _window_valid()
            rrow = lax.broadcasted_iota(jnp.int32, (N_BUCKETS, CHUNK), 0)
            rcol = lax.broadcasted_iota(jnp.int32, (N_BUCKETS, CHUNK), 1)
            acc = jnp.zeros((N_BUCKETS, CHUNK), F32)
            for bb in range(N_BUCKETS):
                hit = (bk == bb) & valid
                for h in range(4):
                    part = jnp.sum(jnp.where(hit, dbias_acc[h], 0.0), axis=-1, keepdims=True)
                    tot = jnp.sum(part, axis=0, keepdims=True)
                    acc = acc + jnp.where((rrow == bb) & (rcol == h), jnp.broadcast_to(tot, (N_BUCKETS, CHUNK)), 0.0)
            drel_ref[...] = acc

    t = nb * s
    tile = lambda w: pl.BlockSpec((tm, w), lambda b, j: (b * nt + nt - 1 - j, 0))
    return pl.pallas_call(
        body, name="mix_bwd", grid=(nb, nt),
        out_shape=(jax.ShapeDtypeStruct((t, UV_W), MM),
                   jax.ShapeDtypeStruct((t, QKV_W), MM),
                   jax.ShapeDtypeStruct((nb, MEM_LEN, 2 * MEM_LEN), F32),
                   jax.ShapeDtypeStruct((A_GROUPS, CHUNK, CHUNK), F32),
                   jax.ShapeDtypeStruct((A_GROUPS, CHUNK), F32),
                   jax.ShapeDtypeStruct((1, A_WIDTH), F32),
                   jax.ShapeDtypeStruct((1, A_WIDTH), F32),
                   jax.ShapeDtypeStruct((8, CHUNK), F32),
                   jax.ShapeDtypeStruct((N_BUCKETS, CHUNK), F32)),
        in_specs=[tile(UV_W), tile(D_MODEL),
                  pl.BlockSpec((None, s, QKV_W), lambda b, j: (b, 0, 0)),
                  pl.BlockSpec((None, MEM_LEN, 2 * MEM_LEN), lambda b, j: (b, 0, 0)),
                  _full((4, CHUNK, 2 * CHUNK)),
                  pl.BlockSpec(memory_space=pltpu.SMEM),
                  _full((1, A_WIDTH)), _full((1, A_WIDTH)),
                  _full((A_GROUPS, CHUNK, CHUNK)), _full((A_GROUPS, CHUNK, CHUNK)), _full((A_GROUPS, CHUNK, CHUNK)),
                  _full((CHUNK, 2 * CHUNK))],
        out_specs=(tile(UV_W), tile(QKV_W),
                   pl.BlockSpec((None, MEM_LEN, 2 * MEM_LEN), lambda b, j: (b, 0, 0)),
                   _full((A_GROUPS, CHUNK, CHUNK)), _full((A_GROUPS, CHUNK)),
                   _full((1, A_WIDTH)), _full((1, A_WIDTH)), _full((8, CHUNK)), _full((N_BUCKETS, CHUNK))),
        scratch_shapes=[pltpu.VMEM((tm + CHUNK, 2 * CHUNK), F32),
                        pltpu.VMEM((4, CHUNK, 2 * CHUNK), F32),
                        pltpu.VMEM((A_GROUPS, CHUNK, CHUNK), F32),
                        pltpu.VMEM((8, CHUNK), F32)],
        compiler_params=_params(dimension_semantics=("arbitrary", "arbitrary")),
    )(uv, dyc, qkv3, mkv3, bias, sinks, vg, vb, wt, wtt, bcol, buckets)


def _inproj_bwd(x2, dxo, duv, dqkv, dz, g1, w_in_t, tm):
    t = x2.shape[0]
    nt = t // tm

    def body(x_ref, dxo_ref, duv_ref, dqkv_ref, dz_ref, g_ref, w_ref, gx_ref, dw_hbm, dg_ref, acc, sem):
        i = pl.program_id(0)

        @pl.when(i == 0)
        def _():
            acc[...] = jnp.zeros_like(acc)
            dg_ref[...] = jnp.zeros_like(dg_ref)

        xf = x_ref[...]
        r = lax.rsqrt(jnp.mean(xf * xf, axis=-1, keepdims=True) + EPS)
        nx = xf * r
        gv = g_ref[...]
        h = (nx * gv).astype(MM)
        duv_t, dqkv_t, dz_t = duv_ref[...], dqkv_ref[...], dz_ref[...]
        acc[0:UV_W, :] += _dot_tn(duv_t, h)
        acc[UV_W:UV_W + QKV_W, :] += _dot_tn(dqkv_t, h)
        acc[UV_W + QKV_W:IN_WIDTH, :] += _dot_tn(dz_t, h)
        dh = (_dot(duv_t, w_ref[0:UV_W, :]) + _dot(dqkv_t, w_ref[UV_W:UV_W + QKV_W, :])
              + _dot(dz_t, w_ref[UV_W + QKV_W:IN_WIDTH, :]))
        dg_ref[...] += jnp.sum(dh * nx, axis=0, keepdims=True)
        dnx = dh * gv
        gx_ref[...] = dxo_ref[...] + r * (dnx - nx * jnp.mean(dnx * nx, axis=-1, keepdims=True))

        @pl.when(i == nt - 1)
        def _():
            cp = pltpu.make_async_copy(acc, dw_hbm, sem)
            cp.start()
            cp.wait()

    tile = lambda w: pl.BlockSpec((tm, w), lambda i: (i, 0))
    return pl.pallas_call(
        body, name="inproj_bwd", grid=(nt,),
        out_shape=(jax.ShapeDtypeStruct((t, D_MODEL), F32),
                   jax.ShapeDtypeStruct((IN_WIDTH, D_MODEL), F32),
                   jax.ShapeDtypeStruct((1, D_MODEL), F32)),
        in_specs=[tile(D_MODEL), tile(D_MODEL), tile(UV_W), tile(QKV_W), tile(Z_W),
                  _full((1, D_MODEL)),
                  pl.BlockSpec((IN_WIDTH, D_MODEL), lambda i: (0, 0), pipeline_mode=pl.Buffered(1))],
        out_specs=(tile(D_MODEL), pl.BlockSpec(memory_space=pl.ANY), _full((1, D_MODEL))),
        scratch_shapes=[pltpu.VMEM((IN_WIDTH, D_MODEL), F32), pltpu.SemaphoreType.DMA],
        compiler_params=_params(dimension_semantics=("arbitrary",)),
    )(x2, dxo, duv, dqkv, dz, g1, w_in_t)


def _adamw(w, g, m, v):
    m = ADAM_B1 * m + (1.0 - ADAM_B1) * g
    v = ADAM_B2 * v + (1.0 - ADAM_B2) * (g * g)
    m_hat = m / (1.0 - ADAM_B1 ** ADAM_STEP)
    v_hat = v / (1.0 - ADAM_B2 ** ADAM_STEP)
    delta = -ADAM_LR * (m_hat / (jnp.sqrt(v_hat) + ADAM_EPS) + ADAM_WD * w)
    return delta, m, v


_ROWS = 32


def _greduce(ga, gb, gc, gs, wmv):
    shapes = (ga.shape[1:], gb.shape[1:], gc.shape[1:])
    rs = gs.shape[0]

    def body(ga_ref, gb_ref, gc_ref, gs_ref,
             wa, ma, va, wb, mb, vb_, wc, mc, vc, ws, ms, vs,
             oga, oda, oma, ova, ogb, odb, omb, ovb, ogc, odc, omc, ovc, ogs, ods, oms, ovs,
             own_a, own_b, own_c, ra_a, ra_b, ra_c, sb_a, sb_b, sb_c, rb_a, rb_b, rb_c, rs_a, rs_b,
             ld_sem, sa_sem, ra_sem, sb_sem, rb_sem):
        x, y, cc = lax.axis_index("x"), lax.axis_index("y"), lax.axis_index("c")
        myq = 2 * x + y
        me = (x, y, cc)
        sib = (x, y, 1 - cc)
        chips = [(1 - x, y), (x, 1 - y), (1 - x, 1 - y)]
        gin = (ga_ref, gb_ref, gc_ref)
        own = (own_a, own_b, own_c)
        rcv_a = (ra_a, ra_b, ra_c)
        sbuf = (sb_a, sb_b, sb_c)
        rcv_b = (rb_a, rb_b, rb_c)

        def remote(src, dst, ssem, rsem, to):
            return pltpu.make_async_remote_copy(src_ref=src, dst_ref=dst, send_sem=ssem, recv_sem=rsem,
                                                device_id=to, device_id_type=MESH)

        loads, sends_a = [], []
        for arr in range(3):
            for q in range(4):
                loads.append(pltpu.make_async_copy(gin[arr].at[2 * q + cc], own[arr].at[q], ld_sem.at[arr, q]))
                sends_a.append(remote(gin[arr].at[2 * q + 1 - cc], rcv_a[arr].at[q],
                                      sa_sem.at[arr, q], ra_sem.at[arr, q], sib))
        small_a = remote(gs_ref, rs_a, sa_sem.at[3, 0], ra_sem.at[3, 0], sib)
        for cp in loads + sends_a + [small_a]:
            cp.start()
        for cp in loads:
            cp.wait()
        for arr in range(3):
            for q in range(4):
                remote(gin[arr].at[2 * q + 1 - cc], rcv_a[arr].at[q],
                       sa_sem.at[arr, q], ra_sem.at[arr, q], me).wait_recv()
        remote(gs_ref, rs_a, sa_sem.at[3, 0], ra_sem.at[3, 0], me).wait_recv()

        for arr in range(3):
            nrow = shapes[arr][0]

            def add_rows(i, _, arr=arr):
                r = pl.ds(pl.multiple_of(i * _ROWS, _ROWS), _ROWS)
                for q in range(4):
                    rcv_a[arr][q, r, :] = rcv_a[arr][q, r, :] + own[arr][q, r, :]
                return 0

            lax.fori_loop(0, nrow // _ROWS, add_rows, 0)
        rs_b[myq] = gs_ref[...] + rs_a[...]

        sends_b = []
        for j, chip in enumerate(chips):
            qj = 2 * chip[0] + chip[1]
            to = (chip[0], chip[1], cc)
            for arr in range(3):
                nrow = shapes[arr][0]

                def cast_rows(i, _, arr=arr, j=j, qj=qj):
                    r = pl.ds(pl.multiple_of(i * _ROWS, _ROWS), _ROWS)
                    sbuf[arr][j, r, :] = rcv_a[arr][qj, r, :].astype(BF16)
                    return 0

                lax.fori_loop(0, nrow // _ROWS, cast_rows, 0)
                cp = remote(sbuf[arr].at[j], rcv_b[arr].at[j], sb_sem.at[arr, j], rb_sem.at[arr, j], to)
                cp.start()
                sends_b.append(cp)
            cp = remote(rs_b.at[myq], rs_b.at[myq], sb_sem.at[3, j], rb_sem.at[3, j], to)
            cp.start()
            sends_b.append(cp)
        for j in range(3):
            for arr in range(3):
                remote(sbuf[arr].at[j], rcv_b[arr].at[j], sb_sem.at[arr, j], rb_sem.at[arr, j], me).wait_recv()
            remote(rs_b.at[myq], rs_b.at[myq], sb_sem.at[3, j], rb_sem.at[3, j], me).wait_recv()

        big = ((wa, ma, va, oga, oda, oma, ova), (wb, mb, vb_, ogb, odb, omb, ovb), (wc, mc, vc, ogc, odc, omc, ovc))
        for arr in range(3):
            w_r, m_r, v_r, og, od, om, ov = big[arr]
            nrow = shapes[arr][0]

            def upd(i, _, arr=arr, w_r=w_r, m_r=m_r, v_r=v_r, og=og, od=od, om=om, ov=ov):
                r = pl.ds(pl.multiple_of(i * _ROWS, _ROWS), _ROWS)
                g = rcv_a[arr][myq, r, :]
                for j in range(3):
                    g = g + rcv_b[arr][j, r, :].astype(F32)
                d, m, v = _adamw(w_r[r, :], g, m_r[r, :], v_r[r, :])
                og[r, :] = g
                od[r, :] = d
                om[r, :] = m
                ov[r, :] = v
                return 0

            lax.fori_loop(0, nrow // _ROWS, upd, 0)

        def upd_s(i, _):
            r = pl.ds(pl.multiple_of(i * 8, 8), 8)
            g = ((rs_b[0, r, :] + rs_b[1, r, :]) + rs_b[2, r, :]) + rs_b[3, r, :]
            d, m, v = _adamw(ws[r, :], g, ms[r, :], vs[r, :])
            ogs[r, :] = g
            ods[r, :] = d
            oms[r, :] = m
            ovs[r, :] = v
            return 0

        lax.fori_loop(0, rs // 8, upd_s, 0)

        for cp in sends_a + [small_a] + sends_b:
            cp.wait_send()

    vm = pl.BlockSpec(memory_space=pltpu.VMEM)
    anyspec = pl.BlockSpec(memory_space=pl.ANY)
    big_out = []
    for shp in shapes:
        big_out += [jax.ShapeDtypeStruct(shp, F32)] * 4
    out_shape = tuple(big_out + [jax.ShapeDtypeStruct((rs, CHUNK), F32)] * 4)
    scratch = ([pltpu.VMEM((4,) + shp, F32) for shp in shapes]
               + [pltpu.VMEM((4,) + shp, F32) for shp in shapes]
               + [pltpu.VMEM((3,) + shp, BF16) for shp in shapes]
               + [pltpu.VMEM((3,) + shp, BF16) for shp in shapes]
               + [pltpu.VMEM((rs, CHUNK), F32), pltpu.VMEM((4, rs, CHUNK), F32)]
               + [pltpu.SemaphoreType.DMA((3, 4)), pltpu.SemaphoreType.DMA((4, 4)), pltpu.SemaphoreType.DMA((4, 4)),
                  pltpu.SemaphoreType.DMA((4, 3)), pltpu.SemaphoreType.DMA((4, 3))])
    return pl.pallas_call(
        body, name="greduce",
        out_shape=out_shape,
        in_specs=[anyspec, anyspec, anyspec, vm] + [vm] * 12,
        out_specs=tuple([vm] * 16),
        scratch_shapes=scratch,
        compiler_params=_params(),
    )(ga, gb, gc, gs, *wmv)


_SMALL = (("pre_norm_g", (8, CHUNK)), ("post_norm_g", (8, CHUNK)), ("mem_norm_g", (8, CHUNK)),
          ("v_norm_g", (4, CHUNK)), ("v_norm_b", (4, CHUNK)), ("b_spatial", (4, CHUNK)),
          ("attn_sinks", (1, 4)), ("rel_bias", (N_BUCKETS, 4)), ("w_spatial", (A_GROUPS * CHUNK, CHUNK)))


def _pack_small(parts):
    rows = []
    for (name, shp), p in zip(_SMALL, parts):
        p = p.reshape(shp)
        pr = (-shp[0]) % 8
        rows.append(jnp.pad(p, ((0, pr), (0, CHUNK - shp[1]))))
    return jnp.concatenate(rows, axis=0)


def _unpack_small(packed, shapes):
    out, r = [], 0
    for (name, shp), full in zip(_SMALL, shapes):
        out.append(packed[r:r + shp[0], 0:shp[1]].reshape(full))
        r += shp[0] + (-shp[0]) % 8
    return out


def _local_step(x, mem, loss_target, pre_norm_g, post_norm_g, mem_norm_g, v_norm_g, v_norm_b, w_spatial, b_spatial,
                attn_sinks, rel_bias, w_in_t, w_o, w_mkv):
    nb, s, _ = x.shape
    t = nb * s
    x2 = x.reshape(t, D_MODEL)
    tgt2 = loss_target.reshape(t, D_MODEL)
    mem2 = mem.reshape(nb * MEM_LEN, D_MODEL)
    tm_mix = min(256, s)
    tm_proj = min(512, t)

    buckets = jnp.asarray(_t5_buckets())
    sinks = attn_sinks.reshape(4)
    bias, wt, wtt, bcol = _prep(rel_bias, w_spatial[0], b_spatial[0], buckets)

    uv, qkv, z = _inproj_fwd(x2, pre_norm_g, w_in_t, tm_proj)
    mkv = _memkv_fwd(mem2, mem_norm_g, w_mkv)
    qkv3 = qkv.reshape(nb, s, QKV_W)
    mkv3 = mkv.reshape(nb, MEM_LEN, 2 * MEM_LEN)
    dyc, dz, dxo, dwo, dg2, loss_p = _mix_fwd(uv, z, qkv3, mkv3, x2, tgt2, bias, sinks, v_norm_g, v_norm_b,
                                              wt, bcol, post_norm_g, w_o, tm_mix)
    duv, dqkv, dmkv, dwsp, dbs, dvg, dvb, dsink, drel = _mix_bwd(uv, dyc, qkv3, mkv3, bias, sinks, v_norm_g, v_norm_b,
                                                                 wt, wtt, bcol, buckets, tm_mix)
    dwmkv, dgm = _memkv_bwd(dmkv.reshape(nb * MEM_LEN, 2 * MEM_LEN), mem2, mem_norm_g, w_mkv)
    gx, dw_in_t, dg1 = _inproj_bwd(x2, dxo, duv, dqkv, dz, pre_norm_g, w_in_t, tm_proj)
    small = [dg1, dg2, dgm, dvg, dvb, dbs, dsink[0:1, 0:4], drel[:, 0:4], dwsp]
    return loss_p, gx.reshape(nb, s, D_MODEL), dw_in_t, dwo, dwmkv, small


def kernel(x, mem, pre_norm_g, post_norm_g, mem_norm_g, w_in, w_mem_kv, v_norm_g, v_norm_b, w_spatial, b_spatial, attn_sinks, rel_bias, w_out, loss_target, m_pre_norm_g, m_post_norm_g, m_mem_norm_g, m_w_in, m_w_mem_kv, m_v_norm_g, m_v_norm_b, m_w_spatial, m_b_spatial, m_attn_sinks, m_rel_bias, m_w_out, v_pre_norm_g, v_post_norm_g, v_mem_norm_g, v_w_in, v_w_mem_kv, v_v_norm_g, v_v_norm_b, v_w_spatial, v_b_spatial, v_attn_sinks, v_rel_bias, v_w_out):
    nb, s, _ = x.shape

    sh_a = (w_in[0].T, m_w_in[0].T, v_w_in[0].T)
    sh_b = (w_out[0], m_w_out[0], v_w_out[0])
    sh_c = (w_mem_kv[0], m_w_mem_kv[0], v_w_mem_kv[0])
    wa, wb, wc = _wgather(sh_a[0], sh_b[0], sh_c[0])
    w_in_t = wa.reshape(IN_WIDTH, D_MODEL)
    w_o = wb.reshape(D_MODEL, D_MODEL)
    w_mkv = wc.reshape(D_MODEL, 2 * MEM_LEN)

    loss_p, gx, dw_in_t, dwo, dwmkv, small_grads = _local_step(
        x, mem, loss_target, pre_norm_g, post_norm_g, mem_norm_g, v_norm_g, v_norm_b, w_spatial, b_spatial,
        attn_sinks, rel_bias, w_in_t, w_o, w_mkv)

    small_names = [n for n, _ in _SMALL]
    given = dict(pre_norm_g=(pre_norm_g, m_pre_norm_g, v_pre_norm_g), post_norm_g=(post_norm_g, m_post_norm_g, v_post_norm_g),
                 mem_norm_g=(mem_norm_g, m_mem_norm_g, v_mem_norm_g), v_norm_g=(v_norm_g, m_v_norm_g, v_v_norm_g),
                 v_norm_b=(v_norm_b, m_v_norm_b, v_v_norm_b), b_spatial=(b_spatial, m_b_spatial, v_b_spatial),
                 attn_sinks=(attn_sinks, m_attn_sinks, v_attn_sinks), rel_bias=(rel_bias, m_rel_bias, v_rel_bias),
                 w_spatial=(w_spatial, m_w_spatial, v_w_spatial))
    small_shapes = [given[n][0].shape for n in small_names]
    gs = _pack_small(small_grads)
    wmv_s = tuple(_pack_small([given[n][k] for n in small_names]) for k in range(3))

    outs = _greduce(dw_in_t.reshape(N_DEV, SHARD_IN, D_MODEL), dwo.reshape(N_DEV, SHARD_O, D_MODEL),
                    dwmkv.reshape(N_DEV, SHARD_O, 2 * MEM_LEN), gs,
                    (*sh_a, *sh_b, *sh_c, *wmv_s))
    ra, rb, rc, rsm = outs[0:4], outs[4:8], outs[8:12], outs[12:16]
    loss = lax.psum(loss_p[0, 0], ("x", "y", "c"))

    res = {}
    for k, kind in enumerate(("grad", "delta", "new_m", "new_v")):
        res[kind, "w_in"] = ra[k].T[None]
        res[kind, "w_out"] = rb[k][None]
        res[kind, "w_mem_kv"] = rc[k][None]
        for n, val in zip(small_names, _unpack_small(rsm[k], small_shapes)):
            res[kind, n] = val
    order = ["pre_norm_g", "post_norm_g", "mem_norm_g", "w_in", "w_mem_kv", "v_norm_g", "v_norm_b", "w_spatial",
             "b_spatial", "attn_sinks", "rel_bias", "w_out"]
    flat = [res[kind, n] for kind in ("grad", "delta", "new_m", "new_v") for n in order]
    return (loss, gx.reshape(nb, s, D_MODEL), *flat)
```

```python
import functools

import numpy as np
import jax
import jax.numpy as jnp
from jax import lax
from jax.experimental import pallas as pl
from jax.experimental.pallas import tpu as pltpu

F32 = jnp.float32
BF16 = jnp.bfloat16
MM = jnp.bfloat16

D_MODEL = 1024
CHUNK = 128
A_GROUPS = 4
A_WIDTH = 512
UV_W = 1024
QKV_W = 768
Z_W = 1024
IN_WIDTH = UV_W + QKV_W + Z_W
MEM_LEN = 256
N_BUCKETS = 32
MAX_DISTANCE = 128
EPS = 1e-6
NEG = -1e30
SCALE = 0.125
N_DEV = 8
SHARD_IN = IN_WIDTH // N_DEV
SHARD_O = D_MODEL // N_DEV

ADAM_LR = 0.001
ADAM_B1 = 0.9
ADAM_B2 = 0.999
ADAM_EPS = 1e-08
ADAM_WD = 0.01
ADAM_STEP = 10

VMEM_LIMIT = 58 * 1024 * 1024

_GELU_C = 0.7978845608028654
_GELU_A = 0.044715

MESH = pl.DeviceIdType.MESH


def _dot(a, b):
    return lax.dot_general(a, b, (((1,), (0,)), ((), ())), preferred_element_type=F32)


def _dot_nt(a, b):
    return lax.dot_general(a, b, (((1,), (1,)), ((), ())), preferred_element_type=F32)


def _dot_tn(a, b):
    return lax.dot_general(a, b, (((0,), (0,)), ((), ())), preferred_element_type=F32)


def _gelu(x):
    x2 = x * x
    t = jnp.tanh(_GELU_C * (x + _GELU_A * x * x2))
    return 0.5 * x * (1.0 + t)


def _gelu_and_grad(x):
    x2 = x * x
    t = jnp.tanh(_GELU_C * (x + _GELU_A * x * x2))
    g = 0.5 * x * (1.0 + t)
    dg = 0.5 * (1.0 + t) + 0.5 * x * (1.0 - t * t) * (_GELU_C * (1.0 + 3.0 * _GELU_A * x2))
    return g, dg


def _t5_buckets():
    qi = np.arange(CHUNK)[:, None]
    kj = np.arange(2 * CHUNK)[None, :]
    n = np.maximum(qi + CHUNK - kj, 0)
    max_exact = N_BUCKETS // 2
    large = max_exact + (np.log(np.maximum(n, 1) / max_exact) / np.log(MAX_DISTANCE / max_exact)
                         * (N_BUCKETS - max_exact)).astype(np.int32)
    large = np.minimum(large, N_BUCKETS - 1)
    return np.where(n < max_exact, n, large).astype(np.int32)


def _params(**kw):
    return pltpu.CompilerParams(vmem_limit_bytes=VMEM_LIMIT, **kw)


def _full(shape):
    nd = len(shape)
    return pl.BlockSpec(shape, lambda *_: (0,) * nd)


def _window_valid():
    qi = lax.broadcasted_iota(jnp.int32, (CHUNK, 2 * CHUNK), 0)
    kj = lax.broadcasted_iota(jnp.int32, (CHUNK, 2 * CHUNK), 1)
    dist = qi + CHUNK - kj
    return (dist >= 0) & (dist < CHUNK)


def _wgather(a, b, c):
    def body(a_ref, b_ref, c_ref, oa, ob, oc, ssem, rsem):
        x, y, cc = lax.axis_index("x"), lax.axis_index("y"), lax.axis_index("c")
        me = 4 * x + 2 * y + cc
        sib = (x, y, 1 - cc)
        chips = [(1 - x, y), (x, 1 - y), (1 - x, 1 - y)]
        outs = (oa, ob, oc)
        oa[me] = a_ref[...].astype(BF16)
        ob[me] = b_ref[...].astype(BF16)
        oc[me] = c_ref[...].astype(BF16)

        def copy(arr, k, blk, to):
            r = outs[arr].at[blk]
            return pltpu.make_async_remote_copy(src_ref=r, dst_ref=r, send_sem=ssem.at[arr, k],
                                                recv_sem=rsem.at[arr, k], device_id=to, device_id_type=MESH)

        def idx(chip, core):
            return 4 * chip[0] + 2 * chip[1] + core

        first = []
        for arr in range(3):
            first.append(copy(arr, 0, me, sib))
            for j, chip in enumerate(chips):
                first.append(copy(arr, 1 + j, me, (chip[0], chip[1], cc)))
        for cp in first:
            cp.start()
        passed = []
        for j, chip in enumerate(chips):
            for arr in range(3):
                copy(arr, 1 + j, idx(chip, cc), (x, y, cc)).wait_recv()
                cp = copy(arr, 4 + j, idx(chip, cc), sib)
                cp.start()
                passed.append(cp)
        for arr in range(3):
            copy(arr, 0, idx((x, y), 1 - cc), (x, y, cc)).wait_recv()
            for j, chip in enumerate(chips):
                copy(arr, 4 + j, idx(chip, 1 - cc), (x, y, cc)).wait_recv()
        for cp in first + passed:
            cp.wait_send()

    vm = pl.BlockSpec(memory_space=pltpu.VMEM)
    return pl.pallas_call(
        body, name="wgather",
        out_shape=(jax.ShapeDtypeStruct((N_DEV,) + a.shape, BF16),
                   jax.ShapeDtypeStruct((N_DEV,) + b.shape, BF16),
                   jax.ShapeDtypeStruct((N_DEV,) + c.shape, BF16)),
        in_specs=[vm, vm, vm], out_specs=(vm, vm, vm),
        scratch_shapes=[pltpu.SemaphoreType.DMA((3, 7)), pltpu.SemaphoreType.DMA((3, 7))],
        compiler_params=_params(),
    )(a, b, c)


def _prep(rel_bias, w_sp, b_sp, buckets):
    def body(rb_ref, w_ref, b_ref, bk_ref, bias_ref, wt_ref, wtt_ref, bcol_ref):
        valid = _window_valid()
        bk = bk_ref[...]
        acc = [jnp.full((CHUNK, 2 * CHUNK), NEG, F32) for _ in range(4)]
        for b in range(N_BUCKETS):
            hit = (bk == b) & valid
            for h in range(4):
                acc[h] = jnp.where(hit, rb_ref[b, h], acc[h])
        for h in range(4):
            bias_ref[h] = acc[h]
        r = lax.broadcasted_iota(jnp.int32, (CHUNK, CHUNK), 0)
        c = lax.broadcasted_iota(jnp.int32, (CHUNK, CHUNK), 1)
        for g in range(A_GROUPS):
            w = jnp.where(r >= c, w_ref[g], 0.0)
            wt_ref[g] = w.astype(MM)
            wtt_ref[g] = w.T.astype(MM)
            bcol_ref[g] = jnp.broadcast_to(b_ref[g:g + 1, :], (CHUNK, CHUNK)).T

    return pl.pallas_call(
        body, name="prep",
        out_shape=(jax.ShapeDtypeStruct((4, CHUNK, 2 * CHUNK), F32),
                   jax.ShapeDtypeStruct((A_GROUPS, CHUNK, CHUNK), MM),
                   jax.ShapeDtypeStruct((A_GROUPS, CHUNK, CHUNK), MM),
                   jax.ShapeDtypeStruct((A_GROUPS, CHUNK, CHUNK), F32)),
        in_specs=[pl.BlockSpec(memory_space=pltpu.SMEM), pl.BlockSpec(memory_space=pltpu.VMEM),
                  pl.BlockSpec(memory_space=pltpu.VMEM), pl.BlockSpec(memory_space=pltpu.VMEM)],
        out_specs=tuple(pl.BlockSpec(memory_space=pltpu.VMEM) for _ in range(4)),
    )(rel_bias, w_sp, b_sp, buckets)


def _inproj_fwd(x2, g1, w_in_t, tm):
    t = x2.shape[0]

    def body(x_ref, g_ref, w_ref, uv_ref, qkv_ref, z_ref):
        xf = x_ref[...]
        r = lax.rsqrt(jnp.mean(xf * xf, axis=-1, keepdims=True) + EPS)
        h = (xf * r * g_ref[...]).astype(MM)
        uv_ref[...] = _dot_nt(h, w_ref[0:UV_W, :])
        qkv_ref[...] = _dot_nt(h, w_ref[UV_W:UV_W + QKV_W, :]).astype(MM)
        z_ref[...] = _dot_nt(h, w_ref[UV_W + QKV_W:IN_WIDTH, :])

    return pl.pallas_call(
        body, name="inproj_fwd", grid=(t // tm,),
        out_shape=(jax.ShapeDtypeStruct((t, UV_W), F32),
                   jax.ShapeDtypeStruct((t, QKV_W), MM),
                   jax.ShapeDtypeStruct((t, Z_W), F32)),
        in_specs=[pl.BlockSpec((tm, D_MODEL), lambda i: (i, 0)),
                  _full((1, D_MODEL)),
                  pl.BlockSpec((IN_WIDTH, D_MODEL), lambda i: (0, 0), pipeline_mode=pl.Buffered(1))],
        out_specs=(pl.BlockSpec((tm, UV_W), lambda i: (i, 0)),
                   pl.BlockSpec((tm, QKV_W), lambda i: (i, 0)),
                   pl.BlockSpec((tm, Z_W), lambda i: (i, 0))),
        compiler_params=_params(dimension_semantics=("arbitrary",)),
    )(x2, g1, w_in_t)


def _memkv_fwd(mem2, gm, w_mkv):
    tmem = mem2.shape[0]

    def body(m_ref, g_ref, w_ref, o_ref):
        xf = m_ref[...]
        r = lax.rsqrt(jnp.mean(xf * xf, axis=-1, keepdims=True) + EPS)
        hm = (xf * r * g_ref[...]).astype(MM)
        o_ref[...] = _dot(hm, w_ref[...]).astype(MM)

    vm = pl.BlockSpec(memory_space=pltpu.VMEM)
    return pl.pallas_call(
        body, name="memkv_fwd",
        out_shape=jax.ShapeDtypeStruct((tmem, 2 * MEM_LEN), MM),
        in_specs=[vm, vm, vm], out_specs=vm,
        compiler_params=_params(),
    )(mem2, gm, w_mkv)


def _memkv_bwd(dmkv, mem2, gm, w_mkv):
    def body(d_ref, m_ref, g_ref, w_ref, dw_ref, dg_ref):
        xf = m_ref[...]
        r = lax.rsqrt(jnp.mean(xf * xf, axis=-1, keepdims=True) + EPS)
        nm = xf * r
        hm = (nm * g_ref[...]).astype(MM)
        d = d_ref[...].astype(MM)
        dw_ref[...] = _dot_tn(hm, d)
        dhm = _dot_nt(d, w_ref[...])
        dg_ref[...] = jnp.sum(dhm * nm, axis=0, keepdims=True)

    vm = pl.BlockSpec(memory_space=pltpu.VMEM)
    return pl.pallas_call(
        body, name="memkv_bwd",
        out_shape=(jax.ShapeDtypeStruct((D_MODEL, 2 * MEM_LEN), F32),
                   jax.ShapeDtypeStruct((1, D_MODEL), F32)),
        in_specs=[vm, vm, vm, vm], out_specs=(vm, vm),
        compiler_params=_params(),
    )(dmkv, mem2, gm, w_mkv)


def _half_masks(rows):
    lane = lax.broadcasted_iota(jnp.int32, (rows, CHUNK), 1)
    return lane < 64


def _dup_heads(band):
    b32 = band.astype(F32)
    rolled = pltpu.roll(b32, 64, 1)
    lo = _half_masks(band.shape[0])
    return (jnp.where(lo, b32, rolled).astype(MM), jnp.where(lo, rolled, b32).astype(MM))


def _swa_probs(qsel, kd, bias_h, sink_h, first_add):
    s = _dot_nt(qsel, kd) * SCALE + bias_h + first_add
    m = jnp.maximum(jnp.max(s, axis=-1, keepdims=True), sink_h)
    p = jnp.exp(s - m)
    es = jnp.exp(sink_h - m)
    inv = 1.0 / (jnp.sum(p, axis=-1, keepdims=True) + es)
    return p * inv, es * inv


def _softmax(s):
    m = jnp.max(s, axis=-1, keepdims=True)
    p = jnp.exp(s - m)
    return p * (1.0 / jnp.sum(p, axis=-1, keepdims=True))


def _band_rows(n):
    cstart = pl.multiple_of(n * CHUNK, CHUNK)
    pstart = pl.multiple_of(jnp.maximum(n - 1, 0) * CHUNK, CHUNK)
    return pstart, cstart


def _first_block_mask(n):
    col = lax.broadcasted_iota(jnp.int32, (CHUNK, 2 * CHUNK), 1)
    return jnp.where((col < CHUNK) & (n == 0), NEG, 0.0)


def _spatial_group(uv_ref, r0, g, vg_ref, vb_ref, wt_ref, bcol_ref, with_grad):
    au = uv_ref[r0:r0 + CHUNK, g * CHUNK:(g + 1) * CHUNK]
    av = uv_ref[r0:r0 + CHUNK, A_WIDTH + g * CHUNK:A_WIDTH + (g + 1) * CHUNK]
    if with_grad:
        u, du = _gelu_and_grad(au)
        v, dv = _gelu_and_grad(av)
    else:
        u, v = _gelu(au), _gelu(av)
        du = dv = None
    mu = jnp.mean(v, axis=-1, keepdims=True)
    xc = v - mu
    rstd = lax.rsqrt(jnp.mean(xc * xc, axis=-1, keepdims=True) + EPS)
    xhat = xc * rstd
    gam = vg_ref[:, g * CHUNK:(g + 1) * CHUNK]
    vc = xhat * gam + vb_ref[:, g * CHUNK:(g + 1) * CHUNK]
    sv = _dot(wt_ref[g], vc.astype(MM)) + bcol_ref[g]
    return u, du, dv, rstd, xhat, gam, vc, sv


def _mix_fwd(uv, z, qkv3, mkv3, x2, tgt2, bias, sinks, vg, vb, wt, bcol, g2, w_o, tm):
    nb, s = qkv3.shape[0], qkv3.shape[1]
    nt = s // tm
    bpt = tm // CHUNK

    def body(uv_ref, z_ref, qkv_ref, mkv_ref, x_ref, t_ref, bias_ref, sink_ref, vg_ref, vb_ref, wt_ref,
             bcol_ref, g2_ref, wo_ref, dyc_ref, dz_ref, dxo_ref, dwo_ref, dg2_ref, loss_ref, ycat):
        b, j = pl.program_id(0), pl.program_id(1)

        @pl.when((b == 0) & (j == 0))
        def _():
            dwo_ref[...] = jnp.zeros_like(dwo_ref)
            dg2_ref[...] = jnp.zeros_like(dg2_ref)
            loss_ref[...] = jnp.zeros_like(loss_ref)

        lo = _half_masks(CHUNK)
        for blk in range(bpt):
            r0 = blk * CHUNK
            n = j * bpt + blk
            for g in range(A_GROUPS):
                u, _, _, _, _, _, _, sv = _spatial_group(uv_ref, r0, g, vg_ref, vb_ref, wt_ref, bcol_ref, False)
                ycat[r0:r0 + CHUNK, g * CHUNK:(g + 1) * CHUNK] = u * sv
            pstart, cstart = _band_rows(n)
            kb = jnp.concatenate([qkv_ref[pl.ds(pstart, CHUNK), 256:384], qkv_ref[pl.ds(cstart, CHUNK), 256:384]], axis=0)
            vbnd = jnp.concatenate([qkv_ref[pl.ds(pstart, CHUNK), 384:512], qkv_ref[pl.ds(cstart, CHUNK), 384:512]], axis=0)
            kd = _dup_heads(kb)
            vd = _dup_heads(vbnd)
            first_add = _first_block_mask(n)
            for kvh in range(2):
                q128 = qkv_ref[pl.ds(cstart, CHUNK), kvh * CHUNK:(kvh + 1) * CHUNK]
                outs = []
                for gi in range(2):
                    h = 2 * kvh + gi
                    qsel = jnp.where(lo if gi == 0 else ~lo, q128.astype(F32), 0.0).astype(MM)
                    probs, _ = _swa_probs(qsel, kd[kvh], bias_ref[h], sink_ref[h], first_add)
                    outs.append(_dot(probs.astype(MM), vd[kvh]))
                ycat[r0:r0 + CHUNK, A_WIDTH + kvh * CHUNK:A_WIDTH + (kvh + 1) * CHUNK] = jnp.where(lo, outs[0], outs[1])
        lot = _half_masks(tm)
        row0 = pl.multiple_of(j * tm, tm)
        for g in range(2):
            q128 = qkv_ref[pl.ds(row0, tm), 512 + g * CHUNK:512 + (g + 1) * CHUNK]
            k128 = mkv_ref[:, g * CHUNK:(g + 1) * CHUNK]
            v128 = mkv_ref[:, MEM_LEN + g * CHUNK:MEM_LEN + (g + 1) * CHUNK]
            outs = []
            for hh in range(2):
                qsel = jnp.where(lot if hh == 0 else ~lot, q128.astype(F32), 0.0).astype(MM)
                probs = _softmax(_dot_nt(qsel, k128) * SCALE)
                outs.append(_dot(probs.astype(MM), v128))
            ycat[:, 768 + g * CHUNK:768 + (g + 1) * CHUNK] = jnp.where(lot, outs[0], outs[1])
        zt = z_ref[...]
        sig = 1.0 / (1.0 + jnp.exp(-zt))
        silu = zt * sig
        yc = ycat[...]
        yb = (yc * silu).astype(MM)
        o = _dot(yb, wo_ref[...])
        r2 = lax.rsqrt(jnp.mean(o * o, axis=-1, keepdims=True) + EPS)
        nrm = o * r2
        g2v = g2_ref[...]
        e = x_ref[...] + nrm * g2v - t_ref[...]
        l1 = jnp.sum(e * e, axis=-1, keepdims=True)
        loss_ref[...] += jnp.broadcast_to(jnp.sum(l1, axis=0, keepdims=True) * (0.5 / D_MODEL), loss_ref.shape)
        dxo = e * (1.0 / D_MODEL)
        dxo_ref[...] = dxo
        dg2_ref[...] += jnp.sum(dxo * nrm, axis=0, keepdims=True)
        dn = dxo * g2v
        do = r2 * (dn - nrm * jnp.mean(dn * nrm, axis=-1, keepdims=True))
        dob = do.astype(MM)
        dy = _dot_nt(dob, wo_ref[...])
        dz_ref[...] = (dy * yc * (sig * (1.0 + zt * (1.0 - sig)))).astype(MM)
        dyc_ref[...] = dy * silu
        dwo_ref[...] += _dot_tn(yb, dob)

    t = nb * s
    tile = lambda w: pl.BlockSpec((tm, w), lambda b, j: (b * nt + j, 0))
    return pl.pallas_call(
        body, name="mix_fwd", grid=(nb, nt),
        out_shape=(jax.ShapeDtypeStruct((t, D_MODEL), F32),
                   jax.ShapeDtypeStruct((t, Z_W), MM),
                   jax.ShapeDtypeStruct((t, D_MODEL), F32),
                   jax.ShapeDtypeStruct((D_MODEL, D_MODEL), F32),
                   jax.ShapeDtypeStruct((1, D_MODEL), F32),
                   jax.ShapeDtypeStruct((8, CHUNK), F32)),
        in_specs=[tile(UV_W), tile(Z_W),
                  pl.BlockSpec((None, s, QKV_W), lambda b, j: (b, 0, 0)),
                  pl.BlockSpec((None, MEM_LEN, 2 * MEM_LEN), lambda b, j: (b, 0, 0)),
                  tile(D_MODEL), tile(D_MODEL),
                  _full((4, CHUNK, 2 * CHUNK)),
                  pl.BlockSpec(memory_space=pltpu.SMEM),
                  _full((1, A_WIDTH)), _full((1, A_WIDTH)),
                  _full((A_GROUPS, CHUNK, CHUNK)), _full((A_GROUPS, CHUNK, CHUNK)),
                  _full((1, D_MODEL)), _full((D_MODEL, D_MODEL))],
        out_specs=(tile(D_MODEL), tile(Z_W), tile(D_MODEL),
                   _full((D_MODEL, D_MODEL)), _full((1, D_MODEL)), _full((8, CHUNK))),
        scratch_shapes=[pltpu.VMEM((tm, D_MODEL), F32)],
        compiler_params=_params(dimension_semantics=("arbitrary", "arbitrary")),
    )(uv, z, qkv3, mkv3, x2, tgt2, bias, sinks, vg, vb, wt, bcol, g2, w_o)


def _mix_bwd(uv, dyc, qkv3, mkv3, bias, sinks, vg, vb, wt, wtt, bcol, buckets, tm):
    nb, s = qkv3.shape[0], qkv3.shape[1]
    nt = s // tm
    bpt = tm // CHUNK

    def body(uv_ref, dyc_ref, qkv_ref, mkv_ref, bias_ref, sink_ref, vg_ref, vb_ref, wt_ref, wtt_ref, bcol_ref,
             bk_ref, duv_ref, dqkv_ref, dmkv_ref, dwsp_ref, dbs_ref, dvg_ref, dvb_ref, dsink_ref, drel_ref,
             dkv_acc, dbias_acc, dsv_acc, dsink_acc):
        b, j = pl.program_id(0), pl.program_id(1)
        jt = nt - 1 - j

        @pl.when((b == 0) & (j == 0))
        def _():
            dwsp_ref[...] = jnp.zeros_like(dwsp_ref)
            dvg_ref[...] = jnp.zeros_like(dvg_ref)
            dvb_ref[...] = jnp.zeros_like(dvb_ref)
            dbias_acc[...] = jnp.zeros_like(dbias_acc)
            dsv_acc[...] = jnp.zeros_like(dsv_acc)
            dsink_acc[...] = jnp.zeros_like(dsink_acc)

        @pl.when(j == 0)
        def _():
            dmkv_ref[...] = jnp.zeros_like(dmkv_ref)
            dkv_acc[...] = jnp.zeros_like(dkv_acc)

        carry = dkv_acc[0:CHUNK, :]
        dkv_acc[...] = jnp.zeros_like(dkv_acc)
        dkv_acc[tm:tm + CHUNK, :] = carry

        lo = _half_masks(CHUNK)
        lob = _half_masks(2 * CHUNK)
        for blk in range(bpt):
            r0 = blk * CHUNK
            n = jt * bpt + blk
            for g in range(A_GROUPS):
                u, gu, gv, rstd, xhat, gam, vc, sv = _spatial_group(uv_ref, r0, g, vg_ref, vb_ref, wt_ref, bcol_ref, True)
                dya = dyc_ref[r0:r0 + CHUNK, g * CHUNK:(g + 1) * CHUNK]
                duv_ref[r0:r0 + CHUNK, g * CHUNK:(g + 1) * CHUNK] = (dya * sv * gu).astype(MM)
                dsv = dya * u
                dsvb = dsv.astype(MM)
                dsv_acc[g] += dsv
                dwsp_ref[g] += _dot_nt(dsvb, vc.astype(MM))
                dvc = _dot(wtt_ref[g], dsvb)
                dvg_ref[:, g * CHUNK:(g + 1) * CHUNK] += jnp.sum(dvc * xhat, axis=0, keepdims=True)
                dvb_ref[:, g * CHUNK:(g + 1) * CHUNK] += jnp.sum(dvc, axis=0, keepdims=True)
                dxh = dvc * gam
                dv = rstd * (dxh - jnp.mean(dxh, axis=-1, keepdims=True)
                             - xhat * jnp.mean(dxh * xhat, axis=-1, keepdims=True))
                duv_ref[r0:r0 + CHUNK, A_WIDTH + g * CHUNK:A_WIDTH + (g + 1) * CHUNK] = (dv * gv).astype(MM)
            pstart, cstart = _band_rows(n)
            kb = jnp.concatenate([qkv_ref[pl.ds(pstart, CHUNK), 256:384], qkv_ref[pl.ds(cstart, CHUNK), 256:384]], axis=0)
            vbnd = jnp.concatenate([qkv_ref[pl.ds(pstart, CHUNK), 384:512], qkv_ref[pl.ds(cstart, CHUNK), 384:512]], axis=0)
            kd = _dup_heads(kb)
            vd = _dup_heads(vbnd)
            first_add = _first_block_mask(n)
            dk_f, dv_f = [], []
            for kvh in range(2):
                q128 = qkv_ref[pl.ds(cstart, CHUNK), kvh * CHUNK:(kvh + 1) * CHUNK]
                do128 = dyc_ref[r0:r0 + CHUNK, A_WIDTH + kvh * CHUNK:A_WIDTH + (kvh + 1) * CHUNK]
                dq128 = jnp.zeros((CHUNK, CHUNK), F32)
                dkd = jnp.zeros((2 * CHUNK, CHUNK), F32)
                dvd = jnp.zeros((2 * CHUNK, CHUNK), F32)
                for gi in range(2):
                    h = 2 * kvh + gi
                    half = lo if gi == 0 else ~lo
                    qsel = jnp.where(half, q128.astype(F32), 0.0).astype(MM)
                    dosel = jnp.where(half, do128, 0.0).astype(MM)
                    probs, ps = _swa_probs(qsel, kd[kvh], bias_ref[h], sink_ref[h], first_add)
                    dp = _dot_nt(dosel, vd[kvh])
                    delta = jnp.sum(probs * dp, axis=-1, keepdims=True)
                    ds = probs * (dp - delta)
                    dbias_acc[h] += ds
                    dsink_acc[h:h + 1, :] += jnp.broadcast_to(
                        -jnp.sum(ps * delta, axis=0, keepdims=True), (1, CHUNK))
                    dss = (ds * SCALE).astype(MM)
                    dq128 = dq128 + jnp.where(half, _dot(dss, kd[kvh]), 0.0)
                    dkd = dkd + _dot_tn(dss, qsel)
                    dvd = dvd + _dot_tn(probs.astype(MM), dosel)
                dqkv_ref[r0:r0 + CHUNK, kvh * CHUNK:(kvh + 1) * CHUNK] = dq128.astype(MM)
                dk_f.append(dkd + pltpu.roll(dkd, 64, 1))
                dv_f.append(dvd + pltpu.roll(dvd, 64, 1))
            dkv_acc[r0:r0 + 2 * CHUNK, 0:CHUNK] += jnp.where(lob, dk_f[0], dk_f[1])
            dkv_acc[r0:r0 + 2 * CHUNK, CHUNK:2 * CHUNK] += jnp.where(lob, dv_f[0], dv_f[1])
        dqkv_ref[:, 256:512] = dkv_acc[CHUNK:CHUNK + tm, :].astype(MM)
        lot = _half_masks(tm)
        row0 = pl.multiple_of(jt * tm, tm)
        for g in range(2):
            q128 = qkv_ref[pl.ds(row0, tm), 512 + g * CHUNK:512 + (g + 1) * CHUNK]
            k128 = mkv_ref[:, g * CHUNK:(g + 1) * CHUNK]
            v128 = mkv_ref[:, MEM_LEN + g * CHUNK:MEM_LEN + (g + 1) * CHUNK]
            do128 = dyc_ref[:, 768 + g * CHUNK:768 + (g + 1) * CHUNK]
            dq128 = jnp.zeros((tm, CHUNK), F32)
            dk128 = jnp.zeros((MEM_LEN, CHUNK), F32)
            dv128 = jnp.zeros((MEM_LEN, CHUNK), F32)
            for hh in range(2):
                half = lot if hh == 0 else ~lot
                qsel = jnp.where(half, q128.astype(F32), 0.0).astype(MM)
                dosel = jnp.where(half, do128, 0.0).astype(MM)
                probs = _softmax(_dot_nt(qsel, k128) * SCALE)
                dp = _dot_nt(dosel, v128)
                ds = probs * (dp - jnp.sum(probs * dp, axis=-1, keepdims=True))
                dss = (ds * SCALE).astype(MM)
                dq128 = dq128 + jnp.where(half, _dot(dss, k128), 0.0)
                dk128 = dk128 + _dot_tn(dss, qsel)
                dv128 = dv128 + _dot_tn(probs.astype(MM), dosel)
            dqkv_ref[:, 512 + g * CHUNK:512 + (g + 1) * CHUNK] = dq128.astype(MM)
            dmkv_ref[:, g * CHUNK:(g + 1) * CHUNK] += dk128
            dmkv_ref[:, MEM_LEN + g * CHUNK:MEM_LEN + (g + 1) * CHUNK] += dv128

        @pl.when((b == nb - 1) & (j == nt - 1))
        def _():
            r = lax.broadcasted_iota(jnp.int32, (CHUNK, CHUNK), 0)
            c = lax.broadcasted_iota(jnp.int32, (CHUNK, CHUNK), 1)
            for g in range(A_GROUPS):
                dwsp_ref[g] = jnp.where(r >= c, dwsp_ref[g], 0.0)
                dbs_ref[g:g + 1, :] = jnp.sum(dsv_acc[g].T, axis=0, keepdims=True)
            rows = lax.broadcasted_iota(jnp.int32, (8, CHUNK), 0)
            cols = lax.broadcasted_iota(jnp.int32, (8, CHUNK), 1)
            sk = jnp.zeros((8, CHUNK), F32)
            for h in range(4):
                sk = sk + jnp.where((rows == 0) & (cols == h), jnp.broadcast_to(dsink_acc[h:h + 1, :], (8, CHUNK)), 0.0)
            dsink_ref[...] = sk
            bk = bk_ref[...]
            valid = _window_valid()
            rrow = lax.broadcasted_iota(jnp.int32, (N_BUCKETS, CHUNK), 0)
            rcol = lax.broadcasted_iota(jnp.int32, (N_BUCKETS, CHUNK), 1)
            acc = jnp.zeros((N_BUCKETS, CHUNK), F32)
            for bb in range(N_BUCKETS):
                hit = (bk == bb) & valid
                for h in range(4):
                    part = jnp.sum(jnp.where(hit, dbias_acc[h], 0.0), axis=-1, keepdims=True)
                    tot = jnp.sum(part, axis=0, keepdims=True)
                    acc = acc + jnp.where((rrow == bb) & (rcol == h), jnp.broadcast_to(tot, (N_BUCKETS, CHUNK)), 0.0)
            drel_ref[...] = acc

    t = nb * s
    tile = lambda w: pl.BlockSpec((tm, w), lambda b, j: (b * nt + nt - 1 - j, 0))
    return pl.pallas_call(
        body, name="mix_bwd", grid=(nb, nt),
        out_shape=(jax.ShapeDtypeStruct((t, UV_W), MM),
                   jax.ShapeDtypeStruct((t, QKV_W), MM),
                   jax.ShapeDtypeStruct((nb, MEM_LEN, 2 * MEM_LEN), F32),
                   jax.ShapeDtypeStruct((A_GROUPS, CHUNK, CHUNK), F32),
                   jax.ShapeDtypeStruct((A_GROUPS, CHUNK), F32),
                   jax.ShapeDtypeStruct((1, A_WIDTH), F32),
                   jax.ShapeDtypeStruct((1, A_WIDTH), F32),
                   jax.ShapeDtypeStruct((8, CHUNK), F32),
                   jax.ShapeDtypeStruct((N_BUCKETS, CHUNK), F32)),
        in_specs=[tile(UV_W), tile(D_MODEL),
                  pl.BlockSpec((None, s, QKV_W), lambda b, j: (b, 0, 0)),
                  pl.BlockSpec((None, MEM_LEN, 2 * MEM_LEN), lambda b, j: (b, 0, 0)),
                  _full((4, CHUNK, 2 * CHUNK)),
                  pl.BlockSpec(memory_space=pltpu.SMEM),
                  _full((1, A_WIDTH)), _full((1, A_WIDTH)),
                  _full((A_GROUPS, CHUNK, CHUNK)), _full((A_GROUPS, CHUNK, CHUNK)), _full((A_GROUPS, CHUNK, CHUNK)),
                  _full((CHUNK, 2 * CHUNK))],
        out_specs=(tile(UV_W), tile(QKV_W),
                   pl.BlockSpec((None, MEM_LEN, 2 * MEM_LEN), lambda b, j: (b, 0, 0)),
                   _full((A_GROUPS, CHUNK, CHUNK)), _full((A_GROUPS, CHUNK)),
                   _full((1, A_WIDTH)), _full((1, A_WIDTH)), _full((8, CHUNK)), _full((N_BUCKETS, CHUNK))),
        scratch_shapes=[pltpu.VMEM((tm + CHUNK, 2 * CHUNK), F32),
                        pltpu.VMEM((4, CHUNK, 2 * CHUNK), F32),
                        pltpu.VMEM((A_GROUPS, CHUNK, CHUNK), F32),
                        pltpu.VMEM((8, CHUNK), F32)],
        compiler_params=_params(dimension_semantics=("arbitrary", "arbitrary")),
    )(uv, dyc, qkv3, mkv3, bias, sinks, vg, vb, wt, wtt, bcol, buckets)


def _inproj_bwd(x2, dxo, duv, dqkv, dz, g1, w_in_t, tm):
    t = x2.shape[0]
    nt = t // tm

    def body(x_ref, dxo_ref, duv_ref, dqkv_ref, dz_ref, g_ref, w_ref, gx_ref, dw_hbm, dg_ref, acc, sem):
        i = pl.program_id(0)

        @pl.when(i == 0)
        def _():
            acc[...] = jnp.zeros_like(acc)
            dg_ref[...] = jnp.zeros_like(dg_ref)

        xf = x_ref[...]
        r = lax.rsqrt(jnp.mean(xf * xf, axis=-1, keepdims=True) + EPS)
        nx = xf * r
        gv = g_ref[...]
        h = (nx * gv).astype(MM)
        duv_t, dqkv_t, dz_t = duv_ref[...], dqkv_ref[...], dz_ref[...]
        acc[0:UV_W, :] += _dot_tn(duv_t, h)
        acc[UV_W:UV_W + QKV_W, :] += _dot_tn(dqkv_t, h)
        acc[UV_W + QKV_W:IN_WIDTH, :] += _dot_tn(dz_t, h)
        dh = (_dot(duv_t, w_ref[0:UV_W, :]) + _dot(dqkv_t, w_ref[UV_W:UV_W + QKV_W, :])
              + _dot(dz_t, w_ref[UV_W + QKV_W:IN_WIDTH, :]))
        dg_ref[...] += jnp.sum(dh * nx, axis=0, keepdims=True)
        dnx = dh * gv
        gx_ref[...] = dxo_ref[...] + r * (dnx - nx * jnp.mean(dnx * nx, axis=-1, keepdims=True))

        @pl.when(i == nt - 1)
        def _():
            cp = pltpu.make_async_copy(acc, dw_hbm, sem)
            cp.start()
            cp.wait()

    tile = lambda w: pl.BlockSpec((tm, w), lambda i: (i, 0))
    return pl.pallas_call(
        body, name="inproj_bwd", grid=(nt,),
        out_shape=(jax.ShapeDtypeStruct((t, D_MODEL), F32),
                   jax.ShapeDtypeStruct((IN_WIDTH, D_MODEL), F32),
                   jax.ShapeDtypeStruct((1, D_MODEL), F32)),
        in_specs=[tile(D_MODEL), tile(D_MODEL), tile(UV_W), tile(QKV_W), tile(Z_W),
                  _full((1, D_MODEL)),
                  pl.BlockSpec((IN_WIDTH, D_MODEL), lambda i: (0, 0), pipeline_mode=pl.Buffered(1))],
        out_specs=(tile(D_MODEL), pl.BlockSpec(memory_space=pl.ANY), _full((1, D_MODEL))),
        scratch_shapes=[pltpu.VMEM((IN_WIDTH, D_MODEL), F32), pltpu.SemaphoreType.DMA],
        compiler_params=_params(dimension_semantics=("arbitrary",)),
    )(x2, dxo, duv, dqkv, dz, g1, w_in_t)


def _adamw(w, g, m, v):
    m = ADAM_B1 * m + (1.0 - ADAM_B1) * g
    v = ADAM_B2 * v + (1.0 - ADAM_B2) * (g * g)
    m_hat = m / (1.0 - ADAM_B1 ** ADAM_STEP)
    v_hat = v / (1.0 - ADAM_B2 ** ADAM_STEP)
    delta = -ADAM_LR * (m_hat / (jnp.sqrt(v_hat) + ADAM_EPS) + ADAM_WD * w)
    return delta, m, v


_ROWS = 32


_S_LAYOUT = (((1, D_MODEL), 0), ((1, D_MODEL), 8), ((1, D_MODEL), 16),
             ((1, A_WIDTH), 24), ((1, A_WIDTH), 28), ((A_GROUPS, CHUNK), 32),
             ((1, 4), 36), ((N_BUCKETS, 4), 40),
             ((A_GROUPS * CHUNK, CHUNK), 72))
_LOSS_ROW = 37
_S_ROWS = 72 + A_GROUPS * CHUNK
_N_SMALL = len(_S_LAYOUT)


def _pack_rows(dst, refs):
    for (shp, r0), ref in zip(_S_LAYOUT, refs):
        if shp[0] == 1 and shp[1] >= CHUNK:
            for i in range(shp[1] // CHUNK):
                dst[r0 + i:r0 + i + 1, :] = ref[:, i * CHUNK:(i + 1) * CHUNK]
        elif ref.shape[-1] == CHUNK:
            dst[r0:r0 + shp[0], :] = ref[0:shp[0], :]
        else:
            dst[r0:r0 + shp[0], 0:shp[1]] = ref[...]


def _unpack_rows(src, refs):
    for (shp, r0), ref in zip(_S_LAYOUT, refs):
        if shp[0] == 1 and shp[1] >= CHUNK:
            for i in range(shp[1] // CHUNK):
                ref[:, i * CHUNK:(i + 1) * CHUNK] = src[r0 + i:r0 + i + 1, :]
        elif shp[1] == CHUNK:
            ref[...] = src[r0:r0 + shp[0], :]
        else:
            ref[...] = src[r0:r0 + shp[0], 0:shp[1]]


def _greduce(ga, gb, gc, small_g, loss_p, big_wmv, small_wmv):
    shapes = (ga.shape[1:], gb.shape[1:], gc.shape[1:])
    rs = _S_ROWS
    small_shapes = [tuple(a.shape) for a in small_wmv[0]]

    def body(*refs):
        it = iter(refs)
        take = lambda n: [next(it) for _ in range(n)]
        ga_ref, gb_ref, gc_ref = take(3)
        sg_refs = take(_N_SMALL)
        loss_ref, = take(1)
        wa, ma, va, wb, mb, vb_, wc, mc, vc = take(9)
        sw_refs, sm_refs, sv_refs = take(_N_SMALL), take(_N_SMALL), take(_N_SMALL)
        oga, oda, oma, ova, ogb, odb, omb, ovb, ogc, odc, omc, ovc = take(12)
        so_refs = [take(_N_SMALL) for _ in range(4)]
        loss_out, = take(1)
        own_a, own_b, own_c, ra_a, ra_b, ra_c, sb_a, sb_b, sb_c, rb_a, rb_b, rb_c = take(12)
        gs_ref, rs_a, rs_b, ws, ms, vs, ogs, ods, oms, ovs = take(10)
        ld_sem, sa_sem, ra_sem, sb_sem, rb_sem = take(5)

        for buf in (gs_ref, ws, ms, vs):
            buf[...] = jnp.zeros_like(buf)
        _pack_rows(gs_ref, sg_refs)
        gs_ref[_LOSS_ROW:_LOSS_ROW + 1, :] = loss_ref[0:1, :]
        _pack_rows(ws, sw_refs)
        _pack_rows(ms, sm_refs)
        _pack_rows(vs, sv_refs)

        x, y, cc = lax.axis_index("x"), lax.axis_index("y"), lax.axis_index("c")
        myq = 2 * x + y
        me = (x, y, cc)
        sib = (x, y, 1 - cc)
        chips = [(1 - x, y), (x, 1 - y), (1 - x, 1 - y)]
        gin = (ga_ref, gb_ref, gc_ref)
        own = (own_a, own_b, own_c)
        rcv_a = (ra_a, ra_b, ra_c)
        sbuf = (sb_a, sb_b, sb_c)
        rcv_b = (rb_a, rb_b, rb_c)

        def remote(src, dst, ssem, rsem, to):
            return pltpu.make_async_remote_copy(src_ref=src, dst_ref=dst, send_sem=ssem, recv_sem=rsem,
                                                device_id=to, device_id_type=MESH)

        loads, sends_a = [], []
        for arr in range(3):
            for q in range(4):
                loads.append(pltpu.make_async_copy(gin[arr].at[2 * q + cc], own[arr].at[q], ld_sem.at[arr, q]))
                sends_a.append(remote(gin[arr].at[2 * q + 1 - cc], rcv_a[arr].at[q],
                                      sa_sem.at[arr, q], ra_sem.at[arr, q], sib))
        small_a = remote(gs_ref, rs_a, sa_sem.at[3, 0], ra_sem.at[3, 0], sib)
        for cp in loads + sends_a + [small_a]:
            cp.start()
        for cp in loads:
            cp.wait()
        for arr in range(3):
            for q in range(4):
                remote(gin[arr].at[2 * q + 1 - cc], rcv_a[arr].at[q],
                       sa_sem.at[arr, q], ra_sem.at[arr, q], me).wait_recv()
        remote(gs_ref, rs_a, sa_sem.at[3, 0], ra_sem.at[3, 0], me).wait_recv()

        for arr in range(3):
            nrow = shapes[arr][0]

            def add_rows(i, _, arr=arr):
                r = pl.ds(pl.multiple_of(i * _ROWS, _ROWS), _ROWS)
                for q in range(4):
                    rcv_a[arr][q, r, :] = rcv_a[arr][q, r, :] + own[arr][q, r, :]
                return 0

            lax.fori_loop(0, nrow // _ROWS, add_rows, 0)
        rs_b[myq] = gs_ref[...] + rs_a[...]

        sends_b = []
        for j, chip in enumerate(chips):
            qj = 2 * chip[0] + chip[1]
            to = (chip[0], chip[1], cc)
            for arr in range(3):
                nrow = shapes[arr][0]

                def cast_rows(i, _, arr=arr, j=j, qj=qj):
                    r = pl.ds(pl.multiple_of(i * _ROWS, _ROWS), _ROWS)
                    sbuf[arr][j, r, :] = rcv_a[arr][qj, r, :].astype(BF16)
                    return 0

                lax.fori_loop(0, nrow // _ROWS, cast_rows, 0)
                cp = remote(sbuf[arr].at[j], rcv_b[arr].at[j], sb_sem.at[arr, j], rb_sem.at[arr, j], to)
                cp.start()
                sends_b.append(cp)
            cp = remote(rs_b.at[myq], rs_b.at[myq], sb_sem.at[3, j], rb_sem.at[3, j], to)
            cp.start()
            sends_b.append(cp)
        for j in range(3):
            for arr in range(3):
                remote(sbuf[arr].at[j], rcv_b[arr].at[j], sb_sem.at[arr, j], rb_sem.at[arr, j], me).wait_recv()
            remote(rs_b.at[myq], rs_b.at[myq], sb_sem.at[3, j], rb_sem.at[3, j], me).wait_recv()

        big = ((wa, ma, va, oga, oda, oma, ova), (wb, mb, vb_, ogb, odb, omb, ovb), (wc, mc, vc, ogc, odc, omc, ovc))
        for arr in range(3):
            w_r, m_r, v_r, og, od, om, ov = big[arr]
            nrow = shapes[arr][0]

            def upd(i, _, arr=arr, w_r=w_r, m_r=m_r, v_r=v_r, og=og, od=od, om=om, ov=ov):
                r = pl.ds(pl.multiple_of(i * _ROWS, _ROWS), _ROWS)
                g = rcv_a[arr][myq, r, :]
                for j in range(3):
                    g = g + rcv_b[arr][j, r, :].astype(F32)
                d, m, v = _adamw(w_r[r, :], g, m_r[r, :], v_r[r, :])
                og[r, :] = g
                od[r, :] = d
                om[r, :] = m
                ov[r, :] = v
                return 0

            lax.fori_loop(0, nrow // _ROWS, upd, 0)

        def upd_s(i, _):
            r = pl.ds(pl.multiple_of(i * 8, 8), 8)
            g = ((rs_b[0, r, :] + rs_b[1, r, :]) + rs_b[2, r, :]) + rs_b[3, r, :]
            d, m, v = _adamw(ws[r, :], g, ms[r, :], vs[r, :])
            ogs[r, :] = g
            ods[r, :] = d
            oms[r, :] = m
            ovs[r, :] = v
            return 0

        lax.fori_loop(0, rs // 8, upd_s, 0)
        for k, buf in enumerate((ogs, ods, oms, ovs)):
            _unpack_rows(buf, so_refs[k])
        loss_out[...] = jnp.broadcast_to(ogs[_LOSS_ROW:_LOSS_ROW + 1, :], loss_out.shape)

        for cp in sends_a + [small_a] + sends_b:
            cp.wait_send()

    vm = pl.BlockSpec(memory_space=pltpu.VMEM)
    anyspec = pl.BlockSpec(memory_space=pl.ANY)
    big_out = []
    for shp in shapes:
        big_out += [jax.ShapeDtypeStruct(shp, F32)] * 4
    small_out = [jax.ShapeDtypeStruct(shp, F32) for shp in small_shapes] * 4
    out_shape = tuple(big_out + small_out + [jax.ShapeDtypeStruct((8, CHUNK), F32)])
    scratch = ([pltpu.VMEM((4,) + shp, F32) for shp in shapes]
               + [pltpu.VMEM((4,) + shp, F32) for shp in shapes]
               + [pltpu.VMEM((3,) + shp, BF16) for shp in shapes]
               + [pltpu.VMEM((3,) + shp, BF16) for shp in shapes]
               + [pltpu.VMEM((rs, CHUNK), F32), pltpu.VMEM((rs, CHUNK), F32), pltpu.VMEM((4, rs, CHUNK), F32)]
               + [pltpu.VMEM((rs, CHUNK), F32) for _ in range(7)]
               + [pltpu.SemaphoreType.DMA((3, 4)), pltpu.SemaphoreType.DMA((4, 4)), pltpu.SemaphoreType.DMA((4, 4)),
                  pltpu.SemaphoreType.DMA((4, 3)), pltpu.SemaphoreType.DMA((4, 3))])
    n_in = 3 + _N_SMALL + 1 + 9 + 3 * _N_SMALL
    return pl.pallas_call(
        body, name="greduce",
        out_shape=out_shape,
        in_specs=[anyspec, anyspec, anyspec] + [vm] * (n_in - 3),
        out_specs=tuple([vm] * len(out_shape)),
        scratch_shapes=scratch,
        compiler_params=_params(),
    )(ga, gb, gc, *small_g, loss_p, *big_wmv, *small_wmv[0], *small_wmv[1], *small_wmv[2])


def _local_step(x, mem, loss_target, pre_norm_g, post_norm_g, mem_norm_g, v_norm_g, v_norm_b, w_spatial, b_spatial,
                attn_sinks, rel_bias, w_in_t, w_o, w_mkv):
    nb, s, _ = x.shape
    t = nb * s
    x2 = x.reshape(t, D_MODEL)
    tgt2 = loss_target.reshape(t, D_MODEL)
    mem2 = mem.reshape(nb * MEM_LEN, D_MODEL)
    tm_mix = min(256, s)
    tm_proj = min(512, t)

    buckets = jnp.asarray(_t5_buckets())
    sinks = attn_sinks.reshape(4)
    bias, wt, wtt, bcol = _prep(rel_bias, w_spatial[0], b_spatial[0], buckets)

    uv, qkv, z = _inproj_fwd(x2, pre_norm_g, w_in_t, tm_proj)
    mkv = _memkv_fwd(mem2, mem_norm_g, w_mkv)
    qkv3 = qkv.reshape(nb, s, QKV_W)
    mkv3 = mkv.reshape(nb, MEM_LEN, 2 * MEM_LEN)
    dyc, dz, dxo, dwo, dg2, loss_p = _mix_fwd(uv, z, qkv3, mkv3, x2, tgt2, bias, sinks, v_norm_g, v_norm_b,
                                              wt, bcol, post_norm_g, w_o, tm_mix)
    duv, dqkv, dmkv, dwsp, dbs, dvg, dvb, dsink, drel = _mix_bwd(uv, dyc, qkv3, mkv3, bias, sinks, v_norm_g, v_norm_b,
                                                                 wt, wtt, bcol, buckets, tm_mix)
    dwmkv, dgm = _memkv_bwd(dmkv.reshape(nb * MEM_LEN, 2 * MEM_LEN), mem2, mem_norm_g, w_mkv)
    gx, dw_in_t, dg1 = _inproj_bwd(x2, dxo, duv, dqkv, dz, pre_norm_g, w_in_t, tm_proj)
    small = [dg1, dg2, dgm, dvg, dvb, dbs, dsink, drel, dwsp.reshape(A_GROUPS * CHUNK, CHUNK)]
    return loss_p, gx.reshape(nb, s, D_MODEL), dw_in_t, dwo, dwmkv, small


def kernel(x, mem, pre_norm_g, post_norm_g, mem_norm_g, w_in, w_mem_kv, v_norm_g, v_norm_b, w_spatial, b_spatial, attn_sinks, rel_bias, w_out, loss_target, m_pre_norm_g, m_post_norm_g, m_mem_norm_g, m_w_in, m_w_mem_kv, m_v_norm_g, m_v_norm_b, m_w_spatial, m_b_spatial, m_attn_sinks, m_rel_bias, m_w_out, v_pre_norm_g, v_post_norm_g, v_mem_norm_g, v_w_in, v_w_mem_kv, v_v_norm_g, v_v_norm_b, v_w_spatial, v_b_spatial, v_attn_sinks, v_rel_bias, v_w_out):
    nb, s, _ = x.shape

    sh_a = (w_in[0].T, m_w_in[0].T, v_w_in[0].T)
    sh_b = (w_out[0], m_w_out[0], v_w_out[0])
    sh_c = (w_mem_kv[0], m_w_mem_kv[0], v_w_mem_kv[0])
    wa, wb, wc = _wgather(sh_a[0], sh_b[0], sh_c[0])
    w_in_t = wa.reshape(IN_WIDTH, D_MODEL)
    w_o = wb.reshape(D_MODEL, D_MODEL)
    w_mkv = wc.reshape(D_MODEL, 2 * MEM_LEN)

    loss_p, gx, dw_in_t, dwo, dwmkv, small_grads = _local_step(
        x, mem, loss_target, pre_norm_g, post_norm_g, mem_norm_g, v_norm_g, v_norm_b, w_spatial, b_spatial,
        attn_sinks, rel_bias, w_in_t, w_o, w_mkv)

    small_names = ["pre_norm_g", "post_norm_g", "mem_norm_g", "v_norm_g", "v_norm_b", "b_spatial", "attn_sinks",
                   "rel_bias", "w_spatial"]
    given = dict(pre_norm_g=(pre_norm_g, m_pre_norm_g, v_pre_norm_g), post_norm_g=(post_norm_g, m_post_norm_g, v_post_norm_g),
                 mem_norm_g=(mem_norm_g, m_mem_norm_g, v_mem_norm_g), v_norm_g=(v_norm_g, m_v_norm_g, v_v_norm_g),
                 v_norm_b=(v_norm_b, m_v_norm_b, v_v_norm_b), b_spatial=(b_spatial, m_b_spatial, v_b_spatial),
                 attn_sinks=(attn_sinks, m_attn_sinks, v_attn_sinks), rel_bias=(rel_bias, m_rel_bias, v_rel_bias),
                 w_spatial=(w_spatial, m_w_spatial, v_w_spatial))
    small_wmv = [[given[n][k].reshape(shp) for n, (shp, _) in zip(small_names, _S_LAYOUT)] for k in range(3)]

    outs = _greduce(dw_in_t.reshape(N_DEV, SHARD_IN, D_MODEL), dwo.reshape(N_DEV, SHARD_O, D_MODEL),
                    dwmkv.reshape(N_DEV, SHARD_O, 2 * MEM_LEN), small_grads, loss_p,
                    (*sh_a, *sh_b, *sh_c), small_wmv)
    ra, rb, rc = outs[0:4], outs[4:8], outs[8:12]
    loss = outs[12 + 4 * _N_SMALL][0, 0]

    res = {}
    for k, kind in enumerate(("grad", "delta", "new_m", "new_v")):
        res[kind, "w_in"] = ra[k].T[None]
        res[kind, "w_out"] = rb[k][None]
        res[kind, "w_mem_kv"] = rc[k][None]
        for i, n in enumerate(small_names):
            res[kind, n] = outs[12 + k * _N_SMALL + i].reshape(given[n][0].shape)
    order = ["pre_norm_g", "post_norm_g", "mem_norm_g", "w_in", "w_mem_kv", "v_norm_g", "v_norm_b", "w_spatial",
             "b_spatial", "attn_sinks", "rel_bias", "w_out"]
    flat = [res[kind, n] for kind in ("grad", "delta", "new_m", "new_v") for n in order]
    return (loss, gx.reshape(nb, s, D_MODEL), *flat)
```

```python
import functools

import numpy as np
import jax
import jax.numpy as jnp
from jax import lax
from jax.experimental import pallas as pl
from jax.experimental.pallas import tpu as pltpu

F32 = jnp.float32
BF16 = jnp.bfloat16
MM = jnp.bfloat16

D_MODEL = 1024
CHUNK = 128
A_GROUPS = 4
A_WIDTH = 512
UV_W = 1024
QKV_W = 768
Z_W = 1024
IN_WIDTH = UV_W + QKV_W + Z_W
MEM_LEN = 256
N_BUCKETS = 32
MAX_DISTANCE = 128
EPS = 1e-6
NEG = -1e30
SCALE = 0.125
N_DEV = 8
SHARD_IN = IN_WIDTH // N_DEV
SHARD_O = D_MODEL // N_DEV

ADAM_LR = 0.001
ADAM_B1 = 0.9
ADAM_B2 = 0.999
ADAM_EPS = 1e-08
ADAM_WD = 0.01
ADAM_STEP = 10

VMEM_LIMIT = 58 * 1024 * 1024

_GELU_C = 0.7978845608028654
_GELU_A = 0.044715

MESH = pl.DeviceIdType.MESH


def _dot(a, b):
    return lax.dot_general(a, b, (((1,), (0,)), ((), ())), preferred_element_type=F32)


def _dot_nt(a, b):
    return lax.dot_general(a, b, (((1,), (1,)), ((), ())), preferred_element_type=F32)


def _dot_tn(a, b):
    return lax.dot_general(a, b, (((0,), (0,)), ((), ())), preferred_element_type=F32)


def _gelu(x):
    x2 = x * x
    t = jnp.tanh(_GELU_C * (x + _GELU_A * x * x2))
    return 0.5 * x * (1.0 + t)


def _gelu_and_grad(x):
    x2 = x * x
    t = jnp.tanh(_GELU_C * (x + _GELU_A * x * x2))
    g = 0.5 * x * (1.0 + t)
    dg = 0.5 * (1.0 + t) + 0.5 * x * (1.0 - t * t) * (_GELU_C * (1.0 + 3.0 * _GELU_A * x2))
    return g, dg


def _t5_buckets():
    qi = np.arange(CHUNK)[:, None]
    kj = np.arange(2 * CHUNK)[None, :]
    n = np.maximum(qi + CHUNK - kj, 0)
    max_exact = N_BUCKETS // 2
    large = max_exact + (np.log(np.maximum(n, 1) / max_exact) / np.log(MAX_DISTANCE / max_exact)
                         * (N_BUCKETS - max_exact)).astype(np.int32)
    large = np.minimum(large, N_BUCKETS - 1)
    return np.where(n < max_exact, n, large).astype(np.int32)


def _params(**kw):
    return pltpu.CompilerParams(vmem_limit_bytes=VMEM_LIMIT, **kw)


def _full(shape):
    nd = len(shape)
    return pl.BlockSpec(shape, lambda *_: (0,) * nd)


def _window_valid():
    qi = lax.broadcasted_iota(jnp.int32, (CHUNK, 2 * CHUNK), 0)
    kj = lax.broadcasted_iota(jnp.int32, (CHUNK, 2 * CHUNK), 1)
    dist = qi + CHUNK - kj
    return (dist >= 0) & (dist < CHUNK)


SHARD_PAD = 384


def _wgather(a, b, c):
    def body(a_ref, b_ref, c_ref, oa, ob, oc, pad, ssem, rsem):
        x, y, cc = lax.axis_index("x"), lax.axis_index("y"), lax.axis_index("c")
        me = 4 * x + 2 * y + cc
        sib = (x, y, 1 - cc)
        chips = [(1 - x, y), (x, 1 - y), (1 - x, 1 - y)]
        outs = (oa, ob, oc)
        ob[me] = b_ref[...].astype(BF16)
        oc[me] = c_ref[...].astype(BF16)
        pad[...] = jnp.zeros_like(pad)
        pad[:, 0:SHARD_IN] = a_ref[...]
        for rb in range(0, SHARD_PAD, CHUNK):
            h = min(CHUNK, SHARD_IN - rb)
            for cb in range(0, D_MODEL, CHUNK):
                oa[me, rb:rb + h, cb:cb + CHUNK] = pad[cb:cb + CHUNK, rb:rb + CHUNK].T[0:h, :].astype(BF16)

        def copy(arr, k, blk, to):
            r = outs[arr].at[blk]
            return pltpu.make_async_remote_copy(src_ref=r, dst_ref=r, send_sem=ssem.at[arr, k],
                                                recv_sem=rsem.at[arr, k], device_id=to, device_id_type=MESH)

        def idx(chip, core):
            return 4 * chip[0] + 2 * chip[1] + core

        first = []
        for arr in range(3):
            first.append(copy(arr, 0, me, sib))
            for j, chip in enumerate(chips):
                first.append(copy(arr, 1 + j, me, (chip[0], chip[1], cc)))
        for cp in first:
            cp.start()
        passed = []
        for j, chip in enumerate(chips):
            for arr in range(3):
                copy(arr, 1 + j, idx(chip, cc), (x, y, cc)).wait_recv()
                cp = copy(arr, 4 + j, idx(chip, cc), sib)
                cp.start()
                passed.append(cp)
        for arr in range(3):
            copy(arr, 0, idx((x, y), 1 - cc), (x, y, cc)).wait_recv()
            for j, chip in enumerate(chips):
                copy(arr, 4 + j, idx(chip, 1 - cc), (x, y, cc)).wait_recv()
        for cp in first + passed:
            cp.wait_send()

    vm = pl.BlockSpec(memory_space=pltpu.VMEM)
    return pl.pallas_call(
        body, name="wgather",
        out_shape=(jax.ShapeDtypeStruct((N_DEV, SHARD_IN, D_MODEL), BF16),
                   jax.ShapeDtypeStruct((N_DEV,) + b.shape, BF16),
                   jax.ShapeDtypeStruct((N_DEV,) + c.shape, BF16)),
        in_specs=[vm, vm, vm], out_specs=(vm, vm, vm),
        scratch_shapes=[pltpu.VMEM((D_MODEL, SHARD_PAD), F32),
                        pltpu.SemaphoreType.DMA((3, 7)), pltpu.SemaphoreType.DMA((3, 7))],
        compiler_params=_params(),
    )(a, b, c)


def _prep(rel_bias, w_sp, b_sp, buckets):
    def body(rb_ref, w_ref, b_ref, bk_ref, bias_ref, wt_ref, wtt_ref, bcol_ref):
        valid = _window_valid()
        bk = bk_ref[...]
        acc = [jnp.full((CHUNK, 2 * CHUNK), NEG, F32) for _ in range(4)]
        for b in range(N_BUCKETS):
            hit = (bk == b) & valid
            for h in range(4):
                acc[h] = jnp.where(hit, rb_ref[b, h], acc[h])
        for h in range(4):
            bias_ref[h] = acc[h]
        r = lax.broadcasted_iota(jnp.int32, (CHUNK, CHUNK), 0)
        c = lax.broadcasted_iota(jnp.int32, (CHUNK, CHUNK), 1)
        for g in range(A_GROUPS):
            w = jnp.where(r >= c, w_ref[g], 0.0)
            wt_ref[g] = w.astype(MM)
            wtt_ref[g] = w.T.astype(MM)
            bcol_ref[g] = jnp.broadcast_to(b_ref[g:g + 1, :], (CHUNK, CHUNK)).T

    return pl.pallas_call(
        body, name="prep",
        out_shape=(jax.ShapeDtypeStruct((4, CHUNK, 2 * CHUNK), F32),
                   jax.ShapeDtypeStruct((A_GROUPS, CHUNK, CHUNK), MM),
                   jax.ShapeDtypeStruct((A_GROUPS, CHUNK, CHUNK), MM),
                   jax.ShapeDtypeStruct((A_GROUPS, CHUNK, CHUNK), F32)),
        in_specs=[pl.BlockSpec(memory_space=pltpu.SMEM), pl.BlockSpec(memory_space=pltpu.VMEM),
                  pl.BlockSpec(memory_space=pltpu.VMEM), pl.BlockSpec(memory_space=pltpu.VMEM)],
        out_specs=tuple(pl.BlockSpec(memory_space=pltpu.VMEM) for _ in range(4)),
    )(rel_bias, w_sp, b_sp, buckets)


def _inproj_fwd(x2, g1, w_in_t, tm):
    t = x2.shape[0]

    def body(x_ref, g_ref, w_ref, uv_ref, qkv_ref, z_ref):
        xf = x_ref[...]
        r = lax.rsqrt(jnp.mean(xf * xf, axis=-1, keepdims=True) + EPS)
        h = (xf * r * g_ref[...]).astype(MM)
        uv_ref[...] = _dot_nt(h, w_ref[0:UV_W, :])
        qkv_ref[...] = _dot_nt(h, w_ref[UV_W:UV_W + QKV_W, :]).astype(MM)
        z_ref[...] = _dot_nt(h, w_ref[UV_W + QKV_W:IN_WIDTH, :])

    return pl.pallas_call(
        body, name="inproj_fwd", grid=(t // tm,),
        out_shape=(jax.ShapeDtypeStruct((t, UV_W), F32),
                   jax.ShapeDtypeStruct((t, QKV_W), MM),
                   jax.ShapeDtypeStruct((t, Z_W), F32)),
        in_specs=[pl.BlockSpec((tm, D_MODEL), lambda i: (i, 0)),
                  _full((1, D_MODEL)),
                  pl.BlockSpec((IN_WIDTH, D_MODEL), lambda i: (0, 0), pipeline_mode=pl.Buffered(1))],
        out_specs=(pl.BlockSpec((tm, UV_W), lambda i: (i, 0)),
                   pl.BlockSpec((tm, QKV_W), lambda i: (i, 0)),
                   pl.BlockSpec((tm, Z_W), lambda i: (i, 0))),
        compiler_params=_params(dimension_semantics=("arbitrary",)),
    )(x2, g1, w_in_t)


def _memkv_fwd(mem2, gm, w_mkv):
    tmem = mem2.shape[0]

    def body(m_ref, g_ref, w_ref, o_ref):
        xf = m_ref[...]
        r = lax.rsqrt(jnp.mean(xf * xf, axis=-1, keepdims=True) + EPS)
        hm = (xf * r * g_ref[...]).astype(MM)
        o_ref[...] = _dot(hm, w_ref[...]).astype(MM)

    vm = pl.BlockSpec(memory_space=pltpu.VMEM)
    return pl.pallas_call(
        body, name="memkv_fwd",
        out_shape=jax.ShapeDtypeStruct((tmem, 2 * MEM_LEN), MM),
        in_specs=[vm, vm, vm], out_specs=vm,
        compiler_params=_params(),
    )(mem2, gm, w_mkv)


def _memkv_bwd(dmkv, mem2, gm, w_mkv):
    def body(d_ref, m_ref, g_ref, w_ref, dw_ref, dg_ref):
        xf = m_ref[...]
        r = lax.rsqrt(jnp.mean(xf * xf, axis=-1, keepdims=True) + EPS)
        nm = xf * r
        hm = (nm * g_ref[...]).astype(MM)
        d = d_ref[...].astype(MM)
        dw_ref[...] = _dot_tn(hm, d)
        dhm = _dot_nt(d, w_ref[...])
        dg_ref[...] = jnp.sum(dhm * nm, axis=0, keepdims=True)

    vm = pl.BlockSpec(memory_space=pltpu.VMEM)
    return pl.pallas_call(
        body, name="memkv_bwd",
        out_shape=(jax.ShapeDtypeStruct((D_MODEL, 2 * MEM_LEN), F32),
                   jax.ShapeDtypeStruct((1, D_MODEL), F32)),
        in_specs=[vm, vm, vm, vm], out_specs=(vm, vm),
        compiler_params=_params(),
    )(dmkv, mem2, gm, w_mkv)


def _half_masks(rows):
    lane = lax.broadcasted_iota(jnp.int32, (rows, CHUNK), 1)
    return lane < 64


def _dup_heads(band):
    b32 = band.astype(F32)
    rolled = pltpu.roll(b32, 64, 1)
    lo = _half_masks(band.shape[0])
    return (jnp.where(lo, b32, rolled).astype(MM), jnp.where(lo, rolled, b32).astype(MM))


def _swa_probs(qsel, kd, bias_h, sink_h, first_add):
    s = _dot_nt(qsel, kd) * SCALE + bias_h + first_add
    m = jnp.maximum(jnp.max(s, axis=-1, keepdims=True), sink_h)
    p = jnp.exp(s - m)
    es = jnp.exp(sink_h - m)
    inv = 1.0 / (jnp.sum(p, axis=-1, keepdims=True) + es)
    return p * inv, es * inv


def _softmax(s):
    m = jnp.max(s, axis=-1, keepdims=True)
    p = jnp.exp(s - m)
    return p * (1.0 / jnp.sum(p, axis=-1, keepdims=True))


def _band_rows(n):
    cstart = pl.multiple_of(n * CHUNK, CHUNK)
    pstart = pl.multiple_of(jnp.maximum(n - 1, 0) * CHUNK, CHUNK)
    return pstart, cstart


def _first_block_mask(n):
    col = lax.broadcasted_iota(jnp.int32, (CHUNK, 2 * CHUNK), 1)
    return jnp.where((col < CHUNK) & (n == 0), NEG, 0.0)


def _spatial_group(uv_ref, r0, g, vg_ref, vb_ref, wt_ref, bcol_ref, with_grad):
    au = uv_ref[r0:r0 + CHUNK, g * CHUNK:(g + 1) * CHUNK]
    av = uv_ref[r0:r0 + CHUNK, A_WIDTH + g * CHUNK:A_WIDTH + (g + 1) * CHUNK]
    if with_grad:
        u, du = _gelu_and_grad(au)
        v, dv = _gelu_and_grad(av)
    else:
        u, v = _gelu(au), _gelu(av)
        du = dv = None
    mu = jnp.mean(v, axis=-1, keepdims=True)
    xc = v - mu
    rstd = lax.rsqrt(jnp.mean(xc * xc, axis=-1, keepdims=True) + EPS)
    xhat = xc * rstd
    gam = vg_ref[:, g * CHUNK:(g + 1) * CHUNK]
    vc = xhat * gam + vb_ref[:, g * CHUNK:(g + 1) * CHUNK]
    sv = _dot(wt_ref[g], vc.astype(MM)) + bcol_ref[g]
    return u, du, dv, rstd, xhat, gam, vc, sv


def _mix_fwd(uv, z, qkv3, mkv3, x2, tgt2, bias, sinks, vg, vb, wt, bcol, g2, w_o, tm):
    nb, s = qkv3.shape[0], qkv3.shape[1]
    nt = s // tm
    bpt = tm // CHUNK

    def body(uv_ref, z_ref, qkv_ref, mkv_ref, x_ref, t_ref, bias_ref, sink_ref, vg_ref, vb_ref, wt_ref,
             bcol_ref, g2_ref, wo_ref, dyc_ref, dz_ref, dxo_ref, dwo_ref, dg2_ref, loss_ref, ycat):
        b, j = pl.program_id(0), pl.program_id(1)

        @pl.when((b == 0) & (j == 0))
        def _():
            dwo_ref[...] = jnp.zeros_like(dwo_ref)
            dg2_ref[...] = jnp.zeros_like(dg2_ref)
            loss_ref[...] = jnp.zeros_like(loss_ref)

        lo = _half_masks(CHUNK)
        for blk in range(bpt):
            r0 = blk * CHUNK
            n = j * bpt + blk
            for g in range(A_GROUPS):
                u, _, _, _, _, _, _, sv = _spatial_group(uv_ref, r0, g, vg_ref, vb_ref, wt_ref, bcol_ref, False)
                ycat[r0:r0 + CHUNK, g * CHUNK:(g + 1) * CHUNK] = u * sv
            pstart, cstart = _band_rows(n)
            kb = jnp.concatenate([qkv_ref[pl.ds(pstart, CHUNK), 256:384], qkv_ref[pl.ds(cstart, CHUNK), 256:384]], axis=0)
            vbnd = jnp.concatenate([qkv_ref[pl.ds(pstart, CHUNK), 384:512], qkv_ref[pl.ds(cstart, CHUNK), 384:512]], axis=0)
            kd = _dup_heads(kb)
            vd = _dup_heads(vbnd)
            first_add = _first_block_mask(n)
            for kvh in range(2):
                q128 = qkv_ref[pl.ds(cstart, CHUNK), kvh * CHUNK:(kvh + 1) * CHUNK]
                outs = []
                for gi in range(2):
                    h = 2 * kvh + gi
                    qsel = jnp.where(lo if gi == 0 else ~lo, q128.astype(F32), 0.0).astype(MM)
                    probs, _ = _swa_probs(qsel, kd[kvh], bias_ref[h], sink_ref[h], first_add)
                    outs.append(_dot(probs.astype(MM), vd[kvh]))
                ycat[r0:r0 + CHUNK, A_WIDTH + kvh * CHUNK:A_WIDTH + (kvh + 1) * CHUNK] = jnp.where(lo, outs[0], outs[1])
        lot = _half_masks(tm)
        row0 = pl.multiple_of(j * tm, tm)
        for g in range(2):
            q128 = qkv_ref[pl.ds(row0, tm), 512 + g * CHUNK:512 + (g + 1) * CHUNK]
            k128 = mkv_ref[:, g * CHUNK:(g + 1) * CHUNK]
            v128 = mkv_ref[:, MEM_LEN + g * CHUNK:MEM_LEN + (g + 1) * CHUNK]
            outs = []
            for hh in range(2):
                qsel = jnp.where(lot if hh == 0 else ~lot, q128.astype(F32), 0.0).astype(MM)
                probs = _softmax(_dot_nt(qsel, k128) * SCALE)
                outs.append(_dot(probs.astype(MM), v128))
            ycat[:, 768 + g * CHUNK:768 + (g + 1) * CHUNK] = jnp.where(lot, outs[0], outs[1])
        zt = z_ref[...]
        sig = 1.0 / (1.0 + jnp.exp(-zt))
        silu = zt * sig
        yc = ycat[...]
        yb = (yc * silu).astype(MM)
        o = _dot(yb, wo_ref[...])
        r2 = lax.rsqrt(jnp.mean(o * o, axis=-1, keepdims=True) + EPS)
        nrm = o * r2
        g2v = g2_ref[...]
        e = x_ref[...] + nrm * g2v - t_ref[...]
        l1 = jnp.sum(e * e, axis=-1, keepdims=True)
        loss_ref[...] += jnp.broadcast_to(jnp.sum(l1, axis=0, keepdims=True) * (0.5 / D_MODEL), loss_ref.shape)
        dxo = e * (1.0 / D_MODEL)
        dxo_ref[...] = dxo
        dg2_ref[...] += jnp.sum(dxo * nrm, axis=0, keepdims=True)
        dn = dxo * g2v
        do = r2 * (dn - nrm * jnp.mean(dn * nrm, axis=-1, keepdims=True))
        dob = do.astype(MM)
        dy = _dot_nt(dob, wo_ref[...])
        dz_ref[...] = (dy * yc * (sig * (1.0 + zt * (1.0 - sig)))).astype(MM)
        dyc_ref[...] = dy * silu
        dwo_ref[...] += _dot_tn(yb, dob)

    t = nb * s
    tile = lambda w: pl.BlockSpec((tm, w), lambda b, j: (b * nt + j, 0))
    return pl.pallas_call(
        body, name="mix_fwd", grid=(nb, nt),
        out_shape=(jax.ShapeDtypeStruct((t, D_MODEL), F32),
                   jax.ShapeDtypeStruct((t, Z_W), MM),
                   jax.ShapeDtypeStruct((t, D_MODEL), F32),
                   jax.ShapeDtypeStruct((D_MODEL, D_MODEL), F32),
                   jax.ShapeDtypeStruct((1, D_MODEL), F32),
                   jax.ShapeDtypeStruct((8, CHUNK), F32)),
        in_specs=[tile(UV_W), tile(Z_W),
                  pl.BlockSpec((None, s, QKV_W), lambda b, j: (b, 0, 0)),
                  pl.BlockSpec((None, MEM_LEN, 2 * MEM_LEN), lambda b, j: (b, 0, 0)),
                  tile(D_MODEL), tile(D_MODEL),
                  _full((4, CHUNK, 2 * CHUNK)),
                  pl.BlockSpec(memory_space=pltpu.SMEM),
                  _full((1, A_WIDTH)), _full((1, A_WIDTH)),
                  _full((A_GROUPS, CHUNK, CHUNK)), _full((A_GROUPS, CHUNK, CHUNK)),
                  _full((1, D_MODEL)), _full((D_MODEL, D_MODEL))],
        out_specs=(tile(D_MODEL), tile(Z_W), tile(D_MODEL),
                   _full((D_MODEL, D_MODEL)), _full((1, D_MODEL)), _full((8, CHUNK))),
        scratch_shapes=[pltpu.VMEM((tm, D_MODEL), F32)],
        compiler_params=_params(dimension_semantics=("arbitrary", "arbitrary")),
    )(uv, z, qkv3, mkv3, x2, tgt2, bias, sinks, vg, vb, wt, bcol, g2, w_o)


def _mix_bwd(uv, dyc, qkv3, mkv3, bias, sinks, vg, vb, wt, wtt, bcol, buckets, tm):
    nb, s = qkv3.shape[0], qkv3.shape[1]
    nt = s // tm
    bpt = tm // CHUNK

    def body(uv_ref, dyc_ref, qkv_ref, mkv_ref, bias_ref, sink_ref, vg_ref, vb_ref, wt_ref, wtt_ref, bcol_ref,
             bk_ref, duv_ref, dqkv_ref, dmkv_ref, dwsp_ref, dbs_ref, dvg_ref, dvb_ref, dsink_ref, drel_ref,
             dkv_acc, dbias_acc, dsv_acc, dsink_acc):
        b, j = pl.program_id(0), pl.program_id(1)
        jt = nt - 1 - j

        @pl.when((b == 0) & (j == 0))
        def _():
            dwsp_ref[...] = jnp.zeros_like(dwsp_ref)
            dvg_ref[...] = jnp.zeros_like(dvg_ref)
            dvb_ref[...] = jnp.zeros_like(dvb_ref)
            dbias_acc[...] = jnp.zeros_like(dbias_acc)
            dsv_acc[...] = jnp.zeros_like(dsv_acc)
            dsink_acc[...] = jnp.zeros_like(dsink_acc)

        @pl.when(j == 0)
        def _():
            dmkv_ref[...] = jnp.zeros_like(dmkv_ref)
            dkv_acc[...] = jnp.zeros_like(dkv_acc)

        carry = dkv_acc[0:CHUNK, :]
        dkv_acc[...] = jnp.zeros_like(dkv_acc)
        dkv_acc[tm:tm + CHUNK, :] = carry

        lo = _half_masks(CHUNK)
        lob = _half_masks(2 * CHUNK)
        for blk in range(bpt):
            r0 = blk * CHUNK
            n = jt * bpt + blk
            for g in range(A_GROUPS):
                u, gu, gv, rstd, xhat, gam, vc, sv = _spatial_group(uv_ref, r0, g, vg_ref, vb_ref, wt_ref, bcol_ref, True)
                dya = dyc_ref[r0:r0 + CHUNK, g * CHUNK:(g + 1) * CHUNK]
                duv_ref[r0:r0 + CHUNK, g * CHUNK:(g + 1) * CHUNK] = (dya * sv * gu).astype(MM)
                dsv = dya * u
                dsvb = dsv.astype(MM)
                dsv_acc[g] += dsv
                dwsp_ref[g] += _dot_nt(dsvb, vc.astype(MM))
                dvc = _dot(wtt_ref[g], dsvb)
                dvg_ref[:, g * CHUNK:(g + 1) * CHUNK] += jnp.sum(dvc * xhat, axis=0, keepdims=True)
                dvb_ref[:, g * CHUNK:(g + 1) * CHUNK] += jnp.sum(dvc, axis=0, keepdims=True)
                dxh = dvc * gam
                dv = rstd * (dxh - jnp.mean(dxh, axis=-1, keepdims=True)
                             - xhat * jnp.mean(dxh * xhat, axis=-1, keepdims=True))
                duv_ref[r0:r0 + CHUNK, A_WIDTH + g * CHUNK:A_WIDTH + (g + 1) * CHUNK] = (dv * gv).astype(MM)
            pstart, cstart = _band_rows(n)
            kb = jnp.concatenate([qkv_ref[pl.ds(pstart, CHUNK), 256:384], qkv_ref[pl.ds(cstart, CHUNK), 256:384]], axis=0)
            vbnd = jnp.concatenate([qkv_ref[pl.ds(pstart, CHUNK), 384:512], qkv_ref[pl.ds(cstart, CHUNK), 384:512]], axis=0)
            kd = _dup_heads(kb)
            vd = _dup_heads(vbnd)
            first_add = _first_block_mask(n)
            dk_f, dv_f = [], []
            for kvh in range(2):
                q128 = qkv_ref[pl.ds(cstart, CHUNK), kvh * CHUNK:(kvh + 1) * CHUNK]
                do128 = dyc_ref[r0:r0 + CHUNK, A_WIDTH + kvh * CHUNK:A_WIDTH + (kvh + 1) * CHUNK]
                dq128 = jnp.zeros((CHUNK, CHUNK), F32)
                dkd = jnp.zeros((2 * CHUNK, CHUNK), F32)
                dvd = jnp.zeros((2 * CHUNK, CHUNK), F32)
                for gi in range(2):
                    h = 2 * kvh + gi
                    half = lo if gi == 0 else ~lo
                    qsel = jnp.where(half, q128.astype(F32), 0.0).astype(MM)
                    dosel = jnp.where(half, do128, 0.0).astype(MM)
                    probs, ps = _swa_probs(qsel, kd[kvh], bias_ref[h], sink_ref[h], first_add)
                    dp = _dot_nt(dosel, vd[kvh])
                    delta = jnp.sum(probs * dp, axis=-1, keepdims=True)
                    ds = probs * (dp - delta)
                    dbias_acc[h] += ds
                    dsink_acc[h:h + 1, :] += jnp.broadcast_to(
                        -jnp.sum(ps * delta, axis=0, keepdims=True), (1, CHUNK))
                    dss = (ds * SCALE).astype(MM)
                    dq128 = dq128 + jnp.where(half, _dot(dss, kd[kvh]), 0.0)
                    dkd = dkd + _dot_tn(dss, qsel)
                    dvd = dvd + _dot_tn(probs.astype(MM), dosel)
                dqkv_ref[r0:r0 + CHUNK, kvh * CHUNK:(kvh + 1) * CHUNK] = dq128.astype(MM)
                dk_f.append(dkd + pltpu.roll(dkd, 64, 1))
                dv_f.append(dvd + pltpu.roll(dvd, 64, 1))
            dkv_acc[r0:r0 + 2 * CHUNK, 0:CHUNK] += jnp.where(lob, dk_f[0], dk_f[1])
            dkv_acc[r0:r0 + 2 * CHUNK, CHUNK:2 * CHUNK] += jnp.where(lob, dv_f[0], dv_f[1])
        dqkv_ref[:, 256:512] = dkv_acc[CHUNK:CHUNK + tm, :].astype(MM)
        lot = _half_masks(tm)
        row0 = pl.multiple_of(jt * tm, tm)
        for g in range(2):
            q128 = qkv_ref[pl.ds(row0, tm), 512 + g * CHUNK:512 + (g + 1) * CHUNK]
            k128 = mkv_ref[:, g * CHUNK:(g + 1) * CHUNK]
            v128 = mkv_ref[:, MEM_LEN + g * CHUNK:MEM_LEN + (g + 1) * CHUNK]
            do128 = dyc_ref[:, 768 + g * CHUNK:768 + (g + 1) * CHUNK]
            dq128 = jnp.zeros((tm, CHUNK), F32)
            dk128 = jnp.zeros((MEM_LEN, CHUNK), F32)
            dv128 = jnp.zeros((MEM_LEN, CHUNK), F32)
            for hh in range(2):
                half = lot if hh == 0 else ~lot
                qsel = jnp.where(half, q128.astype(F32), 0.0).astype(MM)
                dosel = jnp.where(half, do128, 0.0).astype(MM)
                probs = _softmax(_dot_nt(qsel, k128) * SCALE)
                dp = _dot_nt(dosel, v128)
                ds = probs * (dp - jnp.sum(probs * dp, axis=-1, keepdims=True))
                dss = (ds * SCALE).astype(MM)
                dq128 = dq128 + jnp.where(half, _dot(dss, k128), 0.0)
                dk128 = dk128 + _dot_tn(dss, qsel)
                dv128 = dv128 + _dot_tn(probs.astype(MM), dosel)
            dqkv_ref[:, 512 + g * CHUNK:512 + (g + 1) * CHUNK] = dq128.astype(MM)
            dmkv_ref[:, g * CHUNK:(g + 1) * CHUNK] += dk128
            dmkv_ref[:, MEM_LEN + g * CHUNK:MEM_LEN + (g + 1) * CHUNK] += dv128

        @pl.when((b == nb - 1) & (j == nt - 1))
        def _():
            r = lax.broadcasted_iota(jnp.int32, (CHUNK, CHUNK), 0)
            c = lax.broadcasted_iota(jnp.int32, (CHUNK, CHUNK), 1)
            for g in range(A_GROUPS):
                dwsp_ref[g] = jnp.where(r >= c, dwsp_ref[g], 0.0)
                dbs_ref[g:g + 1, :] = jnp.sum(dsv_acc[g].T, axis=0, keepdims=True)
            rows = lax.broadcasted_iota(jnp.int32, (8, CHUNK), 0)
            cols = lax.broadcasted_iota(jnp.int32, (8, CHUNK), 1)
            sk = jnp.zeros((8, CHUNK), F32)
            for h in range(4):
                sk = sk + jnp.where((rows == 0) & (cols == h), jnp.broadcast_to(dsink_acc[h:h + 1, :], (8, CHUNK)), 0.0)
            dsink_ref[...] = sk
            bk = bk_ref[...]
            valid = _window_valid()
            rrow = lax.broadcasted_iota(jnp.int32, (N_BUCKETS, CHUNK), 0)
            rcol = lax.broadcasted_iota(jnp.int32, (N_BUCKETS, CHUNK), 1)
            acc = jnp.zeros((N_BUCKETS, CHUNK), F32)
            for bb in range(N_BUCKETS):
                hit = (bk == bb) & valid
                for h in range(4):
                    part = jnp.sum(jnp.where(hit, dbias_acc[h], 0.0), axis=-1, keepdims=True)
                    tot = jnp.sum(part, axis=0, keepdims=True)
                    acc = acc + jnp.where((rrow == bb) & (rcol == h), jnp.broadcast_to(tot, (N_BUCKETS, CHUNK)), 0.0)
            drel_ref[...] = acc

    t = nb * s
    tile = lambda w: pl.BlockSpec((tm, w), lambda b, j: (b * nt + nt - 1 - j, 0))
    return pl.pallas_call(
        body, name="mix_bwd", grid=(nb, nt),
        out_shape=(jax.ShapeDtypeStruct((t, UV_W), MM),
                   jax.ShapeDtypeStruct((t, QKV_W), MM),
                   jax.ShapeDtypeStruct((nb, MEM_LEN, 2 * MEM_LEN), F32),
                   jax.ShapeDtypeStruct((A_GROUPS, CHUNK, CHUNK), F32),
                   jax.ShapeDtypeStruct((A_GROUPS, CHUNK), F32),
                   jax.ShapeDtypeStruct((1, A_WIDTH), F32),
                   jax.ShapeDtypeStruct((1, A_WIDTH), F32),
                   jax.ShapeDtypeStruct((8, CHUNK), F32),
                   jax.ShapeDtypeStruct((N_BUCKETS, CHUNK), F32)),
        in_specs=[tile(UV_W), tile(D_MODEL),
                  pl.BlockSpec((None, s, QKV_W), lambda b, j: (b, 0, 0)),
                  pl.BlockSpec((None, MEM_LEN, 2 * MEM_LEN), lambda b, j: (b, 0, 0)),
                  _full((4, CHUNK, 2 * CHUNK)),
                  pl.BlockSpec(memory_space=pltpu.SMEM),
                  _full((1, A_WIDTH)), _full((1, A_WIDTH)),
                  _full((A_GROUPS, CHUNK, CHUNK)), _full((A_GROUPS, CHUNK, CHUNK)), _full((A_GROUPS, CHUNK, CHUNK)),
                  _full((CHUNK, 2 * CHUNK))],
        out_specs=(tile(UV_W), tile(QKV_W),
                   pl.BlockSpec((None, MEM_LEN, 2 * MEM_LEN), lambda b, j: (b, 0, 0)),
                   _full((A_GROUPS, CHUNK, CHUNK)), _full((A_GROUPS, CHUNK)),
                   _full((1, A_WIDTH)), _full((1, A_WIDTH)), _full((8, CHUNK)), _full((N_BUCKETS, CHUNK))),
        scratch_shapes=[pltpu.VMEM((tm + CHUNK, 2 * CHUNK), F32),
                        pltpu.VMEM((4, CHUNK, 2 * CHUNK), F32),
                        pltpu.VMEM((A_GROUPS, CHUNK, CHUNK), F32),
                        pltpu.VMEM((8, CHUNK), F32)],
        compiler_params=_params(dimension_semantics=("arbitrary", "arbitrary")),
    )(uv, dyc, qkv3, mkv3, bias, sinks, vg, vb, wt, wtt, bcol, buckets)


def _inproj_bwd(x2, dxo, duv, dqkv, dz, g1, w_in_t, tm):
    t = x2.shape[0]
    nt = t // tm

    def body(x_ref, dxo_ref, duv_ref, dqkv_ref, dz_ref, g_ref, w_ref, gx_ref, dw_hbm, dg_ref, acc, sem):
        i = pl.program_id(0)

        @pl.when(i == 0)
        def _():
            acc[...] = jnp.zeros_like(acc)
            dg_ref[...] = jnp.zeros_like(dg_ref)

        xf = x_ref[...]
        r = lax.rsqrt(jnp.mean(xf * xf, axis=-1, keepdims=True) + EPS)
        nx = xf * r
        gv = g_ref[...]
        h = (nx * gv).astype(MM)
        duv_t, dqkv_t, dz_t = duv_ref[...], dqkv_ref[...], dz_ref[...]
        acc[0:UV_W, :] += _dot_tn(duv_t, h)
        acc[UV_W:UV_W + QKV_W, :] += _dot_tn(dqkv_t, h)
        acc[UV_W + QKV_W:IN_WIDTH, :] += _dot_tn(dz_t, h)
        dh = (_dot(duv_t, w_ref[0:UV_W, :]) + _dot(dqkv_t, w_ref[UV_W:UV_W + QKV_W, :])
              + _dot(dz_t, w_ref[UV_W + QKV_W:IN_WIDTH, :]))
        dg_ref[...] += jnp.sum(dh * nx, axis=0, keepdims=True)
        dnx = dh * gv
        gx_ref[...] = dxo_ref[...] + r * (dnx - nx * jnp.mean(dnx * nx, axis=-1, keepdims=True))

        @pl.when(i == nt - 1)
        def _():
            cp = pltpu.make_async_copy(acc, dw_hbm, sem)
            cp.start()
            cp.wait()

    tile = lambda w: pl.BlockSpec((tm, w), lambda i: (i, 0))
    return pl.pallas_call(
        body, name="inproj_bwd", grid=(nt,),
        out_shape=(jax.ShapeDtypeStruct((t, D_MODEL), F32),
                   jax.ShapeDtypeStruct((IN_WIDTH, D_MODEL), F32),
                   jax.ShapeDtypeStruct((1, D_MODEL), F32)),
        in_specs=[tile(D_MODEL), tile(D_MODEL), tile(UV_W), tile(QKV_W), tile(Z_W),
                  _full((1, D_MODEL)),
                  pl.BlockSpec((IN_WIDTH, D_MODEL), lambda i: (0, 0), pipeline_mode=pl.Buffered(1))],
        out_specs=(tile(D_MODEL), pl.BlockSpec(memory_space=pl.ANY), _full((1, D_MODEL))),
        scratch_shapes=[pltpu.VMEM((IN_WIDTH, D_MODEL), F32), pltpu.SemaphoreType.DMA],
        compiler_params=_params(dimension_semantics=("arbitrary",)),
    )(x2, dxo, duv, dqkv, dz, g1, w_in_t)


def _adamw(w, g, m, v):
    m = ADAM_B1 * m + (1.0 - ADAM_B1) * g
    v = ADAM_B2 * v + (1.0 - ADAM_B2) * (g * g)
    m_hat = m / (1.0 - ADAM_B1 ** ADAM_STEP)
    v_hat = v / (1.0 - ADAM_B2 ** ADAM_STEP)
    delta = -ADAM_LR * (m_hat / (jnp.sqrt(v_hat) + ADAM_EPS) + ADAM_WD * w)
    return delta, m, v


_ROWS = 32


_S_LAYOUT = (((1, D_MODEL), 0), ((1, D_MODEL), 8), ((1, D_MODEL), 16),
             ((1, A_WIDTH), 24), ((1, A_WIDTH), 28), ((A_GROUPS, CHUNK), 32),
             ((1, 4), 36), ((N_BUCKETS, 4), 40),
             ((A_GROUPS * CHUNK, CHUNK), 72))
_LOSS_ROW = 37
_S_ROWS = 72 + A_GROUPS * CHUNK
_N_SMALL = len(_S_LAYOUT)


def _pack_rows(dst, refs):
    for (shp, r0), ref in zip(_S_LAYOUT, refs):
        if shp[0] == 1 and shp[1] >= CHUNK:
            for i in range(shp[1] // CHUNK):
                dst[r0 + i:r0 + i + 1, :] = ref[:, i * CHUNK:(i + 1) * CHUNK]
        elif ref.shape[-1] == CHUNK:
            dst[r0:r0 + shp[0], :] = ref[0:shp[0], :]
        else:
            dst[r0:r0 + shp[0], 0:shp[1]] = ref[...]


def _unpack_rows(src, refs):
    for (shp, r0), ref in zip(_S_LAYOUT, refs):
        if shp[0] == 1 and shp[1] >= CHUNK:
            for i in range(shp[1] // CHUNK):
                ref[:, i * CHUNK:(i + 1) * CHUNK] = src[r0 + i:r0 + i + 1, :]
        elif shp[1] == CHUNK:
            ref[...] = src[r0:r0 + shp[0], :]
        else:
            ref[...] = src[r0:r0 + shp[0], 0:shp[1]]


def _greduce(ga, gb, gc, small_g, loss_p):
    shapes = (ga.shape[1:], gb.shape[1:], gc.shape[1:])
    rs = _S_ROWS

    def body(*refs):
        it = iter(refs)
        take = lambda n: [next(it) for _ in range(n)]
        ga_ref, gb_ref, gc_ref = take(3)
        sg_refs = take(_N_SMALL)
        loss_ref, = take(1)
        oga, ogb, ogc, ogs = take(4)
        own_a, own_b, own_c, ra_a, ra_b, ra_c, sb_a, sb_b, sb_c, rb_a, rb_b, rb_c = take(12)
        gs_ref, rs_a, rs_b = take(3)
        ld_sem, sa_sem, ra_sem, sb_sem, rb_sem = take(5)

        gs_ref[...] = jnp.zeros_like(gs_ref)
        _pack_rows(gs_ref, sg_refs)
        gs_ref[_LOSS_ROW:_LOSS_ROW + 1, :] = loss_ref[0:1, :]

        x, y, cc = lax.axis_index("x"), lax.axis_index("y"), lax.axis_index("c")
        myq = 2 * x + y
        me = (x, y, cc)
        sib = (x, y, 1 - cc)
        chips = [(1 - x, y), (x, 1 - y), (1 - x, 1 - y)]
        gin = (ga_ref, gb_ref, gc_ref)
        own = (own_a, own_b, own_c)
        rcv_a = (ra_a, ra_b, ra_c)
        sbuf = (sb_a, sb_b, sb_c)
        rcv_b = (rb_a, rb_b, rb_c)

        def remote(src, dst, ssem, rsem, to):
            return pltpu.make_async_remote_copy(src_ref=src, dst_ref=dst, send_sem=ssem, recv_sem=rsem,
                                                device_id=to, device_id_type=MESH)

        loads, sends_a = [], []
        for arr in range(3):
            for q in range(4):
                loads.append(pltpu.make_async_copy(gin[arr].at[2 * q + cc], own[arr].at[q], ld_sem.at[arr, q]))
                sends_a.append(remote(gin[arr].at[2 * q + 1 - cc], rcv_a[arr].at[q],
                                      sa_sem.at[arr, q], ra_sem.at[arr, q], sib))
        small_a = remote(gs_ref, rs_a, sa_sem.at[3, 0], ra_sem.at[3, 0], sib)
        for cp in loads + sends_a + [small_a]:
            cp.start()
        for cp in loads:
            cp.wait()
        for arr in range(3):
            for q in range(4):
                remote(gin[arr].at[2 * q + 1 - cc], rcv_a[arr].at[q],
                       sa_sem.at[arr, q], ra_sem.at[arr, q], me).wait_recv()
        remote(gs_ref, rs_a, sa_sem.at[3, 0], ra_sem.at[3, 0], me).wait_recv()

        for arr in range(3):
            nrow = shapes[arr][0]

            def add_rows(i, _, arr=arr):
                r = pl.ds(pl.multiple_of(i * _ROWS, _ROWS), _ROWS)
                for q in range(4):
                    rcv_a[arr][q, r, :] = rcv_a[arr][q, r, :] + own[arr][q, r, :]
                return 0

            lax.fori_loop(0, nrow // _ROWS, add_rows, 0)
        rs_b[myq] = gs_ref[...] + rs_a[...]

        sends_b = []
        for j, chip in enumerate(chips):
            qj = 2 * chip[0] + chip[1]
            to = (chip[0], chip[1], cc)
            for arr in range(3):
                nrow = shapes[arr][0]

                def cast_rows(i, _, arr=arr, j=j, qj=qj):
                    r = pl.ds(pl.multiple_of(i * _ROWS, _ROWS), _ROWS)
                    sbuf[arr][j, r, :] = rcv_a[arr][qj, r, :].astype(BF16)
                    return 0

                lax.fori_loop(0, nrow // _ROWS, cast_rows, 0)
                cp = remote(sbuf[arr].at[j], rcv_b[arr].at[j], sb_sem.at[arr, j], rb_sem.at[arr, j], to)
                cp.start()
                sends_b.append(cp)
            cp = remote(rs_b.at[myq], rs_b.at[myq], sb_sem.at[3, j], rb_sem.at[3, j], to)
            cp.start()
            sends_b.append(cp)
        for j in range(3):
            for arr in range(3):
                remote(sbuf[arr].at[j], rcv_b[arr].at[j], sb_sem.at[arr, j], rb_sem.at[arr, j], me).wait_recv()
            remote(rs_b.at[myq], rs_b.at[myq], sb_sem.at[3, j], rb_sem.at[3, j], me).wait_recv()

        for arr, og in enumerate((oga, ogb, ogc)):
            nrow = shapes[arr][0]

            def tot(i, _, arr=arr, og=og):
                r = pl.ds(pl.multiple_of(i * _ROWS, _ROWS), _ROWS)
                g = rcv_a[arr][myq, r, :]
                for j in range(3):
                    g = g + rcv_b[arr][j, r, :].astype(F32)
                og[r, :] = g
                return 0

            lax.fori_loop(0, nrow // _ROWS, tot, 0)

        def tot_s(i, _):
            r = pl.ds(pl.multiple_of(i * 8, 8), 8)
            ogs[r, :] = ((rs_b[0, r, :] + rs_b[1, r, :]) + rs_b[2, r, :]) + rs_b[3, r, :]
            return 0

        lax.fori_loop(0, rs // 8, tot_s, 0)

        for cp in sends_a + [small_a] + sends_b:
            cp.wait_send()

    vm = pl.BlockSpec(memory_space=pltpu.VMEM)
    anyspec = pl.BlockSpec(memory_space=pl.ANY)
    out_shape = tuple([jax.ShapeDtypeStruct(shp, F32) for shp in shapes] + [jax.ShapeDtypeStruct((rs, CHUNK), F32)])
    scratch = ([pltpu.VMEM((4,) + shp, F32) for shp in shapes]
               + [pltpu.VMEM((4,) + shp, F32) for shp in shapes]
               + [pltpu.VMEM((3,) + shp, BF16) for shp in shapes]
               + [pltpu.VMEM((3,) + shp, BF16) for shp in shapes]
               + [pltpu.VMEM((rs, CHUNK), F32), pltpu.VMEM((rs, CHUNK), F32), pltpu.VMEM((4, rs, CHUNK), F32)]
               + [pltpu.SemaphoreType.DMA((3, 4)), pltpu.SemaphoreType.DMA((4, 4)), pltpu.SemaphoreType.DMA((4, 4)),
                  pltpu.SemaphoreType.DMA((4, 3)), pltpu.SemaphoreType.DMA((4, 3))])
    return pl.pallas_call(
        body, name="greduce",
        out_shape=out_shape,
        in_specs=[anyspec, anyspec, anyspec] + [vm] * (_N_SMALL + 1),
        out_specs=tuple([vm] * len(out_shape)),
        scratch_shapes=scratch,
        compiler_params=_params(),
    )(ga, gb, gc, *small_g, loss_p)


def _update(ta, tb, tc, ts, big_wmv, small_wmv):
    shapes = (big_wmv[0].shape, tb.shape, tc.shape)
    rs = _S_ROWS
    small_shapes = [tuple(a.shape) for a in small_wmv[0]]

    def body(*refs):
        it = iter(refs)
        take = lambda n: [next(it) for _ in range(n)]
        ga_ref, gb_ref, gc_ref, gs_ref = take(4)
        wa, ma, va, wb, mb, vb_, wc, mc, vc = take(9)
        sw_refs, sm_refs, sv_refs = take(_N_SMALL), take(_N_SMALL), take(_N_SMALL)
        oga, oda, oma, ova, ogb, odb, omb, ovb, ogc, odc, omc, ovc = take(12)
        so_refs = [take(_N_SMALL) for _ in range(4)]
        loss_out, = take(1)
        ws, ms, vs, ods, oms, ovs, pad_t, ga_nat = take(8)

        for buf in (ws, ms, vs):
            buf[...] = jnp.zeros_like(buf)
        _pack_rows(ws, sw_refs)
        _pack_rows(ms, sm_refs)
        _pack_rows(vs, sv_refs)

        pad_t[SHARD_IN:SHARD_PAD, :] = jnp.zeros((SHARD_PAD - SHARD_IN, D_MODEL), F32)
        pad_t[0:SHARD_IN, :] = ga_ref[...]
        for rb in range(0, SHARD_PAD, CHUNK):
            for cb in range(0, D_MODEL, CHUNK):
                ga_nat[cb:cb + CHUNK, rb:rb + CHUNK] = pad_t[rb:rb + CHUNK, cb:cb + CHUNK].T

        big = ((ga_nat, wa, ma, va, oga, oda, oma, ova), (gb_ref, wb, mb, vb_, ogb, odb, omb, ovb),
               (gc_ref, wc, mc, vc, ogc, odc, omc, ovc))
        for arr in range(3):
            g_r, w_r, m_r, v_r, og, od, om, ov = big[arr]
            nrow, ncol = shapes[arr]

            def upd(i, _, g_r=g_r, w_r=w_r, m_r=m_r, v_r=v_r, og=og, od=od, om=om, ov=ov, ncol=ncol):
                r = pl.ds(pl.multiple_of(i * _ROWS, _ROWS), _ROWS)
                g = g_r[r, 0:ncol]
                d, m, v = _adamw(w_r[r, :], g, m_r[r, :], v_r[r, :])
                og[r, :] = g
                od[r, :] = d
                om[r, :] = m
                ov[r, :] = v
                return 0

            lax.fori_loop(0, nrow // _ROWS, upd, 0)

        def upd_s(i, _):
            r = pl.ds(pl.multiple_of(i * 8, 8), 8)
            d, m, v = _adamw(ws[r, :], gs_ref[r, :], ms[r, :], vs[r, :])
            ods[r, :] = d
            oms[r, :] = m
            ovs[r, :] = v
            return 0

        lax.fori_loop(0, rs // 8, upd_s, 0)
        for k, buf in enumerate((gs_ref, ods, oms, ovs)):
            _unpack_rows(buf, so_refs[k])
        loss_out[...] = gs_ref[_LOSS_ROW:_LOSS_ROW + 1, 0:1]

    vm = pl.BlockSpec(memory_space=pltpu.VMEM)
    big_out = []
    for shp in shapes:
        big_out += [jax.ShapeDtypeStruct(shp, F32)] * 4
    small_out = [jax.ShapeDtypeStruct(shp, F32) for shp in small_shapes] * 4
    out_shape = tuple(big_out + small_out + [jax.ShapeDtypeStruct((1, 1), F32)])
    n_in = 4 + 9 + 3 * _N_SMALL
    return pl.pallas_call(
        body, name="update",
        out_shape=out_shape,
        in_specs=[vm] * n_in,
        out_specs=tuple([vm] * len(out_shape)),
        scratch_shapes=([pltpu.VMEM((rs, CHUNK), F32) for _ in range(6)]
                        + [pltpu.VMEM((SHARD_PAD, D_MODEL), F32), pltpu.VMEM((D_MODEL, SHARD_PAD), F32)]),
        compiler_params=_params(),
    )(ta, tb, tc, ts, *big_wmv, *small_wmv[0], *small_wmv[1], *small_wmv[2])


def _local_step(x, mem, loss_target, pre_norm_g, post_norm_g, mem_norm_g, v_norm_g, v_norm_b, w_spatial, b_spatial,
                attn_sinks, rel_bias, w_in_t, w_o, w_mkv):
    nb, s, _ = x.shape
    t = nb * s
    x2 = x.reshape(t, D_MODEL)
    tgt2 = loss_target.reshape(t, D_MODEL)
    mem2 = mem.reshape(nb * MEM_LEN, D_MODEL)
    tm_mix = min(256, s)
    tm_proj = min(512, t)

    buckets = jnp.asarray(_t5_buckets())
    sinks = attn_sinks.reshape(4)
    bias, wt, wtt, bcol = _prep(rel_bias, w_spatial[0], b_spatial[0], buckets)

    uv, qkv, z = _inproj_fwd(x2, pre_norm_g, w_in_t, tm_proj)
    mkv = _memkv_fwd(mem2, mem_norm_g, w_mkv)
    qkv3 = qkv.reshape(nb, s, QKV_W)
    mkv3 = mkv.reshape(nb, MEM_LEN, 2 * MEM_LEN)
    dyc, dz, dxo, dwo, dg2, loss_p = _mix_fwd(uv, z, qkv3, mkv3, x2, tgt2, bias, sinks, v_norm_g, v_norm_b,
                                              wt, bcol, post_norm_g, w_o, tm_mix)
    duv, dqkv, dmkv, dwsp, dbs, dvg, dvb, dsink, drel = _mix_bwd(uv, dyc, qkv3, mkv3, bias, sinks, v_norm_g, v_norm_b,
                                                                 wt, wtt, bcol, buckets, tm_mix)
    dwmkv, dgm = _memkv_bwd(dmkv.reshape(nb * MEM_LEN, 2 * MEM_LEN), mem2, mem_norm_g, w_mkv)
    gx, dw_in_t, dg1 = _inproj_bwd(x2, dxo, duv, dqkv, dz, pre_norm_g, w_in_t, tm_proj)
    small = [dg1, dg2, dgm, dvg, dvb, dbs, dsink, drel, dwsp.reshape(A_GROUPS * CHUNK, CHUNK)]
    return loss_p, gx.reshape(nb, s, D_MODEL), dw_in_t, dwo, dwmkv, small


def kernel(x, mem, pre_norm_g, post_norm_g, mem_norm_g, w_in, w_mem_kv, v_norm_g, v_norm_b, w_spatial, b_spatial, attn_sinks, rel_bias, w_out, loss_target, m_pre_norm_g, m_post_norm_g, m_mem_norm_g, m_w_in, m_w_mem_kv, m_v_norm_g, m_v_norm_b, m_w_spatial, m_b_spatial, m_attn_sinks, m_rel_bias, m_w_out, v_pre_norm_g, v_post_norm_g, v_mem_norm_g, v_w_in, v_w_mem_kv, v_v_norm_g, v_v_norm_b, v_w_spatial, v_b_spatial, v_attn_sinks, v_rel_bias, v_w_out):
    nb, s, _ = x.shape

    sh_a = (w_in[0], m_w_in[0], v_w_in[0])
    sh_b = (w_out[0], m_w_out[0], v_w_out[0])
    sh_c = (w_mem_kv[0], m_w_mem_kv[0], v_w_mem_kv[0])
    wa, wb, wc = _wgather(sh_a[0], sh_b[0], sh_c[0])
    w_in_t = wa.reshape(IN_WIDTH, D_MODEL)
    w_o = wb.reshape(D_MODEL, D_MODEL)
    w_mkv = wc.reshape(D_MODEL, 2 * MEM_LEN)

    loss_p, gx, dw_in_t, dwo, dwmkv, small_grads = _local_step(
        x, mem, loss_target, pre_norm_g, post_norm_g, mem_norm_g, v_norm_g, v_norm_b, w_spatial, b_spatial,
        attn_sinks, rel_bias, w_in_t, w_o, w_mkv)

    small_names = ["pre_norm_g", "post_norm_g", "mem_norm_g", "v_norm_g", "v_norm_b", "b_spatial", "attn_sinks",
                   "rel_bias", "w_spatial"]
    given = dict(pre_norm_g=(pre_norm_g, m_pre_norm_g, v_pre_norm_g), post_norm_g=(post_norm_g, m_post_norm_g, v_post_norm_g),
                 mem_norm_g=(mem_norm_g, m_mem_norm_g, v_mem_norm_g), v_norm_g=(v_norm_g, m_v_norm_g, v_v_norm_g),
                 v_norm_b=(v_norm_b, m_v_norm_b, v_v_norm_b), b_spatial=(b_spatial, m_b_spatial, v_b_spatial),
                 attn_sinks=(attn_sinks, m_attn_sinks, v_attn_sinks), rel_bias=(rel_bias, m_rel_bias, v_rel_bias),
                 w_spatial=(w_spatial, m_w_spatial, v_w_spatial))
    small_wmv = [[given[n][k].reshape(shp) for n, (shp, _) in zip(small_names, _S_LAYOUT)] for k in range(3)]

    ta, tb, tc, ts = _greduce(dw_in_t.reshape(N_DEV, SHARD_IN, D_MODEL), dwo.reshape(N_DEV, SHARD_O, D_MODEL),
                              dwmkv.reshape(N_DEV, SHARD_O, 2 * MEM_LEN), small_grads, loss_p)
    outs = _update(ta, tb, tc, ts, (*sh_a, *sh_b, *sh_c), small_wmv)
    ra, rb, rc = outs[0:4], outs[4:8], outs[8:12]
    loss = outs[12 + 4 * _N_SMALL].reshape(())

    res = {}
    for k, kind in enumerate(("grad", "delta", "new_m", "new_v")):
        res[kind, "w_in"] = ra[k][None]
        res[kind, "w_out"] = rb[k][None]
        res[kind, "w_mem_kv"] = rc[k][None]
        for i, n in enumerate(small_names):
            res[kind, n] = outs[12 + k * _N_SMALL + i].reshape(given[n][0].shape)
    order = ["pre_norm_g", "post_norm_g", "mem_norm_g", "w_in", "w_mem_kv", "v_norm_g", "v_norm_b", "w_spatial",
             "b_spatial", "attn_sinks", "rel_bias", "w_out"]
    flat = [res[kind, n] for kind in ("grad", "delta", "new_m", "new_v") for n in order]
    return (loss, gx.reshape(nb, s, D_MODEL), *flat)
```

```python
import functools

import numpy as np
import jax
import jax.numpy as jnp
from jax import lax
from jax.experimental import pallas as pl
from jax.experimental.pallas import tpu as pltpu

F32 = jnp.float32
BF16 = jnp.bfloat16
MM = jnp.bfloat16

D_MODEL = 1024
CHUNK = 128
A_GROUPS = 4
A_WIDTH = 512
UV_W = 1024
QKV_W = 768
Z_W = 1024
IN_WIDTH = UV_W + QKV_W + Z_W
MEM_LEN = 256
N_BUCKETS = 32
MAX_DISTANCE = 128
EPS = 1e-6
NEG = -1e30
SCALE = 0.125
N_DEV = 8
SHARD_IN = IN_WIDTH // N_DEV
SHARD_O = D_MODEL // N_DEV

ADAM_LR = 0.001
ADAM_B1 = 0.9
ADAM_B2 = 0.999
ADAM_EPS = 1e-08
ADAM_WD = 0.01
ADAM_STEP = 10

VMEM_LIMIT = 58 * 1024 * 1024

_GELU_C = 0.7978845608028654
_GELU_A = 0.044715

MESH = pl.DeviceIdType.MESH


def _dot(a, b):
    return lax.dot_general(a, b, (((1,), (0,)), ((), ())), preferred_element_type=F32)


def _dot_nt(a, b):
    return lax.dot_general(a, b, (((1,), (1,)), ((), ())), preferred_element_type=F32)


def _dot_tn(a, b):
    return lax.dot_general(a, b, (((0,), (0,)), ((), ())), preferred_element_type=F32)


def _gelu(x):
    x2 = x * x
    t = jnp.tanh(_GELU_C * (x + _GELU_A * x * x2))
    return 0.5 * x * (1.0 + t)


def _gelu_and_grad(x):
    x2 = x * x
    t = jnp.tanh(_GELU_C * (x + _GELU_A * x * x2))
    g = 0.5 * x * (1.0 + t)
    dg = 0.5 * (1.0 + t) + 0.5 * x * (1.0 - t * t) * (_GELU_C * (1.0 + 3.0 * _GELU_A * x2))
    return g, dg


def _t5_buckets():
    qi = np.arange(CHUNK)[:, None]
    kj = np.arange(2 * CHUNK)[None, :]
    n = np.maximum(qi + CHUNK - kj, 0)
    max_exact = N_BUCKETS // 2
    large = max_exact + (np.log(np.maximum(n, 1) / max_exact) / np.log(MAX_DISTANCE / max_exact)
                         * (N_BUCKETS - max_exact)).astype(np.int32)
    large = np.minimum(large, N_BUCKETS - 1)
    return np.where(n < max_exact, n, large).astype(np.int32)


def _params(**kw):
    return pltpu.CompilerParams(vmem_limit_bytes=VMEM_LIMIT, **kw)


def _full(shape):
    nd = len(shape)
    return pl.BlockSpec(shape, lambda *_: (0,) * nd)


def _window_valid():
    qi = lax.broadcasted_iota(jnp.int32, (CHUNK, 2 * CHUNK), 0)
    kj = lax.broadcasted_iota(jnp.int32, (CHUNK, 2 * CHUNK), 1)
    dist = qi + CHUNK - kj
    return (dist >= 0) & (dist < CHUNK)


def _wgather(a, b, c):
    def body(a_ref, b_ref, c_ref, oa, ob, oc, ssem, rsem):
        x, y, cc = lax.axis_index("x"), lax.axis_index("y"), lax.axis_index("c")
        me = 4 * x + 2 * y + cc
        sib = (x, y, 1 - cc)
        chips = [(1 - x, y), (x, 1 - y), (1 - x, 1 - y)]
        outs = (oa, ob, oc)
        oa[me] = a_ref[...].astype(BF16)
        ob[me] = b_ref[...].astype(BF16)
        oc[me] = c_ref[...].astype(BF16)

        def copy(arr, k, blk, to):
            r = outs[arr].at[blk]
            return pltpu.make_async_remote_copy(src_ref=r, dst_ref=r, send_sem=ssem.at[arr, k],
                                                recv_sem=rsem.at[arr, k], device_id=to, device_id_type=MESH)

        def idx(chip, core):
            return 4 * chip[0] + 2 * chip[1] + core

        first = []
        for arr in range(3):
            first.append(copy(arr, 0, me, sib))
            for j, chip in enumerate(chips):
                first.append(copy(arr, 1 + j, me, (chip[0], chip[1], cc)))
        for cp in first:
            cp.start()
        passed = []
        for j, chip in enumerate(chips):
            for arr in range(3):
                copy(arr, 1 + j, idx(chip, cc), (x, y, cc)).wait_recv()
                cp = copy(arr, 4 + j, idx(chip, cc), sib)
                cp.start()
                passed.append(cp)
        for arr in range(3):
            copy(arr, 0, idx((x, y), 1 - cc), (x, y, cc)).wait_recv()
            for j, chip in enumerate(chips):
                copy(arr, 4 + j, idx(chip, 1 - cc), (x, y, cc)).wait_recv()
        for cp in first + passed:
            cp.wait_send()

    vm = pl.BlockSpec(memory_space=pltpu.VMEM)
    return pl.pallas_call(
        body, name="wgather",
        out_shape=(jax.ShapeDtypeStruct((N_DEV, SHARD_IN, D_MODEL), BF16),
                   jax.ShapeDtypeStruct((N_DEV,) + b.shape, BF16),
                   jax.ShapeDtypeStruct((N_DEV,) + c.shape, BF16)),
        in_specs=[vm, vm, vm], out_specs=(vm, vm, vm),
        scratch_shapes=[pltpu.SemaphoreType.DMA((3, 7)), pltpu.SemaphoreType.DMA((3, 7))],
        compiler_params=_params(),
    )(a, b, c)


def _prep(rel_bias, w_sp, b_sp, buckets):
    def body(rb_ref, w_ref, b_ref, bk_ref, bias_ref, wt_ref, wtt_ref, bcol_ref):
        valid = _window_valid()
        bk = bk_ref[...]
        acc = [jnp.full((CHUNK, 2 * CHUNK), NEG, F32) for _ in range(4)]
        for b in range(N_BUCKETS):
            hit = (bk == b) & valid
            for h in range(4):
                acc[h] = jnp.where(hit, rb_ref[b, h], acc[h])
        for h in range(4):
            bias_ref[h] = acc[h]
        r = lax.broadcasted_iota(jnp.int32, (CHUNK, CHUNK), 0)
        c = lax.broadcasted_iota(jnp.int32, (CHUNK, CHUNK), 1)
        for g in range(A_GROUPS):
            w = jnp.where(r >= c, w_ref[g], 0.0)
            wt_ref[g] = w.astype(MM)
            wtt_ref[g] = w.T.astype(MM)
            bcol_ref[g] = jnp.broadcast_to(b_ref[g:g + 1, :], (CHUNK, CHUNK)).T

    return pl.pallas_call(
        body, name="prep",
        out_shape=(jax.ShapeDtypeStruct((4, CHUNK, 2 * CHUNK), F32),
                   jax.ShapeDtypeStruct((A_GROUPS, CHUNK, CHUNK), MM),
                   jax.ShapeDtypeStruct((A_GROUPS, CHUNK, CHUNK), MM),
                   jax.ShapeDtypeStruct((A_GROUPS, CHUNK, CHUNK), F32)),
        in_specs=[pl.BlockSpec(memory_space=pltpu.SMEM), pl.BlockSpec(memory_space=pltpu.VMEM),
                  pl.BlockSpec(memory_space=pltpu.VMEM), pl.BlockSpec(memory_space=pltpu.VMEM)],
        out_specs=tuple(pl.BlockSpec(memory_space=pltpu.VMEM) for _ in range(4)),
    )(rel_bias, w_sp, b_sp, buckets)


def _inproj_fwd(x2, g1, w_in_t, tm):
    t = x2.shape[0]

    def body(x_ref, g_ref, w_ref, uv_ref, qkv_ref, z_ref):
        xf = x_ref[...]
        r = lax.rsqrt(jnp.mean(xf * xf, axis=-1, keepdims=True) + EPS)
        h = (xf * r * g_ref[...]).astype(MM)
        uv_ref[...] = _dot_nt(h, w_ref[0:UV_W, :])
        qkv_ref[...] = _dot_nt(h, w_ref[UV_W:UV_W + QKV_W, :]).astype(MM)
        z_ref[...] = _dot_nt(h, w_ref[UV_W + QKV_W:IN_WIDTH, :])

    return pl.pallas_call(
        body, name="inproj_fwd", grid=(t // tm,),
        out_shape=(jax.ShapeDtypeStruct((t, UV_W), F32),
                   jax.ShapeDtypeStruct((t, QKV_W), MM),
                   jax.ShapeDtypeStruct((t, Z_W), F32)),
        in_specs=[pl.BlockSpec((tm, D_MODEL), lambda i: (i, 0)),
                  _full((1, D_MODEL)),
                  pl.BlockSpec((IN_WIDTH, D_MODEL), lambda i: (0, 0), pipeline_mode=pl.Buffered(1))],
        out_specs=(pl.BlockSpec((tm, UV_W), lambda i: (i, 0)),
                   pl.BlockSpec((tm, QKV_W), lambda i: (i, 0)),
                   pl.BlockSpec((tm, Z_W), lambda i: (i, 0))),
        compiler_params=_params(dimension_semantics=("arbitrary",)),
    )(x2, g1, w_in_t)


def _memkv_fwd(mem2, gm, w_mkv):
    tmem = mem2.shape[0]

    def body(m_ref, g_ref, w_ref, o_ref):
        xf = m_ref[...]
        r = lax.rsqrt(jnp.mean(xf * xf, axis=-1, keepdims=True) + EPS)
        hm = (xf * r * g_ref[...]).astype(MM)
        o_ref[...] = _dot(hm, w_ref[...]).astype(MM)

    vm = pl.BlockSpec(memory_space=pltpu.VMEM)
    return pl.pallas_call(
        body, name="memkv_fwd",
        out_shape=jax.ShapeDtypeStruct((tmem, 2 * MEM_LEN), MM),
        in_specs=[vm, vm, vm], out_specs=vm,
        compiler_params=_params(),
    )(mem2, gm, w_mkv)


def _memkv_bwd(dmkv, mem2, gm, w_mkv):
    def body(d_ref, m_ref, g_ref, w_ref, dw_ref, dg_ref):
        xf = m_ref[...]
        r = lax.rsqrt(jnp.mean(xf * xf, axis=-1, keepdims=True) + EPS)
        nm = xf * r
        hm = (nm * g_ref[...]).astype(MM)
        d = d_ref[...].astype(MM)
        dw_ref[...] = _dot_tn(hm, d)
        dhm = _dot_nt(d, w_ref[...])
        dg_ref[...] = jnp.sum(dhm * nm, axis=0, keepdims=True)

    vm = pl.BlockSpec(memory_space=pltpu.VMEM)
    return pl.pallas_call(
        body, name="memkv_bwd",
        out_shape=(jax.ShapeDtypeStruct((D_MODEL, 2 * MEM_LEN), F32),
                   jax.ShapeDtypeStruct((1, D_MODEL), F32)),
        in_specs=[vm, vm, vm, vm], out_specs=(vm, vm),
        compiler_params=_params(),
    )(dmkv, mem2, gm, w_mkv)


def _half_masks(rows):
    lane = lax.broadcasted_iota(jnp.int32, (rows, CHUNK), 1)
    return lane < 64


def _dup_heads(band):
    b32 = band.astype(F32)
    rolled = pltpu.roll(b32, 64, 1)
    lo = _half_masks(band.shape[0])
    return (jnp.where(lo, b32, rolled).astype(MM), jnp.where(lo, rolled, b32).astype(MM))


def _swa_probs(qsel, kd, bias_h, sink_h, first_add):
    s = _dot_nt(qsel, kd) * SCALE + bias_h + first_add
    m = jnp.maximum(jnp.max(s, axis=-1, keepdims=True), sink_h)
    p = jnp.exp(s - m)
    es = jnp.exp(sink_h - m)
    inv = 1.0 / (jnp.sum(p, axis=-1, keepdims=True) + es)
    return p * inv, es * inv


def _softmax(s):
    m = jnp.max(s, axis=-1, keepdims=True)
    p = jnp.exp(s - m)
    return p * (1.0 / jnp.sum(p, axis=-1, keepdims=True))


def _band_rows(n):
    cstart = pl.multiple_of(n * CHUNK, CHUNK)
    pstart = pl.multiple_of(jnp.maximum(n - 1, 0) * CHUNK, CHUNK)
    return pstart, cstart


def _first_block_mask(n):
    col = lax.broadcasted_iota(jnp.int32, (CHUNK, 2 * CHUNK), 1)
    return jnp.where((col < CHUNK) & (n == 0), NEG, 0.0)


def _spatial_group(uv_ref, r0, g, vg_ref, vb_ref, wt_ref, bcol_ref, with_grad):
    au = uv_ref[r0:r0 + CHUNK, g * CHUNK:(g + 1) * CHUNK]
    av = uv_ref[r0:r0 + CHUNK, A_WIDTH + g * CHUNK:A_WIDTH + (g + 1) * CHUNK]
    if with_grad:
        u, du = _gelu_and_grad(au)
        v, dv = _gelu_and_grad(av)
    else:
        u, v = _gelu(au), _gelu(av)
        du = dv = None
    mu = jnp.mean(v, axis=-1, keepdims=True)
    xc = v - mu
    rstd = lax.rsqrt(jnp.mean(xc * xc, axis=-1, keepdims=True) + EPS)
    xhat = xc * rstd
    gam = vg_ref[:, g * CHUNK:(g + 1) * CHUNK]
    vc = xhat * gam + vb_ref[:, g * CHUNK:(g + 1) * CHUNK]
    sv = _dot(wt_ref[g], vc.astype(MM)) + bcol_ref[g]
    return u, du, dv, rstd, xhat, gam, vc, sv


SQ_OFF, SK_OFF, SV_OFF, MQ_OFF = 0, 256, 384, 512
YB_OFF, YC_OFF = 512, 768


def _mix(uv, z, qkv3, mkv3, x2, tgt2, bias, sinks, vg, vb, wt, wtt, bcol, g2, w_o, buckets, tm):
    nb, s = qkv3.shape[0], qkv3.shape[1]
    nt = s // tm
    bpt = tm // CHUNK

    def body(uv_ref, z_ref, qkv_ref, mkv_ref, x_ref, t_ref, bias_ref, sink_ref, vg_ref, vb_ref, wt_ref, wtt_ref,
             bcol_ref, g2_ref, wo_ref, bk_ref,
             duv_ref, dqkv_ref, dz_ref, dxo_ref, dmkv_ref, dwo_ref, dg2_ref, loss_ref, dwsp_ref, dbs_ref,
             dvg_ref, dvb_ref, dsink_ref, drel_ref,
             ycat, dyc, u_s, gu_s, gv_s, xh_s, rs_s, sv_s, vc_s, pb_s, ps_s, pc_s, kd_s, vd_s,
             dkv_acc, dbias_acc, dsv_acc, dsink_acc):
        b, j = pl.program_id(0), pl.program_id(1)
        jt = nt - 1 - j

        @pl.when((b == 0) & (j == 0))
        def _():
            for ref in (dwo_ref, dg2_ref, loss_ref, dwsp_ref, dvg_ref, dvb_ref, dbias_acc, dsv_acc, dsink_acc):
                ref[...] = jnp.zeros_like(ref)

        @pl.when(j == 0)
        def _():
            dmkv_ref[...] = jnp.zeros_like(dmkv_ref)
            dkv_acc[...] = jnp.zeros_like(dkv_acc)

        carry = dkv_acc[0:CHUNK, :]
        dkv_acc[...] = jnp.zeros_like(dkv_acc)
        dkv_acc[tm:tm + CHUNK, :] = carry

        lo = _half_masks(CHUNK)
        lob = _half_masks(2 * CHUNK)
        lot = _half_masks(tm)
        row0 = pl.multiple_of(jt * tm, tm)

        for blk in range(bpt):
            r0 = blk * CHUNK
            rows = slice(r0, r0 + CHUNK)
            n = jt * bpt + blk
            for g in range(A_GROUPS):
                cg = slice(g * CHUNK, (g + 1) * CHUNK)
                u, gu = _gelu_and_grad(uv_ref[rows, cg])
                v, gv = _gelu_and_grad(uv_ref[rows, A_WIDTH + g * CHUNK:A_WIDTH + (g + 1) * CHUNK])
                mu = jnp.mean(v, axis=-1, keepdims=True)
                xc = v - mu
                rstd = lax.rsqrt(jnp.mean(xc * xc, axis=-1, keepdims=True) + EPS)
                xhat = xc * rstd
                vc = (xhat * vg_ref[:, cg] + vb_ref[:, cg]).astype(MM)
                sv = _dot(wt_ref[g], vc) + bcol_ref[g]
                u_s[rows, cg] = u
                gu_s[rows, cg] = gu
                gv_s[rows, cg] = gv
                xh_s[rows, cg] = xhat
                rs_s[rows, cg] = jnp.broadcast_to(rstd, (CHUNK, CHUNK))
                sv_s[rows, cg] = sv
                vc_s[rows, cg] = vc
                ycat[rows, cg] = u * sv
            pstart, cstart = _band_rows(n)
            kd = _dup_heads(jnp.concatenate([qkv_ref[pl.ds(pstart, CHUNK), SK_OFF:SK_OFF + CHUNK],
                                             qkv_ref[pl.ds(cstart, CHUNK), SK_OFF:SK_OFF + CHUNK]], axis=0))
            vd = _dup_heads(jnp.concatenate([qkv_ref[pl.ds(pstart, CHUNK), SV_OFF:SV_OFF + CHUNK],
                                             qkv_ref[pl.ds(cstart, CHUNK), SV_OFF:SV_OFF + CHUNK]], axis=0))
            first_add = _first_block_mask(n)
            for kvh in range(2):
                kd_s[blk * 2 + kvh] = kd[kvh]
                vd_s[blk * 2 + kvh] = vd[kvh]
                q128 = qkv_ref[pl.ds(cstart, CHUNK), SQ_OFF + kvh * CHUNK:SQ_OFF + (kvh + 1) * CHUNK].astype(F32)
                outs = []
                for gi in range(2):
                    h = 2 * kvh + gi
                    qsel = jnp.where(lo if gi == 0 else ~lo, q128, 0.0).astype(MM)
                    probs, ps = _swa_probs(qsel, kd[kvh], bias_ref[h], sink_ref[h], first_add)
                    pb_s[blk * 4 + h] = probs
                    ps_s[blk * 4 + h] = jnp.broadcast_to(ps, (CHUNK, CHUNK))
                    outs.append(_dot(probs.astype(MM), vd[kvh]))
                ycat[rows, YB_OFF + kvh * CHUNK:YB_OFF + (kvh + 1) * CHUNK] = jnp.where(lo, outs[0], outs[1])
        for g in range(2):
            q128 = qkv_ref[pl.ds(row0, tm), MQ_OFF + g * CHUNK:MQ_OFF + (g + 1) * CHUNK].astype(F32)
            k128 = mkv_ref[:, g * CHUNK:(g + 1) * CHUNK]
            v128 = mkv_ref[:, MEM_LEN + g * CHUNK:MEM_LEN + (g + 1) * CHUNK]
            outs = []
            for hh in range(2):
                qsel = jnp.where(lot if hh == 0 else ~lot, q128, 0.0).astype(MM)
                probs = _softmax(_dot_nt(qsel, k128) * SCALE)
                pc_s[2 * g + hh] = probs
                outs.append(_dot(probs.astype(MM), v128))
            ycat[:, YC_OFF + g * CHUNK:YC_OFF + (g + 1) * CHUNK] = jnp.where(lot, outs[0], outs[1])

        zt = z_ref[...]
        sig = 1.0 / (1.0 + jnp.exp(-zt))
        silu = zt * sig
        yc = ycat[...]
        yb = (yc * silu).astype(MM)
        o = _dot(yb, wo_ref[...])
        r2 = lax.rsqrt(jnp.mean(o * o, axis=-1, keepdims=True) + EPS)
        nrm = o * r2
        g2v = g2_ref[...]
        e = x_ref[...] + nrm * g2v - t_ref[...]
        l1 = jnp.sum(e * e, axis=-1, keepdims=True)
        loss_ref[...] += jnp.broadcast_to(jnp.sum(l1, axis=0, keepdims=True) * (0.5 / D_MODEL), loss_ref.shape)
        dxo = e * (1.0 / D_MODEL)
        dxo_ref[...] = dxo
        dg2_ref[...] += jnp.sum(dxo * nrm, axis=0, keepdims=True)
        dn = dxo * g2v
        do = r2 * (dn - nrm * jnp.mean(dn * nrm, axis=-1, keepdims=True))
        dob = do.astype(MM)
        dy = _dot_nt(dob, wo_ref[...])
        dz_ref[...] = (dy * yc * (sig * (1.0 + zt * (1.0 - sig)))).astype(MM)
        dyc[...] = dy * silu
        dwo_ref[...] += _dot_tn(yb, dob)

        for blk in range(bpt):
            r0 = blk * CHUNK
            rows = slice(r0, r0 + CHUNK)
            n = jt * bpt + blk
            for g in range(A_GROUPS):
                cg = slice(g * CHUNK, (g + 1) * CHUNK)
                cv = slice(A_WIDTH + g * CHUNK, A_WIDTH + (g + 1) * CHUNK)
                dya = dyc[rows, cg]
                duv_ref[rows, cg] = (dya * sv_s[rows, cg] * gu_s[rows, cg]).astype(MM)
                dsv = dya * u_s[rows, cg]
                dsvb = dsv.astype(MM)
                dsv_acc[g] += dsv
                dwsp_ref[g] += _dot_nt(dsvb, vc_s[rows, cg])
                dvc = _dot(wtt_ref[g], dsvb)
                xhat = xh_s[rows, cg]
                dvg_ref[:, cg] += jnp.sum(dvc * xhat, axis=0, keepdims=True)
                dvb_ref[:, cg] += jnp.sum(dvc, axis=0, keepdims=True)
                dxh = dvc * vg_ref[:, cg]
                dv = rs_s[rows, cg] * (dxh - jnp.mean(dxh, axis=-1, keepdims=True)
                                       - xhat * jnp.mean(dxh * xhat, axis=-1, keepdims=True))
                duv_ref[rows, cv] = (dv * gv_s[rows, cg]).astype(MM)
            _, cstart = _band_rows(n)
            dk_f, dv_f = [], []
            for kvh in range(2):
                kd = kd_s[blk * 2 + kvh]
                vd = vd_s[blk * 2 + kvh]
                q128 = qkv_ref[pl.ds(cstart, CHUNK), SQ_OFF + kvh * CHUNK:SQ_OFF + (kvh + 1) * CHUNK].astype(F32)
                do128 = dyc[rows, YB_OFF + kvh * CHUNK:YB_OFF + (kvh + 1) * CHUNK]
                dq128 = jnp.zeros((CHUNK, CHUNK), F32)
                dkd = jnp.zeros((2 * CHUNK, CHUNK), F32)
                dvd = jnp.zeros((2 * CHUNK, CHUNK), F32)
                for gi in range(2):
                    h = 2 * kvh + gi
                    half = lo if gi == 0 else ~lo
                    qsel = jnp.where(half, q128, 0.0).astype(MM)
                    dosel = jnp.where(half, do128, 0.0).astype(MM)
                    probs = pb_s[blk * 4 + h]
                    ps = ps_s[blk * 4 + h][:, 0:1]
                    dp = _dot_nt(dosel, vd)
                    delta = jnp.sum(probs * dp, axis=-1, keepdims=True)
                    ds = probs * (dp - delta)
                    dbias_acc[h] += ds
                    dsink_acc[h:h + 1, :] += jnp.broadcast_to(-jnp.sum(ps * delta, axis=0, keepdims=True), (1, CHUNK))
                    dss = (ds * SCALE).astype(MM)
                    dq128 = dq128 + jnp.where(half, _dot(dss, kd), 0.0)
                    dkd = dkd + _dot_tn(dss, qsel)
                    dvd = dvd + _dot_tn(probs.astype(MM), dosel)
                dqkv_ref[rows, SQ_OFF + kvh * CHUNK:SQ_OFF + (kvh + 1) * CHUNK] = dq128.astype(MM)
                dk_f.append(dkd + pltpu.roll(dkd, 64, 1))
                dv_f.append(dvd + pltpu.roll(dvd, 64, 1))
            dkv_acc[r0:r0 + 2 * CHUNK, 0:CHUNK] += jnp.where(lob, dk_f[0], dk_f[1])
            dkv_acc[r0:r0 + 2 * CHUNK, CHUNK:2 * CHUNK] += jnp.where(lob, dv_f[0], dv_f[1])
        dqkv_ref[:, SK_OFF:SK_OFF + 2 * CHUNK] = dkv_acc[CHUNK:CHUNK + tm, :].astype(MM)
        for g in range(2):
            q128 = qkv_ref[pl.ds(row0, tm), MQ_OFF + g * CHUNK:MQ_OFF + (g + 1) * CHUNK].astype(F32)
            k128 = mkv_ref[:, g * CHUNK:(g + 1) * CHUNK]
            v128 = mkv_ref[:, MEM_LEN + g * CHUNK:MEM_LEN + (g + 1) * CHUNK]
            do128 = dyc[:, YC_OFF + g * CHUNK:YC_OFF + (g + 1) * CHUNK]
            dq128 = jnp.zeros((tm, CHUNK), F32)
            dk128 = jnp.zeros((MEM_LEN, CHUNK), F32)
            dv128 = jnp.zeros((MEM_LEN, CHUNK), F32)
            for hh in range(2):
                half = lot if hh == 0 else ~lot
                qsel = jnp.where(half, q128, 0.0).astype(MM)
                dosel = jnp.where(half, do128, 0.0).astype(MM)
                probs = pc_s[2 * g + hh]
                dp = _dot_nt(dosel, v128)
                ds = probs * (dp - jnp.sum(probs * dp, axis=-1, keepdims=True))
                dss = (ds * SCALE).astype(MM)
                dq128 = dq128 + jnp.where(half, _dot(dss, k128), 0.0)
                dk128 = dk128 + _dot_tn(dss, qsel)
                dv128 = dv128 + _dot_tn(probs.astype(MM), dosel)
            dqkv_ref[:, MQ_OFF + g * CHUNK:MQ_OFF + (g + 1) * CHUNK] = dq128.astype(MM)
            dmkv_ref[:, g * CHUNK:(g + 1) * CHUNK] += dk128
            dmkv_ref[:, MEM_LEN + g * CHUNK:MEM_LEN + (g + 1) * CHUNK] += dv128

        @pl.when((b == nb - 1) & (j == nt - 1))
        def _():
            r = lax.broadcasted_iota(jnp.int32, (CHUNK, CHUNK), 0)
            c = lax.broadcasted_iota(jnp.int32, (CHUNK, CHUNK), 1)
            for g in range(A_GROUPS):
                dwsp_ref[g] = jnp.where(r >= c, dwsp_ref[g], 0.0)
                dbs_ref[g:g + 1, :] = jnp.sum(dsv_acc[g].T, axis=0, keepdims=True)
            rows8 = lax.broadcasted_iota(jnp.int32, (8, CHUNK), 0)
            cols8 = lax.broadcasted_iota(jnp.int32, (8, CHUNK), 1)
            sk = jnp.zeros((8, CHUNK), F32)
            for h in range(4):
                sk = sk + jnp.where((rows8 == 0) & (cols8 == h),
                                    jnp.broadcast_to(dsink_acc[h:h + 1, :], (8, CHUNK)), 0.0)
            dsink_ref[...] = sk
            bk = bk_ref[...]
            valid = _window_valid()
            rrow = lax.broadcasted_iota(jnp.int32, (N_BUCKETS, CHUNK), 0)
            rcol = lax.broadcasted_iota(jnp.int32, (N_BUCKETS, CHUNK), 1)
            acc = jnp.zeros((N_BUCKETS, CHUNK), F32)
            for bb in range(N_BUCKETS):
                hit = (bk == bb) & valid
                for h in range(4):
                    part = jnp.sum(jnp.where(hit, dbias_acc[h], 0.0), axis=-1, keepdims=True)
                    tot = jnp.sum(part, axis=0, keepdims=True)
                    acc = acc + jnp.where((rrow == bb) & (rcol == h), jnp.broadcast_to(tot, (N_BUCKETS, CHUNK)), 0.0)
            drel_ref[...] = acc

    t = nb * s
    tile = lambda w: pl.BlockSpec((tm, w), lambda b, j: (b * nt + nt - 1 - j, 0))
    per_batch = lambda r, w: pl.BlockSpec((None, r, w), lambda b, j: (b, 0, 0))
    grp = (A_GROUPS, CHUNK, CHUNK)
    return pl.pallas_call(
        body, name="mix", grid=(nb, nt),
        out_shape=(jax.ShapeDtypeStruct((t, UV_W), MM),
                   jax.ShapeDtypeStruct((t, QKV_W), MM),
                   jax.ShapeDtypeStruct((t, Z_W), MM),
                   jax.ShapeDtypeStruct((t, D_MODEL), F32),
                   jax.ShapeDtypeStruct((nb, MEM_LEN, 2 * MEM_LEN), F32),
                   jax.ShapeDtypeStruct((D_MODEL, D_MODEL), F32),
                   jax.ShapeDtypeStruct((1, D_MODEL), F32),
                   jax.ShapeDtypeStruct((8, CHUNK), F32),
                   jax.ShapeDtypeStruct(grp, F32),
                   jax.ShapeDtypeStruct((A_GROUPS, CHUNK), F32),
                   jax.ShapeDtypeStruct((1, A_WIDTH), F32),
                   jax.ShapeDtypeStruct((1, A_WIDTH), F32),
                   jax.ShapeDtypeStruct((8, CHUNK), F32),
                   jax.ShapeDtypeStruct((N_BUCKETS, CHUNK), F32)),
        in_specs=[tile(UV_W), tile(Z_W), per_batch(s, QKV_W), per_batch(MEM_LEN, 2 * MEM_LEN),
                  tile(D_MODEL), tile(D_MODEL),
                  _full((4, CHUNK, 2 * CHUNK)),
                  pl.BlockSpec(memory_space=pltpu.SMEM),
                  _full((1, A_WIDTH)), _full((1, A_WIDTH)),
                  _full(grp), _full(grp), _full(grp),
                  _full((1, D_MODEL)), _full((D_MODEL, D_MODEL)), _full((CHUNK, 2 * CHUNK))],
        out_specs=(tile(UV_W), tile(QKV_W), tile(Z_W), tile(D_MODEL), per_batch(MEM_LEN, 2 * MEM_LEN),
                   _full((D_MODEL, D_MODEL)), _full((1, D_MODEL)), _full((8, CHUNK)),
                   _full(grp), _full((A_GROUPS, CHUNK)), _full((1, A_WIDTH)), _full((1, A_WIDTH)),
                   _full((8, CHUNK)), _full((N_BUCKETS, CHUNK))),
        scratch_shapes=[pltpu.VMEM((tm, D_MODEL), F32), pltpu.VMEM((tm, D_MODEL), F32)]
                       + [pltpu.VMEM((tm, A_WIDTH), F32) for _ in range(6)]
                       + [pltpu.VMEM((tm, A_WIDTH), MM),
                          pltpu.VMEM((bpt * 4, CHUNK, 2 * CHUNK), F32),
                          pltpu.VMEM((bpt * 4, CHUNK, CHUNK), F32),
                          pltpu.VMEM((4, tm, MEM_LEN), F32),
                          pltpu.VMEM((bpt * 2, 2 * CHUNK, CHUNK), MM),
                          pltpu.VMEM((bpt * 2, 2 * CHUNK, CHUNK), MM),
                          pltpu.VMEM((tm + CHUNK, 2 * CHUNK), F32),
                          pltpu.VMEM((4, CHUNK, 2 * CHUNK), F32),
                          pltpu.VMEM(grp, F32),
                          pltpu.VMEM((8, CHUNK), F32)],
        compiler_params=_params(dimension_semantics=("arbitrary", "arbitrary")),
    )(uv, z, qkv3, mkv3, x2, tgt2, bias, sinks, vg, vb, wt, wtt, bcol, g2, w_o, buckets)


def _mix_fwd(uv, z, qkv3, mkv3, x2, tgt2, bias, sinks, vg, vb, wt, bcol, g2, w_o, tm):
    nb, s = qkv3.shape[0], qkv3.shape[1]
    nt = s // tm
    bpt = tm // CHUNK

    def body(uv_ref, z_ref, qkv_ref, mkv_ref, x_ref, t_ref, bias_ref, sink_ref, vg_ref, vb_ref, wt_ref,
             bcol_ref, g2_ref, wo_ref, dyc_ref, dz_ref, dxo_ref, dwo_ref, dg2_ref, loss_ref, ycat):
        b, j = pl.program_id(0), pl.program_id(1)

        @pl.when((b == 0) & (j == 0))
        def _():
            dwo_ref[...] = jnp.zeros_like(dwo_ref)
            dg2_ref[...] = jnp.zeros_like(dg2_ref)
            loss_ref[...] = jnp.zeros_like(loss_ref)

        lo = _half_masks(CHUNK)
        for blk in range(bpt):
            r0 = blk * CHUNK
            n = j * bpt + blk
            for g in range(A_GROUPS):
                u, _, _, _, _, _, _, sv = _spatial_group(uv_ref, r0, g, vg_ref, vb_ref, wt_ref, bcol_ref, False)
                ycat[r0:r0 + CHUNK, g * CHUNK:(g + 1) * CHUNK] = u * sv
            pstart, cstart = _band_rows(n)
            kb = jnp.concatenate([qkv_ref[pl.ds(pstart, CHUNK), 256:384], qkv_ref[pl.ds(cstart, CHUNK), 256:384]], axis=0)
            vbnd = jnp.concatenate([qkv_ref[pl.ds(pstart, CHUNK), 384:512], qkv_ref[pl.ds(cstart, CHUNK), 384:512]], axis=0)
            kd = _dup_heads(kb)
            vd = _dup_heads(vbnd)
            first_add = _first_block_mask(n)
            for kvh in range(2):
                q128 = qkv_ref[pl.ds(cstart, CHUNK), kvh * CHUNK:(kvh + 1) * CHUNK]
                outs = []
                for gi in range(2):
                    h = 2 * kvh + gi
                    qsel = jnp.where(lo if gi == 0 else ~lo, q128.astype(F32), 0.0).astype(MM)
                    probs, _ = _swa_probs(qsel, kd[kvh], bias_ref[h], sink_ref[h], first_add)
                    outs.append(_dot(probs.astype(MM), vd[kvh]))
                ycat[r0:r0 + CHUNK, A_WIDTH + kvh * CHUNK:A_WIDTH + (kvh + 1) * CHUNK] = jnp.where(lo, outs[0], outs[1])
        lot = _half_masks(tm)
        row0 = pl.multiple_of(j * tm, tm)
        for g in range(2):
            q128 = qkv_ref[pl.ds(row0, tm), 512 + g * CHUNK:512 + (g + 1) * CHUNK]
            k128 = mkv_ref[:, g * CHUNK:(g + 1) * CHUNK]
            v128 = mkv_ref[:, MEM_LEN + g * CHUNK:MEM_LEN + (g + 1) * CHUNK]
            outs = []
            for hh in range(2):
                qsel = jnp.where(lot if hh == 0 else ~lot, q128.astype(F32), 0.0).astype(MM)
                probs = _softmax(_dot_nt(qsel, k128) * SCALE)
                outs.append(_dot(probs.astype(MM), v128))
            ycat[:, 768 + g * CHUNK:768 + (g + 1) * CHUNK] = jnp.where(lot, outs[0], outs[1])
        zt = z_ref[...]
        sig = 1.0 / (1.0 + jnp.exp(-zt))
        silu = zt * sig
        yc = ycat[...]
        yb = (yc * silu).astype(MM)
        o = _dot(yb, wo_ref[...])
        r2 = lax.rsqrt(jnp.mean(o * o, axis=-1, keepdims=True) + EPS)
        nrm = o * r2
        g2v = g2_ref[...]
        e = x_ref[...] + nrm * g2v - t_ref[...]
        l1 = jnp.sum(e * e, axis=-1, keepdims=True)
        loss_ref[...] += jnp.broadcast_to(jnp.sum(l1, axis=0, keepdims=True) * (0.5 / D_MODEL), loss_ref.shape)
        dxo = e * (1.0 / D_MODEL)
        dxo_ref[...] = dxo
        dg2_ref[...] += jnp.sum(dxo * nrm, axis=0, keepdims=True)
        dn = dxo * g2v
        do = r2 * (dn - nrm * jnp.mean(dn * nrm, axis=-1, keepdims=True))
        dob = do.astype(MM)
        dy = _dot_nt(dob, wo_ref[...])
        dz_ref[...] = (dy * yc * (sig * (1.0 + zt * (1.0 - sig)))).astype(MM)
        dyc_ref[...] = dy * silu
        dwo_ref[...] += _dot_tn(yb, dob)

    t = nb * s
    tile = lambda w: pl.BlockSpec((tm, w), lambda b, j: (b * nt + j, 0))
    return pl.pallas_call(
        body, name="mix_fwd", grid=(nb, nt),
        out_shape=(jax.ShapeDtypeStruct((t, D_MODEL), F32),
                   jax.ShapeDtypeStruct((t, Z_W), MM),
                   jax.ShapeDtypeStruct((t, D_MODEL), F32),
                   jax.ShapeDtypeStruct((D_MODEL, D_MODEL), F32),
                   jax.ShapeDtypeStruct((1, D_MODEL), F32),
                   jax.ShapeDtypeStruct((8, CHUNK), F32)),
        in_specs=[tile(UV_W), tile(Z_W),
                  pl.BlockSpec((None, s, QKV_W), lambda b, j: (b, 0, 0)),
                  pl.BlockSpec((None, MEM_LEN, 2 * MEM_LEN), lambda b, j: (b, 0, 0)),
                  tile(D_MODEL), tile(D_MODEL),
                  _full((4, CHUNK, 2 * CHUNK)),
                  pl.BlockSpec(memory_space=pltpu.SMEM),
                  _full((1, A_WIDTH)), _full((1, A_WIDTH)),
                  _full((A_GROUPS, CHUNK, CHUNK)), _full((A_GROUPS, CHUNK, CHUNK)),
                  _full((1, D_MODEL)), _full((D_MODEL, D_MODEL))],
        out_specs=(tile(D_MODEL), tile(Z_W), tile(D_MODEL),
                   _full((D_MODEL, D_MODEL)), _full((1, D_MODEL)), _full((8, CHUNK))),
        scratch_shapes=[pltpu.VMEM((tm, D_MODEL), F32)],
        compiler_params=_params(dimension_semantics=("arbitrary", "arbitrary")),
    )(uv, z, qkv3, mkv3, x2, tgt2, bias, sinks, vg, vb, wt, bcol, g2, w_o)


def _mix_bwd(uv, dyc, qkv3, mkv3, bias, sinks, vg, vb, wt, wtt, bcol, buckets, tm):
    nb, s = qkv3.shape[0], qkv3.shape[1]
    nt = s // tm
    bpt = tm // CHUNK

    def body(uv_ref, dyc_ref, qkv_ref, mkv_ref, bias_ref, sink_ref, vg_ref, vb_ref, wt_ref, wtt_ref, bcol_ref,
             bk_ref, duv_ref, dqkv_ref, dmkv_ref, dwsp_ref, dbs_ref, dvg_ref, dvb_ref, dsink_ref, drel_ref,
             dkv_acc, dbias_acc, dsv_acc, dsink_acc):
        b, j = pl.program_id(0), pl.program_id(1)
        jt = nt - 1 - j

        @pl.when((b == 0) & (j == 0))
        def _():
            dwsp_ref[...] = jnp.zeros_like(dwsp_ref)
            dvg_ref[...] = jnp.zeros_like(dvg_ref)
            dvb_ref[...] = jnp.zeros_like(dvb_ref)
            dbias_acc[...] = jnp.zeros_like(dbias_acc)
            dsv_acc[...] = jnp.zeros_like(dsv_acc)
            dsink_acc[...] = jnp.zeros_like(dsink_acc)

        @pl.when(j == 0)
        def _():
            dmkv_ref[...] = jnp.zeros_like(dmkv_ref)
            dkv_acc[...] = jnp.zeros_like(dkv_acc)

        carry = dkv_acc[0:CHUNK, :]
        dkv_acc[...] = jnp.zeros_like(dkv_acc)
        dkv_acc[tm:tm + CHUNK, :] = carry

        lo = _half_masks(CHUNK)
        lob = _half_masks(2 * CHUNK)
        for blk in range(bpt):
            r0 = blk * CHUNK
            n = jt * bpt + blk
            for g in range(A_GROUPS):
                u, gu, gv, rstd, xhat, gam, vc, sv = _spatial_group(uv_ref, r0, g, vg_ref, vb_ref, wt_ref, bcol_ref, True)
                dya = dyc_ref[r0:r0 + CHUNK, g * CHUNK:(g + 1) * CHUNK]
                duv_ref[r0:r0 + CHUNK, g * CHUNK:(g + 1) * CHUNK] = (dya * sv * gu).astype(MM)
                dsv = dya * u
                dsvb = dsv.astype(MM)
                dsv_acc[g] += dsv
                dwsp_ref[g] += _dot_nt(dsvb, vc.astype(MM))
                dvc = _dot(wtt_ref[g], dsvb)
                dvg_ref[:, g * CHUNK:(g + 1) * CHUNK] += jnp.sum(dvc * xhat, axis=0, keepdims=True)
                dvb_ref[:, g * CHUNK:(g + 1) * CHUNK] += jnp.sum(dvc, axis=0, keepdims=True)
                dxh = dvc * gam
                dv = rstd * (dxh - jnp.mean(dxh, axis=-1, keepdims=True)
                             - xhat * jnp.mean(dxh * xhat, axis=-1, keepdims=True))
                duv_ref[r0:r0 + CHUNK, A_WIDTH + g * CHUNK:A_WIDTH + (g + 1) * CHUNK] = (dv * gv).astype(MM)
            pstart, cstart = _band_rows(n)
            kb = jnp.concatenate([qkv_ref[pl.ds(pstart, CHUNK), 256:384], qkv_ref[pl.ds(cstart, CHUNK), 256:384]], axis=0)
            vbnd = jnp.concatenate([qkv_ref[pl.ds(pstart, CHUNK), 384:512], qkv_ref[pl.ds(cstart, CHUNK), 384:512]], axis=0)
            kd = _dup_heads(kb)
            vd = _dup_heads(vbnd)
            first_add = _first_block_mask(n)
            dk_f, dv_f = [], []
            for kvh in range(2):
                q128 = qkv_ref[pl.ds(cstart, CHUNK), kvh * CHUNK:(kvh + 1) * CHUNK]
                do128 = dyc_ref[r0:r0 + CHUNK, A_WIDTH + kvh * CHUNK:A_WIDTH + (kvh + 1) * CHUNK]
                dq128 = jnp.zeros((CHUNK, CHUNK), F32)
                dkd = jnp.zeros((2 * CHUNK, CHUNK), F32)
                dvd = jnp.zeros((2 * CHUNK, CHUNK), F32)
                for gi in range(2):
                    h = 2 * kvh + gi
                    half = lo if gi == 0 else ~lo
                    qsel = jnp.where(half, q128.astype(F32), 0.0).astype(MM)
                    dosel = jnp.where(half, do128, 0.0).astype(MM)
                    probs, ps = _swa_probs(qsel, kd[kvh], bias_ref[h], sink_ref[h], first_add)
                    dp = _dot_nt(dosel, vd[kvh])
                    delta = jnp.sum(probs * dp, axis=-1, keepdims=True)
                    ds = probs * (dp - delta)
                    dbias_acc[h] += ds
                    dsink_acc[h:h + 1, :] += jnp.broadcast_to(
                        -jnp.sum(ps * delta, axis=0, keepdims=True), (1, CHUNK))
                    dss = (ds * SCALE).astype(MM)
                    dq128 = dq128 + jnp.where(half, _dot(dss, kd[kvh]), 0.0)
                    dkd = dkd + _dot_tn(dss, qsel)
                    dvd = dvd + _dot_tn(probs.astype(MM), dosel)
                dqkv_ref[r0:r0 + CHUNK, kvh * CHUNK:(kvh + 1) * CHUNK] = dq128.astype(MM)
                dk_f.append(dkd + pltpu.roll(dkd, 64, 1))
                dv_f.append(dvd + pltpu.roll(dvd, 64, 1))
            dkv_acc[r0:r0 + 2 * CHUNK, 0:CHUNK] += jnp.where(lob, dk_f[0], dk_f[1])
            dkv_acc[r0:r0 + 2 * CHUNK, CHUNK:2 * CHUNK] += jnp.where(lob, dv_f[0], dv_f[1])
        dqkv_ref[:, 256:512] = dkv_acc[CHUNK:CHUNK + tm, :].astype(MM)
        lot = _half_masks(tm)
        row0 = pl.multiple_of(jt * tm, tm)
        for g in range(2):
            q128 = qkv_ref[pl.ds(row0, tm), 512 + g * CHUNK:512 + (g + 1) * CHUNK]
            k128 = mkv_ref[:, g * CHUNK:(g + 1) * CHUNK]
            v128 = mkv_ref[:, MEM_LEN + g * CHUNK:MEM_LEN + (g + 1) * CHUNK]
            do128 = dyc_ref[:, 768 + g * CHUNK:768 + (g + 1) * CHUNK]
            dq128 = jnp.zeros((tm, CHUNK), F32)
            dk128 = jnp.zeros((MEM_LEN, CHUNK), F32)
            dv128 = jnp.zeros((MEM_LEN, CHUNK), F32)
            for hh in range(2):
                half = lot if hh == 0 else ~lot
                qsel = jnp.where(half, q128.astype(F32), 0.0).astype(MM)
                dosel = jnp.where(half, do128, 0.0).astype(MM)
                probs = _softmax(_dot_nt(qsel, k128) * SCALE)
                dp = _dot_nt(dosel, v128)
                ds = probs * (dp - jnp.sum(probs * dp, axis=-1, keepdims=True))
                dss = (ds * SCALE).astype(MM)
                dq128 = dq128 + jnp.where(half, _dot(dss, k128), 0.0)
                dk128 = dk128 + _dot_tn(dss, qsel)
                dv128 = dv128 + _dot_tn(probs.astype(MM), dosel)
            dqkv_ref[:, 512 + g * CHUNK:512 + (g + 1) * CHUNK] = dq128.astype(MM)
            dmkv_ref[:, g * CHUNK:(g + 1) * CHUNK] += dk128
            dmkv_ref[:, MEM_LEN + g * CHUNK:MEM_LEN + (g + 1) * CHUNK] += dv128

        @pl.when((b == nb - 1) & (j == nt - 1))
        def _():
            r = lax.broadcasted_iota(jnp.int32, (CHUNK, CHUNK), 0)
            c = lax.broadcasted_iota(jnp.int32, (CHUNK, CHUNK), 1)
            for g in range(A_GROUPS):
                dwsp_ref[g] = jnp.where(r >= c, dwsp_ref[g], 0.0)
                dbs_ref[g:g + 1, :] = jnp.sum(dsv_acc[g].T, axis=0, keepdims=True)
            rows = lax.broadcasted_iota(jnp.int32, (8, CHUNK), 0)
            cols = lax.broadcasted_iota(jnp.int32, (8, CHUNK), 1)
            sk = jnp.zeros((8, CHUNK), F32)
            for h in range(4):
                sk = sk + jnp.where((rows == 0) & (cols == h), jnp.broadcast_to(dsink_acc[h:h + 1, :], (8, CHUNK)), 0.0)
            dsink_ref[...] = sk
            bk = bk_ref[...]
            valid = _window_valid()
            rrow = lax.broadcasted_iota(jnp.int32, (N_BUCKETS, CHUNK), 0)
            rcol = lax.broadcasted_iota(jnp.int32, (N_BUCKETS, CHUNK), 1)
            acc = jnp.zeros((N_BUCKETS, CHUNK), F32)
            for bb in range(N_BUCKETS):
                hit = (bk == bb) & valid
                for h in range(4):
                    part = jnp.sum(jnp.where(hit, dbias_acc[h], 0.0), axis=-1, keepdims=True)
                    tot = jnp.sum(part, axis=0, keepdims=True)
                    acc = acc + jnp.where((rrow == bb) & (rcol == h), jnp.broadcast_to(tot, (N_BUCKETS, CHUNK)), 0.0)
            drel_ref[...] = acc

    t = nb * s
    tile = lambda w: pl.BlockSpec((tm, w), lambda b, j: (b * nt + nt - 1 - j, 0))
    return pl.pallas_call(
        body, name="mix_bwd", grid=(nb, nt),
        out_shape=(jax.ShapeDtypeStruct((t, UV_W), MM),
                   jax.ShapeDtypeStruct((t, QKV_W), MM),
                   jax.ShapeDtypeStruct((nb, MEM_LEN, 2 * MEM_LEN), F32),
                   jax.ShapeDtypeStruct((A_GROUPS, CHUNK, CHUNK), F32),
                   jax.ShapeDtypeStruct((A_GROUPS, CHUNK), F32),
                   jax.ShapeDtypeStruct((1, A_WIDTH), F32),
                   jax.ShapeDtypeStruct((1, A_WIDTH), F32),
                   jax.ShapeDtypeStruct((8, CHUNK), F32),
                   jax.ShapeDtypeStruct((N_BUCKETS, CHUNK), F32)),
        in_specs=[tile(UV_W), tile(D_MODEL),
                  pl.BlockSpec((None, s, QKV_W), lambda b, j: (b, 0, 0)),
                  pl.BlockSpec((None, MEM_LEN, 2 * MEM_LEN), lambda b, j: (b, 0, 0)),
                  _full((4, CHUNK, 2 * CHUNK)),
                  pl.BlockSpec(memory_space=pltpu.SMEM),
                  _full((1, A_WIDTH)), _full((1, A_WIDTH)),
                  _full((A_GROUPS, CHUNK, CHUNK)), _full((A_GROUPS, CHUNK, CHUNK)), _full((A_GROUPS, CHUNK, CHUNK)),
                  _full((CHUNK, 2 * CHUNK))],
        out_specs=(tile(UV_W), tile(QKV_W),
                   pl.BlockSpec((None, MEM_LEN, 2 * MEM_LEN), lambda b, j: (b, 0, 0)),
                   _full((A_GROUPS, CHUNK, CHUNK)), _full((A_GROUPS, CHUNK)),
                   _full((1, A_WIDTH)), _full((1, A_WIDTH)), _full((8, CHUNK)), _full((N_BUCKETS, CHUNK))),
        scratch_shapes=[pltpu.VMEM((tm + CHUNK, 2 * CHUNK), F32),
                        pltpu.VMEM((4, CHUNK, 2 * CHUNK), F32),
                        pltpu.VMEM((A_GROUPS, CHUNK, CHUNK), F32),
                        pltpu.VMEM((8, CHUNK), F32)],
        compiler_params=_params(dimension_semantics=("arbitrary", "arbitrary")),
    )(uv, dyc, qkv3, mkv3, bias, sinks, vg, vb, wt, wtt, bcol, buckets)


def _inproj_bwd(x2, dxo, duv, dqkv, dz, g1, w_in_t, tm):
    t = x2.shape[0]
    nt = t // tm

    def body(x_ref, dxo_ref, duv_ref, dqkv_ref, dz_ref, g_ref, w_ref, gx_ref, dw_hbm, dg_ref, acc, sem):
        i = pl.program_id(0)

        @pl.when(i == 0)
        def _():
            acc[...] = jnp.zeros_like(acc)
            dg_ref[...] = jnp.zeros_like(dg_ref)

        xf = x_ref[...]
        r = lax.rsqrt(jnp.mean(xf * xf, axis=-1, keepdims=True) + EPS)
        nx = xf * r
        gv = g_ref[...]
        h = (nx * gv).astype(MM)
        duv_t, dqkv_t, dz_t = duv_ref[...], dqkv_ref[...], dz_ref[...]
        acc[0:UV_W, :] += _dot_tn(duv_t, h)
        acc[UV_W:UV_W + QKV_W, :] += _dot_tn(dqkv_t, h)
        acc[UV_W + QKV_W:IN_WIDTH, :] += _dot_tn(dz_t, h)
        dh = (_dot(duv_t, w_ref[0:UV_W, :]) + _dot(dqkv_t, w_ref[UV_W:UV_W + QKV_W, :])
              + _dot(dz_t, w_ref[UV_W + QKV_W:IN_WIDTH, :]))
        dg_ref[...] += jnp.sum(dh * nx, axis=0, keepdims=True)
        dnx = dh * gv
        gx_ref[...] = dxo_ref[...] + r * (dnx - nx * jnp.mean(dnx * nx, axis=-1, keepdims=True))

        @pl.when(i == nt - 1)
        def _():
            cp = pltpu.make_async_copy(acc, dw_hbm, sem)
            cp.start()
            cp.wait()

    tile = lambda w: pl.BlockSpec((tm, w), lambda i: (i, 0))
    return pl.pallas_call(
        body, name="inproj_bwd", grid=(nt,),
        out_shape=(jax.ShapeDtypeStruct((t, D_MODEL), F32),
                   jax.ShapeDtypeStruct((IN_WIDTH, D_MODEL), F32),
                   jax.ShapeDtypeStruct((1, D_MODEL), F32)),
        in_specs=[tile(D_MODEL), tile(D_MODEL), tile(UV_W), tile(QKV_W), tile(Z_W),
                  _full((1, D_MODEL)),
                  pl.BlockSpec((IN_WIDTH, D_MODEL), lambda i: (0, 0), pipeline_mode=pl.Buffered(1))],
        out_specs=(tile(D_MODEL), pl.BlockSpec(memory_space=pl.ANY), _full((1, D_MODEL))),
        scratch_shapes=[pltpu.VMEM((IN_WIDTH, D_MODEL), F32), pltpu.SemaphoreType.DMA],
        compiler_params=_params(dimension_semantics=("arbitrary",)),
    )(x2, dxo, duv, dqkv, dz, g1, w_in_t)


def _adamw(w, g, m, v):
    m = ADAM_B1 * m + (1.0 - ADAM_B1) * g
    v = ADAM_B2 * v + (1.0 - ADAM_B2) * (g * g)
    m_hat = m / (1.0 - ADAM_B1 ** ADAM_STEP)
    v_hat = v / (1.0 - ADAM_B2 ** ADAM_STEP)
    delta = -ADAM_LR * (m_hat / (jnp.sqrt(v_hat) + ADAM_EPS) + ADAM_WD * w)
    return delta, m, v


_ROWS = 32


_S_LAYOUT = (((1, D_MODEL), 0), ((1, D_MODEL), 8), ((1, D_MODEL), 16),
             ((1, A_WIDTH), 24), ((1, A_WIDTH), 28), ((A_GROUPS, CHUNK), 32),
             ((1, 4), 36), ((N_BUCKETS, 4), 40),
             ((A_GROUPS * CHUNK, CHUNK), 72))
_LOSS_ROW = 37
_S_ROWS = 72 + A_GROUPS * CHUNK
_N_SMALL = len(_S_LAYOUT)


def _pack_rows(dst, refs):
    for (shp, r0), ref in zip(_S_LAYOUT, refs):
        if shp[0] == 1 and shp[1] >= CHUNK:
            for i in range(shp[1] // CHUNK):
                dst[r0 + i:r0 + i + 1, :] = ref[:, i * CHUNK:(i + 1) * CHUNK]
        elif ref.shape[-1] == CHUNK:
            dst[r0:r0 + shp[0], :] = ref[0:shp[0], :]
        else:
            dst[r0:r0 + shp[0], 0:shp[1]] = ref[...]


def _unpack_rows(src, refs):
    for (shp, r0), ref in zip(_S_LAYOUT, refs):
        if shp[0] == 1 and shp[1] >= CHUNK:
            for i in range(shp[1] // CHUNK):
                ref[:, i * CHUNK:(i + 1) * CHUNK] = src[r0 + i:r0 + i + 1, :]
        elif shp[1] == CHUNK:
            ref[...] = src[r0:r0 + shp[0], :]
        else:
            ref[...] = src[r0:r0 + shp[0], 0:shp[1]]


def _greduce(ga, gb, gc, small_g, loss_p):
    shapes = (ga.shape[1:], gb.shape[1:], gc.shape[1:])
    rs = _S_ROWS

    def body(*refs):
        it = iter(refs)
        take = lambda n: [next(it) for _ in range(n)]
        ga_ref, gb_ref, gc_ref = take(3)
        sg_refs = take(_N_SMALL)
        loss_ref, = take(1)
        oga, ogb, ogc, ogs = take(4)
        own_a, own_b, own_c, ra_a, ra_b, ra_c, sb_a, sb_b, sb_c, rb_a, rb_b, rb_c = take(12)
        gs_ref, rs_a, rs_b = take(3)
        ld_sem, sa_sem, ra_sem, sb_sem, rb_sem = take(5)

        gs_ref[...] = jnp.zeros_like(gs_ref)
        _pack_rows(gs_ref, sg_refs)
        gs_ref[_LOSS_ROW:_LOSS_ROW + 1, :] = loss_ref[0:1, :]

        x, y, cc = lax.axis_index("x"), lax.axis_index("y"), lax.axis_index("c")
        myq = 2 * x + y
        me = (x, y, cc)
        sib = (x, y, 1 - cc)
        chips = [(1 - x, y), (x, 1 - y), (1 - x, 1 - y)]
        gin = (ga_ref, gb_ref, gc_ref)
        own = (own_a, own_b, own_c)
        rcv_a = (ra_a, ra_b, ra_c)
        sbuf = (sb_a, sb_b, sb_c)
        rcv_b = (rb_a, rb_b, rb_c)

        def remote(src, dst, ssem, rsem, to):
            return pltpu.make_async_remote_copy(src_ref=src, dst_ref=dst, send_sem=ssem, recv_sem=rsem,
                                                device_id=to, device_id_type=MESH)

        loads, sends_a = [], []
        for arr in range(3):
            for q in range(4):
                loads.append(pltpu.make_async_copy(gin[arr].at[2 * q + cc], own[arr].at[q], ld_sem.at[arr, q]))
                sends_a.append(remote(gin[arr].at[2 * q + 1 - cc], rcv_a[arr].at[q],
                                      sa_sem.at[arr, q], ra_sem.at[arr, q], sib))
        small_a = remote(gs_ref, rs_a, sa_sem.at[3, 0], ra_sem.at[3, 0], sib)
        for cp in loads + sends_a + [small_a]:
            cp.start()
        for cp in loads:
            cp.wait()
        for arr in range(3):
            for q in range(4):
                remote(gin[arr].at[2 * q + 1 - cc], rcv_a[arr].at[q],
                       sa_sem.at[arr, q], ra_sem.at[arr, q], me).wait_recv()
        remote(gs_ref, rs_a, sa_sem.at[3, 0], ra_sem.at[3, 0], me).wait_recv()

        for arr in range(3):
            nrow = shapes[arr][0]

            def add_rows(i, _, arr=arr):
                r = pl.ds(pl.multiple_of(i * _ROWS, _ROWS), _ROWS)
                for q in range(4):
                    rcv_a[arr][q, r, :] = rcv_a[arr][q, r, :] + own[arr][q, r, :]
                return 0

            lax.fori_loop(0, nrow // _ROWS, add_rows, 0)
        rs_b[myq] = gs_ref[...] + rs_a[...]

        sends_b = []
        for j, chip in enumerate(chips):
            qj = 2 * chip[0] + chip[1]
            to = (chip[0], chip[1], cc)
            for arr in range(3):
                nrow = shapes[arr][0]

                def cast_rows(i, _, arr=arr, j=j, qj=qj):
                    r = pl.ds(pl.multiple_of(i * _ROWS, _ROWS), _ROWS)
                    sbuf[arr][j, r, :] = rcv_a[arr][qj, r, :].astype(BF16)
                    return 0

                lax.fori_loop(0, nrow // _ROWS, cast_rows, 0)
                cp = remote(sbuf[arr].at[j], rcv_b[arr].at[j], sb_sem.at[arr, j], rb_sem.at[arr, j], to)
                cp.start()
                sends_b.append(cp)
            cp = remote(rs_b.at[myq], rs_b.at[myq], sb_sem.at[3, j], rb_sem.at[3, j], to)
            cp.start()
            sends_b.append(cp)
        for j in range(3):
            for arr in range(3):
                remote(sbuf[arr].at[j], rcv_b[arr].at[j], sb_sem.at[arr, j], rb_sem.at[arr, j], me).wait_recv()
            remote(rs_b.at[myq], rs_b.at[myq], sb_sem.at[3, j], rb_sem.at[3, j], me).wait_recv()

        for arr, og in enumerate((oga, ogb, ogc)):
            nrow = shapes[arr][0]

            def tot(i, _, arr=arr, og=og):
                r = pl.ds(pl.multiple_of(i * _ROWS, _ROWS), _ROWS)
                g = rcv_a[arr][myq, r, :]
                for j in range(3):
                    g = g + rcv_b[arr][j, r, :].astype(F32)
                og[r, :] = g
                return 0

            lax.fori_loop(0, nrow // _ROWS, tot, 0)

        def tot_s(i, _):
            r = pl.ds(pl.multiple_of(i * 8, 8), 8)
            ogs[r, :] = ((rs_b[0, r, :] + rs_b[1, r, :]) + rs_b[2, r, :]) + rs_b[3, r, :]
            return 0

        lax.fori_loop(0, rs // 8, tot_s, 0)

        for cp in sends_a + [small_a] + sends_b:
            cp.wait_send()

    vm = pl.BlockSpec(memory_space=pltpu.VMEM)
    anyspec = pl.BlockSpec(memory_space=pl.ANY)
    out_shape = tuple([jax.ShapeDtypeStruct(shp, F32) for shp in shapes] + [jax.ShapeDtypeStruct((rs, CHUNK), F32)])
    scratch = ([pltpu.VMEM((4,) + shp, F32) for shp in shapes]
               + [pltpu.VMEM((4,) + shp, F32) for shp in shapes]
               + [pltpu.VMEM((3,) + shp, BF16) for shp in shapes]
               + [pltpu.VMEM((3,) + shp, BF16) for shp in shapes]
               + [pltpu.VMEM((rs, CHUNK), F32), pltpu.VMEM((rs, CHUNK), F32), pltpu.VMEM((4, rs, CHUNK), F32)]
               + [pltpu.SemaphoreType.DMA((3, 4)), pltpu.SemaphoreType.DMA((4, 4)), pltpu.SemaphoreType.DMA((4, 4)),
                  pltpu.SemaphoreType.DMA((4, 3)), pltpu.SemaphoreType.DMA((4, 3))])
    return pl.pallas_call(
        body, name="greduce",
        out_shape=out_shape,
        in_specs=[anyspec, anyspec, anyspec] + [vm] * (_N_SMALL + 1),
        out_specs=tuple([vm] * len(out_shape)),
        scratch_shapes=scratch,
        compiler_params=_params(),
    )(ga, gb, gc, *small_g, loss_p)


def _update(ta, tb, tc, ts, big_wmv, small_wmv):
    shapes = (ta.shape, tb.shape, tc.shape)
    rs = _S_ROWS
    small_shapes = [tuple(a.shape) for a in small_wmv[0]]

    def body(*refs):
        it = iter(refs)
        take = lambda n: [next(it) for _ in range(n)]
        ga_ref, gb_ref, gc_ref, gs_ref = take(4)
        wa, ma, va, wb, mb, vb_, wc, mc, vc = take(9)
        sw_refs, sm_refs, sv_refs = take(_N_SMALL), take(_N_SMALL), take(_N_SMALL)
        oga, oda, oma, ova, ogb, odb, omb, ovb, ogc, odc, omc, ovc = take(12)
        so_refs = [take(_N_SMALL) for _ in range(4)]
        loss_out, = take(1)
        ws, ms, vs, ods, oms, ovs = take(6)

        for buf in (ws, ms, vs):
            buf[...] = jnp.zeros_like(buf)
        _pack_rows(ws, sw_refs)
        _pack_rows(ms, sm_refs)
        _pack_rows(vs, sv_refs)

        big = ((ga_ref, wa, ma, va, oga, oda, oma, ova), (gb_ref, wb, mb, vb_, ogb, odb, omb, ovb),
               (gc_ref, wc, mc, vc, ogc, odc, omc, ovc))
        for arr in range(3):
            g_r, w_r, m_r, v_r, og, od, om, ov = big[arr]
            nrow = shapes[arr][0]

            def upd(i, _, g_r=g_r, w_r=w_r, m_r=m_r, v_r=v_r, og=og, od=od, om=om, ov=ov):
                r = pl.ds(pl.multiple_of(i * _ROWS, _ROWS), _ROWS)
                g = g_r[r, :]
                d, m, v = _adamw(w_r[r, :], g, m_r[r, :], v_r[r, :])
                og[r, :] = g
                od[r, :] = d
                om[r, :] = m
                ov[r, :] = v
                return 0

            lax.fori_loop(0, nrow // _ROWS, upd, 0)

        def upd_s(i, _):
            r = pl.ds(pl.multiple_of(i * 8, 8), 8)
            d, m, v = _adamw(ws[r, :], gs_ref[r, :], ms[r, :], vs[r, :])
            ods[r, :] = d
            oms[r, :] = m
            ovs[r, :] = v
            return 0

        lax.fori_loop(0, rs // 8, upd_s, 0)
        for k, buf in enumerate((gs_ref, ods, oms, ovs)):
            _unpack_rows(buf, so_refs[k])
        loss_out[...] = gs_ref[_LOSS_ROW:_LOSS_ROW + 1, 0:1]

    vm = pl.BlockSpec(memory_space=pltpu.VMEM)
    big_out = []
    for shp in shapes:
        big_out += [jax.ShapeDtypeStruct(shp, F32)] * 4
    small_out = [jax.ShapeDtypeStruct(shp, F32) for shp in small_shapes] * 4
    out_shape = tuple(big_out + small_out + [jax.ShapeDtypeStruct((1, 1), F32)])
    n_in = 4 + 9 + 3 * _N_SMALL
    return pl.pallas_call(
        body, name="update",
        out_shape=out_shape,
        in_specs=[vm] * n_in,
        out_specs=tuple([vm] * len(out_shape)),
        scratch_shapes=[pltpu.VMEM((rs, CHUNK), F32) for _ in range(6)],
        compiler_params=_params(),
    )(ta, tb, tc, ts, *big_wmv, *small_wmv[0], *small_wmv[1], *small_wmv[2])


def _local_step(x, mem, loss_target, pre_norm_g, post_norm_g, mem_norm_g, v_norm_g, v_norm_b, w_spatial, b_spatial,
                attn_sinks, rel_bias, w_in_t, w_o, w_mkv):
    nb, s, _ = x.shape
    t = nb * s
    x2 = x.reshape(t, D_MODEL)
    tgt2 = loss_target.reshape(t, D_MODEL)
    mem2 = mem.reshape(nb * MEM_LEN, D_MODEL)
    tm_mix = min(256, s)
    tm_proj = min(512, t)

    buckets = jnp.asarray(_t5_buckets())
    sinks = attn_sinks.reshape(4)
    bias, wt, wtt, bcol = _prep(rel_bias, w_spatial[0], b_spatial[0], buckets)

    uv, qkv, z = _inproj_fwd(x2, pre_norm_g, w_in_t, tm_proj)
    mkv = _memkv_fwd(mem2, mem_norm_g, w_mkv)
    qkv3 = qkv.reshape(nb, s, QKV_W)
    mkv3 = mkv.reshape(nb, MEM_LEN, 2 * MEM_LEN)
    duv, dqkv, dz, dxo, dmkv, dwo, dg2, loss_p, dwsp, dbs, dvg, dvb, dsink, drel = _mix(
        uv, z, qkv3, mkv3, x2, tgt2, bias, sinks, v_norm_g, v_norm_b, wt, wtt, bcol, post_norm_g, w_o, buckets, tm_mix)
    dwmkv, dgm = _memkv_bwd(dmkv.reshape(nb * MEM_LEN, 2 * MEM_LEN), mem2, mem_norm_g, w_mkv)
    gx, dw_in_t, dg1 = _inproj_bwd(x2, dxo, duv, dqkv, dz, pre_norm_g, w_in_t, tm_proj)
    small = [dg1, dg2, dgm, dvg, dvb, dbs, dsink, drel, dwsp.reshape(A_GROUPS * CHUNK, CHUNK)]
    return loss_p, gx.reshape(nb, s, D_MODEL), dw_in_t, dwo, dwmkv, small


def kernel(x, mem, pre_norm_g, post_norm_g, mem_norm_g, w_in, w_mem_kv, v_norm_g, v_norm_b, w_spatial, b_spatial, attn_sinks, rel_bias, w_out, loss_target, m_pre_norm_g, m_post_norm_g, m_mem_norm_g, m_w_in, m_w_mem_kv, m_v_norm_g, m_v_norm_b, m_w_spatial, m_b_spatial, m_attn_sinks, m_rel_bias, m_w_out, v_pre_norm_g, v_post_norm_g, v_mem_norm_g, v_w_in, v_w_mem_kv, v_v_norm_g, v_v_norm_b, v_w_spatial, v_b_spatial, v_attn_sinks, v_rel_bias, v_w_out):
    nb, s, _ = x.shape

    sh_a = (w_in[0].T, m_w_in[0].T, v_w_in[0].T)
    sh_b = (w_out[0], m_w_out[0], v_w_out[0])
    sh_c = (w_mem_kv[0], m_w_mem_kv[0], v_w_mem_kv[0])
    wa, wb, wc = _wgather(sh_a[0], sh_b[0], sh_c[0])
    w_in_t = wa.reshape(IN_WIDTH, D_MODEL)
    w_o = wb.reshape(D_MODEL, D_MODEL)
    w_mkv = wc.reshape(D_MODEL, 2 * MEM_LEN)

    loss_p, gx, dw_in_t, dwo, dwmkv, small_grads = _local_step(
        x, mem, loss_target, pre_norm_g, post_norm_g, mem_norm_g, v_norm_g, v_norm_b, w_spatial, b_spatial,
        attn_sinks, rel_bias, w_in_t, w_o, w_mkv)

    small_names = ["pre_norm_g", "post_norm_g", "mem_norm_g", "v_norm_g", "v_norm_b", "b_spatial", "attn_sinks",
                   "rel_bias", "w_spatial"]
    given = dict(pre_norm_g=(pre_norm_g, m_pre_norm_g, v_pre_norm_g), post_norm_g=(post_norm_g, m_post_norm_g, v_post_norm_g),
                 mem_norm_g=(mem_norm_g, m_mem_norm_g, v_mem_norm_g), v_norm_g=(v_norm_g, m_v_norm_g, v_v_norm_g),
                 v_norm_b=(v_norm_b, m_v_norm_b, v_v_norm_b), b_spatial=(b_spatial, m_b_spatial, v_b_spatial),
                 attn_sinks=(attn_sinks, m_attn_sinks, v_attn_sinks), rel_bias=(rel_bias, m_rel_bias, v_rel_bias),
                 w_spatial=(w_spatial, m_w_spatial, v_w_spatial))
    small_wmv = [[given[n][k].reshape(shp) for n, (shp, _) in zip(small_names, _S_LAYOUT)] for k in range(3)]

    ta, tb, tc, ts = _greduce(dw_in_t.reshape(N_DEV, SHARD_IN, D_MODEL), dwo.reshape(N_DEV, SHARD_O, D_MODEL),
                              dwmkv.reshape(N_DEV, SHARD_O, 2 * MEM_LEN), small_grads, loss_p)
    outs = _update(ta, tb, tc, ts, (*sh_a, *sh_b, *sh_c), small_wmv)
    ra, rb, rc = outs[0:4], outs[4:8], outs[8:12]
    loss = outs[12 + 4 * _N_SMALL].reshape(())

    res = {}
    for k, kind in enumerate(("grad", "delta", "new_m", "new_v")):
        res[kind, "w_in"] = ra[k].T[None]
        res[kind, "w_out"] = rb[k][None]
        res[kind, "w_mem_kv"] = rc[k][None]
        for i, n in enumerate(small_names):
            res[kind, n] = outs[12 + k * _N_SMALL + i].reshape(given[n][0].shape)
    order = ["pre_norm_g", "post_norm_g", "mem_norm_g", "w_in", "w_mem_kv", "v_norm_g", "v_norm_b", "w_spatial",
             "b_spatial", "attn_sinks", "rel_bias", "w_out"]
    flat = [res[kind, n] for kind in ("grad", "delta", "new_m", "new_v") for n in order]
    return (loss, gx.reshape(nb, s, D_MODEL), *flat)
```

```python
import numpy as np
import jax
import jax.numpy as jnp
from jax import lax
from jax.experimental import pallas as pl
from jax.experimental.pallas import tpu as pltpu

F32 = jnp.float32
BF16 = jnp.bfloat16
MM = jnp.bfloat16

D_MODEL = 1024
CHUNK = 128
A_GROUPS = 4
A_WIDTH = 512
UV_W = 1024
QKV_W = 768
Z_W = 1024
IN_WIDTH = UV_W + QKV_W + Z_W
MEM_LEN = 256
N_BUCKETS = 32
MAX_DISTANCE = 128
EPS = 1e-6
NEG = -1e30
SCALE = 0.125
N_DEV = 8
SHARD_IN = IN_WIDTH // N_DEV
SHARD_O = D_MODEL // N_DEV
HALF_IN = IN_WIDTH // 2
QK_W = HALF_IN - UV_W

SQ_OFF, SK_OFF, SV_OFF, MQ_OFF = 0, 256, 384, 512
YB_OFF, YC_OFF = 512, 768

ADAM_LR = 0.001
ADAM_B1 = 0.9
ADAM_B2 = 0.999
ADAM_EPS = 1e-08
ADAM_WD = 0.01
ADAM_STEP = 10

VMEM_LIMIT = 58 * 1024 * 1024

_GELU_C = 0.7978845608028654
_GELU_A = 0.044715

MESH = pl.DeviceIdType.MESH
_ROWS = 32


def _dot(a, b):
    return lax.dot_general(a, b, (((1,), (0,)), ((), ())), preferred_element_type=F32)


def _dot_nt(a, b):
    return lax.dot_general(a, b, (((1,), (1,)), ((), ())), preferred_element_type=F32)


def _dot_tn(a, b):
    return lax.dot_general(a, b, (((0,), (0,)), ((), ())), preferred_element_type=F32)


def _gelu_and_grad(x):
    x2 = x * x
    t = jnp.tanh(_GELU_C * (x + _GELU_A * x * x2))
    g = 0.5 * x * (1.0 + t)
    dg = 0.5 * (1.0 + t) + 0.5 * x * (1.0 - t * t) * (_GELU_C * (1.0 + 3.0 * _GELU_A * x2))
    return g, dg


def _t5_buckets():
    qi = np.arange(CHUNK)[:, None]
    kj = np.arange(2 * CHUNK)[None, :]
    n = np.maximum(qi + CHUNK - kj, 0)
    max_exact = N_BUCKETS // 2
    large = max_exact + (np.log(np.maximum(n, 1) / max_exact) / np.log(MAX_DISTANCE / max_exact)
                         * (N_BUCKETS - max_exact)).astype(np.int32)
    large = np.minimum(large, N_BUCKETS - 1)
    return np.where(n < max_exact, n, large).astype(np.int32)


def _params(**kw):
    return pltpu.CompilerParams(vmem_limit_bytes=VMEM_LIMIT, **kw)


def _full(shape):
    nd = len(shape)
    return pl.BlockSpec(shape, lambda *_: (0,) * nd)


def _window_valid():
    qi = lax.broadcasted_iota(jnp.int32, (CHUNK, 2 * CHUNK), 0)
    kj = lax.broadcasted_iota(jnp.int32, (CHUNK, 2 * CHUNK), 1)
    dist = qi + CHUNK - kj
    return (dist >= 0) & (dist < CHUNK)


def _position():
    return lax.axis_index("x"), lax.axis_index("y"), lax.axis_index("c")


def _other_chips(x, y):
    return [(1 - x, y), (x, 1 - y), (1 - x, 1 - y)]


def _remote(src, dst, ssem, rsem, to):
    return pltpu.make_async_remote_copy(src_ref=src, dst_ref=dst, send_sem=ssem, recv_sem=rsem,
                                        device_id=to, device_id_type=MESH)


def _rows_loop(nrow, fn):
    def step(i, _):
        fn(pl.ds(pl.multiple_of(i * _ROWS, _ROWS), _ROWS))
        return 0

    lax.fori_loop(0, nrow // _ROWS, step, 0)


class _Gather:
    def __init__(self, pos, stage, out, ssem, rsem):
        self.x, self.y, self.c = pos
        self.stage, self.out, self.ssem, self.rsem = stage, out, ssem, rsem
        self.me = 4 * self.x + 2 * self.y + self.c
        self.sib = (self.x, self.y, 1 - self.c)
        self.chips = _other_chips(self.x, self.y)

    def _idx(self, chip, core):
        return 4 * chip[0] + 2 * chip[1] + core

    def _own(self, k, to):
        return _remote(self.stage, self.out.at[self.me], self.ssem.at[k], self.rsem.at[k], to)

    def _passed(self, j, core, to):
        blk = self.out.at[self._idx(self.chips[j], core)]
        return _remote(blk, blk, self.ssem.at[4 + j], self.rsem.at[4 + j], to)

    def start(self):
        self._own(0, self.sib).start()
        for j, chip in enumerate(self.chips):
            self._own(1 + j, (chip[0], chip[1], self.c)).start()

    def forward(self):
        here = (self.x, self.y, self.c)
        for j, chip in enumerate(self.chips):
            blk = self.out.at[self._idx(chip, self.c)]
            _remote(self.stage, blk, self.ssem.at[1 + j], self.rsem.at[1 + j], here).wait_recv()
            self._passed(j, self.c, self.sib).start()

    def finish(self):
        here = (self.x, self.y, self.c)
        blk = self.out.at[self._idx((self.x, self.y), 1 - self.c)]
        _remote(self.stage, blk, self.ssem.at[0], self.rsem.at[0], here).wait_recv()
        for j in range(3):
            self._passed(j, 1 - self.c, here).wait_recv()
        self._own(0, self.sib).wait_send()
        for j, chip in enumerate(self.chips):
            self._own(1 + j, (chip[0], chip[1], self.c)).wait_send()
            self._passed(j, self.c, self.sib).wait_send()


def _wgather(a):
    def body(a_ref, oa, ssem, rsem):
        pos = _position()
        me = 4 * pos[0] + 2 * pos[1] + pos[2]
        oa[me] = a_ref[...].astype(BF16)
        g = _Gather(pos, oa.at[me], oa, ssem, rsem)
        g.start()
        g.forward()
        g.finish()

    vm = pl.BlockSpec(memory_space=pltpu.VMEM)
    return pl.pallas_call(
        body, name="wgather",
        out_shape=jax.ShapeDtypeStruct((N_DEV,) + a.shape, BF16),
        in_specs=[vm], out_specs=vm,
        scratch_shapes=[pltpu.SemaphoreType.DMA((7,)), pltpu.SemaphoreType.DMA((7,))],
        compiler_params=_params(),
    )(a)


def _prep(rel_bias, w_sp, b_sp, buckets):
    def body(rb_ref, w_ref, b_ref, bk_ref, bias_ref, wt_ref, wtt_ref, bcol_ref):
        valid = _window_valid()
        bk = bk_ref[...]
        acc = [jnp.full((CHUNK, 2 * CHUNK), NEG, F32) for _ in range(4)]
        for b in range(N_BUCKETS):
            hit = (bk == b) & valid
            for h in range(4):
                acc[h] = jnp.where(hit, rb_ref[b, h], acc[h])
        for h in range(4):
            bias_ref[h] = acc[h]
        r = lax.broadcasted_iota(jnp.int32, (CHUNK, CHUNK), 0)
        c = lax.broadcasted_iota(jnp.int32, (CHUNK, CHUNK), 1)
        for g in range(A_GROUPS):
            w = jnp.where(r >= c, w_ref[g], 0.0)
            wt_ref[g] = w.astype(MM)
            wtt_ref[g] = w.T.astype(MM)
            bcol_ref[g] = jnp.broadcast_to(b_ref[g:g + 1, :], (CHUNK, CHUNK)).T

    return pl.pallas_call(
        body, name="prep",
        out_shape=(jax.ShapeDtypeStruct((4, CHUNK, 2 * CHUNK), F32),
                   jax.ShapeDtypeStruct((A_GROUPS, CHUNK, CHUNK), MM),
                   jax.ShapeDtypeStruct((A_GROUPS, CHUNK, CHUNK), MM),
                   jax.ShapeDtypeStruct((A_GROUPS, CHUNK, CHUNK), F32)),
        in_specs=[pl.BlockSpec(memory_space=pltpu.SMEM), pl.BlockSpec(memory_space=pltpu.VMEM),
                  pl.BlockSpec(memory_space=pltpu.VMEM), pl.BlockSpec(memory_space=pltpu.VMEM)],
        out_specs=tuple(pl.BlockSpec(memory_space=pltpu.VMEM) for _ in range(4)),
    )(rel_bias, w_sp, b_sp, buckets)


def _inproj_fwd(x2, g1, w_in_t, wo_sh, wm_sh, tm):
    t = x2.shape[0]
    nt = t // tm
    fwd_step = min(3, nt - 1)

    def body(x_ref, g_ref, w_ref, wo_ref, wm_ref, uv_ref, qkv_ref, z_ref, oo_ref, om_ref,
             st_o, st_m, ssem_o, rsem_o, ssem_m, rsem_m, lsem):
        i = pl.program_id(0)
        pos = _position()
        me = 4 * pos[0] + 2 * pos[1] + pos[2]
        gathers = (_Gather(pos, st_o, oo_ref, ssem_o, rsem_o), _Gather(pos, st_m, om_ref, ssem_m, rsem_m))
        local = (pltpu.make_async_copy(st_o, oo_ref.at[me], lsem.at[0]),
                 pltpu.make_async_copy(st_m, om_ref.at[me], lsem.at[1]))

        @pl.when(i == 0)
        def _():
            st_o[...] = wo_ref[...].astype(BF16)
            st_m[...] = wm_ref[...].astype(BF16)
            for cp in local:
                cp.start()
            for g in gathers:
                g.start()

        xf = x_ref[...]
        r = lax.rsqrt(jnp.mean(xf * xf, axis=-1, keepdims=True) + EPS)
        h = (xf * r * g_ref[...]).astype(MM)
        uv_ref[...] = _dot_nt(h, w_ref[0:UV_W, :])
        qkv_ref[...] = _dot_nt(h, w_ref[UV_W:UV_W + QKV_W, :]).astype(MM)
        z_ref[...] = _dot_nt(h, w_ref[UV_W + QKV_W:IN_WIDTH, :])

        @pl.when(i == fwd_step)
        def _():
            for g in gathers:
                g.forward()

        @pl.when(i == nt - 1)
        def _():
            for g in gathers:
                g.finish()
            for cp in local:
                cp.wait()

    anyspec = pl.BlockSpec(memory_space=pl.ANY)
    return pl.pallas_call(
        body, name="inproj_fwd", grid=(nt,),
        out_shape=(jax.ShapeDtypeStruct((t, UV_W), F32),
                   jax.ShapeDtypeStruct((t, QKV_W), MM),
                   jax.ShapeDtypeStruct((t, Z_W), F32),
                   jax.ShapeDtypeStruct((N_DEV,) + wo_sh.shape, BF16),
                   jax.ShapeDtypeStruct((N_DEV,) + wm_sh.shape, BF16)),
        in_specs=[pl.BlockSpec((tm, D_MODEL), lambda i: (i, 0)),
                  _full((1, D_MODEL)),
                  pl.BlockSpec((IN_WIDTH, D_MODEL), lambda i: (0, 0), pipeline_mode=pl.Buffered(1)),
                  _full(wo_sh.shape), _full(wm_sh.shape)],
        out_specs=(pl.BlockSpec((tm, UV_W), lambda i: (i, 0)),
                   pl.BlockSpec((tm, QKV_W), lambda i: (i, 0)),
                   pl.BlockSpec((tm, Z_W), lambda i: (i, 0)),
                   anyspec, anyspec),
        scratch_shapes=[pltpu.VMEM(wo_sh.shape, BF16), pltpu.VMEM(wm_sh.shape, BF16),
                        pltpu.SemaphoreType.DMA((7,)), pltpu.SemaphoreType.DMA((7,)),
                        pltpu.SemaphoreType.DMA((7,)), pltpu.SemaphoreType.DMA((7,)),
                        pltpu.SemaphoreType.DMA((2,))],
        compiler_params=_params(dimension_semantics=("arbitrary",)),
    )(x2, g1, w_in_t, wo_sh, wm_sh)


def _memkv_fwd(mem2, gm, w_mkv):
    tmem = mem2.shape[0]

    def body(m_ref, g_ref, w_ref, o_ref):
        xf = m_ref[...]
        r = lax.rsqrt(jnp.mean(xf * xf, axis=-1, keepdims=True) + EPS)
        hm = (xf * r * g_ref[...]).astype(MM)
        o_ref[...] = _dot(hm, w_ref[...]).astype(MM)

    vm = pl.BlockSpec(memory_space=pltpu.VMEM)
    return pl.pallas_call(
        body, name="memkv_fwd",
        out_shape=jax.ShapeDtypeStruct((tmem, 2 * MEM_LEN), MM),
        in_specs=[vm, vm, vm], out_specs=vm,
        compiler_params=_params(),
    )(mem2, gm, w_mkv)


def _memkv_bwd(dmkv, mem2, gm, w_mkv):
    def body(d_ref, m_ref, g_ref, w_ref, dw_ref, dg_ref):
        xf = m_ref[...]
        r = lax.rsqrt(jnp.mean(xf * xf, axis=-1, keepdims=True) + EPS)
        nm = xf * r
        hm = (nm * g_ref[...]).astype(MM)
        d = d_ref[...].astype(MM)
        dw_ref[...] = _dot_tn(hm, d)
        dhm = _dot_nt(d, w_ref[...])
        dg_ref[...] = jnp.sum(dhm * nm, axis=0, keepdims=True)

    vm = pl.BlockSpec(memory_space=pltpu.VMEM)
    return pl.pallas_call(
        body, name="memkv_bwd",
        out_shape=(jax.ShapeDtypeStruct((D_MODEL, 2 * MEM_LEN), F32),
                   jax.ShapeDtypeStruct((1, D_MODEL), F32)),
        in_specs=[vm, vm, vm, vm], out_specs=(vm, vm),
        compiler_params=_params(),
    )(dmkv, mem2, gm, w_mkv)


def _half_masks(rows):
    lane = lax.broadcasted_iota(jnp.int32, (rows, CHUNK), 1)
    return lane < 64


def _dup_heads(band):
    b32 = band.astype(F32)
    rolled = pltpu.roll(b32, 64, 1)
    lo = _half_masks(band.shape[0])
    return (jnp.where(lo, b32, rolled).astype(MM), jnp.where(lo, rolled, b32).astype(MM))


def _swa_probs(qsel, kd, bias_h, sink_h, first_add):
    s = _dot_nt(qsel, kd) * SCALE + bias_h + first_add
    m = jnp.maximum(jnp.max(s, axis=-1, keepdims=True), sink_h)
    p = jnp.exp(s - m)
    es = jnp.exp(sink_h - m)
    inv = 1.0 / (jnp.sum(p, axis=-1, keepdims=True) + es)
    return p * inv, es * inv


def _softmax(s):
    m = jnp.max(s, axis=-1, keepdims=True)
    p = jnp.exp(s - m)
    return p * (1.0 / jnp.sum(p, axis=-1, keepdims=True))


def _band_rows(n):
    cstart = pl.multiple_of(n * CHUNK, CHUNK)
    pstart = pl.multiple_of(jnp.maximum(n - 1, 0) * CHUNK, CHUNK)
    return pstart, cstart


def _first_block_mask(n):
    col = lax.broadcasted_iota(jnp.int32, (CHUNK, 2 * CHUNK), 1)
    return jnp.where((col < CHUNK) & (n == 0), NEG, 0.0)


def _mix(uv, z, qkv3, mkv3, x2, tgt2, bias, sinks, vg, vb, wt, wtt, bcol, g2, w_o, buckets, tm):
    nb, s = qkv3.shape[0], qkv3.shape[1]
    nt = s // tm
    bpt = tm // CHUNK

    def body(uv_ref, z_ref, qkv_ref, mkv_ref, x_ref, t_ref, bias_ref, sink_ref, vg_ref, vb_ref, wt_ref, wtt_ref,
             bcol_ref, g2_ref, wo_ref, bk_ref,
             duv_ref, dqkv_ref, dz_ref, dxo_ref, dmkv_ref, dwo_ref, dg2_ref, loss_ref, dwsp_ref, dbs_ref,
             dvg_ref, dvb_ref, dsink_ref, drel_ref,
             ycat, dyc, u_s, gu_s, gv_s, xh_s, rs_s, sv_s, vc_s, pb_s, ps_s, pc_s, kd_s, vd_s,
             dkv_acc, dbias_acc, dsv_acc, dsink_acc):
        b, j = pl.program_id(0), pl.program_id(1)
        jt = nt - 1 - j

        @pl.when((b == 0) & (j == 0))
        def _():
            for ref in (dwo_ref, dg2_ref, loss_ref, dwsp_ref, dvg_ref, dvb_ref, dbias_acc, dsv_acc, dsink_acc):
                ref[...] = jnp.zeros_like(ref)

        @pl.when(j == 0)
        def _():
            dmkv_ref[...] = jnp.zeros_like(dmkv_ref)
            dkv_acc[...] = jnp.zeros_like(dkv_acc)

        carry = dkv_acc[0:CHUNK, :]
        dkv_acc[...] = jnp.zeros_like(dkv_acc)
        dkv_acc[tm:tm + CHUNK, :] = carry

        lo = _half_masks(CHUNK)
        lob = _half_masks(2 * CHUNK)
        lot = _half_masks(tm)
        row0 = pl.multiple_of(jt * tm, tm)

        for blk in range(bpt):
            r0 = blk * CHUNK
            rows = slice(r0, r0 + CHUNK)
            n = jt * bpt + blk
            for g in range(A_GROUPS):
                cg = slice(g * CHUNK, (g + 1) * CHUNK)
                u, gu = _gelu_and_grad(uv_ref[rows, cg])
                v, gv = _gelu_and_grad(uv_ref[rows, A_WIDTH + g * CHUNK:A_WIDTH + (g + 1) * CHUNK])
                mu = jnp.mean(v, axis=-1, keepdims=True)
                xc = v - mu
                rstd = lax.rsqrt(jnp.mean(xc * xc, axis=-1, keepdims=True) + EPS)
                xhat = xc * rstd
                vc = (xhat * vg_ref[:, cg] + vb_ref[:, cg]).astype(MM)
                sv = _dot(wt_ref[g], vc) + bcol_ref[g]
                u_s[rows, cg] = u
                gu_s[rows, cg] = gu
                gv_s[rows, cg] = gv
                xh_s[rows, cg] = xhat
                rs_s[rows, cg] = jnp.broadcast_to(rstd, (CHUNK, CHUNK))
                sv_s[rows, cg] = sv
                vc_s[rows, cg] = vc
                ycat[rows, cg] = u * sv
            pstart, cstart = _band_rows(n)
            kd = _dup_heads(jnp.concatenate([qkv_ref[pl.ds(pstart, CHUNK), SK_OFF:SK_OFF + CHUNK],
                                             qkv_ref[pl.ds(cstart, CHUNK), SK_OFF:SK_OFF + CHUNK]], axis=0))
            vd = _dup_heads(jnp.concatenate([qkv_ref[pl.ds(pstart, CHUNK), SV_OFF:SV_OFF + CHUNK],
                                             qkv_ref[pl.ds(cstart, CHUNK), SV_OFF:SV_OFF + CHUNK]], axis=0))
            first_add = _first_block_mask(n)
            for kvh in range(2):
                kd_s[blk * 2 + kvh] = kd[kvh]
                vd_s[blk * 2 + kvh] = vd[kvh]
                q128 = qkv_ref[pl.ds(cstart, CHUNK), SQ_OFF + kvh * CHUNK:SQ_OFF + (kvh + 1) * CHUNK].astype(F32)
                outs = []
                for gi in range(2):
                    h = 2 * kvh + gi
                    qsel = jnp.where(lo if gi == 0 else ~lo, q128, 0.0).astype(MM)
                    probs, ps = _swa_probs(qsel, kd[kvh], bias_ref[h], sink_ref[h], first_add)
                    pb_s[blk * 4 + h] = probs
                    ps_s[blk * 4 + h] = jnp.broadcast_to(ps, (CHUNK, CHUNK))
                    outs.append(_dot(probs.astype(MM), vd[kvh]))
                ycat[rows, YB_OFF + kvh * CHUNK:YB_OFF + (kvh + 1) * CHUNK] = jnp.where(lo, outs[0], outs[1])
        for g in range(2):
            q128 = qkv_ref[pl.ds(row0, tm), MQ_OFF + g * CHUNK:MQ_OFF + (g + 1) * CHUNK].astype(F32)
            k128 = mkv_ref[:, g * CHUNK:(g + 1) * CHUNK]
            v128 = mkv_ref[:, MEM_LEN + g * CHUNK:MEM_LEN + (g + 1) * CHUNK]
            outs = []
            for hh in range(2):
                qsel = jnp.where(lot if hh == 0 else ~lot, q128, 0.0).astype(MM)
                probs = _softmax(_dot_nt(qsel, k128) * SCALE)
                pc_s[2 * g + hh] = probs
                outs.append(_dot(probs.astype(MM), v128))
            ycat[:, YC_OFF + g * CHUNK:YC_OFF + (g + 1) * CHUNK] = jnp.where(lot, outs[0], outs[1])

        zt = z_ref[...]
        sig = 1.0 / (1.0 + jnp.exp(-zt))
        silu = zt * sig
        yc = ycat[...]
        yb = (yc * silu).astype(MM)
        o = _dot(yb, wo_ref[...])
        r2 = lax.rsqrt(jnp.mean(o * o, axis=-1, keepdims=True) + EPS)
        nrm = o * r2
        g2v = g2_ref[...]
        e = x_ref[...] + nrm * g2v - t_ref[...]
        l1 = jnp.sum(e * e, axis=-1, keepdims=True)
        loss_ref[...] += jnp.broadcast_to(jnp.sum(l1, axis=0, keepdims=True) * (0.5 / D_MODEL), loss_ref.shape)
        dxo = e * (1.0 / D_MODEL)
        dxo_ref[...] = dxo
        dg2_ref[...] += jnp.sum(dxo * nrm, axis=0, keepdims=True)
        dn = dxo * g2v
        do = r2 * (dn - nrm * jnp.mean(dn * nrm, axis=-1, keepdims=True))
        dob = do.astype(MM)
        dy = _dot_nt(dob, wo_ref[...])
        dz_ref[...] = (dy * yc * (sig * (1.0 + zt * (1.0 - sig)))).astype(MM)
        dyc[...] = dy * silu
        dwo_ref[...] += _dot_tn(yb, dob)

        for blk in range(bpt):
            r0 = blk * CHUNK
            rows = slice(r0, r0 + CHUNK)
            n = jt * bpt + blk
            for g in range(A_GROUPS):
                cg = slice(g * CHUNK, (g + 1) * CHUNK)
                cv = slice(A_WIDTH + g * CHUNK, A_WIDTH + (g + 1) * CHUNK)
                dya = dyc[rows, cg]
                duv_ref[rows, cg] = (dya * sv_s[rows, cg] * gu_s[rows, cg]).astype(MM)
                dsv = dya * u_s[rows, cg]
                dsvb = dsv.astype(MM)
                dsv_acc[g] += dsv
                dwsp_ref[g] += _dot_nt(dsvb, vc_s[rows, cg])
                dvc = _dot(wtt_ref[g], dsvb)
                xhat = xh_s[rows, cg]
                dvg_ref[:, cg] += jnp.sum(dvc * xhat, axis=0, keepdims=True)
                dvb_ref[:, cg] += jnp.sum(dvc, axis=0, keepdims=True)
                dxh = dvc * vg_ref[:, cg]
                dv = rs_s[rows, cg] * (dxh - jnp.mean(dxh, axis=-1, keepdims=True)
                                       - xhat * jnp.mean(dxh * xhat, axis=-1, keepdims=True))
                duv_ref[rows, cv] = (dv * gv_s[rows, cg]).astype(MM)
            _, cstart = _band_rows(n)
            dk_f, dv_f = [], []
            for kvh in range(2):
                kd = kd_s[blk * 2 + kvh]
                vd = vd_s[blk * 2 + kvh]
                q128 = qkv_ref[pl.ds(cstart, CHUNK), SQ_OFF + kvh * CHUNK:SQ_OFF + (kvh + 1) * CHUNK].astype(F32)
                do128 = dyc[rows, YB_OFF + kvh * CHUNK:YB_OFF + (kvh + 1) * CHUNK]
                dq128 = jnp.zeros((CHUNK, CHUNK), F32)
                dkd = jnp.zeros((2 * CHUNK, CHUNK), F32)
                dvd = jnp.zeros((2 * CHUNK, CHUNK), F32)
                for gi in range(2):
                    h = 2 * kvh + gi
                    half = lo if gi == 0 else ~lo
                    qsel = jnp.where(half, q128, 0.0).astype(MM)
                    dosel = jnp.where(half, do128, 0.0).astype(MM)
                    probs = pb_s[blk * 4 + h]
                    ps = ps_s[blk * 4 + h][:, 0:1]
                    dp = _dot_nt(dosel, vd)
                    delta = jnp.sum(probs * dp, axis=-1, keepdims=True)
                    ds = probs * (dp - delta)
                    dbias_acc[h] += ds
                    dsink_acc[h:h + 1, :] += jnp.broadcast_to(-jnp.sum(ps * delta, axis=0, keepdims=True), (1, CHUNK))
                    dss = (ds * SCALE).astype(MM)
                    dq128 = dq128 + jnp.where(half, _dot(dss, kd), 0.0)
                    dkd = dkd + _dot_tn(dss, qsel)
                    dvd = dvd + _dot_tn(probs.astype(MM), dosel)
                dqkv_ref[rows, SQ_OFF + kvh * CHUNK:SQ_OFF + (kvh + 1) * CHUNK] = dq128.astype(MM)
                dk_f.append(dkd + pltpu.roll(dkd, 64, 1))
                dv_f.append(dvd + pltpu.roll(dvd, 64, 1))
            dkv_acc[r0:r0 + 2 * CHUNK, 0:CHUNK] += jnp.where(lob, dk_f[0], dk_f[1])
            dkv_acc[r0:r0 + 2 * CHUNK, CHUNK:2 * CHUNK] += jnp.where(lob, dv_f[0], dv_f[1])
        dqkv_ref[:, SK_OFF:SK_OFF + 2 * CHUNK] = dkv_acc[CHUNK:CHUNK + tm, :].astype(MM)
        for g in range(2):
            q128 = qkv_ref[pl.ds(row0, tm), MQ_OFF + g * CHUNK:MQ_OFF + (g + 1) * CHUNK].astype(F32)
            k128 = mkv_ref[:, g * CHUNK:(g + 1) * CHUNK]
            v128 = mkv_ref[:, MEM_LEN + g * CHUNK:MEM_LEN + (g + 1) * CHUNK]
            do128 = dyc[:, YC_OFF + g * CHUNK:YC_OFF + (g + 1) * CHUNK]
            dq128 = jnp.zeros((tm, CHUNK), F32)
            dk128 = jnp.zeros((MEM_LEN, CHUNK), F32)
            dv128 = jnp.zeros((MEM_LEN, CHUNK), F32)
            for hh in range(2):
                half = lot if hh == 0 else ~lot
                qsel = jnp.where(half, q128, 0.0).astype(MM)
                dosel = jnp.where(half, do128, 0.0).astype(MM)
                probs = pc_s[2 * g + hh]
                dp = _dot_nt(dosel, v128)
                ds = probs * (dp - jnp.sum(probs * dp, axis=-1, keepdims=True))
                dss = (ds * SCALE).astype(MM)
                dq128 = dq128 + jnp.where(half, _dot(dss, k128), 0.0)
                dk128 = dk128 + _dot_tn(dss, qsel)
                dv128 = dv128 + _dot_tn(probs.astype(MM), dosel)
            dqkv_ref[:, MQ_OFF + g * CHUNK:MQ_OFF + (g + 1) * CHUNK] = dq128.astype(MM)
            dmkv_ref[:, g * CHUNK:(g + 1) * CHUNK] += dk128
            dmkv_ref[:, MEM_LEN + g * CHUNK:MEM_LEN + (g + 1) * CHUNK] += dv128

        @pl.when((b == nb - 1) & (j == nt - 1))
        def _():
            r = lax.broadcasted_iota(jnp.int32, (CHUNK, CHUNK), 0)
            c = lax.broadcasted_iota(jnp.int32, (CHUNK, CHUNK), 1)
            for g in range(A_GROUPS):
                dwsp_ref[g] = jnp.where(r >= c, dwsp_ref[g], 0.0)
                dbs_ref[g:g + 1, :] = jnp.sum(dsv_acc[g].T, axis=0, keepdims=True)
            rows8 = lax.broadcasted_iota(jnp.int32, (8, CHUNK), 0)
            cols8 = lax.broadcasted_iota(jnp.int32, (8, CHUNK), 1)
            sk = jnp.zeros((8, CHUNK), F32)
            for h in range(4):
                sk = sk + jnp.where((rows8 == 0) & (cols8 == h),
                                    jnp.broadcast_to(dsink_acc[h:h + 1, :], (8, CHUNK)), 0.0)
            dsink_ref[...] = sk
            bk = bk_ref[...]
            valid = _window_valid()
            rrow = lax.broadcasted_iota(jnp.int32, (N_BUCKETS, CHUNK), 0)
            rcol = lax.broadcasted_iota(jnp.int32, (N_BUCKETS, CHUNK), 1)
            acc = jnp.zeros((N_BUCKETS, CHUNK), F32)
            for bb in range(N_BUCKETS):
                hit = (bk == bb) & valid
                for h in range(4):
                    part = jnp.sum(jnp.where(hit, dbias_acc[h], 0.0), axis=-1, keepdims=True)
                    tot = jnp.sum(part, axis=0, keepdims=True)
                    acc = acc + jnp.where((rrow == bb) & (rcol == h), jnp.broadcast_to(tot, (N_BUCKETS, CHUNK)), 0.0)
            drel_ref[...] = acc

    t = nb * s
    tile = lambda w: pl.BlockSpec((tm, w), lambda b, j: (b * nt + nt - 1 - j, 0))
    per_batch = lambda r, w: pl.BlockSpec((None, r, w), lambda b, j: (b, 0, 0))
    grp = (A_GROUPS, CHUNK, CHUNK)
    return pl.pallas_call(
        body, name="mix", grid=(nb, nt),
        out_shape=(jax.ShapeDtypeStruct((t, UV_W), MM),
                   jax.ShapeDtypeStruct((t, QKV_W), MM),
                   jax.ShapeDtypeStruct((t, Z_W), MM),
                   jax.ShapeDtypeStruct((t, D_MODEL), F32),
                   jax.ShapeDtypeStruct((nb, MEM_LEN, 2 * MEM_LEN), F32),
                   jax.ShapeDtypeStruct((D_MODEL, D_MODEL), F32),
                   jax.ShapeDtypeStruct((1, D_MODEL), F32),
                   jax.ShapeDtypeStruct((8, CHUNK), F32),
                   jax.ShapeDtypeStruct(grp, F32),
                   jax.ShapeDtypeStruct((A_GROUPS, CHUNK), F32),
                   jax.ShapeDtypeStruct((1, A_WIDTH), F32),
                   jax.ShapeDtypeStruct((1, A_WIDTH), F32),
                   jax.ShapeDtypeStruct((8, CHUNK), F32),
                   jax.ShapeDtypeStruct((N_BUCKETS, CHUNK), F32)),
        in_specs=[tile(UV_W), tile(Z_W), per_batch(s, QKV_W), per_batch(MEM_LEN, 2 * MEM_LEN),
                  tile(D_MODEL), tile(D_MODEL),
                  _full((4, CHUNK, 2 * CHUNK)),
                  pl.BlockSpec(memory_space=pltpu.SMEM),
                  _full((1, A_WIDTH)), _full((1, A_WIDTH)),
                  _full(grp), _full(grp), _full(grp),
                  _full((1, D_MODEL)), _full((D_MODEL, D_MODEL)), _full((CHUNK, 2 * CHUNK))],
        out_specs=(tile(UV_W), tile(QKV_W), tile(Z_W), tile(D_MODEL), per_batch(MEM_LEN, 2 * MEM_LEN),
                   _full((D_MODEL, D_MODEL)), _full((1, D_MODEL)), _full((8, CHUNK)),
                   _full(grp), _full((A_GROUPS, CHUNK)), _full((1, A_WIDTH)), _full((1, A_WIDTH)),
                   _full((8, CHUNK)), _full((N_BUCKETS, CHUNK))),
        scratch_shapes=[pltpu.VMEM((tm, D_MODEL), F32), pltpu.VMEM((tm, D_MODEL), F32)]
                       + [pltpu.VMEM((tm, A_WIDTH), F32) for _ in range(6)]
                       + [pltpu.VMEM((tm, A_WIDTH), MM),
                          pltpu.VMEM((bpt * 4, CHUNK, 2 * CHUNK), F32),
                          pltpu.VMEM((bpt * 4, CHUNK, CHUNK), F32),
                          pltpu.VMEM((4, tm, MEM_LEN), F32),
                          pltpu.VMEM((bpt * 2, 2 * CHUNK, CHUNK), MM),
                          pltpu.VMEM((bpt * 2, 2 * CHUNK, CHUNK), MM),
                          pltpu.VMEM((tm + CHUNK, 2 * CHUNK), F32),
                          pltpu.VMEM((4, CHUNK, 2 * CHUNK), F32),
                          pltpu.VMEM(grp, F32),
                          pltpu.VMEM((8, CHUNK), F32)],
        compiler_params=_params(dimension_semantics=("arbitrary", "arbitrary")),
    )(uv, z, qkv3, mkv3, x2, tgt2, bias, sinks, vg, vb, wt, wtt, bcol, g2, w_o, buckets)


class _ShardReduce:
    def __init__(self, pos, g, bufs, sems, owner_x):
        self.x, self.y, self.c = pos
        self.g = g
        self.own, self.rcv, self.sbuf, self.rbuf = bufs
        self.ld, self.sa, self.ra, self.sb, self.rb = sems
        self.owner_x = owner_x
        self.nq = 4 if owner_x is None else 2
        self.nrow = g.shape[1]
        self.here = (self.x, self.y, self.c)
        self.sib = (self.x, self.y, 1 - self.c)
        self.chips = _other_chips(self.x, self.y)

    def _load(self, q):
        return pltpu.make_async_copy(self.g.at[2 * q + self.c], self.own.at[q], self.ld.at[q])

    def _to_sib(self, q, to):
        return _remote(self.g.at[2 * q + 1 - self.c], self.rcv.at[q], self.sa.at[q], self.ra.at[q], to)

    def _sends(self):
        x, y = self.x, self.y
        if self.owner_x is None:
            return [(None, j, j, chip, 2 * chip[0] + chip[1]) for j, chip in enumerate(self.chips)]
        mine = x == self.owner_x
        return [(mine, 0, 1, self.chips[1], 1 - y),
                (~mine, 0, 0, self.chips[0], y),
                (~mine, 1, 2, self.chips[2], 1 - y)]

    def _to_chip(self, k, j, chip, to_core):
        return _remote(self.sbuf.at[k], self.rbuf.at[j], self.sb.at[k], self.rb.at[j], (chip[0], chip[1], to_core))

    def _when(self, cond, fn):
        if cond is None:
            fn()
        else:
            pl.when(cond)(fn)

    def start(self):
        for q in range(self.nq):
            self._load(q).start()
            self._to_sib(q, self.sib).start()

    def mid(self):
        for q in range(self.nq):
            self._load(q).wait()
            self._to_sib(q, self.here).wait_recv()

        def add(r):
            for q in range(self.nq):
                self.rcv[q, r, :] = self.rcv[q, r, :] + self.own[q, r, :]

        _rows_loop(self.nrow, add)
        for cond, k, j, chip, which in self._sends():
            def send(k=k, j=j, chip=chip, which=which):
                def cast(r):
                    self.sbuf[k, r, :] = self.rcv[which, r, :].astype(BF16)

                _rows_loop(self.nrow, cast)
                self._to_chip(k, j, chip, self.c).start()

            self._when(cond, send)

    def finish(self, out):
        mine = None if self.owner_x is None else self.x == self.owner_x
        which = (2 * self.x + self.y) if self.owner_x is None else self.y

        def total():
            for j in range(3):
                self._to_chip(0, j, self.chips[j], self.c).wait_recv()

            def tot(r):
                g = self.rcv[which, r, :]
                for j in range(3):
                    g = g + self.rbuf[j, r, :].astype(F32)
                out[r, :] = g

            _rows_loop(self.nrow, tot)

        self._when(mine, total)
        if mine is not None:
            def zero():
                def z(r):
                    out[r, :] = jnp.zeros((_ROWS, out.shape[1]), F32)

                _rows_loop(self.nrow, z)

            pl.when(~mine)(zero)
        for q in range(self.nq):
            self._to_sib(q, self.sib).wait_send()
        for cond, k, j, chip, _ in self._sends():
            self._when(cond, lambda k=k, j=j, chip=chip: self._to_chip(k, j, chip, self.c).wait_send())


def _reduce_scratch(shape, nq, nsend):
    return [pltpu.VMEM((nq,) + shape, F32), pltpu.VMEM((nq,) + shape, F32),
            pltpu.VMEM((nsend,) + shape, BF16), pltpu.VMEM((3,) + shape, BF16),
            pltpu.SemaphoreType.DMA((nq,)), pltpu.SemaphoreType.DMA((nq,)), pltpu.SemaphoreType.DMA((nq,)),
            pltpu.SemaphoreType.DMA((3,)), pltpu.SemaphoreType.DMA((3,))]


def _inproj_bwd_a(x2, duv, dqkv, g1, tm):
    t = x2.shape[0]
    nt = t // tm

    def body(x_ref, duv_ref, dqk_ref, g_ref, dw_hbm, acc, sem):
        i = pl.program_id(0)

        @pl.when(i == 0)
        def _():
            acc[...] = jnp.zeros_like(acc)

        xf = x_ref[...]
        r = lax.rsqrt(jnp.mean(xf * xf, axis=-1, keepdims=True) + EPS)
        h = (xf * r * g_ref[...]).astype(MM)
        acc[0:UV_W, :] += _dot_tn(duv_ref[...], h)
        acc[UV_W:HALF_IN, :] += _dot_tn(dqk_ref[...], h)

        @pl.when(i == nt - 1)
        def _():
            cp = pltpu.make_async_copy(acc, dw_hbm, sem)
            cp.start()
            cp.wait()

    tile = lambda w: pl.BlockSpec((tm, w), lambda i: (i, 0))
    return pl.pallas_call(
        body, name="inproj_bwd_a", grid=(nt,),
        out_shape=jax.ShapeDtypeStruct((HALF_IN, D_MODEL), F32),
        in_specs=[tile(D_MODEL), tile(UV_W), tile(QK_W), _full((1, D_MODEL))],
        out_specs=pl.BlockSpec(memory_space=pl.ANY),
        scratch_shapes=[pltpu.VMEM((HALF_IN, D_MODEL), F32), pltpu.SemaphoreType.DMA],
        compiler_params=_params(dimension_semantics=("arbitrary",)),
    )(x2, duv, dqkv, g1)


def _inproj_bwd_b(x2, dxo, duv, dqkv, dz, g1, w_in_t, dwa, dwo, dwm, tm):
    t = x2.shape[0]
    nt = t // tm
    mid_step = min(1, nt - 1)
    sh_a, sh_o, sh_m = dwa.shape[1:], dwo.shape[1:], dwm.shape[1:]
    n_red = 9

    def body(x_ref, dxo_ref, duv_ref, dqkv_ref, dz_ref, g_ref, w_ref, dwa_ref, dwo_ref, dwm_ref,
             gx_ref, dwb_hbm, dg_ref, ta_ref, to_ref, tmk_ref, acc, sem, *red):
        i = pl.program_id(0)
        pos = _position()
        reducers = [_ShardReduce(pos, g, red[k * n_red:k * n_red + 4], red[k * n_red + 4:(k + 1) * n_red], ox)
                    for k, (g, ox) in enumerate(((dwa_ref, 0), (dwo_ref, None), (dwm_ref, None)))]
        outs = (ta_ref, to_ref, tmk_ref)

        @pl.when(i == 0)
        def _():
            acc[...] = jnp.zeros_like(acc)
            dg_ref[...] = jnp.zeros_like(dg_ref)
            for rd in reducers:
                rd.start()

        @pl.when(i == mid_step)
        def _():
            for rd in reducers:
                rd.mid()

        xf = x_ref[...]
        r = lax.rsqrt(jnp.mean(xf * xf, axis=-1, keepdims=True) + EPS)
        nx = xf * r
        gv = g_ref[...]
        h = (nx * gv).astype(MM)
        duv_t, dqkv_t, dz_t = duv_ref[...], dqkv_ref[...], dz_ref[...]
        acc[0:QKV_W - QK_W, :] += _dot_tn(dqkv_t[:, QK_W:QKV_W], h)
        acc[QKV_W - QK_W:HALF_IN, :] += _dot_tn(dz_t, h)
        dh = (_dot(duv_t, w_ref[0:UV_W, :]) + _dot(dqkv_t, w_ref[UV_W:UV_W + QKV_W, :])
              + _dot(dz_t, w_ref[UV_W + QKV_W:IN_WIDTH, :]))
        dg_ref[...] += jnp.sum(dh * nx, axis=0, keepdims=True)
        dnx = dh * gv
        gx_ref[...] = dxo_ref[...] + r * (dnx - nx * jnp.mean(dnx * nx, axis=-1, keepdims=True))

        @pl.when(i == nt - 1)
        def _():
            cp = pltpu.make_async_copy(acc, dwb_hbm, sem)
            cp.start()
            for rd, out in zip(reducers, outs):
                rd.finish(out)
            cp.wait()

    tile = lambda w: pl.BlockSpec((tm, w), lambda i: (i, 0))
    anyspec = pl.BlockSpec(memory_space=pl.ANY)
    return pl.pallas_call(
        body, name="inproj_bwd_b", grid=(nt,),
        out_shape=(jax.ShapeDtypeStruct((t, D_MODEL), F32),
                   jax.ShapeDtypeStruct((HALF_IN, D_MODEL), F32),
                   jax.ShapeDtypeStruct((1, D_MODEL), F32),
                   jax.ShapeDtypeStruct(sh_a, F32),
                   jax.ShapeDtypeStruct(sh_o, F32),
                   jax.ShapeDtypeStruct(sh_m, F32)),
        in_specs=[tile(D_MODEL), tile(D_MODEL), tile(UV_W), tile(QKV_W), tile(Z_W),
                  _full((1, D_MODEL)),
                  pl.BlockSpec((IN_WIDTH, D_MODEL), lambda i: (0, 0), pipeline_mode=pl.Buffered(1)),
                  anyspec, anyspec, anyspec],
        out_specs=(tile(D_MODEL), anyspec, _full((1, D_MODEL)), _full(sh_a), _full(sh_o), _full(sh_m)),
        scratch_shapes=([pltpu.VMEM((HALF_IN, D_MODEL), F32), pltpu.SemaphoreType.DMA]
                        + _reduce_scratch(sh_a, 2, 2) + _reduce_scratch(sh_o, 4, 3) + _reduce_scratch(sh_m, 4, 3)),
        compiler_params=_params(dimension_semantics=("arbitrary",)),
    )(x2, dxo, duv, dqkv, dz, g1, w_in_t, dwa, dwo, dwm)


_S_LAYOUT = (((1, D_MODEL), 0), ((1, D_MODEL), 8), ((1, D_MODEL), 16),
             ((1, A_WIDTH), 24), ((1, A_WIDTH), 28), ((A_GROUPS, CHUNK), 32),
             ((1, 4), 36), ((N_BUCKETS, 4), 40),
             ((A_GROUPS * CHUNK, CHUNK), 72))
_LOSS_ROW = 37
_S_ROWS = 72 + A_GROUPS * CHUNK
_N_SMALL = len(_S_LAYOUT)


def _pack_rows(dst, refs):
    for (shp, r0), ref in zip(_S_LAYOUT, refs):
        if shp[0] == 1 and shp[1] >= CHUNK:
            for i in range(shp[1] // CHUNK):
                dst[r0 + i:r0 + i + 1, :] = ref[:, i * CHUNK:(i + 1) * CHUNK]
        elif ref.shape[-1] == CHUNK:
            dst[r0:r0 + shp[0], :] = ref[0:shp[0], :]
        else:
            dst[r0:r0 + shp[0], 0:shp[1]] = ref[...]


def _unpack_rows(src, refs):
    for (shp, r0), ref in zip(_S_LAYOUT, refs):
        if shp[0] == 1 and shp[1] >= CHUNK:
            for i in range(shp[1] // CHUNK):
                ref[:, i * CHUNK:(i + 1) * CHUNK] = src[r0 + i:r0 + i + 1, :]
        elif shp[1] == CHUNK:
            ref[...] = src[r0:r0 + shp[0], :]
        else:
            ref[...] = src[r0:r0 + shp[0], 0:shp[1]]


def _greduce(dwb, small_g, loss_p):
    sh_a = dwb.shape[1:]
    rs = _S_ROWS
    n_red = 9

    def body(*refs):
        it = iter(refs)
        take = lambda n: [next(it) for _ in range(n)]
        dwb_ref, = take(1)
        sg_refs = take(_N_SMALL)
        loss_ref, = take(1)
        ta_ref, ogs = take(2)
        red = take(n_red)
        gs_ref, rs_a, rs_b = take(3)
        ssem_a, rsem_a, ssem_b, rsem_b = take(4)

        pos = _position()
        x, y, cc = pos
        myq = 2 * x + y
        here, sib = (x, y, cc), (x, y, 1 - cc)
        chips = _other_chips(x, y)
        rd = _ShardReduce(pos, dwb_ref, red[0:4], red[4:n_red], 1)
        rd.start()

        gs_ref[...] = jnp.zeros_like(gs_ref)
        _pack_rows(gs_ref, sg_refs)
        gs_ref[_LOSS_ROW:_LOSS_ROW + 1, :] = loss_ref[0:1, :]
        small_a = _remote(gs_ref, rs_a, ssem_a, rsem_a, sib)
        small_a.start()

        rd.mid()

        _remote(gs_ref, rs_a, ssem_a, rsem_a, here).wait_recv()
        rs_b[myq] = gs_ref[...] + rs_a[...]
        small_b = [_remote(rs_b.at[myq], rs_b.at[myq], ssem_b.at[j], rsem_b.at[j], (chip[0], chip[1], cc))
                   for j, chip in enumerate(chips)]
        for cp in small_b:
            cp.start()
        for j in range(3):
            _remote(rs_b.at[myq], rs_b.at[myq], ssem_b.at[j], rsem_b.at[j], here).wait_recv()

        def tot_s(i, _):
            r = pl.ds(pl.multiple_of(i * 8, 8), 8)
            ogs[r, :] = ((rs_b[0, r, :] + rs_b[1, r, :]) + rs_b[2, r, :]) + rs_b[3, r, :]
            return 0

        lax.fori_loop(0, rs // 8, tot_s, 0)

        rd.finish(ta_ref)
        small_a.wait_send()
        for cp in small_b:
            cp.wait_send()

    vm = pl.BlockSpec(memory_space=pltpu.VMEM)
    anyspec = pl.BlockSpec(memory_space=pl.ANY)
    return pl.pallas_call(
        body, name="greduce",
        out_shape=(jax.ShapeDtypeStruct(sh_a, F32), jax.ShapeDtypeStruct((rs, CHUNK), F32)),
        in_specs=[anyspec] + [vm] * (_N_SMALL + 1),
        out_specs=(vm, vm),
        scratch_shapes=(_reduce_scratch(sh_a, 2, 2)
                        + [pltpu.VMEM((rs, CHUNK), F32), pltpu.VMEM((rs, CHUNK), F32), pltpu.VMEM((4, rs, CHUNK), F32),
                           pltpu.SemaphoreType.DMA, pltpu.SemaphoreType.DMA,
                           pltpu.SemaphoreType.DMA((3,)), pltpu.SemaphoreType.DMA((3,))]),
        compiler_params=_params(),
    )(dwb, *small_g, loss_p)


def _adamw(w, g, m, v):
    m = ADAM_B1 * m + (1.0 - ADAM_B1) * g
    v = ADAM_B2 * v + (1.0 - ADAM_B2) * (g * g)
    m_hat = m / (1.0 - ADAM_B1 ** ADAM_STEP)
    v_hat = v / (1.0 - ADAM_B2 ** ADAM_STEP)
    delta = -ADAM_LR * (m_hat / (jnp.sqrt(v_hat) + ADAM_EPS) + ADAM_WD * w)
    return delta, m, v


def _update(ta0, ta1, tb, tc, ts, big_wmv, small_wmv):
    shapes = (ta0.shape, tb.shape, tc.shape)
    rs = _S_ROWS
    small_shapes = [tuple(a.shape) for a in small_wmv[0]]

    def body(*refs):
        it = iter(refs)
        take = lambda n: [next(it) for _ in range(n)]
        ga0_ref, ga1_ref, gb_ref, gc_ref, gs_ref = take(5)
        wa, ma, va, wb, mb, vb_, wc, mc, vc = take(9)
        sw_refs, sm_refs, sv_refs = take(_N_SMALL), take(_N_SMALL), take(_N_SMALL)
        oga, oda, oma, ova, ogb, odb, omb, ovb, ogc, odc, omc, ovc = take(12)
        so_refs = [take(_N_SMALL) for _ in range(4)]
        loss_out, = take(1)
        ws, ms, vs, ods, oms, ovs = take(6)

        for buf in (ws, ms, vs):
            buf[...] = jnp.zeros_like(buf)
        _pack_rows(ws, sw_refs)
        _pack_rows(ms, sm_refs)
        _pack_rows(vs, sv_refs)

        big = ((lambda r: ga0_ref[r, :] + ga1_ref[r, :], wa, ma, va, oga, oda, oma, ova),
               (lambda r: gb_ref[r, :], wb, mb, vb_, ogb, odb, omb, ovb),
               (lambda r: gc_ref[r, :], wc, mc, vc, ogc, odc, omc, ovc))
        for arr in range(3):
            grad, w_r, m_r, v_r, og, od, om, ov = big[arr]

            def upd(r, grad=grad, w_r=w_r, m_r=m_r, v_r=v_r, og=og, od=od, om=om, ov=ov):
                g = grad(r)
                d, m, v = _adamw(w_r[r, :], g, m_r[r, :], v_r[r, :])
                og[r, :] = g
                od[r, :] = d
                om[r, :] = m
                ov[r, :] = v

            _rows_loop(shapes[arr][0], upd)

        def upd_s(i, _):
            r = pl.ds(pl.multiple_of(i * 8, 8), 8)
            d, m, v = _adamw(ws[r, :], gs_ref[r, :], ms[r, :], vs[r, :])
            ods[r, :] = d
            oms[r, :] = m
            ovs[r, :] = v
            return 0

        lax.fori_loop(0, rs // 8, upd_s, 0)
        for k, buf in enumerate((gs_ref, ods, oms, ovs)):
            _unpack_rows(buf, so_refs[k])
        loss_out[...] = gs_ref[_LOSS_ROW:_LOSS_ROW + 1, 0:1]

    vm = pl.BlockSpec(memory_space=pltpu.VMEM)
    big_out = []
    for shp in shapes:
        big_out += [jax.ShapeDtypeStruct(shp, F32)] * 4
    small_out = [jax.ShapeDtypeStruct(shp, F32) for shp in small_shapes] * 4
    out_shape = tuple(big_out + small_out + [jax.ShapeDtypeStruct((1, 1), F32)])
    n_in = 5 + 9 + 3 * _N_SMALL
    return pl.pallas_call(
        body, name="update",
        out_shape=out_shape,
        in_specs=[vm] * n_in,
        out_specs=tuple([vm] * len(out_shape)),
        scratch_shapes=[pltpu.VMEM((rs, CHUNK), F32) for _ in range(6)],
        compiler_params=_params(),
    )(ta0, ta1, tb, tc, ts, *big_wmv, *small_wmv[0], *small_wmv[1], *small_wmv[2])


def _local_step(x, mem, loss_target, pre_norm_g, post_norm_g, mem_norm_g, v_norm_g, v_norm_b, w_spatial, b_spatial,
                attn_sinks, rel_bias, w_in_t, wo_sh, wm_sh):
    nb, s, _ = x.shape
    t = nb * s
    x2 = x.reshape(t, D_MODEL)
    tgt2 = loss_target.reshape(t, D_MODEL)
    mem2 = mem.reshape(nb * MEM_LEN, D_MODEL)
    tm_mix = min(256, s)
    tm_proj = min(512, t)
    tm_bwd = min(256, t)

    buckets = jnp.asarray(_t5_buckets())
    sinks = attn_sinks.reshape(4)
    bias, wt, wtt, bcol = _prep(rel_bias, w_spatial[0], b_spatial[0], buckets)

    uv, qkv, z, wo_all, wm_all = _inproj_fwd(x2, pre_norm_g, w_in_t, wo_sh, wm_sh, tm_proj)
    w_o = wo_all.reshape(D_MODEL, D_MODEL)
    w_mkv = wm_all.reshape(D_MODEL, 2 * MEM_LEN)
    mkv = _memkv_fwd(mem2, mem_norm_g, w_mkv)
    qkv3 = qkv.reshape(nb, s, QKV_W)
    mkv3 = mkv.reshape(nb, MEM_LEN, 2 * MEM_LEN)
    duv, dqkv, dz, dxo, dmkv, dwo, dg2, loss_p, dwsp, dbs, dvg, dvb, dsink, drel = _mix(
        uv, z, qkv3, mkv3, x2, tgt2, bias, sinks, v_norm_g, v_norm_b, wt, wtt, bcol, post_norm_g, w_o, buckets, tm_mix)
    dwmkv, dgm = _memkv_bwd(dmkv.reshape(nb * MEM_LEN, 2 * MEM_LEN), mem2, mem_norm_g, w_mkv)
    dwa = _inproj_bwd_a(x2, duv, dqkv, pre_norm_g, tm_proj)
    gx, dwb, dg1, ta0, tb, tc = _inproj_bwd_b(
        x2, dxo, duv, dqkv, dz, pre_norm_g, w_in_t, dwa.reshape(N_DEV // 2, SHARD_IN, D_MODEL),
        dwo.reshape(N_DEV, SHARD_O, D_MODEL), dwmkv.reshape(N_DEV, SHARD_O, 2 * MEM_LEN), tm_bwd)
    small = [dg1, dg2, dgm, dvg, dvb, dbs, dsink, drel, dwsp.reshape(A_GROUPS * CHUNK, CHUNK)]
    ta1, ts = _greduce(dwb.reshape(N_DEV // 2, SHARD_IN, D_MODEL), small, loss_p)
    return gx.reshape(nb, s, D_MODEL), ta0, ta1, tb, tc, ts


def kernel(x, mem, pre_norm_g, post_norm_g, mem_norm_g, w_in, w_mem_kv, v_norm_g, v_norm_b, w_spatial, b_spatial, attn_sinks, rel_bias, w_out, loss_target, m_pre_norm_g, m_post_norm_g, m_mem_norm_g, m_w_in, m_w_mem_kv, m_v_norm_g, m_v_norm_b, m_w_spatial, m_b_spatial, m_attn_sinks, m_rel_bias, m_w_out, v_pre_norm_g, v_post_norm_g, v_mem_norm_g, v_w_in, v_w_mem_kv, v_v_norm_g, v_v_norm_b, v_w_spatial, v_b_spatial, v_attn_sinks, v_rel_bias, v_w_out):
    sh_a = (w_in[0].T, m_w_in[0].T, v_w_in[0].T)
    sh_b = (w_out[0], m_w_out[0], v_w_out[0])
    sh_c = (w_mem_kv[0], m_w_mem_kv[0], v_w_mem_kv[0])
    w_in_t = _wgather(sh_a[0]).reshape(IN_WIDTH, D_MODEL)

    gx, ta0, ta1, tb, tc, ts = _local_step(
        x, mem, loss_target, pre_norm_g, post_norm_g, mem_norm_g, v_norm_g, v_norm_b, w_spatial, b_spatial,
        attn_sinks, rel_bias, w_in_t, sh_b[0], sh_c[0])

    small_names = ["pre_norm_g", "post_norm_g", "mem_norm_g", "v_norm_g", "v_norm_b", "b_spatial", "attn_sinks",
                   "rel_bias", "w_spatial"]
    given = dict(pre_norm_g=(pre_norm_g, m_pre_norm_g, v_pre_norm_g), post_norm_g=(post_norm_g, m_post_norm_g, v_post_norm_g),
                 mem_norm_g=(mem_norm_g, m_mem_norm_g, v_mem_norm_g), v_norm_g=(v_norm_g, m_v_norm_g, v_v_norm_g),
                 v_norm_b=(v_norm_b, m_v_norm_b, v_v_norm_b), b_spatial=(b_spatial, m_b_spatial, v_b_spatial),
                 attn_sinks=(attn_sinks, m_attn_sinks, v_attn_sinks), rel_bias=(rel_bias, m_rel_bias, v_rel_bias),
                 w_spatial=(w_spatial, m_w_spatial, v_w_spatial))
    small_wmv = [[given[n][k].reshape(shp) for n, (shp, _) in zip(small_names, _S_LAYOUT)] for k in range(3)]

    outs = _update(ta0, ta1, tb, tc, ts, (*sh_a, *sh_b, *sh_c), small_wmv)
    ra, rb, rc = outs[0:4], outs[4:8], outs[8:12]
    loss = outs[12 + 4 * _N_SMALL].reshape(())

    res = {}
    for k, kind in enumerate(("grad", "delta", "new_m", "new_v")):
        res[kind, "w_in"] = ra[k].T[None]
        res[kind, "w_out"] = rb[k][None]
        res[kind, "w_mem_kv"] = rc[k][None]
        for i, n in enumerate(small_names):
            res[kind, n] = outs[12 + k * _N_SMALL + i].reshape(given[n][0].shape)
    order = ["pre_norm_g", "post_norm_g", "mem_norm_g", "w_in", "w_mem_kv", "v_norm_g", "v_norm_b", "w_spatial",
             "b_spatial", "attn_sinks", "rel_bias", "w_out"]
    flat = [res[kind, n] for kind in ("grad", "delta", "new_m", "new_v") for n in order]
    return (loss, gx, *flat)
```

```python
import numpy as np
import jax
import jax.numpy as jnp
from jax import lax
from jax.experimental import pallas as pl
from jax.experimental.pallas import tpu as pltpu

F32 = jnp.float32
BF16 = jnp.bfloat16
MM = jnp.bfloat16

D_MODEL = 1024
CHUNK = 128
A_GROUPS = 4
A_WIDTH = 512
UV_W = 1024
QKV_W = 768
Z_W = 1024
IN_WIDTH = UV_W + QKV_W + Z_W
MEM_LEN = 256
N_BUCKETS = 32
MAX_DISTANCE = 128
EPS = 1e-6
NEG = -1e30
SCALE = 0.125
N_DEV = 8
SHARD_IN = IN_WIDTH // N_DEV
SHARD_O = D_MODEL // N_DEV
HALF_IN = IN_WIDTH // 2
QK_W = HALF_IN - UV_W

SQ_OFF, SK_OFF, SV_OFF, MQ_OFF = 0, 256, 384, 512
YB_OFF, YC_OFF = 512, 768

ADAM_LR = 0.001
ADAM_B1 = 0.9
ADAM_B2 = 0.999
ADAM_EPS = 1e-08
ADAM_WD = 0.01
ADAM_STEP = 10

VMEM_LIMIT = 58 * 1024 * 1024

_GELU_C = 0.7978845608028654
_GELU_A = 0.044715

MESH = pl.DeviceIdType.MESH
_ROWS = 32


def _dot(a, b):
    return lax.dot_general(a, b, (((1,), (0,)), ((), ())), preferred_element_type=F32)


def _dot_nt(a, b):
    return lax.dot_general(a, b, (((1,), (1,)), ((), ())), preferred_element_type=F32)


def _dot_tn(a, b):
    return lax.dot_general(a, b, (((0,), (0,)), ((), ())), preferred_element_type=F32)


def _gelu_and_grad(x):
    x2 = x * x
    t = jnp.tanh(_GELU_C * (x + _GELU_A * x * x2))
    g = 0.5 * x * (1.0 + t)
    dg = 0.5 * (1.0 + t) + 0.5 * x * (1.0 - t * t) * (_GELU_C * (1.0 + 3.0 * _GELU_A * x2))
    return g, dg


def _t5_buckets():
    qi = np.arange(CHUNK)[:, None]
    kj = np.arange(2 * CHUNK)[None, :]
    n = np.maximum(qi + CHUNK - kj, 0)
    max_exact = N_BUCKETS // 2
    large = max_exact + (np.log(np.maximum(n, 1) / max_exact) / np.log(MAX_DISTANCE / max_exact)
                         * (N_BUCKETS - max_exact)).astype(np.int32)
    large = np.minimum(large, N_BUCKETS - 1)
    return np.where(n < max_exact, n, large).astype(np.int32)


def _params(**kw):
    return pltpu.CompilerParams(vmem_limit_bytes=VMEM_LIMIT, **kw)


def _full(shape):
    nd = len(shape)
    return pl.BlockSpec(shape, lambda *_: (0,) * nd)


def _window_valid():
    qi = lax.broadcasted_iota(jnp.int32, (CHUNK, 2 * CHUNK), 0)
    kj = lax.broadcasted_iota(jnp.int32, (CHUNK, 2 * CHUNK), 1)
    dist = qi + CHUNK - kj
    return (dist >= 0) & (dist < CHUNK)


def _position():
    return lax.axis_index("x"), lax.axis_index("y"), lax.axis_index("c")


def _other_chips(x, y):
    return [(1 - x, y), (x, 1 - y), (1 - x, 1 - y)]


def _remote(src, dst, ssem, rsem, to):
    return pltpu.make_async_remote_copy(src_ref=src, dst_ref=dst, send_sem=ssem, recv_sem=rsem,
                                        device_id=to, device_id_type=MESH)


def _rows_loop(nrow, fn):
    def step(i, _):
        fn(pl.ds(pl.multiple_of(i * _ROWS, _ROWS), _ROWS))
        return 0

    lax.fori_loop(0, nrow // _ROWS, step, 0)


class _Gather:
    def __init__(self, pos, stage, out, ssem, rsem):
        self.x, self.y, self.c = pos
        self.stage, self.out, self.ssem, self.rsem = stage, out, ssem, rsem
        self.me = 4 * self.x + 2 * self.y + self.c
        self.sib = (self.x, self.y, 1 - self.c)
        self.chips = _other_chips(self.x, self.y)

    def _idx(self, chip, core):
        return 4 * chip[0] + 2 * chip[1] + core

    def _own(self, k, to):
        return _remote(self.stage, self.out.at[self.me], self.ssem.at[k], self.rsem.at[k], to)

    def _passed(self, j, core, to):
        blk = self.out.at[self._idx(self.chips[j], core)]
        return _remote(blk, blk, self.ssem.at[4 + j], self.rsem.at[4 + j], to)

    def start(self):
        self._own(0, self.sib).start()
        for j, chip in enumerate(self.chips):
            self._own(1 + j, (chip[0], chip[1], self.c)).start()

    def forward(self):
        here = (self.x, self.y, self.c)
        for j, chip in enumerate(self.chips):
            blk = self.out.at[self._idx(chip, self.c)]
            _remote(self.stage, blk, self.ssem.at[1 + j], self.rsem.at[1 + j], here).wait_recv()
            self._passed(j, self.c, self.sib).start()

    def finish(self):
        here = (self.x, self.y, self.c)
        blk = self.out.at[self._idx((self.x, self.y), 1 - self.c)]
        _remote(self.stage, blk, self.ssem.at[0], self.rsem.at[0], here).wait_recv()
        for j in range(3):
            self._passed(j, 1 - self.c, here).wait_recv()
        self._own(0, self.sib).wait_send()
        for j, chip in enumerate(self.chips):
            self._own(1 + j, (chip[0], chip[1], self.c)).wait_send()
            self._passed(j, self.c, self.sib).wait_send()


def _wgather(a):
    def body(a_ref, oa, ssem, rsem):
        pos = _position()
        me = 4 * pos[0] + 2 * pos[1] + pos[2]
        oa[me] = a_ref[...].astype(BF16)
        g = _Gather(pos, oa.at[me], oa, ssem, rsem)
        g.start()
        g.forward()
        g.finish()

    vm = pl.BlockSpec(memory_space=pltpu.VMEM)
    return pl.pallas_call(
        body, name="wgather",
        out_shape=jax.ShapeDtypeStruct((N_DEV,) + a.shape, BF16),
        in_specs=[vm], out_specs=vm,
        scratch_shapes=[pltpu.SemaphoreType.DMA((7,)), pltpu.SemaphoreType.DMA((7,))],
        compiler_params=_params(),
    )(a)


def _prep(rel_bias, w_sp, b_sp, buckets):
    def body(rb_ref, w_ref, b_ref, bk_ref, bias_ref, wt_ref, wtt_ref, bcol_ref):
        valid = _window_valid()
        bk = bk_ref[...]
        acc = [jnp.full((CHUNK, 2 * CHUNK), NEG, F32) for _ in range(4)]
        for b in range(N_BUCKETS):
            hit = (bk == b) & valid
            for h in range(4):
                acc[h] = jnp.where(hit, rb_ref[b, h], acc[h])
        for h in range(4):
            bias_ref[h] = acc[h]
        r = lax.broadcasted_iota(jnp.int32, (CHUNK, CHUNK), 0)
        c = lax.broadcasted_iota(jnp.int32, (CHUNK, CHUNK), 1)
        for g in range(A_GROUPS):
            w = jnp.where(r >= c, w_ref[g], 0.0)
            wt_ref[g] = w.astype(MM)
            wtt_ref[g] = w.T.astype(MM)
            bcol_ref[g] = jnp.broadcast_to(b_ref[g:g + 1, :], (CHUNK, CHUNK)).T

    return pl.pallas_call(
        body, name="prep",
        out_shape=(jax.ShapeDtypeStruct((4, CHUNK, 2 * CHUNK), F32),
                   jax.ShapeDtypeStruct((A_GROUPS, CHUNK, CHUNK), MM),
                   jax.ShapeDtypeStruct((A_GROUPS, CHUNK, CHUNK), MM),
                   jax.ShapeDtypeStruct((A_GROUPS, CHUNK, CHUNK), F32)),
        in_specs=[pl.BlockSpec(memory_space=pltpu.SMEM), pl.BlockSpec(memory_space=pltpu.VMEM),
                  pl.BlockSpec(memory_space=pltpu.VMEM), pl.BlockSpec(memory_space=pltpu.VMEM)],
        out_specs=tuple(pl.BlockSpec(memory_space=pltpu.VMEM) for _ in range(4)),
    )(rel_bias, w_sp, b_sp, buckets)


def _inproj_fwd(x2, g1, w_in_t, wo_sh, wm_sh, tm):
    t = x2.shape[0]
    nt = t // tm
    fwd_step = max(nt - 2, 0)

    def body(x_ref, g_ref, w_ref, wo_ref, wm_ref, uv_ref, qkv_ref, z_ref, h_ref, oo_ref, om_ref,
             st_o, st_m, ssem_o, rsem_o, ssem_m, rsem_m, lsem):
        i = pl.program_id(0)
        pos = _position()
        me = 4 * pos[0] + 2 * pos[1] + pos[2]
        gathers = (_Gather(pos, st_o, oo_ref, ssem_o, rsem_o), _Gather(pos, st_m, om_ref, ssem_m, rsem_m))
        local = (pltpu.make_async_copy(st_o, oo_ref.at[me], lsem.at[0]),
                 pltpu.make_async_copy(st_m, om_ref.at[me], lsem.at[1]))

        @pl.when(i == 0)
        def _():
            st_o[...] = wo_ref[...].astype(BF16)
            st_m[...] = wm_ref[...].astype(BF16)
            for cp in local:
                cp.start()
            for g in gathers:
                g.start()

        xf = x_ref[...]
        r = lax.rsqrt(jnp.mean(xf * xf, axis=-1, keepdims=True) + EPS)
        h = (xf * r * g_ref[...]).astype(MM)
        h_ref[...] = h
        uv_ref[...] = _dot_nt(h, w_ref[0:UV_W, :])
        qkv_ref[...] = _dot_nt(h, w_ref[UV_W:UV_W + QKV_W, :]).astype(MM)
        z_ref[...] = _dot_nt(h, w_ref[UV_W + QKV_W:IN_WIDTH, :])

        @pl.when(i == fwd_step)
        def _():
            for g in gathers:
                g.forward()

        @pl.when(i == nt - 1)
        def _():
            for g in gathers:
                g.finish()
            for cp in local:
                cp.wait()

    anyspec = pl.BlockSpec(memory_space=pl.ANY)
    return pl.pallas_call(
        body, name="inproj_fwd", grid=(nt,),
        out_shape=(jax.ShapeDtypeStruct((t, UV_W), F32),
                   jax.ShapeDtypeStruct((t, QKV_W), MM),
                   jax.ShapeDtypeStruct((t, Z_W), F32),
                   jax.ShapeDtypeStruct((t, D_MODEL), MM),
                   jax.ShapeDtypeStruct((N_DEV,) + wo_sh.shape, BF16),
                   jax.ShapeDtypeStruct((N_DEV,) + wm_sh.shape, BF16)),
        in_specs=[pl.BlockSpec((tm, D_MODEL), lambda i: (i, 0)),
                  _full((1, D_MODEL)),
                  pl.BlockSpec((IN_WIDTH, D_MODEL), lambda i: (0, 0), pipeline_mode=pl.Buffered(1)),
                  _full(wo_sh.shape), _full(wm_sh.shape)],
        out_specs=(pl.BlockSpec((tm, UV_W), lambda i: (i, 0)),
                   pl.BlockSpec((tm, QKV_W), lambda i: (i, 0)),
                   pl.BlockSpec((tm, Z_W), lambda i: (i, 0)),
                   pl.BlockSpec((tm, D_MODEL), lambda i: (i, 0)),
                   anyspec, anyspec),
        scratch_shapes=[pltpu.VMEM(wo_sh.shape, BF16), pltpu.VMEM(wm_sh.shape, BF16),
                        pltpu.SemaphoreType.DMA((7,)), pltpu.SemaphoreType.DMA((7,)),
                        pltpu.SemaphoreType.DMA((7,)), pltpu.SemaphoreType.DMA((7,)),
                        pltpu.SemaphoreType.DMA((2,))],
        compiler_params=_params(dimension_semantics=("arbitrary",)),
    )(x2, g1, w_in_t, wo_sh, wm_sh)


def _memkv_fwd(mem2, gm, w_mkv):
    tmem = mem2.shape[0]

    def body(m_ref, g_ref, w_ref, o_ref):
        xf = m_ref[...]
        r = lax.rsqrt(jnp.mean(xf * xf, axis=-1, keepdims=True) + EPS)
        hm = (xf * r * g_ref[...]).astype(MM)
        o_ref[...] = _dot(hm, w_ref[...]).astype(MM)

    vm = pl.BlockSpec(memory_space=pltpu.VMEM)
    return pl.pallas_call(
        body, name="memkv_fwd",
        out_shape=jax.ShapeDtypeStruct((tmem, 2 * MEM_LEN), MM),
        in_specs=[vm, vm, vm], out_specs=vm,
        compiler_params=_params(),
    )(mem2, gm, w_mkv)


def _memkv_bwd(dmkv, mem2, gm, w_mkv):
    def body(d_ref, m_ref, g_ref, w_ref, dw_ref, dg_ref):
        xf = m_ref[...]
        r = lax.rsqrt(jnp.mean(xf * xf, axis=-1, keepdims=True) + EPS)
        nm = xf * r
        hm = (nm * g_ref[...]).astype(MM)
        d = d_ref[...].astype(MM)
        dw_ref[...] = _dot_tn(hm, d)
        dhm = _dot_nt(d, w_ref[...])
        dg_ref[...] = jnp.sum(dhm * nm, axis=0, keepdims=True)

    vm = pl.BlockSpec(memory_space=pltpu.VMEM)
    return pl.pallas_call(
        body, name="memkv_bwd",
        out_shape=(jax.ShapeDtypeStruct((D_MODEL, 2 * MEM_LEN), F32),
                   jax.ShapeDtypeStruct((1, D_MODEL), F32)),
        in_specs=[vm, vm, vm, vm], out_specs=(vm, vm),
        compiler_params=_params(),
    )(dmkv, mem2, gm, w_mkv)


def _half_masks(rows):
    lane = lax.broadcasted_iota(jnp.int32, (rows, CHUNK), 1)
    return lane < 64


def _dup_heads(band):
    b32 = band.astype(F32)
    rolled = pltpu.roll(b32, 64, 1)
    lo = _half_masks(band.shape[0])
    return (jnp.where(lo, b32, rolled).astype(MM), jnp.where(lo, rolled, b32).astype(MM))


def _swa_probs(qsel, kd, bias_h, sink_h, first_add):
    s = _dot_nt(qsel, kd) * SCALE + bias_h + first_add
    m = jnp.maximum(jnp.max(s, axis=-1, keepdims=True), sink_h)
    p = jnp.exp(s - m)
    es = jnp.exp(sink_h - m)
    inv = 1.0 / (jnp.sum(p, axis=-1, keepdims=True) + es)
    return p * inv, es * inv


def _softmax(s):
    m = jnp.max(s, axis=-1, keepdims=True)
    p = jnp.exp(s - m)
    return p * (1.0 / jnp.sum(p, axis=-1, keepdims=True))


def _band_rows(n):
    cstart = pl.multiple_of(n * CHUNK, CHUNK)
    pstart = pl.multiple_of(jnp.maximum(n - 1, 0) * CHUNK, CHUNK)
    return pstart, cstart


def _first_block_mask(n):
    col = lax.broadcasted_iota(jnp.int32, (CHUNK, 2 * CHUNK), 1)
    return jnp.where((col < CHUNK) & (n == 0), NEG, 0.0)


def _mix(uv, z, qkv3, mkv3, x2, tgt2, bias, sinks, vg, vb, wt, wtt, bcol, g2, w_o, buckets, tm):
    nb, s = qkv3.shape[0], qkv3.shape[1]
    nt = s // tm
    bpt = tm // CHUNK

    def body(uv_ref, z_ref, qkv_ref, mkv_ref, x_ref, t_ref, bias_ref, sink_ref, vg_ref, vb_ref, wt_ref, wtt_ref,
             bcol_ref, g2_ref, wo_ref, bk_ref,
             duv_ref, dqkv_ref, dz_ref, dxo_ref, dmkv_ref, dwo_ref, dg2_ref, loss_ref, dwsp_ref, dbs_ref,
             dvg_ref, dvb_ref, dsink_ref, drel_ref,
             ycat, dyc, u_s, gu_s, gv_s, xh_s, rs_s, sv_s, vc_s, pb_s, ps_s, pc_s, kd_s, vd_s,
             dkv_acc, dbias_acc, dsv_acc, dsink_acc):
        b, j = pl.program_id(0), pl.program_id(1)
        jt = nt - 1 - j

        @pl.when((b == 0) & (j == 0))
        def _():
            for ref in (dwo_ref, dg2_ref, loss_ref, dwsp_ref, dvg_ref, dvb_ref, dbias_acc, dsv_acc, dsink_acc):
                ref[...] = jnp.zeros_like(ref)

        @pl.when(j == 0)
        def _():
            dmkv_ref[...] = jnp.zeros_like(dmkv_ref)
            dkv_acc[...] = jnp.zeros_like(dkv_acc)

        carry = dkv_acc[0:CHUNK, :]
        dkv_acc[...] = jnp.zeros_like(dkv_acc)
        dkv_acc[tm:tm + CHUNK, :] = carry

        lo = _half_masks(CHUNK)
        lob = _half_masks(2 * CHUNK)
        lot = _half_masks(tm)
        row0 = pl.multiple_of(jt * tm, tm)

        for blk in range(bpt):
            r0 = blk * CHUNK
            rows = slice(r0, r0 + CHUNK)
            n = jt * bpt + blk
            for g in range(A_GROUPS):
                cg = slice(g * CHUNK, (g + 1) * CHUNK)
                u, gu = _gelu_and_grad(uv_ref[rows, cg])
                v, gv = _gelu_and_grad(uv_ref[rows, A_WIDTH + g * CHUNK:A_WIDTH + (g + 1) * CHUNK])
                mu = jnp.mean(v, axis=-1, keepdims=True)
                xc = v - mu
                rstd = lax.rsqrt(jnp.mean(xc * xc, axis=-1, keepdims=True) + EPS)
                xhat = xc * rstd
                vc = (xhat * vg_ref[:, cg] + vb_ref[:, cg]).astype(MM)
                sv = _dot(wt_ref[g], vc) + bcol_ref[g]
                u_s[rows, cg] = u
                gu_s[rows, cg] = gu
                gv_s[rows, cg] = gv
                xh_s[rows, cg] = xhat
                rs_s[rows, cg] = jnp.broadcast_to(rstd, (CHUNK, CHUNK))
                sv_s[rows, cg] = sv
                vc_s[rows, cg] = vc
                ycat[rows, cg] = u * sv
            pstart, cstart = _band_rows(n)
            kd = _dup_heads(jnp.concatenate([qkv_ref[pl.ds(pstart, CHUNK), SK_OFF:SK_OFF + CHUNK],
                                             qkv_ref[pl.ds(cstart, CHUNK), SK_OFF:SK_OFF + CHUNK]], axis=0))
            vd = _dup_heads(jnp.concatenate([qkv_ref[pl.ds(pstart, CHUNK), SV_OFF:SV_OFF + CHUNK],
                                             qkv_ref[pl.ds(cstart, CHUNK), SV_OFF:SV_OFF + CHUNK]], axis=0))
            first_add = _first_block_mask(n)
            for kvh in range(2):
                kd_s[blk * 2 + kvh] = kd[kvh]
                vd_s[blk * 2 + kvh] = vd[kvh]
                q128 = qkv_ref[pl.ds(cstart, CHUNK), SQ_OFF + kvh * CHUNK:SQ_OFF + (kvh + 1) * CHUNK].astype(F32)
                outs = []
                for gi in range(2):
                    h = 2 * kvh + gi
                    qsel = jnp.where(lo if gi == 0 else ~lo, q128, 0.0).astype(MM)
                    probs, ps = _swa_probs(qsel, kd[kvh], bias_ref[h], sink_ref[h], first_add)
                    pb_s[blk * 4 + h] = probs
                    ps_s[blk * 4 + h] = jnp.broadcast_to(ps, (CHUNK, CHUNK))
                    outs.append(_dot(probs.astype(MM), vd[kvh]))
                ycat[rows, YB_OFF + kvh * CHUNK:YB_OFF + (kvh + 1) * CHUNK] = jnp.where(lo, outs[0], outs[1])
        for g in range(2):
            q128 = qkv_ref[pl.ds(row0, tm), MQ_OFF + g * CHUNK:MQ_OFF + (g + 1) * CHUNK].astype(F32)
            k128 = mkv_ref[:, g * CHUNK:(g + 1) * CHUNK]
            v128 = mkv_ref[:, MEM_LEN + g * CHUNK:MEM_LEN + (g + 1) * CHUNK]
            outs = []
            for hh in range(2):
                qsel = jnp.where(lot if hh == 0 else ~lot, q128, 0.0).astype(MM)
                probs = _softmax(_dot_nt(qsel, k128) * SCALE)
                pc_s[2 * g + hh] = probs
                outs.append(_dot(probs.astype(MM), v128))
            ycat[:, YC_OFF + g * CHUNK:YC_OFF + (g + 1) * CHUNK] = jnp.where(lot, outs[0], outs[1])

        zt = z_ref[...]
        sig = 1.0 / (1.0 + jnp.exp(-zt))
        silu = zt * sig
        yc = ycat[...]
        yb = (yc * silu).astype(MM)
        o = _dot(yb, wo_ref[...])
        r2 = lax.rsqrt(jnp.mean(o * o, axis=-1, keepdims=True) + EPS)
        nrm = o * r2
        g2v = g2_ref[...]
        e = x_ref[...] + nrm * g2v - t_ref[...]
        l1 = jnp.sum(e * e, axis=-1, keepdims=True)
        loss_ref[...] += jnp.broadcast_to(jnp.sum(l1, axis=0, keepdims=True) * (0.5 / D_MODEL), loss_ref.shape)
        dxo = e * (1.0 / D_MODEL)
        dxo_ref[...] = dxo
        dg2_ref[...] += jnp.sum(dxo * nrm, axis=0, keepdims=True)
        dn = dxo * g2v
        do = r2 * (dn - nrm * jnp.mean(dn * nrm, axis=-1, keepdims=True))
        dob = do.astype(MM)
        dy = _dot_nt(dob, wo_ref[...])
        dz_ref[...] = (dy * yc * (sig * (1.0 + zt * (1.0 - sig)))).astype(MM)
        dyc[...] = dy * silu
        dwo_ref[...] += _dot_tn(yb, dob)

        for blk in range(bpt):
            r0 = blk * CHUNK
            rows = slice(r0, r0 + CHUNK)
            n = jt * bpt + blk
            for g in range(A_GROUPS):
                cg = slice(g * CHUNK, (g + 1) * CHUNK)
                cv = slice(A_WIDTH + g * CHUNK, A_WIDTH + (g + 1) * CHUNK)
                dya = dyc[rows, cg]
                duv_ref[rows, cg] = (dya * sv_s[rows, cg] * gu_s[rows, cg]).astype(MM)
                dsv = dya * u_s[rows, cg]
                dsvb = dsv.astype(MM)
                dsv_acc[g] += dsv
                dwsp_ref[g] += _dot_nt(dsvb, vc_s[rows, cg])
                dvc = _dot(wtt_ref[g], dsvb)
                xhat = xh_s[rows, cg]
                dvg_ref[:, cg] += jnp.sum(dvc * xhat, axis=0, keepdims=True)
                dvb_ref[:, cg] += jnp.sum(dvc, axis=0, keepdims=True)
                dxh = dvc * vg_ref[:, cg]
                dv = rs_s[rows, cg] * (dxh - jnp.mean(dxh, axis=-1, keepdims=True)
                                       - xhat * jnp.mean(dxh * xhat, axis=-1, keepdims=True))
                duv_ref[rows, cv] = (dv * gv_s[rows, cg]).astype(MM)
            _, cstart = _band_rows(n)
            dk_f, dv_f = [], []
            for kvh in range(2):
                kd = kd_s[blk * 2 + kvh]
                vd = vd_s[blk * 2 + kvh]
                q128 = qkv_ref[pl.ds(cstart, CHUNK), SQ_OFF + kvh * CHUNK:SQ_OFF + (kvh + 1) * CHUNK].astype(F32)
                do128 = dyc[rows, YB_OFF + kvh * CHUNK:YB_OFF + (kvh + 1) * CHUNK]
                dq128 = jnp.zeros((CHUNK, CHUNK), F32)
                dkd = jnp.zeros((2 * CHUNK, CHUNK), F32)
                dvd = jnp.zeros((2 * CHUNK, CHUNK), F32)
                for gi in range(2):
                    h = 2 * kvh + gi
                    half = lo if gi == 0 else ~lo
                    qsel = jnp.where(half, q128, 0.0).astype(MM)
                    dosel = jnp.where(half, do128, 0.0).astype(MM)
                    probs = pb_s[blk * 4 + h]
                    ps = ps_s[blk * 4 + h][:, 0:1]
                    dp = _dot_nt(dosel, vd)
                    delta = jnp.sum(probs * dp, axis=-1, keepdims=True)
                    ds = probs * (dp - delta)
                    dbias_acc[h] += ds
                    dsink_acc[h:h + 1, :] += jnp.broadcast_to(-jnp.sum(ps * delta, axis=0, keepdims=True), (1, CHUNK))
                    dss = (ds * SCALE).astype(MM)
                    dq128 = dq128 + jnp.where(half, _dot(dss, kd), 0.0)
                    dkd = dkd + _dot_tn(dss, qsel)
                    dvd = dvd + _dot_tn(probs.astype(MM), dosel)
                dqkv_ref[rows, SQ_OFF + kvh * CHUNK:SQ_OFF + (kvh + 1) * CHUNK] = dq128.astype(MM)
                dk_f.append(dkd + pltpu.roll(dkd, 64, 1))
                dv_f.append(dvd + pltpu.roll(dvd, 64, 1))
            dkv_acc[r0:r0 + 2 * CHUNK, 0:CHUNK] += jnp.where(lob, dk_f[0], dk_f[1])
            dkv_acc[r0:r0 + 2 * CHUNK, CHUNK:2 * CHUNK] += jnp.where(lob, dv_f[0], dv_f[1])
        dqkv_ref[:, SK_OFF:SK_OFF + 2 * CHUNK] = dkv_acc[CHUNK:CHUNK + tm, :].astype(MM)
        for g in range(2):
            q128 = qkv_ref[pl.ds(row0, tm), MQ_OFF + g * CHUNK:MQ_OFF + (g + 1) * CHUNK].astype(F32)
            k128 = mkv_ref[:, g * CHUNK:(g + 1) * CHUNK]
            v128 = mkv_ref[:, MEM_LEN + g * CHUNK:MEM_LEN + (g + 1) * CHUNK]
            do128 = dyc[:, YC_OFF + g * CHUNK:YC_OFF + (g + 1) * CHUNK]
            dq128 = jnp.zeros((tm, CHUNK), F32)
            dk128 = jnp.zeros((MEM_LEN, CHUNK), F32)
            dv128 = jnp.zeros((MEM_LEN, CHUNK), F32)
            for hh in range(2):
                half = lot if hh == 0 else ~lot
                qsel = jnp.where(half, q128, 0.0).astype(MM)
                dosel = jnp.where(half, do128, 0.0).astype(MM)
                probs = pc_s[2 * g + hh]
                dp = _dot_nt(dosel, v128)
                ds = probs * (dp - jnp.sum(probs * dp, axis=-1, keepdims=True))
                dss = (ds * SCALE).astype(MM)
                dq128 = dq128 + jnp.where(half, _dot(dss, k128), 0.0)
                dk128 = dk128 + _dot_tn(dss, qsel)
                dv128 = dv128 + _dot_tn(probs.astype(MM), dosel)
            dqkv_ref[:, MQ_OFF + g * CHUNK:MQ_OFF + (g + 1) * CHUNK] = dq128.astype(MM)
            dmkv_ref[:, g * CHUNK:(g + 1) * CHUNK] += dk128
            dmkv_ref[:, MEM_LEN + g * CHUNK:MEM_LEN + (g + 1) * CHUNK] += dv128

        @pl.when((b == nb - 1) & (j == nt - 1))
        def _():
            r = lax.broadcasted_iota(jnp.int32, (CHUNK, CHUNK), 0)
            c = lax.broadcasted_iota(jnp.int32, (CHUNK, CHUNK), 1)
            for g in range(A_GROUPS):
                dwsp_ref[g] = jnp.where(r >= c, dwsp_ref[g], 0.0)
                dbs_ref[g:g + 1, :] = jnp.sum(dsv_acc[g].T, axis=0, keepdims=True)
            rows8 = lax.broadcasted_iota(jnp.int32, (8, CHUNK), 0)
            cols8 = lax.broadcasted_iota(jnp.int32, (8, CHUNK), 1)
            sk = jnp.zeros((8, CHUNK), F32)
            for h in range(4):
                sk = sk + jnp.where((rows8 == 0) & (cols8 == h),
                                    jnp.broadcast_to(dsink_acc[h:h + 1, :], (8, CHUNK)), 0.0)
            dsink_ref[...] = sk
            bk = bk_ref[...]
            valid = _window_valid()
            rrow = lax.broadcasted_iota(jnp.int32, (N_BUCKETS, CHUNK), 0)
            rcol = lax.broadcasted_iota(jnp.int32, (N_BUCKETS, CHUNK), 1)
            acc = jnp.zeros((N_BUCKETS, CHUNK), F32)
            for bb in range(N_BUCKETS):
                hit = (bk == bb) & valid
                for h in range(4):
                    part = jnp.sum(jnp.where(hit, dbias_acc[h], 0.0), axis=-1, keepdims=True)
                    tot = jnp.sum(part, axis=0, keepdims=True)
                    acc = acc + jnp.where((rrow == bb) & (rcol == h), jnp.broadcast_to(tot, (N_BUCKETS, CHUNK)), 0.0)
            drel_ref[...] = acc

    t = nb * s
    tile = lambda w: pl.BlockSpec((tm, w), lambda b, j: (b * nt + nt - 1 - j, 0))
    per_batch = lambda r, w: pl.BlockSpec((None, r, w), lambda b, j: (b, 0, 0))
    grp = (A_GROUPS, CHUNK, CHUNK)
    return pl.pallas_call(
        body, name="mix", grid=(nb, nt),
        out_shape=(jax.ShapeDtypeStruct((t, UV_W), MM),
                   jax.ShapeDtypeStruct((t, QKV_W), MM),
                   jax.ShapeDtypeStruct((t, Z_W), MM),
                   jax.ShapeDtypeStruct((t, D_MODEL), F32),
                   jax.ShapeDtypeStruct((nb, MEM_LEN, 2 * MEM_LEN), F32),
                   jax.ShapeDtypeStruct((D_MODEL, D_MODEL), F32),
                   jax.ShapeDtypeStruct((1, D_MODEL), F32),
                   jax.ShapeDtypeStruct((8, CHUNK), F32),
                   jax.ShapeDtypeStruct(grp, F32),
                   jax.ShapeDtypeStruct((A_GROUPS, CHUNK), F32),
                   jax.ShapeDtypeStruct((1, A_WIDTH), F32),
                   jax.ShapeDtypeStruct((1, A_WIDTH), F32),
                   jax.ShapeDtypeStruct((8, CHUNK), F32),
                   jax.ShapeDtypeStruct((N_BUCKETS, CHUNK), F32)),
        in_specs=[tile(UV_W), tile(Z_W), per_batch(s, QKV_W), per_batch(MEM_LEN, 2 * MEM_LEN),
                  tile(D_MODEL), tile(D_MODEL),
                  _full((4, CHUNK, 2 * CHUNK)),
                  pl.BlockSpec(memory_space=pltpu.SMEM),
                  _full((1, A_WIDTH)), _full((1, A_WIDTH)),
                  _full(grp), _full(grp), _full(grp),
                  _full((1, D_MODEL)), _full((D_MODEL, D_MODEL)), _full((CHUNK, 2 * CHUNK))],
        out_specs=(tile(UV_W), tile(QKV_W), tile(Z_W), tile(D_MODEL), per_batch(MEM_LEN, 2 * MEM_LEN),
                   _full((D_MODEL, D_MODEL)), _full((1, D_MODEL)), _full((8, CHUNK)),
                   _full(grp), _full((A_GROUPS, CHUNK)), _full((1, A_WIDTH)), _full((1, A_WIDTH)),
                   _full((8, CHUNK)), _full((N_BUCKETS, CHUNK))),
        scratch_shapes=[pltpu.VMEM((tm, D_MODEL), F32), pltpu.VMEM((tm, D_MODEL), F32)]
                       + [pltpu.VMEM((tm, A_WIDTH), F32) for _ in range(6)]
                       + [pltpu.VMEM((tm, A_WIDTH), MM),
                          pltpu.VMEM((bpt * 4, CHUNK, 2 * CHUNK), F32),
                          pltpu.VMEM((bpt * 4, CHUNK, CHUNK), F32),
                          pltpu.VMEM((4, tm, MEM_LEN), F32),
                          pltpu.VMEM((bpt * 2, 2 * CHUNK, CHUNK), MM),
                          pltpu.VMEM((bpt * 2, 2 * CHUNK, CHUNK), MM),
                          pltpu.VMEM((tm + CHUNK, 2 * CHUNK), F32),
                          pltpu.VMEM((4, CHUNK, 2 * CHUNK), F32),
                          pltpu.VMEM(grp, F32),
                          pltpu.VMEM((8, CHUNK), F32)],
        compiler_params=_params(dimension_semantics=("arbitrary", "arbitrary")),
    )(uv, z, qkv3, mkv3, x2, tgt2, bias, sinks, vg, vb, wt, wtt, bcol, g2, w_o, buckets)


class _ShardReduce:
    def __init__(self, pos, g, bufs, sems, owner_x):
        self.x, self.y, self.c = pos
        self.g = g
        self.own, self.rcv, self.sbuf, self.rbuf = bufs
        self.ld, self.sa, self.ra, self.sb, self.rb = sems
        self.owner_x = owner_x
        self.nq = 4 if owner_x is None else 2
        self.nrow = g.shape[1]
        self.here = (self.x, self.y, self.c)
        self.sib = (self.x, self.y, 1 - self.c)
        self.chips = _other_chips(self.x, self.y)

    def _load(self, q):
        return pltpu.make_async_copy(self.g.at[2 * q + self.c], self.own.at[q], self.ld.at[q])

    def _to_sib(self, q, to):
        return _remote(self.g.at[2 * q + 1 - self.c], self.rcv.at[q], self.sa.at[q], self.ra.at[q], to)

    def _sends(self):
        x, y = self.x, self.y
        if self.owner_x is None:
            return [(None, j, j, chip, 2 * chip[0] + chip[1]) for j, chip in enumerate(self.chips)]
        mine = x == self.owner_x
        return [(mine, 0, 1, self.chips[1], 1 - y),
                (~mine, 0, 0, self.chips[0], y),
                (~mine, 1, 2, self.chips[2], 1 - y)]

    def _to_chip(self, k, j, chip, to_core):
        return _remote(self.sbuf.at[k], self.rbuf.at[j], self.sb.at[k], self.rb.at[j], (chip[0], chip[1], to_core))

    def _when(self, cond, fn):
        if cond is None:
            fn()
        else:
            pl.when(cond)(fn)

    def start(self):
        for q in range(self.nq):
            self._load(q).start()
            self._to_sib(q, self.sib).start()

    def mid(self):
        for q in range(self.nq):
            self._load(q).wait()
            self._to_sib(q, self.here).wait_recv()

        def add(r):
            for q in range(self.nq):
                self.rcv[q, r, :] = self.rcv[q, r, :] + self.own[q, r, :]

        _rows_loop(self.nrow, add)
        for cond, k, j, chip, which in self._sends():
            def send(k=k, j=j, chip=chip, which=which):
                def cast(r):
                    self.sbuf[k, r, :] = self.rcv[which, r, :].astype(BF16)

                _rows_loop(self.nrow, cast)
                self._to_chip(k, j, chip, self.c).start()

            self._when(cond, send)

    def finish(self, out):
        mine = None if self.owner_x is None else self.x == self.owner_x
        which = (2 * self.x + self.y) if self.owner_x is None else self.y

        def total():
            for j in range(3):
                self._to_chip(0, j, self.chips[j], self.c).wait_recv()

            def tot(r):
                g = self.rcv[which, r, :]
                for j in range(3):
                    g = g + self.rbuf[j, r, :].astype(F32)
                out[r, :] = g

            _rows_loop(self.nrow, tot)

        self._when(mine, total)
        if mine is not None:
            def zero():
                def z(r):
                    out[r, :] = jnp.zeros((_ROWS, out.shape[1]), F32)

                _rows_loop(self.nrow, z)

            pl.when(~mine)(zero)
        for q in range(self.nq):
            self._to_sib(q, self.sib).wait_send()
        for cond, k, j, chip, _ in self._sends():
            self._when(cond, lambda k=k, j=j, chip=chip: self._to_chip(k, j, chip, self.c).wait_send())


def _reduce_scratch(shape, nq, nsend):
    return [pltpu.VMEM((nq,) + shape, F32), pltpu.VMEM((nq,) + shape, F32),
            pltpu.VMEM((nsend,) + shape, BF16), pltpu.VMEM((3,) + shape, BF16),
            pltpu.SemaphoreType.DMA((nq,)), pltpu.SemaphoreType.DMA((nq,)), pltpu.SemaphoreType.DMA((nq,)),
            pltpu.SemaphoreType.DMA((3,)), pltpu.SemaphoreType.DMA((3,))]


def _inproj_bwd_x(x2, dxo, duv, dqkv, dz, g1, w_in_t, tm):
    t = x2.shape[0]
    nt = t // tm

    def body(x_ref, dxo_ref, duv_ref, dqkv_ref, dz_ref, g_ref, w_ref, gx_ref, dg_ref):
        @pl.when(pl.program_id(0) == 0)
        def _():
            dg_ref[...] = jnp.zeros_like(dg_ref)

        xf = x_ref[...]
        r = lax.rsqrt(jnp.mean(xf * xf, axis=-1, keepdims=True) + EPS)
        nx = xf * r
        gv = g_ref[...]
        dh = (_dot(duv_ref[...], w_ref[0:UV_W, :]) + _dot(dqkv_ref[...], w_ref[UV_W:UV_W + QKV_W, :])
              + _dot(dz_ref[...], w_ref[UV_W + QKV_W:IN_WIDTH, :]))
        dg_ref[...] += jnp.sum(dh * nx, axis=0, keepdims=True)
        dnx = dh * gv
        gx_ref[...] = dxo_ref[...] + r * (dnx - nx * jnp.mean(dnx * nx, axis=-1, keepdims=True))

    tile = lambda w: pl.BlockSpec((tm, w), lambda i: (i, 0))
    return pl.pallas_call(
        body, name="inproj_bwd_x", grid=(nt,),
        out_shape=(jax.ShapeDtypeStruct((t, D_MODEL), F32), jax.ShapeDtypeStruct((1, D_MODEL), F32)),
        in_specs=[tile(D_MODEL), tile(D_MODEL), tile(UV_W), tile(QKV_W), tile(Z_W),
                  _full((1, D_MODEL)),
                  pl.BlockSpec((IN_WIDTH, D_MODEL), lambda i: (0, 0), pipeline_mode=pl.Buffered(1))],
        out_specs=(tile(D_MODEL), _full((1, D_MODEL))),
        compiler_params=_params(dimension_semantics=("arbitrary",)),
    )(x2, dxo, duv, dqkv, dz, g1, w_in_t)


def _inproj_bwd_w(hb, da, db, reds, tm, name):
    t = hb.shape[0]
    nt = t // tm
    mid_step = min(1, nt - 1)
    n_red = 9
    nr = len(reds)
    wa_, wb_ = da[1], db[1]
    red_shapes = [g.shape[1:] for g, _ in reds]

    def body(*refs):
        h_ref, da_ref, db_ref = refs[0:3]
        g_refs = refs[3:3 + nr]
        dw_hbm = refs[3 + nr]
        t_refs = refs[4 + nr:4 + 2 * nr]
        acc, sem = refs[4 + 2 * nr:6 + 2 * nr]
        red = refs[6 + 2 * nr:]
        i = pl.program_id(0)
        pos = _position()
        reducers = [_ShardReduce(pos, g_refs[k], red[k * n_red:k * n_red + 4], red[k * n_red + 4:(k + 1) * n_red],
                                 reds[k][1]) for k in range(nr)]

        @pl.when(i == 0)
        def _():
            acc[...] = jnp.zeros_like(acc)
            for rd in reducers:
                rd.start()

        @pl.when(i == mid_step)
        def _():
            for rd in reducers:
                rd.mid()

        h = h_ref[...]
        acc[0:wa_, :] += _dot_tn(da_ref[...], h)
        acc[wa_:wa_ + wb_, :] += _dot_tn(db_ref[...], h)

        @pl.when(i == nt - 1)
        def _():
            cp = pltpu.make_async_copy(acc, dw_hbm, sem)
            cp.start()
            for rd, out in zip(reducers, t_refs):
                rd.finish(out)
            cp.wait()

    anyspec = pl.BlockSpec(memory_space=pl.ANY)
    scratch = [pltpu.VMEM((HALF_IN, D_MODEL), F32), pltpu.SemaphoreType.DMA]
    for (g, ox), shp in zip(reds, red_shapes):
        scratch += _reduce_scratch(shp, 4 if ox is None else 2, 3 if ox is None else 2)
    return pl.pallas_call(
        body, name=name, grid=(nt,),
        out_shape=tuple([jax.ShapeDtypeStruct((HALF_IN, D_MODEL), F32)]
                        + [jax.ShapeDtypeStruct(shp, F32) for shp in red_shapes]),
        in_specs=[pl.BlockSpec((tm, D_MODEL), lambda i: (i, 0)),
                  pl.BlockSpec((tm, wa_), lambda i: (i, da[2])),
                  pl.BlockSpec((tm, wb_), lambda i: (i, db[2]))] + [anyspec] * nr,
        out_specs=tuple([anyspec] + [_full(shp) for shp in red_shapes]),
        scratch_shapes=scratch,
        compiler_params=_params(dimension_semantics=("arbitrary",)),
    )(hb, da[0], db[0], *[g for g, _ in reds])


_S_LAYOUT = (((1, D_MODEL), 0), ((1, D_MODEL), 8), ((1, D_MODEL), 16),
             ((1, A_WIDTH), 24), ((1, A_WIDTH), 28), ((A_GROUPS, CHUNK), 32),
             ((1, 4), 36), ((N_BUCKETS, 4), 40),
             ((A_GROUPS * CHUNK, CHUNK), 72))
_LOSS_ROW = 37
_S_ROWS = 72 + A_GROUPS * CHUNK
_N_SMALL = len(_S_LAYOUT)


def _pack_rows(dst, refs):
    for (shp, r0), ref in zip(_S_LAYOUT, refs):
        if shp[0] == 1 and shp[1] >= CHUNK:
            for i in range(shp[1] // CHUNK):
                dst[r0 + i:r0 + i + 1, :] = ref[:, i * CHUNK:(i + 1) * CHUNK]
        elif ref.shape[-1] == CHUNK:
            dst[r0:r0 + shp[0], :] = ref[0:shp[0], :]
        else:
            dst[r0:r0 + shp[0], 0:shp[1]] = ref[...]


def _unpack_rows(src, refs):
    for (shp, r0), ref in zip(_S_LAYOUT, refs):
        if shp[0] == 1 and shp[1] >= CHUNK:
            for i in range(shp[1] // CHUNK):
                ref[:, i * CHUNK:(i + 1) * CHUNK] = src[r0 + i:r0 + i + 1, :]
        elif shp[1] == CHUNK:
            ref[...] = src[r0:r0 + shp[0], :]
        else:
            ref[...] = src[r0:r0 + shp[0], 0:shp[1]]


def _greduce(dwb, small_g, loss_p):
    sh_a = dwb.shape[1:]
    rs = _S_ROWS
    n_red = 9

    def body(*refs):
        it = iter(refs)
        take = lambda n: [next(it) for _ in range(n)]
        dwb_ref, = take(1)
        sg_refs = take(_N_SMALL)
        loss_ref, = take(1)
        ta_ref, ogs = take(2)
        red = take(n_red)
        gs_ref, rs_a, rs_b = take(3)
        ssem_a, rsem_a, ssem_b, rsem_b = take(4)

        pos = _position()
        x, y, cc = pos
        myq = 2 * x + y
        here, sib = (x, y, cc), (x, y, 1 - cc)
        chips = _other_chips(x, y)
        rd = _ShardReduce(pos, dwb_ref, red[0:4], red[4:n_red], 1)
        rd.start()

        gs_ref[...] = jnp.zeros_like(gs_ref)
        _pack_rows(gs_ref, sg_refs)
        gs_ref[_LOSS_ROW:_LOSS_ROW + 1, :] = loss_ref[0:1, :]
        small_a = _remote(gs_ref, rs_a, ssem_a, rsem_a, sib)
        small_a.start()

        rd.mid()

        _remote(gs_ref, rs_a, ssem_a, rsem_a, here).wait_recv()
        rs_b[myq] = gs_ref[...] + rs_a[...]
        small_b = [_remote(rs_b.at[myq], rs_b.at[myq], ssem_b.at[j], rsem_b.at[j], (chip[0], chip[1], cc))
                   for j, chip in enumerate(chips)]
        for cp in small_b:
            cp.start()
        for j in range(3):
            _remote(rs_b.at[myq], rs_b.at[myq], ssem_b.at[j], rsem_b.at[j], here).wait_recv()

        def tot_s(i, _):
            r = pl.ds(pl.multiple_of(i * 8, 8), 8)
            ogs[r, :] = ((rs_b[0, r, :] + rs_b[1, r, :]) + rs_b[2, r, :]) + rs_b[3, r, :]
            return 0

        lax.fori_loop(0, rs // 8, tot_s, 0)

        rd.finish(ta_ref)
        small_a.wait_send()
        for cp in small_b:
            cp.wait_send()

    vm = pl.BlockSpec(memory_space=pltpu.VMEM)
    anyspec = pl.BlockSpec(memory_space=pl.ANY)
    return pl.pallas_call(
        body, name="greduce",
        out_shape=(jax.ShapeDtypeStruct(sh_a, F32), jax.ShapeDtypeStruct((rs, CHUNK), F32)),
        in_specs=[anyspec] + [vm] * (_N_SMALL + 1),
        out_specs=(vm, vm),
        scratch_shapes=(_reduce_scratch(sh_a, 2, 2)
                        + [pltpu.VMEM((rs, CHUNK), F32), pltpu.VMEM((rs, CHUNK), F32), pltpu.VMEM((4, rs, CHUNK), F32),
                           pltpu.SemaphoreType.DMA, pltpu.SemaphoreType.DMA,
                           pltpu.SemaphoreType.DMA((3,)), pltpu.SemaphoreType.DMA((3,))]),
        compiler_params=_params(),
    )(dwb, *small_g, loss_p)


def _adamw(w, g, m, v):
    m = ADAM_B1 * m + (1.0 - ADAM_B1) * g
    v = ADAM_B2 * v + (1.0 - ADAM_B2) * (g * g)
    m_hat = m / (1.0 - ADAM_B1 ** ADAM_STEP)
    v_hat = v / (1.0 - ADAM_B2 ** ADAM_STEP)
    delta = -ADAM_LR * (m_hat / (jnp.sqrt(v_hat) + ADAM_EPS) + ADAM_WD * w)
    return delta, m, v


def _update(ta0, ta1, tb, tc, ts, big_wmv, small_wmv):
    shapes = (ta0.shape, tb.shape, tc.shape)
    rs = _S_ROWS
    small_shapes = [tuple(a.shape) for a in small_wmv[0]]

    def body(*refs):
        it = iter(refs)
        take = lambda n: [next(it) for _ in range(n)]
        ga0_ref, ga1_ref, gb_ref, gc_ref, gs_ref = take(5)
        wa, ma, va, wb, mb, vb_, wc, mc, vc = take(9)
        sw_refs, sm_refs, sv_refs = take(_N_SMALL), take(_N_SMALL), take(_N_SMALL)
        oga, oda, oma, ova, ogb, odb, omb, ovb, ogc, odc, omc, ovc = take(12)
        so_refs = [take(_N_SMALL) for _ in range(4)]
        loss_out, = take(1)
        ws, ms, vs, ods, oms, ovs = take(6)

        for buf in (ws, ms, vs):
            buf[...] = jnp.zeros_like(buf)
        _pack_rows(ws, sw_refs)
        _pack_rows(ms, sm_refs)
        _pack_rows(vs, sv_refs)

        big = ((lambda r: ga0_ref[r, :] + ga1_ref[r, :], wa, ma, va, oga, oda, oma, ova),
               (lambda r: gb_ref[r, :], wb, mb, vb_, ogb, odb, omb, ovb),
               (lambda r: gc_ref[r, :], wc, mc, vc, ogc, odc, omc, ovc))
        for arr in range(3):
            grad, w_r, m_r, v_r, og, od, om, ov = big[arr]

            def upd(r, grad=grad, w_r=w_r, m_r=m_r, v_r=v_r, og=og, od=od, om=om, ov=ov):
                g = grad(r)
                d, m, v = _adamw(w_r[r, :], g, m_r[r, :], v_r[r, :])
                og[r, :] = g
                od[r, :] = d
                om[r, :] = m
                ov[r, :] = v

            _rows_loop(shapes[arr][0], upd)

        def upd_s(i, _):
            r = pl.ds(pl.multiple_of(i * 8, 8), 8)
            d, m, v = _adamw(ws[r, :], gs_ref[r, :], ms[r, :], vs[r, :])
            ods[r, :] = d
            oms[r, :] = m
            ovs[r, :] = v
            return 0

        lax.fori_loop(0, rs // 8, upd_s, 0)
        for k, buf in enumerate((gs_ref, ods, oms, ovs)):
            _unpack_rows(buf, so_refs[k])
        loss_out[...] = gs_ref[_LOSS_ROW:_LOSS_ROW + 1, 0:1]

    vm = pl.BlockSpec(memory_space=pltpu.VMEM)
    big_out = []
    for shp in shapes:
        big_out += [jax.ShapeDtypeStruct(shp, F32)] * 4
    small_out = [jax.ShapeDtypeStruct(shp, F32) for shp in small_shapes] * 4
    out_shape = tuple(big_out + small_out + [jax.ShapeDtypeStruct((1, 1), F32)])
    n_in = 5 + 9 + 3 * _N_SMALL
    return pl.pallas_call(
        body, name="update",
        out_shape=out_shape,
        in_specs=[vm] * n_in,
        out_specs=tuple([vm] * len(out_shape)),
        scratch_shapes=[pltpu.VMEM((rs, CHUNK), F32) for _ in range(6)],
        compiler_params=_params(),
    )(ta0, ta1, tb, tc, ts, *big_wmv, *small_wmv[0], *small_wmv[1], *small_wmv[2])


def _local_step(x, mem, loss_target, pre_norm_g, post_norm_g, mem_norm_g, v_norm_g, v_norm_b, w_spatial, b_spatial,
                attn_sinks, rel_bias, w_in_t, wo_sh, wm_sh):
    nb, s, _ = x.shape
    t = nb * s
    x2 = x.reshape(t, D_MODEL)
    tgt2 = loss_target.reshape(t, D_MODEL)
    mem2 = mem.reshape(nb * MEM_LEN, D_MODEL)
    tm_mix = min(256, s)
    tm_proj = min(512, t)

    buckets = jnp.asarray(_t5_buckets())
    sinks = attn_sinks.reshape(4)
    bias, wt, wtt, bcol = _prep(rel_bias, w_spatial[0], b_spatial[0], buckets)

    uv, qkv, z, hb, wo_all, wm_all = _inproj_fwd(x2, pre_norm_g, w_in_t, wo_sh, wm_sh, tm_proj)
    w_o = wo_all.reshape(D_MODEL, D_MODEL)
    w_mkv = wm_all.reshape(D_MODEL, 2 * MEM_LEN)
    mkv = _memkv_fwd(mem2, mem_norm_g, w_mkv)
    qkv3 = qkv.reshape(nb, s, QKV_W)
    mkv3 = mkv.reshape(nb, MEM_LEN, 2 * MEM_LEN)
    duv, dqkv, dz, dxo, dmkv, dwo, dg2, loss_p, dwsp, dbs, dvg, dvb, dsink, drel = _mix(
        uv, z, qkv3, mkv3, x2, tgt2, bias, sinks, v_norm_g, v_norm_b, wt, wtt, bcol, post_norm_g, w_o, buckets, tm_mix)
    dwmkv, dgm = _memkv_bwd(dmkv.reshape(nb * MEM_LEN, 2 * MEM_LEN), mem2, mem_norm_g, w_mkv)
    gx, dg1 = _inproj_bwd_x(x2, dxo, duv, dqkv, dz, pre_norm_g, w_in_t, tm_proj)
    dwa, tb, tc = _inproj_bwd_w(hb, (duv, UV_W, 0), (dqkv, QK_W, 0),
                                [(dwo.reshape(N_DEV, SHARD_O, D_MODEL), None),
                                 (dwmkv.reshape(N_DEV, SHARD_O, 2 * MEM_LEN), None)], tm_proj, "inproj_bwd_w1")
    dwb, ta0 = _inproj_bwd_w(hb, (dqkv, QKV_W - QK_W, 1), (dz, Z_W, 0),
                             [(dwa.reshape(N_DEV // 2, SHARD_IN, D_MODEL), 0)], tm_proj, "inproj_bwd_w2")
    small = [dg1, dg2, dgm, dvg, dvb, dbs, dsink, drel, dwsp.reshape(A_GROUPS * CHUNK, CHUNK)]
    ta1, ts = _greduce(dwb.reshape(N_DEV // 2, SHARD_IN, D_MODEL), small, loss_p)
    return gx.reshape(nb, s, D_MODEL), ta0, ta1, tb, tc, ts


def kernel(x, mem, pre_norm_g, post_norm_g, mem_norm_g, w_in, w_mem_kv, v_norm_g, v_norm_b, w_spatial, b_spatial, attn_sinks, rel_bias, w_out, loss_target, m_pre_norm_g, m_post_norm_g, m_mem_norm_g, m_w_in, m_w_mem_kv, m_v_norm_g, m_v_norm_b, m_w_spatial, m_b_spatial, m_attn_sinks, m_rel_bias, m_w_out, v_pre_norm_g, v_post_norm_g, v_mem_norm_g, v_w_in, v_w_mem_kv, v_v_norm_g, v_v_norm_b, v_w_spatial, v_b_spatial, v_attn_sinks, v_rel_bias, v_w_out):
    sh_a = (w_in[0].T, m_w_in[0].T, v_w_in[0].T)
    sh_b = (w_out[0], m_w_out[0], v_w_out[0])
    sh_c = (w_mem_kv[0], m_w_mem_kv[0], v_w_mem_kv[0])
    w_in_t = _wgather(sh_a[0]).reshape(IN_WIDTH, D_MODEL)

    gx, ta0, ta1, tb, tc, ts = _local_step(
        x, mem, loss_target, pre_norm_g, post_norm_g, mem_norm_g, v_norm_g, v_norm_b, w_spatial, b_spatial,
        attn_sinks, rel_bias, w_in_t, sh_b[0], sh_c[0])

    small_names = ["pre_norm_g", "post_norm_g", "mem_norm_g", "v_norm_g", "v_norm_b", "b_spatial", "attn_sinks",
                   "rel_bias", "w_spatial"]
    given = dict(pre_norm_g=(pre_norm_g, m_pre_norm_g, v_pre_norm_g), post_norm_g=(post_norm_g, m_post_norm_g, v_post_norm_g),
                 mem_norm_g=(mem_norm_g, m_mem_norm_g, v_mem_norm_g), v_norm_g=(v_norm_g, m_v_norm_g, v_v_norm_g),
                 v_norm_b=(v_norm_b, m_v_norm_b, v_v_norm_b), b_spatial=(b_spatial, m_b_spatial, v_b_spatial),
                 attn_sinks=(attn_sinks, m_attn_sinks, v_attn_sinks), rel_bias=(rel_bias, m_rel_bias, v_rel_bias),
                 w_spatial=(w_spatial, m_w_spatial, v_w_spatial))
    small_wmv = [[given[n][k].reshape(shp) for n, (shp, _) in zip(small_names, _S_LAYOUT)] for k in range(3)]

    outs = _update(ta0, ta1, tb, tc, ts, (*sh_a, *sh_b, *sh_c), small_wmv)
    ra, rb, rc = outs[0:4], outs[4:8], outs[8:12]
    loss = outs[12 + 4 * _N_SMALL].reshape(())

    res = {}
    for k, kind in enumerate(("grad", "delta", "new_m", "new_v")):
        res[kind, "w_in"] = ra[k].T[None]
        res[kind, "w_out"] = rb[k][None]
        res[kind, "w_mem_kv"] = rc[k][None]
        for i, n in enumerate(small_names):
            res[kind, n] = outs[12 + k * _N_SMALL + i].reshape(given[n][0].shape)
    order = ["pre_norm_g", "post_norm_g", "mem_norm_g", "w_in", "w_mem_kv", "v_norm_g", "v_norm_b", "w_spatial",
             "b_spatial", "attn_sinks", "rel_bias", "w_out"]
    flat = [res[kind, n] for kind in ("grad", "delta", "new_m", "new_v") for n in order]
    return (loss, gx, *flat)
```

```python
import numpy as np
import jax
import jax.numpy as jnp
from jax import lax
from jax.experimental import pallas as pl
from jax.experimental.pallas import tpu as pltpu

F32 = jnp.float32
BF16 = jnp.bfloat16
MM = jnp.bfloat16

D_MODEL = 1024
CHUNK = 128
A_GROUPS = 4
A_WIDTH = 512
UV_W = 1024
QKV_W = 768
Z_W = 1024
IN_WIDTH = UV_W + QKV_W + Z_W
MEM_LEN = 256
N_BUCKETS = 32
MAX_DISTANCE = 128
EPS = 1e-6
NEG = -1e30
SCALE = 0.125
N_DEV = 8
SHARD_IN = IN_WIDTH // N_DEV
SHARD_O = D_MODEL // N_DEV

SQ_COL, SK_COL, SV_COL, MQ_COL, Z_COL = UV_W, UV_W + 256, UV_W + 384, UV_W + 512, UV_W + QKV_W
YB_OFF, YC_OFF = 512, 768

ADAM_LR = 0.001
ADAM_B1 = 0.9
ADAM_B2 = 0.999
ADAM_EPS = 1e-08
ADAM_WD = 0.01
ADAM_STEP = 10

VMEM_LIMIT = 60 * 1024 * 1024

_GELU_C = 0.7978845608028654
_GELU_A = 0.044715

MESH = pl.DeviceIdType.MESH
_ROWS = 32


def _dot(a, b):
    return lax.dot_general(a, b, (((1,), (0,)), ((), ())), preferred_element_type=F32)


def _dot_nt(a, b):
    return lax.dot_general(a, b, (((1,), (1,)), ((), ())), preferred_element_type=F32)


def _dot_tn(a, b):
    return lax.dot_general(a, b, (((0,), (0,)), ((), ())), preferred_element_type=F32)


def _gelu_and_grad(x):
    x2 = x * x
    t = jnp.tanh(_GELU_C * (x + _GELU_A * x * x2))
    g = 0.5 * x * (1.0 + t)
    dg = 0.5 * (1.0 + t) + 0.5 * x * (1.0 - t * t) * (_GELU_C * (1.0 + 3.0 * _GELU_A * x2))
    return g, dg


def _t5_buckets():
    qi = np.arange(CHUNK)[:, None]
    kj = np.arange(2 * CHUNK)[None, :]
    n = np.maximum(qi + CHUNK - kj, 0)
    max_exact = N_BUCKETS // 2
    large = max_exact + (np.log(np.maximum(n, 1) / max_exact) / np.log(MAX_DISTANCE / max_exact)
                         * (N_BUCKETS - max_exact)).astype(np.int32)
    large = np.minimum(large, N_BUCKETS - 1)
    return np.where(n < max_exact, n, large).astype(np.int32)


def _params(**kw):
    return pltpu.CompilerParams(vmem_limit_bytes=VMEM_LIMIT, **kw)


def _full(shape, single=False):
    nd = len(shape)
    if single:
        return pl.BlockSpec(shape, lambda *_: (0,) * nd, pipeline_mode=pl.Buffered(1))
    return pl.BlockSpec(shape, lambda *_: (0,) * nd)


def _window_valid():
    qi = lax.broadcasted_iota(jnp.int32, (CHUNK, 2 * CHUNK), 0)
    kj = lax.broadcasted_iota(jnp.int32, (CHUNK, 2 * CHUNK), 1)
    dist = qi + CHUNK - kj
    return (dist >= 0) & (dist < CHUNK)


def _position():
    return lax.axis_index("x"), lax.axis_index("y"), lax.axis_index("c")


def _other_chips(x, y):
    return [(1 - x, y), (x, 1 - y), (1 - x, 1 - y)]


def _remote(src, dst, ssem, rsem, to):
    return pltpu.make_async_remote_copy(src_ref=src, dst_ref=dst, send_sem=ssem, recv_sem=rsem,
                                        device_id=to, device_id_type=MESH)


def _rows_loop(nrow, fn):
    def step(i, _):
        fn(pl.ds(pl.multiple_of(i * _ROWS, _ROWS), _ROWS))
        return 0

    lax.fori_loop(0, nrow // _ROWS, step, 0)


class _Gather:
    def __init__(self, pos, out, ssem, rsem):
        self.x, self.y, self.c = pos
        self.out, self.ssem, self.rsem = out, ssem, rsem
        self.me = 4 * self.x + 2 * self.y + self.c
        self.here = (self.x, self.y, self.c)
        self.sib = (self.x, self.y, 1 - self.c)
        self.chips = _other_chips(self.x, self.y)

    def _copy(self, k, blk, to):
        r = self.out.at[blk]
        return _remote(r, r, self.ssem.at[k], self.rsem.at[k], to)

    def _idx(self, chip, core):
        return 4 * chip[0] + 2 * chip[1] + core

    def start(self):
        self._copy(0, self.me, self.sib).start()
        for j, chip in enumerate(self.chips):
            self._copy(1 + j, self.me, (chip[0], chip[1], self.c)).start()

    def forward(self):
        for j, chip in enumerate(self.chips):
            self._copy(1 + j, self._idx(chip, self.c), self.here).wait_recv()
            self._copy(4 + j, self._idx(chip, self.c), self.sib).start()

    def finish(self):
        self._copy(0, self._idx((self.x, self.y), 1 - self.c), self.here).wait_recv()
        for j, chip in enumerate(self.chips):
            self._copy(4 + j, self._idx(chip, 1 - self.c), self.here).wait_recv()
        self._copy(0, self.me, self.sib).wait_send()
        for j, chip in enumerate(self.chips):
            self._copy(1 + j, self.me, (chip[0], chip[1], self.c)).wait_send()
            self._copy(4 + j, self._idx(chip, self.c), self.sib).wait_send()


def _wgather(a, b, c):
    def body(a_ref, b_ref, c_ref, oa, ob, oc, ssem, rsem):
        pos = _position()
        me = 4 * pos[0] + 2 * pos[1] + pos[2]
        gathers = []
        for k, (src, out) in enumerate(((c_ref, oc), (b_ref, ob), (a_ref, oa))):
            out[me] = src[...].astype(BF16)
            g = _Gather(pos, out, ssem.at[k], rsem.at[k])
            g.start()
            gathers.append(g)
        for g in gathers:
            g.forward()
        for g in gathers:
            g.finish()

    vm = pl.BlockSpec(memory_space=pltpu.VMEM)
    return pl.pallas_call(
        body, name="wgather",
        out_shape=(jax.ShapeDtypeStruct((N_DEV,) + a.shape, BF16),
                   jax.ShapeDtypeStruct((N_DEV,) + b.shape, BF16),
                   jax.ShapeDtypeStruct((N_DEV,) + c.shape, BF16)),
        in_specs=[vm, vm, vm], out_specs=(vm, vm, vm),
        scratch_shapes=[pltpu.SemaphoreType.DMA((3, 7)), pltpu.SemaphoreType.DMA((3, 7))],
        compiler_params=_params(),
    )(a, b, c)


def _prep(rel_bias, w_sp, b_sp, buckets):
    def body(rb_ref, w_ref, b_ref, bk_ref, bias_ref, wt_ref, wtt_ref, bcol_ref):
        valid = _window_valid()
        bk = bk_ref[...]
        acc = [jnp.full((CHUNK, 2 * CHUNK), NEG, F32) for _ in range(4)]
        for b in range(N_BUCKETS):
            hit = (bk == b) & valid
            for h in range(4):
                acc[h] = jnp.where(hit, rb_ref[b, h], acc[h])
        for h in range(4):
            bias_ref[h] = acc[h]
        r = lax.broadcasted_iota(jnp.int32, (CHUNK, CHUNK), 0)
        c = lax.broadcasted_iota(jnp.int32, (CHUNK, CHUNK), 1)
        for g in range(A_GROUPS):
            w = jnp.where(r >= c, w_ref[g], 0.0)
            wt_ref[g] = w.astype(MM)
            wtt_ref[g] = w.T.astype(MM)
            bcol_ref[g] = jnp.broadcast_to(b_ref[g:g + 1, :], (CHUNK, CHUNK)).T

    return pl.pallas_call(
        body, name="prep",
        out_shape=(jax.ShapeDtypeStruct((4, CHUNK, 2 * CHUNK), F32),
                   jax.ShapeDtypeStruct((A_GROUPS, CHUNK, CHUNK), MM),
                   jax.ShapeDtypeStruct((A_GROUPS, CHUNK, CHUNK), MM),
                   jax.ShapeDtypeStruct((A_GROUPS, CHUNK, CHUNK), F32)),
        in_specs=[pl.BlockSpec(memory_space=pltpu.SMEM), pl.BlockSpec(memory_space=pltpu.VMEM),
                  pl.BlockSpec(memory_space=pltpu.VMEM), pl.BlockSpec(memory_space=pltpu.VMEM)],
        out_specs=tuple(pl.BlockSpec(memory_space=pltpu.VMEM) for _ in range(4)),
    )(rel_bias, w_sp, b_sp, buckets)


def _memkv_fwd(mem2, gm, w_mkv):
    tmem = mem2.shape[0]

    def body(m_ref, g_ref, w_ref, o_ref):
        xf = m_ref[...]
        r = lax.rsqrt(jnp.mean(xf * xf, axis=-1, keepdims=True) + EPS)
        hm = (xf * r * g_ref[...]).astype(MM)
        o_ref[...] = _dot(hm, w_ref[...]).astype(MM)

    vm = pl.BlockSpec(memory_space=pltpu.VMEM)
    return pl.pallas_call(
        body, name="memkv_fwd",
        out_shape=jax.ShapeDtypeStruct((tmem, 2 * MEM_LEN), MM),
        in_specs=[vm, vm, vm], out_specs=vm,
        compiler_params=_params(),
    )(mem2, gm, w_mkv)


def _memkv_bwd(dmkv, mem2, gm, w_mkv):
    def body(d_ref, m_ref, g_ref, w_ref, dw_ref, dg_ref):
        xf = m_ref[...]
        r = lax.rsqrt(jnp.mean(xf * xf, axis=-1, keepdims=True) + EPS)
        nm = xf * r
        hm = (nm * g_ref[...]).astype(MM)
        d = d_ref[...].astype(MM)
        dw_ref[...] = _dot_tn(hm, d)
        dhm = _dot_nt(d, w_ref[...])
        dg_ref[...] = jnp.sum(dhm * nm, axis=0, keepdims=True)

    vm = pl.BlockSpec(memory_space=pltpu.VMEM)
    return pl.pallas_call(
        body, name="memkv_bwd",
        out_shape=(jax.ShapeDtypeStruct((D_MODEL, 2 * MEM_LEN), F32),
                   jax.ShapeDtypeStruct((1, D_MODEL), F32)),
        in_specs=[vm, vm, vm, vm], out_specs=(vm, vm),
        compiler_params=_params(),
    )(dmkv, mem2, gm, w_mkv)


def _half_masks(rows):
    lane = lax.broadcasted_iota(jnp.int32, (rows, CHUNK), 1)
    return lane < 64


def _dup_heads(band):
    b32 = band.astype(F32)
    rolled = pltpu.roll(b32, 64, 1)
    lo = _half_masks(band.shape[0])
    return (jnp.where(lo, b32, rolled).astype(MM), jnp.where(lo, rolled, b32).astype(MM))


def _swa_probs(qsel, kd, bias_h, sink_h, first_add):
    s = _dot_nt(qsel, kd) * SCALE + bias_h + first_add
    m = jnp.maximum(jnp.max(s, axis=-1, keepdims=True), sink_h)
    p = jnp.exp(s - m)
    es = jnp.exp(sink_h - m)
    inv = 1.0 / (jnp.sum(p, axis=-1, keepdims=True) + es)
    return p * inv, es * inv


def _softmax(s):
    m = jnp.max(s, axis=-1, keepdims=True)
    p = jnp.exp(s - m)
    return p * (1.0 / jnp.sum(p, axis=-1, keepdims=True))


def _first_block_mask(n):
    col = lax.broadcasted_iota(jnp.int32, (CHUNK, 2 * CHUNK), 1)
    return jnp.where((col < CHUNK) & (n == 0), NEG, 0.0)


def _rms(xf):
    return lax.rsqrt(jnp.mean(xf * xf, axis=-1, keepdims=True) + EPS)


def _layer(x2, tgt2, mkv3, bias, sinks, vg, vb, wt, wtt, bcol, g1, g2, w_in_t, w_o, buckets, nb, s, tm):
    nt = s // tm
    bpt = tm // CHUNK
    bps = s // CHUNK
    t = nb * s

    def body(x_ref, xp_ref, t_ref, mkv_ref, bias_ref, sink_ref, vg_ref, vb_ref, wt_ref, wtt_ref, bcol_ref,
             g1_ref, g2_ref, wi_ref, wo_ref, bk_ref,
             gx_ref, dmkv_ref, dwi_hbm, dwo_hbm, dg1_ref, dg2_ref, loss_ref, dwsp_ref, dbs_ref,
             dvg_ref, dvb_ref, dsink_ref, drel_ref,
             acc_i, acc_o, uv_s, z_s, q_s, kv_s, h_s, dp_s, dxo_s,
             ycat, dyc, u_s, gu_s, gv_s, xh_s, rs_s, sv_s, vc_s, pb_s, ps_s, pc_s, kd_s, vd_s,
             dkv_acc, dbias_acc, dsv_acc, dsink_acc, sems):
        b, j = pl.program_id(0), pl.program_id(1)
        jt = nt - 1 - j

        @pl.when((b == 0) & (j == 0))
        def _():
            for ref in (acc_i, acc_o, dg1_ref, dg2_ref, loss_ref, dwsp_ref, dvg_ref, dvb_ref,
                        dbias_acc, dsv_acc, dsink_acc):
                ref[...] = jnp.zeros_like(ref)

        @pl.when(j == 0)
        def _():
            dmkv_ref[...] = jnp.zeros_like(dmkv_ref)
            dkv_acc[...] = jnp.zeros_like(dkv_acc)

        carry = dkv_acc[0:CHUNK, :]
        dkv_acc[...] = jnp.zeros_like(dkv_acc)
        dkv_acc[tm:tm + CHUNK, :] = carry

        lo = _half_masks(CHUNK)
        lob = _half_masks(2 * CHUNK)
        lot = _half_masks(tm)
        g1v = g1_ref[...]

        xf = x_ref[...]
        h = (xf * _rms(xf) * g1v).astype(MM)
        h_s[...] = h
        uv_s[...] = _dot_nt(h, wi_ref[0:UV_W, :])
        qkv = _dot_nt(h, wi_ref[SQ_COL:Z_COL, :])
        q_s[:, 0:256] = qkv[:, 0:256].astype(MM)
        q_s[:, 256:512] = qkv[:, 512:768].astype(MM)
        kv_s[CHUNK:CHUNK + tm, :] = qkv[:, 256:512].astype(MM)
        z_s[...] = _dot_nt(h, wi_ref[Z_COL:IN_WIDTH, :])
        xp = xp_ref[...]
        hp = (xp * _rms(xp) * g1v).astype(MM)
        kv_s[0:CHUNK, :] = _dot_nt(hp, wi_ref[SK_COL:MQ_COL, :]).astype(MM)

        for blk in range(bpt):
            r0 = blk * CHUNK
            rows = slice(r0, r0 + CHUNK)
            n = jt * bpt + blk
            for g in range(A_GROUPS):
                cg = slice(g * CHUNK, (g + 1) * CHUNK)
                u, gu = _gelu_and_grad(uv_s[rows, cg])
                v, gv = _gelu_and_grad(uv_s[rows, A_WIDTH + g * CHUNK:A_WIDTH + (g + 1) * CHUNK])
                mu = jnp.mean(v, axis=-1, keepdims=True)
                xc = v - mu
                rstd = lax.rsqrt(jnp.mean(xc * xc, axis=-1, keepdims=True) + EPS)
                xhat = xc * rstd
                vc = (xhat * vg_ref[:, cg] + vb_ref[:, cg]).astype(MM)
                sv = _dot(wt_ref[g], vc) + bcol_ref[g]
                u_s[rows, cg] = u
                gu_s[rows, cg] = gu
                gv_s[rows, cg] = gv
                xh_s[rows, cg] = xhat
                rs_s[rows, cg] = jnp.broadcast_to(rstd, (CHUNK, CHUNK))
                sv_s[rows, cg] = sv
                vc_s[rows, cg] = vc
                ycat[rows, cg] = u * sv
            kd = _dup_heads(kv_s[r0:r0 + 2 * CHUNK, 0:CHUNK])
            vd = _dup_heads(kv_s[r0:r0 + 2 * CHUNK, CHUNK:2 * CHUNK])
            first_add = _first_block_mask(n)
            for kvh in range(2):
                kd_s[blk * 2 + kvh] = kd[kvh]
                vd_s[blk * 2 + kvh] = vd[kvh]
                q128 = q_s[rows, kvh * CHUNK:(kvh + 1) * CHUNK].astype(F32)
                outs = []
                for gi in range(2):
                    hd = 2 * kvh + gi
                    qsel = jnp.where(lo if gi == 0 else ~lo, q128, 0.0).astype(MM)
                    probs, ps = _swa_probs(qsel, kd[kvh], bias_ref[hd], sink_ref[hd], first_add)
                    pb_s[blk * 4 + hd] = probs
                    ps_s[blk * 4 + hd] = jnp.broadcast_to(ps, (CHUNK, CHUNK))
                    outs.append(_dot(probs.astype(MM), vd[kvh]))
                ycat[rows, YB_OFF + kvh * CHUNK:YB_OFF + (kvh + 1) * CHUNK] = jnp.where(lo, outs[0], outs[1])
        for g in range(2):
            q128 = q_s[:, 256 + g * CHUNK:256 + (g + 1) * CHUNK].astype(F32)
            k128 = mkv_ref[:, g * CHUNK:(g + 1) * CHUNK]
            v128 = mkv_ref[:, MEM_LEN + g * CHUNK:MEM_LEN + (g + 1) * CHUNK]
            outs = []
            for hh in range(2):
                qsel = jnp.where(lot if hh == 0 else ~lot, q128, 0.0).astype(MM)
                probs = _softmax(_dot_nt(qsel, k128) * SCALE)
                pc_s[2 * g + hh] = probs
                outs.append(_dot(probs.astype(MM), v128))
            ycat[:, YC_OFF + g * CHUNK:YC_OFF + (g + 1) * CHUNK] = jnp.where(lot, outs[0], outs[1])

        zt = z_s[...]
        sig = 1.0 / (1.0 + jnp.exp(-zt))
        silu = zt * sig
        yc = ycat[...]
        yb = (yc * silu).astype(MM)
        o = _dot(yb, wo_ref[...])
        r2 = _rms(o)
        nrm = o * r2
        g2v = g2_ref[...]
        e = x_ref[...] + nrm * g2v - t_ref[...]
        l1 = jnp.sum(e * e, axis=-1, keepdims=True)
        loss_ref[...] += jnp.broadcast_to(jnp.sum(l1, axis=0, keepdims=True) * (0.5 / D_MODEL), loss_ref.shape)
        dxo = e * (1.0 / D_MODEL)
        dxo_s[...] = dxo
        dg2_ref[...] += jnp.sum(dxo * nrm, axis=0, keepdims=True)
        dn = dxo * g2v
        do = r2 * (dn - nrm * jnp.mean(dn * nrm, axis=-1, keepdims=True))
        dob = do.astype(MM)
        dy = _dot_nt(dob, wo_ref[...])
        dp_s[:, Z_COL:IN_WIDTH] = (dy * yc * (sig * (1.0 + zt * (1.0 - sig)))).astype(MM)
        dyc[...] = dy * silu
        acc_o[...] += _dot_tn(yb, dob)

        for blk in range(bpt):
            r0 = blk * CHUNK
            rows = slice(r0, r0 + CHUNK)
            for g in range(A_GROUPS):
                cg = slice(g * CHUNK, (g + 1) * CHUNK)
                cv = slice(A_WIDTH + g * CHUNK, A_WIDTH + (g + 1) * CHUNK)
                dya = dyc[rows, cg]
                dp_s[rows, cg] = (dya * sv_s[rows, cg] * gu_s[rows, cg]).astype(MM)
                dsv = dya * u_s[rows, cg]
                dsvb = dsv.astype(MM)
                dsv_acc[g] += dsv
                dwsp_ref[g] += _dot_nt(dsvb, vc_s[rows, cg])
                dvc = _dot(wtt_ref[g], dsvb)
                xhat = xh_s[rows, cg]
                dvg_ref[:, cg] += jnp.sum(dvc * xhat, axis=0, keepdims=True)
                dvb_ref[:, cg] += jnp.sum(dvc, axis=0, keepdims=True)
                dxh = dvc * vg_ref[:, cg]
                dv = rs_s[rows, cg] * (dxh - jnp.mean(dxh, axis=-1, keepdims=True)
                                       - xhat * jnp.mean(dxh * xhat, axis=-1, keepdims=True))
                dp_s[rows, cv] = (dv * gv_s[rows, cg]).astype(MM)
            dk_f, dv_f = [], []
            for kvh in range(2):
                kd = kd_s[blk * 2 + kvh]
                vd = vd_s[blk * 2 + kvh]
                q128 = q_s[rows, kvh * CHUNK:(kvh + 1) * CHUNK].astype(F32)
                do128 = dyc[rows, YB_OFF + kvh * CHUNK:YB_OFF + (kvh + 1) * CHUNK]
                dq128 = jnp.zeros((CHUNK, CHUNK), F32)
                dkd = jnp.zeros((2 * CHUNK, CHUNK), F32)
                dvd = jnp.zeros((2 * CHUNK, CHUNK), F32)
                for gi in range(2):
                    hd = 2 * kvh + gi
                    half = lo if gi == 0 else ~lo
                    qsel = jnp.where(half, q128, 0.0).astype(MM)
                    dosel = jnp.where(half, do128, 0.0).astype(MM)
                    probs = pb_s[blk * 4 + hd]
                    ps = ps_s[blk * 4 + hd][:, 0:1]
                    dp = _dot_nt(dosel, vd)
                    delta = jnp.sum(probs * dp, axis=-1, keepdims=True)
                    ds = probs * (dp - delta)
                    dbias_acc[hd] += ds
                    dsink_acc[hd:hd + 1, :] += jnp.broadcast_to(-jnp.sum(ps * delta, axis=0, keepdims=True), (1, CHUNK))
                    dss = (ds * SCALE).astype(MM)
                    dq128 = dq128 + jnp.where(half, _dot(dss, kd), 0.0)
                    dkd = dkd + _dot_tn(dss, qsel)
                    dvd = dvd + _dot_tn(probs.astype(MM), dosel)
                dp_s[rows, SQ_COL + kvh * CHUNK:SQ_COL + (kvh + 1) * CHUNK] = dq128.astype(MM)
                dk_f.append(dkd + pltpu.roll(dkd, 64, 1))
                dv_f.append(dvd + pltpu.roll(dvd, 64, 1))
            dkv_acc[r0:r0 + 2 * CHUNK, 0:CHUNK] += jnp.where(lob, dk_f[0], dk_f[1])
            dkv_acc[r0:r0 + 2 * CHUNK, CHUNK:2 * CHUNK] += jnp.where(lob, dv_f[0], dv_f[1])
        dp_s[:, SK_COL:MQ_COL] = dkv_acc[CHUNK:CHUNK + tm, :].astype(MM)
        for g in range(2):
            q128 = q_s[:, 256 + g * CHUNK:256 + (g + 1) * CHUNK].astype(F32)
            k128 = mkv_ref[:, g * CHUNK:(g + 1) * CHUNK]
            v128 = mkv_ref[:, MEM_LEN + g * CHUNK:MEM_LEN + (g + 1) * CHUNK]
            do128 = dyc[:, YC_OFF + g * CHUNK:YC_OFF + (g + 1) * CHUNK]
            dq128 = jnp.zeros((tm, CHUNK), F32)
            dk128 = jnp.zeros((MEM_LEN, CHUNK), F32)
            dv128 = jnp.zeros((MEM_LEN, CHUNK), F32)
            for hh in range(2):
                half = lot if hh == 0 else ~lot
                qsel = jnp.where(half, q128, 0.0).astype(MM)
                dosel = jnp.where(half, do128, 0.0).astype(MM)
                probs = pc_s[2 * g + hh]
                dp = _dot_nt(dosel, v128)
                ds = probs * (dp - jnp.sum(probs * dp, axis=-1, keepdims=True))
                dss = (ds * SCALE).astype(MM)
                dq128 = dq128 + jnp.where(half, _dot(dss, k128), 0.0)
                dk128 = dk128 + _dot_tn(dss, qsel)
                dv128 = dv128 + _dot_tn(probs.astype(MM), dosel)
            dp_s[:, MQ_COL + g * CHUNK:MQ_COL + (g + 1) * CHUNK] = dq128.astype(MM)
            dmkv_ref[:, g * CHUNK:(g + 1) * CHUNK] += dk128
            dmkv_ref[:, MEM_LEN + g * CHUNK:MEM_LEN + (g + 1) * CHUNK] += dv128

        hv = h_s[...]
        dh = jnp.zeros((tm, D_MODEL), F32)
        for c0, c1 in ((0, UV_W), (SQ_COL, Z_COL), (Z_COL, IN_WIDTH)):
            dpt = dp_s[:, c0:c1]
            acc_i[c0:c1, :] += _dot_tn(dpt, hv)
            dh = dh + _dot(dpt, wi_ref[c0:c1, :])
        xf = x_ref[...]
        r = _rms(xf)
        nx = xf * r
        dg1_ref[...] += jnp.sum(dh * nx, axis=0, keepdims=True)
        dnx = dh * g1v
        gx_ref[...] = dxo_s[...] + r * (dnx - nx * jnp.mean(dnx * nx, axis=-1, keepdims=True))

        @pl.when((b == nb - 1) & (j == nt - 1))
        def _():
            out_i = pltpu.make_async_copy(acc_i, dwi_hbm, sems.at[0])
            out_o = pltpu.make_async_copy(acc_o, dwo_hbm, sems.at[1])
            out_i.start()
            out_o.start()
            r_ = lax.broadcasted_iota(jnp.int32, (CHUNK, CHUNK), 0)
            c_ = lax.broadcasted_iota(jnp.int32, (CHUNK, CHUNK), 1)
            for g in range(A_GROUPS):
                dwsp_ref[g] = jnp.where(r_ >= c_, dwsp_ref[g], 0.0)
                dbs_ref[g:g + 1, :] = jnp.sum(dsv_acc[g].T, axis=0, keepdims=True)
            rows8 = lax.broadcasted_iota(jnp.int32, (8, CHUNK), 0)
            cols8 = lax.broadcasted_iota(jnp.int32, (8, CHUNK), 1)
            sk = jnp.zeros((8, CHUNK), F32)
            for hd in range(4):
                sk = sk + jnp.where((rows8 == 0) & (cols8 == hd),
                                    jnp.broadcast_to(dsink_acc[hd:hd + 1, :], (8, CHUNK)), 0.0)
            dsink_ref[...] = sk
            bk = bk_ref[...]
            valid = _window_valid()
            rrow = lax.broadcasted_iota(jnp.int32, (N_BUCKETS, CHUNK), 0)
            rcol = lax.broadcasted_iota(jnp.int32, (N_BUCKETS, CHUNK), 1)
            acc = jnp.zeros((N_BUCKETS, CHUNK), F32)
            for bb in range(N_BUCKETS):
                hit = (bk == bb) & valid
                for hd in range(4):
                    part = jnp.sum(jnp.where(hit, dbias_acc[hd], 0.0), axis=-1, keepdims=True)
                    tot = jnp.sum(part, axis=0, keepdims=True)
                    acc = acc + jnp.where((rrow == bb) & (rcol == hd), jnp.broadcast_to(tot, (N_BUCKETS, CHUNK)), 0.0)
            drel_ref[...] = acc
            out_i.wait()
            out_o.wait()

    tile = lambda w: pl.BlockSpec((tm, w), lambda b, j: (b * nt + nt - 1 - j, 0))
    prev_block = pl.BlockSpec((CHUNK, D_MODEL), lambda b, j: (b * bps + jnp.maximum((nt - 1 - j) * bpt - 1, 0), 0))
    per_batch = lambda r, w: pl.BlockSpec((None, r, w), lambda b, j: (b, 0, 0))
    anyspec = pl.BlockSpec(memory_space=pl.ANY)
    grp = (A_GROUPS, CHUNK, CHUNK)
    return pl.pallas_call(
        body, name="layer", grid=(nb, nt),
        out_shape=(jax.ShapeDtypeStruct((t, D_MODEL), F32),
                   jax.ShapeDtypeStruct((nb, MEM_LEN, 2 * MEM_LEN), F32),
                   jax.ShapeDtypeStruct((IN_WIDTH, D_MODEL), F32),
                   jax.ShapeDtypeStruct((D_MODEL, D_MODEL), F32),
                   jax.ShapeDtypeStruct((1, D_MODEL), F32),
                   jax.ShapeDtypeStruct((1, D_MODEL), F32),
                   jax.ShapeDtypeStruct((8, CHUNK), F32),
                   jax.ShapeDtypeStruct(grp, F32),
                   jax.ShapeDtypeStruct((A_GROUPS, CHUNK), F32),
                   jax.ShapeDtypeStruct((1, A_WIDTH), F32),
                   jax.ShapeDtypeStruct((1, A_WIDTH), F32),
                   jax.ShapeDtypeStruct((8, CHUNK), F32),
                   jax.ShapeDtypeStruct((N_BUCKETS, CHUNK), F32)),
        in_specs=[tile(D_MODEL), prev_block, tile(D_MODEL), per_batch(MEM_LEN, 2 * MEM_LEN),
                  _full((4, CHUNK, 2 * CHUNK)),
                  pl.BlockSpec(memory_space=pltpu.SMEM),
                  _full((1, A_WIDTH)), _full((1, A_WIDTH)),
                  _full(grp), _full(grp), _full(grp),
                  _full((1, D_MODEL)), _full((1, D_MODEL)),
                  _full((IN_WIDTH, D_MODEL), single=True), _full((D_MODEL, D_MODEL), single=True),
                  _full((CHUNK, 2 * CHUNK))],
        out_specs=(tile(D_MODEL), per_batch(MEM_LEN, 2 * MEM_LEN), anyspec, anyspec,
                   _full((1, D_MODEL)), _full((1, D_MODEL)), _full((8, CHUNK)),
                   _full(grp), _full((A_GROUPS, CHUNK)), _full((1, A_WIDTH)), _full((1, A_WIDTH)),
                   _full((8, CHUNK)), _full((N_BUCKETS, CHUNK))),
        scratch_shapes=[pltpu.VMEM((IN_WIDTH, D_MODEL), F32), pltpu.VMEM((D_MODEL, D_MODEL), F32),
                        pltpu.VMEM((tm, UV_W), F32), pltpu.VMEM((tm, Z_W), F32),
                        pltpu.VMEM((tm, 512), MM), pltpu.VMEM((tm + CHUNK, 2 * CHUNK), MM),
                        pltpu.VMEM((tm, D_MODEL), MM), pltpu.VMEM((tm, IN_WIDTH), MM),
                        pltpu.VMEM((tm, D_MODEL), F32),
                        pltpu.VMEM((tm, D_MODEL), F32), pltpu.VMEM((tm, D_MODEL), F32)]
                       + [pltpu.VMEM((tm, A_WIDTH), F32) for _ in range(6)]
                       + [pltpu.VMEM((tm, A_WIDTH), MM),
                          pltpu.VMEM((bpt * 4, CHUNK, 2 * CHUNK), F32),
                          pltpu.VMEM((bpt * 4, CHUNK, CHUNK), F32),
                          pltpu.VMEM((4, tm, MEM_LEN), F32),
                          pltpu.VMEM((bpt * 2, 2 * CHUNK, CHUNK), MM),
                          pltpu.VMEM((bpt * 2, 2 * CHUNK, CHUNK), MM),
                          pltpu.VMEM((tm + CHUNK, 2 * CHUNK), F32),
                          pltpu.VMEM((4, CHUNK, 2 * CHUNK), F32),
                          pltpu.VMEM(grp, F32),
                          pltpu.VMEM((8, CHUNK), F32),
                          pltpu.SemaphoreType.DMA((2,))],
        compiler_params=_params(dimension_semantics=("arbitrary", "arbitrary")),
    )(x2, x2, tgt2, mkv3, bias, sinks, vg, vb, wt, wtt, bcol, g1, g2, w_in_t, w_o, buckets)


class _ShardReduce:
    def __init__(self, pos, g, bufs, sems):
        self.x, self.y, self.c = pos
        self.g = g
        self.own, self.rcv, self.sbuf, self.rbuf = bufs
        self.ld, self.sa, self.ra, self.sb, self.rb = sems
        self.nrow = g.shape[1]
        self.here = (self.x, self.y, self.c)
        self.sib = (self.x, self.y, 1 - self.c)
        self.chips = _other_chips(self.x, self.y)

    def _load(self, q):
        return pltpu.make_async_copy(self.g.at[2 * q + self.c], self.own.at[q], self.ld.at[q])

    def _to_sib(self, q, to):
        return _remote(self.g.at[2 * q + 1 - self.c], self.rcv.at[q], self.sa.at[q], self.ra.at[q], to)

    def _to_chip(self, j, to):
        return _remote(self.sbuf.at[j], self.rbuf.at[j], self.sb.at[j], self.rb.at[j], to)

    def start(self):
        for q in range(4):
            self._load(q).start()
            self._to_sib(q, self.sib).start()

    def mid(self):
        for q in range(4):
            self._load(q).wait()
            self._to_sib(q, self.here).wait_recv()

        def add(r):
            for q in range(4):
                self.rcv[q, r, :] = self.rcv[q, r, :] + self.own[q, r, :]

        _rows_loop(self.nrow, add)
        for j, chip in enumerate(self.chips):
            which = 2 * chip[0] + chip[1]

            def cast(r, j=j, which=which):
                self.sbuf[j, r, :] = self.rcv[which, r, :].astype(BF16)

            _rows_loop(self.nrow, cast)
            self._to_chip(j, (chip[0], chip[1], self.c)).start()

    def finish(self, out):
        for j in range(3):
            self._to_chip(j, self.here).wait_recv()
        which = 2 * self.x + self.y

        def tot(r):
            g = self.rcv[which, r, :]
            for j in range(3):
                g = g + self.rbuf[j, r, :].astype(F32)
            out[r, :] = g

        _rows_loop(self.nrow, tot)
        for q in range(4):
            self._to_sib(q, self.sib).wait_send()
        for j, chip in enumerate(self.chips):
            self._to_chip(j, (chip[0], chip[1], self.c)).wait_send()


def _reduce_scratch(shape):
    return [pltpu.VMEM((4,) + shape, F32), pltpu.VMEM((4,) + shape, F32),
            pltpu.VMEM((3,) + shape, BF16), pltpu.VMEM((3,) + shape, BF16),
            pltpu.SemaphoreType.DMA((4,)), pltpu.SemaphoreType.DMA((4,)), pltpu.SemaphoreType.DMA((4,)),
            pltpu.SemaphoreType.DMA((3,)), pltpu.SemaphoreType.DMA((3,))]


_N_RED = 9

_S_LAYOUT = (((1, D_MODEL), 0), ((1, D_MODEL), 8), ((1, D_MODEL), 16),
             ((1, A_WIDTH), 24), ((1, A_WIDTH), 28), ((A_GROUPS, CHUNK), 32),
             ((1, 4), 36), ((N_BUCKETS, 4), 40),
             ((A_GROUPS * CHUNK, CHUNK), 72))
_LOSS_ROW = 37
_S_ROWS = 72 + A_GROUPS * CHUNK
_N_SMALL = len(_S_LAYOUT)


def _pack_rows(dst, refs):
    for (shp, r0), ref in zip(_S_LAYOUT, refs):
        if shp[0] == 1 and shp[1] >= CHUNK:
            for i in range(shp[1] // CHUNK):
                dst[r0 + i:r0 + i + 1, :] = ref[:, i * CHUNK:(i + 1) * CHUNK]
        elif ref.shape[-1] == CHUNK:
            dst[r0:r0 + shp[0], :] = ref[0:shp[0], :]
        else:
            dst[r0:r0 + shp[0], 0:shp[1]] = ref[...]


def _unpack_rows(src, refs):
    for (shp, r0), ref in zip(_S_LAYOUT, refs):
        if shp[0] == 1 and shp[1] >= CHUNK:
            for i in range(shp[1] // CHUNK):
                ref[:, i * CHUNK:(i + 1) * CHUNK] = src[r0 + i:r0 + i + 1, :]
        elif shp[1] == CHUNK:
            ref[...] = src[r0:r0 + shp[0], :]
        else:
            ref[...] = src[r0:r0 + shp[0], 0:shp[1]]


def _greduce(ga, gb, gc, small_g, loss_p):
    shapes = (gc.shape[1:], gb.shape[1:], ga.shape[1:])
    rs = _S_ROWS

    def body(*refs):
        it = iter(refs)
        take = lambda n: [next(it) for _ in range(n)]
        gc_ref, gb_ref, ga_ref = take(3)
        sg_refs = take(_N_SMALL)
        loss_ref, = take(1)
        oc, ob, oa, ogs = take(4)
        red = take(3 * _N_RED)
        gs_ref, rs_a, rs_b = take(3)
        ssem_a, rsem_a, ssem_b, rsem_b = take(4)

        pos = _position()
        x, y, cc = pos
        myq = 2 * x + y
        here, sib = (x, y, cc), (x, y, 1 - cc)
        chips = _other_chips(x, y)
        reducers = [_ShardReduce(pos, g, red[k * _N_RED:k * _N_RED + 4], red[k * _N_RED + 4:(k + 1) * _N_RED])
                    for k, g in enumerate((gc_ref, gb_ref, ga_ref))]

        gs_ref[...] = jnp.zeros_like(gs_ref)
        _pack_rows(gs_ref, sg_refs)
        gs_ref[_LOSS_ROW:_LOSS_ROW + 1, :] = loss_ref[0:1, :]
        small_a = _remote(gs_ref, rs_a, ssem_a, rsem_a, sib)
        small_a.start()
        for rd in reducers:
            rd.start()

        _remote(gs_ref, rs_a, ssem_a, rsem_a, here).wait_recv()
        rs_b[myq] = gs_ref[...] + rs_a[...]
        small_b = [_remote(rs_b.at[myq], rs_b.at[myq], ssem_b.at[j], rsem_b.at[j], (chip[0], chip[1], cc))
                   for j, chip in enumerate(chips)]
        for cp in small_b:
            cp.start()
        for rd in reducers:
            rd.mid()

        for j in range(3):
            _remote(rs_b.at[myq], rs_b.at[myq], ssem_b.at[j], rsem_b.at[j], here).wait_recv()

        def tot_s(i, _):
            r = pl.ds(pl.multiple_of(i * 8, 8), 8)
            ogs[r, :] = ((rs_b[0, r, :] + rs_b[1, r, :]) + rs_b[2, r, :]) + rs_b[3, r, :]
            return 0

        lax.fori_loop(0, rs // 8, tot_s, 0)
        for rd, out in zip(reducers, (oc, ob, oa)):
            rd.finish(out)
        small_a.wait_send()
        for cp in small_b:
            cp.wait_send()

    vm = pl.BlockSpec(memory_space=pltpu.VMEM)
    anyspec = pl.BlockSpec(memory_space=pl.ANY)
    scratch = []
    for shp in shapes:
        scratch += _reduce_scratch(shp)
    scratch += [pltpu.VMEM((rs, CHUNK), F32), pltpu.VMEM((rs, CHUNK), F32), pltpu.VMEM((4, rs, CHUNK), F32),
                pltpu.SemaphoreType.DMA, pltpu.SemaphoreType.DMA,
                pltpu.SemaphoreType.DMA((3,)), pltpu.SemaphoreType.DMA((3,))]
    tc, tb, ta, ts = pl.pallas_call(
        body, name="greduce",
        out_shape=tuple([jax.ShapeDtypeStruct(shp, F32) for shp in shapes] + [jax.ShapeDtypeStruct((rs, CHUNK), F32)]),
        in_specs=[anyspec] * 3 + [vm] * (_N_SMALL + 1),
        out_specs=(vm, vm, vm, vm),
        scratch_shapes=scratch,
        compiler_params=_params(),
    )(gc, gb, ga, *small_g, loss_p)
    return ta, tb, tc, ts


def _adamw(w, g, m, v):
    m = ADAM_B1 * m + (1.0 - ADAM_B1) * g
    v = ADAM_B2 * v + (1.0 - ADAM_B2) * (g * g)
    m_hat = m / (1.0 - ADAM_B1 ** ADAM_STEP)
    v_hat = v / (1.0 - ADAM_B2 ** ADAM_STEP)
    delta = -ADAM_LR * (m_hat / (jnp.sqrt(v_hat) + ADAM_EPS) + ADAM_WD * w)
    return delta, m, v


def _update(ta, tb, tc, ts, big_wmv, small_wmv):
    shapes = (ta.shape, tb.shape, tc.shape)
    rs = _S_ROWS
    small_shapes = [tuple(a.shape) for a in small_wmv[0]]

    def body(*refs):
        it = iter(refs)
        take = lambda n: [next(it) for _ in range(n)]
        ga_ref, gb_ref, gc_ref, gs_ref = take(4)
        wa, ma, va, wb, mb, vb_, wc, mc, vc = take(9)
        sw_refs, sm_refs, sv_refs = take(_N_SMALL), take(_N_SMALL), take(_N_SMALL)
        oga, oda, oma, ova, ogb, odb, omb, ovb, ogc, odc, omc, ovc = take(12)
        so_refs = [take(_N_SMALL) for _ in range(4)]
        loss_out, = take(1)
        ws, ms, vs, ods, oms, ovs = take(6)

        for buf in (ws, ms, vs):
            buf[...] = jnp.zeros_like(buf)
        _pack_rows(ws, sw_refs)
        _pack_rows(ms, sm_refs)
        _pack_rows(vs, sv_refs)

        big = ((ga_ref, wa, ma, va, oga, oda, oma, ova), (gb_ref, wb, mb, vb_, ogb, odb, omb, ovb),
               (gc_ref, wc, mc, vc, ogc, odc, omc, ovc))
        for arr in range(3):
            g_r, w_r, m_r, v_r, og, od, om, ov = big[arr]

            def upd(r, g_r=g_r, w_r=w_r, m_r=m_r, v_r=v_r, og=og, od=od, om=om, ov=ov):
                g = g_r[r, :]
                d, m, v = _adamw(w_r[r, :], g, m_r[r, :], v_r[r, :])
                og[r, :] = g
                od[r, :] = d
                om[r, :] = m
                ov[r, :] = v

            _rows_loop(shapes[arr][0], upd)

        def upd_s(i, _):
            r = pl.ds(pl.multiple_of(i * 8, 8), 8)
            d, m, v = _adamw(ws[r, :], gs_ref[r, :], ms[r, :], vs[r, :])
            ods[r, :] = d
            oms[r, :] = m
            ovs[r, :] = v
            return 0

        lax.fori_loop(0, rs // 8, upd_s, 0)
        for k, buf in enumerate((gs_ref, ods, oms, ovs)):
            _unpack_rows(buf, so_refs[k])
        loss_out[...] = gs_ref[_LOSS_ROW:_LOSS_ROW + 1, 0:1]

    vm = pl.BlockSpec(memory_space=pltpu.VMEM)
    big_out = []
    for shp in shapes:
        big_out += [jax.ShapeDtypeStruct(shp, F32)] * 4
    small_out = [jax.ShapeDtypeStruct(shp, F32) for shp in small_shapes] * 4
    out_shape = tuple(big_out + small_out + [jax.ShapeDtypeStruct((1, 1), F32)])
    n_in = 4 + 9 + 3 * _N_SMALL
    return pl.pallas_call(
        body, name="update",
        out_shape=out_shape,
        in_specs=[vm] * n_in,
        out_specs=tuple([vm] * len(out_shape)),
        scratch_shapes=[pltpu.VMEM((rs, CHUNK), F32) for _ in range(6)],
        compiler_params=_params(),
    )(ta, tb, tc, ts, *big_wmv, *small_wmv[0], *small_wmv[1], *small_wmv[2])


def _local_step(x, mem, loss_target, pre_norm_g, post_norm_g, mem_norm_g, v_norm_g, v_norm_b, w_spatial, b_spatial,
                attn_sinks, rel_bias, w_in_t, w_o, w_mkv):
    nb, s, _ = x.shape
    t = nb * s
    x2 = x.reshape(t, D_MODEL)
    tgt2 = loss_target.reshape(t, D_MODEL)
    mem2 = mem.reshape(nb * MEM_LEN, D_MODEL)
    tm = min(256, s)

    buckets = jnp.asarray(_t5_buckets())
    sinks = attn_sinks.reshape(4)
    bias, wt, wtt, bcol = _prep(rel_bias, w_spatial[0], b_spatial[0], buckets)
    mkv = _memkv_fwd(mem2, mem_norm_g, w_mkv)
    gx, dmkv, dwi, dwo, dg1, dg2, loss_p, dwsp, dbs, dvg, dvb, dsink, drel = _layer(
        x2, tgt2, mkv.reshape(nb, MEM_LEN, 2 * MEM_LEN), bias, sinks, v_norm_g, v_norm_b, wt, wtt, bcol,
        pre_norm_g, post_norm_g, w_in_t, w_o, buckets, nb, s, tm)
    dwmkv, dgm = _memkv_bwd(dmkv.reshape(nb * MEM_LEN, 2 * MEM_LEN), mem2, mem_norm_g, w_mkv)
    small = [dg1, dg2, dgm, dvg, dvb, dbs, dsink, drel, dwsp.reshape(A_GROUPS * CHUNK, CHUNK)]
    return loss_p, gx.reshape(nb, s, D_MODEL), dwi, dwo, dwmkv, small


def kernel(x, mem, pre_norm_g, post_norm_g, mem_norm_g, w_in, w_mem_kv, v_norm_g, v_norm_b, w_spatial, b_spatial, attn_sinks, rel_bias, w_out, loss_target, m_pre_norm_g, m_post_norm_g, m_mem_norm_g, m_w_in, m_w_mem_kv, m_v_norm_g, m_v_norm_b, m_w_spatial, m_b_spatial, m_attn_sinks, m_rel_bias, m_w_out, v_pre_norm_g, v_post_norm_g, v_mem_norm_g, v_w_in, v_w_mem_kv, v_v_norm_g, v_v_norm_b, v_w_spatial, v_b_spatial, v_attn_sinks, v_rel_bias, v_w_out):
    sh_a = (w_in[0].T, m_w_in[0].T, v_w_in[0].T)
    sh_b = (w_out[0], m_w_out[0], v_w_out[0])
    sh_c = (w_mem_kv[0], m_w_mem_kv[0], v_w_mem_kv[0])
    wa, wb, wc = _wgather(sh_a[0], sh_b[0], sh_c[0])

    loss_p, gx, dwi, dwo, dwmkv, small_grads = _local_step(
        x, mem, loss_target, pre_norm_g, post_norm_g, mem_norm_g, v_norm_g, v_norm_b, w_spatial, b_spatial,
        attn_sinks, rel_bias, wa.reshape(IN_WIDTH, D_MODEL), wb.reshape(D_MODEL, D_MODEL),
        wc.reshape(D_MODEL, 2 * MEM_LEN))

    small_names = ["pre_norm_g", "post_norm_g", "mem_norm_g", "v_norm_g", "v_norm_b", "b_spatial", "attn_sinks",
                   "rel_bias", "w_spatial"]
    given = dict(pre_norm_g=(pre_norm_g, m_pre_norm_g, v_pre_norm_g), post_norm_g=(post_norm_g, m_post_norm_g, v_post_norm_g),
                 mem_norm_g=(mem_norm_g, m_mem_norm_g, v_mem_norm_g), v_norm_g=(v_norm_g, m_v_norm_g, v_v_norm_g),
                 v_norm_b=(v_norm_b, m_v_norm_b, v_v_norm_b), b_spatial=(b_spatial, m_b_spatial, v_b_spatial),
                 attn_sinks=(attn_sinks, m_attn_sinks, v_attn_sinks), rel_bias=(rel_bias, m_rel_bias, v_rel_bias),
                 w_spatial=(w_spatial, m_w_spatial, v_w_spatial))
    small_wmv = [[given[n][k].reshape(shp) for n, (shp, _) in zip(small_names, _S_LAYOUT)] for k in range(3)]

    ta, tb, tc, ts = _greduce(dwi.reshape(N_DEV, SHARD_IN, D_MODEL), dwo.reshape(N_DEV, SHARD_O, D_MODEL),
                              dwmkv.reshape(N_DEV, SHARD_O, 2 * MEM_LEN), small_grads, loss_p)
    outs = _update(ta, tb, tc, ts, (*sh_a, *sh_b, *sh_c), small_wmv)
    ra, rb, rc = outs[0:4], outs[4:8], outs[8:12]
    loss = outs[12 + 4 * _N_SMALL].reshape(())

    res = {}
    for k, kind in enumerate(("grad", "delta", "new_m", "new_v")):
        res[kind, "w_in"] = ra[k].T[None]
        res[kind, "w_out"] = rb[k][None]
        res[kind, "w_mem_kv"] = rc[k][None]
        for i, n in enumerate(small_names):
            res[kind, n] = outs[12 + k * _N_SMALL + i].reshape(given[n][0].shape)
    order = ["pre_norm_g", "post_norm_g", "mem_norm_g", "w_in", "w_mem_kv", "v_norm_g", "v_norm_b", "w_spatial",
             "b_spatial", "attn_sinks", "rel_bias", "w_out"]
    flat = [res[kind, n] for kind in ("grad", "delta", "new_m", "new_v") for n in order]
    return (loss, gx, *flat)
```

```python
import numpy as np
import jax
import jax.numpy as jnp
from jax import lax
from jax.experimental import pallas as pl
from jax.experimental.pallas import tpu as pltpu

F32 = jnp.float32
BF16 = jnp.bfloat16
MM = jnp.bfloat16

D_MODEL = 1024
CHUNK = 128
A_GROUPS = 4
A_WIDTH = 512
UV_W = 1024
QKV_W = 768
Z_W = 1024
IN_WIDTH = UV_W + QKV_W + Z_W
MEM_LEN = 256
N_BUCKETS = 32
MAX_DISTANCE = 128
EPS = 1e-6
NEG = -1e30
SCALE = 0.125
N_DEV = 8
SHARD_IN = IN_WIDTH // N_DEV
SHARD_O = D_MODEL // N_DEV

SQ_COL, SK_COL, SV_COL, MQ_COL, Z_COL = UV_W, UV_W + 256, UV_W + 384, UV_W + 512, UV_W + QKV_W
YB_OFF, YC_OFF = 512, 768

ADAM_LR = 0.001
ADAM_B1 = 0.9
ADAM_B2 = 0.999
ADAM_EPS = 1e-08
ADAM_WD = 0.01
ADAM_STEP = 10

VMEM_LIMIT = 60 * 1024 * 1024

_GELU_C = 0.7978845608028654
_GELU_A = 0.044715

MESH = pl.DeviceIdType.MESH
_ROWS = 32


def _dot(a, b):
    return lax.dot_general(a, b, (((1,), (0,)), ((), ())), preferred_element_type=F32)


def _dot_nt(a, b):
    return lax.dot_general(a, b, (((1,), (1,)), ((), ())), preferred_element_type=F32)


def _dot_tn(a, b):
    return lax.dot_general(a, b, (((0,), (0,)), ((), ())), preferred_element_type=F32)


def _gelu_and_grad(x):
    x2 = x * x
    t = jnp.tanh(_GELU_C * (x + _GELU_A * x * x2))
    g = 0.5 * x * (1.0 + t)
    dg = 0.5 * (1.0 + t) + 0.5 * x * (1.0 - t * t) * (_GELU_C * (1.0 + 3.0 * _GELU_A * x2))
    return g, dg


def _t5_buckets():
    qi = np.arange(CHUNK)[:, None]
    kj = np.arange(2 * CHUNK)[None, :]
    n = np.maximum(qi + CHUNK - kj, 0)
    max_exact = N_BUCKETS // 2
    large = max_exact + (np.log(np.maximum(n, 1) / max_exact) / np.log(MAX_DISTANCE / max_exact)
                         * (N_BUCKETS - max_exact)).astype(np.int32)
    large = np.minimum(large, N_BUCKETS - 1)
    return np.where(n < max_exact, n, large).astype(np.int32)


def _params(**kw):
    return pltpu.CompilerParams(vmem_limit_bytes=VMEM_LIMIT, **kw)


def _full(shape, single=False):
    nd = len(shape)
    if single:
        return pl.BlockSpec(shape, lambda *_: (0,) * nd, pipeline_mode=pl.Buffered(1))
    return pl.BlockSpec(shape, lambda *_: (0,) * nd)


def _window_valid():
    qi = lax.broadcasted_iota(jnp.int32, (CHUNK, 2 * CHUNK), 0)
    kj = lax.broadcasted_iota(jnp.int32, (CHUNK, 2 * CHUNK), 1)
    dist = qi + CHUNK - kj
    return (dist >= 0) & (dist < CHUNK)


def _position():
    return lax.axis_index("x"), lax.axis_index("y"), lax.axis_index("c")


def _other_chips(x, y):
    return [(1 - x, y), (x, 1 - y), (1 - x, 1 - y)]


def _remote(src, dst, ssem, rsem, to):
    return pltpu.make_async_remote_copy(src_ref=src, dst_ref=dst, send_sem=ssem, recv_sem=rsem,
                                        device_id=to, device_id_type=MESH)


def _rows_loop(nrow, fn):
    def step(i, _):
        fn(pl.ds(pl.multiple_of(i * _ROWS, _ROWS), _ROWS))
        return 0

    lax.fori_loop(0, nrow // _ROWS, step, 0)


class _Gather:
    def __init__(self, pos, out, ssem, rsem):
        self.x, self.y, self.c = pos
        self.out, self.ssem, self.rsem = out, ssem, rsem
        self.me = 4 * self.x + 2 * self.y + self.c
        self.here = (self.x, self.y, self.c)
        self.sib = (self.x, self.y, 1 - self.c)
        self.chips = _other_chips(self.x, self.y)

    def _copy(self, k, blk, to):
        r = self.out.at[blk]
        return _remote(r, r, self.ssem.at[k], self.rsem.at[k], to)

    def _idx(self, chip, core):
        return 4 * chip[0] + 2 * chip[1] + core

    def start(self):
        self._copy(0, self.me, self.sib).start()
        for j, chip in enumerate(self.chips):
            self._copy(1 + j, self.me, (chip[0], chip[1], self.c)).start()

    def forward(self):
        for j, chip in enumerate(self.chips):
            self._copy(1 + j, self._idx(chip, self.c), self.here).wait_recv()
            self._copy(4 + j, self._idx(chip, self.c), self.sib).start()

    def finish(self):
        self._copy(0, self._idx((self.x, self.y), 1 - self.c), self.here).wait_recv()
        for j, chip in enumerate(self.chips):
            self._copy(4 + j, self._idx(chip, 1 - self.c), self.here).wait_recv()
        self._copy(0, self.me, self.sib).wait_send()
        for j, chip in enumerate(self.chips):
            self._copy(1 + j, self.me, (chip[0], chip[1], self.c)).wait_send()
            self._copy(4 + j, self._idx(chip, self.c), self.sib).wait_send()


def _wgather(a, b, c):
    def body(a_ref, b_ref, c_ref, oa, ob, oc, ssem, rsem):
        pos = _position()
        me = 4 * pos[0] + 2 * pos[1] + pos[2]
        gathers = []
        for k, (src, out) in enumerate(((c_ref, oc), (b_ref, ob), (a_ref, oa))):
            out[me] = src[...].astype(BF16)
            g = _Gather(pos, out, ssem.at[k], rsem.at[k])
            g.start()
            gathers.append(g)
        for g in gathers:
            g.forward()
        for g in gathers:
            g.finish()

    vm = pl.BlockSpec(memory_space=pltpu.VMEM)
    return pl.pallas_call(
        body, name="wgather",
        out_shape=(jax.ShapeDtypeStruct((N_DEV,) + a.shape, BF16),
                   jax.ShapeDtypeStruct((N_DEV,) + b.shape, BF16),
                   jax.ShapeDtypeStruct((N_DEV,) + c.shape, BF16)),
        in_specs=[vm, vm, vm], out_specs=(vm, vm, vm),
        scratch_shapes=[pltpu.SemaphoreType.DMA((3, 7)), pltpu.SemaphoreType.DMA((3, 7))],
        compiler_params=_params(),
    )(a, b, c)


def _prep(rel_bias, w_sp, b_sp, buckets):
    def body(rb_ref, w_ref, b_ref, bk_ref, bias_ref, wt_ref, wtt_ref, bcol_ref):
        valid = _window_valid()
        bk = bk_ref[...]
        acc = [jnp.full((CHUNK, 2 * CHUNK), NEG, F32) for _ in range(4)]
        for b in range(N_BUCKETS):
            hit = (bk == b) & valid
            for h in range(4):
                acc[h] = jnp.where(hit, rb_ref[b, h], acc[h])
        for h in range(4):
            bias_ref[h] = acc[h]
        r = lax.broadcasted_iota(jnp.int32, (CHUNK, CHUNK), 0)
        c = lax.broadcasted_iota(jnp.int32, (CHUNK, CHUNK), 1)
        for g in range(A_GROUPS):
            w = jnp.where(r >= c, w_ref[g], 0.0)
            wt_ref[g] = w.astype(MM)
            wtt_ref[g] = w.T.astype(MM)
            bcol_ref[g] = jnp.broadcast_to(b_ref[g:g + 1, :], (CHUNK, CHUNK)).T

    return pl.pallas_call(
        body, name="prep",
        out_shape=(jax.ShapeDtypeStruct((4, CHUNK, 2 * CHUNK), F32),
                   jax.ShapeDtypeStruct((A_GROUPS, CHUNK, CHUNK), MM),
                   jax.ShapeDtypeStruct((A_GROUPS, CHUNK, CHUNK), MM),
                   jax.ShapeDtypeStruct((A_GROUPS, CHUNK, CHUNK), F32)),
        in_specs=[pl.BlockSpec(memory_space=pltpu.SMEM), pl.BlockSpec(memory_space=pltpu.VMEM),
                  pl.BlockSpec(memory_space=pltpu.VMEM), pl.BlockSpec(memory_space=pltpu.VMEM)],
        out_specs=tuple(pl.BlockSpec(memory_space=pltpu.VMEM) for _ in range(4)),
    )(rel_bias, w_sp, b_sp, buckets)


def _memkv_fwd(mem2, gm, w_mkv):
    tmem = mem2.shape[0]

    def body(m_ref, g_ref, w_ref, o_ref):
        xf = m_ref[...]
        r = lax.rsqrt(jnp.mean(xf * xf, axis=-1, keepdims=True) + EPS)
        hm = (xf * r * g_ref[...]).astype(MM)
        o_ref[...] = _dot(hm, w_ref[...]).astype(MM)

    vm = pl.BlockSpec(memory_space=pltpu.VMEM)
    return pl.pallas_call(
        body, name="memkv_fwd",
        out_shape=jax.ShapeDtypeStruct((tmem, 2 * MEM_LEN), MM),
        in_specs=[vm, vm, vm], out_specs=vm,
        compiler_params=_params(),
    )(mem2, gm, w_mkv)


def _memkv_bwd(dmkv, mem2, gm, w_mkv):
    def body(d_ref, m_ref, g_ref, w_ref, dw_ref, dg_ref):
        xf = m_ref[...]
        r = lax.rsqrt(jnp.mean(xf * xf, axis=-1, keepdims=True) + EPS)
        nm = xf * r
        hm = (nm * g_ref[...]).astype(MM)
        d = d_ref[...].astype(MM)
        dw_ref[...] = _dot_tn(hm, d)
        dhm = _dot_nt(d, w_ref[...])
        dg_ref[...] = jnp.sum(dhm * nm, axis=0, keepdims=True)

    vm = pl.BlockSpec(memory_space=pltpu.VMEM)
    return pl.pallas_call(
        body, name="memkv_bwd",
        out_shape=(jax.ShapeDtypeStruct((D_MODEL, 2 * MEM_LEN), F32),
                   jax.ShapeDtypeStruct((1, D_MODEL), F32)),
        in_specs=[vm, vm, vm, vm], out_specs=(vm, vm),
        compiler_params=_params(),
    )(dmkv, mem2, gm, w_mkv)


def _half_masks(rows):
    lane = lax.broadcasted_iota(jnp.int32, (rows, CHUNK), 1)
    return lane < 64


def _dup_heads(band):
    b32 = band.astype(F32)
    rolled = pltpu.roll(b32, 64, 1)
    lo = _half_masks(band.shape[0])
    return (jnp.where(lo, b32, rolled).astype(MM), jnp.where(lo, rolled, b32).astype(MM))


def _swa_probs(qsel, kd, bias_h, sink_h, first_add):
    s = _dot_nt(qsel, kd) * SCALE + bias_h + first_add
    m = jnp.maximum(jnp.max(s, axis=-1, keepdims=True), sink_h)
    p = jnp.exp(s - m)
    es = jnp.exp(sink_h - m)
    inv = 1.0 / (jnp.sum(p, axis=-1, keepdims=True) + es)
    return p * inv, es * inv


def _softmax(s):
    m = jnp.max(s, axis=-1, keepdims=True)
    p = jnp.exp(s - m)
    return p * (1.0 / jnp.sum(p, axis=-1, keepdims=True))


def _first_block_mask(n):
    col = lax.broadcasted_iota(jnp.int32, (CHUNK, 2 * CHUNK), 1)
    return jnp.where((col < CHUNK) & (n == 0), NEG, 0.0)


def _rms(xf):
    return lax.rsqrt(jnp.mean(xf * xf, axis=-1, keepdims=True) + EPS)


def _layer(x2, tgt2, mkv3, bias, sinks, vg, vb, wt, wtt, bcol, g1, g2, w_in_t, w_o, buckets, nb, s, tm):
    nt = s // tm
    bpt = tm // CHUNK
    bps = s // CHUNK
    t = nb * s
    n_tiles = nb * nt

    def where_tile(k):
        kk = jnp.clip(k, 0, n_tiles - 1)
        return kk // nt, nt - 1 - kk % nt

    def body(x_ref, xp_ref, t_ref, mkv_ref, bias_ref, sink_ref, vg_ref, vb_ref, wt_ref, wtt_ref, bcol_ref,
             g1_ref, g2_ref, wi_ref, wo_ref, bk_ref,
             gx_ref, dmkv_ref, dwi_hbm, dwo_hbm, dg1_ref, dg2_ref, loss_ref, dwsp_ref, dbs_ref,
             dvg_ref, dvb_ref, dsink_ref, drel_ref,
             acc_i, acc_o, uv_s, z_s, q_s, kv_s, dp_prev, dp_s, dxo_s, dxo_prev, x_prev, hq_s, dh_s,
             ycat, dyc, u_s, gu_s, gv_s, xh_s, rs_s, sv_s, vc_s, pb_s, ps_s, pc_s, kd_s, vd_s,
             dkv_acc, dbias_acc, dsv_acc, dsink_acc, sems):
        k = pl.program_id(0)
        live = k < n_tiles
        b, jt = where_tile(k)
        j = nt - 1 - jt
        g1v = g1_ref[...]

        @pl.when(k == 0)
        def _():
            for ref in (acc_i, acc_o, dg1_ref, dg2_ref, loss_ref, dwsp_ref, dvg_ref, dvb_ref,
                        dbias_acc, dsv_acc, dsink_acc, dp_prev, dxo_prev, x_prev):
                ref[...] = jnp.zeros_like(ref)

        @pl.when(j == 0)
        def _():
            dmkv_ref[...] = jnp.zeros_like(dmkv_ref)
            dkv_acc[...] = jnp.zeros_like(dkv_acc)


        carry = dkv_acc[0:CHUNK, :]
        dkv_acc[...] = jnp.zeros_like(dkv_acc)
        dkv_acc[tm:tm + CHUNK, :] = carry

        lo = _half_masks(CHUNK)
        lob = _half_masks(2 * CHUNK)
        lot = _half_masks(tm)

        hq_s[...] = (x_prev[...] * _rms(x_prev[...]) * g1v).astype(MM)
        segments = ((0, UV_W), (SQ_COL, Z_COL), (Z_COL, IN_WIDTH))

        def chunk_dw(c0, c1):
            acc_i[c0:c1, :] += _dot_tn(dp_prev[:, c0:c1], hq_s[...])

        def chunk_dh(c0, c1):
            part = _dot(dp_prev[:, c0:c1], wi_ref[c0:c1, :])
            if c0 == 0:
                dh_s[...] = part
            else:
                dh_s[...] += part

        def chunk_finish():
            xq = x_prev[...]
            rq = _rms(xq)
            nxq = xq * rq
            dh = dh_s[...]
            dg1_ref[...] += jnp.sum(dh * nxq, axis=0, keepdims=True)
            dnx = dh * g1v
            gx_ref[...] = dxo_prev[...] + rq * (dnx - nxq * jnp.mean(dnx * nxq, axis=-1, keepdims=True))

        pending = ([lambda s=s_: chunk_dw(*s) for s_ in segments] + [lambda s=s_: chunk_dh(*s) for s_ in segments]
                   + [chunk_finish])

        def next_chunk():
            if pending:
                pending.pop(0)()

        xf = jnp.where(live, x_ref[...], 0.0)
        h = (xf * _rms(xf) * g1v).astype(MM)
        uv_s[...] = _dot_nt(h, wi_ref[0:UV_W, :])
        qkv = _dot_nt(h, wi_ref[SQ_COL:Z_COL, :])
        q_s[:, 0:256] = qkv[:, 0:256].astype(MM)
        q_s[:, 256:512] = qkv[:, 512:768].astype(MM)
        kv_s[CHUNK:CHUNK + tm, :] = qkv[:, 256:512].astype(MM)
        z_s[...] = _dot_nt(h, wi_ref[Z_COL:IN_WIDTH, :])
        xp = jnp.where(live, xp_ref[...], 0.0)
        hp = (xp * _rms(xp) * g1v).astype(MM)
        kv_s[0:CHUNK, :] = _dot_nt(hp, wi_ref[SK_COL:MQ_COL, :]).astype(MM)

        for blk in range(bpt):
            r0 = blk * CHUNK
            rows = slice(r0, r0 + CHUNK)
            n = jt * bpt + blk
            for g in range(A_GROUPS):
                cg = slice(g * CHUNK, (g + 1) * CHUNK)
                u, gu = _gelu_and_grad(uv_s[rows, cg])
                v, gv = _gelu_and_grad(uv_s[rows, A_WIDTH + g * CHUNK:A_WIDTH + (g + 1) * CHUNK])
                mu = jnp.mean(v, axis=-1, keepdims=True)
                xc = v - mu
                rstd = lax.rsqrt(jnp.mean(xc * xc, axis=-1, keepdims=True) + EPS)
                xhat = xc * rstd
                vc = (xhat * vg_ref[:, cg] + vb_ref[:, cg]).astype(MM)
                sv = _dot(wt_ref[g], vc) + bcol_ref[g]
                u_s[rows, cg] = u
                gu_s[rows, cg] = gu
                gv_s[rows, cg] = gv
                xh_s[rows, cg] = xhat
                rs_s[rows, cg] = jnp.broadcast_to(rstd, (CHUNK, CHUNK))
                sv_s[rows, cg] = sv
                vc_s[rows, cg] = vc
                ycat[rows, cg] = u * sv
            next_chunk()
            kd = _dup_heads(kv_s[r0:r0 + 2 * CHUNK, 0:CHUNK])
            vd = _dup_heads(kv_s[r0:r0 + 2 * CHUNK, CHUNK:2 * CHUNK])
            first_add = _first_block_mask(n)
            for kvh in range(2):
                kd_s[blk * 2 + kvh] = kd[kvh]
                vd_s[blk * 2 + kvh] = vd[kvh]
                q128 = q_s[rows, kvh * CHUNK:(kvh + 1) * CHUNK].astype(F32)
                outs = []
                for gi in range(2):
                    hd = 2 * kvh + gi
                    qsel = jnp.where(lo if gi == 0 else ~lo, q128, 0.0).astype(MM)
                    probs, ps = _swa_probs(qsel, kd[kvh], bias_ref[hd], sink_ref[hd], first_add)
                    pb_s[blk * 4 + hd] = probs
                    ps_s[blk * 4 + hd] = jnp.broadcast_to(ps, (CHUNK, CHUNK))
                    outs.append(_dot(probs.astype(MM), vd[kvh]))
                ycat[rows, YB_OFF + kvh * CHUNK:YB_OFF + (kvh + 1) * CHUNK] = jnp.where(lo, outs[0], outs[1])
            next_chunk()
        for g in range(2):
            q128 = q_s[:, 256 + g * CHUNK:256 + (g + 1) * CHUNK].astype(F32)
            k128 = mkv_ref[:, g * CHUNK:(g + 1) * CHUNK]
            v128 = mkv_ref[:, MEM_LEN + g * CHUNK:MEM_LEN + (g + 1) * CHUNK]
            outs = []
            for hh in range(2):
                qsel = jnp.where(lot if hh == 0 else ~lot, q128, 0.0).astype(MM)
                probs = _softmax(_dot_nt(qsel, k128) * SCALE)
                pc_s[2 * g + hh] = probs
                outs.append(_dot(probs.astype(MM), v128))
            ycat[:, YC_OFF + g * CHUNK:YC_OFF + (g + 1) * CHUNK] = jnp.where(lot, outs[0], outs[1])

        zt = z_s[...]
        sig = 1.0 / (1.0 + jnp.exp(-zt))
        silu = zt * sig
        yc = ycat[...]
        yb = (yc * silu).astype(MM)
        o = _dot(yb, wo_ref[...])
        r2 = _rms(o)
        nrm = o * r2
        g2v = g2_ref[...]
        e = xf + nrm * g2v - t_ref[...]
        l1 = jnp.sum(e * e, axis=-1, keepdims=True)
        l0 = jnp.where(live, jnp.sum(l1, axis=0, keepdims=True) * (0.5 / D_MODEL), 0.0)
        loss_ref[...] += jnp.broadcast_to(l0, loss_ref.shape)
        dxo = e * (1.0 / D_MODEL)
        dxo_s[...] = dxo
        dg2_ref[...] += jnp.sum(dxo * nrm, axis=0, keepdims=True)
        dn = dxo * g2v
        do = r2 * (dn - nrm * jnp.mean(dn * nrm, axis=-1, keepdims=True))
        dob = do.astype(MM)
        dy = _dot_nt(dob, wo_ref[...])
        dp_s[:, Z_COL:IN_WIDTH] = (dy * yc * (sig * (1.0 + zt * (1.0 - sig)))).astype(MM)
        dyc[...] = dy * silu
        acc_o[...] += _dot_tn(yb, dob)

        for blk in range(bpt):
            r0 = blk * CHUNK
            rows = slice(r0, r0 + CHUNK)
            for g in range(A_GROUPS):
                cg = slice(g * CHUNK, (g + 1) * CHUNK)
                cv = slice(A_WIDTH + g * CHUNK, A_WIDTH + (g + 1) * CHUNK)
                dya = dyc[rows, cg]
                dp_s[rows, cg] = (dya * sv_s[rows, cg] * gu_s[rows, cg]).astype(MM)
                dsv = dya * u_s[rows, cg]
                dsvb = dsv.astype(MM)
                dsv_acc[g] += dsv
                dwsp_ref[g] += _dot_nt(dsvb, vc_s[rows, cg])
                dvc = _dot(wtt_ref[g], dsvb)
                xhat = xh_s[rows, cg]
                dvg_ref[:, cg] += jnp.sum(dvc * xhat, axis=0, keepdims=True)
                dvb_ref[:, cg] += jnp.sum(dvc, axis=0, keepdims=True)
                dxh = dvc * vg_ref[:, cg]
                dv = rs_s[rows, cg] * (dxh - jnp.mean(dxh, axis=-1, keepdims=True)
                                       - xhat * jnp.mean(dxh * xhat, axis=-1, keepdims=True))
                dp_s[rows, cv] = (dv * gv_s[rows, cg]).astype(MM)
            next_chunk()
            dk_f, dv_f = [], []
            for kvh in range(2):
                kd = kd_s[blk * 2 + kvh]
                vd = vd_s[blk * 2 + kvh]
                q128 = q_s[rows, kvh * CHUNK:(kvh + 1) * CHUNK].astype(F32)
                do128 = dyc[rows, YB_OFF + kvh * CHUNK:YB_OFF + (kvh + 1) * CHUNK]
                dq128 = jnp.zeros((CHUNK, CHUNK), F32)
                dkd = jnp.zeros((2 * CHUNK, CHUNK), F32)
                dvd = jnp.zeros((2 * CHUNK, CHUNK), F32)
                for gi in range(2):
                    hd = 2 * kvh + gi
                    half = lo if gi == 0 else ~lo
                    qsel = jnp.where(half, q128, 0.0).astype(MM)
                    dosel = jnp.where(half, do128, 0.0).astype(MM)
                    probs = pb_s[blk * 4 + hd]
                    ps = ps_s[blk * 4 + hd][:, 0:1]
                    dp = _dot_nt(dosel, vd)
                    delta = jnp.sum(probs * dp, axis=-1, keepdims=True)
                    ds = probs * (dp - delta)
                    dbias_acc[hd] += ds
                    dsink_acc[hd:hd + 1, :] += jnp.broadcast_to(-jnp.sum(ps * delta, axis=0, keepdims=True), (1, CHUNK))
                    dss = (ds * SCALE).astype(MM)
                    dq128 = dq128 + jnp.where(half, _dot(dss, kd), 0.0)
                    dkd = dkd + _dot_tn(dss, qsel)
                    dvd = dvd + _dot_tn(probs.astype(MM), dosel)
                dp_s[rows, SQ_COL + kvh * CHUNK:SQ_COL + (kvh + 1) * CHUNK] = dq128.astype(MM)
                dk_f.append(dkd + pltpu.roll(dkd, 64, 1))
                dv_f.append(dvd + pltpu.roll(dvd, 64, 1))
            dkv_acc[r0:r0 + 2 * CHUNK, 0:CHUNK] += jnp.where(lob, dk_f[0], dk_f[1])
            dkv_acc[r0:r0 + 2 * CHUNK, CHUNK:2 * CHUNK] += jnp.where(lob, dv_f[0], dv_f[1])
            next_chunk()
        dp_s[:, SK_COL:MQ_COL] = dkv_acc[CHUNK:CHUNK + tm, :].astype(MM)
        for g in range(2):
            q128 = q_s[:, 256 + g * CHUNK:256 + (g + 1) * CHUNK].astype(F32)
            k128 = mkv_ref[:, g * CHUNK:(g + 1) * CHUNK]
            v128 = mkv_ref[:, MEM_LEN + g * CHUNK:MEM_LEN + (g + 1) * CHUNK]
            do128 = dyc[:, YC_OFF + g * CHUNK:YC_OFF + (g + 1) * CHUNK]
            dq128 = jnp.zeros((tm, CHUNK), F32)
            dk128 = jnp.zeros((MEM_LEN, CHUNK), F32)
            dv128 = jnp.zeros((MEM_LEN, CHUNK), F32)
            for hh in range(2):
                half = lot if hh == 0 else ~lot
                qsel = jnp.where(half, q128, 0.0).astype(MM)
                dosel = jnp.where(half, do128, 0.0).astype(MM)
                probs = pc_s[2 * g + hh]
                dp = _dot_nt(dosel, v128)
                ds = probs * (dp - jnp.sum(probs * dp, axis=-1, keepdims=True))
                dss = (ds * SCALE).astype(MM)
                dq128 = dq128 + jnp.where(half, _dot(dss, k128), 0.0)
                dk128 = dk128 + _dot_tn(dss, qsel)
                dv128 = dv128 + _dot_tn(probs.astype(MM), dosel)
            dp_s[:, MQ_COL + g * CHUNK:MQ_COL + (g + 1) * CHUNK] = dq128.astype(MM)
            dmkv_ref[:, g * CHUNK:(g + 1) * CHUNK] += dk128
            dmkv_ref[:, MEM_LEN + g * CHUNK:MEM_LEN + (g + 1) * CHUNK] += dv128

        while pending:
            next_chunk()

        x_prev[...] = jnp.where(live, x_ref[...], 0.0)
        dxo_prev[...] = dxo_s[...]
        dp_prev[...] = dp_s[...]

        @pl.when(k == n_tiles)
        def _():
            out_i = pltpu.make_async_copy(acc_i, dwi_hbm, sems.at[0])
            out_o = pltpu.make_async_copy(acc_o, dwo_hbm, sems.at[1])
            out_i.start()
            out_o.start()
            r_ = lax.broadcasted_iota(jnp.int32, (CHUNK, CHUNK), 0)
            c_ = lax.broadcasted_iota(jnp.int32, (CHUNK, CHUNK), 1)
            for g in range(A_GROUPS):
                dwsp_ref[g] = jnp.where(r_ >= c_, dwsp_ref[g], 0.0)
                dbs_ref[g:g + 1, :] = jnp.sum(dsv_acc[g].T, axis=0, keepdims=True)
            rows8 = lax.broadcasted_iota(jnp.int32, (8, CHUNK), 0)
            cols8 = lax.broadcasted_iota(jnp.int32, (8, CHUNK), 1)
            sk = jnp.zeros((8, CHUNK), F32)
            for hd in range(4):
                sk = sk + jnp.where((rows8 == 0) & (cols8 == hd),
                                    jnp.broadcast_to(dsink_acc[hd:hd + 1, :], (8, CHUNK)), 0.0)
            dsink_ref[...] = sk
            bk = bk_ref[...]
            valid = _window_valid()
            rrow = lax.broadcasted_iota(jnp.int32, (N_BUCKETS, CHUNK), 0)
            rcol = lax.broadcasted_iota(jnp.int32, (N_BUCKETS, CHUNK), 1)
            acc = jnp.zeros((N_BUCKETS, CHUNK), F32)
            for bb in range(N_BUCKETS):
                hit = (bk == bb) & valid
                for hd in range(4):
                    part = jnp.sum(jnp.where(hit, dbias_acc[hd], 0.0), axis=-1, keepdims=True)
                    tot = jnp.sum(part, axis=0, keepdims=True)
                    acc = acc + jnp.where((rrow == bb) & (rcol == hd), jnp.broadcast_to(tot, (N_BUCKETS, CHUNK)), 0.0)
            drel_ref[...] = acc
            out_i.wait()
            out_o.wait()

    def tile_row(k):
        b, jt = where_tile(k)
        return b * nt + jt

    def prev_row(k):
        b, jt = where_tile(k)
        return b * bps + jnp.maximum(jt * bpt - 1, 0)

    tile = lambda w: pl.BlockSpec((tm, w), lambda k: (tile_row(k), 0))
    tile_done = lambda w: pl.BlockSpec((tm, w), lambda k: (tile_row(k - 1), 0))
    prev_block = pl.BlockSpec((CHUNK, D_MODEL), lambda k: (prev_row(k), 0))
    per_batch = lambda r, w: pl.BlockSpec((None, r, w), lambda k: (where_tile(k)[0], 0, 0))
    anyspec = pl.BlockSpec(memory_space=pl.ANY)
    grp = (A_GROUPS, CHUNK, CHUNK)
    return pl.pallas_call(
        body, name="layer", grid=(n_tiles + 1,),
        out_shape=(jax.ShapeDtypeStruct((t, D_MODEL), F32),
                   jax.ShapeDtypeStruct((nb, MEM_LEN, 2 * MEM_LEN), F32),
                   jax.ShapeDtypeStruct((IN_WIDTH, D_MODEL), F32),
                   jax.ShapeDtypeStruct((D_MODEL, D_MODEL), F32),
                   jax.ShapeDtypeStruct((1, D_MODEL), F32),
                   jax.ShapeDtypeStruct((1, D_MODEL), F32),
                   jax.ShapeDtypeStruct((8, CHUNK), F32),
                   jax.ShapeDtypeStruct(grp, F32),
                   jax.ShapeDtypeStruct((A_GROUPS, CHUNK), F32),
                   jax.ShapeDtypeStruct((1, A_WIDTH), F32),
                   jax.ShapeDtypeStruct((1, A_WIDTH), F32),
                   jax.ShapeDtypeStruct((8, CHUNK), F32),
                   jax.ShapeDtypeStruct((N_BUCKETS, CHUNK), F32)),
        in_specs=[tile(D_MODEL), prev_block, tile(D_MODEL), per_batch(MEM_LEN, 2 * MEM_LEN),
                  _full((4, CHUNK, 2 * CHUNK)),
                  pl.BlockSpec(memory_space=pltpu.SMEM),
                  _full((1, A_WIDTH)), _full((1, A_WIDTH)),
                  _full(grp), _full(grp), _full(grp),
                  _full((1, D_MODEL)), _full((1, D_MODEL)),
                  _full((IN_WIDTH, D_MODEL), single=True), _full((D_MODEL, D_MODEL), single=True),
                  _full((CHUNK, 2 * CHUNK))],
        out_specs=(tile_done(D_MODEL), per_batch(MEM_LEN, 2 * MEM_LEN), anyspec, anyspec,
                   _full((1, D_MODEL)), _full((1, D_MODEL)), _full((8, CHUNK)),
                   _full(grp), _full((A_GROUPS, CHUNK)), _full((1, A_WIDTH)), _full((1, A_WIDTH)),
                   _full((8, CHUNK)), _full((N_BUCKETS, CHUNK))),
        scratch_shapes=[pltpu.VMEM((IN_WIDTH, D_MODEL), F32), pltpu.VMEM((D_MODEL, D_MODEL), F32),
                        pltpu.VMEM((tm, UV_W), F32), pltpu.VMEM((tm, Z_W), F32),
                        pltpu.VMEM((tm, 512), MM), pltpu.VMEM((tm + CHUNK, 2 * CHUNK), MM),
                        pltpu.VMEM((tm, IN_WIDTH), MM), pltpu.VMEM((tm, IN_WIDTH), MM),
                        pltpu.VMEM((tm, D_MODEL), F32),
                        pltpu.VMEM((tm, D_MODEL), F32), pltpu.VMEM((tm, D_MODEL), F32),
                        pltpu.VMEM((tm, D_MODEL), MM), pltpu.VMEM((tm, D_MODEL), F32),
                        pltpu.VMEM((tm, D_MODEL), F32), pltpu.VMEM((tm, D_MODEL), F32)]
                       + [pltpu.VMEM((tm, A_WIDTH), F32) for _ in range(6)]
                       + [pltpu.VMEM((tm, A_WIDTH), MM),
                          pltpu.VMEM((bpt * 4, CHUNK, 2 * CHUNK), F32),
                          pltpu.VMEM((bpt * 4, CHUNK, CHUNK), F32),
                          pltpu.VMEM((4, tm, MEM_LEN), F32),
                          pltpu.VMEM((bpt * 2, 2 * CHUNK, CHUNK), MM),
                          pltpu.VMEM((bpt * 2, 2 * CHUNK, CHUNK), MM),
                          pltpu.VMEM((tm + CHUNK, 2 * CHUNK), F32),
                          pltpu.VMEM((4, CHUNK, 2 * CHUNK), F32),
                          pltpu.VMEM(grp, F32),
                          pltpu.VMEM((8, CHUNK), F32),
                          pltpu.SemaphoreType.DMA((2,))],
        compiler_params=_params(dimension_semantics=("arbitrary",)),
    )(x2, x2, tgt2, mkv3, bias, sinks, vg, vb, wt, wtt, bcol, g1, g2, w_in_t, w_o, buckets)


class _ShardReduce:
    def __init__(self, pos, g, bufs, sems):
        self.x, self.y, self.c = pos
        self.g = g
        self.own, self.rcv, self.sbuf, self.rbuf = bufs
        self.ld, self.sa, self.ra, self.sb, self.rb = sems
        self.nrow = g.shape[1]
        self.here = (self.x, self.y, self.c)
        self.sib = (self.x, self.y, 1 - self.c)
        self.chips = _other_chips(self.x, self.y)

    def _load(self, q):
        return pltpu.make_async_copy(self.g.at[2 * q + self.c], self.own.at[q], self.ld.at[q])

    def _to_sib(self, q, to):
        return _remote(self.g.at[2 * q + 1 - self.c], self.rcv.at[q], self.sa.at[q], self.ra.at[q], to)

    def _to_chip(self, j, to):
        return _remote(self.sbuf.at[j], self.rbuf.at[j], self.sb.at[j], self.rb.at[j], to)

    def start(self):
        for q in range(4):
            self._load(q).start()
            self._to_sib(q, self.sib).start()

    def mid(self):
        for q in range(4):
            self._load(q).wait()
            self._to_sib(q, self.here).wait_recv()

        def add(r):
            for q in range(4):
                self.rcv[q, r, :] = self.rcv[q, r, :] + self.own[q, r, :]

        _rows_loop(self.nrow, add)
        for j, chip in enumerate(self.chips):
            which = 2 * chip[0] + chip[1]

            def cast(r, j=j, which=which):
                self.sbuf[j, r, :] = self.rcv[which, r, :].astype(BF16)

            _rows_loop(self.nrow, cast)
            self._to_chip(j, (chip[0], chip[1], self.c)).start()

    def finish(self, out):
        for j in range(3):
            self._to_chip(j, self.here).wait_recv()
        which = 2 * self.x + self.y

        def tot(r):
            g = self.rcv[which, r, :]
            for j in range(3):
                g = g + self.rbuf[j, r, :].astype(F32)
            out[r, :] = g

        _rows_loop(self.nrow, tot)
        for q in range(4):
            self._to_sib(q, self.sib).wait_send()
        for j, chip in enumerate(self.chips):
            self._to_chip(j, (chip[0], chip[1], self.c)).wait_send()


def _reduce_scratch(shape):
    return [pltpu.VMEM((4,) + shape, F32), pltpu.VMEM((4,) + shape, F32),
            pltpu.VMEM((3,) + shape, BF16), pltpu.VMEM((3,) + shape, BF16),
            pltpu.SemaphoreType.DMA((4,)), pltpu.SemaphoreType.DMA((4,)), pltpu.SemaphoreType.DMA((4,)),
            pltpu.SemaphoreType.DMA((3,)), pltpu.SemaphoreType.DMA((3,))]


_N_RED = 9

_S_LAYOUT = (((1, D_MODEL), 0), ((1, D_MODEL), 8), ((1, D_MODEL), 16),
             ((1, A_WIDTH), 24), ((1, A_WIDTH), 28), ((A_GROUPS, CHUNK), 32),
             ((1, 4), 36), ((N_BUCKETS, 4), 40),
             ((A_GROUPS * CHUNK, CHUNK), 72))
_LOSS_ROW = 37
_S_ROWS = 72 + A_GROUPS * CHUNK
_N_SMALL = len(_S_LAYOUT)


def _pack_rows(dst, refs):
    for (shp, r0), ref in zip(_S_LAYOUT, refs):
        if shp[0] == 1 and shp[1] >= CHUNK:
            for i in range(shp[1] // CHUNK):
                dst[r0 + i:r0 + i + 1, :] = ref[:, i * CHUNK:(i + 1) * CHUNK]
        elif ref.shape[-1] == CHUNK:
            dst[r0:r0 + shp[0], :] = ref[0:shp[0], :]
        else:
            dst[r0:r0 + shp[0], 0:shp[1]] = ref[...]


def _unpack_rows(src, refs):
    for (shp, r0), ref in zip(_S_LAYOUT, refs):
        if shp[0] == 1 and shp[1] >= CHUNK:
            for i in range(shp[1] // CHUNK):
                ref[:, i * CHUNK:(i + 1) * CHUNK] = src[r0 + i:r0 + i + 1, :]
        elif shp[1] == CHUNK:
            ref[...] = src[r0:r0 + shp[0], :]
        else:
            ref[...] = src[r0:r0 + shp[0], 0:shp[1]]


def _greduce(ga, gb, gc, small_g, loss_p):
    shapes = (gc.shape[1:], gb.shape[1:], ga.shape[1:])
    rs = _S_ROWS

    def body(*refs):
        it = iter(refs)
        take = lambda n: [next(it) for _ in range(n)]
        gc_ref, gb_ref, ga_ref = take(3)
        sg_refs = take(_N_SMALL)
        loss_ref, = take(1)
        oc, ob, oa, ogs = take(4)
        red = take(3 * _N_RED)
        gs_ref, rs_a, rs_b = take(3)
        ssem_a, rsem_a, ssem_b, rsem_b = take(4)

        pos = _position()
        x, y, cc = pos
        myq = 2 * x + y
        here, sib = (x, y, cc), (x, y, 1 - cc)
        chips = _other_chips(x, y)
        reducers = [_ShardReduce(pos, g, red[k * _N_RED:k * _N_RED + 4], red[k * _N_RED + 4:(k + 1) * _N_RED])
                    for k, g in enumerate((gc_ref, gb_ref, ga_ref))]

        gs_ref[...] = jnp.zeros_like(gs_ref)
        _pack_rows(gs_ref, sg_refs)
        gs_ref[_LOSS_ROW:_LOSS_ROW + 1, :] = loss_ref[0:1, :]
        small_a = _remote(gs_ref, rs_a, ssem_a, rsem_a, sib)
        small_a.start()
        for rd in reducers:
            rd.start()

        _remote(gs_ref, rs_a, ssem_a, rsem_a, here).wait_recv()
        rs_b[myq] = gs_ref[...] + rs_a[...]
        small_b = [_remote(rs_b.at[myq], rs_b.at[myq], ssem_b.at[j], rsem_b.at[j], (chip[0], chip[1], cc))
                   for j, chip in enumerate(chips)]
        for cp in small_b:
            cp.start()
        for rd in reducers:
            rd.mid()

        for j in range(3):
            _remote(rs_b.at[myq], rs_b.at[myq], ssem_b.at[j], rsem_b.at[j], here).wait_recv()

        def tot_s(i, _):
            r = pl.ds(pl.multiple_of(i * 8, 8), 8)
            ogs[r, :] = ((rs_b[0, r, :] + rs_b[1, r, :]) + rs_b[2, r, :]) + rs_b[3, r, :]
            return 0

        lax.fori_loop(0, rs // 8, tot_s, 0)
        for rd, out in zip(reducers, (oc, ob, oa)):
            rd.finish(out)
        small_a.wait_send()
        for cp in small_b:
            cp.wait_send()

    vm = pl.BlockSpec(memory_space=pltpu.VMEM)
    anyspec = pl.BlockSpec(memory_space=pl.ANY)
    scratch = []
    for shp in shapes:
        scratch += _reduce_scratch(shp)
    scratch += [pltpu.VMEM((rs, CHUNK), F32), pltpu.VMEM((rs, CHUNK), F32), pltpu.VMEM((4, rs, CHUNK), F32),
                pltpu.SemaphoreType.DMA, pltpu.SemaphoreType.DMA,
                pltpu.SemaphoreType.DMA((3,)), pltpu.SemaphoreType.DMA((3,))]
    tc, tb, ta, ts = pl.pallas_call(
        body, name="greduce",
        out_shape=tuple([jax.ShapeDtypeStruct(shp, F32) for shp in shapes] + [jax.ShapeDtypeStruct((rs, CHUNK), F32)]),
        in_specs=[anyspec] * 3 + [vm] * (_N_SMALL + 1),
        out_specs=(vm, vm, vm, vm),
        scratch_shapes=scratch,
        compiler_params=_params(),
    )(gc, gb, ga, *small_g, loss_p)
    return ta, tb, tc, ts


def _adamw(w, g, m, v):
    m = ADAM_B1 * m + (1.0 - ADAM_B1) * g
    v = ADAM_B2 * v + (1.0 - ADAM_B2) * (g * g)
    m_hat = m / (1.0 - ADAM_B1 ** ADAM_STEP)
    v_hat = v / (1.0 - ADAM_B2 ** ADAM_STEP)
    delta = -ADAM_LR * (m_hat / (jnp.sqrt(v_hat) + ADAM_EPS) + ADAM_WD * w)
    return delta, m, v


def _update(ta, tb, tc, ts, big_wmv, small_wmv):
    shapes = (ta.shape, tb.shape, tc.shape)
    rs = _S_ROWS
    small_shapes = [tuple(a.shape) for a in small_wmv[0]]

    def body(*refs):
        it = iter(refs)
        take = lambda n: [next(it) for _ in range(n)]
        ga_ref, gb_ref, gc_ref, gs_ref = take(4)
        wa, ma, va, wb, mb, vb_, wc, mc, vc = take(9)
        sw_refs, sm_refs, sv_refs = take(_N_SMALL), take(_N_SMALL), take(_N_SMALL)
        oga, oda, oma, ova, ogb, odb, omb, ovb, ogc, odc, omc, ovc = take(12)
        so_refs = [take(_N_SMALL) for _ in range(4)]
        loss_out, = take(1)
        ws, ms, vs, ods, oms, ovs = take(6)

        for buf in (ws, ms, vs):
            buf[...] = jnp.zeros_like(buf)
        _pack_rows(ws, sw_refs)
        _pack_rows(ms, sm_refs)
        _pack_rows(vs, sv_refs)

        big = ((ga_ref, wa, ma, va, oga, oda, oma, ova), (gb_ref, wb, mb, vb_, ogb, odb, omb, ovb),
               (gc_ref, wc, mc, vc, ogc, odc, omc, ovc))
        for arr in range(3):
            g_r, w_r, m_r, v_r, og, od, om, ov = big[arr]

            def upd(r, g_r=g_r, w_r=w_r, m_r=m_r, v_r=v_r, og=og, od=od, om=om, ov=ov):
                g = g_r[r, :]
                d, m, v = _adamw(w_r[r, :], g, m_r[r, :], v_r[r, :])
                og[r, :] = g
                od[r, :] = d
                om[r, :] = m
                ov[r, :] = v

            _rows_loop(shapes[arr][0], upd)

        def upd_s(i, _):
            r = pl.ds(pl.multiple_of(i * 8, 8), 8)
            d, m, v = _adamw(ws[r, :], gs_ref[r, :], ms[r, :], vs[r, :])
            ods[r, :] = d
            oms[r, :] = m
            ovs[r, :] = v
            return 0

        lax.fori_loop(0, rs // 8, upd_s, 0)
        for k, buf in enumerate((gs_ref, ods, oms, ovs)):
            _unpack_rows(buf, so_refs[k])
        loss_out[...] = gs_ref[_LOSS_ROW:_LOSS_ROW + 1, 0:1]

    vm = pl.BlockSpec(memory_space=pltpu.VMEM)
    big_out = []
    for shp in shapes:
        big_out += [jax.ShapeDtypeStruct(shp, F32)] * 4
    small_out = [jax.ShapeDtypeStruct(shp, F32) for shp in small_shapes] * 4
    out_shape = tuple(big_out + small_out + [jax.ShapeDtypeStruct((1, 1), F32)])
    n_in = 4 + 9 + 3 * _N_SMALL
    return pl.pallas_call(
        body, name="update",
        out_shape=out_shape,
        in_specs=[vm] * n_in,
        out_specs=tuple([vm] * len(out_shape)),
        scratch_shapes=[pltpu.VMEM((rs, CHUNK), F32) for _ in range(6)],
        compiler_params=_params(),
    )(ta, tb, tc, ts, *big_wmv, *small_wmv[0], *small_wmv[1], *small_wmv[2])


def _local_step(x, mem, loss_target, pre_norm_g, post_norm_g, mem_norm_g, v_norm_g, v_norm_b, w_spatial, b_spatial,
                attn_sinks, rel_bias, w_in_t, w_o, w_mkv):
    nb, s, _ = x.shape
    t = nb * s
    x2 = x.reshape(t, D_MODEL)
    tgt2 = loss_target.reshape(t, D_MODEL)
    mem2 = mem.reshape(nb * MEM_LEN, D_MODEL)
    tm = min(256, s)

    buckets = jnp.asarray(_t5_buckets())
    sinks = attn_sinks.reshape(4)
    bias, wt, wtt, bcol = _prep(rel_bias, w_spatial[0], b_spatial[0], buckets)
    mkv = _memkv_fwd(mem2, mem_norm_g, w_mkv)
    gx, dmkv, dwi, dwo, dg1, dg2, loss_p, dwsp, dbs, dvg, dvb, dsink, drel = _layer(
        x2, tgt2, mkv.reshape(nb, MEM_LEN, 2 * MEM_LEN), bias, sinks, v_norm_g, v_norm_b, wt, wtt, bcol,
        pre_norm_g, post_norm_g, w_in_t, w_o, buckets, nb, s, tm)
    dwmkv, dgm = _memkv_bwd(dmkv.reshape(nb * MEM_LEN, 2 * MEM_LEN), mem2, mem_norm_g, w_mkv)
    small = [dg1, dg2, dgm, dvg, dvb, dbs, dsink, drel, dwsp.reshape(A_GROUPS * CHUNK, CHUNK)]
    return loss_p, gx.reshape(nb, s, D_MODEL), dwi, dwo, dwmkv, small


def kernel(x, mem, pre_norm_g, post_norm_g, mem_norm_g, w_in, w_mem_kv, v_norm_g, v_norm_b, w_spatial, b_spatial, attn_sinks, rel_bias, w_out, loss_target, m_pre_norm_g, m_post_norm_g, m_mem_norm_g, m_w_in, m_w_mem_kv, m_v_norm_g, m_v_norm_b, m_w_spatial, m_b_spatial, m_attn_sinks, m_rel_bias, m_w_out, v_pre_norm_g, v_post_norm_g, v_mem_norm_g, v_w_in, v_w_mem_kv, v_v_norm_g, v_v_norm_b, v_w_spatial, v_b_spatial, v_attn_sinks, v_rel_bias, v_w_out):
    sh_a = (w_in[0].T, m_w_in[0].T, v_w_in[0].T)
    sh_b = (w_out[0], m_w_out[0], v_w_out[0])
    sh_c = (w_mem_kv[0], m_w_mem_kv[0], v_w_mem_kv[0])
    wa, wb, wc = _wgather(sh_a[0], sh_b[0], sh_c[0])

    loss_p, gx, dwi, dwo, dwmkv, small_grads = _local_step(
        x, mem, loss_target, pre_norm_g, post_norm_g, mem_norm_g, v_norm_g, v_norm_b, w_spatial, b_spatial,
        attn_sinks, rel_bias, wa.reshape(IN_WIDTH, D_MODEL), wb.reshape(D_MODEL, D_MODEL),
        wc.reshape(D_MODEL, 2 * MEM_LEN))

    small_names = ["pre_norm_g", "post_norm_g", "mem_norm_g", "v_norm_g", "v_norm_b", "b_spatial", "attn_sinks",
                   "rel_bias", "w_spatial"]
    given = dict(pre_norm_g=(pre_norm_g, m_pre_norm_g, v_pre_norm_g), post_norm_g=(post_norm_g, m_post_norm_g, v_post_norm_g),
                 mem_norm_g=(mem_norm_g, m_mem_norm_g, v_mem_norm_g), v_norm_g=(v_norm_g, m_v_norm_g, v_v_norm_g),
                 v_norm_b=(v_norm_b, m_v_norm_b, v_v_norm_b), b_spatial=(b_spatial, m_b_spatial, v_b_spatial),
                 attn_sinks=(attn_sinks, m_attn_sinks, v_attn_sinks), rel_bias=(rel_bias, m_rel_bias, v_rel_bias),
                 w_spatial=(w_spatial, m_w_spatial, v_w_spatial))
    small_wmv = [[given[n][k].reshape(shp) for n, (shp, _) in zip(small_names, _S_LAYOUT)] for k in range(3)]

    ta, tb, tc, ts = _greduce(dwi.reshape(N_DEV, SHARD_IN, D_MODEL), dwo.reshape(N_DEV, SHARD_O, D_MODEL),
                              dwmkv.reshape(N_DEV, SHARD_O, 2 * MEM_LEN), small_grads, loss_p)
    outs = _update(ta, tb, tc, ts, (*sh_a, *sh_b, *sh_c), small_wmv)
    ra, rb, rc = outs[0:4], outs[4:8], outs[8:12]
    loss = outs[12 + 4 * _N_SMALL].reshape(())

    res = {}
    for k, kind in enumerate(("grad", "delta", "new_m", "new_v")):
        res[kind, "w_in"] = ra[k].T[None]
        res[kind, "w_out"] = rb[k][None]
        res[kind, "w_mem_kv"] = rc[k][None]
        for i, n in enumerate(small_names):
            res[kind, n] = outs[12 + k * _N_SMALL + i].reshape(given[n][0].shape)
    order = ["pre_norm_g", "post_norm_g", "mem_norm_g", "w_in", "w_mem_kv", "v_norm_g", "v_norm_b", "w_spatial",
             "b_spatial", "attn_sinks", "rel_bias", "w_out"]
    flat = [res[kind, n] for kind in ("grad", "delta", "new_m", "new_v") for n in order]
    return (loss, gx, *flat)
```

```python
import numpy as np
import jax
import jax.numpy as jnp
from jax import lax
from jax.experimental import pallas as pl
from jax.experimental.pallas import tpu as pltpu

F32 = jnp.float32
BF16 = jnp.bfloat16
MM = jnp.bfloat16

D_MODEL = 1024
CHUNK = 128
A_GROUPS = 4
A_WIDTH = 512
UV_W = 1024
QKV_W = 768
Z_W = 1024
IN_WIDTH = UV_W + QKV_W + Z_W
MEM_LEN = 256
N_BUCKETS = 32
MAX_DISTANCE = 128
EPS = 1e-6
NEG = -1e30
SCALE = 0.125
N_DEV = 8
SHARD_IN = IN_WIDTH // N_DEV
SHARD_O = D_MODEL // N_DEV

SQ_COL, SK_COL, SV_COL, MQ_COL, Z_COL = UV_W, UV_W + 256, UV_W + 384, UV_W + 512, UV_W + QKV_W
YB_OFF, YC_OFF = 512, 768

ADAM_LR = 0.001
ADAM_B1 = 0.9
ADAM_B2 = 0.999
ADAM_EPS = 1e-08
ADAM_WD = 0.01
ADAM_STEP = 10

VMEM_LIMIT = 60 * 1024 * 1024

_GELU_C = 0.7978845608028654
_GELU_A = 0.044715

MESH = pl.DeviceIdType.MESH
_ROWS = 32


def _dot(a, b):
    return lax.dot_general(a, b, (((1,), (0,)), ((), ())), preferred_element_type=F32)


def _dot_nt(a, b):
    return lax.dot_general(a, b, (((1,), (1,)), ((), ())), preferred_element_type=F32)


def _dot_tn(a, b):
    return lax.dot_general(a, b, (((0,), (0,)), ((), ())), preferred_element_type=F32)


def _gelu_and_grad(x):
    x2 = x * x
    t = jnp.tanh(_GELU_C * (x + _GELU_A * x * x2))
    g = 0.5 * x * (1.0 + t)
    dg = 0.5 * (1.0 + t) + 0.5 * x * (1.0 - t * t) * (_GELU_C * (1.0 + 3.0 * _GELU_A * x2))
    return g, dg


def _t5_buckets():
    qi = np.arange(CHUNK)[:, None]
    kj = np.arange(2 * CHUNK)[None, :]
    n = np.maximum(qi + CHUNK - kj, 0)
    max_exact = N_BUCKETS // 2
    large = max_exact + (np.log(np.maximum(n, 1) / max_exact) / np.log(MAX_DISTANCE / max_exact)
                         * (N_BUCKETS - max_exact)).astype(np.int32)
    large = np.minimum(large, N_BUCKETS - 1)
    return np.where(n < max_exact, n, large).astype(np.int32)


def _params(**kw):
    return pltpu.CompilerParams(vmem_limit_bytes=VMEM_LIMIT, **kw)


def _full(shape, single=False):
    nd = len(shape)
    if single:
        return pl.BlockSpec(shape, lambda *_: (0,) * nd, pipeline_mode=pl.Buffered(1))
    return pl.BlockSpec(shape, lambda *_: (0,) * nd)


def _window_valid():
    qi = lax.broadcasted_iota(jnp.int32, (CHUNK, 2 * CHUNK), 0)
    kj = lax.broadcasted_iota(jnp.int32, (CHUNK, 2 * CHUNK), 1)
    dist = qi + CHUNK - kj
    return (dist >= 0) & (dist < CHUNK)


def _position():
    return lax.axis_index("x"), lax.axis_index("y"), lax.axis_index("c")


def _other_chips(x, y):
    return [(1 - x, y), (x, 1 - y), (1 - x, 1 - y)]


def _route(x, y, c):
    first = (x * c + (1 - x) * (1 - c), y * (1 - c) + (1 - y) * c)
    second = (x * (1 - c) + (1 - x) * c, y * c + (1 - y) * (1 - c))
    return first, second, (1 - x, 1 - y)


def _remote(src, dst, ssem, rsem, to):
    return pltpu.make_async_remote_copy(src_ref=src, dst_ref=dst, send_sem=ssem, recv_sem=rsem,
                                        device_id=to, device_id_type=MESH)


def _rows_loop(nrow, fn):
    def step(i, _):
        fn(pl.ds(pl.multiple_of(i * _ROWS, _ROWS), _ROWS))
        return 0

    lax.fori_loop(0, nrow // _ROWS, step, 0)


class _Gather:
    def __init__(self, pos, out, ssem, rsem):
        self.x, self.y, self.c = pos
        self.out, self.ssem, self.rsem = out, ssem, rsem
        self.me = 4 * self.x + 2 * self.y + self.c
        self.here = (self.x, self.y, self.c)
        self.sib = (self.x, self.y, 1 - self.c)
        self.first, self.second, self.far = _route(*pos)

    def _copy(self, k, blk, to):
        r = self.out.at[blk]
        return _remote(r, r, self.ssem.at[k], self.rsem.at[k], to)

    def _idx(self, chip, core):
        return 4 * chip[0] + 2 * chip[1] + core

    def _on(self, chip):
        return (chip[0], chip[1], self.c)

    def start(self):
        self._copy(0, self.me, self.sib).start()
        self._copy(1, self.me, self._on(self.first)).start()
        self._copy(2, self.me, self._on(self.second)).start()

    def forward(self):
        c = self.c
        self._copy(1, self._idx(self.first, c), self.here).wait_recv()
        self._copy(3, self._idx(self.first, c), self._on(self.second)).start()
        self._copy(4, self._idx(self.first, c), self.sib).start()
        self._copy(2, self._idx(self.second, c), self.here).wait_recv()
        self._copy(5, self._idx(self.second, c), self.sib).start()
        self._copy(3, self._idx(self.far, c), self.here).wait_recv()
        self._copy(6, self._idx(self.far, c), self.sib).start()

    def finish(self):
        c = self.c
        self._copy(0, self._idx((self.x, self.y), 1 - c), self.here).wait_recv()
        for k, chip in ((4, self.second), (5, self.first), (6, self.far)):
            self._copy(k, self._idx(chip, 1 - c), self.here).wait_recv()
        self._copy(0, self.me, self.sib).wait_send()
        self._copy(1, self.me, self._on(self.first)).wait_send()
        self._copy(2, self.me, self._on(self.second)).wait_send()
        self._copy(3, self._idx(self.first, c), self._on(self.second)).wait_send()
        for k, chip in ((4, self.first), (5, self.second), (6, self.far)):
            self._copy(k, self._idx(chip, c), self.sib).wait_send()


def _wgather(a, b, c):
    def body(a_ref, b_ref, c_ref, oa, ob, oc, ssem, rsem):
        pos = _position()
        me = 4 * pos[0] + 2 * pos[1] + pos[2]
        gathers = []
        for k, (src, out) in enumerate(((c_ref, oc), (b_ref, ob), (a_ref, oa))):
            out[me] = src[...].astype(BF16)
            g = _Gather(pos, out, ssem.at[k], rsem.at[k])
            g.start()
            gathers.append(g)
        for g in gathers:
            g.forward()
        for g in gathers:
            g.finish()

    vm = pl.BlockSpec(memory_space=pltpu.VMEM)
    return pl.pallas_call(
        body, name="wgather",
        out_shape=(jax.ShapeDtypeStruct((N_DEV,) + a.shape, BF16),
                   jax.ShapeDtypeStruct((N_DEV,) + b.shape, BF16),
                   jax.ShapeDtypeStruct((N_DEV,) + c.shape, BF16)),
        in_specs=[vm, vm, vm], out_specs=(vm, vm, vm),
        scratch_shapes=[pltpu.SemaphoreType.DMA((3, 7)), pltpu.SemaphoreType.DMA((3, 7))],
        compiler_params=_params(),
    )(a, b, c)


def _prep(rel_bias, w_sp, b_sp, buckets):
    def body(rb_ref, w_ref, b_ref, bk_ref, bias_ref, wt_ref, wtt_ref, bcol_ref):
        valid = _window_valid()
        bk = bk_ref[...]
        acc = [jnp.full((CHUNK, 2 * CHUNK), NEG, F32) for _ in range(4)]
        for b in range(N_BUCKETS):
            hit = (bk == b) & valid
            for h in range(4):
                acc[h] = jnp.where(hit, rb_ref[b, h], acc[h])
        for h in range(4):
            bias_ref[h] = acc[h]
        r = lax.broadcasted_iota(jnp.int32, (CHUNK, CHUNK), 0)
        c = lax.broadcasted_iota(jnp.int32, (CHUNK, CHUNK), 1)
        for g in range(A_GROUPS):
            w = jnp.where(r >= c, w_ref[g], 0.0)
            wt_ref[g] = w.astype(MM)
            wtt_ref[g] = w.T.astype(MM)
            bcol_ref[g] = jnp.broadcast_to(b_ref[g:g + 1, :], (CHUNK, CHUNK)).T

    return pl.pallas_call(
        body, name="prep",
        out_shape=(jax.ShapeDtypeStruct((4, CHUNK, 2 * CHUNK), F32),
                   jax.ShapeDtypeStruct((A_GROUPS, CHUNK, CHUNK), MM),
                   jax.ShapeDtypeStruct((A_GROUPS, CHUNK, CHUNK), MM),
                   jax.ShapeDtypeStruct((A_GROUPS, CHUNK, CHUNK), F32)),
        in_specs=[pl.BlockSpec(memory_space=pltpu.SMEM), pl.BlockSpec(memory_space=pltpu.VMEM),
                  pl.BlockSpec(memory_space=pltpu.VMEM), pl.BlockSpec(memory_space=pltpu.VMEM)],
        out_specs=tuple(pl.BlockSpec(memory_space=pltpu.VMEM) for _ in range(4)),
    )(rel_bias, w_sp, b_sp, buckets)


def _memkv_fwd(mem2, gm, w_mkv):
    tmem = mem2.shape[0]

    def body(m_ref, g_ref, w_ref, o_ref):
        xf = m_ref[...]
        r = lax.rsqrt(jnp.mean(xf * xf, axis=-1, keepdims=True) + EPS)
        hm = (xf * r * g_ref[...]).astype(MM)
        o_ref[...] = _dot(hm, w_ref[...]).astype(MM)

    vm = pl.BlockSpec(memory_space=pltpu.VMEM)
    return pl.pallas_call(
        body, name="memkv_fwd",
        out_shape=jax.ShapeDtypeStruct((tmem, 2 * MEM_LEN), MM),
        in_specs=[vm, vm, vm], out_specs=vm,
        compiler_params=_params(),
    )(mem2, gm, w_mkv)


def _memkv_bwd(dmkv, mem2, gm, w_mkv):
    def body(d_ref, m_ref, g_ref, w_ref, dw_ref, dg_ref):
        xf = m_ref[...]
        r = lax.rsqrt(jnp.mean(xf * xf, axis=-1, keepdims=True) + EPS)
        nm = xf * r
        hm = (nm * g_ref[...]).astype(MM)
        d = d_ref[...].astype(MM)
        dw_ref[...] = _dot_tn(hm, d)
        dhm = _dot_nt(d, w_ref[...])
        dg_ref[...] = jnp.sum(dhm * nm, axis=0, keepdims=True)

    vm = pl.BlockSpec(memory_space=pltpu.VMEM)
    return pl.pallas_call(
        body, name="memkv_bwd",
        out_shape=(jax.ShapeDtypeStruct((D_MODEL, 2 * MEM_LEN), F32),
                   jax.ShapeDtypeStruct((1, D_MODEL), F32)),
        in_specs=[vm, vm, vm, vm], out_specs=(vm, vm),
        compiler_params=_params(),
    )(dmkv, mem2, gm, w_mkv)


def _half_masks(rows):
    lane = lax.broadcasted_iota(jnp.int32, (rows, CHUNK), 1)
    return lane < 64


def _dup_heads(band):
    b32 = band.astype(F32)
    rolled = pltpu.roll(b32, 64, 1)
    lo = _half_masks(band.shape[0])
    return (jnp.where(lo, b32, rolled).astype(MM), jnp.where(lo, rolled, b32).astype(MM))


def _swa_probs(qsel, kd, bias_h, sink_h, first_add):
    s = _dot_nt(qsel, kd) * SCALE + bias_h + first_add
    m = jnp.maximum(jnp.max(s, axis=-1, keepdims=True), sink_h)
    p = jnp.exp(s - m)
    es = jnp.exp(sink_h - m)
    inv = 1.0 / (jnp.sum(p, axis=-1, keepdims=True) + es)
    return p * inv, es * inv


def _softmax(s):
    m = jnp.max(s, axis=-1, keepdims=True)
    p = jnp.exp(s - m)
    return p * (1.0 / jnp.sum(p, axis=-1, keepdims=True))


def _first_block_mask(n):
    col = lax.broadcasted_iota(jnp.int32, (CHUNK, 2 * CHUNK), 1)
    return jnp.where((col < CHUNK) & (n == 0), NEG, 0.0)


def _rms(xf):
    return lax.rsqrt(jnp.mean(xf * xf, axis=-1, keepdims=True) + EPS)


def _layer(x2, tgt2, mkv3, bias, sinks, vg, vb, wt, wtt, bcol, g1, g2, w_in_t, w_o, buckets, nb, s, tm):
    nt = s // tm
    bpt = tm // CHUNK
    bps = s // CHUNK
    t = nb * s

    def body(x_ref, xp_ref, t_ref, mkv_ref, bias_ref, sink_ref, vg_ref, vb_ref, wt_ref, wtt_ref, bcol_ref,
             g1_ref, g2_ref, wi_ref, wo_ref, bk_ref,
             gx_ref, dmkv_ref, dwi_hbm, dwo_hbm, dg1_ref, dg2_ref, loss_ref, dwsp_ref, dbs_ref,
             dvg_ref, dvb_ref, dsink_ref, drel_ref,
             acc_i, acc_o, uv_s, z_s, q_s, kv_s, h_s, dp_s, dxo_s,
             ycat, dyc, u_s, gu_s, gv_s, xh_s, rs_s, sv_s, vc_s, pb_s, ps_s, pc_s, kd_s, vd_s,
             dkv_acc, dbias_acc, dsv_acc, dsink_acc, sems):
        b, j = pl.program_id(0), pl.program_id(1)
        jt = nt - 1 - j

        @pl.when((b == 0) & (j == 0))
        def _():
            for ref in (acc_i, acc_o, dg1_ref, dg2_ref, loss_ref, dwsp_ref, dvg_ref, dvb_ref,
                        dbias_acc, dsv_acc, dsink_acc):
                ref[...] = jnp.zeros_like(ref)

        @pl.when(j == 0)
        def _():
            dmkv_ref[...] = jnp.zeros_like(dmkv_ref)
            dkv_acc[...] = jnp.zeros_like(dkv_acc)

        carry = dkv_acc[0:CHUNK, :]
        dkv_acc[...] = jnp.zeros_like(dkv_acc)
        dkv_acc[tm:tm + CHUNK, :] = carry

        lo = _half_masks(CHUNK)
        lob = _half_masks(2 * CHUNK)
        lot = _half_masks(tm)
        g1v = g1_ref[...]

        xf = x_ref[...]
        h = (xf * _rms(xf) * g1v).astype(MM)
        h_s[...] = h
        uv_s[...] = _dot_nt(h, wi_ref[0:UV_W, :])
        qkv = _dot_nt(h, wi_ref[SQ_COL:Z_COL, :])
        q_s[:, 0:256] = qkv[:, 0:256].astype(MM)
        q_s[:, 256:512] = qkv[:, 512:768].astype(MM)
        kv_s[CHUNK:CHUNK + tm, :] = qkv[:, 256:512].astype(MM)
        z_s[...] = _dot_nt(h, wi_ref[Z_COL:IN_WIDTH, :])
        xp = xp_ref[...]
        hp = (xp * _rms(xp) * g1v).astype(MM)
        kv_s[0:CHUNK, :] = _dot_nt(hp, wi_ref[SK_COL:MQ_COL, :]).astype(MM)

        for blk in range(bpt):
            r0 = blk * CHUNK
            rows = slice(r0, r0 + CHUNK)
            n = jt * bpt + blk
            for g in range(A_GROUPS):
                cg = slice(g * CHUNK, (g + 1) * CHUNK)
                u, gu = _gelu_and_grad(uv_s[rows, cg])
                v, gv = _gelu_and_grad(uv_s[rows, A_WIDTH + g * CHUNK:A_WIDTH + (g + 1) * CHUNK])
                mu = jnp.mean(v, axis=-1, keepdims=True)
                xc = v - mu
                rstd = lax.rsqrt(jnp.mean(xc * xc, axis=-1, keepdims=True) + EPS)
                xhat = xc * rstd
                vc = (xhat * vg_ref[:, cg] + vb_ref[:, cg]).astype(MM)
                sv = _dot(wt_ref[g], vc) + bcol_ref[g]
                u_s[rows, cg] = u
                gu_s[rows, cg] = gu
                gv_s[rows, cg] = gv
                xh_s[rows, cg] = xhat
                rs_s[rows, cg] = jnp.broadcast_to(rstd, (CHUNK, CHUNK))
                sv_s[rows, cg] = sv
                vc_s[rows, cg] = vc
                ycat[rows, cg] = u * sv
            kd = _dup_heads(kv_s[r0:r0 + 2 * CHUNK, 0:CHUNK])
            vd = _dup_heads(kv_s[r0:r0 + 2 * CHUNK, CHUNK:2 * CHUNK])
            first_add = _first_block_mask(n)
            for kvh in range(2):
                kd_s[blk * 2 + kvh] = kd[kvh]
                vd_s[blk * 2 + kvh] = vd[kvh]
                q128 = q_s[rows, kvh * CHUNK:(kvh + 1) * CHUNK].astype(F32)
                outs = []
                for gi in range(2):
                    hd = 2 * kvh + gi
                    qsel = jnp.where(lo if gi == 0 else ~lo, q128, 0.0).astype(MM)
                    probs, ps = _swa_probs(qsel, kd[kvh], bias_ref[hd], sink_ref[hd], first_add)
                    pb_s[blk * 4 + hd] = probs
                    ps_s[blk * 4 + hd] = jnp.broadcast_to(ps, (CHUNK, CHUNK))
                    outs.append(_dot(probs.astype(MM), vd[kvh]))
                ycat[rows, YB_OFF + kvh * CHUNK:YB_OFF + (kvh + 1) * CHUNK] = jnp.where(lo, outs[0], outs[1])
        for g in range(2):
            q128 = q_s[:, 256 + g * CHUNK:256 + (g + 1) * CHUNK].astype(F32)
            k128 = mkv_ref[:, g * CHUNK:(g + 1) * CHUNK]
            v128 = mkv_ref[:, MEM_LEN + g * CHUNK:MEM_LEN + (g + 1) * CHUNK]
            outs = []
            for hh in range(2):
                qsel = jnp.where(lot if hh == 0 else ~lot, q128, 0.0).astype(MM)
                probs = _softmax(_dot_nt(qsel, k128) * SCALE)
                pc_s[2 * g + hh] = probs
                outs.append(_dot(probs.astype(MM), v128))
            ycat[:, YC_OFF + g * CHUNK:YC_OFF + (g + 1) * CHUNK] = jnp.where(lot, outs[0], outs[1])

        zt = z_s[...]
        sig = 1.0 / (1.0 + jnp.exp(-zt))
        silu = zt * sig
        yc = ycat[...]
        yb = (yc * silu).astype(MM)
        o = _dot(yb, wo_ref[...])
        r2 = _rms(o)
        nrm = o * r2
        g2v = g2_ref[...]
        e = x_ref[...] + nrm * g2v - t_ref[...]
        l1 = jnp.sum(e * e, axis=-1, keepdims=True)
        loss_ref[...] += jnp.broadcast_to(jnp.sum(l1, axis=0, keepdims=True) * (0.5 / D_MODEL), loss_ref.shape)
        dxo = e * (1.0 / D_MODEL)
        dxo_s[...] = dxo
        dg2_ref[...] += jnp.sum(dxo * nrm, axis=0, keepdims=True)
        dn = dxo * g2v
        do = r2 * (dn - nrm * jnp.mean(dn * nrm, axis=-1, keepdims=True))
        dob = do.astype(MM)
        dy = _dot_nt(dob, wo_ref[...])
        dp_s[:, Z_COL:IN_WIDTH] = (dy * yc * (sig * (1.0 + zt * (1.0 - sig)))).astype(MM)
        dyc[...] = dy * silu
        acc_o[...] += _dot_tn(yb, dob)

        for blk in range(bpt):
            r0 = blk * CHUNK
            rows = slice(r0, r0 + CHUNK)
            for g in range(A_GROUPS):
                cg = slice(g * CHUNK, (g + 1) * CHUNK)
                cv = slice(A_WIDTH + g * CHUNK, A_WIDTH + (g + 1) * CHUNK)
                dya = dyc[rows, cg]
                dp_s[rows, cg] = (dya * sv_s[rows, cg] * gu_s[rows, cg]).astype(MM)
                dsv = dya * u_s[rows, cg]
                dsvb = dsv.astype(MM)
                dsv_acc[g] += dsv
                dwsp_ref[g] += _dot_nt(dsvb, vc_s[rows, cg])
                dvc = _dot(wtt_ref[g], dsvb)
                xhat = xh_s[rows, cg]
                dvg_ref[:, cg] += jnp.sum(dvc * xhat, axis=0, keepdims=True)
                dvb_ref[:, cg] += jnp.sum(dvc, axis=0, keepdims=True)
                dxh = dvc * vg_ref[:, cg]
                dv = rs_s[rows, cg] * (dxh - jnp.mean(dxh, axis=-1, keepdims=True)
                                       - xhat * jnp.mean(dxh * xhat, axis=-1, keepdims=True))
                dp_s[rows, cv] = (dv * gv_s[rows, cg]).astype(MM)
            dk_f, dv_f = [], []
            for kvh in range(2):
                kd = kd_s[blk * 2 + kvh]
                vd = vd_s[blk * 2 + kvh]
                q128 = q_s[rows, kvh * CHUNK:(kvh + 1) * CHUNK].astype(F32)
                do128 = dyc[rows, YB_OFF + kvh * CHUNK:YB_OFF + (kvh + 1) * CHUNK]
                dq128 = jnp.zeros((CHUNK, CHUNK), F32)
                dkd = jnp.zeros((2 * CHUNK, CHUNK), F32)
                dvd = jnp.zeros((2 * CHUNK, CHUNK), F32)
                for gi in range(2):
                    hd = 2 * kvh + gi
                    half = lo if gi == 0 else ~lo
                    qsel = jnp.where(half, q128, 0.0).astype(MM)
                    dosel = jnp.where(half, do128, 0.0).astype(MM)
                    probs = pb_s[blk * 4 + hd]
                    ps = ps_s[blk * 4 + hd][:, 0:1]
                    dp = _dot_nt(dosel, vd)
                    delta = jnp.sum(probs * dp, axis=-1, keepdims=True)
                    ds = probs * (dp - delta)
                    dbias_acc[hd] += ds
                    dsink_acc[hd:hd + 1, :] += jnp.broadcast_to(-jnp.sum(ps * delta, axis=0, keepdims=True), (1, CHUNK))
                    dss = (ds * SCALE).astype(MM)
                    dq128 = dq128 + jnp.where(half, _dot(dss, kd), 0.0)
                    dkd = dkd + _dot_tn(dss, qsel)
                    dvd = dvd + _dot_tn(probs.astype(MM), dosel)
                dp_s[rows, SQ_COL + kvh * CHUNK:SQ_COL + (kvh + 1) * CHUNK] = dq128.astype(MM)
                dk_f.append(dkd + pltpu.roll(dkd, 64, 1))
                dv_f.append(dvd + pltpu.roll(dvd, 64, 1))
            dkv_acc[r0:r0 + 2 * CHUNK, 0:CHUNK] += jnp.where(lob, dk_f[0], dk_f[1])
            dkv_acc[r0:r0 + 2 * CHUNK, CHUNK:2 * CHUNK] += jnp.where(lob, dv_f[0], dv_f[1])
        dp_s[:, SK_COL:MQ_COL] = dkv_acc[CHUNK:CHUNK + tm, :].astype(MM)
        for g in range(2):
            q128 = q_s[:, 256 + g * CHUNK:256 + (g + 1) * CHUNK].astype(F32)
            k128 = mkv_ref[:, g * CHUNK:(g + 1) * CHUNK]
            v128 = mkv_ref[:, MEM_LEN + g * CHUNK:MEM_LEN + (g + 1) * CHUNK]
            do128 = dyc[:, YC_OFF + g * CHUNK:YC_OFF + (g + 1) * CHUNK]
            dq128 = jnp.zeros((tm, CHUNK), F32)
            dk128 = jnp.zeros((MEM_LEN, CHUNK), F32)
            dv128 = jnp.zeros((MEM_LEN, CHUNK), F32)
            for hh in range(2):
                half = lot if hh == 0 else ~lot
                qsel = jnp.where(half, q128, 0.0).astype(MM)
                dosel = jnp.where(half, do128, 0.0).astype(MM)
                probs = pc_s[2 * g + hh]
                dp = _dot_nt(dosel, v128)
                ds = probs * (dp - jnp.sum(probs * dp, axis=-1, keepdims=True))
                dss = (ds * SCALE).astype(MM)
                dq128 = dq128 + jnp.where(half, _dot(dss, k128), 0.0)
                dk128 = dk128 + _dot_tn(dss, qsel)
                dv128 = dv128 + _dot_tn(probs.astype(MM), dosel)
            dp_s[:, MQ_COL + g * CHUNK:MQ_COL + (g + 1) * CHUNK] = dq128.astype(MM)
            dmkv_ref[:, g * CHUNK:(g + 1) * CHUNK] += dk128
            dmkv_ref[:, MEM_LEN + g * CHUNK:MEM_LEN + (g + 1) * CHUNK] += dv128

        hv = h_s[...]
        dh = jnp.zeros((tm, D_MODEL), F32)
        for c0, c1 in ((0, UV_W), (SQ_COL, Z_COL), (Z_COL, IN_WIDTH)):
            dpt = dp_s[:, c0:c1]
            acc_i[c0:c1, :] += _dot_tn(dpt, hv)
            dh = dh + _dot(dpt, wi_ref[c0:c1, :])
        xf = x_ref[...]
        r = _rms(xf)
        nx = xf * r
        dg1_ref[...] += jnp.sum(dh * nx, axis=0, keepdims=True)
        dnx = dh * g1v
        gx_ref[...] = dxo_s[...] + r * (dnx - nx * jnp.mean(dnx * nx, axis=-1, keepdims=True))

        @pl.when((b == nb - 1) & (j == nt - 1))
        def _():
            out_i = pltpu.make_async_copy(acc_i, dwi_hbm, sems.at[0])
            out_o = pltpu.make_async_copy(acc_o, dwo_hbm, sems.at[1])
            out_i.start()
            out_o.start()
            r_ = lax.broadcasted_iota(jnp.int32, (CHUNK, CHUNK), 0)
            c_ = lax.broadcasted_iota(jnp.int32, (CHUNK, CHUNK), 1)
            for g in range(A_GROUPS):
                dwsp_ref[g] = jnp.where(r_ >= c_, dwsp_ref[g], 0.0)
                dbs_ref[g:g + 1, :] = jnp.sum(dsv_acc[g].T, axis=0, keepdims=True)
            rows8 = lax.broadcasted_iota(jnp.int32, (8, CHUNK), 0)
            cols8 = lax.broadcasted_iota(jnp.int32, (8, CHUNK), 1)
            sk = jnp.zeros((8, CHUNK), F32)
            for hd in range(4):
                sk = sk + jnp.where((rows8 == 0) & (cols8 == hd),
                                    jnp.broadcast_to(dsink_acc[hd:hd + 1, :], (8, CHUNK)), 0.0)
            dsink_ref[...] = sk
            bk = bk_ref[...]
            valid = _window_valid()
            rrow = lax.broadcasted_iota(jnp.int32, (N_BUCKETS, CHUNK), 0)
            rcol = lax.broadcasted_iota(jnp.int32, (N_BUCKETS, CHUNK), 1)
            acc = jnp.zeros((N_BUCKETS, CHUNK), F32)
            for bb in range(N_BUCKETS):
                hit = (bk == bb) & valid
                for hd in range(4):
                    part = jnp.sum(jnp.where(hit, dbias_acc[hd], 0.0), axis=-1, keepdims=True)
                    tot = jnp.sum(part, axis=0, keepdims=True)
                    acc = acc + jnp.where((rrow == bb) & (rcol == hd), jnp.broadcast_to(tot, (N_BUCKETS, CHUNK)), 0.0)
            drel_ref[...] = acc
            out_i.wait()
            out_o.wait()

    tile = lambda w: pl.BlockSpec((tm, w), lambda b, j: (b * nt + nt - 1 - j, 0))
    prev_block = pl.BlockSpec((CHUNK, D_MODEL), lambda b, j: (b * bps + jnp.maximum((nt - 1 - j) * bpt - 1, 0), 0))
    per_batch = lambda r, w: pl.BlockSpec((None, r, w), lambda b, j: (b, 0, 0))
    anyspec = pl.BlockSpec(memory_space=pl.ANY)
    grp = (A_GROUPS, CHUNK, CHUNK)
    return pl.pallas_call(
        body, name="layer", grid=(nb, nt),
        out_shape=(jax.ShapeDtypeStruct((t, D_MODEL), F32),
                   jax.ShapeDtypeStruct((nb, MEM_LEN, 2 * MEM_LEN), F32),
                   jax.ShapeDtypeStruct((IN_WIDTH, D_MODEL), F32),
                   jax.ShapeDtypeStruct((D_MODEL, D_MODEL), F32),
                   jax.ShapeDtypeStruct((1, D_MODEL), F32),
                   jax.ShapeDtypeStruct((1, D_MODEL), F32),
                   jax.ShapeDtypeStruct((8, CHUNK), F32),
                   jax.ShapeDtypeStruct(grp, F32),
                   jax.ShapeDtypeStruct((A_GROUPS, CHUNK), F32),
                   jax.ShapeDtypeStruct((1, A_WIDTH), F32),
                   jax.ShapeDtypeStruct((1, A_WIDTH), F32),
                   jax.ShapeDtypeStruct((8, CHUNK), F32),
                   jax.ShapeDtypeStruct((N_BUCKETS, CHUNK), F32)),
        in_specs=[tile(D_MODEL), prev_block, tile(D_MODEL), per_batch(MEM_LEN, 2 * MEM_LEN),
                  _full((4, CHUNK, 2 * CHUNK)),
                  pl.BlockSpec(memory_space=pltpu.SMEM),
                  _full((1, A_WIDTH)), _full((1, A_WIDTH)),
                  _full(grp), _full(grp), _full(grp),
                  _full((1, D_MODEL)), _full((1, D_MODEL)),
                  _full((IN_WIDTH, D_MODEL), single=True), _full((D_MODEL, D_MODEL), single=True),
                  _full((CHUNK, 2 * CHUNK))],
        out_specs=(tile(D_MODEL), per_batch(MEM_LEN, 2 * MEM_LEN), anyspec, anyspec,
                   _full((1, D_MODEL)), _full((1, D_MODEL)), _full((8, CHUNK)),
                   _full(grp), _full((A_GROUPS, CHUNK)), _full((1, A_WIDTH)), _full((1, A_WIDTH)),
                   _full((8, CHUNK)), _full((N_BUCKETS, CHUNK))),
        scratch_shapes=[pltpu.VMEM((IN_WIDTH, D_MODEL), F32), pltpu.VMEM((D_MODEL, D_MODEL), F32),
                        pltpu.VMEM((tm, UV_W), F32), pltpu.VMEM((tm, Z_W), F32),
                        pltpu.VMEM((tm, 512), MM), pltpu.VMEM((tm + CHUNK, 2 * CHUNK), MM),
                        pltpu.VMEM((tm, D_MODEL), MM), pltpu.VMEM((tm, IN_WIDTH), MM),
                        pltpu.VMEM((tm, D_MODEL), F32),
                        pltpu.VMEM((tm, D_MODEL), F32), pltpu.VMEM((tm, D_MODEL), F32)]
                       + [pltpu.VMEM((tm, A_WIDTH), F32) for _ in range(6)]
                       + [pltpu.VMEM((tm, A_WIDTH), MM),
                          pltpu.VMEM((bpt * 4, CHUNK, 2 * CHUNK), F32),
                          pltpu.VMEM((bpt * 4, CHUNK, CHUNK), F32),
                          pltpu.VMEM((4, tm, MEM_LEN), F32),
                          pltpu.VMEM((bpt * 2, 2 * CHUNK, CHUNK), MM),
                          pltpu.VMEM((bpt * 2, 2 * CHUNK, CHUNK), MM),
                          pltpu.VMEM((tm + CHUNK, 2 * CHUNK), F32),
                          pltpu.VMEM((4, CHUNK, 2 * CHUNK), F32),
                          pltpu.VMEM(grp, F32),
                          pltpu.VMEM((8, CHUNK), F32),
                          pltpu.SemaphoreType.DMA((2,))],
        compiler_params=_params(dimension_semantics=("arbitrary", "arbitrary")),
    )(x2, x2, tgt2, mkv3, bias, sinks, vg, vb, wt, wtt, bcol, g1, g2, w_in_t, w_o, buckets)


class _ShardReduce:
    def __init__(self, pos, g, bufs, sems):
        self.x, self.y, self.c = pos
        self.g = g
        self.own, self.rcv, self.sbuf, self.rbuf, self.cbuf = bufs
        self.ld, self.sa, self.ra, self.sb, self.rb = sems
        self.nrow = g.shape[1]
        self.here = (self.x, self.y, self.c)
        self.sib = (self.x, self.y, 1 - self.c)
        self.first, self.second, self.far = _route(*pos)

    def _load(self, q):
        return pltpu.make_async_copy(self.g.at[2 * q + self.c], self.own.at[q], self.ld.at[q])

    def _to_sib(self, q, to):
        return _remote(self.g.at[2 * q + 1 - self.c], self.rcv.at[q], self.sa.at[q], self.ra.at[q], to)

    def _send(self, k, to):
        dst = self.cbuf.at[0] if k == 1 else self.rbuf.at[0 if k == 0 else 1]
        return _remote(self.sbuf.at[k], dst, self.sb.at[k], self.rb.at[k], to)

    def _stage(self, k, which, extra=None):
        def cast(r):
            v = self.rcv[which, r, :]
            if extra is not None:
                v = v + extra[0, r, :].astype(F32)
            self.sbuf[k, r, :] = v.astype(BF16)

        _rows_loop(self.nrow, cast)

    @staticmethod
    def _q(chip):
        return 2 * chip[0] + chip[1]

    def start(self):
        for q in range(4):
            self._load(q).start()
            self._to_sib(q, self.sib).start()

    def mid(self):
        for q in range(4):
            self._load(q).wait()
            self._to_sib(q, self.here).wait_recv()

        def add(r):
            for q in range(4):
                self.rcv[q, r, :] = self.rcv[q, r, :] + self.own[q, r, :]

        _rows_loop(self.nrow, add)
        to_first = (self.first[0], self.first[1], self.c)
        self._stage(0, self._q(self.first))
        self._send(0, to_first).start()
        self._stage(1, self._q(self.far))
        self._send(1, to_first).start()

    def pass_on(self):
        self._send(1, self.here).wait_recv()
        self._stage(2, self._q(self.second), extra=self.cbuf)
        self._send(2, (self.second[0], self.second[1], self.c)).start()

    def finish(self, out):
        self._send(0, self.here).wait_recv()
        self._send(2, self.here).wait_recv()
        which = 2 * self.x + self.y

        def tot(r):
            out[r, :] = (self.rcv[which, r, :] + self.rbuf[0, r, :].astype(F32)) + self.rbuf[1, r, :].astype(F32)

        _rows_loop(self.nrow, tot)
        for q in range(4):
            self._to_sib(q, self.sib).wait_send()
        to_first = (self.first[0], self.first[1], self.c)
        self._send(0, to_first).wait_send()
        self._send(1, to_first).wait_send()
        self._send(2, (self.second[0], self.second[1], self.c)).wait_send()


def _reduce_scratch(shape):
    return [pltpu.VMEM((4,) + shape, F32), pltpu.VMEM((4,) + shape, F32),
            pltpu.VMEM((3,) + shape, BF16), pltpu.VMEM((2,) + shape, BF16), pltpu.VMEM((1,) + shape, BF16),
            pltpu.SemaphoreType.DMA((4,)), pltpu.SemaphoreType.DMA((4,)), pltpu.SemaphoreType.DMA((4,)),
            pltpu.SemaphoreType.DMA((3,)), pltpu.SemaphoreType.DMA((3,))]


_N_RED = 10

_S_LAYOUT = (((1, D_MODEL), 0), ((1, D_MODEL), 8), ((1, D_MODEL), 16),
             ((1, A_WIDTH), 24), ((1, A_WIDTH), 28), ((A_GROUPS, CHUNK), 32),
             ((1, 4), 36), ((N_BUCKETS, 4), 40),
             ((A_GROUPS * CHUNK, CHUNK), 72))
_LOSS_ROW = 37
_S_ROWS = 72 + A_GROUPS * CHUNK
_N_SMALL = len(_S_LAYOUT)


def _pack_rows(dst, refs):
    for (shp, r0), ref in zip(_S_LAYOUT, refs):
        if shp[0] == 1 and shp[1] >= CHUNK:
            for i in range(shp[1] // CHUNK):
                dst[r0 + i:r0 + i + 1, :] = ref[:, i * CHUNK:(i + 1) * CHUNK]
        elif ref.shape[-1] == CHUNK:
            dst[r0:r0 + shp[0], :] = ref[0:shp[0], :]
        else:
            dst[r0:r0 + shp[0], 0:shp[1]] = ref[...]


def _unpack_rows(src, refs):
    for (shp, r0), ref in zip(_S_LAYOUT, refs):
        if shp[0] == 1 and shp[1] >= CHUNK:
            for i in range(shp[1] // CHUNK):
                ref[:, i * CHUNK:(i + 1) * CHUNK] = src[r0 + i:r0 + i + 1, :]
        elif shp[1] == CHUNK:
            ref[...] = src[r0:r0 + shp[0], :]
        else:
            ref[...] = src[r0:r0 + shp[0], 0:shp[1]]


def _greduce(ga, gb, gc, small_g, loss_p):
    shapes = (gc.shape[1:], gb.shape[1:], ga.shape[1:])
    rs = _S_ROWS

    def body(*refs):
        it = iter(refs)
        take = lambda n: [next(it) for _ in range(n)]
        gc_ref, gb_ref, ga_ref = take(3)
        sg_refs = take(_N_SMALL)
        loss_ref, = take(1)
        oc, ob, oa, ogs = take(4)
        red = take(3 * _N_RED)
        gs_ref, rs_a, rs_b = take(3)
        ssem_a, rsem_a, ssem_b, rsem_b = take(4)

        pos = _position()
        x, y, cc = pos
        myq = 2 * x + y
        here, sib = (x, y, cc), (x, y, 1 - cc)
        chips = _other_chips(x, y)
        reducers = [_ShardReduce(pos, g, red[k * _N_RED:k * _N_RED + 5], red[k * _N_RED + 5:(k + 1) * _N_RED])
                    for k, g in enumerate((gc_ref, gb_ref, ga_ref))]

        gs_ref[...] = jnp.zeros_like(gs_ref)
        _pack_rows(gs_ref, sg_refs)
        gs_ref[_LOSS_ROW:_LOSS_ROW + 1, :] = loss_ref[0:1, :]
        small_a = _remote(gs_ref, rs_a, ssem_a, rsem_a, sib)
        small_a.start()
        for rd in reducers:
            rd.start()

        _remote(gs_ref, rs_a, ssem_a, rsem_a, here).wait_recv()
        rs_b[myq] = gs_ref[...] + rs_a[...]
        small_b = [_remote(rs_b.at[myq], rs_b.at[myq], ssem_b.at[j], rsem_b.at[j], (chip[0], chip[1], cc))
                   for j, chip in enumerate(chips)]
        for cp in small_b:
            cp.start()
        for rd in reducers:
            rd.mid()
        for rd in reducers:
            rd.pass_on()

        for j in range(3):
            _remote(rs_b.at[myq], rs_b.at[myq], ssem_b.at[j], rsem_b.at[j], here).wait_recv()

        def tot_s(i, _):
            r = pl.ds(pl.multiple_of(i * 8, 8), 8)
            ogs[r, :] = ((rs_b[0, r, :] + rs_b[1, r, :]) + rs_b[2, r, :]) + rs_b[3, r, :]
            return 0

        lax.fori_loop(0, rs // 8, tot_s, 0)
        for rd, out in zip(reducers, (oc, ob, oa)):
            rd.finish(out)
        small_a.wait_send()
        for cp in small_b:
            cp.wait_send()

    vm = pl.BlockSpec(memory_space=pltpu.VMEM)
    anyspec = pl.BlockSpec(memory_space=pl.ANY)
    scratch = []
    for shp in shapes:
        scratch += _reduce_scratch(shp)
    scratch += [pltpu.VMEM((rs, CHUNK), F32), pltpu.VMEM((rs, CHUNK), F32), pltpu.VMEM((4, rs, CHUNK), F32),
                pltpu.SemaphoreType.DMA, pltpu.SemaphoreType.DMA,
                pltpu.SemaphoreType.DMA((3,)), pltpu.SemaphoreType.DMA((3,))]
    tc, tb, ta, ts = pl.pallas_call(
        body, name="greduce",
        out_shape=tuple([jax.ShapeDtypeStruct(shp, F32) for shp in shapes] + [jax.ShapeDtypeStruct((rs, CHUNK), F32)]),
        in_specs=[anyspec] * 3 + [vm] * (_N_SMALL + 1),
        out_specs=(vm, vm, vm, vm),
        scratch_shapes=scratch,
        compiler_params=_params(),
    )(gc, gb, ga, *small_g, loss_p)
    return ta, tb, tc, ts


def _adamw(w, g, m, v):
    m = ADAM_B1 * m + (1.0 - ADAM_B1) * g
    v = ADAM_B2 * v + (1.0 - ADAM_B2) * (g * g)
    m_hat = m / (1.0 - ADAM_B1 ** ADAM_STEP)
    v_hat = v / (1.0 - ADAM_B2 ** ADAM_STEP)
    delta = -ADAM_LR * (m_hat / (jnp.sqrt(v_hat) + ADAM_EPS) + ADAM_WD * w)
    return delta, m, v


def _update(ta, tb, tc, ts, big_wmv, small_wmv):
    shapes = (ta.shape, tb.shape, tc.shape)
    rs = _S_ROWS
    small_shapes = [tuple(a.shape) for a in small_wmv[0]]

    def body(*refs):
        it = iter(refs)
        take = lambda n: [next(it) for _ in range(n)]
        ga_ref, gb_ref, gc_ref, gs_ref = take(4)
        wa, ma, va, wb, mb, vb_, wc, mc, vc = take(9)
        sw_refs, sm_refs, sv_refs = take(_N_SMALL), take(_N_SMALL), take(_N_SMALL)
        oga, oda, oma, ova, ogb, odb, omb, ovb, ogc, odc, omc, ovc = take(12)
        so_refs = [take(_N_SMALL) for _ in range(4)]
        loss_out, = take(1)
        ws, ms, vs, ods, oms, ovs = take(6)

        for buf in (ws, ms, vs):
            buf[...] = jnp.zeros_like(buf)
        _pack_rows(ws, sw_refs)
        _pack_rows(ms, sm_refs)
        _pack_rows(vs, sv_refs)

        big = ((ga_ref, wa, ma, va, oga, oda, oma, ova), (gb_ref, wb, mb, vb_, ogb, odb, omb, ovb),
               (gc_ref, wc, mc, vc, ogc, odc, omc, ovc))
        for arr in range(3):
            g_r, w_r, m_r, v_r, og, od, om, ov = big[arr]

            def upd(r, g_r=g_r, w_r=w_r, m_r=m_r, v_r=v_r, og=og, od=od, om=om, ov=ov):
                g = g_r[r, :]
                d, m, v = _adamw(w_r[r, :], g, m_r[r, :], v_r[r, :])
                og[r, :] = g
                od[r, :] = d
                om[r, :] = m
                ov[r, :] = v

            _rows_loop(shapes[arr][0], upd)

        def upd_s(i, _):
            r = pl.ds(pl.multiple_of(i * 8, 8), 8)
            d, m, v = _adamw(ws[r, :], gs_ref[r, :], ms[r, :], vs[r, :])
            ods[r, :] = d
            oms[r, :] = m
            ovs[r, :] = v
            return 0

        lax.fori_loop(0, rs // 8, upd_s, 0)
        for k, buf in enumerate((gs_ref, ods, oms, ovs)):
            _unpack_rows(buf, so_refs[k])
        loss_out[...] = gs_ref[_LOSS_ROW:_LOSS_ROW + 1, 0:1]

    vm = pl.BlockSpec(memory_space=pltpu.VMEM)
    big_out = []
    for shp in shapes:
        big_out += [jax.ShapeDtypeStruct(shp, F32)] * 4
    small_out = [jax.ShapeDtypeStruct(shp, F32) for shp in small_shapes] * 4
    out_shape = tuple(big_out + small_out + [jax.ShapeDtypeStruct((1, 1), F32)])
    n_in = 4 + 9 + 3 * _N_SMALL
    return pl.pallas_call(
        body, name="update",
        out_shape=out_shape,
        in_specs=[vm] * n_in,
        out_specs=tuple([vm] * len(out_shape)),
        scratch_shapes=[pltpu.VMEM((rs, CHUNK), F32) for _ in range(6)],
        compiler_params=_params(),
    )(ta, tb, tc, ts, *big_wmv, *small_wmv[0], *small_wmv[1], *small_wmv[2])


def _local_step(x, mem, loss_target, pre_norm_g, post_norm_g, mem_norm_g, v_norm_g, v_norm_b, w_spatial, b_spatial,
                attn_sinks, rel_bias, w_in_t, w_o, w_mkv):
    nb, s, _ = x.shape
    t = nb * s
    x2 = x.reshape(t, D_MODEL)
    tgt2 = loss_target.reshape(t, D_MODEL)
    mem2 = mem.reshape(nb * MEM_LEN, D_MODEL)
    tm = min(256, s)

    buckets = jnp.asarray(_t5_buckets())
    sinks = attn_sinks.reshape(4)
    bias, wt, wtt, bcol = _prep(rel_bias, w_spatial[0], b_spatial[0], buckets)
    mkv = _memkv_fwd(mem2, mem_norm_g, w_mkv)
    gx, dmkv, dwi, dwo, dg1, dg2, loss_p, dwsp, dbs, dvg, dvb, dsink, drel = _layer(
        x2, tgt2, mkv.reshape(nb, MEM_LEN, 2 * MEM_LEN), bias, sinks, v_norm_g, v_norm_b, wt, wtt, bcol,
        pre_norm_g, post_norm_g, w_in_t, w_o, buckets, nb, s, tm)
    dwmkv, dgm = _memkv_bwd(dmkv.reshape(nb * MEM_LEN, 2 * MEM_LEN), mem2, mem_norm_g, w_mkv)
    small = [dg1, dg2, dgm, dvg, dvb, dbs, dsink, drel, dwsp.reshape(A_GROUPS * CHUNK, CHUNK)]
    return loss_p, gx.reshape(nb, s, D_MODEL), dwi, dwo, dwmkv, small


def kernel(x, mem, pre_norm_g, post_norm_g, mem_norm_g, w_in, w_mem_kv, v_norm_g, v_norm_b, w_spatial, b_spatial, attn_sinks, rel_bias, w_out, loss_target, m_pre_norm_g, m_post_norm_g, m_mem_norm_g, m_w_in, m_w_mem_kv, m_v_norm_g, m_v_norm_b, m_w_spatial, m_b_spatial, m_attn_sinks, m_rel_bias, m_w_out, v_pre_norm_g, v_post_norm_g, v_mem_norm_g, v_w_in, v_w_mem_kv, v_v_norm_g, v_v_norm_b, v_w_spatial, v_b_spatial, v_attn_sinks, v_rel_bias, v_w_out):
    sh_a = (w_in[0].T, m_w_in[0].T, v_w_in[0].T)
    sh_b = (w_out[0], m_w_out[0], v_w_out[0])
    sh_c = (w_mem_kv[0], m_w_mem_kv[0], v_w_mem_kv[0])
    wa, wb, wc = _wgather(sh_a[0], sh_b[0], sh_c[0])

    loss_p, gx, dwi, dwo, dwmkv, small_grads = _local_step(
        x, mem, loss_target, pre_norm_g, post_norm_g, mem_norm_g, v_norm_g, v_norm_b, w_spatial, b_spatial,
        attn_sinks, rel_bias, wa.reshape(IN_WIDTH, D_MODEL), wb.reshape(D_MODEL, D_MODEL),
        wc.reshape(D_MODEL, 2 * MEM_LEN))

    small_names = ["pre_norm_g", "post_norm_g", "mem_norm_g", "v_norm_g", "v_norm_b", "b_spatial", "attn_sinks",
                   "rel_bias", "w_spatial"]
    given = dict(pre_norm_g=(pre_norm_g, m_pre_norm_g, v_pre_norm_g), post_norm_g=(post_norm_g, m_post_norm_g, v_post_norm_g),
                 mem_norm_g=(mem_norm_g, m_mem_norm_g, v_mem_norm_g), v_norm_g=(v_norm_g, m_v_norm_g, v_v_norm_g),
                 v_norm_b=(v_norm_b, m_v_norm_b, v_v_norm_b), b_spatial=(b_spatial, m_b_spatial, v_b_spatial),
                 attn_sinks=(attn_sinks, m_attn_sinks, v_attn_sinks), rel_bias=(rel_bias, m_rel_bias, v_rel_bias),
                 w_spatial=(w_spatial, m_w_spatial, v_w_spatial))
    small_wmv = [[given[n][k].reshape(shp) for n, (shp, _) in zip(small_names, _S_LAYOUT)] for k in range(3)]

    ta, tb, tc, ts = _greduce(dwi.reshape(N_DEV, SHARD_IN, D_MODEL), dwo.reshape(N_DEV, SHARD_O, D_MODEL),
                              dwmkv.reshape(N_DEV, SHARD_O, 2 * MEM_LEN), small_grads, loss_p)
    outs = _update(ta, tb, tc, ts, (*sh_a, *sh_b, *sh_c), small_wmv)
    ra, rb, rc = outs[0:4], outs[4:8], outs[8:12]
    loss = outs[12 + 4 * _N_SMALL].reshape(())

    res = {}
    for k, kind in enumerate(("grad", "delta", "new_m", "new_v")):
        res[kind, "w_in"] = ra[k].T[None]
        res[kind, "w_out"] = rb[k][None]
        res[kind, "w_mem_kv"] = rc[k][None]
        for i, n in enumerate(small_names):
            res[kind, n] = outs[12 + k * _N_SMALL + i].reshape(given[n][0].shape)
    order = ["pre_norm_g", "post_norm_g", "mem_norm_g", "w_in", "w_mem_kv", "v_norm_g", "v_norm_b", "w_spatial",
             "b_spatial", "attn_sinks", "rel_bias", "w_out"]
    flat = [res[kind, n] for kind in ("grad", "delta", "new_m", "new_v") for n in order]
    return (loss, gx, *flat)
```

```python
import numpy as np
import jax
import jax.numpy as jnp
from jax import lax
from jax.experimental import pallas as pl
from jax.experimental.pallas import tpu as pltpu

F32 = jnp.float32
BF16 = jnp.bfloat16
MM = jnp.bfloat16

D_MODEL = 1024
CHUNK = 128
A_GROUPS = 4
A_WIDTH = 512
UV_W = 1024
QKV_W = 768
Z_W = 1024
IN_WIDTH = UV_W + QKV_W + Z_W
MEM_LEN = 256
N_BUCKETS = 32
MAX_DISTANCE = 128
EPS = 1e-6
NEG = -1e30
SCALE = 0.125
N_DEV = 8
SHARD_IN = IN_WIDTH // N_DEV
SHARD_O = D_MODEL // N_DEV

SQ_COL, SK_COL, SV_COL, MQ_COL, Z_COL = UV_W, UV_W + 256, UV_W + 384, UV_W + 512, UV_W + QKV_W
YB_OFF, YC_OFF = 512, 768

ADAM_LR = 0.001
ADAM_B1 = 0.9
ADAM_B2 = 0.999
ADAM_EPS = 1e-08
ADAM_WD = 0.01
ADAM_STEP = 10

VMEM_LIMIT = 60 * 1024 * 1024

_GELU_C = 0.7978845608028654
_GELU_A = 0.044715

MESH = pl.DeviceIdType.MESH
_ROWS = 32


def _dot(a, b):
    return lax.dot_general(a, b, (((1,), (0,)), ((), ())), preferred_element_type=F32)


def _dot_nt(a, b):
    return lax.dot_general(a, b, (((1,), (1,)), ((), ())), preferred_element_type=F32)


def _dot_tn(a, b):
    return lax.dot_general(a, b, (((0,), (0,)), ((), ())), preferred_element_type=F32)


def _gelu_and_grad(x):
    x2 = x * x
    t = jnp.tanh(_GELU_C * (x + _GELU_A * x * x2))
    g = 0.5 * x * (1.0 + t)
    dg = 0.5 * (1.0 + t) + 0.5 * x * (1.0 - t * t) * (_GELU_C * (1.0 + 3.0 * _GELU_A * x2))
    return g, dg


def _t5_buckets():
    qi = np.arange(CHUNK)[:, None]
    kj = np.arange(2 * CHUNK)[None, :]
    n = np.maximum(qi + CHUNK - kj, 0)
    max_exact = N_BUCKETS // 2
    large = max_exact + (np.log(np.maximum(n, 1) / max_exact) / np.log(MAX_DISTANCE / max_exact)
                         * (N_BUCKETS - max_exact)).astype(np.int32)
    large = np.minimum(large, N_BUCKETS - 1)
    return np.where(n < max_exact, n, large).astype(np.int32)


def _params(**kw):
    return pltpu.CompilerParams(vmem_limit_bytes=VMEM_LIMIT, **kw)


def _full(shape, single=False):
    nd = len(shape)
    if single:
        return pl.BlockSpec(shape, lambda *_: (0,) * nd, pipeline_mode=pl.Buffered(1))
    return pl.BlockSpec(shape, lambda *_: (0,) * nd)


def _window_valid():
    qi = lax.broadcasted_iota(jnp.int32, (CHUNK, 2 * CHUNK), 0)
    kj = lax.broadcasted_iota(jnp.int32, (CHUNK, 2 * CHUNK), 1)
    dist = qi + CHUNK - kj
    return (dist >= 0) & (dist < CHUNK)


def _position():
    return lax.axis_index("x"), lax.axis_index("y"), lax.axis_index("c")


def _other_chips(x, y):
    return [(1 - x, y), (x, 1 - y), (1 - x, 1 - y)]


def _route(x, y, c):
    first = (x * c + (1 - x) * (1 - c), y * (1 - c) + (1 - y) * c)
    second = (x * (1 - c) + (1 - x) * c, y * c + (1 - y) * (1 - c))
    return first, second, (1 - x, 1 - y)


def _remote(src, dst, ssem, rsem, to):
    return pltpu.make_async_remote_copy(src_ref=src, dst_ref=dst, send_sem=ssem, recv_sem=rsem,
                                        device_id=to, device_id_type=MESH)


def _rows_loop(nrow, fn):
    def step(i, _):
        fn(pl.ds(pl.multiple_of(i * _ROWS, _ROWS), _ROWS))
        return 0

    lax.fori_loop(0, nrow // _ROWS, step, 0)


class _Gather:
    def __init__(self, pos, out, ssem, rsem):
        self.x, self.y, self.c = pos
        self.out, self.ssem, self.rsem = out, ssem, rsem
        self.me = 4 * self.x + 2 * self.y + self.c
        self.here = (self.x, self.y, self.c)
        self.sib = (self.x, self.y, 1 - self.c)
        self.first, self.second, self.far = _route(*pos)

    def _copy(self, k, blk, to):
        r = self.out.at[blk]
        return _remote(r, r, self.ssem.at[k], self.rsem.at[k], to)

    def _idx(self, chip, core):
        return 4 * chip[0] + 2 * chip[1] + core

    def _on(self, chip):
        return (chip[0], chip[1], self.c)

    def start(self):
        self._copy(0, self.me, self.sib).start()
        self._copy(1, self.me, self._on(self.first)).start()
        self._copy(2, self.me, self._on(self.second)).start()

    def forward(self):
        c = self.c
        self._copy(1, self._idx(self.first, c), self.here).wait_recv()
        self._copy(3, self._idx(self.first, c), self._on(self.second)).start()
        self._copy(4, self._idx(self.first, c), self.sib).start()
        self._copy(2, self._idx(self.second, c), self.here).wait_recv()
        self._copy(5, self._idx(self.second, c), self.sib).start()
        self._copy(3, self._idx(self.far, c), self.here).wait_recv()
        self._copy(6, self._idx(self.far, c), self.sib).start()

    def finish(self):
        c = self.c
        self._copy(0, self._idx((self.x, self.y), 1 - c), self.here).wait_recv()
        for k, chip in ((4, self.second), (5, self.first), (6, self.far)):
            self._copy(k, self._idx(chip, 1 - c), self.here).wait_recv()
        self._copy(0, self.me, self.sib).wait_send()
        self._copy(1, self.me, self._on(self.first)).wait_send()
        self._copy(2, self.me, self._on(self.second)).wait_send()
        self._copy(3, self._idx(self.first, c), self._on(self.second)).wait_send()
        for k, chip in ((4, self.first), (5, self.second), (6, self.far)):
            self._copy(k, self._idx(chip, c), self.sib).wait_send()


def _prep_tables(rb_ref, w_ref, b_ref, bk_ref, bias_ref, wt_ref, wtt_ref, bcol_ref):
    valid = _window_valid()
    bk = bk_ref[...]
    acc = [jnp.full((CHUNK, 2 * CHUNK), NEG, F32) for _ in range(4)]
    for b in range(N_BUCKETS):
        hit = (bk == b) & valid
        for h in range(4):
            acc[h] = jnp.where(hit, rb_ref[b, h], acc[h])
    for h in range(4):
        bias_ref[h] = acc[h]
    r = lax.broadcasted_iota(jnp.int32, (CHUNK, CHUNK), 0)
    c = lax.broadcasted_iota(jnp.int32, (CHUNK, CHUNK), 1)
    for g in range(A_GROUPS):
        w = jnp.where(r >= c, w_ref[g], 0.0)
        wt_ref[g] = w.astype(MM)
        wtt_ref[g] = w.T.astype(MM)
        bcol_ref[g] = jnp.broadcast_to(b_ref[g:g + 1, :], (CHUNK, CHUNK)).T


def _wgather(a, b, c, rel_bias, w_sp, b_sp, buckets, mem2, gm):
    tmem = mem2.shape[0]

    def body(a_ref, b_ref, c_ref, rb_ref, w_ref, bsp_ref, bk_ref, m_ref, gm_ref,
             oa, ob, oc, bias_ref, wt_ref, wtt_ref, bcol_ref, mkv_ref, ssem, rsem):
        pos = _position()
        me = 4 * pos[0] + 2 * pos[1] + pos[2]
        gathers = []
        for k, (src, out) in enumerate(((c_ref, oc), (b_ref, ob), (a_ref, oa))):
            out[me] = src[...].astype(BF16)
            g = _Gather(pos, out, ssem.at[k], rsem.at[k])
            g.start()
            gathers.append(g)
        _prep_tables(rb_ref, w_ref, bsp_ref, bk_ref, bias_ref, wt_ref, wtt_ref, bcol_ref)
        gathers[0].forward()
        gathers[0].finish()
        xf = m_ref[...]
        hm = (xf * _rms(xf) * gm_ref[...]).astype(MM)
        acc = jnp.zeros((tmem, 2 * MEM_LEN), F32)
        for d in range(N_DEV):
            acc = acc + _dot(hm[:, d * SHARD_O:(d + 1) * SHARD_O], oc[d])
        mkv_ref[...] = acc.astype(MM)
        for g in gathers[1:]:
            g.forward()
        for g in gathers[1:]:
            g.finish()

    vm = pl.BlockSpec(memory_space=pltpu.VMEM)
    grp = (A_GROUPS, CHUNK, CHUNK)
    return pl.pallas_call(
        body, name="wgather",
        out_shape=(jax.ShapeDtypeStruct((N_DEV,) + a.shape, BF16),
                   jax.ShapeDtypeStruct((N_DEV,) + b.shape, BF16),
                   jax.ShapeDtypeStruct((N_DEV,) + c.shape, BF16),
                   jax.ShapeDtypeStruct((4, CHUNK, 2 * CHUNK), F32),
                   jax.ShapeDtypeStruct(grp, MM), jax.ShapeDtypeStruct(grp, MM), jax.ShapeDtypeStruct(grp, F32),
                   jax.ShapeDtypeStruct((tmem, 2 * MEM_LEN), MM)),
        in_specs=[vm, vm, vm, pl.BlockSpec(memory_space=pltpu.SMEM), vm, vm, vm, vm, vm],
        out_specs=tuple([vm] * 8),
        scratch_shapes=[pltpu.SemaphoreType.DMA((3, 7)), pltpu.SemaphoreType.DMA((3, 7))],
        compiler_params=_params(),
    )(a, b, c, rel_bias, w_sp, b_sp, buckets, mem2, gm)


def _half_masks(rows):
    lane = lax.broadcasted_iota(jnp.int32, (rows, CHUNK), 1)
    return lane < 64


def _dup_heads(band):
    b32 = band.astype(F32)
    rolled = pltpu.roll(b32, 64, 1)
    lo = _half_masks(band.shape[0])
    return (jnp.where(lo, b32, rolled).astype(MM), jnp.where(lo, rolled, b32).astype(MM))


def _swa_probs(qsel, kd, bias_h, sink_h, first_add):
    s = _dot_nt(qsel, kd) * SCALE + bias_h + first_add
    m = jnp.maximum(jnp.max(s, axis=-1, keepdims=True), sink_h)
    p = jnp.exp(s - m)
    es = jnp.exp(sink_h - m)
    inv = 1.0 / (jnp.sum(p, axis=-1, keepdims=True) + es)
    return p * inv, es * inv


def _softmax(s):
    m = jnp.max(s, axis=-1, keepdims=True)
    p = jnp.exp(s - m)
    return p * (1.0 / jnp.sum(p, axis=-1, keepdims=True))


def _first_block_mask(n):
    col = lax.broadcasted_iota(jnp.int32, (CHUNK, 2 * CHUNK), 1)
    return jnp.where((col < CHUNK) & (n == 0), NEG, 0.0)


def _rms(xf):
    return lax.rsqrt(jnp.mean(xf * xf, axis=-1, keepdims=True) + EPS)


def _layer(x2, tgt2, mkv3, bias, sinks, vg, vb, wt, wtt, bcol, g1, g2, w_in_t, w_o, buckets, nb, s, tm):
    nt = s // tm
    bpt = tm // CHUNK
    bps = s // CHUNK
    t = nb * s

    def body(x_ref, xp_ref, t_ref, mkv_ref, bias_ref, sink_ref, vg_ref, vb_ref, wt_ref, wtt_ref, bcol_ref,
             g1_ref, g2_ref, wi_ref, wo_ref, bk_ref,
             gx_ref, dmkv_ref, dwi_hbm, dwo_hbm, dg1_ref, dg2_ref, loss_ref, dwsp_ref, dbs_ref,
             dvg_ref, dvb_ref, dsink_ref, drel_ref,
             acc_i, acc_o, uv_s, z_s, q_s, kv_s, h_s, dp_s, dxo_s,
             ycat, dyc, u_s, gu_s, gv_s, xh_s, rs_s, sv_s, vc_s, pb_s, ps_s, pc_s, kd_s, vd_s,
             dkv_acc, dbias_acc, dsv_acc, dsink_acc, sems):
        b, j = pl.program_id(0), pl.program_id(1)
        jt = nt - 1 - j

        @pl.when((b == 0) & (j == 0))
        def _():
            for ref in (acc_i, acc_o, dg1_ref, dg2_ref, loss_ref, dwsp_ref, dvg_ref, dvb_ref,
                        dbias_acc, dsv_acc, dsink_acc):
                ref[...] = jnp.zeros_like(ref)

        @pl.when(j == 0)
        def _():
            dmkv_ref[...] = jnp.zeros_like(dmkv_ref)
            dkv_acc[...] = jnp.zeros_like(dkv_acc)

        carry = dkv_acc[0:CHUNK, :]
        dkv_acc[...] = jnp.zeros_like(dkv_acc)
        dkv_acc[tm:tm + CHUNK, :] = carry

        lo = _half_masks(CHUNK)
        lob = _half_masks(2 * CHUNK)
        lot = _half_masks(tm)
        g1v = g1_ref[...]

        xf = x_ref[...]
        h = (xf * _rms(xf) * g1v).astype(MM)
        h_s[...] = h
        uv_s[...] = _dot_nt(h, wi_ref[0:UV_W, :])
        qkv = _dot_nt(h, wi_ref[SQ_COL:Z_COL, :])
        q_s[:, 0:256] = qkv[:, 0:256].astype(MM)
        q_s[:, 256:512] = qkv[:, 512:768].astype(MM)
        kv_s[CHUNK:CHUNK + tm, :] = qkv[:, 256:512].astype(MM)
        z_s[...] = _dot_nt(h, wi_ref[Z_COL:IN_WIDTH, :])
        xp = xp_ref[...]
        hp = (xp * _rms(xp) * g1v).astype(MM)
        kv_s[0:CHUNK, :] = _dot_nt(hp, wi_ref[SK_COL:MQ_COL, :]).astype(MM)

        for blk in range(bpt):
            r0 = blk * CHUNK
            rows = slice(r0, r0 + CHUNK)
            n = jt * bpt + blk
            for g in range(A_GROUPS):
                cg = slice(g * CHUNK, (g + 1) * CHUNK)
                u, gu = _gelu_and_grad(uv_s[rows, cg])
                v, gv = _gelu_and_grad(uv_s[rows, A_WIDTH + g * CHUNK:A_WIDTH + (g + 1) * CHUNK])
                mu = jnp.mean(v, axis=-1, keepdims=True)
                xc = v - mu
                rstd = lax.rsqrt(jnp.mean(xc * xc, axis=-1, keepdims=True) + EPS)
                xhat = xc * rstd
                vc = (xhat * vg_ref[:, cg] + vb_ref[:, cg]).astype(MM)
                sv = _dot(wt_ref[g], vc) + bcol_ref[g]
                u_s[rows, cg] = u
                gu_s[rows, cg] = gu
                gv_s[rows, cg] = gv
                xh_s[rows, cg] = xhat
                rs_s[rows, cg] = jnp.broadcast_to(rstd, (CHUNK, CHUNK))
                sv_s[rows, cg] = sv
                vc_s[rows, cg] = vc
                ycat[rows, cg] = u * sv
            kd = _dup_heads(kv_s[r0:r0 + 2 * CHUNK, 0:CHUNK])
            vd = _dup_heads(kv_s[r0:r0 + 2 * CHUNK, CHUNK:2 * CHUNK])
            first_add = _first_block_mask(n)
            for kvh in range(2):
                kd_s[blk * 2 + kvh] = kd[kvh]
                vd_s[blk * 2 + kvh] = vd[kvh]
                q128 = q_s[rows, kvh * CHUNK:(kvh + 1) * CHUNK].astype(F32)
                outs = []
                for gi in range(2):
                    hd = 2 * kvh + gi
                    qsel = jnp.where(lo if gi == 0 else ~lo, q128, 0.0).astype(MM)
                    probs, ps = _swa_probs(qsel, kd[kvh], bias_ref[hd], sink_ref[hd], first_add)
                    pb_s[blk * 4 + hd] = probs
                    ps_s[blk * 4 + hd] = jnp.broadcast_to(ps, (CHUNK, CHUNK))
                    outs.append(_dot(probs.astype(MM), vd[kvh]))
                ycat[rows, YB_OFF + kvh * CHUNK:YB_OFF + (kvh + 1) * CHUNK] = jnp.where(lo, outs[0], outs[1])
        for g in range(2):
            q128 = q_s[:, 256 + g * CHUNK:256 + (g + 1) * CHUNK].astype(F32)
            k128 = mkv_ref[:, g * CHUNK:(g + 1) * CHUNK]
            v128 = mkv_ref[:, MEM_LEN + g * CHUNK:MEM_LEN + (g + 1) * CHUNK]
            outs = []
            for hh in range(2):
                qsel = jnp.where(lot if hh == 0 else ~lot, q128, 0.0).astype(MM)
                probs = _softmax(_dot_nt(qsel, k128) * SCALE)
                pc_s[2 * g + hh] = probs
                outs.append(_dot(probs.astype(MM), v128))
            ycat[:, YC_OFF + g * CHUNK:YC_OFF + (g + 1) * CHUNK] = jnp.where(lot, outs[0], outs[1])

        zt = z_s[...]
        sig = 1.0 / (1.0 + jnp.exp(-zt))
        silu = zt * sig
        yc = ycat[...]
        yb = (yc * silu).astype(MM)
        o = _dot(yb, wo_ref[...])
        r2 = _rms(o)
        nrm = o * r2
        g2v = g2_ref[...]
        e = x_ref[...] + nrm * g2v - t_ref[...]
        l1 = jnp.sum(e * e, axis=-1, keepdims=True)
        loss_ref[...] += jnp.broadcast_to(jnp.sum(l1, axis=0, keepdims=True) * (0.5 / D_MODEL), loss_ref.shape)
        dxo = e * (1.0 / D_MODEL)
        dxo_s[...] = dxo
        dg2_ref[...] += jnp.sum(dxo * nrm, axis=0, keepdims=True)
        dn = dxo * g2v
        do = r2 * (dn - nrm * jnp.mean(dn * nrm, axis=-1, keepdims=True))
        dob = do.astype(MM)
        dy = _dot_nt(dob, wo_ref[...])
        dp_s[:, Z_COL:IN_WIDTH] = (dy * yc * (sig * (1.0 + zt * (1.0 - sig)))).astype(MM)
        dyc[...] = dy * silu
        acc_o[...] += _dot_tn(yb, dob)

        for blk in range(bpt):
            r0 = blk * CHUNK
            rows = slice(r0, r0 + CHUNK)
            for g in range(A_GROUPS):
                cg = slice(g * CHUNK, (g + 1) * CHUNK)
                cv = slice(A_WIDTH + g * CHUNK, A_WIDTH + (g + 1) * CHUNK)
                dya = dyc[rows, cg]
                dp_s[rows, cg] = (dya * sv_s[rows, cg] * gu_s[rows, cg]).astype(MM)
                dsv = dya * u_s[rows, cg]
                dsvb = dsv.astype(MM)
                dsv_acc[g] += dsv
                dwsp_ref[g] += _dot_nt(dsvb, vc_s[rows, cg])
                dvc = _dot(wtt_ref[g], dsvb)
                xhat = xh_s[rows, cg]
                dvg_ref[:, cg] += jnp.sum(dvc * xhat, axis=0, keepdims=True)
                dvb_ref[:, cg] += jnp.sum(dvc, axis=0, keepdims=True)
                dxh = dvc * vg_ref[:, cg]
                dv = rs_s[rows, cg] * (dxh - jnp.mean(dxh, axis=-1, keepdims=True)
                                       - xhat * jnp.mean(dxh * xhat, axis=-1, keepdims=True))
                dp_s[rows, cv] = (dv * gv_s[rows, cg]).astype(MM)
            dk_f, dv_f = [], []
            for kvh in range(2):
                kd = kd_s[blk * 2 + kvh]
                vd = vd_s[blk * 2 + kvh]
                q128 = q_s[rows, kvh * CHUNK:(kvh + 1) * CHUNK].astype(F32)
                do128 = dyc[rows, YB_OFF + kvh * CHUNK:YB_OFF + (kvh + 1) * CHUNK]
                dq128 = jnp.zeros((CHUNK, CHUNK), F32)
                dkd = jnp.zeros((2 * CHUNK, CHUNK), F32)
                dvd = jnp.zeros((2 * CHUNK, CHUNK), F32)
                for gi in range(2):
                    hd = 2 * kvh + gi
                    half = lo if gi == 0 else ~lo
                    qsel = jnp.where(half, q128, 0.0).astype(MM)
                    dosel = jnp.where(half, do128, 0.0).astype(MM)
                    probs = pb_s[blk * 4 + hd]
                    ps = ps_s[blk * 4 + hd][:, 0:1]
                    dp = _dot_nt(dosel, vd)
                    delta = jnp.sum(probs * dp, axis=-1, keepdims=True)
                    ds = probs * (dp - delta)
                    dbias_acc[hd] += ds
                    dsink_acc[hd:hd + 1, :] += jnp.broadcast_to(-jnp.sum(ps * delta, axis=0, keepdims=True), (1, CHUNK))
                    dss = (ds * SCALE).astype(MM)
                    dq128 = dq128 + jnp.where(half, _dot(dss, kd), 0.0)
                    dkd = dkd + _dot_tn(dss, qsel)
                    dvd = dvd + _dot_tn(probs.astype(MM), dosel)
                dp_s[rows, SQ_COL + kvh * CHUNK:SQ_COL + (kvh + 1) * CHUNK] = dq128.astype(MM)
                dk_f.append(dkd + pltpu.roll(dkd, 64, 1))
                dv_f.append(dvd + pltpu.roll(dvd, 64, 1))
            dkv_acc[r0:r0 + 2 * CHUNK, 0:CHUNK] += jnp.where(lob, dk_f[0], dk_f[1])
            dkv_acc[r0:r0 + 2 * CHUNK, CHUNK:2 * CHUNK] += jnp.where(lob, dv_f[0], dv_f[1])
        dp_s[:, SK_COL:MQ_COL] = dkv_acc[CHUNK:CHUNK + tm, :].astype(MM)
        for g in range(2):
            q128 = q_s[:, 256 + g * CHUNK:256 + (g + 1) * CHUNK].astype(F32)
            k128 = mkv_ref[:, g * CHUNK:(g + 1) * CHUNK]
            v128 = mkv_ref[:, MEM_LEN + g * CHUNK:MEM_LEN + (g + 1) * CHUNK]
            do128 = dyc[:, YC_OFF + g * CHUNK:YC_OFF + (g + 1) * CHUNK]
            dq128 = jnp.zeros((tm, CHUNK), F32)
            dk128 = jnp.zeros((MEM_LEN, CHUNK), F32)
            dv128 = jnp.zeros((MEM_LEN, CHUNK), F32)
            for hh in range(2):
                half = lot if hh == 0 else ~lot
                qsel = jnp.where(half, q128, 0.0).astype(MM)
                dosel = jnp.where(half, do128, 0.0).astype(MM)
                probs = pc_s[2 * g + hh]
                dp = _dot_nt(dosel, v128)
                ds = probs * (dp - jnp.sum(probs * dp, axis=-1, keepdims=True))
                dss = (ds * SCALE).astype(MM)
                dq128 = dq128 + jnp.where(half, _dot(dss, k128), 0.0)
                dk128 = dk128 + _dot_tn(dss, qsel)
                dv128 = dv128 + _dot_tn(probs.astype(MM), dosel)
            dp_s[:, MQ_COL + g * CHUNK:MQ_COL + (g + 1) * CHUNK] = dq128.astype(MM)
            dmkv_ref[:, g * CHUNK:(g + 1) * CHUNK] += dk128
            dmkv_ref[:, MEM_LEN + g * CHUNK:MEM_LEN + (g + 1) * CHUNK] += dv128

        hv = h_s[...]
        dh = jnp.zeros((tm, D_MODEL), F32)
        for c0, c1 in ((0, UV_W), (SQ_COL, Z_COL), (Z_COL, IN_WIDTH)):
            dpt = dp_s[:, c0:c1]
            acc_i[c0:c1, :] += _dot_tn(dpt, hv)
            dh = dh + _dot(dpt, wi_ref[c0:c1, :])
        xf = x_ref[...]
        r = _rms(xf)
        nx = xf * r
        dg1_ref[...] += jnp.sum(dh * nx, axis=0, keepdims=True)
        dnx = dh * g1v
        gx_ref[...] = dxo_s[...] + r * (dnx - nx * jnp.mean(dnx * nx, axis=-1, keepdims=True))

        @pl.when((b == nb - 1) & (j == nt - 1))
        def _():
            out_i = pltpu.make_async_copy(acc_i, dwi_hbm, sems.at[0])
            out_o = pltpu.make_async_copy(acc_o, dwo_hbm, sems.at[1])
            out_i.start()
            out_o.start()
            r_ = lax.broadcasted_iota(jnp.int32, (CHUNK, CHUNK), 0)
            c_ = lax.broadcasted_iota(jnp.int32, (CHUNK, CHUNK), 1)
            for g in range(A_GROUPS):
                dwsp_ref[g] = jnp.where(r_ >= c_, dwsp_ref[g], 0.0)
                dbs_ref[g:g + 1, :] = jnp.sum(dsv_acc[g].T, axis=0, keepdims=True)
            rows8 = lax.broadcasted_iota(jnp.int32, (8, CHUNK), 0)
            cols8 = lax.broadcasted_iota(jnp.int32, (8, CHUNK), 1)
            sk = jnp.zeros((8, CHUNK), F32)
            for hd in range(4):
                sk = sk + jnp.where((rows8 == 0) & (cols8 == hd),
                                    jnp.broadcast_to(dsink_acc[hd:hd + 1, :], (8, CHUNK)), 0.0)
            dsink_ref[...] = sk
            bk = bk_ref[...]
            valid = _window_valid()
            rrow = lax.broadcasted_iota(jnp.int32, (N_BUCKETS, CHUNK), 0)
            rcol = lax.broadcasted_iota(jnp.int32, (N_BUCKETS, CHUNK), 1)
            acc = jnp.zeros((N_BUCKETS, CHUNK), F32)
            for bb in range(N_BUCKETS):
                hit = (bk == bb) & valid
                for hd in range(4):
                    part = jnp.sum(jnp.where(hit, dbias_acc[hd], 0.0), axis=-1, keepdims=True)
                    tot = jnp.sum(part, axis=0, keepdims=True)
                    acc = acc + jnp.where((rrow == bb) & (rcol == hd), jnp.broadcast_to(tot, (N_BUCKETS, CHUNK)), 0.0)
            drel_ref[...] = acc
            out_i.wait()
            out_o.wait()

    tile = lambda w: pl.BlockSpec((tm, w), lambda b, j: (b * nt + nt - 1 - j, 0))
    prev_block = pl.BlockSpec((CHUNK, D_MODEL), lambda b, j: (b * bps + jnp.maximum((nt - 1 - j) * bpt - 1, 0), 0))
    per_batch = lambda r, w: pl.BlockSpec((None, r, w), lambda b, j: (b, 0, 0))
    anyspec = pl.BlockSpec(memory_space=pl.ANY)
    grp = (A_GROUPS, CHUNK, CHUNK)
    return pl.pallas_call(
        body, name="layer", grid=(nb, nt),
        out_shape=(jax.ShapeDtypeStruct((t, D_MODEL), F32),
                   jax.ShapeDtypeStruct((nb, MEM_LEN, 2 * MEM_LEN), F32),
                   jax.ShapeDtypeStruct((IN_WIDTH, D_MODEL), F32),
                   jax.ShapeDtypeStruct((D_MODEL, D_MODEL), F32),
                   jax.ShapeDtypeStruct((1, D_MODEL), F32),
                   jax.ShapeDtypeStruct((1, D_MODEL), F32),
                   jax.ShapeDtypeStruct((8, CHUNK), F32),
                   jax.ShapeDtypeStruct(grp, F32),
                   jax.ShapeDtypeStruct((A_GROUPS, CHUNK), F32),
                   jax.ShapeDtypeStruct((1, A_WIDTH), F32),
                   jax.ShapeDtypeStruct((1, A_WIDTH), F32),
                   jax.ShapeDtypeStruct((8, CHUNK), F32),
                   jax.ShapeDtypeStruct((N_BUCKETS, CHUNK), F32)),
        in_specs=[tile(D_MODEL), prev_block, tile(D_MODEL), per_batch(MEM_LEN, 2 * MEM_LEN),
                  _full((4, CHUNK, 2 * CHUNK)),
                  pl.BlockSpec(memory_space=pltpu.SMEM),
                  _full((1, A_WIDTH)), _full((1, A_WIDTH)),
                  _full(grp), _full(grp), _full(grp),
                  _full((1, D_MODEL)), _full((1, D_MODEL)),
                  _full((IN_WIDTH, D_MODEL), single=True), _full((D_MODEL, D_MODEL), single=True),
                  _full((CHUNK, 2 * CHUNK))],
        out_specs=(tile(D_MODEL), per_batch(MEM_LEN, 2 * MEM_LEN), anyspec, anyspec,
                   _full((1, D_MODEL)), _full((1, D_MODEL)), _full((8, CHUNK)),
                   _full(grp), _full((A_GROUPS, CHUNK)), _full((1, A_WIDTH)), _full((1, A_WIDTH)),
                   _full((8, CHUNK)), _full((N_BUCKETS, CHUNK))),
        scratch_shapes=[pltpu.VMEM((IN_WIDTH, D_MODEL), F32), pltpu.VMEM((D_MODEL, D_MODEL), F32),
                        pltpu.VMEM((tm, UV_W), F32), pltpu.VMEM((tm, Z_W), F32),
                        pltpu.VMEM((tm, 512), MM), pltpu.VMEM((tm + CHUNK, 2 * CHUNK), MM),
                        pltpu.VMEM((tm, D_MODEL), MM), pltpu.VMEM((tm, IN_WIDTH), MM),
                        pltpu.VMEM((tm, D_MODEL), F32),
                        pltpu.VMEM((tm, D_MODEL), F32), pltpu.VMEM((tm, D_MODEL), F32)]
                       + [pltpu.VMEM((tm, A_WIDTH), F32) for _ in range(6)]
                       + [pltpu.VMEM((tm, A_WIDTH), MM),
                          pltpu.VMEM((bpt * 4, CHUNK, 2 * CHUNK), F32),
                          pltpu.VMEM((bpt * 4, CHUNK, CHUNK), F32),
                          pltpu.VMEM((4, tm, MEM_LEN), F32),
                          pltpu.VMEM((bpt * 2, 2 * CHUNK, CHUNK), MM),
                          pltpu.VMEM((bpt * 2, 2 * CHUNK, CHUNK), MM),
                          pltpu.VMEM((tm + CHUNK, 2 * CHUNK), F32),
                          pltpu.VMEM((4, CHUNK, 2 * CHUNK), F32),
                          pltpu.VMEM(grp, F32),
                          pltpu.VMEM((8, CHUNK), F32),
                          pltpu.SemaphoreType.DMA((2,))],
        compiler_params=_params(dimension_semantics=("arbitrary", "arbitrary")),
    )(x2, x2, tgt2, mkv3, bias, sinks, vg, vb, wt, wtt, bcol, g1, g2, w_in_t, w_o, buckets)


class _ShardReduce:
    def __init__(self, pos, g, bufs, sems):
        self.x, self.y, self.c = pos
        self.g = g
        self.own, self.rcv, self.sbuf, self.rbuf, self.cbuf = bufs
        self.ld, self.sa, self.ra, self.sb, self.rb = sems
        self.nrow = g.shape[1]
        self.here = (self.x, self.y, self.c)
        self.sib = (self.x, self.y, 1 - self.c)
        self.first, self.second, self.far = _route(*pos)

    def _load(self, q):
        return pltpu.make_async_copy(self.g.at[2 * q + self.c], self.own.at[q], self.ld.at[q])

    def _to_sib(self, q, to):
        return _remote(self.g.at[2 * q + 1 - self.c], self.rcv.at[q], self.sa.at[q], self.ra.at[q], to)

    def _send(self, k, to):
        dst = self.cbuf.at[0] if k == 1 else self.rbuf.at[0 if k == 0 else 1]
        return _remote(self.sbuf.at[k], dst, self.sb.at[k], self.rb.at[k], to)

    def _stage(self, k, which, extra=None):
        def cast(r):
            v = self.rcv[which, r, :]
            if extra is not None:
                v = v + extra[0, r, :].astype(F32)
            self.sbuf[k, r, :] = v.astype(BF16)

        _rows_loop(self.nrow, cast)

    @staticmethod
    def _q(chip):
        return 2 * chip[0] + chip[1]

    def start(self):
        for q in range(4):
            self._load(q).start()
            self._to_sib(q, self.sib).start()

    def mid(self):
        for q in range(4):
            self._load(q).wait()
            self._to_sib(q, self.here).wait_recv()

        def add(r):
            for q in range(4):
                self.rcv[q, r, :] = self.rcv[q, r, :] + self.own[q, r, :]

        _rows_loop(self.nrow, add)
        to_first = (self.first[0], self.first[1], self.c)
        self._stage(0, self._q(self.first))
        self._send(0, to_first).start()
        self._stage(1, self._q(self.far))
        self._send(1, to_first).start()

    def pass_on(self):
        self._send(1, self.here).wait_recv()
        self._stage(2, self._q(self.second), extra=self.cbuf)
        self._send(2, (self.second[0], self.second[1], self.c)).start()

    def finish(self, out):
        self._send(0, self.here).wait_recv()
        self._send(2, self.here).wait_recv()
        which = 2 * self.x + self.y

        def tot(r):
            out[r, :] = (self.rcv[which, r, :] + self.rbuf[0, r, :].astype(F32)) + self.rbuf[1, r, :].astype(F32)

        _rows_loop(self.nrow, tot)
        for q in range(4):
            self._to_sib(q, self.sib).wait_send()
        to_first = (self.first[0], self.first[1], self.c)
        self._send(0, to_first).wait_send()
        self._send(1, to_first).wait_send()
        self._send(2, (self.second[0], self.second[1], self.c)).wait_send()


def _reduce_scratch(shape):
    return [pltpu.VMEM((4,) + shape, F32), pltpu.VMEM((4,) + shape, F32),
            pltpu.VMEM((3,) + shape, BF16), pltpu.VMEM((2,) + shape, BF16), pltpu.VMEM((1,) + shape, BF16),
            pltpu.SemaphoreType.DMA((4,)), pltpu.SemaphoreType.DMA((4,)), pltpu.SemaphoreType.DMA((4,)),
            pltpu.SemaphoreType.DMA((3,)), pltpu.SemaphoreType.DMA((3,))]


_N_RED = 10

_S_LAYOUT = (((1, D_MODEL), 0), ((1, D_MODEL), 8), ((1, D_MODEL), 16),
             ((1, A_WIDTH), 24), ((1, A_WIDTH), 28), ((A_GROUPS, CHUNK), 32),
             ((1, 4), 36), ((N_BUCKETS, 4), 40),
             ((A_GROUPS * CHUNK, CHUNK), 72))
_LOSS_ROW = 37
_S_ROWS = 72 + A_GROUPS * CHUNK
_N_SMALL = len(_S_LAYOUT)


def _pack_rows(dst, refs):
    for (shp, r0), ref in zip(_S_LAYOUT, refs):
        if shp[0] == 1 and shp[1] >= CHUNK:
            for i in range(shp[1] // CHUNK):
                dst[r0 + i:r0 + i + 1, :] = ref[:, i * CHUNK:(i + 1) * CHUNK]
        elif ref.shape[-1] == CHUNK:
            dst[r0:r0 + shp[0], :] = ref[0:shp[0], :]
        else:
            dst[r0:r0 + shp[0], 0:shp[1]] = ref[...]


def _unpack_rows(src, refs):
    for (shp, r0), ref in zip(_S_LAYOUT, refs):
        if shp[0] == 1 and shp[1] >= CHUNK:
            for i in range(shp[1] // CHUNK):
                ref[:, i * CHUNK:(i + 1) * CHUNK] = src[r0 + i:r0 + i + 1, :]
        elif shp[1] == CHUNK:
            ref[...] = src[r0:r0 + shp[0], :]
        else:
            ref[...] = src[r0:r0 + shp[0], 0:shp[1]]


_MEM_G = 2


def _greduce(ga, gb, dmkv, mem2, gm, w_mkv, small_g, loss_p):
    shp_c = (SHARD_O, 2 * MEM_LEN)
    shapes = (shp_c, gb.shape[1:], ga.shape[1:])
    rs = _S_ROWS

    def body(*refs):
        it = iter(refs)
        take = lambda n: [next(it) for _ in range(n)]
        gb_ref, ga_ref, d_ref, m_ref, gm_ref, wm_ref = take(6)
        sg_refs = take(_N_SMALL - 1)
        loss_ref, = take(1)
        oc, ob, oa, ogs = take(4)
        red = take(3 * _N_RED)
        gs_ref, rs_a, rs_b, gc_ref, dgm_ref = take(5)
        ssem_a, rsem_a, ssem_b, rsem_b = take(4)

        pos = _position()
        x, y, cc = pos
        myq = 2 * x + y
        here, sib = (x, y, cc), (x, y, 1 - cc)
        chips = _other_chips(x, y)
        reducers = [_ShardReduce(pos, g, red[k * _N_RED:k * _N_RED + 5], red[k * _N_RED + 5:(k + 1) * _N_RED])
                    for k, g in enumerate((gc_ref, gb_ref, ga_ref))]
        for rd in reducers[1:]:
            rd.start()

        xf = m_ref[...]
        nm = xf * _rms(xf)
        hm = (nm * gm_ref[...]).astype(MM)
        d = d_ref[...].astype(MM)
        for o in range(N_DEV):
            gc_ref[o] = _dot_tn(hm[:, o * SHARD_O:(o + 1) * SHARD_O], d)
        dgm_ref[...] = jnp.sum(_dot_nt(d, wm_ref[...]) * nm, axis=0, keepdims=True)
        reducers[0].start()

        gs_ref[...] = jnp.zeros_like(gs_ref)
        _pack_rows(gs_ref, sg_refs[:_MEM_G] + [dgm_ref] + sg_refs[_MEM_G:])
        gs_ref[_LOSS_ROW:_LOSS_ROW + 1, :] = loss_ref[0:1, :]
        small_a = _remote(gs_ref, rs_a, ssem_a, rsem_a, sib)
        small_a.start()

        _remote(gs_ref, rs_a, ssem_a, rsem_a, here).wait_recv()
        rs_b[myq] = gs_ref[...] + rs_a[...]
        small_b = [_remote(rs_b.at[myq], rs_b.at[myq], ssem_b.at[j], rsem_b.at[j], (chip[0], chip[1], cc))
                   for j, chip in enumerate(chips)]
        for cp in small_b:
            cp.start()
        for rd in reducers:
            rd.mid()
        for rd in reducers:
            rd.pass_on()

        for j in range(3):
            _remote(rs_b.at[myq], rs_b.at[myq], ssem_b.at[j], rsem_b.at[j], here).wait_recv()

        def tot_s(i, _):
            r = pl.ds(pl.multiple_of(i * 8, 8), 8)
            ogs[r, :] = ((rs_b[0, r, :] + rs_b[1, r, :]) + rs_b[2, r, :]) + rs_b[3, r, :]
            return 0

        lax.fori_loop(0, rs // 8, tot_s, 0)
        for rd, out in zip(reducers, (oc, ob, oa)):
            rd.finish(out)
        small_a.wait_send()
        for cp in small_b:
            cp.wait_send()

    vm = pl.BlockSpec(memory_space=pltpu.VMEM)
    anyspec = pl.BlockSpec(memory_space=pl.ANY)
    scratch = []
    for shp in shapes:
        scratch += _reduce_scratch(shp)
    scratch += [pltpu.VMEM((rs, CHUNK), F32), pltpu.VMEM((rs, CHUNK), F32), pltpu.VMEM((4, rs, CHUNK), F32),
                pltpu.VMEM((N_DEV,) + shp_c, F32), pltpu.VMEM((1, D_MODEL), F32),
                pltpu.SemaphoreType.DMA, pltpu.SemaphoreType.DMA,
                pltpu.SemaphoreType.DMA((3,)), pltpu.SemaphoreType.DMA((3,))]
    tc, tb, ta, ts = pl.pallas_call(
        body, name="greduce",
        out_shape=tuple([jax.ShapeDtypeStruct(shp, F32) for shp in shapes] + [jax.ShapeDtypeStruct((rs, CHUNK), F32)]),
        in_specs=[anyspec] * 2 + [vm] * (4 + _N_SMALL),
        out_specs=(vm, vm, vm, vm),
        scratch_shapes=scratch,
        compiler_params=_params(),
    )(gb, ga, dmkv, mem2, gm, w_mkv, *small_g, loss_p)
    return ta, tb, tc, ts


def _adamw(w, g, m, v):
    m = ADAM_B1 * m + (1.0 - ADAM_B1) * g
    v = ADAM_B2 * v + (1.0 - ADAM_B2) * (g * g)
    m_hat = m / (1.0 - ADAM_B1 ** ADAM_STEP)
    v_hat = v / (1.0 - ADAM_B2 ** ADAM_STEP)
    delta = -ADAM_LR * (m_hat / (jnp.sqrt(v_hat) + ADAM_EPS) + ADAM_WD * w)
    return delta, m, v


def _update(ta, tb, tc, ts, big_wmv, small_wmv):
    shapes = (ta.shape, tb.shape, tc.shape)
    rs = _S_ROWS
    small_shapes = [tuple(a.shape) for a in small_wmv[0]]

    def body(*refs):
        it = iter(refs)
        take = lambda n: [next(it) for _ in range(n)]
        ga_ref, gb_ref, gc_ref, gs_ref = take(4)
        wa, ma, va, wb, mb, vb_, wc, mc, vc = take(9)
        sw_refs, sm_refs, sv_refs = take(_N_SMALL), take(_N_SMALL), take(_N_SMALL)
        oga, oda, oma, ova, ogb, odb, omb, ovb, ogc, odc, omc, ovc = take(12)
        so_refs = [take(_N_SMALL) for _ in range(4)]
        loss_out, = take(1)
        ws, ms, vs, ods, oms, ovs = take(6)

        for buf in (ws, ms, vs):
            buf[...] = jnp.zeros_like(buf)
        _pack_rows(ws, sw_refs)
        _pack_rows(ms, sm_refs)
        _pack_rows(vs, sv_refs)

        big = ((ga_ref, wa, ma, va, oga, oda, oma, ova), (gb_ref, wb, mb, vb_, ogb, odb, omb, ovb),
               (gc_ref, wc, mc, vc, ogc, odc, omc, ovc))
        for arr in range(3):
            g_r, w_r, m_r, v_r, og, od, om, ov = big[arr]

            def upd(r, g_r=g_r, w_r=w_r, m_r=m_r, v_r=v_r, og=og, od=od, om=om, ov=ov):
                g = g_r[r, :]
                d, m, v = _adamw(w_r[r, :], g, m_r[r, :], v_r[r, :])
                og[r, :] = g
                od[r, :] = d
                om[r, :] = m
                ov[r, :] = v

            _rows_loop(shapes[arr][0], upd)

        def upd_s(i, _):
            r = pl.ds(pl.multiple_of(i * 8, 8), 8)
            d, m, v = _adamw(ws[r, :], gs_ref[r, :], ms[r, :], vs[r, :])
            ods[r, :] = d
            oms[r, :] = m
            ovs[r, :] = v
            return 0

        lax.fori_loop(0, rs // 8, upd_s, 0)
        for k, buf in enumerate((gs_ref, ods, oms, ovs)):
            _unpack_rows(buf, so_refs[k])
        loss_out[...] = gs_ref[_LOSS_ROW:_LOSS_ROW + 1, 0:1]

    vm = pl.BlockSpec(memory_space=pltpu.VMEM)
    big_out = []
    for shp in shapes:
        big_out += [jax.ShapeDtypeStruct(shp, F32)] * 4
    small_out = [jax.ShapeDtypeStruct(shp, F32) for shp in small_shapes] * 4
    out_shape = tuple(big_out + small_out + [jax.ShapeDtypeStruct((1, 1), F32)])
    n_in = 4 + 9 + 3 * _N_SMALL
    return pl.pallas_call(
        body, name="update",
        out_shape=out_shape,
        in_specs=[vm] * n_in,
        out_specs=tuple([vm] * len(out_shape)),
        scratch_shapes=[pltpu.VMEM((rs, CHUNK), F32) for _ in range(6)],
        compiler_params=_params(),
    )(ta, tb, tc, ts, *big_wmv, *small_wmv[0], *small_wmv[1], *small_wmv[2])


def kernel(x, mem, pre_norm_g, post_norm_g, mem_norm_g, w_in, w_mem_kv, v_norm_g, v_norm_b, w_spatial, b_spatial, attn_sinks, rel_bias, w_out, loss_target, m_pre_norm_g, m_post_norm_g, m_mem_norm_g, m_w_in, m_w_mem_kv, m_v_norm_g, m_v_norm_b, m_w_spatial, m_b_spatial, m_attn_sinks, m_rel_bias, m_w_out, v_pre_norm_g, v_post_norm_g, v_mem_norm_g, v_w_in, v_w_mem_kv, v_v_norm_g, v_v_norm_b, v_w_spatial, v_b_spatial, v_attn_sinks, v_rel_bias, v_w_out):
    sh_a = (w_in[0].T, m_w_in[0].T, v_w_in[0].T)
    sh_b = (w_out[0], m_w_out[0], v_w_out[0])
    sh_c = (w_mem_kv[0], m_w_mem_kv[0], v_w_mem_kv[0])
    nb, s, _ = x.shape
    t = nb * s
    x2 = x.reshape(t, D_MODEL)
    tgt2 = loss_target.reshape(t, D_MODEL)
    mem2 = mem.reshape(nb * MEM_LEN, D_MODEL)
    buckets = jnp.asarray(_t5_buckets())

    wa, wb, wc, bias, wt, wtt, bcol, mkv = _wgather(sh_a[0], sh_b[0], sh_c[0], rel_bias, w_spatial[0], b_spatial[0],
                                                    buckets, mem2, mem_norm_g)
    w_mkv = wc.reshape(D_MODEL, 2 * MEM_LEN)
    gx, dmkv, dwi, dwo, dg1, dg2, loss_p, dwsp, dbs, dvg, dvb, dsink, drel = _layer(
        x2, tgt2, mkv.reshape(nb, MEM_LEN, 2 * MEM_LEN), bias, attn_sinks.reshape(4), v_norm_g, v_norm_b, wt, wtt, bcol,
        pre_norm_g, post_norm_g, wa.reshape(IN_WIDTH, D_MODEL), wb.reshape(D_MODEL, D_MODEL), buckets,
        nb, s, min(256, s))
    gx = gx.reshape(nb, s, D_MODEL)
    small_grads = [dg1, dg2, dvg, dvb, dbs, dsink, drel, dwsp.reshape(A_GROUPS * CHUNK, CHUNK)]

    small_names = ["pre_norm_g", "post_norm_g", "mem_norm_g", "v_norm_g", "v_norm_b", "b_spatial", "attn_sinks",
                   "rel_bias", "w_spatial"]
    given = dict(pre_norm_g=(pre_norm_g, m_pre_norm_g, v_pre_norm_g), post_norm_g=(post_norm_g, m_post_norm_g, v_post_norm_g),
                 mem_norm_g=(mem_norm_g, m_mem_norm_g, v_mem_norm_g), v_norm_g=(v_norm_g, m_v_norm_g, v_v_norm_g),
                 v_norm_b=(v_norm_b, m_v_norm_b, v_v_norm_b), b_spatial=(b_spatial, m_b_spatial, v_b_spatial),
                 attn_sinks=(attn_sinks, m_attn_sinks, v_attn_sinks), rel_bias=(rel_bias, m_rel_bias, v_rel_bias),
                 w_spatial=(w_spatial, m_w_spatial, v_w_spatial))
    small_wmv = [[given[n][k].reshape(shp) for n, (shp, _) in zip(small_names, _S_LAYOUT)] for k in range(3)]

    ta, tb, tc, ts = _greduce(dwi.reshape(N_DEV, SHARD_IN, D_MODEL), dwo.reshape(N_DEV, SHARD_O, D_MODEL),
                              dmkv.reshape(nb * MEM_LEN, 2 * MEM_LEN), mem2, mem_norm_g, w_mkv, small_grads, loss_p)
    outs = _update(ta, tb, tc, ts, (*sh_a, *sh_b, *sh_c), small_wmv)
    ra, rb, rc = outs[0:4], outs[4:8], outs[8:12]
    loss = outs[12 + 4 * _N_SMALL].reshape(())

    res = {}
    for k, kind in enumerate(("grad", "delta", "new_m", "new_v")):
        res[kind, "w_in"] = ra[k].T[None]
        res[kind, "w_out"] = rb[k][None]
        res[kind, "w_mem_kv"] = rc[k][None]
        for i, n in enumerate(small_names):
            res[kind, n] = outs[12 + k * _N_SMALL + i].reshape(given[n][0].shape)
    order = ["pre_norm_g", "post_norm_g", "mem_norm_g", "w_in", "w_mem_kv", "v_norm_g", "v_norm_b", "w_spatial",
             "b_spatial", "attn_sinks", "rel_bias", "w_out"]
    flat = [res[kind, n] for kind in ("grad", "delta", "new_m", "new_v") for n in order]
    return (loss, gx, *flat)
```

```python
import numpy as np
import jax
import jax.numpy as jnp
from jax import lax
from jax.experimental import pallas as pl
from jax.experimental.pallas import tpu as pltpu

F32 = jnp.float32
BF16 = jnp.bfloat16
MM = jnp.bfloat16

D_MODEL = 1024
CHUNK = 128
A_GROUPS = 4
A_WIDTH = 512
UV_W = 1024
QKV_W = 768
Z_W = 1024
IN_WIDTH = UV_W + QKV_W + Z_W
MEM_LEN = 256
N_BUCKETS = 32
MAX_DISTANCE = 128
EPS = 1e-6
NEG = -1e30
SCALE = 0.125
N_DEV = 8
SHARD_IN = IN_WIDTH // N_DEV
SHARD_O = D_MODEL // N_DEV

SQ_COL, SK_COL, SV_COL, MQ_COL, Z_COL = UV_W, UV_W + 256, UV_W + 384, UV_W + 512, UV_W + QKV_W
YB_OFF, YC_OFF = 512, 768

ADAM_LR = 0.001
ADAM_B1 = 0.9
ADAM_B2 = 0.999
ADAM_EPS = 1e-08
ADAM_WD = 0.01
ADAM_STEP = 10

VMEM_LIMIT = 60 * 1024 * 1024

_GELU_C = 0.7978845608028654
_GELU_A = 0.044715

MESH = pl.DeviceIdType.MESH
_ROWS = 32


def _dot(a, b):
    return lax.dot_general(a, b, (((1,), (0,)), ((), ())), preferred_element_type=F32)


def _dot_nt(a, b):
    return lax.dot_general(a, b, (((1,), (1,)), ((), ())), preferred_element_type=F32)


def _dot_tn(a, b):
    return lax.dot_general(a, b, (((0,), (0,)), ((), ())), preferred_element_type=F32)


def _gelu_and_grad(x):
    x2 = x * x
    t = jnp.tanh(_GELU_C * (x + _GELU_A * x * x2))
    g = 0.5 * x * (1.0 + t)
    dg = 0.5 * (1.0 + t) + 0.5 * x * (1.0 - t * t) * (_GELU_C * (1.0 + 3.0 * _GELU_A * x2))
    return g, dg


def _t5_buckets():
    qi = np.arange(CHUNK)[:, None]
    kj = np.arange(2 * CHUNK)[None, :]
    n = np.maximum(qi + CHUNK - kj, 0)
    max_exact = N_BUCKETS // 2
    large = max_exact + (np.log(np.maximum(n, 1) / max_exact) / np.log(MAX_DISTANCE / max_exact)
                         * (N_BUCKETS - max_exact)).astype(np.int32)
    large = np.minimum(large, N_BUCKETS - 1)
    return np.where(n < max_exact, n, large).astype(np.int32)


def _params(**kw):
    return pltpu.CompilerParams(vmem_limit_bytes=VMEM_LIMIT, **kw)


def _full(shape, single=False):
    nd = len(shape)
    if single:
        return pl.BlockSpec(shape, lambda *_: (0,) * nd, pipeline_mode=pl.Buffered(1))
    return pl.BlockSpec(shape, lambda *_: (0,) * nd)


def _window_valid():
    qi = lax.broadcasted_iota(jnp.int32, (CHUNK, 2 * CHUNK), 0)
    kj = lax.broadcasted_iota(jnp.int32, (CHUNK, 2 * CHUNK), 1)
    dist = qi + CHUNK - kj
    return (dist >= 0) & (dist < CHUNK)


def _position():
    return lax.axis_index("x"), lax.axis_index("y"), lax.axis_index("c")


def _other_chips(x, y):
    return [(1 - x, y), (x, 1 - y), (1 - x, 1 - y)]


def _route(x, y, c):
    first = (x * c + (1 - x) * (1 - c), y * (1 - c) + (1 - y) * c)
    second = (x * (1 - c) + (1 - x) * c, y * c + (1 - y) * (1 - c))
    return first, second, (1 - x, 1 - y)


def _remote(src, dst, ssem, rsem, to):
    return pltpu.make_async_remote_copy(src_ref=src, dst_ref=dst, send_sem=ssem, recv_sem=rsem,
                                        device_id=to, device_id_type=MESH)


def _rows_loop(nrow, fn):
    def step(i, _):
        fn(pl.ds(pl.multiple_of(i * _ROWS, _ROWS), _ROWS))
        return 0

    lax.fori_loop(0, nrow // _ROWS, step, 0)


class _Gather:
    def __init__(self, pos, out, ssem, rsem):
        self.x, self.y, self.c = pos
        self.out, self.ssem, self.rsem = out, ssem, rsem
        self.me = 4 * self.x + 2 * self.y + self.c
        self.here = (self.x, self.y, self.c)
        self.sib = (self.x, self.y, 1 - self.c)
        self.first, self.second, self.far = _route(*pos)

    def _copy(self, k, blk, to):
        r = self.out.at[blk]
        return _remote(r, r, self.ssem.at[k], self.rsem.at[k], to)

    def _idx(self, chip, core):
        return 4 * chip[0] + 2 * chip[1] + core

    def _on(self, chip):
        return (chip[0], chip[1], self.c)

    def start(self):
        self._copy(0, self.me, self.sib).start()
        self._copy(1, self.me, self._on(self.first)).start()
        self._copy(2, self.me, self._on(self.second)).start()

    def forward(self):
        c = self.c
        self._copy(1, self._idx(self.first, c), self.here).wait_recv()
        self._copy(3, self._idx(self.first, c), self._on(self.second)).start()
        self._copy(4, self._idx(self.first, c), self.sib).start()
        self._copy(2, self._idx(self.second, c), self.here).wait_recv()
        self._copy(5, self._idx(self.second, c), self.sib).start()
        self._copy(3, self._idx(self.far, c), self.here).wait_recv()
        self._copy(6, self._idx(self.far, c), self.sib).start()

    def finish(self):
        c = self.c
        self._copy(0, self._idx((self.x, self.y), 1 - c), self.here).wait_recv()
        for k, chip in ((4, self.second), (5, self.first), (6, self.far)):
            self._copy(k, self._idx(chip, 1 - c), self.here).wait_recv()
        self._copy(0, self.me, self.sib).wait_send()
        self._copy(1, self.me, self._on(self.first)).wait_send()
        self._copy(2, self.me, self._on(self.second)).wait_send()
        self._copy(3, self._idx(self.first, c), self._on(self.second)).wait_send()
        for k, chip in ((4, self.first), (5, self.second), (6, self.far)):
            self._copy(k, self._idx(chip, c), self.sib).wait_send()


def _prep_tables(rb_ref, w_ref, b_ref, bk_ref, bias_ref, wt_ref, wtt_ref, bcol_ref):
    valid = _window_valid()
    bk = bk_ref[...]
    acc = [jnp.full((CHUNK, 2 * CHUNK), NEG, F32) for _ in range(4)]
    for b in range(N_BUCKETS):
        hit = (bk == b) & valid
        for h in range(4):
            acc[h] = jnp.where(hit, rb_ref[b, h], acc[h])
    for h in range(4):
        bias_ref[h] = acc[h]
    r = lax.broadcasted_iota(jnp.int32, (CHUNK, CHUNK), 0)
    c = lax.broadcasted_iota(jnp.int32, (CHUNK, CHUNK), 1)
    for g in range(A_GROUPS):
        w = jnp.where(r >= c, w_ref[g], 0.0)
        wt_ref[g] = w.astype(MM)
        wtt_ref[g] = w.T.astype(MM)
        bcol_ref[g] = jnp.broadcast_to(b_ref[g:g + 1, :], (CHUNK, CHUNK)).T


def _wgather(a, b, c, rel_bias, w_sp, b_sp, buckets, mem2, gm):
    tmem = mem2.shape[0]

    def body(a_ref, b_ref, c_ref, rb_ref, w_ref, bsp_ref, bk_ref, m_ref, gm_ref,
             oa, ob, oc, bias_ref, wt_ref, wtt_ref, bcol_ref, mkv_ref, ssem, rsem):
        pos = _position()
        me = 4 * pos[0] + 2 * pos[1] + pos[2]
        gathers = []
        for k, (src, out) in enumerate(((c_ref, oc), (b_ref, ob), (a_ref, oa))):
            out[me] = src[...].astype(BF16)
            g = _Gather(pos, out, ssem.at[k], rsem.at[k])
            g.start()
            gathers.append(g)
        _prep_tables(rb_ref, w_ref, bsp_ref, bk_ref, bias_ref, wt_ref, wtt_ref, bcol_ref)
        for g in gathers:
            g.forward()
        gathers[0].finish()
        xf = m_ref[...]
        hm = (xf * _rms(xf) * gm_ref[...]).astype(MM)
        acc = jnp.zeros((tmem, 2 * MEM_LEN), F32)
        for d in range(N_DEV):
            acc = acc + _dot(hm[:, d * SHARD_O:(d + 1) * SHARD_O], oc[d])
        mkv_ref[...] = acc.astype(MM)
        for g in gathers[1:]:
            g.finish()

    vm = pl.BlockSpec(memory_space=pltpu.VMEM)
    grp = (A_GROUPS, CHUNK, CHUNK)
    return pl.pallas_call(
        body, name="wgather",
        out_shape=(jax.ShapeDtypeStruct((N_DEV,) + a.shape, BF16),
                   jax.ShapeDtypeStruct((N_DEV,) + b.shape, BF16),
                   jax.ShapeDtypeStruct((N_DEV,) + c.shape, BF16),
                   jax.ShapeDtypeStruct((4, CHUNK, 2 * CHUNK), F32),
                   jax.ShapeDtypeStruct(grp, MM), jax.ShapeDtypeStruct(grp, MM), jax.ShapeDtypeStruct(grp, F32),
                   jax.ShapeDtypeStruct((tmem, 2 * MEM_LEN), MM)),
        in_specs=[vm, vm, vm, pl.BlockSpec(memory_space=pltpu.SMEM), vm, vm, vm, vm, vm],
        out_specs=tuple([vm] * 8),
        scratch_shapes=[pltpu.SemaphoreType.DMA((3, 7)), pltpu.SemaphoreType.DMA((3, 7))],
        compiler_params=_params(),
    )(a, b, c, rel_bias, w_sp, b_sp, buckets, mem2, gm)


def _half_masks(rows):
    lane = lax.broadcasted_iota(jnp.int32, (rows, CHUNK), 1)
    return lane < 64


def _dup_heads(band):
    b32 = band.astype(F32)
    rolled = pltpu.roll(b32, 64, 1)
    lo = _half_masks(band.shape[0])
    return (jnp.where(lo, b32, rolled).astype(MM), jnp.where(lo, rolled, b32).astype(MM))


def _swa_probs(qsel, kd, bias_h, sink_h, first_add):
    s = _dot_nt(qsel, kd) * SCALE + bias_h + first_add
    m = jnp.maximum(jnp.max(s, axis=-1, keepdims=True), sink_h)
    p = jnp.exp(s - m)
    es = jnp.exp(sink_h - m)
    inv = 1.0 / (jnp.sum(p, axis=-1, keepdims=True) + es)
    return p * inv, es * inv


def _softmax(s):
    m = jnp.max(s, axis=-1, keepdims=True)
    p = jnp.exp(s - m)
    return p * (1.0 / jnp.sum(p, axis=-1, keepdims=True))


def _first_block_mask(n):
    col = lax.broadcasted_iota(jnp.int32, (CHUNK, 2 * CHUNK), 1)
    return jnp.where((col < CHUNK) & (n == 0), NEG, 0.0)


def _rms(xf):
    return lax.rsqrt(jnp.mean(xf * xf, axis=-1, keepdims=True) + EPS)


def _layer(x2, tgt2, mkv3, bias, sinks, vg, vb, wt, wtt, bcol, g1, g2, w_in_t, w_o, buckets, nb, s, tm):
    nt = s // tm
    bpt = tm // CHUNK
    bps = s // CHUNK
    t = nb * s

    def body(x_ref, xp_ref, t_ref, mkv_ref, bias_ref, sink_ref, vg_ref, vb_ref, wt_ref, wtt_ref, bcol_ref,
             g1_ref, g2_ref, wi_ref, wo_ref, bk_ref,
             gx_ref, dmkv_ref, dwi_hbm, dwo_hbm, dg1_ref, dg2_ref, loss_ref, dwsp_ref, dbs_ref,
             dvg_ref, dvb_ref, dsink_ref, drel_ref,
             acc_i, acc_o, uv_s, z_s, q_s, kv_s, h_s, dp_s, dxo_s,
             ycat, dyc, u_s, gu_s, gv_s, xh_s, rs_s, sv_s, vc_s, pb_s, ps_s, pc_s, kd_s, vd_s,
             dkv_acc, dbias_acc, dsv_acc, dsink_acc, sems):
        b, j = pl.program_id(0), pl.program_id(1)
        jt = nt - 1 - j

        @pl.when((b == 0) & (j == 0))
        def _():
            for ref in (acc_i, acc_o, dg1_ref, dg2_ref, loss_ref, dwsp_ref, dvg_ref, dvb_ref,
                        dbias_acc, dsv_acc, dsink_acc):
                ref[...] = jnp.zeros_like(ref)

        @pl.when(j == 0)
        def _():
            dmkv_ref[...] = jnp.zeros_like(dmkv_ref)
            dkv_acc[...] = jnp.zeros_like(dkv_acc)

        carry = dkv_acc[0:CHUNK, :]
        dkv_acc[...] = jnp.zeros_like(dkv_acc)
        dkv_acc[tm:tm + CHUNK, :] = carry

        lo = _half_masks(CHUNK)
        lob = _half_masks(2 * CHUNK)
        lot = _half_masks(tm)
        g1v = g1_ref[...]

        xf = x_ref[...]
        h = (xf * _rms(xf) * g1v).astype(MM)
        h_s[...] = h
        uv_s[...] = _dot_nt(h, wi_ref[0:UV_W, :])
        qkv = _dot_nt(h, wi_ref[SQ_COL:Z_COL, :])
        q_s[:, 0:256] = qkv[:, 0:256].astype(MM)
        q_s[:, 256:512] = qkv[:, 512:768].astype(MM)
        kv_s[CHUNK:CHUNK + tm, :] = qkv[:, 256:512].astype(MM)
        z_s[...] = _dot_nt(h, wi_ref[Z_COL:IN_WIDTH, :])
        xp = xp_ref[...]
        hp = (xp * _rms(xp) * g1v).astype(MM)
        kv_s[0:CHUNK, :] = _dot_nt(hp, wi_ref[SK_COL:MQ_COL, :]).astype(MM)

        for blk in range(bpt):
            r0 = blk * CHUNK
            rows = slice(r0, r0 + CHUNK)
            n = jt * bpt + blk
            for g in range(A_GROUPS):
                cg = slice(g * CHUNK, (g + 1) * CHUNK)
                u, gu = _gelu_and_grad(uv_s[rows, cg])
                v, gv = _gelu_and_grad(uv_s[rows, A_WIDTH + g * CHUNK:A_WIDTH + (g + 1) * CHUNK])
                mu = jnp.mean(v, axis=-1, keepdims=True)
                xc = v - mu
                rstd = lax.rsqrt(jnp.mean(xc * xc, axis=-1, keepdims=True) + EPS)
                xhat = xc * rstd
                vc = (xhat * vg_ref[:, cg] + vb_ref[:, cg]).astype(MM)
                sv = _dot(wt_ref[g], vc) + bcol_ref[g]
                u_s[rows, cg] = u
                gu_s[rows, cg] = gu
                gv_s[rows, cg] = gv
                xh_s[rows, cg] = xhat
                rs_s[rows, cg] = jnp.broadcast_to(rstd, (CHUNK, CHUNK))
                sv_s[rows, cg] = sv
                vc_s[rows, cg] = vc
                ycat[rows, cg] = u * sv
            kd = _dup_heads(kv_s[r0:r0 + 2 * CHUNK, 0:CHUNK])
            vd = _dup_heads(kv_s[r0:r0 + 2 * CHUNK, CHUNK:2 * CHUNK])
            first_add = _first_block_mask(n)
            for kvh in range(2):
                kd_s[blk * 2 + kvh] = kd[kvh]
                vd_s[blk * 2 + kvh] = vd[kvh]
                q128 = q_s[rows, kvh * CHUNK:(kvh + 1) * CHUNK].astype(F32)
                outs = []
                for gi in range(2):
                    hd = 2 * kvh + gi
                    qsel = jnp.where(lo if gi == 0 else ~lo, q128, 0.0).astype(MM)
                    probs, ps = _swa_probs(qsel, kd[kvh], bias_ref[hd], sink_ref[hd], first_add)
                    pb_s[blk * 4 + hd] = probs
                    ps_s[blk * 4 + hd] = jnp.broadcast_to(ps, (CHUNK, CHUNK))
                    outs.append(_dot(probs.astype(MM), vd[kvh]))
                ycat[rows, YB_OFF + kvh * CHUNK:YB_OFF + (kvh + 1) * CHUNK] = jnp.where(lo, outs[0], outs[1])
        for g in range(2):
            q128 = q_s[:, 256 + g * CHUNK:256 + (g + 1) * CHUNK].astype(F32)
            k128 = mkv_ref[:, g * CHUNK:(g + 1) * CHUNK]
            v128 = mkv_ref[:, MEM_LEN + g * CHUNK:MEM_LEN + (g + 1) * CHUNK]
            outs = []
            for hh in range(2):
                qsel = jnp.where(lot if hh == 0 else ~lot, q128, 0.0).astype(MM)
                probs = _softmax(_dot_nt(qsel, k128) * SCALE)
                pc_s[2 * g + hh] = probs
                outs.append(_dot(probs.astype(MM), v128))
            ycat[:, YC_OFF + g * CHUNK:YC_OFF + (g + 1) * CHUNK] = jnp.where(lot, outs[0], outs[1])

        zt = z_s[...]
        sig = 1.0 / (1.0 + jnp.exp(-zt))
        silu = zt * sig
        yc = ycat[...]
        yb = (yc * silu).astype(MM)
        o = _dot(yb, wo_ref[...])
        r2 = _rms(o)
        nrm = o * r2
        g2v = g2_ref[...]
        e = x_ref[...] + nrm * g2v - t_ref[...]
        l1 = jnp.sum(e * e, axis=-1, keepdims=True)
        loss_ref[...] += jnp.broadcast_to(jnp.sum(l1, axis=0, keepdims=True) * (0.5 / D_MODEL), loss_ref.shape)
        dxo = e * (1.0 / D_MODEL)
        dxo_s[...] = dxo
        dg2_ref[...] += jnp.sum(dxo * nrm, axis=0, keepdims=True)
        dn = dxo * g2v
        do = r2 * (dn - nrm * jnp.mean(dn * nrm, axis=-1, keepdims=True))
        dob = do.astype(MM)
        dy = _dot_nt(dob, wo_ref[...])
        dp_s[:, Z_COL:IN_WIDTH] = (dy * yc * (sig * (1.0 + zt * (1.0 - sig)))).astype(MM)
        dyc[...] = dy * silu
        acc_o[...] += _dot_tn(yb, dob)

        for blk in range(bpt):
            r0 = blk * CHUNK
            rows = slice(r0, r0 + CHUNK)
            for g in range(A_GROUPS):
                cg = slice(g * CHUNK, (g + 1) * CHUNK)
                cv = slice(A_WIDTH + g * CHUNK, A_WIDTH + (g + 1) * CHUNK)
                dya = dyc[rows, cg]
                dp_s[rows, cg] = (dya * sv_s[rows, cg] * gu_s[rows, cg]).astype(MM)
                dsv = dya * u_s[rows, cg]
                dsvb = dsv.astype(MM)
                dsv_acc[g] += dsv
                dwsp_ref[g] += _dot_nt(dsvb, vc_s[rows, cg])
                dvc = _dot(wtt_ref[g], dsvb)
                xhat = xh_s[rows, cg]
                dvg_ref[:, cg] += jnp.sum(dvc * xhat, axis=0, keepdims=True)
                dvb_ref[:, cg] += jnp.sum(dvc, axis=0, keepdims=True)
                dxh = dvc * vg_ref[:, cg]
                dv = rs_s[rows, cg] * (dxh - jnp.mean(dxh, axis=-1, keepdims=True)
                                       - xhat * jnp.mean(dxh * xhat, axis=-1, keepdims=True))
                dp_s[rows, cv] = (dv * gv_s[rows, cg]).astype(MM)
            dk_f, dv_f = [], []
            for kvh in range(2):
                kd = kd_s[blk * 2 + kvh]
                vd = vd_s[blk * 2 + kvh]
                q128 = q_s[rows, kvh * CHUNK:(kvh + 1) * CHUNK].astype(F32)
                do128 = dyc[rows, YB_OFF + kvh * CHUNK:YB_OFF + (kvh + 1) * CHUNK]
                dq128 = jnp.zeros((CHUNK, CHUNK), F32)
                dkd = jnp.zeros((2 * CHUNK, CHUNK), F32)
                dvd = jnp.zeros((2 * CHUNK, CHUNK), F32)
                for gi in range(2):
                    hd = 2 * kvh + gi
                    half = lo if gi == 0 else ~lo
                    qsel = jnp.where(half, q128, 0.0).astype(MM)
                    dosel = jnp.where(half, do128, 0.0).astype(MM)
                    probs = pb_s[blk * 4 + hd]
                    ps = ps_s[blk * 4 + hd][:, 0:1]
                    dp = _dot_nt(dosel, vd)
                    delta = jnp.sum(probs * dp, axis=-1, keepdims=True)
                    ds = probs * (dp - delta)
                    dbias_acc[hd] += ds
                    dsink_acc[hd:hd + 1, :] += jnp.broadcast_to(-jnp.sum(ps * delta, axis=0, keepdims=True), (1, CHUNK))
                    dss = (ds * SCALE).astype(MM)
                    dq128 = dq128 + jnp.where(half, _dot(dss, kd), 0.0)
                    dkd = dkd + _dot_tn(dss, qsel)
                    dvd = dvd + _dot_tn(probs.astype(MM), dosel)
                dp_s[rows, SQ_COL + kvh * CHUNK:SQ_COL + (kvh + 1) * CHUNK] = dq128.astype(MM)
                dk_f.append(dkd + pltpu.roll(dkd, 64, 1))
                dv_f.append(dvd + pltpu.roll(dvd, 64, 1))
            dkv_acc[r0:r0 + 2 * CHUNK, 0:CHUNK] += jnp.where(lob, dk_f[0], dk_f[1])
            dkv_acc[r0:r0 + 2 * CHUNK, CHUNK:2 * CHUNK] += jnp.where(lob, dv_f[0], dv_f[1])
        dp_s[:, SK_COL:MQ_COL] = dkv_acc[CHUNK:CHUNK + tm, :].astype(MM)
        for g in range(2):
            q128 = q_s[:, 256 + g * CHUNK:256 + (g + 1) * CHUNK].astype(F32)
            k128 = mkv_ref[:, g * CHUNK:(g + 1) * CHUNK]
            v128 = mkv_ref[:, MEM_LEN + g * CHUNK:MEM_LEN + (g + 1) * CHUNK]
            do128 = dyc[:, YC_OFF + g * CHUNK:YC_OFF + (g + 1) * CHUNK]
            dq128 = jnp.zeros((tm, CHUNK), F32)
            dk128 = jnp.zeros((MEM_LEN, CHUNK), F32)
            dv128 = jnp.zeros((MEM_LEN, CHUNK), F32)
            for hh in range(2):
                half = lot if hh == 0 else ~lot
                qsel = jnp.where(half, q128, 0.0).astype(MM)
                dosel = jnp.where(half, do128, 0.0).astype(MM)
                probs = pc_s[2 * g + hh]
                dp = _dot_nt(dosel, v128)
                ds = probs * (dp - jnp.sum(probs * dp, axis=-1, keepdims=True))
                dss = (ds * SCALE).astype(MM)
                dq128 = dq128 + jnp.where(half, _dot(dss, k128), 0.0)
                dk128 = dk128 + _dot_tn(dss, qsel)
                dv128 = dv128 + _dot_tn(probs.astype(MM), dosel)
            dp_s[:, MQ_COL + g * CHUNK:MQ_COL + (g + 1) * CHUNK] = dq128.astype(MM)
            dmkv_ref[:, g * CHUNK:(g + 1) * CHUNK] += dk128
            dmkv_ref[:, MEM_LEN + g * CHUNK:MEM_LEN + (g + 1) * CHUNK] += dv128

        hv = h_s[...]
        dh = jnp.zeros((tm, D_MODEL), F32)
        for c0, c1 in ((0, UV_W), (SQ_COL, Z_COL), (Z_COL, IN_WIDTH)):
            dpt = dp_s[:, c0:c1]
            acc_i[c0:c1, :] += _dot_tn(dpt, hv)
            dh = dh + _dot(dpt, wi_ref[c0:c1, :])
        xf = x_ref[...]
        r = _rms(xf)
        nx = xf * r
        dg1_ref[...] += jnp.sum(dh * nx, axis=0, keepdims=True)
        dnx = dh * g1v
        gx_ref[...] = dxo_s[...] + r * (dnx - nx * jnp.mean(dnx * nx, axis=-1, keepdims=True))

        @pl.when((b == nb - 1) & (j == nt - 1))
        def _():
            out_i = pltpu.make_async_copy(acc_i, dwi_hbm, sems.at[0])
            out_o = pltpu.make_async_copy(acc_o, dwo_hbm, sems.at[1])
            out_i.start()
            out_o.start()
            r_ = lax.broadcasted_iota(jnp.int32, (CHUNK, CHUNK), 0)
            c_ = lax.broadcasted_iota(jnp.int32, (CHUNK, CHUNK), 1)
            for g in range(A_GROUPS):
                dwsp_ref[g] = jnp.where(r_ >= c_, dwsp_ref[g], 0.0)
                dbs_ref[g:g + 1, :] = jnp.sum(dsv_acc[g].T, axis=0, keepdims=True)
            rows8 = lax.broadcasted_iota(jnp.int32, (8, CHUNK), 0)
            cols8 = lax.broadcasted_iota(jnp.int32, (8, CHUNK), 1)
            sk = jnp.zeros((8, CHUNK), F32)
            for hd in range(4):
                sk = sk + jnp.where((rows8 == 0) & (cols8 == hd),
                                    jnp.broadcast_to(dsink_acc[hd:hd + 1, :], (8, CHUNK)), 0.0)
            dsink_ref[...] = sk
            bk = bk_ref[...]
            valid = _window_valid()
            rrow = lax.broadcasted_iota(jnp.int32, (N_BUCKETS, CHUNK), 0)
            rcol = lax.broadcasted_iota(jnp.int32, (N_BUCKETS, CHUNK), 1)
            acc = jnp.zeros((N_BUCKETS, CHUNK), F32)
            for bb in range(N_BUCKETS):
                hit = (bk == bb) & valid
                for hd in range(4):
                    part = jnp.sum(jnp.where(hit, dbias_acc[hd], 0.0), axis=-1, keepdims=True)
                    tot = jnp.sum(part, axis=0, keepdims=True)
                    acc = acc + jnp.where((rrow == bb) & (rcol == hd), jnp.broadcast_to(tot, (N_BUCKETS, CHUNK)), 0.0)
            drel_ref[...] = acc
            out_i.wait()
            out_o.wait()

    tile = lambda w: pl.BlockSpec((tm, w), lambda b, j: (b * nt + nt - 1 - j, 0))
    prev_block = pl.BlockSpec((CHUNK, D_MODEL), lambda b, j: (b * bps + jnp.maximum((nt - 1 - j) * bpt - 1, 0), 0))
    per_batch = lambda r, w: pl.BlockSpec((None, r, w), lambda b, j: (b, 0, 0))
    anyspec = pl.BlockSpec(memory_space=pl.ANY)
    grp = (A_GROUPS, CHUNK, CHUNK)
    return pl.pallas_call(
        body, name="layer", grid=(nb, nt),
        out_shape=(jax.ShapeDtypeStruct((t, D_MODEL), F32),
                   jax.ShapeDtypeStruct((nb, MEM_LEN, 2 * MEM_LEN), F32),
                   jax.ShapeDtypeStruct((IN_WIDTH, D_MODEL), F32),
                   jax.ShapeDtypeStruct((D_MODEL, D_MODEL), F32),
                   jax.ShapeDtypeStruct((1, D_MODEL), F32),
                   jax.ShapeDtypeStruct((1, D_MODEL), F32),
                   jax.ShapeDtypeStruct((8, CHUNK), F32),
                   jax.ShapeDtypeStruct(grp, F32),
                   jax.ShapeDtypeStruct((A_GROUPS, CHUNK), F32),
                   jax.ShapeDtypeStruct((1, A_WIDTH), F32),
                   jax.ShapeDtypeStruct((1, A_WIDTH), F32),
                   jax.ShapeDtypeStruct((8, CHUNK), F32),
                   jax.ShapeDtypeStruct((N_BUCKETS, CHUNK), F32)),
        in_specs=[tile(D_MODEL), prev_block, tile(D_MODEL), per_batch(MEM_LEN, 2 * MEM_LEN),
                  _full((4, CHUNK, 2 * CHUNK)),
                  pl.BlockSpec(memory_space=pltpu.SMEM),
                  _full((1, A_WIDTH)), _full((1, A_WIDTH)),
                  _full(grp), _full(grp), _full(grp),
                  _full((1, D_MODEL)), _full((1, D_MODEL)),
                  _full((IN_WIDTH, D_MODEL), single=True), _full((D_MODEL, D_MODEL), single=True),
                  _full((CHUNK, 2 * CHUNK))],
        out_specs=(tile(D_MODEL), per_batch(MEM_LEN, 2 * MEM_LEN), anyspec, anyspec,
                   _full((1, D_MODEL)), _full((1, D_MODEL)), _full((8, CHUNK)),
                   _full(grp), _full((A_GROUPS, CHUNK)), _full((1, A_WIDTH)), _full((1, A_WIDTH)),
                   _full((8, CHUNK)), _full((N_BUCKETS, CHUNK))),
        scratch_shapes=[pltpu.VMEM((IN_WIDTH, D_MODEL), F32), pltpu.VMEM((D_MODEL, D_MODEL), F32),
                        pltpu.VMEM((tm, UV_W), F32), pltpu.VMEM((tm, Z_W), F32),
                        pltpu.VMEM((tm, 512), MM), pltpu.VMEM((tm + CHUNK, 2 * CHUNK), MM),
                        pltpu.VMEM((tm, D_MODEL), MM), pltpu.VMEM((tm, IN_WIDTH), MM),
                        pltpu.VMEM((tm, D_MODEL), F32),
                        pltpu.VMEM((tm, D_MODEL), F32), pltpu.VMEM((tm, D_MODEL), F32)]
                       + [pltpu.VMEM((tm, A_WIDTH), F32) for _ in range(6)]
                       + [pltpu.VMEM((tm, A_WIDTH), MM),
                          pltpu.VMEM((bpt * 4, CHUNK, 2 * CHUNK), F32),
                          pltpu.VMEM((bpt * 4, CHUNK, CHUNK), F32),
                          pltpu.VMEM((4, tm, MEM_LEN), F32),
                          pltpu.VMEM((bpt * 2, 2 * CHUNK, CHUNK), MM),
                          pltpu.VMEM((bpt * 2, 2 * CHUNK, CHUNK), MM),
                          pltpu.VMEM((tm + CHUNK, 2 * CHUNK), F32),
                          pltpu.VMEM((4, CHUNK, 2 * CHUNK), F32),
                          pltpu.VMEM(grp, F32),
                          pltpu.VMEM((8, CHUNK), F32),
                          pltpu.SemaphoreType.DMA((2,))],
        compiler_params=_params(dimension_semantics=("arbitrary", "arbitrary")),
    )(x2, x2, tgt2, mkv3, bias, sinks, vg, vb, wt, wtt, bcol, g1, g2, w_in_t, w_o, buckets)


class _ShardReduce:
    def __init__(self, pos, g, bufs, sems):
        self.x, self.y, self.c = pos
        self.g = g
        self.own, self.rcv, self.sbuf, self.rbuf, self.cbuf = bufs
        self.ld, self.sa, self.ra, self.sb, self.rb = sems
        self.nrow = g.shape[1]
        self.here = (self.x, self.y, self.c)
        self.sib = (self.x, self.y, 1 - self.c)
        self.first, self.second, self.far = _route(*pos)

    def _load(self, q):
        return pltpu.make_async_copy(self.g.at[2 * q + self.c], self.own.at[q], self.ld.at[q])

    def _to_sib(self, q, to):
        return _remote(self.g.at[2 * q + 1 - self.c], self.rcv.at[q], self.sa.at[q], self.ra.at[q], to)

    def _send(self, k, to):
        dst = self.cbuf.at[0] if k == 1 else self.rbuf.at[0 if k == 0 else 1]
        return _remote(self.sbuf.at[k], dst, self.sb.at[k], self.rb.at[k], to)

    def _stage(self, k, which, extra=None):
        def cast(r):
            v = self.rcv[which, r, :]
            if extra is not None:
                v = v + extra[0, r, :].astype(F32)
            self.sbuf[k, r, :] = v.astype(BF16)

        _rows_loop(self.nrow, cast)

    @staticmethod
    def _q(chip):
        return 2 * chip[0] + chip[1]

    def start(self):
        for q in range(4):
            self._load(q).start()
            self._to_sib(q, self.sib).start()

    def mid(self):
        for q in range(4):
            self._load(q).wait()
            self._to_sib(q, self.here).wait_recv()

        def add(r):
            for q in range(4):
                self.rcv[q, r, :] = self.rcv[q, r, :] + self.own[q, r, :]

        _rows_loop(self.nrow, add)
        to_first = (self.first[0], self.first[1], self.c)
        self._stage(0, self._q(self.first))
        self._send(0, to_first).start()
        self._stage(1, self._q(self.far))
        self._send(1, to_first).start()

    def pass_on(self):
        self._send(1, self.here).wait_recv()
        self._stage(2, self._q(self.second), extra=self.cbuf)
        self._send(2, (self.second[0], self.second[1], self.c)).start()

    def finish(self, out):
        self._send(0, self.here).wait_recv()
        self._send(2, self.here).wait_recv()
        which = 2 * self.x + self.y

        def tot(r):
            out[r, :] = (self.rcv[which, r, :] + self.rbuf[0, r, :].astype(F32)) + self.rbuf[1, r, :].astype(F32)

        _rows_loop(self.nrow, tot)
        for q in range(4):
            self._to_sib(q, self.sib).wait_send()
        to_first = (self.first[0], self.first[1], self.c)
        self._send(0, to_first).wait_send()
        self._send(1, to_first).wait_send()
        self._send(2, (self.second[0], self.second[1], self.c)).wait_send()


def _reduce_scratch(shape):
    return [pltpu.VMEM((4,) + shape, F32), pltpu.VMEM((4,) + shape, F32),
            pltpu.VMEM((3,) + shape, BF16), pltpu.VMEM((2,) + shape, BF16), pltpu.VMEM((1,) + shape, BF16),
            pltpu.SemaphoreType.DMA((4,)), pltpu.SemaphoreType.DMA((4,)), pltpu.SemaphoreType.DMA((4,)),
            pltpu.SemaphoreType.DMA((3,)), pltpu.SemaphoreType.DMA((3,))]


_N_RED = 10

_S_LAYOUT = (((1, D_MODEL), 0), ((1, D_MODEL), 8), ((1, D_MODEL), 16),
             ((1, A_WIDTH), 24), ((1, A_WIDTH), 28), ((A_GROUPS, CHUNK), 32),
             ((1, 4), 36), ((N_BUCKETS, 4), 40),
             ((A_GROUPS * CHUNK, CHUNK), 72))
_LOSS_ROW = 37
_S_ROWS = 72 + A_GROUPS * CHUNK
_N_SMALL = len(_S_LAYOUT)


def _pack_rows(dst, refs):
    for (shp, r0), ref in zip(_S_LAYOUT, refs):
        if shp[0] == 1 and shp[1] >= CHUNK:
            for i in range(shp[1] // CHUNK):
                dst[r0 + i:r0 + i + 1, :] = ref[:, i * CHUNK:(i + 1) * CHUNK]
        elif ref.shape[-1] == CHUNK:
            dst[r0:r0 + shp[0], :] = ref[0:shp[0], :]
        else:
            dst[r0:r0 + shp[0], 0:shp[1]] = ref[...]


def _unpack_rows(src, refs):
    for (shp, r0), ref in zip(_S_LAYOUT, refs):
        if shp[0] == 1 and shp[1] >= CHUNK:
            for i in range(shp[1] // CHUNK):
                ref[:, i * CHUNK:(i + 1) * CHUNK] = src[r0 + i:r0 + i + 1, :]
        elif shp[1] == CHUNK:
            ref[...] = src[r0:r0 + shp[0], :]
        else:
            if tuple(ref.shape) == (shp[1], shp[0]):
                ref[...] = src[r0:r0 + CHUNK, :].T[0:shp[1], 0:shp[0]]
            else:
                ref[...] = src[r0:r0 + shp[0], 0:shp[1]]


_MEM_G = 2


def _greduce(ga, gb, dmkv, mem2, gm, w_mkv, small_g, loss_p):
    shp_c = (SHARD_O, 2 * MEM_LEN)
    shapes = (shp_c, gb.shape[1:], ga.shape[1:])
    rs = _S_ROWS

    def body(*refs):
        it = iter(refs)
        take = lambda n: [next(it) for _ in range(n)]
        gb_ref, ga_ref, d_ref, m_ref, gm_ref, wm_ref = take(6)
        sg_refs = take(_N_SMALL - 1)
        loss_ref, = take(1)
        oc, ob, oa, ogs = take(4)
        red = take(3 * _N_RED)
        gs_ref, rs_a, rs_b, gc_ref, dgm_ref = take(5)
        ssem_a, rsem_a, ssem_b, rsem_b = take(4)

        pos = _position()
        x, y, cc = pos
        myq = 2 * x + y
        here, sib = (x, y, cc), (x, y, 1 - cc)
        chips = _other_chips(x, y)
        reducers = [_ShardReduce(pos, g, red[k * _N_RED:k * _N_RED + 5], red[k * _N_RED + 5:(k + 1) * _N_RED])
                    for k, g in enumerate((gc_ref, gb_ref, ga_ref))]
        for rd in reducers[1:]:
            rd.start()

        xf = m_ref[...]
        nm = xf * _rms(xf)
        hm = (nm * gm_ref[...]).astype(MM)
        d = d_ref[...].astype(MM)
        for o in range(N_DEV):
            gc_ref[o] = _dot_tn(hm[:, o * SHARD_O:(o + 1) * SHARD_O], d)
        dgm_ref[...] = jnp.sum(_dot_nt(d, wm_ref[...]) * nm, axis=0, keepdims=True)
        reducers[0].start()

        gs_ref[...] = jnp.zeros_like(gs_ref)
        _pack_rows(gs_ref, sg_refs[:_MEM_G] + [dgm_ref] + sg_refs[_MEM_G:])
        gs_ref[_LOSS_ROW:_LOSS_ROW + 1, :] = loss_ref[0:1, :]
        small_a = _remote(gs_ref, rs_a, ssem_a, rsem_a, sib)
        small_a.start()

        _remote(gs_ref, rs_a, ssem_a, rsem_a, here).wait_recv()
        rs_b[myq] = gs_ref[...] + rs_a[...]
        small_b = [_remote(rs_b.at[myq], rs_b.at[myq], ssem_b.at[j], rsem_b.at[j], (chip[0], chip[1], cc))
                   for j, chip in enumerate(chips)]
        for cp in small_b:
            cp.start()
        late_last = reducers[1:] + reducers[:1]
        for rd in late_last:
            rd.mid()
        for rd in late_last:
            rd.pass_on()

        for j in range(3):
            _remote(rs_b.at[myq], rs_b.at[myq], ssem_b.at[j], rsem_b.at[j], here).wait_recv()

        def tot_s(i, _):
            r = pl.ds(pl.multiple_of(i * 8, 8), 8)
            ogs[r, :] = ((rs_b[0, r, :] + rs_b[1, r, :]) + rs_b[2, r, :]) + rs_b[3, r, :]
            return 0

        lax.fori_loop(0, rs // 8, tot_s, 0)
        for rd, out in zip(late_last, (ob, oa, oc)):
            rd.finish(out)
        small_a.wait_send()
        for cp in small_b:
            cp.wait_send()

    vm = pl.BlockSpec(memory_space=pltpu.VMEM)
    anyspec = pl.BlockSpec(memory_space=pl.ANY)
    scratch = []
    for shp in shapes:
        scratch += _reduce_scratch(shp)
    scratch += [pltpu.VMEM((rs, CHUNK), F32), pltpu.VMEM((rs, CHUNK), F32), pltpu.VMEM((4, rs, CHUNK), F32),
                pltpu.VMEM((N_DEV,) + shp_c, F32), pltpu.VMEM((1, D_MODEL), F32),
                pltpu.SemaphoreType.DMA, pltpu.SemaphoreType.DMA,
                pltpu.SemaphoreType.DMA((3,)), pltpu.SemaphoreType.DMA((3,))]
    tc, tb, ta, ts = pl.pallas_call(
        body, name="greduce",
        out_shape=tuple([jax.ShapeDtypeStruct(shp, F32) for shp in shapes] + [jax.ShapeDtypeStruct((rs, CHUNK), F32)]),
        in_specs=[anyspec] * 2 + [vm] * (4 + _N_SMALL),
        out_specs=(vm, vm, vm, vm),
        scratch_shapes=scratch,
        compiler_params=_params(),
    )(gb, ga, dmkv, mem2, gm, w_mkv, *small_g, loss_p)
    return ta, tb, tc, ts


def _adamw(w, g, m, v):
    m = ADAM_B1 * m + (1.0 - ADAM_B1) * g
    v = ADAM_B2 * v + (1.0 - ADAM_B2) * (g * g)
    m_hat = m / (1.0 - ADAM_B1 ** ADAM_STEP)
    v_hat = v / (1.0 - ADAM_B2 ** ADAM_STEP)
    delta = -ADAM_LR * (m_hat / (jnp.sqrt(v_hat) + ADAM_EPS) + ADAM_WD * w)
    return delta, m, v


def _update(ta, tb, tc, ts, big_wmv, small_wmv):
    shapes = (ta.shape, tb.shape, tc.shape)
    rs = _S_ROWS
    small_shapes = [tuple(a.shape) for a in small_wmv[0]]

    def body(*refs):
        it = iter(refs)
        take = lambda n: [next(it) for _ in range(n)]
        ga_ref, gb_ref, gc_ref, gs_ref = take(4)
        wa, ma, va, wb, mb, vb_, wc, mc, vc = take(9)
        sw_refs, sm_refs, sv_refs = take(_N_SMALL), take(_N_SMALL), take(_N_SMALL)
        oga, oda, oma, ova, ogb, odb, omb, ovb, ogc, odc, omc, ovc = take(12)
        so_refs = [take(_N_SMALL) for _ in range(4)]
        loss_out, = take(1)
        ws, ms, vs, ods, oms, ovs = take(6)

        for buf in (ws, ms, vs):
            buf[...] = jnp.zeros_like(buf)
        _pack_rows(ws, sw_refs)
        _pack_rows(ms, sm_refs)
        _pack_rows(vs, sv_refs)

        big = ((ga_ref, wa, ma, va, oga, oda, oma, ova), (gb_ref, wb, mb, vb_, ogb, odb, omb, ovb),
               (gc_ref, wc, mc, vc, ogc, odc, omc, ovc))
        for arr in range(3):
            g_r, w_r, m_r, v_r, og, od, om, ov = big[arr]

            def upd(r, g_r=g_r, w_r=w_r, m_r=m_r, v_r=v_r, og=og, od=od, om=om, ov=ov):
                g = g_r[r, :]
                d, m, v = _adamw(w_r[r, :], g, m_r[r, :], v_r[r, :])
                og[r, :] = g
                od[r, :] = d
                om[r, :] = m
                ov[r, :] = v

            _rows_loop(shapes[arr][0], upd)

        def upd_s(i, _):
            r = pl.ds(pl.multiple_of(i * 8, 8), 8)
            d, m, v = _adamw(ws[r, :], gs_ref[r, :], ms[r, :], vs[r, :])
            ods[r, :] = d
            oms[r, :] = m
            ovs[r, :] = v
            return 0

        lax.fori_loop(0, rs // 8, upd_s, 0)
        for k, buf in enumerate((gs_ref, ods, oms, ovs)):
            _unpack_rows(buf, so_refs[k])
        loss_out[...] = gs_ref[_LOSS_ROW:_LOSS_ROW + 1, 0:1]

    vm = pl.BlockSpec(memory_space=pltpu.VMEM)
    big_out = []
    for shp in shapes:
        big_out += [jax.ShapeDtypeStruct(shp, F32)] * 4
    small_out = [jax.ShapeDtypeStruct(shp[::-1] if shp == (N_BUCKETS, 4) else shp, F32) for shp in small_shapes] * 4
    out_shape = tuple(big_out + small_out + [jax.ShapeDtypeStruct((1, 1), F32)])
    n_in = 4 + 9 + 3 * _N_SMALL
    return pl.pallas_call(
        body, name="update",
        out_shape=out_shape,
        in_specs=[vm] * n_in,
        out_specs=tuple([vm] * len(out_shape)),
        scratch_shapes=[pltpu.VMEM((rs, CHUNK), F32) for _ in range(6)],
        compiler_params=_params(),
    )(ta, tb, tc, ts, *big_wmv, *small_wmv[0], *small_wmv[1], *small_wmv[2])


def kernel(x, mem, pre_norm_g, post_norm_g, mem_norm_g, w_in, w_mem_kv, v_norm_g, v_norm_b, w_spatial, b_spatial, attn_sinks, rel_bias, w_out, loss_target, m_pre_norm_g, m_post_norm_g, m_mem_norm_g, m_w_in, m_w_mem_kv, m_v_norm_g, m_v_norm_b, m_w_spatial, m_b_spatial, m_attn_sinks, m_rel_bias, m_w_out, v_pre_norm_g, v_post_norm_g, v_mem_norm_g, v_w_in, v_w_mem_kv, v_v_norm_g, v_v_norm_b, v_w_spatial, v_b_spatial, v_attn_sinks, v_rel_bias, v_w_out):
    sh_a = (w_in[0].T, m_w_in[0].T, v_w_in[0].T)
    sh_b = (w_out[0], m_w_out[0], v_w_out[0])
    sh_c = (w_mem_kv[0], m_w_mem_kv[0], v_w_mem_kv[0])
    nb, s, _ = x.shape
    t = nb * s
    x2 = x.reshape(t, D_MODEL)
    tgt2 = loss_target.reshape(t, D_MODEL)
    mem2 = mem.reshape(nb * MEM_LEN, D_MODEL)
    buckets = jnp.asarray(_t5_buckets())

    wa, wb, wc, bias, wt, wtt, bcol, mkv = _wgather(sh_a[0], sh_b[0], sh_c[0], rel_bias, w_spatial[0], b_spatial[0],
                                                    buckets, mem2, mem_norm_g)
    w_mkv = wc.reshape(D_MODEL, 2 * MEM_LEN)
    gx, dmkv, dwi, dwo, dg1, dg2, loss_p, dwsp, dbs, dvg, dvb, dsink, drel = _layer(
        x2, tgt2, mkv.reshape(nb, MEM_LEN, 2 * MEM_LEN), bias, attn_sinks.reshape(4), v_norm_g, v_norm_b, wt, wtt, bcol,
        pre_norm_g, post_norm_g, wa.reshape(IN_WIDTH, D_MODEL), wb.reshape(D_MODEL, D_MODEL), buckets,
        nb, s, min(256, s))
    gx = gx.reshape(nb, s, D_MODEL)
    small_grads = [dg1, dg2, dvg, dvb, dbs, dsink, drel, dwsp.reshape(A_GROUPS * CHUNK, CHUNK)]

    small_names = ["pre_norm_g", "post_norm_g", "mem_norm_g", "v_norm_g", "v_norm_b", "b_spatial", "attn_sinks",
                   "rel_bias", "w_spatial"]
    given = dict(pre_norm_g=(pre_norm_g, m_pre_norm_g, v_pre_norm_g), post_norm_g=(post_norm_g, m_post_norm_g, v_post_norm_g),
                 mem_norm_g=(mem_norm_g, m_mem_norm_g, v_mem_norm_g), v_norm_g=(v_norm_g, m_v_norm_g, v_v_norm_g),
                 v_norm_b=(v_norm_b, m_v_norm_b, v_v_norm_b), b_spatial=(b_spatial, m_b_spatial, v_b_spatial),
                 attn_sinks=(attn_sinks, m_attn_sinks, v_attn_sinks), rel_bias=(rel_bias, m_rel_bias, v_rel_bias),
                 w_spatial=(w_spatial, m_w_spatial, v_w_spatial))
    small_wmv = [[given[n][k].reshape(shp) for n, (shp, _) in zip(small_names, _S_LAYOUT)] for k in range(3)]

    ta, tb, tc, ts = _greduce(dwi.reshape(N_DEV, SHARD_IN, D_MODEL), dwo.reshape(N_DEV, SHARD_O, D_MODEL),
                              dmkv.reshape(nb * MEM_LEN, 2 * MEM_LEN), mem2, mem_norm_g, w_mkv, small_grads, loss_p)
    outs = _update(ta, tb, tc, ts, (*sh_a, *sh_b, *sh_c), small_wmv)
    ra, rb, rc = outs[0:4], outs[4:8], outs[8:12]
    loss = outs[12 + 4 * _N_SMALL].reshape(())

    res = {}
    for k, kind in enumerate(("grad", "delta", "new_m", "new_v")):
        res[kind, "w_in"] = ra[k].T[None]
        res[kind, "w_out"] = rb[k][None]
        res[kind, "w_mem_kv"] = rc[k][None]
        for i, n in enumerate(small_names):
            o = outs[12 + k * _N_SMALL + i]
            res[kind, n] = o.T if n == "rel_bias" else o.reshape(given[n][0].shape)
    order = ["pre_norm_g", "post_norm_g", "mem_norm_g", "w_in", "w_mem_kv", "v_norm_g", "v_norm_b", "w_spatial",
             "b_spatial", "attn_sinks", "rel_bias", "w_out"]
    flat = [res[kind, n] for kind in ("grad", "delta", "new_m", "new_v") for n in order]
    return (loss, gx, *flat)
```

```python
import numpy as np
import jax
import jax.numpy as jnp
from jax import lax
from jax.experimental import pallas as pl
from jax.experimental.pallas import tpu as pltpu

F32 = jnp.float32
BF16 = jnp.bfloat16
MM = jnp.bfloat16

D_MODEL = 1024
CHUNK = 128
A_GROUPS = 4
A_WIDTH = 512
UV_W = 1024
QKV_W = 768
Z_W = 1024
IN_WIDTH = UV_W + QKV_W + Z_W
MEM_LEN = 256
N_BUCKETS = 32
MAX_DISTANCE = 128
EPS = 1e-6
NEG = -1e30
SCALE = 0.125
N_DEV = 8
SHARD_IN = IN_WIDTH // N_DEV
SHARD_O = D_MODEL // N_DEV

SQ_COL, SK_COL, SV_COL, MQ_COL, Z_COL = UV_W, UV_W + 256, UV_W + 384, UV_W + 512, UV_W + QKV_W
YB_OFF, YC_OFF = 512, 768

ADAM_LR = 0.001
ADAM_B1 = 0.9
ADAM_B2 = 0.999
ADAM_EPS = 1e-08
ADAM_WD = 0.01
ADAM_STEP = 10

VMEM_LIMIT = 60 * 1024 * 1024

_GELU_C = 0.7978845608028654
_GELU_A = 0.044715

MESH = pl.DeviceIdType.MESH
_ROWS = 32


def _dot(a, b):
    return lax.dot_general(a, b, (((1,), (0,)), ((), ())), preferred_element_type=F32)


def _dot_nt(a, b):
    return lax.dot_general(a, b, (((1,), (1,)), ((), ())), preferred_element_type=F32)


def _dot_tn(a, b):
    return lax.dot_general(a, b, (((0,), (0,)), ((), ())), preferred_element_type=F32)


def _gelu_and_grad(x):
    x2 = x * x
    t = jnp.tanh(_GELU_C * (x + _GELU_A * x * x2))
    g = 0.5 * x * (1.0 + t)
    dg = 0.5 * (1.0 + t) + 0.5 * x * (1.0 - t * t) * (_GELU_C * (1.0 + 3.0 * _GELU_A * x2))
    return g, dg


def _t5_buckets():
    qi = np.arange(CHUNK)[:, None]
    kj = np.arange(2 * CHUNK)[None, :]
    n = np.maximum(qi + CHUNK - kj, 0)
    max_exact = N_BUCKETS // 2
    large = max_exact + (np.log(np.maximum(n, 1) / max_exact) / np.log(MAX_DISTANCE / max_exact)
                         * (N_BUCKETS - max_exact)).astype(np.int32)
    large = np.minimum(large, N_BUCKETS - 1)
    return np.where(n < max_exact, n, large).astype(np.int32)


def _params(**kw):
    return pltpu.CompilerParams(vmem_limit_bytes=VMEM_LIMIT, **kw)


def _full(shape, single=False):
    nd = len(shape)
    if single:
        return pl.BlockSpec(shape, lambda *_: (0,) * nd, pipeline_mode=pl.Buffered(1))
    return pl.BlockSpec(shape, lambda *_: (0,) * nd)


def _window_valid():
    qi = lax.broadcasted_iota(jnp.int32, (CHUNK, 2 * CHUNK), 0)
    kj = lax.broadcasted_iota(jnp.int32, (CHUNK, 2 * CHUNK), 1)
    dist = qi + CHUNK - kj
    return (dist >= 0) & (dist < CHUNK)


def _position():
    return lax.axis_index("x"), lax.axis_index("y"), lax.axis_index("c")


def _other_chips(x, y):
    return [(1 - x, y), (x, 1 - y), (1 - x, 1 - y)]


def _route(x, y, c):
    first = (x * c + (1 - x) * (1 - c), y * (1 - c) + (1 - y) * c)
    second = (x * (1 - c) + (1 - x) * c, y * c + (1 - y) * (1 - c))
    return first, second, (1 - x, 1 - y)


def _remote(src, dst, ssem, rsem, to):
    return pltpu.make_async_remote_copy(src_ref=src, dst_ref=dst, send_sem=ssem, recv_sem=rsem,
                                        device_id=to, device_id_type=MESH)


def _rows_loop(nrow, fn):
    def step(i, _):
        fn(pl.ds(pl.multiple_of(i * _ROWS, _ROWS), _ROWS))
        return 0

    lax.fori_loop(0, nrow // _ROWS, step, 0)


class _Gather:
    def __init__(self, pos, out, ssem, rsem):
        self.x, self.y, self.c = pos
        self.out, self.ssem, self.rsem = out, ssem, rsem
        self.me = 4 * self.x + 2 * self.y + self.c
        self.here = (self.x, self.y, self.c)
        self.sib = (self.x, self.y, 1 - self.c)
        self.first, self.second, self.far = _route(*pos)

    def _copy(self, k, blk, to):
        r = self.out.at[blk]
        return _remote(r, r, self.ssem.at[k], self.rsem.at[k], to)

    def _idx(self, chip, core):
        return 4 * chip[0] + 2 * chip[1] + core

    def _on(self, chip):
        return (chip[0], chip[1], self.c)

    def start(self):
        self._copy(0, self.me, self.sib).start()
        self._copy(1, self.me, self._on(self.first)).start()
        self._copy(2, self.me, self._on(self.second)).start()

    def forward(self):
        c = self.c
        self._copy(1, self._idx(self.first, c), self.here).wait_recv()
        self._copy(3, self._idx(self.first, c), self._on(self.second)).start()
        self._copy(4, self._idx(self.first, c), self.sib).start()
        self._copy(2, self._idx(self.second, c), self.here).wait_recv()
        self._copy(5, self._idx(self.second, c), self.sib).start()
        self._copy(3, self._idx(self.far, c), self.here).wait_recv()
        self._copy(6, self._idx(self.far, c), self.sib).start()

    def finish(self):
        c = self.c
        self._copy(0, self._idx((self.x, self.y), 1 - c), self.here).wait_recv()
        for k, chip in ((4, self.second), (5, self.first), (6, self.far)):
            self._copy(k, self._idx(chip, 1 - c), self.here).wait_recv()
        self._copy(0, self.me, self.sib).wait_send()
        self._copy(1, self.me, self._on(self.first)).wait_send()
        self._copy(2, self.me, self._on(self.second)).wait_send()
        self._copy(3, self._idx(self.first, c), self._on(self.second)).wait_send()
        for k, chip in ((4, self.first), (5, self.second), (6, self.far)):
            self._copy(k, self._idx(chip, c), self.sib).wait_send()


def _prep_tables(rb_ref, w_ref, b_ref, bk_ref, bias_ref, wt_ref, wtt_ref, bcol_ref):
    valid = _window_valid()
    bk = bk_ref[...]
    acc = [jnp.full((CHUNK, 2 * CHUNK), NEG, F32) for _ in range(4)]
    for b in range(N_BUCKETS):
        hit = (bk == b) & valid
        for h in range(4):
            acc[h] = jnp.where(hit, rb_ref[b, h], acc[h])
    for h in range(4):
        bias_ref[h] = acc[h]
    r = lax.broadcasted_iota(jnp.int32, (CHUNK, CHUNK), 0)
    c = lax.broadcasted_iota(jnp.int32, (CHUNK, CHUNK), 1)
    for g in range(A_GROUPS):
        w = jnp.where(r >= c, w_ref[g], 0.0)
        wt_ref[g] = w.astype(MM)
        wtt_ref[g] = w.T.astype(MM)
        bcol_ref[g] = jnp.broadcast_to(b_ref[g:g + 1, :], (CHUNK, CHUNK)).T


def _wgather(a, b, c, rel_bias, w_sp, b_sp, buckets, mem2, gm):
    tmem = mem2.shape[0]

    def body(a_ref, b_ref, c_ref, rb_ref, w_ref, bsp_ref, bk_ref, m_ref, gm_ref,
             oa, ob, oc, bias_ref, wt_ref, wtt_ref, bcol_ref, mkv_ref, ssem, rsem):
        pos = _position()
        me = 4 * pos[0] + 2 * pos[1] + pos[2]
        gathers = []
        for k, (src, out) in enumerate(((c_ref, oc), (b_ref, ob), (a_ref, oa))):
            out[me] = src[...].astype(BF16)
            g = _Gather(pos, out, ssem.at[k], rsem.at[k])
            g.start()
            gathers.append(g)
        _prep_tables(rb_ref, w_ref, bsp_ref, bk_ref, bias_ref, wt_ref, wtt_ref, bcol_ref)
        for g in gathers:
            g.forward()
        gathers[0].finish()
        xf = m_ref[...]
        hm = (xf * _rms(xf) * gm_ref[...]).astype(MM)
        acc = jnp.zeros((tmem, 2 * MEM_LEN), F32)
        for d in range(N_DEV):
            acc = acc + _dot(hm[:, d * SHARD_O:(d + 1) * SHARD_O], oc[d])
        mkv_ref[...] = acc.astype(MM)
        for g in gathers[1:]:
            g.finish()

    vm = pl.BlockSpec(memory_space=pltpu.VMEM)
    grp = (A_GROUPS, CHUNK, CHUNK)
    return pl.pallas_call(
        body, name="wgather",
        out_shape=(jax.ShapeDtypeStruct((N_DEV,) + a.shape, BF16),
                   jax.ShapeDtypeStruct((N_DEV,) + b.shape, BF16),
                   jax.ShapeDtypeStruct((N_DEV,) + c.shape, BF16),
                   jax.ShapeDtypeStruct((4, CHUNK, 2 * CHUNK), F32),
                   jax.ShapeDtypeStruct(grp, MM), jax.ShapeDtypeStruct(grp, MM), jax.ShapeDtypeStruct(grp, F32),
                   jax.ShapeDtypeStruct((tmem, 2 * MEM_LEN), MM)),
        in_specs=[vm, vm, vm, pl.BlockSpec(memory_space=pltpu.SMEM), vm, vm, vm, vm, vm],
        out_specs=tuple([vm] * 8),
        scratch_shapes=[pltpu.SemaphoreType.DMA((3, 7)), pltpu.SemaphoreType.DMA((3, 7))],
        compiler_params=_params(),
    )(a, b, c, rel_bias, w_sp, b_sp, buckets, mem2, gm)


def _half_masks(rows):
    lane = lax.broadcasted_iota(jnp.int32, (rows, CHUNK), 1)
    return lane < 64


def _dup_heads(band):
    b32 = band.astype(F32)
    rolled = pltpu.roll(b32, 64, 1)
    lo = _half_masks(band.shape[0])
    return (jnp.where(lo, b32, rolled).astype(MM), jnp.where(lo, rolled, b32).astype(MM))


def _swa_probs(qsel, kd, bias_h, sink_h, first_add):
    s = _dot_nt(qsel, kd) * SCALE + bias_h + first_add
    m = jnp.maximum(jnp.max(s, axis=-1, keepdims=True), sink_h)
    p = jnp.exp(s - m)
    es = jnp.exp(sink_h - m)
    inv = 1.0 / (jnp.sum(p, axis=-1, keepdims=True) + es)
    return p * inv, es * inv


def _softmax(s):
    m = jnp.max(s, axis=-1, keepdims=True)
    p = jnp.exp(s - m)
    return p * (1.0 / jnp.sum(p, axis=-1, keepdims=True))


def _first_block_mask(n):
    col = lax.broadcasted_iota(jnp.int32, (CHUNK, 2 * CHUNK), 1)
    return jnp.where((col < CHUNK) & (n == 0), NEG, 0.0)


def _rms(xf):
    return lax.rsqrt(jnp.mean(xf * xf, axis=-1, keepdims=True) + EPS)


def _layer(x2, tgt2, mkv3, bias, sinks, vg, vb, wt, wtt, bcol, g1, g2, w_in_t, w_o, buckets, nb, s, tm):
    nt = s // tm
    bpt = tm // CHUNK
    bps = s // CHUNK
    t = nb * s

    def body(x_ref, xp_ref, t_ref, mkv_ref, bias_ref, sink_ref, vg_ref, vb_ref, wt_ref, wtt_ref, bcol_ref,
             g1_ref, g2_ref, wi_ref, wo_ref, bk_ref,
             gx_ref, dmkv_ref, dwi_hbm, dwo_hbm, dg1_ref, dg2_ref, loss_ref, dwsp_ref, dbs_ref,
             dvg_ref, dvb_ref, dsink_ref, drel_ref,
             acc_i, acc_o, uv_s, z_s, q_s, kv_s, h_s, dp_s, dxo_s,
             ycat, dyc, u_s, gu_s, gv_s, xh_s, rs_s, sv_s, vc_s, pb_s, ps_s, pc_s, kd_s, vd_s,
             dkv_acc, dbias_acc, dsv_acc, dsink_acc, sems):
        b, j = pl.program_id(0), pl.program_id(1)
        jt = nt - 1 - j

        @pl.when((b == 0) & (j == 0))
        def _():
            for ref in (acc_i, acc_o, dg1_ref, dg2_ref, loss_ref, dwsp_ref, dvg_ref, dvb_ref,
                        dbias_acc, dsv_acc, dsink_acc):
                ref[...] = jnp.zeros_like(ref)

        @pl.when(j == 0)
        def _():
            dmkv_ref[...] = jnp.zeros_like(dmkv_ref)
            dkv_acc[...] = jnp.zeros_like(dkv_acc)

        carry = dkv_acc[0:CHUNK, :]
        dkv_acc[...] = jnp.zeros_like(dkv_acc)
        dkv_acc[tm:tm + CHUNK, :] = carry

        lo = _half_masks(CHUNK)
        lob = _half_masks(2 * CHUNK)
        lot = _half_masks(tm)
        g1v = g1_ref[...]

        xf = x_ref[...]
        h = (xf * _rms(xf) * g1v).astype(MM)
        h_s[...] = h
        uv_s[...] = _dot_nt(h, wi_ref[0:UV_W, :])
        qkv = _dot_nt(h, wi_ref[SQ_COL:Z_COL, :])
        q_s[:, 0:256] = qkv[:, 0:256].astype(MM)
        q_s[:, 256:512] = qkv[:, 512:768].astype(MM)
        kv_s[CHUNK:CHUNK + tm, :] = qkv[:, 256:512].astype(MM)
        z_s[...] = _dot_nt(h, wi_ref[Z_COL:IN_WIDTH, :])
        xp = xp_ref[...]
        hp = (xp * _rms(xp) * g1v).astype(MM)
        kv_s[0:CHUNK, :] = _dot_nt(hp, wi_ref[SK_COL:MQ_COL, :]).astype(MM)

        for blk in range(bpt):
            r0 = blk * CHUNK
            rows = slice(r0, r0 + CHUNK)
            n = jt * bpt + blk
            for g in range(A_GROUPS):
                cg = slice(g * CHUNK, (g + 1) * CHUNK)
                u, gu = _gelu_and_grad(uv_s[rows, cg])
                v, gv = _gelu_and_grad(uv_s[rows, A_WIDTH + g * CHUNK:A_WIDTH + (g + 1) * CHUNK])
                mu = jnp.mean(v, axis=-1, keepdims=True)
                xc = v - mu
                rstd = lax.rsqrt(jnp.mean(xc * xc, axis=-1, keepdims=True) + EPS)
                xhat = xc * rstd
                vc = (xhat * vg_ref[:, cg] + vb_ref[:, cg]).astype(MM)
                sv = _dot(wt_ref[g], vc) + bcol_ref[g]
                u_s[rows, cg] = u
                gu_s[rows, cg] = gu
                gv_s[rows, cg] = gv
                xh_s[rows, cg] = xhat
                rs_s[rows, cg] = jnp.broadcast_to(rstd, (CHUNK, CHUNK))
                sv_s[rows, cg] = sv
                vc_s[rows, cg] = vc
                ycat[rows, cg] = u * sv
            kd = _dup_heads(kv_s[r0:r0 + 2 * CHUNK, 0:CHUNK])
            vd = _dup_heads(kv_s[r0:r0 + 2 * CHUNK, CHUNK:2 * CHUNK])
            first_add = _first_block_mask(n)
            for kvh in range(2):
                kd_s[blk * 2 + kvh] = kd[kvh]
                vd_s[blk * 2 + kvh] = vd[kvh]
                q128 = q_s[rows, kvh * CHUNK:(kvh + 1) * CHUNK].astype(F32)
                outs = []
                for gi in range(2):
                    hd = 2 * kvh + gi
                    qsel = jnp.where(lo if gi == 0 else ~lo, q128, 0.0).astype(MM)
                    probs, ps = _swa_probs(qsel, kd[kvh], bias_ref[hd], sink_ref[hd], first_add)
                    pb_s[blk * 4 + hd] = probs
                    ps_s[blk * 4 + hd] = jnp.broadcast_to(ps, (CHUNK, CHUNK))
                    outs.append(_dot(probs.astype(MM), vd[kvh]))
                ycat[rows, YB_OFF + kvh * CHUNK:YB_OFF + (kvh + 1) * CHUNK] = jnp.where(lo, outs[0], outs[1])
        for g in range(2):
            q128 = q_s[:, 256 + g * CHUNK:256 + (g + 1) * CHUNK].astype(F32)
            k128 = mkv_ref[:, g * CHUNK:(g + 1) * CHUNK]
            v128 = mkv_ref[:, MEM_LEN + g * CHUNK:MEM_LEN + (g + 1) * CHUNK]
            outs = []
            for hh in range(2):
                qsel = jnp.where(lot if hh == 0 else ~lot, q128, 0.0).astype(MM)
                probs = _softmax(_dot_nt(qsel, k128) * SCALE)
                pc_s[2 * g + hh] = probs
                outs.append(_dot(probs.astype(MM), v128))
            ycat[:, YC_OFF + g * CHUNK:YC_OFF + (g + 1) * CHUNK] = jnp.where(lot, outs[0], outs[1])

        zt = z_s[...]
        sig = 1.0 / (1.0 + jnp.exp(-zt))
        silu = zt * sig
        yc = ycat[...]
        yb = (yc * silu).astype(MM)
        o = _dot(yb, wo_ref[...])
        r2 = _rms(o)
        nrm = o * r2
        g2v = g2_ref[...]
        e = x_ref[...] + nrm * g2v - t_ref[...]
        l1 = jnp.sum(e * e, axis=-1, keepdims=True)
        loss_ref[...] += jnp.broadcast_to(jnp.sum(l1, axis=0, keepdims=True) * (0.5 / D_MODEL), loss_ref.shape)
        dxo = e * (1.0 / D_MODEL)
        dxo_s[...] = dxo
        dg2_ref[...] += jnp.sum(dxo * nrm, axis=0, keepdims=True)
        dn = dxo * g2v
        do = r2 * (dn - nrm * jnp.mean(dn * nrm, axis=-1, keepdims=True))
        dob = do.astype(MM)
        dy = _dot_nt(dob, wo_ref[...])
        dp_s[:, Z_COL:IN_WIDTH] = (dy * yc * (sig * (1.0 + zt * (1.0 - sig)))).astype(MM)
        dyc[...] = dy * silu
        acc_o[...] += _dot_tn(yb, dob)

        for blk in range(bpt):
            r0 = blk * CHUNK
            rows = slice(r0, r0 + CHUNK)
            for g in range(A_GROUPS):
                cg = slice(g * CHUNK, (g + 1) * CHUNK)
                cv = slice(A_WIDTH + g * CHUNK, A_WIDTH + (g + 1) * CHUNK)
                dya = dyc[rows, cg]
                dp_s[rows, cg] = (dya * sv_s[rows, cg] * gu_s[rows, cg]).astype(MM)
                dsv = dya * u_s[rows, cg]
                dsvb = dsv.astype(MM)
                dsv_acc[g] += dsv
                dwsp_ref[g] += _dot_nt(dsvb, vc_s[rows, cg])
                dvc = _dot(wtt_ref[g], dsvb)
                xhat = xh_s[rows, cg]
                dvg_ref[:, cg] += jnp.sum(dvc * xhat, axis=0, keepdims=True)
                dvb_ref[:, cg] += jnp.sum(dvc, axis=0, keepdims=True)
                dxh = dvc * vg_ref[:, cg]
                dv = rs_s[rows, cg] * (dxh - jnp.mean(dxh, axis=-1, keepdims=True)
                                       - xhat * jnp.mean(dxh * xhat, axis=-1, keepdims=True))
                dp_s[rows, cv] = (dv * gv_s[rows, cg]).astype(MM)
            dk_f, dv_f = [], []
            for kvh in range(2):
                kd = kd_s[blk * 2 + kvh]
                vd = vd_s[blk * 2 + kvh]
                q128 = q_s[rows, kvh * CHUNK:(kvh + 1) * CHUNK].astype(F32)
                do128 = dyc[rows, YB_OFF + kvh * CHUNK:YB_OFF + (kvh + 1) * CHUNK]
                dq128 = jnp.zeros((CHUNK, CHUNK), F32)
                dkd = jnp.zeros((2 * CHUNK, CHUNK), F32)
                dvd = jnp.zeros((2 * CHUNK, CHUNK), F32)
                for gi in range(2):
                    hd = 2 * kvh + gi
                    half = lo if gi == 0 else ~lo
                    qsel = jnp.where(half, q128, 0.0).astype(MM)
                    dosel = jnp.where(half, do128, 0.0).astype(MM)
                    probs = pb_s[blk * 4 + hd]
                    ps = ps_s[blk * 4 + hd][:, 0:1]
                    dp = _dot_nt(dosel, vd)
                    delta = jnp.sum(probs * dp, axis=-1, keepdims=True)
                    ds = probs * (dp - delta)
                    dbias_acc[hd] += ds
                    dsink_acc[hd:hd + 1, :] += jnp.broadcast_to(-jnp.sum(ps * delta, axis=0, keepdims=True), (1, CHUNK))
                    dss = (ds * SCALE).astype(MM)
                    dq128 = dq128 + jnp.where(half, _dot(dss, kd), 0.0)
                    dkd = dkd + _dot_tn(dss, qsel)
                    dvd = dvd + _dot_tn(probs.astype(MM), dosel)
                dp_s[rows, SQ_COL + kvh * CHUNK:SQ_COL + (kvh + 1) * CHUNK] = dq128.astype(MM)
                dk_f.append(dkd + pltpu.roll(dkd, 64, 1))
                dv_f.append(dvd + pltpu.roll(dvd, 64, 1))
            dkv_acc[r0:r0 + 2 * CHUNK, 0:CHUNK] += jnp.where(lob, dk_f[0], dk_f[1])
            dkv_acc[r0:r0 + 2 * CHUNK, CHUNK:2 * CHUNK] += jnp.where(lob, dv_f[0], dv_f[1])
        dp_s[:, SK_COL:MQ_COL] = dkv_acc[CHUNK:CHUNK + tm, :].astype(MM)
        for g in range(2):
            q128 = q_s[:, 256 + g * CHUNK:256 + (g + 1) * CHUNK].astype(F32)
            k128 = mkv_ref[:, g * CHUNK:(g + 1) * CHUNK]
            v128 = mkv_ref[:, MEM_LEN + g * CHUNK:MEM_LEN + (g + 1) * CHUNK]
            do128 = dyc[:, YC_OFF + g * CHUNK:YC_OFF + (g + 1) * CHUNK]
            dq128 = jnp.zeros((tm, CHUNK), F32)
            dk128 = jnp.zeros((MEM_LEN, CHUNK), F32)
            dv128 = jnp.zeros((MEM_LEN, CHUNK), F32)
            for hh in range(2):
                half = lot if hh == 0 else ~lot
                qsel = jnp.where(half, q128, 0.0).astype(MM)
                dosel = jnp.where(half, do128, 0.0).astype(MM)
                probs = pc_s[2 * g + hh]
                dp = _dot_nt(dosel, v128)
                ds = probs * (dp - jnp.sum(probs * dp, axis=-1, keepdims=True))
                dss = (ds * SCALE).astype(MM)
                dq128 = dq128 + jnp.where(half, _dot(dss, k128), 0.0)
                dk128 = dk128 + _dot_tn(dss, qsel)
                dv128 = dv128 + _dot_tn(probs.astype(MM), dosel)
            dp_s[:, MQ_COL + g * CHUNK:MQ_COL + (g + 1) * CHUNK] = dq128.astype(MM)
            dmkv_ref[:, g * CHUNK:(g + 1) * CHUNK] += dk128
            dmkv_ref[:, MEM_LEN + g * CHUNK:MEM_LEN + (g + 1) * CHUNK] += dv128

        hv = h_s[...]
        dh = jnp.zeros((tm, D_MODEL), F32)
        for c0, c1 in ((0, UV_W), (SQ_COL, Z_COL), (Z_COL, IN_WIDTH)):
            dpt = dp_s[:, c0:c1]
            acc_i[c0:c1, :] += _dot_tn(dpt, hv)
            dh = dh + _dot(dpt, wi_ref[c0:c1, :])
        xf = x_ref[...]
        r = _rms(xf)
        nx = xf * r
        dg1_ref[...] += jnp.sum(dh * nx, axis=0, keepdims=True)
        dnx = dh * g1v
        gx_ref[...] = dxo_s[...] + r * (dnx - nx * jnp.mean(dnx * nx, axis=-1, keepdims=True))

        @pl.when((b == nb - 1) & (j == nt - 1))
        def _():
            out_i = pltpu.make_async_copy(acc_i, dwi_hbm, sems.at[0])
            out_o = pltpu.make_async_copy(acc_o, dwo_hbm, sems.at[1])
            out_i.start()
            out_o.start()
            r_ = lax.broadcasted_iota(jnp.int32, (CHUNK, CHUNK), 0)
            c_ = lax.broadcasted_iota(jnp.int32, (CHUNK, CHUNK), 1)
            for g in range(A_GROUPS):
                dwsp_ref[g] = jnp.where(r_ >= c_, dwsp_ref[g], 0.0)
                dbs_ref[g:g + 1, :] = jnp.sum(dsv_acc[g].T, axis=0, keepdims=True)
            rows8 = lax.broadcasted_iota(jnp.int32, (8, CHUNK), 0)
            cols8 = lax.broadcasted_iota(jnp.int32, (8, CHUNK), 1)
            sk = jnp.zeros((8, CHUNK), F32)
            for hd in range(4):
                sk = sk + jnp.where((rows8 == 0) & (cols8 == hd),
                                    jnp.broadcast_to(dsink_acc[hd:hd + 1, :], (8, CHUNK)), 0.0)
            dsink_ref[...] = sk
            bk = bk_ref[...]
            valid = _window_valid()
            rrow = lax.broadcasted_iota(jnp.int32, (N_BUCKETS, CHUNK), 0)
            rcol = lax.broadcasted_iota(jnp.int32, (N_BUCKETS, CHUNK), 1)
            acc = jnp.zeros((N_BUCKETS, CHUNK), F32)
            for bb in range(N_BUCKETS):
                hit = (bk == bb) & valid
                for hd in range(4):
                    part = jnp.sum(jnp.where(hit, dbias_acc[hd], 0.0), axis=-1, keepdims=True)
                    tot = jnp.sum(part, axis=0, keepdims=True)
                    acc = acc + jnp.where((rrow == bb) & (rcol == hd), jnp.broadcast_to(tot, (N_BUCKETS, CHUNK)), 0.0)
            drel_ref[...] = acc
            out_i.wait()
            out_o.wait()

    tile = lambda w: pl.BlockSpec((tm, w), lambda b, j: (b * nt + nt - 1 - j, 0))
    prev_block = pl.BlockSpec((CHUNK, D_MODEL), lambda b, j: (b * bps + jnp.maximum((nt - 1 - j) * bpt - 1, 0), 0))
    per_batch = lambda r, w: pl.BlockSpec((None, r, w), lambda b, j: (b, 0, 0))
    anyspec = pl.BlockSpec(memory_space=pl.ANY)
    grp = (A_GROUPS, CHUNK, CHUNK)
    return pl.pallas_call(
        body, name="layer", grid=(nb, nt),
        out_shape=(jax.ShapeDtypeStruct((t, D_MODEL), F32),
                   jax.ShapeDtypeStruct((nb, MEM_LEN, 2 * MEM_LEN), F32),
                   jax.ShapeDtypeStruct((IN_WIDTH, D_MODEL), F32),
                   jax.ShapeDtypeStruct((D_MODEL, D_MODEL), F32),
                   jax.ShapeDtypeStruct((1, D_MODEL), F32),
                   jax.ShapeDtypeStruct((1, D_MODEL), F32),
                   jax.ShapeDtypeStruct((8, CHUNK), F32),
                   jax.ShapeDtypeStruct(grp, F32),
                   jax.ShapeDtypeStruct((A_GROUPS, CHUNK), F32),
                   jax.ShapeDtypeStruct((1, A_WIDTH), F32),
                   jax.ShapeDtypeStruct((1, A_WIDTH), F32),
                   jax.ShapeDtypeStruct((8, CHUNK), F32),
                   jax.ShapeDtypeStruct((N_BUCKETS, CHUNK), F32)),
        in_specs=[tile(D_MODEL), prev_block, tile(D_MODEL), per_batch(MEM_LEN, 2 * MEM_LEN),
                  _full((4, CHUNK, 2 * CHUNK)),
                  pl.BlockSpec(memory_space=pltpu.SMEM),
                  _full((1, A_WIDTH)), _full((1, A_WIDTH)),
                  _full(grp), _full(grp), _full(grp),
                  _full((1, D_MODEL)), _full((1, D_MODEL)),
                  _full((IN_WIDTH, D_MODEL), single=True), _full((D_MODEL, D_MODEL), single=True),
                  _full((CHUNK, 2 * CHUNK))],
        out_specs=(tile(D_MODEL), per_batch(MEM_LEN, 2 * MEM_LEN), anyspec, anyspec,
                   _full((1, D_MODEL)), _full((1, D_MODEL)), _full((8, CHUNK)),
                   _full(grp), _full((A_GROUPS, CHUNK)), _full((1, A_WIDTH)), _full((1, A_WIDTH)),
                   _full((8, CHUNK)), _full((N_BUCKETS, CHUNK))),
        scratch_shapes=[pltpu.VMEM((IN_WIDTH, D_MODEL), F32), pltpu.VMEM((D_MODEL, D_MODEL), F32),
                        pltpu.VMEM((tm, UV_W), F32), pltpu.VMEM((tm, Z_W), F32),
                        pltpu.VMEM((tm, 512), MM), pltpu.VMEM((tm + CHUNK, 2 * CHUNK), MM),
                        pltpu.VMEM((tm, D_MODEL), MM), pltpu.VMEM((tm, IN_WIDTH), MM),
                        pltpu.VMEM((tm, D_MODEL), F32),
                        pltpu.VMEM((tm, D_MODEL), F32), pltpu.VMEM((tm, D_MODEL), F32)]
                       + [pltpu.VMEM((tm, A_WIDTH), F32) for _ in range(6)]
                       + [pltpu.VMEM((tm, A_WIDTH), MM),
                          pltpu.VMEM((bpt * 4, CHUNK, 2 * CHUNK), F32),
                          pltpu.VMEM((bpt * 4, CHUNK, CHUNK), F32),
                          pltpu.VMEM((4, tm, MEM_LEN), F32),
                          pltpu.VMEM((bpt * 2, 2 * CHUNK, CHUNK), MM),
                          pltpu.VMEM((bpt * 2, 2 * CHUNK, CHUNK), MM),
                          pltpu.VMEM((tm + CHUNK, 2 * CHUNK), F32),
                          pltpu.VMEM((4, CHUNK, 2 * CHUNK), F32),
                          pltpu.VMEM(grp, F32),
                          pltpu.VMEM((8, CHUNK), F32),
                          pltpu.SemaphoreType.DMA((2,))],
        compiler_params=_params(dimension_semantics=("arbitrary", "arbitrary")),
    )(x2, x2, tgt2, mkv3, bias, sinks, vg, vb, wt, wtt, bcol, g1, g2, w_in_t, w_o, buckets)


class _ShardReduce:
    def __init__(self, pos, g, bufs, sems):
        self.x, self.y, self.c = pos
        self.g = g
        self.own, self.rcv, self.sbuf, self.rbuf, self.cbuf = bufs
        self.ld, self.sa, self.ra, self.sb, self.rb = sems
        self.nrow = g.shape[1]
        self.here = (self.x, self.y, self.c)
        self.sib = (self.x, self.y, 1 - self.c)
        self.first, self.second, self.far = _route(*pos)

    def _load(self, q):
        return pltpu.make_async_copy(self.g.at[2 * q + self.c], self.own.at[q], self.ld.at[q])

    def _to_sib(self, q, to):
        return _remote(self.g.at[2 * q + 1 - self.c], self.rcv.at[q], self.sa.at[q], self.ra.at[q], to)

    def _send(self, k, to):
        dst = self.cbuf.at[0] if k == 1 else self.rbuf.at[0 if k == 0 else 1]
        return _remote(self.sbuf.at[k], dst, self.sb.at[k], self.rb.at[k], to)

    def _stage(self, k, which, extra=None):
        def cast(r):
            v = self.rcv[which, r, :]
            if extra is not None:
                v = v + extra[0, r, :].astype(F32)
            self.sbuf[k, r, :] = v.astype(BF16)

        _rows_loop(self.nrow, cast)

    @staticmethod
    def _q(chip):
        return 2 * chip[0] + chip[1]

    def start(self):
        for q in range(4):
            self._load(q).start()
            self._to_sib(q, self.sib).start()

    def mid(self):
        for q in range(4):
            self._load(q).wait()
            self._to_sib(q, self.here).wait_recv()

        def add(r):
            for q in range(4):
                self.rcv[q, r, :] = self.rcv[q, r, :] + self.own[q, r, :]

        _rows_loop(self.nrow, add)
        to_first = (self.first[0], self.first[1], self.c)
        self._stage(0, self._q(self.first))
        self._send(0, to_first).start()
        self._stage(1, self._q(self.far))
        self._send(1, to_first).start()

    def pass_on(self):
        self._send(1, self.here).wait_recv()
        self._stage(2, self._q(self.second), extra=self.cbuf)
        self._send(2, (self.second[0], self.second[1], self.c)).start()

    def finish(self, out):
        self._send(0, self.here).wait_recv()
        self._send(2, self.here).wait_recv()
        which = 2 * self.x + self.y

        def tot(r):
            out[r, :] = (self.rcv[which, r, :] + self.rbuf[0, r, :].astype(F32)) + self.rbuf[1, r, :].astype(F32)

        _rows_loop(self.nrow, tot)
        for q in range(4):
            self._to_sib(q, self.sib).wait_send()
        to_first = (self.first[0], self.first[1], self.c)
        self._send(0, to_first).wait_send()
        self._send(1, to_first).wait_send()
        self._send(2, (self.second[0], self.second[1], self.c)).wait_send()


def _reduce_scratch(shape):
    return [pltpu.VMEM((4,) + shape, F32), pltpu.VMEM((4,) + shape, F32),
            pltpu.VMEM((3,) + shape, BF16), pltpu.VMEM((2,) + shape, BF16), pltpu.VMEM((1,) + shape, BF16),
            pltpu.SemaphoreType.DMA((4,)), pltpu.SemaphoreType.DMA((4,)), pltpu.SemaphoreType.DMA((4,)),
            pltpu.SemaphoreType.DMA((3,)), pltpu.SemaphoreType.DMA((3,))]


_N_RED = 10

_S_LAYOUT = (((1, D_MODEL), 0), ((1, D_MODEL), 8), ((1, D_MODEL), 16),
             ((1, A_WIDTH), 24), ((1, A_WIDTH), 28), ((A_GROUPS, CHUNK), 32),
             ((1, 4), 36), ((N_BUCKETS, 4), 40),
             ((A_GROUPS * CHUNK, CHUNK), 72))
_LOSS_ROW = 37
_W_SP_ROW = _S_LAYOUT[-1][1]
_S_ROWS = _W_SP_ROW + A_GROUPS * CHUNK
_N_SMALL = len(_S_LAYOUT)


def _pack_rows(dst, refs):
    for (shp, r0), ref in zip(_S_LAYOUT, refs):
        if shp[0] == 1 and shp[1] >= CHUNK:
            for i in range(shp[1] // CHUNK):
                dst[r0 + i:r0 + i + 1, :] = ref[:, i * CHUNK:(i + 1) * CHUNK]
        elif ref.shape[-1] == CHUNK:
            dst[r0:r0 + shp[0], :] = ref[0:shp[0], :]
        else:
            dst[r0:r0 + shp[0], 0:shp[1]] = ref[...]


def _unpack_rows(src, refs):
    for (shp, r0), ref in zip(_S_LAYOUT, refs):
        if shp[0] == 1 and shp[1] >= CHUNK:
            for i in range(shp[1] // CHUNK):
                ref[:, i * CHUNK:(i + 1) * CHUNK] = src[r0 + i:r0 + i + 1, :]
        elif shp[1] == CHUNK:
            ref[...] = src[r0:r0 + shp[0], :]
        else:
            if tuple(ref.shape) == (shp[1], shp[0]):
                ref[...] = src[r0:r0 + CHUNK, :].T[0:shp[1], 0:shp[0]]
            else:
                ref[...] = src[r0:r0 + shp[0], 0:shp[1]]


_MEM_G = 2


def _greduce(ga, gb, dmkv, mem2, gm, w_mkv, small_g, loss_p):
    shp_c = (SHARD_O, 2 * MEM_LEN)
    shapes = (shp_c, gb.shape[1:], ga.shape[1:])
    rs = _S_ROWS

    def body(*refs):
        it = iter(refs)
        take = lambda n: [next(it) for _ in range(n)]
        gb_ref, ga_ref, d_ref, m_ref, gm_ref, wm_ref = take(6)
        sg_refs = take(_N_SMALL - 1)
        loss_ref, = take(1)
        oc, ob, oa, ogs = take(4)
        red = take(3 * _N_RED)
        gs_ref, rs_a, rs_b, rs_w, gc_ref, dgm_ref = take(6)
        ssem_a, rsem_a, ssem_b, rsem_b = take(4)

        pos = _position()
        x, y, cc = pos
        myq = 2 * x + y
        here, sib = (x, y, cc), (x, y, 1 - cc)
        chips = _other_chips(x, y)
        reducers = [_ShardReduce(pos, g, red[k * _N_RED:k * _N_RED + 5], red[k * _N_RED + 5:(k + 1) * _N_RED])
                    for k, g in enumerate((gc_ref, gb_ref, ga_ref))]
        for rd in reducers[1:]:
            rd.start()

        xf = m_ref[...]
        nm = xf * _rms(xf)
        hm = (nm * gm_ref[...]).astype(MM)
        d = d_ref[...].astype(MM)
        for o in range(N_DEV):
            gc_ref[o] = _dot_tn(hm[:, o * SHARD_O:(o + 1) * SHARD_O], d)
        dgm_ref[...] = jnp.sum(_dot_nt(d, wm_ref[...]) * nm, axis=0, keepdims=True)
        reducers[0].start()

        gs_ref[...] = jnp.zeros_like(gs_ref)
        _pack_rows(gs_ref, sg_refs[:_MEM_G] + [dgm_ref] + sg_refs[_MEM_G:])
        gs_ref[_LOSS_ROW:_LOSS_ROW + 1, :] = loss_ref[0:1, :]
        small_a = _remote(gs_ref, rs_a, ssem_a, rsem_a, sib)
        small_a.start()

        _remote(gs_ref, rs_a, ssem_a, rsem_a, here).wait_recv()
        rs_b[myq] = gs_ref[0:_W_SP_ROW, :] + rs_a[0:_W_SP_ROW, :]
        rs_w[myq] = (gs_ref[_W_SP_ROW:rs, :] + rs_a[_W_SP_ROW:rs, :]).astype(BF16)
        small_b = []
        for j, chip in enumerate(chips):
            to = (chip[0], chip[1], cc)
            small_b.append(_remote(rs_b.at[myq], rs_b.at[myq], ssem_b.at[0, j], rsem_b.at[0, j], to))
            small_b.append(_remote(rs_w.at[myq], rs_w.at[myq], ssem_b.at[1, j], rsem_b.at[1, j], to))
        for cp in small_b:
            cp.start()
        late_last = reducers[1:] + reducers[:1]
        for rd in late_last:
            rd.mid()
        for rd in late_last:
            rd.pass_on()

        for j in range(3):
            _remote(rs_b.at[myq], rs_b.at[myq], ssem_b.at[0, j], rsem_b.at[0, j], here).wait_recv()
            _remote(rs_w.at[myq], rs_w.at[myq], ssem_b.at[1, j], rsem_b.at[1, j], here).wait_recv()
        ogs[0:_W_SP_ROW, :] = ((rs_b[0] + rs_b[1]) + rs_b[2]) + rs_b[3]

        def tot_w(r):
            w = [rs_w[q, r, :].astype(F32) for q in range(4)]
            ogs[pl.ds(pl.multiple_of(_W_SP_ROW + r.start, 8), _ROWS), :] = ((w[0] + w[1]) + w[2]) + w[3]

        _rows_loop(rs - _W_SP_ROW, tot_w)
        for rd, out in zip(late_last, (ob, oa, oc)):
            rd.finish(out)
        small_a.wait_send()
        for cp in small_b:
            cp.wait_send()

    vm = pl.BlockSpec(memory_space=pltpu.VMEM)
    anyspec = pl.BlockSpec(memory_space=pl.ANY)
    scratch = []
    for shp in shapes:
        scratch += _reduce_scratch(shp)
    scratch += [pltpu.VMEM((rs, CHUNK), F32), pltpu.VMEM((rs, CHUNK), F32),
                pltpu.VMEM((4, _W_SP_ROW, CHUNK), F32), pltpu.VMEM((4, rs - _W_SP_ROW, CHUNK), BF16),
                pltpu.VMEM((N_DEV,) + shp_c, F32), pltpu.VMEM((1, D_MODEL), F32),
                pltpu.SemaphoreType.DMA, pltpu.SemaphoreType.DMA,
                pltpu.SemaphoreType.DMA((2, 3)), pltpu.SemaphoreType.DMA((2, 3))]
    tc, tb, ta, ts = pl.pallas_call(
        body, name="greduce",
        out_shape=tuple([jax.ShapeDtypeStruct(shp, F32) for shp in shapes] + [jax.ShapeDtypeStruct((rs, CHUNK), F32)]),
        in_specs=[anyspec] * 2 + [vm] * (4 + _N_SMALL),
        out_specs=(vm, vm, vm, vm),
        scratch_shapes=scratch,
        compiler_params=_params(),
    )(gb, ga, dmkv, mem2, gm, w_mkv, *small_g, loss_p)
    return ta, tb, tc, ts


def _adamw(w, g, m, v):
    m = ADAM_B1 * m + (1.0 - ADAM_B1) * g
    v = ADAM_B2 * v + (1.0 - ADAM_B2) * (g * g)
    m_hat = m / (1.0 - ADAM_B1 ** ADAM_STEP)
    v_hat = v / (1.0 - ADAM_B2 ** ADAM_STEP)
    delta = -ADAM_LR * (m_hat / (jnp.sqrt(v_hat) + ADAM_EPS) + ADAM_WD * w)
    return delta, m, v


def _update(ta, tb, tc, ts, big_wmv, small_wmv):
    shapes = (ta.shape, tb.shape, tc.shape)
    rs = _S_ROWS
    small_shapes = [tuple(a.shape) for a in small_wmv[0]]

    def body(*refs):
        it = iter(refs)
        take = lambda n: [next(it) for _ in range(n)]
        ga_ref, gb_ref, gc_ref, gs_ref = take(4)
        wa, ma, va, wb, mb, vb_, wc, mc, vc = take(9)
        sw_refs, sm_refs, sv_refs = take(_N_SMALL), take(_N_SMALL), take(_N_SMALL)
        oga, oda, oma, ova, ogb, odb, omb, ovb, ogc, odc, omc, ovc = take(12)
        so_refs = [take(_N_SMALL) for _ in range(4)]
        loss_out, = take(1)
        ws, ms, vs, ods, oms, ovs = take(6)

        for buf in (ws, ms, vs):
            buf[...] = jnp.zeros_like(buf)
        _pack_rows(ws, sw_refs)
        _pack_rows(ms, sm_refs)
        _pack_rows(vs, sv_refs)

        big = ((ga_ref, wa, ma, va, oga, oda, oma, ova), (gb_ref, wb, mb, vb_, ogb, odb, omb, ovb),
               (gc_ref, wc, mc, vc, ogc, odc, omc, ovc))
        for arr in range(3):
            g_r, w_r, m_r, v_r, og, od, om, ov = big[arr]

            def upd(r, g_r=g_r, w_r=w_r, m_r=m_r, v_r=v_r, og=og, od=od, om=om, ov=ov):
                g = g_r[r, :]
                d, m, v = _adamw(w_r[r, :], g, m_r[r, :], v_r[r, :])
                og[r, :] = g
                od[r, :] = d
                om[r, :] = m
                ov[r, :] = v

            _rows_loop(shapes[arr][0], upd)

        def upd_s(i, _):
            r = pl.ds(pl.multiple_of(i * 8, 8), 8)
            d, m, v = _adamw(ws[r, :], gs_ref[r, :], ms[r, :], vs[r, :])
            ods[r, :] = d
            oms[r, :] = m
            ovs[r, :] = v
            return 0

        lax.fori_loop(0, rs // 8, upd_s, 0)
        for k, buf in enumerate((gs_ref, ods, oms, ovs)):
            _unpack_rows(buf, so_refs[k])
        loss_out[...] = gs_ref[_LOSS_ROW:_LOSS_ROW + 1, 0:1]

    vm = pl.BlockSpec(memory_space=pltpu.VMEM)
    big_out = []
    for shp in shapes:
        big_out += [jax.ShapeDtypeStruct(shp, F32)] * 4
    small_out = [jax.ShapeDtypeStruct(shp[::-1] if shp == (N_BUCKETS, 4) else shp, F32) for shp in small_shapes] * 4
    out_shape = tuple(big_out + small_out + [jax.ShapeDtypeStruct((1, 1), F32)])
    n_in = 4 + 9 + 3 * _N_SMALL
    return pl.pallas_call(
        body, name="update",
        out_shape=out_shape,
        in_specs=[vm] * n_in,
        out_specs=tuple([vm] * len(out_shape)),
        scratch_shapes=[pltpu.VMEM((rs, CHUNK), F32) for _ in range(6)],
        compiler_params=_params(),
    )(ta, tb, tc, ts, *big_wmv, *small_wmv[0], *small_wmv[1], *small_wmv[2])


def kernel(x, mem, pre_norm_g, post_norm_g, mem_norm_g, w_in, w_mem_kv, v_norm_g, v_norm_b, w_spatial, b_spatial, attn_sinks, rel_bias, w_out, loss_target, m_pre_norm_g, m_post_norm_g, m_mem_norm_g, m_w_in, m_w_mem_kv, m_v_norm_g, m_v_norm_b, m_w_spatial, m_b_spatial, m_attn_sinks, m_rel_bias, m_w_out, v_pre_norm_g, v_post_norm_g, v_mem_norm_g, v_w_in, v_w_mem_kv, v_v_norm_g, v_v_norm_b, v_w_spatial, v_b_spatial, v_attn_sinks, v_rel_bias, v_w_out):
    sh_a = (w_in[0].T, m_w_in[0].T, v_w_in[0].T)
    sh_b = (w_out[0], m_w_out[0], v_w_out[0])
    sh_c = (w_mem_kv[0], m_w_mem_kv[0], v_w_mem_kv[0])
    nb, s, _ = x.shape
    t = nb * s
    x2 = x.reshape(t, D_MODEL)
    tgt2 = loss_target.reshape(t, D_MODEL)
    mem2 = mem.reshape(nb * MEM_LEN, D_MODEL)
    buckets = jnp.asarray(_t5_buckets())

    wa, wb, wc, bias, wt, wtt, bcol, mkv = _wgather(sh_a[0], sh_b[0], sh_c[0], rel_bias, w_spatial[0], b_spatial[0],
                                                    buckets, mem2, mem_norm_g)
    w_mkv = wc.reshape(D_MODEL, 2 * MEM_LEN)
    gx, dmkv, dwi, dwo, dg1, dg2, loss_p, dwsp, dbs, dvg, dvb, dsink, drel = _layer(
        x2, tgt2, mkv.reshape(nb, MEM_LEN, 2 * MEM_LEN), bias, attn_sinks.reshape(4), v_norm_g, v_norm_b, wt, wtt, bcol,
        pre_norm_g, post_norm_g, wa.reshape(IN_WIDTH, D_MODEL), wb.reshape(D_MODEL, D_MODEL), buckets,
        nb, s, min(256, s))
    gx = gx.reshape(nb, s, D_MODEL)
    small_grads = [dg1, dg2, dvg, dvb, dbs, dsink, drel, dwsp.reshape(A_GROUPS * CHUNK, CHUNK)]

    small_names = ["pre_norm_g", "post_norm_g", "mem_norm_g", "v_norm_g", "v_norm_b", "b_spatial", "attn_sinks",
                   "rel_bias", "w_spatial"]
    given = dict(pre_norm_g=(pre_norm_g, m_pre_norm_g, v_pre_norm_g), post_norm_g=(post_norm_g, m_post_norm_g, v_post_norm_g),
                 mem_norm_g=(mem_norm_g, m_mem_norm_g, v_mem_norm_g), v_norm_g=(v_norm_g, m_v_norm_g, v_v_norm_g),
                 v_norm_b=(v_norm_b, m_v_norm_b, v_v_norm_b), b_spatial=(b_spatial, m_b_spatial, v_b_spatial),
                 attn_sinks=(attn_sinks, m_attn_sinks, v_attn_sinks), rel_bias=(rel_bias, m_rel_bias, v_rel_bias),
                 w_spatial=(w_spatial, m_w_spatial, v_w_spatial))
    small_wmv = [[given[n][k].reshape(shp) for n, (shp, _) in zip(small_names, _S_LAYOUT)] for k in range(3)]

    ta, tb, tc, ts = _greduce(dwi.reshape(N_DEV, SHARD_IN, D_MODEL), dwo.reshape(N_DEV, SHARD_O, D_MODEL),
                              dmkv.reshape(nb * MEM_LEN, 2 * MEM_LEN), mem2, mem_norm_g, w_mkv, small_grads, loss_p)
    outs = _update(ta, tb, tc, ts, (*sh_a, *sh_b, *sh_c), small_wmv)
    ra, rb, rc = outs[0:4], outs[4:8], outs[8:12]
    loss = outs[12 + 4 * _N_SMALL].reshape(())

    res = {}
    for k, kind in enumerate(("grad", "delta", "new_m", "new_v")):
        res[kind, "w_in"] = ra[k].T[None]
        res[kind, "w_out"] = rb[k][None]
        res[kind, "w_mem_kv"] = rc[k][None]
        for i, n in enumerate(small_names):
            o = outs[12 + k * _N_SMALL + i]
            res[kind, n] = o.T if n == "rel_bias" else o.reshape(given[n][0].shape)
    order = ["pre_norm_g", "post_norm_g", "mem_norm_g", "w_in", "w_mem_kv", "v_norm_g", "v_norm_b", "w_spatial",
             "b_spatial", "attn_sinks", "rel_bias", "w_out"]
    flat = [res[kind, n] for kind in ("grad", "delta", "new_m", "new_v") for n in order]
    return (loss, gx, *flat)
```

```python
import numpy as np
import jax
import jax.numpy as jnp
from jax import lax
from jax.experimental import pallas as pl
from jax.experimental.pallas import tpu as pltpu

F32 = jnp.float32
BF16 = jnp.bfloat16
MM = jnp.bfloat16

D_MODEL = 1024
CHUNK = 128
A_GROUPS = 4
A_WIDTH = 512
UV_W = 1024
QKV_W = 768
Z_W = 1024
IN_WIDTH = UV_W + QKV_W + Z_W
MEM_LEN = 256
N_BUCKETS = 32
MAX_DISTANCE = 128
EPS = 1e-6
NEG = -1e30
SCALE = 0.125
N_DEV = 8
SHARD_IN = IN_WIDTH // N_DEV
SHARD_O = D_MODEL // N_DEV

SQ_COL, SK_COL, SV_COL, MQ_COL, Z_COL = UV_W, UV_W + 256, UV_W + 384, UV_W + 512, UV_W + QKV_W
YB_OFF, YC_OFF = 512, 768

ADAM_LR = 0.001
ADAM_B1 = 0.9
ADAM_B2 = 0.999
ADAM_EPS = 1e-08
ADAM_WD = 0.01
ADAM_STEP = 10

VMEM_LIMIT = 60 * 1024 * 1024

_GELU_C = 0.7978845608028654
_GELU_A = 0.044715

MESH = pl.DeviceIdType.MESH
_ROWS = 32


def _dot(a, b):
    return lax.dot_general(a, b, (((1,), (0,)), ((), ())), preferred_element_type=F32)


def _dot_nt(a, b):
    return lax.dot_general(a, b, (((1,), (1,)), ((), ())), preferred_element_type=F32)


def _dot_tn(a, b):
    return lax.dot_general(a, b, (((0,), (0,)), ((), ())), preferred_element_type=F32)


def _gelu_and_grad(x):
    x2 = x * x
    t = jnp.tanh(_GELU_C * (x + _GELU_A * x * x2))
    g = 0.5 * x * (1.0 + t)
    dg = 0.5 * (1.0 + t) + 0.5 * x * (1.0 - t * t) * (_GELU_C * (1.0 + 3.0 * _GELU_A * x2))
    return g, dg


def _t5_buckets():
    qi = np.arange(CHUNK)[:, None]
    kj = np.arange(2 * CHUNK)[None, :]
    n = np.maximum(qi + CHUNK - kj, 0)
    max_exact = N_BUCKETS // 2
    large = max_exact + (np.log(np.maximum(n, 1) / max_exact) / np.log(MAX_DISTANCE / max_exact)
                         * (N_BUCKETS - max_exact)).astype(np.int32)
    large = np.minimum(large, N_BUCKETS - 1)
    return np.where(n < max_exact, n, large).astype(np.int32)


def _params(**kw):
    return pltpu.CompilerParams(vmem_limit_bytes=VMEM_LIMIT, **kw)


def _full(shape, single=False):
    nd = len(shape)
    if single:
        return pl.BlockSpec(shape, lambda *_: (0,) * nd, pipeline_mode=pl.Buffered(1))
    return pl.BlockSpec(shape, lambda *_: (0,) * nd)


def _window_valid():
    qi = lax.broadcasted_iota(jnp.int32, (CHUNK, 2 * CHUNK), 0)
    kj = lax.broadcasted_iota(jnp.int32, (CHUNK, 2 * CHUNK), 1)
    dist = qi + CHUNK - kj
    return (dist >= 0) & (dist < CHUNK)


def _position():
    return lax.axis_index("x"), lax.axis_index("y"), lax.axis_index("c")


def _other_chips(x, y):
    return [(1 - x, y), (x, 1 - y), (1 - x, 1 - y)]


def _route(x, y, c):
    first = (x * c + (1 - x) * (1 - c), y * (1 - c) + (1 - y) * c)
    second = (x * (1 - c) + (1 - x) * c, y * c + (1 - y) * (1 - c))
    return first, second, (1 - x, 1 - y)


def _remote(src, dst, ssem, rsem, to):
    return pltpu.make_async_remote_copy(src_ref=src, dst_ref=dst, send_sem=ssem, recv_sem=rsem,
                                        device_id=to, device_id_type=MESH)


def _rows_loop(nrow, fn):
    def step(i, _):
        fn(pl.ds(pl.multiple_of(i * _ROWS, _ROWS), _ROWS))
        return 0

    lax.fori_loop(0, nrow // _ROWS, step, 0)


class _Gather:
    def __init__(self, pos, out, ssem, rsem):
        self.x, self.y, self.c = pos
        self.out, self.ssem, self.rsem = out, ssem, rsem
        self.me = 4 * self.x + 2 * self.y + self.c
        self.here = (self.x, self.y, self.c)
        self.sib = (self.x, self.y, 1 - self.c)
        self.first, self.second, self.far = _route(*pos)

    def _copy(self, k, blk, to):
        r = self.out.at[blk]
        return _remote(r, r, self.ssem.at[k], self.rsem.at[k], to)

    def _idx(self, chip, core):
        return 4 * chip[0] + 2 * chip[1] + core

    def _on(self, chip):
        return (chip[0], chip[1], self.c)

    def start(self):
        self._copy(0, self.me, self.sib).start()
        self._copy(1, self.me, self._on(self.first)).start()
        self._copy(2, self.me, self._on(self.second)).start()

    def forward(self):
        c = self.c
        self._copy(1, self._idx(self.first, c), self.here).wait_recv()
        self._copy(3, self._idx(self.first, c), self._on(self.second)).start()
        self._copy(4, self._idx(self.first, c), self.sib).start()
        self._copy(2, self._idx(self.second, c), self.here).wait_recv()
        self._copy(5, self._idx(self.second, c), self.sib).start()
        self._copy(3, self._idx(self.far, c), self.here).wait_recv()
        self._copy(6, self._idx(self.far, c), self.sib).start()

    def finish(self):
        c = self.c
        self._copy(0, self._idx((self.x, self.y), 1 - c), self.here).wait_recv()
        for k, chip in ((4, self.second), (5, self.first), (6, self.far)):
            self._copy(k, self._idx(chip, 1 - c), self.here).wait_recv()
        self._copy(0, self.me, self.sib).wait_send()
        self._copy(1, self.me, self._on(self.first)).wait_send()
        self._copy(2, self.me, self._on(self.second)).wait_send()
        self._copy(3, self._idx(self.first, c), self._on(self.second)).wait_send()
        for k, chip in ((4, self.first), (5, self.second), (6, self.far)):
            self._copy(k, self._idx(chip, c), self.sib).wait_send()


def _prep_tables(rb_ref, w_ref, b_ref, bk_ref, bias_ref, wt_ref, wtt_ref, bcol_ref):
    valid = _window_valid()
    bk = bk_ref[...]
    acc = [jnp.full((CHUNK, 2 * CHUNK), NEG, F32) for _ in range(4)]
    for b in range(N_BUCKETS):
        hit = (bk == b) & valid
        for h in range(4):
            acc[h] = jnp.where(hit, rb_ref[b, h], acc[h])
    for h in range(4):
        bias_ref[h] = acc[h]
    r = lax.broadcasted_iota(jnp.int32, (CHUNK, CHUNK), 0)
    c = lax.broadcasted_iota(jnp.int32, (CHUNK, CHUNK), 1)
    for g in range(A_GROUPS):
        w = jnp.where(r >= c, w_ref[g], 0.0)
        wt_ref[g] = w.astype(MM)
        wtt_ref[g] = w.T.astype(MM)
        bcol_ref[g] = jnp.broadcast_to(b_ref[g:g + 1, :], (CHUNK, CHUNK)).T


def _wgather(a, b, c, rel_bias, w_sp, b_sp, buckets, mem2, gm):
    tmem = mem2.shape[0]

    def body(a_ref, b_ref, c_ref, rb_ref, w_ref, bsp_ref, bk_ref, m_ref, gm_ref,
             oa, ob, oc, bias_ref, wt_ref, wtt_ref, bcol_ref, mkv_ref, ssem, rsem):
        pos = _position()
        me = 4 * pos[0] + 2 * pos[1] + pos[2]
        gathers = []
        for k, (src, out) in enumerate(((c_ref, oc), (b_ref, ob), (a_ref, oa))):
            out[me] = src[...].astype(BF16)
            g = _Gather(pos, out, ssem.at[k], rsem.at[k])
            g.start()
            gathers.append(g)
        _prep_tables(rb_ref, w_ref, bsp_ref, bk_ref, bias_ref, wt_ref, wtt_ref, bcol_ref)
        for g in gathers:
            g.forward()
        gathers[0].finish()
        xf = m_ref[...]
        hm = (xf * _rms(xf) * gm_ref[...]).astype(MM)
        acc = jnp.zeros((tmem, 2 * MEM_LEN), F32)
        for d in range(N_DEV):
            acc = acc + _dot(hm[:, d * SHARD_O:(d + 1) * SHARD_O], oc[d])
        mkv_ref[...] = acc.astype(MM)
        for g in gathers[1:]:
            g.finish()

    vm = pl.BlockSpec(memory_space=pltpu.VMEM)
    grp = (A_GROUPS, CHUNK, CHUNK)
    return pl.pallas_call(
        body, name="wgather",
        out_shape=(jax.ShapeDtypeStruct((N_DEV,) + a.shape, BF16),
                   jax.ShapeDtypeStruct((N_DEV,) + b.shape, BF16),
                   jax.ShapeDtypeStruct((N_DEV,) + c.shape, BF16),
                   jax.ShapeDtypeStruct((4, CHUNK, 2 * CHUNK), F32),
                   jax.ShapeDtypeStruct(grp, MM), jax.ShapeDtypeStruct(grp, MM), jax.ShapeDtypeStruct(grp, F32),
                   jax.ShapeDtypeStruct((tmem, 2 * MEM_LEN), MM)),
        in_specs=[vm, vm, vm, pl.BlockSpec(memory_space=pltpu.SMEM), vm, vm, vm, vm, vm],
        out_specs=tuple([vm] * 8),
        scratch_shapes=[pltpu.SemaphoreType.DMA((3, 7)), pltpu.SemaphoreType.DMA((3, 7))],
        compiler_params=_params(),
    )(a, b, c, rel_bias, w_sp, b_sp, buckets, mem2, gm)


def _half_masks(rows):
    lane = lax.broadcasted_iota(jnp.int32, (rows, CHUNK), 1)
    return lane < 64


def _dup_heads(band):
    b32 = band.astype(F32)
    rolled = pltpu.roll(b32, 64, 1)
    lo = _half_masks(band.shape[0])
    return (jnp.where(lo, b32, rolled).astype(MM), jnp.where(lo, rolled, b32).astype(MM))


def _swa_probs(qsel, kd, bias_h, sink_h, first_add):
    s = _dot_nt(qsel, kd) * SCALE + bias_h + first_add
    m = jnp.maximum(jnp.max(s, axis=-1, keepdims=True), sink_h)
    p = jnp.exp(s - m)
    es = jnp.exp(sink_h - m)
    inv = 1.0 / (jnp.sum(p, axis=-1, keepdims=True) + es)
    return p * inv, es * inv


def _softmax(s):
    m = jnp.max(s, axis=-1, keepdims=True)
    p = jnp.exp(s - m)
    return p * (1.0 / jnp.sum(p, axis=-1, keepdims=True))


def _first_block_mask(n):
    col = lax.broadcasted_iota(jnp.int32, (CHUNK, 2 * CHUNK), 1)
    return jnp.where((col < CHUNK) & (n == 0), NEG, 0.0)


def _rms(xf):
    return lax.rsqrt(jnp.mean(xf * xf, axis=-1, keepdims=True) + EPS)


def _layer(x2, tgt2, mkv3, bias, sinks, vg, vb, wt, wtt, bcol, g1, g2, w_in_t, w_o, buckets, nb, s, tm):
    nt = s // tm
    bpt = tm // CHUNK
    bps = s // CHUNK
    t = nb * s

    def body(x_ref, xp_ref, t_ref, mkv_ref, bias_ref, sink_ref, vg_ref, vb_ref, wt_ref, wtt_ref, bcol_ref,
             g1_ref, g2_ref, wi_ref, wo_ref, bk_ref,
             gx_ref, dmkv_ref, dwi_hbm, dwo_hbm, dg1_ref, dg2_ref, loss_ref, dwsp_ref, dbs_ref,
             dvg_ref, dvb_ref, dsink_ref, drel_ref,
             acc_i, acc_o, uv_s, z_s, q_s, kv_s, h_s, dp_s, dxo_s,
             ycat, dyc, u_s, gu_s, gv_s, xh_s, vc_s, pb_s, ps_s, pc_s, kd_s, vd_s,
             dkv_acc, dbias_acc, dsv_acc, dsink_acc, sems):
        b, j = pl.program_id(0), pl.program_id(1)
        jt = nt - 1 - j

        @pl.when((b == 0) & (j == 0))
        def _():
            for ref in (acc_i, acc_o, dg1_ref, dg2_ref, loss_ref, dwsp_ref, dvg_ref, dvb_ref,
                        dbias_acc, dsv_acc, dsink_acc):
                ref[...] = jnp.zeros_like(ref)

        @pl.when(j == 0)
        def _():
            dmkv_ref[...] = jnp.zeros_like(dmkv_ref)
            dkv_acc[...] = jnp.zeros_like(dkv_acc)

        carry = dkv_acc[0:CHUNK, :]
        dkv_acc[...] = jnp.zeros_like(dkv_acc)
        dkv_acc[tm:tm + CHUNK, :] = carry

        lo = _half_masks(CHUNK)
        lob = _half_masks(2 * CHUNK)
        lot = _half_masks(tm)
        g1v = g1_ref[...]

        xf = x_ref[...]
        h = (xf * _rms(xf) * g1v).astype(MM)
        h_s[...] = h
        uv_s[...] = _dot_nt(h, wi_ref[0:UV_W, :])
        qkv = _dot_nt(h, wi_ref[SQ_COL:Z_COL, :])
        q_s[:, 0:256] = qkv[:, 0:256].astype(MM)
        q_s[:, 256:512] = qkv[:, 512:768].astype(MM)
        kv_s[CHUNK:CHUNK + tm, :] = qkv[:, 256:512].astype(MM)
        z_s[...] = _dot_nt(h, wi_ref[Z_COL:IN_WIDTH, :])
        xp = xp_ref[...]
        hp = (xp * _rms(xp) * g1v).astype(MM)
        kv_s[0:CHUNK, :] = _dot_nt(hp, wi_ref[SK_COL:MQ_COL, :]).astype(MM)

        for blk in range(bpt):
            r0 = blk * CHUNK
            rows = slice(r0, r0 + CHUNK)
            n = jt * bpt + blk
            for g in range(A_GROUPS):
                cg = slice(g * CHUNK, (g + 1) * CHUNK)
                u, gu = _gelu_and_grad(uv_s[rows, cg])
                v, gv = _gelu_and_grad(uv_s[rows, A_WIDTH + g * CHUNK:A_WIDTH + (g + 1) * CHUNK])
                mu = jnp.mean(v, axis=-1, keepdims=True)
                xc = v - mu
                rstd = lax.rsqrt(jnp.mean(xc * xc, axis=-1, keepdims=True) + EPS)
                xhat = xc * rstd
                vc = (xhat * vg_ref[:, cg] + vb_ref[:, cg]).astype(MM)
                sv = _dot(wt_ref[g], vc) + bcol_ref[g]
                u_s[rows, cg] = u
                gu_s[rows, cg] = sv * gu
                gv_s[rows, cg] = rstd * gv
                xh_s[rows, cg] = xhat
                vc_s[rows, cg] = vc
                ycat[rows, cg] = u * sv
            kd = _dup_heads(kv_s[r0:r0 + 2 * CHUNK, 0:CHUNK])
            vd = _dup_heads(kv_s[r0:r0 + 2 * CHUNK, CHUNK:2 * CHUNK])
            first_add = _first_block_mask(n)
            for kvh in range(2):
                kd_s[blk * 2 + kvh] = kd[kvh]
                vd_s[blk * 2 + kvh] = vd[kvh]
                q128 = q_s[rows, kvh * CHUNK:(kvh + 1) * CHUNK].astype(F32)
                outs = []
                for gi in range(2):
                    hd = 2 * kvh + gi
                    qsel = jnp.where(lo if gi == 0 else ~lo, q128, 0.0).astype(MM)
                    probs, ps = _swa_probs(qsel, kd[kvh], bias_ref[hd], sink_ref[hd], first_add)
                    pb_s[blk * 4 + hd] = probs
                    ps_s[blk * 4 + hd] = jnp.broadcast_to(ps, (CHUNK, CHUNK))
                    outs.append(_dot(probs.astype(MM), vd[kvh]))
                ycat[rows, YB_OFF + kvh * CHUNK:YB_OFF + (kvh + 1) * CHUNK] = jnp.where(lo, outs[0], outs[1])
        for g in range(2):
            q128 = q_s[:, 256 + g * CHUNK:256 + (g + 1) * CHUNK].astype(F32)
            k128 = mkv_ref[:, g * CHUNK:(g + 1) * CHUNK]
            v128 = mkv_ref[:, MEM_LEN + g * CHUNK:MEM_LEN + (g + 1) * CHUNK]
            outs = []
            for hh in range(2):
                qsel = jnp.where(lot if hh == 0 else ~lot, q128, 0.0).astype(MM)
                probs = _softmax(_dot_nt(qsel, k128) * SCALE)
                pc_s[2 * g + hh] = probs
                outs.append(_dot(probs.astype(MM), v128))
            ycat[:, YC_OFF + g * CHUNK:YC_OFF + (g + 1) * CHUNK] = jnp.where(lot, outs[0], outs[1])

        zt = z_s[...]
        sig = 1.0 / (1.0 + jnp.exp(-zt))
        silu = zt * sig
        yc = ycat[...]
        yb = (yc * silu).astype(MM)
        o = _dot(yb, wo_ref[...])
        r2 = _rms(o)
        nrm = o * r2
        g2v = g2_ref[...]
        e = x_ref[...] + nrm * g2v - t_ref[...]
        l1 = jnp.sum(e * e, axis=-1, keepdims=True)
        loss_ref[...] += jnp.broadcast_to(jnp.sum(l1, axis=0, keepdims=True) * (0.5 / D_MODEL), loss_ref.shape)
        dxo = e * (1.0 / D_MODEL)
        dxo_s[...] = dxo
        dg2_ref[...] += jnp.sum(dxo * nrm, axis=0, keepdims=True)
        dn = dxo * g2v
        do = r2 * (dn - nrm * jnp.mean(dn * nrm, axis=-1, keepdims=True))
        dob = do.astype(MM)
        dy = _dot_nt(dob, wo_ref[...])
        dp_s[:, Z_COL:IN_WIDTH] = (dy * yc * (sig * (1.0 + zt * (1.0 - sig)))).astype(MM)
        dyc[...] = dy * silu
        acc_o[...] += _dot_tn(yb, dob)

        for blk in range(bpt):
            r0 = blk * CHUNK
            rows = slice(r0, r0 + CHUNK)
            for g in range(A_GROUPS):
                cg = slice(g * CHUNK, (g + 1) * CHUNK)
                cv = slice(A_WIDTH + g * CHUNK, A_WIDTH + (g + 1) * CHUNK)
                dya = dyc[rows, cg]
                dp_s[rows, cg] = (dya * gu_s[rows, cg]).astype(MM)
                dsv = dya * u_s[rows, cg]
                dsvb = dsv.astype(MM)
                dsv_acc[g] += dsv
                dwsp_ref[g] += _dot_nt(dsvb, vc_s[rows, cg])
                dvc = _dot(wtt_ref[g], dsvb)
                xhat = xh_s[rows, cg]
                dvg_ref[:, cg] += jnp.sum(dvc * xhat, axis=0, keepdims=True)
                dvb_ref[:, cg] += jnp.sum(dvc, axis=0, keepdims=True)
                dxh = dvc * vg_ref[:, cg]
                dv = (dxh - jnp.mean(dxh, axis=-1, keepdims=True)
                      - xhat * jnp.mean(dxh * xhat, axis=-1, keepdims=True))
                dp_s[rows, cv] = (dv * gv_s[rows, cg]).astype(MM)
            dk_f, dv_f = [], []
            for kvh in range(2):
                kd = kd_s[blk * 2 + kvh]
                vd = vd_s[blk * 2 + kvh]
                q128 = q_s[rows, kvh * CHUNK:(kvh + 1) * CHUNK].astype(F32)
                do128 = dyc[rows, YB_OFF + kvh * CHUNK:YB_OFF + (kvh + 1) * CHUNK]
                dq128 = jnp.zeros((CHUNK, CHUNK), F32)
                dkd = jnp.zeros((2 * CHUNK, CHUNK), F32)
                dvd = jnp.zeros((2 * CHUNK, CHUNK), F32)
                for gi in range(2):
                    hd = 2 * kvh + gi
                    half = lo if gi == 0 else ~lo
                    qsel = jnp.where(half, q128, 0.0).astype(MM)
                    dosel = jnp.where(half, do128, 0.0).astype(MM)
                    probs = pb_s[blk * 4 + hd]
                    ps = ps_s[blk * 4 + hd][:, 0:1]
                    dp = _dot_nt(dosel, vd)
                    delta = jnp.sum(probs * dp, axis=-1, keepdims=True)
                    ds = probs * (dp - delta)
                    dbias_acc[hd] += ds
                    dsink_acc[hd:hd + 1, :] += jnp.broadcast_to(-jnp.sum(ps * delta, axis=0, keepdims=True), (1, CHUNK))
                    dss = (ds * SCALE).astype(MM)
                    dq128 = dq128 + jnp.where(half, _dot(dss, kd), 0.0)
                    dkd = dkd + _dot_tn(dss, qsel)
                    dvd = dvd + _dot_tn(probs.astype(MM), dosel)
                dp_s[rows, SQ_COL + kvh * CHUNK:SQ_COL + (kvh + 1) * CHUNK] = dq128.astype(MM)
                dk_f.append(dkd + pltpu.roll(dkd, 64, 1))
                dv_f.append(dvd + pltpu.roll(dvd, 64, 1))
            dkv_acc[r0:r0 + 2 * CHUNK, 0:CHUNK] += jnp.where(lob, dk_f[0], dk_f[1])
            dkv_acc[r0:r0 + 2 * CHUNK, CHUNK:2 * CHUNK] += jnp.where(lob, dv_f[0], dv_f[1])
        dp_s[:, SK_COL:MQ_COL] = dkv_acc[CHUNK:CHUNK + tm, :].astype(MM)
        for g in range(2):
            q128 = q_s[:, 256 + g * CHUNK:256 + (g + 1) * CHUNK].astype(F32)
            k128 = mkv_ref[:, g * CHUNK:(g + 1) * CHUNK]
            v128 = mkv_ref[:, MEM_LEN + g * CHUNK:MEM_LEN + (g + 1) * CHUNK]
            do128 = dyc[:, YC_OFF + g * CHUNK:YC_OFF + (g + 1) * CHUNK]
            dq128 = jnp.zeros((tm, CHUNK), F32)
            dk128 = jnp.zeros((MEM_LEN, CHUNK), F32)
            dv128 = jnp.zeros((MEM_LEN, CHUNK), F32)
            for hh in range(2):
                half = lot if hh == 0 else ~lot
                qsel = jnp.where(half, q128, 0.0).astype(MM)
                dosel = jnp.where(half, do128, 0.0).astype(MM)
                probs = pc_s[2 * g + hh]
                dp = _dot_nt(dosel, v128)
                ds = probs * (dp - jnp.sum(probs * dp, axis=-1, keepdims=True))
                dss = (ds * SCALE).astype(MM)
                dq128 = dq128 + jnp.where(half, _dot(dss, k128), 0.0)
                dk128 = dk128 + _dot_tn(dss, qsel)
                dv128 = dv128 + _dot_tn(probs.astype(MM), dosel)
            dp_s[:, MQ_COL + g * CHUNK:MQ_COL + (g + 1) * CHUNK] = dq128.astype(MM)
            dmkv_ref[:, g * CHUNK:(g + 1) * CHUNK] += dk128
            dmkv_ref[:, MEM_LEN + g * CHUNK:MEM_LEN + (g + 1) * CHUNK] += dv128

        hv = h_s[...]
        dh = jnp.zeros((tm, D_MODEL), F32)
        for c0, c1 in ((0, UV_W), (SQ_COL, Z_COL), (Z_COL, IN_WIDTH)):
            dpt = dp_s[:, c0:c1]
            acc_i[c0:c1, :] += _dot_tn(dpt, hv)
            dh = dh + _dot(dpt, wi_ref[c0:c1, :])
        xf = x_ref[...]
        r = _rms(xf)
        nx = xf * r
        dg1_ref[...] += jnp.sum(dh * nx, axis=0, keepdims=True)
        dnx = dh * g1v
        gx_ref[...] = dxo_s[...] + r * (dnx - nx * jnp.mean(dnx * nx, axis=-1, keepdims=True))

        @pl.when((b == nb - 1) & (j == nt - 1))
        def _():
            out_i = pltpu.make_async_copy(acc_i, dwi_hbm, sems.at[0])
            out_o = pltpu.make_async_copy(acc_o, dwo_hbm, sems.at[1])
            out_i.start()
            out_o.start()
            r_ = lax.broadcasted_iota(jnp.int32, (CHUNK, CHUNK), 0)
            c_ = lax.broadcasted_iota(jnp.int32, (CHUNK, CHUNK), 1)
            for g in range(A_GROUPS):
                dwsp_ref[g] = jnp.where(r_ >= c_, dwsp_ref[g], 0.0)
                dbs_ref[g:g + 1, :] = jnp.sum(dsv_acc[g].T, axis=0, keepdims=True)
            rows8 = lax.broadcasted_iota(jnp.int32, (8, CHUNK), 0)
            cols8 = lax.broadcasted_iota(jnp.int32, (8, CHUNK), 1)
            sk = jnp.zeros((8, CHUNK), F32)
            for hd in range(4):
                sk = sk + jnp.where((rows8 == 0) & (cols8 == hd),
                                    jnp.broadcast_to(dsink_acc[hd:hd + 1, :], (8, CHUNK)), 0.0)
            dsink_ref[...] = sk
            bk = bk_ref[...]
            valid = _window_valid()
            rrow = lax.broadcasted_iota(jnp.int32, (N_BUCKETS, CHUNK), 0)
            rcol = lax.broadcasted_iota(jnp.int32, (N_BUCKETS, CHUNK), 1)
            acc = jnp.zeros((N_BUCKETS, CHUNK), F32)
            for bb in range(N_BUCKETS):
                hit = (bk == bb) & valid
                for hd in range(4):
                    part = jnp.sum(jnp.where(hit, dbias_acc[hd], 0.0), axis=-1, keepdims=True)
                    tot = jnp.sum(part, axis=0, keepdims=True)
                    acc = acc + jnp.where((rrow == bb) & (rcol == hd), jnp.broadcast_to(tot, (N_BUCKETS, CHUNK)), 0.0)
            drel_ref[...] = acc
            out_i.wait()
            out_o.wait()

    tile = lambda w: pl.BlockSpec((tm, w), lambda b, j: (b * nt + nt - 1 - j, 0))
    prev_block = pl.BlockSpec((CHUNK, D_MODEL), lambda b, j: (b * bps + jnp.maximum((nt - 1 - j) * bpt - 1, 0), 0))
    per_batch = lambda r, w: pl.BlockSpec((None, r, w), lambda b, j: (b, 0, 0))
    anyspec = pl.BlockSpec(memory_space=pl.ANY)
    grp = (A_GROUPS, CHUNK, CHUNK)
    return pl.pallas_call(
        body, name="layer", grid=(nb, nt),
        out_shape=(jax.ShapeDtypeStruct((t, D_MODEL), F32),
                   jax.ShapeDtypeStruct((nb, MEM_LEN, 2 * MEM_LEN), F32),
                   jax.ShapeDtypeStruct((IN_WIDTH, D_MODEL), F32),
                   jax.ShapeDtypeStruct((D_MODEL, D_MODEL), F32),
                   jax.ShapeDtypeStruct((1, D_MODEL), F32),
                   jax.ShapeDtypeStruct((1, D_MODEL), F32),
                   jax.ShapeDtypeStruct((8, CHUNK), F32),
                   jax.ShapeDtypeStruct(grp, F32),
                   jax.ShapeDtypeStruct((A_GROUPS, CHUNK), F32),
                   jax.ShapeDtypeStruct((1, A_WIDTH), F32),
                   jax.ShapeDtypeStruct((1, A_WIDTH), F32),
                   jax.ShapeDtypeStruct((8, CHUNK), F32),
                   jax.ShapeDtypeStruct((N_BUCKETS, CHUNK), F32)),
        in_specs=[tile(D_MODEL), prev_block, tile(D_MODEL), per_batch(MEM_LEN, 2 * MEM_LEN),
                  _full((4, CHUNK, 2 * CHUNK)),
                  pl.BlockSpec(memory_space=pltpu.SMEM),
                  _full((1, A_WIDTH)), _full((1, A_WIDTH)),
                  _full(grp), _full(grp), _full(grp),
                  _full((1, D_MODEL)), _full((1, D_MODEL)),
                  _full((IN_WIDTH, D_MODEL), single=True), _full((D_MODEL, D_MODEL), single=True),
                  _full((CHUNK, 2 * CHUNK))],
        out_specs=(tile(D_MODEL), per_batch(MEM_LEN, 2 * MEM_LEN), anyspec, anyspec,
                   _full((1, D_MODEL)), _full((1, D_MODEL)), _full((8, CHUNK)),
                   _full(grp), _full((A_GROUPS, CHUNK)), _full((1, A_WIDTH)), _full((1, A_WIDTH)),
                   _full((8, CHUNK)), _full((N_BUCKETS, CHUNK))),
        scratch_shapes=[pltpu.VMEM((IN_WIDTH, D_MODEL), F32), pltpu.VMEM((D_MODEL, D_MODEL), F32),
                        pltpu.VMEM((tm, UV_W), F32), pltpu.VMEM((tm, Z_W), F32),
                        pltpu.VMEM((tm, 512), MM), pltpu.VMEM((tm + CHUNK, 2 * CHUNK), MM),
                        pltpu.VMEM((tm, D_MODEL), MM), pltpu.VMEM((tm, IN_WIDTH), MM),
                        pltpu.VMEM((tm, D_MODEL), F32),
                        pltpu.VMEM((tm, D_MODEL), F32), pltpu.VMEM((tm, D_MODEL), F32)]
                       + [pltpu.VMEM((tm, A_WIDTH), F32) for _ in range(4)]
                       + [pltpu.VMEM((tm, A_WIDTH), MM),
                          pltpu.VMEM((bpt * 4, CHUNK, 2 * CHUNK), F32),
                          pltpu.VMEM((bpt * 4, CHUNK, CHUNK), F32),
                          pltpu.VMEM((4, tm, MEM_LEN), F32),
                          pltpu.VMEM((bpt * 2, 2 * CHUNK, CHUNK), MM),
                          pltpu.VMEM((bpt * 2, 2 * CHUNK, CHUNK), MM),
                          pltpu.VMEM((tm + CHUNK, 2 * CHUNK), F32),
                          pltpu.VMEM((4, CHUNK, 2 * CHUNK), F32),
                          pltpu.VMEM(grp, F32),
                          pltpu.VMEM((8, CHUNK), F32),
                          pltpu.SemaphoreType.DMA((2,))],
        compiler_params=_params(dimension_semantics=("arbitrary", "arbitrary")),
    )(x2, x2, tgt2, mkv3, bias, sinks, vg, vb, wt, wtt, bcol, g1, g2, w_in_t, w_o, buckets)


class _ShardReduce:
    def __init__(self, pos, g, bufs, sems):
        self.x, self.y, self.c = pos
        self.g = g
        self.own, self.rcv, self.sbuf, self.rbuf, self.cbuf = bufs
        self.ld, self.sa, self.ra, self.sb, self.rb = sems
        self.nrow = g.shape[1]
        self.here = (self.x, self.y, self.c)
        self.sib = (self.x, self.y, 1 - self.c)
        self.first, self.second, self.far = _route(*pos)

    def _load(self, q):
        return pltpu.make_async_copy(self.g.at[2 * q + self.c], self.own.at[q], self.ld.at[q])

    def _to_sib(self, q, to):
        return _remote(self.g.at[2 * q + 1 - self.c], self.rcv.at[q], self.sa.at[q], self.ra.at[q], to)

    def _send(self, k, to):
        dst = self.cbuf.at[0] if k == 1 else self.rbuf.at[0 if k == 0 else 1]
        return _remote(self.sbuf.at[k], dst, self.sb.at[k], self.rb.at[k], to)

    def _stage(self, k, which, extra=None):
        def cast(r):
            v = self.rcv[which, r, :]
            if extra is not None:
                v = v + extra[0, r, :].astype(F32)
            self.sbuf[k, r, :] = v.astype(BF16)

        _rows_loop(self.nrow, cast)

    @staticmethod
    def _q(chip):
        return 2 * chip[0] + chip[1]

    def start(self):
        for q in range(4):
            self._load(q).start()
            self._to_sib(q, self.sib).start()

    def mid(self):
        for q in range(4):
            self._load(q).wait()
            self._to_sib(q, self.here).wait_recv()

        def add(r):
            for q in range(4):
                self.rcv[q, r, :] = self.rcv[q, r, :] + self.own[q, r, :]

        _rows_loop(self.nrow, add)
        to_first = (self.first[0], self.first[1], self.c)
        self._stage(0, self._q(self.first))
        self._send(0, to_first).start()
        self._stage(1, self._q(self.far))
        self._send(1, to_first).start()

    def pass_on(self):
        self._send(1, self.here).wait_recv()
        self._stage(2, self._q(self.second), extra=self.cbuf)
        self._send(2, (self.second[0], self.second[1], self.c)).start()

    def finish(self, out):
        self._send(0, self.here).wait_recv()
        self._send(2, self.here).wait_recv()
        which = 2 * self.x + self.y

        def tot(r):
            out[r, :] = (self.rcv[which, r, :] + self.rbuf[0, r, :].astype(F32)) + self.rbuf[1, r, :].astype(F32)

        _rows_loop(self.nrow, tot)
        for q in range(4):
            self._to_sib(q, self.sib).wait_send()
        to_first = (self.first[0], self.first[1], self.c)
        self._send(0, to_first).wait_send()
        self._send(1, to_first).wait_send()
        self._send(2, (self.second[0], self.second[1], self.c)).wait_send()


def _reduce_scratch(shape):
    return [pltpu.VMEM((4,) + shape, F32), pltpu.VMEM((4,) + shape, F32),
            pltpu.VMEM((3,) + shape, BF16), pltpu.VMEM((2,) + shape, BF16), pltpu.VMEM((1,) + shape, BF16),
            pltpu.SemaphoreType.DMA((4,)), pltpu.SemaphoreType.DMA((4,)), pltpu.SemaphoreType.DMA((4,)),
            pltpu.SemaphoreType.DMA((3,)), pltpu.SemaphoreType.DMA((3,))]


_N_RED = 10

_S_LAYOUT = (((1, D_MODEL), 0), ((1, D_MODEL), 8), ((1, D_MODEL), 16),
             ((1, A_WIDTH), 24), ((1, A_WIDTH), 28), ((A_GROUPS, CHUNK), 32),
             ((1, 4), 36), ((N_BUCKETS, 4), 40),
             ((A_GROUPS * CHUNK, CHUNK), 72))
_LOSS_ROW = 37
_W_SP_ROW = _S_LAYOUT[-1][1]
_S_ROWS = _W_SP_ROW + A_GROUPS * CHUNK
_N_SMALL = len(_S_LAYOUT)


def _pack_rows(dst, refs):
    for (shp, r0), ref in zip(_S_LAYOUT, refs):
        if shp[0] == 1 and shp[1] >= CHUNK:
            for i in range(shp[1] // CHUNK):
                dst[r0 + i:r0 + i + 1, :] = ref[:, i * CHUNK:(i + 1) * CHUNK]
        elif ref.shape[-1] == CHUNK:
            dst[r0:r0 + shp[0], :] = ref[0:shp[0], :]
        else:
            dst[r0:r0 + shp[0], 0:shp[1]] = ref[...]


def _unpack_rows(src, refs):
    for (shp, r0), ref in zip(_S_LAYOUT, refs):
        if shp[0] == 1 and shp[1] >= CHUNK:
            for i in range(shp[1] // CHUNK):
                ref[:, i * CHUNK:(i + 1) * CHUNK] = src[r0 + i:r0 + i + 1, :]
        elif shp[1] == CHUNK:
            ref[...] = src[r0:r0 + shp[0], :]
        else:
            if tuple(ref.shape) == (shp[1], shp[0]):
                ref[...] = src[r0:r0 + CHUNK, :].T[0:shp[1], 0:shp[0]]
            else:
                ref[...] = src[r0:r0 + shp[0], 0:shp[1]]


_MEM_G = 2


def _greduce(ga, gb, dmkv, mem2, gm, w_mkv, small_g, loss_p):
    shp_c = (SHARD_O, 2 * MEM_LEN)
    shapes = (shp_c, gb.shape[1:], ga.shape[1:])
    rs = _S_ROWS

    def body(*refs):
        it = iter(refs)
        take = lambda n: [next(it) for _ in range(n)]
        gb_ref, ga_ref, d_ref, m_ref, gm_ref, wm_ref = take(6)
        sg_refs = take(_N_SMALL - 1)
        loss_ref, = take(1)
        oc, ob, oa, ogs = take(4)
        red = take(3 * _N_RED)
        gs_ref, rs_a, rs_b, rs_w, gc_ref, dgm_ref = take(6)
        ssem_a, rsem_a, ssem_b, rsem_b = take(4)

        pos = _position()
        x, y, cc = pos
        myq = 2 * x + y
        here, sib = (x, y, cc), (x, y, 1 - cc)
        chips = _other_chips(x, y)
        reducers = [_ShardReduce(pos, g, red[k * _N_RED:k * _N_RED + 5], red[k * _N_RED + 5:(k + 1) * _N_RED])
                    for k, g in enumerate((gc_ref, gb_ref, ga_ref))]
        for rd in reducers[1:]:
            rd.start()

        xf = m_ref[...]
        nm = xf * _rms(xf)
        hm = (nm * gm_ref[...]).astype(MM)
        d = d_ref[...].astype(MM)
        for o in range(N_DEV):
            gc_ref[o] = _dot_tn(hm[:, o * SHARD_O:(o + 1) * SHARD_O], d)
        dgm_ref[...] = jnp.sum(_dot_nt(d, wm_ref[...]) * nm, axis=0, keepdims=True)
        reducers[0].start()

        gs_ref[...] = jnp.zeros_like(gs_ref)
        _pack_rows(gs_ref, sg_refs[:_MEM_G] + [dgm_ref] + sg_refs[_MEM_G:])
        gs_ref[_LOSS_ROW:_LOSS_ROW + 1, :] = loss_ref[0:1, :]
        small_a = _remote(gs_ref, rs_a, ssem_a, rsem_a, sib)
        small_a.start()

        _remote(gs_ref, rs_a, ssem_a, rsem_a, here).wait_recv()
        rs_b[myq] = gs_ref[0:_W_SP_ROW, :] + rs_a[0:_W_SP_ROW, :]
        rs_w[myq] = (gs_ref[_W_SP_ROW:rs, :] + rs_a[_W_SP_ROW:rs, :]).astype(BF16)
        small_b = []
        for j, chip in enumerate(chips):
            to = (chip[0], chip[1], cc)
            small_b.append(_remote(rs_b.at[myq], rs_b.at[myq], ssem_b.at[0, j], rsem_b.at[0, j], to))
            small_b.append(_remote(rs_w.at[myq], rs_w.at[myq], ssem_b.at[1, j], rsem_b.at[1, j], to))
        for cp in small_b:
            cp.start()
        late_last = reducers[1:] + reducers[:1]
        for rd in late_last:
            rd.mid()
        for rd in late_last:
            rd.pass_on()

        for j in range(3):
            _remote(rs_b.at[myq], rs_b.at[myq], ssem_b.at[0, j], rsem_b.at[0, j], here).wait_recv()
            _remote(rs_w.at[myq], rs_w.at[myq], ssem_b.at[1, j], rsem_b.at[1, j], here).wait_recv()
        ogs[0:_W_SP_ROW, :] = ((rs_b[0] + rs_b[1]) + rs_b[2]) + rs_b[3]

        def tot_w(r):
            w = [rs_w[q, r, :].astype(F32) for q in range(4)]
            ogs[pl.ds(pl.multiple_of(_W_SP_ROW + r.start, 8), _ROWS), :] = ((w[0] + w[1]) + w[2]) + w[3]

        _rows_loop(rs - _W_SP_ROW, tot_w)
        for rd, out in zip(late_last, (ob, oa, oc)):
            rd.finish(out)
        small_a.wait_send()
        for cp in small_b:
            cp.wait_send()

    vm = pl.BlockSpec(memory_space=pltpu.VMEM)
    anyspec = pl.BlockSpec(memory_space=pl.ANY)
    scratch = []
    for shp in shapes:
        scratch += _reduce_scratch(shp)
    scratch += [pltpu.VMEM((rs, CHUNK), F32), pltpu.VMEM((rs, CHUNK), F32),
                pltpu.VMEM((4, _W_SP_ROW, CHUNK), F32), pltpu.VMEM((4, rs - _W_SP_ROW, CHUNK), BF16),
                pltpu.VMEM((N_DEV,) + shp_c, F32), pltpu.VMEM((1, D_MODEL), F32),
                pltpu.SemaphoreType.DMA, pltpu.SemaphoreType.DMA,
                pltpu.SemaphoreType.DMA((2, 3)), pltpu.SemaphoreType.DMA((2, 3))]
    tc, tb, ta, ts = pl.pallas_call(
        body, name="greduce",
        out_shape=tuple([jax.ShapeDtypeStruct(shp, F32) for shp in shapes] + [jax.ShapeDtypeStruct((rs, CHUNK), F32)]),
        in_specs=[anyspec] * 2 + [vm] * (4 + _N_SMALL),
        out_specs=(vm, vm, vm, vm),
        scratch_shapes=scratch,
        compiler_params=_params(),
    )(gb, ga, dmkv, mem2, gm, w_mkv, *small_g, loss_p)
    return ta, tb, tc, ts


def _adamw(w, g, m, v):
    m = ADAM_B1 * m + (1.0 - ADAM_B1) * g
    v = ADAM_B2 * v + (1.0 - ADAM_B2) * (g * g)
    m_hat = m / (1.0 - ADAM_B1 ** ADAM_STEP)
    v_hat = v / (1.0 - ADAM_B2 ** ADAM_STEP)
    delta = -ADAM_LR * (m_hat / (jnp.sqrt(v_hat) + ADAM_EPS) + ADAM_WD * w)
    return delta, m, v


def _update(ta, tb, tc, ts, big_wmv, small_wmv):
    shapes = (ta.shape, tb.shape, tc.shape)
    rs = _S_ROWS
    small_shapes = [tuple(a.shape) for a in small_wmv[0]]

    def body(*refs):
        it = iter(refs)
        take = lambda n: [next(it) for _ in range(n)]
        ga_ref, gb_ref, gc_ref, gs_ref = take(4)
        wa, ma, va, wb, mb, vb_, wc, mc, vc = take(9)
        sw_refs, sm_refs, sv_refs = take(_N_SMALL), take(_N_SMALL), take(_N_SMALL)
        oga, oda, oma, ova, ogb, odb, omb, ovb, ogc, odc, omc, ovc = take(12)
        so_refs = [take(_N_SMALL) for _ in range(4)]
        loss_out, = take(1)
        ws, ms, vs, ods, oms, ovs = take(6)

        for buf in (ws, ms, vs):
            buf[...] = jnp.zeros_like(buf)
        _pack_rows(ws, sw_refs)
        _pack_rows(ms, sm_refs)
        _pack_rows(vs, sv_refs)

        big = ((ga_ref, wa, ma, va, oga, oda, oma, ova), (gb_ref, wb, mb, vb_, ogb, odb, omb, ovb),
               (gc_ref, wc, mc, vc, ogc, odc, omc, ovc))
        for arr in range(3):
            g_r, w_r, m_r, v_r, og, od, om, ov = big[arr]

            def upd(r, g_r=g_r, w_r=w_r, m_r=m_r, v_r=v_r, og=og, od=od, om=om, ov=ov):
                g = g_r[r, :]
                d, m, v = _adamw(w_r[r, :], g, m_r[r, :], v_r[r, :])
                og[r, :] = g
                od[r, :] = d
                om[r, :] = m
                ov[r, :] = v

            _rows_loop(shapes[arr][0], upd)

        def upd_s(i, _):
            r = pl.ds(pl.multiple_of(i * 8, 8), 8)
            d, m, v = _adamw(ws[r, :], gs_ref[r, :], ms[r, :], vs[r, :])
            ods[r, :] = d
            oms[r, :] = m
            ovs[r, :] = v
            return 0

        lax.fori_loop(0, rs // 8, upd_s, 0)
        for k, buf in enumerate((gs_ref, ods, oms, ovs)):
            _unpack_rows(buf, so_refs[k])
        loss_out[...] = gs_ref[_LOSS_ROW:_LOSS_ROW + 1, 0:1]

    vm = pl.BlockSpec(memory_space=pltpu.VMEM)
    big_out = []
    for shp in shapes:
        big_out += [jax.ShapeDtypeStruct(shp, F32)] * 4
    small_out = [jax.ShapeDtypeStruct(shp[::-1] if shp == (N_BUCKETS, 4) else shp, F32) for shp in small_shapes] * 4
    out_shape = tuple(big_out + small_out + [jax.ShapeDtypeStruct((1, 1), F32)])
    n_in = 4 + 9 + 3 * _N_SMALL
    return pl.pallas_call(
        body, name="update",
        out_shape=out_shape,
        in_specs=[vm] * n_in,
        out_specs=tuple([vm] * len(out_shape)),
        scratch_shapes=[pltpu.VMEM((rs, CHUNK), F32) for _ in range(6)],
        compiler_params=_params(),
    )(ta, tb, tc, ts, *big_wmv, *small_wmv[0], *small_wmv[1], *small_wmv[2])


def kernel(x, mem, pre_norm_g, post_norm_g, mem_norm_g, w_in, w_mem_kv, v_norm_g, v_norm_b, w_spatial, b_spatial, attn_sinks, rel_bias, w_out, loss_target, m_pre_norm_g, m_post_norm_g, m_mem_norm_g, m_w_in, m_w_mem_kv, m_v_norm_g, m_v_norm_b, m_w_spatial, m_b_spatial, m_attn_sinks, m_rel_bias, m_w_out, v_pre_norm_g, v_post_norm_g, v_mem_norm_g, v_w_in, v_w_mem_kv, v_v_norm_g, v_v_norm_b, v_w_spatial, v_b_spatial, v_attn_sinks, v_rel_bias, v_w_out):
    sh_a = (w_in[0].T, m_w_in[0].T, v_w_in[0].T)
    sh_b = (w_out[0], m_w_out[0], v_w_out[0])
    sh_c = (w_mem_kv[0], m_w_mem_kv[0], v_w_mem_kv[0])
    nb, s, _ = x.shape
    t = nb * s
    x2 = x.reshape(t, D_MODEL)
    tgt2 = loss_target.reshape(t, D_MODEL)
    mem2 = mem.reshape(nb * MEM_LEN, D_MODEL)
    buckets = jnp.asarray(_t5_buckets())

    wa, wb, wc, bias, wt, wtt, bcol, mkv = _wgather(sh_a[0], sh_b[0], sh_c[0], rel_bias, w_spatial[0], b_spatial[0],
                                                    buckets, mem2, mem_norm_g)
    w_mkv = wc.reshape(D_MODEL, 2 * MEM_LEN)
    gx, dmkv, dwi, dwo, dg1, dg2, loss_p, dwsp, dbs, dvg, dvb, dsink, drel = _layer(
        x2, tgt2, mkv.reshape(nb, MEM_LEN, 2 * MEM_LEN), bias, attn_sinks.reshape(4), v_norm_g, v_norm_b, wt, wtt, bcol,
        pre_norm_g, post_norm_g, wa.reshape(IN_WIDTH, D_MODEL), wb.reshape(D_MODEL, D_MODEL), buckets,
        nb, s, min(256, s))
    gx = gx.reshape(nb, s, D_MODEL)
    small_grads = [dg1, dg2, dvg, dvb, dbs, dsink, drel, dwsp.reshape(A_GROUPS * CHUNK, CHUNK)]

    small_names = ["pre_norm_g", "post_norm_g", "mem_norm_g", "v_norm_g", "v_norm_b", "b_spatial", "attn_sinks",
                   "rel_bias", "w_spatial"]
    given = dict(pre_norm_g=(pre_norm_g, m_pre_norm_g, v_pre_norm_g), post_norm_g=(post_norm_g, m_post_norm_g, v_post_norm_g),
                 mem_norm_g=(mem_norm_g, m_mem_norm_g, v_mem_norm_g), v_norm_g=(v_norm_g, m_v_norm_g, v_v_norm_g),
                 v_norm_b=(v_norm_b, m_v_norm_b, v_v_norm_b), b_spatial=(b_spatial, m_b_spatial, v_b_spatial),
                 attn_sinks=(attn_sinks, m_attn_sinks, v_attn_sinks), rel_bias=(rel_bias, m_rel_bias, v_rel_bias),
                 w_spatial=(w_spatial, m_w_spatial, v_w_spatial))
    small_wmv = [[given[n][k].reshape(shp) for n, (shp, _) in zip(small_names, _S_LAYOUT)] for k in range(3)]

    ta, tb, tc, ts = _greduce(dwi.reshape(N_DEV, SHARD_IN, D_MODEL), dwo.reshape(N_DEV, SHARD_O, D_MODEL),
                              dmkv.reshape(nb * MEM_LEN, 2 * MEM_LEN), mem2, mem_norm_g, w_mkv, small_grads, loss_p)
    outs = _update(ta, tb, tc, ts, (*sh_a, *sh_b, *sh_c), small_wmv)
    ra, rb, rc = outs[0:4], outs[4:8], outs[8:12]
    loss = outs[12 + 4 * _N_SMALL].reshape(())

    res = {}
    for k, kind in enumerate(("grad", "delta", "new_m", "new_v")):
        res[kind, "w_in"] = ra[k].T[None]
        res[kind, "w_out"] = rb[k][None]
        res[kind, "w_mem_kv"] = rc[k][None]
        for i, n in enumerate(small_names):
            o = outs[12 + k * _N_SMALL + i]
            res[kind, n] = o.T if n == "rel_bias" else o.reshape(given[n][0].shape)
    order = ["pre_norm_g", "post_norm_g", "mem_norm_g", "w_in", "w_mem_kv", "v_norm_g", "v_norm_b", "w_spatial",
             "b_spatial", "attn_sinks", "rel_bias", "w_out"]
    flat = [res[kind, n] for kind in ("grad", "delta", "new_m", "new_v") for n in order]
    return (loss, gx, *flat)
```

```python
import numpy as np
import jax
import jax.numpy as jnp
from jax import lax
from jax.experimental import pallas as pl
from jax.experimental.pallas import tpu as pltpu

F32 = jnp.float32
BF16 = jnp.bfloat16
MM = jnp.bfloat16

D_MODEL = 1024
CHUNK = 128
A_GROUPS = 4
A_WIDTH = 512
UV_W = 1024
QKV_W = 768
Z_W = 1024
IN_WIDTH = UV_W + QKV_W + Z_W
MEM_LEN = 256
N_BUCKETS = 32
MAX_DISTANCE = 128
EPS = 1e-6
NEG = -1e30
SCALE = 0.125
N_DEV = 8
SHARD_IN = IN_WIDTH // N_DEV
SHARD_O = D_MODEL // N_DEV

SQ_COL, SK_COL, SV_COL, MQ_COL, Z_COL = UV_W, UV_W + 256, UV_W + 384, UV_W + 512, UV_W + QKV_W
YB_OFF, YC_OFF = 512, 768

ADAM_LR = 0.001
ADAM_B1 = 0.9
ADAM_B2 = 0.999
ADAM_EPS = 1e-08
ADAM_WD = 0.01
ADAM_STEP = 10

VMEM_LIMIT = 60 * 1024 * 1024

_GELU_C = 0.7978845608028654
_GELU_A = 0.044715

MESH = pl.DeviceIdType.MESH
_ROWS = 32


def _dot(a, b):
    return lax.dot_general(a, b, (((1,), (0,)), ((), ())), preferred_element_type=F32)


def _dot_nt(a, b):
    return lax.dot_general(a, b, (((1,), (1,)), ((), ())), preferred_element_type=F32)


def _dot_tn(a, b):
    return lax.dot_general(a, b, (((0,), (0,)), ((), ())), preferred_element_type=F32)


def _gelu_and_grad(x):
    x2 = x * x
    t = jnp.tanh(_GELU_C * (x + _GELU_A * x * x2))
    g = 0.5 * x * (1.0 + t)
    dg = 0.5 * (1.0 + t) + 0.5 * x * (1.0 - t * t) * (_GELU_C * (1.0 + 3.0 * _GELU_A * x2))
    return g, dg


def _t5_buckets():
    qi = np.arange(CHUNK)[:, None]
    kj = np.arange(2 * CHUNK)[None, :]
    n = np.maximum(qi + CHUNK - kj, 0)
    max_exact = N_BUCKETS // 2
    large = max_exact + (np.log(np.maximum(n, 1) / max_exact) / np.log(MAX_DISTANCE / max_exact)
                         * (N_BUCKETS - max_exact)).astype(np.int32)
    large = np.minimum(large, N_BUCKETS - 1)
    return np.where(n < max_exact, n, large).astype(np.int32)


def _params(**kw):
    return pltpu.CompilerParams(vmem_limit_bytes=VMEM_LIMIT, **kw)


def _full(shape, single=False):
    nd = len(shape)
    if single:
        return pl.BlockSpec(shape, lambda *_: (0,) * nd, pipeline_mode=pl.Buffered(1))
    return pl.BlockSpec(shape, lambda *_: (0,) * nd)


def _window_valid():
    qi = lax.broadcasted_iota(jnp.int32, (CHUNK, 2 * CHUNK), 0)
    kj = lax.broadcasted_iota(jnp.int32, (CHUNK, 2 * CHUNK), 1)
    dist = qi + CHUNK - kj
    return (dist >= 0) & (dist < CHUNK)


def _position():
    return lax.axis_index("x"), lax.axis_index("y"), lax.axis_index("c")


def _other_chips(x, y):
    return [(1 - x, y), (x, 1 - y), (1 - x, 1 - y)]


def _route(x, y, c):
    first = (x * c + (1 - x) * (1 - c), y * (1 - c) + (1 - y) * c)
    second = (x * (1 - c) + (1 - x) * c, y * c + (1 - y) * (1 - c))
    return first, second, (1 - x, 1 - y)


def _remote(src, dst, ssem, rsem, to):
    return pltpu.make_async_remote_copy(src_ref=src, dst_ref=dst, send_sem=ssem, recv_sem=rsem,
                                        device_id=to, device_id_type=MESH)


def _rows_loop(nrow, fn):
    def step(i, _):
        fn(pl.ds(pl.multiple_of(i * _ROWS, _ROWS), _ROWS))
        return 0

    lax.fori_loop(0, nrow // _ROWS, step, 0)


class _Gather:
    def __init__(self, pos, out, ssem, rsem):
        self.x, self.y, self.c = pos
        self.out, self.ssem, self.rsem = out, ssem, rsem
        self.me = 4 * self.x + 2 * self.y + self.c
        self.here = (self.x, self.y, self.c)
        self.sib = (self.x, self.y, 1 - self.c)
        self.first, self.second, self.far = _route(*pos)

    def _copy(self, k, blk, to):
        r = self.out.at[blk]
        return _remote(r, r, self.ssem.at[k], self.rsem.at[k], to)

    def _idx(self, chip, core):
        return 4 * chip[0] + 2 * chip[1] + core

    def _on(self, chip):
        return (chip[0], chip[1], self.c)

    def start(self):
        self._copy(0, self.me, self.sib).start()
        self._copy(1, self.me, self._on(self.first)).start()
        self._copy(2, self.me, self._on(self.second)).start()

    def forward(self):
        c = self.c
        self._copy(1, self._idx(self.first, c), self.here).wait_recv()
        self._copy(3, self._idx(self.first, c), self._on(self.second)).start()
        self._copy(4, self._idx(self.first, c), self.sib).start()
        self._copy(2, self._idx(self.second, c), self.here).wait_recv()
        self._copy(5, self._idx(self.second, c), self.sib).start()
        self._copy(3, self._idx(self.far, c), self.here).wait_recv()
        self._copy(6, self._idx(self.far, c), self.sib).start()

    def finish(self):
        c = self.c
        self._copy(0, self._idx((self.x, self.y), 1 - c), self.here).wait_recv()
        for k, chip in ((4, self.second), (5, self.first), (6, self.far)):
            self._copy(k, self._idx(chip, 1 - c), self.here).wait_recv()
        self._copy(0, self.me, self.sib).wait_send()
        self._copy(1, self.me, self._on(self.first)).wait_send()
        self._copy(2, self.me, self._on(self.second)).wait_send()
        self._copy(3, self._idx(self.first, c), self._on(self.second)).wait_send()
        for k, chip in ((4, self.first), (5, self.second), (6, self.far)):
            self._copy(k, self._idx(chip, c), self.sib).wait_send()


def _prep_tables(rb_ref, w_ref, b_ref, bk_ref, bias_ref, wt_ref, wtt_ref, bcol_ref):
    valid = _window_valid()
    bk = bk_ref[...]
    acc = [jnp.full((CHUNK, 2 * CHUNK), NEG, F32) for _ in range(4)]
    for b in range(N_BUCKETS):
        hit = (bk == b) & valid
        for h in range(4):
            acc[h] = jnp.where(hit, rb_ref[b, h], acc[h])
    for h in range(4):
        bias_ref[h] = acc[h]
    r = lax.broadcasted_iota(jnp.int32, (CHUNK, CHUNK), 0)
    c = lax.broadcasted_iota(jnp.int32, (CHUNK, CHUNK), 1)
    for g in range(A_GROUPS):
        w = jnp.where(r >= c, w_ref[g], 0.0)
        wt_ref[g] = w.astype(MM)
        wtt_ref[g] = w.T.astype(MM)
        bcol_ref[g] = jnp.broadcast_to(b_ref[g:g + 1, :], (CHUNK, CHUNK)).T


def _wgather(a, b, c, rel_bias, w_sp, b_sp, buckets, mem2, gm):
    tmem = mem2.shape[0]

    def body(a_ref, b_ref, c_ref, rb_ref, w_ref, bsp_ref, bk_ref, m_ref, gm_ref,
             oa, ob, oc, bias_ref, wt_ref, wtt_ref, bcol_ref, mkv_ref, ssem, rsem):
        pos = _position()
        me = 4 * pos[0] + 2 * pos[1] + pos[2]
        gathers = []
        for k, (src, out) in enumerate(((a_ref, oa), (b_ref, ob), (c_ref, oc))):
            out[me] = src[...].astype(BF16)
            g = _Gather(pos, out, ssem.at[k], rsem.at[k])
            g.start()
            gathers.append(g)
        _prep_tables(rb_ref, w_ref, bsp_ref, bk_ref, bias_ref, wt_ref, wtt_ref, bcol_ref)
        for g in gathers:
            g.forward()
        gathers[2].finish()
        xf = m_ref[...]
        hm = (xf * _rms(xf) * gm_ref[...]).astype(MM)
        acc = jnp.zeros((tmem, 2 * MEM_LEN), F32)
        for d in range(N_DEV):
            acc = acc + _dot(hm[:, d * SHARD_O:(d + 1) * SHARD_O], oc[d])
        mkv_ref[...] = acc.astype(MM)
        for g in gathers[:2]:
            g.finish()

    vm = pl.BlockSpec(memory_space=pltpu.VMEM)
    grp = (A_GROUPS, CHUNK, CHUNK)
    return pl.pallas_call(
        body, name="wgather",
        out_shape=(jax.ShapeDtypeStruct((N_DEV,) + a.shape, BF16),
                   jax.ShapeDtypeStruct((N_DEV,) + b.shape, BF16),
                   jax.ShapeDtypeStruct((N_DEV,) + c.shape, BF16),
                   jax.ShapeDtypeStruct((4, CHUNK, 2 * CHUNK), F32),
                   jax.ShapeDtypeStruct(grp, MM), jax.ShapeDtypeStruct(grp, MM), jax.ShapeDtypeStruct(grp, F32),
                   jax.ShapeDtypeStruct((tmem, 2 * MEM_LEN), MM)),
        in_specs=[vm, vm, vm, pl.BlockSpec(memory_space=pltpu.SMEM), vm, vm, vm, vm, vm],
        out_specs=tuple([vm] * 8),
        scratch_shapes=[pltpu.SemaphoreType.DMA((3, 7)), pltpu.SemaphoreType.DMA((3, 7))],
        compiler_params=_params(),
    )(a, b, c, rel_bias, w_sp, b_sp, buckets, mem2, gm)


def _half_masks(rows):
    lane = lax.broadcasted_iota(jnp.int32, (rows, CHUNK), 1)
    return lane < 64


def _dup_heads(band):
    b32 = band.astype(F32)
    rolled = pltpu.roll(b32, 64, 1)
    lo = _half_masks(band.shape[0])
    return (jnp.where(lo, b32, rolled).astype(MM), jnp.where(lo, rolled, b32).astype(MM))


def _swa_probs(qsel, kd, bias_h, sink_h, first_add):
    s = _dot_nt(qsel, kd) * SCALE + bias_h + first_add
    m = jnp.maximum(jnp.max(s, axis=-1, keepdims=True), sink_h)
    p = jnp.exp(s - m)
    es = jnp.exp(sink_h - m)
    inv = 1.0 / (jnp.sum(p, axis=-1, keepdims=True) + es)
    return p * inv, es * inv


def _softmax(s):
    m = jnp.max(s, axis=-1, keepdims=True)
    p = jnp.exp(s - m)
    return p * (1.0 / jnp.sum(p, axis=-1, keepdims=True))


def _first_block_mask(n):
    col = lax.broadcasted_iota(jnp.int32, (CHUNK, 2 * CHUNK), 1)
    return jnp.where((col < CHUNK) & (n == 0), NEG, 0.0)


def _rms(xf):
    return lax.rsqrt(jnp.mean(xf * xf, axis=-1, keepdims=True) + EPS)


def _layer(x2, tgt2, mkv3, bias, sinks, vg, vb, wt, wtt, bcol, g1, g2, w_in_t, w_o, buckets, nb, s, tm):
    nt = s // tm
    bpt = tm // CHUNK
    bps = s // CHUNK
    t = nb * s

    def body(x_ref, xp_ref, t_ref, mkv_ref, bias_ref, sink_ref, vg_ref, vb_ref, wt_ref, wtt_ref, bcol_ref,
             g1_ref, g2_ref, wi_ref, wo_ref, bk_ref,
             gx_ref, dmkv_ref, dwi_hbm, dwo_hbm, dg1_ref, dg2_ref, loss_ref, dwsp_ref, dbs_ref,
             dvg_ref, dvb_ref, dsink_ref, drel_ref,
             acc_i, acc_o, uv_s, z_s, q_s, kv_s, h_s, dp_s, dxo_s,
             ycat, dyc, u_s, gu_s, gv_s, xh_s, vc_s, pb_s, ps_s, pc_s, kd_s, vd_s,
             dkv_acc, dbias_acc, dsv_acc, dsink_acc, sems):
        b, j = pl.program_id(0), pl.program_id(1)
        jt = nt - 1 - j

        @pl.when((b == 0) & (j == 0))
        def _():
            for ref in (acc_i, acc_o, dg1_ref, dg2_ref, loss_ref, dwsp_ref, dvg_ref, dvb_ref,
                        dbias_acc, dsv_acc, dsink_acc):
                ref[...] = jnp.zeros_like(ref)

        @pl.when(j == 0)
        def _():
            dmkv_ref[...] = jnp.zeros_like(dmkv_ref)
            dkv_acc[...] = jnp.zeros_like(dkv_acc)

        carry = dkv_acc[0:CHUNK, :]
        dkv_acc[...] = jnp.zeros_like(dkv_acc)
        dkv_acc[tm:tm + CHUNK, :] = carry

        lo = _half_masks(CHUNK)
        lob = _half_masks(2 * CHUNK)
        lot = _half_masks(tm)
        g1v = g1_ref[...]

        xf = x_ref[...]
        h = (xf * _rms(xf) * g1v).astype(MM)
        h_s[...] = h
        uv_s[...] = _dot_nt(h, wi_ref[0:UV_W, :])
        qkv = _dot_nt(h, wi_ref[SQ_COL:Z_COL, :])
        q_s[:, 0:256] = qkv[:, 0:256].astype(MM)
        q_s[:, 256:512] = qkv[:, 512:768].astype(MM)
        kv_s[CHUNK:CHUNK + tm, :] = qkv[:, 256:512].astype(MM)
        z_s[...] = _dot_nt(h, wi_ref[Z_COL:IN_WIDTH, :])
        xp = xp_ref[...]
        hp = (xp * _rms(xp) * g1v).astype(MM)
        kv_s[0:CHUNK, :] = _dot_nt(hp, wi_ref[SK_COL:MQ_COL, :]).astype(MM)

        for blk in range(bpt):
            r0 = blk * CHUNK
            rows = slice(r0, r0 + CHUNK)
            n = jt * bpt + blk
            for g in range(A_GROUPS):
                cg = slice(g * CHUNK, (g + 1) * CHUNK)
                u, gu = _gelu_and_grad(uv_s[rows, cg])
                v, gv = _gelu_and_grad(uv_s[rows, A_WIDTH + g * CHUNK:A_WIDTH + (g + 1) * CHUNK])
                mu = jnp.mean(v, axis=-1, keepdims=True)
                xc = v - mu
                rstd = lax.rsqrt(jnp.mean(xc * xc, axis=-1, keepdims=True) + EPS)
                xhat = xc * rstd
                vc = (xhat * vg_ref[:, cg] + vb_ref[:, cg]).astype(MM)
                sv = _dot(wt_ref[g], vc) + bcol_ref[g]
                u_s[rows, cg] = u
                gu_s[rows, cg] = sv * gu
                gv_s[rows, cg] = rstd * gv
                xh_s[rows, cg] = xhat
                vc_s[rows, cg] = vc
                ycat[rows, cg] = u * sv
            kd = _dup_heads(kv_s[r0:r0 + 2 * CHUNK, 0:CHUNK])
            vd = _dup_heads(kv_s[r0:r0 + 2 * CHUNK, CHUNK:2 * CHUNK])
            first_add = _first_block_mask(n)
            for kvh in range(2):
                kd_s[blk * 2 + kvh] = kd[kvh]
                vd_s[blk * 2 + kvh] = vd[kvh]
                q128 = q_s[rows, kvh * CHUNK:(kvh + 1) * CHUNK].astype(F32)
                outs = []
                for gi in range(2):
                    hd = 2 * kvh + gi
                    qsel = jnp.where(lo if gi == 0 else ~lo, q128, 0.0).astype(MM)
                    probs, ps = _swa_probs(qsel, kd[kvh], bias_ref[hd], sink_ref[hd], first_add)
                    pb_s[blk * 4 + hd] = probs
                    ps_s[blk * 4 + hd] = jnp.broadcast_to(ps, (CHUNK, CHUNK))
                    outs.append(_dot(probs.astype(MM), vd[kvh]))
                ycat[rows, YB_OFF + kvh * CHUNK:YB_OFF + (kvh + 1) * CHUNK] = jnp.where(lo, outs[0], outs[1])
        for g in range(2):
            q128 = q_s[:, 256 + g * CHUNK:256 + (g + 1) * CHUNK].astype(F32)
            k128 = mkv_ref[:, g * CHUNK:(g + 1) * CHUNK]
            v128 = mkv_ref[:, MEM_LEN + g * CHUNK:MEM_LEN + (g + 1) * CHUNK]
            outs = []
            for hh in range(2):
                qsel = jnp.where(lot if hh == 0 else ~lot, q128, 0.0).astype(MM)
                probs = _softmax(_dot_nt(qsel, k128) * SCALE)
                pc_s[2 * g + hh] = probs
                outs.append(_dot(probs.astype(MM), v128))
            ycat[:, YC_OFF + g * CHUNK:YC_OFF + (g + 1) * CHUNK] = jnp.where(lot, outs[0], outs[1])

        zt = z_s[...]
        sig = 1.0 / (1.0 + jnp.exp(-zt))
        silu = zt * sig
        yc = ycat[...]
        yb = (yc * silu).astype(MM)
        o = _dot(yb, wo_ref[...])
        r2 = _rms(o)
        nrm = o * r2
        g2v = g2_ref[...]
        e = x_ref[...] + nrm * g2v - t_ref[...]
        l1 = jnp.sum(e * e, axis=-1, keepdims=True)
        loss_ref[...] += jnp.broadcast_to(jnp.sum(l1, axis=0, keepdims=True) * (0.5 / D_MODEL), loss_ref.shape)
        dxo = e * (1.0 / D_MODEL)
        dxo_s[...] = dxo
        dg2_ref[...] += jnp.sum(dxo * nrm, axis=0, keepdims=True)
        dn = dxo * g2v
        do = r2 * (dn - nrm * jnp.mean(dn * nrm, axis=-1, keepdims=True))
        dob = do.astype(MM)
        dy = _dot_nt(dob, wo_ref[...])
        dp_s[:, Z_COL:IN_WIDTH] = (dy * yc * (sig * (1.0 + zt * (1.0 - sig)))).astype(MM)
        dyc[...] = dy * silu
        acc_o[...] += _dot_tn(yb, dob)

        for blk in range(bpt):
            r0 = blk * CHUNK
            rows = slice(r0, r0 + CHUNK)
            for g in range(A_GROUPS):
                cg = slice(g * CHUNK, (g + 1) * CHUNK)
                cv = slice(A_WIDTH + g * CHUNK, A_WIDTH + (g + 1) * CHUNK)
                dya = dyc[rows, cg]
                dp_s[rows, cg] = (dya * gu_s[rows, cg]).astype(MM)
                dsv = dya * u_s[rows, cg]
                dsvb = dsv.astype(MM)
                dsv_acc[g] += dsv
                dwsp_ref[g] += _dot_nt(dsvb, vc_s[rows, cg])
                dvc = _dot(wtt_ref[g], dsvb)
                xhat = xh_s[rows, cg]
                dvg_ref[:, cg] += jnp.sum(dvc * xhat, axis=0, keepdims=True)
                dvb_ref[:, cg] += jnp.sum(dvc, axis=0, keepdims=True)
                dxh = dvc * vg_ref[:, cg]
                dv = (dxh - jnp.mean(dxh, axis=-1, keepdims=True)
                      - xhat * jnp.mean(dxh * xhat, axis=-1, keepdims=True))
                dp_s[rows, cv] = (dv * gv_s[rows, cg]).astype(MM)
            dk_f, dv_f = [], []
            for kvh in range(2):
                kd = kd_s[blk * 2 + kvh]
                vd = vd_s[blk * 2 + kvh]
                q128 = q_s[rows, kvh * CHUNK:(kvh + 1) * CHUNK].astype(F32)
                do128 = dyc[rows, YB_OFF + kvh * CHUNK:YB_OFF + (kvh + 1) * CHUNK]
                dq128 = jnp.zeros((CHUNK, CHUNK), F32)
                dkd = jnp.zeros((2 * CHUNK, CHUNK), F32)
                dvd = jnp.zeros((2 * CHUNK, CHUNK), F32)
                for gi in range(2):
                    hd = 2 * kvh + gi
                    half = lo if gi == 0 else ~lo
                    qsel = jnp.where(half, q128, 0.0).astype(MM)
                    dosel = jnp.where(half, do128, 0.0).astype(MM)
                    probs = pb_s[blk * 4 + hd]
                    ps = ps_s[blk * 4 + hd][:, 0:1]
                    dp = _dot_nt(dosel, vd)
                    delta = jnp.sum(probs * dp, axis=-1, keepdims=True)
                    ds = probs * (dp - delta)
                    dbias_acc[hd] += ds
                    dsink_acc[hd:hd + 1, :] += jnp.broadcast_to(-jnp.sum(ps * delta, axis=0, keepdims=True), (1, CHUNK))
                    dss = (ds * SCALE).astype(MM)
                    dq128 = dq128 + jnp.where(half, _dot(dss, kd), 0.0)
                    dkd = dkd + _dot_tn(dss, qsel)
                    dvd = dvd + _dot_tn(probs.astype(MM), dosel)
                dp_s[rows, SQ_COL + kvh * CHUNK:SQ_COL + (kvh + 1) * CHUNK] = dq128.astype(MM)
                dk_f.append(dkd + pltpu.roll(dkd, 64, 1))
                dv_f.append(dvd + pltpu.roll(dvd, 64, 1))
            dkv_acc[r0:r0 + 2 * CHUNK, 0:CHUNK] += jnp.where(lob, dk_f[0], dk_f[1])
            dkv_acc[r0:r0 + 2 * CHUNK, CHUNK:2 * CHUNK] += jnp.where(lob, dv_f[0], dv_f[1])
        dp_s[:, SK_COL:MQ_COL] = dkv_acc[CHUNK:CHUNK + tm, :].astype(MM)
        for g in range(2):
            q128 = q_s[:, 256 + g * CHUNK:256 + (g + 1) * CHUNK].astype(F32)
            k128 = mkv_ref[:, g * CHUNK:(g + 1) * CHUNK]
            v128 = mkv_ref[:, MEM_LEN + g * CHUNK:MEM_LEN + (g + 1) * CHUNK]
            do128 = dyc[:, YC_OFF + g * CHUNK:YC_OFF + (g + 1) * CHUNK]
            dq128 = jnp.zeros((tm, CHUNK), F32)
            dk128 = jnp.zeros((MEM_LEN, CHUNK), F32)
            dv128 = jnp.zeros((MEM_LEN, CHUNK), F32)
            for hh in range(2):
                half = lot if hh == 0 else ~lot
                qsel = jnp.where(half, q128, 0.0).astype(MM)
                dosel = jnp.where(half, do128, 0.0).astype(MM)
                probs = pc_s[2 * g + hh]
                dp = _dot_nt(dosel, v128)
                ds = probs * (dp - jnp.sum(probs * dp, axis=-1, keepdims=True))
                dss = (ds * SCALE).astype(MM)
                dq128 = dq128 + jnp.where(half, _dot(dss, k128), 0.0)
                dk128 = dk128 + _dot_tn(dss, qsel)
                dv128 = dv128 + _dot_tn(probs.astype(MM), dosel)
            dp_s[:, MQ_COL + g * CHUNK:MQ_COL + (g + 1) * CHUNK] = dq128.astype(MM)
            dmkv_ref[:, g * CHUNK:(g + 1) * CHUNK] += dk128
            dmkv_ref[:, MEM_LEN + g * CHUNK:MEM_LEN + (g + 1) * CHUNK] += dv128

        hv = h_s[...]
        dh = jnp.zeros((tm, D_MODEL), F32)
        for c0, c1 in ((0, UV_W), (SQ_COL, Z_COL), (Z_COL, IN_WIDTH)):
            dpt = dp_s[:, c0:c1]
            acc_i[c0:c1, :] += _dot_tn(dpt, hv)
            dh = dh + _dot(dpt, wi_ref[c0:c1, :])
        xf = x_ref[...]
        r = _rms(xf)
        nx = xf * r
        dg1_ref[...] += jnp.sum(dh * nx, axis=0, keepdims=True)
        dnx = dh * g1v
        gx_ref[...] = dxo_s[...] + r * (dnx - nx * jnp.mean(dnx * nx, axis=-1, keepdims=True))

        @pl.when((b == nb - 1) & (j == nt - 1))
        def _():
            out_i = pltpu.make_async_copy(acc_i, dwi_hbm, sems.at[0])
            out_o = pltpu.make_async_copy(acc_o, dwo_hbm, sems.at[1])
            out_i.start()
            out_o.start()
            r_ = lax.broadcasted_iota(jnp.int32, (CHUNK, CHUNK), 0)
            c_ = lax.broadcasted_iota(jnp.int32, (CHUNK, CHUNK), 1)
            for g in range(A_GROUPS):
                dwsp_ref[g] = jnp.where(r_ >= c_, dwsp_ref[g], 0.0)
                dbs_ref[g:g + 1, :] = jnp.sum(dsv_acc[g].T, axis=0, keepdims=True)
            rows8 = lax.broadcasted_iota(jnp.int32, (8, CHUNK), 0)
            cols8 = lax.broadcasted_iota(jnp.int32, (8, CHUNK), 1)
            sk = jnp.zeros((8, CHUNK), F32)
            for hd in range(4):
                sk = sk + jnp.where((rows8 == 0) & (cols8 == hd),
                                    jnp.broadcast_to(dsink_acc[hd:hd + 1, :], (8, CHUNK)), 0.0)
            dsink_ref[...] = sk
            bk = bk_ref[...]
            valid = _window_valid()
            rrow = lax.broadcasted_iota(jnp.int32, (N_BUCKETS, CHUNK), 0)
            rcol = lax.broadcasted_iota(jnp.int32, (N_BUCKETS, CHUNK), 1)
            acc = jnp.zeros((N_BUCKETS, CHUNK), F32)
            for bb in range(N_BUCKETS):
                hit = (bk == bb) & valid
                for hd in range(4):
                    part = jnp.sum(jnp.where(hit, dbias_acc[hd], 0.0), axis=-1, keepdims=True)
                    tot = jnp.sum(part, axis=0, keepdims=True)
                    acc = acc + jnp.where((rrow == bb) & (rcol == hd), jnp.broadcast_to(tot, (N_BUCKETS, CHUNK)), 0.0)
            drel_ref[...] = acc
            out_i.wait()
            out_o.wait()

    tile = lambda w: pl.BlockSpec((tm, w), lambda b, j: (b * nt + nt - 1 - j, 0))
    prev_block = pl.BlockSpec((CHUNK, D_MODEL), lambda b, j: (b * bps + jnp.maximum((nt - 1 - j) * bpt - 1, 0), 0))
    per_batch = lambda r, w: pl.BlockSpec((None, r, w), lambda b, j: (b, 0, 0))
    anyspec = pl.BlockSpec(memory_space=pl.ANY)
    grp = (A_GROUPS, CHUNK, CHUNK)
    return pl.pallas_call(
        body, name="layer", grid=(nb, nt),
        out_shape=(jax.ShapeDtypeStruct((t, D_MODEL), F32),
                   jax.ShapeDtypeStruct((nb, MEM_LEN, 2 * MEM_LEN), F32),
                   jax.ShapeDtypeStruct((IN_WIDTH, D_MODEL), F32),
                   jax.ShapeDtypeStruct((D_MODEL, D_MODEL), F32),
                   jax.ShapeDtypeStruct((1, D_MODEL), F32),
                   jax.ShapeDtypeStruct((1, D_MODEL), F32),
                   jax.ShapeDtypeStruct((8, CHUNK), F32),
                   jax.ShapeDtypeStruct(grp, F32),
                   jax.ShapeDtypeStruct((A_GROUPS, CHUNK), F32),
                   jax.ShapeDtypeStruct((1, A_WIDTH), F32),
                   jax.ShapeDtypeStruct((1, A_WIDTH), F32),
                   jax.ShapeDtypeStruct((8, CHUNK), F32),
                   jax.ShapeDtypeStruct((N_BUCKETS, CHUNK), F32)),
        in_specs=[tile(D_MODEL), prev_block, tile(D_MODEL), per_batch(MEM_LEN, 2 * MEM_LEN),
                  _full((4, CHUNK, 2 * CHUNK)),
                  pl.BlockSpec(memory_space=pltpu.SMEM),
                  _full((1, A_WIDTH)), _full((1, A_WIDTH)),
                  _full(grp), _full(grp), _full(grp),
                  _full((1, D_MODEL)), _full((1, D_MODEL)),
                  _full((IN_WIDTH, D_MODEL), single=True), _full((D_MODEL, D_MODEL), single=True),
                  _full((CHUNK, 2 * CHUNK))],
        out_specs=(tile(D_MODEL), per_batch(MEM_LEN, 2 * MEM_LEN), anyspec, anyspec,
                   _full((1, D_MODEL)), _full((1, D_MODEL)), _full((8, CHUNK)),
                   _full(grp), _full((A_GROUPS, CHUNK)), _full((1, A_WIDTH)), _full((1, A_WIDTH)),
                   _full((8, CHUNK)), _full((N_BUCKETS, CHUNK))),
        scratch_shapes=[pltpu.VMEM((IN_WIDTH, D_MODEL), F32), pltpu.VMEM((D_MODEL, D_MODEL), F32),
                        pltpu.VMEM((tm, UV_W), F32), pltpu.VMEM((tm, Z_W), F32),
                        pltpu.VMEM((tm, 512), MM), pltpu.VMEM((tm + CHUNK, 2 * CHUNK), MM),
                        pltpu.VMEM((tm, D_MODEL), MM), pltpu.VMEM((tm, IN_WIDTH), MM),
                        pltpu.VMEM((tm, D_MODEL), F32),
                        pltpu.VMEM((tm, D_MODEL), F32), pltpu.VMEM((tm, D_MODEL), F32)]
                       + [pltpu.VMEM((tm, A_WIDTH), F32) for _ in range(4)]
                       + [pltpu.VMEM((tm, A_WIDTH), MM),
                          pltpu.VMEM((bpt * 4, CHUNK, 2 * CHUNK), F32),
                          pltpu.VMEM((bpt * 4, CHUNK, CHUNK), F32),
                          pltpu.VMEM((4, tm, MEM_LEN), F32),
                          pltpu.VMEM((bpt * 2, 2 * CHUNK, CHUNK), MM),
                          pltpu.VMEM((bpt * 2, 2 * CHUNK, CHUNK), MM),
                          pltpu.VMEM((tm + CHUNK, 2 * CHUNK), F32),
                          pltpu.VMEM((4, CHUNK, 2 * CHUNK), F32),
                          pltpu.VMEM(grp, F32),
                          pltpu.VMEM((8, CHUNK), F32),
                          pltpu.SemaphoreType.DMA((2,))],
        compiler_params=_params(dimension_semantics=("arbitrary", "arbitrary")),
    )(x2, x2, tgt2, mkv3, bias, sinks, vg, vb, wt, wtt, bcol, g1, g2, w_in_t, w_o, buckets)


class _ShardReduce:
    def __init__(self, pos, g, bufs, sems):
        self.x, self.y, self.c = pos
        self.g = g
        self.own, self.rcv, self.sbuf, self.rbuf, self.cbuf = bufs
        self.ld, self.sa, self.ra, self.sb, self.rb = sems
        self.nrow = g.shape[1]
        self.here = (self.x, self.y, self.c)
        self.sib = (self.x, self.y, 1 - self.c)
        self.first, self.second, self.far = _route(*pos)

    def _load(self, q):
        return pltpu.make_async_copy(self.g.at[2 * q + self.c], self.own.at[q], self.ld.at[q])

    def _to_sib(self, q, to):
        return _remote(self.g.at[2 * q + 1 - self.c], self.rcv.at[q], self.sa.at[q], self.ra.at[q], to)

    def _send(self, k, to):
        dst = self.cbuf.at[0] if k == 1 else self.rbuf.at[0 if k == 0 else 1]
        return _remote(self.sbuf.at[k], dst, self.sb.at[k], self.rb.at[k], to)

    def _stage(self, k, which, extra=None):
        def cast(r):
            v = self.rcv[which, r, :]
            if extra is not None:
                v = v + extra[0, r, :].astype(F32)
            self.sbuf[k, r, :] = v.astype(BF16)

        _rows_loop(self.nrow, cast)

    @staticmethod
    def _q(chip):
        return 2 * chip[0] + chip[1]

    def start(self):
        for q in range(4):
            self._load(q).start()
            self._to_sib(q, self.sib).start()

    def mid(self):
        for q in range(4):
            self._load(q).wait()
            self._to_sib(q, self.here).wait_recv()

        def add(r):
            for q in range(4):
                self.rcv[q, r, :] = self.rcv[q, r, :] + self.own[q, r, :]

        _rows_loop(self.nrow, add)
        to_first = (self.first[0], self.first[1], self.c)
        self._stage(0, self._q(self.first))
        self._send(0, to_first).start()
        self._stage(1, self._q(self.far))
        self._send(1, to_first).start()

    def pass_on(self):
        self._send(1, self.here).wait_recv()
        self._stage(2, self._q(self.second), extra=self.cbuf)
        self._send(2, (self.second[0], self.second[1], self.c)).start()

    def finish(self, out):
        self._send(0, self.here).wait_recv()
        self._send(2, self.here).wait_recv()
        which = 2 * self.x + self.y

        def tot(r):
            out[r, :] = (self.rcv[which, r, :] + self.rbuf[0, r, :].astype(F32)) + self.rbuf[1, r, :].astype(F32)

        _rows_loop(self.nrow, tot)
        for q in range(4):
            self._to_sib(q, self.sib).wait_send()
        to_first = (self.first[0], self.first[1], self.c)
        self._send(0, to_first).wait_send()
        self._send(1, to_first).wait_send()
        self._send(2, (self.second[0], self.second[1], self.c)).wait_send()


def _reduce_scratch(shape):
    return [pltpu.VMEM((4,) + shape, F32), pltpu.VMEM((4,) + shape, F32),
            pltpu.VMEM((3,) + shape, BF16), pltpu.VMEM((2,) + shape, BF16), pltpu.VMEM((1,) + shape, BF16),
            pltpu.SemaphoreType.DMA((4,)), pltpu.SemaphoreType.DMA((4,)), pltpu.SemaphoreType.DMA((4,)),
            pltpu.SemaphoreType.DMA((3,)), pltpu.SemaphoreType.DMA((3,))]


_N_RED = 10

_S_LAYOUT = (((1, D_MODEL), 0), ((1, D_MODEL), 8), ((1, D_MODEL), 16),
             ((1, A_WIDTH), 24), ((1, A_WIDTH), 28), ((A_GROUPS, CHUNK), 32),
             ((1, 4), 36), ((N_BUCKETS, 4), 40),
             ((A_GROUPS * CHUNK, CHUNK), 72))
_LOSS_ROW = 37
_W_SP_ROW = _S_LAYOUT[-1][1]
_S_ROWS = _W_SP_ROW + A_GROUPS * CHUNK
_N_SMALL = len(_S_LAYOUT)


def _pack_rows(dst, refs):
    for (shp, r0), ref in zip(_S_LAYOUT, refs):
        if shp[0] == 1 and shp[1] >= CHUNK:
            for i in range(shp[1] // CHUNK):
                dst[r0 + i:r0 + i + 1, :] = ref[:, i * CHUNK:(i + 1) * CHUNK]
        elif ref.shape[-1] == CHUNK:
            dst[r0:r0 + shp[0], :] = ref[0:shp[0], :]
        else:
            dst[r0:r0 + shp[0], 0:shp[1]] = ref[...]


def _unpack_rows(src, refs):
    for (shp, r0), ref in zip(_S_LAYOUT, refs):
        if shp[0] == 1 and shp[1] >= CHUNK:
            for i in range(shp[1] // CHUNK):
                ref[:, i * CHUNK:(i + 1) * CHUNK] = src[r0 + i:r0 + i + 1, :]
        elif shp[1] == CHUNK:
            ref[...] = src[r0:r0 + shp[0], :]
        else:
            if tuple(ref.shape) == (shp[1], shp[0]):
                ref[...] = src[r0:r0 + CHUNK, :].T[0:shp[1], 0:shp[0]]
            else:
                ref[...] = src[r0:r0 + shp[0], 0:shp[1]]


_MEM_G = 2


def _greduce(ga, gb, dmkv, mem2, gm, w_mkv, small_g, loss_p):
    shp_c = (SHARD_O, 2 * MEM_LEN)
    shapes = (shp_c, gb.shape[1:], ga.shape[1:])
    rs = _S_ROWS

    def body(*refs):
        it = iter(refs)
        take = lambda n: [next(it) for _ in range(n)]
        gb_ref, ga_ref, d_ref, m_ref, gm_ref, wm_ref = take(6)
        sg_refs = take(_N_SMALL - 1)
        loss_ref, = take(1)
        oc, ob, oa, ogs = take(4)
        red = take(3 * _N_RED)
        gs_ref, rs_a, rs_b, rs_w, gc_ref, dgm_ref = take(6)
        ssem_a, rsem_a, ssem_b, rsem_b = take(4)

        pos = _position()
        x, y, cc = pos
        myq = 2 * x + y
        here, sib = (x, y, cc), (x, y, 1 - cc)
        chips = _other_chips(x, y)
        reducers = [_ShardReduce(pos, g, red[k * _N_RED:k * _N_RED + 5], red[k * _N_RED + 5:(k + 1) * _N_RED])
                    for k, g in enumerate((gc_ref, gb_ref, ga_ref))]
        for rd in reducers[:0:-1]:
            rd.start()

        xf = m_ref[...]
        nm = xf * _rms(xf)
        hm = (nm * gm_ref[...]).astype(MM)
        d = d_ref[...].astype(MM)
        for o in range(N_DEV):
            gc_ref[o] = _dot_tn(hm[:, o * SHARD_O:(o + 1) * SHARD_O], d)
        dgm_ref[...] = jnp.sum(_dot_nt(d, wm_ref[...]) * nm, axis=0, keepdims=True)
        reducers[0].start()

        gs_ref[...] = jnp.zeros_like(gs_ref)
        _pack_rows(gs_ref, sg_refs[:_MEM_G] + [dgm_ref] + sg_refs[_MEM_G:])
        gs_ref[_LOSS_ROW:_LOSS_ROW + 1, :] = loss_ref[0:1, :]
        small_a = _remote(gs_ref, rs_a, ssem_a, rsem_a, sib)
        small_a.start()

        _remote(gs_ref, rs_a, ssem_a, rsem_a, here).wait_recv()
        rs_b[myq] = gs_ref[0:_W_SP_ROW, :] + rs_a[0:_W_SP_ROW, :]
        rs_w[myq] = (gs_ref[_W_SP_ROW:rs, :] + rs_a[_W_SP_ROW:rs, :]).astype(BF16)
        small_b = []
        for j, chip in enumerate(chips):
            to = (chip[0], chip[1], cc)
            small_b.append(_remote(rs_b.at[myq], rs_b.at[myq], ssem_b.at[0, j], rsem_b.at[0, j], to))
            small_b.append(_remote(rs_w.at[myq], rs_w.at[myq], ssem_b.at[1, j], rsem_b.at[1, j], to))
        for cp in small_b:
            cp.start()
        late_last = reducers[:0:-1] + reducers[:1]
        for rd in late_last:
            rd.mid()
        for rd in late_last:
            rd.pass_on()

        for j in range(3):
            _remote(rs_b.at[myq], rs_b.at[myq], ssem_b.at[0, j], rsem_b.at[0, j], here).wait_recv()
            _remote(rs_w.at[myq], rs_w.at[myq], ssem_b.at[1, j], rsem_b.at[1, j], here).wait_recv()
        ogs[0:_W_SP_ROW, :] = ((rs_b[0] + rs_b[1]) + rs_b[2]) + rs_b[3]

        def tot_w(r):
            w = [rs_w[q, r, :].astype(F32) for q in range(4)]
            ogs[pl.ds(pl.multiple_of(_W_SP_ROW + r.start, 8), _ROWS), :] = ((w[0] + w[1]) + w[2]) + w[3]

        _rows_loop(rs - _W_SP_ROW, tot_w)
        for rd, out in zip(late_last, (oa, ob, oc)):
            rd.finish(out)
        small_a.wait_send()
        for cp in small_b:
            cp.wait_send()

    vm = pl.BlockSpec(memory_space=pltpu.VMEM)
    anyspec = pl.BlockSpec(memory_space=pl.ANY)
    scratch = []
    for shp in shapes:
        scratch += _reduce_scratch(shp)
    scratch += [pltpu.VMEM((rs, CHUNK), F32), pltpu.VMEM((rs, CHUNK), F32),
                pltpu.VMEM((4, _W_SP_ROW, CHUNK), F32), pltpu.VMEM((4, rs - _W_SP_ROW, CHUNK), BF16),
                pltpu.VMEM((N_DEV,) + shp_c, F32), pltpu.VMEM((1, D_MODEL), F32),
                pltpu.SemaphoreType.DMA, pltpu.SemaphoreType.DMA,
                pltpu.SemaphoreType.DMA((2, 3)), pltpu.SemaphoreType.DMA((2, 3))]
    tc, tb, ta, ts = pl.pallas_call(
        body, name="greduce",
        out_shape=tuple([jax.ShapeDtypeStruct(shp, F32) for shp in shapes] + [jax.ShapeDtypeStruct((rs, CHUNK), F32)]),
        in_specs=[anyspec] * 2 + [vm] * (4 + _N_SMALL),
        out_specs=(vm, vm, vm, vm),
        scratch_shapes=scratch,
        compiler_params=_params(),
    )(gb, ga, dmkv, mem2, gm, w_mkv, *small_g, loss_p)
    return ta, tb, tc, ts


def _adamw(w, g, m, v):
    m = ADAM_B1 * m + (1.0 - ADAM_B1) * g
    v = ADAM_B2 * v + (1.0 - ADAM_B2) * (g * g)
    m_hat = m / (1.0 - ADAM_B1 ** ADAM_STEP)
    v_hat = v / (1.0 - ADAM_B2 ** ADAM_STEP)
    delta = -ADAM_LR * (m_hat / (jnp.sqrt(v_hat) + ADAM_EPS) + ADAM_WD * w)
    return delta, m, v


def _update(ta, tb, tc, ts, big_wmv, small_wmv):
    shapes = (ta.shape, tb.shape, tc.shape)
    rs = _S_ROWS
    small_shapes = [tuple(a.shape) for a in small_wmv[0]]

    def body(*refs):
        it = iter(refs)
        take = lambda n: [next(it) for _ in range(n)]
        ga_ref, gb_ref, gc_ref, gs_ref = take(4)
        wa, ma, va, wb, mb, vb_, wc, mc, vc = take(9)
        sw_refs, sm_refs, sv_refs = take(_N_SMALL), take(_N_SMALL), take(_N_SMALL)
        oga, oda, oma, ova, ogb, odb, omb, ovb, ogc, odc, omc, ovc = take(12)
        so_refs = [take(_N_SMALL) for _ in range(4)]
        loss_out, = take(1)
        ws, ms, vs, ods, oms, ovs = take(6)

        for buf in (ws, ms, vs):
            buf[...] = jnp.zeros_like(buf)
        _pack_rows(ws, sw_refs)
        _pack_rows(ms, sm_refs)
        _pack_rows(vs, sv_refs)

        big = ((ga_ref, wa, ma, va, oga, oda, oma, ova), (gb_ref, wb, mb, vb_, ogb, odb, omb, ovb),
               (gc_ref, wc, mc, vc, ogc, odc, omc, ovc))
        for arr in range(3):
            g_r, w_r, m_r, v_r, og, od, om, ov = big[arr]

            def upd(r, g_r=g_r, w_r=w_r, m_r=m_r, v_r=v_r, og=og, od=od, om=om, ov=ov):
                g = g_r[r, :]
                d, m, v = _adamw(w_r[r, :], g, m_r[r, :], v_r[r, :])
                og[r, :] = g
                od[r, :] = d
                om[r, :] = m
                ov[r, :] = v

            _rows_loop(shapes[arr][0], upd)

        def upd_s(i, _):
            r = pl.ds(pl.multiple_of(i * 8, 8), 8)
            d, m, v = _adamw(ws[r, :], gs_ref[r, :], ms[r, :], vs[r, :])
            ods[r, :] = d
            oms[r, :] = m
            ovs[r, :] = v
            return 0

        lax.fori_loop(0, rs // 8, upd_s, 0)
        for k, buf in enumerate((gs_ref, ods, oms, ovs)):
            _unpack_rows(buf, so_refs[k])
        loss_out[...] = gs_ref[_LOSS_ROW:_LOSS_ROW + 1, 0:1]

    vm = pl.BlockSpec(memory_space=pltpu.VMEM)
    big_out = []
    for shp in shapes:
        big_out += [jax.ShapeDtypeStruct(shp, F32)] * 4
    small_out = [jax.ShapeDtypeStruct(shp[::-1] if shp == (N_BUCKETS, 4) else shp, F32) for shp in small_shapes] * 4
    out_shape = tuple(big_out + small_out + [jax.ShapeDtypeStruct((1, 1), F32)])
    n_in = 4 + 9 + 3 * _N_SMALL
    return pl.pallas_call(
        body, name="update",
        out_shape=out_shape,
        in_specs=[vm] * n_in,
        out_specs=tuple([vm] * len(out_shape)),
        scratch_shapes=[pltpu.VMEM((rs, CHUNK), F32) for _ in range(6)],
        compiler_params=_params(),
    )(ta, tb, tc, ts, *big_wmv, *small_wmv[0], *small_wmv[1], *small_wmv[2])


def kernel(x, mem, pre_norm_g, post_norm_g, mem_norm_g, w_in, w_mem_kv, v_norm_g, v_norm_b, w_spatial, b_spatial, attn_sinks, rel_bias, w_out, loss_target, m_pre_norm_g, m_post_norm_g, m_mem_norm_g, m_w_in, m_w_mem_kv, m_v_norm_g, m_v_norm_b, m_w_spatial, m_b_spatial, m_attn_sinks, m_rel_bias, m_w_out, v_pre_norm_g, v_post_norm_g, v_mem_norm_g, v_w_in, v_w_mem_kv, v_v_norm_g, v_v_norm_b, v_w_spatial, v_b_spatial, v_attn_sinks, v_rel_bias, v_w_out):
    sh_a = (w_in[0].T, m_w_in[0].T, v_w_in[0].T)
    sh_b = (w_out[0], m_w_out[0], v_w_out[0])
    sh_c = (w_mem_kv[0], m_w_mem_kv[0], v_w_mem_kv[0])
    nb, s, _ = x.shape
    t = nb * s
    x2 = x.reshape(t, D_MODEL)
    tgt2 = loss_target.reshape(t, D_MODEL)
    mem2 = mem.reshape(nb * MEM_LEN, D_MODEL)
    buckets = jnp.asarray(_t5_buckets())

    wa, wb, wc, bias, wt, wtt, bcol, mkv = _wgather(sh_a[0], sh_b[0], sh_c[0], rel_bias, w_spatial[0], b_spatial[0],
                                                    buckets, mem2, mem_norm_g)
    w_mkv = wc.reshape(D_MODEL, 2 * MEM_LEN)
    gx, dmkv, dwi, dwo, dg1, dg2, loss_p, dwsp, dbs, dvg, dvb, dsink, drel = _layer(
        x2, tgt2, mkv.reshape(nb, MEM_LEN, 2 * MEM_LEN), bias, attn_sinks.reshape(4), v_norm_g, v_norm_b, wt, wtt, bcol,
        pre_norm_g, post_norm_g, wa.reshape(IN_WIDTH, D_MODEL), wb.reshape(D_MODEL, D_MODEL), buckets,
        nb, s, min(256, s))
    gx = gx.reshape(nb, s, D_MODEL)
    small_grads = [dg1, dg2, dvg, dvb, dbs, dsink, drel, dwsp.reshape(A_GROUPS * CHUNK, CHUNK)]

    small_names = ["pre_norm_g", "post_norm_g", "mem_norm_g", "v_norm_g", "v_norm_b", "b_spatial", "attn_sinks",
                   "rel_bias", "w_spatial"]
    given = dict(pre_norm_g=(pre_norm_g, m_pre_norm_g, v_pre_norm_g), post_norm_g=(post_norm_g, m_post_norm_g, v_post_norm_g),
                 mem_norm_g=(mem_norm_g, m_mem_norm_g, v_mem_norm_g), v_norm_g=(v_norm_g, m_v_norm_g, v_v_norm_g),
                 v_norm_b=(v_norm_b, m_v_norm_b, v_v_norm_b), b_spatial=(b_spatial, m_b_spatial, v_b_spatial),
                 attn_sinks=(attn_sinks, m_attn_sinks, v_attn_sinks), rel_bias=(rel_bias, m_rel_bias, v_rel_bias),
                 w_spatial=(w_spatial, m_w_spatial, v_w_spatial))
    small_wmv = [[given[n][k].reshape(shp) for n, (shp, _) in zip(small_names, _S_LAYOUT)] for k in range(3)]

    ta, tb, tc, ts = _greduce(dwi.reshape(N_DEV, SHARD_IN, D_MODEL), dwo.reshape(N_DEV, SHARD_O, D_MODEL),
                              dmkv.reshape(nb * MEM_LEN, 2 * MEM_LEN), mem2, mem_norm_g, w_mkv, small_grads, loss_p)
    outs = _update(ta, tb, tc, ts, (*sh_a, *sh_b, *sh_c), small_wmv)
    ra, rb, rc = outs[0:4], outs[4:8], outs[8:12]
    loss = outs[12 + 4 * _N_SMALL].reshape(())

    res = {}
    for k, kind in enumerate(("grad", "delta", "new_m", "new_v")):
        res[kind, "w_in"] = ra[k].T[None]
        res[kind, "w_out"] = rb[k][None]
        res[kind, "w_mem_kv"] = rc[k][None]
        for i, n in enumerate(small_names):
            o = outs[12 + k * _N_SMALL + i]
            res[kind, n] = o.T if n == "rel_bias" else o.reshape(given[n][0].shape)
    order = ["pre_norm_g", "post_norm_g", "mem_norm_g", "w_in", "w_mem_kv", "v_norm_g", "v_norm_b", "w_spatial",
             "b_spatial", "attn_sinks", "rel_bias", "w_out"]
    flat = [res[kind, n] for kind in ("grad", "delta", "new_m", "new_v") for n in order]
    return (loss, gx, *flat)
```

```python
import numpy as np
import jax
import jax.numpy as jnp
from jax import lax
from jax.experimental import pallas as pl
from jax.experimental.pallas import tpu as pltpu

F32 = jnp.float32
BF16 = jnp.bfloat16
MM = jnp.bfloat16

D_MODEL = 1024
CHUNK = 128
A_GROUPS = 4
A_WIDTH = 512
UV_W = 1024
QKV_W = 768
Z_W = 1024
IN_WIDTH = UV_W + QKV_W + Z_W
MEM_LEN = 256
N_BUCKETS = 32
MAX_DISTANCE = 128
EPS = 1e-6
NEG = -1e30
SCALE = 0.125
N_DEV = 8
SHARD_IN = IN_WIDTH // N_DEV
SHARD_O = D_MODEL // N_DEV

SQ_COL, SK_COL, SV_COL, MQ_COL, Z_COL = UV_W, UV_W + 256, UV_W + 384, UV_W + 512, UV_W + QKV_W
YB_OFF, YC_OFF = 512, 768

ADAM_LR = 0.001
ADAM_B1 = 0.9
ADAM_B2 = 0.999
ADAM_EPS = 1e-08
ADAM_WD = 0.01
ADAM_STEP = 10

VMEM_LIMIT = 60 * 1024 * 1024

_GELU_C = 0.7978845608028654
_GELU_A = 0.044715

MESH = pl.DeviceIdType.MESH
_ROWS = 32


def _dot(a, b):
    return lax.dot_general(a, b, (((1,), (0,)), ((), ())), preferred_element_type=F32)


def _dot_nt(a, b):
    return lax.dot_general(a, b, (((1,), (1,)), ((), ())), preferred_element_type=F32)


def _dot_tn(a, b):
    return lax.dot_general(a, b, (((0,), (0,)), ((), ())), preferred_element_type=F32)


def _gelu_and_grad(x):
    x2 = x * x
    t = jnp.tanh(_GELU_C * (x + _GELU_A * x * x2))
    g = 0.5 * x * (1.0 + t)
    dg = 0.5 * (1.0 + t) + 0.5 * x * (1.0 - t * t) * (_GELU_C * (1.0 + 3.0 * _GELU_A * x2))
    return g, dg


def _t5_buckets():
    qi = np.arange(CHUNK)[:, None]
    kj = np.arange(2 * CHUNK)[None, :]
    n = np.maximum(qi + CHUNK - kj, 0)
    max_exact = N_BUCKETS // 2
    large = max_exact + (np.log(np.maximum(n, 1) / max_exact) / np.log(MAX_DISTANCE / max_exact)
                         * (N_BUCKETS - max_exact)).astype(np.int32)
    large = np.minimum(large, N_BUCKETS - 1)
    return np.where(n < max_exact, n, large).astype(np.int32)


def _params(**kw):
    return pltpu.CompilerParams(vmem_limit_bytes=VMEM_LIMIT, **kw)


def _full(shape, single=False):
    nd = len(shape)
    if single:
        return pl.BlockSpec(shape, lambda *_: (0,) * nd, pipeline_mode=pl.Buffered(1))
    return pl.BlockSpec(shape, lambda *_: (0,) * nd)


def _window_valid():
    qi = lax.broadcasted_iota(jnp.int32, (CHUNK, 2 * CHUNK), 0)
    kj = lax.broadcasted_iota(jnp.int32, (CHUNK, 2 * CHUNK), 1)
    dist = qi + CHUNK - kj
    return (dist >= 0) & (dist < CHUNK)


def _position():
    return lax.axis_index("x"), lax.axis_index("y"), lax.axis_index("c")


def _other_chips(x, y):
    return [(1 - x, y), (x, 1 - y), (1 - x, 1 - y)]


def _route(x, y, c):
    first = (x * c + (1 - x) * (1 - c), y * (1 - c) + (1 - y) * c)
    second = (x * (1 - c) + (1 - x) * c, y * c + (1 - y) * (1 - c))
    return first, second, (1 - x, 1 - y)


def _remote(src, dst, ssem, rsem, to):
    return pltpu.make_async_remote_copy(src_ref=src, dst_ref=dst, send_sem=ssem, recv_sem=rsem,
                                        device_id=to, device_id_type=MESH)


def _rows_loop(nrow, fn):
    def step(i, _):
        fn(pl.ds(pl.multiple_of(i * _ROWS, _ROWS), _ROWS))
        return 0

    lax.fori_loop(0, nrow // _ROWS, step, 0)


class _Gather:
    def __init__(self, pos, out, ssem, rsem):
        self.x, self.y, self.c = pos
        self.out, self.ssem, self.rsem = out, ssem, rsem
        self.me = 4 * self.x + 2 * self.y + self.c
        self.here = (self.x, self.y, self.c)
        self.sib = (self.x, self.y, 1 - self.c)
        self.first, self.second, self.far = _route(*pos)

    def _copy(self, k, blk, to):
        r = self.out.at[blk]
        return _remote(r, r, self.ssem.at[k], self.rsem.at[k], to)

    def _idx(self, chip, core):
        return 4 * chip[0] + 2 * chip[1] + core

    def _on(self, chip):
        return (chip[0], chip[1], self.c)

    def start(self):
        self._copy(0, self.me, self.sib).start()
        self._copy(1, self.me, self._on(self.first)).start()
        self._copy(2, self.me, self._on(self.second)).start()

    def forward(self):
        c = self.c
        self._copy(1, self._idx(self.first, c), self.here).wait_recv()
        self._copy(3, self._idx(self.first, c), self._on(self.second)).start()
        self._copy(4, self._idx(self.first, c), self.sib).start()
        self._copy(2, self._idx(self.second, c), self.here).wait_recv()
        self._copy(5, self._idx(self.second, c), self.sib).start()
        self._copy(3, self._idx(self.far, c), self.here).wait_recv()
        self._copy(6, self._idx(self.far, c), self.sib).start()

    def finish(self):
        c = self.c
        self._copy(0, self._idx((self.x, self.y), 1 - c), self.here).wait_recv()
        for k, chip in ((4, self.second), (5, self.first), (6, self.far)):
            self._copy(k, self._idx(chip, 1 - c), self.here).wait_recv()
        self._copy(0, self.me, self.sib).wait_send()
        self._copy(1, self.me, self._on(self.first)).wait_send()
        self._copy(2, self.me, self._on(self.second)).wait_send()
        self._copy(3, self._idx(self.first, c), self._on(self.second)).wait_send()
        for k, chip in ((4, self.first), (5, self.second), (6, self.far)):
            self._copy(k, self._idx(chip, c), self.sib).wait_send()


def _prep_tables(rb_ref, w_ref, b_ref, bk_ref, bias_ref, wt_ref, wtt_ref, bcol_ref):
    valid = _window_valid()
    bk = bk_ref[...]
    acc = [jnp.full((CHUNK, 2 * CHUNK), NEG, F32) for _ in range(4)]
    for b in range(N_BUCKETS):
        hit = (bk == b) & valid
        for h in range(4):
            acc[h] = jnp.where(hit, rb_ref[b, h], acc[h])
    for h in range(4):
        bias_ref[h] = acc[h]
    r = lax.broadcasted_iota(jnp.int32, (CHUNK, CHUNK), 0)
    c = lax.broadcasted_iota(jnp.int32, (CHUNK, CHUNK), 1)
    for g in range(A_GROUPS):
        w = jnp.where(r >= c, w_ref[g], 0.0)
        wt_ref[g] = w.astype(MM)
        wtt_ref[g] = w.T.astype(MM)
        bcol_ref[g] = jnp.broadcast_to(b_ref[g:g + 1, :], (CHUNK, CHUNK)).T


def _wgather(a, b, c, rel_bias, w_sp, b_sp, buckets, mem2, gm):
    tmem = mem2.shape[0]

    def body(a_ref, b_ref, c_ref, rb_ref, w_ref, bsp_ref, bk_ref, m_ref, gm_ref,
             oa, ob, oc, bias_ref, wt_ref, wtt_ref, bcol_ref, mkv_ref, ssem, rsem):
        pos = _position()
        me = 4 * pos[0] + 2 * pos[1] + pos[2]
        gathers = []
        for k, (src, out) in enumerate(((c_ref, oc), (b_ref, ob), (a_ref, oa))):
            out[me] = src[...].astype(BF16)
            g = _Gather(pos, out, ssem.at[k], rsem.at[k])
            g.start()
            gathers.append(g)
        _prep_tables(rb_ref, w_ref, bsp_ref, bk_ref, bias_ref, wt_ref, wtt_ref, bcol_ref)
        for g in gathers:
            g.forward()
        gathers[0].finish()
        xf = m_ref[...]
        hm = (xf * _rms(xf) * gm_ref[...]).astype(MM)
        acc = jnp.zeros((tmem, 2 * MEM_LEN), F32)
        for d in range(N_DEV):
            acc = acc + _dot(hm[:, d * SHARD_O:(d + 1) * SHARD_O], oc[d])
        mkv_ref[...] = acc.astype(MM)
        for g in gathers[1:]:
            g.finish()

    vm = pl.BlockSpec(memory_space=pltpu.VMEM)
    grp = (A_GROUPS, CHUNK, CHUNK)
    return pl.pallas_call(
        body, name="wgather",
        out_shape=(jax.ShapeDtypeStruct((N_DEV,) + a.shape, BF16),
                   jax.ShapeDtypeStruct((N_DEV,) + b.shape, BF16),
                   jax.ShapeDtypeStruct((N_DEV,) + c.shape, BF16),
                   jax.ShapeDtypeStruct((4, CHUNK, 2 * CHUNK), F32),
                   jax.ShapeDtypeStruct(grp, MM), jax.ShapeDtypeStruct(grp, MM), jax.ShapeDtypeStruct(grp, F32),
                   jax.ShapeDtypeStruct((tmem, 2 * MEM_LEN), MM)),
        in_specs=[vm, vm, vm, pl.BlockSpec(memory_space=pltpu.SMEM), vm, vm, vm, vm, vm],
        out_specs=tuple([vm] * 8),
        scratch_shapes=[pltpu.SemaphoreType.DMA((3, 7)), pltpu.SemaphoreType.DMA((3, 7))],
        compiler_params=_params(),
    )(a, b, c, rel_bias, w_sp, b_sp, buckets, mem2, gm)


def _half_masks(rows):
    lane = lax.broadcasted_iota(jnp.int32, (rows, CHUNK), 1)
    return lane < 64


def _dup_heads(band):
    b32 = band.astype(F32)
    rolled = pltpu.roll(b32, 64, 1)
    lo = _half_masks(band.shape[0])
    return (jnp.where(lo, b32, rolled).astype(MM), jnp.where(lo, rolled, b32).astype(MM))


def _swa_probs(qsel, kd, bias_h, sink_h, first_add):
    s = _dot_nt(qsel, kd) * SCALE + bias_h + first_add
    m = jnp.maximum(jnp.max(s, axis=-1, keepdims=True), sink_h)
    p = jnp.exp(s - m)
    es = jnp.exp(sink_h - m)
    inv = 1.0 / (jnp.sum(p, axis=-1, keepdims=True) + es)
    return p * inv, es * inv


def _softmax(s):
    m = jnp.max(s, axis=-1, keepdims=True)
    p = jnp.exp(s - m)
    return p * (1.0 / jnp.sum(p, axis=-1, keepdims=True))


def _first_block_mask(n):
    col = lax.broadcasted_iota(jnp.int32, (CHUNK, 2 * CHUNK), 1)
    return jnp.where((col < CHUNK) & (n == 0), NEG, 0.0)


def _rms(xf):
    return lax.rsqrt(jnp.mean(xf * xf, axis=-1, keepdims=True) + EPS)


def _layer(x2, tgt2, mkv3, bias, sinks, vg, vb, wt, wtt, bcol, g1, g2, w_in_t, w_o, buckets, nb, s, tm):
    nt = s // tm
    bpt = tm // CHUNK
    bps = s // CHUNK
    t = nb * s

    def body(x_ref, xp_ref, t_ref, mkv_ref, bias_ref, sink_ref, vg_ref, vb_ref, wt_ref, wtt_ref, bcol_ref,
             g1_ref, g2_ref, wi_ref, wo_ref, bk_ref,
             gx_ref, dmkv_ref, dwi_hbm, dwo_hbm, dg1_ref, dg2_ref, loss_ref, dwsp_ref, dbs_ref,
             dvg_ref, dvb_ref, dsink_ref, drel_ref,
             acc_i, acc_o, uv_s, z_s, q_s, kv_s, h_s, dp_s, dxo_s, dh_s, r_s,
             ycat, dyc, u_s, gu_s, gv_s, xh_s, vc_s, pb_s, ps_s, pc_s, kd_s, vd_s,
             dkv_acc, dbias_acc, dsv_acc, dsink_acc, sems):
        b, j = pl.program_id(0), pl.program_id(1)
        jt = nt - 1 - j

        @pl.when((b == 0) & (j == 0))
        def _():
            for ref in (acc_i, acc_o, dg1_ref, dg2_ref, loss_ref, dwsp_ref, dvg_ref, dvb_ref,
                        dbias_acc, dsv_acc, dsink_acc):
                ref[...] = jnp.zeros_like(ref)

        @pl.when(j == 0)
        def _():
            dmkv_ref[...] = jnp.zeros_like(dmkv_ref)
            dkv_acc[...] = jnp.zeros_like(dkv_acc)

        carry = dkv_acc[0:CHUNK, :]
        dkv_acc[...] = jnp.zeros_like(dkv_acc)
        dkv_acc[tm:tm + CHUNK, :] = carry

        lo = _half_masks(CHUNK)
        lob = _half_masks(2 * CHUNK)
        lot = _half_masks(tm)
        g1v = g1_ref[...]

        xf = x_ref[...]
        r_s[...] = _rms(xf)
        h = (xf * r_s[...] * g1v).astype(MM)
        h_s[...] = h
        uv_s[...] = _dot_nt(h, wi_ref[0:UV_W, :])
        qkv = _dot_nt(h, wi_ref[SQ_COL:Z_COL, :])
        q_s[:, 0:256] = qkv[:, 0:256].astype(MM)
        q_s[:, 256:512] = qkv[:, 512:768].astype(MM)
        kv_s[CHUNK:CHUNK + tm, :] = qkv[:, 256:512].astype(MM)
        z_s[...] = _dot_nt(h, wi_ref[Z_COL:IN_WIDTH, :])
        xp = xp_ref[...]
        hp = (xp * _rms(xp) * g1v).astype(MM)
        kv_s[0:CHUNK, :] = _dot_nt(hp, wi_ref[SK_COL:MQ_COL, :]).astype(MM)

        for blk in range(bpt):
            r0 = blk * CHUNK
            rows = slice(r0, r0 + CHUNK)
            n = jt * bpt + blk
            for g in range(A_GROUPS):
                cg = slice(g * CHUNK, (g + 1) * CHUNK)
                u, gu = _gelu_and_grad(uv_s[rows, cg])
                v, gv = _gelu_and_grad(uv_s[rows, A_WIDTH + g * CHUNK:A_WIDTH + (g + 1) * CHUNK])
                mu = jnp.mean(v, axis=-1, keepdims=True)
                xc = v - mu
                rstd = lax.rsqrt(jnp.mean(xc * xc, axis=-1, keepdims=True) + EPS)
                xhat = xc * rstd
                vc = (xhat * vg_ref[:, cg] + vb_ref[:, cg]).astype(MM)
                sv = _dot(wt_ref[g], vc) + bcol_ref[g]
                u_s[rows, cg] = u
                gu_s[rows, cg] = sv * gu
                gv_s[rows, cg] = rstd * gv
                xh_s[rows, cg] = xhat
                vc_s[rows, cg] = vc
                ycat[rows, cg] = u * sv
            kd = _dup_heads(kv_s[r0:r0 + 2 * CHUNK, 0:CHUNK])
            vd = _dup_heads(kv_s[r0:r0 + 2 * CHUNK, CHUNK:2 * CHUNK])
            first_add = _first_block_mask(n)
            for kvh in range(2):
                kd_s[blk * 2 + kvh] = kd[kvh]
                vd_s[blk * 2 + kvh] = vd[kvh]
                q128 = q_s[rows, kvh * CHUNK:(kvh + 1) * CHUNK].astype(F32)
                outs = []
                for gi in range(2):
                    hd = 2 * kvh + gi
                    qsel = jnp.where(lo if gi == 0 else ~lo, q128, 0.0).astype(MM)
                    probs, ps = _swa_probs(qsel, kd[kvh], bias_ref[hd], sink_ref[hd], first_add)
                    pb_s[blk * 4 + hd] = probs
                    ps_s[blk * 4 + hd] = jnp.broadcast_to(ps, (CHUNK, CHUNK))
                    outs.append(_dot(probs.astype(MM), vd[kvh]))
                ycat[rows, YB_OFF + kvh * CHUNK:YB_OFF + (kvh + 1) * CHUNK] = jnp.where(lo, outs[0], outs[1])
        for g in range(2):
            q128 = q_s[:, 256 + g * CHUNK:256 + (g + 1) * CHUNK].astype(F32)
            k128 = mkv_ref[:, g * CHUNK:(g + 1) * CHUNK]
            v128 = mkv_ref[:, MEM_LEN + g * CHUNK:MEM_LEN + (g + 1) * CHUNK]
            outs = []
            for hh in range(2):
                qsel = jnp.where(lot if hh == 0 else ~lot, q128, 0.0).astype(MM)
                probs = _softmax(_dot_nt(qsel, k128) * SCALE)
                pc_s[2 * g + hh] = probs
                outs.append(_dot(probs.astype(MM), v128))
            ycat[:, YC_OFF + g * CHUNK:YC_OFF + (g + 1) * CHUNK] = jnp.where(lot, outs[0], outs[1])

        zt = z_s[...]
        sig = 1.0 / (1.0 + jnp.exp(-zt))
        silu = zt * sig
        yc = ycat[...]
        yb = (yc * silu).astype(MM)
        o = _dot(yb, wo_ref[...])
        r2 = _rms(o)
        nrm = o * r2
        g2v = g2_ref[...]
        e = x_ref[...] + nrm * g2v - t_ref[...]
        l1 = jnp.sum(e * e, axis=-1, keepdims=True)
        loss_ref[...] += jnp.broadcast_to(jnp.sum(l1, axis=0, keepdims=True) * (0.5 / D_MODEL), loss_ref.shape)
        dxo = e * (1.0 / D_MODEL)
        dxo_s[...] = dxo
        dg2_ref[...] += jnp.sum(dxo * nrm, axis=0, keepdims=True)
        dn = dxo * g2v
        do = r2 * (dn - nrm * jnp.mean(dn * nrm, axis=-1, keepdims=True))
        dob = do.astype(MM)
        dy = _dot_nt(dob, wo_ref[...])
        dp_s[:, Z_COL:IN_WIDTH] = (dy * yc * (sig * (1.0 + zt * (1.0 - sig)))).astype(MM)
        dyc[...] = dy * silu
        acc_o[...] += _dot_tn(yb, dob)

        def in_proj_bwd(c0, c1):
            dpt = dp_s[:, c0:c1]
            acc_i[c0:c1, :] += _dot_tn(dpt, h_s[...])
            part = _dot(dpt, wi_ref[c0:c1, :])
            if c0 == Z_COL:
                dh_s[...] = part
            else:
                dh_s[...] += part

        in_proj_bwd(Z_COL, IN_WIDTH)

        for blk in range(bpt):
            r0 = blk * CHUNK
            rows = slice(r0, r0 + CHUNK)
            for g in range(A_GROUPS):
                cg = slice(g * CHUNK, (g + 1) * CHUNK)
                cv = slice(A_WIDTH + g * CHUNK, A_WIDTH + (g + 1) * CHUNK)
                dya = dyc[rows, cg]
                dp_s[rows, cg] = (dya * gu_s[rows, cg]).astype(MM)
                dsv = dya * u_s[rows, cg]
                dsvb = dsv.astype(MM)
                dsv_acc[g] += dsv
                dwsp_ref[g] += _dot_nt(dsvb, vc_s[rows, cg])
                dvc = _dot(wtt_ref[g], dsvb)
                xhat = xh_s[rows, cg]
                dvg_ref[:, cg] += jnp.sum(dvc * xhat, axis=0, keepdims=True)
                dvb_ref[:, cg] += jnp.sum(dvc, axis=0, keepdims=True)
                dxh = dvc * vg_ref[:, cg]
                dv = (dxh - jnp.mean(dxh, axis=-1, keepdims=True)
                      - xhat * jnp.mean(dxh * xhat, axis=-1, keepdims=True))
                dp_s[rows, cv] = (dv * gv_s[rows, cg]).astype(MM)
        in_proj_bwd(0, UV_W)
        for blk in range(bpt):
            r0 = blk * CHUNK
            rows = slice(r0, r0 + CHUNK)
            dk_f, dv_f = [], []
            for kvh in range(2):
                kd = kd_s[blk * 2 + kvh]
                vd = vd_s[blk * 2 + kvh]
                q128 = q_s[rows, kvh * CHUNK:(kvh + 1) * CHUNK].astype(F32)
                do128 = dyc[rows, YB_OFF + kvh * CHUNK:YB_OFF + (kvh + 1) * CHUNK]
                dq128 = jnp.zeros((CHUNK, CHUNK), F32)
                dkd = jnp.zeros((2 * CHUNK, CHUNK), F32)
                dvd = jnp.zeros((2 * CHUNK, CHUNK), F32)
                for gi in range(2):
                    hd = 2 * kvh + gi
                    half = lo if gi == 0 else ~lo
                    qsel = jnp.where(half, q128, 0.0).astype(MM)
                    dosel = jnp.where(half, do128, 0.0).astype(MM)
                    probs = pb_s[blk * 4 + hd]
                    ps = ps_s[blk * 4 + hd][:, 0:1]
                    dp = _dot_nt(dosel, vd)
                    delta = jnp.sum(probs * dp, axis=-1, keepdims=True)
                    ds = probs * (dp - delta)
                    dbias_acc[hd] += ds
                    dsink_acc[hd:hd + 1, :] += jnp.broadcast_to(-jnp.sum(ps * delta, axis=0, keepdims=True), (1, CHUNK))
                    dss = (ds * SCALE).astype(MM)
                    dq128 = dq128 + jnp.where(half, _dot(dss, kd), 0.0)
                    dkd = dkd + _dot_tn(dss, qsel)
                    dvd = dvd + _dot_tn(probs.astype(MM), dosel)
                dp_s[rows, SQ_COL + kvh * CHUNK:SQ_COL + (kvh + 1) * CHUNK] = dq128.astype(MM)
                dk_f.append(dkd + pltpu.roll(dkd, 64, 1))
                dv_f.append(dvd + pltpu.roll(dvd, 64, 1))
            dkv_acc[r0:r0 + 2 * CHUNK, 0:CHUNK] += jnp.where(lob, dk_f[0], dk_f[1])
            dkv_acc[r0:r0 + 2 * CHUNK, CHUNK:2 * CHUNK] += jnp.where(lob, dv_f[0], dv_f[1])
        dp_s[:, SK_COL:MQ_COL] = dkv_acc[CHUNK:CHUNK + tm, :].astype(MM)
        for g in range(2):
            q128 = q_s[:, 256 + g * CHUNK:256 + (g + 1) * CHUNK].astype(F32)
            k128 = mkv_ref[:, g * CHUNK:(g + 1) * CHUNK]
            v128 = mkv_ref[:, MEM_LEN + g * CHUNK:MEM_LEN + (g + 1) * CHUNK]
            do128 = dyc[:, YC_OFF + g * CHUNK:YC_OFF + (g + 1) * CHUNK]
            dq128 = jnp.zeros((tm, CHUNK), F32)
            dk128 = jnp.zeros((MEM_LEN, CHUNK), F32)
            dv128 = jnp.zeros((MEM_LEN, CHUNK), F32)
            for hh in range(2):
                half = lot if hh == 0 else ~lot
                qsel = jnp.where(half, q128, 0.0).astype(MM)
                dosel = jnp.where(half, do128, 0.0).astype(MM)
                probs = pc_s[2 * g + hh]
                dp = _dot_nt(dosel, v128)
                ds = probs * (dp - jnp.sum(probs * dp, axis=-1, keepdims=True))
                dss = (ds * SCALE).astype(MM)
                dq128 = dq128 + jnp.where(half, _dot(dss, k128), 0.0)
                dk128 = dk128 + _dot_tn(dss, qsel)
                dv128 = dv128 + _dot_tn(probs.astype(MM), dosel)
            dp_s[:, MQ_COL + g * CHUNK:MQ_COL + (g + 1) * CHUNK] = dq128.astype(MM)
            dmkv_ref[:, g * CHUNK:(g + 1) * CHUNK] += dk128
            dmkv_ref[:, MEM_LEN + g * CHUNK:MEM_LEN + (g + 1) * CHUNK] += dv128

        in_proj_bwd(SQ_COL, Z_COL)
        dh = dh_s[...]
        xf = x_ref[...]
        r = r_s[...]
        nx = xf * r
        dg1_ref[...] += jnp.sum(dh * nx, axis=0, keepdims=True)
        dnx = dh * g1v
        gx_ref[...] = dxo_s[...] + r * (dnx - nx * jnp.mean(dnx * nx, axis=-1, keepdims=True))

        @pl.when((b == nb - 1) & (j == nt - 1))
        def _():
            out_i = pltpu.make_async_copy(acc_i, dwi_hbm, sems.at[0])
            out_o = pltpu.make_async_copy(acc_o, dwo_hbm, sems.at[1])
            out_i.start()
            out_o.start()
            r_ = lax.broadcasted_iota(jnp.int32, (CHUNK, CHUNK), 0)
            c_ = lax.broadcasted_iota(jnp.int32, (CHUNK, CHUNK), 1)
            for g in range(A_GROUPS):
                dwsp_ref[g] = jnp.where(r_ >= c_, dwsp_ref[g], 0.0)
                dbs_ref[g:g + 1, :] = jnp.sum(dsv_acc[g].T, axis=0, keepdims=True)
            rows8 = lax.broadcasted_iota(jnp.int32, (8, CHUNK), 0)
            cols8 = lax.broadcasted_iota(jnp.int32, (8, CHUNK), 1)
            sk = jnp.zeros((8, CHUNK), F32)
            for hd in range(4):
                sk = sk + jnp.where((rows8 == 0) & (cols8 == hd),
                                    jnp.broadcast_to(dsink_acc[hd:hd + 1, :], (8, CHUNK)), 0.0)
            dsink_ref[...] = sk
            bk = bk_ref[...]
            valid = _window_valid()
            rrow = lax.broadcasted_iota(jnp.int32, (N_BUCKETS, CHUNK), 0)
            rcol = lax.broadcasted_iota(jnp.int32, (N_BUCKETS, CHUNK), 1)
            acc = jnp.zeros((N_BUCKETS, CHUNK), F32)
            for bb in range(N_BUCKETS):
                hit = (bk == bb) & valid
                for hd in range(4):
                    part = jnp.sum(jnp.where(hit, dbias_acc[hd], 0.0), axis=-1, keepdims=True)
                    tot = jnp.sum(part, axis=0, keepdims=True)
                    acc = acc + jnp.where((rrow == bb) & (rcol == hd), jnp.broadcast_to(tot, (N_BUCKETS, CHUNK)), 0.0)
            drel_ref[...] = acc
            out_i.wait()
            out_o.wait()

    tile = lambda w: pl.BlockSpec((tm, w), lambda b, j: (b * nt + nt - 1 - j, 0))
    prev_block = pl.BlockSpec((CHUNK, D_MODEL), lambda b, j: (b * bps + jnp.maximum((nt - 1 - j) * bpt - 1, 0), 0))
    per_batch = lambda r, w: pl.BlockSpec((None, r, w), lambda b, j: (b, 0, 0))
    anyspec = pl.BlockSpec(memory_space=pl.ANY)
    grp = (A_GROUPS, CHUNK, CHUNK)
    return pl.pallas_call(
        body, name="layer", grid=(nb, nt),
        out_shape=(jax.ShapeDtypeStruct((t, D_MODEL), F32),
                   jax.ShapeDtypeStruct((nb, MEM_LEN, 2 * MEM_LEN), F32),
                   jax.ShapeDtypeStruct((IN_WIDTH, D_MODEL), F32),
                   jax.ShapeDtypeStruct((D_MODEL, D_MODEL), F32),
                   jax.ShapeDtypeStruct((1, D_MODEL), F32),
                   jax.ShapeDtypeStruct((1, D_MODEL), F32),
                   jax.ShapeDtypeStruct((8, CHUNK), F32),
                   jax.ShapeDtypeStruct(grp, F32),
                   jax.ShapeDtypeStruct((A_GROUPS, CHUNK), F32),
                   jax.ShapeDtypeStruct((1, A_WIDTH), F32),
                   jax.ShapeDtypeStruct((1, A_WIDTH), F32),
                   jax.ShapeDtypeStruct((8, CHUNK), F32),
                   jax.ShapeDtypeStruct((N_BUCKETS, CHUNK), F32)),
        in_specs=[tile(D_MODEL), prev_block, tile(D_MODEL), per_batch(MEM_LEN, 2 * MEM_LEN),
                  _full((4, CHUNK, 2 * CHUNK)),
                  pl.BlockSpec(memory_space=pltpu.SMEM),
                  _full((1, A_WIDTH)), _full((1, A_WIDTH)),
                  _full(grp), _full(grp), _full(grp),
                  _full((1, D_MODEL)), _full((1, D_MODEL)),
                  _full((IN_WIDTH, D_MODEL), single=True), _full((D_MODEL, D_MODEL), single=True),
                  _full((CHUNK, 2 * CHUNK))],
        out_specs=(tile(D_MODEL), per_batch(MEM_LEN, 2 * MEM_LEN), anyspec, anyspec,
                   _full((1, D_MODEL)), _full((1, D_MODEL)), _full((8, CHUNK)),
                   _full(grp), _full((A_GROUPS, CHUNK)), _full((1, A_WIDTH)), _full((1, A_WIDTH)),
                   _full((8, CHUNK)), _full((N_BUCKETS, CHUNK))),
        scratch_shapes=[pltpu.VMEM((IN_WIDTH, D_MODEL), F32), pltpu.VMEM((D_MODEL, D_MODEL), F32),
                        pltpu.VMEM((tm, UV_W), F32), pltpu.VMEM((tm, Z_W), F32),
                        pltpu.VMEM((tm, 512), MM), pltpu.VMEM((tm + CHUNK, 2 * CHUNK), MM),
                        pltpu.VMEM((tm, D_MODEL), MM), pltpu.VMEM((tm, IN_WIDTH), MM),
                        pltpu.VMEM((tm, D_MODEL), F32),
                        pltpu.VMEM((tm, D_MODEL), F32), pltpu.VMEM((tm, 1), F32),
                        pltpu.VMEM((tm, D_MODEL), F32), pltpu.VMEM((tm, D_MODEL), F32)]
                       + [pltpu.VMEM((tm, A_WIDTH), F32) for _ in range(4)]
                       + [pltpu.VMEM((tm, A_WIDTH), MM),
                          pltpu.VMEM((bpt * 4, CHUNK, 2 * CHUNK), F32),
                          pltpu.VMEM((bpt * 4, CHUNK, CHUNK), F32),
                          pltpu.VMEM((4, tm, MEM_LEN), F32),
                          pltpu.VMEM((bpt * 2, 2 * CHUNK, CHUNK), MM),
                          pltpu.VMEM((bpt * 2, 2 * CHUNK, CHUNK), MM),
                          pltpu.VMEM((tm + CHUNK, 2 * CHUNK), F32),
                          pltpu.VMEM((4, CHUNK, 2 * CHUNK), F32),
                          pltpu.VMEM(grp, F32),
                          pltpu.VMEM((8, CHUNK), F32),
                          pltpu.SemaphoreType.DMA((2,))],
        compiler_params=_params(dimension_semantics=("arbitrary", "arbitrary")),
    )(x2, x2, tgt2, mkv3, bias, sinks, vg, vb, wt, wtt, bcol, g1, g2, w_in_t, w_o, buckets)


class _ShardReduce:
    def __init__(self, pos, g, bufs, sems):
        self.x, self.y, self.c = pos
        self.g = g
        self.own, self.rcv, self.sbuf, self.rbuf, self.cbuf = bufs
        self.ld, self.sa, self.ra, self.sb, self.rb = sems
        self.nrow = g.shape[1]
        self.here = (self.x, self.y, self.c)
        self.sib = (self.x, self.y, 1 - self.c)
        self.first, self.second, self.far = _route(*pos)

    def _load(self, q):
        return pltpu.make_async_copy(self.g.at[2 * q + self.c], self.own.at[q], self.ld.at[q])

    def _to_sib(self, q, to):
        return _remote(self.g.at[2 * q + 1 - self.c], self.rcv.at[q], self.sa.at[q], self.ra.at[q], to)

    def _send(self, k, to):
        dst = self.cbuf.at[0] if k == 1 else self.rbuf.at[0 if k == 0 else 1]
        return _remote(self.sbuf.at[k], dst, self.sb.at[k], self.rb.at[k], to)

    def _stage(self, k, which, extra=None):
        def cast(r):
            v = self.rcv[which, r, :]
            if extra is not None:
                v = v + extra[0, r, :].astype(F32)
            self.sbuf[k, r, :] = v.astype(BF16)

        _rows_loop(self.nrow, cast)

    @staticmethod
    def _q(chip):
        return 2 * chip[0] + chip[1]

    def start(self):
        for q in range(4):
            self._load(q).start()
            self._to_sib(q, self.sib).start()

    def mid(self):
        for q in range(4):
            self._load(q).wait()
            self._to_sib(q, self.here).wait_recv()

        def add(r):
            for q in range(4):
                self.rcv[q, r, :] = self.rcv[q, r, :] + self.own[q, r, :]

        _rows_loop(self.nrow, add)
        to_first = (self.first[0], self.first[1], self.c)
        self._stage(0, self._q(self.first))
        self._send(0, to_first).start()
        self._stage(1, self._q(self.far))
        self._send(1, to_first).start()

    def pass_on(self):
        self._send(1, self.here).wait_recv()
        self._stage(2, self._q(self.second), extra=self.cbuf)
        self._send(2, (self.second[0], self.second[1], self.c)).start()

    def finish(self, out):
        self._send(0, self.here).wait_recv()
        self._send(2, self.here).wait_recv()
        which = 2 * self.x + self.y

        def tot(r):
            out[r, :] = (self.rcv[which, r, :] + self.rbuf[0, r, :].astype(F32)) + self.rbuf[1, r, :].astype(F32)

        _rows_loop(self.nrow, tot)
        for q in range(4):
            self._to_sib(q, self.sib).wait_send()
        to_first = (self.first[0], self.first[1], self.c)
        self._send(0, to_first).wait_send()
        self._send(1, to_first).wait_send()
        self._send(2, (self.second[0], self.second[1], self.c)).wait_send()


def _reduce_scratch(shape):
    return [pltpu.VMEM((4,) + shape, F32), pltpu.VMEM((4,) + shape, F32),
            pltpu.VMEM((3,) + shape, BF16), pltpu.VMEM((2,) + shape, BF16), pltpu.VMEM((1,) + shape, BF16),
            pltpu.SemaphoreType.DMA((4,)), pltpu.SemaphoreType.DMA((4,)), pltpu.SemaphoreType.DMA((4,)),
            pltpu.SemaphoreType.DMA((3,)), pltpu.SemaphoreType.DMA((3,))]


_N_RED = 10

_S_LAYOUT = (((1, D_MODEL), 0), ((1, D_MODEL), 8), ((1, D_MODEL), 16),
             ((1, A_WIDTH), 24), ((1, A_WIDTH), 28), ((A_GROUPS, CHUNK), 32),
             ((1, 4), 36), ((N_BUCKETS, 4), 40),
             ((A_GROUPS * CHUNK, CHUNK), 72))
_LOSS_ROW = 37
_W_SP_ROW = _S_LAYOUT[-1][1]
_S_ROWS = _W_SP_ROW + A_GROUPS * CHUNK
_N_SMALL = len(_S_LAYOUT)


def _pack_rows(dst, refs):
    for (shp, r0), ref in zip(_S_LAYOUT, refs):
        if shp[0] == 1 and shp[1] >= CHUNK:
            for i in range(shp[1] // CHUNK):
                dst[r0 + i:r0 + i + 1, :] = ref[:, i * CHUNK:(i + 1) * CHUNK]
        elif ref.shape[-1] == CHUNK:
            dst[r0:r0 + shp[0], :] = ref[0:shp[0], :]
        else:
            dst[r0:r0 + shp[0], 0:shp[1]] = ref[...]


def _unpack_rows(src, refs):
    for (shp, r0), ref in zip(_S_LAYOUT, refs):
        if shp[0] == 1 and shp[1] >= CHUNK:
            for i in range(shp[1] // CHUNK):
                ref[:, i * CHUNK:(i + 1) * CHUNK] = src[r0 + i:r0 + i + 1, :]
        elif shp[1] == CHUNK:
            ref[...] = src[r0:r0 + shp[0], :]
        else:
            if tuple(ref.shape) == (shp[1], shp[0]):
                ref[...] = src[r0:r0 + CHUNK, :].T[0:shp[1], 0:shp[0]]
            else:
                ref[...] = src[r0:r0 + shp[0], 0:shp[1]]


_MEM_G = 2


def _greduce(ga, gb, dmkv, mem2, gm, w_mkv, small_g, loss_p):
    shp_c = (SHARD_O, 2 * MEM_LEN)
    shapes = (shp_c, gb.shape[1:], ga.shape[1:])
    rs = _S_ROWS

    def body(*refs):
        it = iter(refs)
        take = lambda n: [next(it) for _ in range(n)]
        gb_ref, ga_ref, d_ref, m_ref, gm_ref, wm_ref = take(6)
        sg_refs = take(_N_SMALL - 1)
        loss_ref, = take(1)
        oc, ob, oa, ogs = take(4)
        red = take(3 * _N_RED)
        gs_ref, rs_a, rs_b, rs_w, gc_ref, dgm_ref = take(6)
        ssem_a, rsem_a, ssem_b, rsem_b = take(4)

        pos = _position()
        x, y, cc = pos
        myq = 2 * x + y
        here, sib = (x, y, cc), (x, y, 1 - cc)
        chips = _other_chips(x, y)
        reducers = [_ShardReduce(pos, g, red[k * _N_RED:k * _N_RED + 5], red[k * _N_RED + 5:(k + 1) * _N_RED])
                    for k, g in enumerate((gc_ref, gb_ref, ga_ref))]
        for rd in reducers[1:]:
            rd.start()

        xf = m_ref[...]
        nm = xf * _rms(xf)
        hm = (nm * gm_ref[...]).astype(MM)
        d = d_ref[...].astype(MM)
        for o in range(N_DEV):
            gc_ref[o] = _dot_tn(hm[:, o * SHARD_O:(o + 1) * SHARD_O], d)
        dgm_ref[...] = jnp.sum(_dot_nt(d, wm_ref[...]) * nm, axis=0, keepdims=True)
        reducers[0].start()

        gs_ref[...] = jnp.zeros_like(gs_ref)
        _pack_rows(gs_ref, sg_refs[:_MEM_G] + [dgm_ref] + sg_refs[_MEM_G:])
        gs_ref[_LOSS_ROW:_LOSS_ROW + 1, :] = loss_ref[0:1, :]
        small_a = _remote(gs_ref, rs_a, ssem_a, rsem_a, sib)
        small_a.start()

        _remote(gs_ref, rs_a, ssem_a, rsem_a, here).wait_recv()
        rs_b[myq] = gs_ref[0:_W_SP_ROW, :] + rs_a[0:_W_SP_ROW, :]
        rs_w[myq] = (gs_ref[_W_SP_ROW:rs, :] + rs_a[_W_SP_ROW:rs, :]).astype(BF16)
        small_b = []
        for j, chip in enumerate(chips):
            to = (chip[0], chip[1], cc)
            small_b.append(_remote(rs_b.at[myq], rs_b.at[myq], ssem_b.at[0, j], rsem_b.at[0, j], to))
            small_b.append(_remote(rs_w.at[myq], rs_w.at[myq], ssem_b.at[1, j], rsem_b.at[1, j], to))
        for cp in small_b:
            cp.start()
        late_last = reducers[1:] + reducers[:1]
        for rd in late_last:
            rd.mid()
        for rd in late_last:
            rd.pass_on()

        for j in range(3):
            _remote(rs_b.at[myq], rs_b.at[myq], ssem_b.at[0, j], rsem_b.at[0, j], here).wait_recv()
            _remote(rs_w.at[myq], rs_w.at[myq], ssem_b.at[1, j], rsem_b.at[1, j], here).wait_recv()
        ogs[0:_W_SP_ROW, :] = ((rs_b[0] + rs_b[1]) + rs_b[2]) + rs_b[3]

        def tot_w(r):
            w = [rs_w[q, r, :].astype(F32) for q in range(4)]
            ogs[pl.ds(pl.multiple_of(_W_SP_ROW + r.start, 8), _ROWS), :] = ((w[0] + w[1]) + w[2]) + w[3]

        _rows_loop(rs - _W_SP_ROW, tot_w)
        for rd, out in zip(late_last, (ob, oa, oc)):
            rd.finish(out)
        small_a.wait_send()
        for cp in small_b:
            cp.wait_send()

    vm = pl.BlockSpec(memory_space=pltpu.VMEM)
    anyspec = pl.BlockSpec(memory_space=pl.ANY)
    scratch = []
    for shp in shapes:
        scratch += _reduce_scratch(shp)
    scratch += [pltpu.VMEM((rs, CHUNK), F32), pltpu.VMEM((rs, CHUNK), F32),
                pltpu.VMEM((4, _W_SP_ROW, CHUNK), F32), pltpu.VMEM((4, rs - _W_SP_ROW, CHUNK), BF16),
                pltpu.VMEM((N_DEV,) + shp_c, F32), pltpu.VMEM((1, D_MODEL), F32),
                pltpu.SemaphoreType.DMA, pltpu.SemaphoreType.DMA,
                pltpu.SemaphoreType.DMA((2, 3)), pltpu.SemaphoreType.DMA((2, 3))]
    tc, tb, ta, ts = pl.pallas_call(
        body, name="greduce",
        out_shape=tuple([jax.ShapeDtypeStruct(shp, F32) for shp in shapes] + [jax.ShapeDtypeStruct((rs, CHUNK), F32)]),
        in_specs=[anyspec] * 2 + [vm] * (4 + _N_SMALL),
        out_specs=(vm, vm, vm, vm),
        scratch_shapes=scratch,
        compiler_params=_params(),
    )(gb, ga, dmkv, mem2, gm, w_mkv, *small_g, loss_p)
    return ta, tb, tc, ts


def _adamw(w, g, m, v):
    m = ADAM_B1 * m + (1.0 - ADAM_B1) * g
    v = ADAM_B2 * v + (1.0 - ADAM_B2) * (g * g)
    m_hat = m / (1.0 - ADAM_B1 ** ADAM_STEP)
    v_hat = v / (1.0 - ADAM_B2 ** ADAM_STEP)
    delta = -ADAM_LR * (m_hat / (jnp.sqrt(v_hat) + ADAM_EPS) + ADAM_WD * w)
    return delta, m, v


def _update(ta, tb, tc, ts, big_wmv, small_wmv):
    shapes = (ta.shape, tb.shape, tc.shape)
    rs = _S_ROWS
    small_shapes = [tuple(a.shape) for a in small_wmv[0]]

    def body(*refs):
        it = iter(refs)
        take = lambda n: [next(it) for _ in range(n)]
        ga_ref, gb_ref, gc_ref, gs_ref = take(4)
        wa, ma, va, wb, mb, vb_, wc, mc, vc = take(9)
        sw_refs, sm_refs, sv_refs = take(_N_SMALL), take(_N_SMALL), take(_N_SMALL)
        oga, oda, oma, ova, ogb, odb, omb, ovb, ogc, odc, omc, ovc = take(12)
        so_refs = [take(_N_SMALL) for _ in range(4)]
        loss_out, = take(1)
        ws, ms, vs, ods, oms, ovs = take(6)

        for buf in (ws, ms, vs):
            buf[...] = jnp.zeros_like(buf)
        _pack_rows(ws, sw_refs)
        _pack_rows(ms, sm_refs)
        _pack_rows(vs, sv_refs)

        big = ((ga_ref, wa, ma, va, oga, oda, oma, ova), (gb_ref, wb, mb, vb_, ogb, odb, omb, ovb),
               (gc_ref, wc, mc, vc, ogc, odc, omc, ovc))
        for arr in range(3):
            g_r, w_r, m_r, v_r, og, od, om, ov = big[arr]

            def upd(r, g_r=g_r, w_r=w_r, m_r=m_r, v_r=v_r, og=og, od=od, om=om, ov=ov):
                g = g_r[r, :]
                d, m, v = _adamw(w_r[r, :], g, m_r[r, :], v_r[r, :])
                og[r, :] = g
                od[r, :] = d
                om[r, :] = m
                ov[r, :] = v

            _rows_loop(shapes[arr][0], upd)

        def upd_s(i, _):
            r = pl.ds(pl.multiple_of(i * 8, 8), 8)
            d, m, v = _adamw(ws[r, :], gs_ref[r, :], ms[r, :], vs[r, :])
            ods[r, :] = d
            oms[r, :] = m
            ovs[r, :] = v
            return 0

        lax.fori_loop(0, rs // 8, upd_s, 0)
        for k, buf in enumerate((gs_ref, ods, oms, ovs)):
            _unpack_rows(buf, so_refs[k])
        loss_out[...] = gs_ref[_LOSS_ROW:_LOSS_ROW + 1, 0:1]

    vm = pl.BlockSpec(memory_space=pltpu.VMEM)
    big_out = []
    for shp in shapes:
        big_out += [jax.ShapeDtypeStruct(shp, F32)] * 4
    small_out = [jax.ShapeDtypeStruct(shp[::-1] if shp == (N_BUCKETS, 4) else shp, F32) for shp in small_shapes] * 4
    out_shape = tuple(big_out + small_out + [jax.ShapeDtypeStruct((1, 1), F32)])
    n_in = 4 + 9 + 3 * _N_SMALL
    return pl.pallas_call(
        body, name="update",
        out_shape=out_shape,
        in_specs=[vm] * n_in,
        out_specs=tuple([vm] * len(out_shape)),
        scratch_shapes=[pltpu.VMEM((rs, CHUNK), F32) for _ in range(6)],
        compiler_params=_params(),
    )(ta, tb, tc, ts, *big_wmv, *small_wmv[0], *small_wmv[1], *small_wmv[2])


def kernel(x, mem, pre_norm_g, post_norm_g, mem_norm_g, w_in, w_mem_kv, v_norm_g, v_norm_b, w_spatial, b_spatial, attn_sinks, rel_bias, w_out, loss_target, m_pre_norm_g, m_post_norm_g, m_mem_norm_g, m_w_in, m_w_mem_kv, m_v_norm_g, m_v_norm_b, m_w_spatial, m_b_spatial, m_attn_sinks, m_rel_bias, m_w_out, v_pre_norm_g, v_post_norm_g, v_mem_norm_g, v_w_in, v_w_mem_kv, v_v_norm_g, v_v_norm_b, v_w_spatial, v_b_spatial, v_attn_sinks, v_rel_bias, v_w_out):
    sh_a = (w_in[0].T, m_w_in[0].T, v_w_in[0].T)
    sh_b = (w_out[0], m_w_out[0], v_w_out[0])
    sh_c = (w_mem_kv[0], m_w_mem_kv[0], v_w_mem_kv[0])
    nb, s, _ = x.shape
    t = nb * s
    x2 = x.reshape(t, D_MODEL)
    tgt2 = loss_target.reshape(t, D_MODEL)
    mem2 = mem.reshape(nb * MEM_LEN, D_MODEL)
    buckets = jnp.asarray(_t5_buckets())

    wa, wb, wc, bias, wt, wtt, bcol, mkv = _wgather(sh_a[0], sh_b[0], sh_c[0], rel_bias, w_spatial[0], b_spatial[0],
                                                    buckets, mem2, mem_norm_g)
    w_mkv = wc.reshape(D_MODEL, 2 * MEM_LEN)
    gx, dmkv, dwi, dwo, dg1, dg2, loss_p, dwsp, dbs, dvg, dvb, dsink, drel = _layer(
        x2, tgt2, mkv.reshape(nb, MEM_LEN, 2 * MEM_LEN), bias, attn_sinks.reshape(4), v_norm_g, v_norm_b, wt, wtt, bcol,
        pre_norm_g, post_norm_g, wa.reshape(IN_WIDTH, D_MODEL), wb.reshape(D_MODEL, D_MODEL), buckets,
        nb, s, min(256, s))
    gx = gx.reshape(nb, s, D_MODEL)
    small_grads = [dg1, dg2, dvg, dvb, dbs, dsink, drel, dwsp.reshape(A_GROUPS * CHUNK, CHUNK)]

    small_names = ["pre_norm_g", "post_norm_g", "mem_norm_g", "v_norm_g", "v_norm_b", "b_spatial", "attn_sinks",
                   "rel_bias", "w_spatial"]
    given = dict(pre_norm_g=(pre_norm_g, m_pre_norm_g, v_pre_norm_g), post_norm_g=(post_norm_g, m_post_norm_g, v_post_norm_g),
                 mem_norm_g=(mem_norm_g, m_mem_norm_g, v_mem_norm_g), v_norm_g=(v_norm_g, m_v_norm_g, v_v_norm_g),
                 v_norm_b=(v_norm_b, m_v_norm_b, v_v_norm_b), b_spatial=(b_spatial, m_b_spatial, v_b_spatial),
                 attn_sinks=(attn_sinks, m_attn_sinks, v_attn_sinks), rel_bias=(rel_bias, m_rel_bias, v_rel_bias),
                 w_spatial=(w_spatial, m_w_spatial, v_w_spatial))
    small_wmv = [[given[n][k].reshape(shp) for n, (shp, _) in zip(small_names, _S_LAYOUT)] for k in range(3)]

    ta, tb, tc, ts = _greduce(dwi.reshape(N_DEV, SHARD_IN, D_MODEL), dwo.reshape(N_DEV, SHARD_O, D_MODEL),
                              dmkv.reshape(nb * MEM_LEN, 2 * MEM_LEN), mem2, mem_norm_g, w_mkv, small_grads, loss_p)
    outs = _update(ta, tb, tc, ts, (*sh_a, *sh_b, *sh_c), small_wmv)
    ra, rb, rc = outs[0:4], outs[4:8], outs[8:12]
    loss = outs[12 + 4 * _N_SMALL].reshape(())

    res = {}
    for k, kind in enumerate(("grad", "delta", "new_m", "new_v")):
        res[kind, "w_in"] = ra[k].T[None]
        res[kind, "w_out"] = rb[k][None]
        res[kind, "w_mem_kv"] = rc[k][None]
        for i, n in enumerate(small_names):
            o = outs[12 + k * _N_SMALL + i]
            res[kind, n] = o.T if n == "rel_bias" else o.reshape(given[n][0].shape)
    order = ["pre_norm_g", "post_norm_g", "mem_norm_g", "w_in", "w_mem_kv", "v_norm_g", "v_norm_b", "w_spatial",
             "b_spatial", "attn_sinks", "rel_bias", "w_out"]
    flat = [res[kind, n] for kind in ("grad", "delta", "new_m", "new_v") for n in order]
    return (loss, gx, *flat)
```

```python
import numpy as np
import jax
import jax.numpy as jnp
from jax import lax
from jax.experimental import pallas as pl
from jax.experimental.pallas import tpu as pltpu

F32 = jnp.float32
BF16 = jnp.bfloat16
MM = jnp.bfloat16

D_MODEL = 1024
CHUNK = 128
A_GROUPS = 4
A_WIDTH = 512
UV_W = 1024
QKV_W = 768
Z_W = 1024
IN_WIDTH = UV_W + QKV_W + Z_W
MEM_LEN = 256
N_BUCKETS = 32
MAX_DISTANCE = 128
EPS = 1e-6
NEG = -1e30
SCALE = 0.125
N_DEV = 8
SHARD_IN = IN_WIDTH // N_DEV
SHARD_O = D_MODEL // N_DEV

SQ_COL, SK_COL, SV_COL, MQ_COL, Z_COL = UV_W, UV_W + 256, UV_W + 384, UV_W + 512, UV_W + QKV_W
YB_OFF, YC_OFF = 512, 768

ADAM_LR = 0.001
ADAM_B1 = 0.9
ADAM_B2 = 0.999
ADAM_EPS = 1e-08
ADAM_WD = 0.01
ADAM_STEP = 10

VMEM_LIMIT = 60 * 1024 * 1024

_GELU_C = 0.7978845608028654
_GELU_A = 0.044715

MESH = pl.DeviceIdType.MESH
_ROWS = 32


def _dot(a, b):
    return lax.dot_general(a, b, (((1,), (0,)), ((), ())), preferred_element_type=F32)


def _dot_nt(a, b):
    return lax.dot_general(a, b, (((1,), (1,)), ((), ())), preferred_element_type=F32)


def _dot_tn(a, b):
    return lax.dot_general(a, b, (((0,), (0,)), ((), ())), preferred_element_type=F32)


def _gelu_and_grad(x):
    x2 = x * x
    t = jnp.tanh(_GELU_C * (x + _GELU_A * x * x2))
    g = 0.5 * x * (1.0 + t)
    dg = 0.5 * (1.0 + t) + 0.5 * x * (1.0 - t * t) * (_GELU_C * (1.0 + 3.0 * _GELU_A * x2))
    return g, dg


def _t5_buckets():
    qi = np.arange(CHUNK)[:, None]
    kj = np.arange(2 * CHUNK)[None, :]
    n = np.maximum(qi + CHUNK - kj, 0)
    max_exact = N_BUCKETS // 2
    large = max_exact + (np.log(np.maximum(n, 1) / max_exact) / np.log(MAX_DISTANCE / max_exact)
                         * (N_BUCKETS - max_exact)).astype(np.int32)
    large = np.minimum(large, N_BUCKETS - 1)
    return np.where(n < max_exact, n, large).astype(np.int32)


def _params(**kw):
    return pltpu.CompilerParams(vmem_limit_bytes=VMEM_LIMIT, **kw)


def _full(shape, single=False):
    nd = len(shape)
    if single:
        return pl.BlockSpec(shape, lambda *_: (0,) * nd, pipeline_mode=pl.Buffered(1))
    return pl.BlockSpec(shape, lambda *_: (0,) * nd)


def _window_valid():
    qi = lax.broadcasted_iota(jnp.int32, (CHUNK, 2 * CHUNK), 0)
    kj = lax.broadcasted_iota(jnp.int32, (CHUNK, 2 * CHUNK), 1)
    dist = qi + CHUNK - kj
    return (dist >= 0) & (dist < CHUNK)


def _position():
    return lax.axis_index("x"), lax.axis_index("y"), lax.axis_index("c")


def _other_chips(x, y):
    return [(1 - x, y), (x, 1 - y), (1 - x, 1 - y)]


def _route(x, y, c):
    first = (x * c + (1 - x) * (1 - c), y * (1 - c) + (1 - y) * c)
    second = (x * (1 - c) + (1 - x) * c, y * c + (1 - y) * (1 - c))
    return first, second, (1 - x, 1 - y)


def _remote(src, dst, ssem, rsem, to):
    return pltpu.make_async_remote_copy(src_ref=src, dst_ref=dst, send_sem=ssem, recv_sem=rsem,
                                        device_id=to, device_id_type=MESH)


def _rows_loop(nrow, fn):
    def step(i, _):
        fn(pl.ds(pl.multiple_of(i * _ROWS, _ROWS), _ROWS))
        return 0

    lax.fori_loop(0, nrow // _ROWS, step, 0)


class _Gather:
    def __init__(self, pos, out, ssem, rsem):
        self.x, self.y, self.c = pos
        self.out, self.ssem, self.rsem = out, ssem, rsem
        self.me = 4 * self.x + 2 * self.y + self.c
        self.here = (self.x, self.y, self.c)
        self.sib = (self.x, self.y, 1 - self.c)
        self.first, self.second, self.far = _route(*pos)

    def _copy(self, k, blk, to):
        r = self.out.at[blk]
        return _remote(r, r, self.ssem.at[k], self.rsem.at[k], to)

    def _idx(self, chip, core):
        return 4 * chip[0] + 2 * chip[1] + core

    def _on(self, chip):
        return (chip[0], chip[1], self.c)

    def start(self):
        self._copy(0, self.me, self.sib).start()
        self._copy(1, self.me, self._on(self.first)).start()
        self._copy(2, self.me, self._on(self.second)).start()

    def forward(self):
        c = self.c
        self._copy(1, self._idx(self.first, c), self.here).wait_recv()
        self._copy(3, self._idx(self.first, c), self._on(self.second)).start()
        self._copy(4, self._idx(self.first, c), self.sib).start()
        self._copy(2, self._idx(self.second, c), self.here).wait_recv()
        self._copy(5, self._idx(self.second, c), self.sib).start()
        self._copy(3, self._idx(self.far, c), self.here).wait_recv()
        self._copy(6, self._idx(self.far, c), self.sib).start()

    def finish(self):
        c = self.c
        self._copy(0, self._idx((self.x, self.y), 1 - c), self.here).wait_recv()
        for k, chip in ((4, self.second), (5, self.first), (6, self.far)):
            self._copy(k, self._idx(chip, 1 - c), self.here).wait_recv()
        self._copy(0, self.me, self.sib).wait_send()
        self._copy(1, self.me, self._on(self.first)).wait_send()
        self._copy(2, self.me, self._on(self.second)).wait_send()
        self._copy(3, self._idx(self.first, c), self._on(self.second)).wait_send()
        for k, chip in ((4, self.first), (5, self.second), (6, self.far)):
            self._copy(k, self._idx(chip, c), self.sib).wait_send()


def _prep_tables(rb_ref, w_ref, b_ref, bk_ref, bias_ref, wt_ref, wtt_ref, bcol_ref):
    valid = _window_valid()
    bk = bk_ref[...]
    acc = [jnp.full((CHUNK, 2 * CHUNK), NEG, F32) for _ in range(4)]
    for b in range(N_BUCKETS):
        hit = (bk == b) & valid
        for h in range(4):
            acc[h] = jnp.where(hit, rb_ref[b, h], acc[h])
    for h in range(4):
        bias_ref[h] = acc[h]
    r = lax.broadcasted_iota(jnp.int32, (CHUNK, CHUNK), 0)
    c = lax.broadcasted_iota(jnp.int32, (CHUNK, CHUNK), 1)
    for g in range(A_GROUPS):
        w = jnp.where(r >= c, w_ref[g], 0.0)
        wt_ref[g] = w.astype(MM)
        wtt_ref[g] = w.T.astype(MM)
        bcol_ref[g] = jnp.broadcast_to(b_ref[g:g + 1, :], (CHUNK, CHUNK)).T


def _wgather(a, b, c, rel_bias, w_sp, b_sp, buckets, mem2, gm):
    tmem = mem2.shape[0]

    def body(a_ref, b_ref, c_ref, rb_ref, w_ref, bsp_ref, bk_ref, m_ref, gm_ref,
             oa, ob, oc, bias_ref, wt_ref, wtt_ref, bcol_ref, mkv_ref, ssem, rsem):
        pos = _position()
        me = 4 * pos[0] + 2 * pos[1] + pos[2]
        gathers = []
        for k, (src, out) in enumerate(((c_ref, oc), (b_ref, ob), (a_ref, oa))):
            out[me] = src[...].astype(BF16)
            g = _Gather(pos, out, ssem.at[k], rsem.at[k])
            g.start()
            gathers.append(g)
        _prep_tables(rb_ref, w_ref, bsp_ref, bk_ref, bias_ref, wt_ref, wtt_ref, bcol_ref)
        for g in gathers:
            g.forward()
        gathers[0].finish()
        xf = m_ref[...]
        hm = (xf * _rms(xf) * gm_ref[...]).astype(MM)
        acc = jnp.zeros((tmem, 2 * MEM_LEN), F32)
        for d in range(N_DEV):
            acc = acc + _dot(hm[:, d * SHARD_O:(d + 1) * SHARD_O], oc[d])
        mkv_ref[...] = acc.astype(MM)
        for g in gathers[1:]:
            g.finish()

    vm = pl.BlockSpec(memory_space=pltpu.VMEM)
    grp = (A_GROUPS, CHUNK, CHUNK)
    return pl.pallas_call(
        body, name="wgather",
        out_shape=(jax.ShapeDtypeStruct((N_DEV,) + a.shape, BF16),
                   jax.ShapeDtypeStruct((N_DEV,) + b.shape, BF16),
                   jax.ShapeDtypeStruct((N_DEV,) + c.shape, BF16),
                   jax.ShapeDtypeStruct((4, CHUNK, 2 * CHUNK), F32),
                   jax.ShapeDtypeStruct(grp, MM), jax.ShapeDtypeStruct(grp, MM), jax.ShapeDtypeStruct(grp, F32),
                   jax.ShapeDtypeStruct((tmem, 2 * MEM_LEN), MM)),
        in_specs=[vm, vm, vm, pl.BlockSpec(memory_space=pltpu.SMEM), vm, vm, vm, vm, vm],
        out_specs=tuple([vm] * 8),
        scratch_shapes=[pltpu.SemaphoreType.DMA((3, 7)), pltpu.SemaphoreType.DMA((3, 7))],
        compiler_params=_params(),
    )(a, b, c, rel_bias, w_sp, b_sp, buckets, mem2, gm)


def _half_masks(rows):
    lane = lax.broadcasted_iota(jnp.int32, (rows, CHUNK), 1)
    return lane < 64


def _dup_heads(band):
    b32 = band.astype(F32)
    rolled = pltpu.roll(b32, 64, 1)
    lo = _half_masks(band.shape[0])
    return (jnp.where(lo, b32, rolled).astype(MM), jnp.where(lo, rolled, b32).astype(MM))


def _swa_probs(qk, bias_h, sink_h, first_add):
    s = qk * SCALE + bias_h + first_add
    m = jnp.maximum(jnp.max(s, axis=-1, keepdims=True), sink_h)
    p = jnp.exp(s - m)
    es = jnp.exp(sink_h - m)
    inv = 1.0 / (jnp.sum(p, axis=-1, keepdims=True) + es)
    return p * inv, es * inv


def _softmax(s):
    m = jnp.max(s, axis=-1, keepdims=True)
    p = jnp.exp(s - m)
    return p * (1.0 / jnp.sum(p, axis=-1, keepdims=True))


def _first_block_mask(n):
    col = lax.broadcasted_iota(jnp.int32, (CHUNK, 2 * CHUNK), 1)
    return jnp.where((col < CHUNK) & (n == 0), NEG, 0.0)


def _rms(xf):
    return lax.rsqrt(jnp.mean(xf * xf, axis=-1, keepdims=True) + EPS)


def _layer(x2, tgt2, mkv3, bias, sinks, vg, vb, wt, wtt, bcol, g1, g2, w_in_t, w_o, buckets, nb, s, tm):
    nt = s // tm
    bpt = tm // CHUNK
    bps = s // CHUNK
    t = nb * s

    def body(x_ref, xp_ref, t_ref, mkv_ref, bias_ref, sink_ref, vg_ref, vb_ref, wt_ref, wtt_ref, bcol_ref,
             g1_ref, g2_ref, wi_ref, wo_ref, bk_ref,
             gx_ref, dmkv_ref, dwi_hbm, dwo_hbm, dg1_ref, dg2_ref, loss_ref, dwsp_ref, dbs_ref,
             dvg_ref, dvb_ref, dsink_ref, drel_ref,
             acc_i, acc_o, uv_s, z_s, q_s, kv_s, h_s, dp_s, dxo_s, dh_s, r_s,
             ycat, dyc, u_s, gu_s, gv_s, xh_s, vc_s, pb_s, ps_s, pc_s, kd_s, vd_s,
             dkv_acc, dbias_acc, dsv_acc, dsink_acc, sems):
        b, j = pl.program_id(0), pl.program_id(1)
        jt = nt - 1 - j

        @pl.when((b == 0) & (j == 0))
        def _():
            for ref in (acc_i, acc_o, dg1_ref, dg2_ref, loss_ref, dwsp_ref, dvg_ref, dvb_ref,
                        dbias_acc, dsv_acc, dsink_acc):
                ref[...] = jnp.zeros_like(ref)

        @pl.when(j == 0)
        def _():
            dmkv_ref[...] = jnp.zeros_like(dmkv_ref)
            dkv_acc[...] = jnp.zeros_like(dkv_acc)

        carry = dkv_acc[0:CHUNK, :]
        dkv_acc[...] = jnp.zeros_like(dkv_acc)
        dkv_acc[tm:tm + CHUNK, :] = carry

        lo = _half_masks(CHUNK)
        lob = _half_masks(2 * CHUNK)
        lot = _half_masks(tm)
        g1v = g1_ref[...]

        xf = x_ref[...]
        r_s[...] = _rms(xf)
        h = (xf * r_s[...] * g1v).astype(MM)
        h_s[...] = h
        uv_s[...] = _dot_nt(h, wi_ref[0:UV_W, :])
        qkv = _dot_nt(h, wi_ref[SQ_COL:Z_COL, :])
        q_s[:, 0:256] = qkv[:, 0:256].astype(MM)
        q_s[:, 256:512] = qkv[:, 512:768].astype(MM)
        kv_s[CHUNK:CHUNK + tm, :] = qkv[:, 256:512].astype(MM)
        z_s[...] = _dot_nt(h, wi_ref[Z_COL:IN_WIDTH, :])
        xp = xp_ref[...]
        hp = (xp * _rms(xp) * g1v).astype(MM)
        kv_s[0:CHUNK, :] = _dot_nt(hp, wi_ref[SK_COL:MQ_COL, :]).astype(MM)

        for blk in range(bpt):
            r0 = blk * CHUNK
            rows = slice(r0, r0 + CHUNK)
            n = jt * bpt + blk
            for g in range(A_GROUPS):
                cg = slice(g * CHUNK, (g + 1) * CHUNK)
                u, gu = _gelu_and_grad(uv_s[rows, cg])
                v, gv = _gelu_and_grad(uv_s[rows, A_WIDTH + g * CHUNK:A_WIDTH + (g + 1) * CHUNK])
                mu = jnp.mean(v, axis=-1, keepdims=True)
                xc = v - mu
                rstd = lax.rsqrt(jnp.mean(xc * xc, axis=-1, keepdims=True) + EPS)
                xhat = xc * rstd
                vc = (xhat * vg_ref[:, cg] + vb_ref[:, cg]).astype(MM)
                sv = _dot(wt_ref[g], vc) + bcol_ref[g]
                u_s[rows, cg] = u
                gu_s[rows, cg] = sv * gu
                gv_s[rows, cg] = rstd * gv
                xh_s[rows, cg] = xhat
                vc_s[rows, cg] = vc
                ycat[rows, cg] = u * sv
            kd = _dup_heads(kv_s[r0:r0 + 2 * CHUNK, 0:CHUNK])
            vd = _dup_heads(kv_s[r0:r0 + 2 * CHUNK, CHUNK:2 * CHUNK])
            first_add = _first_block_mask(n)
            qks, pbs = [], []
            for kvh in range(2):
                kd_s[blk * 2 + kvh] = kd[kvh]
                vd_s[blk * 2 + kvh] = vd[kvh]
                q128 = q_s[rows, kvh * CHUNK:(kvh + 1) * CHUNK].astype(F32)
                for gi in range(2):
                    qsel = jnp.where(lo if gi == 0 else ~lo, q128, 0.0).astype(MM)
                    qks.append(_dot_nt(qsel, kd[kvh]))
            for hd in range(4):
                probs, ps = _swa_probs(qks[hd], bias_ref[hd], sink_ref[hd], first_add)
                pb_s[blk * 4 + hd] = probs
                ps_s[blk * 4 + hd] = jnp.broadcast_to(ps, (CHUNK, CHUNK))
                pbs.append(probs.astype(MM))
            outs = [_dot(pbs[hd], vd[hd // 2]) for hd in range(4)]
            for kvh in range(2):
                ycat[rows, YB_OFF + kvh * CHUNK:YB_OFF + (kvh + 1) * CHUNK] = jnp.where(
                    lo, outs[2 * kvh], outs[2 * kvh + 1])
        qks, pcs = [], []
        for g in range(2):
            q128 = q_s[:, 256 + g * CHUNK:256 + (g + 1) * CHUNK].astype(F32)
            for hh in range(2):
                qsel = jnp.where(lot if hh == 0 else ~lot, q128, 0.0).astype(MM)
                qks.append(_dot_nt(qsel, mkv_ref[:, g * CHUNK:(g + 1) * CHUNK]))
        for hd in range(4):
            probs = _softmax(qks[hd] * SCALE)
            pc_s[hd] = probs
            pcs.append(probs.astype(MM))
        outs = [_dot(pcs[hd], mkv_ref[:, MEM_LEN + (hd // 2) * CHUNK:MEM_LEN + (hd // 2 + 1) * CHUNK])
                for hd in range(4)]
        for g in range(2):
            ycat[:, YC_OFF + g * CHUNK:YC_OFF + (g + 1) * CHUNK] = jnp.where(lot, outs[2 * g], outs[2 * g + 1])

        zt = z_s[...]
        sig = 1.0 / (1.0 + jnp.exp(-zt))
        silu = zt * sig
        yc = ycat[...]
        yb = (yc * silu).astype(MM)
        o = _dot(yb, wo_ref[...])
        r2 = _rms(o)
        nrm = o * r2
        g2v = g2_ref[...]
        e = x_ref[...] + nrm * g2v - t_ref[...]
        l1 = jnp.sum(e * e, axis=-1, keepdims=True)
        loss_ref[...] += jnp.broadcast_to(jnp.sum(l1, axis=0, keepdims=True) * (0.5 / D_MODEL), loss_ref.shape)
        dxo = e * (1.0 / D_MODEL)
        dxo_s[...] = dxo
        dg2_ref[...] += jnp.sum(dxo * nrm, axis=0, keepdims=True)
        dn = dxo * g2v
        do = r2 * (dn - nrm * jnp.mean(dn * nrm, axis=-1, keepdims=True))
        dob = do.astype(MM)
        dy = _dot_nt(dob, wo_ref[...])
        dp_s[:, Z_COL:IN_WIDTH] = (dy * yc * (sig * (1.0 + zt * (1.0 - sig)))).astype(MM)
        dyc[...] = dy * silu
        acc_o[...] += _dot_tn(yb, dob)

        def in_proj_bwd(c0, c1):
            dpt = dp_s[:, c0:c1]
            acc_i[c0:c1, :] += _dot_tn(dpt, h_s[...])
            part = _dot(dpt, wi_ref[c0:c1, :])
            if c0 == Z_COL:
                dh_s[...] = part
            else:
                dh_s[...] += part

        in_proj_bwd(Z_COL, IN_WIDTH)

        for blk in range(bpt):
            r0 = blk * CHUNK
            rows = slice(r0, r0 + CHUNK)
            for g in range(A_GROUPS):
                cg = slice(g * CHUNK, (g + 1) * CHUNK)
                cv = slice(A_WIDTH + g * CHUNK, A_WIDTH + (g + 1) * CHUNK)
                dya = dyc[rows, cg]
                dp_s[rows, cg] = (dya * gu_s[rows, cg]).astype(MM)
                dsv = dya * u_s[rows, cg]
                dsvb = dsv.astype(MM)
                dsv_acc[g] += dsv
                dwsp_ref[g] += _dot_nt(dsvb, vc_s[rows, cg])
                dvc = _dot(wtt_ref[g], dsvb)
                xhat = xh_s[rows, cg]
                dvg_ref[:, cg] += jnp.sum(dvc * xhat, axis=0, keepdims=True)
                dvb_ref[:, cg] += jnp.sum(dvc, axis=0, keepdims=True)
                dxh = dvc * vg_ref[:, cg]
                dv = (dxh - jnp.mean(dxh, axis=-1, keepdims=True)
                      - xhat * jnp.mean(dxh * xhat, axis=-1, keepdims=True))
                dp_s[rows, cv] = (dv * gv_s[rows, cg]).astype(MM)
        in_proj_bwd(0, UV_W)
        for blk in range(bpt):
            r0 = blk * CHUNK
            rows = slice(r0, r0 + CHUNK)
            dosels, dps, dsss = [], [], []
            for hd in range(4):
                do128 = dyc[rows, YB_OFF + (hd // 2) * CHUNK:YB_OFF + (hd // 2 + 1) * CHUNK]
                dosels.append(jnp.where(lo if hd % 2 == 0 else ~lo, do128, 0.0).astype(MM))
                dps.append(_dot_nt(dosels[hd], vd_s[blk * 2 + hd // 2]))
            for hd in range(4):
                probs = pb_s[blk * 4 + hd]
                ps = ps_s[blk * 4 + hd][:, 0:1]
                delta = jnp.sum(probs * dps[hd], axis=-1, keepdims=True)
                ds = probs * (dps[hd] - delta)
                dbias_acc[hd] += ds
                dsink_acc[hd:hd + 1, :] += jnp.broadcast_to(-jnp.sum(ps * delta, axis=0, keepdims=True), (1, CHUNK))
                dsss.append((ds * SCALE).astype(MM))
            dk_f, dv_f = [], []
            for kvh in range(2):
                kd = kd_s[blk * 2 + kvh]
                q128 = q_s[rows, kvh * CHUNK:(kvh + 1) * CHUNK].astype(F32)
                dq128 = jnp.zeros((CHUNK, CHUNK), F32)
                dkd = jnp.zeros((2 * CHUNK, CHUNK), F32)
                dvd = jnp.zeros((2 * CHUNK, CHUNK), F32)
                for gi in range(2):
                    hd = 2 * kvh + gi
                    half = lo if gi == 0 else ~lo
                    qsel = jnp.where(half, q128, 0.0).astype(MM)
                    dq128 = dq128 + jnp.where(half, _dot(dsss[hd], kd), 0.0)
                    dkd = dkd + _dot_tn(dsss[hd], qsel)
                    dvd = dvd + _dot_tn(pb_s[blk * 4 + hd].astype(MM), dosels[hd])
                dp_s[rows, SQ_COL + kvh * CHUNK:SQ_COL + (kvh + 1) * CHUNK] = dq128.astype(MM)
                dk_f.append(dkd + pltpu.roll(dkd, 64, 1))
                dv_f.append(dvd + pltpu.roll(dvd, 64, 1))
            dkv_acc[r0:r0 + 2 * CHUNK, 0:CHUNK] += jnp.where(lob, dk_f[0], dk_f[1])
            dkv_acc[r0:r0 + 2 * CHUNK, CHUNK:2 * CHUNK] += jnp.where(lob, dv_f[0], dv_f[1])
        dp_s[:, SK_COL:MQ_COL] = dkv_acc[CHUNK:CHUNK + tm, :].astype(MM)
        dosels, dps, dsss = [], [], []
        for hd in range(4):
            do128 = dyc[:, YC_OFF + (hd // 2) * CHUNK:YC_OFF + (hd // 2 + 1) * CHUNK]
            dosels.append(jnp.where(lot if hd % 2 == 0 else ~lot, do128, 0.0).astype(MM))
            dps.append(_dot_nt(dosels[hd], mkv_ref[:, MEM_LEN + (hd // 2) * CHUNK:MEM_LEN + (hd // 2 + 1) * CHUNK]))
        for hd in range(4):
            probs = pc_s[hd]
            ds = probs * (dps[hd] - jnp.sum(probs * dps[hd], axis=-1, keepdims=True))
            dsss.append((ds * SCALE).astype(MM))
        for g in range(2):
            q128 = q_s[:, 256 + g * CHUNK:256 + (g + 1) * CHUNK].astype(F32)
            k128 = mkv_ref[:, g * CHUNK:(g + 1) * CHUNK]
            dq128 = jnp.zeros((tm, CHUNK), F32)
            dk128 = jnp.zeros((MEM_LEN, CHUNK), F32)
            dv128 = jnp.zeros((MEM_LEN, CHUNK), F32)
            for hh in range(2):
                hd = 2 * g + hh
                half = lot if hh == 0 else ~lot
                qsel = jnp.where(half, q128, 0.0).astype(MM)
                dq128 = dq128 + jnp.where(half, _dot(dsss[hd], k128), 0.0)
                dk128 = dk128 + _dot_tn(dsss[hd], qsel)
                dv128 = dv128 + _dot_tn(pc_s[hd].astype(MM), dosels[hd])
            dp_s[:, MQ_COL + g * CHUNK:MQ_COL + (g + 1) * CHUNK] = dq128.astype(MM)
            dmkv_ref[:, g * CHUNK:(g + 1) * CHUNK] += dk128
            dmkv_ref[:, MEM_LEN + g * CHUNK:MEM_LEN + (g + 1) * CHUNK] += dv128

        in_proj_bwd(SQ_COL, Z_COL)
        dh = dh_s[...]
        xf = x_ref[...]
        r = r_s[...]
        nx = xf * r
        dg1_ref[...] += jnp.sum(dh * nx, axis=0, keepdims=True)
        dnx = dh * g1v
        gx_ref[...] = dxo_s[...] + r * (dnx - nx * jnp.mean(dnx * nx, axis=-1, keepdims=True))

        @pl.when((b == nb - 1) & (j == nt - 1))
        def _():
            out_i = pltpu.make_async_copy(acc_i, dwi_hbm, sems.at[0])
            out_o = pltpu.make_async_copy(acc_o, dwo_hbm, sems.at[1])
            out_i.start()
            out_o.start()
            r_ = lax.broadcasted_iota(jnp.int32, (CHUNK, CHUNK), 0)
            c_ = lax.broadcasted_iota(jnp.int32, (CHUNK, CHUNK), 1)
            for g in range(A_GROUPS):
                dwsp_ref[g] = jnp.where(r_ >= c_, dwsp_ref[g], 0.0)
                dbs_ref[g:g + 1, :] = jnp.sum(dsv_acc[g].T, axis=0, keepdims=True)
            rows8 = lax.broadcasted_iota(jnp.int32, (8, CHUNK), 0)
            cols8 = lax.broadcasted_iota(jnp.int32, (8, CHUNK), 1)
            sk = jnp.zeros((8, CHUNK), F32)
            for hd in range(4):
                sk = sk + jnp.where((rows8 == 0) & (cols8 == hd),
                                    jnp.broadcast_to(dsink_acc[hd:hd + 1, :], (8, CHUNK)), 0.0)
            dsink_ref[...] = sk
            bk = bk_ref[...]
            valid = _window_valid()
            rrow = lax.broadcasted_iota(jnp.int32, (N_BUCKETS, CHUNK), 0)
            rcol = lax.broadcasted_iota(jnp.int32, (N_BUCKETS, CHUNK), 1)
            acc = jnp.zeros((N_BUCKETS, CHUNK), F32)
            for bb in range(N_BUCKETS):
                hit = (bk == bb) & valid
                for hd in range(4):
                    part = jnp.sum(jnp.where(hit, dbias_acc[hd], 0.0), axis=-1, keepdims=True)
                    tot = jnp.sum(part, axis=0, keepdims=True)
                    acc = acc + jnp.where((rrow == bb) & (rcol == hd), jnp.broadcast_to(tot, (N_BUCKETS, CHUNK)), 0.0)
            drel_ref[...] = acc
            out_i.wait()
            out_o.wait()

    tile = lambda w: pl.BlockSpec((tm, w), lambda b, j: (b * nt + nt - 1 - j, 0))
    prev_block = pl.BlockSpec((CHUNK, D_MODEL), lambda b, j: (b * bps + jnp.maximum((nt - 1 - j) * bpt - 1, 0), 0))
    per_batch = lambda r, w: pl.BlockSpec((None, r, w), lambda b, j: (b, 0, 0))
    anyspec = pl.BlockSpec(memory_space=pl.ANY)
    grp = (A_GROUPS, CHUNK, CHUNK)
    return pl.pallas_call(
        body, name="layer", grid=(nb, nt),
        out_shape=(jax.ShapeDtypeStruct((t, D_MODEL), F32),
                   jax.ShapeDtypeStruct((nb, MEM_LEN, 2 * MEM_LEN), F32),
                   jax.ShapeDtypeStruct((IN_WIDTH, D_MODEL), F32),
                   jax.ShapeDtypeStruct((D_MODEL, D_MODEL), F32),
                   jax.ShapeDtypeStruct((1, D_MODEL), F32),
                   jax.ShapeDtypeStruct((1, D_MODEL), F32),
                   jax.ShapeDtypeStruct((8, CHUNK), F32),
                   jax.ShapeDtypeStruct(grp, F32),
                   jax.ShapeDtypeStruct((A_GROUPS, CHUNK), F32),
                   jax.ShapeDtypeStruct((1, A_WIDTH), F32),
                   jax.ShapeDtypeStruct((1, A_WIDTH), F32),
                   jax.ShapeDtypeStruct((8, CHUNK), F32),
                   jax.ShapeDtypeStruct((N_BUCKETS, CHUNK), F32)),
        in_specs=[tile(D_MODEL), prev_block, tile(D_MODEL), per_batch(MEM_LEN, 2 * MEM_LEN),
                  _full((4, CHUNK, 2 * CHUNK)),
                  pl.BlockSpec(memory_space=pltpu.SMEM),
                  _full((1, A_WIDTH)), _full((1, A_WIDTH)),
                  _full(grp), _full(grp), _full(grp),
                  _full((1, D_MODEL)), _full((1, D_MODEL)),
                  _full((IN_WIDTH, D_MODEL), single=True), _full((D_MODEL, D_MODEL), single=True),
                  _full((CHUNK, 2 * CHUNK))],
        out_specs=(tile(D_MODEL), per_batch(MEM_LEN, 2 * MEM_LEN), anyspec, anyspec,
                   _full((1, D_MODEL)), _full((1, D_MODEL)), _full((8, CHUNK)),
                   _full(grp), _full((A_GROUPS, CHUNK)), _full((1, A_WIDTH)), _full((1, A_WIDTH)),
                   _full((8, CHUNK)), _full((N_BUCKETS, CHUNK))),
        scratch_shapes=[pltpu.VMEM((IN_WIDTH, D_MODEL), F32), pltpu.VMEM((D_MODEL, D_MODEL), F32),
                        pltpu.VMEM((tm, UV_W), F32), pltpu.VMEM((tm, Z_W), F32),
                        pltpu.VMEM((tm, 512), MM), pltpu.VMEM((tm + CHUNK, 2 * CHUNK), MM),
                        pltpu.VMEM((tm, D_MODEL), MM), pltpu.VMEM((tm, IN_WIDTH), MM),
                        pltpu.VMEM((tm, D_MODEL), F32),
                        pltpu.VMEM((tm, D_MODEL), F32), pltpu.VMEM((tm, 1), F32),
                        pltpu.VMEM((tm, D_MODEL), F32), pltpu.VMEM((tm, D_MODEL), F32)]
                       + [pltpu.VMEM((tm, A_WIDTH), F32) for _ in range(4)]
                       + [pltpu.VMEM((tm, A_WIDTH), MM),
                          pltpu.VMEM((bpt * 4, CHUNK, 2 * CHUNK), F32),
                          pltpu.VMEM((bpt * 4, CHUNK, CHUNK), F32),
                          pltpu.VMEM((4, tm, MEM_LEN), F32),
                          pltpu.VMEM((bpt * 2, 2 * CHUNK, CHUNK), MM),
                          pltpu.VMEM((bpt * 2, 2 * CHUNK, CHUNK), MM),
                          pltpu.VMEM((tm + CHUNK, 2 * CHUNK), F32),
                          pltpu.VMEM((4, CHUNK, 2 * CHUNK), F32),
                          pltpu.VMEM(grp, F32),
                          pltpu.VMEM((8, CHUNK), F32),
                          pltpu.SemaphoreType.DMA((2,))],
        compiler_params=_params(dimension_semantics=("arbitrary", "arbitrary")),
    )(x2, x2, tgt2, mkv3, bias, sinks, vg, vb, wt, wtt, bcol, g1, g2, w_in_t, w_o, buckets)


class _ShardReduce:
    def __init__(self, pos, g, bufs, sems):
        self.x, self.y, self.c = pos
        self.g = g
        self.own, self.rcv, self.sbuf, self.rbuf, self.cbuf = bufs
        self.ld, self.sa, self.ra, self.sb, self.rb = sems
        self.nrow = g.shape[1]
        self.here = (self.x, self.y, self.c)
        self.sib = (self.x, self.y, 1 - self.c)
        self.first, self.second, self.far = _route(*pos)

    def _load(self, q):
        return pltpu.make_async_copy(self.g.at[2 * q + self.c], self.own.at[q], self.ld.at[q])

    def _to_sib(self, q, to):
        return _remote(self.g.at[2 * q + 1 - self.c], self.rcv.at[q], self.sa.at[q], self.ra.at[q], to)

    def _send(self, k, to):
        dst = self.cbuf.at[0] if k == 1 else self.rbuf.at[0 if k == 0 else 1]
        return _remote(self.sbuf.at[k], dst, self.sb.at[k], self.rb.at[k], to)

    def _stage(self, k, which, extra=None):
        def cast(r):
            v = self.rcv[which, r, :]
            if extra is not None:
                v = v + extra[0, r, :].astype(F32)
            self.sbuf[k, r, :] = v.astype(BF16)

        _rows_loop(self.nrow, cast)

    @staticmethod
    def _q(chip):
        return 2 * chip[0] + chip[1]

    def start(self):
        for q in range(4):
            self._load(q).start()
            self._to_sib(q, self.sib).start()

    def mid(self):
        for q in range(4):
            self._load(q).wait()
            self._to_sib(q, self.here).wait_recv()

        def add(r):
            for q in range(4):
                self.rcv[q, r, :] = self.rcv[q, r, :] + self.own[q, r, :]

        _rows_loop(self.nrow, add)
        to_first = (self.first[0], self.first[1], self.c)
        self._stage(0, self._q(self.first))
        self._send(0, to_first).start()
        self._stage(1, self._q(self.far))
        self._send(1, to_first).start()

    def pass_on(self):
        self._send(1, self.here).wait_recv()
        self._stage(2, self._q(self.second), extra=self.cbuf)
        self._send(2, (self.second[0], self.second[1], self.c)).start()

    def finish(self, out):
        self._send(0, self.here).wait_recv()
        self._send(2, self.here).wait_recv()
        which = 2 * self.x + self.y

        def tot(r):
            out[r, :] = (self.rcv[which, r, :] + self.rbuf[0, r, :].astype(F32)) + self.rbuf[1, r, :].astype(F32)

        _rows_loop(self.nrow, tot)
        for q in range(4):
            self._to_sib(q, self.sib).wait_send()
        to_first = (self.first[0], self.first[1], self.c)
        self._send(0, to_first).wait_send()
        self._send(1, to_first).wait_send()
        self._send(2, (self.second[0], self.second[1], self.c)).wait_send()


def _reduce_scratch(shape):
    return [pltpu.VMEM((4,) + shape, F32), pltpu.VMEM((4,) + shape, F32),
            pltpu.VMEM((3,) + shape, BF16), pltpu.VMEM((2,) + shape, BF16), pltpu.VMEM((1,) + shape, BF16),
            pltpu.SemaphoreType.DMA((4,)), pltpu.SemaphoreType.DMA((4,)), pltpu.SemaphoreType.DMA((4,)),
            pltpu.SemaphoreType.DMA((3,)), pltpu.SemaphoreType.DMA((3,))]


_N_RED = 10

_S_LAYOUT = (((1, D_MODEL), 0), ((1, D_MODEL), 8), ((1, D_MODEL), 16),
             ((1, A_WIDTH), 24), ((1, A_WIDTH), 28), ((A_GROUPS, CHUNK), 32),
             ((1, 4), 36), ((N_BUCKETS, 4), 40),
             ((A_GROUPS * CHUNK, CHUNK), 72))
_LOSS_ROW = 37
_W_SP_ROW = _S_LAYOUT[-1][1]
_S_ROWS = _W_SP_ROW + A_GROUPS * CHUNK
_N_SMALL = len(_S_LAYOUT)


def _pack_rows(dst, refs):
    for (shp, r0), ref in zip(_S_LAYOUT, refs):
        if shp[0] == 1 and shp[1] >= CHUNK:
            for i in range(shp[1] // CHUNK):
                dst[r0 + i:r0 + i + 1, :] = ref[:, i * CHUNK:(i + 1) * CHUNK]
        elif ref.shape[-1] == CHUNK:
            dst[r0:r0 + shp[0], :] = ref[0:shp[0], :]
        else:
            dst[r0:r0 + shp[0], 0:shp[1]] = ref[...]


def _unpack_rows(src, refs):
    for (shp, r0), ref in zip(_S_LAYOUT, refs):
        if shp[0] == 1 and shp[1] >= CHUNK:
            for i in range(shp[1] // CHUNK):
                ref[:, i * CHUNK:(i + 1) * CHUNK] = src[r0 + i:r0 + i + 1, :]
        elif shp[1] == CHUNK:
            ref[...] = src[r0:r0 + shp[0], :]
        else:
            if tuple(ref.shape) == (shp[1], shp[0]):
                ref[...] = src[r0:r0 + CHUNK, :].T[0:shp[1], 0:shp[0]]
            else:
                ref[...] = src[r0:r0 + shp[0], 0:shp[1]]


_MEM_G = 2


def _greduce(ga, gb, dmkv, mem2, gm, w_mkv, small_g, loss_p):
    shp_c = (SHARD_O, 2 * MEM_LEN)
    shapes = (shp_c, gb.shape[1:], ga.shape[1:])
    rs = _S_ROWS

    def body(*refs):
        it = iter(refs)
        take = lambda n: [next(it) for _ in range(n)]
        gb_ref, ga_ref, d_ref, m_ref, gm_ref, wm_ref = take(6)
        sg_refs = take(_N_SMALL - 1)
        loss_ref, = take(1)
        oc, ob, oa, ogs = take(4)
        red = take(3 * _N_RED)
        gs_ref, rs_a, rs_b, rs_w, gc_ref, dgm_ref = take(6)
        ssem_a, rsem_a, ssem_b, rsem_b = take(4)

        pos = _position()
        x, y, cc = pos
        myq = 2 * x + y
        here, sib = (x, y, cc), (x, y, 1 - cc)
        chips = _other_chips(x, y)
        reducers = [_ShardReduce(pos, g, red[k * _N_RED:k * _N_RED + 5], red[k * _N_RED + 5:(k + 1) * _N_RED])
                    for k, g in enumerate((gc_ref, gb_ref, ga_ref))]
        for rd in reducers[1:]:
            rd.start()

        xf = m_ref[...]
        nm = xf * _rms(xf)
        hm = (nm * gm_ref[...]).astype(MM)
        d = d_ref[...].astype(MM)
        for o in range(N_DEV):
            gc_ref[o] = _dot_tn(hm[:, o * SHARD_O:(o + 1) * SHARD_O], d)
        dgm_ref[...] = jnp.sum(_dot_nt(d, wm_ref[...]) * nm, axis=0, keepdims=True)
        reducers[0].start()

        gs_ref[...] = jnp.zeros_like(gs_ref)
        _pack_rows(gs_ref, sg_refs[:_MEM_G] + [dgm_ref] + sg_refs[_MEM_G:])
        gs_ref[_LOSS_ROW:_LOSS_ROW + 1, :] = loss_ref[0:1, :]
        small_a = _remote(gs_ref, rs_a, ssem_a, rsem_a, sib)
        small_a.start()

        _remote(gs_ref, rs_a, ssem_a, rsem_a, here).wait_recv()
        rs_b[myq] = gs_ref[0:_W_SP_ROW, :] + rs_a[0:_W_SP_ROW, :]
        rs_w[myq] = (gs_ref[_W_SP_ROW:rs, :] + rs_a[_W_SP_ROW:rs, :]).astype(BF16)
        small_b = []
        for j, chip in enumerate(chips):
            to = (chip[0], chip[1], cc)
            small_b.append(_remote(rs_b.at[myq], rs_b.at[myq], ssem_b.at[0, j], rsem_b.at[0, j], to))
            small_b.append(_remote(rs_w.at[myq], rs_w.at[myq], ssem_b.at[1, j], rsem_b.at[1, j], to))
        for cp in small_b:
            cp.start()
        late_last = reducers[1:] + reducers[:1]
        for rd in late_last:
            rd.mid()
        for rd in late_last:
            rd.pass_on()

        for j in range(3):
            _remote(rs_b.at[myq], rs_b.at[myq], ssem_b.at[0, j], rsem_b.at[0, j], here).wait_recv()
            _remote(rs_w.at[myq], rs_w.at[myq], ssem_b.at[1, j], rsem_b.at[1, j], here).wait_recv()
        ogs[0:_W_SP_ROW, :] = ((rs_b[0] + rs_b[1]) + rs_b[2]) + rs_b[3]

        def tot_w(r):
            w = [rs_w[q, r, :].astype(F32) for q in range(4)]
            ogs[pl.ds(pl.multiple_of(_W_SP_ROW + r.start, 8), _ROWS), :] = ((w[0] + w[1]) + w[2]) + w[3]

        _rows_loop(rs - _W_SP_ROW, tot_w)
        for rd, out in zip(late_last, (ob, oa, oc)):
            rd.finish(out)
        small_a.wait_send()
        for cp in small_b:
            cp.wait_send()

    vm = pl.BlockSpec(memory_space=pltpu.VMEM)
    anyspec = pl.BlockSpec(memory_space=pl.ANY)
    scratch = []
    for shp in shapes:
        scratch += _reduce_scratch(shp)
    scratch += [pltpu.VMEM((rs, CHUNK), F32), pltpu.VMEM((rs, CHUNK), F32),
                pltpu.VMEM((4, _W_SP_ROW, CHUNK), F32), pltpu.VMEM((4, rs - _W_SP_ROW, CHUNK), BF16),
                pltpu.VMEM((N_DEV,) + shp_c, F32), pltpu.VMEM((1, D_MODEL), F32),
                pltpu.SemaphoreType.DMA, pltpu.SemaphoreType.DMA,
                pltpu.SemaphoreType.DMA((2, 3)), pltpu.SemaphoreType.DMA((2, 3))]
    tc, tb, ta, ts = pl.pallas_call(
        body, name="greduce",
        out_shape=tuple([jax.ShapeDtypeStruct(shp, F32) for shp in shapes] + [jax.ShapeDtypeStruct((rs, CHUNK), F32)]),
        in_specs=[anyspec] * 2 + [vm] * (4 + _N_SMALL),
        out_specs=(vm, vm, vm, vm),
        scratch_shapes=scratch,
        compiler_params=_params(),
    )(gb, ga, dmkv, mem2, gm, w_mkv, *small_g, loss_p)
    return ta, tb, tc, ts


def _adamw(w, g, m, v):
    m = ADAM_B1 * m + (1.0 - ADAM_B1) * g
    v = ADAM_B2 * v + (1.0 - ADAM_B2) * (g * g)
    m_hat = m / (1.0 - ADAM_B1 ** ADAM_STEP)
    v_hat = v / (1.0 - ADAM_B2 ** ADAM_STEP)
    delta = -ADAM_LR * (m_hat / (jnp.sqrt(v_hat) + ADAM_EPS) + ADAM_WD * w)
    return delta, m, v


def _update(ta, tb, tc, ts, big_wmv, small_wmv):
    shapes = (ta.shape, tb.shape, tc.shape)
    rs = _S_ROWS
    small_shapes = [tuple(a.shape) for a in small_wmv[0]]

    def body(*refs):
        it = iter(refs)
        take = lambda n: [next(it) for _ in range(n)]
        ga_ref, gb_ref, gc_ref, gs_ref = take(4)
        wa, ma, va, wb, mb, vb_, wc, mc, vc = take(9)
        sw_refs, sm_refs, sv_refs = take(_N_SMALL), take(_N_SMALL), take(_N_SMALL)
        oga, oda, oma, ova, ogb, odb, omb, ovb, ogc, odc, omc, ovc = take(12)
        so_refs = [take(_N_SMALL) for _ in range(4)]
        loss_out, = take(1)
        ws, ms, vs, ods, oms, ovs = take(6)

        for buf in (ws, ms, vs):
            buf[...] = jnp.zeros_like(buf)
        _pack_rows(ws, sw_refs)
        _pack_rows(ms, sm_refs)
        _pack_rows(vs, sv_refs)

        big = ((ga_ref, wa, ma, va, oga, oda, oma, ova), (gb_ref, wb, mb, vb_, ogb, odb, omb, ovb),
               (gc_ref, wc, mc, vc, ogc, odc, omc, ovc))
        for arr in range(3):
            g_r, w_r, m_r, v_r, og, od, om, ov = big[arr]

            def upd(r, g_r=g_r, w_r=w_r, m_r=m_r, v_r=v_r, og=og, od=od, om=om, ov=ov):
                g = g_r[r, :]
                d, m, v = _adamw(w_r[r, :], g, m_r[r, :], v_r[r, :])
                og[r, :] = g
                od[r, :] = d
                om[r, :] = m
                ov[r, :] = v

            _rows_loop(shapes[arr][0], upd)

        def upd_s(i, _):
            r = pl.ds(pl.multiple_of(i * 8, 8), 8)
            d, m, v = _adamw(ws[r, :], gs_ref[r, :], ms[r, :], vs[r, :])
            ods[r, :] = d
            oms[r, :] = m
            ovs[r, :] = v
            return 0

        lax.fori_loop(0, rs // 8, upd_s, 0)
        for k, buf in enumerate((gs_ref, ods, oms, ovs)):
            _unpack_rows(buf, so_refs[k])
        loss_out[...] = gs_ref[_LOSS_ROW:_LOSS_ROW + 1, 0:1]

    vm = pl.BlockSpec(memory_space=pltpu.VMEM)
    big_out = []
    for shp in shapes:
        big_out += [jax.ShapeDtypeStruct(shp, F32)] * 4
    small_out = [jax.ShapeDtypeStruct(shp[::-1] if shp == (N_BUCKETS, 4) else shp, F32) for shp in small_shapes] * 4
    out_shape = tuple(big_out + small_out + [jax.ShapeDtypeStruct((1, 1), F32)])
    n_in = 4 + 9 + 3 * _N_SMALL
    return pl.pallas_call(
        body, name="update",
        out_shape=out_shape,
        in_specs=[vm] * n_in,
        out_specs=tuple([vm] * len(out_shape)),
        scratch_shapes=[pltpu.VMEM((rs, CHUNK), F32) for _ in range(6)],
        compiler_params=_params(),
    )(ta, tb, tc, ts, *big_wmv, *small_wmv[0], *small_wmv[1], *small_wmv[2])


def kernel(x, mem, pre_norm_g, post_norm_g, mem_norm_g, w_in, w_mem_kv, v_norm_g, v_norm_b, w_spatial, b_spatial, attn_sinks, rel_bias, w_out, loss_target, m_pre_norm_g, m_post_norm_g, m_mem_norm_g, m_w_in, m_w_mem_kv, m_v_norm_g, m_v_norm_b, m_w_spatial, m_b_spatial, m_attn_sinks, m_rel_bias, m_w_out, v_pre_norm_g, v_post_norm_g, v_mem_norm_g, v_w_in, v_w_mem_kv, v_v_norm_g, v_v_norm_b, v_w_spatial, v_b_spatial, v_attn_sinks, v_rel_bias, v_w_out):
    sh_a = (w_in[0].T, m_w_in[0].T, v_w_in[0].T)
    sh_b = (w_out[0], m_w_out[0], v_w_out[0])
    sh_c = (w_mem_kv[0], m_w_mem_kv[0], v_w_mem_kv[0])
    nb, s, _ = x.shape
    t = nb * s
    x2 = x.reshape(t, D_MODEL)
    tgt2 = loss_target.reshape(t, D_MODEL)
    mem2 = mem.reshape(nb * MEM_LEN, D_MODEL)
    buckets = jnp.asarray(_t5_buckets())

    wa, wb, wc, bias, wt, wtt, bcol, mkv = _wgather(sh_a[0], sh_b[0], sh_c[0], rel_bias, w_spatial[0], b_spatial[0],
                                                    buckets, mem2, mem_norm_g)
    w_mkv = wc.reshape(D_MODEL, 2 * MEM_LEN)
    gx, dmkv, dwi, dwo, dg1, dg2, loss_p, dwsp, dbs, dvg, dvb, dsink, drel = _layer(
        x2, tgt2, mkv.reshape(nb, MEM_LEN, 2 * MEM_LEN), bias, attn_sinks.reshape(4), v_norm_g, v_norm_b, wt, wtt, bcol,
        pre_norm_g, post_norm_g, wa.reshape(IN_WIDTH, D_MODEL), wb.reshape(D_MODEL, D_MODEL), buckets,
        nb, s, min(256, s))
    gx = gx.reshape(nb, s, D_MODEL)
    small_grads = [dg1, dg2, dvg, dvb, dbs, dsink, drel, dwsp.reshape(A_GROUPS * CHUNK, CHUNK)]

    small_names = ["pre_norm_g", "post_norm_g", "mem_norm_g", "v_norm_g", "v_norm_b", "b_spatial", "attn_sinks",
                   "rel_bias", "w_spatial"]
    given = dict(pre_norm_g=(pre_norm_g, m_pre_norm_g, v_pre_norm_g), post_norm_g=(post_norm_g, m_post_norm_g, v_post_norm_g),
                 mem_norm_g=(mem_norm_g, m_mem_norm_g, v_mem_norm_g), v_norm_g=(v_norm_g, m_v_norm_g, v_v_norm_g),
                 v_norm_b=(v_norm_b, m_v_norm_b, v_v_norm_b), b_spatial=(b_spatial, m_b_spatial, v_b_spatial),
                 attn_sinks=(attn_sinks, m_attn_sinks, v_attn_sinks), rel_bias=(rel_bias, m_rel_bias, v_rel_bias),
                 w_spatial=(w_spatial, m_w_spatial, v_w_spatial))
    small_wmv = [[given[n][k].reshape(shp) for n, (shp, _) in zip(small_names, _S_LAYOUT)] for k in range(3)]

    ta, tb, tc, ts = _greduce(dwi.reshape(N_DEV, SHARD_IN, D_MODEL), dwo.reshape(N_DEV, SHARD_O, D_MODEL),
                              dmkv.reshape(nb * MEM_LEN, 2 * MEM_LEN), mem2, mem_norm_g, w_mkv, small_grads, loss_p)
    outs = _update(ta, tb, tc, ts, (*sh_a, *sh_b, *sh_c), small_wmv)
    ra, rb, rc = outs[0:4], outs[4:8], outs[8:12]
    loss = outs[12 + 4 * _N_SMALL].reshape(())

    res = {}
    for k, kind in enumerate(("grad", "delta", "new_m", "new_v")):
        res[kind, "w_in"] = ra[k].T[None]
        res[kind, "w_out"] = rb[k][None]
        res[kind, "w_mem_kv"] = rc[k][None]
        for i, n in enumerate(small_names):
            o = outs[12 + k * _N_SMALL + i]
            res[kind, n] = o.T if n == "rel_bias" else o.reshape(given[n][0].shape)
    order = ["pre_norm_g", "post_norm_g", "mem_norm_g", "w_in", "w_mem_kv", "v_norm_g", "v_norm_b", "w_spatial",
             "b_spatial", "attn_sinks", "rel_bias", "w_out"]
    flat = [res[kind, n] for kind in ("grad", "delta", "new_m", "new_v") for n in order]
    return (loss, gx, *flat)
```

```python
import numpy as np
import jax
import jax.numpy as jnp
from jax import lax
from jax.experimental import pallas as pl
from jax.experimental.pallas import tpu as pltpu

F32 = jnp.float32
BF16 = jnp.bfloat16
MM = jnp.bfloat16

D_MODEL = 1024
CHUNK = 128
A_GROUPS = 4
A_WIDTH = 512
UV_W = 1024
QKV_W = 768
Z_W = 1024
IN_WIDTH = UV_W + QKV_W + Z_W
MEM_LEN = 256
N_BUCKETS = 32
MAX_DISTANCE = 128
EPS = 1e-6
NEG = -1e30
SCALE = 0.125
N_DEV = 8
SHARD_IN = IN_WIDTH // N_DEV
SHARD_O = D_MODEL // N_DEV

SQ_COL, SK_COL, SV_COL, MQ_COL, Z_COL = UV_W, UV_W + 256, UV_W + 384, UV_W + 512, UV_W + QKV_W
YB_OFF, YC_OFF = 512, 768

ADAM_LR = 0.001
ADAM_B1 = 0.9
ADAM_B2 = 0.999
ADAM_EPS = 1e-08
ADAM_WD = 0.01
ADAM_STEP = 10

VMEM_LIMIT = 60 * 1024 * 1024

_GELU_C = 0.7978845608028654
_GELU_A = 0.044715

MESH = pl.DeviceIdType.MESH
_ROWS = 32


def _dot(a, b):
    return lax.dot_general(a, b, (((1,), (0,)), ((), ())), preferred_element_type=F32)


def _dot_nt(a, b):
    return lax.dot_general(a, b, (((1,), (1,)), ((), ())), preferred_element_type=F32)


def _dot_tn(a, b):
    return lax.dot_general(a, b, (((0,), (0,)), ((), ())), preferred_element_type=F32)


def _gelu_and_grad(x):
    x2 = x * x
    t = jnp.tanh(_GELU_C * (x + _GELU_A * x * x2))
    g = 0.5 * x * (1.0 + t)
    dg = 0.5 * (1.0 + t) + 0.5 * x * (1.0 - t * t) * (_GELU_C * (1.0 + 3.0 * _GELU_A * x2))
    return g, dg


def _t5_buckets():
    qi = np.arange(CHUNK)[:, None]
    kj = np.arange(2 * CHUNK)[None, :]
    n = np.maximum(qi + CHUNK - kj, 0)
    max_exact = N_BUCKETS // 2
    large = max_exact + (np.log(np.maximum(n, 1) / max_exact) / np.log(MAX_DISTANCE / max_exact)
                         * (N_BUCKETS - max_exact)).astype(np.int32)
    large = np.minimum(large, N_BUCKETS - 1)
    return np.where(n < max_exact, n, large).astype(np.int32)


def _params(**kw):
    return pltpu.CompilerParams(vmem_limit_bytes=VMEM_LIMIT, **kw)


def _full(shape, single=False):
    nd = len(shape)
    if single:
        return pl.BlockSpec(shape, lambda *_: (0,) * nd, pipeline_mode=pl.Buffered(1))
    return pl.BlockSpec(shape, lambda *_: (0,) * nd)


def _window_valid():
    qi = lax.broadcasted_iota(jnp.int32, (CHUNK, 2 * CHUNK), 0)
    kj = lax.broadcasted_iota(jnp.int32, (CHUNK, 2 * CHUNK), 1)
    dist = qi + CHUNK - kj
    return (dist >= 0) & (dist < CHUNK)


def _position():
    return lax.axis_index("x"), lax.axis_index("y"), lax.axis_index("c")


def _other_chips(x, y):
    return [(1 - x, y), (x, 1 - y), (1 - x, 1 - y)]


def _route(x, y, c):
    first = (x * c + (1 - x) * (1 - c), y * (1 - c) + (1 - y) * c)
    second = (x * (1 - c) + (1 - x) * c, y * c + (1 - y) * (1 - c))
    return first, second, (1 - x, 1 - y)


def _remote(src, dst, ssem, rsem, to):
    return pltpu.make_async_remote_copy(src_ref=src, dst_ref=dst, send_sem=ssem, recv_sem=rsem,
                                        device_id=to, device_id_type=MESH)


def _rows_loop(nrow, fn):
    def step(i, _):
        fn(pl.ds(pl.multiple_of(i * _ROWS, _ROWS), _ROWS))
        return 0

    lax.fori_loop(0, nrow // _ROWS, step, 0)


class _Gather:
    def __init__(self, pos, out, ssem, rsem):
        self.x, self.y, self.c = pos
        self.out, self.ssem, self.rsem = out, ssem, rsem
        self.me = 4 * self.x + 2 * self.y + self.c
        self.here = (self.x, self.y, self.c)
        self.sib = (self.x, self.y, 1 - self.c)
        self.first, self.second, self.far = _route(*pos)

    def _copy(self, k, blk, to):
        r = self.out.at[blk]
        return _remote(r, r, self.ssem.at[k], self.rsem.at[k], to)

    def _idx(self, chip, core):
        return 4 * chip[0] + 2 * chip[1] + core

    def _on(self, chip):
        return (chip[0], chip[1], self.c)

    def start(self):
        self._copy(0, self.me, self.sib).start()
        self._copy(1, self.me, self._on(self.first)).start()
        self._copy(2, self.me, self._on(self.second)).start()

    def forward(self):
        c = self.c
        self._copy(1, self._idx(self.first, c), self.here).wait_recv()
        self._copy(3, self._idx(self.first, c), self._on(self.second)).start()
        self._copy(4, self._idx(self.first, c), self.sib).start()
        self._copy(2, self._idx(self.second, c), self.here).wait_recv()
        self._copy(5, self._idx(self.second, c), self.sib).start()
        self._copy(3, self._idx(self.far, c), self.here).wait_recv()
        self._copy(6, self._idx(self.far, c), self.sib).start()

    def finish(self):
        c = self.c
        self._copy(0, self._idx((self.x, self.y), 1 - c), self.here).wait_recv()
        for k, chip in ((4, self.second), (5, self.first), (6, self.far)):
            self._copy(k, self._idx(chip, 1 - c), self.here).wait_recv()
        self._copy(0, self.me, self.sib).wait_send()
        self._copy(1, self.me, self._on(self.first)).wait_send()
        self._copy(2, self.me, self._on(self.second)).wait_send()
        self._copy(3, self._idx(self.first, c), self._on(self.second)).wait_send()
        for k, chip in ((4, self.first), (5, self.second), (6, self.far)):
            self._copy(k, self._idx(chip, c), self.sib).wait_send()


def _prep_tables(rb_ref, w_ref, b_ref, bk_ref, bias_ref, wt_ref, wtt_ref, bcol_ref):
    valid = _window_valid()
    bk = bk_ref[...]
    acc = [jnp.full((CHUNK, 2 * CHUNK), NEG, F32) for _ in range(4)]
    for b in range(N_BUCKETS):
        hit = (bk == b) & valid
        for h in range(4):
            acc[h] = jnp.where(hit, rb_ref[b, h], acc[h])
    for h in range(4):
        bias_ref[h] = acc[h]
    r = lax.broadcasted_iota(jnp.int32, (CHUNK, CHUNK), 0)
    c = lax.broadcasted_iota(jnp.int32, (CHUNK, CHUNK), 1)
    for g in range(A_GROUPS):
        w = jnp.where(r >= c, w_ref[g], 0.0)
        wt_ref[g] = w.astype(MM)
        wtt_ref[g] = w.T.astype(MM)
        bcol_ref[g] = jnp.broadcast_to(b_ref[g:g + 1, :], (CHUNK, CHUNK)).T


def _wgather(a, b, c, rel_bias, w_sp, b_sp, buckets, mem2, gm):
    tmem = mem2.shape[0]

    def body(a_ref, b_ref, c_ref, rb_ref, w_ref, bsp_ref, bk_ref, m_ref, gm_ref,
             oa, ob, oc, bias_ref, wt_ref, wtt_ref, bcol_ref, mkv_ref, ssem, rsem):
        pos = _position()
        me = 4 * pos[0] + 2 * pos[1] + pos[2]
        gathers = []
        for k, (src, out) in enumerate(((c_ref, oc), (b_ref, ob), (a_ref, oa))):
            out[me] = src[...].astype(BF16)
            g = _Gather(pos, out, ssem.at[k], rsem.at[k])
            g.start()
            gathers.append(g)
        _prep_tables(rb_ref, w_ref, bsp_ref, bk_ref, bias_ref, wt_ref, wtt_ref, bcol_ref)
        for g in gathers:
            g.forward()
        gathers[0].finish()
        xf = m_ref[...]
        hm = (xf * _rms(xf) * gm_ref[...]).astype(MM)
        acc = jnp.zeros((tmem, 2 * MEM_LEN), F32)
        for d in range(N_DEV):
            acc = acc + _dot(hm[:, d * SHARD_O:(d + 1) * SHARD_O], oc[d])
        mkv_ref[...] = acc.astype(MM)
        for g in gathers[1:]:
            g.finish()

    vm = pl.BlockSpec(memory_space=pltpu.VMEM)
    grp = (A_GROUPS, CHUNK, CHUNK)
    return pl.pallas_call(
        body, name="wgather",
        out_shape=(jax.ShapeDtypeStruct((N_DEV,) + a.shape, BF16),
                   jax.ShapeDtypeStruct((N_DEV,) + b.shape, BF16),
                   jax.ShapeDtypeStruct((N_DEV,) + c.shape, BF16),
                   jax.ShapeDtypeStruct((4, CHUNK, 2 * CHUNK), F32),
                   jax.ShapeDtypeStruct(grp, MM), jax.ShapeDtypeStruct(grp, MM), jax.ShapeDtypeStruct(grp, F32),
                   jax.ShapeDtypeStruct((tmem, 2 * MEM_LEN), MM)),
        in_specs=[vm, vm, vm, pl.BlockSpec(memory_space=pltpu.SMEM), vm, vm, vm, vm, vm],
        out_specs=tuple([vm] * 8),
        scratch_shapes=[pltpu.SemaphoreType.DMA((3, 7)), pltpu.SemaphoreType.DMA((3, 7))],
        compiler_params=_params(),
    )(a, b, c, rel_bias, w_sp, b_sp, buckets, mem2, gm)


def _half_masks(rows):
    lane = lax.broadcasted_iota(jnp.int32, (rows, CHUNK), 1)
    return lane < 64


def _dup_heads(band):
    b32 = band.astype(F32)
    rolled = pltpu.roll(b32, 64, 1)
    lo = _half_masks(band.shape[0])
    return (jnp.where(lo, b32, rolled).astype(MM), jnp.where(lo, rolled, b32).astype(MM))


def _swa_probs(qk, bias_h, sink_h, first_add):
    s = qk * SCALE + bias_h + first_add
    m = jnp.maximum(jnp.max(s, axis=-1, keepdims=True), sink_h)
    p = jnp.exp(s - m)
    es = jnp.exp(sink_h - m)
    inv = 1.0 / (jnp.sum(p, axis=-1, keepdims=True) + es)
    return p * inv, es * inv


def _softmax(s):
    m = jnp.max(s, axis=-1, keepdims=True)
    p = jnp.exp(s - m)
    return p * (1.0 / jnp.sum(p, axis=-1, keepdims=True))


def _first_block_mask(n):
    col = lax.broadcasted_iota(jnp.int32, (CHUNK, 2 * CHUNK), 1)
    return jnp.where((col < CHUNK) & (n == 0), NEG, 0.0)


def _rms(xf):
    return lax.rsqrt(jnp.mean(xf * xf, axis=-1, keepdims=True) + EPS)


def _layer(x2, tgt2, mkv3, bias, sinks, vg, vb, wt, wtt, bcol, g1, g2, w_in_t, w_o, buckets, nb, s, tm):
    nt = s // tm
    bpt = tm // CHUNK
    bps = s // CHUNK
    t = nb * s
    last_step = nb * nt - 1

    def tile_at(step):
        return (step // nt) * nt + nt - 1 - step % nt

    def block_before(step):
        return (step // nt) * bps + jnp.maximum((nt - 1 - step % nt) * bpt - 1, 0)

    def body(x_ref, xp_ref, xn_ref, xpn_ref, t_ref, mkv_ref, bias_ref, sink_ref, vg_ref, vb_ref,
             wt_ref, wtt_ref, bcol_ref, g1_ref, g2_ref, wi_ref, wo_ref, bk_ref,
             gx_ref, dmkv_ref, dwi_hbm, dwo_hbm, dg1_ref, dg2_ref, loss_ref, dwsp_ref, dbs_ref,
             dvg_ref, dvb_ref, dsink_ref, drel_ref,
             acc_i, acc_o, uv_s, z_s, q_s, kv_s, h_s, hp_s, dp_s, dxo_s, dh_s, r_s,
             ycat, dyc, u_s, gu_s, gv_s, xh_s, vc_s, pb_s, ps_s, pc_s, kd_s, vd_s,
             dkv_acc, dbias_acc, dsv_acc, dsink_acc, sems):
        b, j = pl.program_id(0), pl.program_id(1)
        jt = nt - 1 - j
        step = b * nt + j
        g1v = g1_ref[...]
        NOW, NEXT = 0, 1

        def pre_norm(x_tile, x_before):
            xf = x_tile[...]
            r_s[NEXT] = _rms(xf)
            h_s[NEXT] = (xf * r_s[NEXT] * g1v).astype(MM)
            xp = x_before[...]
            hp_s[...] = (xp * _rms(xp) * g1v).astype(MM)

        def project_z():
            z_s[...] = _dot_nt(h_s[NEXT], wi_ref[Z_COL:IN_WIDTH, :])

        def project_uv():
            uv_s[...] = _dot_nt(h_s[NEXT], wi_ref[0:UV_W, :])

        @pl.when(step == 0)
        def _():
            for ref in (acc_i, acc_o, dg1_ref, dg2_ref, loss_ref, dwsp_ref, dvg_ref, dvb_ref,
                        dbias_acc, dsv_acc, dsink_acc):
                ref[...] = jnp.zeros_like(ref)
            pre_norm(x_ref, xp_ref)
            project_z()
            project_uv()

        r_s[NOW] = r_s[NEXT]
        h = h_s[NEXT]
        h_s[NOW] = h
        hp = hp_s[...]

        @pl.when(j == 0)
        def _():
            dmkv_ref[...] = jnp.zeros_like(dmkv_ref)
            dkv_acc[...] = jnp.zeros_like(dkv_acc)

        carry = dkv_acc[0:CHUNK, :]
        dkv_acc[...] = jnp.zeros_like(dkv_acc)
        dkv_acc[tm:tm + CHUNK, :] = carry

        lo = _half_masks(CHUNK)
        lob = _half_masks(2 * CHUNK)
        lot = _half_masks(tm)

        qkv = _dot_nt(h, wi_ref[SQ_COL:Z_COL, :])
        q_s[:, 0:256] = qkv[:, 0:256].astype(MM)
        q_s[:, 256:512] = qkv[:, 512:768].astype(MM)
        kv_s[CHUNK:CHUNK + tm, :] = qkv[:, 256:512].astype(MM)
        kv_s[0:CHUNK, :] = _dot_nt(hp, wi_ref[SK_COL:MQ_COL, :]).astype(MM)

        b_qk, b_pb = [], []
        for blk in range(bpt):
            r0 = blk * CHUNK
            rows = slice(r0, r0 + CHUNK)
            for g in range(A_GROUPS):
                cg = slice(g * CHUNK, (g + 1) * CHUNK)
                u, gu = _gelu_and_grad(uv_s[rows, cg])
                v, gv = _gelu_and_grad(uv_s[rows, A_WIDTH + g * CHUNK:A_WIDTH + (g + 1) * CHUNK])
                mu = jnp.mean(v, axis=-1, keepdims=True)
                xc = v - mu
                rstd = lax.rsqrt(jnp.mean(xc * xc, axis=-1, keepdims=True) + EPS)
                xhat = xc * rstd
                vc = (xhat * vg_ref[:, cg] + vb_ref[:, cg]).astype(MM)
                sv = _dot(wt_ref[g], vc) + bcol_ref[g]
                u_s[rows, cg] = u
                gu_s[rows, cg] = sv * gu
                gv_s[rows, cg] = rstd * gv
                xh_s[rows, cg] = xhat
                vc_s[rows, cg] = vc
                ycat[rows, cg] = u * sv
            kd = _dup_heads(kv_s[r0:r0 + 2 * CHUNK, 0:CHUNK])
            vd = _dup_heads(kv_s[r0:r0 + 2 * CHUNK, CHUNK:2 * CHUNK])
            for kvh in range(2):
                kd_s[blk * 2 + kvh] = kd[kvh]
                vd_s[blk * 2 + kvh] = vd[kvh]
                q128 = q_s[rows, kvh * CHUNK:(kvh + 1) * CHUNK].astype(F32)
                for gi in range(2):
                    qsel = jnp.where(lo if gi == 0 else ~lo, q128, 0.0).astype(MM)
                    b_qk.append(_dot_nt(qsel, kd[kvh]))
        qks, pcs = [], []
        for g in range(2):
            q128 = q_s[:, 256 + g * CHUNK:256 + (g + 1) * CHUNK].astype(F32)
            for hh in range(2):
                qsel = jnp.where(lot if hh == 0 else ~lot, q128, 0.0).astype(MM)
                qks.append(_dot_nt(qsel, mkv_ref[:, g * CHUNK:(g + 1) * CHUNK]))
        for blk in range(bpt):
            first_add = _first_block_mask(jt * bpt + blk)
            for hd in range(4):
                probs, ps = _swa_probs(b_qk[blk * 4 + hd], bias_ref[hd], sink_ref[hd], first_add)
                pb_s[blk * 4 + hd] = probs
                ps_s[blk * 4 + hd] = jnp.broadcast_to(ps, (CHUNK, CHUNK))
                b_pb.append(probs.astype(MM))
        for hd in range(4):
            probs = _softmax(qks[hd] * SCALE)
            pc_s[hd] = probs
            pcs.append(probs.astype(MM))
        for blk in range(bpt):
            rows = slice(blk * CHUNK, (blk + 1) * CHUNK)
            outs = [_dot(b_pb[blk * 4 + hd], vd_s[blk * 2 + hd // 2]) for hd in range(4)]
            for kvh in range(2):
                ycat[rows, YB_OFF + kvh * CHUNK:YB_OFF + (kvh + 1) * CHUNK] = jnp.where(
                    lo, outs[2 * kvh], outs[2 * kvh + 1])
        outs = [_dot(pcs[hd], mkv_ref[:, MEM_LEN + (hd // 2) * CHUNK:MEM_LEN + (hd // 2 + 1) * CHUNK])
                for hd in range(4)]
        for g in range(2):
            ycat[:, YC_OFF + g * CHUNK:YC_OFF + (g + 1) * CHUNK] = jnp.where(lot, outs[2 * g], outs[2 * g + 1])

        zt = z_s[...]
        sig = 1.0 / (1.0 + jnp.exp(-zt))
        silu = zt * sig
        yc = ycat[...]
        yb = (yc * silu).astype(MM)
        pre_norm(xn_ref, xpn_ref)
        o = _dot(yb, wo_ref[...])
        project_z()
        r2 = _rms(o)
        nrm = o * r2
        g2v = g2_ref[...]
        e = x_ref[...] + nrm * g2v - t_ref[...]
        l1 = jnp.sum(e * e, axis=-1, keepdims=True)
        loss_ref[...] += jnp.broadcast_to(jnp.sum(l1, axis=0, keepdims=True) * (0.5 / D_MODEL), loss_ref.shape)
        dxo = e * (1.0 / D_MODEL)
        dxo_s[...] = dxo
        dg2_ref[...] += jnp.sum(dxo * nrm, axis=0, keepdims=True)
        dn = dxo * g2v
        do = r2 * (dn - nrm * jnp.mean(dn * nrm, axis=-1, keepdims=True))
        dob = do.astype(MM)
        dy = _dot_nt(dob, wo_ref[...])
        dp_s[:, Z_COL:IN_WIDTH] = (dy * yc * (sig * (1.0 + zt * (1.0 - sig)))).astype(MM)
        dyc[...] = dy * silu
        acc_o[...] += _dot_tn(yb, dob)

        def in_proj_bwd(c0, c1):
            dpt = dp_s[:, c0:c1]
            acc_i[c0:c1, :] += _dot_tn(dpt, h_s[NOW])
            part = _dot(dpt, wi_ref[c0:c1, :])
            if c0 == Z_COL:
                dh_s[...] = part
            else:
                dh_s[...] += part

        in_proj_bwd(Z_COL, IN_WIDTH)

        for blk in range(bpt):
            r0 = blk * CHUNK
            rows = slice(r0, r0 + CHUNK)
            for g in range(A_GROUPS):
                cg = slice(g * CHUNK, (g + 1) * CHUNK)
                cv = slice(A_WIDTH + g * CHUNK, A_WIDTH + (g + 1) * CHUNK)
                dya = dyc[rows, cg]
                dp_s[rows, cg] = (dya * gu_s[rows, cg]).astype(MM)
                dsv = dya * u_s[rows, cg]
                dsvb = dsv.astype(MM)
                dsv_acc[g] += dsv
                dwsp_ref[g] += _dot_nt(dsvb, vc_s[rows, cg])
                dvc = _dot(wtt_ref[g], dsvb)
                xhat = xh_s[rows, cg]
                dvg_ref[:, cg] += jnp.sum(dvc * xhat, axis=0, keepdims=True)
                dvb_ref[:, cg] += jnp.sum(dvc, axis=0, keepdims=True)
                dxh = dvc * vg_ref[:, cg]
                dv = (dxh - jnp.mean(dxh, axis=-1, keepdims=True)
                      - xhat * jnp.mean(dxh * xhat, axis=-1, keepdims=True))
                dp_s[rows, cv] = (dv * gv_s[rows, cg]).astype(MM)
        in_proj_bwd(0, UV_W)
        b_dosel, b_dp, b_dss = [], [], []
        for blk in range(bpt):
            rows = slice(blk * CHUNK, (blk + 1) * CHUNK)
            for hd in range(4):
                do128 = dyc[rows, YB_OFF + (hd // 2) * CHUNK:YB_OFF + (hd // 2 + 1) * CHUNK]
                b_dosel.append(jnp.where(lo if hd % 2 == 0 else ~lo, do128, 0.0).astype(MM))
                b_dp.append(_dot_nt(b_dosel[-1], vd_s[blk * 2 + hd // 2]))
        dosels, dps, dsss = [], [], []
        for hd in range(4):
            do128 = dyc[:, YC_OFF + (hd // 2) * CHUNK:YC_OFF + (hd // 2 + 1) * CHUNK]
            dosels.append(jnp.where(lot if hd % 2 == 0 else ~lot, do128, 0.0).astype(MM))
            dps.append(_dot_nt(dosels[hd], mkv_ref[:, MEM_LEN + (hd // 2) * CHUNK:MEM_LEN + (hd // 2 + 1) * CHUNK]))
        for blk in range(bpt):
            for hd in range(4):
                probs = pb_s[blk * 4 + hd]
                ps = ps_s[blk * 4 + hd][:, 0:1]
                dp = b_dp[blk * 4 + hd]
                delta = jnp.sum(probs * dp, axis=-1, keepdims=True)
                ds = probs * (dp - delta)
                dbias_acc[hd] += ds
                dsink_acc[hd:hd + 1, :] += jnp.broadcast_to(-jnp.sum(ps * delta, axis=0, keepdims=True), (1, CHUNK))
                b_dss.append((ds * SCALE).astype(MM))
        for hd in range(4):
            probs = pc_s[hd]
            ds = probs * (dps[hd] - jnp.sum(probs * dps[hd], axis=-1, keepdims=True))
            dsss.append((ds * SCALE).astype(MM))
        for blk in range(bpt):
            r0 = blk * CHUNK
            rows = slice(r0, r0 + CHUNK)
            dk_f, dv_f = [], []
            for kvh in range(2):
                kd = kd_s[blk * 2 + kvh]
                q128 = q_s[rows, kvh * CHUNK:(kvh + 1) * CHUNK].astype(F32)
                dq128 = jnp.zeros((CHUNK, CHUNK), F32)
                dkd = jnp.zeros((2 * CHUNK, CHUNK), F32)
                dvd = jnp.zeros((2 * CHUNK, CHUNK), F32)
                for gi in range(2):
                    hd = 2 * kvh + gi
                    half = lo if gi == 0 else ~lo
                    qsel = jnp.where(half, q128, 0.0).astype(MM)
                    dq128 = dq128 + jnp.where(half, _dot(b_dss[blk * 4 + hd], kd), 0.0)
                    dkd = dkd + _dot_tn(b_dss[blk * 4 + hd], qsel)
                    dvd = dvd + _dot_tn(pb_s[blk * 4 + hd].astype(MM), b_dosel[blk * 4 + hd])
                dp_s[rows, SQ_COL + kvh * CHUNK:SQ_COL + (kvh + 1) * CHUNK] = dq128.astype(MM)
                dk_f.append(dkd + pltpu.roll(dkd, 64, 1))
                dv_f.append(dvd + pltpu.roll(dvd, 64, 1))
            dkv_acc[r0:r0 + 2 * CHUNK, 0:CHUNK] += jnp.where(lob, dk_f[0], dk_f[1])
            dkv_acc[r0:r0 + 2 * CHUNK, CHUNK:2 * CHUNK] += jnp.where(lob, dv_f[0], dv_f[1])
        dp_s[:, SK_COL:MQ_COL] = dkv_acc[CHUNK:CHUNK + tm, :].astype(MM)
        for g in range(2):
            q128 = q_s[:, 256 + g * CHUNK:256 + (g + 1) * CHUNK].astype(F32)
            k128 = mkv_ref[:, g * CHUNK:(g + 1) * CHUNK]
            dq128 = jnp.zeros((tm, CHUNK), F32)
            dk128 = jnp.zeros((MEM_LEN, CHUNK), F32)
            dv128 = jnp.zeros((MEM_LEN, CHUNK), F32)
            for hh in range(2):
                hd = 2 * g + hh
                half = lot if hh == 0 else ~lot
                qsel = jnp.where(half, q128, 0.0).astype(MM)
                dq128 = dq128 + jnp.where(half, _dot(dsss[hd], k128), 0.0)
                dk128 = dk128 + _dot_tn(dsss[hd], qsel)
                dv128 = dv128 + _dot_tn(pc_s[hd].astype(MM), dosels[hd])
            dp_s[:, MQ_COL + g * CHUNK:MQ_COL + (g + 1) * CHUNK] = dq128.astype(MM)
            dmkv_ref[:, g * CHUNK:(g + 1) * CHUNK] += dk128
            dmkv_ref[:, MEM_LEN + g * CHUNK:MEM_LEN + (g + 1) * CHUNK] += dv128

        in_proj_bwd(SQ_COL, Z_COL)
        project_uv()
        dh = dh_s[...]
        r = r_s[NOW]
        nx = x_ref[...] * r
        dg1_ref[...] += jnp.sum(dh * nx, axis=0, keepdims=True)
        dnx = dh * g1v
        gx_ref[...] = dxo_s[...] + r * (dnx - nx * jnp.mean(dnx * nx, axis=-1, keepdims=True))

        @pl.when(step == last_step)
        def _():
            out_i = pltpu.make_async_copy(acc_i, dwi_hbm, sems.at[0])
            out_o = pltpu.make_async_copy(acc_o, dwo_hbm, sems.at[1])
            out_i.start()
            out_o.start()
            r_ = lax.broadcasted_iota(jnp.int32, (CHUNK, CHUNK), 0)
            c_ = lax.broadcasted_iota(jnp.int32, (CHUNK, CHUNK), 1)
            for g in range(A_GROUPS):
                dwsp_ref[g] = jnp.where(r_ >= c_, dwsp_ref[g], 0.0)
                dbs_ref[g:g + 1, :] = jnp.sum(dsv_acc[g].T, axis=0, keepdims=True)
            rows8 = lax.broadcasted_iota(jnp.int32, (8, CHUNK), 0)
            cols8 = lax.broadcasted_iota(jnp.int32, (8, CHUNK), 1)
            sk = jnp.zeros((8, CHUNK), F32)
            for hd in range(4):
                sk = sk + jnp.where((rows8 == 0) & (cols8 == hd),
                                    jnp.broadcast_to(dsink_acc[hd:hd + 1, :], (8, CHUNK)), 0.0)
            dsink_ref[...] = sk
            bk = bk_ref[...]
            valid = _window_valid()
            rrow = lax.broadcasted_iota(jnp.int32, (N_BUCKETS, CHUNK), 0)
            rcol = lax.broadcasted_iota(jnp.int32, (N_BUCKETS, CHUNK), 1)
            acc = jnp.zeros((N_BUCKETS, CHUNK), F32)
            for bb in range(N_BUCKETS):
                hit = (bk == bb) & valid
                for hd in range(4):
                    part = jnp.sum(jnp.where(hit, dbias_acc[hd], 0.0), axis=-1, keepdims=True)
                    tot = jnp.sum(part, axis=0, keepdims=True)
                    acc = acc + jnp.where((rrow == bb) & (rcol == hd), jnp.broadcast_to(tot, (N_BUCKETS, CHUNK)), 0.0)
            drel_ref[...] = acc
            out_i.wait()
            out_o.wait()

    after = lambda b, j: jnp.minimum(b * nt + j + 1, last_step)
    tile = pl.BlockSpec((tm, D_MODEL), lambda b, j: (tile_at(b * nt + j), 0))
    tile_after = pl.BlockSpec((tm, D_MODEL), lambda b, j: (tile_at(after(b, j)), 0))
    halo = pl.BlockSpec((CHUNK, D_MODEL), lambda b, j: (block_before(b * nt + j), 0))
    halo_after = pl.BlockSpec((CHUNK, D_MODEL), lambda b, j: (block_before(after(b, j)), 0))
    per_batch = lambda r, w: pl.BlockSpec((None, r, w), lambda b, j: (b, 0, 0))
    anyspec = pl.BlockSpec(memory_space=pl.ANY)
    grp = (A_GROUPS, CHUNK, CHUNK)
    return pl.pallas_call(
        body, name="layer", grid=(nb, nt),
        out_shape=(jax.ShapeDtypeStruct((t, D_MODEL), F32),
                   jax.ShapeDtypeStruct((nb, MEM_LEN, 2 * MEM_LEN), F32),
                   jax.ShapeDtypeStruct((IN_WIDTH, D_MODEL), F32),
                   jax.ShapeDtypeStruct((D_MODEL, D_MODEL), F32),
                   jax.ShapeDtypeStruct((1, D_MODEL), F32),
                   jax.ShapeDtypeStruct((1, D_MODEL), F32),
                   jax.ShapeDtypeStruct((8, CHUNK), F32),
                   jax.ShapeDtypeStruct(grp, F32),
                   jax.ShapeDtypeStruct((A_GROUPS, CHUNK), F32),
                   jax.ShapeDtypeStruct((1, A_WIDTH), F32),
                   jax.ShapeDtypeStruct((1, A_WIDTH), F32),
                   jax.ShapeDtypeStruct((8, CHUNK), F32),
                   jax.ShapeDtypeStruct((N_BUCKETS, CHUNK), F32)),
        in_specs=[tile, halo, tile_after, halo_after, tile, per_batch(MEM_LEN, 2 * MEM_LEN),
                  _full((4, CHUNK, 2 * CHUNK)),
                  pl.BlockSpec(memory_space=pltpu.SMEM),
                  _full((1, A_WIDTH)), _full((1, A_WIDTH)),
                  _full(grp), _full(grp), _full(grp),
                  _full((1, D_MODEL)), _full((1, D_MODEL)),
                  _full((IN_WIDTH, D_MODEL), single=True), _full((D_MODEL, D_MODEL), single=True),
                  _full((CHUNK, 2 * CHUNK))],
        out_specs=(tile, per_batch(MEM_LEN, 2 * MEM_LEN), anyspec, anyspec,
                   _full((1, D_MODEL)), _full((1, D_MODEL)), _full((8, CHUNK)),
                   _full(grp), _full((A_GROUPS, CHUNK)), _full((1, A_WIDTH)), _full((1, A_WIDTH)),
                   _full((8, CHUNK)), _full((N_BUCKETS, CHUNK))),
        scratch_shapes=[pltpu.VMEM((IN_WIDTH, D_MODEL), F32), pltpu.VMEM((D_MODEL, D_MODEL), F32),
                        pltpu.VMEM((tm, UV_W), F32), pltpu.VMEM((tm, Z_W), F32),
                        pltpu.VMEM((tm, 512), MM), pltpu.VMEM((tm + CHUNK, 2 * CHUNK), MM),
                        pltpu.VMEM((2, tm, D_MODEL), MM), pltpu.VMEM((CHUNK, D_MODEL), MM),
                        pltpu.VMEM((tm, IN_WIDTH), MM),
                        pltpu.VMEM((tm, D_MODEL), F32),
                        pltpu.VMEM((tm, D_MODEL), F32), pltpu.VMEM((2, tm, 1), F32),
                        pltpu.VMEM((tm, D_MODEL), F32), pltpu.VMEM((tm, D_MODEL), F32)]
                       + [pltpu.VMEM((tm, A_WIDTH), F32) for _ in range(4)]
                       + [pltpu.VMEM((tm, A_WIDTH), MM),
                          pltpu.VMEM((bpt * 4, CHUNK, 2 * CHUNK), F32),
                          pltpu.VMEM((bpt * 4, CHUNK, CHUNK), F32),
                          pltpu.VMEM((4, tm, MEM_LEN), F32),
                          pltpu.VMEM((bpt * 2, 2 * CHUNK, CHUNK), MM),
                          pltpu.VMEM((bpt * 2, 2 * CHUNK, CHUNK), MM),
                          pltpu.VMEM((tm + CHUNK, 2 * CHUNK), F32),
                          pltpu.VMEM((4, CHUNK, 2 * CHUNK), F32),
                          pltpu.VMEM(grp, F32),
                          pltpu.VMEM((8, CHUNK), F32),
                          pltpu.SemaphoreType.DMA((2,))],
        compiler_params=_params(dimension_semantics=("arbitrary", "arbitrary")),
    )(x2, x2, x2, x2, tgt2, mkv3, bias, sinks, vg, vb, wt, wtt, bcol, g1, g2, w_in_t, w_o, buckets)


class _ShardReduce:
    def __init__(self, pos, g, bufs, sems):
        self.x, self.y, self.c = pos
        self.g = g
        self.own, self.rcv, self.sbuf, self.rbuf, self.cbuf = bufs
        self.ld, self.sa, self.ra, self.sb, self.rb = sems
        self.nrow = g.shape[1]
        self.here = (self.x, self.y, self.c)
        self.sib = (self.x, self.y, 1 - self.c)
        self.first, self.second, self.far = _route(*pos)

    def _load(self, q):
        return pltpu.make_async_copy(self.g.at[2 * q + self.c], self.own.at[q], self.ld.at[q])

    def _to_sib(self, q, to):
        return _remote(self.g.at[2 * q + 1 - self.c], self.rcv.at[q], self.sa.at[q], self.ra.at[q], to)

    def _send(self, k, to):
        dst = self.cbuf.at[0] if k == 1 else self.rbuf.at[0 if k == 0 else 1]
        return _remote(self.sbuf.at[k], dst, self.sb.at[k], self.rb.at[k], to)

    def _stage(self, k, which, extra=None):
        def cast(r):
            v = self.rcv[which, r, :]
            if extra is not None:
                v = v + extra[0, r, :].astype(F32)
            self.sbuf[k, r, :] = v.astype(BF16)

        _rows_loop(self.nrow, cast)

    @staticmethod
    def _q(chip):
        return 2 * chip[0] + chip[1]

    def start(self):
        for q in range(4):
            self._load(q).start()
            self._to_sib(q, self.sib).start()

    def mid(self):
        for q in range(4):
            self._load(q).wait()
            self._to_sib(q, self.here).wait_recv()

        def add(r):
            for q in range(4):
                self.rcv[q, r, :] = self.rcv[q, r, :] + self.own[q, r, :]

        _rows_loop(self.nrow, add)
        to_first = (self.first[0], self.first[1], self.c)
        self._stage(0, self._q(self.first))
        self._send(0, to_first).start()
        self._stage(1, self._q(self.far))
        self._send(1, to_first).start()

    def pass_on(self):
        self._send(1, self.here).wait_recv()
        self._stage(2, self._q(self.second), extra=self.cbuf)
        self._send(2, (self.second[0], self.second[1], self.c)).start()

    def finish(self, out):
        self._send(0, self.here).wait_recv()
        self._send(2, self.here).wait_recv()
        which = 2 * self.x + self.y

        def tot(r):
            out[r, :] = (self.rcv[which, r, :] + self.rbuf[0, r, :].astype(F32)) + self.rbuf[1, r, :].astype(F32)

        _rows_loop(self.nrow, tot)
        for q in range(4):
            self._to_sib(q, self.sib).wait_send()
        to_first = (self.first[0], self.first[1], self.c)
        self._send(0, to_first).wait_send()
        self._send(1, to_first).wait_send()
        self._send(2, (self.second[0], self.second[1], self.c)).wait_send()


def _reduce_scratch(shape):
    return [pltpu.VMEM((4,) + shape, F32), pltpu.VMEM((4,) + shape, F32),
            pltpu.VMEM((3,) + shape, BF16), pltpu.VMEM((2,) + shape, BF16), pltpu.VMEM((1,) + shape, BF16),
            pltpu.SemaphoreType.DMA((4,)), pltpu.SemaphoreType.DMA((4,)), pltpu.SemaphoreType.DMA((4,)),
            pltpu.SemaphoreType.DMA((3,)), pltpu.SemaphoreType.DMA((3,))]


_N_RED = 10

_S_LAYOUT = (((1, D_MODEL), 0), ((1, D_MODEL), 8), ((1, D_MODEL), 16),
             ((1, A_WIDTH), 24), ((1, A_WIDTH), 28), ((A_GROUPS, CHUNK), 32),
             ((1, 4), 36), ((N_BUCKETS, 4), 40),
             ((A_GROUPS * CHUNK, CHUNK), 72))
_LOSS_ROW = 37
_W_SP_ROW = _S_LAYOUT[-1][1]
_S_ROWS = _W_SP_ROW + A_GROUPS * CHUNK
_N_SMALL = len(_S_LAYOUT)


def _pack_rows(dst, refs):
    for (shp, r0), ref in zip(_S_LAYOUT, refs):
        if shp[0] == 1 and shp[1] >= CHUNK:
            for i in range(shp[1] // CHUNK):
                dst[r0 + i:r0 + i + 1, :] = ref[:, i * CHUNK:(i + 1) * CHUNK]
        elif ref.shape[-1] == CHUNK:
            dst[r0:r0 + shp[0], :] = ref[0:shp[0], :]
        else:
            dst[r0:r0 + shp[0], 0:shp[1]] = ref[...]


def _unpack_rows(src, refs):
    for (shp, r0), ref in zip(_S_LAYOUT, refs):
        if shp[0] == 1 and shp[1] >= CHUNK:
            for i in range(shp[1] // CHUNK):
                ref[:, i * CHUNK:(i + 1) * CHUNK] = src[r0 + i:r0 + i + 1, :]
        elif shp[1] == CHUNK:
            ref[...] = src[r0:r0 + shp[0], :]
        else:
            if tuple(ref.shape) == (shp[1], shp[0]):
                ref[...] = src[r0:r0 + CHUNK, :].T[0:shp[1], 0:shp[0]]
            else:
                ref[...] = src[r0:r0 + shp[0], 0:shp[1]]


_MEM_G = 2


def _greduce(ga, gb, dmkv, mem2, gm, w_mkv, small_g, loss_p):
    shp_c = (SHARD_O, 2 * MEM_LEN)
    shapes = (shp_c, gb.shape[1:], ga.shape[1:])
    rs = _S_ROWS

    def body(*refs):
        it = iter(refs)
        take = lambda n: [next(it) for _ in range(n)]
        gb_ref, ga_ref, d_ref, m_ref, gm_ref, wm_ref = take(6)
        sg_refs = take(_N_SMALL - 1)
        loss_ref, = take(1)
        oc, ob, oa, ogs = take(4)
        red = take(3 * _N_RED)
        gs_ref, rs_a, rs_b, rs_w, gc_ref, dgm_ref = take(6)
        ssem_a, rsem_a, ssem_b, rsem_b = take(4)

        pos = _position()
        x, y, cc = pos
        myq = 2 * x + y
        here, sib = (x, y, cc), (x, y, 1 - cc)
        chips = _other_chips(x, y)
        reducers = [_ShardReduce(pos, g, red[k * _N_RED:k * _N_RED + 5], red[k * _N_RED + 5:(k + 1) * _N_RED])
                    for k, g in enumerate((gc_ref, gb_ref, ga_ref))]
        for rd in reducers[1:]:
            rd.start()

        xf = m_ref[...]
        nm = xf * _rms(xf)
        hm = (nm * gm_ref[...]).astype(MM)
        d = d_ref[...].astype(MM)
        for o in range(N_DEV):
            gc_ref[o] = _dot_tn(hm[:, o * SHARD_O:(o + 1) * SHARD_O], d)
        dgm_ref[...] = jnp.sum(_dot_nt(d, wm_ref[...]) * nm, axis=0, keepdims=True)
        reducers[0].start()

        gs_ref[...] = jnp.zeros_like(gs_ref)
        _pack_rows(gs_ref, sg_refs[:_MEM_G] + [dgm_ref] + sg_refs[_MEM_G:])
        gs_ref[_LOSS_ROW:_LOSS_ROW + 1, :] = loss_ref[0:1, :]
        small_a = _remote(gs_ref, rs_a, ssem_a, rsem_a, sib)
        small_a.start()

        _remote(gs_ref, rs_a, ssem_a, rsem_a, here).wait_recv()
        rs_b[myq] = gs_ref[0:_W_SP_ROW, :] + rs_a[0:_W_SP_ROW, :]
        rs_w[myq] = (gs_ref[_W_SP_ROW:rs, :] + rs_a[_W_SP_ROW:rs, :]).astype(BF16)
        small_b = []
        for j, chip in enumerate(chips):
            to = (chip[0], chip[1], cc)
            small_b.append(_remote(rs_b.at[myq], rs_b.at[myq], ssem_b.at[0, j], rsem_b.at[0, j], to))
            small_b.append(_remote(rs_w.at[myq], rs_w.at[myq], ssem_b.at[1, j], rsem_b.at[1, j], to))
        for cp in small_b:
            cp.start()
        late_last = reducers[1:] + reducers[:1]
        for rd in late_last:
            rd.mid()
        for rd in late_last:
            rd.pass_on()

        for j in range(3):
            _remote(rs_b.at[myq], rs_b.at[myq], ssem_b.at[0, j], rsem_b.at[0, j], here).wait_recv()
            _remote(rs_w.at[myq], rs_w.at[myq], ssem_b.at[1, j], rsem_b.at[1, j], here).wait_recv()
        ogs[0:_W_SP_ROW, :] = ((rs_b[0] + rs_b[1]) + rs_b[2]) + rs_b[3]

        def tot_w(r):
            w = [rs_w[q, r, :].astype(F32) for q in range(4)]
            ogs[pl.ds(pl.multiple_of(_W_SP_ROW + r.start, 8), _ROWS), :] = ((w[0] + w[1]) + w[2]) + w[3]

        _rows_loop(rs - _W_SP_ROW, tot_w)
        for rd, out in zip(late_last, (ob, oa, oc)):
            rd.finish(out)
        small_a.wait_send()
        for cp in small_b:
            cp.wait_send()

    vm = pl.BlockSpec(memory_space=pltpu.VMEM)
    anyspec = pl.BlockSpec(memory_space=pl.ANY)
    scratch = []
    for shp in shapes:
        scratch += _reduce_scratch(shp)
    scratch += [pltpu.VMEM((rs, CHUNK), F32), pltpu.VMEM((rs, CHUNK), F32),
                pltpu.VMEM((4, _W_SP_ROW, CHUNK), F32), pltpu.VMEM((4, rs - _W_SP_ROW, CHUNK), BF16),
                pltpu.VMEM((N_DEV,) + shp_c, F32), pltpu.VMEM((1, D_MODEL), F32),
                pltpu.SemaphoreType.DMA, pltpu.SemaphoreType.DMA,
                pltpu.SemaphoreType.DMA((2, 3)), pltpu.SemaphoreType.DMA((2, 3))]
    tc, tb, ta, ts = pl.pallas_call(
        body, name="greduce",
        out_shape=tuple([jax.ShapeDtypeStruct(shp, F32) for shp in shapes] + [jax.ShapeDtypeStruct((rs, CHUNK), F32)]),
        in_specs=[anyspec] * 2 + [vm] * (4 + _N_SMALL),
        out_specs=(vm, vm, vm, vm),
        scratch_shapes=scratch,
        compiler_params=_params(),
    )(gb, ga, dmkv, mem2, gm, w_mkv, *small_g, loss_p)
    return ta, tb, tc, ts


def _adamw(w, g, m, v):
    m = ADAM_B1 * m + (1.0 - ADAM_B1) * g
    v = ADAM_B2 * v + (1.0 - ADAM_B2) * (g * g)
    m_hat = m / (1.0 - ADAM_B1 ** ADAM_STEP)
    v_hat = v / (1.0 - ADAM_B2 ** ADAM_STEP)
    delta = -ADAM_LR * (m_hat / (jnp.sqrt(v_hat) + ADAM_EPS) + ADAM_WD * w)
    return delta, m, v


def _update(ta, tb, tc, ts, big_wmv, small_wmv):
    shapes = (ta.shape, tb.shape, tc.shape)
    rs = _S_ROWS
    small_shapes = [tuple(a.shape) for a in small_wmv[0]]

    def body(*refs):
        it = iter(refs)
        take = lambda n: [next(it) for _ in range(n)]
        ga_ref, gb_ref, gc_ref, gs_ref = take(4)
        wa, ma, va, wb, mb, vb_, wc, mc, vc = take(9)
        sw_refs, sm_refs, sv_refs = take(_N_SMALL), take(_N_SMALL), take(_N_SMALL)
        oga, oda, oma, ova, ogb, odb, omb, ovb, ogc, odc, omc, ovc = take(12)
        so_refs = [take(_N_SMALL) for _ in range(4)]
        loss_out, = take(1)
        ws, ms, vs, ods, oms, ovs = take(6)

        for buf in (ws, ms, vs):
            buf[...] = jnp.zeros_like(buf)
        _pack_rows(ws, sw_refs)
        _pack_rows(ms, sm_refs)
        _pack_rows(vs, sv_refs)

        big = ((ga_ref, wa, ma, va, oga, oda, oma, ova), (gb_ref, wb, mb, vb_, ogb, odb, omb, ovb),
               (gc_ref, wc, mc, vc, ogc, odc, omc, ovc))
        for arr in range(3):
            g_r, w_r, m_r, v_r, og, od, om, ov = big[arr]

            def upd(r, g_r=g_r, w_r=w_r, m_r=m_r, v_r=v_r, og=og, od=od, om=om, ov=ov):
                g = g_r[r, :]
                d, m, v = _adamw(w_r[r, :], g, m_r[r, :], v_r[r, :])
                og[r, :] = g
                od[r, :] = d
                om[r, :] = m
                ov[r, :] = v

            _rows_loop(shapes[arr][0], upd)

        def upd_s(i, _):
            r = pl.ds(pl.multiple_of(i * 8, 8), 8)
            d, m, v = _adamw(ws[r, :], gs_ref[r, :], ms[r, :], vs[r, :])
            ods[r, :] = d
            oms[r, :] = m
            ovs[r, :] = v
            return 0

        lax.fori_loop(0, rs // 8, upd_s, 0)
        for k, buf in enumerate((gs_ref, ods, oms, ovs)):
            _unpack_rows(buf, so_refs[k])
        loss_out[...] = gs_ref[_LOSS_ROW:_LOSS_ROW + 1, 0:1]

    vm = pl.BlockSpec(memory_space=pltpu.VMEM)
    big_out = []
    for shp in shapes:
        big_out += [jax.ShapeDtypeStruct(shp, F32)] * 4
    small_out = [jax.ShapeDtypeStruct(shp[::-1] if shp == (N_BUCKETS, 4) else shp, F32) for shp in small_shapes] * 4
    out_shape = tuple(big_out + small_out + [jax.ShapeDtypeStruct((1, 1), F32)])
    n_in = 4 + 9 + 3 * _N_SMALL
    return pl.pallas_call(
        body, name="update",
        out_shape=out_shape,
        in_specs=[vm] * n_in,
        out_specs=tuple([vm] * len(out_shape)),
        scratch_shapes=[pltpu.VMEM((rs, CHUNK), F32) for _ in range(6)],
        compiler_params=_params(),
    )(ta, tb, tc, ts, *big_wmv, *small_wmv[0], *small_wmv[1], *small_wmv[2])


def kernel(x, mem, pre_norm_g, post_norm_g, mem_norm_g, w_in, w_mem_kv, v_norm_g, v_norm_b, w_spatial, b_spatial, attn_sinks, rel_bias, w_out, loss_target, m_pre_norm_g, m_post_norm_g, m_mem_norm_g, m_w_in, m_w_mem_kv, m_v_norm_g, m_v_norm_b, m_w_spatial, m_b_spatial, m_attn_sinks, m_rel_bias, m_w_out, v_pre_norm_g, v_post_norm_g, v_mem_norm_g, v_w_in, v_w_mem_kv, v_v_norm_g, v_v_norm_b, v_w_spatial, v_b_spatial, v_attn_sinks, v_rel_bias, v_w_out):
    sh_a = (w_in[0].T, m_w_in[0].T, v_w_in[0].T)
    sh_b = (w_out[0], m_w_out[0], v_w_out[0])
    sh_c = (w_mem_kv[0], m_w_mem_kv[0], v_w_mem_kv[0])
    nb, s, _ = x.shape
    t = nb * s
    x2 = x.reshape(t, D_MODEL)
    tgt2 = loss_target.reshape(t, D_MODEL)
    mem2 = mem.reshape(nb * MEM_LEN, D_MODEL)
    buckets = jnp.asarray(_t5_buckets())

    wa, wb, wc, bias, wt, wtt, bcol, mkv = _wgather(sh_a[0], sh_b[0], sh_c[0], rel_bias, w_spatial[0], b_spatial[0],
                                                    buckets, mem2, mem_norm_g)
    w_mkv = wc.reshape(D_MODEL, 2 * MEM_LEN)
    gx, dmkv, dwi, dwo, dg1, dg2, loss_p, dwsp, dbs, dvg, dvb, dsink, drel = _layer(
        x2, tgt2, mkv.reshape(nb, MEM_LEN, 2 * MEM_LEN), bias, attn_sinks.reshape(4), v_norm_g, v_norm_b, wt, wtt, bcol,
        pre_norm_g, post_norm_g, wa.reshape(IN_WIDTH, D_MODEL), wb.reshape(D_MODEL, D_MODEL), buckets,
        nb, s, min(256, s))
    gx = gx.reshape(nb, s, D_MODEL)
    small_grads = [dg1, dg2, dvg, dvb, dbs, dsink, drel, dwsp.reshape(A_GROUPS * CHUNK, CHUNK)]

    small_names = ["pre_norm_g", "post_norm_g", "mem_norm_g", "v_norm_g", "v_norm_b", "b_spatial", "attn_sinks",
                   "rel_bias", "w_spatial"]
    given = dict(pre_norm_g=(pre_norm_g, m_pre_norm_g, v_pre_norm_g), post_norm_g=(post_norm_g, m_post_norm_g, v_post_norm_g),
                 mem_norm_g=(mem_norm_g, m_mem_norm_g, v_mem_norm_g), v_norm_g=(v_norm_g, m_v_norm_g, v_v_norm_g),
                 v_norm_b=(v_norm_b, m_v_norm_b, v_v_norm_b), b_spatial=(b_spatial, m_b_spatial, v_b_spatial),
                 attn_sinks=(attn_sinks, m_attn_sinks, v_attn_sinks), rel_bias=(rel_bias, m_rel_bias, v_rel_bias),
                 w_spatial=(w_spatial, m_w_spatial, v_w_spatial))
    small_wmv = [[given[n][k].reshape(shp) for n, (shp, _) in zip(small_names, _S_LAYOUT)] for k in range(3)]

    ta, tb, tc, ts = _greduce(dwi.reshape(N_DEV, SHARD_IN, D_MODEL), dwo.reshape(N_DEV, SHARD_O, D_MODEL),
                              dmkv.reshape(nb * MEM_LEN, 2 * MEM_LEN), mem2, mem_norm_g, w_mkv, small_grads, loss_p)
    outs = _update(ta, tb, tc, ts, (*sh_a, *sh_b, *sh_c), small_wmv)
    ra, rb, rc = outs[0:4], outs[4:8], outs[8:12]
    loss = outs[12 + 4 * _N_SMALL].reshape(())

    res = {}
    for k, kind in enumerate(("grad", "delta", "new_m", "new_v")):
        res[kind, "w_in"] = ra[k].T[None]
        res[kind, "w_out"] = rb[k][None]
        res[kind, "w_mem_kv"] = rc[k][None]
        for i, n in enumerate(small_names):
            o = outs[12 + k * _N_SMALL + i]
            res[kind, n] = o.T if n == "rel_bias" else o.reshape(given[n][0].shape)
    order = ["pre_norm_g", "post_norm_g", "mem_norm_g", "w_in", "w_mem_kv", "v_norm_g", "v_norm_b", "w_spatial",
             "b_spatial", "attn_sinks", "rel_bias", "w_out"]
    flat = [res[kind, n] for kind in ("grad", "delta", "new_m", "new_v") for n in order]
    return (loss, gx, *flat)
```

```python
import numpy as np
import jax
import jax.numpy as jnp
from jax import lax
from jax.experimental import pallas as pl
from jax.experimental.pallas import tpu as pltpu

F32 = jnp.float32
BF16 = jnp.bfloat16
MM = jnp.bfloat16

D_MODEL = 1024
CHUNK = 128
A_GROUPS = 4
A_WIDTH = 512
UV_W = 1024
QKV_W = 768
Z_W = 1024
IN_WIDTH = UV_W + QKV_W + Z_W
MEM_LEN = 256
N_BUCKETS = 32
MAX_DISTANCE = 128
EPS = 1e-6
NEG = -1e30
SCALE = 0.125
N_DEV = 8
SHARD_IN = IN_WIDTH // N_DEV
SHARD_O = D_MODEL // N_DEV
DW_COLS = 256

SQ_COL, SK_COL, SV_COL, MQ_COL, Z_COL = UV_W, UV_W + 256, UV_W + 384, UV_W + 512, UV_W + QKV_W
YB_OFF, YC_OFF = 512, 768

ADAM_LR = 0.001
ADAM_B1 = 0.9
ADAM_B2 = 0.999
ADAM_EPS = 1e-08
ADAM_WD = 0.01
ADAM_STEP = 10

VMEM_LIMIT = 60 * 1024 * 1024

_GELU_C = 0.7978845608028654
_GELU_A = 0.044715

MESH = pl.DeviceIdType.MESH
_ROWS = 32


def _dot(a, b):
    return lax.dot_general(a, b, (((1,), (0,)), ((), ())), preferred_element_type=F32)


def _dot_nt(a, b):
    return lax.dot_general(a, b, (((1,), (1,)), ((), ())), preferred_element_type=F32)


def _dot_tn(a, b):
    return lax.dot_general(a, b, (((0,), (0,)), ((), ())), preferred_element_type=F32)


def _gelu_and_grad(x):
    x2 = x * x
    t = jnp.tanh(_GELU_C * (x + _GELU_A * x * x2))
    g = 0.5 * x * (1.0 + t)
    dg = 0.5 * (1.0 + t) + 0.5 * x * (1.0 - t * t) * (_GELU_C * (1.0 + 3.0 * _GELU_A * x2))
    return g, dg


def _t5_buckets():
    qi = np.arange(CHUNK)[:, None]
    kj = np.arange(2 * CHUNK)[None, :]
    n = np.maximum(qi + CHUNK - kj, 0)
    max_exact = N_BUCKETS // 2
    large = max_exact + (np.log(np.maximum(n, 1) / max_exact) / np.log(MAX_DISTANCE / max_exact)
                         * (N_BUCKETS - max_exact)).astype(np.int32)
    large = np.minimum(large, N_BUCKETS - 1)
    return np.where(n < max_exact, n, large).astype(np.int32)


def _params(**kw):
    return pltpu.CompilerParams(vmem_limit_bytes=VMEM_LIMIT, **kw)


def _full(shape, single=False):
    nd = len(shape)
    if single:
        return pl.BlockSpec(shape, lambda *_: (0,) * nd, pipeline_mode=pl.Buffered(1))
    return pl.BlockSpec(shape, lambda *_: (0,) * nd)


def _window_valid():
    qi = lax.broadcasted_iota(jnp.int32, (CHUNK, 2 * CHUNK), 0)
    kj = lax.broadcasted_iota(jnp.int32, (CHUNK, 2 * CHUNK), 1)
    dist = qi + CHUNK - kj
    return (dist >= 0) & (dist < CHUNK)


def _position():
    return lax.axis_index("x"), lax.axis_index("y"), lax.axis_index("c")


def _other_chips(x, y):
    return [(1 - x, y), (x, 1 - y), (1 - x, 1 - y)]


def _route(x, y, c):
    first = (x * c + (1 - x) * (1 - c), y * (1 - c) + (1 - y) * c)
    second = (x * (1 - c) + (1 - x) * c, y * c + (1 - y) * (1 - c))
    return first, second, (1 - x, 1 - y)


def _remote(src, dst, ssem, rsem, to):
    return pltpu.make_async_remote_copy(src_ref=src, dst_ref=dst, send_sem=ssem, recv_sem=rsem,
                                        device_id=to, device_id_type=MESH)


def _rows_loop(nrow, fn):
    def step(i, _):
        fn(pl.ds(pl.multiple_of(i * _ROWS, _ROWS), _ROWS))
        return 0

    lax.fori_loop(0, nrow // _ROWS, step, 0)


class _Gather:
    def __init__(self, pos, out, ssem, rsem):
        self.x, self.y, self.c = pos
        self.out, self.ssem, self.rsem = out, ssem, rsem
        self.me = 4 * self.x + 2 * self.y + self.c
        self.here = (self.x, self.y, self.c)
        self.sib = (self.x, self.y, 1 - self.c)
        self.first, self.second, self.far = _route(*pos)

    def _copy(self, k, blk, to):
        r = self.out.at[blk]
        return _remote(r, r, self.ssem.at[k], self.rsem.at[k], to)

    def _idx(self, chip, core):
        return 4 * chip[0] + 2 * chip[1] + core

    def _on(self, chip):
        return (chip[0], chip[1], self.c)

    def start(self):
        self._copy(0, self.me, self.sib).start()
        self._copy(1, self.me, self._on(self.first)).start()
        self._copy(2, self.me, self._on(self.second)).start()

    def forward(self):
        c = self.c
        self._copy(1, self._idx(self.first, c), self.here).wait_recv()
        self._copy(3, self._idx(self.first, c), self._on(self.second)).start()
        self._copy(4, self._idx(self.first, c), self.sib).start()
        self._copy(2, self._idx(self.second, c), self.here).wait_recv()
        self._copy(5, self._idx(self.second, c), self.sib).start()
        self._copy(3, self._idx(self.far, c), self.here).wait_recv()
        self._copy(6, self._idx(self.far, c), self.sib).start()

    def finish(self):
        c = self.c
        self._copy(0, self._idx((self.x, self.y), 1 - c), self.here).wait_recv()
        for k, chip in ((4, self.second), (5, self.first), (6, self.far)):
            self._copy(k, self._idx(chip, 1 - c), self.here).wait_recv()
        self._copy(0, self.me, self.sib).wait_send()
        self._copy(1, self.me, self._on(self.first)).wait_send()
        self._copy(2, self.me, self._on(self.second)).wait_send()
        self._copy(3, self._idx(self.first, c), self._on(self.second)).wait_send()
        for k, chip in ((4, self.first), (5, self.second), (6, self.far)):
            self._copy(k, self._idx(chip, c), self.sib).wait_send()


def _prep_tables(rb_ref, w_ref, b_ref, bk_ref, bias_ref, wt_ref, wtt_ref, bcol_ref):
    valid = _window_valid()
    bk = bk_ref[...]
    acc = [jnp.full((CHUNK, 2 * CHUNK), NEG, F32) for _ in range(4)]
    for b in range(N_BUCKETS):
        hit = (bk == b) & valid
        for h in range(4):
            acc[h] = jnp.where(hit, rb_ref[b, h], acc[h])
    for h in range(4):
        bias_ref[h] = acc[h]
    r = lax.broadcasted_iota(jnp.int32, (CHUNK, CHUNK), 0)
    c = lax.broadcasted_iota(jnp.int32, (CHUNK, CHUNK), 1)
    for g in range(A_GROUPS):
        w = jnp.where(r >= c, w_ref[g], 0.0)
        wt_ref[g] = w.astype(MM)
        wtt_ref[g] = w.T.astype(MM)
        bcol_ref[g] = jnp.broadcast_to(b_ref[g:g + 1, :], (CHUNK, CHUNK)).T


def _wgather(a, b, c, rel_bias, w_sp, b_sp, buckets, mem2, gm):
    tmem = mem2.shape[0]

    def body(a_ref, b_ref, c_ref, rb_ref, w_ref, bsp_ref, bk_ref, m_ref, gm_ref,
             oa, ob, oc, bias_ref, wt_ref, wtt_ref, bcol_ref, mkv_ref, ssem, rsem):
        pos = _position()
        me = 4 * pos[0] + 2 * pos[1] + pos[2]
        gathers = []
        for k, (src, out) in enumerate(((c_ref, oc), (b_ref, ob), (a_ref, oa))):
            out[me] = src[...].astype(BF16)
            g = _Gather(pos, out, ssem.at[k], rsem.at[k])
            g.start()
            gathers.append(g)
        _prep_tables(rb_ref, w_ref, bsp_ref, bk_ref, bias_ref, wt_ref, wtt_ref, bcol_ref)
        for g in gathers:
            g.forward()
        gathers[0].finish()
        xf = m_ref[...]
        hm = (xf * _rms(xf) * gm_ref[...]).astype(MM)
        acc = jnp.zeros((tmem, 2 * MEM_LEN), F32)
        for d in range(N_DEV):
            acc = acc + _dot(hm[:, d * SHARD_O:(d + 1) * SHARD_O], oc[d])
        mkv_ref[...] = acc.astype(MM)
        for g in gathers[1:]:
            g.finish()

    vm = pl.BlockSpec(memory_space=pltpu.VMEM)
    grp = (A_GROUPS, CHUNK, CHUNK)
    return pl.pallas_call(
        body, name="wgather",
        out_shape=(jax.ShapeDtypeStruct((N_DEV,) + a.shape, BF16),
                   jax.ShapeDtypeStruct((N_DEV,) + b.shape, BF16),
                   jax.ShapeDtypeStruct((N_DEV,) + c.shape, BF16),
                   jax.ShapeDtypeStruct((4, CHUNK, 2 * CHUNK), F32),
                   jax.ShapeDtypeStruct(grp, MM), jax.ShapeDtypeStruct(grp, MM), jax.ShapeDtypeStruct(grp, F32),
                   jax.ShapeDtypeStruct((tmem, 2 * MEM_LEN), MM)),
        in_specs=[vm, vm, vm, pl.BlockSpec(memory_space=pltpu.SMEM), vm, vm, vm, vm, vm],
        out_specs=tuple([vm] * 8),
        scratch_shapes=[pltpu.SemaphoreType.DMA((3, 7)), pltpu.SemaphoreType.DMA((3, 7))],
        compiler_params=_params(),
    )(a, b, c, rel_bias, w_sp, b_sp, buckets, mem2, gm)


def _half_masks(rows):
    lane = lax.broadcasted_iota(jnp.int32, (rows, CHUNK), 1)
    return lane < 64


def _dup_heads(band):
    b32 = band.astype(F32)
    rolled = pltpu.roll(b32, 64, 1)
    lo = _half_masks(band.shape[0])
    return (jnp.where(lo, b32, rolled).astype(MM), jnp.where(lo, rolled, b32).astype(MM))


def _swa_probs(qk, bias_h, sink_h, first_add):
    s = qk * SCALE + bias_h + first_add
    m = jnp.maximum(jnp.max(s, axis=-1, keepdims=True), sink_h)
    p = jnp.exp(s - m)
    es = jnp.exp(sink_h - m)
    inv = 1.0 / (jnp.sum(p, axis=-1, keepdims=True) + es)
    return p * inv, es * inv


def _softmax(s):
    m = jnp.max(s, axis=-1, keepdims=True)
    p = jnp.exp(s - m)
    return p * (1.0 / jnp.sum(p, axis=-1, keepdims=True))


def _first_block_mask(n):
    col = lax.broadcasted_iota(jnp.int32, (CHUNK, 2 * CHUNK), 1)
    return jnp.where((col < CHUNK) & (n == 0), NEG, 0.0)


def _rms(xf):
    return lax.rsqrt(jnp.mean(xf * xf, axis=-1, keepdims=True) + EPS)


def _layer(x2, tgt2, mkv3, bias, sinks, vg, vb, wt, wtt, bcol, g1, g2, w_in_t, w_o, buckets, nb, s, tm):
    nt = s // tm
    bpt = tm // CHUNK
    bps = s // CHUNK
    t = nb * s
    last_step = nb * nt - 1

    def tile_at(step):
        return (step // nt) * nt + nt - 1 - step % nt

    def block_before(step):
        return (step // nt) * bps + jnp.maximum((nt - 1 - step % nt) * bpt - 1, 0)

    def body(x_ref, xp_ref, xn_ref, xpn_ref, t_ref, mkv_ref, bias_ref, sink_ref, vg_ref, vb_ref,
             wt_ref, wtt_ref, bcol_ref, g1_ref, g2_ref, wi_ref, wo_ref, bk_ref,
             gx_ref, dmkv_ref, dwi_hbm, dwo_hbm, dg1_ref, dg2_ref, loss_ref, dwsp_ref, dbs_ref,
             dvg_ref, dvb_ref, dsink_ref, drel_ref,
             acc_i, acc_o, uv_s, z_s, q_s, kv_s, h_s, hp_s, dp_s, dxo_s, dh_s, r_s,
             ycat, dyc, u_s, gu_s, gv_s, xh_s, vc_s, pb_s, ps_s, pc_s, kd_s, vd_s,
             dkv_acc, dbias_acc, dsv_acc, dsink_acc, sems):
        b, j = pl.program_id(0), pl.program_id(1)
        jt = nt - 1 - j
        step = b * nt + j
        g1v = g1_ref[...]
        NOW, NEXT, DONE = 0, 1, 2
        dw_cols = [(c, c + DW_COLS) for c in range(0, IN_WIDTH, DW_COLS)]

        def weight_grad(n, slot):
            for c0, c1 in dw_cols[:n]:
                acc_i[c0:c1, :] += _dot_tn(dp_s[:, c0:c1], h_s[slot])
            del dw_cols[:n]

        def pre_norm(x_tile, x_before):
            xf = x_tile[...]
            r_s[NEXT] = _rms(xf)
            h_s[NEXT] = (xf * r_s[NEXT] * g1v).astype(MM)
            xp = x_before[...]
            hp_s[...] = (xp * _rms(xp) * g1v).astype(MM)

        def project_z():
            z_s[...] = _dot_nt(h_s[NEXT], wi_ref[Z_COL:IN_WIDTH, :])

        def project_uv():
            uv_s[...] = _dot_nt(h_s[NEXT], wi_ref[0:UV_W, :])

        @pl.when(step == 0)
        def _():
            for ref in (acc_i, acc_o, dg1_ref, dg2_ref, loss_ref, dwsp_ref, dvg_ref, dvb_ref,
                        dbias_acc, dsv_acc, dsink_acc):
                ref[...] = jnp.zeros_like(ref)
            dp_s[...] = jnp.zeros_like(dp_s)
            h_s[NOW] = jnp.zeros((tm, D_MODEL), MM)
            pre_norm(x_ref, xp_ref)
            project_z()
            project_uv()

        h_s[DONE] = h_s[NOW]
        r_s[NOW] = r_s[NEXT]
        h = h_s[NEXT]
        h_s[NOW] = h
        hp = hp_s[...]

        @pl.when(j == 0)
        def _():
            dmkv_ref[...] = jnp.zeros_like(dmkv_ref)
            dkv_acc[...] = jnp.zeros_like(dkv_acc)

        carry = dkv_acc[0:CHUNK, :]
        dkv_acc[...] = jnp.zeros_like(dkv_acc)
        dkv_acc[tm:tm + CHUNK, :] = carry

        lo = _half_masks(CHUNK)
        lob = _half_masks(2 * CHUNK)
        lot = _half_masks(tm)

        qkv = _dot_nt(h, wi_ref[SQ_COL:Z_COL, :])
        q_s[:, 0:256] = qkv[:, 0:256].astype(MM)
        q_s[:, 256:512] = qkv[:, 512:768].astype(MM)
        kv_s[CHUNK:CHUNK + tm, :] = qkv[:, 256:512].astype(MM)
        kv_s[0:CHUNK, :] = _dot_nt(hp, wi_ref[SK_COL:MQ_COL, :]).astype(MM)

        weight_grad(3, DONE)
        b_qk, b_pb = [], []
        for blk in range(bpt):
            r0 = blk * CHUNK
            rows = slice(r0, r0 + CHUNK)
            for g in range(A_GROUPS):
                cg = slice(g * CHUNK, (g + 1) * CHUNK)
                u, gu = _gelu_and_grad(uv_s[rows, cg])
                v, gv = _gelu_and_grad(uv_s[rows, A_WIDTH + g * CHUNK:A_WIDTH + (g + 1) * CHUNK])
                mu = jnp.mean(v, axis=-1, keepdims=True)
                xc = v - mu
                rstd = lax.rsqrt(jnp.mean(xc * xc, axis=-1, keepdims=True) + EPS)
                xhat = xc * rstd
                vc = (xhat * vg_ref[:, cg] + vb_ref[:, cg]).astype(MM)
                sv = _dot(wt_ref[g], vc) + bcol_ref[g]
                u_s[rows, cg] = u
                gu_s[rows, cg] = sv * gu
                gv_s[rows, cg] = rstd * gv
                xh_s[rows, cg] = xhat
                vc_s[rows, cg] = vc
                ycat[rows, cg] = u * sv
            weight_grad(2, DONE)
            kd = _dup_heads(kv_s[r0:r0 + 2 * CHUNK, 0:CHUNK])
            vd = _dup_heads(kv_s[r0:r0 + 2 * CHUNK, CHUNK:2 * CHUNK])
            for kvh in range(2):
                kd_s[blk * 2 + kvh] = kd[kvh]
                vd_s[blk * 2 + kvh] = vd[kvh]
                q128 = q_s[rows, kvh * CHUNK:(kvh + 1) * CHUNK].astype(F32)
                for gi in range(2):
                    qsel = jnp.where(lo if gi == 0 else ~lo, q128, 0.0).astype(MM)
                    b_qk.append(_dot_nt(qsel, kd[kvh]))
        qks, pcs = [], []
        for g in range(2):
            q128 = q_s[:, 256 + g * CHUNK:256 + (g + 1) * CHUNK].astype(F32)
            for hh in range(2):
                qsel = jnp.where(lot if hh == 0 else ~lot, q128, 0.0).astype(MM)
                qks.append(_dot_nt(qsel, mkv_ref[:, g * CHUNK:(g + 1) * CHUNK]))
        weight_grad(2, DONE)
        for blk in range(bpt):
            first_add = _first_block_mask(jt * bpt + blk)
            for hd in range(4):
                probs, ps = _swa_probs(b_qk[blk * 4 + hd], bias_ref[hd], sink_ref[hd], first_add)
                pb_s[blk * 4 + hd] = probs
                ps_s[blk * 4 + hd] = jnp.broadcast_to(ps, (CHUNK, CHUNK))
                b_pb.append(probs.astype(MM))
        weight_grad(len(dw_cols), DONE)
        for hd in range(4):
            probs = _softmax(qks[hd] * SCALE)
            pc_s[hd] = probs
            pcs.append(probs.astype(MM))
        for blk in range(bpt):
            rows = slice(blk * CHUNK, (blk + 1) * CHUNK)
            outs = [_dot(b_pb[blk * 4 + hd], vd_s[blk * 2 + hd // 2]) for hd in range(4)]
            for kvh in range(2):
                ycat[rows, YB_OFF + kvh * CHUNK:YB_OFF + (kvh + 1) * CHUNK] = jnp.where(
                    lo, outs[2 * kvh], outs[2 * kvh + 1])
        outs = [_dot(pcs[hd], mkv_ref[:, MEM_LEN + (hd // 2) * CHUNK:MEM_LEN + (hd // 2 + 1) * CHUNK])
                for hd in range(4)]
        for g in range(2):
            ycat[:, YC_OFF + g * CHUNK:YC_OFF + (g + 1) * CHUNK] = jnp.where(lot, outs[2 * g], outs[2 * g + 1])

        zt = z_s[...]
        sig = 1.0 / (1.0 + jnp.exp(-zt))
        silu = zt * sig
        yc = ycat[...]
        yb = (yc * silu).astype(MM)
        pre_norm(xn_ref, xpn_ref)
        o = _dot(yb, wo_ref[...])
        project_z()
        r2 = _rms(o)
        nrm = o * r2
        g2v = g2_ref[...]
        e = x_ref[...] + nrm * g2v - t_ref[...]
        l1 = jnp.sum(e * e, axis=-1, keepdims=True)
        loss_ref[...] += jnp.broadcast_to(jnp.sum(l1, axis=0, keepdims=True) * (0.5 / D_MODEL), loss_ref.shape)
        dxo = e * (1.0 / D_MODEL)
        dxo_s[...] = dxo
        dg2_ref[...] += jnp.sum(dxo * nrm, axis=0, keepdims=True)
        dn = dxo * g2v
        do = r2 * (dn - nrm * jnp.mean(dn * nrm, axis=-1, keepdims=True))
        dob = do.astype(MM)
        dy = _dot_nt(dob, wo_ref[...])
        dp_s[:, Z_COL:IN_WIDTH] = (dy * yc * (sig * (1.0 + zt * (1.0 - sig)))).astype(MM)
        dyc[...] = dy * silu
        acc_o[...] += _dot_tn(yb, dob)

        def in_proj_bwd(c0, c1):
            part = _dot(dp_s[:, c0:c1], wi_ref[c0:c1, :])
            if c0 == Z_COL:
                dh_s[...] = part
            else:
                dh_s[...] += part

        in_proj_bwd(Z_COL, IN_WIDTH)

        for blk in range(bpt):
            r0 = blk * CHUNK
            rows = slice(r0, r0 + CHUNK)
            for g in range(A_GROUPS):
                cg = slice(g * CHUNK, (g + 1) * CHUNK)
                cv = slice(A_WIDTH + g * CHUNK, A_WIDTH + (g + 1) * CHUNK)
                dya = dyc[rows, cg]
                dp_s[rows, cg] = (dya * gu_s[rows, cg]).astype(MM)
                dsv = dya * u_s[rows, cg]
                dsvb = dsv.astype(MM)
                dsv_acc[g] += dsv
                dwsp_ref[g] += _dot_nt(dsvb, vc_s[rows, cg])
                dvc = _dot(wtt_ref[g], dsvb)
                xhat = xh_s[rows, cg]
                dvg_ref[:, cg] += jnp.sum(dvc * xhat, axis=0, keepdims=True)
                dvb_ref[:, cg] += jnp.sum(dvc, axis=0, keepdims=True)
                dxh = dvc * vg_ref[:, cg]
                dv = (dxh - jnp.mean(dxh, axis=-1, keepdims=True)
                      - xhat * jnp.mean(dxh * xhat, axis=-1, keepdims=True))
                dp_s[rows, cv] = (dv * gv_s[rows, cg]).astype(MM)
        in_proj_bwd(0, UV_W)
        b_dosel, b_dp, b_dss = [], [], []
        for blk in range(bpt):
            rows = slice(blk * CHUNK, (blk + 1) * CHUNK)
            for hd in range(4):
                do128 = dyc[rows, YB_OFF + (hd // 2) * CHUNK:YB_OFF + (hd // 2 + 1) * CHUNK]
                b_dosel.append(jnp.where(lo if hd % 2 == 0 else ~lo, do128, 0.0).astype(MM))
                b_dp.append(_dot_nt(b_dosel[-1], vd_s[blk * 2 + hd // 2]))
        dosels, dps, dsss = [], [], []
        for hd in range(4):
            do128 = dyc[:, YC_OFF + (hd // 2) * CHUNK:YC_OFF + (hd // 2 + 1) * CHUNK]
            dosels.append(jnp.where(lot if hd % 2 == 0 else ~lot, do128, 0.0).astype(MM))
            dps.append(_dot_nt(dosels[hd], mkv_ref[:, MEM_LEN + (hd // 2) * CHUNK:MEM_LEN + (hd // 2 + 1) * CHUNK]))
        for blk in range(bpt):
            for hd in range(4):
                probs = pb_s[blk * 4 + hd]
                ps = ps_s[blk * 4 + hd][:, 0:1]
                dp = b_dp[blk * 4 + hd]
                delta = jnp.sum(probs * dp, axis=-1, keepdims=True)
                ds = probs * (dp - delta)
                dbias_acc[hd] += ds
                dsink_acc[hd:hd + 1, :] += jnp.broadcast_to(-jnp.sum(ps * delta, axis=0, keepdims=True), (1, CHUNK))
                b_dss.append((ds * SCALE).astype(MM))
        for hd in range(4):
            probs = pc_s[hd]
            ds = probs * (dps[hd] - jnp.sum(probs * dps[hd], axis=-1, keepdims=True))
            dsss.append((ds * SCALE).astype(MM))
        for blk in range(bpt):
            r0 = blk * CHUNK
            rows = slice(r0, r0 + CHUNK)
            dk_f, dv_f = [], []
            for kvh in range(2):
                kd = kd_s[blk * 2 + kvh]
                q128 = q_s[rows, kvh * CHUNK:(kvh + 1) * CHUNK].astype(F32)
                dq128 = jnp.zeros((CHUNK, CHUNK), F32)
                dkd = jnp.zeros((2 * CHUNK, CHUNK), F32)
                dvd = jnp.zeros((2 * CHUNK, CHUNK), F32)
                for gi in range(2):
                    hd = 2 * kvh + gi
                    half = lo if gi == 0 else ~lo
                    qsel = jnp.where(half, q128, 0.0).astype(MM)
                    dq128 = dq128 + jnp.where(half, _dot(b_dss[blk * 4 + hd], kd), 0.0)
                    dkd = dkd + _dot_tn(b_dss[blk * 4 + hd], qsel)
                    dvd = dvd + _dot_tn(pb_s[blk * 4 + hd].astype(MM), b_dosel[blk * 4 + hd])
                dp_s[rows, SQ_COL + kvh * CHUNK:SQ_COL + (kvh + 1) * CHUNK] = dq128.astype(MM)
                dk_f.append(dkd + pltpu.roll(dkd, 64, 1))
                dv_f.append(dvd + pltpu.roll(dvd, 64, 1))
            dkv_acc[r0:r0 + 2 * CHUNK, 0:CHUNK] += jnp.where(lob, dk_f[0], dk_f[1])
            dkv_acc[r0:r0 + 2 * CHUNK, CHUNK:2 * CHUNK] += jnp.where(lob, dv_f[0], dv_f[1])
        dp_s[:, SK_COL:MQ_COL] = dkv_acc[CHUNK:CHUNK + tm, :].astype(MM)
        for g in range(2):
            q128 = q_s[:, 256 + g * CHUNK:256 + (g + 1) * CHUNK].astype(F32)
            k128 = mkv_ref[:, g * CHUNK:(g + 1) * CHUNK]
            dq128 = jnp.zeros((tm, CHUNK), F32)
            dk128 = jnp.zeros((MEM_LEN, CHUNK), F32)
            dv128 = jnp.zeros((MEM_LEN, CHUNK), F32)
            for hh in range(2):
                hd = 2 * g + hh
                half = lot if hh == 0 else ~lot
                qsel = jnp.where(half, q128, 0.0).astype(MM)
                dq128 = dq128 + jnp.where(half, _dot(dsss[hd], k128), 0.0)
                dk128 = dk128 + _dot_tn(dsss[hd], qsel)
                dv128 = dv128 + _dot_tn(pc_s[hd].astype(MM), dosels[hd])
            dp_s[:, MQ_COL + g * CHUNK:MQ_COL + (g + 1) * CHUNK] = dq128.astype(MM)
            dmkv_ref[:, g * CHUNK:(g + 1) * CHUNK] += dk128
            dmkv_ref[:, MEM_LEN + g * CHUNK:MEM_LEN + (g + 1) * CHUNK] += dv128

        in_proj_bwd(SQ_COL, Z_COL)
        project_uv()
        dh = dh_s[...]
        r = r_s[NOW]
        nx = x_ref[...] * r
        dg1_ref[...] += jnp.sum(dh * nx, axis=0, keepdims=True)
        dnx = dh * g1v
        gx_ref[...] = dxo_s[...] + r * (dnx - nx * jnp.mean(dnx * nx, axis=-1, keepdims=True))

        @pl.when(step == last_step)
        def _():
            dw_cols.extend((c, c + DW_COLS) for c in range(0, IN_WIDTH, DW_COLS))
            weight_grad(len(dw_cols), NOW)
            out_i = pltpu.make_async_copy(acc_i, dwi_hbm, sems.at[0])
            out_o = pltpu.make_async_copy(acc_o, dwo_hbm, sems.at[1])
            out_i.start()
            out_o.start()
            r_ = lax.broadcasted_iota(jnp.int32, (CHUNK, CHUNK), 0)
            c_ = lax.broadcasted_iota(jnp.int32, (CHUNK, CHUNK), 1)
            for g in range(A_GROUPS):
                dwsp_ref[g] = jnp.where(r_ >= c_, dwsp_ref[g], 0.0)
                dbs_ref[g:g + 1, :] = jnp.sum(dsv_acc[g].T, axis=0, keepdims=True)
            rows8 = lax.broadcasted_iota(jnp.int32, (8, CHUNK), 0)
            cols8 = lax.broadcasted_iota(jnp.int32, (8, CHUNK), 1)
            sk = jnp.zeros((8, CHUNK), F32)
            for hd in range(4):
                sk = sk + jnp.where((rows8 == 0) & (cols8 == hd),
                                    jnp.broadcast_to(dsink_acc[hd:hd + 1, :], (8, CHUNK)), 0.0)
            dsink_ref[...] = sk
            bk = bk_ref[...]
            valid = _window_valid()
            rrow = lax.broadcasted_iota(jnp.int32, (N_BUCKETS, CHUNK), 0)
            rcol = lax.broadcasted_iota(jnp.int32, (N_BUCKETS, CHUNK), 1)
            acc = jnp.zeros((N_BUCKETS, CHUNK), F32)
            for bb in range(N_BUCKETS):
                hit = (bk == bb) & valid
                for hd in range(4):
                    part = jnp.sum(jnp.where(hit, dbias_acc[hd], 0.0), axis=-1, keepdims=True)
                    tot = jnp.sum(part, axis=0, keepdims=True)
                    acc = acc + jnp.where((rrow == bb) & (rcol == hd), jnp.broadcast_to(tot, (N_BUCKETS, CHUNK)), 0.0)
            drel_ref[...] = acc
            out_i.wait()
            out_o.wait()

    after = lambda b, j: jnp.minimum(b * nt + j + 1, last_step)
    tile = pl.BlockSpec((tm, D_MODEL), lambda b, j: (tile_at(b * nt + j), 0))
    tile_after = pl.BlockSpec((tm, D_MODEL), lambda b, j: (tile_at(after(b, j)), 0))
    halo = pl.BlockSpec((CHUNK, D_MODEL), lambda b, j: (block_before(b * nt + j), 0))
    halo_after = pl.BlockSpec((CHUNK, D_MODEL), lambda b, j: (block_before(after(b, j)), 0))
    per_batch = lambda r, w: pl.BlockSpec((None, r, w), lambda b, j: (b, 0, 0))
    anyspec = pl.BlockSpec(memory_space=pl.ANY)
    grp = (A_GROUPS, CHUNK, CHUNK)
    return pl.pallas_call(
        body, name="layer", grid=(nb, nt),
        out_shape=(jax.ShapeDtypeStruct((t, D_MODEL), F32),
                   jax.ShapeDtypeStruct((nb, MEM_LEN, 2 * MEM_LEN), F32),
                   jax.ShapeDtypeStruct((IN_WIDTH, D_MODEL), F32),
                   jax.ShapeDtypeStruct((D_MODEL, D_MODEL), F32),
                   jax.ShapeDtypeStruct((1, D_MODEL), F32),
                   jax.ShapeDtypeStruct((1, D_MODEL), F32),
                   jax.ShapeDtypeStruct((8, CHUNK), F32),
                   jax.ShapeDtypeStruct(grp, F32),
                   jax.ShapeDtypeStruct((A_GROUPS, CHUNK), F32),
                   jax.ShapeDtypeStruct((1, A_WIDTH), F32),
                   jax.ShapeDtypeStruct((1, A_WIDTH), F32),
                   jax.ShapeDtypeStruct((8, CHUNK), F32),
                   jax.ShapeDtypeStruct((N_BUCKETS, CHUNK), F32)),
        in_specs=[tile, halo, tile_after, halo_after, tile, per_batch(MEM_LEN, 2 * MEM_LEN),
                  _full((4, CHUNK, 2 * CHUNK)),
                  pl.BlockSpec(memory_space=pltpu.SMEM),
                  _full((1, A_WIDTH)), _full((1, A_WIDTH)),
                  _full(grp), _full(grp), _full(grp),
                  _full((1, D_MODEL)), _full((1, D_MODEL)),
                  _full((IN_WIDTH, D_MODEL), single=True), _full((D_MODEL, D_MODEL), single=True),
                  _full((CHUNK, 2 * CHUNK))],
        out_specs=(tile, per_batch(MEM_LEN, 2 * MEM_LEN), anyspec, anyspec,
                   _full((1, D_MODEL)), _full((1, D_MODEL)), _full((8, CHUNK)),
                   _full(grp), _full((A_GROUPS, CHUNK)), _full((1, A_WIDTH)), _full((1, A_WIDTH)),
                   _full((8, CHUNK)), _full((N_BUCKETS, CHUNK))),
        scratch_shapes=[pltpu.VMEM((IN_WIDTH, D_MODEL), F32), pltpu.VMEM((D_MODEL, D_MODEL), F32),
                        pltpu.VMEM((tm, UV_W), F32), pltpu.VMEM((tm, Z_W), F32),
                        pltpu.VMEM((tm, 512), MM), pltpu.VMEM((tm + CHUNK, 2 * CHUNK), MM),
                        pltpu.VMEM((3, tm, D_MODEL), MM), pltpu.VMEM((CHUNK, D_MODEL), MM),
                        pltpu.VMEM((tm, IN_WIDTH), MM),
                        pltpu.VMEM((tm, D_MODEL), F32),
                        pltpu.VMEM((tm, D_MODEL), F32), pltpu.VMEM((2, tm, 1), F32),
                        pltpu.VMEM((tm, D_MODEL), F32), pltpu.VMEM((tm, D_MODEL), F32)]
                       + [pltpu.VMEM((tm, A_WIDTH), F32) for _ in range(4)]
                       + [pltpu.VMEM((tm, A_WIDTH), MM),
                          pltpu.VMEM((bpt * 4, CHUNK, 2 * CHUNK), F32),
                          pltpu.VMEM((bpt * 4, CHUNK, CHUNK), F32),
                          pltpu.VMEM((4, tm, MEM_LEN), F32),
                          pltpu.VMEM((bpt * 2, 2 * CHUNK, CHUNK), MM),
                          pltpu.VMEM((bpt * 2, 2 * CHUNK, CHUNK), MM),
                          pltpu.VMEM((tm + CHUNK, 2 * CHUNK), F32),
                          pltpu.VMEM((4, CHUNK, 2 * CHUNK), F32),
                          pltpu.VMEM(grp, F32),
                          pltpu.VMEM((8, CHUNK), F32),
                          pltpu.SemaphoreType.DMA((2,))],
        compiler_params=_params(dimension_semantics=("arbitrary", "arbitrary")),
    )(x2, x2, x2, x2, tgt2, mkv3, bias, sinks, vg, vb, wt, wtt, bcol, g1, g2, w_in_t, w_o, buckets)


class _ShardReduce:
    def __init__(self, pos, g, bufs, sems):
        self.x, self.y, self.c = pos
        self.g = g
        self.own, self.rcv, self.sbuf, self.rbuf, self.cbuf = bufs
        self.ld, self.sa, self.ra, self.sb, self.rb = sems
        self.nrow = g.shape[1]
        self.here = (self.x, self.y, self.c)
        self.sib = (self.x, self.y, 1 - self.c)
        self.first, self.second, self.far = _route(*pos)

    def _load(self, q):
        return pltpu.make_async_copy(self.g.at[2 * q + self.c], self.own.at[q], self.ld.at[q])

    def _to_sib(self, q, to):
        return _remote(self.g.at[2 * q + 1 - self.c], self.rcv.at[q], self.sa.at[q], self.ra.at[q], to)

    def _send(self, k, to):
        dst = self.cbuf.at[0] if k == 1 else self.rbuf.at[0 if k == 0 else 1]
        return _remote(self.sbuf.at[k], dst, self.sb.at[k], self.rb.at[k], to)

    def _stage(self, k, which, extra=None):
        def cast(r):
            v = self.rcv[which, r, :]
            if extra is not None:
                v = v + extra[0, r, :].astype(F32)
            self.sbuf[k, r, :] = v.astype(BF16)

        _rows_loop(self.nrow, cast)

    @staticmethod
    def _q(chip):
        return 2 * chip[0] + chip[1]

    def start(self):
        for q in range(4):
            self._load(q).start()
            self._to_sib(q, self.sib).start()

    def mid(self):
        for q in range(4):
            self._load(q).wait()
            self._to_sib(q, self.here).wait_recv()

        def add(r):
            for q in range(4):
                self.rcv[q, r, :] = self.rcv[q, r, :] + self.own[q, r, :]

        _rows_loop(self.nrow, add)
        to_first = (self.first[0], self.first[1], self.c)
        self._stage(0, self._q(self.first))
        self._send(0, to_first).start()
        self._stage(1, self._q(self.far))
        self._send(1, to_first).start()

    def pass_on(self):
        self._send(1, self.here).wait_recv()
        self._stage(2, self._q(self.second), extra=self.cbuf)
        self._send(2, (self.second[0], self.second[1], self.c)).start()

    def finish(self, out):
        self._send(0, self.here).wait_recv()
        self._send(2, self.here).wait_recv()
        which = 2 * self.x + self.y

        def tot(r):
            out[r, :] = (self.rcv[which, r, :] + self.rbuf[0, r, :].astype(F32)) + self.rbuf[1, r, :].astype(F32)

        _rows_loop(self.nrow, tot)
        for q in range(4):
            self._to_sib(q, self.sib).wait_send()
        to_first = (self.first[0], self.first[1], self.c)
        self._send(0, to_first).wait_send()
        self._send(1, to_first).wait_send()
        self._send(2, (self.second[0], self.second[1], self.c)).wait_send()


def _reduce_scratch(shape):
    return [pltpu.VMEM((4,) + shape, F32), pltpu.VMEM((4,) + shape, F32),
            pltpu.VMEM((3,) + shape, BF16), pltpu.VMEM((2,) + shape, BF16), pltpu.VMEM((1,) + shape, BF16),
            pltpu.SemaphoreType.DMA((4,)), pltpu.SemaphoreType.DMA((4,)), pltpu.SemaphoreType.DMA((4,)),
            pltpu.SemaphoreType.DMA((3,)), pltpu.SemaphoreType.DMA((3,))]


_N_RED = 10

_S_LAYOUT = (((1, D_MODEL), 0), ((1, D_MODEL), 8), ((1, D_MODEL), 16),
             ((1, A_WIDTH), 24), ((1, A_WIDTH), 28), ((A_GROUPS, CHUNK), 32),
             ((1, 4), 36), ((N_BUCKETS, 4), 40),
             ((A_GROUPS * CHUNK, CHUNK), 72))
_LOSS_ROW = 37
_W_SP_ROW = _S_LAYOUT[-1][1]
_S_ROWS = _W_SP_ROW + A_GROUPS * CHUNK
_N_SMALL = len(_S_LAYOUT)


def _pack_rows(dst, refs):
    for (shp, r0), ref in zip(_S_LAYOUT, refs):
        if shp[0] == 1 and shp[1] >= CHUNK:
            for i in range(shp[1] // CHUNK):
                dst[r0 + i:r0 + i + 1, :] = ref[:, i * CHUNK:(i + 1) * CHUNK]
        elif ref.shape[-1] == CHUNK:
            dst[r0:r0 + shp[0], :] = ref[0:shp[0], :]
        else:
            dst[r0:r0 + shp[0], 0:shp[1]] = ref[...]


def _unpack_rows(src, refs):
    for (shp, r0), ref in zip(_S_LAYOUT, refs):
        if shp[0] == 1 and shp[1] >= CHUNK:
            for i in range(shp[1] // CHUNK):
                ref[:, i * CHUNK:(i + 1) * CHUNK] = src[r0 + i:r0 + i + 1, :]
        elif shp[1] == CHUNK:
            ref[...] = src[r0:r0 + shp[0], :]
        else:
            if tuple(ref.shape) == (shp[1], shp[0]):
                ref[...] = src[r0:r0 + CHUNK, :].T[0:shp[1], 0:shp[0]]
            else:
                ref[...] = src[r0:r0 + shp[0], 0:shp[1]]


_MEM_G = 2


def _greduce(ga, gb, dmkv, mem2, gm, w_mkv, small_g, loss_p):
    shp_c = (SHARD_O, 2 * MEM_LEN)
    shapes = (shp_c, gb.shape[1:], ga.shape[1:])
    rs = _S_ROWS

    def body(*refs):
        it = iter(refs)
        take = lambda n: [next(it) for _ in range(n)]
        gb_ref, ga_ref, d_ref, m_ref, gm_ref, wm_ref = take(6)
        sg_refs = take(_N_SMALL - 1)
        loss_ref, = take(1)
        oc, ob, oa, ogs = take(4)
        red = take(3 * _N_RED)
        gs_ref, rs_a, rs_b, rs_w, gc_ref, dgm_ref = take(6)
        ssem_a, rsem_a, ssem_b, rsem_b = take(4)

        pos = _position()
        x, y, cc = pos
        myq = 2 * x + y
        here, sib = (x, y, cc), (x, y, 1 - cc)
        chips = _other_chips(x, y)
        reducers = [_ShardReduce(pos, g, red[k * _N_RED:k * _N_RED + 5], red[k * _N_RED + 5:(k + 1) * _N_RED])
                    for k, g in enumerate((gc_ref, gb_ref, ga_ref))]
        for rd in reducers[1:]:
            rd.start()

        xf = m_ref[...]
        nm = xf * _rms(xf)
        hm = (nm * gm_ref[...]).astype(MM)
        d = d_ref[...].astype(MM)
        for o in range(N_DEV):
            gc_ref[o] = _dot_tn(hm[:, o * SHARD_O:(o + 1) * SHARD_O], d)
        dgm_ref[...] = jnp.sum(_dot_nt(d, wm_ref[...]) * nm, axis=0, keepdims=True)
        reducers[0].start()

        gs_ref[...] = jnp.zeros_like(gs_ref)
        _pack_rows(gs_ref, sg_refs[:_MEM_G] + [dgm_ref] + sg_refs[_MEM_G:])
        gs_ref[_LOSS_ROW:_LOSS_ROW + 1, :] = loss_ref[0:1, :]
        small_a = _remote(gs_ref, rs_a, ssem_a, rsem_a, sib)
        small_a.start()

        _remote(gs_ref, rs_a, ssem_a, rsem_a, here).wait_recv()
        rs_b[myq] = gs_ref[0:_W_SP_ROW, :] + rs_a[0:_W_SP_ROW, :]
        rs_w[myq] = (gs_ref[_W_SP_ROW:rs, :] + rs_a[_W_SP_ROW:rs, :]).astype(BF16)
        small_b = []
        for j, chip in enumerate(chips):
            to = (chip[0], chip[1], cc)
            small_b.append(_remote(rs_b.at[myq], rs_b.at[myq], ssem_b.at[0, j], rsem_b.at[0, j], to))
            small_b.append(_remote(rs_w.at[myq], rs_w.at[myq], ssem_b.at[1, j], rsem_b.at[1, j], to))
        for cp in small_b:
            cp.start()
        late_last = reducers[1:] + reducers[:1]
        for rd in late_last:
            rd.mid()
        for rd in late_last:
            rd.pass_on()

        for j in range(3):
            _remote(rs_b.at[myq], rs_b.at[myq], ssem_b.at[0, j], rsem_b.at[0, j], here).wait_recv()
            _remote(rs_w.at[myq], rs_w.at[myq], ssem_b.at[1, j], rsem_b.at[1, j], here).wait_recv()
        ogs[0:_W_SP_ROW, :] = ((rs_b[0] + rs_b[1]) + rs_b[2]) + rs_b[3]

        def tot_w(r):
            w = [rs_w[q, r, :].astype(F32) for q in range(4)]
            ogs[pl.ds(pl.multiple_of(_W_SP_ROW + r.start, 8), _ROWS), :] = ((w[0] + w[1]) + w[2]) + w[3]

        _rows_loop(rs - _W_SP_ROW, tot_w)
        for rd, out in zip(late_last, (ob, oa, oc)):
            rd.finish(out)
        small_a.wait_send()
        for cp in small_b:
            cp.wait_send()

    vm = pl.BlockSpec(memory_space=pltpu.VMEM)
    anyspec = pl.BlockSpec(memory_space=pl.ANY)
    scratch = []
    for shp in shapes:
        scratch += _reduce_scratch(shp)
    scratch += [pltpu.VMEM((rs, CHUNK), F32), pltpu.VMEM((rs, CHUNK), F32),
                pltpu.VMEM((4, _W_SP_ROW, CHUNK), F32), pltpu.VMEM((4, rs - _W_SP_ROW, CHUNK), BF16),
                pltpu.VMEM((N_DEV,) + shp_c, F32), pltpu.VMEM((1, D_MODEL), F32),
                pltpu.SemaphoreType.DMA, pltpu.SemaphoreType.DMA,
                pltpu.SemaphoreType.DMA((2, 3)), pltpu.SemaphoreType.DMA((2, 3))]
    tc, tb, ta, ts = pl.pallas_call(
        body, name="greduce",
        out_shape=tuple([jax.ShapeDtypeStruct(shp, F32) for shp in shapes] + [jax.ShapeDtypeStruct((rs, CHUNK), F32)]),
        in_specs=[anyspec] * 2 + [vm] * (4 + _N_SMALL),
        out_specs=(vm, vm, vm, vm),
        scratch_shapes=scratch,
        compiler_params=_params(),
    )(gb, ga, dmkv, mem2, gm, w_mkv, *small_g, loss_p)
    return ta, tb, tc, ts


def _adamw(w, g, m, v):
    m = ADAM_B1 * m + (1.0 - ADAM_B1) * g
    v = ADAM_B2 * v + (1.0 - ADAM_B2) * (g * g)
    m_hat = m / (1.0 - ADAM_B1 ** ADAM_STEP)
    v_hat = v / (1.0 - ADAM_B2 ** ADAM_STEP)
    delta = -ADAM_LR * (m_hat / (jnp.sqrt(v_hat) + ADAM_EPS) + ADAM_WD * w)
    return delta, m, v


def _update(ta, tb, tc, ts, big_wmv, small_wmv):
    shapes = (ta.shape, tb.shape, tc.shape)
    rs = _S_ROWS
    small_shapes = [tuple(a.shape) for a in small_wmv[0]]

    def body(*refs):
        it = iter(refs)
        take = lambda n: [next(it) for _ in range(n)]
        ga_ref, gb_ref, gc_ref, gs_ref = take(4)
        wa, ma, va, wb, mb, vb_, wc, mc, vc = take(9)
        sw_refs, sm_refs, sv_refs = take(_N_SMALL), take(_N_SMALL), take(_N_SMALL)
        oga, oda, oma, ova, ogb, odb, omb, ovb, ogc, odc, omc, ovc = take(12)
        so_refs = [take(_N_SMALL) for _ in range(4)]
        loss_out, = take(1)
        ws, ms, vs, ods, oms, ovs = take(6)

        for buf in (ws, ms, vs):
            buf[...] = jnp.zeros_like(buf)
        _pack_rows(ws, sw_refs)
        _pack_rows(ms, sm_refs)
        _pack_rows(vs, sv_refs)

        big = ((ga_ref, wa, ma, va, oga, oda, oma, ova), (gb_ref, wb, mb, vb_, ogb, odb, omb, ovb),
               (gc_ref, wc, mc, vc, ogc, odc, omc, ovc))
        for arr in range(3):
            g_r, w_r, m_r, v_r, og, od, om, ov = big[arr]

            def upd(r, g_r=g_r, w_r=w_r, m_r=m_r, v_r=v_r, og=og, od=od, om=om, ov=ov):
                g = g_r[r, :]
                d, m, v = _adamw(w_r[r, :], g, m_r[r, :], v_r[r, :])
                og[r, :] = g
                od[r, :] = d
                om[r, :] = m
                ov[r, :] = v

            _rows_loop(shapes[arr][0], upd)

        def upd_s(i, _):
            r = pl.ds(pl.multiple_of(i * 8, 8), 8)
            d, m, v = _adamw(ws[r, :], gs_ref[r, :], ms[r, :], vs[r, :])
            ods[r, :] = d
            oms[r, :] = m
            ovs[r, :] = v
            return 0

        lax.fori_loop(0, rs // 8, upd_s, 0)
        for k, buf in enumerate((gs_ref, ods, oms, ovs)):
            _unpack_rows(buf, so_refs[k])
        loss_out[...] = gs_ref[_LOSS_ROW:_LOSS_ROW + 1, 0:1]

    vm = pl.BlockSpec(memory_space=pltpu.VMEM)
    big_out = []
    for shp in shapes:
        big_out += [jax.ShapeDtypeStruct(shp, F32)] * 4
    small_out = [jax.ShapeDtypeStruct(shp[::-1] if shp == (N_BUCKETS, 4) else shp, F32) for shp in small_shapes] * 4
    out_shape = tuple(big_out + small_out + [jax.ShapeDtypeStruct((1, 1), F32)])
    n_in = 4 + 9 + 3 * _N_SMALL
    return pl.pallas_call(
        body, name="update",
        out_shape=out_shape,
        in_specs=[vm] * n_in,
        out_specs=tuple([vm] * len(out_shape)),
        scratch_shapes=[pltpu.VMEM((rs, CHUNK), F32) for _ in range(6)],
        compiler_params=_params(),
    )(ta, tb, tc, ts, *big_wmv, *small_wmv[0], *small_wmv[1], *small_wmv[2])


def kernel(x, mem, pre_norm_g, post_norm_g, mem_norm_g, w_in, w_mem_kv, v_norm_g, v_norm_b, w_spatial, b_spatial, attn_sinks, rel_bias, w_out, loss_target, m_pre_norm_g, m_post_norm_g, m_mem_norm_g, m_w_in, m_w_mem_kv, m_v_norm_g, m_v_norm_b, m_w_spatial, m_b_spatial, m_attn_sinks, m_rel_bias, m_w_out, v_pre_norm_g, v_post_norm_g, v_mem_norm_g, v_w_in, v_w_mem_kv, v_v_norm_g, v_v_norm_b, v_w_spatial, v_b_spatial, v_attn_sinks, v_rel_bias, v_w_out):
    sh_a = (w_in[0].T, m_w_in[0].T, v_w_in[0].T)
    sh_b = (w_out[0], m_w_out[0], v_w_out[0])
    sh_c = (w_mem_kv[0], m_w_mem_kv[0], v_w_mem_kv[0])
    nb, s, _ = x.shape
    t = nb * s
    x2 = x.reshape(t, D_MODEL)
    tgt2 = loss_target.reshape(t, D_MODEL)
    mem2 = mem.reshape(nb * MEM_LEN, D_MODEL)
    buckets = jnp.asarray(_t5_buckets())

    wa, wb, wc, bias, wt, wtt, bcol, mkv = _wgather(sh_a[0], sh_b[0], sh_c[0], rel_bias, w_spatial[0], b_spatial[0],
                                                    buckets, mem2, mem_norm_g)
    w_mkv = wc.reshape(D_MODEL, 2 * MEM_LEN)
    gx, dmkv, dwi, dwo, dg1, dg2, loss_p, dwsp, dbs, dvg, dvb, dsink, drel = _layer(
        x2, tgt2, mkv.reshape(nb, MEM_LEN, 2 * MEM_LEN), bias, attn_sinks.reshape(4), v_norm_g, v_norm_b, wt, wtt, bcol,
        pre_norm_g, post_norm_g, wa.reshape(IN_WIDTH, D_MODEL), wb.reshape(D_MODEL, D_MODEL), buckets,
        nb, s, min(256, s))
    gx = gx.reshape(nb, s, D_MODEL)
    small_grads = [dg1, dg2, dvg, dvb, dbs, dsink, drel, dwsp.reshape(A_GROUPS * CHUNK, CHUNK)]

    small_names = ["pre_norm_g", "post_norm_g", "mem_norm_g", "v_norm_g", "v_norm_b", "b_spatial", "attn_sinks",
                   "rel_bias", "w_spatial"]
    given = dict(pre_norm_g=(pre_norm_g, m_pre_norm_g, v_pre_norm_g), post_norm_g=(post_norm_g, m_post_norm_g, v_post_norm_g),
                 mem_norm_g=(mem_norm_g, m_mem_norm_g, v_mem_norm_g), v_norm_g=(v_norm_g, m_v_norm_g, v_v_norm_g),
                 v_norm_b=(v_norm_b, m_v_norm_b, v_v_norm_b), b_spatial=(b_spatial, m_b_spatial, v_b_spatial),
                 attn_sinks=(attn_sinks, m_attn_sinks, v_attn_sinks), rel_bias=(rel_bias, m_rel_bias, v_rel_bias),
                 w_spatial=(w_spatial, m_w_spatial, v_w_spatial))
    small_wmv = [[given[n][k].reshape(shp) for n, (shp, _) in zip(small_names, _S_LAYOUT)] for k in range(3)]

    ta, tb, tc, ts = _greduce(dwi.reshape(N_DEV, SHARD_IN, D_MODEL), dwo.reshape(N_DEV, SHARD_O, D_MODEL),
                              dmkv.reshape(nb * MEM_LEN, 2 * MEM_LEN), mem2, mem_norm_g, w_mkv, small_grads, loss_p)
    outs = _update(ta, tb, tc, ts, (*sh_a, *sh_b, *sh_c), small_wmv)
    ra, rb, rc = outs[0:4], outs[4:8], outs[8:12]
    loss = outs[12 + 4 * _N_SMALL].reshape(())

    res = {}
    for k, kind in enumerate(("grad", "delta", "new_m", "new_v")):
        res[kind, "w_in"] = ra[k].T[None]
        res[kind, "w_out"] = rb[k][None]
        res[kind, "w_mem_kv"] = rc[k][None]
        for i, n in enumerate(small_names):
            o = outs[12 + k * _N_SMALL + i]
            res[kind, n] = o.T if n == "rel_bias" else o.reshape(given[n][0].shape)
    order = ["pre_norm_g", "post_norm_g", "mem_norm_g", "w_in", "w_mem_kv", "v_norm_g", "v_norm_b", "w_spatial",
             "b_spatial", "attn_sinks", "rel_bias", "w_out"]
    flat = [res[kind, n] for kind in ("grad", "delta", "new_m", "new_v") for n in order]
    return (loss, gx, *flat)
```

```python
import numpy as np
import jax
import jax.numpy as jnp
from jax import lax
from jax.experimental import pallas as pl
from jax.experimental.pallas import tpu as pltpu

F32 = jnp.float32
BF16 = jnp.bfloat16
MM = jnp.bfloat16

D_MODEL = 1024
CHUNK = 128
A_GROUPS = 4
A_WIDTH = 512
UV_W = 1024
QKV_W = 768
Z_W = 1024
IN_WIDTH = UV_W + QKV_W + Z_W
MEM_LEN = 256
N_BUCKETS = 32
MAX_DISTANCE = 128
EPS = 1e-6
NEG = -1e30
SCALE = 0.125
N_DEV = 8
SHARD_IN = IN_WIDTH // N_DEV
SHARD_O = D_MODEL // N_DEV

SQ_COL, SK_COL, SV_COL, MQ_COL, Z_COL = UV_W, UV_W + 256, UV_W + 384, UV_W + 512, UV_W + QKV_W
DW_PIECES = ((0, SQ_COL), (SQ_COL, Z_COL), (Z_COL, IN_WIDTH))
YB_OFF, YC_OFF = 512, 768

ADAM_LR = 0.001
ADAM_B1 = 0.9
ADAM_B2 = 0.999
ADAM_EPS = 1e-08
ADAM_WD = 0.01
ADAM_STEP = 10

VMEM_LIMIT = 60 * 1024 * 1024

_GELU_C = 0.7978845608028654
_GELU_A = 0.044715

MESH = pl.DeviceIdType.MESH
_ROWS = 32


def _dot(a, b):
    return lax.dot_general(a, b, (((1,), (0,)), ((), ())), preferred_element_type=F32)


def _dot_nt(a, b):
    return lax.dot_general(a, b, (((1,), (1,)), ((), ())), preferred_element_type=F32)


def _dot_tn(a, b):
    return lax.dot_general(a, b, (((0,), (0,)), ((), ())), preferred_element_type=F32)


def _gelu_and_grad(x):
    x2 = x * x
    t = jnp.tanh(_GELU_C * (x + _GELU_A * x * x2))
    g = 0.5 * x * (1.0 + t)
    dg = 0.5 * (1.0 + t) + 0.5 * x * (1.0 - t * t) * (_GELU_C * (1.0 + 3.0 * _GELU_A * x2))
    return g, dg


def _t5_buckets():
    qi = np.arange(CHUNK)[:, None]
    kj = np.arange(2 * CHUNK)[None, :]
    n = np.maximum(qi + CHUNK - kj, 0)
    max_exact = N_BUCKETS // 2
    large = max_exact + (np.log(np.maximum(n, 1) / max_exact) / np.log(MAX_DISTANCE / max_exact)
                         * (N_BUCKETS - max_exact)).astype(np.int32)
    large = np.minimum(large, N_BUCKETS - 1)
    return np.where(n < max_exact, n, large).astype(np.int32)


def _params(**kw):
    return pltpu.CompilerParams(vmem_limit_bytes=VMEM_LIMIT, **kw)


def _full(shape, single=False):
    nd = len(shape)
    if single:
        return pl.BlockSpec(shape, lambda *_: (0,) * nd, pipeline_mode=pl.Buffered(1))
    return pl.BlockSpec(shape, lambda *_: (0,) * nd)


def _window_valid():
    qi = lax.broadcasted_iota(jnp.int32, (CHUNK, 2 * CHUNK), 0)
    kj = lax.broadcasted_iota(jnp.int32, (CHUNK, 2 * CHUNK), 1)
    dist = qi + CHUNK - kj
    return (dist >= 0) & (dist < CHUNK)


def _position():
    return lax.axis_index("x"), lax.axis_index("y"), lax.axis_index("c")


def _other_chips(x, y):
    return [(1 - x, y), (x, 1 - y), (1 - x, 1 - y)]


def _route(x, y, c):
    first = (x * c + (1 - x) * (1 - c), y * (1 - c) + (1 - y) * c)
    second = (x * (1 - c) + (1 - x) * c, y * c + (1 - y) * (1 - c))
    return first, second, (1 - x, 1 - y)


def _remote(src, dst, ssem, rsem, to):
    return pltpu.make_async_remote_copy(src_ref=src, dst_ref=dst, send_sem=ssem, recv_sem=rsem,
                                        device_id=to, device_id_type=MESH)


def _rows_loop(nrow, fn):
    def step(i, _):
        fn(pl.ds(pl.multiple_of(i * _ROWS, _ROWS), _ROWS))
        return 0

    lax.fori_loop(0, nrow // _ROWS, step, 0)


class _Gather:
    def __init__(self, pos, out, ssem, rsem):
        self.x, self.y, self.c = pos
        self.out, self.ssem, self.rsem = out, ssem, rsem
        self.me = 4 * self.x + 2 * self.y + self.c
        self.here = (self.x, self.y, self.c)
        self.sib = (self.x, self.y, 1 - self.c)
        self.first, self.second, self.far = _route(*pos)

    def _copy(self, k, blk, to):
        r = self.out.at[blk]
        return _remote(r, r, self.ssem.at[k], self.rsem.at[k], to)

    def _idx(self, chip, core):
        return 4 * chip[0] + 2 * chip[1] + core

    def _on(self, chip):
        return (chip[0], chip[1], self.c)

    def start(self):
        self._copy(0, self.me, self.sib).start()
        self._copy(1, self.me, self._on(self.first)).start()
        self._copy(2, self.me, self._on(self.second)).start()

    def forward(self):
        c = self.c
        self._copy(1, self._idx(self.first, c), self.here).wait_recv()
        self._copy(3, self._idx(self.first, c), self._on(self.second)).start()
        self._copy(4, self._idx(self.first, c), self.sib).start()
        self._copy(2, self._idx(self.second, c), self.here).wait_recv()
        self._copy(5, self._idx(self.second, c), self.sib).start()
        self._copy(3, self._idx(self.far, c), self.here).wait_recv()
        self._copy(6, self._idx(self.far, c), self.sib).start()

    def finish(self):
        c = self.c
        self._copy(0, self._idx((self.x, self.y), 1 - c), self.here).wait_recv()
        for k, chip in ((4, self.second), (5, self.first), (6, self.far)):
            self._copy(k, self._idx(chip, 1 - c), self.here).wait_recv()
        self._copy(0, self.me, self.sib).wait_send()
        self._copy(1, self.me, self._on(self.first)).wait_send()
        self._copy(2, self.me, self._on(self.second)).wait_send()
        self._copy(3, self._idx(self.first, c), self._on(self.second)).wait_send()
        for k, chip in ((4, self.first), (5, self.second), (6, self.far)):
            self._copy(k, self._idx(chip, c), self.sib).wait_send()


def _prep_tables(rb_ref, w_ref, b_ref, bk_ref, bias_ref, wt_ref, wtt_ref, bcol_ref):
    valid = _window_valid()
    bk = bk_ref[...]
    acc = [jnp.full((CHUNK, 2 * CHUNK), NEG, F32) for _ in range(4)]
    for b in range(N_BUCKETS):
        hit = (bk == b) & valid
        for h in range(4):
            acc[h] = jnp.where(hit, rb_ref[b, h], acc[h])
    for h in range(4):
        bias_ref[h] = acc[h]
    r = lax.broadcasted_iota(jnp.int32, (CHUNK, CHUNK), 0)
    c = lax.broadcasted_iota(jnp.int32, (CHUNK, CHUNK), 1)
    for g in range(A_GROUPS):
        w = jnp.where(r >= c, w_ref[g], 0.0)
        wt_ref[g] = w.astype(MM)
        wtt_ref[g] = w.T.astype(MM)
        bcol_ref[g] = jnp.broadcast_to(b_ref[g:g + 1, :], (CHUNK, CHUNK)).T


def _wgather(a, b, c, rel_bias, w_sp, b_sp, buckets, mem2, gm):
    tmem = mem2.shape[0]

    def body(a_ref, b_ref, c_ref, rb_ref, w_ref, bsp_ref, bk_ref, m_ref, gm_ref,
             oa, ob, oc, bias_ref, wt_ref, wtt_ref, bcol_ref, mkv_ref, ssem, rsem):
        pos = _position()
        me = 4 * pos[0] + 2 * pos[1] + pos[2]
        gathers = []
        for k, (src, out) in enumerate(((c_ref, oc), (b_ref, ob), (a_ref, oa))):
            out[me] = src[...].astype(BF16)
            g = _Gather(pos, out, ssem.at[k], rsem.at[k])
            g.start()
            gathers.append(g)
        _prep_tables(rb_ref, w_ref, bsp_ref, bk_ref, bias_ref, wt_ref, wtt_ref, bcol_ref)
        for g in gathers:
            g.forward()
        gathers[0].finish()
        xf = m_ref[...]
        hm = (xf * _rms(xf) * gm_ref[...]).astype(MM)
        acc = jnp.zeros((tmem, 2 * MEM_LEN), F32)
        for d in range(N_DEV):
            acc = acc + _dot(hm[:, d * SHARD_O:(d + 1) * SHARD_O], oc[d])
        mkv_ref[...] = acc.astype(MM)
        for g in gathers[1:]:
            g.finish()

    vm = pl.BlockSpec(memory_space=pltpu.VMEM)
    grp = (A_GROUPS, CHUNK, CHUNK)
    return pl.pallas_call(
        body, name="wgather",
        out_shape=(jax.ShapeDtypeStruct((N_DEV,) + a.shape, BF16),
                   jax.ShapeDtypeStruct((N_DEV,) + b.shape, BF16),
                   jax.ShapeDtypeStruct((N_DEV,) + c.shape, BF16),
                   jax.ShapeDtypeStruct((4, CHUNK, 2 * CHUNK), F32),
                   jax.ShapeDtypeStruct(grp, MM), jax.ShapeDtypeStruct(grp, MM), jax.ShapeDtypeStruct(grp, F32),
                   jax.ShapeDtypeStruct((tmem, 2 * MEM_LEN), MM)),
        in_specs=[vm, vm, vm, pl.BlockSpec(memory_space=pltpu.SMEM), vm, vm, vm, vm, vm],
        out_specs=tuple([vm] * 8),
        scratch_shapes=[pltpu.SemaphoreType.DMA((3, 7)), pltpu.SemaphoreType.DMA((3, 7))],
        compiler_params=_params(),
    )(a, b, c, rel_bias, w_sp, b_sp, buckets, mem2, gm)


def _half_masks(rows):
    lane = lax.broadcasted_iota(jnp.int32, (rows, CHUNK), 1)
    return lane < 64


def _dup_heads(band):
    b32 = band.astype(F32)
    rolled = pltpu.roll(b32, 64, 1)
    lo = _half_masks(band.shape[0])
    return (jnp.where(lo, b32, rolled).astype(MM), jnp.where(lo, rolled, b32).astype(MM))


def _swa_probs(qk, bias_h, sink_h, first_add):
    s = qk * SCALE + bias_h + first_add
    m = jnp.maximum(jnp.max(s, axis=-1, keepdims=True), sink_h)
    p = jnp.exp(s - m)
    es = jnp.exp(sink_h - m)
    inv = 1.0 / (jnp.sum(p, axis=-1, keepdims=True) + es)
    return p * inv, es * inv


def _softmax(s):
    m = jnp.max(s, axis=-1, keepdims=True)
    p = jnp.exp(s - m)
    return p * (1.0 / jnp.sum(p, axis=-1, keepdims=True))


def _first_block_mask(n):
    col = lax.broadcasted_iota(jnp.int32, (CHUNK, 2 * CHUNK), 1)
    return jnp.where((col < CHUNK) & (n == 0), NEG, 0.0)


def _rms(xf):
    return lax.rsqrt(jnp.mean(xf * xf, axis=-1, keepdims=True) + EPS)


def _layer(x2, tgt2, mkv3, bias, sinks, vg, vb, wt, wtt, bcol, g1, g2, w_in_t, w_o, buckets, nb, s, tm):
    nt = s // tm
    bpt = tm // CHUNK
    bps = s // CHUNK
    t = nb * s
    last_step = nb * nt - 1

    def tile_at(step):
        return (step // nt) * nt + nt - 1 - step % nt

    def block_before(step):
        return (step // nt) * bps + jnp.maximum((nt - 1 - step % nt) * bpt - 1, 0)

    def body(x_ref, xp_ref, xn_ref, xpn_ref, t_ref, mkv_ref, bias_ref, sink_ref, vg_ref, vb_ref,
             wt_ref, wtt_ref, bcol_ref, g1_ref, g2_ref, wi_ref, wo_ref, bk_ref,
             gx_ref, dmkv_ref, dwi_hbm, dwo_hbm, dg1_ref, dg2_ref, loss_ref, dwsp_ref, dbs_ref,
             dvg_ref, dvb_ref, dsink_ref, drel_ref,
             acc_i, acc_o, uv_s, z_s, q_s, kv_s, h_s, hp_s, dp_s, dxo_s, dh_s, r_s,
             ycat, dyc, u_s, gu_s, gv_s, xh_s, vc_s, pb_s, ps_s, pc_s, kd_s, vd_s,
             dkv_acc, dbias_acc, dsv_acc, dsink_acc, sems):
        b, j = pl.program_id(0), pl.program_id(1)
        jt = nt - 1 - j
        step = b * nt + j
        g1v = g1_ref[...]
        NOW, NEXT, DONE = 0, 1, 2
        dw_cols = list(DW_PIECES)

        def weight_grad(n, slot):
            for c0, c1 in dw_cols[:n]:
                acc_i[c0:c1, :] += _dot_tn(dp_s[:, c0:c1], h_s[slot])
            del dw_cols[:n]

        def pre_norm(x_tile, x_before):
            xf = x_tile[...]
            r_s[NEXT] = _rms(xf)
            h_s[NEXT] = (xf * r_s[NEXT] * g1v).astype(MM)
            xp = x_before[...]
            hp_s[...] = (xp * _rms(xp) * g1v).astype(MM)

        def project_z():
            z_s[...] = _dot_nt(h_s[NEXT], wi_ref[Z_COL:IN_WIDTH, :])

        def project_uv():
            uv_s[...] = _dot_nt(h_s[NEXT], wi_ref[0:UV_W, :])

        @pl.when(step == 0)
        def _():
            for ref in (acc_i, acc_o, dg1_ref, dg2_ref, loss_ref, dwsp_ref, dvg_ref, dvb_ref,
                        dbias_acc, dsv_acc, dsink_acc):
                ref[...] = jnp.zeros_like(ref)
            dp_s[...] = jnp.zeros_like(dp_s)
            h_s[NOW] = jnp.zeros((tm, D_MODEL), MM)
            pre_norm(x_ref, xp_ref)
            project_z()
            project_uv()

        h_s[DONE] = h_s[NOW]
        r_s[NOW] = r_s[NEXT]
        h = h_s[NEXT]
        h_s[NOW] = h
        hp = hp_s[...]

        @pl.when(j == 0)
        def _():
            dmkv_ref[...] = jnp.zeros_like(dmkv_ref)
            dkv_acc[...] = jnp.zeros_like(dkv_acc)

        carry = dkv_acc[0:CHUNK, :]
        dkv_acc[...] = jnp.zeros_like(dkv_acc)
        dkv_acc[tm:tm + CHUNK, :] = carry

        lo = _half_masks(CHUNK)
        lob = _half_masks(2 * CHUNK)
        lot = _half_masks(tm)

        qkv = _dot_nt(h, wi_ref[SQ_COL:Z_COL, :])
        q_s[:, 0:256] = qkv[:, 0:256].astype(MM)
        q_s[:, 256:512] = qkv[:, 512:768].astype(MM)
        kv_s[CHUNK:CHUNK + tm, :] = qkv[:, 256:512].astype(MM)
        kv_s[0:CHUNK, :] = _dot_nt(hp, wi_ref[SK_COL:MQ_COL, :]).astype(MM)

        weight_grad(1, DONE)
        b_qk, b_pb = [], []
        for blk in range(bpt):
            r0 = blk * CHUNK
            rows = slice(r0, r0 + CHUNK)
            for g in range(A_GROUPS):
                cg = slice(g * CHUNK, (g + 1) * CHUNK)
                u, gu = _gelu_and_grad(uv_s[rows, cg])
                v, gv = _gelu_and_grad(uv_s[rows, A_WIDTH + g * CHUNK:A_WIDTH + (g + 1) * CHUNK])
                mu = jnp.mean(v, axis=-1, keepdims=True)
                xc = v - mu
                rstd = lax.rsqrt(jnp.mean(xc * xc, axis=-1, keepdims=True) + EPS)
                xhat = xc * rstd
                vc = (xhat * vg_ref[:, cg] + vb_ref[:, cg]).astype(MM)
                sv = _dot(wt_ref[g], vc) + bcol_ref[g]
                u_s[rows, cg] = u
                gu_s[rows, cg] = sv * gu
                gv_s[rows, cg] = rstd * gv
                xh_s[rows, cg] = xhat
                vc_s[rows, cg] = vc
                ycat[rows, cg] = u * sv
            weight_grad(1, DONE)
            kd = _dup_heads(kv_s[r0:r0 + 2 * CHUNK, 0:CHUNK])
            vd = _dup_heads(kv_s[r0:r0 + 2 * CHUNK, CHUNK:2 * CHUNK])
            for kvh in range(2):
                kd_s[blk * 2 + kvh] = kd[kvh]
                vd_s[blk * 2 + kvh] = vd[kvh]
                q128 = q_s[rows, kvh * CHUNK:(kvh + 1) * CHUNK].astype(F32)
                for gi in range(2):
                    qsel = jnp.where(lo if gi == 0 else ~lo, q128, 0.0).astype(MM)
                    b_qk.append(_dot_nt(qsel, kd[kvh]))
        qks, pcs = [], []
        for g in range(2):
            q128 = q_s[:, 256 + g * CHUNK:256 + (g + 1) * CHUNK].astype(F32)
            for hh in range(2):
                qsel = jnp.where(lot if hh == 0 else ~lot, q128, 0.0).astype(MM)
                qks.append(_dot_nt(qsel, mkv_ref[:, g * CHUNK:(g + 1) * CHUNK]))
        weight_grad(2, DONE)
        for blk in range(bpt):
            first_add = _first_block_mask(jt * bpt + blk)
            for hd in range(4):
                probs, ps = _swa_probs(b_qk[blk * 4 + hd], bias_ref[hd], sink_ref[hd], first_add)
                pb_s[blk * 4 + hd] = probs
                ps_s[blk * 4 + hd] = jnp.broadcast_to(ps, (CHUNK, CHUNK))
                b_pb.append(probs.astype(MM))
        weight_grad(len(dw_cols), DONE)
        for hd in range(4):
            probs = _softmax(qks[hd] * SCALE)
            pc_s[hd] = probs
            pcs.append(probs.astype(MM))
        for blk in range(bpt):
            rows = slice(blk * CHUNK, (blk + 1) * CHUNK)
            outs = [_dot(b_pb[blk * 4 + hd], vd_s[blk * 2 + hd // 2]) for hd in range(4)]
            for kvh in range(2):
                ycat[rows, YB_OFF + kvh * CHUNK:YB_OFF + (kvh + 1) * CHUNK] = jnp.where(
                    lo, outs[2 * kvh], outs[2 * kvh + 1])
        outs = [_dot(pcs[hd], mkv_ref[:, MEM_LEN + (hd // 2) * CHUNK:MEM_LEN + (hd // 2 + 1) * CHUNK])
                for hd in range(4)]
        for g in range(2):
            ycat[:, YC_OFF + g * CHUNK:YC_OFF + (g + 1) * CHUNK] = jnp.where(lot, outs[2 * g], outs[2 * g + 1])

        zt = z_s[...]
        sig = 1.0 / (1.0 + jnp.exp(-zt))
        silu = zt * sig
        yc = ycat[...]
        yb = (yc * silu).astype(MM)
        pre_norm(xn_ref, xpn_ref)
        o = _dot(yb, wo_ref[...])
        project_z()
        r2 = _rms(o)
        nrm = o * r2
        g2v = g2_ref[...]
        e = x_ref[...] + nrm * g2v - t_ref[...]
        l1 = jnp.sum(e * e, axis=-1, keepdims=True)
        loss_ref[...] += jnp.broadcast_to(jnp.sum(l1, axis=0, keepdims=True) * (0.5 / D_MODEL), loss_ref.shape)
        dxo = e * (1.0 / D_MODEL)
        dxo_s[...] = dxo
        dg2_ref[...] += jnp.sum(dxo * nrm, axis=0, keepdims=True)
        dn = dxo * g2v
        do = r2 * (dn - nrm * jnp.mean(dn * nrm, axis=-1, keepdims=True))
        dob = do.astype(MM)
        dy = _dot_nt(dob, wo_ref[...])
        dp_s[:, Z_COL:IN_WIDTH] = (dy * yc * (sig * (1.0 + zt * (1.0 - sig)))).astype(MM)
        dyc[...] = dy * silu
        acc_o[...] += _dot_tn(yb, dob)

        def in_proj_bwd(c0, c1):
            part = _dot(dp_s[:, c0:c1], wi_ref[c0:c1, :])
            if c0 == Z_COL:
                dh_s[...] = part
            else:
                dh_s[...] += part

        in_proj_bwd(Z_COL, IN_WIDTH)

        for blk in range(bpt):
            r0 = blk * CHUNK
            rows = slice(r0, r0 + CHUNK)
            for g in range(A_GROUPS):
                cg = slice(g * CHUNK, (g + 1) * CHUNK)
                cv = slice(A_WIDTH + g * CHUNK, A_WIDTH + (g + 1) * CHUNK)
                dya = dyc[rows, cg]
                dp_s[rows, cg] = (dya * gu_s[rows, cg]).astype(MM)
                dsv = dya * u_s[rows, cg]
                dsvb = dsv.astype(MM)
                dsv_acc[g] += dsv
                dwsp_ref[g] += _dot_nt(dsvb, vc_s[rows, cg])
                dvc = _dot(wtt_ref[g], dsvb)
                xhat = xh_s[rows, cg]
                dvg_ref[:, cg] += jnp.sum(dvc * xhat, axis=0, keepdims=True)
                dvb_ref[:, cg] += jnp.sum(dvc, axis=0, keepdims=True)
                dxh = dvc * vg_ref[:, cg]
                dv = (dxh - jnp.mean(dxh, axis=-1, keepdims=True)
                      - xhat * jnp.mean(dxh * xhat, axis=-1, keepdims=True))
                dp_s[rows, cv] = (dv * gv_s[rows, cg]).astype(MM)
        in_proj_bwd(0, UV_W)
        b_dosel, b_dp, b_dss = [], [], []
        for blk in range(bpt):
            rows = slice(blk * CHUNK, (blk + 1) * CHUNK)
            for hd in range(4):
                do128 = dyc[rows, YB_OFF + (hd // 2) * CHUNK:YB_OFF + (hd // 2 + 1) * CHUNK]
                b_dosel.append(jnp.where(lo if hd % 2 == 0 else ~lo, do128, 0.0).astype(MM))
                b_dp.append(_dot_nt(b_dosel[-1], vd_s[blk * 2 + hd // 2]))
        dosels, dps, dsss = [], [], []
        for hd in range(4):
            do128 = dyc[:, YC_OFF + (hd // 2) * CHUNK:YC_OFF + (hd // 2 + 1) * CHUNK]
            dosels.append(jnp.where(lot if hd % 2 == 0 else ~lot, do128, 0.0).astype(MM))
            dps.append(_dot_nt(dosels[hd], mkv_ref[:, MEM_LEN + (hd // 2) * CHUNK:MEM_LEN + (hd // 2 + 1) * CHUNK]))
        for blk in range(bpt):
            for hd in range(4):
                probs = pb_s[blk * 4 + hd]
                ps = ps_s[blk * 4 + hd][:, 0:1]
                dp = b_dp[blk * 4 + hd]
                delta = jnp.sum(probs * dp, axis=-1, keepdims=True)
                ds = probs * (dp - delta)
                dbias_acc[hd] += ds
                dsink_acc[hd:hd + 1, :] += jnp.broadcast_to(-jnp.sum(ps * delta, axis=0, keepdims=True), (1, CHUNK))
                b_dss.append((ds * SCALE).astype(MM))
        for hd in range(4):
            probs = pc_s[hd]
            ds = probs * (dps[hd] - jnp.sum(probs * dps[hd], axis=-1, keepdims=True))
            dsss.append((ds * SCALE).astype(MM))
        for blk in range(bpt):
            r0 = blk * CHUNK
            rows = slice(r0, r0 + CHUNK)
            dk_f, dv_f = [], []
            for kvh in range(2):
                kd = kd_s[blk * 2 + kvh]
                q128 = q_s[rows, kvh * CHUNK:(kvh + 1) * CHUNK].astype(F32)
                dq128 = jnp.zeros((CHUNK, CHUNK), F32)
                dkd = jnp.zeros((2 * CHUNK, CHUNK), F32)
                dvd = jnp.zeros((2 * CHUNK, CHUNK), F32)
                for gi in range(2):
                    hd = 2 * kvh + gi
                    half = lo if gi == 0 else ~lo
                    qsel = jnp.where(half, q128, 0.0).astype(MM)
                    dq128 = dq128 + jnp.where(half, _dot(b_dss[blk * 4 + hd], kd), 0.0)
                    dkd = dkd + _dot_tn(b_dss[blk * 4 + hd], qsel)
                    dvd = dvd + _dot_tn(pb_s[blk * 4 + hd].astype(MM), b_dosel[blk * 4 + hd])
                dp_s[rows, SQ_COL + kvh * CHUNK:SQ_COL + (kvh + 1) * CHUNK] = dq128.astype(MM)
                dk_f.append(dkd + pltpu.roll(dkd, 64, 1))
                dv_f.append(dvd + pltpu.roll(dvd, 64, 1))
            dkv_acc[r0:r0 + 2 * CHUNK, 0:CHUNK] += jnp.where(lob, dk_f[0], dk_f[1])
            dkv_acc[r0:r0 + 2 * CHUNK, CHUNK:2 * CHUNK] += jnp.where(lob, dv_f[0], dv_f[1])
        dp_s[:, SK_COL:MQ_COL] = dkv_acc[CHUNK:CHUNK + tm, :].astype(MM)
        for g in range(2):
            q128 = q_s[:, 256 + g * CHUNK:256 + (g + 1) * CHUNK].astype(F32)
            k128 = mkv_ref[:, g * CHUNK:(g + 1) * CHUNK]
            dq128 = jnp.zeros((tm, CHUNK), F32)
            dk128 = jnp.zeros((MEM_LEN, CHUNK), F32)
            dv128 = jnp.zeros((MEM_LEN, CHUNK), F32)
            for hh in range(2):
                hd = 2 * g + hh
                half = lot if hh == 0 else ~lot
                qsel = jnp.where(half, q128, 0.0).astype(MM)
                dq128 = dq128 + jnp.where(half, _dot(dsss[hd], k128), 0.0)
                dk128 = dk128 + _dot_tn(dsss[hd], qsel)
                dv128 = dv128 + _dot_tn(pc_s[hd].astype(MM), dosels[hd])
            dp_s[:, MQ_COL + g * CHUNK:MQ_COL + (g + 1) * CHUNK] = dq128.astype(MM)
            dmkv_ref[:, g * CHUNK:(g + 1) * CHUNK] += dk128
            dmkv_ref[:, MEM_LEN + g * CHUNK:MEM_LEN + (g + 1) * CHUNK] += dv128

        in_proj_bwd(SQ_COL, Z_COL)
        project_uv()
        dh = dh_s[...]
        r = r_s[NOW]
        nx = x_ref[...] * r
        dg1_ref[...] += jnp.sum(dh * nx, axis=0, keepdims=True)
        dnx = dh * g1v
        gx_ref[...] = dxo_s[...] + r * (dnx - nx * jnp.mean(dnx * nx, axis=-1, keepdims=True))

        @pl.when(step == last_step)
        def _():
            dw_cols.extend(DW_PIECES)
            weight_grad(len(dw_cols), NOW)
            out_i = pltpu.make_async_copy(acc_i, dwi_hbm, sems.at[0])
            out_o = pltpu.make_async_copy(acc_o, dwo_hbm, sems.at[1])
            out_i.start()
            out_o.start()
            r_ = lax.broadcasted_iota(jnp.int32, (CHUNK, CHUNK), 0)
            c_ = lax.broadcasted_iota(jnp.int32, (CHUNK, CHUNK), 1)
            for g in range(A_GROUPS):
                dwsp_ref[g] = jnp.where(r_ >= c_, dwsp_ref[g], 0.0)
                dbs_ref[g:g + 1, :] = jnp.sum(dsv_acc[g].T, axis=0, keepdims=True)
            rows8 = lax.broadcasted_iota(jnp.int32, (8, CHUNK), 0)
            cols8 = lax.broadcasted_iota(jnp.int32, (8, CHUNK), 1)
            sk = jnp.zeros((8, CHUNK), F32)
            for hd in range(4):
                sk = sk + jnp.where((rows8 == 0) & (cols8 == hd),
                                    jnp.broadcast_to(dsink_acc[hd:hd + 1, :], (8, CHUNK)), 0.0)
            dsink_ref[...] = sk
            bk = bk_ref[...]
            valid = _window_valid()
            rrow = lax.broadcasted_iota(jnp.int32, (N_BUCKETS, CHUNK), 0)
            rcol = lax.broadcasted_iota(jnp.int32, (N_BUCKETS, CHUNK), 1)
            acc = jnp.zeros((N_BUCKETS, CHUNK), F32)
            for bb in range(N_BUCKETS):
                hit = (bk == bb) & valid
                for hd in range(4):
                    part = jnp.sum(jnp.where(hit, dbias_acc[hd], 0.0), axis=-1, keepdims=True)
                    tot = jnp.sum(part, axis=0, keepdims=True)
                    acc = acc + jnp.where((rrow == bb) & (rcol == hd), jnp.broadcast_to(tot, (N_BUCKETS, CHUNK)), 0.0)
            drel_ref[...] = acc
            out_i.wait()
            out_o.wait()

    after = lambda b, j: jnp.minimum(b * nt + j + 1, last_step)
    tile = pl.BlockSpec((tm, D_MODEL), lambda b, j: (tile_at(b * nt + j), 0))
    tile_after = pl.BlockSpec((tm, D_MODEL), lambda b, j: (tile_at(after(b, j)), 0))
    halo = pl.BlockSpec((CHUNK, D_MODEL), lambda b, j: (block_before(b * nt + j), 0))
    halo_after = pl.BlockSpec((CHUNK, D_MODEL), lambda b, j: (block_before(after(b, j)), 0))
    per_batch = lambda r, w: pl.BlockSpec((None, r, w), lambda b, j: (b, 0, 0))
    anyspec = pl.BlockSpec(memory_space=pl.ANY)
    grp = (A_GROUPS, CHUNK, CHUNK)
    return pl.pallas_call(
        body, name="layer", grid=(nb, nt),
        out_shape=(jax.ShapeDtypeStruct((t, D_MODEL), F32),
                   jax.ShapeDtypeStruct((nb, MEM_LEN, 2 * MEM_LEN), F32),
                   jax.ShapeDtypeStruct((IN_WIDTH, D_MODEL), F32),
                   jax.ShapeDtypeStruct((D_MODEL, D_MODEL), F32),
                   jax.ShapeDtypeStruct((1, D_MODEL), F32),
                   jax.ShapeDtypeStruct((1, D_MODEL), F32),
                   jax.ShapeDtypeStruct((8, CHUNK), F32),
                   jax.ShapeDtypeStruct(grp, F32),
                   jax.ShapeDtypeStruct((A_GROUPS, CHUNK), F32),
                   jax.ShapeDtypeStruct((1, A_WIDTH), F32),
                   jax.ShapeDtypeStruct((1, A_WIDTH), F32),
                   jax.ShapeDtypeStruct((8, CHUNK), F32),
                   jax.ShapeDtypeStruct((N_BUCKETS, CHUNK), F32)),
        in_specs=[tile, halo, tile_after, halo_after, tile, per_batch(MEM_LEN, 2 * MEM_LEN),
                  _full((4, CHUNK, 2 * CHUNK)),
                  pl.BlockSpec(memory_space=pltpu.SMEM),
                  _full((1, A_WIDTH)), _full((1, A_WIDTH)),
                  _full(grp), _full(grp), _full(grp),
                  _full((1, D_MODEL)), _full((1, D_MODEL)),
                  _full((IN_WIDTH, D_MODEL), single=True), _full((D_MODEL, D_MODEL), single=True),
                  _full((CHUNK, 2 * CHUNK))],
        out_specs=(tile, per_batch(MEM_LEN, 2 * MEM_LEN), anyspec, anyspec,
                   _full((1, D_MODEL)), _full((1, D_MODEL)), _full((8, CHUNK)),
                   _full(grp), _full((A_GROUPS, CHUNK)), _full((1, A_WIDTH)), _full((1, A_WIDTH)),
                   _full((8, CHUNK)), _full((N_BUCKETS, CHUNK))),
        scratch_shapes=[pltpu.VMEM((IN_WIDTH, D_MODEL), F32), pltpu.VMEM((D_MODEL, D_MODEL), F32),
                        pltpu.VMEM((tm, UV_W), F32), pltpu.VMEM((tm, Z_W), F32),
                        pltpu.VMEM((tm, 512), MM), pltpu.VMEM((tm + CHUNK, 2 * CHUNK), MM),
                        pltpu.VMEM((3, tm, D_MODEL), MM), pltpu.VMEM((CHUNK, D_MODEL), MM),
                        pltpu.VMEM((tm, IN_WIDTH), MM),
                        pltpu.VMEM((tm, D_MODEL), F32),
                        pltpu.VMEM((tm, D_MODEL), F32), pltpu.VMEM((2, tm, 1), F32),
                        pltpu.VMEM((tm, D_MODEL), F32), pltpu.VMEM((tm, D_MODEL), F32)]
                       + [pltpu.VMEM((tm, A_WIDTH), F32) for _ in range(4)]
                       + [pltpu.VMEM((tm, A_WIDTH), MM),
                          pltpu.VMEM((bpt * 4, CHUNK, 2 * CHUNK), F32),
                          pltpu.VMEM((bpt * 4, CHUNK, CHUNK), F32),
                          pltpu.VMEM((4, tm, MEM_LEN), F32),
                          pltpu.VMEM((bpt * 2, 2 * CHUNK, CHUNK), MM),
                          pltpu.VMEM((bpt * 2, 2 * CHUNK, CHUNK), MM),
                          pltpu.VMEM((tm + CHUNK, 2 * CHUNK), F32),
                          pltpu.VMEM((4, CHUNK, 2 * CHUNK), F32),
                          pltpu.VMEM(grp, F32),
                          pltpu.VMEM((8, CHUNK), F32),
                          pltpu.SemaphoreType.DMA((2,))],
        compiler_params=_params(dimension_semantics=("arbitrary", "arbitrary")),
    )(x2, x2, x2, x2, tgt2, mkv3, bias, sinks, vg, vb, wt, wtt, bcol, g1, g2, w_in_t, w_o, buckets)


class _ShardReduce:
    def __init__(self, pos, g, bufs, sems):
        self.x, self.y, self.c = pos
        self.g = g
        self.own, self.rcv, self.sbuf, self.rbuf, self.cbuf = bufs
        self.ld, self.sa, self.ra, self.sb, self.rb = sems
        self.nrow = g.shape[1]
        self.here = (self.x, self.y, self.c)
        self.sib = (self.x, self.y, 1 - self.c)
        self.first, self.second, self.far = _route(*pos)

    def _load(self, q):
        return pltpu.make_async_copy(self.g.at[2 * q + self.c], self.own.at[q], self.ld.at[q])

    def _to_sib(self, q, to):
        return _remote(self.g.at[2 * q + 1 - self.c], self.rcv.at[q], self.sa.at[q], self.ra.at[q], to)

    def _send(self, k, to):
        dst = self.cbuf.at[0] if k == 1 else self.rbuf.at[0 if k == 0 else 1]
        return _remote(self.sbuf.at[k], dst, self.sb.at[k], self.rb.at[k], to)

    def _stage(self, k, which, extra=None):
        def cast(r):
            v = self.rcv[which, r, :]
            if extra is not None:
                v = v + extra[0, r, :].astype(F32)
            self.sbuf[k, r, :] = v.astype(BF16)

        _rows_loop(self.nrow, cast)

    @staticmethod
    def _q(chip):
        return 2 * chip[0] + chip[1]

    def start(self):
        for q in range(4):
            self._load(q).start()
            self._to_sib(q, self.sib).start()

    def mid(self):
        for q in range(4):
            self._load(q).wait()
            self._to_sib(q, self.here).wait_recv()

        def add(r):
            for q in range(4):
                self.rcv[q, r, :] = self.rcv[q, r, :] + self.own[q, r, :]

        _rows_loop(self.nrow, add)
        to_first = (self.first[0], self.first[1], self.c)
        self._stage(0, self._q(self.first))
        self._send(0, to_first).start()
        self._stage(1, self._q(self.far))
        self._send(1, to_first).start()

    def pass_on(self):
        self._send(1, self.here).wait_recv()
        self._stage(2, self._q(self.second), extra=self.cbuf)
        self._send(2, (self.second[0], self.second[1], self.c)).start()

    def finish(self, out):
        self._send(0, self.here).wait_recv()
        self._send(2, self.here).wait_recv()
        which = 2 * self.x + self.y

        def tot(r):
            out[r, :] = (self.rcv[which, r, :] + self.rbuf[0, r, :].astype(F32)) + self.rbuf[1, r, :].astype(F32)

        _rows_loop(self.nrow, tot)
        for q in range(4):
            self._to_sib(q, self.sib).wait_send()
        to_first = (self.first[0], self.first[1], self.c)
        self._send(0, to_first).wait_send()
        self._send(1, to_first).wait_send()
        self._send(2, (self.second[0], self.second[1], self.c)).wait_send()


def _reduce_scratch(shape):
    return [pltpu.VMEM((4,) + shape, F32), pltpu.VMEM((4,) + shape, F32),
            pltpu.VMEM((3,) + shape, BF16), pltpu.VMEM((2,) + shape, BF16), pltpu.VMEM((1,) + shape, BF16),
            pltpu.SemaphoreType.DMA((4,)), pltpu.SemaphoreType.DMA((4,)), pltpu.SemaphoreType.DMA((4,)),
            pltpu.SemaphoreType.DMA((3,)), pltpu.SemaphoreType.DMA((3,))]


_N_RED = 10

_S_LAYOUT = (((1, D_MODEL), 0), ((1, D_MODEL), 8), ((1, D_MODEL), 16),
             ((1, A_WIDTH), 24), ((1, A_WIDTH), 28), ((A_GROUPS, CHUNK), 32),
             ((1, 4), 36), ((N_BUCKETS, 4), 40),
             ((A_GROUPS * CHUNK, CHUNK), 72))
_LOSS_ROW = 37
_W_SP_ROW = _S_LAYOUT[-1][1]
_S_ROWS = _W_SP_ROW + A_GROUPS * CHUNK
_N_SMALL = len(_S_LAYOUT)


def _pack_rows(dst, refs):
    for (shp, r0), ref in zip(_S_LAYOUT, refs):
        if shp[0] == 1 and shp[1] >= CHUNK:
            for i in range(shp[1] // CHUNK):
                dst[r0 + i:r0 + i + 1, :] = ref[:, i * CHUNK:(i + 1) * CHUNK]
        elif ref.shape[-1] == CHUNK:
            dst[r0:r0 + shp[0], :] = ref[0:shp[0], :]
        else:
            dst[r0:r0 + shp[0], 0:shp[1]] = ref[...]


def _unpack_rows(src, refs):
    for (shp, r0), ref in zip(_S_LAYOUT, refs):
        if shp[0] == 1 and shp[1] >= CHUNK:
            for i in range(shp[1] // CHUNK):
                ref[:, i * CHUNK:(i + 1) * CHUNK] = src[r0 + i:r0 + i + 1, :]
        elif shp[1] == CHUNK:
            ref[...] = src[r0:r0 + shp[0], :]
        else:
            if tuple(ref.shape) == (shp[1], shp[0]):
                ref[...] = src[r0:r0 + CHUNK, :].T[0:shp[1], 0:shp[0]]
            else:
                ref[...] = src[r0:r0 + shp[0], 0:shp[1]]


_MEM_G = 2


def _greduce(ga, gb, dmkv, mem2, gm, w_mkv, small_g, loss_p):
    shp_c = (SHARD_O, 2 * MEM_LEN)
    shapes = (shp_c, gb.shape[1:], ga.shape[1:])
    rs = _S_ROWS

    def body(*refs):
        it = iter(refs)
        take = lambda n: [next(it) for _ in range(n)]
        gb_ref, ga_ref, d_ref, m_ref, gm_ref, wm_ref = take(6)
        sg_refs = take(_N_SMALL - 1)
        loss_ref, = take(1)
        oc, ob, oa, ogs = take(4)
        red = take(3 * _N_RED)
        gs_ref, rs_a, rs_b, rs_w, gc_ref, dgm_ref = take(6)
        ssem_a, rsem_a, ssem_b, rsem_b = take(4)

        pos = _position()
        x, y, cc = pos
        myq = 2 * x + y
        here, sib = (x, y, cc), (x, y, 1 - cc)
        chips = _other_chips(x, y)
        reducers = [_ShardReduce(pos, g, red[k * _N_RED:k * _N_RED + 5], red[k * _N_RED + 5:(k + 1) * _N_RED])
                    for k, g in enumerate((gc_ref, gb_ref, ga_ref))]
        for rd in reducers[1:]:
            rd.start()

        xf = m_ref[...]
        nm = xf * _rms(xf)
        hm = (nm * gm_ref[...]).astype(MM)
        d = d_ref[...].astype(MM)
        for o in range(N_DEV):
            gc_ref[o] = _dot_tn(hm[:, o * SHARD_O:(o + 1) * SHARD_O], d)
        dgm_ref[...] = jnp.sum(_dot_nt(d, wm_ref[...]) * nm, axis=0, keepdims=True)
        reducers[0].start()

        gs_ref[...] = jnp.zeros_like(gs_ref)
        _pack_rows(gs_ref, sg_refs[:_MEM_G] + [dgm_ref] + sg_refs[_MEM_G:])
        gs_ref[_LOSS_ROW:_LOSS_ROW + 1, :] = loss_ref[0:1, :]
        small_a = _remote(gs_ref, rs_a, ssem_a, rsem_a, sib)
        small_a.start()

        _remote(gs_ref, rs_a, ssem_a, rsem_a, here).wait_recv()
        rs_b[myq] = gs_ref[0:_W_SP_ROW, :] + rs_a[0:_W_SP_ROW, :]
        rs_w[myq] = (gs_ref[_W_SP_ROW:rs, :] + rs_a[_W_SP_ROW:rs, :]).astype(BF16)
        small_b = []
        for j, chip in enumerate(chips):
            to = (chip[0], chip[1], cc)
            small_b.append(_remote(rs_b.at[myq], rs_b.at[myq], ssem_b.at[0, j], rsem_b.at[0, j], to))
            small_b.append(_remote(rs_w.at[myq], rs_w.at[myq], ssem_b.at[1, j], rsem_b.at[1, j], to))
        for cp in small_b:
            cp.start()
        late_last = reducers[1:] + reducers[:1]
        for rd in late_last:
            rd.mid()
        for rd in late_last:
            rd.pass_on()

        for j in range(3):
            _remote(rs_b.at[myq], rs_b.at[myq], ssem_b.at[0, j], rsem_b.at[0, j], here).wait_recv()
            _remote(rs_w.at[myq], rs_w.at[myq], ssem_b.at[1, j], rsem_b.at[1, j], here).wait_recv()
        ogs[0:_W_SP_ROW, :] = ((rs_b[0] + rs_b[1]) + rs_b[2]) + rs_b[3]

        def tot_w(r):
            w = [rs_w[q, r, :].astype(F32) for q in range(4)]
            ogs[pl.ds(pl.multiple_of(_W_SP_ROW + r.start, 8), _ROWS), :] = ((w[0] + w[1]) + w[2]) + w[3]

        _rows_loop(rs - _W_SP_ROW, tot_w)
        for rd, out in zip(late_last, (ob, oa, oc)):
            rd.finish(out)
        small_a.wait_send()
        for cp in small_b:
            cp.wait_send()

    vm = pl.BlockSpec(memory_space=pltpu.VMEM)
    anyspec = pl.BlockSpec(memory_space=pl.ANY)
    scratch = []
    for shp in shapes:
        scratch += _reduce_scratch(shp)
    scratch += [pltpu.VMEM((rs, CHUNK), F32), pltpu.VMEM((rs, CHUNK), F32),
                pltpu.VMEM((4, _W_SP_ROW, CHUNK), F32), pltpu.VMEM((4, rs - _W_SP_ROW, CHUNK), BF16),
                pltpu.VMEM((N_DEV,) + shp_c, F32), pltpu.VMEM((1, D_MODEL), F32),
                pltpu.SemaphoreType.DMA, pltpu.SemaphoreType.DMA,
                pltpu.SemaphoreType.DMA((2, 3)), pltpu.SemaphoreType.DMA((2, 3))]
    tc, tb, ta, ts = pl.pallas_call(
        body, name="greduce",
        out_shape=tuple([jax.ShapeDtypeStruct(shp, F32) for shp in shapes] + [jax.ShapeDtypeStruct((rs, CHUNK), F32)]),
        in_specs=[anyspec] * 2 + [vm] * (4 + _N_SMALL),
        out_specs=(vm, vm, vm, vm),
        scratch_shapes=scratch,
        compiler_params=_params(),
    )(gb, ga, dmkv, mem2, gm, w_mkv, *small_g, loss_p)
    return ta, tb, tc, ts


def _adamw(w, g, m, v):
    m = ADAM_B1 * m + (1.0 - ADAM_B1) * g
    v = ADAM_B2 * v + (1.0 - ADAM_B2) * (g * g)
    m_hat = m / (1.0 - ADAM_B1 ** ADAM_STEP)
    v_hat = v / (1.0 - ADAM_B2 ** ADAM_STEP)
    delta = -ADAM_LR * (m_hat / (jnp.sqrt(v_hat) + ADAM_EPS) + ADAM_WD * w)
    return delta, m, v


def _update(ta, tb, tc, ts, big_wmv, small_wmv):
    shapes = (ta.shape, tb.shape, tc.shape)
    rs = _S_ROWS
    small_shapes = [tuple(a.shape) for a in small_wmv[0]]

    def body(*refs):
        it = iter(refs)
        take = lambda n: [next(it) for _ in range(n)]
        ga_ref, gb_ref, gc_ref, gs_ref = take(4)
        wa, ma, va, wb, mb, vb_, wc, mc, vc = take(9)
        sw_refs, sm_refs, sv_refs = take(_N_SMALL), take(_N_SMALL), take(_N_SMALL)
        oga, oda, oma, ova, ogb, odb, omb, ovb, ogc, odc, omc, ovc = take(12)
        so_refs = [take(_N_SMALL) for _ in range(4)]
        loss_out, = take(1)
        ws, ms, vs, ods, oms, ovs = take(6)

        for buf in (ws, ms, vs):
            buf[...] = jnp.zeros_like(buf)
        _pack_rows(ws, sw_refs)
        _pack_rows(ms, sm_refs)
        _pack_rows(vs, sv_refs)

        big = ((ga_ref, wa, ma, va, oga, oda, oma, ova), (gb_ref, wb, mb, vb_, ogb, odb, omb, ovb),
               (gc_ref, wc, mc, vc, ogc, odc, omc, ovc))
        for arr in range(3):
            g_r, w_r, m_r, v_r, og, od, om, ov = big[arr]

            def upd(r, g_r=g_r, w_r=w_r, m_r=m_r, v_r=v_r, og=og, od=od, om=om, ov=ov):
                g = g_r[r, :]
                d, m, v = _adamw(w_r[r, :], g, m_r[r, :], v_r[r, :])
                og[r, :] = g
                od[r, :] = d
                om[r, :] = m
                ov[r, :] = v

            _rows_loop(shapes[arr][0], upd)

        def upd_s(i, _):
            r = pl.ds(pl.multiple_of(i * 8, 8), 8)
            d, m, v = _adamw(ws[r, :], gs_ref[r, :], ms[r, :], vs[r, :])
            ods[r, :] = d
            oms[r, :] = m
            ovs[r, :] = v
            return 0

        lax.fori_loop(0, rs // 8, upd_s, 0)
        for k, buf in enumerate((gs_ref, ods, oms, ovs)):
            _unpack_rows(buf, so_refs[k])
        loss_out[...] = gs_ref[_LOSS_ROW:_LOSS_ROW + 1, 0:1]

    vm = pl.BlockSpec(memory_space=pltpu.VMEM)
    big_out = []
    for shp in shapes:
        big_out += [jax.ShapeDtypeStruct(shp, F32)] * 4
    small_out = [jax.ShapeDtypeStruct(shp[::-1] if shp == (N_BUCKETS, 4) else shp, F32) for shp in small_shapes] * 4
    out_shape = tuple(big_out + small_out + [jax.ShapeDtypeStruct((1, 1), F32)])
    n_in = 4 + 9 + 3 * _N_SMALL
    return pl.pallas_call(
        body, name="update",
        out_shape=out_shape,
        in_specs=[vm] * n_in,
        out_specs=tuple([vm] * len(out_shape)),
        scratch_shapes=[pltpu.VMEM((rs, CHUNK), F32) for _ in range(6)],
        compiler_params=_params(),
    )(ta, tb, tc, ts, *big_wmv, *small_wmv[0], *small_wmv[1], *small_wmv[2])


def kernel(x, mem, pre_norm_g, post_norm_g, mem_norm_g, w_in, w_mem_kv, v_norm_g, v_norm_b, w_spatial, b_spatial, attn_sinks, rel_bias, w_out, loss_target, m_pre_norm_g, m_post_norm_g, m_mem_norm_g, m_w_in, m_w_mem_kv, m_v_norm_g, m_v_norm_b, m_w_spatial, m_b_spatial, m_attn_sinks, m_rel_bias, m_w_out, v_pre_norm_g, v_post_norm_g, v_mem_norm_g, v_w_in, v_w_mem_kv, v_v_norm_g, v_v_norm_b, v_w_spatial, v_b_spatial, v_attn_sinks, v_rel_bias, v_w_out):
    sh_a = (w_in[0].T, m_w_in[0].T, v_w_in[0].T)
    sh_b = (w_out[0], m_w_out[0], v_w_out[0])
    sh_c = (w_mem_kv[0], m_w_mem_kv[0], v_w_mem_kv[0])
    nb, s, _ = x.shape
    t = nb * s
    x2 = x.reshape(t, D_MODEL)
    tgt2 = loss_target.reshape(t, D_MODEL)
    mem2 = mem.reshape(nb * MEM_LEN, D_MODEL)
    buckets = jnp.asarray(_t5_buckets())

    wa, wb, wc, bias, wt, wtt, bcol, mkv = _wgather(sh_a[0], sh_b[0], sh_c[0], rel_bias, w_spatial[0], b_spatial[0],
                                                    buckets, mem2, mem_norm_g)
    w_mkv = wc.reshape(D_MODEL, 2 * MEM_LEN)
    gx, dmkv, dwi, dwo, dg1, dg2, loss_p, dwsp, dbs, dvg, dvb, dsink, drel = _layer(
        x2, tgt2, mkv.reshape(nb, MEM_LEN, 2 * MEM_LEN), bias, attn_sinks.reshape(4), v_norm_g, v_norm_b, wt, wtt, bcol,
        pre_norm_g, post_norm_g, wa.reshape(IN_WIDTH, D_MODEL), wb.reshape(D_MODEL, D_MODEL), buckets,
        nb, s, min(256, s))
    gx = gx.reshape(nb, s, D_MODEL)
    small_grads = [dg1, dg2, dvg, dvb, dbs, dsink, drel, dwsp.reshape(A_GROUPS * CHUNK, CHUNK)]

    small_names = ["pre_norm_g", "post_norm_g", "mem_norm_g", "v_norm_g", "v_norm_b", "b_spatial", "attn_sinks",
                   "rel_bias", "w_spatial"]
    given = dict(pre_norm_g=(pre_norm_g, m_pre_norm_g, v_pre_norm_g), post_norm_g=(post_norm_g, m_post_norm_g, v_post_norm_g),
                 mem_norm_g=(mem_norm_g, m_mem_norm_g, v_mem_norm_g), v_norm_g=(v_norm_g, m_v_norm_g, v_v_norm_g),
                 v_norm_b=(v_norm_b, m_v_norm_b, v_v_norm_b), b_spatial=(b_spatial, m_b_spatial, v_b_spatial),
                 attn_sinks=(attn_sinks, m_attn_sinks, v_attn_sinks), rel_bias=(rel_bias, m_rel_bias, v_rel_bias),
                 w_spatial=(w_spatial, m_w_spatial, v_w_spatial))
    small_wmv = [[given[n][k].reshape(shp) for n, (shp, _) in zip(small_names, _S_LAYOUT)] for k in range(3)]

    ta, tb, tc, ts = _greduce(dwi.reshape(N_DEV, SHARD_IN, D_MODEL), dwo.reshape(N_DEV, SHARD_O, D_MODEL),
                              dmkv.reshape(nb * MEM_LEN, 2 * MEM_LEN), mem2, mem_norm_g, w_mkv, small_grads, loss_p)
    outs = _update(ta, tb, tc, ts, (*sh_a, *sh_b, *sh_c), small_wmv)
    ra, rb, rc = outs[0:4], outs[4:8], outs[8:12]
    loss = outs[12 + 4 * _N_SMALL].reshape(())

    res = {}
    for k, kind in enumerate(("grad", "delta", "new_m", "new_v")):
        res[kind, "w_in"] = ra[k].T[None]
        res[kind, "w_out"] = rb[k][None]
        res[kind, "w_mem_kv"] = rc[k][None]
        for i, n in enumerate(small_names):
            o = outs[12 + k * _N_SMALL + i]
            res[kind, n] = o.T if n == "rel_bias" else o.reshape(given[n][0].shape)
    order = ["pre_norm_g", "post_norm_g", "mem_norm_g", "w_in", "w_mem_kv", "v_norm_g", "v_norm_b", "w_spatial",
             "b_spatial", "attn_sinks", "rel_bias", "w_out"]
    flat = [res[kind, n] for kind in ("grad", "delta", "new_m", "new_v") for n in order]
    return (loss, gx, *flat)
```

```python
import numpy as np
import jax
import jax.numpy as jnp
from jax import lax
from jax.experimental import pallas as pl
from jax.experimental.pallas import tpu as pltpu

F32 = jnp.float32
BF16 = jnp.bfloat16
MM = jnp.bfloat16

D_MODEL = 1024
CHUNK = 128
A_GROUPS = 4
A_WIDTH = 512
UV_W = 1024
QKV_W = 768
Z_W = 1024
IN_WIDTH = UV_W + QKV_W + Z_W
MEM_LEN = 256
N_BUCKETS = 32
MAX_DISTANCE = 128
EPS = 1e-6
NEG = -1e30
SCALE = 0.125
N_DEV = 8
SHARD_IN = IN_WIDTH // N_DEV
SHARD_O = D_MODEL // N_DEV

SQ_COL, SK_COL, SV_COL, MQ_COL, Z_COL = UV_W, UV_W + 256, UV_W + 384, UV_W + 512, UV_W + QKV_W
DW_PIECES = ((0, SQ_COL), (SQ_COL, Z_COL), (Z_COL, IN_WIDTH))
YB_OFF, YC_OFF = 512, 768

ADAM_LR = 0.001
ADAM_B1 = 0.9
ADAM_B2 = 0.999
ADAM_EPS = 1e-08
ADAM_WD = 0.01
ADAM_STEP = 10

VMEM_LIMIT = 60 * 1024 * 1024

_GELU_C = 0.7978845608028654
_GELU_A = 0.044715

MESH = pl.DeviceIdType.MESH
_ROWS = 32


def _dot(a, b):
    return lax.dot_general(a, b, (((1,), (0,)), ((), ())), preferred_element_type=F32)


def _dot_nt(a, b):
    return lax.dot_general(a, b, (((1,), (1,)), ((), ())), preferred_element_type=F32)


def _dot_tn(a, b):
    return lax.dot_general(a, b, (((0,), (0,)), ((), ())), preferred_element_type=F32)


def _gelu_and_grad(x):
    x2 = x * x
    t = jnp.tanh(_GELU_C * (x + _GELU_A * x * x2))
    g = 0.5 * x * (1.0 + t)
    dg = 0.5 * (1.0 + t) + 0.5 * x * (1.0 - t * t) * (_GELU_C * (1.0 + 3.0 * _GELU_A * x2))
    return g, dg


def _t5_buckets():
    qi = np.arange(CHUNK)[:, None]
    kj = np.arange(2 * CHUNK)[None, :]
    n = np.maximum(qi + CHUNK - kj, 0)
    max_exact = N_BUCKETS // 2
    large = max_exact + (np.log(np.maximum(n, 1) / max_exact) / np.log(MAX_DISTANCE / max_exact)
                         * (N_BUCKETS - max_exact)).astype(np.int32)
    large = np.minimum(large, N_BUCKETS - 1)
    return np.where(n < max_exact, n, large).astype(np.int32)


def _params(**kw):
    return pltpu.CompilerParams(vmem_limit_bytes=VMEM_LIMIT, **kw)


def _full(shape, single=False):
    nd = len(shape)
    if single:
        return pl.BlockSpec(shape, lambda *_: (0,) * nd, pipeline_mode=pl.Buffered(1))
    return pl.BlockSpec(shape, lambda *_: (0,) * nd)


def _window_valid():
    qi = lax.broadcasted_iota(jnp.int32, (CHUNK, 2 * CHUNK), 0)
    kj = lax.broadcasted_iota(jnp.int32, (CHUNK, 2 * CHUNK), 1)
    dist = qi + CHUNK - kj
    return (dist >= 0) & (dist < CHUNK)


def _position():
    return lax.axis_index("x"), lax.axis_index("y"), lax.axis_index("c")


def _other_chips(x, y):
    return [(1 - x, y), (x, 1 - y), (1 - x, 1 - y)]


def _route(x, y, c):
    first = (x * c + (1 - x) * (1 - c), y * (1 - c) + (1 - y) * c)
    second = (x * (1 - c) + (1 - x) * c, y * c + (1 - y) * (1 - c))
    return first, second, (1 - x, 1 - y)


def _remote(src, dst, ssem, rsem, to):
    return pltpu.make_async_remote_copy(src_ref=src, dst_ref=dst, send_sem=ssem, recv_sem=rsem,
                                        device_id=to, device_id_type=MESH)


def _rows_loop(nrow, fn):
    def step(i, _):
        fn(pl.ds(pl.multiple_of(i * _ROWS, _ROWS), _ROWS))
        return 0

    lax.fori_loop(0, nrow // _ROWS, step, 0)


class _Gather:
    def __init__(self, pos, out, ssem, rsem):
        self.x, self.y, self.c = pos
        self.out, self.ssem, self.rsem = out, ssem, rsem
        self.me = 4 * self.x + 2 * self.y + self.c
        self.here = (self.x, self.y, self.c)
        self.sib = (self.x, self.y, 1 - self.c)
        self.first, self.second, self.far = _route(*pos)

    def _copy(self, k, blk, to):
        r = self.out.at[blk]
        return _remote(r, r, self.ssem.at[k], self.rsem.at[k], to)

    def _idx(self, chip, core):
        return 4 * chip[0] + 2 * chip[1] + core

    def _on(self, chip):
        return (chip[0], chip[1], self.c)

    def start(self):
        self._copy(0, self.me, self.sib).start()
        self._copy(1, self.me, self._on(self.first)).start()
        self._copy(2, self.me, self._on(self.second)).start()

    def forward(self):
        c = self.c
        self._copy(1, self._idx(self.first, c), self.here).wait_recv()
        self._copy(3, self._idx(self.first, c), self._on(self.second)).start()
        self._copy(4, self._idx(self.first, c), self.sib).start()
        self._copy(2, self._idx(self.second, c), self.here).wait_recv()
        self._copy(5, self._idx(self.second, c), self.sib).start()
        self._copy(3, self._idx(self.far, c), self.here).wait_recv()
        self._copy(6, self._idx(self.far, c), self.sib).start()

    def finish(self):
        c = self.c
        self._copy(0, self._idx((self.x, self.y), 1 - c), self.here).wait_recv()
        for k, chip in ((4, self.second), (5, self.first), (6, self.far)):
            self._copy(k, self._idx(chip, 1 - c), self.here).wait_recv()
        self._copy(0, self.me, self.sib).wait_send()
        self._copy(1, self.me, self._on(self.first)).wait_send()
        self._copy(2, self.me, self._on(self.second)).wait_send()
        self._copy(3, self._idx(self.first, c), self._on(self.second)).wait_send()
        for k, chip in ((4, self.first), (5, self.second), (6, self.far)):
            self._copy(k, self._idx(chip, c), self.sib).wait_send()


def _prep_tables(rb_ref, w_ref, b_ref, bk_ref, bias_ref, wt_ref, wtt_ref, bcol_ref):
    valid = _window_valid()
    bk = bk_ref[...]
    acc = [jnp.full((CHUNK, 2 * CHUNK), NEG, F32) for _ in range(4)]
    for b in range(N_BUCKETS):
        hit = (bk == b) & valid
        for h in range(4):
            acc[h] = jnp.where(hit, rb_ref[b, h], acc[h])
    for h in range(4):
        bias_ref[h] = acc[h]
    r = lax.broadcasted_iota(jnp.int32, (CHUNK, CHUNK), 0)
    c = lax.broadcasted_iota(jnp.int32, (CHUNK, CHUNK), 1)
    for g in range(A_GROUPS):
        w = jnp.where(r >= c, w_ref[g], 0.0)
        wt_ref[g] = w.astype(MM)
        wtt_ref[g] = w.T.astype(MM)
        bcol_ref[g] = jnp.broadcast_to(b_ref[g:g + 1, :], (CHUNK, CHUNK)).T


def _wgather(a, b, c, rel_bias, w_sp, b_sp, buckets, mem2, gm):
    tmem = mem2.shape[0]

    def body(a_ref, b_ref, c_ref, rb_ref, w_ref, bsp_ref, bk_ref, m_ref, gm_ref,
             oa, ob, oc, bias_ref, wt_ref, wtt_ref, bcol_ref, mkv_ref, ssem, rsem):
        pos = _position()
        me = 4 * pos[0] + 2 * pos[1] + pos[2]
        gathers = []
        for k, (src, out) in enumerate(((c_ref, oc), (b_ref, ob), (a_ref, oa))):
            out[me] = src[...].astype(BF16)
            g = _Gather(pos, out, ssem.at[k], rsem.at[k])
            g.start()
            gathers.append(g)
        _prep_tables(rb_ref, w_ref, bsp_ref, bk_ref, bias_ref, wt_ref, wtt_ref, bcol_ref)
        for g in gathers:
            g.forward()
        gathers[0].finish()
        xf = m_ref[...]
        hm = (xf * _rms(xf) * gm_ref[...]).astype(MM)
        acc = jnp.zeros((tmem, 2 * MEM_LEN), F32)
        for d in range(N_DEV):
            acc = acc + _dot(hm[:, d * SHARD_O:(d + 1) * SHARD_O], oc[d])
        mkv_ref[...] = acc.astype(MM)
        for g in gathers[1:]:
            g.finish()

    vm = pl.BlockSpec(memory_space=pltpu.VMEM)
    grp = (A_GROUPS, CHUNK, CHUNK)
    return pl.pallas_call(
        body, name="wgather",
        out_shape=(jax.ShapeDtypeStruct((N_DEV,) + a.shape, BF16),
                   jax.ShapeDtypeStruct((N_DEV,) + b.shape, BF16),
                   jax.ShapeDtypeStruct((N_DEV,) + c.shape, BF16),
                   jax.ShapeDtypeStruct((4, CHUNK, 2 * CHUNK), F32),
                   jax.ShapeDtypeStruct(grp, MM), jax.ShapeDtypeStruct(grp, MM), jax.ShapeDtypeStruct(grp, F32),
                   jax.ShapeDtypeStruct((tmem, 2 * MEM_LEN), MM)),
        in_specs=[vm, vm, vm, pl.BlockSpec(memory_space=pltpu.SMEM), vm, vm, vm, vm, vm],
        out_specs=tuple([vm] * 8),
        scratch_shapes=[pltpu.SemaphoreType.DMA((3, 7)), pltpu.SemaphoreType.DMA((3, 7))],
        compiler_params=_params(),
    )(a, b, c, rel_bias, w_sp, b_sp, buckets, mem2, gm)


def _half_masks(rows):
    lane = lax.broadcasted_iota(jnp.int32, (rows, CHUNK), 1)
    return lane < 64


def _dup_heads(band):
    b32 = band.astype(F32)
    rolled = pltpu.roll(b32, 64, 1)
    lo = _half_masks(band.shape[0])
    return (jnp.where(lo, b32, rolled).astype(MM), jnp.where(lo, rolled, b32).astype(MM))


def _swa_probs(qk, bias_h, sink_h, first_add):
    s = qk * SCALE + bias_h + first_add
    m = jnp.maximum(jnp.max(s, axis=-1, keepdims=True), sink_h)
    p = jnp.exp(s - m)
    es = jnp.exp(sink_h - m)
    inv = 1.0 / (jnp.sum(p, axis=-1, keepdims=True) + es)
    return p * inv, es * inv


def _softmax(s):
    m = jnp.max(s, axis=-1, keepdims=True)
    p = jnp.exp(s - m)
    return p * (1.0 / jnp.sum(p, axis=-1, keepdims=True))


def _first_block_mask(n):
    col = lax.broadcasted_iota(jnp.int32, (2 * CHUNK, 2 * CHUNK), 1)
    return jnp.where((col < CHUNK) & (n == 0), NEG, 0.0)


def _stack_heads(x128, lo):
    return jnp.concatenate([jnp.where(lo, x128, 0.0), jnp.where(lo, 0.0, x128)], axis=0).astype(MM)


def _rms(xf):
    return lax.rsqrt(jnp.mean(xf * xf, axis=-1, keepdims=True) + EPS)


def _layer(x2, tgt2, mkv3, bias, sinks, vg, vb, wt, wtt, bcol, g1, g2, w_in_t, w_o, buckets, nb, s, tm):
    nt = s // tm
    bpt = tm // CHUNK
    bps = s // CHUNK
    t = nb * s
    last_step = nb * nt - 1

    def tile_at(step):
        return (step // nt) * nt + nt - 1 - step % nt

    def block_before(step):
        return (step // nt) * bps + jnp.maximum((nt - 1 - step % nt) * bpt - 1, 0)

    def body(x_ref, xp_ref, xn_ref, xpn_ref, t_ref, mkv_ref, bias_ref, sink_ref, vg_ref, vb_ref,
             wt_ref, wtt_ref, bcol_ref, g1_ref, g2_ref, wi_ref, wo_ref, bk_ref,
             gx_ref, dmkv_ref, dwi_hbm, dwo_hbm, dg1_ref, dg2_ref, loss_ref, dwsp_ref, dbs_ref,
             dvg_ref, dvb_ref, dsink_ref, drel_ref,
             acc_i, acc_o, uv_s, z_s, q_s, kv_s, h_s, hp_s, dp_s, dxo_s, dh_s, r_s,
             ycat, dyc, u_s, gu_s, gv_s, xh_s, vc_s, pb_s, ps_s, pc_s, kd_s, vd_s,
             dkv_acc, dbias_acc, dsv_acc, dsink_acc, sems):
        b, j = pl.program_id(0), pl.program_id(1)
        jt = nt - 1 - j
        step = b * nt + j
        g1v = g1_ref[...]
        NOW, NEXT, DONE = 0, 1, 2
        dw_cols = list(DW_PIECES)

        def weight_grad(n, slot):
            for c0, c1 in dw_cols[:n]:
                acc_i[c0:c1, :] += _dot_tn(dp_s[:, c0:c1], h_s[slot])
            del dw_cols[:n]

        def pre_norm(x_tile, x_before):
            xf = x_tile[...]
            r_s[NEXT] = _rms(xf)
            h_s[NEXT] = (xf * r_s[NEXT] * g1v).astype(MM)
            xp = x_before[...]
            hp_s[...] = (xp * _rms(xp) * g1v).astype(MM)

        def project_z():
            z_s[...] = _dot_nt(h_s[NEXT], wi_ref[Z_COL:IN_WIDTH, :])

        def project_uv():
            uv_s[...] = _dot_nt(h_s[NEXT], wi_ref[0:UV_W, :])

        @pl.when(step == 0)
        def _():
            for ref in (acc_i, acc_o, dg1_ref, dg2_ref, loss_ref, dwsp_ref, dvg_ref, dvb_ref,
                        dbias_acc, dsv_acc, dsink_acc):
                ref[...] = jnp.zeros_like(ref)
            dp_s[...] = jnp.zeros_like(dp_s)
            h_s[NOW] = jnp.zeros((tm, D_MODEL), MM)
            pre_norm(x_ref, xp_ref)
            project_z()
            project_uv()

        h_s[DONE] = h_s[NOW]
        r_s[NOW] = r_s[NEXT]
        h = h_s[NEXT]
        h_s[NOW] = h
        hp = hp_s[...]

        @pl.when(j == 0)
        def _():
            dmkv_ref[...] = jnp.zeros_like(dmkv_ref)
            dkv_acc[...] = jnp.zeros_like(dkv_acc)

        carry = dkv_acc[0:CHUNK, :]
        dkv_acc[...] = jnp.zeros_like(dkv_acc)
        dkv_acc[tm:tm + CHUNK, :] = carry

        lo = _half_masks(CHUNK)
        lob = _half_masks(2 * CHUNK)
        lot = _half_masks(tm)

        qkv = _dot_nt(h, wi_ref[SQ_COL:Z_COL, :])
        q_s[:, 0:256] = qkv[:, 0:256].astype(MM)
        q_s[:, 256:512] = qkv[:, 512:768].astype(MM)
        kv_s[CHUNK:CHUNK + tm, :] = qkv[:, 256:512].astype(MM)
        kv_s[0:CHUNK, :] = _dot_nt(hp, wi_ref[SK_COL:MQ_COL, :]).astype(MM)

        weight_grad(1, DONE)
        b_qk, b_pb = [], []
        for blk in range(bpt):
            r0 = blk * CHUNK
            rows = slice(r0, r0 + CHUNK)
            for g in range(A_GROUPS):
                cg = slice(g * CHUNK, (g + 1) * CHUNK)
                u, gu = _gelu_and_grad(uv_s[rows, cg])
                v, gv = _gelu_and_grad(uv_s[rows, A_WIDTH + g * CHUNK:A_WIDTH + (g + 1) * CHUNK])
                mu = jnp.mean(v, axis=-1, keepdims=True)
                xc = v - mu
                rstd = lax.rsqrt(jnp.mean(xc * xc, axis=-1, keepdims=True) + EPS)
                xhat = xc * rstd
                vc = (xhat * vg_ref[:, cg] + vb_ref[:, cg]).astype(MM)
                sv = _dot(wt_ref[g], vc) + bcol_ref[g]
                u_s[rows, cg] = u
                gu_s[rows, cg] = sv * gu
                gv_s[rows, cg] = rstd * gv
                xh_s[rows, cg] = xhat
                vc_s[rows, cg] = vc
                ycat[rows, cg] = u * sv
            weight_grad(1, DONE)
            kd = _dup_heads(kv_s[r0:r0 + 2 * CHUNK, 0:CHUNK])
            vd = _dup_heads(kv_s[r0:r0 + 2 * CHUNK, CHUNK:2 * CHUNK])
            for kvh in range(2):
                kd_s[blk * 2 + kvh] = kd[kvh]
                vd_s[blk * 2 + kvh] = vd[kvh]
                q2 = _stack_heads(q_s[rows, kvh * CHUNK:(kvh + 1) * CHUNK].astype(F32), lo)
                b_qk.append(_dot_nt(q2, kd[kvh]))
        qks, pcs = [], []
        for g in range(2):
            q128 = q_s[:, 256 + g * CHUNK:256 + (g + 1) * CHUNK].astype(F32)
            for hh in range(2):
                qsel = jnp.where(lot if hh == 0 else ~lot, q128, 0.0).astype(MM)
                qks.append(_dot_nt(qsel, mkv_ref[:, g * CHUNK:(g + 1) * CHUNK]))
        weight_grad(2, DONE)
        top = lax.broadcasted_iota(jnp.int32, (2 * CHUNK, 1), 0) < CHUNK
        for blk in range(bpt):
            first_add = _first_block_mask(jt * bpt + blk)
            for kvh in range(2):
                sink2 = jnp.where(top, sink_ref[2 * kvh], sink_ref[2 * kvh + 1])
                probs, ps = _swa_probs(b_qk[blk * 2 + kvh], bias_ref[kvh], sink2, first_add)
                pb_s[blk * 2 + kvh] = probs
                ps_s[blk * 2 + kvh] = jnp.broadcast_to(ps, (2 * CHUNK, CHUNK))
                b_pb.append(probs.astype(MM))
        weight_grad(len(dw_cols), DONE)
        for hd in range(4):
            probs = _softmax(qks[hd] * SCALE)
            pc_s[hd] = probs
            pcs.append(probs.astype(MM))
        for blk in range(bpt):
            rows = slice(blk * CHUNK, (blk + 1) * CHUNK)
            for kvh in range(2):
                out2 = _dot(b_pb[blk * 2 + kvh], vd_s[blk * 2 + kvh])
                ycat[rows, YB_OFF + kvh * CHUNK:YB_OFF + (kvh + 1) * CHUNK] = jnp.where(
                    lo, out2[0:CHUNK], out2[CHUNK:2 * CHUNK])
        outs = [_dot(pcs[hd], mkv_ref[:, MEM_LEN + (hd // 2) * CHUNK:MEM_LEN + (hd // 2 + 1) * CHUNK])
                for hd in range(4)]
        for g in range(2):
            ycat[:, YC_OFF + g * CHUNK:YC_OFF + (g + 1) * CHUNK] = jnp.where(lot, outs[2 * g], outs[2 * g + 1])

        zt = z_s[...]
        sig = 1.0 / (1.0 + jnp.exp(-zt))
        silu = zt * sig
        yc = ycat[...]
        yb = (yc * silu).astype(MM)
        pre_norm(xn_ref, xpn_ref)
        o = _dot(yb, wo_ref[...])
        project_z()
        r2 = _rms(o)
        nrm = o * r2
        g2v = g2_ref[...]
        e = x_ref[...] + nrm * g2v - t_ref[...]
        l1 = jnp.sum(e * e, axis=-1, keepdims=True)
        loss_ref[...] += jnp.broadcast_to(jnp.sum(l1, axis=0, keepdims=True) * (0.5 / D_MODEL), loss_ref.shape)
        dxo = e * (1.0 / D_MODEL)
        dxo_s[...] = dxo
        dg2_ref[...] += jnp.sum(dxo * nrm, axis=0, keepdims=True)
        dn = dxo * g2v
        do = r2 * (dn - nrm * jnp.mean(dn * nrm, axis=-1, keepdims=True))
        dob = do.astype(MM)
        dy = _dot_nt(dob, wo_ref[...])
        dp_s[:, Z_COL:IN_WIDTH] = (dy * yc * (sig * (1.0 + zt * (1.0 - sig)))).astype(MM)
        dyc[...] = dy * silu
        acc_o[...] += _dot_tn(yb, dob)

        def in_proj_bwd(c0, c1):
            part = _dot(dp_s[:, c0:c1], wi_ref[c0:c1, :])
            if c0 == Z_COL:
                dh_s[...] = part
            else:
                dh_s[...] += part

        in_proj_bwd(Z_COL, IN_WIDTH)

        for blk in range(bpt):
            r0 = blk * CHUNK
            rows = slice(r0, r0 + CHUNK)
            for g in range(A_GROUPS):
                cg = slice(g * CHUNK, (g + 1) * CHUNK)
                cv = slice(A_WIDTH + g * CHUNK, A_WIDTH + (g + 1) * CHUNK)
                dya = dyc[rows, cg]
                dp_s[rows, cg] = (dya * gu_s[rows, cg]).astype(MM)
                dsv = dya * u_s[rows, cg]
                dsvb = dsv.astype(MM)
                dsv_acc[g] += dsv
                dwsp_ref[g] += _dot_nt(dsvb, vc_s[rows, cg])
                dvc = _dot(wtt_ref[g], dsvb)
                xhat = xh_s[rows, cg]
                dvg_ref[:, cg] += jnp.sum(dvc * xhat, axis=0, keepdims=True)
                dvb_ref[:, cg] += jnp.sum(dvc, axis=0, keepdims=True)
                dxh = dvc * vg_ref[:, cg]
                dv = (dxh - jnp.mean(dxh, axis=-1, keepdims=True)
                      - xhat * jnp.mean(dxh * xhat, axis=-1, keepdims=True))
                dp_s[rows, cv] = (dv * gv_s[rows, cg]).astype(MM)
        in_proj_bwd(0, UV_W)
        b_dosel, b_dp, b_dss = [], [], []
        for blk in range(bpt):
            rows = slice(blk * CHUNK, (blk + 1) * CHUNK)
            for kvh in range(2):
                b_dosel.append(_stack_heads(dyc[rows, YB_OFF + kvh * CHUNK:YB_OFF + (kvh + 1) * CHUNK], lo))
                b_dp.append(_dot_nt(b_dosel[-1], vd_s[blk * 2 + kvh]))
        dosels, dps, dsss = [], [], []
        for hd in range(4):
            do128 = dyc[:, YC_OFF + (hd // 2) * CHUNK:YC_OFF + (hd // 2 + 1) * CHUNK]
            dosels.append(jnp.where(lot if hd % 2 == 0 else ~lot, do128, 0.0).astype(MM))
            dps.append(_dot_nt(dosels[hd], mkv_ref[:, MEM_LEN + (hd // 2) * CHUNK:MEM_LEN + (hd // 2 + 1) * CHUNK]))
        for blk in range(bpt):
            for kvh in range(2):
                probs = pb_s[blk * 2 + kvh]
                dp = b_dp[blk * 2 + kvh]
                delta = jnp.sum(probs * dp, axis=-1, keepdims=True)
                ds = probs * (dp - delta)
                dbias_acc[kvh] += ds
                sd = ps_s[blk * 2 + kvh][:, 0:1] * delta
                for gi in range(2):
                    hd = 2 * kvh + gi
                    dsink_acc[hd:hd + 1, :] += jnp.broadcast_to(
                        -jnp.sum(sd[gi * CHUNK:(gi + 1) * CHUNK], axis=0, keepdims=True), (1, CHUNK))
                b_dss.append((ds * SCALE).astype(MM))
        for hd in range(4):
            probs = pc_s[hd]
            ds = probs * (dps[hd] - jnp.sum(probs * dps[hd], axis=-1, keepdims=True))
            dsss.append((ds * SCALE).astype(MM))
        for blk in range(bpt):
            r0 = blk * CHUNK
            rows = slice(r0, r0 + CHUNK)
            dk_f, dv_f = [], []
            for kvh in range(2):
                dss = b_dss[blk * 2 + kvh]
                q2 = _stack_heads(q_s[rows, kvh * CHUNK:(kvh + 1) * CHUNK].astype(F32), lo)
                dq2 = _dot(dss, kd_s[blk * 2 + kvh])
                dkd = _dot_tn(dss, q2)
                dvd = _dot_tn(pb_s[blk * 2 + kvh].astype(MM), b_dosel[blk * 2 + kvh])
                dp_s[rows, SQ_COL + kvh * CHUNK:SQ_COL + (kvh + 1) * CHUNK] = jnp.where(
                    lo, dq2[0:CHUNK], dq2[CHUNK:2 * CHUNK]).astype(MM)
                dk_f.append(dkd + pltpu.roll(dkd, 64, 1))
                dv_f.append(dvd + pltpu.roll(dvd, 64, 1))
            dkv_acc[r0:r0 + 2 * CHUNK, 0:CHUNK] += jnp.where(lob, dk_f[0], dk_f[1])
            dkv_acc[r0:r0 + 2 * CHUNK, CHUNK:2 * CHUNK] += jnp.where(lob, dv_f[0], dv_f[1])
        dp_s[:, SK_COL:MQ_COL] = dkv_acc[CHUNK:CHUNK + tm, :].astype(MM)
        for g in range(2):
            q128 = q_s[:, 256 + g * CHUNK:256 + (g + 1) * CHUNK].astype(F32)
            k128 = mkv_ref[:, g * CHUNK:(g + 1) * CHUNK]
            dq128 = jnp.zeros((tm, CHUNK), F32)
            dk128 = jnp.zeros((MEM_LEN, CHUNK), F32)
            dv128 = jnp.zeros((MEM_LEN, CHUNK), F32)
            for hh in range(2):
                hd = 2 * g + hh
                half = lot if hh == 0 else ~lot
                qsel = jnp.where(half, q128, 0.0).astype(MM)
                dq128 = dq128 + jnp.where(half, _dot(dsss[hd], k128), 0.0)
                dk128 = dk128 + _dot_tn(dsss[hd], qsel)
                dv128 = dv128 + _dot_tn(pc_s[hd].astype(MM), dosels[hd])
            dp_s[:, MQ_COL + g * CHUNK:MQ_COL + (g + 1) * CHUNK] = dq128.astype(MM)
            dmkv_ref[:, g * CHUNK:(g + 1) * CHUNK] += dk128
            dmkv_ref[:, MEM_LEN + g * CHUNK:MEM_LEN + (g + 1) * CHUNK] += dv128

        in_proj_bwd(SQ_COL, Z_COL)
        project_uv()
        dh = dh_s[...]
        r = r_s[NOW]
        nx = x_ref[...] * r
        dg1_ref[...] += jnp.sum(dh * nx, axis=0, keepdims=True)
        dnx = dh * g1v
        gx_ref[...] = dxo_s[...] + r * (dnx - nx * jnp.mean(dnx * nx, axis=-1, keepdims=True))

        @pl.when(step == last_step)
        def _():
            dw_cols.extend(DW_PIECES)
            weight_grad(len(dw_cols), NOW)
            out_i = pltpu.make_async_copy(acc_i, dwi_hbm, sems.at[0])
            out_o = pltpu.make_async_copy(acc_o, dwo_hbm, sems.at[1])
            out_i.start()
            out_o.start()
            r_ = lax.broadcasted_iota(jnp.int32, (CHUNK, CHUNK), 0)
            c_ = lax.broadcasted_iota(jnp.int32, (CHUNK, CHUNK), 1)
            for g in range(A_GROUPS):
                dwsp_ref[g] = jnp.where(r_ >= c_, dwsp_ref[g], 0.0)
                dbs_ref[g:g + 1, :] = jnp.sum(dsv_acc[g].T, axis=0, keepdims=True)
            rows8 = lax.broadcasted_iota(jnp.int32, (8, CHUNK), 0)
            cols8 = lax.broadcasted_iota(jnp.int32, (8, CHUNK), 1)
            sk = jnp.zeros((8, CHUNK), F32)
            for hd in range(4):
                sk = sk + jnp.where((rows8 == 0) & (cols8 == hd),
                                    jnp.broadcast_to(dsink_acc[hd:hd + 1, :], (8, CHUNK)), 0.0)
            dsink_ref[...] = sk
            bk = bk_ref[...]
            valid = _window_valid()
            rrow = lax.broadcasted_iota(jnp.int32, (N_BUCKETS, CHUNK), 0)
            rcol = lax.broadcasted_iota(jnp.int32, (N_BUCKETS, CHUNK), 1)
            acc = jnp.zeros((N_BUCKETS, CHUNK), F32)
            for bb in range(N_BUCKETS):
                hit = (bk == bb) & valid
                for hd in range(4):
                    dbias = dbias_acc[hd // 2, (hd % 2) * CHUNK:(hd % 2 + 1) * CHUNK, :]
                    part = jnp.sum(jnp.where(hit, dbias, 0.0), axis=-1, keepdims=True)
                    tot = jnp.sum(part, axis=0, keepdims=True)
                    acc = acc + jnp.where((rrow == bb) & (rcol == hd), jnp.broadcast_to(tot, (N_BUCKETS, CHUNK)), 0.0)
            drel_ref[...] = acc
            out_i.wait()
            out_o.wait()

    after = lambda b, j: jnp.minimum(b * nt + j + 1, last_step)
    tile = pl.BlockSpec((tm, D_MODEL), lambda b, j: (tile_at(b * nt + j), 0))
    tile_after = pl.BlockSpec((tm, D_MODEL), lambda b, j: (tile_at(after(b, j)), 0))
    halo = pl.BlockSpec((CHUNK, D_MODEL), lambda b, j: (block_before(b * nt + j), 0))
    halo_after = pl.BlockSpec((CHUNK, D_MODEL), lambda b, j: (block_before(after(b, j)), 0))
    per_batch = lambda r, w: pl.BlockSpec((None, r, w), lambda b, j: (b, 0, 0))
    anyspec = pl.BlockSpec(memory_space=pl.ANY)
    grp = (A_GROUPS, CHUNK, CHUNK)
    return pl.pallas_call(
        body, name="layer", grid=(nb, nt),
        out_shape=(jax.ShapeDtypeStruct((t, D_MODEL), F32),
                   jax.ShapeDtypeStruct((nb, MEM_LEN, 2 * MEM_LEN), F32),
                   jax.ShapeDtypeStruct((IN_WIDTH, D_MODEL), F32),
                   jax.ShapeDtypeStruct((D_MODEL, D_MODEL), F32),
                   jax.ShapeDtypeStruct((1, D_MODEL), F32),
                   jax.ShapeDtypeStruct((1, D_MODEL), F32),
                   jax.ShapeDtypeStruct((8, CHUNK), F32),
                   jax.ShapeDtypeStruct(grp, F32),
                   jax.ShapeDtypeStruct((A_GROUPS, CHUNK), F32),
                   jax.ShapeDtypeStruct((1, A_WIDTH), F32),
                   jax.ShapeDtypeStruct((1, A_WIDTH), F32),
                   jax.ShapeDtypeStruct((8, CHUNK), F32),
                   jax.ShapeDtypeStruct((N_BUCKETS, CHUNK), F32)),
        in_specs=[tile, halo, tile_after, halo_after, tile, per_batch(MEM_LEN, 2 * MEM_LEN),
                  _full((2, 2 * CHUNK, 2 * CHUNK)),
                  pl.BlockSpec(memory_space=pltpu.SMEM),
                  _full((1, A_WIDTH)), _full((1, A_WIDTH)),
                  _full(grp), _full(grp), _full(grp),
                  _full((1, D_MODEL)), _full((1, D_MODEL)),
                  _full((IN_WIDTH, D_MODEL), single=True), _full((D_MODEL, D_MODEL), single=True),
                  _full((CHUNK, 2 * CHUNK))],
        out_specs=(tile, per_batch(MEM_LEN, 2 * MEM_LEN), anyspec, anyspec,
                   _full((1, D_MODEL)), _full((1, D_MODEL)), _full((8, CHUNK)),
                   _full(grp), _full((A_GROUPS, CHUNK)), _full((1, A_WIDTH)), _full((1, A_WIDTH)),
                   _full((8, CHUNK)), _full((N_BUCKETS, CHUNK))),
        scratch_shapes=[pltpu.VMEM((IN_WIDTH, D_MODEL), F32), pltpu.VMEM((D_MODEL, D_MODEL), F32),
                        pltpu.VMEM((tm, UV_W), F32), pltpu.VMEM((tm, Z_W), F32),
                        pltpu.VMEM((tm, 512), MM), pltpu.VMEM((tm + CHUNK, 2 * CHUNK), MM),
                        pltpu.VMEM((3, tm, D_MODEL), MM), pltpu.VMEM((CHUNK, D_MODEL), MM),
                        pltpu.VMEM((tm, IN_WIDTH), MM),
                        pltpu.VMEM((tm, D_MODEL), F32),
                        pltpu.VMEM((tm, D_MODEL), F32), pltpu.VMEM((2, tm, 1), F32),
                        pltpu.VMEM((tm, D_MODEL), F32), pltpu.VMEM((tm, D_MODEL), F32)]
                       + [pltpu.VMEM((tm, A_WIDTH), F32) for _ in range(4)]
                       + [pltpu.VMEM((tm, A_WIDTH), MM),
                          pltpu.VMEM((bpt * 2, 2 * CHUNK, 2 * CHUNK), F32),
                          pltpu.VMEM((bpt * 2, 2 * CHUNK, CHUNK), F32),
                          pltpu.VMEM((4, tm, MEM_LEN), F32),
                          pltpu.VMEM((bpt * 2, 2 * CHUNK, CHUNK), MM),
                          pltpu.VMEM((bpt * 2, 2 * CHUNK, CHUNK), MM),
                          pltpu.VMEM((tm + CHUNK, 2 * CHUNK), F32),
                          pltpu.VMEM((2, 2 * CHUNK, 2 * CHUNK), F32),
                          pltpu.VMEM(grp, F32),
                          pltpu.VMEM((8, CHUNK), F32),
                          pltpu.SemaphoreType.DMA((2,))],
        compiler_params=_params(dimension_semantics=("arbitrary", "arbitrary")),
    )(x2, x2, x2, x2, tgt2, mkv3, bias.reshape(2, 2 * CHUNK, 2 * CHUNK), sinks, vg, vb, wt, wtt, bcol, g1, g2, w_in_t, w_o, buckets)


class _ShardReduce:
    def __init__(self, pos, g, bufs, sems):
        self.x, self.y, self.c = pos
        self.g = g
        self.own, self.rcv, self.sbuf, self.rbuf, self.cbuf = bufs
        self.ld, self.sa, self.ra, self.sb, self.rb = sems
        self.nrow = g.shape[1]
        self.here = (self.x, self.y, self.c)
        self.sib = (self.x, self.y, 1 - self.c)
        self.first, self.second, self.far = _route(*pos)

    def _load(self, q):
        return pltpu.make_async_copy(self.g.at[2 * q + self.c], self.own.at[q], self.ld.at[q])

    def _to_sib(self, q, to):
        return _remote(self.g.at[2 * q + 1 - self.c], self.rcv.at[q], self.sa.at[q], self.ra.at[q], to)

    def _send(self, k, to):
        dst = self.cbuf.at[0] if k == 1 else self.rbuf.at[0 if k == 0 else 1]
        return _remote(self.sbuf.at[k], dst, self.sb.at[k], self.rb.at[k], to)

    def _stage(self, k, which, extra=None):
        def cast(r):
            v = self.rcv[which, r, :]
            if extra is not None:
                v = v + extra[0, r, :].astype(F32)
            self.sbuf[k, r, :] = v.astype(BF16)

        _rows_loop(self.nrow, cast)

    @staticmethod
    def _q(chip):
        return 2 * chip[0] + chip[1]

    def start(self):
        for q in range(4):
            self._load(q).start()
            self._to_sib(q, self.sib).start()

    def mid(self):
        for q in range(4):
            self._load(q).wait()
            self._to_sib(q, self.here).wait_recv()

        def add(r):
            for q in range(4):
                self.rcv[q, r, :] = self.rcv[q, r, :] + self.own[q, r, :]

        _rows_loop(self.nrow, add)
        to_first = (self.first[0], self.first[1], self.c)
        self._stage(0, self._q(self.first))
        self._send(0, to_first).start()
        self._stage(1, self._q(self.far))
        self._send(1, to_first).start()

    def pass_on(self):
        self._send(1, self.here).wait_recv()
        self._stage(2, self._q(self.second), extra=self.cbuf)
        self._send(2, (self.second[0], self.second[1], self.c)).start()

    def finish(self, out):
        self._send(0, self.here).wait_recv()
        self._send(2, self.here).wait_recv()
        which = 2 * self.x + self.y

        def tot(r):
            out[r, :] = (self.rcv[which, r, :] + self.rbuf[0, r, :].astype(F32)) + self.rbuf[1, r, :].astype(F32)

        _rows_loop(self.nrow, tot)
        for q in range(4):
            self._to_sib(q, self.sib).wait_send()
        to_first = (self.first[0], self.first[1], self.c)
        self._send(0, to_first).wait_send()
        self._send(1, to_first).wait_send()
        self._send(2, (self.second[0], self.second[1], self.c)).wait_send()


def _reduce_scratch(shape):
    return [pltpu.VMEM((4,) + shape, F32), pltpu.VMEM((4,) + shape, F32),
            pltpu.VMEM((3,) + shape, BF16), pltpu.VMEM((2,) + shape, BF16), pltpu.VMEM((1,) + shape, BF16),
            pltpu.SemaphoreType.DMA((4,)), pltpu.SemaphoreType.DMA((4,)), pltpu.SemaphoreType.DMA((4,)),
            pltpu.SemaphoreType.DMA((3,)), pltpu.SemaphoreType.DMA((3,))]


_N_RED = 10

_S_LAYOUT = (((1, D_MODEL), 0), ((1, D_MODEL), 8), ((1, D_MODEL), 16),
             ((1, A_WIDTH), 24), ((1, A_WIDTH), 28), ((A_GROUPS, CHUNK), 32),
             ((1, 4), 36), ((N_BUCKETS, 4), 40),
             ((A_GROUPS * CHUNK, CHUNK), 72))
_LOSS_ROW = 37
_W_SP_ROW = _S_LAYOUT[-1][1]
_S_ROWS = _W_SP_ROW + A_GROUPS * CHUNK
_N_SMALL = len(_S_LAYOUT)


def _pack_rows(dst, refs):
    for (shp, r0), ref in zip(_S_LAYOUT, refs):
        if shp[0] == 1 and shp[1] >= CHUNK:
            for i in range(shp[1] // CHUNK):
                dst[r0 + i:r0 + i + 1, :] = ref[:, i * CHUNK:(i + 1) * CHUNK]
        elif ref.shape[-1] == CHUNK:
            dst[r0:r0 + shp[0], :] = ref[0:shp[0], :]
        else:
            dst[r0:r0 + shp[0], 0:shp[1]] = ref[...]


def _unpack_rows(src, refs):
    for (shp, r0), ref in zip(_S_LAYOUT, refs):
        if shp[0] == 1 and shp[1] >= CHUNK:
            for i in range(shp[1] // CHUNK):
                ref[:, i * CHUNK:(i + 1) * CHUNK] = src[r0 + i:r0 + i + 1, :]
        elif shp[1] == CHUNK:
            ref[...] = src[r0:r0 + shp[0], :]
        else:
            if tuple(ref.shape) == (shp[1], shp[0]):
                ref[...] = src[r0:r0 + CHUNK, :].T[0:shp[1], 0:shp[0]]
            else:
                ref[...] = src[r0:r0 + shp[0], 0:shp[1]]


_MEM_G = 2


def _greduce(ga, gb, dmkv, mem2, gm, w_mkv, small_g, loss_p):
    shp_c = (SHARD_O, 2 * MEM_LEN)
    shapes = (shp_c, gb.shape[1:], ga.shape[1:])
    rs = _S_ROWS

    def body(*refs):
        it = iter(refs)
        take = lambda n: [next(it) for _ in range(n)]
        gb_ref, ga_ref, d_ref, m_ref, gm_ref, wm_ref = take(6)
        sg_refs = take(_N_SMALL - 1)
        loss_ref, = take(1)
        oc, ob, oa, ogs = take(4)
        red = take(3 * _N_RED)
        gs_ref, rs_a, rs_b, rs_w, gc_ref, dgm_ref = take(6)
        ssem_a, rsem_a, ssem_b, rsem_b = take(4)

        pos = _position()
        x, y, cc = pos
        myq = 2 * x + y
        here, sib = (x, y, cc), (x, y, 1 - cc)
        chips = _other_chips(x, y)
        reducers = [_ShardReduce(pos, g, red[k * _N_RED:k * _N_RED + 5], red[k * _N_RED + 5:(k + 1) * _N_RED])
                    for k, g in enumerate((gc_ref, gb_ref, ga_ref))]
        for rd in reducers[1:]:
            rd.start()

        xf = m_ref[...]
        nm = xf * _rms(xf)
        hm = (nm * gm_ref[...]).astype(MM)
        d = d_ref[...].astype(MM)
        for o in range(N_DEV):
            gc_ref[o] = _dot_tn(hm[:, o * SHARD_O:(o + 1) * SHARD_O], d)
        dgm_ref[...] = jnp.sum(_dot_nt(d, wm_ref[...]) * nm, axis=0, keepdims=True)
        reducers[0].start()

        gs_ref[...] = jnp.zeros_like(gs_ref)
        _pack_rows(gs_ref, sg_refs[:_MEM_G] + [dgm_ref] + sg_refs[_MEM_G:])
        gs_ref[_LOSS_ROW:_LOSS_ROW + 1, :] = loss_ref[0:1, :]
        small_a = _remote(gs_ref, rs_a, ssem_a, rsem_a, sib)
        small_a.start()

        _remote(gs_ref, rs_a, ssem_a, rsem_a, here).wait_recv()
        rs_b[myq] = gs_ref[0:_W_SP_ROW, :] + rs_a[0:_W_SP_ROW, :]
        rs_w[myq] = (gs_ref[_W_SP_ROW:rs, :] + rs_a[_W_SP_ROW:rs, :]).astype(BF16)
        small_b = []
        for j, chip in enumerate(chips):
            to = (chip[0], chip[1], cc)
            small_b.append(_remote(rs_b.at[myq], rs_b.at[myq], ssem_b.at[0, j], rsem_b.at[0, j], to))
            small_b.append(_remote(rs_w.at[myq], rs_w.at[myq], ssem_b.at[1, j], rsem_b.at[1, j], to))
        for cp in small_b:
            cp.start()
        late_last = reducers[1:] + reducers[:1]
        for rd in late_last:
            rd.mid()
        for rd in late_last:
            rd.pass_on()

        for j in range(3):
            _remote(rs_b.at[myq], rs_b.at[myq], ssem_b.at[0, j], rsem_b.at[0, j], here).wait_recv()
            _remote(rs_w.at[myq], rs_w.at[myq], ssem_b.at[1, j], rsem_b.at[1, j], here).wait_recv()
        ogs[0:_W_SP_ROW, :] = ((rs_b[0] + rs_b[1]) + rs_b[2]) + rs_b[3]

        def tot_w(r):
            w = [rs_w[q, r, :].astype(F32) for q in range(4)]
            ogs[pl.ds(pl.multiple_of(_W_SP_ROW + r.start, 8), _ROWS), :] = ((w[0] + w[1]) + w[2]) + w[3]

        _rows_loop(rs - _W_SP_ROW, tot_w)
        for rd, out in zip(late_last, (ob, oa, oc)):
            rd.finish(out)
        small_a.wait_send()
        for cp in small_b:
            cp.wait_send()

    vm = pl.BlockSpec(memory_space=pltpu.VMEM)
    anyspec = pl.BlockSpec(memory_space=pl.ANY)
    scratch = []
    for shp in shapes:
        scratch += _reduce_scratch(shp)
    scratch += [pltpu.VMEM((rs, CHUNK), F32), pltpu.VMEM((rs, CHUNK), F32),
                pltpu.VMEM((4, _W_SP_ROW, CHUNK), F32), pltpu.VMEM((4, rs - _W_SP_ROW, CHUNK), BF16),
                pltpu.VMEM((N_DEV,) + shp_c, F32), pltpu.VMEM((1, D_MODEL), F32),
                pltpu.SemaphoreType.DMA, pltpu.SemaphoreType.DMA,
                pltpu.SemaphoreType.DMA((2, 3)), pltpu.SemaphoreType.DMA((2, 3))]
    tc, tb, ta, ts = pl.pallas_call(
        body, name="greduce",
        out_shape=tuple([jax.ShapeDtypeStruct(shp, F32) for shp in shapes] + [jax.ShapeDtypeStruct((rs, CHUNK), F32)]),
        in_specs=[anyspec] * 2 + [vm] * (4 + _N_SMALL),
        out_specs=(vm, vm, vm, vm),
        scratch_shapes=scratch,
        compiler_params=_params(),
    )(gb, ga, dmkv, mem2, gm, w_mkv, *small_g, loss_p)
    return ta, tb, tc, ts


def _adamw(w, g, m, v):
    m = ADAM_B1 * m + (1.0 - ADAM_B1) * g
    v = ADAM_B2 * v + (1.0 - ADAM_B2) * (g * g)
    m_hat = m / (1.0 - ADAM_B1 ** ADAM_STEP)
    v_hat = v / (1.0 - ADAM_B2 ** ADAM_STEP)
    delta = -ADAM_LR * (m_hat / (jnp.sqrt(v_hat) + ADAM_EPS) + ADAM_WD * w)
    return delta, m, v


def _update(ta, tb, tc, ts, big_wmv, small_wmv):
    shapes = (ta.shape, tb.shape, tc.shape)
    rs = _S_ROWS
    small_shapes = [tuple(a.shape) for a in small_wmv[0]]

    def body(*refs):
        it = iter(refs)
        take = lambda n: [next(it) for _ in range(n)]
        ga_ref, gb_ref, gc_ref, gs_ref = take(4)
        wa, ma, va, wb, mb, vb_, wc, mc, vc = take(9)
        sw_refs, sm_refs, sv_refs = take(_N_SMALL), take(_N_SMALL), take(_N_SMALL)
        oga, oda, oma, ova, ogb, odb, omb, ovb, ogc, odc, omc, ovc = take(12)
        so_refs = [take(_N_SMALL) for _ in range(4)]
        loss_out, = take(1)
        ws, ms, vs, ods, oms, ovs = take(6)

        for buf in (ws, ms, vs):
            buf[...] = jnp.zeros_like(buf)
        _pack_rows(ws, sw_refs)
        _pack_rows(ms, sm_refs)
        _pack_rows(vs, sv_refs)

        big = ((ga_ref, wa, ma, va, oga, oda, oma, ova), (gb_ref, wb, mb, vb_, ogb, odb, omb, ovb),
               (gc_ref, wc, mc, vc, ogc, odc, omc, ovc))
        for arr in range(3):
            g_r, w_r, m_r, v_r, og, od, om, ov = big[arr]

            def upd(r, g_r=g_r, w_r=w_r, m_r=m_r, v_r=v_r, og=og, od=od, om=om, ov=ov):
                g = g_r[r, :]
                d, m, v = _adamw(w_r[r, :], g, m_r[r, :], v_r[r, :])
                og[r, :] = g
                od[r, :] = d
                om[r, :] = m
                ov[r, :] = v

            _rows_loop(shapes[arr][0], upd)

        def upd_s(i, _):
            r = pl.ds(pl.multiple_of(i * 8, 8), 8)
            d, m, v = _adamw(ws[r, :], gs_ref[r, :], ms[r, :], vs[r, :])
            ods[r, :] = d
            oms[r, :] = m
            ovs[r, :] = v
            return 0

        lax.fori_loop(0, rs // 8, upd_s, 0)
        for k, buf in enumerate((gs_ref, ods, oms, ovs)):
            _unpack_rows(buf, so_refs[k])
        loss_out[...] = gs_ref[_LOSS_ROW:_LOSS_ROW + 1, 0:1]

    vm = pl.BlockSpec(memory_space=pltpu.VMEM)
    big_out = []
    for shp in shapes:
        big_out += [jax.ShapeDtypeStruct(shp, F32)] * 4
    small_out = [jax.ShapeDtypeStruct(shp[::-1] if shp == (N_BUCKETS, 4) else shp, F32) for shp in small_shapes] * 4
    out_shape = tuple(big_out + small_out + [jax.ShapeDtypeStruct((1, 1), F32)])
    n_in = 4 + 9 + 3 * _N_SMALL
    return pl.pallas_call(
        body, name="update",
        out_shape=out_shape,
        in_specs=[vm] * n_in,
        out_specs=tuple([vm] * len(out_shape)),
        scratch_shapes=[pltpu.VMEM((rs, CHUNK), F32) for _ in range(6)],
        compiler_params=_params(),
    )(ta, tb, tc, ts, *big_wmv, *small_wmv[0], *small_wmv[1], *small_wmv[2])


def kernel(x, mem, pre_norm_g, post_norm_g, mem_norm_g, w_in, w_mem_kv, v_norm_g, v_norm_b, w_spatial, b_spatial, attn_sinks, rel_bias, w_out, loss_target, m_pre_norm_g, m_post_norm_g, m_mem_norm_g, m_w_in, m_w_mem_kv, m_v_norm_g, m_v_norm_b, m_w_spatial, m_b_spatial, m_attn_sinks, m_rel_bias, m_w_out, v_pre_norm_g, v_post_norm_g, v_mem_norm_g, v_w_in, v_w_mem_kv, v_v_norm_g, v_v_norm_b, v_w_spatial, v_b_spatial, v_attn_sinks, v_rel_bias, v_w_out):
    sh_a = (w_in[0].T, m_w_in[0].T, v_w_in[0].T)
    sh_b = (w_out[0], m_w_out[0], v_w_out[0])
    sh_c = (w_mem_kv[0], m_w_mem_kv[0], v_w_mem_kv[0])
    nb, s, _ = x.shape
    t = nb * s
    x2 = x.reshape(t, D_MODEL)
    tgt2 = loss_target.reshape(t, D_MODEL)
    mem2 = mem.reshape(nb * MEM_LEN, D_MODEL)
    buckets = jnp.asarray(_t5_buckets())

    wa, wb, wc, bias, wt, wtt, bcol, mkv = _wgather(sh_a[0], sh_b[0], sh_c[0], rel_bias, w_spatial[0], b_spatial[0],
                                                    buckets, mem2, mem_norm_g)
    w_mkv = wc.reshape(D_MODEL, 2 * MEM_LEN)
    gx, dmkv, dwi, dwo, dg1, dg2, loss_p, dwsp, dbs, dvg, dvb, dsink, drel = _layer(
        x2, tgt2, mkv.reshape(nb, MEM_LEN, 2 * MEM_LEN), bias, attn_sinks.reshape(4), v_norm_g, v_norm_b, wt, wtt, bcol,
        pre_norm_g, post_norm_g, wa.reshape(IN_WIDTH, D_MODEL), wb.reshape(D_MODEL, D_MODEL), buckets,
        nb, s, min(256, s))
    gx = gx.reshape(nb, s, D_MODEL)
    small_grads = [dg1, dg2, dvg, dvb, dbs, dsink, drel, dwsp.reshape(A_GROUPS * CHUNK, CHUNK)]

    small_names = ["pre_norm_g", "post_norm_g", "mem_norm_g", "v_norm_g", "v_norm_b", "b_spatial", "attn_sinks",
                   "rel_bias", "w_spatial"]
    given = dict(pre_norm_g=(pre_norm_g, m_pre_norm_g, v_pre_norm_g), post_norm_g=(post_norm_g, m_post_norm_g, v_post_norm_g),
                 mem_norm_g=(mem_norm_g, m_mem_norm_g, v_mem_norm_g), v_norm_g=(v_norm_g, m_v_norm_g, v_v_norm_g),
                 v_norm_b=(v_norm_b, m_v_norm_b, v_v_norm_b), b_spatial=(b_spatial, m_b_spatial, v_b_spatial),
                 attn_sinks=(attn_sinks, m_attn_sinks, v_attn_sinks), rel_bias=(rel_bias, m_rel_bias, v_rel_bias),
                 w_spatial=(w_spatial, m_w_spatial, v_w_spatial))
    small_wmv = [[given[n][k].reshape(shp) for n, (shp, _) in zip(small_names, _S_LAYOUT)] for k in range(3)]

    ta, tb, tc, ts = _greduce(dwi.reshape(N_DEV, SHARD_IN, D_MODEL), dwo.reshape(N_DEV, SHARD_O, D_MODEL),
                              dmkv.reshape(nb * MEM_LEN, 2 * MEM_LEN), mem2, mem_norm_g, w_mkv, small_grads, loss_p)
    outs = _update(ta, tb, tc, ts, (*sh_a, *sh_b, *sh_c), small_wmv)
    ra, rb, rc = outs[0:4], outs[4:8], outs[8:12]
    loss = outs[12 + 4 * _N_SMALL].reshape(())

    res = {}
    for k, kind in enumerate(("grad", "delta", "new_m", "new_v")):
        res[kind, "w_in"] = ra[k].T[None]
        res[kind, "w_out"] = rb[k][None]
        res[kind, "w_mem_kv"] = rc[k][None]
        for i, n in enumerate(small_names):
            o = outs[12 + k * _N_SMALL + i]
            res[kind, n] = o.T if n == "rel_bias" else o.reshape(given[n][0].shape)
    order = ["pre_norm_g", "post_norm_g", "mem_norm_g", "w_in", "w_mem_kv", "v_norm_g", "v_norm_b", "w_spatial",
             "b_spatial", "attn_sinks", "rel_bias", "w_out"]
    flat = [res[kind, n] for kind in ("grad", "delta", "new_m", "new_v") for n in order]
    return (loss, gx, *flat)
```

```python
import numpy as np
import jax
import jax.numpy as jnp
from jax import lax
from jax.experimental import pallas as pl
from jax.experimental.pallas import tpu as pltpu

F32 = jnp.float32
BF16 = jnp.bfloat16
MM = jnp.bfloat16

D_MODEL = 1024
CHUNK = 128
A_GROUPS = 4
A_WIDTH = 512
UV_W = 1024
QKV_W = 768
Z_W = 1024
IN_WIDTH = UV_W + QKV_W + Z_W
MEM_LEN = 256
N_BUCKETS = 32
MAX_DISTANCE = 128
EPS = 1e-6
NEG = -1e30
SCALE = 0.125
N_DEV = 8
SHARD_IN = IN_WIDTH // N_DEV
SHARD_O = D_MODEL // N_DEV

SQ_COL, SK_COL, SV_COL, MQ_COL, Z_COL = UV_W, UV_W + 256, UV_W + 384, UV_W + 512, UV_W + QKV_W
DW_PIECES = ((0, SQ_COL), (SQ_COL, Z_COL), (Z_COL, IN_WIDTH))
YB_OFF, YC_OFF = 512, 768

ADAM_LR = 0.001
ADAM_B1 = 0.9
ADAM_B2 = 0.999
ADAM_EPS = 1e-08
ADAM_WD = 0.01
ADAM_STEP = 10

VMEM_LIMIT = 60 * 1024 * 1024

_GELU_C = 0.7978845608028654
_GELU_A = 0.044715

MESH = pl.DeviceIdType.MESH
_ROWS = 32


def _dot(a, b):
    return lax.dot_general(a, b, (((1,), (0,)), ((), ())), preferred_element_type=F32)


def _dot_nt(a, b):
    return lax.dot_general(a, b, (((1,), (1,)), ((), ())), preferred_element_type=F32)


def _dot_tn(a, b):
    return lax.dot_general(a, b, (((0,), (0,)), ((), ())), preferred_element_type=F32)


def _gelu_and_grad(x):
    x2 = x * x
    t = jnp.tanh(_GELU_C * (x + _GELU_A * x * x2))
    g = 0.5 * x * (1.0 + t)
    dg = 0.5 * (1.0 + t) + 0.5 * x * (1.0 - t * t) * (_GELU_C * (1.0 + 3.0 * _GELU_A * x2))
    return g, dg


def _t5_buckets():
    qi = np.arange(CHUNK)[:, None]
    kj = np.arange(2 * CHUNK)[None, :]
    n = np.maximum(qi + CHUNK - kj, 0)
    max_exact = N_BUCKETS // 2
    large = max_exact + (np.log(np.maximum(n, 1) / max_exact) / np.log(MAX_DISTANCE / max_exact)
                         * (N_BUCKETS - max_exact)).astype(np.int32)
    large = np.minimum(large, N_BUCKETS - 1)
    return np.where(n < max_exact, n, large).astype(np.int32)


def _params(**kw):
    return pltpu.CompilerParams(vmem_limit_bytes=VMEM_LIMIT, **kw)


def _full(shape, single=False):
    nd = len(shape)
    if single:
        return pl.BlockSpec(shape, lambda *_: (0,) * nd, pipeline_mode=pl.Buffered(1))
    return pl.BlockSpec(shape, lambda *_: (0,) * nd)


def _window_valid():
    qi = lax.broadcasted_iota(jnp.int32, (CHUNK, 2 * CHUNK), 0)
    kj = lax.broadcasted_iota(jnp.int32, (CHUNK, 2 * CHUNK), 1)
    dist = qi + CHUNK - kj
    return (dist >= 0) & (dist < CHUNK)


def _position():
    return lax.axis_index("x"), lax.axis_index("y"), lax.axis_index("c")


def _other_chips(x, y):
    return [(1 - x, y), (x, 1 - y), (1 - x, 1 - y)]


def _route(x, y, c):
    first = (x * c + (1 - x) * (1 - c), y * (1 - c) + (1 - y) * c)
    second = (x * (1 - c) + (1 - x) * c, y * c + (1 - y) * (1 - c))
    return first, second, (1 - x, 1 - y)


def _remote(src, dst, ssem, rsem, to):
    return pltpu.make_async_remote_copy(src_ref=src, dst_ref=dst, send_sem=ssem, recv_sem=rsem,
                                        device_id=to, device_id_type=MESH)


def _rows_loop(nrow, fn):
    def step(i, _):
        fn(pl.ds(pl.multiple_of(i * _ROWS, _ROWS), _ROWS))
        return 0

    lax.fori_loop(0, nrow // _ROWS, step, 0)


class _Gather:
    def __init__(self, pos, out, ssem, rsem):
        self.x, self.y, self.c = pos
        self.out, self.ssem, self.rsem = out, ssem, rsem
        self.me = 4 * self.x + 2 * self.y + self.c
        self.here = (self.x, self.y, self.c)
        self.sib = (self.x, self.y, 1 - self.c)
        self.first, self.second, self.far = _route(*pos)

    def _copy(self, k, blk, to):
        r = self.out.at[blk]
        return _remote(r, r, self.ssem.at[k], self.rsem.at[k], to)

    def _idx(self, chip, core):
        return 4 * chip[0] + 2 * chip[1] + core

    def _on(self, chip):
        return (chip[0], chip[1], self.c)

    def start(self):
        self._copy(0, self.me, self.sib).start()
        self._copy(1, self.me, self._on(self.first)).start()
        self._copy(2, self.me, self._on(self.second)).start()

    def forward(self):
        c = self.c
        self._copy(1, self._idx(self.first, c), self.here).wait_recv()
        self._copy(3, self._idx(self.first, c), self._on(self.second)).start()
        self._copy(4, self._idx(self.first, c), self.sib).start()
        self._copy(2, self._idx(self.second, c), self.here).wait_recv()
        self._copy(5, self._idx(self.second, c), self.sib).start()
        self._copy(3, self._idx(self.far, c), self.here).wait_recv()
        self._copy(6, self._idx(self.far, c), self.sib).start()

    def finish(self):
        c = self.c
        self._copy(0, self._idx((self.x, self.y), 1 - c), self.here).wait_recv()
        for k, chip in ((4, self.second), (5, self.first), (6, self.far)):
            self._copy(k, self._idx(chip, 1 - c), self.here).wait_recv()
        self._copy(0, self.me, self.sib).wait_send()
        self._copy(1, self.me, self._on(self.first)).wait_send()
        self._copy(2, self.me, self._on(self.second)).wait_send()
        self._copy(3, self._idx(self.first, c), self._on(self.second)).wait_send()
        for k, chip in ((4, self.first), (5, self.second), (6, self.far)):
            self._copy(k, self._idx(chip, c), self.sib).wait_send()


def _prep_tables(rb_ref, w_ref, b_ref, bk_ref, bias_ref, wt_ref, wtt_ref, bcol_ref):
    valid = _window_valid()
    bk = bk_ref[...]
    acc = [jnp.full((CHUNK, 2 * CHUNK), NEG, F32) for _ in range(4)]
    for b in range(N_BUCKETS):
        hit = (bk == b) & valid
        for h in range(4):
            acc[h] = jnp.where(hit, rb_ref[b, h], acc[h])
    for h in range(4):
        bias_ref[h] = acc[h]
    r = lax.broadcasted_iota(jnp.int32, (CHUNK, CHUNK), 0)
    c = lax.broadcasted_iota(jnp.int32, (CHUNK, CHUNK), 1)
    for g in range(A_GROUPS):
        w = jnp.where(r >= c, w_ref[g], 0.0)
        wt_ref[g] = w.astype(MM)
        wtt_ref[g] = w.T.astype(MM)
        bcol_ref[g] = jnp.broadcast_to(b_ref[g:g + 1, :], (CHUNK, CHUNK)).T


def _wgather(a, b, c, rel_bias, w_sp, b_sp, buckets, mem2, gm):
    tmem = mem2.shape[0]

    def body(a_ref, b_ref, c_ref, rb_ref, w_ref, bsp_ref, bk_ref, m_ref, gm_ref,
             oa, ob, oc, bias_ref, wt_ref, wtt_ref, bcol_ref, mkv_ref, ssem, rsem):
        pos = _position()
        me = 4 * pos[0] + 2 * pos[1] + pos[2]
        gathers = []
        for k, (src, out) in enumerate(((c_ref, oc), (b_ref, ob), (a_ref, oa))):
            out[me] = src[...].astype(BF16)
            g = _Gather(pos, out, ssem.at[k], rsem.at[k])
            g.start()
            gathers.append(g)
        _prep_tables(rb_ref, w_ref, bsp_ref, bk_ref, bias_ref, wt_ref, wtt_ref, bcol_ref)
        for g in gathers:
            g.forward()
        gathers[0].finish()
        xf = m_ref[...]
        hm = (xf * _rms(xf) * gm_ref[...]).astype(MM)
        acc = jnp.zeros((tmem, 2 * MEM_LEN), F32)
        for d in range(N_DEV):
            acc = acc + _dot(hm[:, d * SHARD_O:(d + 1) * SHARD_O], oc[d])
        mkv_ref[...] = acc.astype(MM)
        for g in gathers[1:]:
            g.finish()

    vm = pl.BlockSpec(memory_space=pltpu.VMEM)
    grp = (A_GROUPS, CHUNK, CHUNK)
    return pl.pallas_call(
        body, name="wgather",
        out_shape=(jax.ShapeDtypeStruct((N_DEV,) + a.shape, BF16),
                   jax.ShapeDtypeStruct((N_DEV,) + b.shape, BF16),
                   jax.ShapeDtypeStruct((N_DEV,) + c.shape, BF16),
                   jax.ShapeDtypeStruct((4, CHUNK, 2 * CHUNK), F32),
                   jax.ShapeDtypeStruct(grp, MM), jax.ShapeDtypeStruct(grp, MM), jax.ShapeDtypeStruct(grp, F32),
                   jax.ShapeDtypeStruct((tmem, 2 * MEM_LEN), MM)),
        in_specs=[vm, vm, vm, pl.BlockSpec(memory_space=pltpu.SMEM), vm, vm, vm, vm, vm],
        out_specs=tuple([vm] * 8),
        scratch_shapes=[pltpu.SemaphoreType.DMA((3, 7)), pltpu.SemaphoreType.DMA((3, 7))],
        compiler_params=_params(),
    )(a, b, c, rel_bias, w_sp, b_sp, buckets, mem2, gm)


def _half_masks(rows):
    lane = lax.broadcasted_iota(jnp.int32, (rows, CHUNK), 1)
    return lane < 64


def _dup_heads(band):
    b32 = band.astype(F32)
    rolled = pltpu.roll(b32, 64, 1)
    lo = _half_masks(band.shape[0])
    return (jnp.where(lo, b32, rolled).astype(MM), jnp.where(lo, rolled, b32).astype(MM))


def _swa_probs(qk, bias_h, sink_h, first_add):
    s = qk * SCALE + bias_h + first_add
    m = jnp.maximum(jnp.max(s, axis=-1, keepdims=True), sink_h)
    p = jnp.exp(s - m)
    es = jnp.exp(sink_h - m)
    inv = 1.0 / (jnp.sum(p, axis=-1, keepdims=True) + es)
    return p * inv, es * inv


def _softmax(s):
    m = jnp.max(s, axis=-1, keepdims=True)
    p = jnp.exp(s - m)
    return p * (1.0 / jnp.sum(p, axis=-1, keepdims=True))


def _first_block_mask(n):
    col = lax.broadcasted_iota(jnp.int32, (2 * CHUNK, 2 * CHUNK), 1)
    return jnp.where((col < CHUNK) & (n == 0), NEG, 0.0)


def _stack_heads(x128, lo):
    return jnp.concatenate([jnp.where(lo, x128, 0.0), jnp.where(lo, 0.0, x128)], axis=0).astype(MM)


def _rms(xf):
    return lax.rsqrt(jnp.mean(xf * xf, axis=-1, keepdims=True) + EPS)


def _layer(x2, tgt2, mkv3, bias, sinks, vg, vb, wt, wtt, bcol, g1, g2, w_in_t, w_o, buckets, nb, s, tm):
    nt = s // tm
    bpt = tm // CHUNK
    bps = s // CHUNK
    t = nb * s
    last_step = nb * nt - 1

    def tile_at(step):
        return (step // nt) * nt + nt - 1 - step % nt

    def block_before(step):
        return (step // nt) * bps + jnp.maximum((nt - 1 - step % nt) * bpt - 1, 0)

    def body(x_ref, xp_ref, xn_ref, xpn_ref, t_ref, mkv_ref, bias_ref, sink_ref, vg_ref, vb_ref,
             wt_ref, wtt_ref, bcol_ref, g1_ref, g2_ref, wi_ref, wo_ref, bk_ref,
             gx_ref, dmkv_ref, dwi_hbm, dwo_hbm, dg1_ref, dg2_ref, loss_ref, dwsp_ref, dbs_ref,
             dvg_ref, dvb_ref, dsink_ref, drel_ref,
             acc_i, acc_o, uv_s, z_s, q_s, kv_s, h_s, hp_s, dp_s, dxo_s, dh_s, r_s,
             ycat, dyc, u_s, gu_s, gv_s, xh_s, vc_s, pb_s, ps_s, pc_s, kd_s, vd_s,
             dkv_acc, dbias_acc, dsv_acc, dsink_acc, sems):
        b, j = pl.program_id(0), pl.program_id(1)
        jt = nt - 1 - j
        step = b * nt + j
        g1v = g1_ref[...]
        NOW, NEXT, DONE = 0, 1, 2
        dw_cols = list(DW_PIECES)

        def weight_grad(n, slot):
            for c0, c1 in dw_cols[:n]:
                acc_i[c0:c1, :] += _dot_tn(dp_s[:, c0:c1], h_s[slot])
            del dw_cols[:n]

        def pre_norm(x_tile, x_before):
            xf = x_tile[...]
            r_s[NEXT] = _rms(xf)
            h_s[NEXT] = (xf * r_s[NEXT] * g1v).astype(MM)
            xp = x_before[...]
            hp_s[...] = (xp * _rms(xp) * g1v).astype(MM)

        def project_z():
            z_s[...] = _dot_nt(h_s[NEXT], wi_ref[Z_COL:IN_WIDTH, :])

        def project_uv():
            uv_s[...] = _dot_nt(h_s[NEXT], wi_ref[0:UV_W, :])

        @pl.when(step == 0)
        def _():
            for ref in (acc_i, acc_o, dg1_ref, dg2_ref, loss_ref, dwsp_ref, dvg_ref, dvb_ref,
                        dbias_acc, dsv_acc, dsink_acc):
                ref[...] = jnp.zeros_like(ref)
            dp_s[...] = jnp.zeros_like(dp_s)
            h_s[NOW] = jnp.zeros((tm, D_MODEL), MM)
            pre_norm(x_ref, xp_ref)
            project_z()
            project_uv()

        h_s[DONE] = h_s[NOW]
        r_s[NOW] = r_s[NEXT]
        h = h_s[NEXT]
        h_s[NOW] = h
        hp = hp_s[...]

        @pl.when(j == 0)
        def _():
            dmkv_ref[...] = jnp.zeros_like(dmkv_ref)
            dkv_acc[...] = jnp.zeros_like(dkv_acc)

        carry = dkv_acc[0:CHUNK, :]
        dkv_acc[...] = jnp.zeros_like(dkv_acc)
        dkv_acc[tm:tm + CHUNK, :] = carry

        lo = _half_masks(CHUNK)
        lob = _half_masks(2 * CHUNK)
        lot = _half_masks(tm)

        qkv = _dot_nt(h, wi_ref[SQ_COL:Z_COL, :])
        q_s[:, 0:256] = qkv[:, 0:256].astype(MM)
        q_s[:, 256:512] = qkv[:, 512:768].astype(MM)
        kv_s[CHUNK:CHUNK + tm, :] = qkv[:, 256:512].astype(MM)
        kv_s[0:CHUNK, :] = _dot_nt(hp, wi_ref[SK_COL:MQ_COL, :]).astype(MM)

        weight_grad(1, DONE)
        b_qk, b_pb = [], []
        blocks = [slice(blk * CHUNK, (blk + 1) * CHUNK) for blk in range(bpt)]
        for g in range(A_GROUPS):
            cg = slice(g * CHUNK, (g + 1) * CHUNK)
            us, gus, vcs = [], [], []
            for rows in blocks:
                u, gu = _gelu_and_grad(uv_s[rows, cg])
                v, gv = _gelu_and_grad(uv_s[rows, A_WIDTH + g * CHUNK:A_WIDTH + (g + 1) * CHUNK])
                mu = jnp.mean(v, axis=-1, keepdims=True)
                xc = v - mu
                rstd = lax.rsqrt(jnp.mean(xc * xc, axis=-1, keepdims=True) + EPS)
                xhat = xc * rstd
                vc = (xhat * vg_ref[:, cg] + vb_ref[:, cg]).astype(MM)
                u_s[rows, cg] = u
                gv_s[rows, cg] = rstd * gv
                xh_s[rows, cg] = xhat
                vc_s[rows, cg] = vc
                us.append(u)
                gus.append(gu)
                vcs.append(vc)
            sv_all = _dot(wt_ref[g], jnp.concatenate(vcs, axis=1))
            for blk, rows in enumerate(blocks):
                sv = sv_all[:, blk * CHUNK:(blk + 1) * CHUNK] + bcol_ref[g]
                gu_s[rows, cg] = sv * gus[blk]
                ycat[rows, cg] = us[blk] * sv
            if g % 2 == 1:
                weight_grad(1, DONE)
        for blk in range(bpt):
            r0 = blk * CHUNK
            rows = slice(r0, r0 + CHUNK)
            kd = _dup_heads(kv_s[r0:r0 + 2 * CHUNK, 0:CHUNK])
            vd = _dup_heads(kv_s[r0:r0 + 2 * CHUNK, CHUNK:2 * CHUNK])
            for kvh in range(2):
                kd_s[blk * 2 + kvh] = kd[kvh]
                vd_s[blk * 2 + kvh] = vd[kvh]
                q2 = _stack_heads(q_s[rows, kvh * CHUNK:(kvh + 1) * CHUNK].astype(F32), lo)
                b_qk.append(_dot_nt(q2, kd[kvh]))
        qks, pcs = [], []
        for g in range(2):
            q2 = _stack_heads(q_s[:, 256 + g * CHUNK:256 + (g + 1) * CHUNK].astype(F32), lot)
            qks.append(_dot_nt(q2, mkv_ref[:, g * CHUNK:(g + 1) * CHUNK]))
        top = lax.broadcasted_iota(jnp.int32, (2 * CHUNK, 1), 0) < CHUNK
        for blk in range(bpt):
            first_add = _first_block_mask(jt * bpt + blk)
            for kvh in range(2):
                sink2 = jnp.where(top, sink_ref[2 * kvh], sink_ref[2 * kvh + 1])
                probs, ps = _swa_probs(b_qk[blk * 2 + kvh], bias_ref[kvh], sink2, first_add)
                pb_s[blk * 2 + kvh] = probs
                ps_s[blk * 2 + kvh] = jnp.broadcast_to(ps, (2 * CHUNK, CHUNK))
                b_pb.append(probs.astype(MM))
        for g in range(2):
            probs = _softmax(qks[g] * SCALE)
            pc_s[g] = probs
            pcs.append(probs.astype(MM))
        for blk in range(bpt):
            rows = slice(blk * CHUNK, (blk + 1) * CHUNK)
            for kvh in range(2):
                out2 = _dot(b_pb[blk * 2 + kvh], vd_s[blk * 2 + kvh])
                ycat[rows, YB_OFF + kvh * CHUNK:YB_OFF + (kvh + 1) * CHUNK] = jnp.where(
                    lo, out2[0:CHUNK], out2[CHUNK:2 * CHUNK])
        for g in range(2):
            out2 = _dot(pcs[g], mkv_ref[:, MEM_LEN + g * CHUNK:MEM_LEN + (g + 1) * CHUNK])
            ycat[:, YC_OFF + g * CHUNK:YC_OFF + (g + 1) * CHUNK] = jnp.where(lot, out2[0:tm], out2[tm:2 * tm])

        zt = z_s[...]
        sig = 1.0 / (1.0 + jnp.exp(-zt))
        silu = zt * sig
        yc = ycat[...]
        yb = (yc * silu).astype(MM)
        pre_norm(xn_ref, xpn_ref)
        o = _dot(yb, wo_ref[...])
        project_z()
        r2 = _rms(o)
        nrm = o * r2
        g2v = g2_ref[...]
        e = x_ref[...] + nrm * g2v - t_ref[...]
        l1 = jnp.sum(e * e, axis=-1, keepdims=True)
        loss_ref[...] += jnp.broadcast_to(jnp.sum(l1, axis=0, keepdims=True) * (0.5 / D_MODEL), loss_ref.shape)
        dxo = e * (1.0 / D_MODEL)
        dxo_s[...] = dxo
        dg2_ref[...] += jnp.sum(dxo * nrm, axis=0, keepdims=True)
        dn = dxo * g2v
        do = r2 * (dn - nrm * jnp.mean(dn * nrm, axis=-1, keepdims=True))
        dob = do.astype(MM)
        dy = _dot_nt(dob, wo_ref[...])
        dp_s[:, Z_COL:IN_WIDTH] = (dy * yc * (sig * (1.0 + zt * (1.0 - sig)))).astype(MM)
        dyc[...] = dy * silu
        acc_o[...] += _dot_tn(yb, dob)

        def in_proj_bwd(c0, c1):
            part = _dot(dp_s[:, c0:c1], wi_ref[c0:c1, :])
            if c0 == Z_COL:
                dh_s[...] = part
            else:
                dh_s[...] += part

        in_proj_bwd(Z_COL, IN_WIDTH)

        for g in range(A_GROUPS):
            cg = slice(g * CHUNK, (g + 1) * CHUNK)
            cv = slice(A_WIDTH + g * CHUNK, A_WIDTH + (g + 1) * CHUNK)
            dsvbs = []
            for rows in blocks:
                dya = dyc[rows, cg]
                dp_s[rows, cg] = (dya * gu_s[rows, cg]).astype(MM)
                dsv = dya * u_s[rows, cg]
                dsv_acc[g] += dsv
                dsvbs.append(dsv.astype(MM))
            dsvb_all = jnp.concatenate(dsvbs, axis=1)
            dwsp_ref[g] += _dot_nt(dsvb_all, jnp.concatenate([vc_s[rows, cg] for rows in blocks], axis=1))
            dvc_all = _dot(wtt_ref[g], dsvb_all)
            for blk, rows in enumerate(blocks):
                dvc = dvc_all[:, blk * CHUNK:(blk + 1) * CHUNK]
                xhat = xh_s[rows, cg]
                dvg_ref[:, cg] += jnp.sum(dvc * xhat, axis=0, keepdims=True)
                dvb_ref[:, cg] += jnp.sum(dvc, axis=0, keepdims=True)
                dxh = dvc * vg_ref[:, cg]
                dv = (dxh - jnp.mean(dxh, axis=-1, keepdims=True)
                      - xhat * jnp.mean(dxh * xhat, axis=-1, keepdims=True))
                dp_s[rows, cv] = (dv * gv_s[rows, cg]).astype(MM)
        in_proj_bwd(0, UV_W)
        b_dosel, b_dp, b_dss = [], [], []
        for blk in range(bpt):
            rows = slice(blk * CHUNK, (blk + 1) * CHUNK)
            for kvh in range(2):
                b_dosel.append(_stack_heads(dyc[rows, YB_OFF + kvh * CHUNK:YB_OFF + (kvh + 1) * CHUNK], lo))
                b_dp.append(_dot_nt(b_dosel[-1], vd_s[blk * 2 + kvh]))
        dosels, dps, dsss = [], [], []
        for g in range(2):
            dosels.append(_stack_heads(dyc[:, YC_OFF + g * CHUNK:YC_OFF + (g + 1) * CHUNK], lot))
            dps.append(_dot_nt(dosels[g], mkv_ref[:, MEM_LEN + g * CHUNK:MEM_LEN + (g + 1) * CHUNK]))
        for blk in range(bpt):
            for kvh in range(2):
                probs = pb_s[blk * 2 + kvh]
                dp = b_dp[blk * 2 + kvh]
                delta = jnp.sum(probs * dp, axis=-1, keepdims=True)
                ds = probs * (dp - delta)
                dbias_acc[kvh] += ds
                sd = ps_s[blk * 2 + kvh][:, 0:1] * delta
                for gi in range(2):
                    hd = 2 * kvh + gi
                    dsink_acc[hd:hd + 1, :] += jnp.broadcast_to(
                        -jnp.sum(sd[gi * CHUNK:(gi + 1) * CHUNK], axis=0, keepdims=True), (1, CHUNK))
                b_dss.append((ds * SCALE).astype(MM))
        for g in range(2):
            probs = pc_s[g]
            ds = probs * (dps[g] - jnp.sum(probs * dps[g], axis=-1, keepdims=True))
            dsss.append((ds * SCALE).astype(MM))
        for blk in range(bpt):
            r0 = blk * CHUNK
            rows = slice(r0, r0 + CHUNK)
            dk_f, dv_f = [], []
            for kvh in range(2):
                dss = b_dss[blk * 2 + kvh]
                q2 = _stack_heads(q_s[rows, kvh * CHUNK:(kvh + 1) * CHUNK].astype(F32), lo)
                dq2 = _dot(dss, kd_s[blk * 2 + kvh])
                dkd = _dot_tn(dss, q2)
                dvd = _dot_tn(pb_s[blk * 2 + kvh].astype(MM), b_dosel[blk * 2 + kvh])
                dp_s[rows, SQ_COL + kvh * CHUNK:SQ_COL + (kvh + 1) * CHUNK] = jnp.where(
                    lo, dq2[0:CHUNK], dq2[CHUNK:2 * CHUNK]).astype(MM)
                dk_f.append(dkd + pltpu.roll(dkd, 64, 1))
                dv_f.append(dvd + pltpu.roll(dvd, 64, 1))
            dkv_acc[r0:r0 + 2 * CHUNK, 0:CHUNK] += jnp.where(lob, dk_f[0], dk_f[1])
            dkv_acc[r0:r0 + 2 * CHUNK, CHUNK:2 * CHUNK] += jnp.where(lob, dv_f[0], dv_f[1])
        dp_s[:, SK_COL:MQ_COL] = dkv_acc[CHUNK:CHUNK + tm, :].astype(MM)
        for g in range(2):
            q2 = _stack_heads(q_s[:, 256 + g * CHUNK:256 + (g + 1) * CHUNK].astype(F32), lot)
            dq2 = _dot(dsss[g], mkv_ref[:, g * CHUNK:(g + 1) * CHUNK])
            dp_s[:, MQ_COL + g * CHUNK:MQ_COL + (g + 1) * CHUNK] = jnp.where(lot, dq2[0:tm], dq2[tm:2 * tm]).astype(MM)
            dmkv_ref[:, g * CHUNK:(g + 1) * CHUNK] += _dot_tn(dsss[g], q2)
            dmkv_ref[:, MEM_LEN + g * CHUNK:MEM_LEN + (g + 1) * CHUNK] += _dot_tn(pc_s[g].astype(MM), dosels[g])

        in_proj_bwd(SQ_COL, Z_COL)
        project_uv()
        dh = dh_s[...]
        r = r_s[NOW]
        nx = x_ref[...] * r
        dg1_ref[...] += jnp.sum(dh * nx, axis=0, keepdims=True)
        dnx = dh * g1v
        gx_ref[...] = dxo_s[...] + r * (dnx - nx * jnp.mean(dnx * nx, axis=-1, keepdims=True))

        @pl.when(step == last_step)
        def _():
            dw_cols.extend(DW_PIECES)
            weight_grad(len(dw_cols), NOW)
            out_i = pltpu.make_async_copy(acc_i, dwi_hbm, sems.at[0])
            out_o = pltpu.make_async_copy(acc_o, dwo_hbm, sems.at[1])
            out_i.start()
            out_o.start()
            r_ = lax.broadcasted_iota(jnp.int32, (CHUNK, CHUNK), 0)
            c_ = lax.broadcasted_iota(jnp.int32, (CHUNK, CHUNK), 1)
            for g in range(A_GROUPS):
                dwsp_ref[g] = jnp.where(r_ >= c_, dwsp_ref[g], 0.0)
                dbs_ref[g:g + 1, :] = jnp.sum(dsv_acc[g].T, axis=0, keepdims=True)
            rows8 = lax.broadcasted_iota(jnp.int32, (8, CHUNK), 0)
            cols8 = lax.broadcasted_iota(jnp.int32, (8, CHUNK), 1)
            sk = jnp.zeros((8, CHUNK), F32)
            for hd in range(4):
                sk = sk + jnp.where((rows8 == 0) & (cols8 == hd),
                                    jnp.broadcast_to(dsink_acc[hd:hd + 1, :], (8, CHUNK)), 0.0)
            dsink_ref[...] = sk
            bk = bk_ref[...]
            valid = _window_valid()
            rrow = lax.broadcasted_iota(jnp.int32, (N_BUCKETS, CHUNK), 0)
            rcol = lax.broadcasted_iota(jnp.int32, (N_BUCKETS, CHUNK), 1)
            acc = jnp.zeros((N_BUCKETS, CHUNK), F32)
            for bb in range(N_BUCKETS):
                hit = (bk == bb) & valid
                for hd in range(4):
                    dbias = dbias_acc[hd // 2, (hd % 2) * CHUNK:(hd % 2 + 1) * CHUNK, :]
                    part = jnp.sum(jnp.where(hit, dbias, 0.0), axis=-1, keepdims=True)
                    tot = jnp.sum(part, axis=0, keepdims=True)
                    acc = acc + jnp.where((rrow == bb) & (rcol == hd), jnp.broadcast_to(tot, (N_BUCKETS, CHUNK)), 0.0)
            drel_ref[...] = acc
            out_i.wait()
            out_o.wait()

    after = lambda b, j: jnp.minimum(b * nt + j + 1, last_step)
    tile = pl.BlockSpec((tm, D_MODEL), lambda b, j: (tile_at(b * nt + j), 0))
    tile_after = pl.BlockSpec((tm, D_MODEL), lambda b, j: (tile_at(after(b, j)), 0))
    halo = pl.BlockSpec((CHUNK, D_MODEL), lambda b, j: (block_before(b * nt + j), 0))
    halo_after = pl.BlockSpec((CHUNK, D_MODEL), lambda b, j: (block_before(after(b, j)), 0))
    per_batch = lambda r, w: pl.BlockSpec((None, r, w), lambda b, j: (b, 0, 0))
    anyspec = pl.BlockSpec(memory_space=pl.ANY)
    grp = (A_GROUPS, CHUNK, CHUNK)
    return pl.pallas_call(
        body, name="layer", grid=(nb, nt),
        out_shape=(jax.ShapeDtypeStruct((t, D_MODEL), F32),
                   jax.ShapeDtypeStruct((nb, MEM_LEN, 2 * MEM_LEN), F32),
                   jax.ShapeDtypeStruct((IN_WIDTH, D_MODEL), F32),
                   jax.ShapeDtypeStruct((D_MODEL, D_MODEL), F32),
                   jax.ShapeDtypeStruct((1, D_MODEL), F32),
                   jax.ShapeDtypeStruct((1, D_MODEL), F32),
                   jax.ShapeDtypeStruct((8, CHUNK), F32),
                   jax.ShapeDtypeStruct(grp, F32),
                   jax.ShapeDtypeStruct((A_GROUPS, CHUNK), F32),
                   jax.ShapeDtypeStruct((1, A_WIDTH), F32),
                   jax.ShapeDtypeStruct((1, A_WIDTH), F32),
                   jax.ShapeDtypeStruct((8, CHUNK), F32),
                   jax.ShapeDtypeStruct((N_BUCKETS, CHUNK), F32)),
        in_specs=[tile, halo, tile_after, halo_after, tile, per_batch(MEM_LEN, 2 * MEM_LEN),
                  _full((2, 2 * CHUNK, 2 * CHUNK)),
                  pl.BlockSpec(memory_space=pltpu.SMEM),
                  _full((1, A_WIDTH)), _full((1, A_WIDTH)),
                  _full(grp), _full(grp), _full(grp),
                  _full((1, D_MODEL)), _full((1, D_MODEL)),
                  _full((IN_WIDTH, D_MODEL), single=True), _full((D_MODEL, D_MODEL), single=True),
                  _full((CHUNK, 2 * CHUNK))],
        out_specs=(tile, per_batch(MEM_LEN, 2 * MEM_LEN), anyspec, anyspec,
                   _full((1, D_MODEL)), _full((1, D_MODEL)), _full((8, CHUNK)),
                   _full(grp), _full((A_GROUPS, CHUNK)), _full((1, A_WIDTH)), _full((1, A_WIDTH)),
                   _full((8, CHUNK)), _full((N_BUCKETS, CHUNK))),
        scratch_shapes=[pltpu.VMEM((IN_WIDTH, D_MODEL), F32), pltpu.VMEM((D_MODEL, D_MODEL), F32),
                        pltpu.VMEM((tm, UV_W), F32), pltpu.VMEM((tm, Z_W), F32),
                        pltpu.VMEM((tm, 512), MM), pltpu.VMEM((tm + CHUNK, 2 * CHUNK), MM),
                        pltpu.VMEM((3, tm, D_MODEL), MM), pltpu.VMEM((CHUNK, D_MODEL), MM),
                        pltpu.VMEM((tm, IN_WIDTH), MM),
                        pltpu.VMEM((tm, D_MODEL), F32),
                        pltpu.VMEM((tm, D_MODEL), F32), pltpu.VMEM((2, tm, 1), F32),
                        pltpu.VMEM((tm, D_MODEL), F32), pltpu.VMEM((tm, D_MODEL), F32)]
                       + [pltpu.VMEM((tm, A_WIDTH), F32) for _ in range(4)]
                       + [pltpu.VMEM((tm, A_WIDTH), MM),
                          pltpu.VMEM((bpt * 2, 2 * CHUNK, 2 * CHUNK), F32),
                          pltpu.VMEM((bpt * 2, 2 * CHUNK, CHUNK), F32),
                          pltpu.VMEM((2, 2 * tm, MEM_LEN), F32),
                          pltpu.VMEM((bpt * 2, 2 * CHUNK, CHUNK), MM),
                          pltpu.VMEM((bpt * 2, 2 * CHUNK, CHUNK), MM),
                          pltpu.VMEM((tm + CHUNK, 2 * CHUNK), F32),
                          pltpu.VMEM((2, 2 * CHUNK, 2 * CHUNK), F32),
                          pltpu.VMEM(grp, F32),
                          pltpu.VMEM((8, CHUNK), F32),
                          pltpu.SemaphoreType.DMA((2,))],
        compiler_params=_params(dimension_semantics=("arbitrary", "arbitrary")),
    )(x2, x2, x2, x2, tgt2, mkv3, bias.reshape(2, 2 * CHUNK, 2 * CHUNK), sinks, vg, vb, wt, wtt, bcol, g1, g2, w_in_t, w_o, buckets)


class _ShardReduce:
    def __init__(self, pos, g, bufs, sems):
        self.x, self.y, self.c = pos
        self.g = g
        self.own, self.rcv, self.sbuf, self.rbuf, self.cbuf = bufs
        self.ld, self.sa, self.ra, self.sb, self.rb = sems
        self.nrow = g.shape[1]
        self.here = (self.x, self.y, self.c)
        self.sib = (self.x, self.y, 1 - self.c)
        self.first, self.second, self.far = _route(*pos)

    def _load(self, q):
        return pltpu.make_async_copy(self.g.at[2 * q + self.c], self.own.at[q], self.ld.at[q])

    def _to_sib(self, q, to):
        return _remote(self.g.at[2 * q + 1 - self.c], self.rcv.at[q], self.sa.at[q], self.ra.at[q], to)

    def _send(self, k, to):
        dst = self.cbuf.at[0] if k == 1 else self.rbuf.at[0 if k == 0 else 1]
        return _remote(self.sbuf.at[k], dst, self.sb.at[k], self.rb.at[k], to)

    def _stage(self, k, which, extra=None):
        def cast(r):
            v = self.rcv[which, r, :]
            if extra is not None:
                v = v + extra[0, r, :].astype(F32)
            self.sbuf[k, r, :] = v.astype(BF16)

        _rows_loop(self.nrow, cast)

    @staticmethod
    def _q(chip):
        return 2 * chip[0] + chip[1]

    def start(self):
        for q in range(4):
            self._load(q).start()
            self._to_sib(q, self.sib).start()

    def mid(self):
        for q in range(4):
            self._load(q).wait()
            self._to_sib(q, self.here).wait_recv()

        def add(r):
            for q in range(4):
                self.rcv[q, r, :] = self.rcv[q, r, :] + self.own[q, r, :]

        _rows_loop(self.nrow, add)
        to_first = (self.first[0], self.first[1], self.c)
        self._stage(0, self._q(self.first))
        self._send(0, to_first).start()
        self._stage(1, self._q(self.far))
        self._send(1, to_first).start()

    def pass_on(self):
        self._send(1, self.here).wait_recv()
        self._stage(2, self._q(self.second), extra=self.cbuf)
        self._send(2, (self.second[0], self.second[1], self.c)).start()

    def finish(self, out):
        self._send(0, self.here).wait_recv()
        self._send(2, self.here).wait_recv()
        which = 2 * self.x + self.y

        def tot(r):
            out[r, :] = (self.rcv[which, r, :] + self.rbuf[0, r, :].astype(F32)) + self.rbuf[1, r, :].astype(F32)

        _rows_loop(self.nrow, tot)
        for q in range(4):
            self._to_sib(q, self.sib).wait_send()
        to_first = (self.first[0], self.first[1], self.c)
        self._send(0, to_first).wait_send()
        self._send(1, to_first).wait_send()
        self._send(2, (self.second[0], self.second[1], self.c)).wait_send()


def _reduce_scratch(shape):
    return [pltpu.VMEM((4,) + shape, F32), pltpu.VMEM((4,) + shape, F32),
            pltpu.VMEM((3,) + shape, BF16), pltpu.VMEM((2,) + shape, BF16), pltpu.VMEM((1,) + shape, BF16),
            pltpu.SemaphoreType.DMA((4,)), pltpu.SemaphoreType.DMA((4,)), pltpu.SemaphoreType.DMA((4,)),
            pltpu.SemaphoreType.DMA((3,)), pltpu.SemaphoreType.DMA((3,))]


_N_RED = 10

_S_LAYOUT = (((1, D_MODEL), 0), ((1, D_MODEL), 8), ((1, D_MODEL), 16),
             ((1, A_WIDTH), 24), ((1, A_WIDTH), 28), ((A_GROUPS, CHUNK), 32),
             ((1, 4), 36), ((N_BUCKETS, 4), 40),
             ((A_GROUPS * CHUNK, CHUNK), 72))
_LOSS_ROW = 37
_W_SP_ROW = _S_LAYOUT[-1][1]
_S_ROWS = _W_SP_ROW + A_GROUPS * CHUNK
_N_SMALL = len(_S_LAYOUT)


def _pack_rows(dst, refs):
    for (shp, r0), ref in zip(_S_LAYOUT, refs):
        if shp[0] == 1 and shp[1] >= CHUNK:
            for i in range(shp[1] // CHUNK):
                dst[r0 + i:r0 + i + 1, :] = ref[:, i * CHUNK:(i + 1) * CHUNK]
        elif ref.shape[-1] == CHUNK:
            dst[r0:r0 + shp[0], :] = ref[0:shp[0], :]
        else:
            dst[r0:r0 + shp[0], 0:shp[1]] = ref[...]


def _unpack_rows(src, refs):
    for (shp, r0), ref in zip(_S_LAYOUT, refs):
        if shp[0] == 1 and shp[1] >= CHUNK:
            for i in range(shp[1] // CHUNK):
                ref[:, i * CHUNK:(i + 1) * CHUNK] = src[r0 + i:r0 + i + 1, :]
        elif shp[1] == CHUNK:
            ref[...] = src[r0:r0 + shp[0], :]
        else:
            if tuple(ref.shape) == (shp[1], shp[0]):
                ref[...] = src[r0:r0 + CHUNK, :].T[0:shp[1], 0:shp[0]]
            else:
                ref[...] = src[r0:r0 + shp[0], 0:shp[1]]


_MEM_G = 2


def _greduce(ga, gb, dmkv, mem2, gm, w_mkv, small_g, loss_p):
    shp_c = (SHARD_O, 2 * MEM_LEN)
    shapes = (shp_c, gb.shape[1:], ga.shape[1:])
    rs = _S_ROWS

    def body(*refs):
        it = iter(refs)
        take = lambda n: [next(it) for _ in range(n)]
        gb_ref, ga_ref, d_ref, m_ref, gm_ref, wm_ref = take(6)
        sg_refs = take(_N_SMALL - 1)
        loss_ref, = take(1)
        oc, ob, oa, ogs = take(4)
        red = take(3 * _N_RED)
        gs_ref, rs_a, rs_b, rs_w, gc_ref, dgm_ref = take(6)
        ssem_a, rsem_a, ssem_b, rsem_b = take(4)

        pos = _position()
        x, y, cc = pos
        myq = 2 * x + y
        here, sib = (x, y, cc), (x, y, 1 - cc)
        chips = _other_chips(x, y)
        reducers = [_ShardReduce(pos, g, red[k * _N_RED:k * _N_RED + 5], red[k * _N_RED + 5:(k + 1) * _N_RED])
                    for k, g in enumerate((gc_ref, gb_ref, ga_ref))]
        for rd in reducers[1:]:
            rd.start()

        xf = m_ref[...]
        nm = xf * _rms(xf)
        hm = (nm * gm_ref[...]).astype(MM)
        d = d_ref[...].astype(MM)
        for o in range(N_DEV):
            gc_ref[o] = _dot_tn(hm[:, o * SHARD_O:(o + 1) * SHARD_O], d)
        dgm_ref[...] = jnp.sum(_dot_nt(d, wm_ref[...]) * nm, axis=0, keepdims=True)
        reducers[0].start()

        gs_ref[...] = jnp.zeros_like(gs_ref)
        _pack_rows(gs_ref, sg_refs[:_MEM_G] + [dgm_ref] + sg_refs[_MEM_G:])
        gs_ref[_LOSS_ROW:_LOSS_ROW + 1, :] = loss_ref[0:1, :]
        small_a = _remote(gs_ref, rs_a, ssem_a, rsem_a, sib)
        small_a.start()

        _remote(gs_ref, rs_a, ssem_a, rsem_a, here).wait_recv()
        rs_b[myq] = gs_ref[0:_W_SP_ROW, :] + rs_a[0:_W_SP_ROW, :]
        rs_w[myq] = (gs_ref[_W_SP_ROW:rs, :] + rs_a[_W_SP_ROW:rs, :]).astype(BF16)
        small_b = []
        for j, chip in enumerate(chips):
            to = (chip[0], chip[1], cc)
            small_b.append(_remote(rs_b.at[myq], rs_b.at[myq], ssem_b.at[0, j], rsem_b.at[0, j], to))
            small_b.append(_remote(rs_w.at[myq], rs_w.at[myq], ssem_b.at[1, j], rsem_b.at[1, j], to))
        for cp in small_b:
            cp.start()
        late_last = reducers[1:] + reducers[:1]
        for rd in late_last:
            rd.mid()
        for rd in late_last:
            rd.pass_on()

        for j in range(3):
            _remote(rs_b.at[myq], rs_b.at[myq], ssem_b.at[0, j], rsem_b.at[0, j], here).wait_recv()
            _remote(rs_w.at[myq], rs_w.at[myq], ssem_b.at[1, j], rsem_b.at[1, j], here).wait_recv()
        ogs[0:_W_SP_ROW, :] = ((rs_b[0] + rs_b[1]) + rs_b[2]) + rs_b[3]

        def tot_w(r):
            w = [rs_w[q, r, :].astype(F32) for q in range(4)]
            ogs[pl.ds(pl.multiple_of(_W_SP_ROW + r.start, 8), _ROWS), :] = ((w[0] + w[1]) + w[2]) + w[3]

        _rows_loop(rs - _W_SP_ROW, tot_w)
        for rd, out in zip(late_last, (ob, oa, oc)):
            rd.finish(out)
        small_a.wait_send()
        for cp in small_b:
            cp.wait_send()

    vm = pl.BlockSpec(memory_space=pltpu.VMEM)
    anyspec = pl.BlockSpec(memory_space=pl.ANY)
    scratch = []
    for shp in shapes:
        scratch += _reduce_scratch(shp)
    scratch += [pltpu.VMEM((rs, CHUNK), F32), pltpu.VMEM((rs, CHUNK), F32),
                pltpu.VMEM((4, _W_SP_ROW, CHUNK), F32), pltpu.VMEM((4, rs - _W_SP_ROW, CHUNK), BF16),
                pltpu.VMEM((N_DEV,) + shp_c, F32), pltpu.VMEM((1, D_MODEL), F32),
                pltpu.SemaphoreType.DMA, pltpu.SemaphoreType.DMA,
                pltpu.SemaphoreType.DMA((2, 3)), pltpu.SemaphoreType.DMA((2, 3))]
    tc, tb, ta, ts = pl.pallas_call(
        body, name="greduce",
        out_shape=tuple([jax.ShapeDtypeStruct(shp, F32) for shp in shapes] + [jax.ShapeDtypeStruct((rs, CHUNK), F32)]),
        in_specs=[anyspec] * 2 + [vm] * (4 + _N_SMALL),
        out_specs=(vm, vm, vm, vm),
        scratch_shapes=scratch,
        compiler_params=_params(),
    )(gb, ga, dmkv, mem2, gm, w_mkv, *small_g, loss_p)
    return ta, tb, tc, ts


def _adamw(w, g, m, v):
    m = ADAM_B1 * m + (1.0 - ADAM_B1) * g
    v = ADAM_B2 * v + (1.0 - ADAM_B2) * (g * g)
    m_hat = m / (1.0 - ADAM_B1 ** ADAM_STEP)
    v_hat = v / (1.0 - ADAM_B2 ** ADAM_STEP)
    delta = -ADAM_LR * (m_hat / (jnp.sqrt(v_hat) + ADAM_EPS) + ADAM_WD * w)
    return delta, m, v


def _update(ta, tb, tc, ts, big_wmv, small_wmv):
    shapes = (ta.shape, tb.shape, tc.shape)
    rs = _S_ROWS
    small_shapes = [tuple(a.shape) for a in small_wmv[0]]

    def body(*refs):
        it = iter(refs)
        take = lambda n: [next(it) for _ in range(n)]
        ga_ref, gb_ref, gc_ref, gs_ref = take(4)
        wa, ma, va, wb, mb, vb_, wc, mc, vc = take(9)
        sw_refs, sm_refs, sv_refs = take(_N_SMALL), take(_N_SMALL), take(_N_SMALL)
        oga, oda, oma, ova, ogb, odb, omb, ovb, ogc, odc, omc, ovc = take(12)
        so_refs = [take(_N_SMALL) for _ in range(4)]
        loss_out, = take(1)
        ws, ms, vs, ods, oms, ovs = take(6)

        for buf in (ws, ms, vs):
            buf[...] = jnp.zeros_like(buf)
        _pack_rows(ws, sw_refs)
        _pack_rows(ms, sm_refs)
        _pack_rows(vs, sv_refs)

        big = ((ga_ref, wa, ma, va, oga, oda, oma, ova), (gb_ref, wb, mb, vb_, ogb, odb, omb, ovb),
               (gc_ref, wc, mc, vc, ogc, odc, omc, ovc))
        for arr in range(3):
            g_r, w_r, m_r, v_r, og, od, om, ov = big[arr]

            def upd(r, g_r=g_r, w_r=w_r, m_r=m_r, v_r=v_r, og=og, od=od, om=om, ov=ov):
                g = g_r[r, :]
                d, m, v = _adamw(w_r[r, :], g, m_r[r, :], v_r[r, :])
                og[r, :] = g
                od[r, :] = d
                om[r, :] = m
                ov[r, :] = v

            _rows_loop(shapes[arr][0], upd)

        def upd_s(i, _):
            r = pl.ds(pl.multiple_of(i * 8, 8), 8)
            d, m, v = _adamw(ws[r, :], gs_ref[r, :], ms[r, :], vs[r, :])
            ods[r, :] = d
            oms[r, :] = m
            ovs[r, :] = v
            return 0

        lax.fori_loop(0, rs // 8, upd_s, 0)
        for k, buf in enumerate((gs_ref, ods, oms, ovs)):
            _unpack_rows(buf, so_refs[k])
        loss_out[...] = gs_ref[_LOSS_ROW:_LOSS_ROW + 1, 0:1]

    vm = pl.BlockSpec(memory_space=pltpu.VMEM)
    big_out = []
    for shp in shapes:
        big_out += [jax.ShapeDtypeStruct(shp, F32)] * 4
    small_out = [jax.ShapeDtypeStruct(shp[::-1] if shp == (N_BUCKETS, 4) else shp, F32) for shp in small_shapes] * 4
    out_shape = tuple(big_out + small_out + [jax.ShapeDtypeStruct((1, 1), F32)])
    n_in = 4 + 9 + 3 * _N_SMALL
    return pl.pallas_call(
        body, name="update",
        out_shape=out_shape,
        in_specs=[vm] * n_in,
        out_specs=tuple([vm] * len(out_shape)),
        scratch_shapes=[pltpu.VMEM((rs, CHUNK), F32) for _ in range(6)],
        compiler_params=_params(),
    )(ta, tb, tc, ts, *big_wmv, *small_wmv[0], *small_wmv[1], *small_wmv[2])


def kernel(x, mem, pre_norm_g, post_norm_g, mem_norm_g, w_in, w_mem_kv, v_norm_g, v_norm_b, w_spatial, b_spatial, attn_sinks, rel_bias, w_out, loss_target, m_pre_norm_g, m_post_norm_g, m_mem_norm_g, m_w_in, m_w_mem_kv, m_v_norm_g, m_v_norm_b, m_w_spatial, m_b_spatial, m_attn_sinks, m_rel_bias, m_w_out, v_pre_norm_g, v_post_norm_g, v_mem_norm_g, v_w_in, v_w_mem_kv, v_v_norm_g, v_v_norm_b, v_w_spatial, v_b_spatial, v_attn_sinks, v_rel_bias, v_w_out):
    sh_a = (w_in[0].T, m_w_in[0].T, v_w_in[0].T)
    sh_b = (w_out[0], m_w_out[0], v_w_out[0])
    sh_c = (w_mem_kv[0], m_w_mem_kv[0], v_w_mem_kv[0])
    nb, s, _ = x.shape
    t = nb * s
    x2 = x.reshape(t, D_MODEL)
    tgt2 = loss_target.reshape(t, D_MODEL)
    mem2 = mem.reshape(nb * MEM_LEN, D_MODEL)
    buckets = jnp.asarray(_t5_buckets())

    wa, wb, wc, bias, wt, wtt, bcol, mkv = _wgather(sh_a[0], sh_b[0], sh_c[0], rel_bias, w_spatial[0], b_spatial[0],
                                                    buckets, mem2, mem_norm_g)
    w_mkv = wc.reshape(D_MODEL, 2 * MEM_LEN)
    gx, dmkv, dwi, dwo, dg1, dg2, loss_p, dwsp, dbs, dvg, dvb, dsink, drel = _layer(
        x2, tgt2, mkv.reshape(nb, MEM_LEN, 2 * MEM_LEN), bias, attn_sinks.reshape(4), v_norm_g, v_norm_b, wt, wtt, bcol,
        pre_norm_g, post_norm_g, wa.reshape(IN_WIDTH, D_MODEL), wb.reshape(D_MODEL, D_MODEL), buckets,
        nb, s, min(256, s))
    gx = gx.reshape(nb, s, D_MODEL)
    small_grads = [dg1, dg2, dvg, dvb, dbs, dsink, drel, dwsp.reshape(A_GROUPS * CHUNK, CHUNK)]

    small_names = ["pre_norm_g", "post_norm_g", "mem_norm_g", "v_norm_g", "v_norm_b", "b_spatial", "attn_sinks",
                   "rel_bias", "w_spatial"]
    given = dict(pre_norm_g=(pre_norm_g, m_pre_norm_g, v_pre_norm_g), post_norm_g=(post_norm_g, m_post_norm_g, v_post_norm_g),
                 mem_norm_g=(mem_norm_g, m_mem_norm_g, v_mem_norm_g), v_norm_g=(v_norm_g, m_v_norm_g, v_v_norm_g),
                 v_norm_b=(v_norm_b, m_v_norm_b, v_v_norm_b), b_spatial=(b_spatial, m_b_spatial, v_b_spatial),
                 attn_sinks=(attn_sinks, m_attn_sinks, v_attn_sinks), rel_bias=(rel_bias, m_rel_bias, v_rel_bias),
                 w_spatial=(w_spatial, m_w_spatial, v_w_spatial))
    small_wmv = [[given[n][k].reshape(shp) for n, (shp, _) in zip(small_names, _S_LAYOUT)] for k in range(3)]

    ta, tb, tc, ts = _greduce(dwi.reshape(N_DEV, SHARD_IN, D_MODEL), dwo.reshape(N_DEV, SHARD_O, D_MODEL),
                              dmkv.reshape(nb * MEM_LEN, 2 * MEM_LEN), mem2, mem_norm_g, w_mkv, small_grads, loss_p)
    outs = _update(ta, tb, tc, ts, (*sh_a, *sh_b, *sh_c), small_wmv)
    ra, rb, rc = outs[0:4], outs[4:8], outs[8:12]
    loss = outs[12 + 4 * _N_SMALL].reshape(())

    res = {}
    for k, kind in enumerate(("grad", "delta", "new_m", "new_v")):
        res[kind, "w_in"] = ra[k].T[None]
        res[kind, "w_out"] = rb[k][None]
        res[kind, "w_mem_kv"] = rc[k][None]
        for i, n in enumerate(small_names):
            o = outs[12 + k * _N_SMALL + i]
            res[kind, n] = o.T if n == "rel_bias" else o.reshape(given[n][0].shape)
    order = ["pre_norm_g", "post_norm_g", "mem_norm_g", "w_in", "w_mem_kv", "v_norm_g", "v_norm_b", "w_spatial",
             "b_spatial", "attn_sinks", "rel_bias", "w_out"]
    flat = [res[kind, n] for kind in ("grad", "delta", "new_m", "new_v") for n in order]
    return (loss, gx, *flat)
```

```python
import numpy as np
import jax
import jax.numpy as jnp
from jax import lax
from jax.experimental import pallas as pl
from jax.experimental.pallas import tpu as pltpu

F32 = jnp.float32
BF16 = jnp.bfloat16
MM = jnp.bfloat16

D_MODEL = 1024
CHUNK = 128
A_GROUPS = 4
A_WIDTH = 512
UV_W = 1024
QKV_W = 768
Z_W = 1024
IN_WIDTH = UV_W + QKV_W + Z_W
MEM_LEN = 256
N_BUCKETS = 32
MAX_DISTANCE = 128
EPS = 1e-6
NEG = -1e30
SCALE = 0.125
N_DEV = 8
SHARD_IN = IN_WIDTH // N_DEV
SHARD_O = D_MODEL // N_DEV
_A_PIECES = ((0, 128), (128, 128), (256, SHARD_IN - 256))

SQ_COL, SK_COL, SV_COL, MQ_COL, Z_COL = UV_W, UV_W + 256, UV_W + 384, UV_W + 512, UV_W + QKV_W
DW_PIECES = ((0, SQ_COL), (SQ_COL, Z_COL), (Z_COL, IN_WIDTH))
YB_OFF, YC_OFF = 512, 768

ADAM_LR = 0.001
ADAM_B1 = 0.9
ADAM_B2 = 0.999
ADAM_EPS = 1e-08
ADAM_WD = 0.01
ADAM_STEP = 10

VMEM_LIMIT = 60 * 1024 * 1024

_GELU_C = 0.7978845608028654
_GELU_A = 0.044715

MESH = pl.DeviceIdType.MESH
_ROWS = 32


def _dot(a, b):
    return lax.dot_general(a, b, (((1,), (0,)), ((), ())), preferred_element_type=F32)


def _dot_nt(a, b):
    return lax.dot_general(a, b, (((1,), (1,)), ((), ())), preferred_element_type=F32)


def _dot_tn(a, b):
    return lax.dot_general(a, b, (((0,), (0,)), ((), ())), preferred_element_type=F32)


def _gelu_and_grad(x):
    x2 = x * x
    t = jnp.tanh(_GELU_C * (x + _GELU_A * x * x2))
    g = 0.5 * x * (1.0 + t)
    dg = 0.5 * (1.0 + t) + 0.5 * x * (1.0 - t * t) * (_GELU_C * (1.0 + 3.0 * _GELU_A * x2))
    return g, dg


def _t5_buckets():
    qi = np.arange(CHUNK)[:, None]
    kj = np.arange(2 * CHUNK)[None, :]
    n = np.maximum(qi + CHUNK - kj, 0)
    max_exact = N_BUCKETS // 2
    large = max_exact + (np.log(np.maximum(n, 1) / max_exact) / np.log(MAX_DISTANCE / max_exact)
                         * (N_BUCKETS - max_exact)).astype(np.int32)
    large = np.minimum(large, N_BUCKETS - 1)
    return np.where(n < max_exact, n, large).astype(np.int32)


def _params(**kw):
    return pltpu.CompilerParams(vmem_limit_bytes=VMEM_LIMIT, **kw)


def _full(shape, single=False):
    nd = len(shape)
    if single:
        return pl.BlockSpec(shape, lambda *_: (0,) * nd, pipeline_mode=pl.Buffered(1))
    return pl.BlockSpec(shape, lambda *_: (0,) * nd)


def _window_valid():
    qi = lax.broadcasted_iota(jnp.int32, (CHUNK, 2 * CHUNK), 0)
    kj = lax.broadcasted_iota(jnp.int32, (CHUNK, 2 * CHUNK), 1)
    dist = qi + CHUNK - kj
    return (dist >= 0) & (dist < CHUNK)


def _position():
    return lax.axis_index("x"), lax.axis_index("y"), lax.axis_index("c")


def _other_chips(x, y):
    return [(1 - x, y), (x, 1 - y), (1 - x, 1 - y)]


def _route(x, y, c):
    first = (x * c + (1 - x) * (1 - c), y * (1 - c) + (1 - y) * c)
    second = (x * (1 - c) + (1 - x) * c, y * c + (1 - y) * (1 - c))
    return first, second, (1 - x, 1 - y)


def _remote(src, dst, ssem, rsem, to):
    return pltpu.make_async_remote_copy(src_ref=src, dst_ref=dst, send_sem=ssem, recv_sem=rsem,
                                        device_id=to, device_id_type=MESH)


def _rows_loop(nrow, fn):
    def step(i, _):
        fn(pl.ds(pl.multiple_of(i * _ROWS, _ROWS), _ROWS))
        return 0

    lax.fori_loop(0, nrow // _ROWS, step, 0)


class _Gather:
    def __init__(self, pos, out, ssem, rsem, rows=None):
        self.x, self.y, self.c = pos
        self.out, self.ssem, self.rsem, self.rows = out, ssem, rsem, rows
        self.me = 4 * self.x + 2 * self.y + self.c
        self.here = (self.x, self.y, self.c)
        self.sib = (self.x, self.y, 1 - self.c)
        self.first, self.second, self.far = _route(*pos)

    def _copy(self, k, blk, to):
        r = self.out.at[blk] if self.rows is None else self.out.at[blk, self.rows]
        return _remote(r, r, self.ssem.at[k], self.rsem.at[k], to)

    def _idx(self, chip, core):
        return 4 * chip[0] + 2 * chip[1] + core

    def _on(self, chip):
        return (chip[0], chip[1], self.c)

    def start(self):
        self._copy(0, self.me, self.sib).start()
        self._copy(1, self.me, self._on(self.first)).start()
        self._copy(2, self.me, self._on(self.second)).start()

    def forward(self):
        c = self.c
        self._copy(1, self._idx(self.first, c), self.here).wait_recv()
        self._copy(3, self._idx(self.first, c), self._on(self.second)).start()
        self._copy(4, self._idx(self.first, c), self.sib).start()
        self._copy(2, self._idx(self.second, c), self.here).wait_recv()
        self._copy(5, self._idx(self.second, c), self.sib).start()
        self._copy(3, self._idx(self.far, c), self.here).wait_recv()
        self._copy(6, self._idx(self.far, c), self.sib).start()

    def finish(self):
        c = self.c
        self._copy(0, self._idx((self.x, self.y), 1 - c), self.here).wait_recv()
        for k, chip in ((4, self.second), (5, self.first), (6, self.far)):
            self._copy(k, self._idx(chip, 1 - c), self.here).wait_recv()
        self._copy(0, self.me, self.sib).wait_send()
        self._copy(1, self.me, self._on(self.first)).wait_send()
        self._copy(2, self.me, self._on(self.second)).wait_send()
        self._copy(3, self._idx(self.first, c), self._on(self.second)).wait_send()
        for k, chip in ((4, self.first), (5, self.second), (6, self.far)):
            self._copy(k, self._idx(chip, c), self.sib).wait_send()


def _prep_tables(rb_ref, w_ref, b_ref, bk_ref, bias_ref, wt_ref, wtt_ref, bcol_ref):
    valid = _window_valid()
    bk = bk_ref[...]
    acc = [jnp.full((CHUNK, 2 * CHUNK), NEG, F32) for _ in range(4)]
    for b in range(N_BUCKETS):
        hit = (bk == b) & valid
        for h in range(4):
            acc[h] = jnp.where(hit, rb_ref[b, h], acc[h])
    for h in range(4):
        bias_ref[h] = acc[h]
    r = lax.broadcasted_iota(jnp.int32, (CHUNK, CHUNK), 0)
    c = lax.broadcasted_iota(jnp.int32, (CHUNK, CHUNK), 1)
    for g in range(A_GROUPS):
        w = jnp.where(r >= c, w_ref[g], 0.0)
        wt_ref[g] = w.astype(MM)
        wtt_ref[g] = w.T.astype(MM)
        bcol_ref[g] = jnp.broadcast_to(b_ref[g:g + 1, :], (CHUNK, CHUNK)).T


def _wgather(a, b, c, rel_bias, w_sp, b_sp, buckets, mem2, gm):
    tmem = mem2.shape[0]

    def body(a_ref, b_ref, c_ref, rb_ref, w_ref, bsp_ref, bk_ref, m_ref, gm_ref,
             oa, ob, oc, bias_ref, wt_ref, wtt_ref, bcol_ref, mkv_ref, ssem, rsem):
        pos = _position()
        me = 4 * pos[0] + 2 * pos[1] + pos[2]
        gathers = []
        jobs = [(c_ref, oc, None), (b_ref, ob, None)] + [(a_ref, oa, pl.ds(r0, n)) for r0, n in _A_PIECES]
        for k, (src, out, rows) in enumerate(jobs):
            if rows is None:
                out[me] = src[...].astype(BF16)
            else:
                out[me, rows] = src[rows, :].astype(BF16)
            g = _Gather(pos, out, ssem.at[k], rsem.at[k], rows)
            g.start()
            gathers.append(g)
        _prep_tables(rb_ref, w_ref, bsp_ref, bk_ref, bias_ref, wt_ref, wtt_ref, bcol_ref)
        for g in gathers:
            g.forward()
        gathers[0].finish()
        xf = m_ref[...]
        hm = (xf * _rms(xf) * gm_ref[...]).astype(MM)
        acc = jnp.zeros((tmem, 2 * MEM_LEN), F32)
        for d in range(N_DEV):
            acc = acc + _dot(hm[:, d * SHARD_O:(d + 1) * SHARD_O], oc[d])
        mkv_ref[...] = acc.astype(MM)
        for g in gathers[1:]:
            g.finish()

    vm = pl.BlockSpec(memory_space=pltpu.VMEM)
    grp = (A_GROUPS, CHUNK, CHUNK)
    return pl.pallas_call(
        body, name="wgather",
        out_shape=(jax.ShapeDtypeStruct((N_DEV,) + a.shape, BF16),
                   jax.ShapeDtypeStruct((N_DEV,) + b.shape, BF16),
                   jax.ShapeDtypeStruct((N_DEV,) + c.shape, BF16),
                   jax.ShapeDtypeStruct((4, CHUNK, 2 * CHUNK), F32),
                   jax.ShapeDtypeStruct(grp, MM), jax.ShapeDtypeStruct(grp, MM), jax.ShapeDtypeStruct(grp, F32),
                   jax.ShapeDtypeStruct((tmem, 2 * MEM_LEN), MM)),
        in_specs=[vm, vm, vm, pl.BlockSpec(memory_space=pltpu.SMEM), vm, vm, vm, vm, vm],
        out_specs=tuple([vm] * 8),
        scratch_shapes=[pltpu.SemaphoreType.DMA((2 + len(_A_PIECES), 7)),
                        pltpu.SemaphoreType.DMA((2 + len(_A_PIECES), 7))],
        compiler_params=_params(),
    )(a, b, c, rel_bias, w_sp, b_sp, buckets, mem2, gm)


def _half_masks(rows):
    lane = lax.broadcasted_iota(jnp.int32, (rows, CHUNK), 1)
    return lane < 64


def _dup_heads(band):
    b32 = band.astype(F32)
    rolled = pltpu.roll(b32, 64, 1)
    lo = _half_masks(band.shape[0])
    return (jnp.where(lo, b32, rolled).astype(MM), jnp.where(lo, rolled, b32).astype(MM))


def _swa_probs(qk, bias_h, sink_h, first_add):
    s = qk * SCALE + bias_h + first_add
    m = jnp.maximum(jnp.max(s, axis=-1, keepdims=True), sink_h)
    p = jnp.exp(s - m)
    es = jnp.exp(sink_h - m)
    inv = 1.0 / (jnp.sum(p, axis=-1, keepdims=True) + es)
    return p * inv, es * inv


def _softmax(s):
    m = jnp.max(s, axis=-1, keepdims=True)
    p = jnp.exp(s - m)
    return p * (1.0 / jnp.sum(p, axis=-1, keepdims=True))


def _first_block_mask(n):
    col = lax.broadcasted_iota(jnp.int32, (2 * CHUNK, 2 * CHUNK), 1)
    return jnp.where((col < CHUNK) & (n == 0), NEG, 0.0)


def _stack_heads(x128, lo):
    return jnp.concatenate([jnp.where(lo, x128, 0.0), jnp.where(lo, 0.0, x128)], axis=0).astype(MM)


def _rms(xf):
    return lax.rsqrt(jnp.mean(xf * xf, axis=-1, keepdims=True) + EPS)


def _layer(x2, tgt2, mkv3, bias, sinks, vg, vb, wt, wtt, bcol, g1, g2, w_in_t, w_o, buckets, nb, s, tm):
    nt = s // tm
    bpt = tm // CHUNK
    bps = s // CHUNK
    t = nb * s
    last_step = nb * nt - 1

    def tile_at(step):
        return (step // nt) * nt + nt - 1 - step % nt

    def block_before(step):
        return (step // nt) * bps + jnp.maximum((nt - 1 - step % nt) * bpt - 1, 0)

    def body(x_ref, xp_ref, xn_ref, xpn_ref, t_ref, mkv_ref, bias_ref, sink_ref, vg_ref, vb_ref,
             wt_ref, wtt_ref, bcol_ref, g1_ref, g2_ref, wi_ref, wo_ref, bk_ref,
             gx_ref, dmkv_ref, dwi_hbm, dwo_hbm, dg1_ref, dg2_ref, loss_ref, dwsp_ref, dbs_ref,
             dvg_ref, dvb_ref, dsink_ref, drel_ref,
             acc_i, acc_o, uv_s, z_s, q_s, kv_s, h_s, hp_s, dp_s, dxo_s, dh_s, r_s,
             ycat, dyc, u_s, gu_s, gv_s, xh_s, vc_s, pb_s, ps_s, pc_s, kd_s, vd_s,
             dkv_acc, dbias_acc, dsv_acc, dsink_acc, sems):
        b, j = pl.program_id(0), pl.program_id(1)
        jt = nt - 1 - j
        step = b * nt + j
        g1v = g1_ref[...]
        NOW, NEXT, DONE = 0, 1, 2
        dw_cols = list(DW_PIECES)

        def weight_grad(n, slot):
            for c0, c1 in dw_cols[:n]:
                acc_i[c0:c1, :] += _dot_tn(dp_s[:, c0:c1], h_s[slot])
            del dw_cols[:n]

        def pre_norm(x_tile, x_before):
            xf = x_tile[...]
            r_s[NEXT] = _rms(xf)
            h_s[NEXT] = (xf * r_s[NEXT] * g1v).astype(MM)
            xp = x_before[...]
            hp_s[...] = (xp * _rms(xp) * g1v).astype(MM)

        def project_z():
            z_s[...] = _dot_nt(h_s[NEXT], wi_ref[Z_COL:IN_WIDTH, :])

        def project_uv():
            uv_s[...] = _dot_nt(h_s[NEXT], wi_ref[0:UV_W, :])

        @pl.when(step == 0)
        def _():
            for ref in (acc_i, acc_o, dg1_ref, dg2_ref, loss_ref, dwsp_ref, dvg_ref, dvb_ref,
                        dbias_acc, dsv_acc, dsink_acc):
                ref[...] = jnp.zeros_like(ref)
            dp_s[...] = jnp.zeros_like(dp_s)
            h_s[NOW] = jnp.zeros((tm, D_MODEL), MM)
            pre_norm(x_ref, xp_ref)
            project_z()
            project_uv()

        h_s[DONE] = h_s[NOW]
        r_s[NOW] = r_s[NEXT]
        h = h_s[NEXT]
        h_s[NOW] = h
        hp = hp_s[...]

        @pl.when(j == 0)
        def _():
            dmkv_ref[...] = jnp.zeros_like(dmkv_ref)
            dkv_acc[...] = jnp.zeros_like(dkv_acc)

        carry = dkv_acc[0:CHUNK, :]
        dkv_acc[...] = jnp.zeros_like(dkv_acc)
        dkv_acc[tm:tm + CHUNK, :] = carry

        lo = _half_masks(CHUNK)
        lob = _half_masks(2 * CHUNK)
        lot = _half_masks(tm)

        qkv = _dot_nt(h, wi_ref[SQ_COL:Z_COL, :])
        q_s[:, 0:256] = qkv[:, 0:256].astype(MM)
        q_s[:, 256:512] = qkv[:, 512:768].astype(MM)
        kv_s[CHUNK:CHUNK + tm, :] = qkv[:, 256:512].astype(MM)
        kv_s[0:CHUNK, :] = _dot_nt(hp, wi_ref[SK_COL:MQ_COL, :]).astype(MM)

        weight_grad(1, DONE)
        b_qk, b_pb = [], []
        for blk in range(bpt):
            r0 = blk * CHUNK
            rows = slice(r0, r0 + CHUNK)
            for g in range(A_GROUPS):
                cg = slice(g * CHUNK, (g + 1) * CHUNK)
                u, gu = _gelu_and_grad(uv_s[rows, cg])
                v, gv = _gelu_and_grad(uv_s[rows, A_WIDTH + g * CHUNK:A_WIDTH + (g + 1) * CHUNK])
                mu = jnp.mean(v, axis=-1, keepdims=True)
                xc = v - mu
                rstd = lax.rsqrt(jnp.mean(xc * xc, axis=-1, keepdims=True) + EPS)
                xhat = xc * rstd
                vc = (xhat * vg_ref[:, cg] + vb_ref[:, cg]).astype(MM)
                sv = _dot(wt_ref[g], vc) + bcol_ref[g]
                u_s[rows, cg] = u
                gu_s[rows, cg] = sv * gu
                gv_s[rows, cg] = rstd * gv
                xh_s[rows, cg] = xhat
                vc_s[rows, cg] = vc
                ycat[rows, cg] = u * sv
            weight_grad(1, DONE)
            kd = _dup_heads(kv_s[r0:r0 + 2 * CHUNK, 0:CHUNK])
            vd = _dup_heads(kv_s[r0:r0 + 2 * CHUNK, CHUNK:2 * CHUNK])
            for kvh in range(2):
                kd_s[blk * 2 + kvh] = kd[kvh]
                vd_s[blk * 2 + kvh] = vd[kvh]
                q2 = _stack_heads(q_s[rows, kvh * CHUNK:(kvh + 1) * CHUNK].astype(F32), lo)
                b_qk.append(_dot_nt(q2, kd[kvh]))
        qks, pcs = [], []
        for g in range(2):
            q2 = _stack_heads(q_s[:, 256 + g * CHUNK:256 + (g + 1) * CHUNK].astype(F32), lot)
            qks.append(_dot_nt(q2, mkv_ref[:, g * CHUNK:(g + 1) * CHUNK]))
        top = lax.broadcasted_iota(jnp.int32, (2 * CHUNK, 1), 0) < CHUNK
        for blk in range(bpt):
            first_add = _first_block_mask(jt * bpt + blk)
            for kvh in range(2):
                sink2 = jnp.where(top, sink_ref[2 * kvh], sink_ref[2 * kvh + 1])
                probs, ps = _swa_probs(b_qk[blk * 2 + kvh], bias_ref[kvh], sink2, first_add)
                pb_s[blk * 2 + kvh] = probs
                ps_s[blk * 2 + kvh] = jnp.broadcast_to(ps, (2 * CHUNK, CHUNK))
                b_pb.append(probs.astype(MM))
        for g in range(2):
            probs = _softmax(qks[g] * SCALE)
            pc_s[g] = probs
            pcs.append(probs.astype(MM))
        for blk in range(bpt):
            rows = slice(blk * CHUNK, (blk + 1) * CHUNK)
            for kvh in range(2):
                out2 = _dot(b_pb[blk * 2 + kvh], vd_s[blk * 2 + kvh])
                ycat[rows, YB_OFF + kvh * CHUNK:YB_OFF + (kvh + 1) * CHUNK] = jnp.where(
                    lo, out2[0:CHUNK], out2[CHUNK:2 * CHUNK])
        for g in range(2):
            out2 = _dot(pcs[g], mkv_ref[:, MEM_LEN + g * CHUNK:MEM_LEN + (g + 1) * CHUNK])
            ycat[:, YC_OFF + g * CHUNK:YC_OFF + (g + 1) * CHUNK] = jnp.where(lot, out2[0:tm], out2[tm:2 * tm])

        zt = z_s[...]
        sig = 1.0 / (1.0 + jnp.exp(-zt))
        silu = zt * sig
        yc = ycat[...]
        yb = (yc * silu).astype(MM)
        pre_norm(xn_ref, xpn_ref)
        o = _dot(yb, wo_ref[...])
        project_z()
        r2 = _rms(o)
        nrm = o * r2
        g2v = g2_ref[...]
        e = x_ref[...] + nrm * g2v - t_ref[...]
        l1 = jnp.sum(e * e, axis=-1, keepdims=True)
        loss_ref[...] += jnp.broadcast_to(jnp.sum(l1, axis=0, keepdims=True) * (0.5 / D_MODEL), loss_ref.shape)
        dxo = e * (1.0 / D_MODEL)
        dxo_s[...] = dxo
        dg2_ref[...] += jnp.sum(dxo * nrm, axis=0, keepdims=True)
        dn = dxo * g2v
        do = r2 * (dn - nrm * jnp.mean(dn * nrm, axis=-1, keepdims=True))
        dob = do.astype(MM)
        dy = _dot_nt(dob, wo_ref[...])
        dp_s[:, Z_COL:IN_WIDTH] = (dy * yc * (sig * (1.0 + zt * (1.0 - sig)))).astype(MM)
        dyc[...] = dy * silu
        acc_o[...] += _dot_tn(yb, dob)

        def in_proj_bwd(c0, c1):
            part = _dot(dp_s[:, c0:c1], wi_ref[c0:c1, :])
            if c0 == Z_COL:
                dh_s[...] = part
            else:
                dh_s[...] += part

        in_proj_bwd(Z_COL, IN_WIDTH)

        for blk in range(bpt):
            r0 = blk * CHUNK
            rows = slice(r0, r0 + CHUNK)
            for g in range(A_GROUPS):
                cg = slice(g * CHUNK, (g + 1) * CHUNK)
                cv = slice(A_WIDTH + g * CHUNK, A_WIDTH + (g + 1) * CHUNK)
                dya = dyc[rows, cg]
                dp_s[rows, cg] = (dya * gu_s[rows, cg]).astype(MM)
                dsv = dya * u_s[rows, cg]
                dsvb = dsv.astype(MM)
                dsv_acc[g] += dsv
                dwsp_ref[g] += _dot_nt(dsvb, vc_s[rows, cg])
                dvc = _dot(wtt_ref[g], dsvb)
                xhat = xh_s[rows, cg]
                dvg_ref[:, cg] += jnp.sum(dvc * xhat, axis=0, keepdims=True)
                dvb_ref[:, cg] += jnp.sum(dvc, axis=0, keepdims=True)
                dxh = dvc * vg_ref[:, cg]
                dv = (dxh - jnp.mean(dxh, axis=-1, keepdims=True)
                      - xhat * jnp.mean(dxh * xhat, axis=-1, keepdims=True))
                dp_s[rows, cv] = (dv * gv_s[rows, cg]).astype(MM)
        in_proj_bwd(0, UV_W)
        b_dosel, b_dp, b_dss = [], [], []
        for blk in range(bpt):
            rows = slice(blk * CHUNK, (blk + 1) * CHUNK)
            for kvh in range(2):
                b_dosel.append(_stack_heads(dyc[rows, YB_OFF + kvh * CHUNK:YB_OFF + (kvh + 1) * CHUNK], lo))
                b_dp.append(_dot_nt(b_dosel[-1], vd_s[blk * 2 + kvh]))
        dosels, dps, dsss = [], [], []
        for g in range(2):
            dosels.append(_stack_heads(dyc[:, YC_OFF + g * CHUNK:YC_OFF + (g + 1) * CHUNK], lot))
            dps.append(_dot_nt(dosels[g], mkv_ref[:, MEM_LEN + g * CHUNK:MEM_LEN + (g + 1) * CHUNK]))
        for blk in range(bpt):
            for kvh in range(2):
                probs = pb_s[blk * 2 + kvh]
                dp = b_dp[blk * 2 + kvh]
                delta = jnp.sum(probs * dp, axis=-1, keepdims=True)
                ds = probs * (dp - delta)
                dbias_acc[kvh] += ds
                sd = ps_s[blk * 2 + kvh][:, 0:1] * delta
                for gi in range(2):
                    hd = 2 * kvh + gi
                    dsink_acc[hd:hd + 1, :] += jnp.broadcast_to(
                        -jnp.sum(sd[gi * CHUNK:(gi + 1) * CHUNK], axis=0, keepdims=True), (1, CHUNK))
                b_dss.append((ds * SCALE).astype(MM))
        for g in range(2):
            probs = pc_s[g]
            ds = probs * (dps[g] - jnp.sum(probs * dps[g], axis=-1, keepdims=True))
            dsss.append((ds * SCALE).astype(MM))
        for blk in range(bpt):
            r0 = blk * CHUNK
            rows = slice(r0, r0 + CHUNK)
            dk_f, dv_f = [], []
            for kvh in range(2):
                dss = b_dss[blk * 2 + kvh]
                q2 = _stack_heads(q_s[rows, kvh * CHUNK:(kvh + 1) * CHUNK].astype(F32), lo)
                dq2 = _dot(dss, kd_s[blk * 2 + kvh])
                dkd = _dot_tn(dss, q2)
                dvd = _dot_tn(pb_s[blk * 2 + kvh].astype(MM), b_dosel[blk * 2 + kvh])
                dp_s[rows, SQ_COL + kvh * CHUNK:SQ_COL + (kvh + 1) * CHUNK] = jnp.where(
                    lo, dq2[0:CHUNK], dq2[CHUNK:2 * CHUNK]).astype(MM)
                dk_f.append(dkd + pltpu.roll(dkd, 64, 1))
                dv_f.append(dvd + pltpu.roll(dvd, 64, 1))
            dkv_acc[r0:r0 + 2 * CHUNK, 0:CHUNK] += jnp.where(lob, dk_f[0], dk_f[1])
            dkv_acc[r0:r0 + 2 * CHUNK, CHUNK:2 * CHUNK] += jnp.where(lob, dv_f[0], dv_f[1])
        dp_s[:, SK_COL:MQ_COL] = dkv_acc[CHUNK:CHUNK + tm, :].astype(MM)
        for g in range(2):
            q2 = _stack_heads(q_s[:, 256 + g * CHUNK:256 + (g + 1) * CHUNK].astype(F32), lot)
            dq2 = _dot(dsss[g], mkv_ref[:, g * CHUNK:(g + 1) * CHUNK])
            dp_s[:, MQ_COL + g * CHUNK:MQ_COL + (g + 1) * CHUNK] = jnp.where(lot, dq2[0:tm], dq2[tm:2 * tm]).astype(MM)
            dmkv_ref[:, g * CHUNK:(g + 1) * CHUNK] += _dot_tn(dsss[g], q2)
            dmkv_ref[:, MEM_LEN + g * CHUNK:MEM_LEN + (g + 1) * CHUNK] += _dot_tn(pc_s[g].astype(MM), dosels[g])

        in_proj_bwd(SQ_COL, Z_COL)
        project_uv()
        dh = dh_s[...]
        r = r_s[NOW]
        nx = x_ref[...] * r
        dg1_ref[...] += jnp.sum(dh * nx, axis=0, keepdims=True)
        dnx = dh * g1v
        gx_ref[...] = dxo_s[...] + r * (dnx - nx * jnp.mean(dnx * nx, axis=-1, keepdims=True))

        @pl.when(step == last_step)
        def _():
            dw_cols.extend(DW_PIECES)
            weight_grad(len(dw_cols), NOW)
            out_i = pltpu.make_async_copy(acc_i, dwi_hbm, sems.at[0])
            out_o = pltpu.make_async_copy(acc_o, dwo_hbm, sems.at[1])
            out_i.start()
            out_o.start()
            r_ = lax.broadcasted_iota(jnp.int32, (CHUNK, CHUNK), 0)
            c_ = lax.broadcasted_iota(jnp.int32, (CHUNK, CHUNK), 1)
            for g in range(A_GROUPS):
                dwsp_ref[g] = jnp.where(r_ >= c_, dwsp_ref[g], 0.0)
                dbs_ref[g:g + 1, :] = jnp.sum(dsv_acc[g].T, axis=0, keepdims=True)
            rows8 = lax.broadcasted_iota(jnp.int32, (8, CHUNK), 0)
            cols8 = lax.broadcasted_iota(jnp.int32, (8, CHUNK), 1)
            sk = jnp.zeros((8, CHUNK), F32)
            for hd in range(4):
                sk = sk + jnp.where((rows8 == 0) & (cols8 == hd),
                                    jnp.broadcast_to(dsink_acc[hd:hd + 1, :], (8, CHUNK)), 0.0)
            dsink_ref[...] = sk
            bk = bk_ref[...]
            valid = _window_valid()
            rrow = lax.broadcasted_iota(jnp.int32, (N_BUCKETS, CHUNK), 0)
            rcol = lax.broadcasted_iota(jnp.int32, (N_BUCKETS, CHUNK), 1)
            acc = jnp.zeros((N_BUCKETS, CHUNK), F32)
            for bb in range(N_BUCKETS):
                hit = (bk == bb) & valid
                for hd in range(4):
                    dbias = dbias_acc[hd // 2, (hd % 2) * CHUNK:(hd % 2 + 1) * CHUNK, :]
                    part = jnp.sum(jnp.where(hit, dbias, 0.0), axis=-1, keepdims=True)
                    tot = jnp.sum(part, axis=0, keepdims=True)
                    acc = acc + jnp.where((rrow == bb) & (rcol == hd), jnp.broadcast_to(tot, (N_BUCKETS, CHUNK)), 0.0)
            drel_ref[...] = acc
            out_i.wait()
            out_o.wait()

    after = lambda b, j: jnp.minimum(b * nt + j + 1, last_step)
    tile = pl.BlockSpec((tm, D_MODEL), lambda b, j: (tile_at(b * nt + j), 0))
    tile_after = pl.BlockSpec((tm, D_MODEL), lambda b, j: (tile_at(after(b, j)), 0))
    halo = pl.BlockSpec((CHUNK, D_MODEL), lambda b, j: (block_before(b * nt + j), 0))
    halo_after = pl.BlockSpec((CHUNK, D_MODEL), lambda b, j: (block_before(after(b, j)), 0))
    per_batch = lambda r, w: pl.BlockSpec((None, r, w), lambda b, j: (b, 0, 0))
    anyspec = pl.BlockSpec(memory_space=pl.ANY)
    grp = (A_GROUPS, CHUNK, CHUNK)
    return pl.pallas_call(
        body, name="layer", grid=(nb, nt),
        out_shape=(jax.ShapeDtypeStruct((t, D_MODEL), F32),
                   jax.ShapeDtypeStruct((nb, MEM_LEN, 2 * MEM_LEN), F32),
                   jax.ShapeDtypeStruct((IN_WIDTH, D_MODEL), F32),
                   jax.ShapeDtypeStruct((D_MODEL, D_MODEL), F32),
                   jax.ShapeDtypeStruct((1, D_MODEL), F32),
                   jax.ShapeDtypeStruct((1, D_MODEL), F32),
                   jax.ShapeDtypeStruct((8, CHUNK), F32),
                   jax.ShapeDtypeStruct(grp, F32),
                   jax.ShapeDtypeStruct((A_GROUPS, CHUNK), F32),
                   jax.ShapeDtypeStruct((1, A_WIDTH), F32),
                   jax.ShapeDtypeStruct((1, A_WIDTH), F32),
                   jax.ShapeDtypeStruct((8, CHUNK), F32),
                   jax.ShapeDtypeStruct((N_BUCKETS, CHUNK), F32)),
        in_specs=[tile, halo, tile_after, halo_after, tile, per_batch(MEM_LEN, 2 * MEM_LEN),
                  _full((2, 2 * CHUNK, 2 * CHUNK)),
                  pl.BlockSpec(memory_space=pltpu.SMEM),
                  _full((1, A_WIDTH)), _full((1, A_WIDTH)),
                  _full(grp), _full(grp), _full(grp),
                  _full((1, D_MODEL)), _full((1, D_MODEL)),
                  _full((IN_WIDTH, D_MODEL), single=True), _full((D_MODEL, D_MODEL), single=True),
                  _full((CHUNK, 2 * CHUNK))],
        out_specs=(tile, per_batch(MEM_LEN, 2 * MEM_LEN), anyspec, anyspec,
                   _full((1, D_MODEL)), _full((1, D_MODEL)), _full((8, CHUNK)),
                   _full(grp), _full((A_GROUPS, CHUNK)), _full((1, A_WIDTH)), _full((1, A_WIDTH)),
                   _full((8, CHUNK)), _full((N_BUCKETS, CHUNK))),
        scratch_shapes=[pltpu.VMEM((IN_WIDTH, D_MODEL), F32), pltpu.VMEM((D_MODEL, D_MODEL), F32),
                        pltpu.VMEM((tm, UV_W), F32), pltpu.VMEM((tm, Z_W), F32),
                        pltpu.VMEM((tm, 512), MM), pltpu.VMEM((tm + CHUNK, 2 * CHUNK), MM),
                        pltpu.VMEM((3, tm, D_MODEL), MM), pltpu.VMEM((CHUNK, D_MODEL), MM),
                        pltpu.VMEM((tm, IN_WIDTH), MM),
                        pltpu.VMEM((tm, D_MODEL), F32),
                        pltpu.VMEM((tm, D_MODEL), F32), pltpu.VMEM((2, tm, 1), F32),
                        pltpu.VMEM((tm, D_MODEL), F32), pltpu.VMEM((tm, D_MODEL), F32)]
                       + [pltpu.VMEM((tm, A_WIDTH), F32) for _ in range(4)]
                       + [pltpu.VMEM((tm, A_WIDTH), MM),
                          pltpu.VMEM((bpt * 2, 2 * CHUNK, 2 * CHUNK), F32),
                          pltpu.VMEM((bpt * 2, 2 * CHUNK, CHUNK), F32),
                          pltpu.VMEM((2, 2 * tm, MEM_LEN), F32),
                          pltpu.VMEM((bpt * 2, 2 * CHUNK, CHUNK), MM),
                          pltpu.VMEM((bpt * 2, 2 * CHUNK, CHUNK), MM),
                          pltpu.VMEM((tm + CHUNK, 2 * CHUNK), F32),
                          pltpu.VMEM((2, 2 * CHUNK, 2 * CHUNK), F32),
                          pltpu.VMEM(grp, F32),
                          pltpu.VMEM((8, CHUNK), F32),
                          pltpu.SemaphoreType.DMA((2,))],
        compiler_params=_params(dimension_semantics=("arbitrary", "arbitrary")),
    )(x2, x2, x2, x2, tgt2, mkv3, bias.reshape(2, 2 * CHUNK, 2 * CHUNK), sinks, vg, vb, wt, wtt, bcol, g1, g2, w_in_t, w_o, buckets)


class _ShardReduce:
    def __init__(self, pos, g, bufs, sems, rows=None):
        self.x, self.y, self.c = pos
        self.g = g
        self.own, self.rcv, self.sbuf, self.rbuf, self.cbuf = bufs
        self.ld, self.sa, self.ra, self.sb, self.rb = sems
        self.rows = None if rows is None else pl.ds(*rows)
        self.nrow = g.shape[1] if rows is None else rows[1]
        self.here = (self.x, self.y, self.c)
        self.sib = (self.x, self.y, 1 - self.c)
        self.first, self.second, self.far = _route(*pos)

    def _block(self, owner):
        return self.g.at[owner] if self.rows is None else self.g.at[owner, self.rows]

    def _load(self, q):
        return pltpu.make_async_copy(self._block(2 * q + self.c), self.own.at[q], self.ld.at[q])

    def _to_sib(self, q, to):
        return _remote(self._block(2 * q + 1 - self.c), self.rcv.at[q], self.sa.at[q], self.ra.at[q], to)

    def _send(self, k, to):
        dst = self.cbuf.at[0] if k == 1 else self.rbuf.at[0 if k == 0 else 1]
        return _remote(self.sbuf.at[k], dst, self.sb.at[k], self.rb.at[k], to)

    def _stage(self, k, which, extra=None):
        def cast(r):
            v = self.rcv[which, r, :]
            if extra is not None:
                v = v + extra[0, r, :].astype(F32)
            self.sbuf[k, r, :] = v.astype(BF16)

        _rows_loop(self.nrow, cast)

    @staticmethod
    def _q(chip):
        return 2 * chip[0] + chip[1]

    def start(self):
        for q in range(4):
            self._load(q).start()
            self._to_sib(q, self.sib).start()

    def mid(self):
        for q in range(4):
            self._load(q).wait()
            self._to_sib(q, self.here).wait_recv()

        def add(r):
            for q in range(4):
                self.rcv[q, r, :] = self.rcv[q, r, :] + self.own[q, r, :]

        _rows_loop(self.nrow, add)
        to_first = (self.first[0], self.first[1], self.c)
        self._stage(0, self._q(self.first))
        self._send(0, to_first).start()
        self._stage(1, self._q(self.far))
        self._send(1, to_first).start()

    def pass_on(self):
        self._send(1, self.here).wait_recv()
        self._stage(2, self._q(self.second), extra=self.cbuf)
        self._send(2, (self.second[0], self.second[1], self.c)).start()

    def finish(self, out):
        self._send(0, self.here).wait_recv()
        self._send(2, self.here).wait_recv()
        which = 2 * self.x + self.y

        def tot(r):
            out[r, :] = (self.rcv[which, r, :] + self.rbuf[0, r, :].astype(F32)) + self.rbuf[1, r, :].astype(F32)

        _rows_loop(self.nrow, tot)
        for q in range(4):
            self._to_sib(q, self.sib).wait_send()
        to_first = (self.first[0], self.first[1], self.c)
        self._send(0, to_first).wait_send()
        self._send(1, to_first).wait_send()
        self._send(2, (self.second[0], self.second[1], self.c)).wait_send()


def _reduce_scratch(shape):
    return [pltpu.VMEM((4,) + shape, F32), pltpu.VMEM((4,) + shape, F32),
            pltpu.VMEM((3,) + shape, BF16), pltpu.VMEM((2,) + shape, BF16), pltpu.VMEM((1,) + shape, BF16),
            pltpu.SemaphoreType.DMA((4,)), pltpu.SemaphoreType.DMA((4,)), pltpu.SemaphoreType.DMA((4,)),
            pltpu.SemaphoreType.DMA((3,)), pltpu.SemaphoreType.DMA((3,))]


_N_RED = 10

_S_LAYOUT = (((1, D_MODEL), 0), ((1, D_MODEL), 8), ((1, D_MODEL), 16),
             ((1, A_WIDTH), 24), ((1, A_WIDTH), 28), ((A_GROUPS, CHUNK), 32),
             ((1, 4), 36), ((N_BUCKETS, 4), 40),
             ((A_GROUPS * CHUNK, CHUNK), 72))
_LOSS_ROW = 37
_W_SP_ROW = _S_LAYOUT[-1][1]
_S_ROWS = _W_SP_ROW + A_GROUPS * CHUNK
_N_SMALL = len(_S_LAYOUT)


def _pack_rows(dst, refs):
    for (shp, r0), ref in zip(_S_LAYOUT, refs):
        if shp[0] == 1 and shp[1] >= CHUNK:
            for i in range(shp[1] // CHUNK):
                dst[r0 + i:r0 + i + 1, :] = ref[:, i * CHUNK:(i + 1) * CHUNK]
        elif ref.shape[-1] == CHUNK:
            dst[r0:r0 + shp[0], :] = ref[0:shp[0], :]
        else:
            dst[r0:r0 + shp[0], 0:shp[1]] = ref[...]


def _unpack_rows(src, refs):
    for (shp, r0), ref in zip(_S_LAYOUT, refs):
        if shp[0] == 1 and shp[1] >= CHUNK:
            for i in range(shp[1] // CHUNK):
                ref[:, i * CHUNK:(i + 1) * CHUNK] = src[r0 + i:r0 + i + 1, :]
        elif shp[1] == CHUNK:
            ref[...] = src[r0:r0 + shp[0], :]
        else:
            if tuple(ref.shape) == (shp[1], shp[0]):
                ref[...] = src[r0:r0 + CHUNK, :].T[0:shp[1], 0:shp[0]]
            else:
                ref[...] = src[r0:r0 + shp[0], 0:shp[1]]


_MEM_G = 2


def _greduce(ga, gb, dmkv, mem2, gm, w_mkv, small_g, loss_p):
    shp_c = (SHARD_O, 2 * MEM_LEN)
    shapes = (shp_c, gb.shape[1:], ga.shape[1:])
    pieces = [None, None] + list(_A_PIECES)
    rs = _S_ROWS

    def body(*refs):
        it = iter(refs)
        take = lambda n: [next(it) for _ in range(n)]
        gb_ref, ga_ref, d_ref, m_ref, gm_ref, wm_ref = take(6)
        sg_refs = take(_N_SMALL - 1)
        loss_ref, = take(1)
        oc, ob, oa, ogs = take(4)
        red = take(len(pieces) * _N_RED)
        gs_ref, rs_a, rs_b, rs_w, gc_ref, dgm_ref = take(6)
        ssem_a, rsem_a, ssem_b, rsem_b = take(4)

        pos = _position()
        x, y, cc = pos
        myq = 2 * x + y
        here, sib = (x, y, cc), (x, y, 1 - cc)
        chips = _other_chips(x, y)
        arrays = [gc_ref, gb_ref] + [ga_ref] * len(_A_PIECES)
        reducers = [_ShardReduce(pos, g, red[k * _N_RED:k * _N_RED + 5], red[k * _N_RED + 5:(k + 1) * _N_RED], rows)
                    for k, (g, rows) in enumerate(zip(arrays, pieces))]
        for rd in reducers[1:]:
            rd.start()

        xf = m_ref[...]
        nm = xf * _rms(xf)
        hm = (nm * gm_ref[...]).astype(MM)
        d = d_ref[...].astype(MM)
        for o in range(N_DEV):
            gc_ref[o] = _dot_tn(hm[:, o * SHARD_O:(o + 1) * SHARD_O], d)
        dgm_ref[...] = jnp.sum(_dot_nt(d, wm_ref[...]) * nm, axis=0, keepdims=True)
        reducers[0].start()

        gs_ref[...] = jnp.zeros_like(gs_ref)
        _pack_rows(gs_ref, sg_refs[:_MEM_G] + [dgm_ref] + sg_refs[_MEM_G:])
        gs_ref[_LOSS_ROW:_LOSS_ROW + 1, :] = loss_ref[0:1, :]
        small_a = _remote(gs_ref, rs_a, ssem_a, rsem_a, sib)
        small_a.start()

        _remote(gs_ref, rs_a, ssem_a, rsem_a, here).wait_recv()
        rs_b[myq] = gs_ref[0:_W_SP_ROW, :] + rs_a[0:_W_SP_ROW, :]
        rs_w[myq] = (gs_ref[_W_SP_ROW:rs, :] + rs_a[_W_SP_ROW:rs, :]).astype(BF16)
        small_b = []
        for j, chip in enumerate(chips):
            to = (chip[0], chip[1], cc)
            small_b.append(_remote(rs_b.at[myq], rs_b.at[myq], ssem_b.at[0, j], rsem_b.at[0, j], to))
            small_b.append(_remote(rs_w.at[myq], rs_w.at[myq], ssem_b.at[1, j], rsem_b.at[1, j], to))
        for cp in small_b:
            cp.start()
        late_last = reducers[1:] + reducers[:1]
        for rd in late_last:
            rd.mid()
        for rd in late_last:
            rd.pass_on()

        for j in range(3):
            _remote(rs_b.at[myq], rs_b.at[myq], ssem_b.at[0, j], rsem_b.at[0, j], here).wait_recv()
            _remote(rs_w.at[myq], rs_w.at[myq], ssem_b.at[1, j], rsem_b.at[1, j], here).wait_recv()
        ogs[0:_W_SP_ROW, :] = ((rs_b[0] + rs_b[1]) + rs_b[2]) + rs_b[3]

        def tot_w(r):
            w = [rs_w[q, r, :].astype(F32) for q in range(4)]
            ogs[pl.ds(pl.multiple_of(_W_SP_ROW + r.start, 8), _ROWS), :] = ((w[0] + w[1]) + w[2]) + w[3]

        _rows_loop(rs - _W_SP_ROW, tot_w)
        outs = [ob] + [oa.at[pl.ds(r0, n)] for r0, n in _A_PIECES] + [oc]
        for rd, out in zip(late_last, outs):
            rd.finish(out)
        small_a.wait_send()
        for cp in small_b:
            cp.wait_send()

    vm = pl.BlockSpec(memory_space=pltpu.VMEM)
    anyspec = pl.BlockSpec(memory_space=pl.ANY)
    scratch = []
    for shp in (shapes[0], shapes[1]) + tuple((n, ga.shape[2]) for _, n in _A_PIECES):
        scratch += _reduce_scratch(shp)
    scratch += [pltpu.VMEM((rs, CHUNK), F32), pltpu.VMEM((rs, CHUNK), F32),
                pltpu.VMEM((4, _W_SP_ROW, CHUNK), F32), pltpu.VMEM((4, rs - _W_SP_ROW, CHUNK), BF16),
                pltpu.VMEM((N_DEV,) + shp_c, F32), pltpu.VMEM((1, D_MODEL), F32),
                pltpu.SemaphoreType.DMA, pltpu.SemaphoreType.DMA,
                pltpu.SemaphoreType.DMA((2, 3)), pltpu.SemaphoreType.DMA((2, 3))]
    tc, tb, ta, ts = pl.pallas_call(
        body, name="greduce",
        out_shape=tuple([jax.ShapeDtypeStruct(shp, F32) for shp in shapes] + [jax.ShapeDtypeStruct((rs, CHUNK), F32)]),
        in_specs=[anyspec] * 2 + [vm] * (4 + _N_SMALL),
        out_specs=(vm, vm, vm, vm),
        scratch_shapes=scratch,
        compiler_params=_params(),
    )(gb, ga, dmkv, mem2, gm, w_mkv, *small_g, loss_p)
    return ta, tb, tc, ts


def _adamw(w, g, m, v):
    m = ADAM_B1 * m + (1.0 - ADAM_B1) * g
    v = ADAM_B2 * v + (1.0 - ADAM_B2) * (g * g)
    m_hat = m / (1.0 - ADAM_B1 ** ADAM_STEP)
    v_hat = v / (1.0 - ADAM_B2 ** ADAM_STEP)
    delta = -ADAM_LR * (m_hat / (jnp.sqrt(v_hat) + ADAM_EPS) + ADAM_WD * w)
    return delta, m, v


def _update(ta, tb, tc, ts, big_wmv, small_wmv):
    shapes = (ta.shape, tb.shape, tc.shape)
    rs = _S_ROWS
    small_shapes = [tuple(a.shape) for a in small_wmv[0]]

    def body(*refs):
        it = iter(refs)
        take = lambda n: [next(it) for _ in range(n)]
        ga_ref, gb_ref, gc_ref, gs_ref = take(4)
        wa, ma, va, wb, mb, vb_, wc, mc, vc = take(9)
        sw_refs, sm_refs, sv_refs = take(_N_SMALL), take(_N_SMALL), take(_N_SMALL)
        oga, oda, oma, ova, ogb, odb, omb, ovb, ogc, odc, omc, ovc = take(12)
        so_refs = [take(_N_SMALL) for _ in range(4)]
        loss_out, = take(1)
        ws, ms, vs, ods, oms, ovs = take(6)

        for buf in (ws, ms, vs):
            buf[...] = jnp.zeros_like(buf)
        _pack_rows(ws, sw_refs)
        _pack_rows(ms, sm_refs)
        _pack_rows(vs, sv_refs)

        big = ((ga_ref, wa, ma, va, oga, oda, oma, ova), (gb_ref, wb, mb, vb_, ogb, odb, omb, ovb),
               (gc_ref, wc, mc, vc, ogc, odc, omc, ovc))
        for arr in range(3):
            g_r, w_r, m_r, v_r, og, od, om, ov = big[arr]

            def upd(r, g_r=g_r, w_r=w_r, m_r=m_r, v_r=v_r, og=og, od=od, om=om, ov=ov):
                g = g_r[r, :]
                d, m, v = _adamw(w_r[r, :], g, m_r[r, :], v_r[r, :])
                og[r, :] = g
                od[r, :] = d
                om[r, :] = m
                ov[r, :] = v

            _rows_loop(shapes[arr][0], upd)

        def upd_s(i, _):
            r = pl.ds(pl.multiple_of(i * 8, 8), 8)
            d, m, v = _adamw(ws[r, :], gs_ref[r, :], ms[r, :], vs[r, :])
            ods[r, :] = d
            oms[r, :] = m
            ovs[r, :] = v
            return 0

        lax.fori_loop(0, rs // 8, upd_s, 0)
        for k, buf in enumerate((gs_ref, ods, oms, ovs)):
            _unpack_rows(buf, so_refs[k])
        loss_out[...] = gs_ref[_LOSS_ROW:_LOSS_ROW + 1, 0:1]

    vm = pl.BlockSpec(memory_space=pltpu.VMEM)
    big_out = []
    for shp in shapes:
        big_out += [jax.ShapeDtypeStruct(shp, F32)] * 4
    small_out = [jax.ShapeDtypeStruct(shp[::-1] if shp == (N_BUCKETS, 4) else shp, F32) for shp in small_shapes] * 4
    out_shape = tuple(big_out + small_out + [jax.ShapeDtypeStruct((1, 1), F32)])
    n_in = 4 + 9 + 3 * _N_SMALL
    return pl.pallas_call(
        body, name="update",
        out_shape=out_shape,
        in_specs=[vm] * n_in,
        out_specs=tuple([vm] * len(out_shape)),
        scratch_shapes=[pltpu.VMEM((rs, CHUNK), F32) for _ in range(6)],
        compiler_params=_params(),
    )(ta, tb, tc, ts, *big_wmv, *small_wmv[0], *small_wmv[1], *small_wmv[2])


def kernel(x, mem, pre_norm_g, post_norm_g, mem_norm_g, w_in, w_mem_kv, v_norm_g, v_norm_b, w_spatial, b_spatial, attn_sinks, rel_bias, w_out, loss_target, m_pre_norm_g, m_post_norm_g, m_mem_norm_g, m_w_in, m_w_mem_kv, m_v_norm_g, m_v_norm_b, m_w_spatial, m_b_spatial, m_attn_sinks, m_rel_bias, m_w_out, v_pre_norm_g, v_post_norm_g, v_mem_norm_g, v_w_in, v_w_mem_kv, v_v_norm_g, v_v_norm_b, v_w_spatial, v_b_spatial, v_attn_sinks, v_rel_bias, v_w_out):
    sh_a = (w_in[0].T, m_w_in[0].T, v_w_in[0].T)
    sh_b = (w_out[0], m_w_out[0], v_w_out[0])
    sh_c = (w_mem_kv[0], m_w_mem_kv[0], v_w_mem_kv[0])
    nb, s, _ = x.shape
    t = nb * s
    x2 = x.reshape(t, D_MODEL)
    tgt2 = loss_target.reshape(t, D_MODEL)
    mem2 = mem.reshape(nb * MEM_LEN, D_MODEL)
    buckets = jnp.asarray(_t5_buckets())

    wa, wb, wc, bias, wt, wtt, bcol, mkv = _wgather(sh_a[0], sh_b[0], sh_c[0], rel_bias, w_spatial[0], b_spatial[0],
                                                    buckets, mem2, mem_norm_g)
    w_mkv = wc.reshape(D_MODEL, 2 * MEM_LEN)
    gx, dmkv, dwi, dwo, dg1, dg2, loss_p, dwsp, dbs, dvg, dvb, dsink, drel = _layer(
        x2, tgt2, mkv.reshape(nb, MEM_LEN, 2 * MEM_LEN), bias, attn_sinks.reshape(4), v_norm_g, v_norm_b, wt, wtt, bcol,
        pre_norm_g, post_norm_g, wa.reshape(IN_WIDTH, D_MODEL), wb.reshape(D_MODEL, D_MODEL), buckets,
        nb, s, min(256, s))
    gx = gx.reshape(nb, s, D_MODEL)
    small_grads = [dg1, dg2, dvg, dvb, dbs, dsink, drel, dwsp.reshape(A_GROUPS * CHUNK, CHUNK)]

    small_names = ["pre_norm_g", "post_norm_g", "mem_norm_g", "v_norm_g", "v_norm_b", "b_spatial", "attn_sinks",
                   "rel_bias", "w_spatial"]
    given = dict(pre_norm_g=(pre_norm_g, m_pre_norm_g, v_pre_norm_g), post_norm_g=(post_norm_g, m_post_norm_g, v_post_norm_g),
                 mem_norm_g=(mem_norm_g, m_mem_norm_g, v_mem_norm_g), v_norm_g=(v_norm_g, m_v_norm_g, v_v_norm_g),
                 v_norm_b=(v_norm_b, m_v_norm_b, v_v_norm_b), b_spatial=(b_spatial, m_b_spatial, v_b_spatial),
                 attn_sinks=(attn_sinks, m_attn_sinks, v_attn_sinks), rel_bias=(rel_bias, m_rel_bias, v_rel_bias),
                 w_spatial=(w_spatial, m_w_spatial, v_w_spatial))
    small_wmv = [[given[n][k].reshape(shp) for n, (shp, _) in zip(small_names, _S_LAYOUT)] for k in range(3)]

    ta, tb, tc, ts = _greduce(dwi.reshape(N_DEV, SHARD_IN, D_MODEL), dwo.reshape(N_DEV, SHARD_O, D_MODEL),
                              dmkv.reshape(nb * MEM_LEN, 2 * MEM_LEN), mem2, mem_norm_g, w_mkv, small_grads, loss_p)
    outs = _update(ta, tb, tc, ts, (*sh_a, *sh_b, *sh_c), small_wmv)
    ra, rb, rc = outs[0:4], outs[4:8], outs[8:12]
    loss = outs[12 + 4 * _N_SMALL].reshape(())

    res = {}
    for k, kind in enumerate(("grad", "delta", "new_m", "new_v")):
        res[kind, "w_in"] = ra[k].T[None]
        res[kind, "w_out"] = rb[k][None]
        res[kind, "w_mem_kv"] = rc[k][None]
        for i, n in enumerate(small_names):
            o = outs[12 + k * _N_SMALL + i]
            res[kind, n] = o.T if n == "rel_bias" else o.reshape(given[n][0].shape)
    order = ["pre_norm_g", "post_norm_g", "mem_norm_g", "w_in", "w_mem_kv", "v_norm_g", "v_norm_b", "w_spatial",
             "b_spatial", "attn_sinks", "rel_bias", "w_out"]
    flat = [res[kind, n] for kind in ("grad", "delta", "new_m", "new_v") for n in order]
    return (loss, gx, *flat)
```

```python
import numpy as np
import jax
import jax.numpy as jnp
from jax import lax
from jax.experimental import pallas as pl
from jax.experimental.pallas import tpu as pltpu

F32 = jnp.float32
BF16 = jnp.bfloat16
MM = jnp.bfloat16

D_MODEL = 1024
CHUNK = 128
A_GROUPS = 4
A_WIDTH = 512
UV_W = 1024
QKV_W = 768
Z_W = 1024
IN_WIDTH = UV_W + QKV_W + Z_W
MEM_LEN = 256
N_BUCKETS = 32
MAX_DISTANCE = 128
EPS = 1e-6
NEG = -1e30
SCALE = 0.125
N_DEV = 8
SHARD_IN = IN_WIDTH // N_DEV
SHARD_O = D_MODEL // N_DEV

SQ_COL, SK_COL, SV_COL, MQ_COL, Z_COL = UV_W, UV_W + 256, UV_W + 384, UV_W + 512, UV_W + QKV_W
DW_PIECES = ((0, SQ_COL), (SQ_COL, Z_COL), (Z_COL, IN_WIDTH))
YB_OFF, YC_OFF = 512, 768

ADAM_LR = 0.001
ADAM_B1 = 0.9
ADAM_B2 = 0.999
ADAM_EPS = 1e-08
ADAM_WD = 0.01
ADAM_STEP = 10

VMEM_LIMIT = 60 * 1024 * 1024

_GELU_C = 0.7978845608028654
_GELU_A = 0.044715

MESH = pl.DeviceIdType.MESH
_ROWS = 32


def _dot(a, b):
    return lax.dot_general(a, b, (((1,), (0,)), ((), ())), preferred_element_type=F32)


def _dot_nt(a, b):
    return lax.dot_general(a, b, (((1,), (1,)), ((), ())), preferred_element_type=F32)


def _dot_tn(a, b):
    return lax.dot_general(a, b, (((0,), (0,)), ((), ())), preferred_element_type=F32)


def _gelu_and_grad(x):
    x2 = x * x
    t = jnp.tanh(_GELU_C * (x + _GELU_A * x * x2))
    g = 0.5 * x * (1.0 + t)
    dg = 0.5 * (1.0 + t) + 0.5 * x * (1.0 - t * t) * (_GELU_C * (1.0 + 3.0 * _GELU_A * x2))
    return g, dg


def _t5_buckets():
    qi = np.arange(CHUNK)[:, None]
    kj = np.arange(2 * CHUNK)[None, :]
    n = np.maximum(qi + CHUNK - kj, 0)
    max_exact = N_BUCKETS // 2
    large = max_exact + (np.log(np.maximum(n, 1) / max_exact) / np.log(MAX_DISTANCE / max_exact)
                         * (N_BUCKETS - max_exact)).astype(np.int32)
    large = np.minimum(large, N_BUCKETS - 1)
    return np.where(n < max_exact, n, large).astype(np.int32)


def _params(**kw):
    return pltpu.CompilerParams(vmem_limit_bytes=VMEM_LIMIT, **kw)


def _full(shape, single=False):
    nd = len(shape)
    if single:
        return pl.BlockSpec(shape, lambda *_: (0,) * nd, pipeline_mode=pl.Buffered(1))
    return pl.BlockSpec(shape, lambda *_: (0,) * nd)


def _window_valid():
    qi = lax.broadcasted_iota(jnp.int32, (CHUNK, 2 * CHUNK), 0)
    kj = lax.broadcasted_iota(jnp.int32, (CHUNK, 2 * CHUNK), 1)
    dist = qi + CHUNK - kj
    return (dist >= 0) & (dist < CHUNK)


def _position():
    return lax.axis_index("x"), lax.axis_index("y"), lax.axis_index("c")


def _other_chips(x, y):
    return [(1 - x, y), (x, 1 - y), (1 - x, 1 - y)]


def _route(x, y, c):
    first = (x * c + (1 - x) * (1 - c), y * (1 - c) + (1 - y) * c)
    second = (x * (1 - c) + (1 - x) * c, y * c + (1 - y) * (1 - c))
    return first, second, (1 - x, 1 - y)


def _remote(src, dst, ssem, rsem, to):
    return pltpu.make_async_remote_copy(src_ref=src, dst_ref=dst, send_sem=ssem, recv_sem=rsem,
                                        device_id=to, device_id_type=MESH)


def _rows_loop(nrow, fn, rows=_ROWS):
    assert nrow % rows == 0

    def step(i, _):
        fn(pl.ds(pl.multiple_of(i * rows, rows), rows))
        return 0

    lax.fori_loop(0, nrow // rows, step, 0)


class _Gather:
    def __init__(self, pos, out, ssem, rsem):
        self.x, self.y, self.c = pos
        self.out, self.ssem, self.rsem = out, ssem, rsem
        self.me = 4 * self.x + 2 * self.y + self.c
        self.here = (self.x, self.y, self.c)
        self.sib = (self.x, self.y, 1 - self.c)
        self.first, self.second, self.far = _route(*pos)

    def _copy(self, k, blk, to):
        r = self.out.at[blk]
        return _remote(r, r, self.ssem.at[k], self.rsem.at[k], to)

    def _idx(self, chip, core):
        return 4 * chip[0] + 2 * chip[1] + core

    def _on(self, chip):
        return (chip[0], chip[1], self.c)

    def start(self):
        self._copy(0, self.me, self.sib).start()
        self._copy(1, self.me, self._on(self.first)).start()
        self._copy(2, self.me, self._on(self.second)).start()

    def forward(self):
        c = self.c
        self._copy(1, self._idx(self.first, c), self.here).wait_recv()
        self._copy(3, self._idx(self.first, c), self._on(self.second)).start()
        self._copy(4, self._idx(self.first, c), self.sib).start()
        self._copy(2, self._idx(self.second, c), self.here).wait_recv()
        self._copy(5, self._idx(self.second, c), self.sib).start()
        self._copy(3, self._idx(self.far, c), self.here).wait_recv()
        self._copy(6, self._idx(self.far, c), self.sib).start()

    def finish(self):
        c = self.c
        self._copy(0, self._idx((self.x, self.y), 1 - c), self.here).wait_recv()
        for k, chip in ((4, self.second), (5, self.first), (6, self.far)):
            self._copy(k, self._idx(chip, 1 - c), self.here).wait_recv()
        self._copy(0, self.me, self.sib).wait_send()
        self._copy(1, self.me, self._on(self.first)).wait_send()
        self._copy(2, self.me, self._on(self.second)).wait_send()
        self._copy(3, self._idx(self.first, c), self._on(self.second)).wait_send()
        for k, chip in ((4, self.first), (5, self.second), (6, self.far)):
            self._copy(k, self._idx(chip, c), self.sib).wait_send()


def _prep_tables(rb_ref, w_ref, b_ref, bk_ref, bias_ref, wt_ref, wtt_ref, bcol_ref):
    valid = _window_valid()
    bk = bk_ref[...]
    acc = [jnp.full((CHUNK, 2 * CHUNK), NEG, F32) for _ in range(4)]
    for b in range(N_BUCKETS):
        hit = (bk == b) & valid
        for h in range(4):
            acc[h] = jnp.where(hit, rb_ref[b, h], acc[h])
    for h in range(4):
        bias_ref[h] = acc[h]
    r = lax.broadcasted_iota(jnp.int32, (CHUNK, CHUNK), 0)
    c = lax.broadcasted_iota(jnp.int32, (CHUNK, CHUNK), 1)
    for g in range(A_GROUPS):
        w = jnp.where(r >= c, w_ref[g], 0.0)
        wt_ref[g] = w.astype(MM)
        wtt_ref[g] = w.T.astype(MM)
        bcol_ref[g] = jnp.broadcast_to(b_ref[g:g + 1, :], (CHUNK, CHUNK)).T


def _wgather(a, b, c, rel_bias, w_sp, b_sp, buckets, mem2, gm):
    tmem = mem2.shape[0]

    def body(a_ref, b_ref, c_ref, rb_ref, w_ref, bsp_ref, bk_ref, m_ref, gm_ref,
             oa, ob, oc, bias_ref, wt_ref, wtt_ref, bcol_ref, mkv_ref, ssem, rsem):
        pos = _position()
        me = 4 * pos[0] + 2 * pos[1] + pos[2]
        gathers = []
        for k, (src, out) in enumerate(((c_ref, oc), (b_ref, ob), (a_ref, oa))):
            out[me] = src[...].astype(BF16)
            g = _Gather(pos, out, ssem.at[k], rsem.at[k])
            g.start()
            gathers.append(g)
        _prep_tables(rb_ref, w_ref, bsp_ref, bk_ref, bias_ref, wt_ref, wtt_ref, bcol_ref)
        for g in gathers:
            g.forward()
        gathers[0].finish()
        xf = m_ref[...]
        hm = (xf * _rms(xf) * gm_ref[...]).astype(MM)
        acc = jnp.zeros((tmem, 2 * MEM_LEN), F32)
        for d in range(N_DEV):
            acc = acc + _dot(hm[:, d * SHARD_O:(d + 1) * SHARD_O], oc[d])
        mkv_ref[...] = acc.astype(MM)
        for g in gathers[1:]:
            g.finish()

    vm = pl.BlockSpec(memory_space=pltpu.VMEM)
    grp = (A_GROUPS, CHUNK, CHUNK)
    return pl.pallas_call(
        body, name="wgather",
        out_shape=(jax.ShapeDtypeStruct((N_DEV,) + a.shape, BF16),
                   jax.ShapeDtypeStruct((N_DEV,) + b.shape, BF16),
                   jax.ShapeDtypeStruct((N_DEV,) + c.shape, BF16),
                   jax.ShapeDtypeStruct((4, CHUNK, 2 * CHUNK), F32),
                   jax.ShapeDtypeStruct(grp, MM), jax.ShapeDtypeStruct(grp, MM), jax.ShapeDtypeStruct(grp, F32),
                   jax.ShapeDtypeStruct((tmem, 2 * MEM_LEN), MM)),
        in_specs=[vm, vm, vm, pl.BlockSpec(memory_space=pltpu.SMEM), vm, vm, vm, vm, vm],
        out_specs=tuple([vm] * 8),
        scratch_shapes=[pltpu.SemaphoreType.DMA((3, 7)), pltpu.SemaphoreType.DMA((3, 7))],
        compiler_params=_params(),
    )(a, b, c, rel_bias, w_sp, b_sp, buckets, mem2, gm)


def _half_masks(rows):
    lane = lax.broadcasted_iota(jnp.int32, (rows, CHUNK), 1)
    return lane < 64


def _dup_heads(band):
    b32 = band.astype(F32)
    rolled = pltpu.roll(b32, 64, 1)
    lo = _half_masks(band.shape[0])
    return (jnp.where(lo, b32, rolled).astype(MM), jnp.where(lo, rolled, b32).astype(MM))


def _swa_probs(qk, bias_h, sink_h, first_add):
    s = qk * SCALE + bias_h + first_add
    m = jnp.maximum(jnp.max(s, axis=-1, keepdims=True), sink_h)
    p = jnp.exp(s - m)
    es = jnp.exp(sink_h - m)
    inv = 1.0 / (jnp.sum(p, axis=-1, keepdims=True) + es)
    return p * inv, es * inv


def _softmax(s):
    m = jnp.max(s, axis=-1, keepdims=True)
    p = jnp.exp(s - m)
    return p * (1.0 / jnp.sum(p, axis=-1, keepdims=True))


def _first_block_mask(n):
    col = lax.broadcasted_iota(jnp.int32, (2 * CHUNK, 2 * CHUNK), 1)
    return jnp.where((col < CHUNK) & (n == 0), NEG, 0.0)


def _stack_heads(x128, lo):
    return jnp.concatenate([jnp.where(lo, x128, 0.0), jnp.where(lo, 0.0, x128)], axis=0).astype(MM)


def _rms(xf):
    return lax.rsqrt(jnp.mean(xf * xf, axis=-1, keepdims=True) + EPS)


def _layer(x2, tgt2, mkv3, bias, sinks, vg, vb, wt, wtt, bcol, g1, g2, w_in_t, w_o, buckets, nb, s, tm):
    nt = s // tm
    bpt = tm // CHUNK
    bps = s // CHUNK
    t = nb * s
    last_step = nb * nt - 1

    def tile_at(step):
        return (step // nt) * nt + nt - 1 - step % nt

    def block_before(step):
        return (step // nt) * bps + jnp.maximum((nt - 1 - step % nt) * bpt - 1, 0)

    def body(x_ref, xp_ref, xn_ref, xpn_ref, t_ref, mkv_ref, bias_ref, sink_ref, vg_ref, vb_ref,
             wt_ref, wtt_ref, bcol_ref, g1_ref, g2_ref, wi_ref, wo_ref, bk_ref,
             gx_ref, dmkv_ref, dwi_hbm, dwo_hbm, dg1_ref, dg2_ref, loss_ref, dwsp_ref, dbs_ref,
             dvg_ref, dvb_ref, dsink_ref, drel_ref,
             acc_i, acc_o, uv_s, z_s, q_s, kv_s, h_s, hp_s, dp_s, dxo_s, dh_s, r_s,
             ycat, dyc, u_s, gu_s, gv_s, xh_s, vc_s, pb_s, ps_s, pc_s, kd_s, vd_s,
             dkv_acc, dbias_acc, dsv_acc, dsink_acc, sems):
        b, j = pl.program_id(0), pl.program_id(1)
        jt = nt - 1 - j
        step = b * nt + j
        g1v = g1_ref[...]
        NOW, NEXT, DONE = 0, 1, 2
        dw_cols = list(DW_PIECES)

        def weight_grad(n, slot):
            for c0, c1 in dw_cols[:n]:
                acc_i[c0:c1, :] += _dot_tn(dp_s[:, c0:c1], h_s[slot])
            del dw_cols[:n]

        def pre_norm(x_tile, x_before):
            xf = x_tile[...]
            r_s[NEXT] = _rms(xf)
            h_s[NEXT] = (xf * r_s[NEXT] * g1v).astype(MM)
            xp = x_before[...]
            hp_s[...] = (xp * _rms(xp) * g1v).astype(MM)

        def project_z():
            z_s[...] = _dot_nt(h_s[NEXT], wi_ref[Z_COL:IN_WIDTH, :])

        def project_uv():
            uv_s[...] = _dot_nt(h_s[NEXT], wi_ref[0:UV_W, :])

        @pl.when(step == 0)
        def _():
            for ref in (acc_i, acc_o, dg1_ref, dg2_ref, loss_ref, dwsp_ref, dvg_ref, dvb_ref,
                        dbias_acc, dsv_acc, dsink_acc):
                ref[...] = jnp.zeros_like(ref)
            dp_s[...] = jnp.zeros_like(dp_s)
            h_s[NOW] = jnp.zeros((tm, D_MODEL), MM)
            pre_norm(x_ref, xp_ref)
            project_z()
            project_uv()

        h_s[DONE] = h_s[NOW]
        r_s[NOW] = r_s[NEXT]
        h = h_s[NEXT]
        h_s[NOW] = h
        hp = hp_s[...]

        @pl.when(j == 0)
        def _():
            dmkv_ref[...] = jnp.zeros_like(dmkv_ref)
            dkv_acc[...] = jnp.zeros_like(dkv_acc)

        carry = dkv_acc[0:CHUNK, :]
        dkv_acc[...] = jnp.zeros_like(dkv_acc)
        dkv_acc[tm:tm + CHUNK, :] = carry

        lo = _half_masks(CHUNK)
        lob = _half_masks(2 * CHUNK)
        lot = _half_masks(tm)

        qkv = _dot_nt(h, wi_ref[SQ_COL:Z_COL, :])
        q_s[:, 0:256] = qkv[:, 0:256].astype(MM)
        q_s[:, 256:512] = qkv[:, 512:768].astype(MM)
        kv_s[CHUNK:CHUNK + tm, :] = qkv[:, 256:512].astype(MM)
        kv_s[0:CHUNK, :] = _dot_nt(hp, wi_ref[SK_COL:MQ_COL, :]).astype(MM)

        weight_grad(1, DONE)
        b_qk, b_pb = [], []
        for blk in range(bpt):
            r0 = blk * CHUNK
            rows = slice(r0, r0 + CHUNK)
            for g in range(A_GROUPS):
                cg = slice(g * CHUNK, (g + 1) * CHUNK)
                u, gu = _gelu_and_grad(uv_s[rows, cg])
                v, gv = _gelu_and_grad(uv_s[rows, A_WIDTH + g * CHUNK:A_WIDTH + (g + 1) * CHUNK])
                mu = jnp.mean(v, axis=-1, keepdims=True)
                xc = v - mu
                rstd = lax.rsqrt(jnp.mean(xc * xc, axis=-1, keepdims=True) + EPS)
                xhat = xc * rstd
                vc = (xhat * vg_ref[:, cg] + vb_ref[:, cg]).astype(MM)
                sv = _dot(wt_ref[g], vc) + bcol_ref[g]
                u_s[rows, cg] = u
                gu_s[rows, cg] = sv * gu
                gv_s[rows, cg] = rstd * gv
                xh_s[rows, cg] = xhat
                vc_s[rows, cg] = vc
                ycat[rows, cg] = u * sv
            weight_grad(1, DONE)
            kd = _dup_heads(kv_s[r0:r0 + 2 * CHUNK, 0:CHUNK])
            vd = _dup_heads(kv_s[r0:r0 + 2 * CHUNK, CHUNK:2 * CHUNK])
            for kvh in range(2):
                kd_s[blk * 2 + kvh] = kd[kvh]
                vd_s[blk * 2 + kvh] = vd[kvh]
                q2 = _stack_heads(q_s[rows, kvh * CHUNK:(kvh + 1) * CHUNK].astype(F32), lo)
                b_qk.append(_dot_nt(q2, kd[kvh]))
        qks, pcs = [], []
        for g in range(2):
            q128 = q_s[:, 256 + g * CHUNK:256 + (g + 1) * CHUNK].astype(F32)
            for hh in range(2):
                qsel = jnp.where(lot if hh == 0 else ~lot, q128, 0.0).astype(MM)
                qks.append(_dot_nt(qsel, mkv_ref[:, g * CHUNK:(g + 1) * CHUNK]))
        weight_grad(2, DONE)
        top = lax.broadcasted_iota(jnp.int32, (2 * CHUNK, 1), 0) < CHUNK
        for blk in range(bpt):
            first_add = _first_block_mask(jt * bpt + blk)
            for kvh in range(2):
                sink2 = jnp.where(top, sink_ref[2 * kvh], sink_ref[2 * kvh + 1])
                probs, ps = _swa_probs(b_qk[blk * 2 + kvh], bias_ref[kvh], sink2, first_add)
                pb_s[blk * 2 + kvh] = probs
                ps_s[blk * 2 + kvh] = jnp.broadcast_to(ps, (2 * CHUNK, CHUNK))
                b_pb.append(probs.astype(MM))
        weight_grad(len(dw_cols), DONE)
        for hd in range(4):
            probs = _softmax(qks[hd] * SCALE)
            pc_s[hd] = probs
            pcs.append(probs.astype(MM))
        for blk in range(bpt):
            rows = slice(blk * CHUNK, (blk + 1) * CHUNK)
            for kvh in range(2):
                out2 = _dot(b_pb[blk * 2 + kvh], vd_s[blk * 2 + kvh])
                ycat[rows, YB_OFF + kvh * CHUNK:YB_OFF + (kvh + 1) * CHUNK] = jnp.where(
                    lo, out2[0:CHUNK], out2[CHUNK:2 * CHUNK])
        outs = [_dot(pcs[hd], mkv_ref[:, MEM_LEN + (hd // 2) * CHUNK:MEM_LEN + (hd // 2 + 1) * CHUNK])
                for hd in range(4)]
        for g in range(2):
            ycat[:, YC_OFF + g * CHUNK:YC_OFF + (g + 1) * CHUNK] = jnp.where(lot, outs[2 * g], outs[2 * g + 1])

        zt = z_s[...]
        sig = 1.0 / (1.0 + jnp.exp(-zt))
        silu = zt * sig
        yc = ycat[...]
        yb = (yc * silu).astype(MM)
        pre_norm(xn_ref, xpn_ref)
        o = _dot(yb, wo_ref[...])
        project_z()
        r2 = _rms(o)
        nrm = o * r2
        g2v = g2_ref[...]
        e = x_ref[...] + nrm * g2v - t_ref[...]
        l1 = jnp.sum(e * e, axis=-1, keepdims=True)
        loss_ref[...] += jnp.broadcast_to(jnp.sum(l1, axis=0, keepdims=True) * (0.5 / D_MODEL), loss_ref.shape)
        dxo = e * (1.0 / D_MODEL)
        dxo_s[...] = dxo
        dg2_ref[...] += jnp.sum(dxo * nrm, axis=0, keepdims=True)
        dn = dxo * g2v
        do = r2 * (dn - nrm * jnp.mean(dn * nrm, axis=-1, keepdims=True))
        dob = do.astype(MM)
        dy = _dot_nt(dob, wo_ref[...])
        dp_s[:, Z_COL:IN_WIDTH] = (dy * yc * (sig * (1.0 + zt * (1.0 - sig)))).astype(MM)
        dyc[...] = dy * silu
        acc_o[...] += _dot_tn(yb, dob)

        def in_proj_bwd(c0, c1):
            part = _dot(dp_s[:, c0:c1], wi_ref[c0:c1, :])
            if c0 == Z_COL:
                dh_s[...] = part
            else:
                dh_s[...] += part

        in_proj_bwd(Z_COL, IN_WIDTH)

        for blk in range(bpt):
            r0 = blk * CHUNK
            rows = slice(r0, r0 + CHUNK)
            for g in range(A_GROUPS):
                cg = slice(g * CHUNK, (g + 1) * CHUNK)
                cv = slice(A_WIDTH + g * CHUNK, A_WIDTH + (g + 1) * CHUNK)
                dya = dyc[rows, cg]
                dp_s[rows, cg] = (dya * gu_s[rows, cg]).astype(MM)
                dsv = dya * u_s[rows, cg]
                dsvb = dsv.astype(MM)
                dsv_acc[g] += dsv
                dwsp_ref[g] += _dot_nt(dsvb, vc_s[rows, cg])
                dvc = _dot(wtt_ref[g], dsvb)
                xhat = xh_s[rows, cg]
                dvg_ref[:, cg] += jnp.sum(dvc * xhat, axis=0, keepdims=True)
                dvb_ref[:, cg] += jnp.sum(dvc, axis=0, keepdims=True)
                dxh = dvc * vg_ref[:, cg]
                dv = (dxh - jnp.mean(dxh, axis=-1, keepdims=True)
                      - xhat * jnp.mean(dxh * xhat, axis=-1, keepdims=True))
                dp_s[rows, cv] = (dv * gv_s[rows, cg]).astype(MM)
        in_proj_bwd(0, UV_W)
        b_dosel, b_dp, b_dss = [], [], []
        for blk in range(bpt):
            rows = slice(blk * CHUNK, (blk + 1) * CHUNK)
            for kvh in range(2):
                b_dosel.append(_stack_heads(dyc[rows, YB_OFF + kvh * CHUNK:YB_OFF + (kvh + 1) * CHUNK], lo))
                b_dp.append(_dot_nt(b_dosel[-1], vd_s[blk * 2 + kvh]))
        dosels, dps, dsss = [], [], []
        for hd in range(4):
            do128 = dyc[:, YC_OFF + (hd // 2) * CHUNK:YC_OFF + (hd // 2 + 1) * CHUNK]
            dosels.append(jnp.where(lot if hd % 2 == 0 else ~lot, do128, 0.0).astype(MM))
            dps.append(_dot_nt(dosels[hd], mkv_ref[:, MEM_LEN + (hd // 2) * CHUNK:MEM_LEN + (hd // 2 + 1) * CHUNK]))
        for blk in range(bpt):
            for kvh in range(2):
                probs = pb_s[blk * 2 + kvh]
                dp = b_dp[blk * 2 + kvh]
                delta = jnp.sum(probs * dp, axis=-1, keepdims=True)
                ds = probs * (dp - delta)
                dbias_acc[kvh] += ds
                sd = ps_s[blk * 2 + kvh][:, 0:1] * delta
                for gi in range(2):
                    hd = 2 * kvh + gi
                    dsink_acc[hd:hd + 1, :] += jnp.broadcast_to(
                        -jnp.sum(sd[gi * CHUNK:(gi + 1) * CHUNK], axis=0, keepdims=True), (1, CHUNK))
                b_dss.append((ds * SCALE).astype(MM))
        for hd in range(4):
            probs = pc_s[hd]
            ds = probs * (dps[hd] - jnp.sum(probs * dps[hd], axis=-1, keepdims=True))
            dsss.append((ds * SCALE).astype(MM))
        for blk in range(bpt):
            r0 = blk * CHUNK
            rows = slice(r0, r0 + CHUNK)
            dk_f, dv_f = [], []
            for kvh in range(2):
                dss = b_dss[blk * 2 + kvh]
                q2 = _stack_heads(q_s[rows, kvh * CHUNK:(kvh + 1) * CHUNK].astype(F32), lo)
                dq2 = _dot(dss, kd_s[blk * 2 + kvh])
                dkd = _dot_tn(dss, q2)
                dvd = _dot_tn(pb_s[blk * 2 + kvh].astype(MM), b_dosel[blk * 2 + kvh])
                dp_s[rows, SQ_COL + kvh * CHUNK:SQ_COL + (kvh + 1) * CHUNK] = jnp.where(
                    lo, dq2[0:CHUNK], dq2[CHUNK:2 * CHUNK]).astype(MM)
                dk_f.append(dkd + pltpu.roll(dkd, 64, 1))
                dv_f.append(dvd + pltpu.roll(dvd, 64, 1))
            dkv_acc[r0:r0 + 2 * CHUNK, 0:CHUNK] += jnp.where(lob, dk_f[0], dk_f[1])
            dkv_acc[r0:r0 + 2 * CHUNK, CHUNK:2 * CHUNK] += jnp.where(lob, dv_f[0], dv_f[1])
        dp_s[:, SK_COL:MQ_COL] = dkv_acc[CHUNK:CHUNK + tm, :].astype(MM)
        for g in range(2):
            q128 = q_s[:, 256 + g * CHUNK:256 + (g + 1) * CHUNK].astype(F32)
            k128 = mkv_ref[:, g * CHUNK:(g + 1) * CHUNK]
            dq128 = jnp.zeros((tm, CHUNK), F32)
            dk128 = jnp.zeros((MEM_LEN, CHUNK), F32)
            dv128 = jnp.zeros((MEM_LEN, CHUNK), F32)
            for hh in range(2):
                hd = 2 * g + hh
                half = lot if hh == 0 else ~lot
                qsel = jnp.where(half, q128, 0.0).astype(MM)
                dq128 = dq128 + jnp.where(half, _dot(dsss[hd], k128), 0.0)
                dk128 = dk128 + _dot_tn(dsss[hd], qsel)
                dv128 = dv128 + _dot_tn(pc_s[hd].astype(MM), dosels[hd])
            dp_s[:, MQ_COL + g * CHUNK:MQ_COL + (g + 1) * CHUNK] = dq128.astype(MM)
            dmkv_ref[:, g * CHUNK:(g + 1) * CHUNK] += dk128
            dmkv_ref[:, MEM_LEN + g * CHUNK:MEM_LEN + (g + 1) * CHUNK] += dv128

        in_proj_bwd(SQ_COL, Z_COL)
        project_uv()
        dh = dh_s[...]
        r = r_s[NOW]
        nx = x_ref[...] * r
        dg1_ref[...] += jnp.sum(dh * nx, axis=0, keepdims=True)
        dnx = dh * g1v
        gx_ref[...] = dxo_s[...] + r * (dnx - nx * jnp.mean(dnx * nx, axis=-1, keepdims=True))

        @pl.when(step == last_step)
        def _():
            dw_cols.extend(DW_PIECES)
            weight_grad(len(dw_cols), NOW)
            out_i = pltpu.make_async_copy(acc_i, dwi_hbm, sems.at[0])
            out_o = pltpu.make_async_copy(acc_o, dwo_hbm, sems.at[1])
            out_i.start()
            out_o.start()
            r_ = lax.broadcasted_iota(jnp.int32, (CHUNK, CHUNK), 0)
            c_ = lax.broadcasted_iota(jnp.int32, (CHUNK, CHUNK), 1)
            for g in range(A_GROUPS):
                dwsp_ref[g] = jnp.where(r_ >= c_, dwsp_ref[g], 0.0)
                dbs_ref[g:g + 1, :] = jnp.sum(dsv_acc[g].T, axis=0, keepdims=True)
            rows8 = lax.broadcasted_iota(jnp.int32, (8, CHUNK), 0)
            cols8 = lax.broadcasted_iota(jnp.int32, (8, CHUNK), 1)
            sk = jnp.zeros((8, CHUNK), F32)
            for hd in range(4):
                sk = sk + jnp.where((rows8 == 0) & (cols8 == hd),
                                    jnp.broadcast_to(dsink_acc[hd:hd + 1, :], (8, CHUNK)), 0.0)
            dsink_ref[...] = sk
            bk = bk_ref[...]
            valid = _window_valid()
            rrow = lax.broadcasted_iota(jnp.int32, (N_BUCKETS, CHUNK), 0)
            rcol = lax.broadcasted_iota(jnp.int32, (N_BUCKETS, CHUNK), 1)
            acc = jnp.zeros((N_BUCKETS, CHUNK), F32)
            for bb in range(N_BUCKETS):
                hit = (bk == bb) & valid
                for hd in range(4):
                    dbias = dbias_acc[hd // 2, (hd % 2) * CHUNK:(hd % 2 + 1) * CHUNK, :]
                    part = jnp.sum(jnp.where(hit, dbias, 0.0), axis=-1, keepdims=True)
                    tot = jnp.sum(part, axis=0, keepdims=True)
                    acc = acc + jnp.where((rrow == bb) & (rcol == hd), jnp.broadcast_to(tot, (N_BUCKETS, CHUNK)), 0.0)
            drel_ref[...] = acc
            out_i.wait()
            out_o.wait()

    after = lambda b, j: jnp.minimum(b * nt + j + 1, last_step)
    tile = pl.BlockSpec((tm, D_MODEL), lambda b, j: (tile_at(b * nt + j), 0))
    tile_after = pl.BlockSpec((tm, D_MODEL), lambda b, j: (tile_at(after(b, j)), 0))
    halo = pl.BlockSpec((CHUNK, D_MODEL), lambda b, j: (block_before(b * nt + j), 0))
    halo_after = pl.BlockSpec((CHUNK, D_MODEL), lambda b, j: (block_before(after(b, j)), 0))
    per_batch = lambda r, w: pl.BlockSpec((None, r, w), lambda b, j: (b, 0, 0))
    anyspec = pl.BlockSpec(memory_space=pl.ANY)
    grp = (A_GROUPS, CHUNK, CHUNK)
    return pl.pallas_call(
        body, name="layer", grid=(nb, nt),
        out_shape=(jax.ShapeDtypeStruct((t, D_MODEL), F32),
                   jax.ShapeDtypeStruct((nb, MEM_LEN, 2 * MEM_LEN), F32),
                   jax.ShapeDtypeStruct((IN_WIDTH, D_MODEL), F32),
                   jax.ShapeDtypeStruct((D_MODEL, D_MODEL), F32),
                   jax.ShapeDtypeStruct((1, D_MODEL), F32),
                   jax.ShapeDtypeStruct((1, D_MODEL), F32),
                   jax.ShapeDtypeStruct((8, CHUNK), F32),
                   jax.ShapeDtypeStruct(grp, F32),
                   jax.ShapeDtypeStruct((A_GROUPS, CHUNK), F32),
                   jax.ShapeDtypeStruct((1, A_WIDTH), F32),
                   jax.ShapeDtypeStruct((1, A_WIDTH), F32),
                   jax.ShapeDtypeStruct((8, CHUNK), F32),
                   jax.ShapeDtypeStruct((N_BUCKETS, CHUNK), F32)),
        in_specs=[tile, halo, tile_after, halo_after, tile, per_batch(MEM_LEN, 2 * MEM_LEN),
                  _full((2, 2 * CHUNK, 2 * CHUNK)),
                  pl.BlockSpec(memory_space=pltpu.SMEM),
                  _full((1, A_WIDTH)), _full((1, A_WIDTH)),
                  _full(grp), _full(grp), _full(grp),
                  _full((1, D_MODEL)), _full((1, D_MODEL)),
                  _full((IN_WIDTH, D_MODEL), single=True), _full((D_MODEL, D_MODEL), single=True),
                  _full((CHUNK, 2 * CHUNK))],
        out_specs=(tile, per_batch(MEM_LEN, 2 * MEM_LEN), anyspec, anyspec,
                   _full((1, D_MODEL)), _full((1, D_MODEL)), _full((8, CHUNK)),
                   _full(grp), _full((A_GROUPS, CHUNK)), _full((1, A_WIDTH)), _full((1, A_WIDTH)),
                   _full((8, CHUNK)), _full((N_BUCKETS, CHUNK))),
        scratch_shapes=[pltpu.VMEM((IN_WIDTH, D_MODEL), F32), pltpu.VMEM((D_MODEL, D_MODEL), F32),
                        pltpu.VMEM((tm, UV_W), F32), pltpu.VMEM((tm, Z_W), F32),
                        pltpu.VMEM((tm, 512), MM), pltpu.VMEM((tm + CHUNK, 2 * CHUNK), MM),
                        pltpu.VMEM((3, tm, D_MODEL), MM), pltpu.VMEM((CHUNK, D_MODEL), MM),
                        pltpu.VMEM((tm, IN_WIDTH), MM),
                        pltpu.VMEM((tm, D_MODEL), F32),
                        pltpu.VMEM((tm, D_MODEL), F32), pltpu.VMEM((2, tm, 1), F32),
                        pltpu.VMEM((tm, D_MODEL), F32), pltpu.VMEM((tm, D_MODEL), F32)]
                       + [pltpu.VMEM((tm, A_WIDTH), F32) for _ in range(4)]
                       + [pltpu.VMEM((tm, A_WIDTH), MM),
                          pltpu.VMEM((bpt * 2, 2 * CHUNK, 2 * CHUNK), F32),
                          pltpu.VMEM((bpt * 2, 2 * CHUNK, CHUNK), F32),
                          pltpu.VMEM((4, tm, MEM_LEN), F32),
                          pltpu.VMEM((bpt * 2, 2 * CHUNK, CHUNK), MM),
                          pltpu.VMEM((bpt * 2, 2 * CHUNK, CHUNK), MM),
                          pltpu.VMEM((tm + CHUNK, 2 * CHUNK), F32),
                          pltpu.VMEM((2, 2 * CHUNK, 2 * CHUNK), F32),
                          pltpu.VMEM(grp, F32),
                          pltpu.VMEM((8, CHUNK), F32),
                          pltpu.SemaphoreType.DMA((2,))],
        compiler_params=_params(dimension_semantics=("arbitrary", "arbitrary")),
    )(x2, x2, x2, x2, tgt2, mkv3, bias.reshape(2, 2 * CHUNK, 2 * CHUNK), sinks, vg, vb, wt, wtt, bcol, g1, g2, w_in_t, w_o, buckets)


class _ShardReduce:
    def __init__(self, pos, g, bufs, sems):
        self.x, self.y, self.c = pos
        self.g = g
        self.own, self.rcv, self.sbuf, self.rbuf, self.cbuf = bufs
        self.ld, self.sa, self.ra, self.sb, self.rb = sems
        self.nrow = g.shape[1]
        self.here = (self.x, self.y, self.c)
        self.sib = (self.x, self.y, 1 - self.c)
        self.first, self.second, self.far = _route(*pos)

    def _load(self, q):
        return pltpu.make_async_copy(self.g.at[2 * q + self.c], self.own.at[q], self.ld.at[q])

    def _to_sib(self, q, to):
        return _remote(self.g.at[2 * q + 1 - self.c], self.rcv.at[q], self.sa.at[q], self.ra.at[q], to)

    def _send(self, k, to):
        dst = self.cbuf.at[0] if k == 1 else self.rbuf.at[0 if k == 0 else 1]
        return _remote(self.sbuf.at[k], dst, self.sb.at[k], self.rb.at[k], to)

    def _stage(self, k, which, extra=None):
        def cast(r):
            v = self.rcv[which, r, :]
            if extra is not None:
                v = v + extra[0, r, :].astype(F32)
            self.sbuf[k, r, :] = v.astype(BF16)

        _rows_loop(self.nrow, cast)

    @staticmethod
    def _q(chip):
        return 2 * chip[0] + chip[1]

    def start(self):
        for q in range(4):
            self._load(q).start()
            self._to_sib(q, self.sib).start()

    def mid(self):
        for q in range(4):
            self._load(q).wait()
            self._to_sib(q, self.here).wait_recv()

        def add(r):
            for q in range(4):
                self.rcv[q, r, :] = self.rcv[q, r, :] + self.own[q, r, :]

        _rows_loop(self.nrow, add)
        to_first = (self.first[0], self.first[1], self.c)
        self._stage(0, self._q(self.first))
        self._send(0, to_first).start()
        self._stage(1, self._q(self.far))
        self._send(1, to_first).start()

    def pass_on(self):
        self._send(1, self.here).wait_recv()
        self._stage(2, self._q(self.second), extra=self.cbuf)
        self._send(2, (self.second[0], self.second[1], self.c)).start()

    def finish(self, out):
        self._send(0, self.here).wait_recv()
        self._send(2, self.here).wait_recv()
        which = 2 * self.x + self.y

        def tot(r):
            out[r, :] = (self.rcv[which, r, :] + self.rbuf[0, r, :].astype(F32)) + self.rbuf[1, r, :].astype(F32)

        _rows_loop(self.nrow, tot)
        for q in range(4):
            self._to_sib(q, self.sib).wait_send()
        to_first = (self.first[0], self.first[1], self.c)
        self._send(0, to_first).wait_send()
        self._send(1, to_first).wait_send()
        self._send(2, (self.second[0], self.second[1], self.c)).wait_send()


def _reduce_scratch(shape):
    return [pltpu.VMEM((4,) + shape, F32), pltpu.VMEM((4,) + shape, F32),
            pltpu.VMEM((3,) + shape, BF16), pltpu.VMEM((2,) + shape, BF16), pltpu.VMEM((1,) + shape, BF16),
            pltpu.SemaphoreType.DMA((4,)), pltpu.SemaphoreType.DMA((4,)), pltpu.SemaphoreType.DMA((4,)),
            pltpu.SemaphoreType.DMA((3,)), pltpu.SemaphoreType.DMA((3,))]


_N_RED = 10

_S_LAYOUT = (((1, D_MODEL), 0), ((1, D_MODEL), 8), ((1, D_MODEL), 16),
             ((1, A_WIDTH), 24), ((1, A_WIDTH), 28), ((A_GROUPS, CHUNK), 32),
             ((1, 4), 36), ((N_BUCKETS, 4), 40),
             ((A_GROUPS * CHUNK, CHUNK), 72))
_LOSS_ROW = 37
_W_SP_ROW = _S_LAYOUT[-1][1]
_S_ROWS = _W_SP_ROW + A_GROUPS * CHUNK
_N_SMALL = len(_S_LAYOUT)


def _pack_rows(dst, refs):
    for (shp, r0), ref in zip(_S_LAYOUT, refs):
        if shp[0] == 1 and shp[1] >= CHUNK:
            for i in range(shp[1] // CHUNK):
                dst[r0 + i:r0 + i + 1, :] = ref[:, i * CHUNK:(i + 1) * CHUNK]
        elif ref.shape[-1] == CHUNK:
            dst[r0:r0 + shp[0], :] = ref[0:shp[0], :]
        else:
            dst[r0:r0 + shp[0], 0:shp[1]] = ref[...]


def _unpack_rows(src, refs):
    for (shp, r0), ref in zip(_S_LAYOUT, refs):
        if shp[0] == 1 and shp[1] >= CHUNK:
            for i in range(shp[1] // CHUNK):
                ref[:, i * CHUNK:(i + 1) * CHUNK] = src[r0 + i:r0 + i + 1, :]
        elif shp[1] == CHUNK:
            ref[...] = src[r0:r0 + shp[0], :]
        else:
            if tuple(ref.shape) == (shp[1], shp[0]):
                ref[...] = src[r0:r0 + CHUNK, :].T[0:shp[1], 0:shp[0]]
            else:
                ref[...] = src[r0:r0 + shp[0], 0:shp[1]]


_MEM_G = 2


def _greduce(ga, gb, dmkv, mem2, gm, w_mkv, small_g, loss_p):
    shp_c = (SHARD_O, 2 * MEM_LEN)
    shapes = (shp_c, gb.shape[1:], ga.shape[1:])
    rs = _S_ROWS

    def body(*refs):
        it = iter(refs)
        take = lambda n: [next(it) for _ in range(n)]
        gb_ref, ga_ref, d_ref, m_ref, gm_ref, wm_ref = take(6)
        sg_refs = take(_N_SMALL - 1)
        loss_ref, = take(1)
        oc, ob, oa, ogs = take(4)
        red = take(3 * _N_RED)
        gs_ref, rs_a, rs_b, rs_w, gc_ref, dgm_ref = take(6)
        ssem_a, rsem_a, ssem_b, rsem_b = take(4)

        pos = _position()
        x, y, cc = pos
        myq = 2 * x + y
        here, sib = (x, y, cc), (x, y, 1 - cc)
        chips = _other_chips(x, y)
        reducers = [_ShardReduce(pos, g, red[k * _N_RED:k * _N_RED + 5], red[k * _N_RED + 5:(k + 1) * _N_RED])
                    for k, g in enumerate((gc_ref, gb_ref, ga_ref))]
        for rd in reducers[1:]:
            rd.start()

        xf = m_ref[...]
        nm = xf * _rms(xf)
        hm = (nm * gm_ref[...]).astype(MM)
        d = d_ref[...].astype(MM)
        for o in range(N_DEV):
            gc_ref[o] = _dot_tn(hm[:, o * SHARD_O:(o + 1) * SHARD_O], d)
        dgm_ref[...] = jnp.sum(_dot_nt(d, wm_ref[...]) * nm, axis=0, keepdims=True)
        reducers[0].start()

        gs_ref[...] = jnp.zeros_like(gs_ref)
        _pack_rows(gs_ref, sg_refs[:_MEM_G] + [dgm_ref] + sg_refs[_MEM_G:])
        gs_ref[_LOSS_ROW:_LOSS_ROW + 1, :] = loss_ref[0:1, :]
        small_a = _remote(gs_ref, rs_a, ssem_a, rsem_a, sib)
        small_a.start()

        _remote(gs_ref, rs_a, ssem_a, rsem_a, here).wait_recv()
        rs_b[myq] = gs_ref[0:_W_SP_ROW, :] + rs_a[0:_W_SP_ROW, :]
        rs_w[myq] = (gs_ref[_W_SP_ROW:rs, :] + rs_a[_W_SP_ROW:rs, :]).astype(BF16)
        small_b = []
        for j, chip in enumerate(chips):
            to = (chip[0], chip[1], cc)
            small_b.append(_remote(rs_b.at[myq], rs_b.at[myq], ssem_b.at[0, j], rsem_b.at[0, j], to))
            small_b.append(_remote(rs_w.at[myq], rs_w.at[myq], ssem_b.at[1, j], rsem_b.at[1, j], to))
        for cp in small_b:
            cp.start()
        late_last = reducers[1:] + reducers[:1]
        for rd in late_last:
            rd.mid()
        for rd in late_last:
            rd.pass_on()

        for j in range(3):
            _remote(rs_b.at[myq], rs_b.at[myq], ssem_b.at[0, j], rsem_b.at[0, j], here).wait_recv()
            _remote(rs_w.at[myq], rs_w.at[myq], ssem_b.at[1, j], rsem_b.at[1, j], here).wait_recv()
        ogs[0:_W_SP_ROW, :] = ((rs_b[0] + rs_b[1]) + rs_b[2]) + rs_b[3]

        def tot_w(r):
            w = [rs_w[q, r, :].astype(F32) for q in range(4)]
            ogs[pl.ds(pl.multiple_of(_W_SP_ROW + r.start, 8), _ROWS), :] = ((w[0] + w[1]) + w[2]) + w[3]

        _rows_loop(rs - _W_SP_ROW, tot_w)
        for rd, out in zip(late_last, (ob, oa, oc)):
            rd.finish(out)
        small_a.wait_send()
        for cp in small_b:
            cp.wait_send()

    vm = pl.BlockSpec(memory_space=pltpu.VMEM)
    anyspec = pl.BlockSpec(memory_space=pl.ANY)
    scratch = []
    for shp in shapes:
        scratch += _reduce_scratch(shp)
    scratch += [pltpu.VMEM((rs, CHUNK), F32), pltpu.VMEM((rs, CHUNK), F32),
                pltpu.VMEM((4, _W_SP_ROW, CHUNK), F32), pltpu.VMEM((4, rs - _W_SP_ROW, CHUNK), BF16),
                pltpu.VMEM((N_DEV,) + shp_c, F32), pltpu.VMEM((1, D_MODEL), F32),
                pltpu.SemaphoreType.DMA, pltpu.SemaphoreType.DMA,
                pltpu.SemaphoreType.DMA((2, 3)), pltpu.SemaphoreType.DMA((2, 3))]
    tc, tb, ta, ts = pl.pallas_call(
        body, name="greduce",
        out_shape=tuple([jax.ShapeDtypeStruct(shp, F32) for shp in shapes] + [jax.ShapeDtypeStruct((rs, CHUNK), F32)]),
        in_specs=[anyspec] * 2 + [vm] * (4 + _N_SMALL),
        out_specs=(vm, vm, vm, vm),
        scratch_shapes=scratch,
        compiler_params=_params(),
    )(gb, ga, dmkv, mem2, gm, w_mkv, *small_g, loss_p)
    return ta, tb, tc, ts


def _adamw(w, g, m, v):
    m = ADAM_B1 * m + (1.0 - ADAM_B1) * g
    v = ADAM_B2 * v + (1.0 - ADAM_B2) * (g * g)
    m_hat = m / (1.0 - ADAM_B1 ** ADAM_STEP)
    v_hat = v / (1.0 - ADAM_B2 ** ADAM_STEP)
    delta = -ADAM_LR * (m_hat / (jnp.sqrt(v_hat) + ADAM_EPS) + ADAM_WD * w)
    return delta, m, v


def _update(ta, tb, tc, ts, big_wmv, small_wmv):
    shapes = (ta.shape, tb.shape, tc.shape)
    rs = _S_ROWS
    small_shapes = [tuple(a.shape) for a in small_wmv[0]]

    def body(*refs):
        it = iter(refs)
        take = lambda n: [next(it) for _ in range(n)]
        ga_ref, gb_ref, gc_ref, gs_ref = take(4)
        wa, ma, va, wb, mb, vb_, wc, mc, vc = take(9)
        sw_refs, sm_refs, sv_refs = take(_N_SMALL), take(_N_SMALL), take(_N_SMALL)
        oga, oda, oma, ova, ogb, odb, omb, ovb, ogc, odc, omc, ovc = take(12)
        so_refs = [take(_N_SMALL) for _ in range(4)]
        loss_out, = take(1)
        ws, ms, vs, ods, oms, ovs = take(6)

        def update_rows(nrow, rows, g_r, w_r, m_r, v_r, og, od, om, ov):
            def upd(r):
                g = g_r[r, :]
                d, m, v = _adamw(w_r[r, :], g, m_r[r, :], v_r[r, :])
                og[r, :] = g
                od[r, :] = d
                om[r, :] = m
                ov[r, :] = v

            _rows_loop(nrow, upd, rows)

        update_rows(a_rows, 16, ga_ref, wa, ma, va, oga, oda, oma, ova)

        @pl.when(pl.program_id(0) == 0)
        def _():
            update_rows(shapes[1][0], _ROWS, gb_ref, wb, mb, vb_, ogb, odb, omb, ovb)
            update_rows(shapes[2][0], _ROWS, gc_ref, wc, mc, vc, ogc, odc, omc, ovc)
            for buf in (ws, ms, vs):
                buf[...] = jnp.zeros_like(buf)
            _pack_rows(ws, sw_refs)
            _pack_rows(ms, sm_refs)
            _pack_rows(vs, sv_refs)

            def upd_s(i, _):
                r = pl.ds(pl.multiple_of(i * 8, 8), 8)
                d, m, v = _adamw(ws[r, :], gs_ref[r, :], ms[r, :], vs[r, :])
                ods[r, :] = d
                oms[r, :] = m
                ovs[r, :] = v
                return 0

            lax.fori_loop(0, rs // 8, upd_s, 0)
            for k, buf in enumerate((gs_ref, ods, oms, ovs)):
                _unpack_rows(buf, so_refs[k])
            loss_out[...] = gs_ref[_LOSS_ROW:_LOSS_ROW + 1, 0:1]

    n_blocks = 2
    a_rows = shapes[0][0] // n_blocks
    a_spec = pl.BlockSpec((a_rows, shapes[0][1]), lambda i: (i, 0))
    big_out, big_out_specs = [], []
    for shp in shapes:
        big_out += [jax.ShapeDtypeStruct(shp, F32)] * 4
        big_out_specs += [a_spec if shp == shapes[0] else _full(shp)] * 4
    small_out_shapes = [shp[::-1] if shp == (N_BUCKETS, 4) else shp for shp in small_shapes] * 4
    small_out = [jax.ShapeDtypeStruct(shp, F32) for shp in small_out_shapes]
    out_shape = tuple(big_out + small_out + [jax.ShapeDtypeStruct((1, 1), F32)])
    in_specs = ([a_spec, _full(shapes[1]), _full(shapes[2]), _full((rs, CHUNK))]
                + [a_spec] * 3 + [_full(shapes[1])] * 3 + [_full(shapes[2])] * 3
                + [_full(shp) for shp in small_shapes] * 3)
    return pl.pallas_call(
        body, name="update", grid=(n_blocks,),
        out_shape=out_shape,
        in_specs=in_specs,
        out_specs=tuple(big_out_specs + [_full(shp) for shp in small_out_shapes] + [_full((1, 1))]),
        scratch_shapes=[pltpu.VMEM((rs, CHUNK), F32) for _ in range(6)],
        compiler_params=_params(dimension_semantics=("arbitrary",)),
    )(ta, tb, tc, ts, *big_wmv, *small_wmv[0], *small_wmv[1], *small_wmv[2])


def kernel(x, mem, pre_norm_g, post_norm_g, mem_norm_g, w_in, w_mem_kv, v_norm_g, v_norm_b, w_spatial, b_spatial, attn_sinks, rel_bias, w_out, loss_target, m_pre_norm_g, m_post_norm_g, m_mem_norm_g, m_w_in, m_w_mem_kv, m_v_norm_g, m_v_norm_b, m_w_spatial, m_b_spatial, m_attn_sinks, m_rel_bias, m_w_out, v_pre_norm_g, v_post_norm_g, v_mem_norm_g, v_w_in, v_w_mem_kv, v_v_norm_g, v_v_norm_b, v_w_spatial, v_b_spatial, v_attn_sinks, v_rel_bias, v_w_out):
    sh_a = (w_in[0].T, m_w_in[0].T, v_w_in[0].T)
    sh_b = (w_out[0], m_w_out[0], v_w_out[0])
    sh_c = (w_mem_kv[0], m_w_mem_kv[0], v_w_mem_kv[0])
    nb, s, _ = x.shape
    t = nb * s
    x2 = x.reshape(t, D_MODEL)
    tgt2 = loss_target.reshape(t, D_MODEL)
    mem2 = mem.reshape(nb * MEM_LEN, D_MODEL)
    buckets = jnp.asarray(_t5_buckets())

    wa, wb, wc, bias, wt, wtt, bcol, mkv = _wgather(sh_a[0], sh_b[0], sh_c[0], rel_bias, w_spatial[0], b_spatial[0],
                                                    buckets, mem2, mem_norm_g)
    w_mkv = wc.reshape(D_MODEL, 2 * MEM_LEN)
    gx, dmkv, dwi, dwo, dg1, dg2, loss_p, dwsp, dbs, dvg, dvb, dsink, drel = _layer(
        x2, tgt2, mkv.reshape(nb, MEM_LEN, 2 * MEM_LEN), bias, attn_sinks.reshape(4), v_norm_g, v_norm_b, wt, wtt, bcol,
        pre_norm_g, post_norm_g, wa.reshape(IN_WIDTH, D_MODEL), wb.reshape(D_MODEL, D_MODEL), buckets,
        nb, s, min(256, s))
    gx = gx.reshape(nb, s, D_MODEL)
    small_grads = [dg1, dg2, dvg, dvb, dbs, dsink, drel, dwsp.reshape(A_GROUPS * CHUNK, CHUNK)]

    small_names = ["pre_norm_g", "post_norm_g", "mem_norm_g", "v_norm_g", "v_norm_b", "b_spatial", "attn_sinks",
                   "rel_bias", "w_spatial"]
    given = dict(pre_norm_g=(pre_norm_g, m_pre_norm_g, v_pre_norm_g), post_norm_g=(post_norm_g, m_post_norm_g, v_post_norm_g),
                 mem_norm_g=(mem_norm_g, m_mem_norm_g, v_mem_norm_g), v_norm_g=(v_norm_g, m_v_norm_g, v_v_norm_g),
                 v_norm_b=(v_norm_b, m_v_norm_b, v_v_norm_b), b_spatial=(b_spatial, m_b_spatial, v_b_spatial),
                 attn_sinks=(attn_sinks, m_attn_sinks, v_attn_sinks), rel_bias=(rel_bias, m_rel_bias, v_rel_bias),
                 w_spatial=(w_spatial, m_w_spatial, v_w_spatial))
    small_wmv = [[given[n][k].reshape(shp) for n, (shp, _) in zip(small_names, _S_LAYOUT)] for k in range(3)]

    ta, tb, tc, ts = _greduce(dwi.reshape(N_DEV, SHARD_IN, D_MODEL), dwo.reshape(N_DEV, SHARD_O, D_MODEL),
                              dmkv.reshape(nb * MEM_LEN, 2 * MEM_LEN), mem2, mem_norm_g, w_mkv, small_grads, loss_p)
    outs = _update(ta, tb, tc, ts, (*sh_a, *sh_b, *sh_c), small_wmv)
    ra, rb, rc = outs[0:4], outs[4:8], outs[8:12]
    loss = outs[12 + 4 * _N_SMALL].reshape(())

    res = {}
    for k, kind in enumerate(("grad", "delta", "new_m", "new_v")):
        res[kind, "w_in"] = ra[k].T[None]
        res[kind, "w_out"] = rb[k][None]
        res[kind, "w_mem_kv"] = rc[k][None]
        for i, n in enumerate(small_names):
            o = outs[12 + k * _N_SMALL + i]
            res[kind, n] = o.T if n == "rel_bias" else o.reshape(given[n][0].shape)
    order = ["pre_norm_g", "post_norm_g", "mem_norm_g", "w_in", "w_mem_kv", "v_norm_g", "v_norm_b", "w_spatial",
             "b_spatial", "attn_sinks", "rel_bias", "w_out"]
    flat = [res[kind, n] for kind in ("grad", "delta", "new_m", "new_v") for n in order]
    return (loss, gx, *flat)
```

```python
import numpy as np
import jax
import jax.numpy as jnp
from jax import lax
from jax.experimental import pallas as pl
from jax.experimental.pallas import tpu as pltpu

F32 = jnp.float32
BF16 = jnp.bfloat16
MM = jnp.bfloat16

D_MODEL = 1024
CHUNK = 128
A_GROUPS = 4
A_WIDTH = 512
UV_W = 1024
QKV_W = 768
Z_W = 1024
IN_WIDTH = UV_W + QKV_W + Z_W
MEM_LEN = 256
N_BUCKETS = 32
MAX_DISTANCE = 128
EPS = 1e-6
NEG = -1e30
SCALE = 0.125
N_DEV = 8
SHARD_IN = IN_WIDTH // N_DEV
SHARD_O = D_MODEL // N_DEV

SQ_COL, SK_COL, SV_COL, MQ_COL, Z_COL = UV_W, UV_W + 256, UV_W + 384, UV_W + 512, UV_W + QKV_W
DW_PIECES = ((0, SQ_COL), (SQ_COL, Z_COL), (Z_COL, IN_WIDTH))
YB_OFF, YC_OFF = 512, 768

ADAM_LR = 0.001
ADAM_B1 = 0.9
ADAM_B2 = 0.999
ADAM_EPS = 1e-08
ADAM_WD = 0.01
ADAM_STEP = 10

VMEM_LIMIT = 60 * 1024 * 1024

_GELU_C = 0.7978845608028654
_GELU_A = 0.044715

MESH = pl.DeviceIdType.MESH
_ROWS = 32


def _dot(a, b):
    return lax.dot_general(a, b, (((1,), (0,)), ((), ())), preferred_element_type=F32)


def _dot_nt(a, b):
    return lax.dot_general(a, b, (((1,), (1,)), ((), ())), preferred_element_type=F32)


def _dot_tn(a, b):
    return lax.dot_general(a, b, (((0,), (0,)), ((), ())), preferred_element_type=F32)


def _gelu_and_grad(x):
    x2 = x * x
    t = jnp.tanh(x * (_GELU_C + (_GELU_C * _GELU_A) * x2))
    w = 0.5 * t + 0.5
    g = x * w
    dg = w * (1.0 + (x - g) * ((2.0 * _GELU_C) + (6.0 * _GELU_C * _GELU_A) * x2))
    return g, dg


def _t5_buckets():
    qi = np.arange(CHUNK)[:, None]
    kj = np.arange(2 * CHUNK)[None, :]
    n = np.maximum(qi + CHUNK - kj, 0)
    max_exact = N_BUCKETS // 2
    large = max_exact + (np.log(np.maximum(n, 1) / max_exact) / np.log(MAX_DISTANCE / max_exact)
                         * (N_BUCKETS - max_exact)).astype(np.int32)
    large = np.minimum(large, N_BUCKETS - 1)
    return np.where(n < max_exact, n, large).astype(np.int32)


def _params(**kw):
    return pltpu.CompilerParams(vmem_limit_bytes=VMEM_LIMIT, **kw)


def _full(shape, single=False):
    nd = len(shape)
    if single:
        return pl.BlockSpec(shape, lambda *_: (0,) * nd, pipeline_mode=pl.Buffered(1))
    return pl.BlockSpec(shape, lambda *_: (0,) * nd)


def _window_valid():
    qi = lax.broadcasted_iota(jnp.int32, (CHUNK, 2 * CHUNK), 0)
    kj = lax.broadcasted_iota(jnp.int32, (CHUNK, 2 * CHUNK), 1)
    dist = qi + CHUNK - kj
    return (dist >= 0) & (dist < CHUNK)


def _position():
    return lax.axis_index("x"), lax.axis_index("y"), lax.axis_index("c")


def _other_chips(x, y):
    return [(1 - x, y), (x, 1 - y), (1 - x, 1 - y)]


def _route(x, y, c):
    first = (x * c + (1 - x) * (1 - c), y * (1 - c) + (1 - y) * c)
    second = (x * (1 - c) + (1 - x) * c, y * c + (1 - y) * (1 - c))
    return first, second, (1 - x, 1 - y)


def _remote(src, dst, ssem, rsem, to):
    return pltpu.make_async_remote_copy(src_ref=src, dst_ref=dst, send_sem=ssem, recv_sem=rsem,
                                        device_id=to, device_id_type=MESH)


def _rows_loop(nrow, fn, rows=_ROWS):
    assert nrow % rows == 0

    def step(i, _):
        fn(pl.ds(pl.multiple_of(i * rows, rows), rows))
        return 0

    lax.fori_loop(0, nrow // rows, step, 0)


class _Gather:
    def __init__(self, pos, out, ssem, rsem):
        self.x, self.y, self.c = pos
        self.out, self.ssem, self.rsem = out, ssem, rsem
        self.me = 4 * self.x + 2 * self.y + self.c
        self.here = (self.x, self.y, self.c)
        self.sib = (self.x, self.y, 1 - self.c)
        self.first, self.second, self.far = _route(*pos)

    def _copy(self, k, blk, to):
        r = self.out.at[blk]
        return _remote(r, r, self.ssem.at[k], self.rsem.at[k], to)

    def _idx(self, chip, core):
        return 4 * chip[0] + 2 * chip[1] + core

    def _on(self, chip):
        return (chip[0], chip[1], self.c)

    def start(self):
        self._copy(0, self.me, self.sib).start()
        self._copy(1, self.me, self._on(self.first)).start()
        self._copy(2, self.me, self._on(self.second)).start()

    def forward(self):
        c = self.c
        self._copy(1, self._idx(self.first, c), self.here).wait_recv()
        self._copy(3, self._idx(self.first, c), self._on(self.second)).start()
        self._copy(4, self._idx(self.first, c), self.sib).start()
        self._copy(2, self._idx(self.second, c), self.here).wait_recv()
        self._copy(5, self._idx(self.second, c), self.sib).start()
        self._copy(3, self._idx(self.far, c), self.here).wait_recv()
        self._copy(6, self._idx(self.far, c), self.sib).start()

    def finish(self):
        c = self.c
        self._copy(0, self._idx((self.x, self.y), 1 - c), self.here).wait_recv()
        for k, chip in ((4, self.second), (5, self.first), (6, self.far)):
            self._copy(k, self._idx(chip, 1 - c), self.here).wait_recv()
        self._copy(0, self.me, self.sib).wait_send()
        self._copy(1, self.me, self._on(self.first)).wait_send()
        self._copy(2, self.me, self._on(self.second)).wait_send()
        self._copy(3, self._idx(self.first, c), self._on(self.second)).wait_send()
        for k, chip in ((4, self.first), (5, self.second), (6, self.far)):
            self._copy(k, self._idx(chip, c), self.sib).wait_send()


def _prep_tables(rb_ref, w_ref, b_ref, bk_ref, bias_ref, wt_ref, wtt_ref, bcol_ref):
    valid = _window_valid()
    bk = bk_ref[...]
    acc = [jnp.full((CHUNK, 2 * CHUNK), NEG, F32) for _ in range(4)]
    for b in range(N_BUCKETS):
        hit = (bk == b) & valid
        for h in range(4):
            acc[h] = jnp.where(hit, rb_ref[b, h], acc[h])
    for h in range(4):
        bias_ref[h] = acc[h]
    r = lax.broadcasted_iota(jnp.int32, (CHUNK, CHUNK), 0)
    c = lax.broadcasted_iota(jnp.int32, (CHUNK, CHUNK), 1)
    for g in range(A_GROUPS):
        w = jnp.where(r >= c, w_ref[g], 0.0)
        wt_ref[g] = w.astype(MM)
        wtt_ref[g] = w.T.astype(MM)
        bcol_ref[g] = jnp.broadcast_to(b_ref[g:g + 1, :], (CHUNK, CHUNK)).T


def _wgather(a, b, c, rel_bias, w_sp, b_sp, buckets, mem2, gm):
    tmem = mem2.shape[0]

    def body(a_ref, b_ref, c_ref, rb_ref, w_ref, bsp_ref, bk_ref, m_ref, gm_ref,
             oa, ob, oc, bias_ref, wt_ref, wtt_ref, bcol_ref, mkv_ref, ssem, rsem):
        pos = _position()
        me = 4 * pos[0] + 2 * pos[1] + pos[2]
        gathers = []
        for k, (src, out) in enumerate(((c_ref, oc), (b_ref, ob), (a_ref, oa))):
            out[me] = src[...].astype(BF16)
            g = _Gather(pos, out, ssem.at[k], rsem.at[k])
            g.start()
            gathers.append(g)
        _prep_tables(rb_ref, w_ref, bsp_ref, bk_ref, bias_ref, wt_ref, wtt_ref, bcol_ref)
        for g in gathers:
            g.forward()
        gathers[0].finish()
        xf = m_ref[...]
        hm = (xf * _rms(xf) * gm_ref[...]).astype(MM)
        acc = jnp.zeros((tmem, 2 * MEM_LEN), F32)
        for d in range(N_DEV):
            acc = acc + _dot(hm[:, d * SHARD_O:(d + 1) * SHARD_O], oc[d])
        mkv_ref[...] = acc.astype(MM)
        for g in gathers[1:]:
            g.finish()

    vm = pl.BlockSpec(memory_space=pltpu.VMEM)
    grp = (A_GROUPS, CHUNK, CHUNK)
    return pl.pallas_call(
        body, name="wgather",
        out_shape=(jax.ShapeDtypeStruct((N_DEV,) + a.shape, BF16),
                   jax.ShapeDtypeStruct((N_DEV,) + b.shape, BF16),
                   jax.ShapeDtypeStruct((N_DEV,) + c.shape, BF16),
                   jax.ShapeDtypeStruct((4, CHUNK, 2 * CHUNK), F32),
                   jax.ShapeDtypeStruct(grp, MM), jax.ShapeDtypeStruct(grp, MM), jax.ShapeDtypeStruct(grp, F32),
                   jax.ShapeDtypeStruct((tmem, 2 * MEM_LEN), MM)),
        in_specs=[vm, vm, vm, pl.BlockSpec(memory_space=pltpu.SMEM), vm, vm, vm, vm, vm],
        out_specs=tuple([vm] * 8),
        scratch_shapes=[pltpu.SemaphoreType.DMA((3, 7)), pltpu.SemaphoreType.DMA((3, 7))],
        compiler_params=_params(),
    )(a, b, c, rel_bias, w_sp, b_sp, buckets, mem2, gm)


def _half_masks(rows):
    lane = lax.broadcasted_iota(jnp.int32, (rows, CHUNK), 1)
    return lane < 64


def _dup_heads(band):
    b32 = band.astype(F32)
    rolled = pltpu.roll(b32, 64, 1)
    lo = _half_masks(band.shape[0])
    return (jnp.where(lo, b32, rolled).astype(MM), jnp.where(lo, rolled, b32).astype(MM))


def _swa_probs(qk, bias_h, sink_h, first_add):
    s = qk * SCALE + bias_h + first_add
    m = jnp.maximum(jnp.max(s, axis=-1, keepdims=True), sink_h)
    p = jnp.exp(s - m)
    es = jnp.exp(sink_h - m)
    inv = 1.0 / (jnp.sum(p, axis=-1, keepdims=True) + es)
    return p * inv, es * inv


def _softmax(s):
    m = jnp.max(s, axis=-1, keepdims=True)
    p = jnp.exp(s - m)
    return p * (1.0 / jnp.sum(p, axis=-1, keepdims=True))


def _first_block_mask(n):
    col = lax.broadcasted_iota(jnp.int32, (2 * CHUNK, 2 * CHUNK), 1)
    return jnp.where((col < CHUNK) & (n == 0), NEG, 0.0)


def _stack_heads(x128, lo):
    return jnp.concatenate([jnp.where(lo, x128, 0.0), jnp.where(lo, 0.0, x128)], axis=0).astype(MM)


def _rms(xf):
    return lax.rsqrt(jnp.mean(xf * xf, axis=-1, keepdims=True) + EPS)


def _layer(x2, tgt2, mkv3, bias, sinks, vg, vb, wt, wtt, bcol, g1, g2, w_in_t, w_o, buckets, nb, s, tm):
    nt = s // tm
    bpt = tm // CHUNK
    bps = s // CHUNK
    t = nb * s
    last_step = nb * nt - 1

    def tile_at(step):
        return (step // nt) * nt + nt - 1 - step % nt

    def block_before(step):
        return (step // nt) * bps + jnp.maximum((nt - 1 - step % nt) * bpt - 1, 0)

    def body(x_ref, xp_ref, xn_ref, xpn_ref, t_ref, mkv_ref, bias_ref, sink_ref, vg_ref, vb_ref,
             wt_ref, wtt_ref, bcol_ref, g1_ref, g2_ref, wi_ref, wo_ref, bk_ref,
             gx_ref, dmkv_ref, dwi_hbm, dwo_hbm, dg1_ref, dg2_ref, loss_ref, dwsp_ref, dbs_ref,
             dvg_ref, dvb_ref, dsink_ref, drel_ref,
             acc_i, acc_o, uv_s, z_s, q_s, kv_s, h_s, hp_s, dp_s, dxo_s, dh_s, r_s,
             ycat, dyc, u_s, gu_s, gv_s, xh_s, vc_s, pb_s, ps_s, pc_s, kd_s, vd_s,
             dkv_acc, dbias_acc, dsv_acc, dsink_acc, sems):
        b, j = pl.program_id(0), pl.program_id(1)
        jt = nt - 1 - j
        step = b * nt + j
        g1v = g1_ref[...]
        NOW, NEXT, DONE = 0, 1, 2
        dw_cols = list(DW_PIECES)

        def weight_grad(n, slot):
            for c0, c1 in dw_cols[:n]:
                acc_i[c0:c1, :] += _dot_tn(dp_s[:, c0:c1], h_s[slot])
            del dw_cols[:n]

        def pre_norm(x_tile, x_before):
            xf = x_tile[...]
            r_s[NEXT] = _rms(xf)
            h_s[NEXT] = (xf * r_s[NEXT] * g1v).astype(MM)
            xp = x_before[...]
            hp_s[...] = (xp * _rms(xp) * g1v).astype(MM)

        def project_z():
            z_s[...] = _dot_nt(h_s[NEXT], wi_ref[Z_COL:IN_WIDTH, :])

        def project_uv():
            uv_s[...] = _dot_nt(h_s[NEXT], wi_ref[0:UV_W, :])

        @pl.when(step == 0)
        def _():
            for ref in (acc_i, acc_o, dg1_ref, dg2_ref, loss_ref, dwsp_ref, dvg_ref, dvb_ref,
                        dbias_acc, dsv_acc, dsink_acc):
                ref[...] = jnp.zeros_like(ref)
            dp_s[...] = jnp.zeros_like(dp_s)
            h_s[NOW] = jnp.zeros((tm, D_MODEL), MM)
            pre_norm(x_ref, xp_ref)
            project_z()
            project_uv()

        h_s[DONE] = h_s[NOW]
        r_s[NOW] = r_s[NEXT]
        h = h_s[NEXT]
        h_s[NOW] = h
        hp = hp_s[...]

        @pl.when(j == 0)
        def _():
            dmkv_ref[...] = jnp.zeros_like(dmkv_ref)
            dkv_acc[...] = jnp.zeros_like(dkv_acc)

        carry = dkv_acc[0:CHUNK, :]
        dkv_acc[...] = jnp.zeros_like(dkv_acc)
        dkv_acc[tm:tm + CHUNK, :] = carry

        lo = _half_masks(CHUNK)
        lob = _half_masks(2 * CHUNK)
        lot = _half_masks(tm)

        qkv = _dot_nt(h, wi_ref[SQ_COL:Z_COL, :])
        q_s[:, 0:256] = qkv[:, 0:256].astype(MM)
        q_s[:, 256:512] = qkv[:, 512:768].astype(MM)
        kv_s[CHUNK:CHUNK + tm, :] = qkv[:, 256:512].astype(MM)
        kv_s[0:CHUNK, :] = _dot_nt(hp, wi_ref[SK_COL:MQ_COL, :]).astype(MM)

        weight_grad(1, DONE)
        b_qk, b_pb = [], []
        for blk in range(bpt):
            r0 = blk * CHUNK
            rows = slice(r0, r0 + CHUNK)
            for g in range(A_GROUPS):
                cg = slice(g * CHUNK, (g + 1) * CHUNK)
                u, gu = _gelu_and_grad(uv_s[rows, cg])
                v, gv = _gelu_and_grad(uv_s[rows, A_WIDTH + g * CHUNK:A_WIDTH + (g + 1) * CHUNK])
                mu = jnp.mean(v, axis=-1, keepdims=True)
                xc = v - mu
                rstd = lax.rsqrt(jnp.mean(xc * xc, axis=-1, keepdims=True) + EPS)
                xhat = xc * rstd
                vc = (xhat * vg_ref[:, cg] + vb_ref[:, cg]).astype(MM)
                sv = _dot(wt_ref[g], vc) + bcol_ref[g]
                u_s[rows, cg] = u
                gu_s[rows, cg] = sv * gu
                gv_s[rows, cg] = rstd * gv
                xh_s[rows, cg] = xhat
                vc_s[rows, cg] = vc
                ycat[rows, cg] = u * sv
            weight_grad(1, DONE)
            kd = _dup_heads(kv_s[r0:r0 + 2 * CHUNK, 0:CHUNK])
            vd = _dup_heads(kv_s[r0:r0 + 2 * CHUNK, CHUNK:2 * CHUNK])
            for kvh in range(2):
                kd_s[blk * 2 + kvh] = kd[kvh]
                vd_s[blk * 2 + kvh] = vd[kvh]
                q2 = _stack_heads(q_s[rows, kvh * CHUNK:(kvh + 1) * CHUNK].astype(F32), lo)
                b_qk.append(_dot_nt(q2, kd[kvh]))
        qks, pcs = [], []
        for g in range(2):
            q128 = q_s[:, 256 + g * CHUNK:256 + (g + 1) * CHUNK].astype(F32)
            for hh in range(2):
                qsel = jnp.where(lot if hh == 0 else ~lot, q128, 0.0).astype(MM)
                qks.append(_dot_nt(qsel, mkv_ref[:, g * CHUNK:(g + 1) * CHUNK]))
        weight_grad(2, DONE)
        top = lax.broadcasted_iota(jnp.int32, (2 * CHUNK, 1), 0) < CHUNK
        for blk in range(bpt):
            first_add = _first_block_mask(jt * bpt + blk)
            for kvh in range(2):
                sink2 = jnp.where(top, sink_ref[2 * kvh], sink_ref[2 * kvh + 1])
                probs, ps = _swa_probs(b_qk[blk * 2 + kvh], bias_ref[kvh], sink2, first_add)
                pb_s[blk * 2 + kvh] = probs
                ps_s[blk * 2 + kvh] = jnp.broadcast_to(ps, (2 * CHUNK, CHUNK))
                b_pb.append(probs.astype(MM))
        weight_grad(len(dw_cols), DONE)
        for hd in range(4):
            probs = _softmax(qks[hd] * SCALE)
            pc_s[hd] = probs
            pcs.append(probs.astype(MM))
        for blk in range(bpt):
            rows = slice(blk * CHUNK, (blk + 1) * CHUNK)
            for kvh in range(2):
                out2 = _dot(b_pb[blk * 2 + kvh], vd_s[blk * 2 + kvh])
                ycat[rows, YB_OFF + kvh * CHUNK:YB_OFF + (kvh + 1) * CHUNK] = jnp.where(
                    lo, out2[0:CHUNK], out2[CHUNK:2 * CHUNK])
        outs = [_dot(pcs[hd], mkv_ref[:, MEM_LEN + (hd // 2) * CHUNK:MEM_LEN + (hd // 2 + 1) * CHUNK])
                for hd in range(4)]
        for g in range(2):
            ycat[:, YC_OFF + g * CHUNK:YC_OFF + (g + 1) * CHUNK] = jnp.where(lot, outs[2 * g], outs[2 * g + 1])

        zt = z_s[...]
        sig = 1.0 / (1.0 + jnp.exp(-zt))
        silu = zt * sig
        yc = ycat[...]
        yb = (yc * silu).astype(MM)
        pre_norm(xn_ref, xpn_ref)
        o = _dot(yb, wo_ref[...])
        project_z()
        r2 = _rms(o)
        nrm = o * r2
        g2v = g2_ref[...]
        e = x_ref[...] + nrm * g2v - t_ref[...]
        l1 = jnp.sum(e * e, axis=-1, keepdims=True)
        loss_ref[...] += jnp.broadcast_to(jnp.sum(l1, axis=0, keepdims=True) * (0.5 / D_MODEL), loss_ref.shape)
        dxo = e * (1.0 / D_MODEL)
        dxo_s[...] = dxo
        dg2_ref[...] += jnp.sum(dxo * nrm, axis=0, keepdims=True)
        dn = dxo * g2v
        do = r2 * (dn - nrm * jnp.mean(dn * nrm, axis=-1, keepdims=True))
        dob = do.astype(MM)
        dy = _dot_nt(dob, wo_ref[...])
        dp_s[:, Z_COL:IN_WIDTH] = (dy * yc * (sig * (1.0 + zt * (1.0 - sig)))).astype(MM)
        dyc[...] = dy * silu
        acc_o[...] += _dot_tn(yb, dob)

        def in_proj_bwd(c0, c1):
            part = _dot(dp_s[:, c0:c1], wi_ref[c0:c1, :])
            if c0 == Z_COL:
                dh_s[...] = part
            else:
                dh_s[...] += part

        in_proj_bwd(Z_COL, IN_WIDTH)

        for blk in range(bpt):
            r0 = blk * CHUNK
            rows = slice(r0, r0 + CHUNK)
            for g in range(A_GROUPS):
                cg = slice(g * CHUNK, (g + 1) * CHUNK)
                cv = slice(A_WIDTH + g * CHUNK, A_WIDTH + (g + 1) * CHUNK)
                dya = dyc[rows, cg]
                dp_s[rows, cg] = (dya * gu_s[rows, cg]).astype(MM)
                dsv = dya * u_s[rows, cg]
                dsvb = dsv.astype(MM)
                dsv_acc[g] += dsv
                dwsp_ref[g] += _dot_nt(dsvb, vc_s[rows, cg])
                dvc = _dot(wtt_ref[g], dsvb)
                xhat = xh_s[rows, cg]
                dvg_ref[:, cg] += jnp.sum(dvc * xhat, axis=0, keepdims=True)
                dvb_ref[:, cg] += jnp.sum(dvc, axis=0, keepdims=True)
                dxh = dvc * vg_ref[:, cg]
                dv = (dxh - jnp.mean(dxh, axis=-1, keepdims=True)
                      - xhat * jnp.mean(dxh * xhat, axis=-1, keepdims=True))
                dp_s[rows, cv] = (dv * gv_s[rows, cg]).astype(MM)
        in_proj_bwd(0, UV_W)
        b_dosel, b_dp, b_dss = [], [], []
        for blk in range(bpt):
            rows = slice(blk * CHUNK, (blk + 1) * CHUNK)
            for kvh in range(2):
                b_dosel.append(_stack_heads(dyc[rows, YB_OFF + kvh * CHUNK:YB_OFF + (kvh + 1) * CHUNK], lo))
                b_dp.append(_dot_nt(b_dosel[-1], vd_s[blk * 2 + kvh]))
        dosels, dps, dsss = [], [], []
        for hd in range(4):
            do128 = dyc[:, YC_OFF + (hd // 2) * CHUNK:YC_OFF + (hd // 2 + 1) * CHUNK]
            dosels.append(jnp.where(lot if hd % 2 == 0 else ~lot, do128, 0.0).astype(MM))
            dps.append(_dot_nt(dosels[hd], mkv_ref[:, MEM_LEN + (hd // 2) * CHUNK:MEM_LEN + (hd // 2 + 1) * CHUNK]))
        for blk in range(bpt):
            for kvh in range(2):
                probs = pb_s[blk * 2 + kvh]
                dp = b_dp[blk * 2 + kvh]
                delta = jnp.sum(probs * dp, axis=-1, keepdims=True)
                ds = probs * (dp - delta)
                dbias_acc[kvh] += ds
                sd = ps_s[blk * 2 + kvh][:, 0:1] * delta
                for gi in range(2):
                    hd = 2 * kvh + gi
                    dsink_acc[hd:hd + 1, :] += jnp.broadcast_to(
                        -jnp.sum(sd[gi * CHUNK:(gi + 1) * CHUNK], axis=0, keepdims=True), (1, CHUNK))
                b_dss.append((ds * SCALE).astype(MM))
        for hd in range(4):
            probs = pc_s[hd]
            ds = probs * (dps[hd] - jnp.sum(probs * dps[hd], axis=-1, keepdims=True))
            dsss.append((ds * SCALE).astype(MM))
        for blk in range(bpt):
            r0 = blk * CHUNK
            rows = slice(r0, r0 + CHUNK)
            dk_f, dv_f = [], []
            for kvh in range(2):
                dss = b_dss[blk * 2 + kvh]
                q2 = _stack_heads(q_s[rows, kvh * CHUNK:(kvh + 1) * CHUNK].astype(F32), lo)
                dq2 = _dot(dss, kd_s[blk * 2 + kvh])
                dkd = _dot_tn(dss, q2)
                dvd = _dot_tn(pb_s[blk * 2 + kvh].astype(MM), b_dosel[blk * 2 + kvh])
                dp_s[rows, SQ_COL + kvh * CHUNK:SQ_COL + (kvh + 1) * CHUNK] = jnp.where(
                    lo, dq2[0:CHUNK], dq2[CHUNK:2 * CHUNK]).astype(MM)
                dk_f.append(dkd + pltpu.roll(dkd, 64, 1))
                dv_f.append(dvd + pltpu.roll(dvd, 64, 1))
            dkv_acc[r0:r0 + 2 * CHUNK, 0:CHUNK] += jnp.where(lob, dk_f[0], dk_f[1])
            dkv_acc[r0:r0 + 2 * CHUNK, CHUNK:2 * CHUNK] += jnp.where(lob, dv_f[0], dv_f[1])
        dp_s[:, SK_COL:MQ_COL] = dkv_acc[CHUNK:CHUNK + tm, :].astype(MM)
        for g in range(2):
            q128 = q_s[:, 256 + g * CHUNK:256 + (g + 1) * CHUNK].astype(F32)
            k128 = mkv_ref[:, g * CHUNK:(g + 1) * CHUNK]
            dq128 = jnp.zeros((tm, CHUNK), F32)
            dk128 = jnp.zeros((MEM_LEN, CHUNK), F32)
            dv128 = jnp.zeros((MEM_LEN, CHUNK), F32)
            for hh in range(2):
                hd = 2 * g + hh
                half = lot if hh == 0 else ~lot
                qsel = jnp.where(half, q128, 0.0).astype(MM)
                dq128 = dq128 + jnp.where(half, _dot(dsss[hd], k128), 0.0)
                dk128 = dk128 + _dot_tn(dsss[hd], qsel)
                dv128 = dv128 + _dot_tn(pc_s[hd].astype(MM), dosels[hd])
            dp_s[:, MQ_COL + g * CHUNK:MQ_COL + (g + 1) * CHUNK] = dq128.astype(MM)
            dmkv_ref[:, g * CHUNK:(g + 1) * CHUNK] += dk128
            dmkv_ref[:, MEM_LEN + g * CHUNK:MEM_LEN + (g + 1) * CHUNK] += dv128

        in_proj_bwd(SQ_COL, Z_COL)
        project_uv()
        dh = dh_s[...]
        r = r_s[NOW]
        nx = x_ref[...] * r
        dg1_ref[...] += jnp.sum(dh * nx, axis=0, keepdims=True)
        dnx = dh * g1v
        gx_ref[...] = dxo_s[...] + r * (dnx - nx * jnp.mean(dnx * nx, axis=-1, keepdims=True))

        @pl.when(step == last_step)
        def _():
            dw_cols.extend(DW_PIECES)
            weight_grad(len(dw_cols), NOW)
            out_i = pltpu.make_async_copy(acc_i, dwi_hbm, sems.at[0])
            out_o = pltpu.make_async_copy(acc_o, dwo_hbm, sems.at[1])
            out_i.start()
            out_o.start()
            r_ = lax.broadcasted_iota(jnp.int32, (CHUNK, CHUNK), 0)
            c_ = lax.broadcasted_iota(jnp.int32, (CHUNK, CHUNK), 1)
            for g in range(A_GROUPS):
                dwsp_ref[g] = jnp.where(r_ >= c_, dwsp_ref[g], 0.0)
                dbs_ref[g:g + 1, :] = jnp.sum(dsv_acc[g].T, axis=0, keepdims=True)
            rows8 = lax.broadcasted_iota(jnp.int32, (8, CHUNK), 0)
            cols8 = lax.broadcasted_iota(jnp.int32, (8, CHUNK), 1)
            sk = jnp.zeros((8, CHUNK), F32)
            for hd in range(4):
                sk = sk + jnp.where((rows8 == 0) & (cols8 == hd),
                                    jnp.broadcast_to(dsink_acc[hd:hd + 1, :], (8, CHUNK)), 0.0)
            dsink_ref[...] = sk
            bk = bk_ref[...]
            valid = _window_valid()
            rrow = lax.broadcasted_iota(jnp.int32, (N_BUCKETS, CHUNK), 0)
            rcol = lax.broadcasted_iota(jnp.int32, (N_BUCKETS, CHUNK), 1)
            acc = jnp.zeros((N_BUCKETS, CHUNK), F32)
            for bb in range(N_BUCKETS):
                hit = (bk == bb) & valid
                for hd in range(4):
                    dbias = dbias_acc[hd // 2, (hd % 2) * CHUNK:(hd % 2 + 1) * CHUNK, :]
                    part = jnp.sum(jnp.where(hit, dbias, 0.0), axis=-1, keepdims=True)
                    tot = jnp.sum(part, axis=0, keepdims=True)
                    acc = acc + jnp.where((rrow == bb) & (rcol == hd), jnp.broadcast_to(tot, (N_BUCKETS, CHUNK)), 0.0)
            drel_ref[...] = acc
            out_i.wait()
            out_o.wait()

    after = lambda b, j: jnp.minimum(b * nt + j + 1, last_step)
    tile = pl.BlockSpec((tm, D_MODEL), lambda b, j: (tile_at(b * nt + j), 0))
    tile_after = pl.BlockSpec((tm, D_MODEL), lambda b, j: (tile_at(after(b, j)), 0))
    halo = pl.BlockSpec((CHUNK, D_MODEL), lambda b, j: (block_before(b * nt + j), 0))
    halo_after = pl.BlockSpec((CHUNK, D_MODEL), lambda b, j: (block_before(after(b, j)), 0))
    per_batch = lambda r, w: pl.BlockSpec((None, r, w), lambda b, j: (b, 0, 0))
    anyspec = pl.BlockSpec(memory_space=pl.ANY)
    grp = (A_GROUPS, CHUNK, CHUNK)
    return pl.pallas_call(
        body, name="layer", grid=(nb, nt),
        out_shape=(jax.ShapeDtypeStruct((t, D_MODEL), F32),
                   jax.ShapeDtypeStruct((nb, MEM_LEN, 2 * MEM_LEN), F32),
                   jax.ShapeDtypeStruct((IN_WIDTH, D_MODEL), F32),
                   jax.ShapeDtypeStruct((D_MODEL, D_MODEL), F32),
                   jax.ShapeDtypeStruct((1, D_MODEL), F32),
                   jax.ShapeDtypeStruct((1, D_MODEL), F32),
                   jax.ShapeDtypeStruct((8, CHUNK), F32),
                   jax.ShapeDtypeStruct(grp, F32),
                   jax.ShapeDtypeStruct((A_GROUPS, CHUNK), F32),
                   jax.ShapeDtypeStruct((1, A_WIDTH), F32),
                   jax.ShapeDtypeStruct((1, A_WIDTH), F32),
                   jax.ShapeDtypeStruct((8, CHUNK), F32),
                   jax.ShapeDtypeStruct((N_BUCKETS, CHUNK), F32)),
        in_specs=[tile, halo, tile_after, halo_after, tile, per_batch(MEM_LEN, 2 * MEM_LEN),
                  _full((2, 2 * CHUNK, 2 * CHUNK)),
                  pl.BlockSpec(memory_space=pltpu.SMEM),
                  _full((1, A_WIDTH)), _full((1, A_WIDTH)),
                  _full(grp), _full(grp), _full(grp),
                  _full((1, D_MODEL)), _full((1, D_MODEL)),
                  _full((IN_WIDTH, D_MODEL), single=True), _full((D_MODEL, D_MODEL), single=True),
                  _full((CHUNK, 2 * CHUNK))],
        out_specs=(tile, per_batch(MEM_LEN, 2 * MEM_LEN), anyspec, anyspec,
                   _full((1, D_MODEL)), _full((1, D_MODEL)), _full((8, CHUNK)),
                   _full(grp), _full((A_GROUPS, CHUNK)), _full((1, A_WIDTH)), _full((1, A_WIDTH)),
                   _full((8, CHUNK)), _full((N_BUCKETS, CHUNK))),
        scratch_shapes=[pltpu.VMEM((IN_WIDTH, D_MODEL), F32), pltpu.VMEM((D_MODEL, D_MODEL), F32),
                        pltpu.VMEM((tm, UV_W), F32), pltpu.VMEM((tm, Z_W), F32),
                        pltpu.VMEM((tm, 512), MM), pltpu.VMEM((tm + CHUNK, 2 * CHUNK), MM),
                        pltpu.VMEM((3, tm, D_MODEL), MM), pltpu.VMEM((CHUNK, D_MODEL), MM),
                        pltpu.VMEM((tm, IN_WIDTH), MM),
                        pltpu.VMEM((tm, D_MODEL), F32),
                        pltpu.VMEM((tm, D_MODEL), F32), pltpu.VMEM((2, tm, 1), F32),
                        pltpu.VMEM((tm, D_MODEL), F32), pltpu.VMEM((tm, D_MODEL), F32)]
                       + [pltpu.VMEM((tm, A_WIDTH), F32) for _ in range(4)]
                       + [pltpu.VMEM((tm, A_WIDTH), MM),
                          pltpu.VMEM((bpt * 2, 2 * CHUNK, 2 * CHUNK), F32),
                          pltpu.VMEM((bpt * 2, 2 * CHUNK, CHUNK), F32),
                          pltpu.VMEM((4, tm, MEM_LEN), F32),
                          pltpu.VMEM((bpt * 2, 2 * CHUNK, CHUNK), MM),
                          pltpu.VMEM((bpt * 2, 2 * CHUNK, CHUNK), MM),
                          pltpu.VMEM((tm + CHUNK, 2 * CHUNK), F32),
                          pltpu.VMEM((2, 2 * CHUNK, 2 * CHUNK), F32),
                          pltpu.VMEM(grp, F32),
                          pltpu.VMEM((8, CHUNK), F32),
                          pltpu.SemaphoreType.DMA((2,))],
        compiler_params=_params(dimension_semantics=("arbitrary", "arbitrary")),
    )(x2, x2, x2, x2, tgt2, mkv3, bias.reshape(2, 2 * CHUNK, 2 * CHUNK), sinks, vg, vb, wt, wtt, bcol, g1, g2, w_in_t, w_o, buckets)


class _ShardReduce:
    def __init__(self, pos, g, bufs, sems):
        self.x, self.y, self.c = pos
        self.g = g
        self.own, self.rcv, self.sbuf, self.rbuf, self.cbuf = bufs
        self.ld, self.sa, self.ra, self.sb, self.rb = sems
        self.nrow = g.shape[1]
        self.here = (self.x, self.y, self.c)
        self.sib = (self.x, self.y, 1 - self.c)
        self.first, self.second, self.far = _route(*pos)

    def _load(self, q):
        return pltpu.make_async_copy(self.g.at[2 * q + self.c], self.own.at[q], self.ld.at[q])

    def _to_sib(self, q, to):
        return _remote(self.g.at[2 * q + 1 - self.c], self.rcv.at[q], self.sa.at[q], self.ra.at[q], to)

    def _send(self, k, to):
        dst = self.cbuf.at[0] if k == 1 else self.rbuf.at[0 if k == 0 else 1]
        return _remote(self.sbuf.at[k], dst, self.sb.at[k], self.rb.at[k], to)

    def _stage(self, k, which, extra=None):
        def cast(r):
            v = self.rcv[which, r, :]
            if extra is not None:
                v = v + extra[0, r, :].astype(F32)
            self.sbuf[k, r, :] = v.astype(BF16)

        _rows_loop(self.nrow, cast)

    @staticmethod
    def _q(chip):
        return 2 * chip[0] + chip[1]

    def start(self):
        for q in range(4):
            self._load(q).start()
            self._to_sib(q, self.sib).start()

    def mid(self):
        for q in range(4):
            self._load(q).wait()
            self._to_sib(q, self.here).wait_recv()

        def add(r):
            for q in range(4):
                self.rcv[q, r, :] = self.rcv[q, r, :] + self.own[q, r, :]

        _rows_loop(self.nrow, add)
        to_first = (self.first[0], self.first[1], self.c)
        self._stage(0, self._q(self.first))
        self._send(0, to_first).start()
        self._stage(1, self._q(self.far))
        self._send(1, to_first).start()

    def pass_on(self):
        self._send(1, self.here).wait_recv()
        self._stage(2, self._q(self.second), extra=self.cbuf)
        self._send(2, (self.second[0], self.second[1], self.c)).start()

    def finish(self, out):
        self._send(0, self.here).wait_recv()
        self._send(2, self.here).wait_recv()
        which = 2 * self.x + self.y

        def tot(r):
            out[r, :] = (self.rcv[which, r, :] + self.rbuf[0, r, :].astype(F32)) + self.rbuf[1, r, :].astype(F32)

        _rows_loop(self.nrow, tot)
        for q in range(4):
            self._to_sib(q, self.sib).wait_send()
        to_first = (self.first[0], self.first[1], self.c)
        self._send(0, to_first).wait_send()
        self._send(1, to_first).wait_send()
        self._send(2, (self.second[0], self.second[1], self.c)).wait_send()


def _reduce_scratch(shape):
    return [pltpu.VMEM((4,) + shape, F32), pltpu.VMEM((4,) + shape, F32),
            pltpu.VMEM((3,) + shape, BF16), pltpu.VMEM((2,) + shape, BF16), pltpu.VMEM((1,) + shape, BF16),
            pltpu.SemaphoreType.DMA((4,)), pltpu.SemaphoreType.DMA((4,)), pltpu.SemaphoreType.DMA((4,)),
            pltpu.SemaphoreType.DMA((3,)), pltpu.SemaphoreType.DMA((3,))]


_N_RED = 10

_S_LAYOUT = (((1, D_MODEL), 0), ((1, D_MODEL), 8), ((1, D_MODEL), 16),
             ((1, A_WIDTH), 24), ((1, A_WIDTH), 28), ((A_GROUPS, CHUNK), 32),
             ((1, 4), 36), ((N_BUCKETS, 4), 40),
             ((A_GROUPS * CHUNK, CHUNK), 72))
_LOSS_ROW = 37
_W_SP_ROW = _S_LAYOUT[-1][1]
_S_ROWS = _W_SP_ROW + A_GROUPS * CHUNK
_N_SMALL = len(_S_LAYOUT)


def _pack_rows(dst, refs):
    for (shp, r0), ref in zip(_S_LAYOUT, refs):
        if shp[0] == 1 and shp[1] >= CHUNK:
            for i in range(shp[1] // CHUNK):
                dst[r0 + i:r0 + i + 1, :] = ref[:, i * CHUNK:(i + 1) * CHUNK]
        elif ref.shape[-1] == CHUNK:
            dst[r0:r0 + shp[0], :] = ref[0:shp[0], :]
        else:
            dst[r0:r0 + shp[0], 0:shp[1]] = ref[...]


def _unpack_rows(src, refs):
    for (shp, r0), ref in zip(_S_LAYOUT, refs):
        if shp[0] == 1 and shp[1] >= CHUNK:
            for i in range(shp[1] // CHUNK):
                ref[:, i * CHUNK:(i + 1) * CHUNK] = src[r0 + i:r0 + i + 1, :]
        elif shp[1] == CHUNK:
            ref[...] = src[r0:r0 + shp[0], :]
        else:
            if tuple(ref.shape) == (shp[1], shp[0]):
                ref[...] = src[r0:r0 + CHUNK, :].T[0:shp[1], 0:shp[0]]
            else:
                ref[...] = src[r0:r0 + shp[0], 0:shp[1]]


_MEM_G = 2


def _greduce(ga, gb, dmkv, mem2, gm, w_mkv, small_g, loss_p):
    shp_c = (SHARD_O, 2 * MEM_LEN)
    shapes = (shp_c, gb.shape[1:], ga.shape[1:])
    rs = _S_ROWS

    def body(*refs):
        it = iter(refs)
        take = lambda n: [next(it) for _ in range(n)]
        gb_ref, ga_ref, d_ref, m_ref, gm_ref, wm_ref = take(6)
        sg_refs = take(_N_SMALL - 1)
        loss_ref, = take(1)
        oc, ob, oa, ogs = take(4)
        red = take(3 * _N_RED)
        gs_ref, rs_a, rs_b, rs_w, gc_ref, dgm_ref = take(6)
        ssem_a, rsem_a, ssem_b, rsem_b = take(4)

        pos = _position()
        x, y, cc = pos
        myq = 2 * x + y
        here, sib = (x, y, cc), (x, y, 1 - cc)
        chips = _other_chips(x, y)
        reducers = [_ShardReduce(pos, g, red[k * _N_RED:k * _N_RED + 5], red[k * _N_RED + 5:(k + 1) * _N_RED])
                    for k, g in enumerate((gc_ref, gb_ref, ga_ref))]
        for rd in reducers[1:]:
            rd.start()

        xf = m_ref[...]
        nm = xf * _rms(xf)
        hm = (nm * gm_ref[...]).astype(MM)
        d = d_ref[...].astype(MM)
        for o in range(N_DEV):
            gc_ref[o] = _dot_tn(hm[:, o * SHARD_O:(o + 1) * SHARD_O], d)
        dgm_ref[...] = jnp.sum(_dot_nt(d, wm_ref[...]) * nm, axis=0, keepdims=True)
        reducers[0].start()

        gs_ref[...] = jnp.zeros_like(gs_ref)
        _pack_rows(gs_ref, sg_refs[:_MEM_G] + [dgm_ref] + sg_refs[_MEM_G:])
        gs_ref[_LOSS_ROW:_LOSS_ROW + 1, :] = loss_ref[0:1, :]
        small_a = _remote(gs_ref, rs_a, ssem_a, rsem_a, sib)
        small_a.start()

        _remote(gs_ref, rs_a, ssem_a, rsem_a, here).wait_recv()
        rs_b[myq] = gs_ref[0:_W_SP_ROW, :] + rs_a[0:_W_SP_ROW, :]
        rs_w[myq] = (gs_ref[_W_SP_ROW:rs, :] + rs_a[_W_SP_ROW:rs, :]).astype(BF16)
        small_b = []
        for j, chip in enumerate(chips):
            to = (chip[0], chip[1], cc)
            small_b.append(_remote(rs_b.at[myq], rs_b.at[myq], ssem_b.at[0, j], rsem_b.at[0, j], to))
            small_b.append(_remote(rs_w.at[myq], rs_w.at[myq], ssem_b.at[1, j], rsem_b.at[1, j], to))
        for cp in small_b:
            cp.start()
        late_last = reducers[1:] + reducers[:1]
        for rd in late_last:
            rd.mid()
        for rd in late_last:
            rd.pass_on()

        for j in range(3):
            _remote(rs_b.at[myq], rs_b.at[myq], ssem_b.at[0, j], rsem_b.at[0, j], here).wait_recv()
            _remote(rs_w.at[myq], rs_w.at[myq], ssem_b.at[1, j], rsem_b.at[1, j], here).wait_recv()
        ogs[0:_W_SP_ROW, :] = ((rs_b[0] + rs_b[1]) + rs_b[2]) + rs_b[3]

        def tot_w(r):
            w = [rs_w[q, r, :].astype(F32) for q in range(4)]
            ogs[pl.ds(pl.multiple_of(_W_SP_ROW + r.start, 8), _ROWS), :] = ((w[0] + w[1]) + w[2]) + w[3]

        _rows_loop(rs - _W_SP_ROW, tot_w)
        for rd, out in zip(late_last, (ob, oa, oc)):
            rd.finish(out)
        small_a.wait_send()
        for cp in small_b:
            cp.wait_send()

    vm = pl.BlockSpec(memory_space=pltpu.VMEM)
    anyspec = pl.BlockSpec(memory_space=pl.ANY)
    scratch = []
    for shp in shapes:
        scratch += _reduce_scratch(shp)
    scratch += [pltpu.VMEM((rs, CHUNK), F32), pltpu.VMEM((rs, CHUNK), F32),
                pltpu.VMEM((4, _W_SP_ROW, CHUNK), F32), pltpu.VMEM((4, rs - _W_SP_ROW, CHUNK), BF16),
                pltpu.VMEM((N_DEV,) + shp_c, F32), pltpu.VMEM((1, D_MODEL), F32),
                pltpu.SemaphoreType.DMA, pltpu.SemaphoreType.DMA,
                pltpu.SemaphoreType.DMA((2, 3)), pltpu.SemaphoreType.DMA((2, 3))]
    tc, tb, ta, ts = pl.pallas_call(
        body, name="greduce",
        out_shape=tuple([jax.ShapeDtypeStruct(shp, F32) for shp in shapes] + [jax.ShapeDtypeStruct((rs, CHUNK), F32)]),
        in_specs=[anyspec] * 2 + [vm] * (4 + _N_SMALL),
        out_specs=(vm, vm, vm, vm),
        scratch_shapes=scratch,
        compiler_params=_params(),
    )(gb, ga, dmkv, mem2, gm, w_mkv, *small_g, loss_p)
    return ta, tb, tc, ts


def _adamw(w, g, m, v):
    m = ADAM_B1 * m + (1.0 - ADAM_B1) * g
    v = ADAM_B2 * v + (1.0 - ADAM_B2) * (g * g)
    m_hat = m / (1.0 - ADAM_B1 ** ADAM_STEP)
    v_hat = v / (1.0 - ADAM_B2 ** ADAM_STEP)
    delta = -ADAM_LR * (m_hat / (jnp.sqrt(v_hat) + ADAM_EPS) + ADAM_WD * w)
    return delta, m, v


def _update(ta, tb, tc, ts, big_wmv, small_wmv):
    shapes = (ta.shape, tb.shape, tc.shape)
    rs = _S_ROWS
    small_shapes = [tuple(a.shape) for a in small_wmv[0]]

    def body(*refs):
        it = iter(refs)
        take = lambda n: [next(it) for _ in range(n)]
        ga_ref, gb_ref, gc_ref, gs_ref = take(4)
        wa, ma, va, wb, mb, vb_, wc, mc, vc = take(9)
        sw_refs, sm_refs, sv_refs = take(_N_SMALL), take(_N_SMALL), take(_N_SMALL)
        oga, oda, oma, ova, ogb, odb, omb, ovb, ogc, odc, omc, ovc = take(12)
        so_refs = [take(_N_SMALL) for _ in range(4)]
        loss_out, = take(1)
        ws, ms, vs, ods, oms, ovs = take(6)

        def update_rows(nrow, rows, g_r, w_r, m_r, v_r, og, od, om, ov):
            def upd(r):
                g = g_r[r, :]
                d, m, v = _adamw(w_r[r, :], g, m_r[r, :], v_r[r, :])
                og[r, :] = g
                od[r, :] = d
                om[r, :] = m
                ov[r, :] = v

            _rows_loop(nrow, upd, rows)

        update_rows(a_rows, 16, ga_ref, wa, ma, va, oga, oda, oma, ova)

        @pl.when(pl.program_id(0) == 0)
        def _():
            update_rows(shapes[1][0], _ROWS, gb_ref, wb, mb, vb_, ogb, odb, omb, ovb)
            update_rows(shapes[2][0], _ROWS, gc_ref, wc, mc, vc, ogc, odc, omc, ovc)
            for buf in (ws, ms, vs):
                buf[...] = jnp.zeros_like(buf)
            _pack_rows(ws, sw_refs)
            _pack_rows(ms, sm_refs)
            _pack_rows(vs, sv_refs)

            def upd_s(i, _):
                r = pl.ds(pl.multiple_of(i * 8, 8), 8)
                d, m, v = _adamw(ws[r, :], gs_ref[r, :], ms[r, :], vs[r, :])
                ods[r, :] = d
                oms[r, :] = m
                ovs[r, :] = v
                return 0

            lax.fori_loop(0, rs // 8, upd_s, 0)
            for k, buf in enumerate((gs_ref, ods, oms, ovs)):
                _unpack_rows(buf, so_refs[k])
            loss_out[...] = gs_ref[_LOSS_ROW:_LOSS_ROW + 1, 0:1]

    n_blocks = 2
    a_rows = shapes[0][0] // n_blocks
    a_spec = pl.BlockSpec((a_rows, shapes[0][1]), lambda i: (i, 0))
    big_out, big_out_specs = [], []
    for shp in shapes:
        big_out += [jax.ShapeDtypeStruct(shp, F32)] * 4
        big_out_specs += [a_spec if shp == shapes[0] else _full(shp)] * 4
    small_out_shapes = [shp[::-1] if shp == (N_BUCKETS, 4) else shp for shp in small_shapes] * 4
    small_out = [jax.ShapeDtypeStruct(shp, F32) for shp in small_out_shapes]
    out_shape = tuple(big_out + small_out + [jax.ShapeDtypeStruct((1, 1), F32)])
    in_specs = ([a_spec, _full(shapes[1]), _full(shapes[2]), _full((rs, CHUNK))]
                + [a_spec] * 3 + [_full(shapes[1])] * 3 + [_full(shapes[2])] * 3
                + [_full(shp) for shp in small_shapes] * 3)
    return pl.pallas_call(
        body, name="update", grid=(n_blocks,),
        out_shape=out_shape,
        in_specs=in_specs,
        out_specs=tuple(big_out_specs + [_full(shp) for shp in small_out_shapes] + [_full((1, 1))]),
        scratch_shapes=[pltpu.VMEM((rs, CHUNK), F32) for _ in range(6)],
        compiler_params=_params(dimension_semantics=("arbitrary",)),
    )(ta, tb, tc, ts, *big_wmv, *small_wmv[0], *small_wmv[1], *small_wmv[2])


def kernel(x, mem, pre_norm_g, post_norm_g, mem_norm_g, w_in, w_mem_kv, v_norm_g, v_norm_b, w_spatial, b_spatial, attn_sinks, rel_bias, w_out, loss_target, m_pre_norm_g, m_post_norm_g, m_mem_norm_g, m_w_in, m_w_mem_kv, m_v_norm_g, m_v_norm_b, m_w_spatial, m_b_spatial, m_attn_sinks, m_rel_bias, m_w_out, v_pre_norm_g, v_post_norm_g, v_mem_norm_g, v_w_in, v_w_mem_kv, v_v_norm_g, v_v_norm_b, v_w_spatial, v_b_spatial, v_attn_sinks, v_rel_bias, v_w_out):
    sh_a = (w_in[0].T, m_w_in[0].T, v_w_in[0].T)
    sh_b = (w_out[0], m_w_out[0], v_w_out[0])
    sh_c = (w_mem_kv[0], m_w_mem_kv[0], v_w_mem_kv[0])
    nb, s, _ = x.shape
    t = nb * s
    x2 = x.reshape(t, D_MODEL)
    tgt2 = loss_target.reshape(t, D_MODEL)
    mem2 = mem.reshape(nb * MEM_LEN, D_MODEL)
    buckets = jnp.asarray(_t5_buckets())

    wa, wb, wc, bias, wt, wtt, bcol, mkv = _wgather(sh_a[0], sh_b[0], sh_c[0], rel_bias, w_spatial[0], b_spatial[0],
                                                    buckets, mem2, mem_norm_g)
    w_mkv = wc.reshape(D_MODEL, 2 * MEM_LEN)
    gx, dmkv, dwi, dwo, dg1, dg2, loss_p, dwsp, dbs, dvg, dvb, dsink, drel = _layer(
        x2, tgt2, mkv.reshape(nb, MEM_LEN, 2 * MEM_LEN), bias, attn_sinks.reshape(4), v_norm_g, v_norm_b, wt, wtt, bcol,
        pre_norm_g, post_norm_g, wa.reshape(IN_WIDTH, D_MODEL), wb.reshape(D_MODEL, D_MODEL), buckets,
        nb, s, min(256, s))
    gx = gx.reshape(nb, s, D_MODEL)
    small_grads = [dg1, dg2, dvg, dvb, dbs, dsink, drel, dwsp.reshape(A_GROUPS * CHUNK, CHUNK)]

    small_names = ["pre_norm_g", "post_norm_g", "mem_norm_g", "v_norm_g", "v_norm_b", "b_spatial", "attn_sinks",
                   "rel_bias", "w_spatial"]
    given = dict(pre_norm_g=(pre_norm_g, m_pre_norm_g, v_pre_norm_g), post_norm_g=(post_norm_g, m_post_norm_g, v_post_norm_g),
                 mem_norm_g=(mem_norm_g, m_mem_norm_g, v_mem_norm_g), v_norm_g=(v_norm_g, m_v_norm_g, v_v_norm_g),
                 v_norm_b=(v_norm_b, m_v_norm_b, v_v_norm_b), b_spatial=(b_spatial, m_b_spatial, v_b_spatial),
                 attn_sinks=(attn_sinks, m_attn_sinks, v_attn_sinks), rel_bias=(rel_bias, m_rel_bias, v_rel_bias),
                 w_spatial=(w_spatial, m_w_spatial, v_w_spatial))
    small_wmv = [[given[n][k].reshape(shp) for n, (shp, _) in zip(small_names, _S_LAYOUT)] for k in range(3)]

    ta, tb, tc, ts = _greduce(dwi.reshape(N_DEV, SHARD_IN, D_MODEL), dwo.reshape(N_DEV, SHARD_O, D_MODEL),
                              dmkv.reshape(nb * MEM_LEN, 2 * MEM_LEN), mem2, mem_norm_g, w_mkv, small_grads, loss_p)
    outs = _update(ta, tb, tc, ts, (*sh_a, *sh_b, *sh_c), small_wmv)
    ra, rb, rc = outs[0:4], outs[4:8], outs[8:12]
    loss = outs[12 + 4 * _N_SMALL].reshape(())

    res = {}
    for k, kind in enumerate(("grad", "delta", "new_m", "new_v")):
        res[kind, "w_in"] = ra[k].T[None]
        res[kind, "w_out"] = rb[k][None]
        res[kind, "w_mem_kv"] = rc[k][None]
        for i, n in enumerate(small_names):
            o = outs[12 + k * _N_SMALL + i]
            res[kind, n] = o.T if n == "rel_bias" else o.reshape(given[n][0].shape)
    order = ["pre_norm_g", "post_norm_g", "mem_norm_g", "w_in", "w_mem_kv", "v_norm_g", "v_norm_b", "w_spatial",
             "b_spatial", "attn_sinks", "rel_bias", "w_out"]
    flat = [res[kind, n] for kind in ("grad", "delta", "new_m", "new_v") for n in order]
    return (loss, gx, *flat)
```

```python
import numpy as np
import jax
import jax.numpy as jnp
from jax import lax
from jax.experimental import pallas as pl
from jax.experimental.pallas import tpu as pltpu

F32 = jnp.float32
BF16 = jnp.bfloat16
MM = jnp.bfloat16

D_MODEL = 1024
CHUNK = 128
A_GROUPS = 4
A_WIDTH = 512
UV_W = 1024
QKV_W = 768
Z_W = 1024
IN_WIDTH = UV_W + QKV_W + Z_W
MEM_LEN = 256
N_BUCKETS = 32
MAX_DISTANCE = 128
EPS = 1e-6
NEG = -1e30
SCALE = 0.125
N_DEV = 8
SHARD_IN = IN_WIDTH // N_DEV
SHARD_O = D_MODEL // N_DEV

SQ_COL, SK_COL, SV_COL, MQ_COL, Z_COL = UV_W, UV_W + 256, UV_W + 384, UV_W + 512, UV_W + QKV_W
DW_PIECES = ((0, SQ_COL), (SQ_COL, Z_COL), (Z_COL, IN_WIDTH))
YB_OFF, YC_OFF = 512, 768

ADAM_LR = 0.001
ADAM_B1 = 0.9
ADAM_B2 = 0.999
ADAM_EPS = 1e-08
ADAM_WD = 0.01
ADAM_STEP = 10

VMEM_LIMIT = 60 * 1024 * 1024

_GELU_C = 0.7978845608028654
_GELU_A = 0.044715

MESH = pl.DeviceIdType.MESH
_ROWS = 32


def _dot(a, b):
    return lax.dot_general(a, b, (((1,), (0,)), ((), ())), preferred_element_type=F32)


def _dot_nt(a, b):
    return lax.dot_general(a, b, (((1,), (1,)), ((), ())), preferred_element_type=F32)


def _dot_tn(a, b):
    return lax.dot_general(a, b, (((0,), (0,)), ((), ())), preferred_element_type=F32)


def _gelu_and_grad(x):
    x2 = x * x
    t = jnp.tanh(x * (_GELU_C + (_GELU_C * _GELU_A) * x2))
    w = 0.5 * t + 0.5
    g = x * w
    dg = w * (1.0 + (x - g) * ((2.0 * _GELU_C) + (6.0 * _GELU_C * _GELU_A) * x2))
    return g, dg


def _t5_buckets():
    qi = np.arange(CHUNK)[:, None]
    kj = np.arange(2 * CHUNK)[None, :]
    n = np.maximum(qi + CHUNK - kj, 0)
    max_exact = N_BUCKETS // 2
    large = max_exact + (np.log(np.maximum(n, 1) / max_exact) / np.log(MAX_DISTANCE / max_exact)
                         * (N_BUCKETS - max_exact)).astype(np.int32)
    large = np.minimum(large, N_BUCKETS - 1)
    return np.where(n < max_exact, n, large).astype(np.int32)


def _params(**kw):
    return pltpu.CompilerParams(vmem_limit_bytes=VMEM_LIMIT, **kw)


def _full(shape, single=False):
    nd = len(shape)
    if single:
        return pl.BlockSpec(shape, lambda *_: (0,) * nd, pipeline_mode=pl.Buffered(1))
    return pl.BlockSpec(shape, lambda *_: (0,) * nd)


def _window_valid():
    qi = lax.broadcasted_iota(jnp.int32, (CHUNK, 2 * CHUNK), 0)
    kj = lax.broadcasted_iota(jnp.int32, (CHUNK, 2 * CHUNK), 1)
    dist = qi + CHUNK - kj
    return (dist >= 0) & (dist < CHUNK)


def _position():
    return lax.axis_index("x"), lax.axis_index("y"), lax.axis_index("c")


def _other_chips(x, y):
    return [(1 - x, y), (x, 1 - y), (1 - x, 1 - y)]


def _route(x, y, c):
    first = (x * c + (1 - x) * (1 - c), y * (1 - c) + (1 - y) * c)
    second = (x * (1 - c) + (1 - x) * c, y * c + (1 - y) * (1 - c))
    return first, second, (1 - x, 1 - y)


def _remote(src, dst, ssem, rsem, to):
    return pltpu.make_async_remote_copy(src_ref=src, dst_ref=dst, send_sem=ssem, recv_sem=rsem,
                                        device_id=to, device_id_type=MESH)


def _rows_loop(nrow, fn, rows=_ROWS):
    assert nrow % rows == 0

    def step(i, _):
        fn(pl.ds(pl.multiple_of(i * rows, rows), rows))
        return 0

    lax.fori_loop(0, nrow // rows, step, 0)


class _Gather:
    def __init__(self, pos, out, ssem, rsem):
        self.x, self.y, self.c = pos
        self.out, self.ssem, self.rsem = out, ssem, rsem
        self.me = 4 * self.x + 2 * self.y + self.c
        self.here = (self.x, self.y, self.c)
        self.sib = (self.x, self.y, 1 - self.c)
        self.first, self.second, self.far = _route(*pos)

    def _copy(self, k, blk, to):
        r = self.out.at[blk]
        return _remote(r, r, self.ssem.at[k], self.rsem.at[k], to)

    def _idx(self, chip, core):
        return 4 * chip[0] + 2 * chip[1] + core

    def _on(self, chip):
        return (chip[0], chip[1], self.c)

    def start(self):
        self._copy(0, self.me, self.sib).start()
        self._copy(1, self.me, self._on(self.first)).start()
        self._copy(2, self.me, self._on(self.second)).start()

    def forward(self):
        c = self.c
        self._copy(1, self._idx(self.first, c), self.here).wait_recv()
        self._copy(3, self._idx(self.first, c), self._on(self.second)).start()
        self._copy(4, self._idx(self.first, c), self.sib).start()
        self._copy(2, self._idx(self.second, c), self.here).wait_recv()
        self._copy(5, self._idx(self.second, c), self.sib).start()
        self._copy(3, self._idx(self.far, c), self.here).wait_recv()
        self._copy(6, self._idx(self.far, c), self.sib).start()

    def finish(self):
        c = self.c
        self._copy(0, self._idx((self.x, self.y), 1 - c), self.here).wait_recv()
        for k, chip in ((4, self.second), (5, self.first), (6, self.far)):
            self._copy(k, self._idx(chip, 1 - c), self.here).wait_recv()
        self._copy(0, self.me, self.sib).wait_send()
        self._copy(1, self.me, self._on(self.first)).wait_send()
        self._copy(2, self.me, self._on(self.second)).wait_send()
        self._copy(3, self._idx(self.first, c), self._on(self.second)).wait_send()
        for k, chip in ((4, self.first), (5, self.second), (6, self.far)):
            self._copy(k, self._idx(chip, c), self.sib).wait_send()


def _prep_tables(rb_ref, w_ref, b_ref, bk_ref, bias_ref, wt_ref, wtt_ref, bcol_ref):
    valid = _window_valid()
    bk = bk_ref[...]
    acc = [jnp.full((CHUNK, 2 * CHUNK), NEG, F32) for _ in range(4)]
    for b in range(N_BUCKETS):
        hit = (bk == b) & valid
        for h in range(4):
            acc[h] = jnp.where(hit, rb_ref[h, b], acc[h])
    for h in range(4):
        bias_ref[h] = acc[h]
    r = lax.broadcasted_iota(jnp.int32, (CHUNK, CHUNK), 0)
    c = lax.broadcasted_iota(jnp.int32, (CHUNK, CHUNK), 1)
    for g in range(A_GROUPS):
        w = jnp.where(r >= c, w_ref[g], 0.0)
        wt_ref[g] = w.astype(MM)
        wtt_ref[g] = w.T.astype(MM)
        bcol_ref[g] = jnp.broadcast_to(b_ref[g:g + 1, :], (CHUNK, CHUNK)).T


def _wgather(a, b, c, rel_bias, w_sp, b_sp, buckets, mem2, gm):
    tmem = mem2.shape[0]

    def body(a_ref, b_ref, c_ref, rb_ref, w_ref, bsp_ref, bk_ref, m_ref, gm_ref,
             oa, ob, oc, bias_ref, wt_ref, wtt_ref, bcol_ref, mkv_ref, ssem, rsem):
        pos = _position()
        me = 4 * pos[0] + 2 * pos[1] + pos[2]
        gathers = []
        for k, (src, out) in enumerate(((c_ref, oc), (b_ref, ob), (a_ref, oa))):
            out[me] = src[...].astype(BF16)
            g = _Gather(pos, out, ssem.at[k], rsem.at[k])
            g.start()
            gathers.append(g)
        _prep_tables(rb_ref, w_ref, bsp_ref, bk_ref, bias_ref, wt_ref, wtt_ref, bcol_ref)
        for g in gathers:
            g.forward()
        gathers[0].finish()
        xf = m_ref[...]
        hm = (xf * _rms(xf) * gm_ref[...]).astype(MM)
        acc = jnp.zeros((tmem, 2 * MEM_LEN), F32)
        for d in range(N_DEV):
            acc = acc + _dot(hm[:, d * SHARD_O:(d + 1) * SHARD_O], oc[d])
        mkv_ref[...] = acc.astype(MM)
        for g in gathers[1:]:
            g.finish()

    vm = pl.BlockSpec(memory_space=pltpu.VMEM)
    grp = (A_GROUPS, CHUNK, CHUNK)
    return pl.pallas_call(
        body, name="wgather",
        out_shape=(jax.ShapeDtypeStruct((N_DEV,) + a.shape, BF16),
                   jax.ShapeDtypeStruct((N_DEV,) + b.shape, BF16),
                   jax.ShapeDtypeStruct((N_DEV,) + c.shape, BF16),
                   jax.ShapeDtypeStruct((4, CHUNK, 2 * CHUNK), F32),
                   jax.ShapeDtypeStruct(grp, MM), jax.ShapeDtypeStruct(grp, MM), jax.ShapeDtypeStruct(grp, F32),
                   jax.ShapeDtypeStruct((tmem, 2 * MEM_LEN), MM)),
        in_specs=[vm, vm, vm, pl.BlockSpec(memory_space=pltpu.SMEM), vm, vm, vm, vm, vm],
        out_specs=tuple([vm] * 8),
        scratch_shapes=[pltpu.SemaphoreType.DMA((3, 7)), pltpu.SemaphoreType.DMA((3, 7))],
        compiler_params=_params(),
    )(a, b, c, rel_bias, w_sp, b_sp, buckets, mem2, gm)


def _half_masks(rows):
    lane = lax.broadcasted_iota(jnp.int32, (rows, CHUNK), 1)
    return lane < 64


def _dup_heads(band):
    b32 = band.astype(F32)
    rolled = pltpu.roll(b32, 64, 1)
    lo = _half_masks(band.shape[0])
    return (jnp.where(lo, b32, rolled).astype(MM), jnp.where(lo, rolled, b32).astype(MM))


def _swa_probs(qk, bias_h, sink_h, first_add):
    s = qk * SCALE + bias_h + first_add
    m = jnp.maximum(jnp.max(s, axis=-1, keepdims=True), sink_h)
    p = jnp.exp(s - m)
    es = jnp.exp(sink_h - m)
    inv = 1.0 / (jnp.sum(p, axis=-1, keepdims=True) + es)
    return p * inv, es * inv


def _softmax(s):
    m = jnp.max(s, axis=-1, keepdims=True)
    p = jnp.exp(s - m)
    return p * (1.0 / jnp.sum(p, axis=-1, keepdims=True))


def _first_block_mask(n):
    col = lax.broadcasted_iota(jnp.int32, (2 * CHUNK, 2 * CHUNK), 1)
    return jnp.where((col < CHUNK) & (n == 0), NEG, 0.0)


def _stack_heads(x128, lo):
    return jnp.concatenate([jnp.where(lo, x128, 0.0), jnp.where(lo, 0.0, x128)], axis=0).astype(MM)


def _rms(xf):
    return lax.rsqrt(jnp.mean(xf * xf, axis=-1, keepdims=True) + EPS)


def _layer(x2, tgt2, mkv3, bias, sinks, vg, vb, wt, wtt, bcol, g1, g2, w_in_t, w_o, buckets, nb, s, tm):
    nt = s // tm
    bpt = tm // CHUNK
    bps = s // CHUNK
    t = nb * s
    last_step = nb * nt - 1

    def tile_at(step):
        return (step // nt) * nt + nt - 1 - step % nt

    def block_before(step):
        return (step // nt) * bps + jnp.maximum((nt - 1 - step % nt) * bpt - 1, 0)

    def body(x_ref, xp_ref, xn_ref, xpn_ref, t_ref, mkv_ref, bias_ref, sink_ref, vg_ref, vb_ref,
             wt_ref, wtt_ref, bcol_ref, g1_ref, g2_ref, wi_ref, wo_ref, bk_ref,
             gx_ref, dmkv_ref, dwi_hbm, dwo_hbm, dg1_ref, dg2_ref, loss_ref, dwsp_ref, dbs_ref,
             dvg_ref, dvb_ref, dsink_ref, drel_ref,
             acc_i, acc_o, uv_s, z_s, q_s, kv_s, h_s, hp_s, dp_s, dxo_s, dh_s, r_s,
             ycat, dyc, u_s, gu_s, gv_s, xh_s, vc_s, pb_s, ps_s, pc_s, kd_s, vd_s,
             dkv_acc, dbias_acc, dsv_acc, dsink_acc, sems):
        b, j = pl.program_id(0), pl.program_id(1)
        jt = nt - 1 - j
        step = b * nt + j
        g1v = g1_ref[...]
        NOW, NEXT, DONE = 0, 1, 2
        dw_cols = list(DW_PIECES)

        def weight_grad(n, slot):
            for c0, c1 in dw_cols[:n]:
                acc_i[c0:c1, :] += _dot_tn(dp_s[:, c0:c1], h_s[slot])
            del dw_cols[:n]

        def pre_norm(x_tile, x_before):
            xf = x_tile[...]
            r_s[NEXT] = _rms(xf)
            h_s[NEXT] = (xf * r_s[NEXT] * g1v).astype(MM)
            xp = x_before[...]
            hp_s[...] = (xp * _rms(xp) * g1v).astype(MM)

        def project_z():
            z_s[...] = _dot_nt(h_s[NEXT], wi_ref[Z_COL:IN_WIDTH, :])

        def project_uv():
            uv_s[...] = _dot_nt(h_s[NEXT], wi_ref[0:UV_W, :])

        @pl.when(step == 0)
        def _():
            for ref in (acc_i, acc_o, dg1_ref, dg2_ref, loss_ref, dwsp_ref, dvg_ref, dvb_ref,
                        dbias_acc, dsv_acc, dsink_acc):
                ref[...] = jnp.zeros_like(ref)
            dp_s[...] = jnp.zeros_like(dp_s)
            h_s[NOW] = jnp.zeros((tm, D_MODEL), MM)
            pre_norm(x_ref, xp_ref)
            project_z()
            project_uv()

        h_s[DONE] = h_s[NOW]
        r_s[NOW] = r_s[NEXT]
        h = h_s[NEXT]
        h_s[NOW] = h
        hp = hp_s[...]

        @pl.when(j == 0)
        def _():
            dmkv_ref[...] = jnp.zeros_like(dmkv_ref)
            dkv_acc[...] = jnp.zeros_like(dkv_acc)

        carry = dkv_acc[0:CHUNK, :]
        dkv_acc[...] = jnp.zeros_like(dkv_acc)
        dkv_acc[tm:tm + CHUNK, :] = carry

        lo = _half_masks(CHUNK)
        lob = _half_masks(2 * CHUNK)
        lot = _half_masks(tm)

        qkv = _dot_nt(h, wi_ref[SQ_COL:Z_COL, :])
        q_s[:, 0:256] = qkv[:, 0:256].astype(MM)
        q_s[:, 256:512] = qkv[:, 512:768].astype(MM)
        kv_s[CHUNK:CHUNK + tm, :] = qkv[:, 256:512].astype(MM)
        kv_s[0:CHUNK, :] = _dot_nt(hp, wi_ref[SK_COL:MQ_COL, :]).astype(MM)

        weight_grad(1, DONE)
        b_qk, b_pb = [], []
        for blk in range(bpt):
            r0 = blk * CHUNK
            rows = slice(r0, r0 + CHUNK)
            for g in range(A_GROUPS):
                cg = slice(g * CHUNK, (g + 1) * CHUNK)
                u, gu = _gelu_and_grad(uv_s[rows, cg])
                v, gv = _gelu_and_grad(uv_s[rows, A_WIDTH + g * CHUNK:A_WIDTH + (g + 1) * CHUNK])
                mu = jnp.mean(v, axis=-1, keepdims=True)
                xc = v - mu
                rstd = lax.rsqrt(jnp.mean(xc * xc, axis=-1, keepdims=True) + EPS)
                xhat = xc * rstd
                vc = (xhat * vg_ref[:, cg] + vb_ref[:, cg]).astype(MM)
                sv = _dot(wt_ref[g], vc) + bcol_ref[g]
                u_s[rows, cg] = u
                gu_s[rows, cg] = sv * gu
                gv_s[rows, cg] = rstd * gv
                xh_s[rows, cg] = xhat
                vc_s[rows, cg] = vc
                ycat[rows, cg] = u * sv
            weight_grad(1, DONE)
            kd = _dup_heads(kv_s[r0:r0 + 2 * CHUNK, 0:CHUNK])
            vd = _dup_heads(kv_s[r0:r0 + 2 * CHUNK, CHUNK:2 * CHUNK])
            for kvh in range(2):
                kd_s[blk * 2 + kvh] = kd[kvh]
                vd_s[blk * 2 + kvh] = vd[kvh]
                q2 = _stack_heads(q_s[rows, kvh * CHUNK:(kvh + 1) * CHUNK].astype(F32), lo)
                b_qk.append(_dot_nt(q2, kd[kvh]))
        qks, pcs = [], []
        for g in range(2):
            q128 = q_s[:, 256 + g * CHUNK:256 + (g + 1) * CHUNK].astype(F32)
            for hh in range(2):
                qsel = jnp.where(lot if hh == 0 else ~lot, q128, 0.0).astype(MM)
                qks.append(_dot_nt(qsel, mkv_ref[:, g * CHUNK:(g + 1) * CHUNK]))
        top =lax.broadcasted_iota(jnp.int32, (2 * CHUNK, 1), 0) < CHUNK
        for blk in range(bpt):
            first_add = _first_block_mask(jt * bpt + blk)
            for kvh in range(2):
                sink2 = jnp.where(top, sink_ref[2 * kvh], sink_ref[2 * kvh + 1])
                probs, ps = _swa_probs(b_qk[blk * 2 + kvh], bias_ref[kvh], sink2, first_add)
                pb_s[blk * 2 + kvh] = probs
                ps_s[blk * 2 + kvh] = jnp.broadcast_to(ps, (2 * CHUNK, CHUNK))
                b_pb.append(probs.astype(MM))
        weight_grad(len(dw_cols), DONE)
        for hd in range(4):
            probs = _softmax(qks[hd] * SCALE)
            pc_s[hd] = probs
            pcs.append(probs.astype(MM))
        for blk in range(bpt):
            rows = slice(blk * CHUNK, (blk + 1) * CHUNK)
            for kvh in range(2):
                out2 = _dot(b_pb[blk * 2 + kvh], vd_s[blk * 2 + kvh])
                ycat[rows, YB_OFF + kvh * CHUNK:YB_OFF + (kvh + 1) * CHUNK] = jnp.where(
                    lo, out2[0:CHUNK], out2[CHUNK:2 * CHUNK])
        outs = [_dot(pcs[hd], mkv_ref[:, MEM_LEN + (hd // 2) * CHUNK:MEM_LEN + (hd // 2 + 1) * CHUNK])
                for hd in range(4)]
        for g in range(2):
            ycat[:, YC_OFF + g * CHUNK:YC_OFF + (g + 1) * CHUNK] = jnp.where(lot, outs[2 * g], outs[2 * g + 1])

        zt = z_s[...]
        sig = 1.0 / (1.0 + jnp.exp(-zt))
        silu = zt * sig
        yc = ycat[...]
        yb = (yc * silu).astype(MM)
        pre_norm(xn_ref, xpn_ref)
        o = _dot(yb, wo_ref[...])
        project_z()
        r2 = _rms(o)
        nrm = o * r2
        g2v = g2_ref[...]
        e = x_ref[...] + nrm * g2v - t_ref[...]
        l1 = jnp.sum(e * e, axis=-1, keepdims=True)
        loss_ref[...] += jnp.broadcast_to(jnp.sum(l1, axis=0, keepdims=True) * (0.5 / D_MODEL), loss_ref.shape)
        dxo = e * (1.0 / D_MODEL)
        dxo_s[...] = dxo
        dg2_ref[...] += jnp.sum(dxo * nrm, axis=0, keepdims=True)
        dn = dxo * g2v
        do = r2 * (dn - nrm * jnp.mean(dn * nrm, axis=-1, keepdims=True))
        dob = do.astype(MM)
        dy = _dot_nt(dob, wo_ref[...])
        dp_s[:, Z_COL:IN_WIDTH] = (dy * yc * (sig * (1.0 + zt * (1.0 - sig)))).astype(MM)
        dyc[...] = dy * silu
        acc_o[...] += _dot_tn(yb, dob)

        def in_proj_bwd(c0, c1):
            part = _dot(dp_s[:, c0:c1], wi_ref[c0:c1, :])
            if c0 == Z_COL:
                dh_s[...] = part
            else:
                dh_s[...] += part

        in_proj_bwd(Z_COL, IN_WIDTH)

        for blk in range(bpt):
            r0 = blk * CHUNK
            rows = slice(r0, r0 + CHUNK)
            for g in range(A_GROUPS):
                cg = slice(g * CHUNK, (g + 1) * CHUNK)
                cv = slice(A_WIDTH + g * CHUNK, A_WIDTH + (g + 1) * CHUNK)
                dya = dyc[rows, cg]
                dp_s[rows, cg] = (dya * gu_s[rows, cg]).astype(MM)
                dsv = dya * u_s[rows, cg]
                dsvb = dsv.astype(MM)
                dsv_acc[g] += dsv
                dwsp_ref[g] += _dot_nt(dsvb, vc_s[rows, cg])
                dvc = _dot(wtt_ref[g], dsvb)
                xhat = xh_s[rows, cg]
                dvg_ref[:, cg] += jnp.sum(dvc * xhat, axis=0, keepdims=True)
                dvb_ref[:, cg] += jnp.sum(dvc, axis=0, keepdims=True)
                dxh = dvc * vg_ref[:, cg]
                dv = (dxh - jnp.mean(dxh, axis=-1, keepdims=True)
                      - xhat * jnp.mean(dxh * xhat, axis=-1, keepdims=True))
                dp_s[rows, cv] = (dv * gv_s[rows, cg]).astype(MM)
        in_proj_bwd(0, UV_W)
        b_dosel, b_dp, b_dss = [], [], []
        for blk in range(bpt):
            rows = slice(blk * CHUNK, (blk + 1) * CHUNK)
            for kvh in range(2):
                b_dosel.append(_stack_heads(dyc[rows, YB_OFF + kvh * CHUNK:YB_OFF + (kvh + 1) * CHUNK], lo))
                b_dp.append(_dot_nt(b_dosel[-1], vd_s[blk * 2 + kvh]))
        dosels, dps, dsss = [], [], []
        for hd in range(4):
            do128 = dyc[:, YC_OFF + (hd // 2) * CHUNK:YC_OFF + (hd // 2 + 1) * CHUNK]
            dosels.append(jnp.where(lot if hd % 2 == 0 else ~lot, do128, 0.0).astype(MM))
            dps.append(_dot_nt(dosels[hd], mkv_ref[:, MEM_LEN + (hd // 2) * CHUNK:MEM_LEN + (hd // 2 + 1) * CHUNK]))
        for blk in range(bpt):
            for kvh in range(2):
                probs = pb_s[blk * 2 + kvh]
                dp = b_dp[blk * 2 + kvh]
                delta = jnp.sum(probs * dp, axis=-1, keepdims=True)
                ds = probs * (dp - delta)
                dbias_acc[kvh] += ds
                sd = ps_s[blk * 2 + kvh][:, 0:1] * delta
                for gi in range(2):
                    hd = 2 * kvh + gi
                    dsink_acc[hd:hd + 1, :] += jnp.broadcast_to(
                        -jnp.sum(sd[gi * CHUNK:(gi + 1) * CHUNK], axis=0, keepdims=True), (1, CHUNK))
                b_dss.append((ds * SCALE).astype(MM))
        for hd in range(4):
            probs = pc_s[hd]
            ds = probs * (dps[hd] - jnp.sum(probs * dps[hd], axis=-1, keepdims=True))
            dsss.append((ds * SCALE).astype(MM))
        for blk in range(bpt):
            r0 = blk * CHUNK
            rows = slice(r0, r0 + CHUNK)
            dk_f, dv_f = [], []
            for kvh in range(2):
                dss = b_dss[blk * 2 + kvh]
                q2 = _stack_heads(q_s[rows, kvh * CHUNK:(kvh + 1) * CHUNK].astype(F32), lo)
                dq2 = _dot(dss, kd_s[blk * 2 + kvh])
                dkd = _dot_tn(dss, q2)
                dvd = _dot_tn(pb_s[blk * 2 + kvh].astype(MM), b_dosel[blk * 2 + kvh])
                dp_s[rows, SQ_COL + kvh * CHUNK:SQ_COL + (kvh + 1) * CHUNK] = jnp.where(
                    lo, dq2[0:CHUNK], dq2[CHUNK:2 * CHUNK]).astype(MM)
                dk_f.append(dkd + pltpu.roll(dkd, 64, 1))
                dv_f.append(dvd + pltpu.roll(dvd, 64, 1))
            dkv_acc[r0:r0 + 2 * CHUNK, 0:CHUNK] += jnp.where(lob, dk_f[0], dk_f[1])
            dkv_acc[r0:r0 + 2 * CHUNK, CHUNK:2 * CHUNK] += jnp.where(lob, dv_f[0], dv_f[1])
        dp_s[:, SK_COL:MQ_COL] = dkv_acc[CHUNK:CHUNK + tm, :].astype(MM)
        for g in range(2):
            q128 = q_s[:, 256 + g * CHUNK:256 + (g + 1) * CHUNK].astype(F32)
            k128 = mkv_ref[:, g * CHUNK:(g + 1) * CHUNK]
            dq128 = jnp.zeros((tm, CHUNK), F32)
            dk128 = jnp.zeros((MEM_LEN, CHUNK), F32)
            dv128 = jnp.zeros((MEM_LEN, CHUNK), F32)
            for hh in range(2):
                hd = 2 * g + hh
                half = lot if hh == 0 else ~lot
                qsel = jnp.where(half, q128, 0.0).astype(MM)
                dq128 = dq128 + jnp.where(half, _dot(dsss[hd], k128), 0.0)
                dk128 = dk128 + _dot_tn(dsss[hd], qsel)
                dv128 = dv128 + _dot_tn(pc_s[hd].astype(MM), dosels[hd])
            dp_s[:, MQ_COL + g * CHUNK:MQ_COL + (g + 1) * CHUNK] = dq128.astype(MM)
            dmkv_ref[:, g * CHUNK:(g + 1) * CHUNK] += dk128
            dmkv_ref[:, MEM_LEN + g * CHUNK:MEM_LEN + (g + 1) * CHUNK] += dv128

        in_proj_bwd(SQ_COL, Z_COL)
        project_uv()
        dh = dh_s[...]
        r = r_s[NOW]
        nx = x_ref[...] * r
        dg1_ref[...] += jnp.sum(dh * nx, axis=0, keepdims=True)
        dnx = dh * g1v
        gx_ref[...] = dxo_s[...] + r * (dnx - nx * jnp.mean(dnx * nx, axis=-1, keepdims=True))

        @pl.when(step == last_step)
        def _():
            dw_cols.extend(DW_PIECES)
            weight_grad(len(dw_cols), NOW)
            out_i = pltpu.make_async_copy(acc_i, dwi_hbm, sems.at[0])
            out_o = pltpu.make_async_copy(acc_o, dwo_hbm, sems.at[1])
            out_i.start()
            out_o.start()
            r_ = lax.broadcasted_iota(jnp.int32, (CHUNK, CHUNK), 0)
            c_ = lax.broadcasted_iota(jnp.int32, (CHUNK, CHUNK), 1)
            for g in range(A_GROUPS):
                dwsp_ref[g] = jnp.where(r_ >= c_, dwsp_ref[g], 0.0)
                dbs_ref[g:g + 1, :] = jnp.sum(dsv_acc[g].T, axis=0, keepdims=True)
            rows8 = lax.broadcasted_iota(jnp.int32, (8, CHUNK), 0)
            cols8 = lax.broadcasted_iota(jnp.int32, (8, CHUNK), 1)
            sk = jnp.zeros((8, CHUNK), F32)
            for hd in range(4):
                sk = sk + jnp.where((rows8 == 0) & (cols8 == hd),
                                    jnp.broadcast_to(dsink_acc[hd:hd + 1, :], (8, CHUNK)), 0.0)
            dsink_ref[...] = sk
            bk = bk_ref[...]
            valid = _window_valid()
            rrow = lax.broadcasted_iota(jnp.int32, (N_BUCKETS, CHUNK), 0)
            rcol = lax.broadcasted_iota(jnp.int32, (N_BUCKETS, CHUNK), 1)
            acc = jnp.zeros((N_BUCKETS, CHUNK), F32)
            for bb in range(N_BUCKETS):
                hit = (bk == bb) & valid
                for hd in range(4):
                    dbias = dbias_acc[hd // 2, (hd % 2) * CHUNK:(hd % 2 + 1) * CHUNK, :]
                    part = jnp.sum(jnp.where(hit, dbias, 0.0), axis=-1, keepdims=True)
                    tot = jnp.sum(part, axis=0, keepdims=True)
                    acc = acc + jnp.where((rrow == bb) & (rcol == hd), jnp.broadcast_to(tot, (N_BUCKETS, CHUNK)), 0.0)
            drel_ref[...] = acc
            out_i.wait()
            out_o.wait()

    after = lambda b, j: jnp.minimum(b * nt + j + 1, last_step)
    tile = pl.BlockSpec((tm, D_MODEL), lambda b, j: (tile_at(b * nt + j), 0))
    tile_after = pl.BlockSpec((tm, D_MODEL), lambda b, j: (tile_at(after(b, j)), 0))
    halo = pl.BlockSpec((CHUNK, D_MODEL), lambda b, j: (block_before(b * nt + j), 0))
    halo_after = pl.BlockSpec((CHUNK, D_MODEL), lambda b, j: (block_before(after(b, j)), 0))
    per_batch = lambda r, w: pl.BlockSpec((None, r, w), lambda b, j: (b, 0, 0))
    anyspec = pl.BlockSpec(memory_space=pl.ANY)
    grp = (A_GROUPS, CHUNK, CHUNK)
    return pl.pallas_call(
        body, name="layer", grid=(nb, nt),
        out_shape=(jax.ShapeDtypeStruct((t, D_MODEL), F32),
                   jax.ShapeDtypeStruct((nb, MEM_LEN, 2 * MEM_LEN), F32),
                   jax.ShapeDtypeStruct((IN_WIDTH, D_MODEL), F32),
                   jax.ShapeDtypeStruct((D_MODEL, D_MODEL), F32),
                   jax.ShapeDtypeStruct((1, D_MODEL), F32),
                   jax.ShapeDtypeStruct((1, D_MODEL), F32),
                   jax.ShapeDtypeStruct((8, CHUNK), F32),
                   jax.ShapeDtypeStruct(grp, F32),
                   jax.ShapeDtypeStruct((A_GROUPS, CHUNK), F32),
                   jax.ShapeDtypeStruct((1, A_WIDTH), F32),
                   jax.ShapeDtypeStruct((1, A_WIDTH), F32),
                   jax.ShapeDtypeStruct((8, CHUNK), F32),
                   jax.ShapeDtypeStruct((N_BUCKETS, CHUNK), F32)),
        in_specs=[tile, halo, tile_after, halo_after, tile, per_batch(MEM_LEN, 2 * MEM_LEN),
                  _full((2, 2 * CHUNK, 2 * CHUNK)),
                  pl.BlockSpec(memory_space=pltpu.SMEM),
                  _full((1, A_WIDTH)), _full((1, A_WIDTH)),
                  _full(grp), _full(grp), _full(grp),
                  _full((1, D_MODEL)), _full((1, D_MODEL)),
                  _full((IN_WIDTH, D_MODEL), single=True), _full((D_MODEL, D_MODEL), single=True),
                  _full((CHUNK, 2 * CHUNK))],
        out_specs=(tile, per_batch(MEM_LEN, 2 * MEM_LEN), anyspec, anyspec,
                   _full((1, D_MODEL)), _full((1, D_MODEL)), _full((8, CHUNK)),
                   _full(grp), _full((A_GROUPS, CHUNK)), _full((1, A_WIDTH)), _full((1, A_WIDTH)),
                   _full((8, CHUNK)), _full((N_BUCKETS, CHUNK))),
        scratch_shapes=[pltpu.VMEM((IN_WIDTH, D_MODEL), F32), pltpu.VMEM((D_MODEL, D_MODEL), F32),
                        pltpu.VMEM((tm, UV_W), F32), pltpu.VMEM((tm, Z_W), F32),
                        pltpu.VMEM((tm, 512), MM), pltpu.VMEM((tm + CHUNK, 2 * CHUNK), MM),
                        pltpu.VMEM((3, tm, D_MODEL), MM), pltpu.VMEM((CHUNK, D_MODEL), MM),
                        pltpu.VMEM((tm, IN_WIDTH), MM),
                        pltpu.VMEM((tm, D_MODEL), F32),
                        pltpu.VMEM((tm, D_MODEL), F32), pltpu.VMEM((2, tm, 1), F32),
                        pltpu.VMEM((tm, D_MODEL), F32), pltpu.VMEM((tm, D_MODEL), F32)]
                       + [pltpu.VMEM((tm, A_WIDTH), F32) for _ in range(4)]
                       + [pltpu.VMEM((tm, A_WIDTH), MM),
                          pltpu.VMEM((bpt * 2, 2 * CHUNK, 2 * CHUNK), F32),
                          pltpu.VMEM((bpt * 2, 2 * CHUNK, CHUNK), F32),
                          pltpu.VMEM((4, tm, MEM_LEN), F32),
                          pltpu.VMEM((bpt * 2, 2 * CHUNK, CHUNK), MM),
                          pltpu.VMEM((bpt * 2, 2 * CHUNK, CHUNK), MM),
                          pltpu.VMEM((tm + CHUNK, 2 * CHUNK), F32),
                          pltpu.VMEM((2, 2 * CHUNK, 2 * CHUNK), F32),
                          pltpu.VMEM(grp, F32),
                          pltpu.VMEM((8, CHUNK), F32),
                          pltpu.SemaphoreType.DMA((2,))],
        compiler_params=_params(dimension_semantics=("arbitrary", "arbitrary")),
    )(x2, x2, x2, x2, tgt2, mkv3, bias.reshape(2, 2 * CHUNK, 2 * CHUNK), sinks, vg, vb, wt, wtt, bcol, g1, g2, w_in_t, w_o, buckets)


class _ShardReduce:
    def __init__(self, pos, g, bufs, sems):
        self.x, self.y, self.c = pos
        self.g = g
        self.own, self.rcv, self.sbuf, self.rbuf, self.cbuf = bufs
        self.ld, self.sa, self.ra, self.sb, self.rb = sems
        self.nrow = g.shape[1]
        self.here = (self.x, self.y, self.c)
        self.sib = (self.x, self.y, 1 - self.c)
        self.first, self.second, self.far = _route(*pos)

    def _load(self, q):
        return pltpu.make_async_copy(self.g.at[2 * q + self.c], self.own.at[q], self.ld.at[q])

    def _to_sib(self, q, to):
        return _remote(self.g.at[2 * q + 1 - self.c], self.rcv.at[q], self.sa.at[q], self.ra.at[q], to)

    def _send(self, k, to):
        dst = self.cbuf.at[0] if k == 1 else self.rbuf.at[0 if k == 0 else 1]
        return _remote(self.sbuf.at[k], dst, self.sb.at[k], self.rb.at[k], to)

    def _stage(self, k, which, extra=None):
        def cast(r):
            v = self.rcv[which, r, :]
            if extra is not None:
                v = v + extra[0, r, :].astype(F32)
            self.sbuf[k, r, :] = v.astype(BF16)

        _rows_loop(self.nrow, cast)

    @staticmethod
    def _q(chip):
        return 2 * chip[0] + chip[1]

    def start(self):
        for q in range(4):
            self._load(q).start()
            self._to_sib(q, self.sib).start()

    def mid(self):
        for q in range(4):
            self._load(q).wait()
            self._to_sib(q, self.here).wait_recv()

        def add(r):
            for q in range(4):
                self.rcv[q, r, :] = self.rcv[q, r, :] + self.own[q, r, :]

        _rows_loop(self.nrow, add)
        to_first = (self.first[0], self.first[1], self.c)
        self._stage(0, self._q(self.first))
        self._send(0, to_first).start()
        self._stage(1, self._q(self.far))
        self._send(1, to_first).start()

    def pass_on(self):
        self._send(1, self.here).wait_recv()
        self._stage(2, self._q(self.second), extra=self.cbuf)
        self._send(2, (self.second[0], self.second[1], self.c)).start()

    def finish(self, out):
        self._send(0, self.here).wait_recv()
        self._send(2, self.here).wait_recv()
        which = 2 * self.x + self.y

        def tot(r):
            out[r, :] = (self.rcv[which, r, :] + self.rbuf[0, r, :].astype(F32)) + self.rbuf[1, r, :].astype(F32)

        _rows_loop(self.nrow, tot)
        for q in range(4):
            self._to_sib(q, self.sib).wait_send()
        to_first = (self.first[0], self.first[1], self.c)
        self._send(0, to_first).wait_send()
        self._send(1, to_first).wait_send()
        self._send(2, (self.second[0], self.second[1], self.c)).wait_send()


def _reduce_scratch(shape):
    return [pltpu.VMEM((4,) + shape, F32), pltpu.VMEM((4,) + shape, F32),
            pltpu.VMEM((3,) + shape, BF16), pltpu.VMEM((2,) + shape, BF16), pltpu.VMEM((1,) + shape, BF16),
            pltpu.SemaphoreType.DMA((4,)), pltpu.SemaphoreType.DMA((4,)), pltpu.SemaphoreType.DMA((4,)),
            pltpu.SemaphoreType.DMA((3,)), pltpu.SemaphoreType.DMA((3,))]


_N_RED = 10

_S_LAYOUT = (((1, D_MODEL), 0), ((1, D_MODEL), 8), ((1, D_MODEL), 16),
             ((1, A_WIDTH), 24), ((1, A_WIDTH), 28), ((A_GROUPS, CHUNK), 32),
             ((1, 4), 36), ((N_BUCKETS, 4), 40),
             ((A_GROUPS * CHUNK, CHUNK), 72))
_LOSS_ROW = 37
_W_SP_ROW = _S_LAYOUT[-1][1]
_S_ROWS = _W_SP_ROW + A_GROUPS * CHUNK
_N_SMALL = len(_S_LAYOUT)


def _pack_rows(dst, refs, tile=None):
    for (shp, r0), ref in zip(_S_LAYOUT, refs):
        if tuple(ref.shape) == (shp[1], shp[0]) and shp[0] != shp[1]:
            tile[...] = jnp.zeros_like(tile)
            tile[0:shp[1], 0:shp[0]] = ref[...]
            dst[r0:r0 + shp[0], 0:shp[1]] = tile[...].T[0:shp[0], 0:shp[1]]
        elif shp[0] == 1 and shp[1] >= CHUNK:
            for i in range(shp[1] // CHUNK):
                dst[r0 + i:r0 + i + 1, :] = ref[:, i * CHUNK:(i + 1) * CHUNK]
        elif ref.shape[-1] == CHUNK:
            dst[r0:r0 + shp[0], :] = ref[0:shp[0], :]
        else:
            dst[r0:r0 + shp[0], 0:shp[1]] = ref[...]


def _unpack_rows(src, refs):
    for (shp, r0), ref in zip(_S_LAYOUT, refs):
        if shp[0] == 1 and shp[1] >= CHUNK:
            for i in range(shp[1] // CHUNK):
                ref[:, i * CHUNK:(i + 1) * CHUNK] = src[r0 + i:r0 + i + 1, :]
        elif shp[1] == CHUNK:
            ref[...] = src[r0:r0 + shp[0], :]
        else:
            if tuple(ref.shape) == (shp[1], shp[0]):
                ref[...] = src[r0:r0 + CHUNK, :].T[0:shp[1], 0:shp[0]]
            else:
                ref[...] = src[r0:r0 + shp[0], 0:shp[1]]


_MEM_G = 2


def _greduce(ga, gb, dmkv, mem2, gm, w_mkv, small_g, loss_p):
    shp_c = (SHARD_O, 2 * MEM_LEN)
    shapes = (shp_c, gb.shape[1:], ga.shape[1:])
    rs = _S_ROWS

    def body(*refs):
        it = iter(refs)
        take = lambda n: [next(it) for _ in range(n)]
        gb_ref, ga_ref, d_ref, m_ref, gm_ref, wm_ref = take(6)
        sg_refs = take(_N_SMALL - 1)
        loss_ref, = take(1)
        oc, ob, oa, ogs = take(4)
        red = take(3 * _N_RED)
        gs_ref, rs_a, rs_b, rs_w, gc_ref, dgm_ref = take(6)
        ssem_a, rsem_a, ssem_b, rsem_b = take(4)

        pos = _position()
        x, y, cc = pos
        myq = 2 * x + y
        here, sib = (x, y, cc), (x, y, 1 - cc)
        chips = _other_chips(x, y)
        reducers = [_ShardReduce(pos, g, red[k * _N_RED:k * _N_RED + 5], red[k * _N_RED + 5:(k + 1) * _N_RED])
                    for k, g in enumerate((gc_ref, gb_ref, ga_ref))]
        for rd in reducers[1:]:
            rd.start()

        xf = m_ref[...]
        nm = xf * _rms(xf)
        hm = (nm * gm_ref[...]).astype(MM)
        d = d_ref[...].astype(MM)
        for o in range(N_DEV):
            gc_ref[o] = _dot_tn(hm[:, o * SHARD_O:(o + 1) * SHARD_O], d)
        dgm_ref[...] = jnp.sum(_dot_nt(d, wm_ref[...]) * nm, axis=0, keepdims=True)
        reducers[0].start()

        gs_ref[...] = jnp.zeros_like(gs_ref)
        _pack_rows(gs_ref, sg_refs[:_MEM_G] + [dgm_ref] + sg_refs[_MEM_G:])
        gs_ref[_LOSS_ROW:_LOSS_ROW + 1, :] = loss_ref[0:1, :]
        small_a = _remote(gs_ref, rs_a, ssem_a, rsem_a, sib)
        small_a.start()

        _remote(gs_ref, rs_a, ssem_a, rsem_a, here).wait_recv()
        rs_b[myq] = gs_ref[0:_W_SP_ROW, :] + rs_a[0:_W_SP_ROW, :]
        rs_w[myq] = (gs_ref[_W_SP_ROW:rs, :] + rs_a[_W_SP_ROW:rs, :]).astype(BF16)
        small_b = []
        for j, chip in enumerate(chips):
            to = (chip[0], chip[1], cc)
            small_b.append(_remote(rs_b.at[myq], rs_b.at[myq], ssem_b.at[0, j], rsem_b.at[0, j], to))
            small_b.append(_remote(rs_w.at[myq], rs_w.at[myq], ssem_b.at[1, j], rsem_b.at[1, j], to))
        for cp in small_b:
            cp.start()
        late_last = reducers[1:] + reducers[:1]
        for rd in late_last:
            rd.mid()
        for rd in late_last:
            rd.pass_on()

        for j in range(3):
            _remote(rs_b.at[myq], rs_b.at[myq], ssem_b.at[0, j], rsem_b.at[0, j], here).wait_recv()
            _remote(rs_w.at[myq], rs_w.at[myq], ssem_b.at[1, j], rsem_b.at[1, j], here).wait_recv()
        ogs[0:_W_SP_ROW, :] = ((rs_b[0] + rs_b[1]) + rs_b[2]) + rs_b[3]

        def tot_w(r):
            w = [rs_w[q, r, :].astype(F32) for q in range(4)]
            ogs[pl.ds(pl.multiple_of(_W_SP_ROW + r.start, 8), _ROWS), :] = ((w[0] + w[1]) + w[2]) + w[3]

        _rows_loop(rs - _W_SP_ROW, tot_w)
        for rd, out in zip(late_last, (ob, oa, oc)):
            rd.finish(out)
        small_a.wait_send()
        for cp in small_b:
            cp.wait_send()

    vm = pl.BlockSpec(memory_space=pltpu.VMEM)
    anyspec = pl.BlockSpec(memory_space=pl.ANY)
    scratch = []
    for shp in shapes:
        scratch += _reduce_scratch(shp)
    scratch += [pltpu.VMEM((rs, CHUNK), F32), pltpu.VMEM((rs, CHUNK), F32),
                pltpu.VMEM((4, _W_SP_ROW, CHUNK), F32), pltpu.VMEM((4, rs - _W_SP_ROW, CHUNK), BF16),
                pltpu.VMEM((N_DEV,) + shp_c, F32), pltpu.VMEM((1, D_MODEL), F32),
                pltpu.SemaphoreType.DMA, pltpu.SemaphoreType.DMA,
                pltpu.SemaphoreType.DMA((2, 3)), pltpu.SemaphoreType.DMA((2, 3))]
    tc, tb, ta, ts = pl.pallas_call(
        body, name="greduce",
        out_shape=tuple([jax.ShapeDtypeStruct(shp, F32) for shp in shapes] + [jax.ShapeDtypeStruct((rs, CHUNK), F32)]),
        in_specs=[anyspec] * 2 + [vm] * (4 + _N_SMALL),
        out_specs=(vm, vm, vm, vm),
        scratch_shapes=scratch,
        compiler_params=_params(),
    )(gb, ga, dmkv, mem2, gm, w_mkv, *small_g, loss_p)
    return ta, tb, tc, ts


def _adamw(w, g, m, v):
    m = ADAM_B1 * m + (1.0 - ADAM_B1) * g
    v = ADAM_B2 * v + (1.0 - ADAM_B2) * (g * g)
    m_hat = m / (1.0 - ADAM_B1 ** ADAM_STEP)
    v_hat = v / (1.0 - ADAM_B2 ** ADAM_STEP)
    delta = -ADAM_LR * (m_hat / (jnp.sqrt(v_hat) + ADAM_EPS) + ADAM_WD * w)
    return delta, m, v


def _update(ta, tb, tc, ts, big_wmv, small_wmv):
    shapes = (ta.shape, tb.shape, tc.shape)
    rs = _S_ROWS
    small_shapes = [tuple(a.shape) for a in small_wmv[0]]

    def body(*refs):
        it = iter(refs)
        take = lambda n: [next(it) for _ in range(n)]
        ga_ref, gb_ref, gc_ref, gs_ref = take(4)
        wa, ma, va, wb, mb, vb_, wc, mc, vc = take(9)
        sw_refs, sm_refs, sv_refs = take(_N_SMALL), take(_N_SMALL), take(_N_SMALL)
        oga, oda, oma, ova, ogb, odb, omb, ovb, ogc, odc, omc, ovc = take(12)
        so_refs = [take(_N_SMALL) for _ in range(4)]
        loss_out, = take(1)
        ws, ms, vs, ods, oms, ovs, turn = take(7)

        def update_rows(nrow, rows, g_r, w_r, m_r, v_r, og, od, om, ov):
            def upd(r):
                g = g_r[r, :]
                d, m, v = _adamw(w_r[r, :], g, m_r[r, :], v_r[r, :])
                og[r, :] = g
                od[r, :] = d
                om[r, :] = m
                ov[r, :] = v

            _rows_loop(nrow, upd, rows)

        update_rows(a_rows, 16, ga_ref, wa, ma, va, oga, oda, oma, ova)

        @pl.when(pl.program_id(0) == 0)
        def _():
            update_rows(shapes[1][0], _ROWS, gb_ref, wb, mb, vb_, ogb, odb, omb, ovb)
            update_rows(shapes[2][0], _ROWS, gc_ref, wc, mc, vc, ogc, odc, omc, ovc)
            for buf in (ws, ms, vs):
                buf[...] = jnp.zeros_like(buf)
            _pack_rows(ws, sw_refs, turn)
            _pack_rows(ms, sm_refs, turn)
            _pack_rows(vs, sv_refs, turn)

            def upd_s(i, _):
                r = pl.ds(pl.multiple_of(i * 8, 8), 8)
                d, m, v = _adamw(ws[r, :], gs_ref[r, :], ms[r, :], vs[r, :])
                ods[r, :] = d
                oms[r, :] = m
                ovs[r, :] = v
                return 0

            lax.fori_loop(0, rs // 8, upd_s, 0)
            for k, buf in enumerate((gs_ref, ods, oms, ovs)):
                _unpack_rows(buf, so_refs[k])
            loss_out[...] = gs_ref[_LOSS_ROW:_LOSS_ROW + 1, 0:1]

    n_blocks = 2
    a_rows = shapes[0][0] // n_blocks
    a_spec = pl.BlockSpec((a_rows, shapes[0][1]), lambda i: (i, 0))
    big_out, big_out_specs = [], []
    for shp in shapes:
        big_out += [jax.ShapeDtypeStruct(shp, F32)] * 4
        big_out_specs += [a_spec if shp == shapes[0] else _full(shp)] * 4
    small_out_shapes = [shp[::-1] if shp == (N_BUCKETS, 4) else shp for shp in small_shapes] * 4
    small_out = [jax.ShapeDtypeStruct(shp, F32) for shp in small_out_shapes]
    out_shape = tuple(big_out + small_out + [jax.ShapeDtypeStruct((1, 1), F32)])
    in_specs = ([a_spec, _full(shapes[1]), _full(shapes[2]), _full((rs, CHUNK))]
                + [a_spec] * 3 + [_full(shapes[1])] * 3 + [_full(shapes[2])] * 3
                + [_full(shp) for shp in small_shapes] * 3)
    return pl.pallas_call(
        body, name="update", grid=(n_blocks,),
        out_shape=out_shape,
        in_specs=in_specs,
        out_specs=tuple(big_out_specs + [_full(shp) for shp in small_out_shapes] + [_full((1, 1))]),
        scratch_shapes=[pltpu.VMEM((rs, CHUNK), F32) for _ in range(6)] + [pltpu.VMEM((CHUNK, CHUNK), F32)],
        compiler_params=_params(dimension_semantics=("arbitrary",)),
    )(ta, tb, tc, ts, *big_wmv, *small_wmv[0], *small_wmv[1], *small_wmv[2])


def kernel(x, mem, pre_norm_g, post_norm_g, mem_norm_g, w_in, w_mem_kv, v_norm_g, v_norm_b, w_spatial, b_spatial, attn_sinks, rel_bias, w_out, loss_target, m_pre_norm_g, m_post_norm_g, m_mem_norm_g, m_w_in, m_w_mem_kv, m_v_norm_g, m_v_norm_b, m_w_spatial, m_b_spatial, m_attn_sinks, m_rel_bias, m_w_out, v_pre_norm_g, v_post_norm_g, v_mem_norm_g, v_w_in, v_w_mem_kv, v_v_norm_g, v_v_norm_b, v_w_spatial, v_b_spatial, v_attn_sinks, v_rel_bias, v_w_out):
    sh_a = (w_in[0].T, m_w_in[0].T, v_w_in[0].T)
    sh_b = (w_out[0], m_w_out[0], v_w_out[0])
    sh_c = (w_mem_kv[0], m_w_mem_kv[0], v_w_mem_kv[0])
    nb, s, _ = x.shape
    t = nb * s
    x2 = x.reshape(t, D_MODEL)
    tgt2 = loss_target.reshape(t, D_MODEL)
    mem2 = mem.reshape(nb * MEM_LEN, D_MODEL)
    buckets = jnp.asarray(_t5_buckets())

    wa, wb, wc, bias, wt, wtt, bcol, mkv = _wgather(sh_a[0], sh_b[0], sh_c[0], rel_bias.T, w_spatial[0], b_spatial[0],
                                                    buckets, mem2, mem_norm_g)
    w_mkv = wc.reshape(D_MODEL, 2 * MEM_LEN)
    gx, dmkv, dwi, dwo, dg1, dg2, loss_p, dwsp, dbs, dvg, dvb, dsink, drel = _layer(
        x2, tgt2, mkv.reshape(nb, MEM_LEN, 2 * MEM_LEN), bias, attn_sinks.reshape(4), v_norm_g, v_norm_b, wt, wtt, bcol,
        pre_norm_g, post_norm_g, wa.reshape(IN_WIDTH, D_MODEL), wb.reshape(D_MODEL, D_MODEL), buckets,
        nb, s, min(256, s))
    gx = gx.reshape(nb, s, D_MODEL)
    small_grads = [dg1, dg2, dvg, dvb, dbs, dsink, drel, dwsp.reshape(A_GROUPS * CHUNK, CHUNK)]

    small_names = ["pre_norm_g", "post_norm_g", "mem_norm_g", "v_norm_g", "v_norm_b", "b_spatial", "attn_sinks",
                   "rel_bias", "w_spatial"]
    given = dict(pre_norm_g=(pre_norm_g, m_pre_norm_g, v_pre_norm_g), post_norm_g=(post_norm_g, m_post_norm_g, v_post_norm_g),
                 mem_norm_g=(mem_norm_g, m_mem_norm_g, v_mem_norm_g), v_norm_g=(v_norm_g, m_v_norm_g, v_v_norm_g),
                 v_norm_b=(v_norm_b, m_v_norm_b, v_v_norm_b), b_spatial=(b_spatial, m_b_spatial, v_b_spatial),
                 attn_sinks=(attn_sinks, m_attn_sinks, v_attn_sinks), rel_bias=(rel_bias, m_rel_bias, v_rel_bias),
                 w_spatial=(w_spatial, m_w_spatial, v_w_spatial))
    small_wmv = [[given[n][k].T if n == "rel_bias" else given[n][k].reshape(shp)
                  for n, (shp, _) in zip(small_names, _S_LAYOUT)] for k in range(3)]

    ta, tb, tc, ts = _greduce(dwi.reshape(N_DEV, SHARD_IN, D_MODEL), dwo.reshape(N_DEV, SHARD_O, D_MODEL),
                              dmkv.reshape(nb * MEM_LEN, 2 * MEM_LEN), mem2, mem_norm_g, w_mkv, small_grads, loss_p)
    outs = _update(ta, tb, tc, ts, (*sh_a, *sh_b, *sh_c), small_wmv)
    ra, rb, rc = outs[0:4], outs[4:8], outs[8:12]
    loss = outs[12 + 4 * _N_SMALL].reshape(())

    res = {}
    for k, kind in enumerate(("grad", "delta", "new_m", "new_v")):
        res[kind, "w_in"] = ra[k].T[None]
        res[kind, "w_out"] = rb[k][None]
        res[kind, "w_mem_kv"] = rc[k][None]
        for i, n in enumerate(small_names):
            o = outs[12 + k * _N_SMALL + i]
            res[kind, n] = o.T if n == "rel_bias" else o.reshape(given[n][0].shape)
    order = ["pre_norm_g", "post_norm_g", "mem_norm_g", "w_in", "w_mem_kv", "v_norm_g", "v_norm_b", "w_spatial",
             "b_spatial", "attn_sinks", "rel_bias", "w_out"]
    flat = [res[kind, n] for kind in ("grad", "delta", "new_m", "new_v") for n in order]
    return (loss, gx, *flat)
```

```python
import numpy as np
import jax
import jax.numpy as jnp
from jax import lax
from jax.experimental import pallas as pl
from jax.experimental.pallas import tpu as pltpu

F32 = jnp.float32
BF16 = jnp.bfloat16
MM = jnp.bfloat16

D_MODEL = 1024
CHUNK = 128
A_GROUPS = 4
A_WIDTH = 512
UV_W = 1024
QKV_W = 768
Z_W = 1024
IN_WIDTH = UV_W + QKV_W + Z_W
MEM_LEN = 256
N_BUCKETS = 32
MAX_DISTANCE = 128
EPS = 1e-6
NEG = -1e30
SCALE = 0.125
N_DEV = 8
SHARD_IN = IN_WIDTH // N_DEV
SHARD_O = D_MODEL // N_DEV

SQ_COL, SK_COL, SV_COL, MQ_COL, Z_COL = UV_W, UV_W + 256, UV_W + 384, UV_W + 512, UV_W + QKV_W
DW_PIECES = ((0, SQ_COL), (SQ_COL, Z_COL), (Z_COL, IN_WIDTH))
YB_OFF, YC_OFF = 512, 768

ADAM_LR = 0.001
ADAM_B1 = 0.9
ADAM_B2 = 0.999
ADAM_EPS = 1e-08
ADAM_WD = 0.01
ADAM_STEP = 10

VMEM_LIMIT = 60 * 1024 * 1024

_GELU_C = 0.7978845608028654
_GELU_A = 0.044715

MESH = pl.DeviceIdType.MESH
_ROWS = 32


def _dot(a, b):
    return lax.dot_general(a, b, (((1,), (0,)), ((), ())), preferred_element_type=F32)


def _dot_nt(a, b):
    return lax.dot_general(a, b, (((1,), (1,)), ((), ())), preferred_element_type=F32)


def _dot_tn(a, b):
    return lax.dot_general(a, b, (((0,), (0,)), ((), ())), preferred_element_type=F32)


def _gelu_and_grad(x):
    x2 = x * x
    t = jnp.tanh(x * (_GELU_C + (_GELU_C * _GELU_A) * x2))
    w = 0.5 * t + 0.5
    g = x * w
    dg = w * (1.0 + (x - g) * ((2.0 * _GELU_C) + (6.0 * _GELU_C * _GELU_A) * x2))
    return g, dg


def _t5_buckets():
    qi = np.arange(CHUNK)[:, None]
    kj = np.arange(2 * CHUNK)[None, :]
    n = np.maximum(qi + CHUNK - kj, 0)
    max_exact = N_BUCKETS // 2
    large = max_exact + (np.log(np.maximum(n, 1) / max_exact) / np.log(MAX_DISTANCE / max_exact)
                         * (N_BUCKETS - max_exact)).astype(np.int32)
    large = np.minimum(large, N_BUCKETS - 1)
    return np.where(n < max_exact, n, large).astype(np.int32)


def _params(**kw):
    return pltpu.CompilerParams(vmem_limit_bytes=VMEM_LIMIT, **kw)


def _full(shape, single=False):
    nd = len(shape)
    if single:
        return pl.BlockSpec(shape, lambda *_: (0,) * nd, pipeline_mode=pl.Buffered(1))
    return pl.BlockSpec(shape, lambda *_: (0,) * nd)


def _window_valid():
    qi = lax.broadcasted_iota(jnp.int32, (CHUNK, 2 * CHUNK), 0)
    kj = lax.broadcasted_iota(jnp.int32, (CHUNK, 2 * CHUNK), 1)
    dist = qi + CHUNK - kj
    return (dist >= 0) & (dist < CHUNK)


def _position():
    return lax.axis_index("x"), lax.axis_index("y"), lax.axis_index("c")


def _other_chips(x, y):
    return [(1 - x, y), (x, 1 - y), (1 - x, 1 - y)]


def _route(x, y, c):
    first = (x * c + (1 - x) * (1 - c), y * (1 - c) + (1 - y) * c)
    second = (x * (1 - c) + (1 - x) * c, y * c + (1 - y) * (1 - c))
    return first, second, (1 - x, 1 - y)


def _remote(src, dst, ssem, rsem, to):
    return pltpu.make_async_remote_copy(src_ref=src, dst_ref=dst, send_sem=ssem, recv_sem=rsem,
                                        device_id=to, device_id_type=MESH)


def _rows_loop(nrow, fn, rows=_ROWS):
    assert nrow % rows == 0

    def step(i, _):
        fn(pl.ds(pl.multiple_of(i * rows, rows), rows))
        return 0

    lax.fori_loop(0, nrow // rows, step, 0)


class _Gather:
    def __init__(self, pos, out, ssem, rsem):
        self.x, self.y, self.c = pos
        self.out, self.ssem, self.rsem = out, ssem, rsem
        self.me = 4 * self.x + 2 * self.y + self.c
        self.here = (self.x, self.y, self.c)
        self.sib = (self.x, self.y, 1 - self.c)
        self.first, self.second, self.far = _route(*pos)

    def _copy(self, k, blk, to):
        r = self.out.at[blk]
        return _remote(r, r, self.ssem.at[k], self.rsem.at[k], to)

    def _idx(self, chip, core):
        return 4 * chip[0] + 2 * chip[1] + core

    def _on(self, chip):
        return (chip[0], chip[1], self.c)

    def start(self):
        self._copy(0, self.me, self.sib).start()
        self._copy(1, self.me, self._on(self.first)).start()
        self._copy(2, self.me, self._on(self.second)).start()

    def forward(self):
        c = self.c
        self._copy(1, self._idx(self.first, c), self.here).wait_recv()
        self._copy(3, self._idx(self.first, c), self._on(self.second)).start()
        self._copy(4, self._idx(self.first, c), self.sib).start()
        self._copy(2, self._idx(self.second, c), self.here).wait_recv()
        self._copy(5, self._idx(self.second, c), self.sib).start()
        self._copy(3, self._idx(self.far, c), self.here).wait_recv()
        self._copy(6, self._idx(self.far, c), self.sib).start()

    def finish(self):
        c = self.c
        self._copy(0, self._idx((self.x, self.y), 1 - c), self.here).wait_recv()
        for k, chip in ((4, self.second), (5, self.first), (6, self.far)):
            self._copy(k, self._idx(chip, 1 - c), self.here).wait_recv()
        self._copy(0, self.me, self.sib).wait_send()
        self._copy(1, self.me, self._on(self.first)).wait_send()
        self._copy(2, self.me, self._on(self.second)).wait_send()
        self._copy(3, self._idx(self.first, c), self._on(self.second)).wait_send()
        for k, chip in ((4, self.first), (5, self.second), (6, self.far)):
            self._copy(k, self._idx(chip, c), self.sib).wait_send()


def _prep_tables(rb_ref, w_ref, b_ref, bk_ref, bias_ref, wt_ref, wtt_ref, bcol_ref):
    valid = _window_valid()
    bk = bk_ref[...]
    acc = [jnp.full((CHUNK, 2 * CHUNK), NEG, F32) for _ in range(4)]
    for b in range(N_BUCKETS):
        hit = (bk == b) & valid
        for h in range(4):
            acc[h] = jnp.where(hit, rb_ref[h, b], acc[h])
    for h in range(4):
        bias_ref[h] = acc[h]
    r = lax.broadcasted_iota(jnp.int32, (CHUNK, CHUNK), 0)
    c = lax.broadcasted_iota(jnp.int32, (CHUNK, CHUNK), 1)
    for g in range(A_GROUPS):
        w = jnp.where(r >= c, w_ref[g], 0.0)
        wt_ref[g] = w.astype(MM)
        wtt_ref[g] = w.T.astype(MM)
        bcol_ref[g] = jnp.broadcast_to(b_ref[g:g + 1, :], (CHUNK, CHUNK)).T


def _wgather(a, b, c, rel_bias, w_sp, b_sp, buckets, mem2, gm):
    tmem = mem2.shape[0]

    def body(a_ref, b_ref, c_ref, rb_ref, w_ref, bsp_ref, bk_ref, m_ref, gm_ref,
             oa, ob, oc, bias_ref, wt_ref, wtt_ref, bcol_ref, mkv_ref, ssem, rsem):
        pos = _position()
        me = 4 * pos[0] + 2 * pos[1] + pos[2]
        gathers = []
        for k, (src, out) in enumerate(((c_ref, oc), (b_ref, ob), (a_ref, oa))):
            out[me] = src[...].astype(BF16)
            g = _Gather(pos, out, ssem.at[k], rsem.at[k])
            g.start()
            gathers.append(g)
        _prep_tables(rb_ref, w_ref, bsp_ref, bk_ref, bias_ref, wt_ref, wtt_ref, bcol_ref)
        for g in gathers:
            g.forward()
        gathers[0].finish()
        xf = m_ref[...]
        hm = (xf * _rms(xf) * gm_ref[...]).astype(MM)
        acc = jnp.zeros((tmem, 2 * MEM_LEN), F32)
        for d in range(N_DEV):
            acc = acc + _dot(hm[:, d * SHARD_O:(d + 1) * SHARD_O], oc[d])
        mkv_ref[...] = acc.astype(MM)
        for g in gathers[1:]:
            g.finish()

    vm = pl.BlockSpec(memory_space=pltpu.VMEM)
    grp = (A_GROUPS, CHUNK, CHUNK)
    return pl.pallas_call(
        body, name="wgather",
        out_shape=(jax.ShapeDtypeStruct((N_DEV,) + a.shape, BF16),
                   jax.ShapeDtypeStruct((N_DEV,) + b.shape, BF16),
                   jax.ShapeDtypeStruct((N_DEV,) + c.shape, BF16),
                   jax.ShapeDtypeStruct((4, CHUNK, 2 * CHUNK), F32),
                   jax.ShapeDtypeStruct(grp, MM), jax.ShapeDtypeStruct(grp, MM), jax.ShapeDtypeStruct(grp, F32),
                   jax.ShapeDtypeStruct((tmem, 2 * MEM_LEN), MM)),
        in_specs=[vm, vm, vm, pl.BlockSpec(memory_space=pltpu.SMEM), vm, vm, vm, vm, vm],
        out_specs=tuple([vm] * 8),
        scratch_shapes=[pltpu.SemaphoreType.DMA((3, 7)), pltpu.SemaphoreType.DMA((3, 7))],
        compiler_params=_params(),
    )(a, b, c, rel_bias, w_sp, b_sp, buckets, mem2, gm)


def _half_masks(rows):
    lane = lax.broadcasted_iota(jnp.int32, (rows, CHUNK), 1)
    return lane < 64


def _dup_heads(band):
    b32 = band.astype(F32)
    rolled = pltpu.roll(b32, 64, 1)
    lo = _half_masks(band.shape[0])
    return (jnp.where(lo, b32, rolled).astype(MM), jnp.where(lo, rolled, b32).astype(MM))


def _swa_probs(qk, bias_h, sink_h, first_add):
    s = qk * SCALE + bias_h + first_add
    m = jnp.maximum(jnp.max(s, axis=-1, keepdims=True), sink_h)
    p = jnp.exp(s - m)
    es = jnp.exp(sink_h - m)
    inv = 1.0 / (jnp.sum(p, axis=-1, keepdims=True) + es)
    return p * inv, es * inv


def _softmax(s):
    m = jnp.max(s, axis=-1, keepdims=True)
    p = jnp.exp(s - m)
    return p * (1.0 / jnp.sum(p, axis=-1, keepdims=True))


def _first_block_mask(n):
    col = lax.broadcasted_iota(jnp.int32, (2 * CHUNK, 2 * CHUNK), 1)
    return jnp.where((col < CHUNK) & (n == 0), NEG, 0.0)


def _stack_heads(x128, lo):
    return jnp.concatenate([jnp.where(lo, x128, 0.0), jnp.where(lo, 0.0, x128)], axis=0).astype(MM)


def _rms(xf):
    return lax.rsqrt(jnp.mean(xf * xf, axis=-1, keepdims=True) + EPS)


def _layer(x2, tgt2, mkv3, bias, sinks, vg, vb, wt, wtt, bcol, g1, g2, w_in_t, w_o, buckets, nb, s, tm):
    nt = s // tm
    bpt = tm // CHUNK
    bps = s // CHUNK
    t = nb * s
    last_step = nb * nt - 1

    def tile_at(step):
        return (step // nt) * nt + nt - 1 - step % nt

    def block_before(step):
        return (step // nt) * bps + jnp.maximum((nt - 1 - step % nt) * bpt - 1, 0)

    def body(x_ref, xp_ref, xn_ref, xpn_ref, t_ref, mkv_ref, bias_ref, sink_ref, vg_ref, vb_ref,
             wt_ref, wtt_ref, bcol_ref, g1_ref, g2_ref, wi_ref, wo_ref, bk_ref,
             gx_ref, dmkv_ref, dwi_hbm, dwo_hbm, dg1_ref, dg2_ref, loss_ref, dwsp_ref, dbs_ref,
             dvg_ref, dvb_ref, dsink_ref, drel_ref,
             acc_i, acc_o, uv_s, z_s, q_s, kv_s, h_s, hp_s, dp_s, dxo_s, dh_s, r_s,
             ycat, dyc, u_s, gu_s, gv_s, xh_s, vc_s, pb_s, ps_s, pc_s, kd_s, vd_s,
             dkv_acc, dbias_acc, dsv_acc, dsink_acc, sems):
        b, j = pl.program_id(0), pl.program_id(1)
        jt = nt - 1 - j
        step = b * nt + j
        g1v = g1_ref[...]
        NOW, NEXT, DONE = 0, 1, 2
        dw_cols = list(DW_PIECES)

        def weight_grad(n, slot):
            for c0, c1 in dw_cols[:n]:
                acc_i[c0:c1, :] += _dot_tn(dp_s[:, c0:c1], h_s[slot])
            del dw_cols[:n]

        def pre_norm(x_tile, x_before):
            xf = x_tile[...]
            r_s[NEXT] = _rms(xf)
            h_s[NEXT] = (xf * r_s[NEXT] * g1v).astype(MM)
            xp = x_before[...]
            hp_s[...] = (xp * _rms(xp) * g1v).astype(MM)

        def project_z():
            z_s[...] = _dot_nt(h_s[NEXT], wi_ref[Z_COL:IN_WIDTH, :])

        def project_uv():
            uv_s[...] = _dot_nt(h_s[NEXT], wi_ref[0:UV_W, :])

        @pl.when(step == 0)
        def _():
            for ref in (acc_i, acc_o, dg1_ref, dg2_ref, loss_ref, dwsp_ref, dvg_ref, dvb_ref,
                        dbias_acc, dsv_acc, dsink_acc):
                ref[...] = jnp.zeros_like(ref)
            dp_s[...] = jnp.zeros_like(dp_s)
            h_s[NOW] = jnp.zeros((tm, D_MODEL), MM)
            pre_norm(x_ref, xp_ref)
            project_z()
            project_uv()

        h_s[DONE] = h_s[NOW]
        r_s[NOW] = r_s[NEXT]
        h = h_s[NEXT]
        h_s[NOW] = h
        hp = hp_s[...]

        @pl.when(j == 0)
        def _():
            dmkv_ref[...] = jnp.zeros_like(dmkv_ref)
            dkv_acc[...] = jnp.zeros_like(dkv_acc)

        carry = dkv_acc[0:CHUNK, :]
        dkv_acc[...] = jnp.zeros_like(dkv_acc)
        dkv_acc[tm:tm + CHUNK, :] = carry

        lo = _half_masks(CHUNK)
        lob = _half_masks(2 * CHUNK)
        lot = _half_masks(tm)

        qkv = _dot_nt(h, wi_ref[SQ_COL:Z_COL, :])
        q_s[:, 0:256] = qkv[:, 0:256].astype(MM)
        q_s[:, 256:512] = qkv[:, 512:768].astype(MM)
        kv_s[CHUNK:CHUNK + tm, :] = qkv[:, 256:512].astype(MM)
        kv_s[0:CHUNK, :] = _dot_nt(hp, wi_ref[SK_COL:MQ_COL, :]).astype(MM)

        weight_grad(1, DONE)
        b_qk, b_pb = [], []
        for blk in range(bpt):
            r0 = blk * CHUNK
            rows = slice(r0, r0 + CHUNK)
            for g in range(A_GROUPS):
                cg = slice(g * CHUNK, (g + 1) * CHUNK)
                u, gu = _gelu_and_grad(uv_s[rows, cg])
                v, gv = _gelu_and_grad(uv_s[rows, A_WIDTH + g * CHUNK:A_WIDTH + (g + 1) * CHUNK])
                mu = jnp.mean(v, axis=-1, keepdims=True)
                xc = v - mu
                rstd = lax.rsqrt(jnp.mean(xc * xc, axis=-1, keepdims=True) + EPS)
                xhat = xc * rstd
                vc = (xhat * vg_ref[:, cg] + vb_ref[:, cg]).astype(MM)
                sv = _dot(wt_ref[g], vc) + bcol_ref[g]
                u_s[rows, cg] = u
                gu_s[rows, cg] = sv * gu
                gv_s[rows, cg] = rstd * gv
                xh_s[rows, cg] = xhat
                vc_s[rows, cg] = vc
                ycat[rows, cg] = u * sv
            weight_grad(1, DONE)
            kd = _dup_heads(kv_s[r0:r0 + 2 * CHUNK, 0:CHUNK])
            vd = _dup_heads(kv_s[r0:r0 + 2 * CHUNK, CHUNK:2 * CHUNK])
            for kvh in range(2):
                kd_s[blk * 2 + kvh] = kd[kvh]
                vd_s[blk * 2 + kvh] = vd[kvh]
                q2 = _stack_heads(q_s[rows, kvh * CHUNK:(kvh + 1) * CHUNK].astype(F32), lo)
                b_qk.append(_dot_nt(q2, kd[kvh]))
        qks, pcs = [], []
        for g in range(2):
            q128 = q_s[:, 256 + g * CHUNK:256 + (g + 1) * CHUNK].astype(F32)
            for hh in range(2):
                qsel = jnp.where(lot if hh == 0 else ~lot, q128, 0.0).astype(MM)
                qks.append(_dot_nt(qsel, mkv_ref[:, g * CHUNK:(g + 1) * CHUNK]))
        top =lax.broadcasted_iota(jnp.int32, (2 * CHUNK, 1), 0) < CHUNK
        for blk in range(bpt):
            first_add = _first_block_mask(jt * bpt + blk)
            for kvh in range(2):
                sink2 = jnp.where(top, sink_ref[2 * kvh], sink_ref[2 * kvh + 1])
                probs, ps = _swa_probs(b_qk[blk * 2 + kvh], bias_ref[kvh], sink2, first_add)
                pb_s[blk * 2 + kvh] = probs
                ps_s[blk * 2 + kvh] = jnp.broadcast_to(ps, (2 * CHUNK, CHUNK))
                b_pb.append(probs.astype(MM))
        weight_grad(len(dw_cols), DONE)
        for hd in range(4):
            probs = _softmax(qks[hd] * SCALE)
            pc_s[hd] = probs
            pcs.append(probs.astype(MM))
        for blk in range(bpt):
            rows = slice(blk * CHUNK, (blk + 1) * CHUNK)
            for kvh in range(2):
                out2 = _dot(b_pb[blk * 2 + kvh], vd_s[blk * 2 + kvh])
                ycat[rows, YB_OFF + kvh * CHUNK:YB_OFF + (kvh + 1) * CHUNK] = jnp.where(
                    lo, out2[0:CHUNK], out2[CHUNK:2 * CHUNK])
        outs = [_dot(pcs[hd], mkv_ref[:, MEM_LEN + (hd // 2) * CHUNK:MEM_LEN + (hd // 2 + 1) * CHUNK])
                for hd in range(4)]
        for g in range(2):
            ycat[:, YC_OFF + g * CHUNK:YC_OFF + (g + 1) * CHUNK] = jnp.where(lot, outs[2 * g], outs[2 * g + 1])

        zt = z_s[...]
        sig = 1.0 / (1.0 + jnp.exp(-zt))
        silu = zt * sig
        yc = ycat[...]
        yb = (yc * silu).astype(MM)
        pre_norm(xn_ref, xpn_ref)
        o = _dot(yb, wo_ref[...])
        project_z()
        r2 = _rms(o)
        nrm = o * r2
        g2v = g2_ref[...]
        e = x_ref[...] + nrm * g2v - t_ref[...]
        l1 = jnp.sum(e * e, axis=-1, keepdims=True)
        loss_ref[...] += jnp.broadcast_to(jnp.sum(l1, axis=0, keepdims=True) * (0.5 / D_MODEL), loss_ref.shape)
        dxo = e * (1.0 / D_MODEL)
        dxo_s[...] = dxo
        dg2_ref[...] += jnp.sum(dxo * nrm, axis=0, keepdims=True)
        dn = dxo * g2v
        do = r2 * (dn - nrm * jnp.mean(dn * nrm, axis=-1, keepdims=True))
        dob = do.astype(MM)
        dy = _dot_nt(dob, wo_ref[...])
        dp_s[:, Z_COL:IN_WIDTH] = (dy * yc * (sig * (1.0 + zt * (1.0 - sig)))).astype(MM)
        dyc[...] = dy * silu
        acc_o[...] += _dot_tn(yb, dob)

        def in_proj_bwd(c0, c1):
            part = _dot(dp_s[:, c0:c1], wi_ref[c0:c1, :])
            if c0 == Z_COL:
                dh_s[...] = part
            else:
                dh_s[...] += part

        in_proj_bwd(Z_COL, IN_WIDTH)

        for blk in range(bpt):
            r0 = blk * CHUNK
            rows = slice(r0, r0 + CHUNK)
            for g in range(A_GROUPS):
                cg = slice(g * CHUNK, (g + 1) * CHUNK)
                cv = slice(A_WIDTH + g * CHUNK, A_WIDTH + (g + 1) * CHUNK)
                dya = dyc[rows, cg]
                dp_s[rows, cg] = (dya * gu_s[rows, cg]).astype(MM)
                dsv = dya * u_s[rows, cg]
                dsvb = dsv.astype(MM)
                dsv_acc[g] += dsv
                dwsp_ref[g] += _dot_nt(dsvb, vc_s[rows, cg])
                dvc = _dot(wtt_ref[g], dsvb)
                xhat = xh_s[rows, cg]
                dvg_ref[:, cg] += jnp.sum(dvc * xhat, axis=0, keepdims=True)
                dvb_ref[:, cg] += jnp.sum(dvc, axis=0, keepdims=True)
                dxh = dvc * vg_ref[:, cg]
                dv = (dxh - jnp.mean(dxh, axis=-1, keepdims=True)
                      - xhat * jnp.mean(dxh * xhat, axis=-1, keepdims=True))
                dp_s[rows, cv] = (dv * gv_s[rows, cg]).astype(MM)
        in_proj_bwd(0, UV_W)
        b_dosel, b_dp, b_dss = [], [], []
        for blk in range(bpt):
            rows = slice(blk * CHUNK, (blk + 1) * CHUNK)
            for kvh in range(2):
                b_dosel.append(_stack_heads(dyc[rows, YB_OFF + kvh * CHUNK:YB_OFF + (kvh + 1) * CHUNK], lo))
                b_dp.append(_dot_nt(b_dosel[-1], vd_s[blk * 2 + kvh]))
        dosels, dps, dsss = [], [], []
        for hd in range(4):
            do128 = dyc[:, YC_OFF + (hd // 2) * CHUNK:YC_OFF + (hd // 2 + 1) * CHUNK]
            dosels.append(jnp.where(lot if hd % 2 == 0 else ~lot, do128, 0.0).astype(MM))
            dps.append(_dot_nt(dosels[hd], mkv_ref[:, MEM_LEN + (hd // 2) * CHUNK:MEM_LEN + (hd // 2 + 1) * CHUNK]))
        for blk in range(bpt):
            for kvh in range(2):
                probs = pb_s[blk * 2 + kvh]
                dp = b_dp[blk * 2 + kvh]
                delta = jnp.sum(probs * dp, axis=-1, keepdims=True)
                ds = probs * (dp - delta)
                dbias_acc[kvh] += ds
                sd = ps_s[blk * 2 + kvh][:, 0:1] * delta
                for gi in range(2):
                    hd = 2 * kvh + gi
                    dsink_acc[hd:hd + 1, :] += jnp.broadcast_to(
                        -jnp.sum(sd[gi * CHUNK:(gi + 1) * CHUNK], axis=0, keepdims=True), (1, CHUNK))
                b_dss.append((ds * SCALE).astype(MM))
        for hd in range(4):
            probs = pc_s[hd]
            ds = probs * (dps[hd] - jnp.sum(probs * dps[hd], axis=-1, keepdims=True))
            dsss.append((ds * SCALE).astype(MM))
        for blk in range(bpt):
            r0 = blk * CHUNK
            rows = slice(r0, r0 + CHUNK)
            dk_f, dv_f = [], []
            for kvh in range(2):
                dss = b_dss[blk * 2 + kvh]
                q2 = _stack_heads(q_s[rows, kvh * CHUNK:(kvh + 1) * CHUNK].astype(F32), lo)
                dq2 = _dot(dss, kd_s[blk * 2 + kvh])
                dkd = _dot_tn(dss, q2)
                dvd = _dot_tn(pb_s[blk * 2 + kvh].astype(MM), b_dosel[blk * 2 + kvh])
                dp_s[rows, SQ_COL + kvh * CHUNK:SQ_COL + (kvh + 1) * CHUNK] = jnp.where(
                    lo, dq2[0:CHUNK], dq2[CHUNK:2 * CHUNK]).astype(MM)
                dk_f.append(dkd + pltpu.roll(dkd, 64, 1))
                dv_f.append(dvd + pltpu.roll(dvd, 64, 1))
            dkv_acc[r0:r0 + 2 * CHUNK, 0:CHUNK] += jnp.where(lob, dk_f[0], dk_f[1])
            dkv_acc[r0:r0 + 2 * CHUNK, CHUNK:2 * CHUNK] += jnp.where(lob, dv_f[0], dv_f[1])
        dp_s[:, SK_COL:MQ_COL] = dkv_acc[CHUNK:CHUNK + tm, :].astype(MM)
        for g in range(2):
            q128 = q_s[:, 256 + g * CHUNK:256 + (g + 1) * CHUNK].astype(F32)
            k128 = mkv_ref[:, g * CHUNK:(g + 1) * CHUNK]
            dq128 = jnp.zeros((tm, CHUNK), F32)
            dk128 = jnp.zeros((MEM_LEN, CHUNK), F32)
            dv128 = jnp.zeros((MEM_LEN, CHUNK), F32)
            for hh in range(2):
                hd = 2 * g + hh
                half = lot if hh == 0 else ~lot
                qsel = jnp.where(half, q128, 0.0).astype(MM)
                dq128 = dq128 + jnp.where(half, _dot(dsss[hd], k128), 0.0)
                dk128 = dk128 + _dot_tn(dsss[hd], qsel)
                dv128 = dv128 + _dot_tn(pc_s[hd].astype(MM), dosels[hd])
            dp_s[:, MQ_COL + g * CHUNK:MQ_COL + (g + 1) * CHUNK] = dq128.astype(MM)
            dmkv_ref[:, g * CHUNK:(g + 1) * CHUNK] += dk128
            dmkv_ref[:, MEM_LEN + g * CHUNK:MEM_LEN + (g + 1) * CHUNK] += dv128

        in_proj_bwd(SQ_COL, Z_COL)
        project_uv()
        dh = dh_s[...]
        r = r_s[NOW]
        nx = x_ref[...] * r
        dg1_ref[...] += jnp.sum(dh * nx, axis=0, keepdims=True)
        dnx = dh * g1v
        gx_ref[...] = dxo_s[...] + r * (dnx - nx * jnp.mean(dnx * nx, axis=-1, keepdims=True))

        @pl.when(step == last_step)
        def _():
            dw_cols.extend(DW_PIECES)
            weight_grad(len(dw_cols), NOW)
            out_i = pltpu.make_async_copy(acc_i, dwi_hbm, sems.at[0])
            out_o = pltpu.make_async_copy(acc_o, dwo_hbm, sems.at[1])
            out_i.start()
            out_o.start()
            r_ = lax.broadcasted_iota(jnp.int32, (CHUNK, CHUNK), 0)
            c_ = lax.broadcasted_iota(jnp.int32, (CHUNK, CHUNK), 1)
            for g in range(A_GROUPS):
                dwsp_ref[g] = jnp.where(r_ >= c_, dwsp_ref[g], 0.0)
                dbs_ref[g:g + 1, :] = jnp.sum(dsv_acc[g].T, axis=0, keepdims=True)
            rows8 = lax.broadcasted_iota(jnp.int32, (8, CHUNK), 0)
            cols8 = lax.broadcasted_iota(jnp.int32, (8, CHUNK), 1)
            sk = jnp.zeros((8, CHUNK), F32)
            for hd in range(4):
                sk = sk + jnp.where((rows8 == 0) & (cols8 == hd),
                                    jnp.broadcast_to(dsink_acc[hd:hd + 1, :], (8, CHUNK)), 0.0)
            dsink_ref[...] = sk
            bk = bk_ref[...]
            valid = _window_valid()
            rrow = lax.broadcasted_iota(jnp.int32, (N_BUCKETS, CHUNK), 0)
            rcol = lax.broadcasted_iota(jnp.int32, (N_BUCKETS, CHUNK), 1)
            acc = jnp.zeros((N_BUCKETS, CHUNK), F32)
            for bb in range(N_BUCKETS):
                hit = (bk == bb) & valid
                for hd in range(4):
                    dbias = dbias_acc[hd // 2, (hd % 2) * CHUNK:(hd % 2 + 1) * CHUNK, :]
                    part = jnp.sum(jnp.where(hit, dbias, 0.0), axis=-1, keepdims=True)
                    tot = jnp.sum(part, axis=0, keepdims=True)
                    acc = acc + jnp.where((rrow == bb) & (rcol == hd), jnp.broadcast_to(tot, (N_BUCKETS, CHUNK)), 0.0)
            drel_ref[...] = acc
            out_i.wait()
            out_o.wait()

    after = lambda b, j: jnp.minimum(b * nt + j + 1, last_step)
    tile = pl.BlockSpec((tm, D_MODEL), lambda b, j: (tile_at(b * nt + j), 0))
    tile_after = pl.BlockSpec((tm, D_MODEL), lambda b, j: (tile_at(after(b, j)), 0))
    halo = pl.BlockSpec((CHUNK, D_MODEL), lambda b, j: (block_before(b * nt + j), 0))
    halo_after = pl.BlockSpec((CHUNK, D_MODEL), lambda b, j: (block_before(after(b, j)), 0))
    per_batch = lambda r, w: pl.BlockSpec((None, r, w), lambda b, j: (b, 0, 0))
    anyspec = pl.BlockSpec(memory_space=pl.ANY)
    grp = (A_GROUPS, CHUNK, CHUNK)
    return pl.pallas_call(
        body, name="layer", grid=(nb, nt),
        out_shape=(jax.ShapeDtypeStruct((t, D_MODEL), F32),
                   jax.ShapeDtypeStruct((nb, MEM_LEN, 2 * MEM_LEN), F32),
                   jax.ShapeDtypeStruct((IN_WIDTH, D_MODEL), F32),
                   jax.ShapeDtypeStruct((D_MODEL, D_MODEL), F32),
                   jax.ShapeDtypeStruct((1, D_MODEL), F32),
                   jax.ShapeDtypeStruct((1, D_MODEL), F32),
                   jax.ShapeDtypeStruct((8, CHUNK), F32),
                   jax.ShapeDtypeStruct(grp, F32),
                   jax.ShapeDtypeStruct((A_GROUPS, CHUNK), F32),
                   jax.ShapeDtypeStruct((1, A_WIDTH), F32),
                   jax.ShapeDtypeStruct((1, A_WIDTH), F32),
                   jax.ShapeDtypeStruct((8, CHUNK), F32),
                   jax.ShapeDtypeStruct((N_BUCKETS, CHUNK), F32)),
        in_specs=[tile, halo, tile_after, halo_after, tile, per_batch(MEM_LEN, 2 * MEM_LEN),
                  _full((2, 2 * CHUNK, 2 * CHUNK)),
                  pl.BlockSpec(memory_space=pltpu.SMEM),
                  _full((1, A_WIDTH)), _full((1, A_WIDTH)),
                  _full(grp), _full(grp), _full(grp),
                  _full((1, D_MODEL)), _full((1, D_MODEL)),
                  _full((IN_WIDTH, D_MODEL), single=True), _full((D_MODEL, D_MODEL), single=True),
                  _full((CHUNK, 2 * CHUNK))],
        out_specs=(tile, per_batch(MEM_LEN, 2 * MEM_LEN), anyspec, anyspec,
                   _full((1, D_MODEL)), _full((1, D_MODEL)), _full((8, CHUNK)),
                   _full(grp), _full((A_GROUPS, CHUNK)), _full((1, A_WIDTH)), _full((1, A_WIDTH)),
                   _full((8, CHUNK)), _full((N_BUCKETS, CHUNK))),
        scratch_shapes=[pltpu.VMEM((IN_WIDTH, D_MODEL), F32), pltpu.VMEM((D_MODEL, D_MODEL), F32),
                        pltpu.VMEM((tm, UV_W), F32), pltpu.VMEM((tm, Z_W), F32),
                        pltpu.VMEM((tm, 512), MM), pltpu.VMEM((tm + CHUNK, 2 * CHUNK), MM),
                        pltpu.VMEM((3, tm, D_MODEL), MM), pltpu.VMEM((CHUNK, D_MODEL), MM),
                        pltpu.VMEM((tm, IN_WIDTH), MM),
                        pltpu.VMEM((tm, D_MODEL), F32),
                        pltpu.VMEM((tm, D_MODEL), F32), pltpu.VMEM((2, tm, 1), F32),
                        pltpu.VMEM((tm, D_MODEL), F32), pltpu.VMEM((tm, D_MODEL), F32)]
                       + [pltpu.VMEM((tm, A_WIDTH), F32) for _ in range(4)]
                       + [pltpu.VMEM((tm, A_WIDTH), MM),
                          pltpu.VMEM((bpt * 2, 2 * CHUNK, 2 * CHUNK), F32),
                          pltpu.VMEM((bpt * 2, 2 * CHUNK, CHUNK), F32),
                          pltpu.VMEM((4, tm, MEM_LEN), F32),
                          pltpu.VMEM((bpt * 2, 2 * CHUNK, CHUNK), MM),
                          pltpu.VMEM((bpt * 2, 2 * CHUNK, CHUNK), MM),
                          pltpu.VMEM((tm + CHUNK, 2 * CHUNK), F32),
                          pltpu.VMEM((2, 2 * CHUNK, 2 * CHUNK), F32),
                          pltpu.VMEM(grp, F32),
                          pltpu.VMEM((8, CHUNK), F32),
                          pltpu.SemaphoreType.DMA((2,))],
        compiler_params=_params(dimension_semantics=("arbitrary", "arbitrary")),
    )(x2, x2, x2, x2, tgt2, mkv3, bias.reshape(2, 2 * CHUNK, 2 * CHUNK), sinks, vg, vb, wt, wtt, bcol, g1, g2, w_in_t, w_o, buckets)


class _ShardReduce:
    def __init__(self, pos, g, bufs, sems):
        self.x, self.y, self.c = pos
        self.g = g
        self.own, self.rcv, self.sbuf, self.rbuf, self.cbuf = bufs
        self.ld, self.sa, self.ra, self.sb, self.rb = sems
        self.nrow = g.shape[1]
        self.here = (self.x, self.y, self.c)
        self.sib = (self.x, self.y, 1 - self.c)
        self.first, self.second, self.far = _route(*pos)

    def _load(self, q):
        return pltpu.make_async_copy(self.g.at[2 * q + self.c], self.own.at[q], self.ld.at[q])

    def _to_sib(self, q, to):
        return _remote(self.g.at[2 * q + 1 - self.c], self.rcv.at[q], self.sa.at[q], self.ra.at[q], to)

    def _send(self, k, to):
        dst = self.cbuf.at[0] if k == 1 else self.rbuf.at[0 if k == 0 else 1]
        return _remote(self.sbuf.at[k], dst, self.sb.at[k], self.rb.at[k], to)

    def _stage(self, k, which, extra=None):
        def cast(r):
            v = self.rcv[which, r, :]
            if extra is not None:
                v = v + extra[0, r, :].astype(F32)
            self.sbuf[k, r, :] = v.astype(BF16)

        _rows_loop(self.nrow, cast)

    @staticmethod
    def _q(chip):
        return 2 * chip[0] + chip[1]

    def start(self):
        for q in range(4):
            self._load(q).start()
            self._to_sib(q, self.sib).start()

    def mid(self):
        for q in range(4):
            self._load(q).wait()
            self._to_sib(q, self.here).wait_recv()

        def add(r):
            for q in range(4):
                self.rcv[q, r, :] = self.rcv[q, r, :] + self.own[q, r, :]

        _rows_loop(self.nrow, add)
        to_first = (self.first[0], self.first[1], self.c)
        self._stage(0, self._q(self.first))
        self._send(0, to_first).start()
        self._stage(1, self._q(self.far))
        self._send(1, to_first).start()

    def pass_on(self):
        self._send(1, self.here).wait_recv()
        self._stage(2, self._q(self.second), extra=self.cbuf)
        self._send(2, (self.second[0], self.second[1], self.c)).start()

    def finish(self, out):
        self._send(0, self.here).wait_recv()
        self._send(2, self.here).wait_recv()
        which = 2 * self.x + self.y

        def tot(r):
            out[r, :] = (self.rcv[which, r, :] + self.rbuf[0, r, :].astype(F32)) + self.rbuf[1, r, :].astype(F32)

        _rows_loop(self.nrow, tot)
        for q in range(4):
            self._to_sib(q, self.sib).wait_send()
        to_first = (self.first[0], self.first[1], self.c)
        self._send(0, to_first).wait_send()
        self._send(1, to_first).wait_send()
        self._send(2, (self.second[0], self.second[1], self.c)).wait_send()


def _reduce_scratch(shape):
    return [pltpu.VMEM((4,) + shape, F32), pltpu.VMEM((4,) + shape, F32),
            pltpu.VMEM((3,) + shape, BF16), pltpu.VMEM((2,) + shape, BF16), pltpu.VMEM((1,) + shape, BF16),
            pltpu.SemaphoreType.DMA((4,)), pltpu.SemaphoreType.DMA((4,)), pltpu.SemaphoreType.DMA((4,)),
            pltpu.SemaphoreType.DMA((3,)), pltpu.SemaphoreType.DMA((3,))]


_N_RED = 10

_S_LAYOUT = (((1, D_MODEL), 0), ((1, D_MODEL), 8), ((1, D_MODEL), 16),
             ((1, A_WIDTH), 24), ((1, A_WIDTH), 28), ((A_GROUPS, CHUNK), 32),
             ((1, 4), 36), ((N_BUCKETS, 4), 40),
             ((A_GROUPS * CHUNK, CHUNK), 72))
_LOSS_ROW = 37
_W_SP_ROW = _S_LAYOUT[-1][1]
_S_ROWS = _W_SP_ROW + A_GROUPS * CHUNK
_N_SMALL = len(_S_LAYOUT)


def _pack_rows(dst, refs, tile=None):
    for (shp, r0), ref in zip(_S_LAYOUT, refs):
        if tuple(ref.shape) == (shp[1], shp[0]) and shp[0] != shp[1]:
            tile[...] = jnp.zeros_like(tile)
            tile[0:shp[1], 0:shp[0]] = ref[...]
            dst[r0:r0 + shp[0], 0:shp[1]] = tile[...].T[0:shp[0], 0:shp[1]]
        elif shp[0] == 1 and shp[1] >= CHUNK:
            for i in range(shp[1] // CHUNK):
                dst[r0 + i:r0 + i + 1, :] = ref[:, i * CHUNK:(i + 1) * CHUNK]
        elif ref.shape[-1] == CHUNK:
            dst[r0:r0 + shp[0], :] = ref[0:shp[0], :]
        else:
            dst[r0:r0 + shp[0], 0:shp[1]] = ref[...]


def _unpack_rows(src, refs):
    for (shp, r0), ref in zip(_S_LAYOUT, refs):
        if shp[0] == 1 and shp[1] >= CHUNK:
            for i in range(shp[1] // CHUNK):
                ref[:, i * CHUNK:(i + 1) * CHUNK] = src[r0 + i:r0 + i + 1, :]
        elif shp[1] == CHUNK:
            ref[...] = src[r0:r0 + shp[0], :]
        else:
            if tuple(ref.shape) == (shp[1], shp[0]):
                ref[...] = src[r0:r0 + CHUNK, :].T[0:shp[1], 0:shp[0]]
            else:
                ref[...] = src[r0:r0 + shp[0], 0:shp[1]]


_MEM_G = 2


def _greduce(ga, gb, dmkv, mem2, gm, w_mkv, small_g, loss_p):
    shp_c = (SHARD_O, 2 * MEM_LEN)
    shapes = (shp_c, gb.shape[1:], ga.shape[1:])
    rs = _S_ROWS

    def body(*refs):
        it = iter(refs)
        take = lambda n: [next(it) for _ in range(n)]
        gb_ref, ga_ref, d_ref, m_ref, gm_ref, wm_ref = take(6)
        sg_refs = take(_N_SMALL - 1)
        loss_ref, = take(1)
        oc, ob, oa, ogs = take(4)
        red = take(3 * _N_RED)
        gs_ref, rs_a, rs_b, rs_w, gc_ref, dgm_ref = take(6)
        ssem_a, rsem_a, ssem_b, rsem_b = take(4)

        pos = _position()
        x, y, cc = pos
        myq = 2 * x + y
        here, sib = (x, y, cc), (x, y, 1 - cc)
        chips = _other_chips(x, y)
        reducers = [_ShardReduce(pos, g, red[k * _N_RED:k * _N_RED + 5], red[k * _N_RED + 5:(k + 1) * _N_RED])
                    for k, g in enumerate((gc_ref, gb_ref, ga_ref))]
        for rd in reducers[1:]:
            rd.start()

        xf = m_ref[...]
        nm = xf * _rms(xf)
        hm = (nm * gm_ref[...]).astype(MM)
        d = d_ref[...].astype(MM)
        for o in range(N_DEV):
            gc_ref[o] = _dot_tn(hm[:, o * SHARD_O:(o + 1) * SHARD_O], d)
        dgm_ref[...] = jnp.sum(_dot_nt(d, wm_ref[...]) * nm, axis=0, keepdims=True)
        reducers[0].start()

        gs_ref[...] = jnp.zeros_like(gs_ref)
        _pack_rows(gs_ref, sg_refs[:_MEM_G] + [dgm_ref] + sg_refs[_MEM_G:])
        gs_ref[_LOSS_ROW:_LOSS_ROW + 1, :] = loss_ref[0:1, :]
        small_a = _remote(gs_ref, rs_a, ssem_a, rsem_a, sib)
        small_a.start()

        _remote(gs_ref, rs_a, ssem_a, rsem_a, here).wait_recv()
        rs_b[myq] = gs_ref[0:_W_SP_ROW, :] + rs_a[0:_W_SP_ROW, :]
        rs_w[myq] = (gs_ref[_W_SP_ROW:rs, :] + rs_a[_W_SP_ROW:rs, :]).astype(BF16)
        small_b = []
        for j, chip in enumerate(chips):
            to = (chip[0], chip[1], cc)
            small_b.append(_remote(rs_b.at[myq], rs_b.at[myq], ssem_b.at[0, j], rsem_b.at[0, j], to))
            small_b.append(_remote(rs_w.at[myq], rs_w.at[myq], ssem_b.at[1, j], rsem_b.at[1, j], to))
        for cp in small_b:
            cp.start()
        late_last = reducers[1:] + reducers[:1]
        for rd in late_last:
            rd.mid()
        for rd in late_last:
            rd.pass_on()

        for j in range(3):
            _remote(rs_b.at[myq], rs_b.at[myq], ssem_b.at[0, j], rsem_b.at[0, j], here).wait_recv()
            _remote(rs_w.at[myq], rs_w.at[myq], ssem_b.at[1, j], rsem_b.at[1, j], here).wait_recv()
        ogs[0:_W_SP_ROW, :] = ((rs_b[0] + rs_b[1]) + rs_b[2]) + rs_b[3]

        def tot_w(r):
            w = [rs_w[q, r, :].astype(F32) for q in range(4)]
            ogs[pl.ds(pl.multiple_of(_W_SP_ROW + r.start, 8), _ROWS), :] = ((w[0] + w[1]) + w[2]) + w[3]

        _rows_loop(rs - _W_SP_ROW, tot_w)
        for rd, out in zip(late_last, (ob, oa, oc)):
            rd.finish(out)
        small_a.wait_send()
        for cp in small_b:
            cp.wait_send()

    vm = pl.BlockSpec(memory_space=pltpu.VMEM)
    anyspec = pl.BlockSpec(memory_space=pl.ANY)
    scratch = []
    for shp in shapes:
        scratch += _reduce_scratch(shp)
    scratch += [pltpu.VMEM((rs, CHUNK), F32), pltpu.VMEM((rs, CHUNK), F32),
                pltpu.VMEM((4, _W_SP_ROW, CHUNK), F32), pltpu.VMEM((4, rs - _W_SP_ROW, CHUNK), BF16),
                pltpu.VMEM((N_DEV,) + shp_c, F32), pltpu.VMEM((1, D_MODEL), F32),
                pltpu.SemaphoreType.DMA, pltpu.SemaphoreType.DMA,
                pltpu.SemaphoreType.DMA((2, 3)), pltpu.SemaphoreType.DMA((2, 3))]
    tc, tb, ta, ts = pl.pallas_call(
        body, name="greduce",
        out_shape=tuple([jax.ShapeDtypeStruct(shp, F32) for shp in shapes] + [jax.ShapeDtypeStruct((rs, CHUNK), F32)]),
        in_specs=[anyspec] * 2 + [vm] * (4 + _N_SMALL),
        out_specs=(vm, vm, vm, vm),
        scratch_shapes=scratch,
        compiler_params=_params(),
    )(gb, ga, dmkv, mem2, gm, w_mkv, *small_g, loss_p)
    return ta, tb, tc, ts


def _adamw(w, g, m, v):
    m = ADAM_B1 * m + (1.0 - ADAM_B1) * g
    v = ADAM_B2 * v + (1.0 - ADAM_B2) * (g * g)
    m_hat = m / (1.0 - ADAM_B1 ** ADAM_STEP)
    v_hat = v / (1.0 - ADAM_B2 ** ADAM_STEP)
    delta = -ADAM_LR * (m_hat / (jnp.sqrt(v_hat) + ADAM_EPS) + ADAM_WD * w)
    return delta, m, v


def _update(ta, tb, tc, ts, big_wmv, small_wmv):
    shapes = (ta.shape, tb.shape, tc.shape)
    rs = _S_ROWS
    small_shapes = [tuple(a.shape) for a in small_wmv[0]]

    def body(*refs):
        it = iter(refs)
        take = lambda n: [next(it) for _ in range(n)]
        ga_ref, gb_ref, gc_ref, gs_ref = take(4)
        wa, ma, va, wb, mb, vb_, wc, mc, vc = take(9)
        sw_refs, sm_refs, sv_refs = take(_N_SMALL), take(_N_SMALL), take(_N_SMALL)
        oga, oda, oma, ova, ogb, odb, omb, ovb, ogc, odc, omc, ovc = take(12)
        so_refs = [take(_N_SMALL) for _ in range(4)]
        loss_out, = take(1)
        ws, ms, vs, ods, oms, ovs, turn = take(7)

        def update_rows(nrow, rows, g_r, w_r, m_r, v_r, og, od, om, ov):
            def upd(r):
                g = g_r[r, :]
                d, m, v = _adamw(w_r[r, :], g, m_r[r, :], v_r[r, :])
                og[r, :] = g
                od[r, :] = d
                om[r, :] = m
                ov[r, :] = v

            _rows_loop(nrow, upd, rows)

        update_rows(a_rows, 8, ga_ref, wa, ma, va, oga, oda, oma, ova)

        @pl.when(pl.program_id(0) == 0)
        def _():
            update_rows(shapes[1][0], _ROWS, gb_ref, wb, mb, vb_, ogb, odb, omb, ovb)
            update_rows(shapes[2][0], _ROWS, gc_ref, wc, mc, vc, ogc, odc, omc, ovc)
            for buf in (ws, ms, vs):
                buf[...] = jnp.zeros_like(buf)
            _pack_rows(ws, sw_refs, turn)
            _pack_rows(ms, sm_refs, turn)
            _pack_rows(vs, sv_refs, turn)

            def upd_s(i, _):
                r = pl.ds(pl.multiple_of(i * 8, 8), 8)
                d, m, v = _adamw(ws[r, :], gs_ref[r, :], ms[r, :], vs[r, :])
                ods[r, :] = d
                oms[r, :] = m
                ovs[r, :] = v
                return 0

            lax.fori_loop(0, rs // 8, upd_s, 0)
            for k, buf in enumerate((gs_ref, ods, oms, ovs)):
                _unpack_rows(buf, so_refs[k])
            loss_out[...] = gs_ref[_LOSS_ROW:_LOSS_ROW + 1, 0:1]

    n_blocks = 4
    a_rows = shapes[0][0] // n_blocks
    a_spec = pl.BlockSpec((a_rows, shapes[0][1]), lambda i: (i, 0))
    big_out, big_out_specs = [], []
    for shp in shapes:
        big_out += [jax.ShapeDtypeStruct(shp, F32)] * 4
        big_out_specs += [a_spec if shp == shapes[0] else _full(shp)] * 4
    small_out_shapes = [shp[::-1] if shp == (N_BUCKETS, 4) else shp for shp in small_shapes] * 4
    small_out = [jax.ShapeDtypeStruct(shp, F32) for shp in small_out_shapes]
    out_shape = tuple(big_out + small_out + [jax.ShapeDtypeStruct((1, 1), F32)])
    in_specs = ([a_spec, _full(shapes[1]), _full(shapes[2]), _full((rs, CHUNK))]
                + [a_spec] * 3 + [_full(shapes[1])] * 3 + [_full(shapes[2])] * 3
                + [_full(shp) for shp in small_shapes] * 3)
    return pl.pallas_call(
        body, name="update", grid=(n_blocks,),
        out_shape=out_shape,
        in_specs=in_specs,
        out_specs=tuple(big_out_specs + [_full(shp) for shp in small_out_shapes] + [_full((1, 1))]),
        scratch_shapes=[pltpu.VMEM((rs, CHUNK), F32) for _ in range(6)] + [pltpu.VMEM((CHUNK, CHUNK), F32)],
        compiler_params=_params(dimension_semantics=("arbitrary",)),
    )(ta, tb, tc, ts, *big_wmv, *small_wmv[0], *small_wmv[1], *small_wmv[2])


def kernel(x, mem, pre_norm_g, post_norm_g, mem_norm_g, w_in, w_mem_kv, v_norm_g, v_norm_b, w_spatial, b_spatial, attn_sinks, rel_bias, w_out, loss_target, m_pre_norm_g, m_post_norm_g, m_mem_norm_g, m_w_in, m_w_mem_kv, m_v_norm_g, m_v_norm_b, m_w_spatial, m_b_spatial, m_attn_sinks, m_rel_bias, m_w_out, v_pre_norm_g, v_post_norm_g, v_mem_norm_g, v_w_in, v_w_mem_kv, v_v_norm_g, v_v_norm_b, v_w_spatial, v_b_spatial, v_attn_sinks, v_rel_bias, v_w_out):
    sh_a = (w_in[0].T, m_w_in[0].T, v_w_in[0].T)
    sh_b = (w_out[0], m_w_out[0], v_w_out[0])
    sh_c = (w_mem_kv[0], m_w_mem_kv[0], v_w_mem_kv[0])
    nb, s, _ = x.shape
    t = nb * s
    x2 = x.reshape(t, D_MODEL)
    tgt2 = loss_target.reshape(t, D_MODEL)
    mem2 = mem.reshape(nb * MEM_LEN, D_MODEL)
    buckets = jnp.asarray(_t5_buckets())

    wa, wb, wc, bias, wt, wtt, bcol, mkv = _wgather(sh_a[0], sh_b[0], sh_c[0], rel_bias.T, w_spatial[0], b_spatial[0],
                                                    buckets, mem2, mem_norm_g)
    w_mkv = wc.reshape(D_MODEL, 2 * MEM_LEN)
    gx, dmkv, dwi, dwo, dg1, dg2, loss_p, dwsp, dbs, dvg, dvb, dsink, drel = _layer(
        x2, tgt2, mkv.reshape(nb, MEM_LEN, 2 * MEM_LEN), bias, attn_sinks.reshape(4), v_norm_g, v_norm_b, wt, wtt, bcol,
        pre_norm_g, post_norm_g, wa.reshape(IN_WIDTH, D_MODEL), wb.reshape(D_MODEL, D_MODEL), buckets,
        nb, s, min(256, s))
    gx = gx.reshape(nb, s, D_MODEL)
    small_grads = [dg1, dg2, dvg, dvb, dbs, dsink, drel, dwsp.reshape(A_GROUPS * CHUNK, CHUNK)]

    small_names = ["pre_norm_g", "post_norm_g", "mem_norm_g", "v_norm_g", "v_norm_b", "b_spatial", "attn_sinks",
                   "rel_bias", "w_spatial"]
    given = dict(pre_norm_g=(pre_norm_g, m_pre_norm_g, v_pre_norm_g), post_norm_g=(post_norm_g, m_post_norm_g, v_post_norm_g),
                 mem_norm_g=(mem_norm_g, m_mem_norm_g, v_mem_norm_g), v_norm_g=(v_norm_g, m_v_norm_g, v_v_norm_g),
                 v_norm_b=(v_norm_b, m_v_norm_b, v_v_norm_b), b_spatial=(b_spatial, m_b_spatial, v_b_spatial),
                 attn_sinks=(attn_sinks, m_attn_sinks, v_attn_sinks), rel_bias=(rel_bias, m_rel_bias, v_rel_bias),
                 w_spatial=(w_spatial, m_w_spatial, v_w_spatial))
    small_wmv = [[given[n][k].T if n == "rel_bias" else given[n][k].reshape(shp)
                  for n, (shp, _) in zip(small_names, _S_LAYOUT)] for k in range(3)]

    ta, tb, tc, ts = _greduce(dwi.reshape(N_DEV, SHARD_IN, D_MODEL), dwo.reshape(N_DEV, SHARD_O, D_MODEL),
                              dmkv.reshape(nb * MEM_LEN, 2 * MEM_LEN), mem2, mem_norm_g, w_mkv, small_grads, loss_p)
    outs = _update(ta, tb, tc, ts, (*sh_a, *sh_b, *sh_c), small_wmv)
    ra, rb, rc = outs[0:4], outs[4:8], outs[8:12]
    loss = outs[12 + 4 * _N_SMALL].reshape(())

    res = {}
    for k, kind in enumerate(("grad", "delta", "new_m", "new_v")):
        res[kind, "w_in"] = ra[k].T[None]
        res[kind, "w_out"] = rb[k][None]
        res[kind, "w_mem_kv"] = rc[k][None]
        for i, n in enumerate(small_names):
            o = outs[12 + k * _N_SMALL + i]
            res[kind, n] = o.T if n == "rel_bias" else o.reshape(given[n][0].shape)
    order = ["pre_norm_g", "post_norm_g", "mem_norm_g", "w_in", "w_mem_kv", "v_norm_g", "v_norm_b", "w_spatial",
             "b_spatial", "attn_sinks", "rel_bias", "w_out"]
    flat = [res[kind, n] for kind in ("grad", "delta", "new_m", "new_v") for n in order]
    return (loss, gx, *flat)
```

```python
import numpy as np
import jax
import jax.numpy as jnp
from jax import lax
from jax.experimental import pallas as pl
from jax.experimental.pallas import tpu as pltpu

F32 = jnp.float32
BF16 = jnp.bfloat16
MM = jnp.bfloat16

D_MODEL = 1024
CHUNK = 128
A_GROUPS = 4
A_WIDTH = 512
UV_W = 1024
QKV_W = 768
Z_W = 1024
IN_WIDTH = UV_W + QKV_W + Z_W
MEM_LEN = 256
N_BUCKETS = 32
MAX_DISTANCE = 128
EPS = 1e-6
NEG = -1e30
SCALE = 0.125
N_DEV = 8
SHARD_IN = IN_WIDTH // N_DEV
SHARD_O = D_MODEL // N_DEV

SQ_COL, SK_COL, SV_COL, MQ_COL, Z_COL = UV_W, UV_W + 256, UV_W + 384, UV_W + 512, UV_W + QKV_W
DW_PIECES = ((0, SQ_COL), (SQ_COL, Z_COL), (Z_COL, IN_WIDTH))
YB_OFF, YC_OFF = 512, 768

ADAM_LR = 0.001
ADAM_B1 = 0.9
ADAM_B2 = 0.999
ADAM_EPS = 1e-08
ADAM_WD = 0.01
ADAM_STEP = 10

VMEM_LIMIT = 60 * 1024 * 1024

_GELU_C = 0.7978845608028654
_GELU_A = 0.044715

MESH = pl.DeviceIdType.MESH
_ROWS = 32


def _dot(a, b):
    return lax.dot_general(a, b, (((1,), (0,)), ((), ())), preferred_element_type=F32)


def _dot_nt(a, b):
    return lax.dot_general(a, b, (((1,), (1,)), ((), ())), preferred_element_type=F32)


def _dot_tn(a, b):
    return lax.dot_general(a, b, (((0,), (0,)), ((), ())), preferred_element_type=F32)


def _gelu_and_grad(x):
    x2 = x * x
    t = jnp.tanh(x * (_GELU_C + (_GELU_C * _GELU_A) * x2))
    w = 0.5 * t + 0.5
    g = x * w
    dg = w * (1.0 + (x - g) * ((2.0 * _GELU_C) + (6.0 * _GELU_C * _GELU_A) * x2))
    return g, dg


def _t5_buckets():
    qi = np.arange(CHUNK)[:, None]
    kj = np.arange(2 * CHUNK)[None, :]
    n = np.maximum(qi + CHUNK - kj, 0)
    max_exact = N_BUCKETS // 2
    large = max_exact + (np.log(np.maximum(n, 1) / max_exact) / np.log(MAX_DISTANCE / max_exact)
                         * (N_BUCKETS - max_exact)).astype(np.int32)
    large = np.minimum(large, N_BUCKETS - 1)
    return np.where(n < max_exact, n, large).astype(np.int32)


def _params(**kw):
    return pltpu.CompilerParams(vmem_limit_bytes=VMEM_LIMIT, **kw)


def _full(shape, single=False):
    nd = len(shape)
    if single:
        return pl.BlockSpec(shape, lambda *_: (0,) * nd, pipeline_mode=pl.Buffered(1))
    return pl.BlockSpec(shape, lambda *_: (0,) * nd)


def _window_valid():
    qi = lax.broadcasted_iota(jnp.int32, (CHUNK, 2 * CHUNK), 0)
    kj = lax.broadcasted_iota(jnp.int32, (CHUNK, 2 * CHUNK), 1)
    dist = qi + CHUNK - kj
    return (dist >= 0) & (dist < CHUNK)


def _position():
    return lax.axis_index("x"), lax.axis_index("y"), lax.axis_index("c")


def _other_chips(x, y):
    return [(1 - x, y), (x, 1 - y), (1 - x, 1 - y)]


def _route(x, y, c):
    first = (x * c + (1 - x) * (1 - c), y * (1 - c) + (1 - y) * c)
    second = (x * (1 - c) + (1 - x) * c, y * c + (1 - y) * (1 - c))
    return first, second, (1 - x, 1 - y)


def _remote(src, dst, ssem, rsem, to):
    return pltpu.make_async_remote_copy(src_ref=src, dst_ref=dst, send_sem=ssem, recv_sem=rsem,
                                        device_id=to, device_id_type=MESH)


def _rows_loop(nrow, fn, rows=_ROWS):
    assert nrow % rows == 0

    def step(i, _):
        fn(pl.ds(pl.multiple_of(i * rows, rows), rows))
        return 0

    lax.fori_loop(0, nrow // rows, step, 0)


class _Gather:
    def __init__(self, pos, out, ssem, rsem):
        self.x, self.y, self.c = pos
        self.out, self.ssem, self.rsem = out, ssem, rsem
        self.me = 4 * self.x + 2 * self.y + self.c
        self.here = (self.x, self.y, self.c)
        self.sib = (self.x, self.y, 1 - self.c)
        self.first, self.second, self.far = _route(*pos)

    def _copy(self, k, blk, to):
        r = self.out.at[blk]
        return _remote(r, r, self.ssem.at[k], self.rsem.at[k], to)

    def _idx(self, chip, core):
        return 4 * chip[0] + 2 * chip[1] + core

    def _on(self, chip):
        return (chip[0], chip[1], self.c)

    def start(self):
        self._copy(0, self.me, self.sib).start()
        self._copy(1, self.me, self._on(self.first)).start()
        self._copy(2, self.me, self._on(self.second)).start()

    def forward(self):
        c = self.c
        self._copy(1, self._idx(self.first, c), self.here).wait_recv()
        self._copy(3, self._idx(self.first, c), self._on(self.second)).start()
        self._copy(4, self._idx(self.first, c), self.sib).start()
        self._copy(2, self._idx(self.second, c), self.here).wait_recv()
        self._copy(5, self._idx(self.second, c), self.sib).start()
        self._copy(3, self._idx(self.far, c), self.here).wait_recv()
        self._copy(6, self._idx(self.far, c), self.sib).start()

    def finish(self):
        c = self.c
        self._copy(0, self._idx((self.x, self.y), 1 - c), self.here).wait_recv()
        for k, chip in ((4, self.second), (5, self.first), (6, self.far)):
            self._copy(k, self._idx(chip, 1 - c), self.here).wait_recv()
        self._copy(0, self.me, self.sib).wait_send()
        self._copy(1, self.me, self._on(self.first)).wait_send()
        self._copy(2, self.me, self._on(self.second)).wait_send()
        self._copy(3, self._idx(self.first, c), self._on(self.second)).wait_send()
        for k, chip in ((4, self.first), (5, self.second), (6, self.far)):
            self._copy(k, self._idx(chip, c), self.sib).wait_send()


def _prep_tables(rb_ref, w_ref, b_ref, bk_ref, bias_ref, wt_ref, wtt_ref, bcol_ref):
    valid = _window_valid()
    bk = bk_ref[...]
    acc = [jnp.full((CHUNK, 2 * CHUNK), NEG, F32) for _ in range(4)]
    for b in range(N_BUCKETS):
        hit = (bk == b) & valid
        for h in range(4):
            acc[h] = jnp.where(hit, rb_ref[h, b], acc[h])
    for h in range(4):
        bias_ref[h] = acc[h]
    r = lax.broadcasted_iota(jnp.int32, (CHUNK, CHUNK), 0)
    c = lax.broadcasted_iota(jnp.int32, (CHUNK, CHUNK), 1)
    for g in range(A_GROUPS):
        w = jnp.where(r >= c, w_ref[g], 0.0)
        wt_ref[g] = w.astype(MM)
        wtt_ref[g] = w.T.astype(MM)
        bcol_ref[g] = jnp.broadcast_to(b_ref[g:g + 1, :], (CHUNK, CHUNK)).T


def _wgather(a, b, c, rel_bias, w_sp, b_sp, buckets, mem2, gm):
    tmem = mem2.shape[0]

    def body(a_ref, b_ref, c_ref, rb_ref, w_ref, bsp_ref, bk_ref, m_ref, gm_ref,
             oa, ob, oc, bias_ref, wt_ref, wtt_ref, bcol_ref, mkv_ref, ssem, rsem):
        pos = _position()
        me = 4 * pos[0] + 2 * pos[1] + pos[2]
        gathers = []
        for k, (src, out) in enumerate(((c_ref, oc), (b_ref, ob), (a_ref, oa))):
            out[me] = src[...].astype(BF16)
            g = _Gather(pos, out, ssem.at[k], rsem.at[k])
            g.start()
            gathers.append(g)
        _prep_tables(rb_ref, w_ref, bsp_ref, bk_ref, bias_ref, wt_ref, wtt_ref, bcol_ref)
        for g in gathers:
            g.forward()
        gathers[0].finish()
        xf = m_ref[...]
        hm = (xf * _rms(xf) * gm_ref[...]).astype(MM)
        acc = jnp.zeros((tmem, 2 * MEM_LEN), F32)
        for d in range(N_DEV):
            acc = acc + _dot(hm[:, d * SHARD_O:(d + 1) * SHARD_O], oc[d])
        mkv_ref[...] = acc.astype(MM)
        for g in gathers[1:]:
            g.finish()

    vm = pl.BlockSpec(memory_space=pltpu.VMEM)
    grp = (A_GROUPS, CHUNK, CHUNK)
    return pl.pallas_call(
        body, name="wgather",
        out_shape=(jax.ShapeDtypeStruct((N_DEV,) + a.shape, BF16),
                   jax.ShapeDtypeStruct((N_DEV,) + b.shape, BF16),
                   jax.ShapeDtypeStruct((N_DEV,) + c.shape, BF16),
                   jax.ShapeDtypeStruct((4, CHUNK, 2 * CHUNK), F32),
                   jax.ShapeDtypeStruct(grp, MM), jax.ShapeDtypeStruct(grp, MM), jax.ShapeDtypeStruct(grp, F32),
                   jax.ShapeDtypeStruct((tmem, 2 * MEM_LEN), MM)),
        in_specs=[vm, vm, vm, pl.BlockSpec(memory_space=pltpu.SMEM), vm, vm, vm, vm, vm],
        out_specs=tuple([vm] * 8),
        scratch_shapes=[pltpu.SemaphoreType.DMA((3, 7)), pltpu.SemaphoreType.DMA((3, 7))],
        compiler_params=_params(),
    )(a, b, c, rel_bias, w_sp, b_sp, buckets, mem2, gm)


def _half_masks(rows):
    lane = lax.broadcasted_iota(jnp.int32, (rows, CHUNK), 1)
    return lane < 64


def _dup_heads(band):
    b32 = band.astype(F32)
    rolled = pltpu.roll(b32, 64, 1)
    lo = _half_masks(band.shape[0])
    return (jnp.where(lo, b32, rolled).astype(MM), jnp.where(lo, rolled, b32).astype(MM))


def _swa_probs(qk, bias_h, sink_h, first_add):
    s = qk * SCALE + bias_h + first_add
    m = jnp.maximum(jnp.max(s, axis=-1, keepdims=True), sink_h)
    p = jnp.exp(s - m)
    es = jnp.exp(sink_h - m)
    inv = 1.0 / (jnp.sum(p, axis=-1, keepdims=True) + es)
    return p * inv, es * inv


def _softmax(s):
    m = jnp.max(s, axis=-1, keepdims=True)
    p = jnp.exp(s - m)
    return p * (1.0 / jnp.sum(p, axis=-1, keepdims=True))


def _first_block_mask(n):
    col = lax.broadcasted_iota(jnp.int32, (2 * CHUNK, 2 * CHUNK), 1)
    return jnp.where((col < CHUNK) & (n == 0), NEG, 0.0)


def _stack_heads(x128, lo):
    return jnp.concatenate([jnp.where(lo, x128, 0.0), jnp.where(lo, 0.0, x128)], axis=0).astype(MM)


def _rms(xf):
    return lax.rsqrt(jnp.mean(xf * xf, axis=-1, keepdims=True) + EPS)


def _layer(x2, tgt2, mkv3, bias, sinks, vg, vb, wt, wtt, bcol, g1, g2, w_in_t, w_o, buckets, nb, s, tm):
    nt = s // tm
    bpt = tm // CHUNK
    bps = s // CHUNK
    t = nb * s
    last_step = nb * nt - 1

    def tile_at(step):
        return (step // nt) * nt + nt - 1 - step % nt

    def block_before(step):
        return (step // nt) * bps + jnp.maximum((nt - 1 - step % nt) * bpt - 1, 0)

    def body(x_ref, xp_ref, xn_ref, xpn_ref, t_ref, mkv_ref, bias_ref, sink_ref, vg_ref, vb_ref,
             wt_ref, wtt_ref, bcol_ref, g1_ref, g2_ref, wi_ref, wo_ref, bk_ref,
             gx_ref, dmkv_ref, dwi_hbm, dwo_hbm, dg1_ref, dg2_ref, loss_ref, dwsp_ref, dbs_ref,
             dvg_ref, dvb_ref, dsink_ref, drel_ref,
             acc_i, acc_o, uv_s, z_s, q_s, kv_s, h_s, hp_s, dp_s, dxo_s, dh_s, r_s,
             ycat, dyc, u_s, gu_s, gv_s, xh_s, vc_s, pb_s, ps_s, pc_s, kd_s, vd_s,
             dkv_acc, dbias_acc, dsv_acc, dsink_acc, sems):
        b, j = pl.program_id(0), pl.program_id(1)
        jt = nt - 1 - j
        step = b * nt + j
        g1v = g1_ref[...]
        NOW, NEXT, DONE = 0, 1, 2
        dw_cols = list(DW_PIECES)

        def weight_grad(n, slot):
            for c0, c1 in dw_cols[:n]:
                acc_i[c0:c1, :] += _dot_tn(dp_s[:, c0:c1], h_s[slot])
            del dw_cols[:n]

        def pre_norm(x_tile, x_before):
            xf = x_tile[...]
            r_s[NEXT] = _rms(xf)
            h_s[NEXT] = (xf * r_s[NEXT] * g1v).astype(MM)
            xp = x_before[...]
            hp_s[...] = (xp * _rms(xp) * g1v).astype(MM)

        def project_z():
            z_s[...] = _dot_nt(h_s[NEXT], wi_ref[Z_COL:IN_WIDTH, :])

        def project_uv():
            uv_s[...] = _dot_nt(h_s[NEXT], wi_ref[0:UV_W, :])

        @pl.when(step == 0)
        def _():
            for ref in (acc_i, acc_o, dg1_ref, dg2_ref, loss_ref, dwsp_ref, dvg_ref, dvb_ref,
                        dbias_acc, dsv_acc, dsink_acc):
                ref[...] = jnp.zeros_like(ref)
            dp_s[...] = jnp.zeros_like(dp_s)
            h_s[NOW] = jnp.zeros((tm, D_MODEL), MM)
            pre_norm(x_ref, xp_ref)
            project_z()
            project_uv()

        h_s[DONE] = h_s[NOW]
        r_s[NOW] = r_s[NEXT]
        h = h_s[NEXT]
        h_s[NOW] = h
        hp = hp_s[...]

        @pl.when(j == 0)
        def _():
            dmkv_ref[...] = jnp.zeros_like(dmkv_ref)
            dkv_acc[...] = jnp.zeros_like(dkv_acc)

        carry = dkv_acc[0:CHUNK, :]
        dkv_acc[...] = jnp.zeros_like(dkv_acc)
        dkv_acc[tm:tm + CHUNK, :] = carry

        lo = _half_masks(CHUNK)
        lob = _half_masks(2 * CHUNK)
        lot = _half_masks(tm)

        qkv = _dot_nt(h, wi_ref[SQ_COL:Z_COL, :])
        q_s[:, 0:256] = qkv[:, 0:256].astype(MM)
        q_s[:, 256:512] = qkv[:, 512:768].astype(MM)
        kv_s[CHUNK:CHUNK + tm, :] = qkv[:, 256:512].astype(MM)
        kv_s[0:CHUNK, :] = _dot_nt(hp, wi_ref[SK_COL:MQ_COL, :]).astype(MM)

        weight_grad(1, DONE)
        b_qk, b_pb = [], []
        for blk in range(bpt):
            r0 = blk * CHUNK
            rows = slice(r0, r0 + CHUNK)
            for g in range(A_GROUPS):
                cg = slice(g * CHUNK, (g + 1) * CHUNK)
                u, gu = _gelu_and_grad(uv_s[rows, cg])
                v, gv = _gelu_and_grad(uv_s[rows, A_WIDTH + g * CHUNK:A_WIDTH + (g + 1) * CHUNK])
                mu = jnp.mean(v, axis=-1, keepdims=True)
                xc = v - mu
                rstd = lax.rsqrt(jnp.mean(xc * xc, axis=-1, keepdims=True) + EPS)
                xhat = xc * rstd
                vc = (xhat * vg_ref[:, cg] + vb_ref[:, cg]).astype(MM)
                sv = _dot(wt_ref[g], vc) + bcol_ref[g]
                u_s[rows, cg] = u
                gu_s[rows, cg] = sv * gu
                gv_s[rows, cg] = rstd * gv
                xh_s[rows, cg] = xhat
                vc_s[rows, cg] = vc
                ycat[rows, cg] = u * sv
            weight_grad(1, DONE)
            kd = _dup_heads(kv_s[r0:r0 + 2 * CHUNK, 0:CHUNK])
            vd = _dup_heads(kv_s[r0:r0 + 2 * CHUNK, CHUNK:2 * CHUNK])
            for kvh in range(2):
                kd_s[blk * 2 + kvh] = kd[kvh]
                vd_s[blk * 2 + kvh] = vd[kvh]
                q2 = _stack_heads(q_s[rows, kvh * CHUNK:(kvh + 1) * CHUNK].astype(F32), lo)
                b_qk.append(_dot_nt(q2, kd[kvh]))
        qks, pcs = [], []
        for g in range(2):
            q128 = q_s[:, 256 + g * CHUNK:256 + (g + 1) * CHUNK].astype(F32)
            for hh in range(2):
                qsel = jnp.where(lot if hh == 0 else ~lot, q128, 0.0).astype(MM)
                qks.append(_dot_nt(qsel, mkv_ref[:, g * CHUNK:(g + 1) * CHUNK]))
        top =lax.broadcasted_iota(jnp.int32, (2 * CHUNK, 1), 0) < CHUNK
        for blk in range(bpt):
            first_add = _first_block_mask(jt * bpt + blk)
            for kvh in range(2):
                sink2 = jnp.where(top, sink_ref[2 * kvh], sink_ref[2 * kvh + 1])
                probs, ps = _swa_probs(b_qk[blk * 2 + kvh], bias_ref[kvh], sink2, first_add)
                pb_s[blk * 2 + kvh] = probs
                ps_s[blk * 2 + kvh] = jnp.broadcast_to(ps, (2 * CHUNK, CHUNK))
                b_pb.append(probs.astype(MM))
        weight_grad(len(dw_cols), DONE)
        for hd in range(4):
            probs = _softmax(qks[hd] * SCALE)
            pc_s[hd] = probs
            pcs.append(probs.astype(MM))
        for blk in range(bpt):
            rows = slice(blk * CHUNK, (blk + 1) * CHUNK)
            for kvh in range(2):
                out2 = _dot(b_pb[blk * 2 + kvh], vd_s[blk * 2 + kvh])
                ycat[rows, YB_OFF + kvh * CHUNK:YB_OFF + (kvh + 1) * CHUNK] = jnp.where(
                    lo, out2[0:CHUNK], out2[CHUNK:2 * CHUNK])
        outs = [_dot(pcs[hd], mkv_ref[:, MEM_LEN + (hd // 2) * CHUNK:MEM_LEN + (hd // 2 + 1) * CHUNK])
                for hd in range(4)]
        for g in range(2):
            ycat[:, YC_OFF + g * CHUNK:YC_OFF + (g + 1) * CHUNK] = jnp.where(lot, outs[2 * g], outs[2 * g + 1])

        zt = z_s[...]
        sig = 1.0 / (1.0 + jnp.exp(-zt))
        silu = zt * sig
        yc = ycat[...]
        yb = (yc * silu).astype(MM)
        pre_norm(xn_ref, xpn_ref)
        o = _dot(yb, wo_ref[...])
        project_z()
        r2 = _rms(o)
        nrm = o * r2
        g2v = g2_ref[...]
        e = x_ref[...] + nrm * g2v - t_ref[...]
        l1 = jnp.sum(e * e, axis=-1, keepdims=True)
        loss_ref[...] += jnp.broadcast_to(jnp.sum(l1, axis=0, keepdims=True) * (0.5 / D_MODEL), loss_ref.shape)
        dxo = e * (1.0 / D_MODEL)
        dxo_s[...] = dxo
        dg2_ref[...] += jnp.sum(dxo * nrm, axis=0, keepdims=True)
        dn = dxo * g2v
        do = r2 * (dn - nrm * jnp.mean(dn * nrm, axis=-1, keepdims=True))
        dob = do.astype(MM)
        dy = _dot_nt(dob, wo_ref[...])
        dp_s[:, Z_COL:IN_WIDTH] = (dy * yc * (sig * (1.0 + zt * (1.0 - sig)))).astype(MM)
        dyc[...] = dy * silu
        acc_o[...] += _dot_tn(yb, dob)

        def in_proj_bwd(c0, c1):
            part = _dot(dp_s[:, c0:c1], wi_ref[c0:c1, :])
            if c0 == Z_COL:
                dh_s[...] = part
            else:
                dh_s[...] += part

        in_proj_bwd(Z_COL, IN_WIDTH)

        for blk in range(bpt):
            r0 = blk * CHUNK
            rows = slice(r0, r0 + CHUNK)
            for g in range(A_GROUPS):
                cg = slice(g * CHUNK, (g + 1) * CHUNK)
                cv = slice(A_WIDTH + g * CHUNK, A_WIDTH + (g + 1) * CHUNK)
                dya = dyc[rows, cg]
                dp_s[rows, cg] = (dya * gu_s[rows, cg]).astype(MM)
                dsv = dya * u_s[rows, cg]
                dsvb = dsv.astype(MM)
                dsv_acc[g] += dsv
                dwsp_ref[g] += _dot_nt(dsvb, vc_s[rows, cg])
                dvc = _dot(wtt_ref[g], dsvb)
                xhat = xh_s[rows, cg]
                dvg_ref[:, cg] += jnp.sum(dvc * xhat, axis=0, keepdims=True)
                dvb_ref[:, cg] += jnp.sum(dvc, axis=0, keepdims=True)
                dxh = dvc * vg_ref[:, cg]
                dv = (dxh - jnp.mean(dxh, axis=-1, keepdims=True)
                      - xhat * jnp.mean(dxh * xhat, axis=-1, keepdims=True))
                dp_s[rows, cv] = (dv * gv_s[rows, cg]).astype(MM)
        b_dosel, b_dp, b_dss = [], [], []
        for blk in range(bpt):
            rows = slice(blk * CHUNK, (blk + 1) * CHUNK)
            for kvh in range(2):
                b_dosel.append(_stack_heads(dyc[rows, YB_OFF + kvh * CHUNK:YB_OFF + (kvh + 1) * CHUNK], lo))
                b_dp.append(_dot_nt(b_dosel[-1], vd_s[blk * 2 + kvh]))
        dosels, dps, dsss = [], [], []
        for hd in range(4):
            do128 = dyc[:, YC_OFF + (hd // 2) * CHUNK:YC_OFF + (hd // 2 + 1) * CHUNK]
            dosels.append(jnp.where(lot if hd % 2 == 0 else ~lot, do128, 0.0).astype(MM))
            dps.append(_dot_nt(dosels[hd], mkv_ref[:, MEM_LEN + (hd // 2) * CHUNK:MEM_LEN + (hd // 2 + 1) * CHUNK]))
        in_proj_bwd(0, UV_W)
        for blk in range(bpt):
            for kvh in range(2):
                probs = pb_s[blk * 2 + kvh]
                dp = b_dp[blk * 2 + kvh]
                delta = jnp.sum(probs * dp, axis=-1, keepdims=True)
                ds = probs * (dp - delta)
                dbias_acc[kvh] += ds
                sd = ps_s[blk * 2 + kvh][:, 0:1] * delta
                for gi in range(2):
                    hd = 2 * kvh + gi
                    dsink_acc[hd:hd + 1, :] += jnp.broadcast_to(
                        -jnp.sum(sd[gi * CHUNK:(gi + 1) * CHUNK], axis=0, keepdims=True), (1, CHUNK))
                b_dss.append((ds * SCALE).astype(MM))
        for hd in range(4):
            probs = pc_s[hd]
            ds = probs * (dps[hd] - jnp.sum(probs * dps[hd], axis=-1, keepdims=True))
            dsss.append((ds * SCALE).astype(MM))
        for blk in range(bpt):
            r0 = blk * CHUNK
            rows = slice(r0, r0 + CHUNK)
            dk_f, dv_f = [], []
            for kvh in range(2):
                dss = b_dss[blk * 2 + kvh]
                q2 = _stack_heads(q_s[rows, kvh * CHUNK:(kvh + 1) * CHUNK].astype(F32), lo)
                dq2 = _dot(dss, kd_s[blk * 2 + kvh])
                dkd = _dot_tn(dss, q2)
                dvd = _dot_tn(pb_s[blk * 2 + kvh].astype(MM), b_dosel[blk * 2 + kvh])
                dp_s[rows, SQ_COL + kvh * CHUNK:SQ_COL + (kvh + 1) * CHUNK] = jnp.where(
                    lo, dq2[0:CHUNK], dq2[CHUNK:2 * CHUNK]).astype(MM)
                dk_f.append(dkd + pltpu.roll(dkd, 64, 1))
                dv_f.append(dvd + pltpu.roll(dvd, 64, 1))
            dkv_acc[r0:r0 + 2 * CHUNK, 0:CHUNK] += jnp.where(lob, dk_f[0], dk_f[1])
            dkv_acc[r0:r0 + 2 * CHUNK, CHUNK:2 * CHUNK] += jnp.where(lob, dv_f[0], dv_f[1])
        dp_s[:, SK_COL:MQ_COL] = dkv_acc[CHUNK:CHUNK + tm, :].astype(MM)
        for g in range(2):
            q128 = q_s[:, 256 + g * CHUNK:256 + (g + 1) * CHUNK].astype(F32)
            k128 = mkv_ref[:, g * CHUNK:(g + 1) * CHUNK]
            dq128 = jnp.zeros((tm, CHUNK), F32)
            dk128 = jnp.zeros((MEM_LEN, CHUNK), F32)
            dv128 = jnp.zeros((MEM_LEN, CHUNK), F32)
            for hh in range(2):
                hd = 2 * g + hh
                half = lot if hh == 0 else ~lot
                qsel = jnp.where(half, q128, 0.0).astype(MM)
                dq128 = dq128 + jnp.where(half, _dot(dsss[hd], k128), 0.0)
                dk128 = dk128 + _dot_tn(dsss[hd], qsel)
                dv128 = dv128 + _dot_tn(pc_s[hd].astype(MM), dosels[hd])
            dp_s[:, MQ_COL + g * CHUNK:MQ_COL + (g + 1) * CHUNK] = dq128.astype(MM)
            dmkv_ref[:, g * CHUNK:(g + 1) * CHUNK] += dk128
            dmkv_ref[:, MEM_LEN + g * CHUNK:MEM_LEN + (g + 1) * CHUNK] += dv128

        in_proj_bwd(SQ_COL, Z_COL)
        project_uv()
        dh = dh_s[...]
        r = r_s[NOW]
        nx = x_ref[...] * r
        dg1_ref[...] += jnp.sum(dh * nx, axis=0, keepdims=True)
        dnx = dh * g1v
        gx_ref[...] = dxo_s[...] + r * (dnx - nx * jnp.mean(dnx * nx, axis=-1, keepdims=True))

        @pl.when(step == last_step)
        def _():
            dw_cols.extend(DW_PIECES)
            weight_grad(len(dw_cols), NOW)
            out_i = pltpu.make_async_copy(acc_i, dwi_hbm, sems.at[0])
            out_o = pltpu.make_async_copy(acc_o, dwo_hbm, sems.at[1])
            out_i.start()
            out_o.start()
            r_ = lax.broadcasted_iota(jnp.int32, (CHUNK, CHUNK), 0)
            c_ = lax.broadcasted_iota(jnp.int32, (CHUNK, CHUNK), 1)
            for g in range(A_GROUPS):
                dwsp_ref[g] = jnp.where(r_ >= c_, dwsp_ref[g], 0.0)
                dbs_ref[g:g + 1, :] = jnp.sum(dsv_acc[g].T, axis=0, keepdims=True)
            rows8 = lax.broadcasted_iota(jnp.int32, (8, CHUNK), 0)
            cols8 = lax.broadcasted_iota(jnp.int32, (8, CHUNK), 1)
            sk = jnp.zeros((8, CHUNK), F32)
            for hd in range(4):
                sk = sk + jnp.where((rows8 == 0) & (cols8 == hd),
                                    jnp.broadcast_to(dsink_acc[hd:hd + 1, :], (8, CHUNK)), 0.0)
            dsink_ref[...] = sk
            bk = bk_ref[...]
            valid = _window_valid()
            rrow = lax.broadcasted_iota(jnp.int32, (N_BUCKETS, CHUNK), 0)
            rcol = lax.broadcasted_iota(jnp.int32, (N_BUCKETS, CHUNK), 1)
            acc = jnp.zeros((N_BUCKETS, CHUNK), F32)
            for bb in range(N_BUCKETS):
                hit = (bk == bb) & valid
                for hd in range(4):
                    dbias = dbias_acc[hd // 2, (hd % 2) * CHUNK:(hd % 2 + 1) * CHUNK, :]
                    part = jnp.sum(jnp.where(hit, dbias, 0.0), axis=-1, keepdims=True)
                    tot = jnp.sum(part, axis=0, keepdims=True)
                    acc = acc + jnp.where((rrow == bb) & (rcol == hd), jnp.broadcast_to(tot, (N_BUCKETS, CHUNK)), 0.0)
            drel_ref[...] = acc
            out_i.wait()
            out_o.wait()

    after = lambda b, j: jnp.minimum(b * nt + j + 1, last_step)
    tile = pl.BlockSpec((tm, D_MODEL), lambda b, j: (tile_at(b * nt + j), 0))
    tile_after = pl.BlockSpec((tm, D_MODEL), lambda b, j: (tile_at(after(b, j)), 0))
    halo = pl.BlockSpec((CHUNK, D_MODEL), lambda b, j: (block_before(b * nt + j), 0))
    halo_after = pl.BlockSpec((CHUNK, D_MODEL), lambda b, j: (block_before(after(b, j)), 0))
    per_batch = lambda r, w: pl.BlockSpec((None, r, w), lambda b, j: (b, 0, 0))
    anyspec = pl.BlockSpec(memory_space=pl.ANY)
    grp = (A_GROUPS, CHUNK, CHUNK)
    return pl.pallas_call(
        body, name="layer", grid=(nb, nt),
        out_shape=(jax.ShapeDtypeStruct((t, D_MODEL), F32),
                   jax.ShapeDtypeStruct((nb, MEM_LEN, 2 * MEM_LEN), F32),
                   jax.ShapeDtypeStruct((IN_WIDTH, D_MODEL), F32),
                   jax.ShapeDtypeStruct((D_MODEL, D_MODEL), F32),
                   jax.ShapeDtypeStruct((1, D_MODEL), F32),
                   jax.ShapeDtypeStruct((1, D_MODEL), F32),
                   jax.ShapeDtypeStruct((8, CHUNK), F32),
                   jax.ShapeDtypeStruct(grp, F32),
                   jax.ShapeDtypeStruct((A_GROUPS, CHUNK), F32),
                   jax.ShapeDtypeStruct((1, A_WIDTH), F32),
                   jax.ShapeDtypeStruct((1, A_WIDTH), F32),
                   jax.ShapeDtypeStruct((8, CHUNK), F32),
                   jax.ShapeDtypeStruct((N_BUCKETS, CHUNK), F32)),
        in_specs=[tile, halo, tile_after, halo_after, tile, per_batch(MEM_LEN, 2 * MEM_LEN),
                  _full((2, 2 * CHUNK, 2 * CHUNK)),
                  pl.BlockSpec(memory_space=pltpu.SMEM),
                  _full((1, A_WIDTH)), _full((1, A_WIDTH)),
                  _full(grp), _full(grp), _full(grp),
                  _full((1, D_MODEL)), _full((1, D_MODEL)),
                  _full((IN_WIDTH, D_MODEL), single=True), _full((D_MODEL, D_MODEL), single=True),
                  _full((CHUNK, 2 * CHUNK))],
        out_specs=(tile, per_batch(MEM_LEN, 2 * MEM_LEN), anyspec, anyspec,
                   _full((1, D_MODEL)), _full((1, D_MODEL)), _full((8, CHUNK)),
                   _full(grp), _full((A_GROUPS, CHUNK)), _full((1, A_WIDTH)), _full((1, A_WIDTH)),
                   _full((8, CHUNK)), _full((N_BUCKETS, CHUNK))),
        scratch_shapes=[pltpu.VMEM((IN_WIDTH, D_MODEL), F32), pltpu.VMEM((D_MODEL, D_MODEL), F32),
                        pltpu.VMEM((tm, UV_W), F32), pltpu.VMEM((tm, Z_W), F32),
                        pltpu.VMEM((tm, 512), MM), pltpu.VMEM((tm + CHUNK, 2 * CHUNK), MM),
                        pltpu.VMEM((3, tm, D_MODEL), MM), pltpu.VMEM((CHUNK, D_MODEL), MM),
                        pltpu.VMEM((tm, IN_WIDTH), MM),
                        pltpu.VMEM((tm, D_MODEL), F32),
                        pltpu.VMEM((tm, D_MODEL), F32), pltpu.VMEM((2, tm, 1), F32),
                        pltpu.VMEM((tm, D_MODEL), F32), pltpu.VMEM((tm, D_MODEL), F32)]
                       + [pltpu.VMEM((tm, A_WIDTH), F32) for _ in range(4)]
                       + [pltpu.VMEM((tm, A_WIDTH), MM),
                          pltpu.VMEM((bpt * 2, 2 * CHUNK, 2 * CHUNK), F32),
                          pltpu.VMEM((bpt * 2, 2 * CHUNK, CHUNK), F32),
                          pltpu.VMEM((4, tm, MEM_LEN), F32),
                          pltpu.VMEM((bpt * 2, 2 * CHUNK, CHUNK), MM),
                          pltpu.VMEM((bpt * 2, 2 * CHUNK, CHUNK), MM),
                          pltpu.VMEM((tm + CHUNK, 2 * CHUNK), F32),
                          pltpu.VMEM((2, 2 * CHUNK, 2 * CHUNK), F32),
                          pltpu.VMEM(grp, F32),
                          pltpu.VMEM((8, CHUNK), F32),
                          pltpu.SemaphoreType.DMA((2,))],
        compiler_params=_params(dimension_semantics=("arbitrary", "arbitrary")),
    )(x2, x2, x2, x2, tgt2, mkv3, bias.reshape(2, 2 * CHUNK, 2 * CHUNK), sinks, vg, vb, wt, wtt, bcol, g1, g2, w_in_t, w_o, buckets)


class _ShardReduce:
    def __init__(self, pos, g, bufs, sems):
        self.x, self.y, self.c = pos
        self.g = g
        self.own, self.rcv, self.sbuf, self.rbuf, self.cbuf = bufs
        self.ld, self.sa, self.ra, self.sb, self.rb = sems
        self.nrow = g.shape[1]
        self.here = (self.x, self.y, self.c)
        self.sib = (self.x, self.y, 1 - self.c)
        self.first, self.second, self.far = _route(*pos)

    def _load(self, q):
        return pltpu.make_async_copy(self.g.at[2 * q + self.c], self.own.at[q], self.ld.at[q])

    def _to_sib(self, q, to):
        return _remote(self.g.at[2 * q + 1 - self.c], self.rcv.at[q], self.sa.at[q], self.ra.at[q], to)

    def _send(self, k, to):
        dst = self.cbuf.at[0] if k == 1 else self.rbuf.at[0 if k == 0 else 1]
        return _remote(self.sbuf.at[k], dst, self.sb.at[k], self.rb.at[k], to)

    def _stage(self, k, which, extra=None):
        def cast(r):
            v = self.rcv[which, r, :]
            if extra is not None:
                v = v + extra[0, r, :].astype(F32)
            self.sbuf[k, r, :] = v.astype(BF16)

        _rows_loop(self.nrow, cast)

    @staticmethod
    def _q(chip):
        return 2 * chip[0] + chip[1]

    def start(self):
        for q in range(4):
            self._load(q).start()
            self._to_sib(q, self.sib).start()

    def mid(self):
        for q in range(4):
            self._load(q).wait()
            self._to_sib(q, self.here).wait_recv()

        def add(r):
            for q in range(4):
                self.rcv[q, r, :] = self.rcv[q, r, :] + self.own[q, r, :]

        _rows_loop(self.nrow, add)
        to_first = (self.first[0], self.first[1], self.c)
        self._stage(0, self._q(self.first))
        self._send(0, to_first).start()
        self._stage(1, self._q(self.far))
        self._send(1, to_first).start()

    def pass_on(self):
        self._send(1, self.here).wait_recv()
        self._stage(2, self._q(self.second), extra=self.cbuf)
        self._send(2, (self.second[0], self.second[1], self.c)).start()

    def finish(self, out):
        self._send(0, self.here).wait_recv()
        self._send(2, self.here).wait_recv()
        which = 2 * self.x + self.y

        def tot(r):
            out[r, :] = (self.rcv[which, r, :] + self.rbuf[0, r, :].astype(F32)) + self.rbuf[1, r, :].astype(F32)

        _rows_loop(self.nrow, tot)
        for q in range(4):
            self._to_sib(q, self.sib).wait_send()
        to_first = (self.first[0], self.first[1], self.c)
        self._send(0, to_first).wait_send()
        self._send(1, to_first).wait_send()
        self._send(2, (self.second[0], self.second[1], self.c)).wait_send()


def _reduce_scratch(shape):
    return [pltpu.VMEM((4,) + shape, F32), pltpu.VMEM((4,) + shape, F32),
            pltpu.VMEM((3,) + shape, BF16), pltpu.VMEM((2,) + shape, BF16), pltpu.VMEM((1,) + shape, BF16),
            pltpu.SemaphoreType.DMA((4,)), pltpu.SemaphoreType.DMA((4,)), pltpu.SemaphoreType.DMA((4,)),
            pltpu.SemaphoreType.DMA((3,)), pltpu.SemaphoreType.DMA((3,))]


_N_RED = 10

_S_LAYOUT = (((1, D_MODEL), 0), ((1, D_MODEL), 8), ((1, D_MODEL), 16),
             ((1, A_WIDTH), 24), ((1, A_WIDTH), 28), ((A_GROUPS, CHUNK), 32),
             ((1, 4), 36), ((N_BUCKETS, 4), 40),
             ((A_GROUPS * CHUNK, CHUNK), 72))
_LOSS_ROW = 37
_W_SP_ROW = _S_LAYOUT[-1][1]
_S_ROWS = _W_SP_ROW + A_GROUPS * CHUNK
_N_SMALL = len(_S_LAYOUT)


def _pack_rows(dst, refs, tile=None):
    for (shp, r0), ref in zip(_S_LAYOUT, refs):
        if tuple(ref.shape) == (shp[1], shp[0]) and shp[0] != shp[1]:
            tile[...] = jnp.zeros_like(tile)
            tile[0:shp[1], 0:shp[0]] = ref[...]
            dst[r0:r0 + shp[0], 0:shp[1]] = tile[...].T[0:shp[0], 0:shp[1]]
        elif shp[0] == 1 and shp[1] >= CHUNK:
            for i in range(shp[1] // CHUNK):
                dst[r0 + i:r0 + i + 1, :] = ref[:, i * CHUNK:(i + 1) * CHUNK]
        elif ref.shape[-1] == CHUNK:
            dst[r0:r0 + shp[0], :] = ref[0:shp[0], :]
        else:
            dst[r0:r0 + shp[0], 0:shp[1]] = ref[...]


def _unpack_rows(src, refs):
    for (shp, r0), ref in zip(_S_LAYOUT, refs):
        if shp[0] == 1 and shp[1] >= CHUNK:
            for i in range(shp[1] // CHUNK):
                ref[:, i * CHUNK:(i + 1) * CHUNK] = src[r0 + i:r0 + i + 1, :]
        elif shp[1] == CHUNK:
            ref[...] = src[r0:r0 + shp[0], :]
        else:
            if tuple(ref.shape) == (shp[1], shp[0]):
                ref[...] = src[r0:r0 + CHUNK, :].T[0:shp[1], 0:shp[0]]
            else:
                ref[...] = src[r0:r0 + shp[0], 0:shp[1]]


_MEM_G = 2


def _greduce(ga, gb, dmkv, mem2, gm, w_mkv, small_g, loss_p):
    shp_c = (SHARD_O, 2 * MEM_LEN)
    shapes = (shp_c, gb.shape[1:], ga.shape[1:])
    rs = _S_ROWS

    def body(*refs):
        it = iter(refs)
        take = lambda n: [next(it) for _ in range(n)]
        gb_ref, ga_ref, d_ref, m_ref, gm_ref, wm_ref = take(6)
        sg_refs = take(_N_SMALL - 1)
        loss_ref, = take(1)
        oc, ob, oa, ogs = take(4)
        red = take(3 * _N_RED)
        gs_ref, rs_a, rs_b, rs_w, gc_ref, dgm_ref = take(6)
        ssem_a, rsem_a, ssem_b, rsem_b = take(4)

        pos = _position()
        x, y, cc = pos
        myq = 2 * x + y
        here, sib = (x, y, cc), (x, y, 1 - cc)
        chips = _other_chips(x, y)
        reducers = [_ShardReduce(pos, g, red[k * _N_RED:k * _N_RED + 5], red[k * _N_RED + 5:(k + 1) * _N_RED])
                    for k, g in enumerate((gc_ref, gb_ref, ga_ref))]
        for rd in reducers[1:]:
            rd.start()

        xf = m_ref[...]
        nm = xf * _rms(xf)
        hm = (nm * gm_ref[...]).astype(MM)
        d = d_ref[...].astype(MM)
        for o in range(N_DEV):
            gc_ref[o] = _dot_tn(hm[:, o * SHARD_O:(o + 1) * SHARD_O], d)
        dgm_ref[...] = jnp.sum(_dot_nt(d, wm_ref[...]) * nm, axis=0, keepdims=True)
        reducers[0].start()

        gs_ref[...] = jnp.zeros_like(gs_ref)
        _pack_rows(gs_ref, sg_refs[:_MEM_G] + [dgm_ref] + sg_refs[_MEM_G:])
        gs_ref[_LOSS_ROW:_LOSS_ROW + 1, :] = loss_ref[0:1, :]
        small_a = _remote(gs_ref, rs_a, ssem_a, rsem_a, sib)
        small_a.start()

        _remote(gs_ref, rs_a, ssem_a, rsem_a, here).wait_recv()
        rs_b[myq] = gs_ref[0:_W_SP_ROW, :] + rs_a[0:_W_SP_ROW, :]
        rs_w[myq] = (gs_ref[_W_SP_ROW:rs, :] + rs_a[_W_SP_ROW:rs, :]).astype(BF16)
        small_b = []
        for j, chip in enumerate(chips):
            to = (chip[0], chip[1], cc)
            small_b.append(_remote(rs_b.at[myq], rs_b.at[myq], ssem_b.at[0, j], rsem_b.at[0, j], to))
            small_b.append(_remote(rs_w.at[myq], rs_w.at[myq], ssem_b.at[1, j], rsem_b.at[1, j], to))
        for cp in small_b:
            cp.start()
        late_last = reducers[1:] + reducers[:1]
        for rd in late_last:
            rd.mid()
        for rd in late_last:
            rd.pass_on()

        for j in range(3):
            _remote(rs_b.at[myq], rs_b.at[myq], ssem_b.at[0, j], rsem_b.at[0, j], here).wait_recv()
            _remote(rs_w.at[myq], rs_w.at[myq], ssem_b.at[1, j], rsem_b.at[1, j], here).wait_recv()
        ogs[0:_W_SP_ROW, :] = ((rs_b[0] + rs_b[1]) + rs_b[2]) + rs_b[3]

        def tot_w(r):
            w = [rs_w[q, r, :].astype(F32) for q in range(4)]
            ogs[pl.ds(pl.multiple_of(_W_SP_ROW + r.start, 8), _ROWS), :] = ((w[0] + w[1]) + w[2]) + w[3]

        _rows_loop(rs - _W_SP_ROW, tot_w)
        for rd, out in zip(late_last, (ob, oa, oc)):
            rd.finish(out)
        small_a.wait_send()
        for cp in small_b:
            cp.wait_send()

    vm = pl.BlockSpec(memory_space=pltpu.VMEM)
    anyspec = pl.BlockSpec(memory_space=pl.ANY)
    scratch = []
    for shp in shapes:
        scratch += _reduce_scratch(shp)
    scratch += [pltpu.VMEM((rs, CHUNK), F32), pltpu.VMEM((rs, CHUNK), F32),
                pltpu.VMEM((4, _W_SP_ROW, CHUNK), F32), pltpu.VMEM((4, rs - _W_SP_ROW, CHUNK), BF16),
                pltpu.VMEM((N_DEV,) + shp_c, F32), pltpu.VMEM((1, D_MODEL), F32),
                pltpu.SemaphoreType.DMA, pltpu.SemaphoreType.DMA,
                pltpu.SemaphoreType.DMA((2, 3)), pltpu.SemaphoreType.DMA((2, 3))]
    tc, tb, ta, ts = pl.pallas_call(
        body, name="greduce",
        out_shape=tuple([jax.ShapeDtypeStruct(shp, F32) for shp in shapes] + [jax.ShapeDtypeStruct((rs, CHUNK), F32)]),
        in_specs=[anyspec] * 2 + [vm] * (4 + _N_SMALL),
        out_specs=(vm, vm, vm, vm),
        scratch_shapes=scratch,
        compiler_params=_params(),
    )(gb, ga, dmkv, mem2, gm, w_mkv, *small_g, loss_p)
    return ta, tb, tc, ts


def _adamw(w, g, m, v):
    m = ADAM_B1 * m + (1.0 - ADAM_B1) * g
    v = ADAM_B2 * v + (1.0 - ADAM_B2) * (g * g)
    m_hat = m / (1.0 - ADAM_B1 ** ADAM_STEP)
    v_hat = v / (1.0 - ADAM_B2 ** ADAM_STEP)
    delta = -ADAM_LR * (m_hat / (jnp.sqrt(v_hat) + ADAM_EPS) + ADAM_WD * w)
    return delta, m, v


def _update(ta, tb, tc, ts, big_wmv, small_wmv):
    shapes = (ta.shape, tb.shape, tc.shape)
    rs = _S_ROWS
    small_shapes = [tuple(a.shape) for a in small_wmv[0]]

    def body(*refs):
        it = iter(refs)
        take = lambda n: [next(it) for _ in range(n)]
        ga_ref, gb_ref, gc_ref, gs_ref = take(4)
        wa, ma, va, wb, mb, vb_, wc, mc, vc = take(9)
        sw_refs, sm_refs, sv_refs = take(_N_SMALL), take(_N_SMALL), take(_N_SMALL)
        oga, oda, oma, ova, ogb, odb, omb, ovb, ogc, odc, omc, ovc = take(12)
        so_refs = [take(_N_SMALL) for _ in range(4)]
        loss_out, = take(1)
        ws, ms, vs, ods, oms, ovs, turn = take(7)

        def update_rows(nrow, rows, g_r, w_r, m_r, v_r, og, od, om, ov):
            def upd(r):
                g = g_r[r, :]
                d, m, v = _adamw(w_r[r, :], g, m_r[r, :], v_r[r, :])
                og[r, :] = g
                od[r, :] = d
                om[r, :] = m
                ov[r, :] = v

            _rows_loop(nrow, upd, rows)

        update_rows(a_rows, 16, ga_ref, wa, ma, va, oga, oda, oma, ova)

        @pl.when(pl.program_id(0) == 0)
        def _():
            update_rows(shapes[1][0], _ROWS, gb_ref, wb, mb, vb_, ogb, odb, omb, ovb)
            update_rows(shapes[2][0], _ROWS, gc_ref, wc, mc, vc, ogc, odc, omc, ovc)
            for buf in (ws, ms, vs):
                buf[...] = jnp.zeros_like(buf)
            _pack_rows(ws, sw_refs, turn)
            _pack_rows(ms, sm_refs, turn)
            _pack_rows(vs, sv_refs, turn)

            def upd_s(i, _):
                r = pl.ds(pl.multiple_of(i * 8, 8), 8)
                d, m, v = _adamw(ws[r, :], gs_ref[r, :], ms[r, :], vs[r, :])
                ods[r, :] = d
                oms[r, :] = m
                ovs[r, :] = v
                return 0

            lax.fori_loop(0, rs // 8, upd_s, 0)
            for k, buf in enumerate((gs_ref, ods, oms, ovs)):
                _unpack_rows(buf, so_refs[k])
            loss_out[...] = gs_ref[_LOSS_ROW:_LOSS_ROW + 1, 0:1]

    n_blocks = 2
    a_rows = shapes[0][0] // n_blocks
    a_spec = pl.BlockSpec((a_rows, shapes[0][1]), lambda i: (i, 0))
    big_out, big_out_specs = [], []
    for shp in shapes:
        big_out += [jax.ShapeDtypeStruct(shp, F32)] * 4
        big_out_specs += [a_spec if shp == shapes[0] else _full(shp)] * 4
    small_out_shapes = [shp[::-1] if shp == (N_BUCKETS, 4) else shp for shp in small_shapes] * 4
    small_out = [jax.ShapeDtypeStruct(shp, F32) for shp in small_out_shapes]
    out_shape = tuple(big_out + small_out + [jax.ShapeDtypeStruct((1, 1), F32)])
    in_specs = ([a_spec, _full(shapes[1]), _full(shapes[2]), _full((rs, CHUNK))]
                + [a_spec] * 3 + [_full(shapes[1])] * 3 + [_full(shapes[2])] * 3
                + [_full(shp) for shp in small_shapes] * 3)
    return pl.pallas_call(
        body, name="update", grid=(n_blocks,),
        out_shape=out_shape,
        in_specs=in_specs,
        out_specs=tuple(big_out_specs + [_full(shp) for shp in small_out_shapes] + [_full((1, 1))]),
        scratch_shapes=[pltpu.VMEM((rs, CHUNK), F32) for _ in range(6)] + [pltpu.VMEM((CHUNK, CHUNK), F32)],
        compiler_params=_params(dimension_semantics=("arbitrary",)),
    )(ta, tb, tc, ts, *big_wmv, *small_wmv[0], *small_wmv[1], *small_wmv[2])


def kernel(x, mem, pre_norm_g, post_norm_g, mem_norm_g, w_in, w_mem_kv, v_norm_g, v_norm_b, w_spatial, b_spatial, attn_sinks, rel_bias, w_out, loss_target, m_pre_norm_g, m_post_norm_g, m_mem_norm_g, m_w_in, m_w_mem_kv, m_v_norm_g, m_v_norm_b, m_w_spatial, m_b_spatial, m_attn_sinks, m_rel_bias, m_w_out, v_pre_norm_g, v_post_norm_g, v_mem_norm_g, v_w_in, v_w_mem_kv, v_v_norm_g, v_v_norm_b, v_w_spatial, v_b_spatial, v_attn_sinks, v_rel_bias, v_w_out):
    sh_a = (w_in[0].T, m_w_in[0].T, v_w_in[0].T)
    sh_b = (w_out[0], m_w_out[0], v_w_out[0])
    sh_c = (w_mem_kv[0], m_w_mem_kv[0], v_w_mem_kv[0])
    nb, s, _ = x.shape
    t = nb * s
    x2 = x.reshape(t, D_MODEL)
    tgt2 = loss_target.reshape(t, D_MODEL)
    mem2 = mem.reshape(nb * MEM_LEN, D_MODEL)
    buckets = jnp.asarray(_t5_buckets())

    wa, wb, wc, bias, wt, wtt, bcol, mkv = _wgather(sh_a[0], sh_b[0], sh_c[0], rel_bias.T, w_spatial[0], b_spatial[0],
                                                    buckets, mem2, mem_norm_g)
    w_mkv = wc.reshape(D_MODEL, 2 * MEM_LEN)
    gx, dmkv, dwi, dwo, dg1, dg2, loss_p, dwsp, dbs, dvg, dvb, dsink, drel = _layer(
        x2, tgt2, mkv.reshape(nb, MEM_LEN, 2 * MEM_LEN), bias, attn_sinks.reshape(4), v_norm_g, v_norm_b, wt, wtt, bcol,
        pre_norm_g, post_norm_g, wa.reshape(IN_WIDTH, D_MODEL), wb.reshape(D_MODEL, D_MODEL), buckets,
        nb, s, min(256, s))
    gx = gx.reshape(nb, s, D_MODEL)
    small_grads = [dg1, dg2, dvg, dvb, dbs, dsink, drel, dwsp.reshape(A_GROUPS * CHUNK, CHUNK)]

    small_names = ["pre_norm_g", "post_norm_g", "mem_norm_g", "v_norm_g", "v_norm_b", "b_spatial", "attn_sinks",
                   "rel_bias", "w_spatial"]
    given = dict(pre_norm_g=(pre_norm_g, m_pre_norm_g, v_pre_norm_g), post_norm_g=(post_norm_g, m_post_norm_g, v_post_norm_g),
                 mem_norm_g=(mem_norm_g, m_mem_norm_g, v_mem_norm_g), v_norm_g=(v_norm_g, m_v_norm_g, v_v_norm_g),
                 v_norm_b=(v_norm_b, m_v_norm_b, v_v_norm_b), b_spatial=(b_spatial, m_b_spatial, v_b_spatial),
                 attn_sinks=(attn_sinks, m_attn_sinks, v_attn_sinks), rel_bias=(rel_bias, m_rel_bias, v_rel_bias),
                 w_spatial=(w_spatial, m_w_spatial, v_w_spatial))
    small_wmv = [[given[n][k].T if n == "rel_bias" else given[n][k].reshape(shp)
                  for n, (shp, _) in zip(small_names, _S_LAYOUT)] for k in range(3)]

    ta, tb, tc, ts = _greduce(dwi.reshape(N_DEV, SHARD_IN, D_MODEL), dwo.reshape(N_DEV, SHARD_O, D_MODEL),
                              dmkv.reshape(nb * MEM_LEN, 2 * MEM_LEN), mem2, mem_norm_g, w_mkv, small_grads, loss_p)
    outs = _update(ta, tb, tc, ts, (*sh_a, *sh_b, *sh_c), small_wmv)
    ra, rb, rc = outs[0:4], outs[4:8], outs[8:12]
    loss = outs[12 + 4 * _N_SMALL].reshape(())

    res = {}
    for k, kind in enumerate(("grad", "delta", "new_m", "new_v")):
        res[kind, "w_in"] = ra[k].T[None]
        res[kind, "w_out"] = rb[k][None]
        res[kind, "w_mem_kv"] = rc[k][None]
        for i, n in enumerate(small_names):
            o = outs[12 + k * _N_SMALL + i]
            res[kind, n] = o.T if n == "rel_bias" else o.reshape(given[n][0].shape)
    order = ["pre_norm_g", "post_norm_g", "mem_norm_g", "w_in", "w_mem_kv", "v_norm_g", "v_norm_b", "w_spatial",
             "b_spatial", "attn_sinks", "rel_bias", "w_out"]
    flat = [res[kind, n] for kind in ("grad", "delta", "new_m", "new_v") for n in order]
    return (loss, gx, *flat)
```

```python
import numpy as np
import jax
import jax.numpy as jnp
from jax import lax
from jax.experimental import pallas as pl
from jax.experimental.pallas import tpu as pltpu

F32 = jnp.float32
BF16 = jnp.bfloat16
MM = jnp.bfloat16

D_MODEL = 1024
CHUNK = 128
A_GROUPS = 4
A_WIDTH = 512
UV_W = 1024
QKV_W = 768
Z_W = 1024
IN_WIDTH = UV_W + QKV_W + Z_W
MEM_LEN = 256
N_BUCKETS = 32
MAX_DISTANCE = 128
EPS = 1e-6
NEG = -1e30
SCALE = 0.125
N_DEV = 8
SHARD_IN = IN_WIDTH // N_DEV
SHARD_O = D_MODEL // N_DEV

SQ_COL, SK_COL, SV_COL, MQ_COL, Z_COL = UV_W, UV_W + 256, UV_W + 384, UV_W + 512, UV_W + QKV_W
DW_PIECES = ((0, SQ_COL), (SQ_COL, Z_COL), (Z_COL, IN_WIDTH))
YB_OFF, YC_OFF = 512, 768

ADAM_LR = 0.001
ADAM_B1 = 0.9
ADAM_B2 = 0.999
ADAM_EPS = 1e-08
ADAM_WD = 0.01
ADAM_STEP = 10

VMEM_LIMIT = 60 * 1024 * 1024

_GELU_C = 0.7978845608028654
_GELU_A = 0.044715

MESH = pl.DeviceIdType.MESH
_ROWS = 32


def _dot(a, b):
    return lax.dot_general(a, b, (((1,), (0,)), ((), ())), preferred_element_type=F32)


def _dot_nt(a, b):
    return lax.dot_general(a, b, (((1,), (1,)), ((), ())), preferred_element_type=F32)


def _dot_tn(a, b):
    return lax.dot_general(a, b, (((0,), (0,)), ((), ())), preferred_element_type=F32)


def _gelu_and_grad(x):
    x2 = x * x
    t = jnp.tanh(x * (_GELU_C + (_GELU_C * _GELU_A) * x2))
    w = 0.5 * t + 0.5
    g = x * w
    dg = w * (1.0 + (x - g) * ((2.0 * _GELU_C) + (6.0 * _GELU_C * _GELU_A) * x2))
    return g, dg


def _t5_buckets():
    qi = np.arange(CHUNK)[:, None]
    kj = np.arange(2 * CHUNK)[None, :]
    n = np.maximum(qi + CHUNK - kj, 0)
    max_exact = N_BUCKETS // 2
    large = max_exact + (np.log(np.maximum(n, 1) / max_exact) / np.log(MAX_DISTANCE / max_exact)
                         * (N_BUCKETS - max_exact)).astype(np.int32)
    large = np.minimum(large, N_BUCKETS - 1)
    return np.where(n < max_exact, n, large).astype(np.int32)


def _params(**kw):
    return pltpu.CompilerParams(vmem_limit_bytes=VMEM_LIMIT, **kw)


def _full(shape, single=False):
    nd = len(shape)
    if single:
        return pl.BlockSpec(shape, lambda *_: (0,) * nd, pipeline_mode=pl.Buffered(1))
    return pl.BlockSpec(shape, lambda *_: (0,) * nd)


def _window_valid():
    qi = lax.broadcasted_iota(jnp.int32, (CHUNK, 2 * CHUNK), 0)
    kj = lax.broadcasted_iota(jnp.int32, (CHUNK, 2 * CHUNK), 1)
    dist = qi + CHUNK - kj
    return (dist >= 0) & (dist < CHUNK)


def _position():
    return lax.axis_index("x"), lax.axis_index("y"), lax.axis_index("c")


def _other_chips(x, y):
    return [(1 - x, y), (x, 1 - y), (1 - x, 1 - y)]


def _route(x, y, c):
    first = (x * c + (1 - x) * (1 - c), y * (1 - c) + (1 - y) * c)
    second = (x * (1 - c) + (1 - x) * c, y * c + (1 - y) * (1 - c))
    return first, second, (1 - x, 1 - y)


def _remote(src, dst, ssem, rsem, to):
    return pltpu.make_async_remote_copy(src_ref=src, dst_ref=dst, send_sem=ssem, recv_sem=rsem,
                                        device_id=to, device_id_type=MESH)


def _rows_loop(nrow, fn, rows=_ROWS):
    assert nrow % rows == 0

    def step(i, _):
        fn(pl.ds(pl.multiple_of(i * rows, rows), rows))
        return 0

    lax.fori_loop(0, nrow // rows, step, 0)


class _Gather:
    def __init__(self, pos, out, ssem, rsem):
        self.x, self.y, self.c = pos
        self.out, self.ssem, self.rsem = out, ssem, rsem
        self.me = 4 * self.x + 2 * self.y + self.c
        self.here = (self.x, self.y, self.c)
        self.sib = (self.x, self.y, 1 - self.c)
        self.first, self.second, self.far = _route(*pos)

    def _copy(self, k, blk, to):
        r = self.out.at[blk]
        return _remote(r, r, self.ssem.at[k], self.rsem.at[k], to)

    def _idx(self, chip, core):
        return 4 * chip[0] + 2 * chip[1] + core

    def _on(self, chip):
        return (chip[0], chip[1], self.c)

    def start(self):
        self._copy(0, self.me, self.sib).start()
        self._copy(1, self.me, self._on(self.first)).start()
        self._copy(2, self.me, self._on(self.second)).start()

    def forward(self):
        c = self.c
        self._copy(1, self._idx(self.first, c), self.here).wait_recv()
        self._copy(3, self._idx(self.first, c), self._on(self.second)).start()
        self._copy(4, self._idx(self.first, c), self.sib).start()
        self._copy(2, self._idx(self.second, c), self.here).wait_recv()
        self._copy(5, self._idx(self.second, c), self.sib).start()
        self._copy(3, self._idx(self.far, c), self.here).wait_recv()
        self._copy(6, self._idx(self.far, c), self.sib).start()

    def finish(self):
        c = self.c
        self._copy(0, self._idx((self.x, self.y), 1 - c), self.here).wait_recv()
        for k, chip in ((4, self.second), (5, self.first), (6, self.far)):
            self._copy(k, self._idx(chip, 1 - c), self.here).wait_recv()
        self._copy(0, self.me, self.sib).wait_send()
        self._copy(1, self.me, self._on(self.first)).wait_send()
        self._copy(2, self.me, self._on(self.second)).wait_send()
        self._copy(3, self._idx(self.first, c), self._on(self.second)).wait_send()
        for k, chip in ((4, self.first), (5, self.second), (6, self.far)):
            self._copy(k, self._idx(chip, c), self.sib).wait_send()


def _prep_tables(rb_ref, w_ref, b_ref, bk_ref, bias_ref, wt_ref, wtt_ref, bcol_ref):
    valid = _window_valid()
    bk = bk_ref[...]
    acc = [jnp.full((CHUNK, 2 * CHUNK), NEG, F32) for _ in range(4)]
    for b in range(N_BUCKETS):
        hit = (bk == b) & valid
        for h in range(4):
            acc[h] = jnp.where(hit, rb_ref[h, b], acc[h])
    for h in range(4):
        bias_ref[h] = acc[h]
    r = lax.broadcasted_iota(jnp.int32, (CHUNK, CHUNK), 0)
    c = lax.broadcasted_iota(jnp.int32, (CHUNK, CHUNK), 1)
    for g in range(A_GROUPS):
        w = jnp.where(r >= c, w_ref[g], 0.0)
        wt_ref[g] = w.astype(MM)
        wtt_ref[g] = w.T.astype(MM)
        bcol_ref[g] = jnp.broadcast_to(b_ref[g:g + 1, :], (CHUNK, CHUNK)).T


def _wgather(a, b, c, rel_bias, w_sp, b_sp, buckets, mem2, gm):
    tmem = mem2.shape[0]

    def body(a_ref, b_ref, c_ref, rb_ref, w_ref, bsp_ref, bk_ref, m_ref, gm_ref,
             oa, ob, oc, bias_ref, wt_ref, wtt_ref, bcol_ref, mkv_ref, ssem, rsem):
        pos = _position()
        me = 4 * pos[0] + 2 * pos[1] + pos[2]
        gathers = []
        for k, (src, out) in enumerate(((c_ref, oc), (b_ref, ob), (a_ref, oa))):
            out[me] = src[...].astype(BF16)
            g = _Gather(pos, out, ssem.at[k], rsem.at[k])
            g.start()
            gathers.append(g)
        _prep_tables(rb_ref, w_ref, bsp_ref, bk_ref, bias_ref, wt_ref, wtt_ref, bcol_ref)
        for g in gathers:
            g.forward()
        gathers[0].finish()
        xf = m_ref[...]
        hm = (xf * _rms(xf) * gm_ref[...]).astype(MM)
        acc = jnp.zeros((tmem, 2 * MEM_LEN), F32)
        for d in range(N_DEV):
            acc = acc + _dot(hm[:, d * SHARD_O:(d + 1) * SHARD_O], oc[d])
        mkv_ref[...] = acc.astype(MM)
        for g in gathers[1:]:
            g.finish()

    vm = pl.BlockSpec(memory_space=pltpu.VMEM)
    grp = (A_GROUPS, CHUNK, CHUNK)
    return pl.pallas_call(
        body, name="wgather",
        out_shape=(jax.ShapeDtypeStruct((N_DEV,) + a.shape, BF16),
                   jax.ShapeDtypeStruct((N_DEV,) + b.shape, BF16),
                   jax.ShapeDtypeStruct((N_DEV,) + c.shape, BF16),
                   jax.ShapeDtypeStruct((4, CHUNK, 2 * CHUNK), F32),
                   jax.ShapeDtypeStruct(grp, MM), jax.ShapeDtypeStruct(grp, MM), jax.ShapeDtypeStruct(grp, F32),
                   jax.ShapeDtypeStruct((tmem, 2 * MEM_LEN), MM)),
        in_specs=[vm, vm, vm, pl.BlockSpec(memory_space=pltpu.SMEM), vm, vm, vm, vm, vm],
        out_specs=tuple([vm] * 8),
        scratch_shapes=[pltpu.SemaphoreType.DMA((3, 7)), pltpu.SemaphoreType.DMA((3, 7))],
        compiler_params=_params(),
    )(a, b, c, rel_bias, w_sp, b_sp, buckets, mem2, gm)


def _half_masks(rows):
    lane = lax.broadcasted_iota(jnp.int32, (rows, CHUNK), 1)
    return lane < 64


def _dup_heads(band):
    b32 = band.astype(F32)
    rolled = pltpu.roll(b32, 64, 1)
    lo = _half_masks(band.shape[0])
    return (jnp.where(lo, b32, rolled).astype(MM), jnp.where(lo, rolled, b32).astype(MM))


def _swa_probs(qk, bias_h, sink_h, first_add):
    s = qk * SCALE + bias_h + first_add
    m = jnp.maximum(jnp.max(s, axis=-1, keepdims=True), sink_h)
    p = jnp.exp(s - m)
    es = jnp.exp(sink_h - m)
    inv = 1.0 / (jnp.sum(p, axis=-1, keepdims=True) + es)
    return p * inv, es * inv


def _softmax(s):
    m = jnp.max(s, axis=-1, keepdims=True)
    p = jnp.exp(s - m)
    return p * (1.0 / jnp.sum(p, axis=-1, keepdims=True))


def _first_block_mask(n):
    col = lax.broadcasted_iota(jnp.int32, (2 * CHUNK, 2 * CHUNK), 1)
    return jnp.where((col < CHUNK) & (n == 0), NEG, 0.0)


def _stack_heads(x128, lo):
    return jnp.concatenate([jnp.where(lo, x128, 0.0), jnp.where(lo, 0.0, x128)], axis=0).astype(MM)


def _rms(xf):
    return lax.rsqrt(jnp.mean(xf * xf, axis=-1, keepdims=True) + EPS)


def _layer(x2, tgt2, mkv3, bias, sinks, vg, vb, wt, wtt, bcol, g1, g2, w_in_t, w_o, buckets, nb, s, tm):
    nt = s // tm
    bpt = tm // CHUNK
    bps = s // CHUNK
    t = nb * s
    last_step = nb * nt - 1

    def tile_at(step):
        return (step // nt) * nt + nt - 1 - step % nt

    def block_before(step):
        return (step // nt) * bps + jnp.maximum((nt - 1 - step % nt) * bpt - 1, 0)

    def body(x_ref, xp_ref, xn_ref, xpn_ref, t_ref, mkv_ref, bias_ref, sink_ref, vg_ref, vb_ref,
             wt_ref, wtt_ref, bcol_ref, g1_ref, g2_ref, wi_ref, wo_ref, bk_ref,
             gx_ref, dmkv_ref, dwi_hbm, dwo_hbm, dg1_ref, dg2_ref, loss_ref, dwsp_ref, dbs_ref,
             dvg_ref, dvb_ref, dsink_ref, drel_ref,
             acc_i, acc_o, uv_s, z_s, q_s, kv_s, h_s, hp_s, dp_s, dxo_s, dh_s, r_s,
             ycat, dyc, u_s, gu_s, gv_s, xh_s, vc_s, pb_s, ps_s, pc_s, kd_s, vd_s,
             dkv_acc, dbias_acc, dsv_acc, dsink_acc, sems):
        b, j = pl.program_id(0), pl.program_id(1)
        jt = nt - 1 - j
        step = b * nt + j
        g1v = g1_ref[...]
        NOW, NEXT, DONE = 0, 1, 2
        dw_cols = list(DW_PIECES)

        def weight_grad(n, slot):
            for c0, c1 in dw_cols[:n]:
                acc_i[c0:c1, :] += _dot_tn(dp_s[:, c0:c1], h_s[slot])
            del dw_cols[:n]

        def pre_norm(x_tile, x_before):
            xf = x_tile[...]
            r_s[NEXT] = _rms(xf)
            h_s[NEXT] = (xf * r_s[NEXT] * g1v).astype(MM)
            xp = x_before[...]
            hp_s[...] = (xp * _rms(xp) * g1v).astype(MM)

        def project_z():
            z_s[...] = _dot_nt(h_s[NEXT], wi_ref[Z_COL:IN_WIDTH, :])

        def project_uv():
            uv_s[...] = _dot_nt(h_s[NEXT], wi_ref[0:UV_W, :])

        @pl.when(step == 0)
        def _():
            for ref in (acc_i, acc_o, dg1_ref, dg2_ref, loss_ref, dwsp_ref, dvg_ref, dvb_ref,
                        dbias_acc, dsv_acc, dsink_acc):
                ref[...] = jnp.zeros_like(ref)
            dp_s[...] = jnp.zeros_like(dp_s)
            h_s[NOW] = jnp.zeros((tm, D_MODEL), MM)
            pre_norm(x_ref, xp_ref)
            project_z()
            project_uv()

        h_s[DONE] = h_s[NOW]
        r_s[NOW] = r_s[NEXT]
        h = h_s[NEXT]
        h_s[NOW] = h
        hp = hp_s[...]

        @pl.when(j == 0)
        def _():
            dmkv_ref[...] = jnp.zeros_like(dmkv_ref)
            dkv_acc[...] = jnp.zeros_like(dkv_acc)

        carry = dkv_acc[0:CHUNK, :]
        dkv_acc[...] = jnp.zeros_like(dkv_acc)
        dkv_acc[tm:tm + CHUNK, :] = carry

        lo = _half_masks(CHUNK)
        lob = _half_masks(2 * CHUNK)
        lot = _half_masks(tm)

        qkv = _dot_nt(h, wi_ref[SQ_COL:Z_COL, :])
        q_s[:, 0:256] = qkv[:, 0:256].astype(MM)
        q_s[:, 256:512] = qkv[:, 512:768].astype(MM)
        kv_s[CHUNK:CHUNK + tm, :] = qkv[:, 256:512].astype(MM)
        kv_s[0:CHUNK, :] = _dot_nt(hp, wi_ref[SK_COL:MQ_COL, :]).astype(MM)

        weight_grad(1, DONE)
        b_qk, b_pb = [], []
        for blk in range(bpt):
            r0 = blk * CHUNK
            rows = slice(r0, r0 + CHUNK)
            for g in range(A_GROUPS):
                cg = slice(g * CHUNK, (g + 1) * CHUNK)
                u, gu = _gelu_and_grad(uv_s[rows, cg])
                v, gv = _gelu_and_grad(uv_s[rows, A_WIDTH + g * CHUNK:A_WIDTH + (g + 1) * CHUNK])
                mu = jnp.mean(v, axis=-1, keepdims=True)
                xc = v - mu
                rstd = lax.rsqrt(jnp.mean(xc * xc, axis=-1, keepdims=True) + EPS)
                xhat = xc * rstd
                vc = (xhat * vg_ref[:, cg] + vb_ref[:, cg]).astype(MM)
                sv = _dot(wt_ref[g], vc) + bcol_ref[g]
                u_s[rows, cg] = u
                gu_s[rows, cg] = sv * gu
                gv_s[rows, cg] = rstd * gv
                xh_s[rows, cg] = xhat
                vc_s[rows, cg] = vc
                ycat[rows, cg] = u * sv
            weight_grad(1, DONE)
            kd = _dup_heads(kv_s[r0:r0 + 2 * CHUNK, 0:CHUNK])
            vd = _dup_heads(kv_s[r0:r0 + 2 * CHUNK, CHUNK:2 * CHUNK])
            for kvh in range(2):
                kd_s[blk * 2 + kvh] = kd[kvh]
                vd_s[blk * 2 + kvh] = vd[kvh]
                q2 = _stack_heads(q_s[rows, kvh * CHUNK:(kvh + 1) * CHUNK].astype(F32), lo)
                b_qk.append(_dot_nt(q2, kd[kvh]))
        qks, pcs = [], []
        for g in range(2):
            q128 = q_s[:, 256 + g * CHUNK:256 + (g + 1) * CHUNK].astype(F32)
            for hh in range(2):
                qsel = jnp.where(lot if hh == 0 else ~lot, q128, 0.0).astype(MM)
                qks.append(_dot_nt(qsel, mkv_ref[:, g * CHUNK:(g + 1) * CHUNK]))
        top =lax.broadcasted_iota(jnp.int32, (2 * CHUNK, 1), 0) < CHUNK
        for blk in range(bpt):
            first_add = _first_block_mask(jt * bpt + blk)
            for kvh in range(2):
                sink2 = jnp.where(top, sink_ref[2 * kvh], sink_ref[2 * kvh + 1])
                probs, ps = _swa_probs(b_qk[blk * 2 + kvh], bias_ref[kvh], sink2, first_add)
                pb_s[blk * 2 + kvh] = probs
                ps_s[blk * 2 + kvh] = jnp.broadcast_to(ps, (2 * CHUNK, CHUNK))
                b_pb.append(probs.astype(MM))
        weight_grad(len(dw_cols), DONE)
        for hd in range(4):
            probs = _softmax(qks[hd] * SCALE)
            pc_s[hd] = probs
            pcs.append(probs.astype(MM))
        for blk in range(bpt):
            rows = slice(blk * CHUNK, (blk + 1) * CHUNK)
            for kvh in range(2):
                out2 = _dot(b_pb[blk * 2 + kvh], vd_s[blk * 2 + kvh])
                ycat[rows, YB_OFF + kvh * CHUNK:YB_OFF + (kvh + 1) * CHUNK] = jnp.where(
                    lo, out2[0:CHUNK], out2[CHUNK:2 * CHUNK])
        outs = [_dot(pcs[hd], mkv_ref[:, MEM_LEN + (hd // 2) * CHUNK:MEM_LEN + (hd // 2 + 1) * CHUNK])
                for hd in range(4)]
        for g in range(2):
            ycat[:, YC_OFF + g * CHUNK:YC_OFF + (g + 1) * CHUNK] = jnp.where(lot, outs[2 * g], outs[2 * g + 1])

        zt = z_s[...]
        sig = 1.0 / (1.0 + jnp.exp(-zt))
        silu = zt * sig
        yc = ycat[...]
        yb = (yc * silu).astype(MM)
        pre_norm(xn_ref, xpn_ref)
        o = _dot(yb, wo_ref[...])
        project_z()
        r2 = _rms(o)
        nrm = o * r2
        g2v = g2_ref[...]
        e = x_ref[...] + nrm * g2v - t_ref[...]
        l1 = jnp.sum(e * e, axis=-1, keepdims=True)
        loss_ref[...] += jnp.broadcast_to(jnp.sum(l1, axis=0, keepdims=True) * (0.5 / D_MODEL), loss_ref.shape)
        dxo = e * (1.0 / D_MODEL)
        dxo_s[...] = dxo
        dg2_ref[...] += jnp.sum(dxo * nrm, axis=0, keepdims=True)
        dn = dxo * g2v
        do = r2 * (dn - nrm * jnp.mean(dn * nrm, axis=-1, keepdims=True))
        dob = do.astype(MM)
        dy = _dot_nt(dob, wo_ref[...])
        dp_s[:, Z_COL:IN_WIDTH] = (dy * yc * (sig * (1.0 + zt * (1.0 - sig)))).astype(MM)
        dyc[...] = dy * silu
        acc_o[...] += _dot_tn(yb, dob)

        def in_proj_bwd(c0, c1):
            part = _dot(dp_s[:, c0:c1], wi_ref[c0:c1, :])
            if c0 == Z_COL:
                dh_s[...] = part
            else:
                dh_s[...] += part

        for blk in range(bpt):
            r0 = blk * CHUNK
            rows = slice(r0, r0 + CHUNK)
            for g in range(A_GROUPS):
                cg = slice(g * CHUNK, (g + 1) * CHUNK)
                cv = slice(A_WIDTH + g * CHUNK, A_WIDTH + (g + 1) * CHUNK)
                dya = dyc[rows, cg]
                dp_s[rows, cg] = (dya * gu_s[rows, cg]).astype(MM)
                dsv = dya * u_s[rows, cg]
                dsvb = dsv.astype(MM)
                dsv_acc[g] += dsv
                dwsp_ref[g] += _dot_nt(dsvb, vc_s[rows, cg])
                dvc = _dot(wtt_ref[g], dsvb)
                xhat = xh_s[rows, cg]
                dvg_ref[:, cg] += jnp.sum(dvc * xhat, axis=0, keepdims=True)
                dvb_ref[:, cg] += jnp.sum(dvc, axis=0, keepdims=True)
                dxh = dvc * vg_ref[:, cg]
                dv = (dxh - jnp.mean(dxh, axis=-1, keepdims=True)
                      - xhat * jnp.mean(dxh * xhat, axis=-1, keepdims=True))
                dp_s[rows, cv] = (dv * gv_s[rows, cg]).astype(MM)
            if blk == 0:
                in_proj_bwd(Z_COL, IN_WIDTH)
        b_dosel, b_dp, b_dss = [], [], []
        for blk in range(bpt):
            rows = slice(blk * CHUNK, (blk + 1) * CHUNK)
            for kvh in range(2):
                b_dosel.append(_stack_heads(dyc[rows, YB_OFF + kvh * CHUNK:YB_OFF + (kvh + 1) * CHUNK], lo))
                b_dp.append(_dot_nt(b_dosel[-1], vd_s[blk * 2 + kvh]))
        dosels, dps, dsss = [], [], []
        for hd in range(4):
            do128 = dyc[:, YC_OFF + (hd // 2) * CHUNK:YC_OFF + (hd // 2 + 1) * CHUNK]
            dosels.append(jnp.where(lot if hd % 2 == 0 else ~lot, do128, 0.0).astype(MM))
            dps.append(_dot_nt(dosels[hd], mkv_ref[:, MEM_LEN + (hd // 2) * CHUNK:MEM_LEN + (hd // 2 + 1) * CHUNK]))
        in_proj_bwd(0, UV_W)
        for blk in range(bpt):
            for kvh in range(2):
                probs = pb_s[blk * 2 + kvh]
                dp = b_dp[blk * 2 + kvh]
                delta = jnp.sum(probs * dp, axis=-1, keepdims=True)
                ds = probs * (dp - delta)
                dbias_acc[kvh] += ds
                sd = ps_s[blk * 2 + kvh][:, 0:1] * delta
                for gi in range(2):
                    hd = 2 * kvh + gi
                    dsink_acc[hd:hd + 1, :] += jnp.broadcast_to(
                        -jnp.sum(sd[gi * CHUNK:(gi + 1) * CHUNK], axis=0, keepdims=True), (1, CHUNK))
                b_dss.append((ds * SCALE).astype(MM))
        for hd in range(4):
            probs = pc_s[hd]
            ds = probs * (dps[hd] - jnp.sum(probs * dps[hd], axis=-1, keepdims=True))
            dsss.append((ds * SCALE).astype(MM))
        for blk in range(bpt):
            r0 = blk * CHUNK
            rows = slice(r0, r0 + CHUNK)
            dk_f, dv_f = [], []
            for kvh in range(2):
                dss = b_dss[blk * 2 + kvh]
                q2 = _stack_heads(q_s[rows, kvh * CHUNK:(kvh + 1) * CHUNK].astype(F32), lo)
                dq2 = _dot(dss, kd_s[blk * 2 + kvh])
                dkd = _dot_tn(dss, q2)
                dvd = _dot_tn(pb_s[blk * 2 + kvh].astype(MM), b_dosel[blk * 2 + kvh])
                dp_s[rows, SQ_COL + kvh * CHUNK:SQ_COL + (kvh + 1) * CHUNK] = jnp.where(
                    lo, dq2[0:CHUNK], dq2[CHUNK:2 * CHUNK]).astype(MM)
                dk_f.append(dkd + pltpu.roll(dkd, 64, 1))
                dv_f.append(dvd + pltpu.roll(dvd, 64, 1))
            dkv_acc[r0:r0 + 2 * CHUNK, 0:CHUNK] += jnp.where(lob, dk_f[0], dk_f[1])
            dkv_acc[r0:r0 + 2 * CHUNK, CHUNK:2 * CHUNK] += jnp.where(lob, dv_f[0], dv_f[1])
        dp_s[:, SK_COL:MQ_COL] = dkv_acc[CHUNK:CHUNK + tm, :].astype(MM)
        for g in range(2):
            q128 = q_s[:, 256 + g * CHUNK:256 + (g + 1) * CHUNK].astype(F32)
            k128 = mkv_ref[:, g * CHUNK:(g + 1) * CHUNK]
            dq128 = jnp.zeros((tm, CHUNK), F32)
            dk128 = jnp.zeros((MEM_LEN, CHUNK), F32)
            dv128 = jnp.zeros((MEM_LEN, CHUNK), F32)
            for hh in range(2):
                hd = 2 * g + hh
                half = lot if hh == 0 else ~lot
                qsel = jnp.where(half, q128, 0.0).astype(MM)
                dq128 = dq128 + jnp.where(half, _dot(dsss[hd], k128), 0.0)
                dk128 = dk128 + _dot_tn(dsss[hd], qsel)
                dv128 = dv128 + _dot_tn(pc_s[hd].astype(MM), dosels[hd])
            dp_s[:, MQ_COL + g * CHUNK:MQ_COL + (g + 1) * CHUNK] = dq128.astype(MM)
            dmkv_ref[:, g * CHUNK:(g + 1) * CHUNK] += dk128
            dmkv_ref[:, MEM_LEN + g * CHUNK:MEM_LEN + (g + 1) * CHUNK] += dv128

        in_proj_bwd(SQ_COL, Z_COL)
        project_uv()
        dh = dh_s[...]
        r = r_s[NOW]
        nx = x_ref[...] * r
        dg1_ref[...] += jnp.sum(dh * nx, axis=0, keepdims=True)
        dnx = dh * g1v
        gx_ref[...] = dxo_s[...] + r * (dnx - nx * jnp.mean(dnx * nx, axis=-1, keepdims=True))

        @pl.when(step == last_step)
        def _():
            dw_cols.extend(DW_PIECES)
            weight_grad(len(dw_cols), NOW)
            out_i = pltpu.make_async_copy(acc_i, dwi_hbm, sems.at[0])
            out_o = pltpu.make_async_copy(acc_o, dwo_hbm, sems.at[1])
            out_i.start()
            out_o.start()
            r_ = lax.broadcasted_iota(jnp.int32, (CHUNK, CHUNK), 0)
            c_ = lax.broadcasted_iota(jnp.int32, (CHUNK, CHUNK), 1)
            for g in range(A_GROUPS):
                dwsp_ref[g] = jnp.where(r_ >= c_, dwsp_ref[g], 0.0)
                dbs_ref[g:g + 1, :] = jnp.sum(dsv_acc[g].T, axis=0, keepdims=True)
            rows8 = lax.broadcasted_iota(jnp.int32, (8, CHUNK), 0)
            cols8 = lax.broadcasted_iota(jnp.int32, (8, CHUNK), 1)
            sk = jnp.zeros((8, CHUNK), F32)
            for hd in range(4):
                sk = sk + jnp.where((rows8 == 0) & (cols8 == hd),
                                    jnp.broadcast_to(dsink_acc[hd:hd + 1, :], (8, CHUNK)), 0.0)
            dsink_ref[...] = sk
            bk = bk_ref[...]
            valid = _window_valid()
            rrow = lax.broadcasted_iota(jnp.int32, (N_BUCKETS, CHUNK), 0)
            rcol = lax.broadcasted_iota(jnp.int32, (N_BUCKETS, CHUNK), 1)
            acc = jnp.zeros((N_BUCKETS, CHUNK), F32)
            for bb in range(N_BUCKETS):
                hit = (bk == bb) & valid
                for hd in range(4):
                    dbias = dbias_acc[hd // 2, (hd % 2) * CHUNK:(hd % 2 + 1) * CHUNK, :]
                    part = jnp.sum(jnp.where(hit, dbias, 0.0), axis=-1, keepdims=True)
                    tot = jnp.sum(part, axis=0, keepdims=True)
                    acc = acc + jnp.where((rrow == bb) & (rcol == hd), jnp.broadcast_to(tot, (N_BUCKETS, CHUNK)), 0.0)
            drel_ref[...] = acc
            out_i.wait()
            out_o.wait()

    after = lambda b, j: jnp.minimum(b * nt + j + 1, last_step)
    tile = pl.BlockSpec((tm, D_MODEL), lambda b, j: (tile_at(b * nt + j), 0))
    tile_after = pl.BlockSpec((tm, D_MODEL), lambda b, j: (tile_at(after(b, j)), 0))
    halo = pl.BlockSpec((CHUNK, D_MODEL), lambda b, j: (block_before(b * nt + j), 0))
    halo_after = pl.BlockSpec((CHUNK, D_MODEL), lambda b, j: (block_before(after(b, j)), 0))
    per_batch = lambda r, w: pl.BlockSpec((None, r, w), lambda b, j: (b, 0, 0))
    anyspec = pl.BlockSpec(memory_space=pl.ANY)
    grp = (A_GROUPS, CHUNK, CHUNK)
    return pl.pallas_call(
        body, name="layer", grid=(nb, nt),
        out_shape=(jax.ShapeDtypeStruct((t, D_MODEL), F32),
                   jax.ShapeDtypeStruct((nb, MEM_LEN, 2 * MEM_LEN), F32),
                   jax.ShapeDtypeStruct((IN_WIDTH, D_MODEL), F32),
                   jax.ShapeDtypeStruct((D_MODEL, D_MODEL), F32),
                   jax.ShapeDtypeStruct((1, D_MODEL), F32),
                   jax.ShapeDtypeStruct((1, D_MODEL), F32),
                   jax.ShapeDtypeStruct((8, CHUNK), F32),
                   jax.ShapeDtypeStruct(grp, F32),
                   jax.ShapeDtypeStruct((A_GROUPS, CHUNK), F32),
                   jax.ShapeDtypeStruct((1, A_WIDTH), F32),
                   jax.ShapeDtypeStruct((1, A_WIDTH), F32),
                   jax.ShapeDtypeStruct((8, CHUNK), F32),
                   jax.ShapeDtypeStruct((N_BUCKETS, CHUNK), F32)),
        in_specs=[tile, halo, tile_after, halo_after, tile, per_batch(MEM_LEN, 2 * MEM_LEN),
                  _full((2, 2 * CHUNK, 2 * CHUNK)),
                  pl.BlockSpec(memory_space=pltpu.SMEM),
                  _full((1, A_WIDTH)), _full((1, A_WIDTH)),
                  _full(grp), _full(grp), _full(grp),
                  _full((1, D_MODEL)), _full((1, D_MODEL)),
                  _full((IN_WIDTH, D_MODEL), single=True), _full((D_MODEL, D_MODEL), single=True),
                  _full((CHUNK, 2 * CHUNK))],
        out_specs=(tile, per_batch(MEM_LEN, 2 * MEM_LEN), anyspec, anyspec,
                   _full((1, D_MODEL)), _full((1, D_MODEL)), _full((8, CHUNK)),
                   _full(grp), _full((A_GROUPS, CHUNK)), _full((1, A_WIDTH)), _full((1, A_WIDTH)),
                   _full((8, CHUNK)), _full((N_BUCKETS, CHUNK))),
        scratch_shapes=[pltpu.VMEM((IN_WIDTH, D_MODEL), F32), pltpu.VMEM((D_MODEL, D_MODEL), F32),
                        pltpu.VMEM((tm, UV_W), F32), pltpu.VMEM((tm, Z_W), F32),
                        pltpu.VMEM((tm, 512), MM), pltpu.VMEM((tm + CHUNK, 2 * CHUNK), MM),
                        pltpu.VMEM((3, tm, D_MODEL), MM), pltpu.VMEM((CHUNK, D_MODEL), MM),
                        pltpu.VMEM((tm, IN_WIDTH), MM),
                        pltpu.VMEM((tm, D_MODEL), F32),
                        pltpu.VMEM((tm, D_MODEL), F32), pltpu.VMEM((2, tm, 1), F32),
                        pltpu.VMEM((tm, D_MODEL), F32), pltpu.VMEM((tm, D_MODEL), F32)]
                       + [pltpu.VMEM((tm, A_WIDTH), F32) for _ in range(4)]
                       + [pltpu.VMEM((tm, A_WIDTH), MM),
                          pltpu.VMEM((bpt * 2, 2 * CHUNK, 2 * CHUNK), F32),
                          pltpu.VMEM((bpt * 2, 2 * CHUNK, CHUNK), F32),
                          pltpu.VMEM((4, tm, MEM_LEN), F32),
                          pltpu.VMEM((bpt * 2, 2 * CHUNK, CHUNK), MM),
                          pltpu.VMEM((bpt * 2, 2 * CHUNK, CHUNK), MM),
                          pltpu.VMEM((tm + CHUNK, 2 * CHUNK), F32),
                          pltpu.VMEM((2, 2 * CHUNK, 2 * CHUNK), F32),
                          pltpu.VMEM(grp, F32),
                          pltpu.VMEM((8, CHUNK), F32),
                          pltpu.SemaphoreType.DMA((2,))],
        compiler_params=_params(dimension_semantics=("arbitrary", "arbitrary")),
    )(x2, x2, x2, x2, tgt2, mkv3, bias.reshape(2, 2 * CHUNK, 2 * CHUNK), sinks, vg, vb, wt, wtt, bcol, g1, g2, w_in_t, w_o, buckets)


class _ShardReduce:
    def __init__(self, pos, g, bufs, sems):
        self.x, self.y, self.c = pos
        self.g = g
        self.own, self.rcv, self.sbuf, self.rbuf, self.cbuf = bufs
        self.ld, self.sa, self.ra, self.sb, self.rb = sems
        self.nrow = g.shape[1]
        self.here = (self.x, self.y, self.c)
        self.sib = (self.x, self.y, 1 - self.c)
        self.first, self.second, self.far = _route(*pos)

    def _load(self, q):
        return pltpu.make_async_copy(self.g.at[2 * q + self.c], self.own.at[q], self.ld.at[q])

    def _to_sib(self, q, to):
        return _remote(self.g.at[2 * q + 1 - self.c], self.rcv.at[q], self.sa.at[q], self.ra.at[q], to)

    def _send(self, k, to):
        dst = self.cbuf.at[0] if k == 1 else self.rbuf.at[0 if k == 0 else 1]
        return _remote(self.sbuf.at[k], dst, self.sb.at[k], self.rb.at[k], to)

    def _stage(self, k, which, extra=None):
        def cast(r):
            v = self.rcv[which, r, :]
            if extra is not None:
                v = v + extra[0, r, :].astype(F32)
            self.sbuf[k, r, :] = v.astype(BF16)

        _rows_loop(self.nrow, cast)

    @staticmethod
    def _q(chip):
        return 2 * chip[0] + chip[1]

    def start(self):
        for q in range(4):
            self._load(q).start()
            self._to_sib(q, self.sib).start()

    def mid(self):
        for q in range(4):
            self._load(q).wait()
            self._to_sib(q, self.here).wait_recv()

        def add(r):
            for q in range(4):
                self.rcv[q, r, :] = self.rcv[q, r, :] + self.own[q, r, :]

        _rows_loop(self.nrow, add)
        to_first = (self.first[0], self.first[1], self.c)
        self._stage(0, self._q(self.first))
        self._send(0, to_first).start()
        self._stage(1, self._q(self.far))
        self._send(1, to_first).start()

    def pass_on(self):
        self._send(1, self.here).wait_recv()
        self._stage(2, self._q(self.second), extra=self.cbuf)
        self._send(2, (self.second[0], self.second[1], self.c)).start()

    def finish(self, out):
        self._send(0, self.here).wait_recv()
        self._send(2, self.here).wait_recv()
        which = 2 * self.x + self.y

        def tot(r):
            out[r, :] = (self.rcv[which, r, :] + self.rbuf[0, r, :].astype(F32)) + self.rbuf[1, r, :].astype(F32)

        _rows_loop(self.nrow, tot)
        for q in range(4):
            self._to_sib(q, self.sib).wait_send()
        to_first = (self.first[0], self.first[1], self.c)
        self._send(0, to_first).wait_send()
        self._send(1, to_first).wait_send()
        self._send(2, (self.second[0], self.second[1], self.c)).wait_send()


def _reduce_scratch(shape):
    return [pltpu.VMEM((4,) + shape, F32), pltpu.VMEM((4,) + shape, F32),
            pltpu.VMEM((3,) + shape, BF16), pltpu.VMEM((2,) + shape, BF16), pltpu.VMEM((1,) + shape, BF16),
            pltpu.SemaphoreType.DMA((4,)), pltpu.SemaphoreType.DMA((4,)), pltpu.SemaphoreType.DMA((4,)),
            pltpu.SemaphoreType.DMA((3,)), pltpu.SemaphoreType.DMA((3,))]


_N_RED = 10

_S_LAYOUT = (((1, D_MODEL), 0), ((1, D_MODEL), 8), ((1, D_MODEL), 16),
             ((1, A_WIDTH), 24), ((1, A_WIDTH), 28), ((A_GROUPS, CHUNK), 32),
             ((1, 4), 36), ((N_BUCKETS, 4), 40),
             ((A_GROUPS * CHUNK, CHUNK), 72))
_LOSS_ROW = 37
_W_SP_ROW = _S_LAYOUT[-1][1]
_S_ROWS = _W_SP_ROW + A_GROUPS * CHUNK
_N_SMALL = len(_S_LAYOUT)


def _pack_rows(dst, refs, tile=None):
    for (shp, r0), ref in zip(_S_LAYOUT, refs):
        if tuple(ref.shape) == (shp[1], shp[0]) and shp[0] != shp[1]:
            tile[...] = jnp.zeros_like(tile)
            tile[0:shp[1], 0:shp[0]] = ref[...]
            dst[r0:r0 + shp[0], 0:shp[1]] = tile[...].T[0:shp[0], 0:shp[1]]
        elif shp[0] == 1 and shp[1] >= CHUNK:
            for i in range(shp[1] // CHUNK):
                dst[r0 + i:r0 + i + 1, :] = ref[:, i * CHUNK:(i + 1) * CHUNK]
        elif ref.shape[-1] == CHUNK:
            dst[r0:r0 + shp[0], :] = ref[0:shp[0], :]
        else:
            dst[r0:r0 + shp[0], 0:shp[1]] = ref[...]


def _unpack_rows(src, refs):
    for (shp, r0), ref in zip(_S_LAYOUT, refs):
        if shp[0] == 1 and shp[1] >= CHUNK:
            for i in range(shp[1] // CHUNK):
                ref[:, i * CHUNK:(i + 1) * CHUNK] = src[r0 + i:r0 + i + 1, :]
        elif shp[1] == CHUNK:
            ref[...] = src[r0:r0 + shp[0], :]
        else:
            if tuple(ref.shape) == (shp[1], shp[0]):
                ref[...] = src[r0:r0 + CHUNK, :].T[0:shp[1], 0:shp[0]]
            else:
                ref[...] = src[r0:r0 + shp[0], 0:shp[1]]


_MEM_G = 2


def _greduce(ga, gb, dmkv, mem2, gm, w_mkv, small_g, loss_p):
    shp_c = (SHARD_O, 2 * MEM_LEN)
    shapes = (shp_c, gb.shape[1:], ga.shape[1:])
    rs = _S_ROWS

    def body(*refs):
        it = iter(refs)
        take = lambda n: [next(it) for _ in range(n)]
        gb_ref, ga_ref, d_ref, m_ref, gm_ref, wm_ref = take(6)
        sg_refs = take(_N_SMALL - 1)
        loss_ref, = take(1)
        oc, ob, oa, ogs = take(4)
        red = take(3 * _N_RED)
        gs_ref, rs_a, rs_b, rs_w, gc_ref, dgm_ref = take(6)
        ssem_a, rsem_a, ssem_b, rsem_b = take(4)

        pos = _position()
        x, y, cc = pos
        myq = 2 * x + y
        here, sib = (x, y, cc), (x, y, 1 - cc)
        chips = _other_chips(x, y)
        reducers = [_ShardReduce(pos, g, red[k * _N_RED:k * _N_RED + 5], red[k * _N_RED + 5:(k + 1) * _N_RED])
                    for k, g in enumerate((gc_ref, gb_ref, ga_ref))]
        for rd in reducers[1:]:
            rd.start()

        xf = m_ref[...]
        nm = xf * _rms(xf)
        hm = (nm * gm_ref[...]).astype(MM)
        d = d_ref[...].astype(MM)
        for o in range(N_DEV):
            gc_ref[o] = _dot_tn(hm[:, o * SHARD_O:(o + 1) * SHARD_O], d)
        dgm_ref[...] = jnp.sum(_dot_nt(d, wm_ref[...]) * nm, axis=0, keepdims=True)
        reducers[0].start()

        gs_ref[...] = jnp.zeros_like(gs_ref)
        _pack_rows(gs_ref, sg_refs[:_MEM_G] + [dgm_ref] + sg_refs[_MEM_G:])
        gs_ref[_LOSS_ROW:_LOSS_ROW + 1, :] = loss_ref[0:1, :]
        small_a = _remote(gs_ref, rs_a, ssem_a, rsem_a, sib)
        small_a.start()

        _remote(gs_ref, rs_a, ssem_a, rsem_a, here).wait_recv()
        rs_b[myq] = gs_ref[0:_W_SP_ROW, :] + rs_a[0:_W_SP_ROW, :]
        rs_w[myq] = (gs_ref[_W_SP_ROW:rs, :] + rs_a[_W_SP_ROW:rs, :]).astype(BF16)
        small_b = []
        for j, chip in enumerate(chips):
            to = (chip[0], chip[1], cc)
            small_b.append(_remote(rs_b.at[myq], rs_b.at[myq], ssem_b.at[0, j], rsem_b.at[0, j], to))
            small_b.append(_remote(rs_w.at[myq], rs_w.at[myq], ssem_b.at[1, j], rsem_b.at[1, j], to))
        for cp in small_b:
            cp.start()
        late_last = reducers[1:] + reducers[:1]
        for rd in late_last:
            rd.mid()
        for rd in late_last:
            rd.pass_on()

        for j in range(3):
            _remote(rs_b.at[myq], rs_b.at[myq], ssem_b.at[0, j], rsem_b.at[0, j], here).wait_recv()
            _remote(rs_w.at[myq], rs_w.at[myq], ssem_b.at[1, j], rsem_b.at[1, j], here).wait_recv()
        ogs[0:_W_SP_ROW, :] = ((rs_b[0] + rs_b[1]) + rs_b[2]) + rs_b[3]

        def tot_w(r):
            w = [rs_w[q, r, :].astype(F32) for q in range(4)]
            ogs[pl.ds(pl.multiple_of(_W_SP_ROW + r.start, 8), _ROWS), :] = ((w[0] + w[1]) + w[2]) + w[3]

        _rows_loop(rs - _W_SP_ROW, tot_w)
        for rd, out in zip(late_last, (ob, oa, oc)):
            rd.finish(out)
        small_a.wait_send()
        for cp in small_b:
            cp.wait_send()

    vm = pl.BlockSpec(memory_space=pltpu.VMEM)
    anyspec = pl.BlockSpec(memory_space=pl.ANY)
    scratch = []
    for shp in shapes:
        scratch += _reduce_scratch(shp)
    scratch += [pltpu.VMEM((rs, CHUNK), F32), pltpu.VMEM((rs, CHUNK), F32),
                pltpu.VMEM((4, _W_SP_ROW, CHUNK), F32), pltpu.VMEM((4, rs - _W_SP_ROW, CHUNK), BF16),
                pltpu.VMEM((N_DEV,) + shp_c, F32), pltpu.VMEM((1, D_MODEL), F32),
                pltpu.SemaphoreType.DMA, pltpu.SemaphoreType.DMA,
                pltpu.SemaphoreType.DMA((2, 3)), pltpu.SemaphoreType.DMA((2, 3))]
    tc, tb, ta, ts = pl.pallas_call(
        body, name="greduce",
        out_shape=tuple([jax.ShapeDtypeStruct(shp, F32) for shp in shapes] + [jax.ShapeDtypeStruct((rs, CHUNK), F32)]),
        in_specs=[anyspec] * 2 + [vm] * (4 + _N_SMALL),
        out_specs=(vm, vm, vm, vm),
        scratch_shapes=scratch,
        compiler_params=_params(),
    )(gb, ga, dmkv, mem2, gm, w_mkv, *small_g, loss_p)
    return ta, tb, tc, ts


def _adamw(w, g, m, v):
    m = ADAM_B1 * m + (1.0 - ADAM_B1) * g
    v = ADAM_B2 * v + (1.0 - ADAM_B2) * (g * g)
    m_hat = m / (1.0 - ADAM_B1 ** ADAM_STEP)
    v_hat = v / (1.0 - ADAM_B2 ** ADAM_STEP)
    delta = -ADAM_LR * (m_hat / (jnp.sqrt(v_hat) + ADAM_EPS) + ADAM_WD * w)
    return delta, m, v


def _update(ta, tb, tc, ts, big_wmv, small_wmv):
    shapes = (ta.shape, tb.shape, tc.shape)
    rs = _S_ROWS
    small_shapes = [tuple(a.shape) for a in small_wmv[0]]

    def body(*refs):
        it = iter(refs)
        take = lambda n: [next(it) for _ in range(n)]
        ga_ref, gb_ref, gc_ref, gs_ref = take(4)
        wa, ma, va, wb, mb, vb_, wc, mc, vc = take(9)
        sw_refs, sm_refs, sv_refs = take(_N_SMALL), take(_N_SMALL), take(_N_SMALL)
        oga, oda, oma, ova, ogb, odb, omb, ovb, ogc, odc, omc, ovc = take(12)
        so_refs = [take(_N_SMALL) for _ in range(4)]
        loss_out, = take(1)
        ws, ms, vs, ods, oms, ovs, turn = take(7)

        def update_rows(nrow, rows, g_r, w_r, m_r, v_r, og, od, om, ov):
            def upd(r):
                g = g_r[r, :]
                d, m, v = _adamw(w_r[r, :], g, m_r[r, :], v_r[r, :])
                og[r, :] = g
                od[r, :] = d
                om[r, :] = m
                ov[r, :] = v

            _rows_loop(nrow, upd, rows)

        update_rows(a_rows, 16, ga_ref, wa, ma, va, oga, oda, oma, ova)

        @pl.when(pl.program_id(0) == 0)
        def _():
            update_rows(shapes[1][0], _ROWS, gb_ref, wb, mb, vb_, ogb, odb, omb, ovb)
            update_rows(shapes[2][0], _ROWS, gc_ref, wc, mc, vc, ogc, odc, omc, ovc)
            for buf in (ws, ms, vs):
                buf[...] = jnp.zeros_like(buf)
            _pack_rows(ws, sw_refs, turn)
            _pack_rows(ms, sm_refs, turn)
            _pack_rows(vs, sv_refs, turn)

            def upd_s(i, _):
                r = pl.ds(pl.multiple_of(i * 8, 8), 8)
                d, m, v = _adamw(ws[r, :], gs_ref[r, :], ms[r, :], vs[r, :])
                ods[r, :] = d
                oms[r, :] = m
                ovs[r, :] = v
                return 0

            lax.fori_loop(0, rs // 8, upd_s, 0)
            for k, buf in enumerate((gs_ref, ods, oms, ovs)):
                _unpack_rows(buf, so_refs[k])
            loss_out[...] = gs_ref[_LOSS_ROW:_LOSS_ROW + 1, 0:1]

    n_blocks = 2
    a_rows = shapes[0][0] // n_blocks
    a_spec = pl.BlockSpec((a_rows, shapes[0][1]), lambda i: (i, 0))
    big_out, big_out_specs = [], []
    for shp in shapes:
        big_out += [jax.ShapeDtypeStruct(shp, F32)] * 4
        big_out_specs += [a_spec if shp == shapes[0] else _full(shp)] * 4
    small_out_shapes = [shp[::-1] if shp == (N_BUCKETS, 4) else shp for shp in small_shapes] * 4
    small_out = [jax.ShapeDtypeStruct(shp, F32) for shp in small_out_shapes]
    out_shape = tuple(big_out + small_out + [jax.ShapeDtypeStruct((1, 1), F32)])
    in_specs = ([a_spec, _full(shapes[1]), _full(shapes[2]), _full((rs, CHUNK))]
                + [a_spec] * 3 + [_full(shapes[1])] * 3 + [_full(shapes[2])] * 3
                + [_full(shp) for shp in small_shapes] * 3)
    return pl.pallas_call(
        body, name="update", grid=(n_blocks,),
        out_shape=out_shape,
        in_specs=in_specs,
        out_specs=tuple(big_out_specs + [_full(shp) for shp in small_out_shapes] + [_full((1, 1))]),
        scratch_shapes=[pltpu.VMEM((rs, CHUNK), F32) for _ in range(6)] + [pltpu.VMEM((CHUNK, CHUNK), F32)],
        compiler_params=_params(dimension_semantics=("arbitrary",)),
    )(ta, tb, tc, ts, *big_wmv, *small_wmv[0], *small_wmv[1], *small_wmv[2])


def kernel(x, mem, pre_norm_g, post_norm_g, mem_norm_g, w_in, w_mem_kv, v_norm_g, v_norm_b, w_spatial, b_spatial, attn_sinks, rel_bias, w_out, loss_target, m_pre_norm_g, m_post_norm_g, m_mem_norm_g, m_w_in, m_w_mem_kv, m_v_norm_g, m_v_norm_b, m_w_spatial, m_b_spatial, m_attn_sinks, m_rel_bias, m_w_out, v_pre_norm_g, v_post_norm_g, v_mem_norm_g, v_w_in, v_w_mem_kv, v_v_norm_g, v_v_norm_b, v_w_spatial, v_b_spatial, v_attn_sinks, v_rel_bias, v_w_out):
    sh_a = (w_in[0].T, m_w_in[0].T, v_w_in[0].T)
    sh_b = (w_out[0], m_w_out[0], v_w_out[0])
    sh_c = (w_mem_kv[0], m_w_mem_kv[0], v_w_mem_kv[0])
    nb, s, _ = x.shape
    t = nb * s
    x2 = x.reshape(t, D_MODEL)
    tgt2 = loss_target.reshape(t, D_MODEL)
    mem2 = mem.reshape(nb * MEM_LEN, D_MODEL)
    buckets = jnp.asarray(_t5_buckets())

    wa, wb, wc, bias, wt, wtt, bcol, mkv = _wgather(sh_a[0], sh_b[0], sh_c[0], rel_bias.T, w_spatial[0], b_spatial[0],
                                                    buckets, mem2, mem_norm_g)
    w_mkv = wc.reshape(D_MODEL, 2 * MEM_LEN)
    gx, dmkv, dwi, dwo, dg1, dg2, loss_p, dwsp, dbs, dvg, dvb, dsink, drel = _layer(
        x2, tgt2, mkv.reshape(nb, MEM_LEN, 2 * MEM_LEN), bias, attn_sinks.reshape(4), v_norm_g, v_norm_b, wt, wtt, bcol,
        pre_norm_g, post_norm_g, wa.reshape(IN_WIDTH, D_MODEL), wb.reshape(D_MODEL, D_MODEL), buckets,
        nb, s, min(256, s))
    gx = gx.reshape(nb, s, D_MODEL)
    small_grads = [dg1, dg2, dvg, dvb, dbs, dsink, drel, dwsp.reshape(A_GROUPS * CHUNK, CHUNK)]

    small_names = ["pre_norm_g", "post_norm_g", "mem_norm_g", "v_norm_g", "v_norm_b", "b_spatial", "attn_sinks",
                   "rel_bias", "w_spatial"]
    given = dict(pre_norm_g=(pre_norm_g, m_pre_norm_g, v_pre_norm_g), post_norm_g=(post_norm_g, m_post_norm_g, v_post_norm_g),
                 mem_norm_g=(mem_norm_g, m_mem_norm_g, v_mem_norm_g), v_norm_g=(v_norm_g, m_v_norm_g, v_v_norm_g),
                 v_norm_b=(v_norm_b, m_v_norm_b, v_v_norm_b), b_spatial=(b_spatial, m_b_spatial, v_b_spatial),
                 attn_sinks=(attn_sinks, m_attn_sinks, v_attn_sinks), rel_bias=(rel_bias, m_rel_bias, v_rel_bias),
                 w_spatial=(w_spatial, m_w_spatial, v_w_spatial))
    small_wmv = [[given[n][k].T if n == "rel_bias" else given[n][k].reshape(shp)
                  for n, (shp, _) in zip(small_names, _S_LAYOUT)] for k in range(3)]

    ta, tb, tc, ts = _greduce(dwi.reshape(N_DEV, SHARD_IN, D_MODEL), dwo.reshape(N_DEV, SHARD_O, D_MODEL),
                              dmkv.reshape(nb * MEM_LEN, 2 * MEM_LEN), mem2, mem_norm_g, w_mkv, small_grads, loss_p)
    outs = _update(ta, tb, tc, ts, (*sh_a, *sh_b, *sh_c), small_wmv)
    ra, rb, rc = outs[0:4], outs[4:8], outs[8:12]
    loss = outs[12 + 4 * _N_SMALL].reshape(())

    res = {}
    for k, kind in enumerate(("grad", "delta", "new_m", "new_v")):
        res[kind, "w_in"] = ra[k].T[None]
        res[kind, "w_out"] = rb[k][None]
        res[kind, "w_mem_kv"] = rc[k][None]
        for i, n in enumerate(small_names):
            o = outs[12 + k * _N_SMALL + i]
            res[kind, n] = o.T if n == "rel_bias" else o.reshape(given[n][0].shape)
    order = ["pre_norm_g", "post_norm_g", "mem_norm_g", "w_in", "w_mem_kv", "v_norm_g", "v_norm_b", "w_spatial",
             "b_spatial", "attn_sinks", "rel_bias", "w_out"]
    flat = [res[kind, n] for kind in ("grad", "delta", "new_m", "new_v") for n in order]
    return (loss, gx, *flat)
```

```python
import numpy as np
import jax
import jax.numpy as jnp
from jax import lax
from jax.experimental import pallas as pl
from jax.experimental.pallas import tpu as pltpu

F32 = jnp.float32
BF16 = jnp.bfloat16
MM = jnp.bfloat16

D_MODEL = 1024
CHUNK = 128
A_GROUPS = 4
A_WIDTH = 512
UV_W = 1024
QKV_W = 768
Z_W = 1024
IN_WIDTH = UV_W + QKV_W + Z_W
MEM_LEN = 256
N_BUCKETS = 32
MAX_DISTANCE = 128
EPS = 1e-6
NEG = -1e30
SCALE = 0.125
N_DEV = 8
SHARD_IN = IN_WIDTH // N_DEV
SHARD_O = D_MODEL // N_DEV

SQ_COL, SK_COL, SV_COL, MQ_COL, Z_COL = UV_W, UV_W + 256, UV_W + 384, UV_W + 512, UV_W + QKV_W
DW_PIECES = ((0, SQ_COL), (SQ_COL, Z_COL), (Z_COL, IN_WIDTH))
YB_OFF, YC_OFF = 512, 768

ADAM_LR = 0.001
ADAM_B1 = 0.9
ADAM_B2 = 0.999
ADAM_EPS = 1e-08
ADAM_WD = 0.01
ADAM_STEP = 10

VMEM_LIMIT = 60 * 1024 * 1024

_GELU_C = 0.7978845608028654
_GELU_A = 0.044715

MESH = pl.DeviceIdType.MESH
_ROWS = 32


def _dot(a, b):
    return lax.dot_general(a, b, (((1,), (0,)), ((), ())), preferred_element_type=F32)


def _dot_nt(a, b):
    return lax.dot_general(a, b, (((1,), (1,)), ((), ())), preferred_element_type=F32)


def _dot_tn(a, b):
    return lax.dot_general(a, b, (((0,), (0,)), ((), ())), preferred_element_type=F32)


def _gelu_and_grad(x):
    x2 = x * x
    t = jnp.tanh(x * (_GELU_C + (_GELU_C * _GELU_A) * x2))
    w = 0.5 * t + 0.5
    g = x * w
    dg = w * (1.0 + (x - g) * ((2.0 * _GELU_C) + (6.0 * _GELU_C * _GELU_A) * x2))
    return g, dg


def _t5_buckets():
    qi = np.arange(CHUNK)[:, None]
    kj = np.arange(2 * CHUNK)[None, :]
    n = np.maximum(qi + CHUNK - kj, 0)
    max_exact = N_BUCKETS // 2
    large = max_exact + (np.log(np.maximum(n, 1) / max_exact) / np.log(MAX_DISTANCE / max_exact)
                         * (N_BUCKETS - max_exact)).astype(np.int32)
    large = np.minimum(large, N_BUCKETS - 1)
    return np.where(n < max_exact, n, large).astype(np.int32)


def _params(**kw):
    return pltpu.CompilerParams(vmem_limit_bytes=VMEM_LIMIT, **kw)


def _full(shape, single=False):
    nd = len(shape)
    if single:
        return pl.BlockSpec(shape, lambda *_: (0,) * nd, pipeline_mode=pl.Buffered(1))
    return pl.BlockSpec(shape, lambda *_: (0,) * nd)


def _window_valid():
    qi = lax.broadcasted_iota(jnp.int32, (CHUNK, 2 * CHUNK), 0)
    kj = lax.broadcasted_iota(jnp.int32, (CHUNK, 2 * CHUNK), 1)
    dist = qi + CHUNK - kj
    return (dist >= 0) & (dist < CHUNK)


def _position():
    return lax.axis_index("x"), lax.axis_index("y"), lax.axis_index("c")


def _other_chips(x, y):
    return [(1 - x, y), (x, 1 - y), (1 - x, 1 - y)]


def _route(x, y, c):
    first = (x * c + (1 - x) * (1 - c), y * (1 - c) + (1 - y) * c)
    second = (x * (1 - c) + (1 - x) * c, y * c + (1 - y) * (1 - c))
    return first, second, (1 - x, 1 - y)


def _remote(src, dst, ssem, rsem, to):
    return pltpu.make_async_remote_copy(src_ref=src, dst_ref=dst, send_sem=ssem, recv_sem=rsem,
                                        device_id=to, device_id_type=MESH)


def _rows_loop(nrow, fn, rows=_ROWS):
    assert nrow % rows == 0

    def step(i, _):
        fn(pl.ds(pl.multiple_of(i * rows, rows), rows))
        return 0

    lax.fori_loop(0, nrow // rows, step, 0)


class _Gather:
    def __init__(self, pos, out, ssem, rsem):
        self.x, self.y, self.c = pos
        self.out, self.ssem, self.rsem = out, ssem, rsem
        self.me = 4 * self.x + 2 * self.y + self.c
        self.here = (self.x, self.y, self.c)
        self.sib = (self.x, self.y, 1 - self.c)
        self.first, self.second, self.far = _route(*pos)

    def _copy(self, k, blk, to):
        r = self.out.at[blk]
        return _remote(r, r, self.ssem.at[k], self.rsem.at[k], to)

    def _idx(self, chip, core):
        return 4 * chip[0] + 2 * chip[1] + core

    def _on(self, chip):
        return (chip[0], chip[1], self.c)

    def start(self):
        self._copy(0, self.me, self.sib).start()
        self._copy(1, self.me, self._on(self.first)).start()
        self._copy(2, self.me, self._on(self.second)).start()

    def forward(self):
        c = self.c
        self._copy(1, self._idx(self.first, c), self.here).wait_recv()
        self._copy(3, self._idx(self.first, c), self._on(self.second)).start()
        self._copy(4, self._idx(self.first, c), self.sib).start()
        self._copy(2, self._idx(self.second, c), self.here).wait_recv()
        self._copy(5, self._idx(self.second, c), self.sib).start()
        self._copy(3, self._idx(self.far, c), self.here).wait_recv()
        self._copy(6, self._idx(self.far, c), self.sib).start()

    def finish(self):
        c = self.c
        self._copy(0, self._idx((self.x, self.y), 1 - c), self.here).wait_recv()
        for k, chip in ((4, self.second), (5, self.first), (6, self.far)):
            self._copy(k, self._idx(chip, 1 - c), self.here).wait_recv()
        self._copy(0, self.me, self.sib).wait_send()
        self._copy(1, self.me, self._on(self.first)).wait_send()
        self._copy(2, self.me, self._on(self.second)).wait_send()
        self._copy(3, self._idx(self.first, c), self._on(self.second)).wait_send()
        for k, chip in ((4, self.first), (5, self.second), (6, self.far)):
            self._copy(k, self._idx(chip, c), self.sib).wait_send()


def _prep_tables(rb_ref, w_ref, b_ref, bk_ref, bias_ref, wt_ref, wtt_ref, bcol_ref):
    valid = _window_valid()
    bk = bk_ref[...]
    acc = [jnp.full((CHUNK, 2 * CHUNK), NEG, F32) for _ in range(4)]
    for b in range(N_BUCKETS):
        hit = (bk == b) & valid
        for h in range(4):
            acc[h] = jnp.where(hit, rb_ref[h, b], acc[h])
    for h in range(4):
        bias_ref[h] = acc[h]
    r = lax.broadcasted_iota(jnp.int32, (CHUNK, CHUNK), 0)
    c = lax.broadcasted_iota(jnp.int32, (CHUNK, CHUNK), 1)
    for g in range(A_GROUPS):
        w = jnp.where(r >= c, w_ref[g], 0.0)
        wt_ref[g] = w.astype(MM)
        wtt_ref[g] = w.T.astype(MM)
        bcol_ref[g] = jnp.broadcast_to(b_ref[g:g + 1, :], (CHUNK, CHUNK)).T


def _wgather(a, b, c, rel_bias, w_sp, b_sp, buckets, mem2, gm):
    tmem = mem2.shape[0]

    def body(a_ref, b_ref, c_ref, rb_ref, w_ref, bsp_ref, bk_ref, m_ref, gm_ref,
             oa, ob, oc, bias_ref, wt_ref, wtt_ref, bcol_ref, mkv_ref, ssem, rsem):
        pos = _position()
        me = 4 * pos[0] + 2 * pos[1] + pos[2]
        gathers = []
        for k, (src, out) in enumerate(((c_ref, oc), (b_ref, ob), (a_ref, oa))):
            out[me] = src[...].astype(BF16)
            g = _Gather(pos, out, ssem.at[k], rsem.at[k])
            g.start()
            gathers.append(g)
        _prep_tables(rb_ref, w_ref, bsp_ref, bk_ref, bias_ref, wt_ref, wtt_ref, bcol_ref)
        for g in gathers:
            g.forward()
        gathers[0].finish()
        xf = m_ref[...]
        hm = (xf * _rms(xf) * gm_ref[...]).astype(MM)
        acc = jnp.zeros((tmem, 2 * MEM_LEN), F32)
        for d in range(N_DEV):
            acc = acc + _dot(hm[:, d * SHARD_O:(d + 1) * SHARD_O], oc[d])
        mkv_ref[...] = acc.astype(MM)
        for g in gathers[1:]:
            g.finish()

    vm = pl.BlockSpec(memory_space=pltpu.VMEM)
    grp = (A_GROUPS, CHUNK, CHUNK)
    return pl.pallas_call(
        body, name="wgather",
        out_shape=(jax.ShapeDtypeStruct((N_DEV,) + a.shape, BF16),
                   jax.ShapeDtypeStruct((N_DEV,) + b.shape, BF16),
                   jax.ShapeDtypeStruct((N_DEV,) + c.shape, BF16),
                   jax.ShapeDtypeStruct((4, CHUNK, 2 * CHUNK), F32),
                   jax.ShapeDtypeStruct(grp, MM), jax.ShapeDtypeStruct(grp, MM), jax.ShapeDtypeStruct(grp, F32),
                   jax.ShapeDtypeStruct((tmem, 2 * MEM_LEN), MM)),
        in_specs=[vm, vm, vm, pl.BlockSpec(memory_space=pltpu.SMEM), vm, vm, vm, vm, vm],
        out_specs=tuple([vm] * 8),
        scratch_shapes=[pltpu.SemaphoreType.DMA((3, 7)), pltpu.SemaphoreType.DMA((3, 7))],
        compiler_params=_params(),
    )(a, b, c, rel_bias, w_sp, b_sp, buckets, mem2, gm)


def _half_masks(rows):
    lane = lax.broadcasted_iota(jnp.int32, (rows, CHUNK), 1)
    return lane < 64


def _dup_heads(band):
    b32 = band.astype(F32)
    rolled = pltpu.roll(b32, 64, 1)
    lo = _half_masks(band.shape[0])
    return (jnp.where(lo, b32, rolled).astype(MM), jnp.where(lo, rolled, b32).astype(MM))


def _swa_probs(qk, bias_h, sink_h, first_add):
    s = qk * SCALE + bias_h + first_add
    m = jnp.maximum(jnp.max(s, axis=-1, keepdims=True), sink_h)
    p = jnp.exp(s - m)
    es = jnp.exp(sink_h - m)
    inv = 1.0 / (jnp.sum(p, axis=-1, keepdims=True) + es)
    return p * inv, es * inv


def _softmax(s):
    m = jnp.max(s, axis=-1, keepdims=True)
    p = jnp.exp(s - m)
    return p * (1.0 / jnp.sum(p, axis=-1, keepdims=True))


def _first_block_mask(n):
    col = lax.broadcasted_iota(jnp.int32, (2 * CHUNK, 2 * CHUNK), 1)
    return jnp.where((col < CHUNK) & (n == 0), NEG, 0.0)


def _stack_heads(x128, lo):
    return jnp.concatenate([jnp.where(lo, x128, 0.0), jnp.where(lo, 0.0, x128)], axis=0).astype(MM)


def _rms(xf):
    return lax.rsqrt(jnp.mean(xf * xf, axis=-1, keepdims=True) + EPS)


def _layer(x2, tgt2, mkv3, bias, sinks, vg, vb, wt, wtt, bcol, g1, g2, w_in_t, w_o, buckets, nb, s, tm):
    nt = s // tm
    bpt = tm // CHUNK
    bps = s // CHUNK
    t = nb * s
    last_step = nb * nt - 1

    def tile_at(step):
        return (step // nt) * nt + nt - 1 - step % nt

    def block_before(step):
        return (step // nt) * bps + jnp.maximum((nt - 1 - step % nt) * bpt - 1, 0)

    def body(x_ref, xp_ref, xn_ref, xpn_ref, t_ref, mkv_ref, bias_ref, sink_ref, vg_ref, vb_ref,
             wt_ref, wtt_ref, bcol_ref, g1_ref, g2_ref, wi_ref, wo_ref, bk_ref,
             gx_ref, dmkv_ref, dwi_hbm, dwo_hbm, dg1_ref, dg2_ref, loss_ref, dwsp_ref, dbs_ref,
             dvg_ref, dvb_ref, dsink_ref, drel_ref,
             acc_i, acc_o, uv_s, z_s, q_s, kv_s, h_s, hp_s, dp_s, dxo_s, dh_s, r_s,
             ycat, dyc, u_s, gu_s, gv_s, xh_s, vc_s, pb_s, ps_s, pc_s, kd_s, vd_s,
             dkv_acc, dbias_acc, dsv_acc, dsink_acc, sems):
        b, j = pl.program_id(0), pl.program_id(1)
        jt = nt - 1 - j
        step = b * nt + j
        g1v = g1_ref[...]
        NOW, NEXT, DONE = 0, 1, 2
        dw_cols = list(DW_PIECES)

        def weight_grad(n, slot):
            for c0, c1 in dw_cols[:n]:
                acc_i[c0:c1, :] += _dot_tn(dp_s[:, c0:c1], h_s[slot])
            del dw_cols[:n]

        def pre_norm(x_tile, x_before):
            xf = x_tile[...]
            r_s[NEXT] = _rms(xf)
            h_s[NEXT] = (xf * r_s[NEXT] * g1v).astype(MM)
            xp = x_before[...]
            hp_s[...] = (xp * _rms(xp) * g1v).astype(MM)

        def project_z():
            z_s[...] = _dot_nt(h_s[NEXT], wi_ref[Z_COL:IN_WIDTH, :])

        def project_uv():
            uv_s[...] = _dot_nt(h_s[NEXT], wi_ref[0:UV_W, :])

        @pl.when(step == 0)
        def _():
            for ref in (acc_i, acc_o, dg1_ref, dg2_ref, loss_ref, dwsp_ref, dvg_ref, dvb_ref,
                        dbias_acc, dsv_acc, dsink_acc):
                ref[...] = jnp.zeros_like(ref)
            dp_s[...] = jnp.zeros_like(dp_s)
            h_s[NOW] = jnp.zeros((tm, D_MODEL), MM)
            pre_norm(x_ref, xp_ref)
            project_z()
            project_uv()

        h_s[DONE] = h_s[NOW]
        r_s[NOW] = r_s[NEXT]
        h = h_s[NEXT]
        h_s[NOW] = h
        hp = hp_s[...]

        @pl.when(j == 0)
        def _():
            dmkv_ref[...] = jnp.zeros_like(dmkv_ref)
            dkv_acc[...] = jnp.zeros_like(dkv_acc)

        carry = dkv_acc[0:CHUNK, :]
        dkv_acc[...] = jnp.zeros_like(dkv_acc)
        dkv_acc[tm:tm + CHUNK, :] = carry

        lo = _half_masks(CHUNK)
        lob = _half_masks(2 * CHUNK)
        lot = _half_masks(tm)

        qkv = _dot_nt(h, wi_ref[SQ_COL:Z_COL, :])
        q_s[:, 0:256] = qkv[:, 0:256].astype(MM)
        q_s[:, 256:512] = qkv[:, 512:768].astype(MM)
        kv_s[CHUNK:CHUNK + tm, :] = qkv[:, 256:512].astype(MM)
        kv_s[0:CHUNK, :] = _dot_nt(hp, wi_ref[SK_COL:MQ_COL, :]).astype(MM)

        weight_grad(1, DONE)
        b_qk, b_pb = [], []
        for blk in range(bpt):
            r0 = blk * CHUNK
            rows = slice(r0, r0 + CHUNK)
            for g in range(A_GROUPS):
                cg = slice(g * CHUNK, (g + 1) * CHUNK)
                u, gu = _gelu_and_grad(uv_s[rows, cg])
                v, gv = _gelu_and_grad(uv_s[rows, A_WIDTH + g * CHUNK:A_WIDTH + (g + 1) * CHUNK])
                mu = jnp.mean(v, axis=-1, keepdims=True)
                xc = v - mu
                rstd = lax.rsqrt(jnp.mean(xc * xc, axis=-1, keepdims=True) + EPS)
                xhat = xc * rstd
                vc = (xhat * vg_ref[:, cg] + vb_ref[:, cg]).astype(MM)
                sv = _dot(wt_ref[g], vc) + bcol_ref[g]
                u_s[rows, cg] = u
                gu_s[rows, cg] = sv * gu
                gv_s[rows, cg] = rstd * gv
                xh_s[rows, cg] = xhat
                vc_s[rows, cg] = vc
                ycat[rows, cg] = u * sv
            weight_grad(1, DONE)
            kd = _dup_heads(kv_s[r0:r0 + 2 * CHUNK, 0:CHUNK])
            vd = _dup_heads(kv_s[r0:r0 + 2 * CHUNK, CHUNK:2 * CHUNK])
            for kvh in range(2):
                kd_s[blk * 2 + kvh] = kd[kvh]
                vd_s[blk * 2 + kvh] = vd[kvh]
                q2 = _stack_heads(q_s[rows, kvh * CHUNK:(kvh + 1) * CHUNK].astype(F32), lo)
                b_qk.append(_dot_nt(q2, kd[kvh]))
        qks, pcs = [], []
        for g in range(2):
            q128 = q_s[:, 256 + g * CHUNK:256 + (g + 1) * CHUNK].astype(F32)
            for hh in range(2):
                qsel = jnp.where(lot if hh == 0 else ~lot, q128, 0.0).astype(MM)
                qks.append(_dot_nt(qsel, mkv_ref[:, g * CHUNK:(g + 1) * CHUNK]))
        top =lax.broadcasted_iota(jnp.int32, (2 * CHUNK, 1), 0) < CHUNK
        for blk in range(bpt):
            first_add = _first_block_mask(jt * bpt + blk)
            for kvh in range(2):
                sink2 = jnp.where(top, sink_ref[2 * kvh], sink_ref[2 * kvh + 1])
                probs, ps = _swa_probs(b_qk[blk * 2 + kvh], bias_ref[kvh], sink2, first_add)
                pb_s[blk * 2 + kvh] = probs
                ps_s[blk * 2 + kvh] = jnp.broadcast_to(ps, (2 * CHUNK, CHUNK))
                b_pb.append(probs.astype(MM))
        weight_grad(len(dw_cols), DONE)
        for hd in range(4):
            probs = _softmax(qks[hd] * SCALE)
            pc_s[hd] = probs
            pcs.append(probs.astype(MM))
        for blk in range(bpt):
            rows = slice(blk * CHUNK, (blk + 1) * CHUNK)
            for kvh in range(2):
                out2 = _dot(b_pb[blk * 2 + kvh], vd_s[blk * 2 + kvh])
                ycat[rows, YB_OFF + kvh * CHUNK:YB_OFF + (kvh + 1) * CHUNK] = jnp.where(
                    lo, out2[0:CHUNK], out2[CHUNK:2 * CHUNK])
        outs = [_dot(pcs[hd], mkv_ref[:, MEM_LEN + (hd // 2) * CHUNK:MEM_LEN + (hd // 2 + 1) * CHUNK])
                for hd in range(4)]
        for g in range(2):
            ycat[:, YC_OFF + g * CHUNK:YC_OFF + (g + 1) * CHUNK] = jnp.where(lot, outs[2 * g], outs[2 * g + 1])

        zt = z_s[...]
        sig = 1.0 / (1.0 + jnp.exp(-zt))
        silu = zt * sig
        yc = ycat[...]
        yb = (yc * silu).astype(MM)
        pre_norm(xn_ref, xpn_ref)
        o = _dot(yb, wo_ref[...])
        project_z()
        r2 = _rms(o)
        nrm = o * r2
        g2v = g2_ref[...]
        e = x_ref[...] + nrm * g2v - t_ref[...]
        l1 = jnp.sum(e * e, axis=-1, keepdims=True)
        loss_ref[...] += jnp.broadcast_to(jnp.sum(l1, axis=0, keepdims=True) * (0.5 / D_MODEL), loss_ref.shape)
        dxo = e * (1.0 / D_MODEL)
        dxo_s[...] = dxo
        dg2_ref[...] += jnp.sum(dxo * nrm, axis=0, keepdims=True)
        dn = dxo * g2v
        do = r2 * (dn - nrm * jnp.mean(dn * nrm, axis=-1, keepdims=True))
        dob = do.astype(MM)
        dy = _dot_nt(dob, wo_ref[...])
        dp_s[:, Z_COL:IN_WIDTH] = (dy * yc * (sig * (1.0 + zt * (1.0 - sig)))).astype(MM)
        dyc[...] = dy * silu
        acc_o[...] += _dot_tn(yb, dob)

        def in_proj_bwd(c0, c1):
            part = _dot(dp_s[:, c0:c1], wi_ref[c0:c1, :])
            if c0 == Z_COL:
                dh_s[...] = part
            else:
                dh_s[...] += part

        in_proj_bwd(Z_COL, IN_WIDTH)

        for blk in range(bpt):
            r0 = blk * CHUNK
            rows = slice(r0, r0 + CHUNK)
            for g in range(A_GROUPS):
                cg = slice(g * CHUNK, (g + 1) * CHUNK)
                cv = slice(A_WIDTH + g * CHUNK, A_WIDTH + (g + 1) * CHUNK)
                dya = dyc[rows, cg]
                dp_s[rows, cg] = (dya * gu_s[rows, cg]).astype(MM)
                dsv = dya * u_s[rows, cg]
                dsvb = dsv.astype(MM)
                dsv_acc[g] += dsv
                dwsp_ref[g] += _dot_nt(dsvb, vc_s[rows, cg])
                dvc = _dot(wtt_ref[g], dsvb)
                xhat = xh_s[rows, cg]
                dvg_ref[:, cg] += jnp.sum(dvc * xhat, axis=0, keepdims=True)
                dvb_ref[:, cg] += jnp.sum(dvc, axis=0, keepdims=True)
                dxh = dvc * vg_ref[:, cg]
                dv = (dxh - jnp.mean(dxh, axis=-1, keepdims=True)
                      - xhat * jnp.mean(dxh * xhat, axis=-1, keepdims=True))
                dp_s[rows, cv] = (dv * gv_s[rows, cg]).astype(MM)
        b_dosel, b_dp, b_dss = [], [], []
        for blk in range(bpt):
            rows = slice(blk * CHUNK, (blk + 1) * CHUNK)
            for kvh in range(2):
                b_dosel.append(_stack_heads(dyc[rows, YB_OFF + kvh * CHUNK:YB_OFF + (kvh + 1) * CHUNK], lo))
                b_dp.append(_dot_nt(b_dosel[-1], vd_s[blk * 2 + kvh]))
        dosels, dps, dsss = [], [], []
        for hd in range(4):
            do128 = dyc[:, YC_OFF + (hd // 2) * CHUNK:YC_OFF + (hd // 2 + 1) * CHUNK]
            dosels.append(jnp.where(lot if hd % 2 == 0 else ~lot, do128, 0.0).astype(MM))
            dps.append(_dot_nt(dosels[hd], mkv_ref[:, MEM_LEN + (hd // 2) * CHUNK:MEM_LEN + (hd // 2 + 1) * CHUNK]))
        in_proj_bwd(0, UV_W)
        for blk in range(bpt):
            for kvh in range(2):
                probs = pb_s[blk * 2 + kvh]
                dp = b_dp[blk * 2 + kvh]
                delta = jnp.sum(probs * dp, axis=-1, keepdims=True)
                ds = probs * (dp - delta)
                dbias_acc[kvh] += ds
                sd = ps_s[blk * 2 + kvh][:, 0:1] * delta
                for gi in range(2):
                    hd = 2 * kvh + gi
                    dsink_acc[hd:hd + 1, :] += jnp.broadcast_to(
                        -jnp.sum(sd[gi * CHUNK:(gi + 1) * CHUNK], axis=0, keepdims=True), (1, CHUNK))
                b_dss.append((ds * SCALE).astype(MM))
        for hd in range(4):
            probs = pc_s[hd]
            ds = probs * (dps[hd] - jnp.sum(probs * dps[hd], axis=-1, keepdims=True))
            dsss.append((ds * SCALE).astype(MM))
        for blk in range(bpt):
            r0 = blk * CHUNK
            rows = slice(r0, r0 + CHUNK)
            dk_f, dv_f = [], []
            for kvh in range(2):
                dss = b_dss[blk * 2 + kvh]
                q2 = _stack_heads(q_s[rows, kvh * CHUNK:(kvh + 1) * CHUNK].astype(F32), lo)
                dq2 = _dot(dss, kd_s[blk * 2 + kvh])
                dkd = _dot_tn(dss, q2)
                dvd = _dot_tn(pb_s[blk * 2 + kvh].astype(MM), b_dosel[blk * 2 + kvh])
                dp_s[rows, SQ_COL + kvh * CHUNK:SQ_COL + (kvh + 1) * CHUNK] = jnp.where(
                    lo, dq2[0:CHUNK], dq2[CHUNK:2 * CHUNK]).astype(MM)
                dk_f.append(dkd + pltpu.roll(dkd, 64, 1))
                dv_f.append(dvd + pltpu.roll(dvd, 64, 1))
            dkv_acc[r0:r0 + 2 * CHUNK, 0:CHUNK] += jnp.where(lob, dk_f[0], dk_f[1])
            dkv_acc[r0:r0 + 2 * CHUNK, CHUNK:2 * CHUNK] += jnp.where(lob, dv_f[0], dv_f[1])
        dp_s[:, SK_COL:MQ_COL] = dkv_acc[CHUNK:CHUNK + tm, :].astype(MM)
        for g in range(2):
            q128 = q_s[:, 256 + g * CHUNK:256 + (g + 1) * CHUNK].astype(F32)
            k128 = mkv_ref[:, g * CHUNK:(g + 1) * CHUNK]
            dq128 = jnp.zeros((tm, CHUNK), F32)
            dk128 = jnp.zeros((MEM_LEN, CHUNK), F32)
            dv128 = jnp.zeros((MEM_LEN, CHUNK), F32)
            for hh in range(2):
                hd = 2 * g + hh
                half = lot if hh == 0 else ~lot
                qsel = jnp.where(half, q128, 0.0).astype(MM)
                dq128 = dq128 + jnp.where(half, _dot(dsss[hd], k128), 0.0)
                dk128 = dk128 + _dot_tn(dsss[hd], qsel)
                dv128 = dv128 + _dot_tn(pc_s[hd].astype(MM), dosels[hd])
            dp_s[:, MQ_COL + g * CHUNK:MQ_COL + (g + 1) * CHUNK] = dq128.astype(MM)
            dmkv_ref[:, g * CHUNK:(g + 1) * CHUNK] += dk128
            dmkv_ref[:, MEM_LEN + g * CHUNK:MEM_LEN + (g + 1) * CHUNK] += dv128

        in_proj_bwd(SQ_COL, Z_COL)
        project_uv()
        dh = dh_s[...]
        r = r_s[NOW]
        nx = x_ref[...] * r
        dg1_ref[...] += jnp.sum(dh * nx, axis=0, keepdims=True)
        dnx = dh * g1v
        gx_ref[...] = dxo_s[...] + r * (dnx - nx * jnp.mean(dnx * nx, axis=-1, keepdims=True))

        @pl.when(step == last_step)
        def _():
            dw_cols.extend(DW_PIECES)
            weight_grad(len(dw_cols), NOW)
            out_i = pltpu.make_async_copy(acc_i, dwi_hbm, sems.at[0])
            out_o = pltpu.make_async_copy(acc_o, dwo_hbm, sems.at[1])
            out_i.start()
            out_o.start()
            r_ = lax.broadcasted_iota(jnp.int32, (CHUNK, CHUNK), 0)
            c_ = lax.broadcasted_iota(jnp.int32, (CHUNK, CHUNK), 1)
            for g in range(A_GROUPS):
                dwsp_ref[g] = jnp.where(r_ >= c_, dwsp_ref[g], 0.0)
                dbs_ref[g:g + 1, :] = jnp.sum(dsv_acc[g].T, axis=0, keepdims=True)
            rows8 = lax.broadcasted_iota(jnp.int32, (8, CHUNK), 0)
            cols8 = lax.broadcasted_iota(jnp.int32, (8, CHUNK), 1)
            sk = jnp.zeros((8, CHUNK), F32)
            for hd in range(4):
                sk = sk + jnp.where((rows8 == 0) & (cols8 == hd),
                                    jnp.broadcast_to(dsink_acc[hd:hd + 1, :], (8, CHUNK)), 0.0)
            dsink_ref[...] = sk
            bk = bk_ref[...]
            valid = _window_valid()
            rrow = lax.broadcasted_iota(jnp.int32, (N_BUCKETS, CHUNK), 0)
            rcol = lax.broadcasted_iota(jnp.int32, (N_BUCKETS, CHUNK), 1)
            acc = jnp.zeros((N_BUCKETS, CHUNK), F32)
            for bb in range(N_BUCKETS):
                hit = (bk == bb) & valid
                for hd in range(4):
                    dbias = dbias_acc[hd // 2, (hd % 2) * CHUNK:(hd % 2 + 1) * CHUNK, :]
                    part = jnp.sum(jnp.where(hit, dbias, 0.0), axis=-1, keepdims=True)
                    tot = jnp.sum(part, axis=0, keepdims=True)
                    acc = acc + jnp.where((rrow == bb) & (rcol == hd), jnp.broadcast_to(tot, (N_BUCKETS, CHUNK)), 0.0)
            drel_ref[...] = acc
            out_i.wait()
            out_o.wait()

    after = lambda b, j: jnp.minimum(b * nt + j + 1, last_step)
    tile = pl.BlockSpec((tm, D_MODEL), lambda b, j: (tile_at(b * nt + j), 0))
    tile_after = pl.BlockSpec((tm, D_MODEL), lambda b, j: (tile_at(after(b, j)), 0))
    halo = pl.BlockSpec((CHUNK, D_MODEL), lambda b, j: (block_before(b * nt + j), 0))
    halo_after = pl.BlockSpec((CHUNK, D_MODEL), lambda b, j: (block_before(after(b, j)), 0))
    per_batch = lambda r, w: pl.BlockSpec((None, r, w), lambda b, j: (b, 0, 0))
    anyspec = pl.BlockSpec(memory_space=pl.ANY)
    grp = (A_GROUPS, CHUNK, CHUNK)
    return pl.pallas_call(
        body, name="layer", grid=(nb, nt),
        out_shape=(jax.ShapeDtypeStruct((t, D_MODEL), F32),
                   jax.ShapeDtypeStruct((nb, MEM_LEN, 2 * MEM_LEN), F32),
                   jax.ShapeDtypeStruct((IN_WIDTH, D_MODEL), F32),
                   jax.ShapeDtypeStruct((D_MODEL, D_MODEL), F32),
                   jax.ShapeDtypeStruct((1, D_MODEL), F32),
                   jax.ShapeDtypeStruct((1, D_MODEL), F32),
                   jax.ShapeDtypeStruct((8, CHUNK), F32),
                   jax.ShapeDtypeStruct(grp, F32),
                   jax.ShapeDtypeStruct((A_GROUPS, CHUNK), F32),
                   jax.ShapeDtypeStruct((1, A_WIDTH), F32),
                   jax.ShapeDtypeStruct((1, A_WIDTH), F32),
                   jax.ShapeDtypeStruct((8, CHUNK), F32),
                   jax.ShapeDtypeStruct((N_BUCKETS, CHUNK), F32)),
        in_specs=[tile, halo, tile_after, halo_after, tile, per_batch(MEM_LEN, 2 * MEM_LEN),
                  _full((2, 2 * CHUNK, 2 * CHUNK)),
                  pl.BlockSpec(memory_space=pltpu.SMEM),
                  _full((1, A_WIDTH)), _full((1, A_WIDTH)),
                  _full(grp), _full(grp), _full(grp),
                  _full((1, D_MODEL)), _full((1, D_MODEL)),
                  _full((IN_WIDTH, D_MODEL), single=True), _full((D_MODEL, D_MODEL), single=True),
                  _full((CHUNK, 2 * CHUNK))],
        out_specs=(tile, per_batch(MEM_LEN, 2 * MEM_LEN), anyspec, anyspec,
                   _full((1, D_MODEL)), _full((1, D_MODEL)), _full((8, CHUNK)),
                   _full(grp), _full((A_GROUPS, CHUNK)), _full((1, A_WIDTH)), _full((1, A_WIDTH)),
                   _full((8, CHUNK)), _full((N_BUCKETS, CHUNK))),
        scratch_shapes=[pltpu.VMEM((IN_WIDTH, D_MODEL), F32), pltpu.VMEM((D_MODEL, D_MODEL), F32),
                        pltpu.VMEM((tm, UV_W), F32), pltpu.VMEM((tm, Z_W), F32),
                        pltpu.VMEM((tm, 512), MM), pltpu.VMEM((tm + CHUNK, 2 * CHUNK), MM),
                        pltpu.VMEM((3, tm, D_MODEL), MM), pltpu.VMEM((CHUNK, D_MODEL), MM),
                        pltpu.VMEM((tm, IN_WIDTH), MM),
                        pltpu.VMEM((tm, D_MODEL), F32),
                        pltpu.VMEM((tm, D_MODEL), F32), pltpu.VMEM((2, tm, 1), F32),
                        pltpu.VMEM((tm, D_MODEL), F32), pltpu.VMEM((tm, D_MODEL), F32)]
                       + [pltpu.VMEM((tm, A_WIDTH), F32) for _ in range(4)]
                       + [pltpu.VMEM((tm, A_WIDTH), MM),
                          pltpu.VMEM((bpt * 2, 2 * CHUNK, 2 * CHUNK), F32),
                          pltpu.VMEM((bpt * 2, 2 * CHUNK, CHUNK), F32),
                          pltpu.VMEM((4, tm, MEM_LEN), F32),
                          pltpu.VMEM((bpt * 2, 2 * CHUNK, CHUNK), MM),
                          pltpu.VMEM((bpt * 2, 2 * CHUNK, CHUNK), MM),
                          pltpu.VMEM((tm + CHUNK, 2 * CHUNK), F32),
                          pltpu.VMEM((2, 2 * CHUNK, 2 * CHUNK), F32),
                          pltpu.VMEM(grp, F32),
                          pltpu.VMEM((8, CHUNK), F32),
                          pltpu.SemaphoreType.DMA((2,))],
        compiler_params=_params(dimension_semantics=("arbitrary", "arbitrary")),
    )(x2, x2, x2, x2, tgt2, mkv3, bias.reshape(2, 2 * CHUNK, 2 * CHUNK), sinks, vg, vb, wt, wtt, bcol, g1, g2, w_in_t, w_o, buckets)


class _ShardReduce:
    def __init__(self, pos, g, bufs, sems):
        self.x, self.y, self.c = pos
        self.g = g
        self.own, self.rcv, self.sbuf, self.rbuf, self.cbuf = bufs
        self.ld, self.sa, self.ra, self.sb, self.rb = sems
        self.nrow = g.shape[1]
        self.here = (self.x, self.y, self.c)
        self.sib = (self.x, self.y, 1 - self.c)
        self.first, self.second, self.far = _route(*pos)

    def _load(self, q):
        return pltpu.make_async_copy(self.g.at[2 * q + self.c], self.own.at[q], self.ld.at[q])

    def _to_sib(self, q, to):
        return _remote(self.g.at[2 * q + 1 - self.c], self.rcv.at[q], self.sa.at[q], self.ra.at[q], to)

    def _send(self, k, to):
        dst = self.cbuf.at[0] if k == 1 else self.rbuf.at[0 if k == 0 else 1]
        return _remote(self.sbuf.at[k], dst, self.sb.at[k], self.rb.at[k], to)

    def _stage(self, k, which, extra=None):
        def cast(r):
            v = self.rcv[which, r, :]
            if extra is not None:
                v = v + extra[0, r, :].astype(F32)
            self.sbuf[k, r, :] = v.astype(BF16)

        _rows_loop(self.nrow, cast)

    @staticmethod
    def _q(chip):
        return 2 * chip[0] + chip[1]

    def start(self):
        for q in range(4):
            self._load(q).start()
            self._to_sib(q, self.sib).start()

    def mid(self):
        for q in range(4):
            self._load(q).wait()
            self._to_sib(q, self.here).wait_recv()

        def add(r):
            for q in range(4):
                self.rcv[q, r, :] = self.rcv[q, r, :] + self.own[q, r, :]

        _rows_loop(self.nrow, add)
        to_first = (self.first[0], self.first[1], self.c)
        self._stage(0, self._q(self.first))
        self._send(0, to_first).start()
        self._stage(1, self._q(self.far))
        self._send(1, to_first).start()

    def pass_on(self):
        self._send(1, self.here).wait_recv()
        self._stage(2, self._q(self.second), extra=self.cbuf)
        self._send(2, (self.second[0], self.second[1], self.c)).start()

    def finish(self, out):
        self._send(0, self.here).wait_recv()
        self._send(2, self.here).wait_recv()
        which = 2 * self.x + self.y

        def tot(r):
            out[r, :] = (self.rcv[which, r, :] + self.rbuf[0, r, :].astype(F32)) + self.rbuf[1, r, :].astype(F32)

        _rows_loop(self.nrow, tot)
        for q in range(4):
            self._to_sib(q, self.sib).wait_send()
        to_first = (self.first[0], self.first[1], self.c)
        self._send(0, to_first).wait_send()
        self._send(1, to_first).wait_send()
        self._send(2, (self.second[0], self.second[1], self.c)).wait_send()


def _reduce_scratch(shape):
    return [pltpu.VMEM((4,) + shape, F32), pltpu.VMEM((4,) + shape, F32),
            pltpu.VMEM((3,) + shape, BF16), pltpu.VMEM((2,) + shape, BF16), pltpu.VMEM((1,) + shape, BF16),
            pltpu.SemaphoreType.DMA((4,)), pltpu.SemaphoreType.DMA((4,)), pltpu.SemaphoreType.DMA((4,)),
            pltpu.SemaphoreType.DMA((3,)), pltpu.SemaphoreType.DMA((3,))]


_N_RED = 10

_S_LAYOUT = (((1, D_MODEL), 0), ((1, D_MODEL), 8), ((1, D_MODEL), 16),
             ((1, A_WIDTH), 24), ((1, A_WIDTH), 28), ((A_GROUPS, CHUNK), 32),
             ((1, 4), 36), ((N_BUCKETS, 4), 40),
             ((A_GROUPS * CHUNK, CHUNK), 72))
_LOSS_ROW = 37
_W_SP_ROW = _S_LAYOUT[-1][1]
_S_ROWS = _W_SP_ROW + A_GROUPS * CHUNK
_N_SMALL = len(_S_LAYOUT)


def _pack_rows(dst, refs, tile=None):
    for (shp, r0), ref in zip(_S_LAYOUT, refs):
        if tuple(ref.shape) == (shp[1], shp[0]) and shp[0] != shp[1]:
            tile[...] = jnp.zeros_like(tile)
            tile[0:shp[1], 0:shp[0]] = ref[...]
            dst[r0:r0 + shp[0], 0:shp[1]] = tile[...].T[0:shp[0], 0:shp[1]]
        elif shp[0] == 1 and shp[1] >= CHUNK:
            for i in range(shp[1] // CHUNK):
                dst[r0 + i:r0 + i + 1, :] = ref[:, i * CHUNK:(i + 1) * CHUNK]
        elif ref.shape[-1] == CHUNK:
            dst[r0:r0 + shp[0], :] = ref[0:shp[0], :]
        else:
            dst[r0:r0 + shp[0], 0:shp[1]] = ref[...]


def _unpack_rows(src, refs):
    for (shp, r0), ref in zip(_S_LAYOUT, refs):
        if shp[0] == 1 and shp[1] >= CHUNK:
            for i in range(shp[1] // CHUNK):
                ref[:, i * CHUNK:(i + 1) * CHUNK] = src[r0 + i:r0 + i + 1, :]
        elif shp[1] == CHUNK:
            ref[...] = src[r0:r0 + shp[0], :]
        else:
            if tuple(ref.shape) == (shp[1], shp[0]):
                ref[...] = src[r0:r0 + CHUNK, :].T[0:shp[1], 0:shp[0]]
            else:
                ref[...] = src[r0:r0 + shp[0], 0:shp[1]]


_MEM_G = 2


def _greduce(ga, gb, dmkv, mem2, gm, w_mkv, small_g, loss_p):
    shp_c = (SHARD_O, 2 * MEM_LEN)
    shapes = (shp_c, gb.shape[1:], ga.shape[1:])
    rs = _S_ROWS

    def body(*refs):
        it = iter(refs)
        take = lambda n: [next(it) for _ in range(n)]
        gb_ref, ga_ref, d_ref, m_ref, gm_ref, wm_ref = take(6)
        sg_refs = take(_N_SMALL - 1)
        loss_ref, = take(1)
        oc, ob, oa, ogs = take(4)
        red = take(3 * _N_RED)
        gs_ref, rs_a, rs_b, rs_w, gc_ref, dgm_ref = take(6)
        ssem_a, rsem_a, ssem_b, rsem_b = take(4)

        pos = _position()
        x, y, cc = pos
        myq = 2 * x + y
        here, sib = (x, y, cc), (x, y, 1 - cc)
        chips = _other_chips(x, y)
        reducers = [_ShardReduce(pos, g, red[k * _N_RED:k * _N_RED + 5], red[k * _N_RED + 5:(k + 1) * _N_RED])
                    for k, g in enumerate((gc_ref, gb_ref, ga_ref))]
        for rd in reducers[1:]:
            rd.start()

        xf = m_ref[...]
        nm = xf * _rms(xf)
        hm = (nm * gm_ref[...]).astype(MM)
        d = d_ref[...].astype(MM)
        for o in range(N_DEV):
            gc_ref[o] = _dot_tn(hm[:, o * SHARD_O:(o + 1) * SHARD_O], d)
        dgm_ref[...] = jnp.sum(_dot_nt(d, wm_ref[...]) * nm, axis=0, keepdims=True)
        reducers[0].start()

        gs_ref[...] = jnp.zeros_like(gs_ref)
        _pack_rows(gs_ref, sg_refs[:_MEM_G] + [dgm_ref] + sg_refs[_MEM_G:])
        gs_ref[_LOSS_ROW:_LOSS_ROW + 1, :] = loss_ref[0:1, :]
        small_a = _remote(gs_ref, rs_a, ssem_a, rsem_a, sib)
        small_a.start()

        _remote(gs_ref, rs_a, ssem_a, rsem_a, here).wait_recv()
        rs_b[myq] = gs_ref[0:_W_SP_ROW, :] + rs_a[0:_W_SP_ROW, :]
        rs_w[myq] = (gs_ref[_W_SP_ROW:rs, :] + rs_a[_W_SP_ROW:rs, :]).astype(BF16)
        small_b = []
        for j, chip in enumerate(chips):
            to = (chip[0], chip[1], cc)
            small_b.append(_remote(rs_b.at[myq], rs_b.at[myq], ssem_b.at[0, j], rsem_b.at[0, j], to))
            small_b.append(_remote(rs_w.at[myq], rs_w.at[myq], ssem_b.at[1, j], rsem_b.at[1, j], to))
        for cp in small_b:
            cp.start()
        late_last = reducers[1:] + reducers[:1]
        for rd in late_last:
            rd.mid()
        for rd in late_last:
            rd.pass_on()

        for j in range(3):
            _remote(rs_b.at[myq], rs_b.at[myq], ssem_b.at[0, j], rsem_b.at[0, j], here).wait_recv()
            _remote(rs_w.at[myq], rs_w.at[myq], ssem_b.at[1, j], rsem_b.at[1, j], here).wait_recv()
        ogs[0:_W_SP_ROW, :] = ((rs_b[0] + rs_b[1]) + rs_b[2]) + rs_b[3]

        def tot_w(r):
            w = [rs_w[q, r, :].astype(F32) for q in range(4)]
            ogs[pl.ds(pl.multiple_of(_W_SP_ROW + r.start, 8), _ROWS), :] = ((w[0] + w[1]) + w[2]) + w[3]

        _rows_loop(rs - _W_SP_ROW, tot_w)
        for rd, out in zip(late_last, (ob, oa, oc)):
            rd.finish(out)
        small_a.wait_send()
        for cp in small_b:
            cp.wait_send()

    vm = pl.BlockSpec(memory_space=pltpu.VMEM)
    anyspec = pl.BlockSpec(memory_space=pl.ANY)
    scratch = []
    for shp in shapes:
        scratch += _reduce_scratch(shp)
    scratch += [pltpu.VMEM((rs, CHUNK), F32), pltpu.VMEM((rs, CHUNK), F32),
                pltpu.VMEM((4, _W_SP_ROW, CHUNK), F32), pltpu.VMEM((4, rs - _W_SP_ROW, CHUNK), BF16),
                pltpu.VMEM((N_DEV,) + shp_c, F32), pltpu.VMEM((1, D_MODEL), F32),
                pltpu.SemaphoreType.DMA, pltpu.SemaphoreType.DMA,
                pltpu.SemaphoreType.DMA((2, 3)), pltpu.SemaphoreType.DMA((2, 3))]
    tc, tb, ta, ts = pl.pallas_call(
        body, name="greduce",
        out_shape=tuple([jax.ShapeDtypeStruct(shp, F32) for shp in shapes] + [jax.ShapeDtypeStruct((rs, CHUNK), F32)]),
        in_specs=[anyspec] * 2 + [vm] * (4 + _N_SMALL),
        out_specs=(vm, vm, vm, vm),
        scratch_shapes=scratch,
        compiler_params=_params(),
    )(gb, ga, dmkv, mem2, gm, w_mkv, *small_g, loss_p)
    return ta, tb, tc, ts


def _adamw(w, g, m, v):
    m = ADAM_B1 * m + (1.0 - ADAM_B1) * g
    v = ADAM_B2 * v + (1.0 - ADAM_B2) * (g * g)
    m_hat = m / (1.0 - ADAM_B1 ** ADAM_STEP)
    v_hat = v / (1.0 - ADAM_B2 ** ADAM_STEP)
    delta = -ADAM_LR * (m_hat / (jnp.sqrt(v_hat) + ADAM_EPS) + ADAM_WD * w)
    return delta, m, v


def _update(ta, tb, tc, ts, big_wmv, small_wmv):
    shapes = (ta.shape, tb.shape, tc.shape)
    rs = _S_ROWS
    small_shapes = [tuple(a.shape) for a in small_wmv[0]]

    def body(*refs):
        it = iter(refs)
        take = lambda n: [next(it) for _ in range(n)]
        ga_ref, gb_ref, gc_ref, gs_ref = take(4)
        wa, ma, va, wb, mb, vb_, wc, mc, vc = take(9)
        sw_refs, sm_refs, sv_refs = take(_N_SMALL), take(_N_SMALL), take(_N_SMALL)
        oga, oda, oma, ova, ogb, odb, omb, ovb, ogc, odc, omc, ovc = take(12)
        so_refs = [take(_N_SMALL) for _ in range(4)]
        loss_out, = take(1)
        ws, ms, vs, ods, oms, ovs, turn = take(7)
        in_a, in_b, in_c, out_a, out_b, out_c, isem, osem = take(8)

        ins = ((ga_ref, wa, ma, va), (gb_ref, wb, mb, vb_), (gc_ref, wc, mc, vc))
        outs = ((oga, oda, oma, ova), (ogb, odb, omb, ovb), (ogc, odc, omc, ovc))
        in_buf, out_buf = (in_a, in_b, in_c), (out_a, out_b, out_c)

        def load(p, k):
            arr, r0, n = pieces[p]
            return pltpu.make_async_copy(ins[arr][k].at[pl.ds(r0, n)], in_buf[arr].at[k, pl.ds(r0, n)], isem.at[p, k])

        def store(p, k):
            arr, r0, n = pieces[p]
            src = in_buf[arr].at[0, pl.ds(r0, n)] if k == 0 else out_buf[arr].at[k - 1, pl.ds(r0, n)]
            return pltpu.make_async_copy(src, outs[arr][k].at[pl.ds(r0, n)], osem.at[p, k])

        for p in range(len(pieces)):
            for k in range(4):
                load(p, k).start()

        for buf in (ws, ms, vs):
            buf[...] = jnp.zeros_like(buf)
        _pack_rows(ws, sw_refs, turn)
        _pack_rows(ms, sm_refs, turn)
        _pack_rows(vs, sv_refs, turn)

        def upd_s(i, _):
            r = pl.ds(pl.multiple_of(i * 8, 8), 8)
            d, m, v = _adamw(ws[r, :], gs_ref[r, :], ms[r, :], vs[r, :])
            ods[r, :] = d
            oms[r, :] = m
            ovs[r, :] = v
            return 0

        lax.fori_loop(0, rs // 8, upd_s, 0)
        for k, buf in enumerate((gs_ref, ods, oms, ovs)):
            _unpack_rows(buf, so_refs[k])
        loss_out[...] = gs_ref[_LOSS_ROW:_LOSS_ROW + 1, 0:1]

        for p, (arr, r0, n) in enumerate(pieces):
            for k in range(4):
                load(p, k).wait()
            g_v, w_v, m_v, v_v = [in_buf[arr].at[k, pl.ds(r0, n)] for k in range(4)]
            d_o, m_o, v_o = [out_buf[arr].at[k, pl.ds(r0, n)] for k in range(3)]

            def upd(r, g_v=g_v, w_v=w_v, m_v=m_v, v_v=v_v, d_o=d_o, m_o=m_o, v_o=v_o):
                d, m, v = _adamw(w_v[r, :], g_v[r, :], m_v[r, :], v_v[r, :])
                d_o[r, :] = d
                m_o[r, :] = m
                v_o[r, :] = v

            _rows_loop(n, upd, 16)
            for k in range(4):
                store(p, k).start()
        for p in range(len(pieces)):
            for k in range(4):
                store(p, k).wait()

    half = shapes[0][0] // 2
    pieces = ((2, 0, shapes[2][0]), (1, 0, shapes[1][0]), (0, 0, half), (0, half, half))
    vm = pl.BlockSpec(memory_space=pltpu.VMEM)
    anyspec = pl.BlockSpec(memory_space=pl.ANY)
    big_out = []
    for shp in shapes:
        big_out += [jax.ShapeDtypeStruct(shp, F32)] * 4
    small_out_shapes = [shp[::-1] if shp == (N_BUCKETS, 4) else shp for shp in small_shapes] * 4
    small_out = [jax.ShapeDtypeStruct(shp, F32) for shp in small_out_shapes]
    out_shape = tuple(big_out + small_out + [jax.ShapeDtypeStruct((1, 1), F32)])
    in_specs = [anyspec] * 3 + [vm] + [anyspec] * 9 + [vm] * (3 * _N_SMALL)
    return pl.pallas_call(
        body, name="update",
        out_shape=out_shape,
        in_specs=in_specs,
        out_specs=tuple([anyspec] * 12 + [vm] * (len(small_out) + 1)),
        scratch_shapes=([pltpu.VMEM((rs, CHUNK), F32) for _ in range(6)] + [pltpu.VMEM((CHUNK, CHUNK), F32)]
                        + [pltpu.VMEM((4,) + shp, F32) for shp in shapes]
                        + [pltpu.VMEM((3,) + shp, F32) for shp in shapes]
                        + [pltpu.SemaphoreType.DMA((len(pieces), 4)), pltpu.SemaphoreType.DMA((len(pieces), 4))]),
        compiler_params=_params(),
    )(ta, tb, tc, ts, *big_wmv, *small_wmv[0], *small_wmv[1], *small_wmv[2])


def kernel(x, mem, pre_norm_g, post_norm_g, mem_norm_g, w_in, w_mem_kv, v_norm_g, v_norm_b, w_spatial, b_spatial, attn_sinks, rel_bias, w_out, loss_target, m_pre_norm_g, m_post_norm_g, m_mem_norm_g, m_w_in, m_w_mem_kv, m_v_norm_g, m_v_norm_b, m_w_spatial, m_b_spatial, m_attn_sinks, m_rel_bias, m_w_out, v_pre_norm_g, v_post_norm_g, v_mem_norm_g, v_w_in, v_w_mem_kv, v_v_norm_g, v_v_norm_b, v_w_spatial, v_b_spatial, v_attn_sinks, v_rel_bias, v_w_out):
    sh_a = (w_in[0].T, m_w_in[0].T, v_w_in[0].T)
    sh_b = (w_out[0], m_w_out[0], v_w_out[0])
    sh_c = (w_mem_kv[0], m_w_mem_kv[0], v_w_mem_kv[0])
    nb, s, _ = x.shape
    t = nb * s
    x2 = x.reshape(t, D_MODEL)
    tgt2 = loss_target.reshape(t, D_MODEL)
    mem2 = mem.reshape(nb * MEM_LEN, D_MODEL)
    buckets = jnp.asarray(_t5_buckets())

    wa, wb, wc, bias, wt, wtt, bcol, mkv = _wgather(sh_a[0], sh_b[0], sh_c[0], rel_bias.T, w_spatial[0], b_spatial[0],
                                                    buckets, mem2, mem_norm_g)
    w_mkv = wc.reshape(D_MODEL, 2 * MEM_LEN)
    gx, dmkv, dwi, dwo, dg1, dg2, loss_p, dwsp, dbs, dvg, dvb, dsink, drel = _layer(
        x2, tgt2, mkv.reshape(nb, MEM_LEN, 2 * MEM_LEN), bias, attn_sinks.reshape(4), v_norm_g, v_norm_b, wt, wtt, bcol,
        pre_norm_g, post_norm_g, wa.reshape(IN_WIDTH, D_MODEL), wb.reshape(D_MODEL, D_MODEL), buckets,
        nb, s, min(256, s))
    gx = gx.reshape(nb, s, D_MODEL)
    small_grads = [dg1, dg2, dvg, dvb, dbs, dsink, drel, dwsp.reshape(A_GROUPS * CHUNK, CHUNK)]

    small_names = ["pre_norm_g", "post_norm_g", "mem_norm_g", "v_norm_g", "v_norm_b", "b_spatial", "attn_sinks",
                   "rel_bias", "w_spatial"]
    given = dict(pre_norm_g=(pre_norm_g, m_pre_norm_g, v_pre_norm_g), post_norm_g=(post_norm_g, m_post_norm_g, v_post_norm_g),
                 mem_norm_g=(mem_norm_g, m_mem_norm_g, v_mem_norm_g), v_norm_g=(v_norm_g, m_v_norm_g, v_v_norm_g),
                 v_norm_b=(v_norm_b, m_v_norm_b, v_v_norm_b), b_spatial=(b_spatial, m_b_spatial, v_b_spatial),
                 attn_sinks=(attn_sinks, m_attn_sinks, v_attn_sinks), rel_bias=(rel_bias, m_rel_bias, v_rel_bias),
                 w_spatial=(w_spatial, m_w_spatial, v_w_spatial))
    small_wmv = [[given[n][k].T if n == "rel_bias" else given[n][k].reshape(shp)
                  for n, (shp, _) in zip(small_names, _S_LAYOUT)] for k in range(3)]

    ta, tb, tc, ts = _greduce(dwi.reshape(N_DEV, SHARD_IN, D_MODEL), dwo.reshape(N_DEV, SHARD_O, D_MODEL),
                              dmkv.reshape(nb * MEM_LEN, 2 * MEM_LEN), mem2, mem_norm_g, w_mkv, small_grads, loss_p)
    outs = _update(ta, tb, tc, ts, (*sh_a, *sh_b, *sh_c), small_wmv)
    ra, rb, rc = outs[0:4], outs[4:8], outs[8:12]
    loss = outs[12 + 4 * _N_SMALL].reshape(())

    res = {}
    for k, kind in enumerate(("grad", "delta", "new_m", "new_v")):
        res[kind, "w_in"] = ra[k].T[None]
        res[kind, "w_out"] = rb[k][None]
        res[kind, "w_mem_kv"] = rc[k][None]
        for i, n in enumerate(small_names):
            o = outs[12 + k * _N_SMALL + i]
            res[kind, n] = o.T if n == "rel_bias" else o.reshape(given[n][0].shape)
    order = ["pre_norm_g", "post_norm_g", "mem_norm_g", "w_in", "w_mem_kv", "v_norm_g", "v_norm_b", "w_spatial",
             "b_spatial", "attn_sinks", "rel_bias", "w_out"]
    flat = [res[kind, n] for kind in ("grad", "delta", "new_m", "new_v") for n in order]
    return (loss, gx, *flat)
```

```python
import numpy as np
import jax
import jax.numpy as jnp
from jax import lax
from jax.experimental import pallas as pl
from jax.experimental.pallas import tpu as pltpu

F32 = jnp.float32
BF16 = jnp.bfloat16
MM = jnp.bfloat16

D_MODEL = 1024
CHUNK = 128
A_GROUPS = 4
A_WIDTH = 512
UV_W = 1024
QKV_W = 768
Z_W = 1024
IN_WIDTH = UV_W + QKV_W + Z_W
MEM_LEN = 256
N_BUCKETS = 32
MAX_DISTANCE = 128
EPS = 1e-6
NEG = -1e30
SCALE = 0.125
N_DEV = 8
SHARD_IN = IN_WIDTH // N_DEV
SHARD_O = D_MODEL // N_DEV

SQ_COL, SK_COL, SV_COL, MQ_COL, Z_COL = UV_W, UV_W + 256, UV_W + 384, UV_W + 512, UV_W + QKV_W
DW_PIECES = ((0, SQ_COL), (SQ_COL, Z_COL), (Z_COL, IN_WIDTH))
YB_OFF, YC_OFF = 512, 768

ADAM_LR = 0.001
ADAM_B1 = 0.9
ADAM_B2 = 0.999
ADAM_EPS = 1e-08
ADAM_WD = 0.01
ADAM_STEP = 10

VMEM_LIMIT = 60 * 1024 * 1024

_GELU_C = 0.7978845608028654
_GELU_A = 0.044715

MESH = pl.DeviceIdType.MESH
_ROWS = 32


def _dot(a, b):
    return lax.dot_general(a, b, (((1,), (0,)), ((), ())), preferred_element_type=F32)


def _dot_nt(a, b):
    return lax.dot_general(a, b, (((1,), (1,)), ((), ())), preferred_element_type=F32)


def _dot_tn(a, b):
    return lax.dot_general(a, b, (((0,), (0,)), ((), ())), preferred_element_type=F32)


def _gelu_and_grad(x):
    x2 = x * x
    t = jnp.tanh(x * (_GELU_C + (_GELU_C * _GELU_A) * x2))
    w = 0.5 * t + 0.5
    g = x * w
    dg = w * (1.0 + (x - g) * ((2.0 * _GELU_C) + (6.0 * _GELU_C * _GELU_A) * x2))
    return g, dg


def _t5_buckets():
    qi = np.arange(CHUNK)[:, None]
    kj = np.arange(2 * CHUNK)[None, :]
    n = np.maximum(qi + CHUNK - kj, 0)
    max_exact = N_BUCKETS // 2
    large = max_exact + (np.log(np.maximum(n, 1) / max_exact) / np.log(MAX_DISTANCE / max_exact)
                         * (N_BUCKETS - max_exact)).astype(np.int32)
    large = np.minimum(large, N_BUCKETS - 1)
    return np.where(n < max_exact, n, large).astype(np.int32)


def _params(**kw):
    return pltpu.CompilerParams(vmem_limit_bytes=VMEM_LIMIT, **kw)


def _full(shape, single=False):
    nd = len(shape)
    if single:
        return pl.BlockSpec(shape, lambda *_: (0,) * nd, pipeline_mode=pl.Buffered(1))
    return pl.BlockSpec(shape, lambda *_: (0,) * nd)


def _window_valid():
    qi = lax.broadcasted_iota(jnp.int32, (CHUNK, 2 * CHUNK), 0)
    kj = lax.broadcasted_iota(jnp.int32, (CHUNK, 2 * CHUNK), 1)
    dist = qi + CHUNK - kj
    return (dist >= 0) & (dist < CHUNK)


def _position():
    return lax.axis_index("x"), lax.axis_index("y"), lax.axis_index("c")


def _other_chips(x, y):
    return [(1 - x, y), (x, 1 - y), (1 - x, 1 - y)]


def _route(x, y, c):
    first = (x * c + (1 - x) * (1 - c), y * (1 - c) + (1 - y) * c)
    second = (x * (1 - c) + (1 - x) * c, y * c + (1 - y) * (1 - c))
    return first, second, (1 - x, 1 - y)


def _remote(src, dst, ssem, rsem, to):
    return pltpu.make_async_remote_copy(src_ref=src, dst_ref=dst, send_sem=ssem, recv_sem=rsem,
                                        device_id=to, device_id_type=MESH)


def _rows_loop(nrow, fn, rows=_ROWS):
    assert nrow % rows == 0

    def step(i, _):
        fn(pl.ds(pl.multiple_of(i * rows, rows), rows))
        return 0

    lax.fori_loop(0, nrow // rows, step, 0)


class _Gather:
    def __init__(self, pos, out, ssem, rsem):
        self.x, self.y, self.c = pos
        self.out, self.ssem, self.rsem = out, ssem, rsem
        self.me = 4 * self.x + 2 * self.y + self.c
        self.here = (self.x, self.y, self.c)
        self.sib = (self.x, self.y, 1 - self.c)
        self.first, self.second, self.far = _route(*pos)

    def _copy(self, k, blk, to):
        r = self.out.at[blk]
        return _remote(r, r, self.ssem.at[k], self.rsem.at[k], to)

    def _idx(self, chip, core):
        return 4 * chip[0] + 2 * chip[1] + core

    def _on(self, chip):
        return (chip[0], chip[1], self.c)

    def start(self):
        self._copy(0, self.me, self.sib).start()
        self._copy(1, self.me, self._on(self.first)).start()
        self._copy(2, self.me, self._on(self.second)).start()

    def forward(self):
        c = self.c
        self._copy(1, self._idx(self.first, c), self.here).wait_recv()
        self._copy(3, self._idx(self.first, c), self._on(self.second)).start()
        self._copy(4, self._idx(self.first, c), self.sib).start()
        self._copy(2, self._idx(self.second, c), self.here).wait_recv()
        self._copy(5, self._idx(self.second, c), self.sib).start()
        self._copy(3, self._idx(self.far, c), self.here).wait_recv()
        self._copy(6, self._idx(self.far, c), self.sib).start()

    def finish(self):
        c = self.c
        self._copy(0, self._idx((self.x, self.y), 1 - c), self.here).wait_recv()
        for k, chip in ((4, self.second), (5, self.first), (6, self.far)):
            self._copy(k, self._idx(chip, 1 - c), self.here).wait_recv()
        self._copy(0, self.me, self.sib).wait_send()
        self._copy(1, self.me, self._on(self.first)).wait_send()
        self._copy(2, self.me, self._on(self.second)).wait_send()
        self._copy(3, self._idx(self.first, c), self._on(self.second)).wait_send()
        for k, chip in ((4, self.first), (5, self.second), (6, self.far)):
            self._copy(k, self._idx(chip, c), self.sib).wait_send()


def _prep_tables(rb_ref, w_ref, b_ref, bk_ref, bias_ref, wt_ref, wtt_ref, bcol_ref):
    valid = _window_valid()
    bk = bk_ref[...]
    acc = [jnp.full((CHUNK, 2 * CHUNK), NEG, F32) for _ in range(4)]
    for b in range(N_BUCKETS):
        hit = (bk == b) & valid
        for h in range(4):
            acc[h] = jnp.where(hit, rb_ref[h, b], acc[h])
    for h in range(4):
        bias_ref[h] = acc[h]
    r = lax.broadcasted_iota(jnp.int32, (CHUNK, CHUNK), 0)
    c = lax.broadcasted_iota(jnp.int32, (CHUNK, CHUNK), 1)
    for g in range(A_GROUPS):
        w = jnp.where(r >= c, w_ref[g], 0.0)
        wt_ref[g] = w.astype(MM)
        wtt_ref[g] = w.T.astype(MM)
        bcol_ref[g] = jnp.broadcast_to(b_ref[g:g + 1, :], (CHUNK, CHUNK)).T


def _wgather(a, b, c, rel_bias, w_sp, b_sp, buckets, mem2, gm):
    tmem = mem2.shape[0]

    def body(a_ref, b_ref, c_ref, rb_ref, w_ref, bsp_ref, bk_ref, m_ref, gm_ref,
             oa, ob, oc, bias_ref, wt_ref, wtt_ref, bcol_ref, mkv_ref, ssem, rsem):
        pos = _position()
        me = 4 * pos[0] + 2 * pos[1] + pos[2]
        gathers = []
        for k, (src, out) in enumerate(((c_ref, oc), (b_ref, ob), (a_ref, oa))):
            out[me] = src[...].astype(BF16)
            g = _Gather(pos, out, ssem.at[k], rsem.at[k])
            g.start()
            gathers.append(g)
        _prep_tables(rb_ref, w_ref, bsp_ref, bk_ref, bias_ref, wt_ref, wtt_ref, bcol_ref)
        for g in gathers:
            g.forward()
        gathers[0].finish()
        xf = m_ref[...]
        hm = (xf * _rms(xf) * gm_ref[...]).astype(MM)
        acc = jnp.zeros((tmem, 2 * MEM_LEN), F32)
        for d in range(N_DEV):
            acc = acc + _dot(hm[:, d * SHARD_O:(d + 1) * SHARD_O], oc[d])
        mkv_ref[...] = acc.astype(MM)
        for g in gathers[1:]:
            g.finish()

    vm = pl.BlockSpec(memory_space=pltpu.VMEM)
    grp = (A_GROUPS, CHUNK, CHUNK)
    return pl.pallas_call(
        body, name="wgather",
        out_shape=(jax.ShapeDtypeStruct((N_DEV,) + a.shape, BF16),
                   jax.ShapeDtypeStruct((N_DEV,) + b.shape, BF16),
                   jax.ShapeDtypeStruct((N_DEV,) + c.shape, BF16),
                   jax.ShapeDtypeStruct((4, CHUNK, 2 * CHUNK), F32),
                   jax.ShapeDtypeStruct(grp, MM), jax.ShapeDtypeStruct(grp, MM), jax.ShapeDtypeStruct(grp, F32),
                   jax.ShapeDtypeStruct((tmem, 2 * MEM_LEN), MM)),
        in_specs=[vm, vm, vm, pl.BlockSpec(memory_space=pltpu.SMEM), vm, vm, vm, vm, vm],
        out_specs=tuple([vm] * 8),
        scratch_shapes=[pltpu.SemaphoreType.DMA((3, 7)), pltpu.SemaphoreType.DMA((3, 7))],
        compiler_params=_params(),
    )(a, b, c, rel_bias, w_sp, b_sp, buckets, mem2, gm)


def _half_masks(rows):
    lane = lax.broadcasted_iota(jnp.int32, (rows, CHUNK), 1)
    return lane < 64


def _dup_heads(band):
    b32 = band.astype(F32)
    rolled = pltpu.roll(b32, 64, 1)
    lo = _half_masks(band.shape[0])
    return (jnp.where(lo, b32, rolled).astype(MM), jnp.where(lo, rolled, b32).astype(MM))


def _swa_probs(qk, bias_h, sink_h, first_add):
    s = qk * SCALE + bias_h + first_add
    m = jnp.maximum(jnp.max(s, axis=-1, keepdims=True), sink_h)
    p = jnp.exp(s - m)
    es = jnp.exp(sink_h - m)
    inv = 1.0 / (jnp.sum(p, axis=-1, keepdims=True) + es)
    return p * inv, es * inv


def _softmax(s):
    m = jnp.max(s, axis=-1, keepdims=True)
    p = jnp.exp(s - m)
    return p * (1.0 / jnp.sum(p, axis=-1, keepdims=True))


def _first_block_mask(n):
    col = lax.broadcasted_iota(jnp.int32, (2 * CHUNK, 2 * CHUNK), 1)
    return jnp.where((col < CHUNK) & (n == 0), NEG, 0.0)


def _stack_heads(x128, lo):
    return jnp.concatenate([jnp.where(lo, x128, 0.0), jnp.where(lo, 0.0, x128)], axis=0).astype(MM)


def _rms(xf):
    return lax.rsqrt(jnp.mean(xf * xf, axis=-1, keepdims=True) + EPS)


def _layer(x2, tgt2, mkv3, bias, sinks, vg, vb, wt, wtt, bcol, g1, g2, w_in_t, w_o, buckets, nb, s, tm):
    nt = s // tm
    bpt = tm // CHUNK
    bps = s // CHUNK
    t = nb * s
    last_step = nb * nt - 1

    def tile_at(step):
        return (step // nt) * nt + nt - 1 - step % nt

    def block_before(step):
        return (step // nt) * bps + jnp.maximum((nt - 1 - step % nt) * bpt - 1, 0)

    def body(x_ref, xp_ref, xn_ref, xpn_ref, t_ref, mkv_ref, bias_ref, sink_ref, vg_ref, vb_ref,
             wt_ref, wtt_ref, bcol_ref, g1_ref, g2_ref, wi_ref, wo_ref, bk_ref,
             gx_ref, dmkv_ref, dwi_hbm, dwo_hbm, dg1_ref, dg2_ref, loss_ref, dwsp_ref, dbs_ref,
             dvg_ref, dvb_ref, dsink_ref, drel_ref,
             acc_i, acc_o, uv_s, z_s, q_s, kv_s, h_s, hp_s, dp_s, dxo_s, dh_s, r_s,
             ycat, dyc, u_s, gu_s, gv_s, xh_s, vc_s, pb_s, ps_s, pc_s, kd_s, vd_s,
             dkv_acc, dbias_acc, dsv_acc, dsink_acc, sems):
        b, j = pl.program_id(0), pl.program_id(1)
        jt = nt - 1 - j
        step = b * nt + j
        g1v = g1_ref[...]
        NOW, NEXT, DONE = 0, 1, 2
        dw_cols = list(DW_PIECES)

        def weight_grad(n, slot):
            for c0, c1 in dw_cols[:n]:
                acc_i[c0:c1, :] += _dot_tn(dp_s[:, c0:c1], h_s[slot])
            del dw_cols[:n]

        def pre_norm(x_tile, x_before):
            xf = x_tile[...]
            r_s[NEXT] = _rms(xf)
            h_s[NEXT] = (xf * r_s[NEXT] * g1v).astype(MM)
            xp = x_before[...]
            hp_s[...] = (xp * _rms(xp) * g1v).astype(MM)

        def project_z():
            z_s[...] = _dot_nt(h_s[NEXT], wi_ref[Z_COL:IN_WIDTH, :])

        def project_uv():
            uv_s[...] = _dot_nt(h_s[NEXT], wi_ref[0:UV_W, :])

        @pl.when(step == 0)
        def _():
            for ref in (acc_i, acc_o, dg1_ref, dg2_ref, loss_ref, dwsp_ref, dvg_ref, dvb_ref,
                        dbias_acc, dsv_acc, dsink_acc):
                ref[...] = jnp.zeros_like(ref)
            dp_s[...] = jnp.zeros_like(dp_s)
            h_s[NOW] = jnp.zeros((tm, D_MODEL), MM)
            pre_norm(x_ref, xp_ref)
            project_z()
            project_uv()

        h_s[DONE] = h_s[NOW]
        r_s[NOW] = r_s[NEXT]
        h = h_s[NEXT]
        h_s[NOW] = h
        hp = hp_s[...]

        @pl.when(j == 0)
        def _():
            dmkv_ref[...] = jnp.zeros_like(dmkv_ref)
            dkv_acc[...] = jnp.zeros_like(dkv_acc)

        carry = dkv_acc[0:CHUNK, :]
        dkv_acc[...] = jnp.zeros_like(dkv_acc)
        dkv_acc[tm:tm + CHUNK, :] = carry

        lo = _half_masks(CHUNK)
        lob = _half_masks(2 * CHUNK)
        lot = _half_masks(tm)

        qkv = _dot_nt(h, wi_ref[SQ_COL:Z_COL, :])
        q_s[:, 0:256] = qkv[:, 0:256].astype(MM)
        q_s[:, 256:512] = qkv[:, 512:768].astype(MM)
        kv_s[CHUNK:CHUNK + tm, :] = qkv[:, 256:512].astype(MM)
        kv_s[0:CHUNK, :] = _dot_nt(hp, wi_ref[SK_COL:MQ_COL, :]).astype(MM)

        weight_grad(1, DONE)
        b_qk, b_pb = [], []
        for blk in range(bpt):
            r0 = blk * CHUNK
            rows = slice(r0, r0 + CHUNK)
            for g in range(A_GROUPS):
                cg = slice(g * CHUNK, (g + 1) * CHUNK)
                u, gu = _gelu_and_grad(uv_s[rows, cg])
                v, gv = _gelu_and_grad(uv_s[rows, A_WIDTH + g * CHUNK:A_WIDTH + (g + 1) * CHUNK])
                mu = jnp.mean(v, axis=-1, keepdims=True)
                xc = v - mu
                rstd = lax.rsqrt(jnp.mean(xc * xc, axis=-1, keepdims=True) + EPS)
                xhat = xc * rstd
                vc = (xhat * vg_ref[:, cg] + vb_ref[:, cg]).astype(MM)
                sv = _dot(wt_ref[g], vc) + bcol_ref[g]
                u_s[rows, cg] = u
                gu_s[rows, cg] = sv * gu
                gv_s[rows, cg] = rstd * gv
                xh_s[rows, cg] = xhat
                vc_s[rows, cg] = vc
                ycat[rows, cg] = u * sv
            weight_grad(1, DONE)
            kd = _dup_heads(kv_s[r0:r0 + 2 * CHUNK, 0:CHUNK])
            vd = _dup_heads(kv_s[r0:r0 + 2 * CHUNK, CHUNK:2 * CHUNK])
            for kvh in range(2):
                kd_s[blk * 2 + kvh] = kd[kvh]
                vd_s[blk * 2 + kvh] = vd[kvh]
                q2 = _stack_heads(q_s[rows, kvh * CHUNK:(kvh + 1) * CHUNK].astype(F32), lo)
                b_qk.append(_dot_nt(q2, kd[kvh]))
        qks, pcs = [], []
        for g in range(2):
            q128 = q_s[:, 256 + g * CHUNK:256 + (g + 1) * CHUNK].astype(F32)
            for hh in range(2):
                qsel = jnp.where(lot if hh == 0 else ~lot, q128, 0.0).astype(MM)
                qks.append(_dot_nt(qsel, mkv_ref[:, g * CHUNK:(g + 1) * CHUNK]))
        top =lax.broadcasted_iota(jnp.int32, (2 * CHUNK, 1), 0) < CHUNK
        for blk in range(bpt):
            first_add = _first_block_mask(jt * bpt + blk)
            for kvh in range(2):
                sink2 = jnp.where(top, sink_ref[2 * kvh], sink_ref[2 * kvh + 1])
                probs, ps = _swa_probs(b_qk[blk * 2 + kvh], bias_ref[kvh], sink2, first_add)
                pb_s[blk * 2 + kvh] = probs
                ps_s[blk * 2 + kvh] = jnp.broadcast_to(ps, (2 * CHUNK, CHUNK))
                b_pb.append(probs.astype(MM))
        weight_grad(len(dw_cols), DONE)
        for hd in range(4):
            probs = _softmax(qks[hd] * SCALE)
            pc_s[hd] = probs
            pcs.append(probs.astype(MM))
        for blk in range(bpt):
            rows = slice(blk * CHUNK, (blk + 1) * CHUNK)
            for kvh in range(2):
                out2 = _dot(b_pb[blk * 2 + kvh], vd_s[blk * 2 + kvh])
                ycat[rows, YB_OFF + kvh * CHUNK:YB_OFF + (kvh + 1) * CHUNK] = jnp.where(
                    lo, out2[0:CHUNK], out2[CHUNK:2 * CHUNK])
        outs = [_dot(pcs[hd], mkv_ref[:, MEM_LEN + (hd // 2) * CHUNK:MEM_LEN + (hd // 2 + 1) * CHUNK])
                for hd in range(4)]
        for g in range(2):
            ycat[:, YC_OFF + g * CHUNK:YC_OFF + (g + 1) * CHUNK] = jnp.where(lot, outs[2 * g], outs[2 * g + 1])

        zt = z_s[...]
        sig = 1.0 / (1.0 + jnp.exp(-zt))
        silu = zt * sig
        yc = ycat[...]
        yb = (yc * silu).astype(MM)
        pre_norm(xn_ref, xpn_ref)
        o = _dot(yb, wo_ref[...])
        project_z()
        r2 = _rms(o)
        nrm = o * r2
        g2v = g2_ref[...]
        e = x_ref[...] + nrm * g2v - t_ref[...]
        l1 = jnp.sum(e * e, axis=-1, keepdims=True)
        loss_ref[...] += jnp.broadcast_to(jnp.sum(l1, axis=0, keepdims=True) * (0.5 / D_MODEL), loss_ref.shape)
        dxo = e * (1.0 / D_MODEL)
        dxo_s[...] = dxo
        dg2_ref[...] += jnp.sum(dxo * nrm, axis=0, keepdims=True)
        dn = dxo * g2v
        do = r2 * (dn - nrm * jnp.mean(dn * nrm, axis=-1, keepdims=True))
        dob = do.astype(MM)
        dy = _dot_nt(dob, wo_ref[...])
        dp_s[:, Z_COL:IN_WIDTH] = (dy * yc * (sig * (1.0 + zt * (1.0 - sig)))).astype(MM)
        dyc[...] = dy * silu
        acc_o[...] += _dot_tn(yb, dob)

        def in_proj_bwd(c0, c1):
            part = _dot(dp_s[:, c0:c1], wi_ref[c0:c1, :])
            if c0 == Z_COL:
                dh_s[...] = part
            else:
                dh_s[...] += part

        in_proj_bwd(Z_COL, IN_WIDTH)

        for blk in range(bpt):
            r0 = blk * CHUNK
            rows = slice(r0, r0 + CHUNK)
            for g in range(A_GROUPS):
                cg = slice(g * CHUNK, (g + 1) * CHUNK)
                cv = slice(A_WIDTH + g * CHUNK, A_WIDTH + (g + 1) * CHUNK)
                dya = dyc[rows, cg]
                dp_s[rows, cg] = (dya * gu_s[rows, cg]).astype(MM)
                dsv = dya * u_s[rows, cg]
                dsvb = dsv.astype(MM)
                dsv_acc[g] += dsv
                dwsp_ref[g] += _dot_nt(dsvb, vc_s[rows, cg])
                dvc = _dot(wtt_ref[g], dsvb)
                xhat = xh_s[rows, cg]
                dvg_ref[:, cg] += jnp.sum(dvc * xhat, axis=0, keepdims=True)
                dvb_ref[:, cg] += jnp.sum(dvc, axis=0, keepdims=True)
                dxh = dvc * vg_ref[:, cg]
                dv = (dxh - jnp.mean(dxh, axis=-1, keepdims=True)
                      - xhat * jnp.mean(dxh * xhat, axis=-1, keepdims=True))
                dp_s[rows, cv] = (dv * gv_s[rows, cg]).astype(MM)
        b_dosel, b_dp, b_dss = [], [], []
        for blk in range(bpt):
            rows = slice(blk * CHUNK, (blk + 1) * CHUNK)
            for kvh in range(2):
                b_dosel.append(_stack_heads(dyc[rows, YB_OFF + kvh * CHUNK:YB_OFF + (kvh + 1) * CHUNK], lo))
                b_dp.append(_dot_nt(b_dosel[-1], vd_s[blk * 2 + kvh]))
        dosels, dps, dsss = [], [], []
        for hd in range(4):
            do128 = dyc[:, YC_OFF + (hd // 2) * CHUNK:YC_OFF + (hd // 2 + 1) * CHUNK]
            dosels.append(jnp.where(lot if hd % 2 == 0 else ~lot, do128, 0.0).astype(MM))
            dps.append(_dot_nt(dosels[hd], mkv_ref[:, MEM_LEN + (hd // 2) * CHUNK:MEM_LEN + (hd // 2 + 1) * CHUNK]))
        in_proj_bwd(0, UV_W)
        for blk in range(bpt):
            for kvh in range(2):
                probs = pb_s[blk * 2 + kvh]
                dp = b_dp[blk * 2 + kvh]
                delta = jnp.sum(probs * dp, axis=-1, keepdims=True)
                ds = probs * (dp - delta)
                dbias_acc[kvh] += ds
                sd = ps_s[blk * 2 + kvh][:, 0:1] * delta
                for gi in range(2):
                    hd = 2 * kvh + gi
                    dsink_acc[hd:hd + 1, :] += jnp.broadcast_to(
                        -jnp.sum(sd[gi * CHUNK:(gi + 1) * CHUNK], axis=0, keepdims=True), (1, CHUNK))
                b_dss.append((ds * SCALE).astype(MM))
        for hd in range(4):
            probs = pc_s[hd]
            ds = probs * (dps[hd] - jnp.sum(probs * dps[hd], axis=-1, keepdims=True))
            dsss.append((ds * SCALE).astype(MM))
        for blk in range(bpt):
            r0 = blk * CHUNK
            rows = slice(r0, r0 + CHUNK)
            dk_f, dv_f = [], []
            for kvh in range(2):
                dss = b_dss[blk * 2 + kvh]
                q2 = _stack_heads(q_s[rows, kvh * CHUNK:(kvh + 1) * CHUNK].astype(F32), lo)
                dq2 = _dot(dss, kd_s[blk * 2 + kvh])
                dkd = _dot_tn(dss, q2)
                dvd = _dot_tn(pb_s[blk * 2 + kvh].astype(MM), b_dosel[blk * 2 + kvh])
                dp_s[rows, SQ_COL + kvh * CHUNK:SQ_COL + (kvh + 1) * CHUNK] = jnp.where(
                    lo, dq2[0:CHUNK], dq2[CHUNK:2 * CHUNK]).astype(MM)
                dk_f.append(dkd + pltpu.roll(dkd, 64, 1))
                dv_f.append(dvd + pltpu.roll(dvd, 64, 1))
            dkv_acc[r0:r0 + 2 * CHUNK, 0:CHUNK] += jnp.where(lob, dk_f[0], dk_f[1])
            dkv_acc[r0:r0 + 2 * CHUNK, CHUNK:2 * CHUNK] += jnp.where(lob, dv_f[0], dv_f[1])
        dp_s[:, SK_COL:MQ_COL] = dkv_acc[CHUNK:CHUNK + tm, :].astype(MM)
        for g in range(2):
            q128 = q_s[:, 256 + g * CHUNK:256 + (g + 1) * CHUNK].astype(F32)
            k128 = mkv_ref[:, g * CHUNK:(g + 1) * CHUNK]
            dq128 = jnp.zeros((tm, CHUNK), F32)
            dk128 = jnp.zeros((MEM_LEN, CHUNK), F32)
            dv128 = jnp.zeros((MEM_LEN, CHUNK), F32)
            for hh in range(2):
                hd = 2 * g + hh
                half = lot if hh == 0 else ~lot
                qsel = jnp.where(half, q128, 0.0).astype(MM)
                dq128 = dq128 + jnp.where(half, _dot(dsss[hd], k128), 0.0)
                dk128 = dk128 + _dot_tn(dsss[hd], qsel)
                dv128 = dv128 + _dot_tn(pc_s[hd].astype(MM), dosels[hd])
            dp_s[:, MQ_COL + g * CHUNK:MQ_COL + (g + 1) * CHUNK] = dq128.astype(MM)
            dmkv_ref[:, g * CHUNK:(g + 1) * CHUNK] += dk128
            dmkv_ref[:, MEM_LEN + g * CHUNK:MEM_LEN + (g + 1) * CHUNK] += dv128

        in_proj_bwd(SQ_COL, Z_COL)
        project_uv()
        dh = dh_s[...]
        r = r_s[NOW]
        nx = x_ref[...] * r
        dg1_ref[...] += jnp.sum(dh * nx, axis=0, keepdims=True)
        dnx = dh * g1v
        gx_ref[...] = dxo_s[...] + r * (dnx - nx * jnp.mean(dnx * nx, axis=-1, keepdims=True))

        @pl.when(step == last_step)
        def _():
            dw_cols.extend(DW_PIECES)
            weight_grad(len(dw_cols), NOW)
            out_i = pltpu.make_async_copy(acc_i, dwi_hbm, sems.at[0])
            out_o = pltpu.make_async_copy(acc_o, dwo_hbm, sems.at[1])
            out_i.start()
            out_o.start()
            r_ = lax.broadcasted_iota(jnp.int32, (CHUNK, CHUNK), 0)
            c_ = lax.broadcasted_iota(jnp.int32, (CHUNK, CHUNK), 1)
            for g in range(A_GROUPS):
                dwsp_ref[g] = jnp.where(r_ >= c_, dwsp_ref[g], 0.0)
                dbs_ref[g:g + 1, :] = jnp.sum(dsv_acc[g].T, axis=0, keepdims=True)
            rows8 = lax.broadcasted_iota(jnp.int32, (8, CHUNK), 0)
            cols8 = lax.broadcasted_iota(jnp.int32, (8, CHUNK), 1)
            sk = jnp.zeros((8, CHUNK), F32)
            for hd in range(4):
                sk = sk + jnp.where((rows8 == 0) & (cols8 == hd),
                                    jnp.broadcast_to(dsink_acc[hd:hd + 1, :], (8, CHUNK)), 0.0)
            dsink_ref[...] = sk
            bk = bk_ref[...]
            valid = _window_valid()
            rrow = lax.broadcasted_iota(jnp.int32, (N_BUCKETS, CHUNK), 0)
            rcol = lax.broadcasted_iota(jnp.int32, (N_BUCKETS, CHUNK), 1)
            acc = jnp.zeros((N_BUCKETS, CHUNK), F32)
            for bb in range(N_BUCKETS):
                hit = (bk == bb) & valid
                for hd in range(4):
                    dbias = dbias_acc[hd // 2, (hd % 2) * CHUNK:(hd % 2 + 1) * CHUNK, :]
                    part = jnp.sum(jnp.where(hit, dbias, 0.0), axis=-1, keepdims=True)
                    tot = jnp.sum(part, axis=0, keepdims=True)
                    acc = acc + jnp.where((rrow == bb) & (rcol == hd), jnp.broadcast_to(tot, (N_BUCKETS, CHUNK)), 0.0)
            drel_ref[...] = acc
            out_i.wait()
            out_o.wait()

    after = lambda b, j: jnp.minimum(b * nt + j + 1, last_step)
    tile = pl.BlockSpec((tm, D_MODEL), lambda b, j: (tile_at(b * nt + j), 0))
    tile_after = pl.BlockSpec((tm, D_MODEL), lambda b, j: (tile_at(after(b, j)), 0))
    halo = pl.BlockSpec((CHUNK, D_MODEL), lambda b, j: (block_before(b * nt + j), 0))
    halo_after = pl.BlockSpec((CHUNK, D_MODEL), lambda b, j: (block_before(after(b, j)), 0))
    per_batch = lambda r, w: pl.BlockSpec((None, r, w), lambda b, j: (b, 0, 0))
    anyspec = pl.BlockSpec(memory_space=pl.ANY)
    grp = (A_GROUPS, CHUNK, CHUNK)
    return pl.pallas_call(
        body, name="layer", grid=(nb, nt),
        out_shape=(jax.ShapeDtypeStruct((t, D_MODEL), F32),
                   jax.ShapeDtypeStruct((nb, MEM_LEN, 2 * MEM_LEN), F32),
                   jax.ShapeDtypeStruct((IN_WIDTH, D_MODEL), F32),
                   jax.ShapeDtypeStruct((D_MODEL, D_MODEL), F32),
                   jax.ShapeDtypeStruct((1, D_MODEL), F32),
                   jax.ShapeDtypeStruct((1, D_MODEL), F32),
                   jax.ShapeDtypeStruct((8, CHUNK), F32),
                   jax.ShapeDtypeStruct(grp, F32),
                   jax.ShapeDtypeStruct((A_GROUPS, CHUNK), F32),
                   jax.ShapeDtypeStruct((1, A_WIDTH), F32),
                   jax.ShapeDtypeStruct((1, A_WIDTH), F32),
                   jax.ShapeDtypeStruct((8, CHUNK), F32),
                   jax.ShapeDtypeStruct((N_BUCKETS, CHUNK), F32)),
        in_specs=[tile, halo, tile_after, halo_after, tile, per_batch(MEM_LEN, 2 * MEM_LEN),
                  _full((2, 2 * CHUNK, 2 * CHUNK)),
                  pl.BlockSpec(memory_space=pltpu.SMEM),
                  _full((1, A_WIDTH)), _full((1, A_WIDTH)),
                  _full(grp), _full(grp), _full(grp),
                  _full((1, D_MODEL)), _full((1, D_MODEL)),
                  _full((IN_WIDTH, D_MODEL), single=True), _full((D_MODEL, D_MODEL), single=True),
                  _full((CHUNK, 2 * CHUNK))],
        out_specs=(tile, per_batch(MEM_LEN, 2 * MEM_LEN), anyspec, anyspec,
                   _full((1, D_MODEL)), _full((1, D_MODEL)), _full((8, CHUNK)),
                   _full(grp), _full((A_GROUPS, CHUNK)), _full((1, A_WIDTH)), _full((1, A_WIDTH)),
                   _full((8, CHUNK)), _full((N_BUCKETS, CHUNK))),
        scratch_shapes=[pltpu.VMEM((IN_WIDTH, D_MODEL), F32), pltpu.VMEM((D_MODEL, D_MODEL), F32),
                        pltpu.VMEM((tm, UV_W), F32), pltpu.VMEM((tm, Z_W), F32),
                        pltpu.VMEM((tm, 512), MM), pltpu.VMEM((tm + CHUNK, 2 * CHUNK), MM),
                        pltpu.VMEM((3, tm, D_MODEL), MM), pltpu.VMEM((CHUNK, D_MODEL), MM),
                        pltpu.VMEM((tm, IN_WIDTH), MM),
                        pltpu.VMEM((tm, D_MODEL), F32),
                        pltpu.VMEM((tm, D_MODEL), F32), pltpu.VMEM((2, tm, 1), F32),
                        pltpu.VMEM((tm, D_MODEL), F32), pltpu.VMEM((tm, D_MODEL), F32)]
                       + [pltpu.VMEM((tm, A_WIDTH), F32) for _ in range(4)]
                       + [pltpu.VMEM((tm, A_WIDTH), MM),
                          pltpu.VMEM((bpt * 2, 2 * CHUNK, 2 * CHUNK), F32),
                          pltpu.VMEM((bpt * 2, 2 * CHUNK, CHUNK), F32),
                          pltpu.VMEM((4, tm, MEM_LEN), F32),
                          pltpu.VMEM((bpt * 2, 2 * CHUNK, CHUNK), MM),
                          pltpu.VMEM((bpt * 2, 2 * CHUNK, CHUNK), MM),
                          pltpu.VMEM((tm + CHUNK, 2 * CHUNK), F32),
                          pltpu.VMEM((2, 2 * CHUNK, 2 * CHUNK), F32),
                          pltpu.VMEM(grp, F32),
                          pltpu.VMEM((8, CHUNK), F32),
                          pltpu.SemaphoreType.DMA((2,))],
        compiler_params=_params(dimension_semantics=("arbitrary", "arbitrary")),
    )(x2, x2, x2, x2, tgt2, mkv3, bias.reshape(2, 2 * CHUNK, 2 * CHUNK), sinks, vg, vb, wt, wtt, bcol, g1, g2, w_in_t, w_o, buckets)


class _ShardReduce:
    def __init__(self, pos, g, bufs, sems):
        self.x, self.y, self.c = pos
        self.g = g
        self.own, self.rcv, self.sbuf, self.rbuf, self.cbuf = bufs
        self.ld, self.sa, self.ra, self.sb, self.rb = sems
        self.nrow = g.shape[1]
        self.here = (self.x, self.y, self.c)
        self.sib = (self.x, self.y, 1 - self.c)
        self.first, self.second, self.far = _route(*pos)

    def _load(self, q):
        return pltpu.make_async_copy(self.g.at[2 * q + self.c], self.own.at[q], self.ld.at[q])

    def _to_sib(self, q, to):
        return _remote(self.g.at[2 * q + 1 - self.c], self.rcv.at[q], self.sa.at[q], self.ra.at[q], to)

    def _send(self, k, to):
        dst = self.cbuf.at[0] if k == 1 else self.rbuf.at[0 if k == 0 else 1]
        return _remote(self.sbuf.at[k], dst, self.sb.at[k], self.rb.at[k], to)

    def _stage(self, k, which, extra=None):
        def cast(r):
            v = self.rcv[which, r, :]
            if extra is not None:
                v = v + extra[0, r, :].astype(F32)
            self.sbuf[k, r, :] = v.astype(BF16)

        _rows_loop(self.nrow, cast)

    @staticmethod
    def _q(chip):
        return 2 * chip[0] + chip[1]

    def start(self):
        for q in range(4):
            self._load(q).start()
            self._to_sib(q, self.sib).start()

    def mid(self):
        for q in range(4):
            self._load(q).wait()
            self._to_sib(q, self.here).wait_recv()

        def add(r):
            for q in range(4):
                self.rcv[q, r, :] = self.rcv[q, r, :] + self.own[q, r, :]

        _rows_loop(self.nrow, add)
        to_first = (self.first[0], self.first[1], self.c)
        self._stage(0, self._q(self.first))
        self._send(0, to_first).start()
        self._stage(1, self._q(self.far))
        self._send(1, to_first).start()

    def pass_on(self):
        self._send(1, self.here).wait_recv()
        self._stage(2, self._q(self.second), extra=self.cbuf)
        self._send(2, (self.second[0], self.second[1], self.c)).start()

    def finish(self, out):
        self._send(0, self.here).wait_recv()
        self._send(2, self.here).wait_recv()
        which = 2 * self.x + self.y

        def tot(r):
            out[r, :] = (self.rcv[which, r, :] + self.rbuf[0, r, :].astype(F32)) + self.rbuf[1, r, :].astype(F32)

        _rows_loop(self.nrow, tot)
        for q in range(4):
            self._to_sib(q, self.sib).wait_send()
        to_first = (self.first[0], self.first[1], self.c)
        self._send(0, to_first).wait_send()
        self._send(1, to_first).wait_send()
        self._send(2, (self.second[0], self.second[1], self.c)).wait_send()


def _reduce_scratch(shape):
    return [pltpu.VMEM((4,) + shape, F32), pltpu.VMEM((4,) + shape, F32),
            pltpu.VMEM((3,) + shape, BF16), pltpu.VMEM((2,) + shape, BF16), pltpu.VMEM((1,) + shape, BF16),
            pltpu.SemaphoreType.DMA((4,)), pltpu.SemaphoreType.DMA((4,)), pltpu.SemaphoreType.DMA((4,)),
            pltpu.SemaphoreType.DMA((3,)), pltpu.SemaphoreType.DMA((3,))]


_N_RED = 10

_S_LAYOUT = (((1, D_MODEL), 0), ((1, D_MODEL), 8), ((1, D_MODEL), 16),
             ((1, A_WIDTH), 24), ((1, A_WIDTH), 28), ((A_GROUPS, CHUNK), 32),
             ((1, 4), 36), ((N_BUCKETS, 4), 40),
             ((A_GROUPS * CHUNK, CHUNK), 72))
_LOSS_ROW = 37
_W_SP_ROW = _S_LAYOUT[-1][1]
_S_ROWS = _W_SP_ROW + A_GROUPS * CHUNK
_N_SMALL = len(_S_LAYOUT)


def _pack_rows(dst, refs, tile=None):
    for (shp, r0), ref in zip(_S_LAYOUT, refs):
        if tuple(ref.shape) == (shp[1], shp[0]) and shp[0] != shp[1]:
            tile[...] = jnp.zeros_like(tile)
            tile[0:shp[1], 0:shp[0]] = ref[...]
            dst[r0:r0 + shp[0], 0:shp[1]] = tile[...].T[0:shp[0], 0:shp[1]]
        elif shp[0] == 1 and shp[1] >= CHUNK:
            for i in range(shp[1] // CHUNK):
                dst[r0 + i:r0 + i + 1, :] = ref[:, i * CHUNK:(i + 1) * CHUNK]
        elif ref.shape[-1] == CHUNK:
            dst[r0:r0 + shp[0], :] = ref[0:shp[0], :]
        else:
            dst[r0:r0 + shp[0], 0:shp[1]] = ref[...]


def _unpack_rows(src, refs):
    for (shp, r0), ref in zip(_S_LAYOUT, refs):
        if shp[0] == 1 and shp[1] >= CHUNK:
            for i in range(shp[1] // CHUNK):
                ref[:, i * CHUNK:(i + 1) * CHUNK] = src[r0 + i:r0 + i + 1, :]
        elif shp[1] == CHUNK:
            ref[...] = src[r0:r0 + shp[0], :]
        else:
            if tuple(ref.shape) == (shp[1], shp[0]):
                ref[...] = src[r0:r0 + CHUNK, :].T[0:shp[1], 0:shp[0]]
            else:
                ref[...] = src[r0:r0 + shp[0], 0:shp[1]]


_MEM_G = 2


def _greduce(ga, gb, dmkv, mem2, gm, w_mkv, small_g, loss_p):
    shp_c = (SHARD_O, 2 * MEM_LEN)
    shapes = (shp_c, gb.shape[1:], ga.shape[1:])
    rs = _S_ROWS

    def body(*refs):
        it = iter(refs)
        take = lambda n: [next(it) for _ in range(n)]
        gb_ref, ga_ref, d_ref, m_ref, gm_ref, wm_ref = take(6)
        sg_refs = take(_N_SMALL - 1)
        loss_ref, = take(1)
        oc, ob, oa, ogs = take(4)
        red = take(3 * _N_RED)
        gs_ref, rs_a, rs_b, rs_w, gc_ref, dgm_ref = take(6)
        ssem_a, rsem_a, ssem_b, rsem_b = take(4)

        pos = _position()
        x, y, cc = pos
        myq = 2 * x + y
        here, sib = (x, y, cc), (x, y, 1 - cc)
        chips = _other_chips(x, y)
        reducers = [_ShardReduce(pos, g, red[k * _N_RED:k * _N_RED + 5], red[k * _N_RED + 5:(k + 1) * _N_RED])
                    for k, g in enumerate((gc_ref, gb_ref, ga_ref))]
        for rd in reducers[1:]:
            rd.start()

        xf = m_ref[...]
        nm = xf * _rms(xf)
        hm = (nm * gm_ref[...]).astype(MM)
        d = d_ref[...].astype(MM)
        for o in range(N_DEV):
            gc_ref[o] = _dot_tn(hm[:, o * SHARD_O:(o + 1) * SHARD_O], d)
        dgm_ref[...] = jnp.sum(_dot_nt(d, wm_ref[...]) * nm, axis=0, keepdims=True)
        reducers[0].start()

        gs_ref[...] = jnp.zeros_like(gs_ref)
        _pack_rows(gs_ref, sg_refs[:_MEM_G] + [dgm_ref] + sg_refs[_MEM_G:])
        gs_ref[_LOSS_ROW:_LOSS_ROW + 1, :] = loss_ref[0:1, :]
        small_a = _remote(gs_ref, rs_a, ssem_a, rsem_a, sib)
        small_a.start()

        _remote(gs_ref, rs_a, ssem_a, rsem_a, here).wait_recv()
        rs_b[myq] = gs_ref[0:_W_SP_ROW, :] + rs_a[0:_W_SP_ROW, :]
        rs_w[myq] = (gs_ref[_W_SP_ROW:rs, :] + rs_a[_W_SP_ROW:rs, :]).astype(BF16)
        small_b = []
        for j, chip in enumerate(chips):
            to = (chip[0], chip[1], cc)
            small_b.append(_remote(rs_b.at[myq], rs_b.at[myq], ssem_b.at[0, j], rsem_b.at[0, j], to))
            small_b.append(_remote(rs_w.at[myq], rs_w.at[myq], ssem_b.at[1, j], rsem_b.at[1, j], to))
        for cp in small_b:
            cp.start()
        late_last = reducers[1:] + reducers[:1]
        for rd in late_last:
            rd.mid()
        for rd in late_last:
            rd.pass_on()

        for j in range(3):
            _remote(rs_b.at[myq], rs_b.at[myq], ssem_b.at[0, j], rsem_b.at[0, j], here).wait_recv()
            _remote(rs_w.at[myq], rs_w.at[myq], ssem_b.at[1, j], rsem_b.at[1, j], here).wait_recv()
        ogs[0:_W_SP_ROW, :] = ((rs_b[0] + rs_b[1]) + rs_b[2]) + rs_b[3]

        def tot_w(r):
            w = [rs_w[q, r, :].astype(F32) for q in range(4)]
            ogs[pl.ds(pl.multiple_of(_W_SP_ROW + r.start, 8), _ROWS), :] = ((w[0] + w[1]) + w[2]) + w[3]

        _rows_loop(rs - _W_SP_ROW, tot_w)
        for rd, out in zip(late_last, (ob, oa, oc)):
            rd.finish(out)
        small_a.wait_send()
        for cp in small_b:
            cp.wait_send()

    vm = pl.BlockSpec(memory_space=pltpu.VMEM)
    anyspec = pl.BlockSpec(memory_space=pl.ANY)
    scratch = []
    for shp in shapes:
        scratch += _reduce_scratch(shp)
    scratch += [pltpu.VMEM((rs, CHUNK), F32), pltpu.VMEM((rs, CHUNK), F32),
                pltpu.VMEM((4, _W_SP_ROW, CHUNK), F32), pltpu.VMEM((4, rs - _W_SP_ROW, CHUNK), BF16),
                pltpu.VMEM((N_DEV,) + shp_c, F32), pltpu.VMEM((1, D_MODEL), F32),
                pltpu.SemaphoreType.DMA, pltpu.SemaphoreType.DMA,
                pltpu.SemaphoreType.DMA((2, 3)), pltpu.SemaphoreType.DMA((2, 3))]
    tc, tb, ta, ts = pl.pallas_call(
        body, name="greduce",
        out_shape=tuple([jax.ShapeDtypeStruct(shp, F32) for shp in shapes] + [jax.ShapeDtypeStruct((rs, CHUNK), F32)]),
        in_specs=[anyspec] * 2 + [vm] * (4 + _N_SMALL),
        out_specs=(vm, vm, vm, vm),
        scratch_shapes=scratch,
        compiler_params=_params(),
    )(gb, ga, dmkv, mem2, gm, w_mkv, *small_g, loss_p)
    return ta, tb, tc, ts


def _adamw(w, g, m, v):
    m = ADAM_B1 * m + (1.0 - ADAM_B1) * g
    v = ADAM_B2 * v + (1.0 - ADAM_B2) * (g * g)
    m_hat = m / (1.0 - ADAM_B1 ** ADAM_STEP)
    v_hat = v / (1.0 - ADAM_B2 ** ADAM_STEP)
    delta = -ADAM_LR * (m_hat / (jnp.sqrt(v_hat) + ADAM_EPS) + ADAM_WD * w)
    return delta, m, v


def _update(ta, tb, tc, ts, big_wmv, small_wmv):
    shapes = (ta.shape, tb.shape, tc.shape)
    rs = _S_ROWS
    small_shapes = [tuple(a.shape) for a in small_wmv[0]]

    def body(*refs):
        it = iter(refs)
        take = lambda n: [next(it) for _ in range(n)]
        ga_ref, gb_ref, gc_ref, gs_ref = take(4)
        wa, ma, va, wb, mb, vb_, wc, mc, vc = take(9)
        sw_refs, sm_refs, sv_refs = take(_N_SMALL), take(_N_SMALL), take(_N_SMALL)
        oga, oda, oma, ova, ogb, odb, omb, ovb, ogc, odc, omc, ovc = take(12)
        so_refs = [take(_N_SMALL) for _ in range(4)]
        loss_out, = take(1)
        ws, ms, vs, ods, oms, ovs, turn = take(7)
        in_a, in_b, in_c, out_a, out_b, out_c, isem, osem = take(8)

        ins = ((ga_ref, wa, ma, va), (gb_ref, wb, mb, vb_), (gc_ref, wc, mc, vc))
        outs = ((oga, oda, oma, ova), (ogb, odb, omb, ovb), (ogc, odc, omc, ovc))
        in_buf, out_buf = (in_a, in_b, in_c), (out_a, out_b, out_c)

        def load(p, k):
            arr, r0, n = pieces[p]
            return pltpu.make_async_copy(ins[arr][k].at[pl.ds(r0, n)], in_buf[arr].at[k, pl.ds(r0, n)], isem.at[p, k])

        def store(p, k):
            arr, r0, n = pieces[p]
            src = in_buf[arr].at[0, pl.ds(r0, n)] if k == 0 else out_buf[arr].at[k - 1, pl.ds(r0, n)]
            return pltpu.make_async_copy(src, outs[arr][k].at[pl.ds(r0, n)], osem.at[p, k])

        for p in range(len(pieces)):
            for k in range(4):
                load(p, k).start()

        for buf in (ws, ms, vs):
            buf[...] = jnp.zeros_like(buf)
        _pack_rows(ws, sw_refs, turn)
        _pack_rows(ms, sm_refs, turn)
        _pack_rows(vs, sv_refs, turn)

        def upd_s(i, _):
            r = pl.ds(pl.multiple_of(i * 8, 8), 8)
            d, m, v = _adamw(ws[r, :], gs_ref[r, :], ms[r, :], vs[r, :])
            ods[r, :] = d
            oms[r, :] = m
            ovs[r, :] = v
            return 0

        lax.fori_loop(0, rs // 8, upd_s, 0)
        for k, buf in enumerate((gs_ref, ods, oms, ovs)):
            _unpack_rows(buf, so_refs[k])
        loss_out[...] = gs_ref[_LOSS_ROW:_LOSS_ROW + 1, 0:1]

        for p, (arr, r0, n) in enumerate(pieces):
            for k in range(4):
                load(p, k).wait()
            g_v, w_v, m_v, v_v = [in_buf[arr].at[k, pl.ds(r0, n)] for k in range(4)]
            d_o, m_o, v_o = [out_buf[arr].at[k, pl.ds(r0, n)] for k in range(3)]

            def upd(r, g_v=g_v, w_v=w_v, m_v=m_v, v_v=v_v, d_o=d_o, m_o=m_o, v_o=v_o):
                d, m, v = _adamw(w_v[r, :], g_v[r, :], m_v[r, :], v_v[r, :])
                d_o[r, :] = d
                m_o[r, :] = m
                v_o[r, :] = v

            _rows_loop(n, upd, 8)
            for k in range(4):
                store(p, k).start()
        for p in range(len(pieces)):
            for k in range(4):
                store(p, k).wait()

    quarter = shapes[0][0] // 4
    pieces = ((2, 0, shapes[2][0]), (1, 0, shapes[1][0])) + tuple((0, i * quarter, quarter) for i in range(4))
    vm = pl.BlockSpec(memory_space=pltpu.VMEM)
    anyspec = pl.BlockSpec(memory_space=pl.ANY)
    big_out = []
    for shp in shapes:
        big_out += [jax.ShapeDtypeStruct(shp, F32)] * 4
    small_out_shapes = [shp[::-1] if shp == (N_BUCKETS, 4) else shp for shp in small_shapes] * 4
    small_out = [jax.ShapeDtypeStruct(shp, F32) for shp in small_out_shapes]
    out_shape = tuple(big_out + small_out + [jax.ShapeDtypeStruct((1, 1), F32)])
    in_specs = [anyspec] * 3 + [vm] + [anyspec] * 9 + [vm] * (3 * _N_SMALL)
    return pl.pallas_call(
        body, name="update",
        out_shape=out_shape,
        in_specs=in_specs,
        out_specs=tuple([anyspec] * 12 + [vm] * (len(small_out) + 1)),
        scratch_shapes=([pltpu.VMEM((rs, CHUNK), F32) for _ in range(6)] + [pltpu.VMEM((CHUNK, CHUNK), F32)]
                        + [pltpu.VMEM((4,) + shp, F32) for shp in shapes]
                        + [pltpu.VMEM((3,) + shp, F32) for shp in shapes]
                        + [pltpu.SemaphoreType.DMA((len(pieces), 4)), pltpu.SemaphoreType.DMA((len(pieces), 4))]),
        compiler_params=_params(),
    )(ta, tb, tc, ts, *big_wmv, *small_wmv[0], *small_wmv[1], *small_wmv[2])


def kernel(x, mem, pre_norm_g, post_norm_g, mem_norm_g, w_in, w_mem_kv, v_norm_g, v_norm_b, w_spatial, b_spatial, attn_sinks, rel_bias, w_out, loss_target, m_pre_norm_g, m_post_norm_g, m_mem_norm_g, m_w_in, m_w_mem_kv, m_v_norm_g, m_v_norm_b, m_w_spatial, m_b_spatial, m_attn_sinks, m_rel_bias, m_w_out, v_pre_norm_g, v_post_norm_g, v_mem_norm_g, v_w_in, v_w_mem_kv, v_v_norm_g, v_v_norm_b, v_w_spatial, v_b_spatial, v_attn_sinks, v_rel_bias, v_w_out):
    sh_a = (w_in[0].T, m_w_in[0].T, v_w_in[0].T)
    sh_b = (w_out[0], m_w_out[0], v_w_out[0])
    sh_c = (w_mem_kv[0], m_w_mem_kv[0], v_w_mem_kv[0])
    nb, s, _ = x.shape
    t = nb * s
    x2 = x.reshape(t, D_MODEL)
    tgt2 = loss_target.reshape(t, D_MODEL)
    mem2 = mem.reshape(nb * MEM_LEN, D_MODEL)
    buckets = jnp.asarray(_t5_buckets())

    wa, wb, wc, bias, wt, wtt, bcol, mkv = _wgather(sh_a[0], sh_b[0], sh_c[0], rel_bias.T, w_spatial[0], b_spatial[0],
                                                    buckets, mem2, mem_norm_g)
    w_mkv = wc.reshape(D_MODEL, 2 * MEM_LEN)
    gx, dmkv, dwi, dwo, dg1, dg2, loss_p, dwsp, dbs, dvg, dvb, dsink, drel = _layer(
        x2, tgt2, mkv.reshape(nb, MEM_LEN, 2 * MEM_LEN), bias, attn_sinks.reshape(4), v_norm_g, v_norm_b, wt, wtt, bcol,
        pre_norm_g, post_norm_g, wa.reshape(IN_WIDTH, D_MODEL), wb.reshape(D_MODEL, D_MODEL), buckets,
        nb, s, min(256, s))
    gx = gx.reshape(nb, s, D_MODEL)
    small_grads = [dg1, dg2, dvg, dvb, dbs, dsink, drel, dwsp.reshape(A_GROUPS * CHUNK, CHUNK)]

    small_names = ["pre_norm_g", "post_norm_g", "mem_norm_g", "v_norm_g", "v_norm_b", "b_spatial", "attn_sinks",
                   "rel_bias", "w_spatial"]
    given = dict(pre_norm_g=(pre_norm_g, m_pre_norm_g, v_pre_norm_g), post_norm_g=(post_norm_g, m_post_norm_g, v_post_norm_g),
                 mem_norm_g=(mem_norm_g, m_mem_norm_g, v_mem_norm_g), v_norm_g=(v_norm_g, m_v_norm_g, v_v_norm_g),
                 v_norm_b=(v_norm_b, m_v_norm_b, v_v_norm_b), b_spatial=(b_spatial, m_b_spatial, v_b_spatial),
                 attn_sinks=(attn_sinks, m_attn_sinks, v_attn_sinks), rel_bias=(rel_bias, m_rel_bias, v_rel_bias),
                 w_spatial=(w_spatial, m_w_spatial, v_w_spatial))
    small_wmv = [[given[n][k].T if n == "rel_bias" else given[n][k].reshape(shp)
                  for n, (shp, _) in zip(small_names, _S_LAYOUT)] for k in range(3)]

    ta, tb, tc, ts = _greduce(dwi.reshape(N_DEV, SHARD_IN, D_MODEL), dwo.reshape(N_DEV, SHARD_O, D_MODEL),
                              dmkv.reshape(nb * MEM_LEN, 2 * MEM_LEN), mem2, mem_norm_g, w_mkv, small_grads, loss_p)
    outs = _update(ta, tb, tc, ts, (*sh_a, *sh_b, *sh_c), small_wmv)
    ra, rb, rc = outs[0:4], outs[4:8], outs[8:12]
    loss = outs[12 + 4 * _N_SMALL].reshape(())

    res = {}
    for k, kind in enumerate(("grad", "delta", "new_m", "new_v")):
        res[kind, "w_in"] = ra[k].T[None]
        res[kind, "w_out"] = rb[k][None]
        res[kind, "w_mem_kv"] = rc[k][None]
        for i, n in enumerate(small_names):
            o = outs[12 + k * _N_SMALL + i]
            res[kind, n] = o.T if n == "rel_bias" else o.reshape(given[n][0].shape)
    order = ["pre_norm_g", "post_norm_g", "mem_norm_g", "w_in", "w_mem_kv", "v_norm_g", "v_norm_b", "w_spatial",
             "b_spatial", "attn_sinks", "rel_bias", "w_out"]
    flat = [res[kind, n] for kind in ("grad", "delta", "new_m", "new_v") for n in order]
    return (loss, gx, *flat)
```
